```python
import jax, jax.numpy as jnp
from jax import lax
import numpy as np

D_MODEL = 1024
BATCH = 16
SEQ = 4096
DEPTH = 1

N_META = 16
CHUNK = 64
N_PAD = CHUNK - N_META
DN_HEADS = 4
DN_DK = 128
DN_DV = 128
CONV_K = 4
GLA_HEADS = 4
GLA_DK = 64
GLA_DV = 128
GLA_RANK = 16
GLA_NORMALIZER = 16.0
D_FF = 4 * D_MODEL
EPS = 1e-6

DN_QK = DN_HEADS * DN_DK
DN_V = DN_HEADS * DN_DV
GLA_QK = GLA_HEADS * GLA_DK
GLA_V = GLA_HEADS * GLA_DV
MIX_WIDTH = DN_V + GLA_V
SPLITS = (DN_QK, DN_QK, DN_V, DN_V, DN_HEADS, DN_HEADS, GLA_QK, GLA_QK, GLA_V, GLA_V, GLA_RANK)
IN_WIDTH = DN_QK * 2 + DN_V * 2 + DN_HEADS * 2 + GLA_QK * 2 + GLA_V * 2 + GLA_RANK

kernel_name = "hymba_gdn_gla_hybrid"


def rmsnorm(x, g):
    xf = x.astype(jnp.float32)
    y = xf * lax.rsqrt(jnp.mean(xf * xf, axis=-1, keepdims=True) + EPS)
    return (y * g.astype(jnp.float32)).astype(x.dtype)


def l2norm(x):
    xf = x.astype(jnp.float32)
    return xf * lax.rsqrt(jnp.sum(xf * xf, axis=-1, keepdims=True) + EPS)


def causal_conv(x, w):
    K = w.shape[0]
    T = x.shape[1]
    xp = jnp.pad(x, ((0, 0), (K - 1, 0), (0, 0)))
    y = xp[:, 0:T] * w[0]
    for i in range(1, K):
        y = y + xp[:, i:i + T] * w[i]
    return y


def to_head_chunks(x, n_heads):
    B, T, W = x.shape
    return x.reshape(B, T // CHUNK, CHUNK, n_heads, W // n_heads).transpose(0, 3, 1, 2, 4)


def scalar_chunks(x):
    B, T, H = x.shape
    return x.reshape(B, T // CHUNK, CHUNK, H).transpose(0, 3, 1, 2)


def from_head_chunks(o):
    B, H, N, C, d = o.shape
    return o.transpose(0, 2, 3, 1, 4).reshape(B, N * C, H, d)


def gated_delta_chunked(q, k, v, beta, g):
    B, H, N, C, dk = q.shape
    dv = v.shape[-1]
    q = q * (dk ** -0.5)
    gc = jnp.cumsum(g, axis=-1)
    tril = jnp.tril(jnp.ones((C, C), dtype=bool))
    strict = jnp.tril(jnp.ones((C, C), dtype=bool), -1)
    decay = jnp.exp(jnp.where(tril, gc[..., :, None] - gc[..., None, :], -jnp.inf))
    kb = k * beta[..., None]
    a = jnp.einsum('bhncd,bhnsd->bhncs', kb, k) * decay
    m = jnp.eye(C, dtype=jnp.float32) + jnp.where(strict, a, 0.0)
    rhs = jnp.concatenate([v * beta[..., None], kb * jnp.exp(gc)[..., None]], axis=-1)
    sol = lax.linalg.triangular_solve(m, rhs, left_side=True, lower=True, unit_diagonal=True)
    u = sol[..., :dv]
    w = sol[..., dv:]
    attn = jnp.einsum('bhncd,bhnsd->bhncs', q, k) * decay
    qg = q * jnp.exp(gc)[..., None]
    kd = k * jnp.exp(gc[..., -1:] - gc)[..., None]
    glast = jnp.exp(gc[..., -1])

    def step(S, xs):
        u_c, w_c, attn_c, qg_c, kd_c, gl_c = xs
        v_new = u_c - jnp.einsum('bhcd,bhde->bhce', w_c, S)
        o = jnp.einsum('bhcd,bhde->bhce', qg_c, S) + jnp.einsum('bhcs,bhse->bhce', attn_c, v_new)
        S = S * gl_c[..., None, None] + jnp.einsum('bhcd,bhce->bhde', kd_c, v_new)
        return S, o

    xs = (jnp.moveaxis(u, 2, 0), jnp.moveaxis(w, 2, 0), jnp.moveaxis(attn, 2, 0),
          jnp.moveaxis(qg, 2, 0), jnp.moveaxis(kd, 2, 0), jnp.moveaxis(glast, 2, 0))
    S0 = jnp.zeros((B, H, dk, dv), jnp.float32)
    _, o = lax.scan(step, S0, xs)
    return jnp.moveaxis(o, 0, 2)


def gla_chunked(q, k, v, g):
    B, H, N, C, dk = q.shape
    dv = v.shape[-1]
    q = q * (dk ** -0.5)
    b = jnp.cumsum(g, axis=-2)
    bref = b[..., C // 2:C // 2 + 1, :]
    qi = q * jnp.exp(b - bref)
    ki = k * jnp.exp(bref - b)
    tril = jnp.tril(jnp.ones((C, C), dtype=bool))
    A = jnp.where(tril, jnp.einsum('bhncd,bhnsd->bhncs', qi, ki), 0.0)
    o_intra = jnp.einsum('bhncs,bhnse->bhnce', A, v)
    qg = q * jnp.exp(b)
    kd = k * jnp.exp(b[..., -1:, :] - b)
    glast = jnp.exp(b[..., -1, :])

    def step(S, xs):
        qg_c, kd_c, v_c, gl_c = xs
        o = jnp.einsum('bhcd,bhde->bhce', qg_c, S)
        S = S * gl_c[..., None] + jnp.einsum('bhcd,bhce->bhde', kd_c, v_c)
        return S, o

    xs = (jnp.moveaxis(qg, 2, 0), jnp.moveaxis(kd, 2, 0), jnp.moveaxis(v, 2, 0), jnp.moveaxis(glast, 2, 0))
    S0 = jnp.zeros((B, H, dk, dv), jnp.float32)
    _, o_inter = lax.scan(step, S0, xs)
    return o_intra + jnp.moveaxis(o_inter, 0, 2)


def hybrid_layer(x, valid, norm1_g, w_in, conv_w, a_log, dt_bias, dn_norm_g,
                 gla_w2, gla_b, gla_norm_g, w_out, norm2_g, w_up, w_down):
    B, T, _ = x.shape
    vmask = valid[None, :, None]
    h = jnp.where(vmask, rmsnorm(x, norm1_g), 0).astype(x.dtype)
    proj = h @ w_in
    offs = [0]
    for s in SPLITS:
        offs.append(offs[-1] + s)
    (dq, dk_, dv, dz, db, da, gq, gk, gv, gr, glr) = [proj[..., offs[i]:offs[i + 1]] for i in range(len(SPLITS))]

    qkv = jax.nn.silu(causal_conv(jnp.concatenate([dq, dk_, dv], axis=-1), conv_w))
    q_dn = l2norm(to_head_chunks(qkv[..., :DN_QK], DN_HEADS))
    k_dn = l2norm(to_head_chunks(qkv[..., DN_QK:2 * DN_QK], DN_HEADS))
    v_dn = to_head_chunks(qkv[..., 2 * DN_QK:], DN_HEADS).astype(jnp.float32)
    beta = jax.nn.sigmoid(db.astype(jnp.float32))
    g_dn = -jnp.exp(a_log.astype(jnp.float32)) * jax.nn.softplus(da.astype(jnp.float32) + dt_bias.astype(jnp.float32))
    g_dn = jnp.where(vmask, g_dn, 0.0)
    o_dn = from_head_chunks(gated_delta_chunked(q_dn, k_dn, v_dn, scalar_chunks(beta), scalar_chunks(g_dn)))
    o_dn = rmsnorm(o_dn, dn_norm_g) * jax.nn.silu(dz.reshape(B, T, DN_HEADS, DN_DV).astype(jnp.float32))
    o_dn = o_dn.reshape(B, T, DN_V)

    g_gla = jax.nn.log_sigmoid((glr @ gla_w2 + gla_b).astype(jnp.float32)) / GLA_NORMALIZER
    g_gla = jnp.where(vmask, g_gla, 0.0)
    o_gla = gla_chunked(to_head_chunks(gq, GLA_HEADS).astype(jnp.float32),
                        to_head_chunks(gk, GLA_HEADS).astype(jnp.float32),
                        to_head_chunks(gv, GLA_HEADS).astype(jnp.float32),
                        to_head_chunks(g_gla, GLA_HEADS))
    o_gla = rmsnorm(from_head_chunks(o_gla), gla_norm_g) * jax.nn.silu(gr.reshape(B, T, GLA_HEADS, GLA_DV).astype(jnp.float32))
    o_gla = o_gla.reshape(B, T, GLA_V)

    mix = jnp.concatenate([o_dn, o_gla], axis=-1).astype(x.dtype)
    x = x + mix @ w_out

    h2 = rmsnorm(x, norm2_g)
    x = x + jnp.square(jax.nn.relu(h2 @ w_up)) @ w_down
    return x


def _fwd_setup_inputs(seed: int = 0) -> dict:
    key = jax.random.key(seed)
    ks = jax.random.split(key, 20)
    f32 = jnp.float32
    x = jax.random.normal(ks[0], (BATCH, SEQ, D_MODEL), f32)
    meta_tokens = jax.random.normal(ks[1], (N_META, D_MODEL), f32)
    norm1_g = 1.0 + 0.02 * jax.random.normal(ks[2], (DEPTH, D_MODEL), f32)
    w_in = jax.random.normal(ks[3], (DEPTH, D_MODEL, IN_WIDTH), f32) * D_MODEL ** -0.5
    conv_w = jax.random.normal(ks[4], (DEPTH, CONV_K, 2 * DN_QK + DN_V), f32) * CONV_K ** -0.5
    a_log = jnp.log(jax.random.uniform(ks[5], (DEPTH, DN_HEADS), f32, 1.0, 16.0))
    dt = jnp.exp(jax.random.uniform(ks[6], (DEPTH, DN_HEADS), f32, math_log(0.001), math_log(0.1)))
    dt_bias = dt + jnp.log(-jnp.expm1(-dt))
    dn_norm_g = 1.0 + 0.02 * jax.random.normal(ks[7], (DEPTH, DN_DV), f32)
    gla_w2 = jax.random.normal(ks[8], (DEPTH, GLA_RANK, GLA_QK), f32) * GLA_RANK ** -0.5
    gla_b = 0.01 * jax.random.normal(ks[9], (DEPTH, GLA_QK), f32)
    gla_norm_g = 1.0 + 0.02 * jax.random.normal(ks[10], (DEPTH, GLA_DV), f32)
    w_out = jax.random.normal(ks[11], (DEPTH, MIX_WIDTH, D_MODEL), f32) * MIX_WIDTH ** -0.5
    norm2_g = 1.0 + 0.02 * jax.random.normal(ks[12], (DEPTH, D_MODEL), f32)
    w_up = jax.random.normal(ks[13], (DEPTH, D_MODEL, D_FF), f32) * D_MODEL ** -0.5
    w_down = jax.random.normal(ks[14], (DEPTH, D_FF, D_MODEL), f32) * D_FF ** -0.5
    final_norm_g = 1.0 + 0.02 * jax.random.normal(ks[15], (D_MODEL,), f32)
    return {"x": x, "meta_tokens": meta_tokens, "norm1_g": norm1_g, "w_in": w_in, "conv_w": conv_w,
            "a_log": a_log, "dt_bias": dt_bias, "dn_norm_g": dn_norm_g, "gla_w2": gla_w2, "gla_b": gla_b,
            "gla_norm_g": gla_norm_g, "w_out": w_out, "norm2_g": norm2_g, "w_up": w_up, "w_down": w_down,
            "final_norm_g": final_norm_g}


def math_log(v):
    return float(np.log(v))


def _fwd_reference(x, meta_tokens, norm1_g, w_in, conv_w, a_log, dt_bias, dn_norm_g, gla_w2, gla_b,
              gla_norm_g, w_out, norm2_g, w_up, w_down, final_norm_g):
    B = x.shape[0]
    pad = jnp.zeros((B, N_PAD, D_MODEL), x.dtype)
    meta = jnp.broadcast_to(meta_tokens.astype(x.dtype)[None], (B, N_META, D_MODEL))
    h = jnp.concatenate([pad, meta, x], axis=1)
    T = h.shape[1]
    valid = jnp.arange(T) >= N_PAD
    for l in range(DEPTH):
        h = hybrid_layer(h, valid, norm1_g[l], w_in[l], conv_w[l], a_log[l], dt_bias[l], dn_norm_g[l],
                         gla_w2[l], gla_b[l], gla_norm_g[l], w_out[l], norm2_g[l], w_up[l], w_down[l])
    return rmsnorm(h, final_norm_g)[:, CHUNK:]


import jax as _jax
import jax.numpy as _jnp

TWIN_FORMAT = 'train_step'
FWD_PARAMS = ['x', 'meta_tokens', 'norm1_g', 'w_in', 'conv_w', 'a_log', 'dt_bias', 'dn_norm_g', 'gla_w2', 'gla_b', 'gla_norm_g', 'w_out', 'norm2_g', 'w_up', 'w_down', 'final_norm_g']
TWIN_WEIGHTS = ['meta_tokens', 'norm1_g', 'w_in', 'conv_w', 'a_log', 'dt_bias', 'dn_norm_g', 'gla_w2', 'gla_b', 'gla_norm_g', 'w_out', 'norm2_g', 'w_up', 'w_down', 'final_norm_g']
TWIN_DIFF_INPUT = 'x'
TWIN_INPUTS = ['x', 'meta_tokens', 'norm1_g', 'w_in', 'conv_w', 'a_log', 'dt_bias', 'dn_norm_g', 'gla_w2', 'gla_b', 'gla_norm_g', 'w_out', 'norm2_g', 'w_up', 'w_down', 'final_norm_g', 'loss_target', 'm_meta_tokens', 'm_norm1_g', 'm_w_in', 'm_conv_w', 'm_a_log', 'm_dt_bias', 'm_dn_norm_g', 'm_gla_w2', 'm_gla_b', 'm_gla_norm_g', 'm_w_out', 'm_norm2_g', 'm_w_up', 'm_w_down', 'm_final_norm_g', 'v_meta_tokens', 'v_norm1_g', 'v_w_in', 'v_conv_w', 'v_a_log', 'v_dt_bias', 'v_dn_norm_g', 'v_gla_w2', 'v_gla_b', 'v_gla_norm_g', 'v_w_out', 'v_norm2_g', 'v_w_up', 'v_w_down', 'v_final_norm_g']
TWIN_OUTPUTS = ['loss', 'grad_x', 'grad_meta_tokens', 'grad_norm1_g', 'grad_w_in', 'grad_conv_w', 'grad_a_log', 'grad_dt_bias', 'grad_dn_norm_g', 'grad_gla_w2', 'grad_gla_b', 'grad_gla_norm_g', 'grad_w_out', 'grad_norm2_g', 'grad_w_up', 'grad_w_down', 'grad_final_norm_g', 'delta_meta_tokens', 'delta_norm1_g', 'delta_w_in', 'delta_conv_w', 'delta_a_log', 'delta_dt_bias', 'delta_dn_norm_g', 'delta_gla_w2', 'delta_gla_b', 'delta_gla_norm_g', 'delta_w_out', 'delta_norm2_g', 'delta_w_up', 'delta_w_down', 'delta_final_norm_g', 'new_m_meta_tokens', 'new_m_norm1_g', 'new_m_w_in', 'new_m_conv_w', 'new_m_a_log', 'new_m_dt_bias', 'new_m_dn_norm_g', 'new_m_gla_w2', 'new_m_gla_b', 'new_m_gla_norm_g', 'new_m_w_out', 'new_m_norm2_g', 'new_m_w_up', 'new_m_w_down', 'new_m_final_norm_g', 'new_v_meta_tokens', 'new_v_norm1_g', 'new_v_w_in', 'new_v_conv_w', 'new_v_a_log', 'new_v_dt_bias', 'new_v_dn_norm_g', 'new_v_gla_w2', 'new_v_gla_b', 'new_v_gla_norm_g', 'new_v_w_out', 'new_v_norm2_g', 'new_v_w_up', 'new_v_w_down', 'new_v_final_norm_g']
TWIN_LEAF_KINDS = {'loss': 'loss', 'grad_x': 'grad_x', 'grad_meta_tokens': 'grad_w', 'grad_norm1_g': 'grad_w', 'grad_w_in': 'grad_w', 'grad_conv_w': 'grad_w', 'grad_a_log': 'grad_w', 'grad_dt_bias': 'grad_w', 'grad_dn_norm_g': 'grad_w', 'grad_gla_w2': 'grad_w', 'grad_gla_b': 'grad_w', 'grad_gla_norm_g': 'grad_w', 'grad_w_out': 'grad_w', 'grad_norm2_g': 'grad_w', 'grad_w_up': 'grad_w', 'grad_w_down': 'grad_w', 'grad_final_norm_g': 'grad_w', 'delta_meta_tokens': 'delta_w', 'delta_norm1_g': 'delta_w', 'delta_w_in': 'delta_w', 'delta_conv_w': 'delta_w', 'delta_a_log': 'delta_w', 'delta_dt_bias': 'delta_w', 'delta_dn_norm_g': 'delta_w', 'delta_gla_w2': 'delta_w', 'delta_gla_b': 'delta_w', 'delta_gla_norm_g': 'delta_w', 'delta_w_out': 'delta_w', 'delta_norm2_g': 'delta_w', 'delta_w_up': 'delta_w', 'delta_w_down': 'delta_w', 'delta_final_norm_g': 'delta_w', 'new_m_meta_tokens': 'new_m', 'new_m_norm1_g': 'new_m', 'new_m_w_in': 'new_m', 'new_m_conv_w': 'new_m', 'new_m_a_log': 'new_m', 'new_m_dt_bias': 'new_m', 'new_m_dn_norm_g': 'new_m', 'new_m_gla_w2': 'new_m', 'new_m_gla_b': 'new_m', 'new_m_gla_norm_g': 'new_m', 'new_m_w_out': 'new_m', 'new_m_norm2_g': 'new_m', 'new_m_w_up': 'new_m', 'new_m_w_down': 'new_m', 'new_m_final_norm_g': 'new_m', 'new_v_meta_tokens': 'new_v', 'new_v_norm1_g': 'new_v', 'new_v_w_in': 'new_v', 'new_v_conv_w': 'new_v', 'new_v_a_log': 'new_v', 'new_v_dt_bias': 'new_v', 'new_v_dn_norm_g': 'new_v', 'new_v_gla_w2': 'new_v', 'new_v_gla_b': 'new_v', 'new_v_gla_norm_g': 'new_v', 'new_v_w_out': 'new_v', 'new_v_norm2_g': 'new_v', 'new_v_w_up': 'new_v', 'new_v_w_down': 'new_v', 'new_v_final_norm_g': 'new_v'}


def _forward(args):
    return _fwd_reference(*[args[k] for k in FWD_PARAMS])


def _output_shape():
    out = _jax.eval_shape(lambda: _forward(_fwd_setup_inputs(0)))
    return out.shape, out.dtype

N_MICROBATCH = 1
ADAM_LR = 0.001
ADAM_B1 = 0.9
ADAM_B2 = 0.999
ADAM_EPS = 1e-08
ADAM_WD = 0.01
ADAM_STEP = 10
PER_EXAMPLE_BATCH_AXIS = {'x': 0, 'loss_target': 0}
SHARED_INPUTS = []
_WEIGHT_DTYPES = {'meta_tokens': _jnp.float32, 'norm1_g': _jnp.float32, 'w_in': _jnp.float32, 'conv_w': _jnp.float32, 'a_log': _jnp.float32, 'dt_bias': _jnp.float32, 'dn_norm_g': _jnp.float32, 'gla_w2': _jnp.float32, 'gla_b': _jnp.float32, 'gla_norm_g': _jnp.float32, 'w_out': _jnp.float32, 'norm2_g': _jnp.float32, 'w_up': _jnp.float32, 'w_down': _jnp.float32, 'final_norm_g': _jnp.float32}
MOMENT_SCALE = {'meta_tokens': 7.482999e-03, 'norm1_g': 2.776471e-01, 'w_in': 1.390914e-01, 'conv_w': 1.081884e-01, 'a_log': 7.864739e-01, 'dt_bias': 7.663849e-01, 'dn_norm_g': 2.964049e-01, 'gla_w2': 2.221127e-02, 'gla_b': 8.678190e-02, 'gla_norm_g': 3.203387e-01, 'w_out': 1.372400e-01, 'norm2_g': 2.243364e-01, 'w_up': 1.054342e-01, 'w_down': 2.047085e-01, 'final_norm_g': 6.446361e+01}


def _to_microbatches(a, axis):
    t = _jnp.moveaxis(a, axis, 0)
    t = t.reshape((N_MICROBATCH, t.shape[0] // N_MICROBATCH) + t.shape[1:])
    return _jnp.moveaxis(t, 1, axis + 1)


def setup_inputs(seed: int = 0) -> dict:
    inp = _fwd_setup_inputs(seed)
    key = _jax.random.fold_in(_jax.random.key(seed), 7919)
    shape, _ = _output_shape()
    out = dict(inp)
    out["loss_target"] = _jax.random.normal(_jax.random.fold_in(key, 0), shape, _jnp.float32)
    for i, name in enumerate(TWIN_WEIGHTS):
        w = inp[name].astype(_jnp.float32)
        if MOMENT_SCALE is None:
            s = _jnp.sqrt(_jnp.mean(_jnp.square(w)) + 1e-30)
        else:
            s = MOMENT_SCALE[name]
        km, kv = _jax.random.split(_jax.random.fold_in(key, i + 1))
        out[name] = w
        out["m_" + name] = s * _jax.random.normal(km, w.shape, _jnp.float32)
        out["v_" + name] = (s * s) * _jax.random.uniform(kv, w.shape, _jnp.float32, 0.5, 1.5)
    if N_MICROBATCH > 1:
        for name, axis in PER_EXAMPLE_BATCH_AXIS.items():
            out[name] = _to_microbatches(out[name], axis)
    return {'x': out['x'], 'meta_tokens': out['meta_tokens'], 'norm1_g': out['norm1_g'], 'w_in': out['w_in'], 'conv_w': out['conv_w'], 'a_log': out['a_log'], 'dt_bias': out['dt_bias'], 'dn_norm_g': out['dn_norm_g'], 'gla_w2': out['gla_w2'], 'gla_b': out['gla_b'], 'gla_norm_g': out['gla_norm_g'], 'w_out': out['w_out'], 'norm2_g': out['norm2_g'], 'w_up': out['w_up'], 'w_down': out['w_down'], 'final_norm_g': out['final_norm_g'], 'loss_target': out['loss_target'], 'm_meta_tokens': out['m_meta_tokens'], 'm_norm1_g': out['m_norm1_g'], 'm_w_in': out['m_w_in'], 'm_conv_w': out['m_conv_w'], 'm_a_log': out['m_a_log'], 'm_dt_bias': out['m_dt_bias'], 'm_dn_norm_g': out['m_dn_norm_g'], 'm_gla_w2': out['m_gla_w2'], 'm_gla_b': out['m_gla_b'], 'm_gla_norm_g': out['m_gla_norm_g'], 'm_w_out': out['m_w_out'], 'm_norm2_g': out['m_norm2_g'], 'm_w_up': out['m_w_up'], 'm_w_down': out['m_w_down'], 'm_final_norm_g': out['m_final_norm_g'], 'v_meta_tokens': out['v_meta_tokens'], 'v_norm1_g': out['v_norm1_g'], 'v_w_in': out['v_w_in'], 'v_conv_w': out['v_conv_w'], 'v_a_log': out['v_a_log'], 'v_dt_bias': out['v_dt_bias'], 'v_dn_norm_g': out['v_dn_norm_g'], 'v_gla_w2': out['v_gla_w2'], 'v_gla_b': out['v_gla_b'], 'v_gla_norm_g': out['v_gla_norm_g'], 'v_w_out': out['v_w_out'], 'v_norm2_g': out['v_norm2_g'], 'v_w_up': out['v_w_up'], 'v_w_down': out['v_w_down'], 'v_final_norm_g': out['v_final_norm_g']}


def _loss(weights, diff, rest, loss_target):
    with _jax.named_scope("forward"):
        args = {**rest, TWIN_DIFF_INPUT: diff, **{k: w.astype(_WEIGHT_DTYPES[k]) for k, w in weights.items()}}
        y = _forward(args)
    with _jax.named_scope("loss_head"):
        err = _jnp.square(y.astype(_jnp.float32) - loss_target)
        return 0.5 * _jnp.sum(_jnp.mean(err, axis=-1)) if err.ndim else 0.5 * err


def _adamw(w, g, m, v):
    m = ADAM_B1 * m + (1.0 - ADAM_B1) * g
    v = ADAM_B2 * v + (1.0 - ADAM_B2) * _jnp.square(g)
    m_hat = m / (1.0 - ADAM_B1 ** ADAM_STEP)
    v_hat = v / (1.0 - ADAM_B2 ** ADAM_STEP)
    delta = -ADAM_LR * (m_hat / (_jnp.sqrt(v_hat) + ADAM_EPS) + ADAM_WD * w)
    return delta, m, v


def reference(x, meta_tokens, norm1_g, w_in, conv_w, a_log, dt_bias, dn_norm_g, gla_w2, gla_b, gla_norm_g, w_out, norm2_g, w_up, w_down, final_norm_g, loss_target, m_meta_tokens, m_norm1_g, m_w_in, m_conv_w, m_a_log, m_dt_bias, m_dn_norm_g, m_gla_w2, m_gla_b, m_gla_norm_g, m_w_out, m_norm2_g, m_w_up, m_w_down, m_final_norm_g, v_meta_tokens, v_norm1_g, v_w_in, v_conv_w, v_a_log, v_dt_bias, v_dn_norm_g, v_gla_w2, v_gla_b, v_gla_norm_g, v_w_out, v_norm2_g, v_w_up, v_w_down, v_final_norm_g):
    given = dict(x=x, meta_tokens=meta_tokens, norm1_g=norm1_g, w_in=w_in, conv_w=conv_w, a_log=a_log, dt_bias=dt_bias, dn_norm_g=dn_norm_g, gla_w2=gla_w2, gla_b=gla_b, gla_norm_g=gla_norm_g, w_out=w_out, norm2_g=norm2_g, w_up=w_up, w_down=w_down, final_norm_g=final_norm_g, loss_target=loss_target, m_meta_tokens=m_meta_tokens, m_norm1_g=m_norm1_g, m_w_in=m_w_in, m_conv_w=m_conv_w, m_a_log=m_a_log, m_dt_bias=m_dt_bias, m_dn_norm_g=m_dn_norm_g, m_gla_w2=m_gla_w2, m_gla_b=m_gla_b, m_gla_norm_g=m_gla_norm_g, m_w_out=m_w_out, m_norm2_g=m_norm2_g, m_w_up=m_w_up, m_w_down=m_w_down, m_final_norm_g=m_final_norm_g, v_meta_tokens=v_meta_tokens, v_norm1_g=v_norm1_g, v_w_in=v_w_in, v_conv_w=v_conv_w, v_a_log=v_a_log, v_dt_bias=v_dt_bias, v_dn_norm_g=v_dn_norm_g, v_gla_w2=v_gla_w2, v_gla_b=v_gla_b, v_gla_norm_g=v_gla_norm_g, v_w_out=v_w_out, v_norm2_g=v_norm2_g, v_w_up=v_w_up, v_w_down=v_w_down, v_final_norm_g=v_final_norm_g)
    weights = {n: given[n] for n in TWIN_WEIGHTS}
    shared = {n: given[n] for n in SHARED_INPUTS}
    per_example = {n: given[n] for n in ['x']}
    grad_fn = _jax.value_and_grad(_loss, argnums=(0, 1))

    def one_microbatch(ex, loss_target):
        ex = dict(ex)
        diff = ex.pop(TWIN_DIFF_INPUT)
        return grad_fn(weights, diff, {**shared, **ex}, loss_target)

    if N_MICROBATCH == 1:
        loss, (grad_w, grad_x) = one_microbatch(per_example, given["loss_target"])
    else:
        def body(carry, xs):
            loss_sum, grad_sum = carry
            l_k, (gw_k, gx_k) = one_microbatch(xs[0], xs[1])
            with _jax.named_scope("update"):
                return (loss_sum + l_k, _jax.tree.map(_jnp.add, grad_sum, gw_k)), gx_k

        init = (_jnp.zeros((), _jnp.float32), _jax.tree.map(_jnp.zeros_like, weights))
        (loss, grad_w), grad_x = _jax.lax.scan(body, init, (per_example, given["loss_target"]))
    with _jax.named_scope("update"):
        delta_w, new_m, new_v = {}, {}, {}
        for n in TWIN_WEIGHTS:
            delta_w[n], new_m[n], new_v[n] = _adamw(weights[n], grad_w[n], given["m_" + n], given["v_" + n])
    return (loss, grad_x, *[grad_w[n] for n in TWIN_WEIGHTS], *[delta_w[n] for n in TWIN_WEIGHTS],
            *[new_m[n] for n in TWIN_WEIGHTS], *[new_v[n] for n in TWIN_WEIGHTS])
```

```python
import functools

import jax
import jax.numpy as jnp
import numpy as np
from jax import lax
from jax.experimental import pallas as pl
from jax.experimental.pallas import tpu as pltpu

F32 = jnp.float32
BF16 = jnp.bfloat16
HI = lax.Precision.HIGHEST
MESH = pl.DeviceIdType.MESH

D_MODEL = 1024
N_META = 16
CHUNK = 64
N_PAD = CHUNK - N_META
NH = 4
DN_D = 128
GLA_DK = 64
GLA_DV = 128
GLA_RANK = 16
D_FF = 4 * D_MODEL
EPS = 1e-6
IN_WIDTH = 3608
C_QKV, C_DZ, C_GQK, C_GV, C_GR, C_SA, C_SB, PW = 0, 1536, 2048, 2560, 3072, 3584, 3712, 3840
LANE = 128
N_CHIPS = 4

ADAM_LR, ADAM_B1, ADAM_B2, ADAM_EPS, ADAM_WD, ADAM_STEP = 0.001, 0.9, 0.999, 1e-08, 0.01, 10

VMEM_BIG = 56 * 1024 * 1024


def _cp(vmem=None, sem=None):
    kw = {}
    if vmem is not None:
        kw["vmem_limit_bytes"] = vmem
    if sem is not None:
        kw["dimension_semantics"] = sem
    return pltpu.CompilerParams(**kw)


def _tile(n, target, mult=16):
    best = None
    for t in range(mult, min(n, target) + 1, mult):
        if n % t == 0:
            best = t
    assert best is not None, (n, target)
    return best


def _dot(a, b, dims, prec=None):
    return lax.dot_general(a, b, (dims, ((), ())), preferred_element_type=F32, precision=prec)


def _nn(a, b):
    return _dot(a.astype(BF16), b.astype(BF16), ((1,), (0,)))


def _nt(a, b):
    return _dot(a.astype(BF16), b.astype(BF16), ((1,), (1,)))


def _tn(a, b):
    return _dot(a.astype(BF16), b.astype(BF16), ((0,), (0,)))


def _nn_hi(a, b):
    return _dot(a, b, ((1,), (0,)), HI)


def _nt_hi(a, b):
    return _dot(a, b, ((1,), (1,)), HI)


def _tn_hi(a, b):
    return _dot(a, b, ((0,), (0,)), HI)


def _sigmoid(x):
    return 1.0 / (1.0 + jnp.exp(-x))


def _softplus(x):
    return jnp.maximum(x, 0.0) + jnp.log(1.0 + jnp.exp(-jnp.abs(x)))


def _logsigmoid(x):
    return -_softplus(-x)


def _iota2(shape, dim):
    return lax.broadcasted_iota(jnp.int32, shape, dim)


def _mm(a, b, mode, *, tm, tn, tk, out_dtypes, extras=(), epilogue=None, name, vmem=VMEM_BIG):
    if mode == "tn":
        K, M = a.shape
    else:
        M, K = a.shape
    N = b.shape[0] if mode == "nt" else b.shape[1]
    assert M % tm == 0 and N % tn == 0 and K % tk == 0, (name, M, N, K, tm, tn, tk)
    nk = K // tk
    n_ex, n_out = len(extras), len(out_dtypes)
    if mode == "tn":
        a_spec = pl.BlockSpec((tk, tm), lambda i, j, k: (k, i))
    else:
        a_spec = pl.BlockSpec((tm, tk), lambda i, j, k: (i, k))
    if mode == "nt":
        b_spec = pl.BlockSpec((tn, tk), lambda i, j, k: (j, k))
    else:
        b_spec = pl.BlockSpec((tk, tn), lambda i, j, k: (k, j))
    mn_spec = pl.BlockSpec((tm, tn), lambda i, j, k: (i, j))
    dims = {"nn": ((1,), (0,)), "nt": ((1,), (1,)), "tn": ((0,), (0,))}[mode]

    def body(*refs):
        a_ref, b_ref = refs[0], refs[1]
        ex_refs = refs[2:2 + n_ex]
        out_refs = refs[2 + n_ex:2 + n_ex + n_out]
        acc_ref = refs[2 + n_ex + n_out]
        k = pl.program_id(2)
        part = _dot(a_ref[...].astype(BF16), b_ref[...].astype(BF16), dims)

        @pl.when(k == 0)
        def _():
            acc_ref[...] = part

        @pl.when(k > 0)
        def _():
            acc_ref[...] += part

        @pl.when(k == nk - 1)
        def _():
            acc = acc_ref[...]
            res = (acc,) if epilogue is None else epilogue(acc, *[e[...] for e in ex_refs])
            for o_ref, r in zip(out_refs, res):
                o_ref[...] = r.astype(o_ref.dtype)

    outs = pl.pallas_call(
        body, name=name, grid=(M // tm, N // tn, nk),
        in_specs=[a_spec, b_spec] + [mn_spec] * n_ex,
        out_specs=[mn_spec] * n_out,
        out_shape=[jax.ShapeDtypeStruct((M, N), dt) for dt in out_dtypes],
        scratch_shapes=[pltpu.VMEM((tm, tn), F32)],
        compiler_params=_cp(vmem, ("parallel", "parallel", "arbitrary")),
    )(a, b, *extras)
    return tuple(outs)


def _rms_fwd(x, g, *, tr, name):
    n, d = x.shape

    def body(x_ref, g_ref, o_ref):
        xv = x_ref[...]
        r = lax.rsqrt(jnp.mean(xv * xv, axis=-1, keepdims=True) + EPS)
        o_ref[...] = (xv * r * g_ref[...]).astype(o_ref.dtype)

    return pl.pallas_call(
        body, name=name, grid=(n // tr,),
        in_specs=[pl.BlockSpec((tr, d), lambda i: (i, 0)), pl.BlockSpec((1, d), lambda i: (0, 0))],
        out_specs=pl.BlockSpec((tr, d), lambda i: (i, 0)),
        out_shape=jax.ShapeDtypeStruct((n, d), BF16),
        compiler_params=_cp(VMEM_BIG),
    )(x, g)


def _rms_bwd_math(xv, g, dy):
    r = lax.rsqrt(jnp.mean(xv * xv, axis=-1, keepdims=True) + EPS)
    xh = xv * r
    gdy = dy * g
    dx = r * (gdy - xh * jnp.mean(xh * gdy, axis=-1, keepdims=True))
    return dx, jnp.sum(dy * xh, axis=0, keepdims=True)


def _rms_bwd_add(x, g, dy, res, *, tr, name):
    n, d = x.shape

    def body(x_ref, g_ref, dy_ref, res_ref, o_ref, dg_ref):
        dx, dg = _rms_bwd_math(x_ref[...], g_ref[...], dy_ref[...])
        o_ref[...] = res_ref[...] + dx

        @pl.when(pl.program_id(0) == 0)
        def _():
            dg_ref[...] = dg

        @pl.when(pl.program_id(0) > 0)
        def _():
            dg_ref[...] += dg

    row = pl.BlockSpec((tr, d), lambda i: (i, 0))
    vec = pl.BlockSpec((1, d), lambda i: (0, 0))
    return pl.pallas_call(
        body, name=name, grid=(n // tr,),
        in_specs=[row, vec, row, row], out_specs=[row, vec],
        out_shape=[jax.ShapeDtypeStruct((n, d), F32), jax.ShapeDtypeStruct((1, d), F32)],
        compiler_params=_cp(VMEM_BIG),
    )(x, g, dy, res)


def _final_loss(x2, gf, tgt, *, t_seq, tr, name):
    n, d = x2.shape
    per_seq = t_seq // tr

    def body(x_ref, g_ref, t_ref, dx_ref, dg_ref, loss_ref):
        i = pl.program_id(0)
        xv = x_ref[...]
        g = g_ref[...]
        r = lax.rsqrt(jnp.mean(xv * xv, axis=-1, keepdims=True) + EPS)
        xh = xv * r
        pos = (i % per_seq) * tr + _iota2((tr, 1), 0)
        real = pos >= CHUNK
        err = jnp.where(real, xh * g - t_ref[...], 0.0)
        dy = err * (1.0 / d)
        gdy = dy * g
        dx_ref[...] = r * (gdy - xh * jnp.mean(xh * gdy, axis=-1, keepdims=True))
        dg = jnp.sum(dy * xh, axis=0, keepdims=True)
        ls = 0.5 * jnp.sum(jnp.mean(err * err, axis=-1, keepdims=True), axis=0, keepdims=True)
        ls = jnp.where(_iota2((1, LANE), 1) == 0, ls, 0.0)

        @pl.when(i == 0)
        def _():
            dg_ref[...] = dg
            loss_ref[...] = ls

        @pl.when(i > 0)
        def _():
            dg_ref[...] += dg
            loss_ref[...] += ls

    row = pl.BlockSpec((tr, d), lambda i: (i, 0))
    vec = pl.BlockSpec((1, d), lambda i: (0, 0))
    one = pl.BlockSpec((1, LANE), lambda i: (0, 0))
    return pl.pallas_call(
        body, name=name, grid=(n // tr,),
        in_specs=[row, vec, row], out_specs=[row, vec, one],
        out_shape=[jax.ShapeDtypeStruct((n, d), F32), jax.ShapeDtypeStruct((1, d), F32),
                   jax.ShapeDtypeStruct((1, LANE), F32)],
        compiler_params=_cp(VMEM_BIG),
    )(x2, gf, tgt)


def _gnorm_fwd(o_dn, o_gla, projp, g_dn, g_gla, *, tr, name):
    n = o_dn.shape[0]
    w = NH * DN_D

    def body(odn_ref, ogl_ref, z_ref, r_ref, gdn_ref, ggl_ref, mix_ref):
        for grp, (o_ref, gate_ref, gain_ref) in enumerate(((odn_ref, z_ref, gdn_ref), (ogl_ref, r_ref, ggl_ref))):
            gain = gain_ref[...]
            for h in range(NH):
                sl = slice(h * DN_D, (h + 1) * DN_D)
                o = o_ref[:, sl]
                z = gate_ref[:, sl]
                r = lax.rsqrt(jnp.mean(o * o, axis=-1, keepdims=True) + EPS)
                y = (o * r * gain) * (z * _sigmoid(z))
                mix_ref[:, grp * w + h * DN_D: grp * w + (h + 1) * DN_D] = y.astype(mix_ref.dtype)

    row = pl.BlockSpec((tr, w), lambda i: (i, 0))
    vec = pl.BlockSpec((1, DN_D), lambda i: (0, 0))
    return pl.pallas_call(
        body, name=name, grid=(n // tr,),
        in_specs=[row, row, pl.BlockSpec((tr, w), lambda i: (i, C_DZ // w)),
                  pl.BlockSpec((tr, w), lambda i: (i, C_GR // w)), vec, vec],
        out_specs=pl.BlockSpec((tr, 2 * w), lambda i: (i, 0)),
        out_shape=jax.ShapeDtypeStruct((n, 2 * w), BF16),
        compiler_params=_cp(VMEM_BIG),
    )(o_dn, o_gla, projp, projp, g_dn, g_gla)


def _gnorm_bwd(dmix, o_dn, o_gla, projp, g_dn, g_gla, *, tr, name):
    n = o_dn.shape[0]
    w = NH * DN_D

    def body(dm_ref, odn_ref, ogl_ref, z_ref, r_ref, gdn_ref, ggl_ref,
             dodn_ref, ddz_ref, dogl_ref, dgr_ref, dgdn_ref, dggl_ref):
        first = pl.program_id(0) == 0
        groups = ((odn_ref, z_ref, gdn_ref, dodn_ref, ddz_ref, dgdn_ref),
                  (ogl_ref, r_ref, ggl_ref, dogl_ref, dgr_ref, dggl_ref))
        for grp, (o_ref, gate_ref, gain_ref, do_ref, dgate_ref, dgain_ref) in enumerate(groups):
            gain = gain_ref[...]
            dgain = jnp.zeros((1, DN_D), F32)
            for h in range(NH):
                sl = slice(h * DN_D, (h + 1) * DN_D)
                o = o_ref[:, sl]
                z = gate_ref[:, sl]
                dm = dm_ref[:, grp * w + h * DN_D: grp * w + (h + 1) * DN_D]
                r = lax.rsqrt(jnp.mean(o * o, axis=-1, keepdims=True) + EPS)
                oh = o * r
                s = _sigmoid(z)
                dn = dm * (z * s)
                dgate_ref[:, sl] = dm * (oh * gain) * (s * (1.0 + z * (1.0 - s)))
                gdn = dn * gain
                do_ref[:, sl] = r * (gdn - oh * jnp.mean(oh * gdn, axis=-1, keepdims=True))
                dgain = dgain + jnp.sum(dn * oh, axis=0, keepdims=True)

            @pl.when(first)
            def _():
                dgain_ref[...] = dgain

            @pl.when(jnp.logical_not(first))
            def _():
                dgain_ref[...] += dgain

    row = pl.BlockSpec((tr, w), lambda i: (i, 0))
    vec = pl.BlockSpec((1, DN_D), lambda i: (0, 0))
    big = jax.ShapeDtypeStruct((n, w), F32)
    small = jax.ShapeDtypeStruct((1, DN_D), F32)
    return pl.pallas_call(
        body, name=name, grid=(n // tr,),
        in_specs=[pl.BlockSpec((tr, 2 * w), lambda i: (i, 0)), row, row,
                  pl.BlockSpec((tr, w), lambda i: (i, C_DZ // w)), pl.BlockSpec((tr, w), lambda i: (i, C_GR // w)), vec, vec],
        out_specs=[row, row, row, row, vec, vec],
        out_shape=[big, big, big, big, small, small],
        compiler_params=_cp(VMEM_BIG),
    )(dmix, o_dn, o_gla, projp, projp, g_dn, g_gla)


QKV_W = 3 * NH * DN_D
HALO = 8


def _conv_z(xs_ref, cw_ref, tt):
    z = cw_ref[0:1, :] * xs_ref[pl.ds(HALO - 3, tt), :]
    for j in range(1, 4):
        z = z + cw_ref[j:j + 1, :] * xs_ref[pl.ds(HALO - 3 + j, tt), :]
    return z


def _dnprep_fwd(projp, conv_w, *, bsz, t_seq, tt, name):
    n = bsz * t_seq
    per_seq = t_seq // tt
    hw = NH * DN_D

    def body(x_ref, halo_ref, cw_ref, q_ref, k_ref, v_ref, xs_ref):
        i = pl.program_id(1)
        xs_ref[0:HALO, :] = jnp.where(i == 0, 0.0, halo_ref[...])
        xs_ref[HALO:HALO + tt, :] = x_ref[...]
        z = _conv_z(xs_ref, cw_ref, tt)
        a = z * _sigmoid(z)
        for grp, o_ref in enumerate((q_ref, k_ref)):
            for h in range(NH):
                ah = a[:, grp * hw + h * DN_D: grp * hw + (h + 1) * DN_D]
                rs = lax.rsqrt(jnp.sum(ah * ah, axis=-1, keepdims=True) + EPS)
                o_ref[:, h * DN_D:(h + 1) * DN_D] = ah * rs
        v_ref[...] = a[:, 2 * hw:3 * hw]

    def halo_map(b, i):
        return (jnp.maximum((b * t_seq + i * tt) // HALO - 1, 0), 0)

    out = pl.BlockSpec((tt, hw), lambda b, i: (b * per_seq + i, 0))
    sds = jax.ShapeDtypeStruct((n, hw), F32)
    return pl.pallas_call(
        body, name=name, grid=(bsz, per_seq),
        in_specs=[pl.BlockSpec((tt, QKV_W), lambda b, i: (b * per_seq + i, 0)),
                  pl.BlockSpec((HALO, QKV_W), halo_map),
                  pl.BlockSpec((4, QKV_W), lambda b, i: (0, 0))],
        out_specs=[out, out, out], out_shape=[sds, sds, sds],
        scratch_shapes=[pltpu.VMEM((tt + HALO, QKV_W), F32)],
        compiler_params=_cp(VMEM_BIG),
    )(projp, projp, conv_w)


def _dnprep_bwd_a(projp, conv_w, dq, dk, dv, *, bsz, t_seq, tt, name):
    n = bsz * t_seq
    per_seq = t_seq // tt
    hw = NH * DN_D

    def body(x_ref, halo_ref, cw_ref, dq_ref, dk_ref, dv_ref, dz_ref, dcw_ref, xs_ref):
        b, i = pl.program_id(0), pl.program_id(1)
        xs_ref[0:HALO, :] = jnp.where(i == 0, 0.0, halo_ref[...])
        xs_ref[HALO:HALO + tt, :] = x_ref[...]
        z = _conv_z(xs_ref, cw_ref, tt)
        s = _sigmoid(z)
        a = z * s
        dsilu = s * (1.0 + z * (1.0 - s))
        for grp, d_ref in enumerate((dq_ref, dk_ref)):
            for h in range(NH):
                sl = slice(grp * hw + h * DN_D, grp * hw + (h + 1) * DN_D)
                ah = a[:, sl]
                rs = lax.rsqrt(jnp.sum(ah * ah, axis=-1, keepdims=True) + EPS)
                y = ah * rs
                dy = d_ref[:, h * DN_D:(h + 1) * DN_D]
                da = rs * (dy - y * jnp.sum(dy * y, axis=-1, keepdims=True))
                dz_ref[:, sl] = da * dsilu[:, sl]
        dz_ref[:, 2 * hw:3 * hw] = dv_ref[...] * dsilu[:, 2 * hw:3 * hw]
        dz = dz_ref[...]
        first = jnp.logical_and(b == 0, i == 0)
        for j in range(4):
            part = jnp.sum(dz * xs_ref[pl.ds(HALO - 3 + j, tt), :], axis=0, keepdims=True)

            @pl.when(first)
            def _():
                dcw_ref[j:j + 1, :] = part

            @pl.when(jnp.logical_not(first))
            def _():
                dcw_ref[j:j + 1, :] += part

    def halo_map(b, i):
        return (jnp.maximum((b * t_seq + i * tt) // HALO - 1, 0), 0)

    hrow = pl.BlockSpec((tt, hw), lambda b, i: (b * per_seq + i, 0))
    return pl.pallas_call(
        body, name=name, grid=(bsz, per_seq),
        in_specs=[pl.BlockSpec((tt, QKV_W), lambda b, i: (b * per_seq + i, 0)),
                  pl.BlockSpec((HALO, QKV_W), halo_map),
                  pl.BlockSpec((4, QKV_W), lambda b, i: (0, 0)), hrow, hrow, hrow],
        out_specs=[pl.BlockSpec((tt, QKV_W), lambda b, i: (b * per_seq + i, 0)),
                   pl.BlockSpec((4, QKV_W), lambda b, i: (0, 0))],
        out_shape=[jax.ShapeDtypeStruct((n, QKV_W), F32), jax.ShapeDtypeStruct((4, QKV_W), F32)],
        scratch_shapes=[pltpu.VMEM((tt + HALO, QKV_W), F32)],
        compiler_params=_cp(VMEM_BIG),
    )(projp, projp, conv_w, dq, dk, dv)


def _dnprep_bwd_b(dz, conv_w, *, bsz, t_seq, tt, name):
    n = bsz * t_seq
    per_seq = t_seq // tt
    last_blk = n // HALO - 1

    def body(dz_ref, halo_ref, cw_ref, dx_ref, ds_ref):
        i = pl.program_id(1)
        ds_ref[0:tt, :] = dz_ref[...]
        ds_ref[tt:tt + HALO, :] = jnp.where(i == per_seq - 1, 0.0, halo_ref[...])
        dx = cw_ref[0:1, :] * ds_ref[pl.ds(3, tt), :]
        for j in range(1, 4):
            dx = dx + cw_ref[j:j + 1, :] * ds_ref[pl.ds(3 - j, tt), :]
        dx_ref[...] = dx

    def halo_map(b, i):
        return (jnp.minimum((b * t_seq + (i + 1) * tt) // HALO, last_blk), 0)

    row = pl.BlockSpec((tt, QKV_W), lambda b, i: (b * per_seq + i, 0))
    return pl.pallas_call(
        body, name=name, grid=(bsz, per_seq),
        in_specs=[row, pl.BlockSpec((HALO, QKV_W), halo_map), pl.BlockSpec((4, QKV_W), lambda b, i: (0, 0))],
        out_specs=row, out_shape=jax.ShapeDtypeStruct((n, QKV_W), F32),
        scratch_shapes=[pltpu.VMEM((tt + HALO, QKV_W), F32)],
        compiler_params=_cp(VMEM_BIG),
    )(dz, dz, conv_w)


def _masks64():
    r = _iota2((CHUNK, CHUNK), 0)
    c = _iota2((CHUNK, CHUNK), 1)
    return r, c


def _tri_inv(a_strict):
    r, c = _masks64()
    eye = (r == c).astype(F32)
    blk16 = (r // 16) == (c // 16)
    blk32 = (r // 32) == (c // 32)
    ld = jnp.where(blk16, a_strict, 0.0)
    x = eye - ld
    p = _nn_hi(ld, ld)
    x = x + _nn_hi(x, p)
    p = _nn_hi(p, p)
    x = x + _nn_hi(x, p)
    p = _nn_hi(p, p)
    x = x + _nn_hi(x, p)
    l1 = jnp.where(jnp.logical_and(blk32, jnp.logical_not(blk16)), a_strict, 0.0)
    x = x - _nn_hi(_nn_hi(x, l1), x)
    l2 = jnp.where(blk32, 0.0, a_strict)
    x = x - _nn_hi(_nn_hi(x, l2), x)
    return x


def _dn_gates(sa, alog, dtb, chunk_in_seq):
    rows = _iota2((CHUNK, LANE), 0)
    valid = jnp.logical_or(rows >= N_PAD, chunk_in_seq > 0)
    beta_t = _sigmoid(sa)
    ea = jnp.exp(alog)
    g_t = jnp.where(valid, -ea * _softplus(sa + dtb), 0.0)
    r, c = _masks64()
    ltri = (r >= c).astype(F32)
    gam_t = _nn_hi(ltri, g_t)
    return beta_t, g_t, gam_t, valid, ea


def _dn_intra_fwd(qn, kn, v, projp, alog_row, dtb_row, *, nc_seq, name):
    n = qn.shape[0]
    nct = n // CHUNK
    hw = NH * DN_D
    scale = DN_D ** -0.5

    def body(q_ref, k_ref, v_ref, sa_ref, al_ref, dt_ref, u_ref, w_ref, qg_ref, kd_ref, p_ref, t_ref, gl_ref):
        ci = pl.program_id(0) % nc_seq
        beta_t, _, gam_t, _, _ = _dn_gates(sa_ref[...], al_ref[...], dt_ref[...], ci)
        gam_tt = gam_t.T
        r, c = _masks64()
        incl = r >= c
        strict = r > c
        for h in range(NH):
            sl = slice(h * DN_D, (h + 1) * DN_D)
            beta = beta_t[:, h:h + 1]
            gam = gam_t[:, 4 + h:5 + h]
            gam_row = gam_tt[4 + h:5 + h, :]
            gl = gam_t[CHUNK - 1:CHUNK, 4 + h:5 + h]
            dec = jnp.exp(jnp.where(incl, gam - gam_row, -jnp.inf))
            kh = k_ref[:, sl]
            qh = q_ref[:, sl] * scale
            vh = v_ref[:, sl]
            kk = _nt(kh, kh)
            qk = _nt(qh, kh)
            a = jnp.where(strict, beta * kk * dec, 0.0)
            tm = _tri_inv(a)
            egam = jnp.exp(gam)
            u_ref[:, sl] = _nn(tm, beta * vh)
            w_ref[:, sl] = _nn(tm, (beta * egam) * kh)
            qg_ref[:, sl] = egam * qh
            kd_ref[:, sl] = jnp.exp(gl - gam) * kh
            p_ref[0, h] = qk * dec
            t_ref[0, h] = tm
            gl_ref[0, h:h + 1, :] = jnp.broadcast_to(jnp.exp(gl), (1, LANE))

    row = pl.BlockSpec((CHUNK, hw), lambda i: (i, 0))
    vec = pl.BlockSpec((1, LANE), lambda i: (0, 0))
    mat = pl.BlockSpec((1, NH, CHUNK, CHUNK), lambda i: (i, 0, 0, 0))
    big = jax.ShapeDtypeStruct((n, hw), F32)
    msd = jax.ShapeDtypeStruct((nct, NH, CHUNK, CHUNK), F32)
    return pl.pallas_call(
        body, name=name, grid=(nct,),
        in_specs=[row, row, row, pl.BlockSpec((CHUNK, LANE), lambda i: (i, C_SA // LANE)), vec, vec],
        out_specs=[row, row, row, row, mat, mat, pl.BlockSpec((1, NH, LANE), lambda i: (i, 0, 0))],
        out_shape=[big, big, big, big, msd, msd, jax.ShapeDtypeStruct((nct, NH, LANE), F32)],
        compiler_params=_cp(VMEM_BIG),
    )(qn, kn, v, projp, alog_row, dtb_row)


def _dn_scan_fwd(u, w, qg, kd, p, gl, *, bsz, nc_seq, name):
    hw = NH * DN_D
    t_seq = nc_seq * CHUNK
    u, w, qg, kd = (z.reshape(bsz, t_seq, hw) for z in (u, w, qg, kd))
    p = p.reshape(bsz, nc_seq, NH, CHUNK, CHUNK)
    gl = gl.reshape(bsz, nc_seq, NH, LANE)

    def body(u_ref, w_ref, qg_ref, kd_ref, p_ref, gl_ref, o_ref, vn_ref, hist_ref, s_ref):
        @pl.when(pl.program_id(0) == 0)
        def _():
            s_ref[...] = jnp.zeros_like(s_ref)

        for b in range(bsz):
            for h in range(NH):
                sl = slice(h * DN_D, (h + 1) * DN_D)
                s = s_ref[b, h]
                hist_ref[b, 0, h] = s
                vn = u_ref[b, :, sl] - _nn(w_ref[b, :, sl], s)
                vn_ref[b, :, sl] = vn
                o_ref[b, :, sl] = _nn(qg_ref[b, :, sl], s) + _nn(p_ref[b, 0, h], vn)
                s_ref[b, h] = gl_ref[b, 0, h:h + 1, :] * s + _tn(kd_ref[b, :, sl], vn)

    row = pl.BlockSpec((bsz, CHUNK, hw), lambda i: (0, i, 0))
    outs = pl.pallas_call(
        body, name=name, grid=(nc_seq,),
        in_specs=[row, row, row, row, pl.BlockSpec((bsz, 1, NH, CHUNK, CHUNK), lambda i: (0, i, 0, 0, 0)),
                  pl.BlockSpec((bsz, 1, NH, LANE), lambda i: (0, i, 0, 0))],
        out_specs=[row, row, pl.BlockSpec((bsz, 1, NH, DN_D, DN_D), lambda i: (0, i, 0, 0, 0))],
        out_shape=[jax.ShapeDtypeStruct((bsz, t_seq, hw), F32), jax.ShapeDtypeStruct((bsz, t_seq, hw), F32),
                   jax.ShapeDtypeStruct((bsz, nc_seq, NH, DN_D, DN_D), F32)],
        scratch_shapes=[pltpu.VMEM((bsz, NH, DN_D, DN_D), F32)],
        compiler_params=_cp(VMEM_BIG, ("arbitrary",)),
    )(u, w, qg, kd, p, gl)
    o, vn, hist = outs
    return o.reshape(bsz * t_seq, hw), vn.reshape(bsz * t_seq, hw), hist


def _dn_scan_bwd(do, w, qg, kd, vn, p, gl, hist, *, bsz, nc_seq, name):
    hw = NH * DN_D
    t_seq = nc_seq * CHUNK
    do, w, qg, kd, vn = (z.reshape(bsz, t_seq, hw) for z in (do, w, qg, kd, vn))
    p = p.reshape(bsz, nc_seq, NH, CHUNK, CHUNK)
    gl = gl.reshape(bsz, nc_seq, NH, LANE)

    def body(do_ref, w_ref, qg_ref, kd_ref, vn_ref, p_ref, gl_ref, hist_ref,
             du_ref, dw_ref, dqg_ref, dkd_ref, dp_ref, dgl_ref, ds_ref):
        @pl.when(pl.program_id(0) == 0)
        def _():
            ds_ref[...] = jnp.zeros_like(ds_ref)

        for b in range(bsz):
            for h in range(NH):
                sl = slice(h * DN_D, (h + 1) * DN_D)
                s = hist_ref[b, 0, h]
                dsn = ds_ref[b, h]
                doh = do_ref[b, :, sl]
                vnh = vn_ref[b, :, sl]
                kdh = kd_ref[b, :, sl]
                dvn = _tn(p_ref[b, 0, h], doh) + _nn(kdh, dsn)
                du_ref[b, :, sl] = dvn
                dqg_ref[b, :, sl] = _nt(doh, s)
                dp_ref[b, 0, h] = _nt(doh, vnh)
                dkd_ref[b, :, sl] = _nt(vnh, dsn)
                dw_ref[b, :, sl] = -_nt(dvn, s)
                dgl = jnp.sum(jnp.sum(dsn * s, axis=0, keepdims=True), axis=1, keepdims=True)
                dgl_ref[b, 0, h:h + 1, :] = jnp.broadcast_to(dgl, (1, LANE))
                ds_ref[b, h] = (_tn(qg_ref[b, :, sl], doh) + gl_ref[b, 0, h:h + 1, :] * dsn
                                - _tn(w_ref[b, :, sl], dvn))

    rev = lambda i: nc_seq - 1 - i
    row = pl.BlockSpec((bsz, CHUNK, hw), lambda i: (0, rev(i), 0))
    mat = pl.BlockSpec((bsz, 1, NH, CHUNK, CHUNK), lambda i: (0, rev(i), 0, 0, 0))
    glb = pl.BlockSpec((bsz, 1, NH, LANE), lambda i: (0, rev(i), 0, 0))
    big = jax.ShapeDtypeStruct((bsz, t_seq, hw), F32)
    outs = pl.pallas_call(
        body, name=name, grid=(nc_seq,),
        in_specs=[row, row, row, row, row, mat, glb,
                  pl.BlockSpec((bsz, 1, NH, DN_D, DN_D), lambda i: (0, rev(i), 0, 0, 0))],
        out_specs=[row, row, row, row, mat, glb],
        out_shape=[big, big, big, big, jax.ShapeDtypeStruct((bsz, nc_seq, NH, CHUNK, CHUNK), F32),
                   jax.ShapeDtypeStruct((bsz, nc_seq, NH, LANE), F32)],
        scratch_shapes=[pltpu.VMEM((bsz, NH, DN_D, DN_D), F32)],
        compiler_params=_cp(VMEM_BIG, ("arbitrary",)),
    )(do, w, qg, kd, vn, p, gl, hist)
    du, dw, dqg, dkd, dp, dgl = outs
    n = bsz * t_seq
    return (du.reshape(n, hw), dw.reshape(n, hw), dqg.reshape(n, hw), dkd.reshape(n, hw),
            dp.reshape(bsz * nc_seq, NH, CHUNK, CHUNK), dgl.reshape(bsz * nc_seq, NH, LANE))


def _dn_intra_bwd(qn, kn, v, projp, alog_row, dtb_row, u, w, tmat, du, dw, dqg, dkd, dp, dgl, *, nc_seq, name):
    n = qn.shape[0]
    nct = n // CHUNK
    hw = NH * DN_D
    scale = DN_D ** -0.5

    def body(q_ref, k_ref, v_ref, sa_ref, al_ref, dt_ref, u_ref, w_ref, t_ref, du_ref, dw_ref, dqg_ref, dkd_ref,
             dp_ref, dgl_ref, dq_ref, dk_ref, dv_ref, dsa_ref, dal_ref, ddt_ref):
        ci = pl.program_id(0) % nc_seq
        sa = sa_ref[...]
        beta_t, g_t, gam_t, valid, ea = _dn_gates(sa, al_ref[...], dt_ref[...], ci)
        gam_tt = gam_t.T
        r, c = _masks64()
        incl = r >= c
        strict = r > c
        lane = _iota2((CHUNK, LANE), 1)
        rows1 = _iota2((CHUNK, 1), 0)
        acc_t = jnp.zeros((CHUNK, LANE), F32)
        for h in range(NH):
            sl = slice(h * DN_D, (h + 1) * DN_D)
            beta = beta_t[:, h:h + 1]
            gam = gam_t[:, 4 + h:5 + h]
            gam_row = gam_tt[4 + h:5 + h, :]
            gl = gam_t[CHUNK - 1:CHUNK, 4 + h:5 + h]
            dec = jnp.exp(jnp.where(incl, gam - gam_row, -jnp.inf))
            kh = k_ref[:, sl]
            qh = q_ref[:, sl] * scale
            vh = v_ref[:, sl]
            kk = _nt(kh, kh)
            qk = _nt(qh, kh)
            a = jnp.where(strict, beta * kk * dec, 0.0)
            pm = qk * dec
            tm = t_ref[0, h]
            uh = u_ref[:, sl]
            wh = w_ref[:, sl]
            egam = jnp.exp(gam)
            ekd = jnp.exp(gl - gam)
            dvb = _tn(tm, du_ref[:, sl])
            dkg = _tn(tm, dw_ref[:, sl])
            da = jnp.where(strict, -(_nt(dvb, uh) + _nt(dkg, wh)), 0.0)
            dad = da * dec
            dkk = beta * dad
            dbeta = jnp.sum(dad * kk, axis=1, keepdims=True)
            dpm = jnp.where(incl, dp_ref[0, h], 0.0)
            dqk = dpm * dec
            e = da * a + dpm * pm
            dgam = jnp.sum(e, axis=1, keepdims=True) - jnp.sum(e.T, axis=1, keepdims=True)
            dqgh = dqg_ref[:, sl]
            dkdh = dkd_ref[:, sl]
            dkh = (_nn(dkk, kh) + _tn(dkk, kh) + _tn(dqk, qh) + (beta * egam) * dkg + ekd * dkdh)
            dqh = _nn(dqk, kh) + egam * dqgh
            dbeta = dbeta + jnp.sum(dkg * (egam * kh), axis=1, keepdims=True) + jnp.sum(dvb * vh, axis=1, keepdims=True)
            rkd = jnp.sum(dkdh * (ekd * kh), axis=1, keepdims=True)
            dgam = (dgam + jnp.sum(dkg * ((beta * egam) * kh), axis=1, keepdims=True)
                    + jnp.sum(dqgh * (egam * qh), axis=1, keepdims=True) - rkd)
            dgam_last = jnp.sum(rkd, axis=0, keepdims=True) + dgl_ref[0, h:h + 1, 0:1] * jnp.exp(gl)
            dgam = dgam + jnp.where(rows1 == CHUNK - 1, dgam_last, 0.0)
            dq_ref[:, sl] = dqh * scale
            dk_ref[:, sl] = dkh
            dv_ref[:, sl] = beta * dvb
            acc_t = acc_t + jnp.where(lane == h, dbeta, 0.0) + jnp.where(lane == 4 + h, dgam, 0.0)
        utri = (r <= c).astype(F32)
        dg_t = _nn_hi(utri, acc_t)
        ddb = acc_t * beta_t * (1.0 - beta_t)
        dda = jnp.where(valid, dg_t * (-ea) * _sigmoid(sa + dt_ref[...]), 0.0)
        dsa_ref[...] = jnp.where(lane < 4, ddb, jnp.where(lane < 8, dda, 0.0))
        in_g = jnp.logical_and(lane >= 4, lane < 8)
        dal = jnp.sum(jnp.where(in_g, dg_t * g_t, 0.0), axis=0, keepdims=True)
        ddt = jnp.sum(jnp.where(in_g, dda, 0.0), axis=0, keepdims=True)
        first = pl.program_id(0) == 0

        @pl.when(first)
        def _():
            dal_ref[...] = dal
            ddt_ref[...] = ddt

        @pl.when(jnp.logical_not(first))
        def _():
            dal_ref[...] += dal
            ddt_ref[...] += ddt

    row = pl.BlockSpec((CHUNK, hw), lambda i: (i, 0))
    vec = pl.BlockSpec((1, LANE), lambda i: (0, 0))
    mat = pl.BlockSpec((1, NH, CHUNK, CHUNK), lambda i: (i, 0, 0, 0))
    glb = pl.BlockSpec((1, NH, LANE), lambda i: (i, 0, 0))
    big = jax.ShapeDtypeStruct((n, hw), F32)
    v128 = jax.ShapeDtypeStruct((1, LANE), F32)
    return pl.pallas_call(
        body, name=name, grid=(nct,),
        in_specs=[row, row, row, pl.BlockSpec((CHUNK, LANE), lambda i: (i, C_SA // LANE)), vec, vec,
                  row, row, mat, row, row, row, row, mat, glb],
        out_specs=[row, row, row, pl.BlockSpec((CHUNK, LANE), lambda i: (i, 0)), vec, vec],
        out_shape=[big, big, big, jax.ShapeDtypeStruct((n, LANE), F32), v128, v128],
        compiler_params=_cp(VMEM_BIG, ("arbitrary",)),
    )(qn, kn, v, projp, alog_row, dtb_row, u, w, tmat, du, dw, dqg, dkd, dp, dgl)


GQ_W = NH * GLA_DK
GV_W = NH * GLA_DV
GLA_NORM = 16.0
MID = CHUNK // 2


def _gla_gates(sb, w2p, gb, chunk_in_seq):
    rows = _iota2((CHUNK, GQ_W), 0)
    valid = jnp.logical_or(rows >= N_PAD, chunk_in_seq > 0)
    graw = _nn_hi(sb, w2p) + gb
    g = jnp.where(valid, _logsigmoid(graw) * (1.0 / GLA_NORM), 0.0)
    r, c = _masks64()
    bcum = _nn_hi((r >= c).astype(F32), g)
    return graw, bcum, valid


def _head_mask(h):
    lane = _iota2((1, GQ_W), 1)
    return jnp.logical_and(lane >= h * GLA_DK, lane < (h + 1) * GLA_DK)


def _gla_intra_fwd(projp, w2p, gb, *, nc_seq, name):
    n = projp.shape[0]
    nct = n // CHUNK
    scale = GLA_DK ** -0.5

    def body(qk_ref, v_ref, sb_ref, w2_ref, gb_ref, oi_ref, qg_ref, kd_ref, gl_ref):
        ci = pl.program_id(0) % nc_seq
        _, bc, _ = _gla_gates(sb_ref[...], w2_ref[...], gb_ref[...], ci)
        bref = bc[MID:MID + 1, :]
        bl = bc[CHUNK - 1:CHUNK, :]
        q = qk_ref[:, 0:GQ_W] * scale
        k = qk_ref[:, GQ_W:2 * GQ_W]
        qi = q * jnp.exp(bc - bref)
        ki = k * jnp.exp(bref - bc)
        qg_ref[...] = q * jnp.exp(bc)
        kd_ref[...] = k * jnp.exp(bl - bc)
        gl_ref[0] = jnp.exp(bl)
        r, c = _masks64()
        incl = r >= c
        for h in range(NH):
            a = jnp.where(incl, _nt(jnp.where(_head_mask(h), qi, 0.0), ki), 0.0)
            oi_ref[:, h * GLA_DV:(h + 1) * GLA_DV] = _nn(a, v_ref[:, h * GLA_DV:(h + 1) * GLA_DV])

    return pl.pallas_call(
        body, name=name, grid=(nct,),
        in_specs=[pl.BlockSpec((CHUNK, 2 * GQ_W), lambda i: (i, C_GQK // (2 * GQ_W))),
                  pl.BlockSpec((CHUNK, GV_W), lambda i: (i, C_GV // GV_W)),
                  pl.BlockSpec((CHUNK, LANE), lambda i: (i, C_SB // LANE)),
                  pl.BlockSpec((LANE, GQ_W), lambda i: (0, 0)), pl.BlockSpec((1, GQ_W), lambda i: (0, 0))],
        out_specs=[pl.BlockSpec((CHUNK, GV_W), lambda i: (i, 0)), pl.BlockSpec((CHUNK, GQ_W), lambda i: (i, 0)),
                   pl.BlockSpec((CHUNK, GQ_W), lambda i: (i, 0)), pl.BlockSpec((1, 1, GQ_W), lambda i: (i, 0, 0))],
        out_shape=[jax.ShapeDtypeStruct((n, GV_W), F32), jax.ShapeDtypeStruct((n, GQ_W), F32),
                   jax.ShapeDtypeStruct((n, GQ_W), F32), jax.ShapeDtypeStruct((nct, 1, GQ_W), F32)],
        compiler_params=_cp(VMEM_BIG),
    )(projp, projp, projp, w2p, gb)


def _gla_scan_fwd(oi, qg, kd, gl, projp, *, bsz, nc_seq, name):
    t_seq = nc_seq * CHUNK
    oi = oi.reshape(bsz, t_seq, GV_W)
    qg, kd = qg.reshape(bsz, t_seq, GQ_W), kd.reshape(bsz, t_seq, GQ_W)
    gl = gl.reshape(bsz, nc_seq, 1, GQ_W)
    pj = projp.reshape(bsz, t_seq, PW)

    def body(oi_ref, qg_ref, kd_ref, gl_ref, v_ref, o_ref, hist_ref, st_ref):
        @pl.when(pl.program_id(0) == 0)
        def _():
            st_ref[...] = jnp.zeros_like(st_ref)

        for b in range(bsz):
            st = st_ref[b]
            hist_ref[b, 0] = st
            qgb = qg_ref[b]
            kdb = kd_ref[b]
            upd = jnp.zeros((GLA_DV, GQ_W), F32)
            for h in range(NH):
                sl = slice(h * GLA_DV, (h + 1) * GLA_DV)
                m = _head_mask(h)
                o_ref[b, :, sl] = oi_ref[b, :, sl] + _nt(jnp.where(m, qgb, 0.0), st)
                upd = upd + jnp.where(m, _tn(v_ref[b, :, sl], kdb), 0.0)
            st_ref[b] = gl_ref[b, 0] * st + upd

    outs = pl.pallas_call(
        body, name=name, grid=(nc_seq,),
        in_specs=[pl.BlockSpec((bsz, CHUNK, GV_W), lambda i: (0, i, 0)),
                  pl.BlockSpec((bsz, CHUNK, GQ_W), lambda i: (0, i, 0)),
                  pl.BlockSpec((bsz, CHUNK, GQ_W), lambda i: (0, i, 0)),
                  pl.BlockSpec((bsz, 1, 1, GQ_W), lambda i: (0, i, 0, 0)),
                  pl.BlockSpec((bsz, CHUNK, GV_W), lambda i: (0, i, C_GV // GV_W))],
        out_specs=[pl.BlockSpec((bsz, CHUNK, GV_W), lambda i: (0, i, 0)),
                   pl.BlockSpec((bsz, 1, GLA_DV, GQ_W), lambda i: (0, i, 0, 0))],
        out_shape=[jax.ShapeDtypeStruct((bsz, t_seq, GV_W), F32),
                   jax.ShapeDtypeStruct((bsz, nc_seq, GLA_DV, GQ_W), F32)],
        scratch_shapes=[pltpu.VMEM((bsz, GLA_DV, GQ_W), F32)],
        compiler_params=_cp(VMEM_BIG, ("arbitrary",)),
    )(oi, qg, kd, gl, pj)
    return outs[0].reshape(bsz * t_seq, GV_W), outs[1]


def _gla_scan_bwd(do, qg, kd, gl, projp, hist, *, bsz, nc_seq, name):
    t_seq = nc_seq * CHUNK
    do = do.reshape(bsz, t_seq, GV_W)
    qg, kd = qg.reshape(bsz, t_seq, GQ_W), kd.reshape(bsz, t_seq, GQ_W)
    gl = gl.reshape(bsz, nc_seq, 1, GQ_W)
    pj = projp.reshape(bsz, t_seq, PW)

    def body(do_ref, qg_ref, kd_ref, gl_ref, v_ref, hist_ref, dqg_ref, dkd_ref, dv_ref, dgl_ref, dst_ref):
        @pl.when(pl.program_id(0) == 0)
        def _():
            dst_ref[...] = jnp.zeros_like(dst_ref)

        for b in range(bsz):
            st = hist_ref[b, 0]
            dst = dst_ref[b]
            qgb = qg_ref[b]
            kdb = kd_ref[b]
            dqg = jnp.zeros((CHUNK, GQ_W), F32)
            dkd = jnp.zeros((CHUNK, GQ_W), F32)
            add = jnp.zeros((GLA_DV, GQ_W), F32)
            for h in range(NH):
                sl = slice(h * GLA_DV, (h + 1) * GLA_DV)
                m = _head_mask(h)
                doh = do_ref[b, :, sl]
                vh = v_ref[b, :, sl]
                dqg = dqg + jnp.where(m, _nn(doh, st), 0.0)
                dkd = dkd + jnp.where(m, _nn(vh, dst), 0.0)
                dv_ref[b, :, sl] = _nt(jnp.where(m, kdb, 0.0), dst)
                add = add + jnp.where(m, _tn(doh, qgb), 0.0)
            dqg_ref[b] = dqg
            dkd_ref[b] = dkd
            dgl_ref[b, 0] = jnp.sum(dst * st, axis=0, keepdims=True)
            dst_ref[b] = gl_ref[b, 0] * dst + add

    rev = lambda i: nc_seq - 1 - i
    outs = pl.pallas_call(
        body, name=name, grid=(nc_seq,),
        in_specs=[pl.BlockSpec((bsz, CHUNK, GV_W), lambda i: (0, rev(i), 0)),
                  pl.BlockSpec((bsz, CHUNK, GQ_W), lambda i: (0, rev(i), 0)),
                  pl.BlockSpec((bsz, CHUNK, GQ_W), lambda i: (0, rev(i), 0)),
                  pl.BlockSpec((bsz, 1, 1, GQ_W), lambda i: (0, rev(i), 0, 0)),
                  pl.BlockSpec((bsz, CHUNK, GV_W), lambda i: (0, rev(i), C_GV // GV_W)),
                  pl.BlockSpec((bsz, 1, GLA_DV, GQ_W), lambda i: (0, rev(i), 0, 0))],
        out_specs=[pl.BlockSpec((bsz, CHUNK, GQ_W), lambda i: (0, rev(i), 0)),
                   pl.BlockSpec((bsz, CHUNK, GQ_W), lambda i: (0, rev(i), 0)),
                   pl.BlockSpec((bsz, CHUNK, GV_W), lambda i: (0, rev(i), 0)),
                   pl.BlockSpec((bsz, 1, 1, GQ_W), lambda i: (0, rev(i), 0, 0))],
        out_shape=[jax.ShapeDtypeStruct((bsz, t_seq, GQ_W), F32), jax.ShapeDtypeStruct((bsz, t_seq, GQ_W), F32),
                   jax.ShapeDtypeStruct((bsz, t_seq, GV_W), F32), jax.ShapeDtypeStruct((bsz, nc_seq, 1, GQ_W), F32)],
        scratch_shapes=[pltpu.VMEM((bsz, GLA_DV, GQ_W), F32)],
        compiler_params=_cp(VMEM_BIG, ("arbitrary",)),
    )(do, qg, kd, gl, pj, hist)
    n = bsz * t_seq
    return (outs[0].reshape(n, GQ_W), outs[1].reshape(n, GQ_W), outs[2].reshape(n, GV_W),
            outs[3].reshape(bsz * nc_seq, 1, GQ_W))


def _gla_intra_bwd(projp, w2p, gb, do, dqg, dkd, dvi, dgl, *, nc_seq, name):
    n = projp.shape[0]
    nct = n // CHUNK
    scale = GLA_DK ** -0.5

    def body(qk_ref, v_ref, sb_ref, w2_ref, gb_ref, do_ref, dqg_ref, dkd_ref, dvi_ref, dgl_ref,
             dqk_ref, dv_ref, dsb_ref, dw2_ref, dgb_ref):
        ci = pl.program_id(0) % nc_seq
        sb = sb_ref[...]
        w2 = w2_ref[...]
        graw, bc, valid = _gla_gates(sb, w2, gb_ref[...], ci)
        bref = bc[MID:MID + 1, :]
        bl = bc[CHUNK - 1:CHUNK, :]
        q = qk_ref[:, 0:GQ_W] * scale
        k = qk_ref[:, GQ_W:2 * GQ_W]
        ex1 = jnp.exp(bc - bref)
        ex2 = jnp.exp(bref - bc)
        eb = jnp.exp(bc)
        ekd = jnp.exp(bl - bc)
        qi, ki = q * ex1, k * ex2
        r, c = _masks64()
        incl = r >= c
        upper = r <= c
        dqi = jnp.zeros((CHUNK, GQ_W), F32)
        dki = jnp.zeros((CHUNK, GQ_W), F32)
        for h in range(NH):
            sl = slice(h * GLA_DV, (h + 1) * GLA_DV)
            m = _head_mask(h)
            doh = do_ref[:, sl]
            vh = v_ref[:, sl]
            a_t = jnp.where(upper, _nt(jnp.where(m, ki, 0.0), qi), 0.0)
            da = jnp.where(incl, _nt(doh, vh), 0.0)
            da_t = jnp.where(upper, _nt(vh, doh), 0.0)
            dv_ref[:, sl] = _nn(a_t, doh) + dvi_ref[:, sl]
            dqi = dqi + jnp.where(m, _nn(da, ki), 0.0)
            dki = dki + jnp.where(m, _nn(da_t, qi), 0.0)
        dqg = dqg_ref[...]
        dkd = dkd_ref[...]
        dqk_ref[:, 0:GQ_W] = (dqi * ex1 + dqg * eb) * scale
        dqk_ref[:, GQ_W:2 * GQ_W] = dki * ex2 + dkd * ekd
        t_qi, t_ki, t_kd = dqi * qi, dki * ki, dkd * (k * ekd)
        db = t_qi - t_ki + dqg * (q * eb) - t_kd
        dbref = jnp.sum(t_ki - t_qi, axis=0, keepdims=True)
        dbl = jnp.sum(t_kd, axis=0, keepdims=True) + dgl_ref[0] * jnp.exp(bl)
        rows = _iota2((CHUNK, GQ_W), 0)
        db = db + jnp.where(rows == MID, dbref, 0.0) + jnp.where(rows == CHUNK - 1, dbl, 0.0)
        dg = _nn_hi(upper.astype(F32), db)
        dgraw = jnp.where(valid, dg * (1.0 / GLA_NORM) * _sigmoid(-graw), 0.0)
        dsb_ref[...] = _nt_hi(dgraw, w2)
        dw2 = _tn_hi(sb, dgraw)
        dgb = jnp.sum(dgraw, axis=0, keepdims=True)
        first = pl.program_id(0) == 0

        @pl.when(first)
        def _():
            dw2_ref[...] = dw2
            dgb_ref[...] = dgb

        @pl.when(jnp.logical_not(first))
        def _():
            dw2_ref[...] += dw2
            dgb_ref[...] += dgb

    rq = pl.BlockSpec((CHUNK, GQ_W), lambda i: (i, 0))
    rv = pl.BlockSpec((CHUNK, GV_W), lambda i: (i, 0))
    return pl.pallas_call(
        body, name=name, grid=(nct,),
        in_specs=[pl.BlockSpec((CHUNK, 2 * GQ_W), lambda i: (i, C_GQK // (2 * GQ_W))),
                  pl.BlockSpec((CHUNK, GV_W), lambda i: (i, C_GV // GV_W)),
                  pl.BlockSpec((CHUNK, LANE), lambda i: (i, C_SB // LANE)),
                  pl.BlockSpec((LANE, GQ_W), lambda i: (0, 0)), pl.BlockSpec((1, GQ_W), lambda i: (0, 0)),
                  rv, rq, rq, rv, pl.BlockSpec((1, 1, GQ_W), lambda i: (i, 0, 0))],
        out_specs=[pl.BlockSpec((CHUNK, 2 * GQ_W), lambda i: (i, 0)), rv, pl.BlockSpec((CHUNK, LANE), lambda i: (i, 0)),
                   pl.BlockSpec((LANE, GQ_W), lambda i: (0, 0)), pl.BlockSpec((1, GQ_W), lambda i: (0, 0))],
        out_shape=[jax.ShapeDtypeStruct((n, 2 * GQ_W), F32), jax.ShapeDtypeStruct((n, GV_W), F32),
                   jax.ShapeDtypeStruct((n, LANE), F32), jax.ShapeDtypeStruct((LANE, GQ_W), F32),
                   jax.ShapeDtypeStruct((1, GQ_W), F32)],
        compiler_params=_cp(VMEM_BIG, ("arbitrary",)),
    )(projp, projp, projp, w2p, gb, do, dqg, dkd, dvi, dgl)


SECTIONS = ((C_QKV, 1536), (C_DZ, 512), (C_GQK, 512), (C_GV, 512), (C_GR, 512), (C_SA, 128), (C_SB, 128))


def _inproj_bwd(secs, wp, h0, g1, dx1, *, tr, name):
    n, d = h0.shape

    def body(*refs):
        sec_refs = refs[:len(SECTIONS)]
        wp_ref, h0_ref, g_ref, dx1_ref, o_ref, dg_ref = refs[len(SECTIONS):]
        dh = None
        for s_ref, (off, wd) in zip(sec_refs, SECTIONS):
            part = _nt(s_ref[...], wp_ref[:, off:off + wd])
            dh = part if dh is None else dh + part
        dx, dg = _rms_bwd_math(h0_ref[...], g_ref[...], dh)
        o_ref[...] = dx1_ref[...] + dx

        @pl.when(pl.program_id(0) == 0)
        def _():
            dg_ref[...] = dg

        @pl.when(pl.program_id(0) > 0)
        def _():
            dg_ref[...] += dg

    row = pl.BlockSpec((tr, d), lambda i: (i, 0))
    vec = pl.BlockSpec((1, d), lambda i: (0, 0))
    return pl.pallas_call(
        body, name=name, grid=(n // tr,),
        in_specs=[pl.BlockSpec((tr, wd), lambda i: (i, 0)) for _, wd in SECTIONS]
        + [pl.BlockSpec((d, PW), lambda i: (0, 0)), row, vec, row],
        out_specs=[row, vec],
        out_shape=[jax.ShapeDtypeStruct((n, d), F32), jax.ShapeDtypeStruct((1, d), F32)],
        compiler_params=_cp(VMEM_BIG),
    )(*secs, wp, h0, g1, dx1)


def _adamw(w, g, m, v, *, name):
    r, c = w.shape
    tr = _tile(r, 256, 8) if r > 256 else r
    c1 = 1.0 - ADAM_B1 ** ADAM_STEP
    c2 = 1.0 - ADAM_B2 ** ADAM_STEP

    def body(w_ref, g_ref, m_ref, v_ref, d_ref, nm_ref, nv_ref):
        gv = g_ref[...]
        nm = ADAM_B1 * m_ref[...] + (1.0 - ADAM_B1) * gv
        nv = ADAM_B2 * v_ref[...] + (1.0 - ADAM_B2) * (gv * gv)
        d_ref[...] = -ADAM_LR * ((nm / c1) / (jnp.sqrt(nv / c2) + ADAM_EPS) + ADAM_WD * w_ref[...])
        nm_ref[...] = nm
        nv_ref[...] = nv

    blk = pl.BlockSpec((tr, c), lambda i: (i, 0))
    sds = jax.ShapeDtypeStruct((r, c), F32)
    return pl.pallas_call(
        body, name=name, grid=(r // tr,), in_specs=[blk] * 4, out_specs=[blk] * 3, out_shape=[sds] * 3,
        compiler_params=_cp(VMEM_BIG),
    )(w, g, m, v)


def _add2(a, b, *, name):
    lead, r, c = a.shape
    tr = _tile(r, 256, 8)

    def body(a_ref, b_ref, o_ref):
        o_ref[...] = a_ref[...] + b_ref[...]

    blk = pl.BlockSpec((1, tr, c), lambda s, i: (s, i, 0))
    return pl.pallas_call(
        body, name=name, grid=(lead, r // tr), in_specs=[blk, blk], out_specs=blk,
        out_shape=jax.ShapeDtypeStruct(a.shape, F32), compiler_params=_cp(VMEM_BIG),
    )(a, b)


def _sum_chips(q, *, name):
    _, r, c = q.shape
    tr = _tile(r, 256, 8)

    def body(q_ref, o_ref):
        o_ref[...] = ((q_ref[0] + q_ref[1]) + q_ref[2]) + q_ref[3]

    return pl.pallas_call(
        body, name=name, grid=(r // tr,),
        in_specs=[pl.BlockSpec((N_CHIPS, tr, c), lambda i: (0, i, 0))],
        out_specs=pl.BlockSpec((tr, c), lambda i: (i, 0)),
        out_shape=jax.ShapeDtypeStruct((r, c), F32), compiler_params=_cp(VMEM_BIG),
    )(q)


ANY = pl.BlockSpec(memory_space=pl.ANY)
VM = pl.BlockSpec(memory_space=pltpu.VMEM)
CAST_ROWS = 64


def _place():
    return lax.axis_index("x"), lax.axis_index("y"), lax.axis_index("c")


def _other_chips(x, y):
    return [(1 - x, y, 2 * (1 - x) + y), (x, 1 - y, 2 * x + 1 - y), (1 - x, 1 - y, 2 * (1 - x) + 1 - y)]


def _gather_weights(w_in, w_out, w_up, w_down, meta, conv, gw2):
    big = (w_in, w_out, w_up, w_down)
    small = (meta, conv, gw2)
    n_arr = len(big) + len(small)

    def body(*refs):
        ins = refs[:n_arr]
        outs = refs[n_arr:2 * n_arr]
        stage = refs[2 * n_arr:2 * n_arr + len(big)]
        send_sems, recv_sems, local_sems = refs[2 * n_arr + len(big):]
        x, y, c = _place()
        me = 2 * x + y
        srcs = []
        for k in range(n_arr):
            if k < len(big):
                src, dst = ins[k], stage[k]

                def cast_rows(i, carry, src=src, dst=dst):
                    rows = pl.ds(pl.multiple_of(i * CAST_ROWS, CAST_ROWS), CAST_ROWS)
                    dst[rows, :] = src[rows, :].astype(BF16)
                    return carry

                lax.fori_loop(0, src.shape[0] // CAST_ROWS, cast_rows, 0)
                srcs.append(stage[k])
            else:
                srcs.append(ins[k])
        local = [pltpu.make_async_copy(srcs[k], outs[k].at[me], local_sems.at[k]) for k in range(n_arr)]
        for cp in local:
            cp.start()
        sends = []
        for k in range(n_arr):
            for d, (px, py, _) in enumerate(_other_chips(x, y)):
                cp = pltpu.make_async_remote_copy(
                    src_ref=srcs[k], dst_ref=outs[k].at[me], send_sem=send_sems.at[k, d], recv_sem=recv_sems.at[k, d],
                    device_id=(px, py, c), device_id_type=MESH)
                cp.start()
                sends.append(cp)
        for k in range(n_arr):
            for d, (px, py, pj) in enumerate(_other_chips(x, y)):
                pltpu.make_async_remote_copy(
                    src_ref=srcs[k], dst_ref=outs[k].at[pj], send_sem=send_sems.at[k, d], recv_sem=recv_sems.at[k, d],
                    device_id=(px, py, c), device_id_type=MESH).wait_recv()
        for cp in sends:
            cp.wait_send()
        for cp in local:
            cp.wait()

    out_shape = [jax.ShapeDtypeStruct((N_CHIPS,) + a.shape, BF16) for a in big]
    out_shape += [jax.ShapeDtypeStruct((N_CHIPS,) + a.shape, F32) for a in small]
    return pl.pallas_call(
        body, name="gather_weights", in_specs=[VM] * n_arr, out_specs=[ANY] * n_arr, out_shape=out_shape,
        scratch_shapes=[pltpu.VMEM(a.shape, BF16) for a in big]
        + [pltpu.SemaphoreType.DMA((n_arr, 3)), pltpu.SemaphoreType.DMA((n_arr, 3)), pltpu.SemaphoreType.DMA((n_arr,))],
        compiler_params=_cp(VMEM_BIG),
    )(*big, *small)


def _sibling_halves(grads):
    n_arr = len(grads)

    def body(*refs):
        ins = refs[:n_arr]
        mine = refs[n_arr:2 * n_arr]
        theirs = refs[2 * n_arr:3 * n_arr]
        send_sems, recv_sems, local_sems = refs[3 * n_arr:]
        x, y, c = _place()
        copies = []
        for k in range(n_arr):
            half = ins[k].shape[1] // 2
            keep = pltpu.make_async_copy(ins[k].at[:, pl.ds(c * half, half), :], mine[k], local_sems.at[k])
            keep.start()
            give = pltpu.make_async_remote_copy(
                src_ref=ins[k].at[:, pl.ds((1 - c) * half, half), :], dst_ref=theirs[k],
                send_sem=send_sems.at[k], recv_sem=recv_sems.at[k], device_id=(x, y, 1 - c), device_id_type=MESH)
            give.start()
            copies.append((keep, give))
        for keep, give in copies:
            give.wait()
            keep.wait()

    halves = [jax.ShapeDtypeStruct((g.shape[0], g.shape[1] // 2, g.shape[2]), F32) for g in grads]
    outs = pl.pallas_call(
        body, name="sibling_halves", in_specs=[ANY] * n_arr, out_specs=[ANY] * (2 * n_arr), out_shape=halves + halves,
        scratch_shapes=[pltpu.SemaphoreType.DMA((n_arr,)), pltpu.SemaphoreType.DMA((n_arr,)),
                        pltpu.SemaphoreType.DMA((n_arr,))],
    )(*grads)
    return outs[:n_arr], outs[n_arr:]


def _chip_exchange(parts):
    n_arr = len(parts)

    def body(*refs):
        ins = refs[:n_arr]
        outs = refs[n_arr:2 * n_arr]
        send_sems, recv_sems, local_sems = refs[2 * n_arr:]
        x, y, c = _place()
        me = 2 * x + y
        local, sends = [], []
        for k in range(n_arr):
            cp = pltpu.make_async_copy(ins[k].at[me], outs[k].at[me], local_sems.at[k])
            cp.start()
            local.append(cp)
            for d, (px, py, pj) in enumerate(_other_chips(x, y)):
                cp = pltpu.make_async_remote_copy(
                    src_ref=ins[k].at[pj], dst_ref=outs[k].at[me], send_sem=send_sems.at[k, d],
                    recv_sem=recv_sems.at[k, d], device_id=(px, py, c), device_id_type=MESH)
                cp.start()
                sends.append(cp)
        for k in range(n_arr):
            for d, (px, py, pj) in enumerate(_other_chips(x, y)):
                pltpu.make_async_remote_copy(
                    src_ref=ins[k].at[pj], dst_ref=outs[k].at[pj], send_sem=send_sems.at[k, d],
                    recv_sem=recv_sems.at[k, d], device_id=(px, py, c), device_id_type=MESH).wait_recv()
        for cp in sends:
            cp.wait_send()
        for cp in local:
            cp.wait()

    return pl.pallas_call(
        body, name="chip_exchange", in_specs=[ANY] * n_arr, out_specs=[ANY] * n_arr,
        out_shape=[jax.ShapeDtypeStruct(p.shape, F32) for p in parts],
        scratch_shapes=[pltpu.SemaphoreType.DMA((n_arr, 3)), pltpu.SemaphoreType.DMA((n_arr, 3)),
                        pltpu.SemaphoreType.DMA((n_arr,))],
    )(*parts)


def _sibling_join(halves):
    n_arr = len(halves)

    def body(*refs):
        ins = refs[:n_arr]
        outs = refs[n_arr:2 * n_arr]
        send_sems, recv_sems, local_sems = refs[2 * n_arr:]
        x, y, c = _place()
        copies = []
        for k in range(n_arr):
            keep = pltpu.make_async_copy(ins[k], outs[k].at[c], local_sems.at[k])
            keep.start()
            give = pltpu.make_async_remote_copy(
                src_ref=ins[k], dst_ref=outs[k].at[c], send_sem=send_sems.at[k], recv_sem=recv_sems.at[k],
                device_id=(x, y, 1 - c), device_id_type=MESH)
            give.start()
            copies.append((keep, give))
        for k, (keep, give) in enumerate(copies):
            pltpu.make_async_remote_copy(
                src_ref=ins[k], dst_ref=outs[k].at[1 - c], send_sem=send_sems.at[k], recv_sem=recv_sems.at[k],
                device_id=(x, y, 1 - c), device_id_type=MESH).wait_recv()
            give.wait_send()
            keep.wait()

    return pl.pallas_call(
        body, name="sibling_join", in_specs=[ANY] * n_arr, out_specs=[ANY] * n_arr,
        out_shape=[jax.ShapeDtypeStruct((2,) + h.shape, F32) for h in halves],
        scratch_shapes=[pltpu.SemaphoreType.DMA((n_arr,)), pltpu.SemaphoreType.DMA((n_arr,)),
                        pltpu.SemaphoreType.DMA((n_arr,))],
    )(*halves)


PACK_ROWS = 48


def _small_allreduce(pack):
    masks = [(dx, dy, dc) for dx in (0, 1) for dy in (0, 1) for dc in (0, 1)][1:]

    def body(p_ref, o_ref, buf, send_sems, recv_sems):
        x, y, c = _place()
        me = 4 * x + 2 * y + c
        buf[me] = p_ref[...]
        sends = []
        for k, (dx, dy, dc) in enumerate(masks):
            peer = (1 - x if dx else x, 1 - y if dy else y, 1 - c if dc else c)
            cp = pltpu.make_async_remote_copy(
                src_ref=p_ref, dst_ref=buf.at[me], send_sem=send_sems.at[k], recv_sem=recv_sems.at[k],
                device_id=peer, device_id_type=MESH)
            cp.start()
            sends.append(cp)
        for k, (dx, dy, dc) in enumerate(masks):
            peer = (1 - x if dx else x, 1 - y if dy else y, 1 - c if dc else c)
            pj = 4 * peer[0] + 2 * peer[1] + peer[2]
            pltpu.make_async_remote_copy(
                src_ref=p_ref, dst_ref=buf.at[pj], send_sem=send_sems.at[k], recv_sem=recv_sems.at[k],
                device_id=peer, device_id_type=MESH).wait_recv()
        for cp in sends:
            cp.wait_send()
        tot = buf[0]
        for k in range(1, 8):
            tot = tot + buf[k]
        o_ref[...] = tot
        o_ref[0:N_META, :] = tot[0:N_META] + tot[N_META:2 * N_META]

    return pl.pallas_call(
        body, name="small_allreduce", in_specs=[VM], out_specs=VM,
        out_shape=jax.ShapeDtypeStruct((PACK_ROWS, D_MODEL), F32),
        scratch_shapes=[pltpu.VMEM((8, PACK_ROWS, D_MODEL), F32), pltpu.SemaphoreType.DMA((7,)),
                        pltpu.SemaphoreType.DMA((7,))],
    )(pack)


def _pad_lanes(vec, offset):
    k = vec.shape[1]
    return jnp.concatenate([jnp.zeros((1, offset), F32), vec, jnp.zeros((1, LANE - offset - k), F32)], axis=1)


def _local_step(x, tgt, meta, norm1_g, wp, conv_w, a_log, dt_bias, dn_norm_g, gla_w2, gla_b, gla_norm_g,
                w_out, norm2_g, w_up, w_down, final_norm_g):
    bsz, s_len, d = x.shape
    t_seq = s_len + CHUNK
    nc_seq = t_seq // CHUNK
    n = bsz * t_seq
    tr = _tile(t_seq, 832)
    tt = _tile(t_seq, 416)

    lead = jnp.concatenate([jnp.zeros((N_PAD, d), F32), meta], axis=0)
    h0 = jnp.concatenate([jnp.broadcast_to(lead[None], (bsz, CHUNK, d)), x], axis=1).reshape(n, d)
    tgt_p = jnp.concatenate([jnp.zeros((bsz, CHUNK, d), F32), tgt], axis=1).reshape(n, d)
    alog_row = _pad_lanes(a_log, 4)
    dtb_row = _pad_lanes(dt_bias, 4)
    w2p = jnp.concatenate([gla_w2, jnp.zeros((LANE - GLA_RANK, GQ_W), F32)], axis=0)

    h = _rms_fwd(h0, norm1_g, tr=tr, name="norm1")
    (projp,) = _mm(h, wp, "nn", tm=tr, tn=1280, tk=d, out_dtypes=(F32,), name="in_proj")
    qn, kn, v = _dnprep_fwd(projp, conv_w, bsz=bsz, t_seq=t_seq, tt=tt, name="dn_prep")
    u, w, qg, kd, pmat, tmat, gl = _dn_intra_fwd(qn, kn, v, projp, alog_row, dtb_row, nc_seq=nc_seq, name="dn_intra")
    o_dn, vn, hist = _dn_scan_fwd(u, w, qg, kd, pmat, gl, bsz=bsz, nc_seq=nc_seq, name="dn_scan")
    oi, gqg, gkd, ggl = _gla_intra_fwd(projp, w2p, gla_b, nc_seq=nc_seq, name="gla_intra")
    o_gla, ghist = _gla_scan_fwd(oi, gqg, gkd, ggl, projp, bsz=bsz, nc_seq=nc_seq, name="gla_scan")
    mix = _gnorm_fwd(o_dn, o_gla, projp, dn_norm_g, gla_norm_g, tr=tr, name="gated_norm")
    (x1,) = _mm(mix, w_out, "nn", tm=tr, tn=d, tk=d, out_dtypes=(F32,), extras=(h0,),
                epilogue=lambda acc, res: (res + acc,), name="out_proj")
    h2 = _rms_fwd(x1, norm2_g, tr=tr, name="norm2")

    def act_epilogue(acc):
        r = jnp.maximum(acc, 0.0)
        return acc, r * r

    up, act = _mm(h2, w_up, "nn", tm=tr, tn=1024, tk=d, out_dtypes=(BF16, BF16), epilogue=act_epilogue, name="mlp_up")
    (x2,) = _mm(act, w_down, "nn", tm=tr, tn=d, tk=1024, out_dtypes=(F32,), extras=(x1,),
                epilogue=lambda acc, res: (res + acc,), name="mlp_down")
    dx2, d_final_g, loss_tile = _final_loss(x2, final_norm_g, tgt_p, t_seq=t_seq, tr=tr, name="final_loss")

    (dup,) = _mm(dx2, w_down, "nt", tm=tr, tn=1024, tk=d, out_dtypes=(BF16,), extras=(up,),
                 epilogue=lambda acc, upv: (acc * (2.0 * jnp.maximum(upv.astype(F32), 0.0)),), name="mlp_down_bwd")
    (d_w_down,) = _mm(act, dx2, "tn", tm=1024, tn=d, tk=tr, out_dtypes=(F32,), name="w_down_grad")
    (d_w_up,) = _mm(h2, dup, "tn", tm=d, tn=1024, tk=tr, out_dtypes=(F32,), name="w_up_grad")
    (dh2,) = _mm(dup, w_up, "nt", tm=tr, tn=d, tk=1024, out_dtypes=(F32,), name="mlp_up_bwd")
    dx1, d_norm2_g = _rms_bwd_add(x1, norm2_g, dh2, dx2, tr=tr, name="norm2_bwd")

    (dmix,) = _mm(dx1, w_out, "nt", tm=tr, tn=d, tk=d, out_dtypes=(F32,), name="out_proj_bwd")
    (d_w_out,) = _mm(mix, dx1, "tn", tm=d, tn=d, tk=tr, out_dtypes=(F32,), name="w_out_grad")
    do_dn, ddz, do_gla, dgr, d_dn_norm_g, d_gla_norm_g = _gnorm_bwd(
        dmix, o_dn, o_gla, projp, dn_norm_g, gla_norm_g, tr=tr, name="gated_norm_bwd")
    du, dw, dqg, dkd, dpm, dgl = _dn_scan_bwd(do_dn, w, qg, kd, vn, pmat, gl, hist, bsz=bsz, nc_seq=nc_seq,
                                               name="dn_scan_bwd")
    dqn, dkn, dv, dsa, d_alog, d_dtb = _dn_intra_bwd(qn, kn, v, projp, alog_row, dtb_row, u, w, tmat,
                                                     du, dw, dqg, dkd, dpm, dgl, nc_seq=nc_seq, name="dn_intra_bwd")
    dz, d_conv_w = _dnprep_bwd_a(projp, conv_w, dqn, dkn, dv, bsz=bsz, t_seq=t_seq, tt=tt, name="dn_prep_bwd")
    dcin = _dnprep_bwd_b(dz, conv_w, bsz=bsz, t_seq=t_seq, tt=tt, name="conv_bwd")
    gdqg, gdkd, gdvi, gdgl = _gla_scan_bwd(do_gla, gqg, gkd, ggl, projp, ghist, bsz=bsz, nc_seq=nc_seq,
                                            name="gla_scan_bwd")
    dgqk, dgv, dsb, d_w2p, d_gla_b = _gla_intra_bwd(projp, w2p, gla_b, do_gla, gdqg, gdkd, gdvi, gdgl,
                                                    nc_seq=nc_seq, name="gla_intra_bwd")

    secs = (dcin, ddz, dgqk, dgv, dgr, dsa, dsb)
    d_wp_secs = []
    for idx, (sec, (_, wd)) in enumerate(zip(secs, SECTIONS)):
        (g_sec,) = _mm(h, sec, "tn", tm=d, tn=min(wd, 512), tk=tr, out_dtypes=(F32,), name=f"w_in_grad_{idx}")
        d_wp_secs.append(g_sec)
    dh0, d_norm1_g = _inproj_bwd(secs, wp, h0, norm1_g, dx1, tr=tt, name="in_proj_bwd")
    dh0 = dh0.reshape(bsz, t_seq, d)
    grad_x = dh0[:, CHUNK:]
    d_meta_rows = dh0[:, N_PAD:CHUNK].reshape(bsz * N_META, d)

    g_qkv, g_dz, g_gqk, g_gv, g_gr, g_sa, g_sb = d_wp_secs
    d_w_in = jnp.concatenate([g_qkv, g_dz, g_sa[:, 0:8], g_gqk, g_gv, g_gr, g_sb[:, 0:GLA_RANK]], axis=1)
    grads = dict(w_in=d_w_in, w_out=d_w_out, w_up=d_w_up, w_down=d_w_down, meta_rows=d_meta_rows,
                 norm1_g=d_norm1_g, conv_w=d_conv_w, a_log_tile=d_alog, dt_bias_tile=d_dtb, dn_norm_g=d_dn_norm_g,
                 gla_w2=d_w2p[0:GLA_RANK], gla_b=d_gla_b, gla_norm_g=d_gla_norm_g, norm2_g=d_norm2_g,
                 final_norm_g=d_final_g, loss_tile=loss_tile)
    return grad_x, grads


def _pad_layout(w_full):
    z = lambda k: jnp.zeros((w_full.shape[0], k), w_full.dtype)
    return jnp.concatenate([w_full[:, 0:2048], w_full[:, 2056:3592], w_full[:, 2048:2056], z(LANE - 8),
                            w_full[:, 3592:3608], z(LANE - GLA_RANK)], axis=1)


def _pack_small(g, bsz):
    assert bsz * N_META == 32
    row = jnp.concatenate([g["a_log_tile"], g["dt_bias_tile"], g["dn_norm_g"], g["gla_norm_g"], g["gla_b"],
                           g["loss_tile"], jnp.zeros((1, LANE), F32)], axis=1)
    return jnp.concatenate([g["meta_rows"], g["norm1_g"], g["conv_w"].reshape(6, D_MODEL), row,
                            g["gla_w2"].reshape(4, D_MODEL), g["norm2_g"], g["final_norm_g"],
                            jnp.zeros((2, D_MODEL), F32)], axis=0)


def kernel(x, meta_tokens, norm1_g, w_in, conv_w, a_log, dt_bias, dn_norm_g, gla_w2, gla_b, gla_norm_g, w_out, norm2_g, w_up, w_down, final_norm_g, loss_target, m_meta_tokens, m_norm1_g, m_w_in, m_conv_w, m_a_log, m_dt_bias, m_dn_norm_g, m_gla_w2, m_gla_b, m_gla_norm_g, m_w_out, m_norm2_g, m_w_up, m_w_down, m_final_norm_g, v_meta_tokens, v_norm1_g, v_w_in, v_conv_w, v_a_log, v_dt_bias, v_dn_norm_g, v_gla_w2, v_gla_b, v_gla_norm_g, v_w_out, v_norm2_g, v_w_up, v_w_down, v_final_norm_g):
    bsz = x.shape[0]
    chip = 2 * lax.axis_index("x") + lax.axis_index("y")

    g_in, g_out, g_up, g_down, g_meta, g_conv, g_w2 = _gather_weights(
        w_in[0], w_out[0], w_up[0], w_down[0], meta_tokens, conv_w[0], gla_w2[0])
    wp = _pad_layout(g_in.transpose(1, 0, 2).reshape(D_MODEL, IN_WIDTH))
    w_out_f = g_out.reshape(D_MODEL, D_MODEL)
    w_up_f = g_up.transpose(1, 0, 2).reshape(D_MODEL, D_FF)
    w_down_f = g_down.reshape(D_FF, D_MODEL)
    meta_f = g_meta.transpose(1, 0, 2).reshape(N_META, D_MODEL)
    conv_f = g_conv.transpose(1, 0, 2).reshape(4, QKV_W)
    w2_f = g_w2.transpose(1, 0, 2).reshape(GLA_RANK, GQ_W)

    grad_x, g = _local_step(x, loss_target, meta_f, norm1_g, wp, conv_f, a_log, dt_bias, dn_norm_g, w2_f, gla_b,
                            gla_norm_g, w_out_f, norm2_g, w_up_f, w_down_f, final_norm_g.reshape(1, D_MODEL))

    shard_major = [
        g["w_in"].reshape(D_MODEL, N_CHIPS, IN_WIDTH // N_CHIPS).transpose(1, 0, 2),
        g["w_out"].reshape(N_CHIPS, D_MODEL // N_CHIPS, D_MODEL),
        g["w_up"].reshape(D_MODEL, N_CHIPS, D_FF // N_CHIPS).transpose(1, 0, 2),
        g["w_down"].reshape(N_CHIPS, D_FF // N_CHIPS, D_MODEL),
    ]
    mine, theirs = _sibling_halves(shard_major)
    pair = [_add2(a, b, name=f"pair_sum_{k}") for k, (a, b) in enumerate(zip(mine, theirs))]
    parts = _chip_exchange(pair)
    halves = [_sum_chips(q, name=f"chip_sum_{k}") for k, q in enumerate(parts)]
    joined = _sibling_join(halves)
    gw_in, gw_out, gw_up, gw_down = [j.reshape(2 * j.shape[1], j.shape[2]) for j in joined]

    red = _small_allreduce(_pack_small(g, bsz))
    g_meta_full = red[0:N_META]
    g_norm1 = red[32:33]
    g_conv_full = red[33:39].reshape(4, QKV_W)
    srow = red[39:40]
    g_alog, g_dtb = srow[:, 4:8], srow[:, LANE + 4:LANE + 8]
    g_dn_norm, g_gla_norm = srow[:, 2 * LANE:3 * LANE], srow[:, 3 * LANE:4 * LANE]
    g_gla_b = srow[:, 4 * LANE:6 * LANE]
    loss = srow[0, 6 * LANE]
    g_w2_full = red[40:44].reshape(GLA_RANK, GQ_W)
    g_norm2 = red[44:45]
    g_final = red[45:46]
    g_meta_sh = lax.dynamic_slice_in_dim(g_meta_full, chip * (D_MODEL // N_CHIPS), D_MODEL // N_CHIPS, axis=1)
    g_conv_sh = lax.dynamic_slice_in_dim(g_conv_full, chip * (QKV_W // N_CHIPS), QKV_W // N_CHIPS, axis=1)
    g_w2_sh = lax.dynamic_slice_in_dim(g_w2_full, chip * (GQ_W // N_CHIPS), GQ_W // N_CHIPS, axis=1)

    names = ["meta_tokens", "norm1_g", "w_in", "conv_w", "a_log", "dt_bias", "dn_norm_g", "gla_w2", "gla_b",
             "gla_norm_g", "w_out", "norm2_g", "w_up", "w_down", "final_norm_g"]
    weights = dict(meta_tokens=meta_tokens, norm1_g=norm1_g, w_in=w_in, conv_w=conv_w, a_log=a_log, dt_bias=dt_bias,
                   dn_norm_g=dn_norm_g, gla_w2=gla_w2, gla_b=gla_b, gla_norm_g=gla_norm_g, w_out=w_out,
                   norm2_g=norm2_g, w_up=w_up, w_down=w_down, final_norm_g=final_norm_g)
    ms = dict(meta_tokens=m_meta_tokens, norm1_g=m_norm1_g, w_in=m_w_in, conv_w=m_conv_w, a_log=m_a_log,
              dt_bias=m_dt_bias, dn_norm_g=m_dn_norm_g, gla_w2=m_gla_w2, gla_b=m_gla_b, gla_norm_g=m_gla_norm_g,
              w_out=m_w_out, norm2_g=m_norm2_g, w_up=m_w_up, w_down=m_w_down, final_norm_g=m_final_norm_g)
    vs = dict(meta_tokens=v_meta_tokens, norm1_g=v_norm1_g, w_in=v_w_in, conv_w=v_conv_w, a_log=v_a_log,
              dt_bias=v_dt_bias, dn_norm_g=v_dn_norm_g, gla_w2=v_gla_w2, gla_b=v_gla_b, gla_norm_g=v_gla_norm_g,
              w_out=v_w_out, norm2_g=v_norm2_g, w_up=v_w_up, w_down=v_w_down, final_norm_g=v_final_norm_g)
    grads2d = dict(meta_tokens=g_meta_sh, norm1_g=g_norm1, w_in=gw_in, conv_w=g_conv_sh, a_log=g_alog, dt_bias=g_dtb,
                   dn_norm_g=g_dn_norm, gla_w2=g_w2_sh, gla_b=g_gla_b, gla_norm_g=g_gla_norm, w_out=gw_out,
                   norm2_g=g_norm2, w_up=gw_up, w_down=gw_down, final_norm_g=g_final)
    out_g, out_d, out_m, out_v = [], [], [], []
    for nm in names:
        shape = weights[nm].shape
        g2 = grads2d[nm]
        as2d = lambda a: a.reshape(g2.shape)
        dlt, nm_, nv_ = _adamw(as2d(weights[nm]), g2, as2d(ms[nm]), as2d(vs[nm]), name=f"adamw_{nm}")
        out_g.append(g2.reshape(shape))
        out_d.append(dlt.reshape(shape))
        out_m.append(nm_.reshape(shape))
        out_v.append(nv_.reshape(shape))
    return (loss, grad_x, *out_g, *out_d, *out_m, *out_v)
```

```python
import functools

import jax
import jax.numpy as jnp
import numpy as np
from jax import lax
from jax.experimental import pallas as pl
from jax.experimental.pallas import tpu as pltpu

F32 = jnp.float32
BF16 = jnp.bfloat16
HI = lax.Precision.HIGHEST
MESH = pl.DeviceIdType.MESH

D_MODEL = 1024
N_META = 16
CHUNK = 64
N_PAD = CHUNK - N_META
NH = 4
DN_D = 128
GLA_DK = 64
GLA_DV = 128
GLA_RANK = 16
D_FF = 4 * D_MODEL
EPS = 1e-6
IN_WIDTH = 3608
C_QKV, C_DZ, C_GQK, C_GV, C_GR, C_SA, C_SB, PW = 0, 1536, 2048, 2560, 3072, 3584, 3712, 3840
LANE = 128
N_CHIPS = 4

ADAM_LR, ADAM_B1, ADAM_B2, ADAM_EPS, ADAM_WD, ADAM_STEP = 0.001, 0.9, 0.999, 1e-08, 0.01, 10

VMEM_BIG = 56 * 1024 * 1024


def _cp(vmem=None, sem=None):
    kw = {}
    if vmem is not None:
        kw["vmem_limit_bytes"] = vmem
    if sem is not None:
        kw["dimension_semantics"] = sem
    return pltpu.CompilerParams(**kw)


def _tile(n, target, mult=16):
    best = None
    for t in range(mult, min(n, target) + 1, mult):
        if n % t == 0:
            best = t
    assert best is not None, (n, target)
    return best


def _dot(a, b, dims, prec=None):
    return lax.dot_general(a, b, (dims, ((), ())), preferred_element_type=F32, precision=prec)


def _nn(a, b):
    return _dot(a.astype(BF16), b.astype(BF16), ((1,), (0,)))


def _nt(a, b):
    return _dot(a.astype(BF16), b.astype(BF16), ((1,), (1,)))


def _tn(a, b):
    return _dot(a.astype(BF16), b.astype(BF16), ((0,), (0,)))


def _nn_hi(a, b):
    return _dot(a, b, ((1,), (0,)), HI)


def _nt_hi(a, b):
    return _dot(a, b, ((1,), (1,)), HI)


def _tn_hi(a, b):
    return _dot(a, b, ((0,), (0,)), HI)


def _sigmoid(x):
    return 1.0 / (1.0 + jnp.exp(-x))


def _softplus(x):
    return jnp.maximum(x, 0.0) + jnp.log(1.0 + jnp.exp(-jnp.abs(x)))


def _logsigmoid(x):
    return -_softplus(-x)


def _iota2(shape, dim):
    return lax.broadcasted_iota(jnp.int32, shape, dim)


def _mm(a, b, mode, *, tm, tn, tk, out_dtypes, extras=(), epilogue=None, name, vmem=VMEM_BIG):
    if mode == "tn":
        K, M = a.shape
    else:
        M, K = a.shape
    N = b.shape[0] if mode == "nt" else b.shape[1]
    assert M % tm == 0 and N % tn == 0 and K % tk == 0, (name, M, N, K, tm, tn, tk)
    nk = K // tk
    n_ex, n_out = len(extras), len(out_dtypes)
    if mode == "tn":
        a_spec = pl.BlockSpec((tk, tm), lambda i, j, k: (k, i))
    else:
        a_spec = pl.BlockSpec((tm, tk), lambda i, j, k: (i, k))
    if mode == "nt":
        b_spec = pl.BlockSpec((tn, tk), lambda i, j, k: (j, k))
    else:
        b_spec = pl.BlockSpec((tk, tn), lambda i, j, k: (k, j))
    mn_spec = pl.BlockSpec((tm, tn), lambda i, j, k: (i, j))
    dims = {"nn": ((1,), (0,)), "nt": ((1,), (1,)), "tn": ((0,), (0,))}[mode]

    def body(*refs):
        a_ref, b_ref = refs[0], refs[1]
        ex_refs = refs[2:2 + n_ex]
        out_refs = refs[2 + n_ex:2 + n_ex + n_out]
        acc_ref = refs[2 + n_ex + n_out]
        k = pl.program_id(2)
        part = _dot(a_ref[...].astype(BF16), b_ref[...].astype(BF16), dims)

        @pl.when(k == 0)
        def _():
            acc_ref[...] = part

        @pl.when(k > 0)
        def _():
            acc_ref[...] += part

        @pl.when(k == nk - 1)
        def _():
            acc = acc_ref[...]
            res = (acc,) if epilogue is None else epilogue(acc, *[e[...] for e in ex_refs])
            for o_ref, r in zip(out_refs, res):
                o_ref[...] = r.astype(o_ref.dtype)

    outs = pl.pallas_call(
        body, name=name, grid=(M // tm, N // tn, nk),
        in_specs=[a_spec, b_spec] + [mn_spec] * n_ex,
        out_specs=[mn_spec] * n_out,
        out_shape=[jax.ShapeDtypeStruct((M, N), dt) for dt in out_dtypes],
        scratch_shapes=[pltpu.VMEM((tm, tn), F32)],
        compiler_params=_cp(vmem, ("parallel", "parallel", "arbitrary")),
    )(a, b, *extras)
    return tuple(outs)


def _rms_fwd(x, g, *, tr, name):
    n, d = x.shape

    def body(x_ref, g_ref, o_ref):
        xv = x_ref[...]
        r = lax.rsqrt(jnp.mean(xv * xv, axis=-1, keepdims=True) + EPS)
        o_ref[...] = (xv * r * g_ref[...]).astype(o_ref.dtype)

    return pl.pallas_call(
        body, name=name, grid=(n // tr,),
        in_specs=[pl.BlockSpec((tr, d), lambda i: (i, 0)), pl.BlockSpec((1, d), lambda i: (0, 0))],
        out_specs=pl.BlockSpec((tr, d), lambda i: (i, 0)),
        out_shape=jax.ShapeDtypeStruct((n, d), BF16),
        compiler_params=_cp(VMEM_BIG),
    )(x, g)


def _rms_bwd_math(xv, g, dy):
    r = lax.rsqrt(jnp.mean(xv * xv, axis=-1, keepdims=True) + EPS)
    xh = xv * r
    gdy = dy * g
    dx = r * (gdy - xh * jnp.mean(xh * gdy, axis=-1, keepdims=True))
    return dx, jnp.sum(dy * xh, axis=0, keepdims=True)


def _rms_bwd_add(x, g, dy, res, *, tr, name):
    n, d = x.shape

    def body(x_ref, g_ref, dy_ref, res_ref, o_ref, dg_ref):
        dx, dg = _rms_bwd_math(x_ref[...], g_ref[...], dy_ref[...])
        o_ref[...] = res_ref[...] + dx

        @pl.when(pl.program_id(0) == 0)
        def _():
            dg_ref[...] = dg

        @pl.when(pl.program_id(0) > 0)
        def _():
            dg_ref[...] += dg

    row = pl.BlockSpec((tr, d), lambda i: (i, 0))
    vec = pl.BlockSpec((1, d), lambda i: (0, 0))
    return pl.pallas_call(
        body, name=name, grid=(n // tr,),
        in_specs=[row, vec, row, row], out_specs=[row, vec],
        out_shape=[jax.ShapeDtypeStruct((n, d), F32), jax.ShapeDtypeStruct((1, d), F32)],
        compiler_params=_cp(VMEM_BIG),
    )(x, g, dy, res)


def _final_loss(x2, gf, tgt, *, t_seq, tr, name):
    n, d = x2.shape
    per_seq = t_seq // tr

    def body(x_ref, g_ref, t_ref, dx_ref, dg_ref, loss_ref):
        i = pl.program_id(0)
        xv = x_ref[...]
        g = g_ref[...]
        r = lax.rsqrt(jnp.mean(xv * xv, axis=-1, keepdims=True) + EPS)
        xh = xv * r
        pos = (i % per_seq) * tr + _iota2((tr, 1), 0)
        real = pos >= CHUNK
        err = jnp.where(real, xh * g - t_ref[...], 0.0)
        dy = err * (1.0 / d)
        gdy = dy * g
        dx_ref[...] = r * (gdy - xh * jnp.mean(xh * gdy, axis=-1, keepdims=True))
        dg = jnp.sum(dy * xh, axis=0, keepdims=True)
        ls = 0.5 * jnp.sum(jnp.mean(err * err, axis=-1, keepdims=True), axis=0, keepdims=True)
        ls = jnp.where(_iota2((1, LANE), 1) == 0, ls, 0.0)

        @pl.when(i == 0)
        def _():
            dg_ref[...] = dg
            loss_ref[...] = ls

        @pl.when(i > 0)
        def _():
            dg_ref[...] += dg
            loss_ref[...] += ls

    row = pl.BlockSpec((tr, d), lambda i: (i, 0))
    vec = pl.BlockSpec((1, d), lambda i: (0, 0))
    one = pl.BlockSpec((1, LANE), lambda i: (0, 0))
    return pl.pallas_call(
        body, name=name, grid=(n // tr,),
        in_specs=[row, vec, row], out_specs=[row, vec, one],
        out_shape=[jax.ShapeDtypeStruct((n, d), F32), jax.ShapeDtypeStruct((1, d), F32),
                   jax.ShapeDtypeStruct((1, LANE), F32)],
        compiler_params=_cp(VMEM_BIG),
    )(x2, gf, tgt)


def _gnorm_fwd(o_dn, o_gla, projp, g_dn, g_gla, *, tr, name):
    n = o_dn.shape[0]
    w = NH * DN_D

    def body(odn_ref, ogl_ref, z_ref, r_ref, gdn_ref, ggl_ref, mix_ref):
        for grp, (o_ref, gate_ref, gain_ref) in enumerate(((odn_ref, z_ref, gdn_ref), (ogl_ref, r_ref, ggl_ref))):
            gain = gain_ref[...]
            for h in range(NH):
                sl = slice(h * DN_D, (h + 1) * DN_D)
                o = o_ref[:, sl]
                z = gate_ref[:, sl]
                r = lax.rsqrt(jnp.mean(o * o, axis=-1, keepdims=True) + EPS)
                y = (o * r * gain) * (z * _sigmoid(z))
                mix_ref[:, grp * w + h * DN_D: grp * w + (h + 1) * DN_D] = y.astype(mix_ref.dtype)

    row = pl.BlockSpec((tr, w), lambda i: (i, 0))
    vec = pl.BlockSpec((1, DN_D), lambda i: (0, 0))
    return pl.pallas_call(
        body, name=name, grid=(n // tr,),
        in_specs=[row, row, pl.BlockSpec((tr, w), lambda i: (i, C_DZ // w)),
                  pl.BlockSpec((tr, w), lambda i: (i, C_GR // w)), vec, vec],
        out_specs=pl.BlockSpec((tr, 2 * w), lambda i: (i, 0)),
        out_shape=jax.ShapeDtypeStruct((n, 2 * w), BF16),
        compiler_params=_cp(VMEM_BIG),
    )(o_dn, o_gla, projp, projp, g_dn, g_gla)


def _gnorm_bwd(dmix, o_dn, o_gla, projp, g_dn, g_gla, *, tr, name):
    n = o_dn.shape[0]
    w = NH * DN_D

    def body(dm_ref, odn_ref, ogl_ref, z_ref, r_ref, gdn_ref, ggl_ref,
             dodn_ref, ddz_ref, dogl_ref, dgr_ref, dgdn_ref, dggl_ref):
        first = pl.program_id(0) == 0
        groups = ((odn_ref, z_ref, gdn_ref, dodn_ref, ddz_ref, dgdn_ref),
                  (ogl_ref, r_ref, ggl_ref, dogl_ref, dgr_ref, dggl_ref))
        for grp, (o_ref, gate_ref, gain_ref, do_ref, dgate_ref, dgain_ref) in enumerate(groups):
            gain = gain_ref[...]
            dgain = jnp.zeros((1, DN_D), F32)
            for h in range(NH):
                sl = slice(h * DN_D, (h + 1) * DN_D)
                o = o_ref[:, sl]
                z = gate_ref[:, sl]
                dm = dm_ref[:, grp * w + h * DN_D: grp * w + (h + 1) * DN_D]
                r = lax.rsqrt(jnp.mean(o * o, axis=-1, keepdims=True) + EPS)
                oh = o * r
                s = _sigmoid(z)
                dn = dm * (z * s)
                dgate_ref[:, sl] = dm * (oh * gain) * (s * (1.0 + z * (1.0 - s)))
                gdn = dn * gain
                do_ref[:, sl] = r * (gdn - oh * jnp.mean(oh * gdn, axis=-1, keepdims=True))
                dgain = dgain + jnp.sum(dn * oh, axis=0, keepdims=True)

            @pl.when(first)
            def _():
                dgain_ref[...] = dgain

            @pl.when(jnp.logical_not(first))
            def _():
                dgain_ref[...] += dgain

    row = pl.BlockSpec((tr, w), lambda i: (i, 0))
    vec = pl.BlockSpec((1, DN_D), lambda i: (0, 0))
    big = jax.ShapeDtypeStruct((n, w), F32)
    small = jax.ShapeDtypeStruct((1, DN_D), F32)
    return pl.pallas_call(
        body, name=name, grid=(n // tr,),
        in_specs=[pl.BlockSpec((tr, 2 * w), lambda i: (i, 0)), row, row,
                  pl.BlockSpec((tr, w), lambda i: (i, C_DZ // w)), pl.BlockSpec((tr, w), lambda i: (i, C_GR // w)), vec, vec],
        out_specs=[row, row, row, row, vec, vec],
        out_shape=[big, big, big, big, small, small],
        compiler_params=_cp(VMEM_BIG),
    )(dmix, o_dn, o_gla, projp, projp, g_dn, g_gla)


QKV_W = 3 * NH * DN_D
HALO = 8


def _conv_z(xs_ref, cw_ref, tt):
    z = cw_ref[0:1, :] * xs_ref[pl.ds(HALO - 3, tt), :]
    for j in range(1, 4):
        z = z + cw_ref[j:j + 1, :] * xs_ref[pl.ds(HALO - 3 + j, tt), :]
    return z


def _dnprep_fwd(projp, conv_w, *, bsz, t_seq, tt, name):
    n = bsz * t_seq
    per_seq = t_seq // tt
    hw = NH * DN_D

    def body(x_ref, halo_ref, cw_ref, q_ref, k_ref, v_ref, xs_ref):
        i = pl.program_id(1)
        xs_ref[0:HALO, :] = jnp.where(i == 0, 0.0, halo_ref[...])
        xs_ref[HALO:HALO + tt, :] = x_ref[...]
        z = _conv_z(xs_ref, cw_ref, tt)
        a = z * _sigmoid(z)
        for grp, o_ref in enumerate((q_ref, k_ref)):
            for h in range(NH):
                ah = a[:, grp * hw + h * DN_D: grp * hw + (h + 1) * DN_D]
                rs = lax.rsqrt(jnp.sum(ah * ah, axis=-1, keepdims=True) + EPS)
                o_ref[:, h * DN_D:(h + 1) * DN_D] = ah * rs
        v_ref[...] = a[:, 2 * hw:3 * hw]

    def halo_map(b, i):
        return (jnp.maximum((b * t_seq + i * tt) // HALO - 1, 0), 0)

    out = pl.BlockSpec((tt, hw), lambda b, i: (b * per_seq + i, 0))
    sds = jax.ShapeDtypeStruct((n, hw), F32)
    return pl.pallas_call(
        body, name=name, grid=(bsz, per_seq),
        in_specs=[pl.BlockSpec((tt, QKV_W), lambda b, i: (b * per_seq + i, 0)),
                  pl.BlockSpec((HALO, QKV_W), halo_map),
                  pl.BlockSpec((4, QKV_W), lambda b, i: (0, 0))],
        out_specs=[out, out, out], out_shape=[sds, sds, sds],
        scratch_shapes=[pltpu.VMEM((tt + HALO, QKV_W), F32)],
        compiler_params=_cp(VMEM_BIG),
    )(projp, projp, conv_w)


def _dnprep_bwd_a(projp, conv_w, dq, dk, dv, *, bsz, t_seq, tt, name):
    n = bsz * t_seq
    per_seq = t_seq // tt
    hw = NH * DN_D

    def body(x_ref, halo_ref, cw_ref, dq_ref, dk_ref, dv_ref, dz_ref, dcw_ref, xs_ref):
        b, i = pl.program_id(0), pl.program_id(1)
        xs_ref[0:HALO, :] = jnp.where(i == 0, 0.0, halo_ref[...])
        xs_ref[HALO:HALO + tt, :] = x_ref[...]
        z = _conv_z(xs_ref, cw_ref, tt)
        s = _sigmoid(z)
        a = z * s
        dsilu = s * (1.0 + z * (1.0 - s))
        for grp, d_ref in enumerate((dq_ref, dk_ref)):
            for h in range(NH):
                sl = slice(grp * hw + h * DN_D, grp * hw + (h + 1) * DN_D)
                ah = a[:, sl]
                rs = lax.rsqrt(jnp.sum(ah * ah, axis=-1, keepdims=True) + EPS)
                y = ah * rs
                dy = d_ref[:, h * DN_D:(h + 1) * DN_D]
                da = rs * (dy - y * jnp.sum(dy * y, axis=-1, keepdims=True))
                dz_ref[:, sl] = da * dsilu[:, sl]
        dz_ref[:, 2 * hw:3 * hw] = dv_ref[...] * dsilu[:, 2 * hw:3 * hw]
        dz = dz_ref[...]
        first = jnp.logical_and(b == 0, i == 0)
        for j in range(4):
            part = jnp.sum(dz * xs_ref[pl.ds(HALO - 3 + j, tt), :], axis=0, keepdims=True)

            @pl.when(first)
            def _():
                dcw_ref[j:j + 1, :] = part

            @pl.when(jnp.logical_not(first))
            def _():
                dcw_ref[j:j + 1, :] += part

    def halo_map(b, i):
        return (jnp.maximum((b * t_seq + i * tt) // HALO - 1, 0), 0)

    hrow = pl.BlockSpec((tt, hw), lambda b, i: (b * per_seq + i, 0))
    return pl.pallas_call(
        body, name=name, grid=(bsz, per_seq),
        in_specs=[pl.BlockSpec((tt, QKV_W), lambda b, i: (b * per_seq + i, 0)),
                  pl.BlockSpec((HALO, QKV_W), halo_map),
                  pl.BlockSpec((4, QKV_W), lambda b, i: (0, 0)), hrow, hrow, hrow],
        out_specs=[pl.BlockSpec((tt, QKV_W), lambda b, i: (b * per_seq + i, 0)),
                   pl.BlockSpec((4, QKV_W), lambda b, i: (0, 0))],
        out_shape=[jax.ShapeDtypeStruct((n, QKV_W), F32), jax.ShapeDtypeStruct((4, QKV_W), F32)],
        scratch_shapes=[pltpu.VMEM((tt + HALO, QKV_W), F32)],
        compiler_params=_cp(VMEM_BIG),
    )(projp, projp, conv_w, dq, dk, dv)


def _dnprep_bwd_b(dz, conv_w, *, bsz, t_seq, tt, name):
    n = bsz * t_seq
    per_seq = t_seq // tt
    last_blk = n // HALO - 1

    def body(dz_ref, halo_ref, cw_ref, dx_ref, ds_ref):
        i = pl.program_id(1)
        ds_ref[0:tt, :] = dz_ref[...]
        ds_ref[tt:tt + HALO, :] = jnp.where(i == per_seq - 1, 0.0, halo_ref[...])
        dx = cw_ref[0:1, :] * ds_ref[pl.ds(3, tt), :]
        for j in range(1, 4):
            dx = dx + cw_ref[j:j + 1, :] * ds_ref[pl.ds(3 - j, tt), :]
        dx_ref[...] = dx

    def halo_map(b, i):
        return (jnp.minimum((b * t_seq + (i + 1) * tt) // HALO, last_blk), 0)

    row = pl.BlockSpec((tt, QKV_W), lambda b, i: (b * per_seq + i, 0))
    return pl.pallas_call(
        body, name=name, grid=(bsz, per_seq),
        in_specs=[row, pl.BlockSpec((HALO, QKV_W), halo_map), pl.BlockSpec((4, QKV_W), lambda b, i: (0, 0))],
        out_specs=row, out_shape=jax.ShapeDtypeStruct((n, QKV_W), F32),
        scratch_shapes=[pltpu.VMEM((tt + HALO, QKV_W), F32)],
        compiler_params=_cp(VMEM_BIG),
    )(dz, dz, conv_w)


def _masks64():
    r = _iota2((CHUNK, CHUNK), 0)
    c = _iota2((CHUNK, CHUNK), 1)
    return r, c


def _group(nc_seq, target=5):
    return max(g for g in range(1, target + 1) if nc_seq % g == 0)


def _round_robin(chains):
    live = list(chains)
    while live:
        nxt = []
        for ch in live:
            try:
                next(ch)
                nxt.append(ch)
            except StopIteration:
                pass
        live = nxt
        yield


def _run(chains):
    for _ in _round_robin(chains):
        pass


def _per_chunk(inner, kinds, grp):
    def body(*refs):
        chains = []
        for gi in range(grp):
            views = []
            for r, kind in zip(refs, kinds):
                if kind == "row":
                    views.append(r.at[pl.ds(gi * CHUNK, CHUNK)])
                elif kind == "lead":
                    views.append(r.at[pl.ds(gi, 1)])
                else:
                    views.append(r)
            chains.append(inner(gi, *views))
        _run(chains)
    return body


def _accumulate(ref, val, gi):
    if gi > 0:
        ref[...] += val
        return
    first = pl.program_id(0) == 0

    @pl.when(first)
    def _():
        ref[...] = val

    @pl.when(jnp.logical_not(first))
    def _():
        ref[...] += val


def _tri_inv(a_strict):
    r, c = _masks64()
    eye = (r == c).astype(F32)
    blk16 = (r // 16) == (c // 16)
    blk32 = (r // 32) == (c // 32)
    ld = jnp.where(blk16, a_strict, 0.0)
    x = eye - ld
    p = _nn(ld, ld)
    yield
    for step in range(3):
        xp = _nn(x, p)
        if step < 2:
            p = _nn(p, p)
        x = x + xp
        yield
    for lk in (jnp.where(jnp.logical_and(blk32, jnp.logical_not(blk16)), a_strict, 0.0),
               jnp.where(blk32, 0.0, a_strict)):
        y = x - eye
        s = lk + _nn(y, lk)
        yield
        x = x - s - _nn(s, y)
        yield
    return x


def _dn_gates(sa, alog, dtb, chunk_in_seq):
    rows = _iota2((CHUNK, LANE), 0)
    valid = jnp.logical_or(rows >= N_PAD, chunk_in_seq > 0)
    beta_t = _sigmoid(sa)
    ea = jnp.exp(alog)
    g_t = jnp.where(valid, -ea * _softplus(sa + dtb), 0.0)
    r, c = _masks64()
    ltri = (r >= c).astype(F32)
    gam_t = _nn_hi(ltri, g_t)
    return beta_t, g_t, gam_t, valid, ea


def _dn_intra_fwd(qn, kn, v, projp, alog_row, dtb_row, *, nc_seq, name):
    n = qn.shape[0]
    nct = n // CHUNK
    hw = NH * DN_D
    scale = DN_D ** -0.5

    grp = _group(nc_seq)

    def inner(gi, q_ref, k_ref, v_ref, sa_ref, al_ref, dt_ref, u_ref, w_ref, qg_ref, kd_ref, p_ref, t_ref, gl_ref):
        ci = (pl.program_id(0) * grp + gi) % nc_seq
        beta_t, _, gam_t, _, _ = _dn_gates(sa_ref[...], al_ref[...], dt_ref[...], ci)
        yield
        gam_tt = gam_t.T
        r, c = _masks64()
        incl = r >= c
        strict = r > c

        def head(h):
            sl = slice(h * DN_D, (h + 1) * DN_D)
            beta = beta_t[:, h:h + 1]
            gam = gam_t[:, 4 + h:5 + h]
            gam_row = gam_tt[4 + h:5 + h, :]
            gl = gam_t[CHUNK - 1:CHUNK, 4 + h:5 + h]
            dec = jnp.exp(jnp.where(incl, gam - gam_row, -jnp.inf))
            kh = k_ref[:, sl]
            qh = q_ref[:, sl] * scale
            vh = v_ref[:, sl]
            kk = _nt(kh, kh)
            qk = _nt(qh, kh)
            yield
            a = jnp.where(strict, beta * kk * dec, 0.0)
            tm = yield from _tri_inv(a)
            egam = jnp.exp(gam)
            u_ref[:, sl] = _nn(tm, beta * vh)
            w_ref[:, sl] = _nn(tm, (beta * egam) * kh)
            qg_ref[:, sl] = egam * qh
            kd_ref[:, sl] = jnp.exp(gl - gam) * kh
            p_ref[0, h] = qk * dec
            t_ref[0, h] = tm
            gl_ref[0, h:h + 1, :] = jnp.broadcast_to(jnp.exp(gl), (1, LANE))

        yield from _round_robin([head(h) for h in range(NH)])

    rows = grp * CHUNK
    row = pl.BlockSpec((rows, hw), lambda i: (i, 0))
    vec = pl.BlockSpec((1, LANE), lambda i: (0, 0))
    mat = pl.BlockSpec((grp, NH, CHUNK, CHUNK), lambda i: (i, 0, 0, 0))
    big = jax.ShapeDtypeStruct((n, hw), F32)
    msd = jax.ShapeDtypeStruct((nct, NH, CHUNK, CHUNK), F32)
    kinds = ["row"] * 4 + ["whole"] * 2 + ["row"] * 4 + ["lead"] * 3
    return pl.pallas_call(
        _per_chunk(inner, kinds, grp), name=name, grid=(nct // grp,),
        in_specs=[row, row, row, pl.BlockSpec((rows, LANE), lambda i: (i, C_SA // LANE)), vec, vec],
        out_specs=[row, row, row, row, mat, mat, pl.BlockSpec((grp, NH, LANE), lambda i: (i, 0, 0))],
        out_shape=[big, big, big, big, msd, msd, jax.ShapeDtypeStruct((nct, NH, LANE), F32)],
        compiler_params=_cp(VMEM_BIG),
    )(qn, kn, v, projp, alog_row, dtb_row)


def _dn_scan_fwd(u, w, qg, kd, p, gl, *, bsz, nc_seq, name):
    hw = NH * DN_D
    t_seq = nc_seq * CHUNK
    u, w, qg, kd = (z.reshape(bsz, t_seq, hw) for z in (u, w, qg, kd))
    p = p.reshape(bsz, nc_seq, NH, CHUNK, CHUNK)
    gl = gl.reshape(bsz, nc_seq, NH, LANE)

    def body(u_ref, w_ref, qg_ref, kd_ref, p_ref, gl_ref, o_ref, vn_ref, hist_ref, s_ref):
        @pl.when(pl.program_id(0) == 0)
        def _():
            s_ref[...] = jnp.zeros_like(s_ref)

        def chain(b, h):
            sl = slice(h * DN_D, (h + 1) * DN_D)
            s = s_ref[b, h]
            hist_ref[b, 0, h] = s
            ws = _nn(w_ref[b, :, sl], s)
            qs = _nn(qg_ref[b, :, sl], s)
            yield
            vn = u_ref[b, :, sl] - ws
            vn_ref[b, :, sl] = vn
            o_ref[b, :, sl] = qs + _nn(p_ref[b, 0, h], vn)
            s_ref[b, h] = gl_ref[b, 0, h:h + 1, :] * s + _tn(kd_ref[b, :, sl], vn)

        _run([chain(b, h) for b in range(bsz) for h in range(NH)])

    row = pl.BlockSpec((bsz, CHUNK, hw), lambda i: (0, i, 0))
    outs = pl.pallas_call(
        body, name=name, grid=(nc_seq,),
        in_specs=[row, row, row, row, pl.BlockSpec((bsz, 1, NH, CHUNK, CHUNK), lambda i: (0, i, 0, 0, 0)),
                  pl.BlockSpec((bsz, 1, NH, LANE), lambda i: (0, i, 0, 0))],
        out_specs=[row, row, pl.BlockSpec((bsz, 1, NH, DN_D, DN_D), lambda i: (0, i, 0, 0, 0))],
        out_shape=[jax.ShapeDtypeStruct((bsz, t_seq, hw), F32), jax.ShapeDtypeStruct((bsz, t_seq, hw), F32),
                   jax.ShapeDtypeStruct((bsz, nc_seq, NH, DN_D, DN_D), F32)],
        scratch_shapes=[pltpu.VMEM((bsz, NH, DN_D, DN_D), F32)],
        compiler_params=_cp(VMEM_BIG, ("arbitrary",)),
    )(u, w, qg, kd, p, gl)
    o, vn, hist = outs
    return o.reshape(bsz * t_seq, hw), vn.reshape(bsz * t_seq, hw), hist


def _dn_scan_bwd(do, w, qg, kd, vn, p, gl, hist, *, bsz, nc_seq, name):
    hw = NH * DN_D
    t_seq = nc_seq * CHUNK
    do, w, qg, kd, vn = (z.reshape(bsz, t_seq, hw) for z in (do, w, qg, kd, vn))
    p = p.reshape(bsz, nc_seq, NH, CHUNK, CHUNK)
    gl = gl.reshape(bsz, nc_seq, NH, LANE)

    def body(do_ref, w_ref, qg_ref, kd_ref, vn_ref, p_ref, gl_ref, hist_ref,
             du_ref, dw_ref, dqg_ref, dkd_ref, dp_ref, dgl_ref, ds_ref):
        @pl.when(pl.program_id(0) == 0)
        def _():
            ds_ref[...] = jnp.zeros_like(ds_ref)

        def chain(b, h):
            sl = slice(h * DN_D, (h + 1) * DN_D)
            s = hist_ref[b, 0, h]
            dsn = ds_ref[b, h]
            doh = do_ref[b, :, sl]
            vnh = vn_ref[b, :, sl]
            kdh = kd_ref[b, :, sl]
            dvn = _tn(p_ref[b, 0, h], doh) + _nn(kdh, dsn)
            du_ref[b, :, sl] = dvn
            dqg_ref[b, :, sl] = _nt(doh, s)
            dp_ref[b, 0, h] = _nt(doh, vnh)
            dkd_ref[b, :, sl] = _nt(vnh, dsn)
            ds_part = _tn(qg_ref[b, :, sl], doh) + gl_ref[b, 0, h:h + 1, :] * dsn
            dgl = jnp.sum(jnp.sum(dsn * s, axis=0, keepdims=True), axis=1, keepdims=True)
            dgl_ref[b, 0, h:h + 1, :] = jnp.broadcast_to(dgl, (1, LANE))
            yield
            dw_ref[b, :, sl] = -_nt(dvn, s)
            ds_ref[b, h] = ds_part - _tn(w_ref[b, :, sl], dvn)

        _run([chain(b, h) for b in range(bsz) for h in range(NH)])

    rev = lambda i: nc_seq - 1 - i
    row = pl.BlockSpec((bsz, CHUNK, hw), lambda i: (0, rev(i), 0))
    mat = pl.BlockSpec((bsz, 1, NH, CHUNK, CHUNK), lambda i: (0, rev(i), 0, 0, 0))
    glb = pl.BlockSpec((bsz, 1, NH, LANE), lambda i: (0, rev(i), 0, 0))
    big = jax.ShapeDtypeStruct((bsz, t_seq, hw), F32)
    outs = pl.pallas_call(
        body, name=name, grid=(nc_seq,),
        in_specs=[row, row, row, row, row, mat, glb,
                  pl.BlockSpec((bsz, 1, NH, DN_D, DN_D), lambda i: (0, rev(i), 0, 0, 0))],
        out_specs=[row, row, row, row, mat, glb],
        out_shape=[big, big, big, big, jax.ShapeDtypeStruct((bsz, nc_seq, NH, CHUNK, CHUNK), F32),
                   jax.ShapeDtypeStruct((bsz, nc_seq, NH, LANE), F32)],
        scratch_shapes=[pltpu.VMEM((bsz, NH, DN_D, DN_D), F32)],
        compiler_params=_cp(VMEM_BIG, ("arbitrary",)),
    )(do, w, qg, kd, vn, p, gl, hist)
    du, dw, dqg, dkd, dp, dgl = outs
    n = bsz * t_seq
    return (du.reshape(n, hw), dw.reshape(n, hw), dqg.reshape(n, hw), dkd.reshape(n, hw),
            dp.reshape(bsz * nc_seq, NH, CHUNK, CHUNK), dgl.reshape(bsz * nc_seq, NH, LANE))


def _dn_intra_bwd(qn, kn, v, projp, alog_row, dtb_row, u, w, tmat, du, dw, dqg, dkd, dp, dgl, *, nc_seq, name):
    n = qn.shape[0]
    nct = n // CHUNK
    hw = NH * DN_D
    scale = DN_D ** -0.5

    grp = _group(nc_seq)

    def inner(gi, q_ref, k_ref, v_ref, sa_ref, al_ref, dt_ref, u_ref, w_ref, t_ref, du_ref, dw_ref, dqg_ref, dkd_ref,
              dp_ref, dgl_ref, dq_ref, dk_ref, dv_ref, dsa_ref, dal_ref, ddt_ref):
        ci = (pl.program_id(0) * grp + gi) % nc_seq
        sa = sa_ref[...]
        beta_t, g_t, gam_t, valid, ea = _dn_gates(sa, al_ref[...], dt_ref[...], ci)
        yield
        gam_tt = gam_t.T
        r, c = _masks64()
        incl = r >= c
        strict = r > c
        lane = _iota2((CHUNK, LANE), 1)
        rows1 = _iota2((CHUNK, 1), 0)
        acc = [jnp.zeros((CHUNK, LANE), F32)]

        def head(h):
            sl = slice(h * DN_D, (h + 1) * DN_D)
            beta = beta_t[:, h:h + 1]
            gam = gam_t[:, 4 + h:5 + h]
            gam_row = gam_tt[4 + h:5 + h, :]
            gl = gam_t[CHUNK - 1:CHUNK, 4 + h:5 + h]
            dec = jnp.exp(jnp.where(incl, gam - gam_row, -jnp.inf))
            kh = k_ref[:, sl]
            qh = q_ref[:, sl] * scale
            vh = v_ref[:, sl]
            kk = _nt(kh, kh)
            qk = _nt(qh, kh)
            a = jnp.where(strict, beta * kk * dec, 0.0)
            pm = qk * dec
            tm = t_ref[0, h]
            uh = u_ref[:, sl]
            wh = w_ref[:, sl]
            egam = jnp.exp(gam)
            ekd = jnp.exp(gl - gam)
            dvb = _tn(tm, du_ref[:, sl])
            dkg = _tn(tm, dw_ref[:, sl])
            yield
            da = jnp.where(strict, -(_nt(dvb, uh) + _nt(dkg, wh)), 0.0)
            yield
            dad = da * dec
            dkk = beta * dad
            dbeta = jnp.sum(dad * kk, axis=1, keepdims=True)
            dpm = jnp.where(incl, dp_ref[0, h], 0.0)
            dqk = dpm * dec
            e = da * a + dpm * pm
            dgam = jnp.sum(e, axis=1, keepdims=True) - jnp.sum(e.T, axis=1, keepdims=True)
            dqgh = dqg_ref[:, sl]
            dkdh = dkd_ref[:, sl]
            dkh = (_nn(dkk, kh) + _tn(dkk, kh) + _tn(dqk, qh) + (beta * egam) * dkg + ekd * dkdh)
            dqh = _nn(dqk, kh) + egam * dqgh
            dbeta = dbeta + jnp.sum(dkg * (egam * kh), axis=1, keepdims=True) + jnp.sum(dvb * vh, axis=1, keepdims=True)
            rkd = jnp.sum(dkdh * (ekd * kh), axis=1, keepdims=True)
            dgam = (dgam + jnp.sum(dkg * ((beta * egam) * kh), axis=1, keepdims=True)
                    + jnp.sum(dqgh * (egam * qh), axis=1, keepdims=True) - rkd)
            dgam_last = jnp.sum(rkd, axis=0, keepdims=True) + dgl_ref[0, h:h + 1, 0:1] * jnp.exp(gl)
            dgam = dgam + jnp.where(rows1 == CHUNK - 1, dgam_last, 0.0)
            dq_ref[:, sl] = dqh * scale
            dk_ref[:, sl] = dkh
            dv_ref[:, sl] = beta * dvb
            acc[0] = acc[0] + jnp.where(lane == h, dbeta, 0.0) + jnp.where(lane == 4 + h, dgam, 0.0)

        yield from _round_robin([head(h) for h in range(NH)])
        acc_t = acc[0]
        utri = (r <= c).astype(F32)
        dg_t = _nn_hi(utri, acc_t)
        ddb = acc_t * beta_t * (1.0 - beta_t)
        dda = jnp.where(valid, dg_t * (-ea) * _sigmoid(sa + dt_ref[...]), 0.0)
        dsa_ref[...] = jnp.where(lane < 4, ddb, jnp.where(lane < 8, dda, 0.0))
        in_g = jnp.logical_and(lane >= 4, lane < 8)
        dal = jnp.sum(jnp.where(in_g, dg_t * g_t, 0.0), axis=0, keepdims=True)
        ddt = jnp.sum(jnp.where(in_g, dda, 0.0), axis=0, keepdims=True)
        _accumulate(dal_ref, dal, gi)
        _accumulate(ddt_ref, ddt, gi)

    rows = grp * CHUNK
    row = pl.BlockSpec((rows, hw), lambda i: (i, 0))
    vec = pl.BlockSpec((1, LANE), lambda i: (0, 0))
    mat = pl.BlockSpec((grp, NH, CHUNK, CHUNK), lambda i: (i, 0, 0, 0))
    glb = pl.BlockSpec((grp, NH, LANE), lambda i: (i, 0, 0))
    big = jax.ShapeDtypeStruct((n, hw), F32)
    v128 = jax.ShapeDtypeStruct((1, LANE), F32)
    kinds = (["row"] * 4 + ["whole"] * 2 + ["row"] * 2 + ["lead"] + ["row"] * 4 + ["lead"] * 2
             + ["row"] * 4 + ["whole"] * 2)
    return pl.pallas_call(
        _per_chunk(inner, kinds, grp), name=name, grid=(nct // grp,),
        in_specs=[row, row, row, pl.BlockSpec((rows, LANE), lambda i: (i, C_SA // LANE)), vec, vec,
                  row, row, mat, row, row, row, row, mat, glb],
        out_specs=[row, row, row, pl.BlockSpec((rows, LANE), lambda i: (i, 0)), vec, vec],
        out_shape=[big, big, big, jax.ShapeDtypeStruct((n, LANE), F32), v128, v128],
        compiler_params=_cp(VMEM_BIG, ("arbitrary",)),
    )(qn, kn, v, projp, alog_row, dtb_row, u, w, tmat, du, dw, dqg, dkd, dp, dgl)


GQ_W = NH * GLA_DK
GV_W = NH * GLA_DV
GLA_NORM = 16.0
MID = CHUNK // 2


def _gla_gates(sb, w2p, gb, chunk_in_seq):
    rows = _iota2((CHUNK, GQ_W), 0)
    valid = jnp.logical_or(rows >= N_PAD, chunk_in_seq > 0)
    graw = _nn_hi(sb, w2p) + gb
    yield
    g = jnp.where(valid, _logsigmoid(graw) * (1.0 / GLA_NORM), 0.0)
    r, c = _masks64()
    bcum = _nn_hi((r >= c).astype(F32), g)
    yield
    return graw, bcum, valid


def _head_mask(h):
    lane = _iota2((1, GQ_W), 1)
    return jnp.logical_and(lane >= h * GLA_DK, lane < (h + 1) * GLA_DK)


def _gla_intra_fwd(projp, w2p, gb, *, nc_seq, name):
    n = projp.shape[0]
    nct = n // CHUNK
    scale = GLA_DK ** -0.5

    grp = _group(nc_seq)
    rows = grp * CHUNK

    def inner(gi, qk_ref, v_ref, sb_ref, w2_ref, gb_ref, oi_ref, qg_ref, kd_ref, gl_ref):
        ci = (pl.program_id(0) * grp + gi) % nc_seq
        _, bc, _ = yield from _gla_gates(sb_ref[...], w2_ref[...], gb_ref[...], ci)
        bref = bc[MID:MID + 1, :]
        bl = bc[CHUNK - 1:CHUNK, :]
        q = qk_ref[:, 0:GQ_W] * scale
        k = qk_ref[:, GQ_W:2 * GQ_W]
        qi = q * jnp.exp(bc - bref)
        ki = k * jnp.exp(bref - bc)
        qg_ref[...] = q * jnp.exp(bc)
        kd_ref[...] = k * jnp.exp(bl - bc)
        gl_ref[0] = jnp.exp(bl)
        r, c = _masks64()
        incl = r >= c
        a = [jnp.where(incl, _nt(jnp.where(_head_mask(h), qi, 0.0), ki), 0.0) for h in range(NH)]
        yield
        for h in range(NH):
            oi_ref[:, h * GLA_DV:(h + 1) * GLA_DV] = _nn(a[h], v_ref[:, h * GLA_DV:(h + 1) * GLA_DV])

    kinds = ["row"] * 3 + ["whole"] * 2 + ["row"] * 3 + ["lead"]
    return pl.pallas_call(
        _per_chunk(inner, kinds, grp), name=name, grid=(nct // grp,),
        in_specs=[pl.BlockSpec((rows, 2 * GQ_W), lambda i: (i, C_GQK // (2 * GQ_W))),
                  pl.BlockSpec((rows, GV_W), lambda i: (i, C_GV // GV_W)),
                  pl.BlockSpec((rows, LANE), lambda i: (i, C_SB // LANE)),
                  pl.BlockSpec((LANE, GQ_W), lambda i: (0, 0)), pl.BlockSpec((1, GQ_W), lambda i: (0, 0))],
        out_specs=[pl.BlockSpec((rows, GV_W), lambda i: (i, 0)), pl.BlockSpec((rows, GQ_W), lambda i: (i, 0)),
                   pl.BlockSpec((rows, GQ_W), lambda i: (i, 0)), pl.BlockSpec((grp, 1, GQ_W), lambda i: (i, 0, 0))],
        out_shape=[jax.ShapeDtypeStruct((n, GV_W), F32), jax.ShapeDtypeStruct((n, GQ_W), F32),
                   jax.ShapeDtypeStruct((n, GQ_W), F32), jax.ShapeDtypeStruct((nct, 1, GQ_W), F32)],
        compiler_params=_cp(VMEM_BIG),
    )(projp, projp, projp, w2p, gb)


def _gla_scan_fwd(oi, qg, kd, gl, projp, *, bsz, nc_seq, name):
    t_seq = nc_seq * CHUNK
    oi = oi.reshape(bsz, t_seq, GV_W)
    qg, kd = qg.reshape(bsz, t_seq, GQ_W), kd.reshape(bsz, t_seq, GQ_W)
    gl = gl.reshape(bsz, nc_seq, 1, GQ_W)
    pj = projp.reshape(bsz, t_seq, PW)

    def body(oi_ref, qg_ref, kd_ref, gl_ref, v_ref, o_ref, hist_ref, st_ref):
        @pl.when(pl.program_id(0) == 0)
        def _():
            st_ref[...] = jnp.zeros_like(st_ref)

        for b in range(bsz):
            st = st_ref[b]
            hist_ref[b, 0] = st
            qgb = qg_ref[b]
            kdb = kd_ref[b]
            upd = jnp.zeros((GLA_DV, GQ_W), F32)
            for h in range(NH):
                sl = slice(h * GLA_DV, (h + 1) * GLA_DV)
                m = _head_mask(h)
                o_ref[b, :, sl] = oi_ref[b, :, sl] + _nt(jnp.where(m, qgb, 0.0), st)
                upd = upd + jnp.where(m, _tn(v_ref[b, :, sl], kdb), 0.0)
            st_ref[b] = gl_ref[b, 0] * st + upd

    outs = pl.pallas_call(
        body, name=name, grid=(nc_seq,),
        in_specs=[pl.BlockSpec((bsz, CHUNK, GV_W), lambda i: (0, i, 0)),
                  pl.BlockSpec((bsz, CHUNK, GQ_W), lambda i: (0, i, 0)),
                  pl.BlockSpec((bsz, CHUNK, GQ_W), lambda i: (0, i, 0)),
                  pl.BlockSpec((bsz, 1, 1, GQ_W), lambda i: (0, i, 0, 0)),
                  pl.BlockSpec((bsz, CHUNK, GV_W), lambda i: (0, i, C_GV // GV_W))],
        out_specs=[pl.BlockSpec((bsz, CHUNK, GV_W), lambda i: (0, i, 0)),
                   pl.BlockSpec((bsz, 1, GLA_DV, GQ_W), lambda i: (0, i, 0, 0))],
        out_shape=[jax.ShapeDtypeStruct((bsz, t_seq, GV_W), F32),
                   jax.ShapeDtypeStruct((bsz, nc_seq, GLA_DV, GQ_W), F32)],
        scratch_shapes=[pltpu.VMEM((bsz, GLA_DV, GQ_W), F32)],
        compiler_params=_cp(VMEM_BIG, ("arbitrary",)),
    )(oi, qg, kd, gl, pj)
    return outs[0].reshape(bsz * t_seq, GV_W), outs[1]


def _gla_scan_bwd(do, qg, kd, gl, projp, hist, *, bsz, nc_seq, name):
    t_seq = nc_seq * CHUNK
    do = do.reshape(bsz, t_seq, GV_W)
    qg, kd = qg.reshape(bsz, t_seq, GQ_W), kd.reshape(bsz, t_seq, GQ_W)
    gl = gl.reshape(bsz, nc_seq, 1, GQ_W)
    pj = projp.reshape(bsz, t_seq, PW)

    def body(do_ref, qg_ref, kd_ref, gl_ref, v_ref, hist_ref, dqg_ref, dkd_ref, dv_ref, dgl_ref, dst_ref):
        @pl.when(pl.program_id(0) == 0)
        def _():
            dst_ref[...] = jnp.zeros_like(dst_ref)

        for b in range(bsz):
            st = hist_ref[b, 0]
            dst = dst_ref[b]
            qgb = qg_ref[b]
            kdb = kd_ref[b]
            dqg = jnp.zeros((CHUNK, GQ_W), F32)
            dkd = jnp.zeros((CHUNK, GQ_W), F32)
            add = jnp.zeros((GLA_DV, GQ_W), F32)
            for h in range(NH):
                sl = slice(h * GLA_DV, (h + 1) * GLA_DV)
                m = _head_mask(h)
                doh = do_ref[b, :, sl]
                vh = v_ref[b, :, sl]
                dqg = dqg + jnp.where(m, _nn(doh, st), 0.0)
                dkd = dkd + jnp.where(m, _nn(vh, dst), 0.0)
                dv_ref[b, :, sl] = _nt(jnp.where(m, kdb, 0.0), dst)
                add = add + jnp.where(m, _tn(doh, qgb), 0.0)
            dqg_ref[b] = dqg
            dkd_ref[b] = dkd
            dgl_ref[b, 0] = jnp.sum(dst * st, axis=0, keepdims=True)
            dst_ref[b] = gl_ref[b, 0] * dst + add

    rev = lambda i: nc_seq - 1 - i
    outs = pl.pallas_call(
        body, name=name, grid=(nc_seq,),
        in_specs=[pl.BlockSpec((bsz, CHUNK, GV_W), lambda i: (0, rev(i), 0)),
                  pl.BlockSpec((bsz, CHUNK, GQ_W), lambda i: (0, rev(i), 0)),
                  pl.BlockSpec((bsz, CHUNK, GQ_W), lambda i: (0, rev(i), 0)),
                  pl.BlockSpec((bsz, 1, 1, GQ_W), lambda i: (0, rev(i), 0, 0)),
                  pl.BlockSpec((bsz, CHUNK, GV_W), lambda i: (0, rev(i), C_GV // GV_W)),
                  pl.BlockSpec((bsz, 1, GLA_DV, GQ_W), lambda i: (0, rev(i), 0, 0))],
        out_specs=[pl.BlockSpec((bsz, CHUNK, GQ_W), lambda i: (0, rev(i), 0)),
                   pl.BlockSpec((bsz, CHUNK, GQ_W), lambda i: (0, rev(i), 0)),
                   pl.BlockSpec((bsz, CHUNK, GV_W), lambda i: (0, rev(i), 0)),
                   pl.BlockSpec((bsz, 1, 1, GQ_W), lambda i: (0, rev(i), 0, 0))],
        out_shape=[jax.ShapeDtypeStruct((bsz, t_seq, GQ_W), F32), jax.ShapeDtypeStruct((bsz, t_seq, GQ_W), F32),
                   jax.ShapeDtypeStruct((bsz, t_seq, GV_W), F32), jax.ShapeDtypeStruct((bsz, nc_seq, 1, GQ_W), F32)],
        scratch_shapes=[pltpu.VMEM((bsz, GLA_DV, GQ_W), F32)],
        compiler_params=_cp(VMEM_BIG, ("arbitrary",)),
    )(do, qg, kd, gl, pj, hist)
    n = bsz * t_seq
    return (outs[0].reshape(n, GQ_W), outs[1].reshape(n, GQ_W), outs[2].reshape(n, GV_W),
            outs[3].reshape(bsz * nc_seq, 1, GQ_W))


def _gla_intra_bwd(projp, w2p, gb, do, dqg, dkd, dvi, dgl, *, nc_seq, name):
    n = projp.shape[0]
    nct = n // CHUNK
    scale = GLA_DK ** -0.5

    grp = _group(nc_seq)
    rows = grp * CHUNK

    def inner(gi, qk_ref, v_ref, sb_ref, w2_ref, gb_ref, do_ref, dqg_ref, dkd_ref, dvi_ref, dgl_ref,
              dqk_ref, dv_ref, dsb_ref, dw2_ref, dgb_ref):
        ci = (pl.program_id(0) * grp + gi) % nc_seq
        sb = sb_ref[...]
        w2 = w2_ref[...]
        graw, bc, valid = yield from _gla_gates(sb, w2, gb_ref[...], ci)
        bref = bc[MID:MID + 1, :]
        bl = bc[CHUNK - 1:CHUNK, :]
        q = qk_ref[:, 0:GQ_W] * scale
        k = qk_ref[:, GQ_W:2 * GQ_W]
        ex1 = jnp.exp(bc - bref)
        ex2 = jnp.exp(bref - bc)
        eb = jnp.exp(bc)
        ekd = jnp.exp(bl - bc)
        qi, ki = q * ex1, k * ex2
        r, c = _masks64()
        incl = r >= c
        upper = r <= c
        a_t, da, da_t = [], [], []
        for h in range(NH):
            sl = slice(h * GLA_DV, (h + 1) * GLA_DV)
            doh = do_ref[:, sl]
            vh = v_ref[:, sl]
            a_t.append(jnp.where(upper, _nt(jnp.where(_head_mask(h), ki, 0.0), qi), 0.0))
            da.append(jnp.where(incl, _nt(doh, vh), 0.0))
            da_t.append(jnp.where(upper, _nt(vh, doh), 0.0))
        yield
        dqi = jnp.zeros((CHUNK, GQ_W), F32)
        dki = jnp.zeros((CHUNK, GQ_W), F32)
        for h in range(NH):
            sl = slice(h * GLA_DV, (h + 1) * GLA_DV)
            m = _head_mask(h)
            dv_ref[:, sl] = _nn(a_t[h], do_ref[:, sl]) + dvi_ref[:, sl]
            dqi = dqi + jnp.where(m, _nn(da[h], ki), 0.0)
            dki = dki + jnp.where(m, _nn(da_t[h], qi), 0.0)
        yield
        dqg = dqg_ref[...]
        dkd = dkd_ref[...]
        dqk_ref[:, 0:GQ_W] = (dqi * ex1 + dqg * eb) * scale
        dqk_ref[:, GQ_W:2 * GQ_W] = dki * ex2 + dkd * ekd
        t_qi, t_ki, t_kd = dqi * qi, dki * ki, dkd * (k * ekd)
        db = t_qi - t_ki + dqg * (q * eb) - t_kd
        dbref = jnp.sum(t_ki - t_qi, axis=0, keepdims=True)
        dbl = jnp.sum(t_kd, axis=0, keepdims=True) + dgl_ref[0] * jnp.exp(bl)
        rows = _iota2((CHUNK, GQ_W), 0)
        db = db + jnp.where(rows == MID, dbref, 0.0) + jnp.where(rows == CHUNK - 1, dbl, 0.0)
        dg = _nn_hi(upper.astype(F32), db)
        yield
        dgraw = jnp.where(valid, dg * (1.0 / GLA_NORM) * _sigmoid(-graw), 0.0)
        dsb_ref[...] = _nt_hi(dgraw, w2)
        dw2 = _tn_hi(sb, dgraw)
        dgb = jnp.sum(dgraw, axis=0, keepdims=True)
        _accumulate(dw2_ref, dw2, gi)
        _accumulate(dgb_ref, dgb, gi)

    rq = pl.BlockSpec((rows, GQ_W), lambda i: (i, 0))
    rv = pl.BlockSpec((rows, GV_W), lambda i: (i, 0))
    kinds = ["row"] * 3 + ["whole"] * 2 + ["row"] * 4 + ["lead"] + ["row"] * 3 + ["whole"] * 2
    return pl.pallas_call(
        _per_chunk(inner, kinds, grp), name=name, grid=(nct // grp,),
        in_specs=[pl.BlockSpec((rows, 2 * GQ_W), lambda i: (i, C_GQK // (2 * GQ_W))),
                  pl.BlockSpec((rows, GV_W), lambda i: (i, C_GV // GV_W)),
                  pl.BlockSpec((rows, LANE), lambda i: (i, C_SB // LANE)),
                  pl.BlockSpec((LANE, GQ_W), lambda i: (0, 0)), pl.BlockSpec((1, GQ_W), lambda i: (0, 0)),
                  rv, rq, rq, rv, pl.BlockSpec((grp, 1, GQ_W), lambda i: (i, 0, 0))],
        out_specs=[pl.BlockSpec((rows, 2 * GQ_W), lambda i: (i, 0)), rv, pl.BlockSpec((rows, LANE), lambda i: (i, 0)),
                   pl.BlockSpec((LANE, GQ_W), lambda i: (0, 0)), pl.BlockSpec((1, GQ_W), lambda i: (0, 0))],
        out_shape=[jax.ShapeDtypeStruct((n, 2 * GQ_W), F32), jax.ShapeDtypeStruct((n, GV_W), F32),
                   jax.ShapeDtypeStruct((n, LANE), F32), jax.ShapeDtypeStruct((LANE, GQ_W), F32),
                   jax.ShapeDtypeStruct((1, GQ_W), F32)],
        compiler_params=_cp(VMEM_BIG, ("arbitrary",)),
    )(projp, projp, projp, w2p, gb, do, dqg, dkd, dvi, dgl)


SECTIONS = ((C_QKV, 1536), (C_DZ, 512), (C_GQK, 512), (C_GV, 512), (C_GR, 512), (C_SA, 128), (C_SB, 128))


def _inproj_bwd(secs, wp, h0, g1, dx1, *, tr, name):
    n, d = h0.shape

    def body(*refs):
        sec_refs = refs[:len(SECTIONS)]
        wp_ref, h0_ref, g_ref, dx1_ref, o_ref, dg_ref = refs[len(SECTIONS):]
        dh = None
        for s_ref, (off, wd) in zip(sec_refs, SECTIONS):
            part = _nt(s_ref[...], wp_ref[:, off:off + wd])
            dh = part if dh is None else dh + part
        dx, dg = _rms_bwd_math(h0_ref[...], g_ref[...], dh)
        o_ref[...] = dx1_ref[...] + dx

        @pl.when(pl.program_id(0) == 0)
        def _():
            dg_ref[...] = dg

        @pl.when(pl.program_id(0) > 0)
        def _():
            dg_ref[...] += dg

    row = pl.BlockSpec((tr, d), lambda i: (i, 0))
    vec = pl.BlockSpec((1, d), lambda i: (0, 0))
    return pl.pallas_call(
        body, name=name, grid=(n // tr,),
        in_specs=[pl.BlockSpec((tr, wd), lambda i: (i, 0)) for _, wd in SECTIONS]
        + [pl.BlockSpec((d, PW), lambda i: (0, 0)), row, vec, row],
        out_specs=[row, vec],
        out_shape=[jax.ShapeDtypeStruct((n, d), F32), jax.ShapeDtypeStruct((1, d), F32)],
        compiler_params=_cp(VMEM_BIG),
    )(*secs, wp, h0, g1, dx1)


def _adamw(w, g, m, v, *, name):
    r, c = w.shape
    tr = _tile(r, 256, 8) if r > 256 else r
    c1 = 1.0 - ADAM_B1 ** ADAM_STEP
    c2 = 1.0 - ADAM_B2 ** ADAM_STEP

    def body(w_ref, g_ref, m_ref, v_ref, d_ref, nm_ref, nv_ref):
        gv = g_ref[...]
        nm = ADAM_B1 * m_ref[...] + (1.0 - ADAM_B1) * gv
        nv = ADAM_B2 * v_ref[...] + (1.0 - ADAM_B2) * (gv * gv)
        d_ref[...] = -ADAM_LR * ((nm / c1) / (jnp.sqrt(nv / c2) + ADAM_EPS) + ADAM_WD * w_ref[...])
        nm_ref[...] = nm
        nv_ref[...] = nv

    blk = pl.BlockSpec((tr, c), lambda i: (i, 0))
    sds = jax.ShapeDtypeStruct((r, c), F32)
    return pl.pallas_call(
        body, name=name, grid=(r // tr,), in_specs=[blk] * 4, out_specs=[blk] * 3, out_shape=[sds] * 3,
        compiler_params=_cp(VMEM_BIG),
    )(w, g, m, v)


def _add2(a, b, *, name):
    lead, r, c = a.shape
    tr = _tile(r, 256, 8)

    def body(a_ref, b_ref, o_ref):
        o_ref[...] = a_ref[...] + b_ref[...]

    blk = pl.BlockSpec((1, tr, c), lambda s, i: (s, i, 0))
    return pl.pallas_call(
        body, name=name, grid=(lead, r // tr), in_specs=[blk, blk], out_specs=blk,
        out_shape=jax.ShapeDtypeStruct(a.shape, F32), compiler_params=_cp(VMEM_BIG),
    )(a, b)


def _sum_chips(q, *, name):
    _, r, c = q.shape
    tr = _tile(r, 256, 8)

    def body(q_ref, o_ref):
        o_ref[...] = ((q_ref[0] + q_ref[1]) + q_ref[2]) + q_ref[3]

    return pl.pallas_call(
        body, name=name, grid=(r // tr,),
        in_specs=[pl.BlockSpec((N_CHIPS, tr, c), lambda i: (0, i, 0))],
        out_specs=pl.BlockSpec((tr, c), lambda i: (i, 0)),
        out_shape=jax.ShapeDtypeStruct((r, c), F32), compiler_params=_cp(VMEM_BIG),
    )(q)


ANY = pl.BlockSpec(memory_space=pl.ANY)
VM = pl.BlockSpec(memory_space=pltpu.VMEM)
CAST_ROWS = 64


def _place():
    return lax.axis_index("x"), lax.axis_index("y"), lax.axis_index("c")


def _other_chips(x, y):
    return [(1 - x, y, 2 * (1 - x) + y), (x, 1 - y, 2 * x + 1 - y), (1 - x, 1 - y, 2 * (1 - x) + 1 - y)]


def _gather_weights(w_in, w_out, w_up, w_down, meta, conv, gw2):
    big = (w_in, w_out, w_up, w_down)
    small = (meta, conv, gw2)
    n_arr = len(big) + len(small)

    def body(*refs):
        ins = refs[:n_arr]
        outs = refs[n_arr:2 * n_arr]
        stage = refs[2 * n_arr:2 * n_arr + len(big)]
        send_sems, recv_sems, local_sems = refs[2 * n_arr + len(big):]
        x, y, c = _place()
        me = 2 * x + y
        srcs = []
        for k in range(n_arr):
            if k < len(big):
                src, dst = ins[k], stage[k]

                def cast_rows(i, carry, src=src, dst=dst):
                    rows = pl.ds(pl.multiple_of(i * CAST_ROWS, CAST_ROWS), CAST_ROWS)
                    dst[rows, :] = src[rows, :].astype(BF16)
                    return carry

                lax.fori_loop(0, src.shape[0] // CAST_ROWS, cast_rows, 0)
                srcs.append(stage[k])
            else:
                srcs.append(ins[k])
        local = [pltpu.make_async_copy(srcs[k], outs[k].at[me], local_sems.at[k]) for k in range(n_arr)]
        for cp in local:
            cp.start()
        sends = []
        for k in range(n_arr):
            for d, (px, py, _) in enumerate(_other_chips(x, y)):
                cp = pltpu.make_async_remote_copy(
                    src_ref=srcs[k], dst_ref=outs[k].at[me], send_sem=send_sems.at[k, d], recv_sem=recv_sems.at[k, d],
                    device_id=(px, py, c), device_id_type=MESH)
                cp.start()
                sends.append(cp)
        for k in range(n_arr):
            for d, (px, py, pj) in enumerate(_other_chips(x, y)):
                pltpu.make_async_remote_copy(
                    src_ref=srcs[k], dst_ref=outs[k].at[pj], send_sem=send_sems.at[k, d], recv_sem=recv_sems.at[k, d],
                    device_id=(px, py, c), device_id_type=MESH).wait_recv()
        for cp in sends:
            cp.wait_send()
        for cp in local:
            cp.wait()

    out_shape = [jax.ShapeDtypeStruct((N_CHIPS,) + a.shape, BF16) for a in big]
    out_shape += [jax.ShapeDtypeStruct((N_CHIPS,) + a.shape, F32) for a in small]
    return pl.pallas_call(
        body, name="gather_weights", in_specs=[VM] * n_arr, out_specs=[ANY] * n_arr, out_shape=out_shape,
        scratch_shapes=[pltpu.VMEM(a.shape, BF16) for a in big]
        + [pltpu.SemaphoreType.DMA((n_arr, 3)), pltpu.SemaphoreType.DMA((n_arr, 3)), pltpu.SemaphoreType.DMA((n_arr,))],
        compiler_params=_cp(VMEM_BIG),
    )(*big, *small)


def _sibling_halves(grads):
    n_arr = len(grads)

    def body(*refs):
        ins = refs[:n_arr]
        mine = refs[n_arr:2 * n_arr]
        theirs = refs[2 * n_arr:3 * n_arr]
        send_sems, recv_sems, local_sems = refs[3 * n_arr:]
        x, y, c = _place()
        copies = []
        for k in range(n_arr):
            half = ins[k].shape[1] // 2
            keep = pltpu.make_async_copy(ins[k].at[:, pl.ds(c * half, half), :], mine[k], local_sems.at[k])
            keep.start()
            give = pltpu.make_async_remote_copy(
                src_ref=ins[k].at[:, pl.ds((1 - c) * half, half), :], dst_ref=theirs[k],
                send_sem=send_sems.at[k], recv_sem=recv_sems.at[k], device_id=(x, y, 1 - c), device_id_type=MESH)
            give.start()
            copies.append((keep, give))
        for keep, give in copies:
            give.wait()
            keep.wait()

    halves = [jax.ShapeDtypeStruct((g.shape[0], g.shape[1] // 2, g.shape[2]), F32) for g in grads]
    outs = pl.pallas_call(
        body, name="sibling_halves", in_specs=[ANY] * n_arr, out_specs=[ANY] * (2 * n_arr), out_shape=halves + halves,
        scratch_shapes=[pltpu.SemaphoreType.DMA((n_arr,)), pltpu.SemaphoreType.DMA((n_arr,)),
                        pltpu.SemaphoreType.DMA((n_arr,))],
    )(*grads)
    return outs[:n_arr], outs[n_arr:]


def _chip_exchange(parts):
    n_arr = len(parts)

    def body(*refs):
        ins = refs[:n_arr]
        outs = refs[n_arr:2 * n_arr]
        send_sems, recv_sems, local_sems = refs[2 * n_arr:]
        x, y, c = _place()
        me = 2 * x + y
        local, sends = [], []
        for k in range(n_arr):
            cp = pltpu.make_async_copy(ins[k].at[me], outs[k].at[me], local_sems.at[k])
            cp.start()
            local.append(cp)
            for d, (px, py, pj) in enumerate(_other_chips(x, y)):
                cp = pltpu.make_async_remote_copy(
                    src_ref=ins[k].at[pj], dst_ref=outs[k].at[me], send_sem=send_sems.at[k, d],
                    recv_sem=recv_sems.at[k, d], device_id=(px, py, c), device_id_type=MESH)
                cp.start()
                sends.append(cp)
        for k in range(n_arr):
            for d, (px, py, pj) in enumerate(_other_chips(x, y)):
                pltpu.make_async_remote_copy(
                    src_ref=ins[k].at[pj], dst_ref=outs[k].at[pj], send_sem=send_sems.at[k, d],
                    recv_sem=recv_sems.at[k, d], device_id=(px, py, c), device_id_type=MESH).wait_recv()
        for cp in sends:
            cp.wait_send()
        for cp in local:
            cp.wait()

    return pl.pallas_call(
        body, name="chip_exchange", in_specs=[ANY] * n_arr, out_specs=[ANY] * n_arr,
        out_shape=[jax.ShapeDtypeStruct(p.shape, F32) for p in parts],
        scratch_shapes=[pltpu.SemaphoreType.DMA((n_arr, 3)), pltpu.SemaphoreType.DMA((n_arr, 3)),
                        pltpu.SemaphoreType.DMA((n_arr,))],
    )(*parts)


def _sibling_join(halves):
    n_arr = len(halves)

    def body(*refs):
        ins = refs[:n_arr]
        outs = refs[n_arr:2 * n_arr]
        send_sems, recv_sems, local_sems = refs[2 * n_arr:]
        x, y, c = _place()
        copies = []
        for k in range(n_arr):
            keep = pltpu.make_async_copy(ins[k], outs[k].at[c], local_sems.at[k])
            keep.start()
            give = pltpu.make_async_remote_copy(
                src_ref=ins[k], dst_ref=outs[k].at[c], send_sem=send_sems.at[k], recv_sem=recv_sems.at[k],
                device_id=(x, y, 1 - c), device_id_type=MESH)
            give.start()
            copies.append((keep, give))
        for k, (keep, give) in enumerate(copies):
            pltpu.make_async_remote_copy(
                src_ref=ins[k], dst_ref=outs[k].at[1 - c], send_sem=send_sems.at[k], recv_sem=recv_sems.at[k],
                device_id=(x, y, 1 - c), device_id_type=MESH).wait_recv()
            give.wait_send()
            keep.wait()

    return pl.pallas_call(
        body, name="sibling_join", in_specs=[ANY] * n_arr, out_specs=[ANY] * n_arr,
        out_shape=[jax.ShapeDtypeStruct((2,) + h.shape, F32) for h in halves],
        scratch_shapes=[pltpu.SemaphoreType.DMA((n_arr,)), pltpu.SemaphoreType.DMA((n_arr,)),
                        pltpu.SemaphoreType.DMA((n_arr,))],
    )(*halves)


PACK_ROWS = 48


def _small_allreduce(pack):
    masks = [(dx, dy, dc) for dx in (0, 1) for dy in (0, 1) for dc in (0, 1)][1:]

    def body(p_ref, o_ref, buf, send_sems, recv_sems):
        x, y, c = _place()
        me = 4 * x + 2 * y + c
        buf[me] = p_ref[...]
        sends = []
        for k, (dx, dy, dc) in enumerate(masks):
            peer = (1 - x if dx else x, 1 - y if dy else y, 1 - c if dc else c)
            cp = pltpu.make_async_remote_copy(
                src_ref=p_ref, dst_ref=buf.at[me], send_sem=send_sems.at[k], recv_sem=recv_sems.at[k],
                device_id=peer, device_id_type=MESH)
            cp.start()
            sends.append(cp)
        for k, (dx, dy, dc) in enumerate(masks):
            peer = (1 - x if dx else x, 1 - y if dy else y, 1 - c if dc else c)
            pj = 4 * peer[0] + 2 * peer[1] + peer[2]
            pltpu.make_async_remote_copy(
                src_ref=p_ref, dst_ref=buf.at[pj], send_sem=send_sems.at[k], recv_sem=recv_sems.at[k],
                device_id=peer, device_id_type=MESH).wait_recv()
        for cp in sends:
            cp.wait_send()
        tot = buf[0]
        for k in range(1, 8):
            tot = tot + buf[k]
        o_ref[...] = tot
        o_ref[0:N_META, :] = tot[0:N_META] + tot[N_META:2 * N_META]

    return pl.pallas_call(
        body, name="small_allreduce", in_specs=[VM], out_specs=VM,
        out_shape=jax.ShapeDtypeStruct((PACK_ROWS, D_MODEL), F32),
        scratch_shapes=[pltpu.VMEM((8, PACK_ROWS, D_MODEL), F32), pltpu.SemaphoreType.DMA((7,)),
                        pltpu.SemaphoreType.DMA((7,))],
    )(pack)


def _pad_lanes(vec, offset):
    k = vec.shape[1]
    return jnp.concatenate([jnp.zeros((1, offset), F32), vec, jnp.zeros((1, LANE - offset - k), F32)], axis=1)


def _local_step(x, tgt, meta, norm1_g, wp, conv_w, a_log, dt_bias, dn_norm_g, gla_w2, gla_b, gla_norm_g,
                w_out, norm2_g, w_up, w_down, final_norm_g):
    bsz, s_len, d = x.shape
    t_seq = s_len + CHUNK
    nc_seq = t_seq // CHUNK
    n = bsz * t_seq
    tr = _tile(t_seq, 832)
    tt = _tile(t_seq, 416)

    lead = jnp.concatenate([jnp.zeros((N_PAD, d), F32), meta], axis=0)
    h0 = jnp.concatenate([jnp.broadcast_to(lead[None], (bsz, CHUNK, d)), x], axis=1).reshape(n, d)
    tgt_p = jnp.concatenate([jnp.zeros((bsz, CHUNK, d), F32), tgt], axis=1).reshape(n, d)
    alog_row = _pad_lanes(a_log, 4)
    dtb_row = _pad_lanes(dt_bias, 4)
    w2p = jnp.concatenate([gla_w2, jnp.zeros((LANE - GLA_RANK, GQ_W), F32)], axis=0)

    h = _rms_fwd(h0, norm1_g, tr=tr, name="norm1")
    (projp,) = _mm(h, wp, "nn", tm=tr, tn=1280, tk=d, out_dtypes=(F32,), name="in_proj")
    qn, kn, v = _dnprep_fwd(projp, conv_w, bsz=bsz, t_seq=t_seq, tt=tt, name="dn_prep")
    u, w, qg, kd, pmat, tmat, gl = _dn_intra_fwd(qn, kn, v, projp, alog_row, dtb_row, nc_seq=nc_seq, name="dn_intra")
    o_dn, vn, hist = _dn_scan_fwd(u, w, qg, kd, pmat, gl, bsz=bsz, nc_seq=nc_seq, name="dn_scan")
    oi, gqg, gkd, ggl = _gla_intra_fwd(projp, w2p, gla_b, nc_seq=nc_seq, name="gla_intra")
    o_gla, ghist = _gla_scan_fwd(oi, gqg, gkd, ggl, projp, bsz=bsz, nc_seq=nc_seq, name="gla_scan")
    mix = _gnorm_fwd(o_dn, o_gla, projp, dn_norm_g, gla_norm_g, tr=tr, name="gated_norm")
    (x1,) = _mm(mix, w_out, "nn", tm=tr, tn=d, tk=d, out_dtypes=(F32,), extras=(h0,),
                epilogue=lambda acc, res: (res + acc,), name="out_proj")
    h2 = _rms_fwd(x1, norm2_g, tr=tr, name="norm2")

    def act_epilogue(acc):
        r = jnp.maximum(acc, 0.0)
        return acc, r * r

    up, act = _mm(h2, w_up, "nn", tm=tr, tn=1024, tk=d, out_dtypes=(BF16, BF16), epilogue=act_epilogue, name="mlp_up")
    (x2,) = _mm(act, w_down, "nn", tm=tr, tn=d, tk=1024, out_dtypes=(F32,), extras=(x1,),
                epilogue=lambda acc, res: (res + acc,), name="mlp_down")
    dx2, d_final_g, loss_tile = _final_loss(x2, final_norm_g, tgt_p, t_seq=t_seq, tr=tr, name="final_loss")

    (dup,) = _mm(dx2, w_down, "nt", tm=tr, tn=1024, tk=d, out_dtypes=(BF16,), extras=(up,),
                 epilogue=lambda acc, upv: (acc * (2.0 * jnp.maximum(upv.astype(F32), 0.0)),), name="mlp_down_bwd")
    (d_w_down,) = _mm(act, dx2, "tn", tm=1024, tn=d, tk=tr, out_dtypes=(F32,), name="w_down_grad")
    (d_w_up,) = _mm(h2, dup, "tn", tm=d, tn=1024, tk=tr, out_dtypes=(F32,), name="w_up_grad")
    (dh2,) = _mm(dup, w_up, "nt", tm=tr, tn=d, tk=1024, out_dtypes=(F32,), name="mlp_up_bwd")
    dx1, d_norm2_g = _rms_bwd_add(x1, norm2_g, dh2, dx2, tr=tr, name="norm2_bwd")

    (dmix,) = _mm(dx1, w_out, "nt", tm=tr, tn=d, tk=d, out_dtypes=(F32,), name="out_proj_bwd")
    (d_w_out,) = _mm(mix, dx1, "tn", tm=d, tn=d, tk=tr, out_dtypes=(F32,), name="w_out_grad")
    do_dn, ddz, do_gla, dgr, d_dn_norm_g, d_gla_norm_g = _gnorm_bwd(
        dmix, o_dn, o_gla, projp, dn_norm_g, gla_norm_g, tr=tr, name="gated_norm_bwd")
    du, dw, dqg, dkd, dpm, dgl = _dn_scan_bwd(do_dn, w, qg, kd, vn, pmat, gl, hist, bsz=bsz, nc_seq=nc_seq,
                                               name="dn_scan_bwd")
    dqn, dkn, dv, dsa, d_alog, d_dtb = _dn_intra_bwd(qn, kn, v, projp, alog_row, dtb_row, u, w, tmat,
                                                     du, dw, dqg, dkd, dpm, dgl, nc_seq=nc_seq, name="dn_intra_bwd")
    dz, d_conv_w = _dnprep_bwd_a(projp, conv_w, dqn, dkn, dv, bsz=bsz, t_seq=t_seq, tt=tt, name="dn_prep_bwd")
    dcin = _dnprep_bwd_b(dz, conv_w, bsz=bsz, t_seq=t_seq, tt=tt, name="conv_bwd")
    gdqg, gdkd, gdvi, gdgl = _gla_scan_bwd(do_gla, gqg, gkd, ggl, projp, ghist, bsz=bsz, nc_seq=nc_seq,
                                            name="gla_scan_bwd")
    dgqk, dgv, dsb, d_w2p, d_gla_b = _gla_intra_bwd(projp, w2p, gla_b, do_gla, gdqg, gdkd, gdvi, gdgl,
                                                    nc_seq=nc_seq, name="gla_intra_bwd")

    secs = (dcin, ddz, dgqk, dgv, dgr, dsa, dsb)
    d_wp_secs = []
    for idx, (sec, (_, wd)) in enumerate(zip(secs, SECTIONS)):
        (g_sec,) = _mm(h, sec, "tn", tm=d, tn=min(wd, 512), tk=tr, out_dtypes=(F32,), name=f"w_in_grad_{idx}")
        d_wp_secs.append(g_sec)
    dh0, d_norm1_g = _inproj_bwd(secs, wp, h0, norm1_g, dx1, tr=tt, name="in_proj_bwd")
    dh0 = dh0.reshape(bsz, t_seq, d)
    grad_x = dh0[:, CHUNK:]
    d_meta_rows = dh0[:, N_PAD:CHUNK].reshape(bsz * N_META, d)

    g_qkv, g_dz, g_gqk, g_gv, g_gr, g_sa, g_sb = d_wp_secs
    d_w_in = jnp.concatenate([g_qkv, g_dz, g_sa[:, 0:8], g_gqk, g_gv, g_gr, g_sb[:, 0:GLA_RANK]], axis=1)
    grads = dict(w_in=d_w_in, w_out=d_w_out, w_up=d_w_up, w_down=d_w_down, meta_rows=d_meta_rows,
                 norm1_g=d_norm1_g, conv_w=d_conv_w, a_log_tile=d_alog, dt_bias_tile=d_dtb, dn_norm_g=d_dn_norm_g,
                 gla_w2=d_w2p[0:GLA_RANK], gla_b=d_gla_b, gla_norm_g=d_gla_norm_g, norm2_g=d_norm2_g,
                 final_norm_g=d_final_g, loss_tile=loss_tile)
    return grad_x, grads


def _pad_layout(w_full):
    z = lambda k: jnp.zeros((w_full.shape[0], k), w_full.dtype)
    return jnp.concatenate([w_full[:, 0:2048], w_full[:, 2056:3592], w_full[:, 2048:2056], z(LANE - 8),
                            w_full[:, 3592:3608], z(LANE - GLA_RANK)], axis=1)


def _pack_small(g, bsz):
    assert bsz * N_META == 32
    row = jnp.concatenate([g["a_log_tile"], g["dt_bias_tile"], g["dn_norm_g"], g["gla_norm_g"], g["gla_b"],
                           g["loss_tile"], jnp.zeros((1, LANE), F32)], axis=1)
    return jnp.concatenate([g["meta_rows"], g["norm1_g"], g["conv_w"].reshape(6, D_MODEL), row,
                            g["gla_w2"].reshape(4, D_MODEL), g["norm2_g"], g["final_norm_g"],
                            jnp.zeros((2, D_MODEL), F32)], axis=0)


def kernel(x, meta_tokens, norm1_g, w_in, conv_w, a_log, dt_bias, dn_norm_g, gla_w2, gla_b, gla_norm_g, w_out, norm2_g, w_up, w_down, final_norm_g, loss_target, m_meta_tokens, m_norm1_g, m_w_in, m_conv_w, m_a_log, m_dt_bias, m_dn_norm_g, m_gla_w2, m_gla_b, m_gla_norm_g, m_w_out, m_norm2_g, m_w_up, m_w_down, m_final_norm_g, v_meta_tokens, v_norm1_g, v_w_in, v_conv_w, v_a_log, v_dt_bias, v_dn_norm_g, v_gla_w2, v_gla_b, v_gla_norm_g, v_w_out, v_norm2_g, v_w_up, v_w_down, v_final_norm_g):
    bsz = x.shape[0]
    chip = 2 * lax.axis_index("x") + lax.axis_index("y")

    shard_w = IN_WIDTH // N_CHIPS
    lane_pad = lambda a, wd: jnp.pad(a, ((0, 0), (0, wd - a.shape[1])))
    g_in, g_out, g_up, g_down, g_meta, g_conv, g_w2 = _gather_weights(
        lane_pad(w_in[0], D_MODEL), w_out[0], w_up[0], w_down[0], meta_tokens, conv_w[0], lane_pad(gla_w2[0], LANE))
    wp = _pad_layout(g_in[:, :, 0:shard_w].transpose(1, 0, 2).reshape(D_MODEL, IN_WIDTH))
    w_out_f = g_out.reshape(D_MODEL, D_MODEL)
    w_up_f = g_up.transpose(1, 0, 2).reshape(D_MODEL, D_FF)
    w_down_f = g_down.reshape(D_FF, D_MODEL)
    meta_f = g_meta.transpose(1, 0, 2).reshape(N_META, D_MODEL)
    conv_f = g_conv.transpose(1, 0, 2).reshape(4, QKV_W)
    w2_f = g_w2[:, :, 0:GQ_W // N_CHIPS].transpose(1, 0, 2).reshape(GLA_RANK, GQ_W)

    grad_x, g = _local_step(x, loss_target, meta_f, norm1_g, wp, conv_f, a_log, dt_bias, dn_norm_g, w2_f, gla_b,
                            gla_norm_g, w_out_f, norm2_g, w_up_f, w_down_f, final_norm_g.reshape(1, D_MODEL))

    shard_major = [
        jnp.pad(g["w_in"].reshape(D_MODEL, N_CHIPS, shard_w).transpose(1, 0, 2),
                ((0, 0), (0, 0), (0, D_MODEL - shard_w))),
        g["w_out"].reshape(N_CHIPS, D_MODEL // N_CHIPS, D_MODEL),
        g["w_up"].reshape(D_MODEL, N_CHIPS, D_FF // N_CHIPS).transpose(1, 0, 2),
        g["w_down"].reshape(N_CHIPS, D_FF // N_CHIPS, D_MODEL),
    ]
    mine, theirs = _sibling_halves(shard_major)
    pair = [_add2(a, b, name=f"pair_sum_{k}") for k, (a, b) in enumerate(zip(mine, theirs))]
    parts = _chip_exchange(pair)
    halves = [_sum_chips(q, name=f"chip_sum_{k}") for k, q in enumerate(parts)]
    joined = _sibling_join(halves)
    gw_in, gw_out, gw_up, gw_down = [j.reshape(2 * j.shape[1], j.shape[2]) for j in joined]
    gw_in = gw_in[:, 0:shard_w]

    red = _small_allreduce(_pack_small(g, bsz))
    g_meta_full = red[0:N_META]
    g_norm1 = red[32:33]
    g_conv_full = red[33:39].reshape(4, QKV_W)
    srow = red[39:40]
    g_alog, g_dtb = srow[:, 4:8], srow[:, LANE + 4:LANE + 8]
    g_dn_norm, g_gla_norm = srow[:, 2 * LANE:3 * LANE], srow[:, 3 * LANE:4 * LANE]
    g_gla_b = srow[:, 4 * LANE:6 * LANE]
    loss = srow[0, 6 * LANE]
    g_w2_full = red[40:44].reshape(GLA_RANK, GQ_W)
    g_norm2 = red[44:45]
    g_final = red[45:46]
    g_meta_sh = lax.dynamic_slice_in_dim(g_meta_full, chip * (D_MODEL // N_CHIPS), D_MODEL // N_CHIPS, axis=1)
    g_conv_sh = lax.dynamic_slice_in_dim(g_conv_full, chip * (QKV_W // N_CHIPS), QKV_W // N_CHIPS, axis=1)
    g_w2_sh = lax.dynamic_slice_in_dim(g_w2_full, chip * (GQ_W // N_CHIPS), GQ_W // N_CHIPS, axis=1)

    names = ["meta_tokens", "norm1_g", "w_in", "conv_w", "a_log", "dt_bias", "dn_norm_g", "gla_w2", "gla_b",
             "gla_norm_g", "w_out", "norm2_g", "w_up", "w_down", "final_norm_g"]
    weights = dict(meta_tokens=meta_tokens, norm1_g=norm1_g, w_in=w_in, conv_w=conv_w, a_log=a_log, dt_bias=dt_bias,
                   dn_norm_g=dn_norm_g, gla_w2=gla_w2, gla_b=gla_b, gla_norm_g=gla_norm_g, w_out=w_out,
                   norm2_g=norm2_g, w_up=w_up, w_down=w_down, final_norm_g=final_norm_g)
    ms = dict(meta_tokens=m_meta_tokens, norm1_g=m_norm1_g, w_in=m_w_in, conv_w=m_conv_w, a_log=m_a_log,
              dt_bias=m_dt_bias, dn_norm_g=m_dn_norm_g, gla_w2=m_gla_w2, gla_b=m_gla_b, gla_norm_g=m_gla_norm_g,
              w_out=m_w_out, norm2_g=m_norm2_g, w_up=m_w_up, w_down=m_w_down, final_norm_g=m_final_norm_g)
    vs = dict(meta_tokens=v_meta_tokens, norm1_g=v_norm1_g, w_in=v_w_in, conv_w=v_conv_w, a_log=v_a_log,
              dt_bias=v_dt_bias, dn_norm_g=v_dn_norm_g, gla_w2=v_gla_w2, gla_b=v_gla_b, gla_norm_g=v_gla_norm_g,
              w_out=v_w_out, norm2_g=v_norm2_g, w_up=v_w_up, w_down=v_w_down, final_norm_g=v_final_norm_g)
    grads2d = dict(meta_tokens=g_meta_sh, norm1_g=g_norm1, w_in=gw_in, conv_w=g_conv_sh, a_log=g_alog, dt_bias=g_dtb,
                   dn_norm_g=g_dn_norm, gla_w2=g_w2_sh, gla_b=g_gla_b, gla_norm_g=g_gla_norm, w_out=gw_out,
                   norm2_g=g_norm2, w_up=gw_up, w_down=gw_down, final_norm_g=g_final)
    out_g, out_d, out_m, out_v = [], [], [], []
    for nm in names:
        shape = weights[nm].shape
        g2 = grads2d[nm]
        as2d = lambda a: a.reshape(g2.shape)
        dlt, nm_, nv_ = _adamw(as2d(weights[nm]), g2, as2d(ms[nm]), as2d(vs[nm]), name=f"adamw_{nm}")
        out_g.append(g2.reshape(shape))
        out_d.append(dlt.reshape(shape))
        out_m.append(nm_.reshape(shape))
        out_v.append(nv_.reshape(shape))
    return (loss, grad_x, *out_g, *out_d, *out_m, *out_v)
```

```python
import functools

import jax
import jax.numpy as jnp
import numpy as np
from jax import lax
from jax.experimental import pallas as pl
from jax.experimental.pallas import tpu as pltpu

F32 = jnp.float32
BF16 = jnp.bfloat16
HI = lax.Precision.HIGHEST
MESH = pl.DeviceIdType.MESH

D_MODEL = 1024
N_META = 16
CHUNK = 64
N_PAD = CHUNK - N_META
NH = 4
DN_D = 128
GLA_DK = 64
GLA_DV = 128
GLA_RANK = 16
D_FF = 4 * D_MODEL
EPS = 1e-6
IN_WIDTH = 3608
C_QKV, C_DZ, C_GQK, C_GV, C_GR, C_SA, C_SB, PW = 0, 1536, 2048, 2560, 3072, 3584, 3712, 3840
LANE = 128
N_CHIPS = 4

ADAM_LR, ADAM_B1, ADAM_B2, ADAM_EPS, ADAM_WD, ADAM_STEP = 0.001, 0.9, 0.999, 1e-08, 0.01, 10

VMEM_BIG = 56 * 1024 * 1024


def _cp(vmem=None, sem=None):
    kw = {}
    if vmem is not None:
        kw["vmem_limit_bytes"] = vmem
    if sem is not None:
        kw["dimension_semantics"] = sem
    return pltpu.CompilerParams(**kw)


def _tile(n, target, mult=16):
    best = None
    for t in range(mult, min(n, target) + 1, mult):
        if n % t == 0:
            best = t
    assert best is not None, (n, target)
    return best


def _dot(a, b, dims, prec=None):
    return lax.dot_general(a, b, (dims, ((), ())), preferred_element_type=F32, precision=prec)


def _nn(a, b):
    return _dot(a.astype(BF16), b.astype(BF16), ((1,), (0,)))


def _nt(a, b):
    return _dot(a.astype(BF16), b.astype(BF16), ((1,), (1,)))


def _tn(a, b):
    return _dot(a.astype(BF16), b.astype(BF16), ((0,), (0,)))


def _nn_hi(a, b):
    return _dot(a, b, ((1,), (0,)), HI)


def _nt_hi(a, b):
    return _dot(a, b, ((1,), (1,)), HI)


def _tn_hi(a, b):
    return _dot(a, b, ((0,), (0,)), HI)


def _sigmoid(x):
    return 1.0 / (1.0 + jnp.exp(-x))


def _softplus(x):
    return jnp.maximum(x, 0.0) + jnp.log(1.0 + jnp.exp(-jnp.abs(x)))


def _logsigmoid(x):
    return -_softplus(-x)


def _iota2(shape, dim):
    return lax.broadcasted_iota(jnp.int32, shape, dim)


def _mm(a, b, mode, *, tm, tn, tk, out_dtypes, extras=(), epilogue=None, name, vmem=VMEM_BIG):
    if mode == "tn":
        K, M = a.shape
    else:
        M, K = a.shape
    N = b.shape[0] if mode == "nt" else b.shape[1]
    assert M % tm == 0 and N % tn == 0 and K % tk == 0, (name, M, N, K, tm, tn, tk)
    nk = K // tk
    n_ex, n_out = len(extras), len(out_dtypes)
    if mode == "tn":
        a_spec = pl.BlockSpec((tk, tm), lambda i, j, k: (k, i))
    else:
        a_spec = pl.BlockSpec((tm, tk), lambda i, j, k: (i, k))
    if mode == "nt":
        b_spec = pl.BlockSpec((tn, tk), lambda i, j, k: (j, k))
    else:
        b_spec = pl.BlockSpec((tk, tn), lambda i, j, k: (k, j))
    mn_spec = pl.BlockSpec((tm, tn), lambda i, j, k: (i, j))
    dims = {"nn": ((1,), (0,)), "nt": ((1,), (1,)), "tn": ((0,), (0,))}[mode]

    def body(*refs):
        a_ref, b_ref = refs[0], refs[1]
        ex_refs = refs[2:2 + n_ex]
        out_refs = refs[2 + n_ex:2 + n_ex + n_out]
        acc_ref = refs[2 + n_ex + n_out]
        k = pl.program_id(2)
        part = _dot(a_ref[...].astype(BF16), b_ref[...].astype(BF16), dims)

        @pl.when(k == 0)
        def _():
            acc_ref[...] = part

        @pl.when(k > 0)
        def _():
            acc_ref[...] += part

        @pl.when(k == nk - 1)
        def _():
            acc = acc_ref[...]
            res = (acc,) if epilogue is None else epilogue(acc, *[e[...] for e in ex_refs])
            for o_ref, r in zip(out_refs, res):
                o_ref[...] = r.astype(o_ref.dtype)

    outs = pl.pallas_call(
        body, name=name, grid=(M // tm, N // tn, nk),
        in_specs=[a_spec, b_spec] + [mn_spec] * n_ex,
        out_specs=[mn_spec] * n_out,
        out_shape=[jax.ShapeDtypeStruct((M, N), dt) for dt in out_dtypes],
        scratch_shapes=[pltpu.VMEM((tm, tn), F32)],
        compiler_params=_cp(vmem, ("parallel", "parallel", "arbitrary")),
    )(a, b, *extras)
    return tuple(outs)


def _rms_fwd(x, g, *, tr, name):
    n, d = x.shape

    def body(x_ref, g_ref, o_ref):
        xv = x_ref[...]
        r = lax.rsqrt(jnp.mean(xv * xv, axis=-1, keepdims=True) + EPS)
        o_ref[...] = (xv * r * g_ref[...]).astype(o_ref.dtype)

    return pl.pallas_call(
        body, name=name, grid=(n // tr,),
        in_specs=[pl.BlockSpec((tr, d), lambda i: (i, 0)), pl.BlockSpec((1, d), lambda i: (0, 0))],
        out_specs=pl.BlockSpec((tr, d), lambda i: (i, 0)),
        out_shape=jax.ShapeDtypeStruct((n, d), BF16),
        compiler_params=_cp(VMEM_BIG),
    )(x, g)


def _rms_bwd_math(xv, g, dy):
    r = lax.rsqrt(jnp.mean(xv * xv, axis=-1, keepdims=True) + EPS)
    xh = xv * r
    gdy = dy * g
    dx = r * (gdy - xh * jnp.mean(xh * gdy, axis=-1, keepdims=True))
    return dx, jnp.sum(dy * xh, axis=0, keepdims=True)


def _rms_bwd_add(x, g, dy, res, *, tr, name):
    n, d = x.shape

    def body(x_ref, g_ref, dy_ref, res_ref, o_ref, dg_ref):
        dx, dg = _rms_bwd_math(x_ref[...], g_ref[...], dy_ref[...])
        o_ref[...] = res_ref[...] + dx

        @pl.when(pl.program_id(0) == 0)
        def _():
            dg_ref[...] = dg

        @pl.when(pl.program_id(0) > 0)
        def _():
            dg_ref[...] += dg

    row = pl.BlockSpec((tr, d), lambda i: (i, 0))
    vec = pl.BlockSpec((1, d), lambda i: (0, 0))
    return pl.pallas_call(
        body, name=name, grid=(n // tr,),
        in_specs=[row, vec, row, row], out_specs=[row, vec],
        out_shape=[jax.ShapeDtypeStruct((n, d), F32), jax.ShapeDtypeStruct((1, d), F32)],
        compiler_params=_cp(VMEM_BIG),
    )(x, g, dy, res)


def _final_loss(x2, gf, tgt, *, t_seq, tr, name):
    n, d = x2.shape
    per_seq = t_seq // tr

    def body(x_ref, g_ref, t_ref, dx_ref, dg_ref, loss_ref):
        i = pl.program_id(0)
        xv = x_ref[...]
        g = g_ref[...]
        r = lax.rsqrt(jnp.mean(xv * xv, axis=-1, keepdims=True) + EPS)
        xh = xv * r
        pos = (i % per_seq) * tr + _iota2((tr, 1), 0)
        real = pos >= CHUNK
        err = jnp.where(real, xh * g - t_ref[...], 0.0)
        dy = err * (1.0 / d)
        gdy = dy * g
        dx_ref[...] = r * (gdy - xh * jnp.mean(xh * gdy, axis=-1, keepdims=True))
        dg = jnp.sum(dy * xh, axis=0, keepdims=True)
        ls = 0.5 * jnp.sum(jnp.mean(err * err, axis=-1, keepdims=True), axis=0, keepdims=True)
        ls = jnp.where(_iota2((1, LANE), 1) == 0, ls, 0.0)

        @pl.when(i == 0)
        def _():
            dg_ref[...] = dg
            loss_ref[...] = ls

        @pl.when(i > 0)
        def _():
            dg_ref[...] += dg
            loss_ref[...] += ls

    row = pl.BlockSpec((tr, d), lambda i: (i, 0))
    vec = pl.BlockSpec((1, d), lambda i: (0, 0))
    one = pl.BlockSpec((1, LANE), lambda i: (0, 0))
    return pl.pallas_call(
        body, name=name, grid=(n // tr,),
        in_specs=[row, vec, row], out_specs=[row, vec, one],
        out_shape=[jax.ShapeDtypeStruct((n, d), F32), jax.ShapeDtypeStruct((1, d), F32),
                   jax.ShapeDtypeStruct((1, LANE), F32)],
        compiler_params=_cp(VMEM_BIG),
    )(x2, gf, tgt)


def _gnorm_fwd(o_dn, o_gla, projp, g_dn, g_gla, *, tr, name):
    n = o_dn.shape[0]
    w = NH * DN_D

    def body(odn_ref, ogl_ref, z_ref, r_ref, gdn_ref, ggl_ref, mix_ref):
        for grp, (o_ref, gate_ref, gain_ref) in enumerate(((odn_ref, z_ref, gdn_ref), (ogl_ref, r_ref, ggl_ref))):
            gain = gain_ref[...]
            for h in range(NH):
                sl = slice(h * DN_D, (h + 1) * DN_D)
                o = o_ref[:, sl]
                z = gate_ref[:, sl]
                r = lax.rsqrt(jnp.mean(o * o, axis=-1, keepdims=True) + EPS)
                y = (o * r * gain) * (z * _sigmoid(z))
                mix_ref[:, grp * w + h * DN_D: grp * w + (h + 1) * DN_D] = y.astype(mix_ref.dtype)

    row = pl.BlockSpec((tr, w), lambda i: (i, 0))
    vec = pl.BlockSpec((1, DN_D), lambda i: (0, 0))
    return pl.pallas_call(
        body, name=name, grid=(n // tr,),
        in_specs=[row, row, pl.BlockSpec((tr, w), lambda i: (i, C_DZ // w)),
                  pl.BlockSpec((tr, w), lambda i: (i, C_GR // w)), vec, vec],
        out_specs=pl.BlockSpec((tr, 2 * w), lambda i: (i, 0)),
        out_shape=jax.ShapeDtypeStruct((n, 2 * w), BF16),
        compiler_params=_cp(VMEM_BIG),
    )(o_dn, o_gla, projp, projp, g_dn, g_gla)


def _gnorm_bwd(dmix, o_dn, o_gla, projp, g_dn, g_gla, *, tr, name):
    n = o_dn.shape[0]
    w = NH * DN_D

    def body(dm_ref, odn_ref, ogl_ref, z_ref, r_ref, gdn_ref, ggl_ref,
             dodn_ref, ddz_ref, dogl_ref, dgr_ref, dgdn_ref, dggl_ref):
        first = pl.program_id(0) == 0
        groups = ((odn_ref, z_ref, gdn_ref, dodn_ref, ddz_ref, dgdn_ref),
                  (ogl_ref, r_ref, ggl_ref, dogl_ref, dgr_ref, dggl_ref))
        for grp, (o_ref, gate_ref, gain_ref, do_ref, dgate_ref, dgain_ref) in enumerate(groups):
            gain = gain_ref[...]
            dgain = jnp.zeros((1, DN_D), F32)
            for h in range(NH):
                sl = slice(h * DN_D, (h + 1) * DN_D)
                o = o_ref[:, sl]
                z = gate_ref[:, sl]
                dm = dm_ref[:, grp * w + h * DN_D: grp * w + (h + 1) * DN_D]
                r = lax.rsqrt(jnp.mean(o * o, axis=-1, keepdims=True) + EPS)
                oh = o * r
                s = _sigmoid(z)
                dn = dm * (z * s)
                dgate_ref[:, sl] = dm * (oh * gain) * (s * (1.0 + z * (1.0 - s)))
                gdn = dn * gain
                do_ref[:, sl] = r * (gdn - oh * jnp.mean(oh * gdn, axis=-1, keepdims=True))
                dgain = dgain + jnp.sum(dn * oh, axis=0, keepdims=True)

            @pl.when(first)
            def _():
                dgain_ref[...] = dgain

            @pl.when(jnp.logical_not(first))
            def _():
                dgain_ref[...] += dgain

    row = pl.BlockSpec((tr, w), lambda i: (i, 0))
    vec = pl.BlockSpec((1, DN_D), lambda i: (0, 0))
    big = jax.ShapeDtypeStruct((n, w), F32)
    small = jax.ShapeDtypeStruct((1, DN_D), F32)
    return pl.pallas_call(
        body, name=name, grid=(n // tr,),
        in_specs=[pl.BlockSpec((tr, 2 * w), lambda i: (i, 0)), row, row,
                  pl.BlockSpec((tr, w), lambda i: (i, C_DZ // w)), pl.BlockSpec((tr, w), lambda i: (i, C_GR // w)), vec, vec],
        out_specs=[row, row, row, row, vec, vec],
        out_shape=[big, big, big, big, small, small],
        compiler_params=_cp(VMEM_BIG),
    )(dmix, o_dn, o_gla, projp, projp, g_dn, g_gla)


QKV_W = 3 * NH * DN_D
HALO = 8


def _conv_z(xs_ref, cw_ref, tt):
    z = cw_ref[0:1, :] * xs_ref[pl.ds(HALO - 3, tt), :]
    for j in range(1, 4):
        z = z + cw_ref[j:j + 1, :] * xs_ref[pl.ds(HALO - 3 + j, tt), :]
    return z


def _dnprep_fwd(projp, conv_w, *, bsz, t_seq, tt, name):
    n = bsz * t_seq
    per_seq = t_seq // tt
    hw = NH * DN_D

    def body(x_ref, halo_ref, cw_ref, q_ref, k_ref, v_ref, xs_ref):
        i = pl.program_id(1)
        xs_ref[0:HALO, :] = jnp.where(i == 0, 0.0, halo_ref[...])
        xs_ref[HALO:HALO + tt, :] = x_ref[...]
        z = _conv_z(xs_ref, cw_ref, tt)
        a = z * _sigmoid(z)
        for grp, o_ref in enumerate((q_ref, k_ref)):
            for h in range(NH):
                ah = a[:, grp * hw + h * DN_D: grp * hw + (h + 1) * DN_D]
                rs = lax.rsqrt(jnp.sum(ah * ah, axis=-1, keepdims=True) + EPS)
                o_ref[:, h * DN_D:(h + 1) * DN_D] = ah * rs
        v_ref[...] = a[:, 2 * hw:3 * hw]

    def halo_map(b, i):
        return (jnp.maximum((b * t_seq + i * tt) // HALO - 1, 0), 0)

    out = pl.BlockSpec((tt, hw), lambda b, i: (b * per_seq + i, 0))
    sds = jax.ShapeDtypeStruct((n, hw), F32)
    return pl.pallas_call(
        body, name=name, grid=(bsz, per_seq),
        in_specs=[pl.BlockSpec((tt, QKV_W), lambda b, i: (b * per_seq + i, 0)),
                  pl.BlockSpec((HALO, QKV_W), halo_map),
                  pl.BlockSpec((4, QKV_W), lambda b, i: (0, 0))],
        out_specs=[out, out, out], out_shape=[sds, sds, sds],
        scratch_shapes=[pltpu.VMEM((tt + HALO, QKV_W), F32)],
        compiler_params=_cp(VMEM_BIG),
    )(projp, projp, conv_w)


def _dnprep_bwd_a(projp, conv_w, dq, dk, dv, *, bsz, t_seq, tt, name):
    n = bsz * t_seq
    per_seq = t_seq // tt
    hw = NH * DN_D

    def body(x_ref, halo_ref, cw_ref, dq_ref, dk_ref, dv_ref, dz_ref, dcw_ref, xs_ref):
        b, i = pl.program_id(0), pl.program_id(1)
        xs_ref[0:HALO, :] = jnp.where(i == 0, 0.0, halo_ref[...])
        xs_ref[HALO:HALO + tt, :] = x_ref[...]
        z = _conv_z(xs_ref, cw_ref, tt)
        s = _sigmoid(z)
        a = z * s
        dsilu = s * (1.0 + z * (1.0 - s))
        for grp, d_ref in enumerate((dq_ref, dk_ref)):
            for h in range(NH):
                sl = slice(grp * hw + h * DN_D, grp * hw + (h + 1) * DN_D)
                ah = a[:, sl]
                rs = lax.rsqrt(jnp.sum(ah * ah, axis=-1, keepdims=True) + EPS)
                y = ah * rs
                dy = d_ref[:, h * DN_D:(h + 1) * DN_D]
                da = rs * (dy - y * jnp.sum(dy * y, axis=-1, keepdims=True))
                dz_ref[:, sl] = da * dsilu[:, sl]
        dz_ref[:, 2 * hw:3 * hw] = dv_ref[...] * dsilu[:, 2 * hw:3 * hw]
        dz = dz_ref[...]
        first = jnp.logical_and(b == 0, i == 0)
        for j in range(4):
            part = jnp.sum(dz * xs_ref[pl.ds(HALO - 3 + j, tt), :], axis=0, keepdims=True)

            @pl.when(first)
            def _():
                dcw_ref[j:j + 1, :] = part

            @pl.when(jnp.logical_not(first))
            def _():
                dcw_ref[j:j + 1, :] += part

    def halo_map(b, i):
        return (jnp.maximum((b * t_seq + i * tt) // HALO - 1, 0), 0)

    hrow = pl.BlockSpec((tt, hw), lambda b, i: (b * per_seq + i, 0))
    return pl.pallas_call(
        body, name=name, grid=(bsz, per_seq),
        in_specs=[pl.BlockSpec((tt, QKV_W), lambda b, i: (b * per_seq + i, 0)),
                  pl.BlockSpec((HALO, QKV_W), halo_map),
                  pl.BlockSpec((4, QKV_W), lambda b, i: (0, 0)), hrow, hrow, hrow],
        out_specs=[pl.BlockSpec((tt, QKV_W), lambda b, i: (b * per_seq + i, 0)),
                   pl.BlockSpec((4, QKV_W), lambda b, i: (0, 0))],
        out_shape=[jax.ShapeDtypeStruct((n, QKV_W), F32), jax.ShapeDtypeStruct((4, QKV_W), F32)],
        scratch_shapes=[pltpu.VMEM((tt + HALO, QKV_W), F32)],
        compiler_params=_cp(VMEM_BIG),
    )(projp, projp, conv_w, dq, dk, dv)


def _dnprep_bwd_b(dz, conv_w, *, bsz, t_seq, tt, name):
    n = bsz * t_seq
    per_seq = t_seq // tt
    last_blk = n // HALO - 1

    def body(dz_ref, halo_ref, cw_ref, dx_ref, ds_ref):
        i = pl.program_id(1)
        ds_ref[0:tt, :] = dz_ref[...]
        ds_ref[tt:tt + HALO, :] = jnp.where(i == per_seq - 1, 0.0, halo_ref[...])
        dx = cw_ref[0:1, :] * ds_ref[pl.ds(3, tt), :]
        for j in range(1, 4):
            dx = dx + cw_ref[j:j + 1, :] * ds_ref[pl.ds(3 - j, tt), :]
        dx_ref[...] = dx

    def halo_map(b, i):
        return (jnp.minimum((b * t_seq + (i + 1) * tt) // HALO, last_blk), 0)

    row = pl.BlockSpec((tt, QKV_W), lambda b, i: (b * per_seq + i, 0))
    return pl.pallas_call(
        body, name=name, grid=(bsz, per_seq),
        in_specs=[row, pl.BlockSpec((HALO, QKV_W), halo_map), pl.BlockSpec((4, QKV_W), lambda b, i: (0, 0))],
        out_specs=row, out_shape=jax.ShapeDtypeStruct((n, QKV_W), F32),
        scratch_shapes=[pltpu.VMEM((tt + HALO, QKV_W), F32)],
        compiler_params=_cp(VMEM_BIG),
    )(dz, dz, conv_w)


def _masks64():
    r = _iota2((CHUNK, CHUNK), 0)
    c = _iota2((CHUNK, CHUNK), 1)
    return r, c


def _group(nc_seq, target=5):
    return max(g for g in range(1, target + 1) if nc_seq % g == 0)


def _round_robin(chains):
    live = list(chains)
    while live:
        nxt = []
        for ch in live:
            try:
                next(ch)
                nxt.append(ch)
            except StopIteration:
                pass
        live = nxt
        yield


def _run(chains):
    for _ in _round_robin(chains):
        pass


def _per_chunk(inner, kinds, grp):
    def body(*refs):
        chains = []
        for gi in range(grp):
            views = []
            for r, kind in zip(refs, kinds):
                if kind == "row":
                    views.append(r.at[pl.ds(gi * CHUNK, CHUNK)])
                elif kind == "lead":
                    views.append(r.at[pl.ds(gi, 1)])
                else:
                    views.append(r)
            chains.append(inner(gi, *views))
        _run(chains)
    return body


def _accumulate(ref, val, gi):
    if gi > 0:
        ref[...] += val
        return
    first = pl.program_id(0) == 0

    @pl.when(first)
    def _():
        ref[...] = val

    @pl.when(jnp.logical_not(first))
    def _():
        ref[...] += val


def _tri_inv(a_strict):
    r, c = _masks64()
    eye = (r == c).astype(F32)
    blk16 = (r // 16) == (c // 16)
    blk32 = (r // 32) == (c // 32)
    ld = jnp.where(blk16, a_strict, 0.0)
    x = eye - ld
    p = _nn(ld, ld)
    yield
    for step in range(3):
        xp = _nn(x, p)
        if step < 2:
            p = _nn(p, p)
        x = x + xp
        yield
    for lk in (jnp.where(jnp.logical_and(blk32, jnp.logical_not(blk16)), a_strict, 0.0),
               jnp.where(blk32, 0.0, a_strict)):
        y = x - eye
        s = lk + _nn(y, lk)
        yield
        x = x - s - _nn(s, y)
        yield
    return x


def _dn_gates(sa, alog, dtb, chunk_in_seq):
    rows = _iota2((CHUNK, LANE), 0)
    valid = jnp.logical_or(rows >= N_PAD, chunk_in_seq > 0)
    beta_t = _sigmoid(sa)
    ea = jnp.exp(alog)
    g_t = jnp.where(valid, -ea * _softplus(sa + dtb), 0.0)
    r, c = _masks64()
    ltri = (r >= c).astype(F32)
    gam_t = _nn_hi(ltri, g_t)
    return beta_t, g_t, gam_t, valid, ea


def _dn_intra_fwd(qn, kn, v, projp, alog_row, dtb_row, *, nc_seq, name):
    n = qn.shape[0]
    nct = n // CHUNK
    hw = NH * DN_D
    scale = DN_D ** -0.5

    grp = _group(nc_seq)

    def inner(gi, q_ref, k_ref, v_ref, sa_ref, al_ref, dt_ref, u_ref, w_ref, qg_ref, kd_ref, p_ref, t_ref, gl_ref):
        ci = (pl.program_id(0) * grp + gi) % nc_seq
        beta_t, _, gam_t, _, _ = _dn_gates(sa_ref[...], al_ref[...], dt_ref[...], ci)
        yield
        gam_tt = gam_t.T
        r, c = _masks64()
        incl = r >= c
        strict = r > c

        def head(h):
            sl = slice(h * DN_D, (h + 1) * DN_D)
            beta = beta_t[:, h:h + 1]
            gam = gam_t[:, 4 + h:5 + h]
            gam_row = gam_tt[4 + h:5 + h, :]
            gl = gam_t[CHUNK - 1:CHUNK, 4 + h:5 + h]
            dec = jnp.exp(jnp.where(incl, gam - gam_row, -jnp.inf))
            kh = k_ref[:, sl]
            qh = q_ref[:, sl] * scale
            vh = v_ref[:, sl]
            kk = _nt(kh, kh)
            qk = _nt(qh, kh)
            yield
            a = jnp.where(strict, beta * kk * dec, 0.0)
            tm = yield from _tri_inv(a)
            egam = jnp.exp(gam)
            u_ref[:, sl] = _nn(tm, beta * vh)
            w_ref[:, sl] = _nn(tm, (beta * egam) * kh)
            qg_ref[:, sl] = egam * qh
            kd_ref[:, sl] = jnp.exp(gl - gam) * kh
            p_ref[0, h] = qk * dec
            t_ref[0, h] = tm
            gl_ref[0, h:h + 1, :] = jnp.broadcast_to(jnp.exp(gl), (1, LANE))

        yield from _round_robin([head(h) for h in range(NH)])

    rows = grp * CHUNK
    row = pl.BlockSpec((rows, hw), lambda i: (i, 0))
    vec = pl.BlockSpec((1, LANE), lambda i: (0, 0))
    mat = pl.BlockSpec((grp, NH, CHUNK, CHUNK), lambda i: (i, 0, 0, 0))
    big = jax.ShapeDtypeStruct((n, hw), F32)
    msd = jax.ShapeDtypeStruct((nct, NH, CHUNK, CHUNK), F32)
    kinds = ["row"] * 4 + ["whole"] * 2 + ["row"] * 4 + ["lead"] * 3
    return pl.pallas_call(
        _per_chunk(inner, kinds, grp), name=name, grid=(nct // grp,),
        in_specs=[row, row, row, pl.BlockSpec((rows, LANE), lambda i: (i, C_SA // LANE)), vec, vec],
        out_specs=[row, row, row, row, mat, mat, pl.BlockSpec((grp, NH, LANE), lambda i: (i, 0, 0))],
        out_shape=[big, big, big, big, msd, msd, jax.ShapeDtypeStruct((nct, NH, LANE), F32)],
        compiler_params=_cp(VMEM_BIG),
    )(qn, kn, v, projp, alog_row, dtb_row)


def _dn_scan_fwd(u, w, qg, kd, p, gl, *, bsz, nc_seq, name):
    hw = NH * DN_D
    t_seq = nc_seq * CHUNK
    u, w, qg, kd = (z.reshape(bsz, t_seq, hw) for z in (u, w, qg, kd))
    p = p.reshape(bsz, nc_seq, NH, CHUNK, CHUNK)
    gl = gl.reshape(bsz, nc_seq, NH, LANE)

    def body(u_ref, w_ref, qg_ref, kd_ref, p_ref, gl_ref, o_ref, vn_ref, hist_ref, s_ref):
        @pl.when(pl.program_id(0) == 0)
        def _():
            s_ref[...] = jnp.zeros_like(s_ref)

        def chain(b, h):
            sl = slice(h * DN_D, (h + 1) * DN_D)
            s = s_ref[b, h]
            hist_ref[b, 0, h] = s
            ws = _nn(w_ref[b, :, sl], s)
            qs = _nn(qg_ref[b, :, sl], s)
            yield
            vn = u_ref[b, :, sl] - ws
            vn_ref[b, :, sl] = vn
            o_ref[b, :, sl] = qs + _nn(p_ref[b, 0, h], vn)
            s_ref[b, h] = gl_ref[b, 0, h:h + 1, :] * s + _tn(kd_ref[b, :, sl], vn)

        _run([chain(b, h) for b in range(bsz) for h in range(NH)])

    row = pl.BlockSpec((bsz, CHUNK, hw), lambda i: (0, i, 0))
    outs = pl.pallas_call(
        body, name=name, grid=(nc_seq,),
        in_specs=[row, row, row, row, pl.BlockSpec((bsz, 1, NH, CHUNK, CHUNK), lambda i: (0, i, 0, 0, 0)),
                  pl.BlockSpec((bsz, 1, NH, LANE), lambda i: (0, i, 0, 0))],
        out_specs=[row, row, pl.BlockSpec((bsz, 1, NH, DN_D, DN_D), lambda i: (0, i, 0, 0, 0))],
        out_shape=[jax.ShapeDtypeStruct((bsz, t_seq, hw), F32), jax.ShapeDtypeStruct((bsz, t_seq, hw), F32),
                   jax.ShapeDtypeStruct((bsz, nc_seq, NH, DN_D, DN_D), F32)],
        scratch_shapes=[pltpu.VMEM((bsz, NH, DN_D, DN_D), F32)],
        compiler_params=_cp(VMEM_BIG, ("arbitrary",)),
    )(u, w, qg, kd, p, gl)
    o, vn, hist = outs
    return o.reshape(bsz * t_seq, hw), vn.reshape(bsz * t_seq, hw), hist


def _dn_scan_bwd(do, w, qg, kd, vn, p, gl, hist, *, bsz, nc_seq, name):
    hw = NH * DN_D
    t_seq = nc_seq * CHUNK
    do, w, qg, kd, vn = (z.reshape(bsz, t_seq, hw) for z in (do, w, qg, kd, vn))
    p = p.reshape(bsz, nc_seq, NH, CHUNK, CHUNK)
    gl = gl.reshape(bsz, nc_seq, NH, LANE)

    def body(do_ref, w_ref, qg_ref, kd_ref, vn_ref, p_ref, gl_ref, hist_ref,
             du_ref, dw_ref, dqg_ref, dkd_ref, dp_ref, dgl_ref, ds_ref):
        @pl.when(pl.program_id(0) == 0)
        def _():
            ds_ref[...] = jnp.zeros_like(ds_ref)

        def chain(b, h):
            sl = slice(h * DN_D, (h + 1) * DN_D)
            s = hist_ref[b, 0, h]
            dsn = ds_ref[b, h]
            doh = do_ref[b, :, sl]
            vnh = vn_ref[b, :, sl]
            kdh = kd_ref[b, :, sl]
            dvn = _tn(p_ref[b, 0, h], doh) + _nn(kdh, dsn)
            du_ref[b, :, sl] = dvn
            dqg_ref[b, :, sl] = _nt(doh, s)
            dp_ref[b, 0, h] = _nt(doh, vnh)
            dkd_ref[b, :, sl] = _nt(vnh, dsn)
            ds_part = _tn(qg_ref[b, :, sl], doh) + gl_ref[b, 0, h:h + 1, :] * dsn
            dgl = jnp.sum(jnp.sum(dsn * s, axis=0, keepdims=True), axis=1, keepdims=True)
            dgl_ref[b, 0, h:h + 1, :] = jnp.broadcast_to(dgl, (1, LANE))
            yield
            dw_ref[b, :, sl] = -_nt(dvn, s)
            ds_ref[b, h] = ds_part - _tn(w_ref[b, :, sl], dvn)

        _run([chain(b, h) for b in range(bsz) for h in range(NH)])

    rev = lambda i: nc_seq - 1 - i
    row = pl.BlockSpec((bsz, CHUNK, hw), lambda i: (0, rev(i), 0))
    mat = pl.BlockSpec((bsz, 1, NH, CHUNK, CHUNK), lambda i: (0, rev(i), 0, 0, 0))
    glb = pl.BlockSpec((bsz, 1, NH, LANE), lambda i: (0, rev(i), 0, 0))
    big = jax.ShapeDtypeStruct((bsz, t_seq, hw), F32)
    outs = pl.pallas_call(
        body, name=name, grid=(nc_seq,),
        in_specs=[row, row, row, row, row, mat, glb,
                  pl.BlockSpec((bsz, 1, NH, DN_D, DN_D), lambda i: (0, rev(i), 0, 0, 0))],
        out_specs=[row, row, row, row, mat, glb],
        out_shape=[big, big, big, big, jax.ShapeDtypeStruct((bsz, nc_seq, NH, CHUNK, CHUNK), F32),
                   jax.ShapeDtypeStruct((bsz, nc_seq, NH, LANE), F32)],
        scratch_shapes=[pltpu.VMEM((bsz, NH, DN_D, DN_D), F32)],
        compiler_params=_cp(VMEM_BIG, ("arbitrary",)),
    )(do, w, qg, kd, vn, p, gl, hist)
    du, dw, dqg, dkd, dp, dgl = outs
    n = bsz * t_seq
    return (du.reshape(n, hw), dw.reshape(n, hw), dqg.reshape(n, hw), dkd.reshape(n, hw),
            dp.reshape(bsz * nc_seq, NH, CHUNK, CHUNK), dgl.reshape(bsz * nc_seq, NH, LANE))


def _dn_intra_bwd(qn, kn, v, projp, alog_row, dtb_row, u, w, tmat, du, dw, dqg, dkd, dp, dgl, *, nc_seq, name):
    n = qn.shape[0]
    nct = n // CHUNK
    hw = NH * DN_D
    scale = DN_D ** -0.5

    grp = _group(nc_seq)

    def inner(gi, q_ref, k_ref, v_ref, sa_ref, al_ref, dt_ref, u_ref, w_ref, t_ref, du_ref, dw_ref, dqg_ref, dkd_ref,
              dp_ref, dgl_ref, dq_ref, dk_ref, dv_ref, dsa_ref, dal_ref, ddt_ref):
        ci = (pl.program_id(0) * grp + gi) % nc_seq
        sa = sa_ref[...]
        beta_t, g_t, gam_t, valid, ea = _dn_gates(sa, al_ref[...], dt_ref[...], ci)
        yield
        gam_tt = gam_t.T
        r, c = _masks64()
        incl = r >= c
        strict = r > c
        lane = _iota2((CHUNK, LANE), 1)
        rows1 = _iota2((CHUNK, 1), 0)
        acc = [jnp.zeros((CHUNK, LANE), F32)]

        def head(h):
            sl = slice(h * DN_D, (h + 1) * DN_D)
            beta = beta_t[:, h:h + 1]
            gam = gam_t[:, 4 + h:5 + h]
            gam_row = gam_tt[4 + h:5 + h, :]
            gl = gam_t[CHUNK - 1:CHUNK, 4 + h:5 + h]
            dec = jnp.exp(jnp.where(incl, gam - gam_row, -jnp.inf))
            kh = k_ref[:, sl]
            qh = q_ref[:, sl] * scale
            vh = v_ref[:, sl]
            kk = _nt(kh, kh)
            qk = _nt(qh, kh)
            a = jnp.where(strict, beta * kk * dec, 0.0)
            pm = qk * dec
            tm = t_ref[0, h]
            uh = u_ref[:, sl]
            wh = w_ref[:, sl]
            egam = jnp.exp(gam)
            ekd = jnp.exp(gl - gam)
            dvb = _tn(tm, du_ref[:, sl])
            dkg = _tn(tm, dw_ref[:, sl])
            yield
            da = jnp.where(strict, -(_nt(dvb, uh) + _nt(dkg, wh)), 0.0)
            yield
            dad = da * dec
            dkk = beta * dad
            dbeta = jnp.sum(dad * kk, axis=1, keepdims=True)
            dpm = jnp.where(incl, dp_ref[0, h], 0.0)
            dqk = dpm * dec
            e = da * a + dpm * pm
            dgam = jnp.sum(e, axis=1, keepdims=True) - jnp.sum(e.T, axis=1, keepdims=True)
            dqgh = dqg_ref[:, sl]
            dkdh = dkd_ref[:, sl]
            dkh = (_nn(dkk, kh) + _tn(dkk, kh) + _tn(dqk, qh) + (beta * egam) * dkg + ekd * dkdh)
            dqh = _nn(dqk, kh) + egam * dqgh
            dbeta = dbeta + jnp.sum(dkg * (egam * kh), axis=1, keepdims=True) + jnp.sum(dvb * vh, axis=1, keepdims=True)
            rkd = jnp.sum(dkdh * (ekd * kh), axis=1, keepdims=True)
            dgam = (dgam + jnp.sum(dkg * ((beta * egam) * kh), axis=1, keepdims=True)
                    + jnp.sum(dqgh * (egam * qh), axis=1, keepdims=True) - rkd)
            dgam_last = jnp.sum(rkd, axis=0, keepdims=True) + dgl_ref[0, h:h + 1, 0:1] * jnp.exp(gl)
            dgam = dgam + jnp.where(rows1 == CHUNK - 1, dgam_last, 0.0)
            dq_ref[:, sl] = dqh * scale
            dk_ref[:, sl] = dkh
            dv_ref[:, sl] = beta * dvb
            acc[0] = acc[0] + jnp.where(lane == h, dbeta, 0.0) + jnp.where(lane == 4 + h, dgam, 0.0)

        yield from _round_robin([head(h) for h in range(NH)])
        acc_t = acc[0]
        utri = (r <= c).astype(F32)
        dg_t = _nn_hi(utri, acc_t)
        ddb = acc_t * beta_t * (1.0 - beta_t)
        dda = jnp.where(valid, dg_t * (-ea) * _sigmoid(sa + dt_ref[...]), 0.0)
        dsa_ref[...] = jnp.where(lane < 4, ddb, jnp.where(lane < 8, dda, 0.0))
        in_g = jnp.logical_and(lane >= 4, lane < 8)
        dal = jnp.sum(jnp.where(in_g, dg_t * g_t, 0.0), axis=0, keepdims=True)
        ddt = jnp.sum(jnp.where(in_g, dda, 0.0), axis=0, keepdims=True)
        _accumulate(dal_ref, dal, gi)
        _accumulate(ddt_ref, ddt, gi)

    rows = grp * CHUNK
    row = pl.BlockSpec((rows, hw), lambda i: (i, 0))
    vec = pl.BlockSpec((1, LANE), lambda i: (0, 0))
    mat = pl.BlockSpec((grp, NH, CHUNK, CHUNK), lambda i: (i, 0, 0, 0))
    glb = pl.BlockSpec((grp, NH, LANE), lambda i: (i, 0, 0))
    big = jax.ShapeDtypeStruct((n, hw), F32)
    v128 = jax.ShapeDtypeStruct((1, LANE), F32)
    kinds = (["row"] * 4 + ["whole"] * 2 + ["row"] * 2 + ["lead"] + ["row"] * 4 + ["lead"] * 2
             + ["row"] * 4 + ["whole"] * 2)
    return pl.pallas_call(
        _per_chunk(inner, kinds, grp), name=name, grid=(nct // grp,),
        in_specs=[row, row, row, pl.BlockSpec((rows, LANE), lambda i: (i, C_SA // LANE)), vec, vec,
                  row, row, mat, row, row, row, row, mat, glb],
        out_specs=[row, row, row, pl.BlockSpec((rows, LANE), lambda i: (i, 0)), vec, vec],
        out_shape=[big, big, big, jax.ShapeDtypeStruct((n, LANE), F32), v128, v128],
        compiler_params=_cp(VMEM_BIG, ("arbitrary",)),
    )(qn, kn, v, projp, alog_row, dtb_row, u, w, tmat, du, dw, dqg, dkd, dp, dgl)


GQ_W = NH * GLA_DK
GV_W = NH * GLA_DV
GLA_NORM = 16.0
MID = CHUNK // 2


def _gla_gates(sb, w2p, gb, chunk_in_seq):
    rows = _iota2((CHUNK, GQ_W), 0)
    valid = jnp.logical_or(rows >= N_PAD, chunk_in_seq > 0)
    graw = _nn_hi(sb, w2p) + gb
    yield
    g = jnp.where(valid, _logsigmoid(graw) * (1.0 / GLA_NORM), 0.0)
    r, c = _masks64()
    bcum = _nn_hi((r >= c).astype(F32), g)
    yield
    return graw, bcum, valid


def _head_mask(h):
    lane = _iota2((1, GQ_W), 1)
    return jnp.logical_and(lane >= h * GLA_DK, lane < (h + 1) * GLA_DK)


def _gla_intra_fwd(projp, w2p, gb, *, nc_seq, name):
    n = projp.shape[0]
    nct = n // CHUNK
    scale = GLA_DK ** -0.5

    grp = _group(nc_seq)
    rows = grp * CHUNK

    def inner(gi, qk_ref, v_ref, sb_ref, w2_ref, gb_ref, oi_ref, qg_ref, kd_ref, gl_ref):
        ci = (pl.program_id(0) * grp + gi) % nc_seq
        _, bc, _ = yield from _gla_gates(sb_ref[...], w2_ref[...], gb_ref[...], ci)
        bref = bc[MID:MID + 1, :]
        bl = bc[CHUNK - 1:CHUNK, :]
        q = qk_ref[:, 0:GQ_W] * scale
        k = qk_ref[:, GQ_W:2 * GQ_W]
        qi = q * jnp.exp(bc - bref)
        ki = k * jnp.exp(bref - bc)
        qg_ref[...] = q * jnp.exp(bc)
        kd_ref[...] = k * jnp.exp(bl - bc)
        gl_ref[0] = jnp.exp(bl)
        r, c = _masks64()
        incl = r >= c
        a = [jnp.where(incl, _nt(jnp.where(_head_mask(h), qi, 0.0), ki), 0.0) for h in range(NH)]
        yield
        for h in range(NH):
            oi_ref[:, h * GLA_DV:(h + 1) * GLA_DV] = _nn(a[h], v_ref[:, h * GLA_DV:(h + 1) * GLA_DV])

    kinds = ["row"] * 3 + ["whole"] * 2 + ["row"] * 3 + ["lead"]
    return pl.pallas_call(
        _per_chunk(inner, kinds, grp), name=name, grid=(nct // grp,),
        in_specs=[pl.BlockSpec((rows, 2 * GQ_W), lambda i: (i, C_GQK // (2 * GQ_W))),
                  pl.BlockSpec((rows, GV_W), lambda i: (i, C_GV // GV_W)),
                  pl.BlockSpec((rows, LANE), lambda i: (i, C_SB // LANE)),
                  pl.BlockSpec((LANE, GQ_W), lambda i: (0, 0)), pl.BlockSpec((1, GQ_W), lambda i: (0, 0))],
        out_specs=[pl.BlockSpec((rows, GV_W), lambda i: (i, 0)), pl.BlockSpec((rows, GQ_W), lambda i: (i, 0)),
                   pl.BlockSpec((rows, GQ_W), lambda i: (i, 0)), pl.BlockSpec((grp, 1, GQ_W), lambda i: (i, 0, 0))],
        out_shape=[jax.ShapeDtypeStruct((n, GV_W), F32), jax.ShapeDtypeStruct((n, GQ_W), F32),
                   jax.ShapeDtypeStruct((n, GQ_W), F32), jax.ShapeDtypeStruct((nct, 1, GQ_W), F32)],
        compiler_params=_cp(VMEM_BIG),
    )(projp, projp, projp, w2p, gb)


def _gla_scan_fwd(oi, qg, kd, gl, projp, *, bsz, nc_seq, name):
    t_seq = nc_seq * CHUNK
    oi = oi.reshape(bsz, t_seq, GV_W)
    qg, kd = qg.reshape(bsz, t_seq, GQ_W), kd.reshape(bsz, t_seq, GQ_W)
    gl = gl.reshape(bsz, nc_seq, 1, GQ_W)
    pj = projp.reshape(bsz, t_seq, PW)

    def body(oi_ref, qg_ref, kd_ref, gl_ref, v_ref, o_ref, hist_ref, st_ref):
        @pl.when(pl.program_id(0) == 0)
        def _():
            st_ref[...] = jnp.zeros_like(st_ref)

        for b in range(bsz):
            st = st_ref[b]
            hist_ref[b, 0] = st
            qgb = qg_ref[b]
            kdb = kd_ref[b]
            upd = jnp.zeros((GLA_DV, GQ_W), F32)
            for h in range(NH):
                sl = slice(h * GLA_DV, (h + 1) * GLA_DV)
                m = _head_mask(h)
                o_ref[b, :, sl] = oi_ref[b, :, sl] + _nt(jnp.where(m, qgb, 0.0), st)
                upd = upd + jnp.where(m, _tn(v_ref[b, :, sl], kdb), 0.0)
            st_ref[b] = gl_ref[b, 0] * st + upd

    outs = pl.pallas_call(
        body, name=name, grid=(nc_seq,),
        in_specs=[pl.BlockSpec((bsz, CHUNK, GV_W), lambda i: (0, i, 0)),
                  pl.BlockSpec((bsz, CHUNK, GQ_W), lambda i: (0, i, 0)),
                  pl.BlockSpec((bsz, CHUNK, GQ_W), lambda i: (0, i, 0)),
                  pl.BlockSpec((bsz, 1, 1, GQ_W), lambda i: (0, i, 0, 0)),
                  pl.BlockSpec((bsz, CHUNK, GV_W), lambda i: (0, i, C_GV // GV_W))],
        out_specs=[pl.BlockSpec((bsz, CHUNK, GV_W), lambda i: (0, i, 0)),
                   pl.BlockSpec((bsz, 1, GLA_DV, GQ_W), lambda i: (0, i, 0, 0))],
        out_shape=[jax.ShapeDtypeStruct((bsz, t_seq, GV_W), F32),
                   jax.ShapeDtypeStruct((bsz, nc_seq, GLA_DV, GQ_W), F32)],
        scratch_shapes=[pltpu.VMEM((bsz, GLA_DV, GQ_W), F32)],
        compiler_params=_cp(VMEM_BIG, ("arbitrary",)),
    )(oi, qg, kd, gl, pj)
    return outs[0].reshape(bsz * t_seq, GV_W), outs[1]


def _gla_scan_bwd(do, qg, kd, gl, projp, hist, *, bsz, nc_seq, name):
    t_seq = nc_seq * CHUNK
    do = do.reshape(bsz, t_seq, GV_W)
    qg, kd = qg.reshape(bsz, t_seq, GQ_W), kd.reshape(bsz, t_seq, GQ_W)
    gl = gl.reshape(bsz, nc_seq, 1, GQ_W)
    pj = projp.reshape(bsz, t_seq, PW)

    def body(do_ref, qg_ref, kd_ref, gl_ref, v_ref, hist_ref, dqg_ref, dkd_ref, dv_ref, dgl_ref, dst_ref):
        @pl.when(pl.program_id(0) == 0)
        def _():
            dst_ref[...] = jnp.zeros_like(dst_ref)

        for b in range(bsz):
            st = hist_ref[b, 0]
            dst = dst_ref[b]
            qgb = qg_ref[b]
            kdb = kd_ref[b]
            dqg = jnp.zeros((CHUNK, GQ_W), F32)
            dkd = jnp.zeros((CHUNK, GQ_W), F32)
            add = jnp.zeros((GLA_DV, GQ_W), F32)
            for h in range(NH):
                sl = slice(h * GLA_DV, (h + 1) * GLA_DV)
                m = _head_mask(h)
                doh = do_ref[b, :, sl]
                vh = v_ref[b, :, sl]
                dqg = dqg + jnp.where(m, _nn(doh, st), 0.0)
                dkd = dkd + jnp.where(m, _nn(vh, dst), 0.0)
                dv_ref[b, :, sl] = _nt(jnp.where(m, kdb, 0.0), dst)
                add = add + jnp.where(m, _tn(doh, qgb), 0.0)
            dqg_ref[b] = dqg
            dkd_ref[b] = dkd
            dgl_ref[b, 0] = jnp.sum(dst * st, axis=0, keepdims=True)
            dst_ref[b] = gl_ref[b, 0] * dst + add

    rev = lambda i: nc_seq - 1 - i
    outs = pl.pallas_call(
        body, name=name, grid=(nc_seq,),
        in_specs=[pl.BlockSpec((bsz, CHUNK, GV_W), lambda i: (0, rev(i), 0)),
                  pl.BlockSpec((bsz, CHUNK, GQ_W), lambda i: (0, rev(i), 0)),
                  pl.BlockSpec((bsz, CHUNK, GQ_W), lambda i: (0, rev(i), 0)),
                  pl.BlockSpec((bsz, 1, 1, GQ_W), lambda i: (0, rev(i), 0, 0)),
                  pl.BlockSpec((bsz, CHUNK, GV_W), lambda i: (0, rev(i), C_GV // GV_W)),
                  pl.BlockSpec((bsz, 1, GLA_DV, GQ_W), lambda i: (0, rev(i), 0, 0))],
        out_specs=[pl.BlockSpec((bsz, CHUNK, GQ_W), lambda i: (0, rev(i), 0)),
                   pl.BlockSpec((bsz, CHUNK, GQ_W), lambda i: (0, rev(i), 0)),
                   pl.BlockSpec((bsz, CHUNK, GV_W), lambda i: (0, rev(i), 0)),
                   pl.BlockSpec((bsz, 1, 1, GQ_W), lambda i: (0, rev(i), 0, 0))],
        out_shape=[jax.ShapeDtypeStruct((bsz, t_seq, GQ_W), F32), jax.ShapeDtypeStruct((bsz, t_seq, GQ_W), F32),
                   jax.ShapeDtypeStruct((bsz, t_seq, GV_W), F32), jax.ShapeDtypeStruct((bsz, nc_seq, 1, GQ_W), F32)],
        scratch_shapes=[pltpu.VMEM((bsz, GLA_DV, GQ_W), F32)],
        compiler_params=_cp(VMEM_BIG, ("arbitrary",)),
    )(do, qg, kd, gl, pj, hist)
    n = bsz * t_seq
    return (outs[0].reshape(n, GQ_W), outs[1].reshape(n, GQ_W), outs[2].reshape(n, GV_W),
            outs[3].reshape(bsz * nc_seq, 1, GQ_W))


def _gla_intra_bwd(projp, w2p, gb, do, dqg, dkd, dvi, dgl, *, nc_seq, name):
    n = projp.shape[0]
    nct = n // CHUNK
    scale = GLA_DK ** -0.5

    grp = _group(nc_seq)
    rows = grp * CHUNK

    def inner(gi, qk_ref, v_ref, sb_ref, w2_ref, gb_ref, do_ref, dqg_ref, dkd_ref, dvi_ref, dgl_ref,
              dqk_ref, dv_ref, dsb_ref, dw2_ref, dgb_ref):
        ci = (pl.program_id(0) * grp + gi) % nc_seq
        sb = sb_ref[...]
        w2 = w2_ref[...]
        graw, bc, valid = yield from _gla_gates(sb, w2, gb_ref[...], ci)
        bref = bc[MID:MID + 1, :]
        bl = bc[CHUNK - 1:CHUNK, :]
        q = qk_ref[:, 0:GQ_W] * scale
        k = qk_ref[:, GQ_W:2 * GQ_W]
        ex1 = jnp.exp(bc - bref)
        ex2 = jnp.exp(bref - bc)
        eb = jnp.exp(bc)
        ekd = jnp.exp(bl - bc)
        qi, ki = q * ex1, k * ex2
        r, c = _masks64()
        incl = r >= c
        upper = r <= c
        a_t, da, da_t = [], [], []
        for h in range(NH):
            sl = slice(h * GLA_DV, (h + 1) * GLA_DV)
            doh = do_ref[:, sl]
            vh = v_ref[:, sl]
            a_t.append(jnp.where(upper, _nt(jnp.where(_head_mask(h), ki, 0.0), qi), 0.0))
            da.append(jnp.where(incl, _nt(doh, vh), 0.0))
            da_t.append(jnp.where(upper, _nt(vh, doh), 0.0))
        yield
        dqi = jnp.zeros((CHUNK, GQ_W), F32)
        dki = jnp.zeros((CHUNK, GQ_W), F32)
        for h in range(NH):
            sl = slice(h * GLA_DV, (h + 1) * GLA_DV)
            m = _head_mask(h)
            dv_ref[:, sl] = _nn(a_t[h], do_ref[:, sl]) + dvi_ref[:, sl]
            dqi = dqi + jnp.where(m, _nn(da[h], ki), 0.0)
            dki = dki + jnp.where(m, _nn(da_t[h], qi), 0.0)
        yield
        dqg = dqg_ref[...]
        dkd = dkd_ref[...]
        dqk_ref[:, 0:GQ_W] = (dqi * ex1 + dqg * eb) * scale
        dqk_ref[:, GQ_W:2 * GQ_W] = dki * ex2 + dkd * ekd
        t_qi, t_ki, t_kd = dqi * qi, dki * ki, dkd * (k * ekd)
        db = t_qi - t_ki + dqg * (q * eb) - t_kd
        dbref = jnp.sum(t_ki - t_qi, axis=0, keepdims=True)
        dbl = jnp.sum(t_kd, axis=0, keepdims=True) + dgl_ref[0] * jnp.exp(bl)
        rows = _iota2((CHUNK, GQ_W), 0)
        db = db + jnp.where(rows == MID, dbref, 0.0) + jnp.where(rows == CHUNK - 1, dbl, 0.0)
        dg = _nn_hi(upper.astype(F32), db)
        yield
        dgraw = jnp.where(valid, dg * (1.0 / GLA_NORM) * _sigmoid(-graw), 0.0)
        dsb_ref[...] = _nt_hi(dgraw, w2)
        dw2 = _tn_hi(sb, dgraw)
        dgb = jnp.sum(dgraw, axis=0, keepdims=True)
        _accumulate(dw2_ref, dw2, gi)
        _accumulate(dgb_ref, dgb, gi)

    rq = pl.BlockSpec((rows, GQ_W), lambda i: (i, 0))
    rv = pl.BlockSpec((rows, GV_W), lambda i: (i, 0))
    kinds = ["row"] * 3 + ["whole"] * 2 + ["row"] * 4 + ["lead"] + ["row"] * 3 + ["whole"] * 2
    return pl.pallas_call(
        _per_chunk(inner, kinds, grp), name=name, grid=(nct // grp,),
        in_specs=[pl.BlockSpec((rows, 2 * GQ_W), lambda i: (i, C_GQK // (2 * GQ_W))),
                  pl.BlockSpec((rows, GV_W), lambda i: (i, C_GV // GV_W)),
                  pl.BlockSpec((rows, LANE), lambda i: (i, C_SB // LANE)),
                  pl.BlockSpec((LANE, GQ_W), lambda i: (0, 0)), pl.BlockSpec((1, GQ_W), lambda i: (0, 0)),
                  rv, rq, rq, rv, pl.BlockSpec((grp, 1, GQ_W), lambda i: (i, 0, 0))],
        out_specs=[pl.BlockSpec((rows, 2 * GQ_W), lambda i: (i, 0)), rv, pl.BlockSpec((rows, LANE), lambda i: (i, 0)),
                   pl.BlockSpec((LANE, GQ_W), lambda i: (0, 0)), pl.BlockSpec((1, GQ_W), lambda i: (0, 0))],
        out_shape=[jax.ShapeDtypeStruct((n, 2 * GQ_W), F32), jax.ShapeDtypeStruct((n, GV_W), F32),
                   jax.ShapeDtypeStruct((n, LANE), F32), jax.ShapeDtypeStruct((LANE, GQ_W), F32),
                   jax.ShapeDtypeStruct((1, GQ_W), F32)],
        compiler_params=_cp(VMEM_BIG, ("arbitrary",)),
    )(projp, projp, projp, w2p, gb, do, dqg, dkd, dvi, dgl)


SECTIONS = ((C_QKV, 1536), (C_DZ, 512), (C_GQK, 512), (C_GV, 512), (C_GR, 512), (C_SA, 128), (C_SB, 128))


def _inproj_bwd(secs, wp, h0, g1, dx1, *, tr, name):
    n, d = h0.shape

    def body(*refs):
        sec_refs = refs[:len(SECTIONS)]
        wp_ref, h0_ref, g_ref, dx1_ref, o_ref, dg_ref = refs[len(SECTIONS):]
        dh = None
        for s_ref, (off, wd) in zip(sec_refs, SECTIONS):
            part = _nt(s_ref[...], wp_ref[:, off:off + wd])
            dh = part if dh is None else dh + part
        dx, dg = _rms_bwd_math(h0_ref[...], g_ref[...], dh)
        o_ref[...] = dx1_ref[...] + dx

        @pl.when(pl.program_id(0) == 0)
        def _():
            dg_ref[...] = dg

        @pl.when(pl.program_id(0) > 0)
        def _():
            dg_ref[...] += dg

    row = pl.BlockSpec((tr, d), lambda i: (i, 0))
    vec = pl.BlockSpec((1, d), lambda i: (0, 0))
    return pl.pallas_call(
        body, name=name, grid=(n // tr,),
        in_specs=[pl.BlockSpec((tr, wd), lambda i: (i, 0)) for _, wd in SECTIONS]
        + [pl.BlockSpec((d, PW), lambda i: (0, 0)), row, vec, row],
        out_specs=[row, vec],
        out_shape=[jax.ShapeDtypeStruct((n, d), F32), jax.ShapeDtypeStruct((1, d), F32)],
        compiler_params=_cp(VMEM_BIG),
    )(*secs, wp, h0, g1, dx1)


def _adamw(w, g, m, v, *, name):
    r, c = w.shape
    tr = _tile(r, 256, 8) if r > 256 else r
    c1 = 1.0 - ADAM_B1 ** ADAM_STEP
    c2 = 1.0 - ADAM_B2 ** ADAM_STEP

    def body(w_ref, g_ref, m_ref, v_ref, d_ref, nm_ref, nv_ref):
        gv = g_ref[...]
        nm = ADAM_B1 * m_ref[...] + (1.0 - ADAM_B1) * gv
        nv = ADAM_B2 * v_ref[...] + (1.0 - ADAM_B2) * (gv * gv)
        d_ref[...] = -ADAM_LR * ((nm / c1) / (jnp.sqrt(nv / c2) + ADAM_EPS) + ADAM_WD * w_ref[...])
        nm_ref[...] = nm
        nv_ref[...] = nv

    blk = pl.BlockSpec((tr, c), lambda i: (i, 0))
    sds = jax.ShapeDtypeStruct((r, c), F32)
    return pl.pallas_call(
        body, name=name, grid=(r // tr,), in_specs=[blk] * 4, out_specs=[blk] * 3, out_shape=[sds] * 3,
        compiler_params=_cp(VMEM_BIG),
    )(w, g, m, v)


def _add2(a, b, *, out_dtype, name):
    lead, r, c = a.shape
    tr = _tile(r, 256, 16)

    def body(a_ref, b_ref, o_ref):
        o_ref[...] = (a_ref[...] + b_ref[...]).astype(o_ref.dtype)

    blk = pl.BlockSpec((1, tr, c), lambda s, i: (s, i, 0))
    return pl.pallas_call(
        body, name=name, grid=(lead, r // tr), in_specs=[blk, blk], out_specs=blk,
        out_shape=jax.ShapeDtypeStruct(a.shape, out_dtype), compiler_params=_cp(VMEM_BIG),
    )(a, b)


def _sum_chips(q, *, name):
    _, r, c = q.shape
    tr = _tile(r, 256, 16)

    def body(q_ref, o_ref):
        p = [q_ref[j].astype(F32) for j in range(N_CHIPS)]
        o_ref[...] = ((p[0] + p[1]) + p[2]) + p[3]

    return pl.pallas_call(
        body, name=name, grid=(r // tr,),
        in_specs=[pl.BlockSpec((N_CHIPS, tr, c), lambda i: (0, i, 0))],
        out_specs=pl.BlockSpec((tr, c), lambda i: (i, 0)),
        out_shape=jax.ShapeDtypeStruct((r, c), F32), compiler_params=_cp(VMEM_BIG),
    )(q)


ANY = pl.BlockSpec(memory_space=pl.ANY)
VM = pl.BlockSpec(memory_space=pltpu.VMEM)
CAST_ROWS = 64


def _place():
    return lax.axis_index("x"), lax.axis_index("y"), lax.axis_index("c")


def _other_chips(x, y):
    return [(1 - x, y, 2 * (1 - x) + y), (x, 1 - y, 2 * x + 1 - y), (1 - x, 1 - y, 2 * (1 - x) + 1 - y)]


def _gather_weights(w_in, w_out, w_up, w_down, meta, conv, gw2):
    big = (w_in, w_out, w_up, w_down)
    small = (meta, conv, gw2)
    n_arr = len(big) + len(small)

    def body(*refs):
        ins = refs[:n_arr]
        outs = refs[n_arr:2 * n_arr]
        stage = refs[2 * n_arr:2 * n_arr + len(big)]
        send_sems, recv_sems, local_sems = refs[2 * n_arr + len(big):]
        x, y, c = _place()
        me = 2 * x + y
        srcs = []
        for k in range(n_arr):
            if k < len(big):
                src, dst = ins[k], stage[k]

                def cast_rows(i, carry, src=src, dst=dst):
                    rows = pl.ds(pl.multiple_of(i * CAST_ROWS, CAST_ROWS), CAST_ROWS)
                    dst[rows, :] = src[rows, :].astype(BF16)
                    return carry

                lax.fori_loop(0, src.shape[0] // CAST_ROWS, cast_rows, 0)
                srcs.append(stage[k])
            else:
                srcs.append(ins[k])
        local = [pltpu.make_async_copy(srcs[k], outs[k].at[me], local_sems.at[k]) for k in range(n_arr)]
        for cp in local:
            cp.start()
        def pieces(k):
            if k < len(big):
                return _row_chunks(srcs[k].shape[0], ICI_SPLIT)[0]
            return [(0, pl.ds(0, srcs[k].shape[0]))]

        sends = []
        for k in range(n_arr):
            for d, (px, py, _) in enumerate(_other_chips(x, y)):
                for s, rows in pieces(k):
                    cp = pltpu.make_async_remote_copy(
                        src_ref=srcs[k].at[rows, :], dst_ref=outs[k].at[me, rows, :], send_sem=send_sems.at[k, d, s],
                        recv_sem=recv_sems.at[k, d, s], device_id=(px, py, c), device_id_type=MESH)
                    cp.start()
                    sends.append(cp)
        for k in range(n_arr):
            for d, (px, py, pj) in enumerate(_other_chips(x, y)):
                for s, rows in pieces(k):
                    pltpu.make_async_remote_copy(
                        src_ref=srcs[k].at[rows, :], dst_ref=outs[k].at[pj, rows, :], send_sem=send_sems.at[k, d, s],
                        recv_sem=recv_sems.at[k, d, s], device_id=(px, py, c), device_id_type=MESH).wait_recv()
        for cp in sends:
            cp.wait_send()
        for cp in local:
            cp.wait()

    out_shape = [jax.ShapeDtypeStruct((N_CHIPS,) + a.shape, BF16) for a in big]
    out_shape += [jax.ShapeDtypeStruct((N_CHIPS,) + a.shape, F32) for a in small]
    sem = pltpu.SemaphoreType.DMA((n_arr, 3, ICI_SPLIT))
    return pl.pallas_call(
        body, name="gather_weights", in_specs=[VM] * n_arr, out_specs=[ANY] * n_arr, out_shape=out_shape,
        scratch_shapes=[pltpu.VMEM(a.shape, BF16) for a in big] + [sem, sem, pltpu.SemaphoreType.DMA((n_arr,))],
        compiler_params=_cp(VMEM_BIG),
    )(*big, *small)


def _row_chunks(rows, n_split):
    size = rows // n_split
    assert size * n_split == rows and size % 16 == 0, (rows, n_split)
    return [(s, pl.ds(s * size, size)) for s in range(n_split)], size


D2D_SPLIT = 8
ICI_SPLIT = 2


def _sibling_halves(grads):
    n_arr = len(grads)

    def body(*refs):
        ins = refs[:n_arr]
        mine = refs[n_arr:2 * n_arr]
        theirs = refs[2 * n_arr:3 * n_arr]
        send_sems, recv_sems, local_sems = refs[3 * n_arr:]
        x, y, c = _place()
        copies = []
        for k in range(n_arr):
            half = ins[k].shape[1] // 2
            chunks, size = _row_chunks(half, D2D_SPLIT)
            for s, dst_rows in chunks:
                keep = pltpu.make_async_copy(ins[k].at[:, pl.ds(c * half + s * size, size), :],
                                             mine[k].at[:, dst_rows, :], local_sems.at[k, s])
                keep.start()
                give = pltpu.make_async_remote_copy(
                    src_ref=ins[k].at[:, pl.ds((1 - c) * half + s * size, size), :], dst_ref=theirs[k].at[:, dst_rows, :],
                    send_sem=send_sems.at[k, s], recv_sem=recv_sems.at[k, s], device_id=(x, y, 1 - c),
                    device_id_type=MESH)
                give.start()
                copies.append((keep, give))
        for keep, give in copies:
            give.wait()
            keep.wait()

    halves = [jax.ShapeDtypeStruct((g.shape[0], g.shape[1] // 2, g.shape[2]), F32) for g in grads]
    sem = pltpu.SemaphoreType.DMA((n_arr, D2D_SPLIT))
    outs = pl.pallas_call(
        body, name="sibling_halves", in_specs=[ANY] * n_arr, out_specs=[ANY] * (2 * n_arr), out_shape=halves + halves,
        scratch_shapes=[sem, sem, sem],
    )(*grads)
    return outs[:n_arr], outs[n_arr:]


def _chip_exchange(parts):
    n_arr = len(parts)

    def body(*refs):
        ins = refs[:n_arr]
        outs = refs[n_arr:2 * n_arr]
        send_sems, recv_sems, local_sems = refs[2 * n_arr:]
        x, y, c = _place()
        me = 2 * x + y
        local, sends = [], []
        for k in range(n_arr):
            cp = pltpu.make_async_copy(ins[k].at[me], outs[k].at[me], local_sems.at[k])
            cp.start()
            local.append(cp)
            chunks, _ = _row_chunks(ins[k].shape[1], ICI_SPLIT)
            for d, (px, py, pj) in enumerate(_other_chips(x, y)):
                for s, rows in chunks:
                    cp = pltpu.make_async_remote_copy(
                        src_ref=ins[k].at[pj, rows, :], dst_ref=outs[k].at[me, rows, :], send_sem=send_sems.at[k, d, s],
                        recv_sem=recv_sems.at[k, d, s], device_id=(px, py, c), device_id_type=MESH)
                    cp.start()
                    sends.append(cp)
        for k in range(n_arr):
            chunks, _ = _row_chunks(ins[k].shape[1], ICI_SPLIT)
            for d, (px, py, pj) in enumerate(_other_chips(x, y)):
                for s, rows in chunks:
                    pltpu.make_async_remote_copy(
                        src_ref=ins[k].at[pj, rows, :], dst_ref=outs[k].at[pj, rows, :], send_sem=send_sems.at[k, d, s],
                        recv_sem=recv_sems.at[k, d, s], device_id=(px, py, c), device_id_type=MESH).wait_recv()
        for cp in sends:
            cp.wait_send()
        for cp in local:
            cp.wait()

    sem = pltpu.SemaphoreType.DMA((n_arr, 3, ICI_SPLIT))
    return pl.pallas_call(
        body, name="chip_exchange", in_specs=[ANY] * n_arr, out_specs=[ANY] * n_arr,
        out_shape=[jax.ShapeDtypeStruct(p.shape, p.dtype) for p in parts],
        scratch_shapes=[sem, sem, pltpu.SemaphoreType.DMA((n_arr,))],
    )(*parts)


def _sibling_join(halves):
    n_arr = len(halves)

    def body(*refs):
        ins = refs[:n_arr]
        outs = refs[n_arr:2 * n_arr]
        send_sems, recv_sems, local_sems = refs[2 * n_arr:]
        x, y, c = _place()
        copies = []
        for k in range(n_arr):
            chunks, _ = _row_chunks(ins[k].shape[0], D2D_SPLIT)
            for s, rows in chunks:
                keep = pltpu.make_async_copy(ins[k].at[rows, :], outs[k].at[c, rows, :], local_sems.at[k, s])
                keep.start()
                give = pltpu.make_async_remote_copy(
                    src_ref=ins[k].at[rows, :], dst_ref=outs[k].at[c, rows, :], send_sem=send_sems.at[k, s],
                    recv_sem=recv_sems.at[k, s], device_id=(x, y, 1 - c), device_id_type=MESH)
                give.start()
                copies.append((k, s, rows, keep, give))
        for k, s, rows, keep, give in copies:
            pltpu.make_async_remote_copy(
                src_ref=ins[k].at[rows, :], dst_ref=outs[k].at[1 - c, rows, :], send_sem=send_sems.at[k, s],
                recv_sem=recv_sems.at[k, s], device_id=(x, y, 1 - c), device_id_type=MESH).wait_recv()
            give.wait_send()
            keep.wait()

    sem = pltpu.SemaphoreType.DMA((n_arr, D2D_SPLIT))
    return pl.pallas_call(
        body, name="sibling_join", in_specs=[ANY] * n_arr, out_specs=[ANY] * n_arr,
        out_shape=[jax.ShapeDtypeStruct((2,) + h.shape, F32) for h in halves],
        scratch_shapes=[sem, sem, sem],
    )(*halves)


PACK_ROWS = 48


def _small_allreduce(pack):
    masks = [(dx, dy, dc) for dx in (0, 1) for dy in (0, 1) for dc in (0, 1)][1:]

    def body(p_ref, o_ref, buf, send_sems, recv_sems):
        x, y, c = _place()
        me = 4 * x + 2 * y + c
        buf[me] = p_ref[...]
        sends = []
        for k, (dx, dy, dc) in enumerate(masks):
            peer = (1 - x if dx else x, 1 - y if dy else y, 1 - c if dc else c)
            cp = pltpu.make_async_remote_copy(
                src_ref=p_ref, dst_ref=buf.at[me], send_sem=send_sems.at[k], recv_sem=recv_sems.at[k],
                device_id=peer, device_id_type=MESH)
            cp.start()
            sends.append(cp)
        for k, (dx, dy, dc) in enumerate(masks):
            peer = (1 - x if dx else x, 1 - y if dy else y, 1 - c if dc else c)
            pj = 4 * peer[0] + 2 * peer[1] + peer[2]
            pltpu.make_async_remote_copy(
                src_ref=p_ref, dst_ref=buf.at[pj], send_sem=send_sems.at[k], recv_sem=recv_sems.at[k],
                device_id=peer, device_id_type=MESH).wait_recv()
        for cp in sends:
            cp.wait_send()
        tot = buf[0]
        for k in range(1, 8):
            tot = tot + buf[k]
        o_ref[...] = tot
        o_ref[0:N_META, :] = tot[0:N_META] + tot[N_META:2 * N_META]

    return pl.pallas_call(
        body, name="small_allreduce", in_specs=[VM], out_specs=VM,
        out_shape=jax.ShapeDtypeStruct((PACK_ROWS, D_MODEL), F32),
        scratch_shapes=[pltpu.VMEM((8, PACK_ROWS, D_MODEL), F32), pltpu.SemaphoreType.DMA((7,)),
                        pltpu.SemaphoreType.DMA((7,))],
    )(pack)


def _pad_lanes(vec, offset):
    k = vec.shape[1]
    return jnp.concatenate([jnp.zeros((1, offset), F32), vec, jnp.zeros((1, LANE - offset - k), F32)], axis=1)


def _local_step(x, tgt, meta, norm1_g, wp, conv_w, a_log, dt_bias, dn_norm_g, gla_w2, gla_b, gla_norm_g,
                w_out, norm2_g, w_up, w_down, final_norm_g):
    bsz, s_len, d = x.shape
    t_seq = s_len + CHUNK
    nc_seq = t_seq // CHUNK
    n = bsz * t_seq
    tr = _tile(t_seq, 832)
    tt = _tile(t_seq, 416)

    lead = jnp.concatenate([jnp.zeros((N_PAD, d), F32), meta], axis=0)
    h0 = jnp.concatenate([jnp.broadcast_to(lead[None], (bsz, CHUNK, d)), x], axis=1).reshape(n, d)
    tgt_p = jnp.concatenate([jnp.zeros((bsz, CHUNK, d), F32), tgt], axis=1).reshape(n, d)
    alog_row = _pad_lanes(a_log, 4)
    dtb_row = _pad_lanes(dt_bias, 4)
    w2p = jnp.concatenate([gla_w2, jnp.zeros((LANE - GLA_RANK, GQ_W), F32)], axis=0)

    h = _rms_fwd(h0, norm1_g, tr=tr, name="norm1")
    (projp,) = _mm(h, wp, "nn", tm=tr, tn=1280, tk=d, out_dtypes=(F32,), name="in_proj")
    qn, kn, v = _dnprep_fwd(projp, conv_w, bsz=bsz, t_seq=t_seq, tt=tt, name="dn_prep")
    u, w, qg, kd, pmat, tmat, gl = _dn_intra_fwd(qn, kn, v, projp, alog_row, dtb_row, nc_seq=nc_seq, name="dn_intra")
    o_dn, vn, hist = _dn_scan_fwd(u, w, qg, kd, pmat, gl, bsz=bsz, nc_seq=nc_seq, name="dn_scan")
    oi, gqg, gkd, ggl = _gla_intra_fwd(projp, w2p, gla_b, nc_seq=nc_seq, name="gla_intra")
    o_gla, ghist = _gla_scan_fwd(oi, gqg, gkd, ggl, projp, bsz=bsz, nc_seq=nc_seq, name="gla_scan")
    mix = _gnorm_fwd(o_dn, o_gla, projp, dn_norm_g, gla_norm_g, tr=tr, name="gated_norm")
    (x1,) = _mm(mix, w_out, "nn", tm=tr, tn=d, tk=d, out_dtypes=(F32,), extras=(h0,),
                epilogue=lambda acc, res: (res + acc,), name="out_proj")
    h2 = _rms_fwd(x1, norm2_g, tr=tr, name="norm2")

    def act_epilogue(acc):
        r = jnp.maximum(acc, 0.0)
        return acc, r * r

    up, act = _mm(h2, w_up, "nn", tm=tr, tn=1024, tk=d, out_dtypes=(BF16, BF16), epilogue=act_epilogue, name="mlp_up")
    (x2,) = _mm(act, w_down, "nn", tm=tr, tn=d, tk=1024, out_dtypes=(F32,), extras=(x1,),
                epilogue=lambda acc, res: (res + acc,), name="mlp_down")
    dx2, d_final_g, loss_tile = _final_loss(x2, final_norm_g, tgt_p, t_seq=t_seq, tr=tr, name="final_loss")

    (dup,) = _mm(dx2, w_down, "nt", tm=tr, tn=1024, tk=d, out_dtypes=(BF16,), extras=(up,),
                 epilogue=lambda acc, upv: (acc * (2.0 * jnp.maximum(upv.astype(F32), 0.0)),), name="mlp_down_bwd")
    (d_w_down,) = _mm(act, dx2, "tn", tm=1024, tn=d, tk=tr, out_dtypes=(F32,), name="w_down_grad")
    (d_w_up,) = _mm(h2, dup, "tn", tm=d, tn=1024, tk=tr, out_dtypes=(F32,), name="w_up_grad")
    (dh2,) = _mm(dup, w_up, "nt", tm=tr, tn=d, tk=1024, out_dtypes=(F32,), name="mlp_up_bwd")
    dx1, d_norm2_g = _rms_bwd_add(x1, norm2_g, dh2, dx2, tr=tr, name="norm2_bwd")

    (dmix,) = _mm(dx1, w_out, "nt", tm=tr, tn=d, tk=d, out_dtypes=(F32,), name="out_proj_bwd")
    (d_w_out,) = _mm(mix, dx1, "tn", tm=d, tn=d, tk=tr, out_dtypes=(F32,), name="w_out_grad")
    do_dn, ddz, do_gla, dgr, d_dn_norm_g, d_gla_norm_g = _gnorm_bwd(
        dmix, o_dn, o_gla, projp, dn_norm_g, gla_norm_g, tr=tr, name="gated_norm_bwd")
    du, dw, dqg, dkd, dpm, dgl = _dn_scan_bwd(do_dn, w, qg, kd, vn, pmat, gl, hist, bsz=bsz, nc_seq=nc_seq,
                                               name="dn_scan_bwd")
    dqn, dkn, dv, dsa, d_alog, d_dtb = _dn_intra_bwd(qn, kn, v, projp, alog_row, dtb_row, u, w, tmat,
                                                     du, dw, dqg, dkd, dpm, dgl, nc_seq=nc_seq, name="dn_intra_bwd")
    dz, d_conv_w = _dnprep_bwd_a(projp, conv_w, dqn, dkn, dv, bsz=bsz, t_seq=t_seq, tt=tt, name="dn_prep_bwd")
    dcin = _dnprep_bwd_b(dz, conv_w, bsz=bsz, t_seq=t_seq, tt=tt, name="conv_bwd")
    gdqg, gdkd, gdvi, gdgl = _gla_scan_bwd(do_gla, gqg, gkd, ggl, projp, ghist, bsz=bsz, nc_seq=nc_seq,
                                            name="gla_scan_bwd")
    dgqk, dgv, dsb, d_w2p, d_gla_b = _gla_intra_bwd(projp, w2p, gla_b, do_gla, gdqg, gdkd, gdvi, gdgl,
                                                    nc_seq=nc_seq, name="gla_intra_bwd")

    secs = (dcin, ddz, dgqk, dgv, dgr, dsa, dsb)
    d_wp_secs = []
    for idx, (sec, (_, wd)) in enumerate(zip(secs, SECTIONS)):
        (g_sec,) = _mm(h, sec, "tn", tm=d, tn=min(wd, 512), tk=tr, out_dtypes=(F32,), name=f"w_in_grad_{idx}")
        d_wp_secs.append(g_sec)
    dh0, d_norm1_g = _inproj_bwd(secs, wp, h0, norm1_g, dx1, tr=tt, name="in_proj_bwd")
    dh0 = dh0.reshape(bsz, t_seq, d)
    grad_x = dh0[:, CHUNK:]
    d_meta_rows = dh0[:, N_PAD:CHUNK].reshape(bsz * N_META, d)

    g_qkv, g_dz, g_gqk, g_gv, g_gr, g_sa, g_sb = d_wp_secs
    d_w_in = jnp.concatenate([g_qkv, g_dz, g_sa[:, 0:8], g_gqk, g_gv, g_gr, g_sb[:, 0:GLA_RANK]], axis=1)
    grads = dict(w_in=d_w_in, w_out=d_w_out, w_up=d_w_up, w_down=d_w_down, meta_rows=d_meta_rows,
                 norm1_g=d_norm1_g, conv_w=d_conv_w, a_log_tile=d_alog, dt_bias_tile=d_dtb, dn_norm_g=d_dn_norm_g,
                 gla_w2=d_w2p[0:GLA_RANK], gla_b=d_gla_b, gla_norm_g=d_gla_norm_g, norm2_g=d_norm2_g,
                 final_norm_g=d_final_g, loss_tile=loss_tile)
    return grad_x, grads


def _pad_layout(w_full):
    z = lambda k: jnp.zeros((w_full.shape[0], k), w_full.dtype)
    return jnp.concatenate([w_full[:, 0:2048], w_full[:, 2056:3592], w_full[:, 2048:2056], z(LANE - 8),
                            w_full[:, 3592:3608], z(LANE - GLA_RANK)], axis=1)


def _pack_small(g, bsz):
    assert bsz * N_META == 32
    row = jnp.concatenate([g["a_log_tile"], g["dt_bias_tile"], g["dn_norm_g"], g["gla_norm_g"], g["gla_b"],
                           g["loss_tile"], jnp.zeros((1, LANE), F32)], axis=1)
    return jnp.concatenate([g["meta_rows"], g["norm1_g"], g["conv_w"].reshape(6, D_MODEL), row,
                            g["gla_w2"].reshape(4, D_MODEL), g["norm2_g"], g["final_norm_g"],
                            jnp.zeros((2, D_MODEL), F32)], axis=0)


def kernel(x, meta_tokens, norm1_g, w_in, conv_w, a_log, dt_bias, dn_norm_g, gla_w2, gla_b, gla_norm_g, w_out, norm2_g, w_up, w_down, final_norm_g, loss_target, m_meta_tokens, m_norm1_g, m_w_in, m_conv_w, m_a_log, m_dt_bias, m_dn_norm_g, m_gla_w2, m_gla_b, m_gla_norm_g, m_w_out, m_norm2_g, m_w_up, m_w_down, m_final_norm_g, v_meta_tokens, v_norm1_g, v_w_in, v_conv_w, v_a_log, v_dt_bias, v_dn_norm_g, v_gla_w2, v_gla_b, v_gla_norm_g, v_w_out, v_norm2_g, v_w_up, v_w_down, v_final_norm_g):
    bsz = x.shape[0]
    chip = 2 * lax.axis_index("x") + lax.axis_index("y")

    shard_w = IN_WIDTH // N_CHIPS
    lane_pad = lambda a, wd: jnp.pad(a, ((0, 0), (0, wd - a.shape[1])))
    g_in, g_out, g_up, g_down, g_meta, g_conv, g_w2 = _gather_weights(
        lane_pad(w_in[0], D_MODEL), w_out[0], w_up[0], w_down[0], meta_tokens, conv_w[0], lane_pad(gla_w2[0], LANE))
    wp = _pad_layout(g_in[:, :, 0:shard_w].transpose(1, 0, 2).reshape(D_MODEL, IN_WIDTH))
    w_out_f = g_out.reshape(D_MODEL, D_MODEL)
    w_up_f = g_up.transpose(1, 0, 2).reshape(D_MODEL, D_FF)
    w_down_f = g_down.reshape(D_FF, D_MODEL)
    meta_f = g_meta.transpose(1, 0, 2).reshape(N_META, D_MODEL)
    conv_f = g_conv.transpose(1, 0, 2).reshape(4, QKV_W)
    w2_f = g_w2[:, :, 0:GQ_W // N_CHIPS].transpose(1, 0, 2).reshape(GLA_RANK, GQ_W)

    grad_x, g = _local_step(x, loss_target, meta_f, norm1_g, wp, conv_f, a_log, dt_bias, dn_norm_g, w2_f, gla_b,
                            gla_norm_g, w_out_f, norm2_g, w_up_f, w_down_f, final_norm_g.reshape(1, D_MODEL))

    shard_major = [
        jnp.pad(g["w_in"].reshape(D_MODEL, N_CHIPS, shard_w).transpose(1, 0, 2),
                ((0, 0), (0, 0), (0, D_MODEL - shard_w))),
        g["w_out"].reshape(N_CHIPS, D_MODEL // N_CHIPS, D_MODEL),
        g["w_up"].reshape(D_MODEL, N_CHIPS, D_FF // N_CHIPS).transpose(1, 0, 2),
        g["w_down"].reshape(N_CHIPS, D_FF // N_CHIPS, D_MODEL),
    ]
    mine, theirs = _sibling_halves(shard_major)
    pair = [_add2(a, b, out_dtype=BF16, name=f"pair_sum_{k}") for k, (a, b) in enumerate(zip(mine, theirs))]
    parts = _chip_exchange(pair)
    halves = [_sum_chips(q, name=f"chip_sum_{k}") for k, q in enumerate(parts)]
    joined = _sibling_join(halves)
    gw_in, gw_out, gw_up, gw_down = [j.reshape(2 * j.shape[1], j.shape[2]) for j in joined]
    gw_in = gw_in[:, 0:shard_w]

    red = _small_allreduce(_pack_small(g, bsz))
    g_meta_full = red[0:N_META]
    g_norm1 = red[32:33]
    g_conv_full = red[33:39].reshape(4, QKV_W)
    srow = red[39:40]
    g_alog, g_dtb = srow[:, 4:8], srow[:, LANE + 4:LANE + 8]
    g_dn_norm, g_gla_norm = srow[:, 2 * LANE:3 * LANE], srow[:, 3 * LANE:4 * LANE]
    g_gla_b = srow[:, 4 * LANE:6 * LANE]
    loss = srow[0, 6 * LANE]
    g_w2_full = red[40:44].reshape(GLA_RANK, GQ_W)
    g_norm2 = red[44:45]
    g_final = red[45:46]
    g_meta_sh = lax.dynamic_slice_in_dim(g_meta_full, chip * (D_MODEL // N_CHIPS), D_MODEL // N_CHIPS, axis=1)
    g_conv_sh = lax.dynamic_slice_in_dim(g_conv_full, chip * (QKV_W // N_CHIPS), QKV_W // N_CHIPS, axis=1)
    g_w2_sh = lax.dynamic_slice_in_dim(g_w2_full, chip * (GQ_W // N_CHIPS), GQ_W // N_CHIPS, axis=1)

    names = ["meta_tokens", "norm1_g", "w_in", "conv_w", "a_log", "dt_bias", "dn_norm_g", "gla_w2", "gla_b",
             "gla_norm_g", "w_out", "norm2_g", "w_up", "w_down", "final_norm_g"]
    weights = dict(meta_tokens=meta_tokens, norm1_g=norm1_g, w_in=w_in, conv_w=conv_w, a_log=a_log, dt_bias=dt_bias,
                   dn_norm_g=dn_norm_g, gla_w2=gla_w2, gla_b=gla_b, gla_norm_g=gla_norm_g, w_out=w_out,
                   norm2_g=norm2_g, w_up=w_up, w_down=w_down, final_norm_g=final_norm_g)
    ms = dict(meta_tokens=m_meta_tokens, norm1_g=m_norm1_g, w_in=m_w_in, conv_w=m_conv_w, a_log=m_a_log,
              dt_bias=m_dt_bias, dn_norm_g=m_dn_norm_g, gla_w2=m_gla_w2, gla_b=m_gla_b, gla_norm_g=m_gla_norm_g,
              w_out=m_w_out, norm2_g=m_norm2_g, w_up=m_w_up, w_down=m_w_down, final_norm_g=m_final_norm_g)
    vs = dict(meta_tokens=v_meta_tokens, norm1_g=v_norm1_g, w_in=v_w_in, conv_w=v_conv_w, a_log=v_a_log,
              dt_bias=v_dt_bias, dn_norm_g=v_dn_norm_g, gla_w2=v_gla_w2, gla_b=v_gla_b, gla_norm_g=v_gla_norm_g,
              w_out=v_w_out, norm2_g=v_norm2_g, w_up=v_w_up, w_down=v_w_down, final_norm_g=v_final_norm_g)
    grads2d = dict(meta_tokens=g_meta_sh, norm1_g=g_norm1, w_in=gw_in, conv_w=g_conv_sh, a_log=g_alog, dt_bias=g_dtb,
                   dn_norm_g=g_dn_norm, gla_w2=g_w2_sh, gla_b=g_gla_b, gla_norm_g=g_gla_norm, w_out=gw_out,
                   norm2_g=g_norm2, w_up=gw_up, w_down=gw_down, final_norm_g=g_final)
    out_g, out_d, out_m, out_v = [], [], [], []
    for nm in names:
        shape = weights[nm].shape
        g2 = grads2d[nm]
        as2d = lambda a: a.reshape(g2.shape)
        dlt, nm_, nv_ = _adamw(as2d(weights[nm]), g2, as2d(ms[nm]), as2d(vs[nm]), name=f"adamw_{nm}")
        out_g.append(g2.reshape(shape))
        out_d.append(dlt.reshape(shape))
        out_m.append(nm_.reshape(shape))
        out_v.append(nv_.reshape(shape))
    return (loss, grad_x, *out_g, *out_d, *out_m, *out_v)
```

```python
import functools

import jax
import jax.numpy as jnp
import numpy as np
from jax import lax
from jax.experimental import pallas as pl
from jax.experimental.pallas import tpu as pltpu

F32 = jnp.float32
BF16 = jnp.bfloat16
HI = lax.Precision.HIGHEST
MESH = pl.DeviceIdType.MESH

D_MODEL = 1024
N_META = 16
CHUNK = 64
N_PAD = CHUNK - N_META
NH = 4
DN_D = 128
GLA_DK = 64
GLA_DV = 128
GLA_RANK = 16
D_FF = 4 * D_MODEL
EPS = 1e-6
IN_WIDTH = 3608
C_QKV, C_DZ, C_GQK, C_GV, C_GR, C_SA, C_SB, PW = 0, 1536, 2048, 2560, 3072, 3584, 3712, 3840
LANE = 128
N_CHIPS = 4

ADAM_LR, ADAM_B1, ADAM_B2, ADAM_EPS, ADAM_WD, ADAM_STEP = 0.001, 0.9, 0.999, 1e-08, 0.01, 10

VMEM_BIG = 56 * 1024 * 1024


def _cp(vmem=None, sem=None):
    kw = {}
    if vmem is not None:
        kw["vmem_limit_bytes"] = vmem
    if sem is not None:
        kw["dimension_semantics"] = sem
    return pltpu.CompilerParams(**kw)


def _tile(n, target, mult=16):
    best = None
    for t in range(mult, min(n, target) + 1, mult):
        if n % t == 0:
            best = t
    assert best is not None, (n, target)
    return best


def _dot(a, b, dims, prec=None):
    return lax.dot_general(a, b, (dims, ((), ())), preferred_element_type=F32, precision=prec)


def _nn(a, b):
    return _dot(a.astype(BF16), b.astype(BF16), ((1,), (0,)))


def _nt(a, b):
    return _dot(a.astype(BF16), b.astype(BF16), ((1,), (1,)))


def _tn(a, b):
    return _dot(a.astype(BF16), b.astype(BF16), ((0,), (0,)))


def _nn_hi(a, b):
    return _dot(a, b, ((1,), (0,)), HI)


def _nt_hi(a, b):
    return _dot(a, b, ((1,), (1,)), HI)


def _tn_hi(a, b):
    return _dot(a, b, ((0,), (0,)), HI)


def _sigmoid(x):
    return 1.0 / (1.0 + jnp.exp(-x))


def _softplus(x):
    return jnp.maximum(x, 0.0) + jnp.log(1.0 + jnp.exp(-jnp.abs(x)))


def _logsigmoid(x):
    return -_softplus(-x)


def _iota2(shape, dim):
    return lax.broadcasted_iota(jnp.int32, shape, dim)


def _mm(a, b, mode, *, tm, tn, tk, out_dtypes, extras=(), epilogue=None, name, vmem=VMEM_BIG):
    if mode == "tn":
        K, M = a.shape
    else:
        M, K = a.shape
    N = b.shape[0] if mode == "nt" else b.shape[1]
    assert M % tm == 0 and N % tn == 0 and K % tk == 0, (name, M, N, K, tm, tn, tk)
    nk = K // tk
    n_ex, n_out = len(extras), len(out_dtypes)
    if mode == "tn":
        a_spec = pl.BlockSpec((tk, tm), lambda i, j, k: (k, i))
    else:
        a_spec = pl.BlockSpec((tm, tk), lambda i, j, k: (i, k))
    if mode == "nt":
        b_spec = pl.BlockSpec((tn, tk), lambda i, j, k: (j, k))
    else:
        b_spec = pl.BlockSpec((tk, tn), lambda i, j, k: (k, j))
    mn_spec = pl.BlockSpec((tm, tn), lambda i, j, k: (i, j))
    dims = {"nn": ((1,), (0,)), "nt": ((1,), (1,)), "tn": ((0,), (0,))}[mode]

    def body(*refs):
        a_ref, b_ref = refs[0], refs[1]
        ex_refs = refs[2:2 + n_ex]
        out_refs = refs[2 + n_ex:2 + n_ex + n_out]
        acc_ref = refs[2 + n_ex + n_out]
        k = pl.program_id(2)
        part = _dot(a_ref[...].astype(BF16), b_ref[...].astype(BF16), dims)

        @pl.when(k == 0)
        def _():
            acc_ref[...] = part

        @pl.when(k > 0)
        def _():
            acc_ref[...] += part

        @pl.when(k == nk - 1)
        def _():
            acc = acc_ref[...]
            res = (acc,) if epilogue is None else epilogue(acc, *[e[...] for e in ex_refs])
            for o_ref, r in zip(out_refs, res):
                o_ref[...] = r.astype(o_ref.dtype)

    outs = pl.pallas_call(
        body, name=name, grid=(M // tm, N // tn, nk),
        in_specs=[a_spec, b_spec] + [mn_spec] * n_ex,
        out_specs=[mn_spec] * n_out,
        out_shape=[jax.ShapeDtypeStruct((M, N), dt) for dt in out_dtypes],
        scratch_shapes=[pltpu.VMEM((tm, tn), F32)],
        compiler_params=_cp(vmem, ("parallel", "parallel", "arbitrary")),
    )(a, b, *extras)
    return tuple(outs)


def _rms_fwd(x, g, *, tr, name):
    n, d = x.shape

    def body(x_ref, g_ref, o_ref):
        xv = x_ref[...]
        r = lax.rsqrt(jnp.mean(xv * xv, axis=-1, keepdims=True) + EPS)
        o_ref[...] = (xv * r * g_ref[...]).astype(o_ref.dtype)

    return pl.pallas_call(
        body, name=name, grid=(n // tr,),
        in_specs=[pl.BlockSpec((tr, d), lambda i: (i, 0)), pl.BlockSpec((1, d), lambda i: (0, 0))],
        out_specs=pl.BlockSpec((tr, d), lambda i: (i, 0)),
        out_shape=jax.ShapeDtypeStruct((n, d), BF16),
        compiler_params=_cp(VMEM_BIG),
    )(x, g)


def _rms_bwd_math(xv, g, dy):
    r = lax.rsqrt(jnp.mean(xv * xv, axis=-1, keepdims=True) + EPS)
    xh = xv * r
    gdy = dy * g
    dx = r * (gdy - xh * jnp.mean(xh * gdy, axis=-1, keepdims=True))
    return dx, jnp.sum(dy * xh, axis=0, keepdims=True)


def _rms_bwd_add(x, g, dy, res, *, tr, name):
    n, d = x.shape

    def body(x_ref, g_ref, dy_ref, res_ref, o_ref, dg_ref):
        dx, dg = _rms_bwd_math(x_ref[...], g_ref[...], dy_ref[...])
        o_ref[...] = res_ref[...] + dx

        @pl.when(pl.program_id(0) == 0)
        def _():
            dg_ref[...] = dg

        @pl.when(pl.program_id(0) > 0)
        def _():
            dg_ref[...] += dg

    row = pl.BlockSpec((tr, d), lambda i: (i, 0))
    vec = pl.BlockSpec((1, d), lambda i: (0, 0))
    return pl.pallas_call(
        body, name=name, grid=(n // tr,),
        in_specs=[row, vec, row, row], out_specs=[row, vec],
        out_shape=[jax.ShapeDtypeStruct((n, d), F32), jax.ShapeDtypeStruct((1, d), F32)],
        compiler_params=_cp(VMEM_BIG),
    )(x, g, dy, res)


def _final_loss(x2, gf, tgt, *, t_seq, tr, name):
    n, d = x2.shape
    per_seq = t_seq // tr

    def body(x_ref, g_ref, t_ref, dx_ref, dg_ref, loss_ref):
        i = pl.program_id(0)
        xv = x_ref[...]
        g = g_ref[...]
        r = lax.rsqrt(jnp.mean(xv * xv, axis=-1, keepdims=True) + EPS)
        xh = xv * r
        pos = (i % per_seq) * tr + _iota2((tr, 1), 0)
        real = pos >= CHUNK
        err = jnp.where(real, xh * g - t_ref[...], 0.0)
        dy = err * (1.0 / d)
        gdy = dy * g
        dx_ref[...] = r * (gdy - xh * jnp.mean(xh * gdy, axis=-1, keepdims=True))
        dg = jnp.sum(dy * xh, axis=0, keepdims=True)
        ls = 0.5 * jnp.sum(jnp.mean(err * err, axis=-1, keepdims=True), axis=0, keepdims=True)
        ls = jnp.where(_iota2((1, LANE), 1) == 0, ls, 0.0)

        @pl.when(i == 0)
        def _():
            dg_ref[...] = dg
            loss_ref[...] = ls

        @pl.when(i > 0)
        def _():
            dg_ref[...] += dg
            loss_ref[...] += ls

    row = pl.BlockSpec((tr, d), lambda i: (i, 0))
    vec = pl.BlockSpec((1, d), lambda i: (0, 0))
    one = pl.BlockSpec((1, LANE), lambda i: (0, 0))
    return pl.pallas_call(
        body, name=name, grid=(n // tr,),
        in_specs=[row, vec, row], out_specs=[row, vec, one],
        out_shape=[jax.ShapeDtypeStruct((n, d), F32), jax.ShapeDtypeStruct((1, d), F32),
                   jax.ShapeDtypeStruct((1, LANE), F32)],
        compiler_params=_cp(VMEM_BIG),
    )(x2, gf, tgt)


def _gnorm_fwd(o_dn, o_gla, projp, g_dn, g_gla, *, tr, name):
    n = o_dn.shape[0]
    w = NH * DN_D

    def body(odn_ref, ogl_ref, z_ref, r_ref, gdn_ref, ggl_ref, mix_ref):
        for grp, (o_ref, gate_ref, gain_ref) in enumerate(((odn_ref, z_ref, gdn_ref), (ogl_ref, r_ref, ggl_ref))):
            gain = gain_ref[...]
            for h in range(NH):
                sl = slice(h * DN_D, (h + 1) * DN_D)
                o = o_ref[:, sl]
                z = gate_ref[:, sl]
                r = lax.rsqrt(jnp.mean(o * o, axis=-1, keepdims=True) + EPS)
                y = (o * r * gain) * (z * _sigmoid(z))
                mix_ref[:, grp * w + h * DN_D: grp * w + (h + 1) * DN_D] = y.astype(mix_ref.dtype)

    row = pl.BlockSpec((tr, w), lambda i: (i, 0))
    vec = pl.BlockSpec((1, DN_D), lambda i: (0, 0))
    return pl.pallas_call(
        body, name=name, grid=(n // tr,),
        in_specs=[row, row, pl.BlockSpec((tr, w), lambda i: (i, C_DZ // w)),
                  pl.BlockSpec((tr, w), lambda i: (i, C_GR // w)), vec, vec],
        out_specs=pl.BlockSpec((tr, 2 * w), lambda i: (i, 0)),
        out_shape=jax.ShapeDtypeStruct((n, 2 * w), BF16),
        compiler_params=_cp(VMEM_BIG),
    )(o_dn, o_gla, projp, projp, g_dn, g_gla)


def _gnorm_bwd(dmix, o_dn, o_gla, projp, g_dn, g_gla, *, tr, name):
    n = o_dn.shape[0]
    w = NH * DN_D

    def body(dm_ref, odn_ref, ogl_ref, z_ref, r_ref, gdn_ref, ggl_ref,
             dodn_ref, ddz_ref, dogl_ref, dgr_ref, dgdn_ref, dggl_ref):
        first = pl.program_id(0) == 0
        groups = ((odn_ref, z_ref, gdn_ref, dodn_ref, ddz_ref, dgdn_ref),
                  (ogl_ref, r_ref, ggl_ref, dogl_ref, dgr_ref, dggl_ref))
        for grp, (o_ref, gate_ref, gain_ref, do_ref, dgate_ref, dgain_ref) in enumerate(groups):
            gain = gain_ref[...]
            dgain = jnp.zeros((1, DN_D), F32)
            for h in range(NH):
                sl = slice(h * DN_D, (h + 1) * DN_D)
                o = o_ref[:, sl]
                z = gate_ref[:, sl]
                dm = dm_ref[:, grp * w + h * DN_D: grp * w + (h + 1) * DN_D]
                r = lax.rsqrt(jnp.mean(o * o, axis=-1, keepdims=True) + EPS)
                oh = o * r
                s = _sigmoid(z)
                dn = dm * (z * s)
                dgate_ref[:, sl] = dm * (oh * gain) * (s * (1.0 + z * (1.0 - s)))
                gdn = dn * gain
                do_ref[:, sl] = r * (gdn - oh * jnp.mean(oh * gdn, axis=-1, keepdims=True))
                dgain = dgain + jnp.sum(dn * oh, axis=0, keepdims=True)

            @pl.when(first)
            def _():
                dgain_ref[...] = dgain

            @pl.when(jnp.logical_not(first))
            def _():
                dgain_ref[...] += dgain

    row = pl.BlockSpec((tr, w), lambda i: (i, 0))
    vec = pl.BlockSpec((1, DN_D), lambda i: (0, 0))
    big = jax.ShapeDtypeStruct((n, w), F32)
    small = jax.ShapeDtypeStruct((1, DN_D), F32)
    return pl.pallas_call(
        body, name=name, grid=(n // tr,),
        in_specs=[pl.BlockSpec((tr, 2 * w), lambda i: (i, 0)), row, row,
                  pl.BlockSpec((tr, w), lambda i: (i, C_DZ // w)), pl.BlockSpec((tr, w), lambda i: (i, C_GR // w)), vec, vec],
        out_specs=[row, row, row, row, vec, vec],
        out_shape=[big, big, big, big, small, small],
        compiler_params=_cp(VMEM_BIG),
    )(dmix, o_dn, o_gla, projp, projp, g_dn, g_gla)


QKV_W = 3 * NH * DN_D
HALO = 8


def _conv_z(xs_ref, cw_ref, tt):
    z = cw_ref[0:1, :] * xs_ref[pl.ds(HALO - 3, tt), :]
    for j in range(1, 4):
        z = z + cw_ref[j:j + 1, :] * xs_ref[pl.ds(HALO - 3 + j, tt), :]
    return z


def _dnprep_fwd(projp, conv_w, *, bsz, t_seq, tt, name):
    n = bsz * t_seq
    per_seq = t_seq // tt
    hw = NH * DN_D

    def body(x_ref, halo_ref, cw_ref, q_ref, k_ref, v_ref, xs_ref):
        i = pl.program_id(1)
        xs_ref[0:HALO, :] = jnp.where(i == 0, 0.0, halo_ref[...])
        xs_ref[HALO:HALO + tt, :] = x_ref[...]
        z = _conv_z(xs_ref, cw_ref, tt)
        a = z * _sigmoid(z)
        for grp, o_ref in enumerate((q_ref, k_ref)):
            for h in range(NH):
                ah = a[:, grp * hw + h * DN_D: grp * hw + (h + 1) * DN_D]
                rs = lax.rsqrt(jnp.sum(ah * ah, axis=-1, keepdims=True) + EPS)
                o_ref[:, h * DN_D:(h + 1) * DN_D] = ah * rs
        v_ref[...] = a[:, 2 * hw:3 * hw]

    def halo_map(b, i):
        return (jnp.maximum((b * t_seq + i * tt) // HALO - 1, 0), 0)

    out = pl.BlockSpec((tt, hw), lambda b, i: (b * per_seq + i, 0))
    sds = jax.ShapeDtypeStruct((n, hw), F32)
    return pl.pallas_call(
        body, name=name, grid=(bsz, per_seq),
        in_specs=[pl.BlockSpec((tt, QKV_W), lambda b, i: (b * per_seq + i, 0)),
                  pl.BlockSpec((HALO, QKV_W), halo_map),
                  pl.BlockSpec((4, QKV_W), lambda b, i: (0, 0))],
        out_specs=[out, out, out], out_shape=[sds, sds, sds],
        scratch_shapes=[pltpu.VMEM((tt + HALO, QKV_W), F32)],
        compiler_params=_cp(VMEM_BIG),
    )(projp, projp, conv_w)


def _dnprep_bwd_a(projp, conv_w, dq, dk, dv, *, bsz, t_seq, tt, name):
    n = bsz * t_seq
    per_seq = t_seq // tt
    hw = NH * DN_D

    def body(x_ref, halo_ref, cw_ref, dq_ref, dk_ref, dv_ref, dz_ref, dcw_ref, xs_ref):
        b, i = pl.program_id(0), pl.program_id(1)
        xs_ref[0:HALO, :] = jnp.where(i == 0, 0.0, halo_ref[...])
        xs_ref[HALO:HALO + tt, :] = x_ref[...]
        z = _conv_z(xs_ref, cw_ref, tt)
        s = _sigmoid(z)
        a = z * s
        dsilu = s * (1.0 + z * (1.0 - s))
        for grp, d_ref in enumerate((dq_ref, dk_ref)):
            for h in range(NH):
                sl = slice(grp * hw + h * DN_D, grp * hw + (h + 1) * DN_D)
                ah = a[:, sl]
                rs = lax.rsqrt(jnp.sum(ah * ah, axis=-1, keepdims=True) + EPS)
                y = ah * rs
                dy = d_ref[:, h * DN_D:(h + 1) * DN_D]
                da = rs * (dy - y * jnp.sum(dy * y, axis=-1, keepdims=True))
                dz_ref[:, sl] = da * dsilu[:, sl]
        dz_ref[:, 2 * hw:3 * hw] = dv_ref[...] * dsilu[:, 2 * hw:3 * hw]
        dz = dz_ref[...]
        first = jnp.logical_and(b == 0, i == 0)
        for j in range(4):
            part = jnp.sum(dz * xs_ref[pl.ds(HALO - 3 + j, tt), :], axis=0, keepdims=True)

            @pl.when(first)
            def _():
                dcw_ref[j:j + 1, :] = part

            @pl.when(jnp.logical_not(first))
            def _():
                dcw_ref[j:j + 1, :] += part

    def halo_map(b, i):
        return (jnp.maximum((b * t_seq + i * tt) // HALO - 1, 0), 0)

    hrow = pl.BlockSpec((tt, hw), lambda b, i: (b * per_seq + i, 0))
    return pl.pallas_call(
        body, name=name, grid=(bsz, per_seq),
        in_specs=[pl.BlockSpec((tt, QKV_W), lambda b, i: (b * per_seq + i, 0)),
                  pl.BlockSpec((HALO, QKV_W), halo_map),
                  pl.BlockSpec((4, QKV_W), lambda b, i: (0, 0)), hrow, hrow, hrow],
        out_specs=[pl.BlockSpec((tt, QKV_W), lambda b, i: (b * per_seq + i, 0)),
                   pl.BlockSpec((4, QKV_W), lambda b, i: (0, 0))],
        out_shape=[jax.ShapeDtypeStruct((n, QKV_W), F32), jax.ShapeDtypeStruct((4, QKV_W), F32)],
        scratch_shapes=[pltpu.VMEM((tt + HALO, QKV_W), F32)],
        compiler_params=_cp(VMEM_BIG),
    )(projp, projp, conv_w, dq, dk, dv)


def _dnprep_bwd_b(dz, conv_w, *, bsz, t_seq, tt, name):
    n = bsz * t_seq
    per_seq = t_seq // tt
    last_blk = n // HALO - 1

    def body(dz_ref, halo_ref, cw_ref, dx_ref, ds_ref):
        i = pl.program_id(1)
        ds_ref[0:tt, :] = dz_ref[...]
        ds_ref[tt:tt + HALO, :] = jnp.where(i == per_seq - 1, 0.0, halo_ref[...])
        dx = cw_ref[0:1, :] * ds_ref[pl.ds(3, tt), :]
        for j in range(1, 4):
            dx = dx + cw_ref[j:j + 1, :] * ds_ref[pl.ds(3 - j, tt), :]
        dx_ref[...] = dx

    def halo_map(b, i):
        return (jnp.minimum((b * t_seq + (i + 1) * tt) // HALO, last_blk), 0)

    row = pl.BlockSpec((tt, QKV_W), lambda b, i: (b * per_seq + i, 0))
    return pl.pallas_call(
        body, name=name, grid=(bsz, per_seq),
        in_specs=[row, pl.BlockSpec((HALO, QKV_W), halo_map), pl.BlockSpec((4, QKV_W), lambda b, i: (0, 0))],
        out_specs=row, out_shape=jax.ShapeDtypeStruct((n, QKV_W), F32),
        scratch_shapes=[pltpu.VMEM((tt + HALO, QKV_W), F32)],
        compiler_params=_cp(VMEM_BIG),
    )(dz, dz, conv_w)


def _masks64():
    r = _iota2((CHUNK, CHUNK), 0)
    c = _iota2((CHUNK, CHUNK), 1)
    return r, c


def _group(nc_seq, target=5):
    return max(g for g in range(1, target + 1) if nc_seq % g == 0)


def _round_robin(chains):
    live = list(chains)
    while live:
        nxt = []
        for ch in live:
            try:
                next(ch)
                nxt.append(ch)
            except StopIteration:
                pass
        live = nxt
        yield


def _run(chains):
    for _ in _round_robin(chains):
        pass


def _per_chunk(inner, kinds, grp):
    def body(*refs):
        chains = []
        for gi in range(grp):
            views = []
            for r, kind in zip(refs, kinds):
                if kind == "row":
                    views.append(r.at[pl.ds(gi * CHUNK, CHUNK)])
                elif kind == "lead":
                    views.append(r.at[pl.ds(gi, 1)])
                else:
                    views.append(r)
            chains.append(inner(gi, *views))
        _run(chains)
    return body


def _accumulate(ref, val, gi):
    if gi > 0:
        ref[...] += val
        return
    first = pl.program_id(0) == 0

    @pl.when(first)
    def _():
        ref[...] = val

    @pl.when(jnp.logical_not(first))
    def _():
        ref[...] += val


def _tri_inv(a_strict):
    r, c = _masks64()
    eye = (r == c).astype(F32)
    blk16 = (r // 16) == (c // 16)
    blk32 = (r // 32) == (c // 32)
    ld = jnp.where(blk16, a_strict, 0.0)
    x = eye - ld
    p = _nn(ld, ld)
    yield
    for step in range(3):
        xp = _nn(x, p)
        if step < 2:
            p = _nn(p, p)
        x = x + xp
        yield
    for lk in (jnp.where(jnp.logical_and(blk32, jnp.logical_not(blk16)), a_strict, 0.0),
               jnp.where(blk32, 0.0, a_strict)):
        y = x - eye
        s = lk + _nn(y, lk)
        yield
        x = x - s - _nn(s, y)
        yield
    return x


def _dn_gates(sa, alog, dtb, chunk_in_seq):
    rows = _iota2((CHUNK, LANE), 0)
    valid = jnp.logical_or(rows >= N_PAD, chunk_in_seq > 0)
    beta_t = _sigmoid(sa)
    ea = jnp.exp(alog)
    g_t = jnp.where(valid, -ea * _softplus(sa + dtb), 0.0)
    r, c = _masks64()
    ltri = (r >= c).astype(F32)
    gam_t = _nn_hi(ltri, g_t)
    return beta_t, g_t, gam_t, valid, ea


def _dn_intra_fwd(qn, kn, v, projp, alog_row, dtb_row, *, nc_seq, name):
    n = qn.shape[0]
    nct = n // CHUNK
    hw = NH * DN_D
    scale = DN_D ** -0.5

    grp = _group(nc_seq)

    def inner(gi, q_ref, k_ref, v_ref, sa_ref, al_ref, dt_ref, u_ref, w_ref, qg_ref, kd_ref, p_ref, t_ref, gl_ref):
        ci = (pl.program_id(0) * grp + gi) % nc_seq
        beta_t, _, gam_t, _, _ = _dn_gates(sa_ref[...], al_ref[...], dt_ref[...], ci)
        yield
        gam_tt = gam_t.T
        r, c = _masks64()
        incl = r >= c
        strict = r > c

        def head(h):
            sl = slice(h * DN_D, (h + 1) * DN_D)
            beta = beta_t[:, h:h + 1]
            gam = gam_t[:, 4 + h:5 + h]
            gam_row = gam_tt[4 + h:5 + h, :]
            gl = gam_t[CHUNK - 1:CHUNK, 4 + h:5 + h]
            dec = jnp.exp(jnp.where(incl, gam - gam_row, -jnp.inf))
            kh = k_ref[:, sl]
            qh = q_ref[:, sl] * scale
            vh = v_ref[:, sl]
            kk = _nt(kh, kh)
            qk = _nt(qh, kh)
            yield
            a = jnp.where(strict, beta * kk * dec, 0.0)
            tm = yield from _tri_inv(a)
            egam = jnp.exp(gam)
            u_ref[:, sl] = _nn(tm, beta * vh)
            w_ref[:, sl] = _nn(tm, (beta * egam) * kh)
            qg_ref[:, sl] = egam * qh
            kd_ref[:, sl] = jnp.exp(gl - gam) * kh
            p_ref[0, h] = qk * dec
            t_ref[0, h] = tm
            gl_ref[0, h:h + 1, :] = jnp.broadcast_to(jnp.exp(gl), (1, LANE))

        yield from _round_robin([head(h) for h in range(NH)])

    rows = grp * CHUNK
    row = pl.BlockSpec((rows, hw), lambda i: (i, 0))
    vec = pl.BlockSpec((1, LANE), lambda i: (0, 0))
    mat = pl.BlockSpec((grp, NH, CHUNK, CHUNK), lambda i: (i, 0, 0, 0))
    big = jax.ShapeDtypeStruct((n, hw), F32)
    msd = jax.ShapeDtypeStruct((nct, NH, CHUNK, CHUNK), F32)
    kinds = ["row"] * 4 + ["whole"] * 2 + ["row"] * 4 + ["lead"] * 3
    return pl.pallas_call(
        _per_chunk(inner, kinds, grp), name=name, grid=(nct // grp,),
        in_specs=[row, row, row, pl.BlockSpec((rows, LANE), lambda i: (i, C_SA // LANE)), vec, vec],
        out_specs=[row, row, row, row, mat, mat, pl.BlockSpec((grp, NH, LANE), lambda i: (i, 0, 0))],
        out_shape=[big, big, big, big, msd, msd, jax.ShapeDtypeStruct((nct, NH, LANE), F32)],
        compiler_params=_cp(VMEM_BIG),
    )(qn, kn, v, projp, alog_row, dtb_row)


def _dn_scan_fwd(u, w, qg, kd, p, gl, *, bsz, nc_seq, name):
    hw = NH * DN_D
    t_seq = nc_seq * CHUNK
    u, w, qg, kd = (z.reshape(bsz, t_seq, hw) for z in (u, w, qg, kd))
    p = p.reshape(bsz, nc_seq, NH, CHUNK, CHUNK)
    gl = gl.reshape(bsz, nc_seq, NH, LANE)

    def body(u_ref, w_ref, qg_ref, kd_ref, p_ref, gl_ref, o_ref, vn_ref, hist_ref, s_ref):
        @pl.when(pl.program_id(0) == 0)
        def _():
            s_ref[...] = jnp.zeros_like(s_ref)

        def chain(b, h):
            sl = slice(h * DN_D, (h + 1) * DN_D)
            s = s_ref[b, h]
            hist_ref[b, 0, h] = s
            ws = _nn(w_ref[b, :, sl], s)
            qs = _nn(qg_ref[b, :, sl], s)
            yield
            vn = u_ref[b, :, sl] - ws
            vn_ref[b, :, sl] = vn
            o_ref[b, :, sl] = qs + _nn(p_ref[b, 0, h], vn)
            s_ref[b, h] = gl_ref[b, 0, h:h + 1, :] * s + _tn(kd_ref[b, :, sl], vn)

        _run([chain(b, h) for b in range(bsz) for h in range(NH)])

    row = pl.BlockSpec((bsz, CHUNK, hw), lambda i: (0, i, 0))
    outs = pl.pallas_call(
        body, name=name, grid=(nc_seq,),
        in_specs=[row, row, row, row, pl.BlockSpec((bsz, 1, NH, CHUNK, CHUNK), lambda i: (0, i, 0, 0, 0)),
                  pl.BlockSpec((bsz, 1, NH, LANE), lambda i: (0, i, 0, 0))],
        out_specs=[row, row, pl.BlockSpec((bsz, 1, NH, DN_D, DN_D), lambda i: (0, i, 0, 0, 0))],
        out_shape=[jax.ShapeDtypeStruct((bsz, t_seq, hw), F32), jax.ShapeDtypeStruct((bsz, t_seq, hw), F32),
                   jax.ShapeDtypeStruct((bsz, nc_seq, NH, DN_D, DN_D), F32)],
        scratch_shapes=[pltpu.VMEM((bsz, NH, DN_D, DN_D), F32)],
        compiler_params=_cp(VMEM_BIG, ("arbitrary",)),
    )(u, w, qg, kd, p, gl)
    o, vn, hist = outs
    return o.reshape(bsz * t_seq, hw), vn.reshape(bsz * t_seq, hw), hist


def _dn_scan_bwd(do, w, qg, kd, vn, p, gl, hist, *, bsz, nc_seq, name):
    hw = NH * DN_D
    t_seq = nc_seq * CHUNK
    do, w, qg, kd, vn = (z.reshape(bsz, t_seq, hw) for z in (do, w, qg, kd, vn))
    p = p.reshape(bsz, nc_seq, NH, CHUNK, CHUNK)
    gl = gl.reshape(bsz, nc_seq, NH, LANE)

    def body(do_ref, w_ref, qg_ref, kd_ref, vn_ref, p_ref, gl_ref, hist_ref,
             du_ref, dw_ref, dqg_ref, dkd_ref, dp_ref, dgl_ref, ds_ref):
        @pl.when(pl.program_id(0) == 0)
        def _():
            ds_ref[...] = jnp.zeros_like(ds_ref)

        def chain(b, h):
            sl = slice(h * DN_D, (h + 1) * DN_D)
            s = hist_ref[b, 0, h]
            dsn = ds_ref[b, h]
            doh = do_ref[b, :, sl]
            vnh = vn_ref[b, :, sl]
            kdh = kd_ref[b, :, sl]
            dvn = _tn(p_ref[b, 0, h], doh) + _nn(kdh, dsn)
            du_ref[b, :, sl] = dvn
            dqg_ref[b, :, sl] = _nt(doh, s)
            dp_ref[b, 0, h] = _nt(doh, vnh)
            dkd_ref[b, :, sl] = _nt(vnh, dsn)
            ds_part = _tn(qg_ref[b, :, sl], doh) + gl_ref[b, 0, h:h + 1, :] * dsn
            dgl = jnp.sum(jnp.sum(dsn * s, axis=0, keepdims=True), axis=1, keepdims=True)
            dgl_ref[b, 0, h:h + 1, :] = jnp.broadcast_to(dgl, (1, LANE))
            yield
            dw_ref[b, :, sl] = -_nt(dvn, s)
            ds_ref[b, h] = ds_part - _tn(w_ref[b, :, sl], dvn)

        _run([chain(b, h) for b in range(bsz) for h in range(NH)])

    rev = lambda i: nc_seq - 1 - i
    row = pl.BlockSpec((bsz, CHUNK, hw), lambda i: (0, rev(i), 0))
    mat = pl.BlockSpec((bsz, 1, NH, CHUNK, CHUNK), lambda i: (0, rev(i), 0, 0, 0))
    glb = pl.BlockSpec((bsz, 1, NH, LANE), lambda i: (0, rev(i), 0, 0))
    big = jax.ShapeDtypeStruct((bsz, t_seq, hw), F32)
    outs = pl.pallas_call(
        body, name=name, grid=(nc_seq,),
        in_specs=[row, row, row, row, row, mat, glb,
                  pl.BlockSpec((bsz, 1, NH, DN_D, DN_D), lambda i: (0, rev(i), 0, 0, 0))],
        out_specs=[row, row, row, row, mat, glb],
        out_shape=[big, big, big, big, jax.ShapeDtypeStruct((bsz, nc_seq, NH, CHUNK, CHUNK), F32),
                   jax.ShapeDtypeStruct((bsz, nc_seq, NH, LANE), F32)],
        scratch_shapes=[pltpu.VMEM((bsz, NH, DN_D, DN_D), F32)],
        compiler_params=_cp(VMEM_BIG, ("arbitrary",)),
    )(do, w, qg, kd, vn, p, gl, hist)
    du, dw, dqg, dkd, dp, dgl = outs
    n = bsz * t_seq
    return (du.reshape(n, hw), dw.reshape(n, hw), dqg.reshape(n, hw), dkd.reshape(n, hw),
            dp.reshape(bsz * nc_seq, NH, CHUNK, CHUNK), dgl.reshape(bsz * nc_seq, NH, LANE))


def _dn_intra_bwd(qn, kn, v, projp, alog_row, dtb_row, u, w, tmat, du, dw, dqg, dkd, dp, dgl, *, nc_seq, name):
    n = qn.shape[0]
    nct = n // CHUNK
    hw = NH * DN_D
    scale = DN_D ** -0.5

    grp = _group(nc_seq)

    def inner(gi, q_ref, k_ref, v_ref, sa_ref, al_ref, dt_ref, u_ref, w_ref, t_ref, du_ref, dw_ref, dqg_ref, dkd_ref,
              dp_ref, dgl_ref, dq_ref, dk_ref, dv_ref, dsa_ref, dal_ref, ddt_ref):
        ci = (pl.program_id(0) * grp + gi) % nc_seq
        sa = sa_ref[...]
        beta_t, g_t, gam_t, valid, ea = _dn_gates(sa, al_ref[...], dt_ref[...], ci)
        yield
        gam_tt = gam_t.T
        r, c = _masks64()
        incl = r >= c
        strict = r > c
        lane = _iota2((CHUNK, LANE), 1)
        rows1 = _iota2((CHUNK, 1), 0)
        acc = [jnp.zeros((CHUNK, LANE), F32)]

        def head(h):
            sl = slice(h * DN_D, (h + 1) * DN_D)
            beta = beta_t[:, h:h + 1]
            gam = gam_t[:, 4 + h:5 + h]
            gam_row = gam_tt[4 + h:5 + h, :]
            gl = gam_t[CHUNK - 1:CHUNK, 4 + h:5 + h]
            dec = jnp.exp(jnp.where(incl, gam - gam_row, -jnp.inf))
            kh = k_ref[:, sl]
            qh = q_ref[:, sl] * scale
            vh = v_ref[:, sl]
            kk = _nt(kh, kh)
            qk = _nt(qh, kh)
            a = jnp.where(strict, beta * kk * dec, 0.0)
            pm = qk * dec
            tm = t_ref[0, h]
            uh = u_ref[:, sl]
            wh = w_ref[:, sl]
            egam = jnp.exp(gam)
            ekd = jnp.exp(gl - gam)
            dvb = _tn(tm, du_ref[:, sl])
            dkg = _tn(tm, dw_ref[:, sl])
            yield
            da = jnp.where(strict, -(_nt(dvb, uh) + _nt(dkg, wh)), 0.0)
            yield
            dad = da * dec
            dkk = beta * dad
            dbeta = jnp.sum(dad * kk, axis=1, keepdims=True)
            dpm = jnp.where(incl, dp_ref[0, h], 0.0)
            dqk = dpm * dec
            e = da * a + dpm * pm
            dgam = jnp.sum(e, axis=1, keepdims=True) - jnp.sum(e.T, axis=1, keepdims=True)
            dqgh = dqg_ref[:, sl]
            dkdh = dkd_ref[:, sl]
            dkh = (_nn(dkk, kh) + _tn(dkk, kh) + _tn(dqk, qh) + (beta * egam) * dkg + ekd * dkdh)
            dqh = _nn(dqk, kh) + egam * dqgh
            dbeta = dbeta + jnp.sum(dkg * (egam * kh), axis=1, keepdims=True) + jnp.sum(dvb * vh, axis=1, keepdims=True)
            rkd = jnp.sum(dkdh * (ekd * kh), axis=1, keepdims=True)
            dgam = (dgam + jnp.sum(dkg * ((beta * egam) * kh), axis=1, keepdims=True)
                    + jnp.sum(dqgh * (egam * qh), axis=1, keepdims=True) - rkd)
            dgam_last = jnp.sum(rkd, axis=0, keepdims=True) + dgl_ref[0, h:h + 1, 0:1] * jnp.exp(gl)
            dgam = dgam + jnp.where(rows1 == CHUNK - 1, dgam_last, 0.0)
            dq_ref[:, sl] = dqh * scale
            dk_ref[:, sl] = dkh
            dv_ref[:, sl] = beta * dvb
            acc[0] = acc[0] + jnp.where(lane == h, dbeta, 0.0) + jnp.where(lane == 4 + h, dgam, 0.0)

        yield from _round_robin([head(h) for h in range(NH)])
        acc_t = acc[0]
        utri = (r <= c).astype(F32)
        dg_t = _nn_hi(utri, acc_t)
        ddb = acc_t * beta_t * (1.0 - beta_t)
        dda = jnp.where(valid, dg_t * (-ea) * _sigmoid(sa + dt_ref[...]), 0.0)
        dsa_ref[...] = jnp.where(lane < 4, ddb, jnp.where(lane < 8, dda, 0.0))
        in_g = jnp.logical_and(lane >= 4, lane < 8)
        dal = jnp.sum(jnp.where(in_g, dg_t * g_t, 0.0), axis=0, keepdims=True)
        ddt = jnp.sum(jnp.where(in_g, dda, 0.0), axis=0, keepdims=True)
        _accumulate(dal_ref, dal, gi)
        _accumulate(ddt_ref, ddt, gi)

    rows = grp * CHUNK
    row = pl.BlockSpec((rows, hw), lambda i: (i, 0))
    vec = pl.BlockSpec((1, LANE), lambda i: (0, 0))
    mat = pl.BlockSpec((grp, NH, CHUNK, CHUNK), lambda i: (i, 0, 0, 0))
    glb = pl.BlockSpec((grp, NH, LANE), lambda i: (i, 0, 0))
    big = jax.ShapeDtypeStruct((n, hw), F32)
    v128 = jax.ShapeDtypeStruct((1, LANE), F32)
    kinds = (["row"] * 4 + ["whole"] * 2 + ["row"] * 2 + ["lead"] + ["row"] * 4 + ["lead"] * 2
             + ["row"] * 4 + ["whole"] * 2)
    return pl.pallas_call(
        _per_chunk(inner, kinds, grp), name=name, grid=(nct // grp,),
        in_specs=[row, row, row, pl.BlockSpec((rows, LANE), lambda i: (i, C_SA // LANE)), vec, vec,
                  row, row, mat, row, row, row, row, mat, glb],
        out_specs=[row, row, row, pl.BlockSpec((rows, LANE), lambda i: (i, 0)), vec, vec],
        out_shape=[big, big, big, jax.ShapeDtypeStruct((n, LANE), F32), v128, v128],
        compiler_params=_cp(VMEM_BIG, ("arbitrary",)),
    )(qn, kn, v, projp, alog_row, dtb_row, u, w, tmat, du, dw, dqg, dkd, dp, dgl)


GQ_W = NH * GLA_DK
GV_W = NH * GLA_DV
GLA_NORM = 16.0
MID = CHUNK // 2


def _gla_gates(sb, w2p, gb, chunk_in_seq):
    rows = _iota2((CHUNK, GQ_W), 0)
    valid = jnp.logical_or(rows >= N_PAD, chunk_in_seq > 0)
    graw = _nn_hi(sb, w2p) + gb
    yield
    g = jnp.where(valid, _logsigmoid(graw) * (1.0 / GLA_NORM), 0.0)
    r, c = _masks64()
    bcum = _nn_hi((r >= c).astype(F32), g)
    yield
    return graw, bcum, valid


def _head_mask(h):
    lane = _iota2((1, GQ_W), 1)
    return jnp.logical_and(lane >= h * GLA_DK, lane < (h + 1) * GLA_DK)


def _gla_intra_fwd(projp, w2p, gb, *, nc_seq, name):
    n = projp.shape[0]
    nct = n // CHUNK
    scale = GLA_DK ** -0.5

    grp = _group(nc_seq)
    rows = grp * CHUNK

    def inner(gi, qk_ref, v_ref, sb_ref, w2_ref, gb_ref, oi_ref, qg_ref, kd_ref, gl_ref):
        ci = (pl.program_id(0) * grp + gi) % nc_seq
        _, bc, _ = yield from _gla_gates(sb_ref[...], w2_ref[...], gb_ref[...], ci)
        bref = bc[MID:MID + 1, :]
        bl = bc[CHUNK - 1:CHUNK, :]
        q = qk_ref[:, 0:GQ_W] * scale
        k = qk_ref[:, GQ_W:2 * GQ_W]
        qi = q * jnp.exp(bc - bref)
        ki = k * jnp.exp(bref - bc)
        qg_ref[...] = q * jnp.exp(bc)
        kd_ref[...] = k * jnp.exp(bl - bc)
        gl_ref[0] = jnp.exp(bl)
        r, c = _masks64()
        incl = r >= c
        a = [jnp.where(incl, _nt(jnp.where(_head_mask(h), qi, 0.0), ki), 0.0) for h in range(NH)]
        yield
        for h in range(NH):
            oi_ref[:, h * GLA_DV:(h + 1) * GLA_DV] = _nn(a[h], v_ref[:, h * GLA_DV:(h + 1) * GLA_DV])

    kinds = ["row"] * 3 + ["whole"] * 2 + ["row"] * 3 + ["lead"]
    return pl.pallas_call(
        _per_chunk(inner, kinds, grp), name=name, grid=(nct // grp,),
        in_specs=[pl.BlockSpec((rows, 2 * GQ_W), lambda i: (i, C_GQK // (2 * GQ_W))),
                  pl.BlockSpec((rows, GV_W), lambda i: (i, C_GV // GV_W)),
                  pl.BlockSpec((rows, LANE), lambda i: (i, C_SB // LANE)),
                  pl.BlockSpec((LANE, GQ_W), lambda i: (0, 0)), pl.BlockSpec((1, GQ_W), lambda i: (0, 0))],
        out_specs=[pl.BlockSpec((rows, GV_W), lambda i: (i, 0)), pl.BlockSpec((rows, GQ_W), lambda i: (i, 0)),
                   pl.BlockSpec((rows, GQ_W), lambda i: (i, 0)), pl.BlockSpec((grp, 1, GQ_W), lambda i: (i, 0, 0))],
        out_shape=[jax.ShapeDtypeStruct((n, GV_W), F32), jax.ShapeDtypeStruct((n, GQ_W), F32),
                   jax.ShapeDtypeStruct((n, GQ_W), F32), jax.ShapeDtypeStruct((nct, 1, GQ_W), F32)],
        compiler_params=_cp(VMEM_BIG),
    )(projp, projp, projp, w2p, gb)


def _gla_scan_fwd(oi, qg, kd, gl, projp, *, bsz, nc_seq, name):
    t_seq = nc_seq * CHUNK
    oi = oi.reshape(bsz, t_seq, GV_W)
    qg, kd = qg.reshape(bsz, t_seq, GQ_W), kd.reshape(bsz, t_seq, GQ_W)
    gl = gl.reshape(bsz, nc_seq, 1, GQ_W)
    pj = projp.reshape(bsz, t_seq, PW)

    def body(oi_ref, qg_ref, kd_ref, gl_ref, v_ref, o_ref, hist_ref, st_ref):
        @pl.when(pl.program_id(0) == 0)
        def _():
            st_ref[...] = jnp.zeros_like(st_ref)

        for b in range(bsz):
            st = st_ref[b]
            hist_ref[b, 0] = st
            qgb = qg_ref[b]
            kdb = kd_ref[b]
            upd = jnp.zeros((GLA_DV, GQ_W), F32)
            for h in range(NH):
                sl = slice(h * GLA_DV, (h + 1) * GLA_DV)
                m = _head_mask(h)
                o_ref[b, :, sl] = oi_ref[b, :, sl] + _nt(jnp.where(m, qgb, 0.0), st)
                upd = upd + jnp.where(m, _tn(v_ref[b, :, sl], kdb), 0.0)
            st_ref[b] = gl_ref[b, 0] * st + upd

    outs = pl.pallas_call(
        body, name=name, grid=(nc_seq,),
        in_specs=[pl.BlockSpec((bsz, CHUNK, GV_W), lambda i: (0, i, 0)),
                  pl.BlockSpec((bsz, CHUNK, GQ_W), lambda i: (0, i, 0)),
                  pl.BlockSpec((bsz, CHUNK, GQ_W), lambda i: (0, i, 0)),
                  pl.BlockSpec((bsz, 1, 1, GQ_W), lambda i: (0, i, 0, 0)),
                  pl.BlockSpec((bsz, CHUNK, GV_W), lambda i: (0, i, C_GV // GV_W))],
        out_specs=[pl.BlockSpec((bsz, CHUNK, GV_W), lambda i: (0, i, 0)),
                   pl.BlockSpec((bsz, 1, GLA_DV, GQ_W), lambda i: (0, i, 0, 0))],
        out_shape=[jax.ShapeDtypeStruct((bsz, t_seq, GV_W), F32),
                   jax.ShapeDtypeStruct((bsz, nc_seq, GLA_DV, GQ_W), F32)],
        scratch_shapes=[pltpu.VMEM((bsz, GLA_DV, GQ_W), F32)],
        compiler_params=_cp(VMEM_BIG, ("arbitrary",)),
    )(oi, qg, kd, gl, pj)
    return outs[0].reshape(bsz * t_seq, GV_W), outs[1]


def _gla_scan_bwd(do, qg, kd, gl, projp, hist, *, bsz, nc_seq, name):
    t_seq = nc_seq * CHUNK
    do = do.reshape(bsz, t_seq, GV_W)
    qg, kd = qg.reshape(bsz, t_seq, GQ_W), kd.reshape(bsz, t_seq, GQ_W)
    gl = gl.reshape(bsz, nc_seq, 1, GQ_W)
    pj = projp.reshape(bsz, t_seq, PW)

    def body(do_ref, qg_ref, kd_ref, gl_ref, v_ref, hist_ref, dqg_ref, dkd_ref, dv_ref, dgl_ref, dst_ref):
        @pl.when(pl.program_id(0) == 0)
        def _():
            dst_ref[...] = jnp.zeros_like(dst_ref)

        for b in range(bsz):
            st = hist_ref[b, 0]
            dst = dst_ref[b]
            qgb = qg_ref[b]
            kdb = kd_ref[b]
            dqg = jnp.zeros((CHUNK, GQ_W), F32)
            dkd = jnp.zeros((CHUNK, GQ_W), F32)
            add = jnp.zeros((GLA_DV, GQ_W), F32)
            for h in range(NH):
                sl = slice(h * GLA_DV, (h + 1) * GLA_DV)
                m = _head_mask(h)
                doh = do_ref[b, :, sl]
                vh = v_ref[b, :, sl]
                dqg = dqg + jnp.where(m, _nn(doh, st), 0.0)
                dkd = dkd + jnp.where(m, _nn(vh, dst), 0.0)
                dv_ref[b, :, sl] = _nt(jnp.where(m, kdb, 0.0), dst)
                add = add + jnp.where(m, _tn(doh, qgb), 0.0)
            dqg_ref[b] = dqg
            dkd_ref[b] = dkd
            dgl_ref[b, 0] = jnp.sum(dst * st, axis=0, keepdims=True)
            dst_ref[b] = gl_ref[b, 0] * dst + add

    rev = lambda i: nc_seq - 1 - i
    outs = pl.pallas_call(
        body, name=name, grid=(nc_seq,),
        in_specs=[pl.BlockSpec((bsz, CHUNK, GV_W), lambda i: (0, rev(i), 0)),
                  pl.BlockSpec((bsz, CHUNK, GQ_W), lambda i: (0, rev(i), 0)),
                  pl.BlockSpec((bsz, CHUNK, GQ_W), lambda i: (0, rev(i), 0)),
                  pl.BlockSpec((bsz, 1, 1, GQ_W), lambda i: (0, rev(i), 0, 0)),
                  pl.BlockSpec((bsz, CHUNK, GV_W), lambda i: (0, rev(i), C_GV // GV_W)),
                  pl.BlockSpec((bsz, 1, GLA_DV, GQ_W), lambda i: (0, rev(i), 0, 0))],
        out_specs=[pl.BlockSpec((bsz, CHUNK, GQ_W), lambda i: (0, rev(i), 0)),
                   pl.BlockSpec((bsz, CHUNK, GQ_W), lambda i: (0, rev(i), 0)),
                   pl.BlockSpec((bsz, CHUNK, GV_W), lambda i: (0, rev(i), 0)),
                   pl.BlockSpec((bsz, 1, 1, GQ_W), lambda i: (0, rev(i), 0, 0))],
        out_shape=[jax.ShapeDtypeStruct((bsz, t_seq, GQ_W), F32), jax.ShapeDtypeStruct((bsz, t_seq, GQ_W), F32),
                   jax.ShapeDtypeStruct((bsz, t_seq, GV_W), F32), jax.ShapeDtypeStruct((bsz, nc_seq, 1, GQ_W), F32)],
        scratch_shapes=[pltpu.VMEM((bsz, GLA_DV, GQ_W), F32)],
        compiler_params=_cp(VMEM_BIG, ("arbitrary",)),
    )(do, qg, kd, gl, pj, hist)
    n = bsz * t_seq
    return (outs[0].reshape(n, GQ_W), outs[1].reshape(n, GQ_W), outs[2].reshape(n, GV_W),
            outs[3].reshape(bsz * nc_seq, 1, GQ_W))


def _gla_intra_bwd(projp, w2p, gb, do, dqg, dkd, dvi, dgl, *, nc_seq, name):
    n = projp.shape[0]
    nct = n // CHUNK
    scale = GLA_DK ** -0.5

    grp = _group(nc_seq)
    rows = grp * CHUNK

    def inner(gi, qk_ref, v_ref, sb_ref, w2_ref, gb_ref, do_ref, dqg_ref, dkd_ref, dvi_ref, dgl_ref,
              dqk_ref, dv_ref, dsb_ref, dw2_ref, dgb_ref):
        ci = (pl.program_id(0) * grp + gi) % nc_seq
        sb = sb_ref[...]
        w2 = w2_ref[...]
        graw, bc, valid = yield from _gla_gates(sb, w2, gb_ref[...], ci)
        bref = bc[MID:MID + 1, :]
        bl = bc[CHUNK - 1:CHUNK, :]
        q = qk_ref[:, 0:GQ_W] * scale
        k = qk_ref[:, GQ_W:2 * GQ_W]
        ex1 = jnp.exp(bc - bref)
        ex2 = jnp.exp(bref - bc)
        eb = jnp.exp(bc)
        ekd = jnp.exp(bl - bc)
        qi, ki = q * ex1, k * ex2
        r, c = _masks64()
        incl = r >= c
        upper = r <= c
        a_t, da, da_t = [], [], []
        for h in range(NH):
            sl = slice(h * GLA_DV, (h + 1) * GLA_DV)
            doh = do_ref[:, sl]
            vh = v_ref[:, sl]
            a_t.append(jnp.where(upper, _nt(jnp.where(_head_mask(h), ki, 0.0), qi), 0.0))
            da.append(jnp.where(incl, _nt(doh, vh), 0.0))
            da_t.append(jnp.where(upper, _nt(vh, doh), 0.0))
        yield
        dqi = jnp.zeros((CHUNK, GQ_W), F32)
        dki = jnp.zeros((CHUNK, GQ_W), F32)
        for h in range(NH):
            sl = slice(h * GLA_DV, (h + 1) * GLA_DV)
            m = _head_mask(h)
            dv_ref[:, sl] = _nn(a_t[h], do_ref[:, sl]) + dvi_ref[:, sl]
            dqi = dqi + jnp.where(m, _nn(da[h], ki), 0.0)
            dki = dki + jnp.where(m, _nn(da_t[h], qi), 0.0)
        yield
        dqg = dqg_ref[...]
        dkd = dkd_ref[...]
        dqk_ref[:, 0:GQ_W] = (dqi * ex1 + dqg * eb) * scale
        dqk_ref[:, GQ_W:2 * GQ_W] = dki * ex2 + dkd * ekd
        t_qi, t_ki, t_kd = dqi * qi, dki * ki, dkd * (k * ekd)
        db = t_qi - t_ki + dqg * (q * eb) - t_kd
        dbref = jnp.sum(t_ki - t_qi, axis=0, keepdims=True)
        dbl = jnp.sum(t_kd, axis=0, keepdims=True) + dgl_ref[0] * jnp.exp(bl)
        rows = _iota2((CHUNK, GQ_W), 0)
        db = db + jnp.where(rows == MID, dbref, 0.0) + jnp.where(rows == CHUNK - 1, dbl, 0.0)
        dg = _nn_hi(upper.astype(F32), db)
        yield
        dgraw = jnp.where(valid, dg * (1.0 / GLA_NORM) * _sigmoid(-graw), 0.0)
        dsb_ref[...] = _nt_hi(dgraw, w2)
        dw2 = _tn_hi(sb, dgraw)
        dgb = jnp.sum(dgraw, axis=0, keepdims=True)
        _accumulate(dw2_ref, dw2, gi)
        _accumulate(dgb_ref, dgb, gi)

    rq = pl.BlockSpec((rows, GQ_W), lambda i: (i, 0))
    rv = pl.BlockSpec((rows, GV_W), lambda i: (i, 0))
    kinds = ["row"] * 3 + ["whole"] * 2 + ["row"] * 4 + ["lead"] + ["row"] * 3 + ["whole"] * 2
    return pl.pallas_call(
        _per_chunk(inner, kinds, grp), name=name, grid=(nct // grp,),
        in_specs=[pl.BlockSpec((rows, 2 * GQ_W), lambda i: (i, C_GQK // (2 * GQ_W))),
                  pl.BlockSpec((rows, GV_W), lambda i: (i, C_GV // GV_W)),
                  pl.BlockSpec((rows, LANE), lambda i: (i, C_SB // LANE)),
                  pl.BlockSpec((LANE, GQ_W), lambda i: (0, 0)), pl.BlockSpec((1, GQ_W), lambda i: (0, 0)),
                  rv, rq, rq, rv, pl.BlockSpec((grp, 1, GQ_W), lambda i: (i, 0, 0))],
        out_specs=[pl.BlockSpec((rows, 2 * GQ_W), lambda i: (i, 0)), rv, pl.BlockSpec((rows, LANE), lambda i: (i, 0)),
                   pl.BlockSpec((LANE, GQ_W), lambda i: (0, 0)), pl.BlockSpec((1, GQ_W), lambda i: (0, 0))],
        out_shape=[jax.ShapeDtypeStruct((n, 2 * GQ_W), F32), jax.ShapeDtypeStruct((n, GV_W), F32),
                   jax.ShapeDtypeStruct((n, LANE), F32), jax.ShapeDtypeStruct((LANE, GQ_W), F32),
                   jax.ShapeDtypeStruct((1, GQ_W), F32)],
        compiler_params=_cp(VMEM_BIG, ("arbitrary",)),
    )(projp, projp, projp, w2p, gb, do, dqg, dkd, dvi, dgl)


SECTIONS = ((C_QKV, 1536), (C_DZ, 512), (C_GQK, 512), (C_GV, 512), (C_GR, 512), (C_SA, 128), (C_SB, 128))


def _inproj_bwd(secs, wp, h0, g1, dx1, *, tr, name):
    n, d = h0.shape

    def body(*refs):
        sec_refs = refs[:len(SECTIONS)]
        wp_ref, h0_ref, g_ref, dx1_ref, o_ref, dg_ref = refs[len(SECTIONS):]
        dh = None
        for s_ref, (off, wd) in zip(sec_refs, SECTIONS):
            part = _nt(s_ref[...], wp_ref[:, off:off + wd])
            dh = part if dh is None else dh + part
        dx, dg = _rms_bwd_math(h0_ref[...], g_ref[...], dh)
        o_ref[...] = dx1_ref[...] + dx

        @pl.when(pl.program_id(0) == 0)
        def _():
            dg_ref[...] = dg

        @pl.when(pl.program_id(0) > 0)
        def _():
            dg_ref[...] += dg

    row = pl.BlockSpec((tr, d), lambda i: (i, 0))
    vec = pl.BlockSpec((1, d), lambda i: (0, 0))
    return pl.pallas_call(
        body, name=name, grid=(n // tr,),
        in_specs=[pl.BlockSpec((tr, wd), lambda i: (i, 0)) for _, wd in SECTIONS]
        + [pl.BlockSpec((d, PW), lambda i: (0, 0)), row, vec, row],
        out_specs=[row, vec],
        out_shape=[jax.ShapeDtypeStruct((n, d), F32), jax.ShapeDtypeStruct((1, d), F32)],
        compiler_params=_cp(VMEM_BIG),
    )(*secs, wp, h0, g1, dx1)


def _adamw(w, g, m, v, *, name):
    r, c = w.shape
    tr = _tile(r, 256, 8) if r > 256 else r
    c1 = 1.0 - ADAM_B1 ** ADAM_STEP
    c2 = 1.0 - ADAM_B2 ** ADAM_STEP

    def body(w_ref, g_ref, m_ref, v_ref, d_ref, nm_ref, nv_ref):
        gv = g_ref[...]
        nm = ADAM_B1 * m_ref[...] + (1.0 - ADAM_B1) * gv
        nv = ADAM_B2 * v_ref[...] + (1.0 - ADAM_B2) * (gv * gv)
        d_ref[...] = -ADAM_LR * ((nm / c1) / (jnp.sqrt(nv / c2) + ADAM_EPS) + ADAM_WD * w_ref[...])
        nm_ref[...] = nm
        nv_ref[...] = nv

    blk = pl.BlockSpec((tr, c), lambda i: (i, 0))
    sds = jax.ShapeDtypeStruct((r, c), F32)
    return pl.pallas_call(
        body, name=name, grid=(r // tr,), in_specs=[blk] * 4, out_specs=[blk] * 3, out_shape=[sds] * 3,
        compiler_params=_cp(VMEM_BIG),
    )(w, g, m, v)


def _pair_sum(where, g, theirs, *, name):
    lead, r, cols = g.shape
    half = r // 2
    tr = _tile(half, 256, 16)
    nh = half // tr

    def body(w_ref, a_ref, b_ref, o_ref):
        o_ref[...] = (a_ref[...] + b_ref[...]).astype(o_ref.dtype)

    blk = pl.BlockSpec((1, tr, cols), lambda s, i, w: (s, i, 0))
    return pl.pallas_call(
        body, name=name,
        grid_spec=pltpu.PrefetchScalarGridSpec(
            num_scalar_prefetch=1, grid=(lead, nh),
            in_specs=[pl.BlockSpec((1, tr, cols), lambda s, i, w: (s, w[0] * nh + i, 0)), blk], out_specs=blk),
        out_shape=jax.ShapeDtypeStruct((lead, half, cols), BF16), compiler_params=_cp(VMEM_BIG),
    )(where, g, theirs)


def _chip_sum(where, pair, q, *, name):
    _, half, cols = pair.shape
    tr = _tile(half, 256, 16)
    nh = half // tr

    def body(w_ref, own_ref, q1_ref, q2_ref, q3_ref, o_ref):
        f = lambda ref: ref[0].astype(F32)
        o_ref[...] = ((f(own_ref) + f(q1_ref)) + f(q2_ref)) + f(q3_ref)

    def peer(d):
        return pl.BlockSpec((1, tr, cols), lambda i, w: ((w[1] + d) % N_CHIPS, i, 0))

    return pl.pallas_call(
        body, name=name,
        grid_spec=pltpu.PrefetchScalarGridSpec(
            num_scalar_prefetch=1, grid=(nh,),
            in_specs=[peer(0), peer(1), peer(2), peer(3)],
            out_specs=pl.BlockSpec((tr, cols), lambda i, w: (w[0] * nh + i, 0))),
        out_shape=jax.ShapeDtypeStruct((2 * half, cols), F32), compiler_params=_cp(VMEM_BIG),
    )(where, pair, q, q, q)


ANY = pl.BlockSpec(memory_space=pl.ANY)
VM = pl.BlockSpec(memory_space=pltpu.VMEM)
CAST_ROWS = 64


def _place():
    return lax.axis_index("x"), lax.axis_index("y"), lax.axis_index("c")


def _other_chips(x, y):
    return [(1 - x, y, 2 * (1 - x) + y), (x, 1 - y, 2 * x + 1 - y), (1 - x, 1 - y, 2 * (1 - x) + 1 - y)]


def _gather_weights(w_in, w_out, w_up, w_down, meta, conv, gw2):
    big = (w_in, w_out, w_up, w_down)
    small = (meta, conv, gw2)
    n_arr = len(big) + len(small)

    def body(*refs):
        ins = refs[:n_arr]
        outs = refs[n_arr:2 * n_arr]
        stage = refs[2 * n_arr:2 * n_arr + len(big)]
        send_sems, recv_sems, local_sems = refs[2 * n_arr + len(big):]
        x, y, c = _place()
        me = 2 * x + y
        srcs = []
        for k in range(n_arr):
            if k < len(big):
                src, dst = ins[k], stage[k]

                def cast_rows(i, carry, src=src, dst=dst):
                    rows = pl.ds(pl.multiple_of(i * CAST_ROWS, CAST_ROWS), CAST_ROWS)
                    dst[rows, :] = src[rows, :].astype(BF16)
                    return carry

                lax.fori_loop(0, src.shape[0] // CAST_ROWS, cast_rows, 0)
                srcs.append(stage[k])
            else:
                srcs.append(ins[k])
        local = [pltpu.make_async_copy(srcs[k], outs[k].at[me], local_sems.at[k]) for k in range(n_arr)]
        for cp in local:
            cp.start()
        def pieces(k):
            if k < len(big):
                return _row_chunks(srcs[k].shape[0], ICI_SPLIT)[0]
            return [(0, pl.ds(0, srcs[k].shape[0]))]

        sends = []
        for k in range(n_arr):
            for d, (px, py, _) in enumerate(_other_chips(x, y)):
                for s, rows in pieces(k):
                    cp = pltpu.make_async_remote_copy(
                        src_ref=srcs[k].at[rows, :], dst_ref=outs[k].at[me, rows, :], send_sem=send_sems.at[k, d, s],
                        recv_sem=recv_sems.at[k, d, s], device_id=(px, py, c), device_id_type=MESH)
                    cp.start()
                    sends.append(cp)
        for k in range(n_arr):
            for d, (px, py, pj) in enumerate(_other_chips(x, y)):
                for s, rows in pieces(k):
                    pltpu.make_async_remote_copy(
                        src_ref=srcs[k].at[rows, :], dst_ref=outs[k].at[pj, rows, :], send_sem=send_sems.at[k, d, s],
                        recv_sem=recv_sems.at[k, d, s], device_id=(px, py, c), device_id_type=MESH).wait_recv()
        for cp in sends:
            cp.wait_send()
        for cp in local:
            cp.wait()

    out_shape = [jax.ShapeDtypeStruct((N_CHIPS,) + a.shape, BF16) for a in big]
    out_shape += [jax.ShapeDtypeStruct((N_CHIPS,) + a.shape, F32) for a in small]
    sem = pltpu.SemaphoreType.DMA((n_arr, 3, ICI_SPLIT))
    return pl.pallas_call(
        body, name="gather_weights", in_specs=[VM] * n_arr, out_specs=[ANY] * n_arr, out_shape=out_shape,
        scratch_shapes=[pltpu.VMEM(a.shape, BF16) for a in big] + [sem, sem, pltpu.SemaphoreType.DMA((n_arr,))],
        compiler_params=_cp(VMEM_BIG),
    )(*big, *small)


def _row_chunks(rows, n_split):
    size = rows // n_split
    assert size * n_split == rows and size % 16 == 0, (rows, n_split)
    return [(s, pl.ds(s * size, size)) for s in range(n_split)], size


D2D_SPLIT = 4
ICI_SPLIT = 2


def _sibling_halves(grads):
    n_arr = len(grads)

    def body(*refs):
        ins = refs[:n_arr]
        theirs = refs[n_arr:2 * n_arr]
        send_sems, recv_sems = refs[2 * n_arr:]
        x, y, c = _place()
        copies = []
        for k in range(n_arr):
            half = ins[k].shape[1] // 2
            chunks, size = _row_chunks(half, D2D_SPLIT)
            for s, dst_rows in chunks:
                give = pltpu.make_async_remote_copy(
                    src_ref=ins[k].at[:, pl.ds((1 - c) * half + s * size, size), :], dst_ref=theirs[k].at[:, dst_rows, :],
                    send_sem=send_sems.at[k, s], recv_sem=recv_sems.at[k, s], device_id=(x, y, 1 - c),
                    device_id_type=MESH)
                give.start()
                copies.append(give)
        for give in copies:
            give.wait()

    halves = [jax.ShapeDtypeStruct((g.shape[0], g.shape[1] // 2, g.shape[2]), F32) for g in grads]
    sem = pltpu.SemaphoreType.DMA((n_arr, D2D_SPLIT))
    return pl.pallas_call(
        body, name="sibling_halves", in_specs=[ANY] * n_arr, out_specs=[ANY] * n_arr, out_shape=halves,
        scratch_shapes=[sem, sem],
    )(*grads)


def _chip_exchange(parts):
    n_arr = len(parts)

    def body(*refs):
        ins = refs[:n_arr]
        outs = refs[n_arr:2 * n_arr]
        send_sems, recv_sems = refs[2 * n_arr:]
        x, y, c = _place()
        me = 2 * x + y
        sends = []
        for k in range(n_arr):
            chunks, _ = _row_chunks(ins[k].shape[1], ICI_SPLIT)
            for d, (px, py, pj) in enumerate(_other_chips(x, y)):
                for s, rows in chunks:
                    cp = pltpu.make_async_remote_copy(
                        src_ref=ins[k].at[pj, rows, :], dst_ref=outs[k].at[me, rows, :], send_sem=send_sems.at[k, d, s],
                        recv_sem=recv_sems.at[k, d, s], device_id=(px, py, c), device_id_type=MESH)
                    cp.start()
                    sends.append(cp)
        for k in range(n_arr):
            chunks, _ = _row_chunks(ins[k].shape[1], ICI_SPLIT)
            for d, (px, py, pj) in enumerate(_other_chips(x, y)):
                for s, rows in chunks:
                    pltpu.make_async_remote_copy(
                        src_ref=ins[k].at[pj, rows, :], dst_ref=outs[k].at[pj, rows, :], send_sem=send_sems.at[k, d, s],
                        recv_sem=recv_sems.at[k, d, s], device_id=(px, py, c), device_id_type=MESH).wait_recv()
        for cp in sends:
            cp.wait_send()

    sem = pltpu.SemaphoreType.DMA((n_arr, 3, ICI_SPLIT))
    return pl.pallas_call(
        body, name="chip_exchange", in_specs=[ANY] * n_arr, out_specs=[ANY] * n_arr,
        out_shape=[jax.ShapeDtypeStruct(p.shape, p.dtype) for p in parts],
        scratch_shapes=[sem, sem],
    )(*parts)


def _sibling_join(bufs):
    n_arr = len(bufs)

    def body(*refs):
        bufs_out = refs[n_arr:2 * n_arr]
        send_sems, recv_sems = refs[2 * n_arr:]
        x, y, c = _place()
        copies = []
        for k in range(n_arr):
            half = bufs_out[k].shape[0] // 2
            chunks, size = _row_chunks(half, D2D_SPLIT)
            for s, _ in chunks:
                rows = pl.ds(c * half + s * size, size)
                give = pltpu.make_async_remote_copy(
                    src_ref=bufs_out[k].at[rows, :], dst_ref=bufs_out[k].at[rows, :], send_sem=send_sems.at[k, s],
                    recv_sem=recv_sems.at[k, s], device_id=(x, y, 1 - c), device_id_type=MESH)
                give.start()
                copies.append((k, s, half, size, give))
        for k, s, half, size, give in copies:
            rows = pl.ds((1 - c) * half + s * size, size)
            pltpu.make_async_remote_copy(
                src_ref=bufs_out[k].at[rows, :], dst_ref=bufs_out[k].at[rows, :], send_sem=send_sems.at[k, s],
                recv_sem=recv_sems.at[k, s], device_id=(x, y, 1 - c), device_id_type=MESH).wait_recv()
            give.wait_send()

    sem = pltpu.SemaphoreType.DMA((n_arr, D2D_SPLIT))
    return pl.pallas_call(
        body, name="sibling_join", in_specs=[ANY] * n_arr, out_specs=[ANY] * n_arr,
        out_shape=[jax.ShapeDtypeStruct(b.shape, F32) for b in bufs],
        input_output_aliases={k: k for k in range(n_arr)},
        scratch_shapes=[sem, sem],
    )(*bufs)


PACK_ROWS = 48


def _small_allreduce(pack):
    masks = [(dx, dy, dc) for dx in (0, 1) for dy in (0, 1) for dc in (0, 1)][1:]

    def body(p_ref, o_ref, buf, send_sems, recv_sems):
        x, y, c = _place()
        me = 4 * x + 2 * y + c
        buf[me] = p_ref[...]
        sends = []
        for k, (dx, dy, dc) in enumerate(masks):
            peer = (1 - x if dx else x, 1 - y if dy else y, 1 - c if dc else c)
            cp = pltpu.make_async_remote_copy(
                src_ref=p_ref, dst_ref=buf.at[me], send_sem=send_sems.at[k], recv_sem=recv_sems.at[k],
                device_id=peer, device_id_type=MESH)
            cp.start()
            sends.append(cp)
        for k, (dx, dy, dc) in enumerate(masks):
            peer = (1 - x if dx else x, 1 - y if dy else y, 1 - c if dc else c)
            pj = 4 * peer[0] + 2 * peer[1] + peer[2]
            pltpu.make_async_remote_copy(
                src_ref=p_ref, dst_ref=buf.at[pj], send_sem=send_sems.at[k], recv_sem=recv_sems.at[k],
                device_id=peer, device_id_type=MESH).wait_recv()
        for cp in sends:
            cp.wait_send()
        tot = buf[0]
        for k in range(1, 8):
            tot = tot + buf[k]
        o_ref[...] = tot
        o_ref[0:N_META, :] = tot[0:N_META] + tot[N_META:2 * N_META]

    return pl.pallas_call(
        body, name="small_allreduce", in_specs=[VM], out_specs=VM,
        out_shape=jax.ShapeDtypeStruct((PACK_ROWS, D_MODEL), F32),
        scratch_shapes=[pltpu.VMEM((8, PACK_ROWS, D_MODEL), F32), pltpu.SemaphoreType.DMA((7,)),
                        pltpu.SemaphoreType.DMA((7,))],
    )(pack)


def _pad_lanes(vec, offset):
    k = vec.shape[1]
    return jnp.concatenate([jnp.zeros((1, offset), F32), vec, jnp.zeros((1, LANE - offset - k), F32)], axis=1)


def _local_step(x, tgt, meta, norm1_g, wp, conv_w, a_log, dt_bias, dn_norm_g, gla_w2, gla_b, gla_norm_g,
                w_out, norm2_g, w_up, w_down, final_norm_g):
    bsz, s_len, d = x.shape
    t_seq = s_len + CHUNK
    nc_seq = t_seq // CHUNK
    n = bsz * t_seq
    tr = _tile(t_seq, 832)
    tt = _tile(t_seq, 416)

    lead = jnp.concatenate([jnp.zeros((N_PAD, d), F32), meta], axis=0)
    h0 = jnp.concatenate([jnp.broadcast_to(lead[None], (bsz, CHUNK, d)), x], axis=1).reshape(n, d)
    tgt_p = jnp.concatenate([jnp.zeros((bsz, CHUNK, d), F32), tgt], axis=1).reshape(n, d)
    alog_row = _pad_lanes(a_log, 4)
    dtb_row = _pad_lanes(dt_bias, 4)
    w2p = jnp.concatenate([gla_w2, jnp.zeros((LANE - GLA_RANK, GQ_W), F32)], axis=0)

    h = _rms_fwd(h0, norm1_g, tr=tr, name="norm1")
    (projp,) = _mm(h, wp, "nn", tm=tr, tn=1280, tk=d, out_dtypes=(F32,), name="in_proj")
    qn, kn, v = _dnprep_fwd(projp, conv_w, bsz=bsz, t_seq=t_seq, tt=tt, name="dn_prep")
    u, w, qg, kd, pmat, tmat, gl = _dn_intra_fwd(qn, kn, v, projp, alog_row, dtb_row, nc_seq=nc_seq, name="dn_intra")
    o_dn, vn, hist = _dn_scan_fwd(u, w, qg, kd, pmat, gl, bsz=bsz, nc_seq=nc_seq, name="dn_scan")
    oi, gqg, gkd, ggl = _gla_intra_fwd(projp, w2p, gla_b, nc_seq=nc_seq, name="gla_intra")
    o_gla, ghist = _gla_scan_fwd(oi, gqg, gkd, ggl, projp, bsz=bsz, nc_seq=nc_seq, name="gla_scan")
    mix = _gnorm_fwd(o_dn, o_gla, projp, dn_norm_g, gla_norm_g, tr=tr, name="gated_norm")
    (x1,) = _mm(mix, w_out, "nn", tm=tr, tn=d, tk=d, out_dtypes=(F32,), extras=(h0,),
                epilogue=lambda acc, res: (res + acc,), name="out_proj")
    h2 = _rms_fwd(x1, norm2_g, tr=tr, name="norm2")

    def act_epilogue(acc):
        r = jnp.maximum(acc, 0.0)
        return acc, r * r

    up, act = _mm(h2, w_up, "nn", tm=tr, tn=1024, tk=d, out_dtypes=(BF16, BF16), epilogue=act_epilogue, name="mlp_up")
    (x2,) = _mm(act, w_down, "nn", tm=tr, tn=d, tk=1024, out_dtypes=(F32,), extras=(x1,),
                epilogue=lambda acc, res: (res + acc,), name="mlp_down")
    dx2, d_final_g, loss_tile = _final_loss(x2, final_norm_g, tgt_p, t_seq=t_seq, tr=tr, name="final_loss")

    (dup,) = _mm(dx2, w_down, "nt", tm=tr, tn=1024, tk=d, out_dtypes=(BF16,), extras=(up,),
                 epilogue=lambda acc, upv: (acc * (2.0 * jnp.maximum(upv.astype(F32), 0.0)),), name="mlp_down_bwd")
    (d_w_down,) = _mm(act, dx2, "tn", tm=1024, tn=d, tk=tr, out_dtypes=(F32,), name="w_down_grad")
    (d_w_up,) = _mm(h2, dup, "tn", tm=d, tn=1024, tk=tr, out_dtypes=(F32,), name="w_up_grad")
    (dh2,) = _mm(dup, w_up, "nt", tm=tr, tn=d, tk=1024, out_dtypes=(F32,), name="mlp_up_bwd")
    dx1, d_norm2_g = _rms_bwd_add(x1, norm2_g, dh2, dx2, tr=tr, name="norm2_bwd")

    (dmix,) = _mm(dx1, w_out, "nt", tm=tr, tn=d, tk=d, out_dtypes=(F32,), name="out_proj_bwd")
    (d_w_out,) = _mm(mix, dx1, "tn", tm=d, tn=d, tk=tr, out_dtypes=(F32,), name="w_out_grad")
    do_dn, ddz, do_gla, dgr, d_dn_norm_g, d_gla_norm_g = _gnorm_bwd(
        dmix, o_dn, o_gla, projp, dn_norm_g, gla_norm_g, tr=tr, name="gated_norm_bwd")
    du, dw, dqg, dkd, dpm, dgl = _dn_scan_bwd(do_dn, w, qg, kd, vn, pmat, gl, hist, bsz=bsz, nc_seq=nc_seq,
                                               name="dn_scan_bwd")
    dqn, dkn, dv, dsa, d_alog, d_dtb = _dn_intra_bwd(qn, kn, v, projp, alog_row, dtb_row, u, w, tmat,
                                                     du, dw, dqg, dkd, dpm, dgl, nc_seq=nc_seq, name="dn_intra_bwd")
    dz, d_conv_w = _dnprep_bwd_a(projp, conv_w, dqn, dkn, dv, bsz=bsz, t_seq=t_seq, tt=tt, name="dn_prep_bwd")
    dcin = _dnprep_bwd_b(dz, conv_w, bsz=bsz, t_seq=t_seq, tt=tt, name="conv_bwd")
    gdqg, gdkd, gdvi, gdgl = _gla_scan_bwd(do_gla, gqg, gkd, ggl, projp, ghist, bsz=bsz, nc_seq=nc_seq,
                                            name="gla_scan_bwd")
    dgqk, dgv, dsb, d_w2p, d_gla_b = _gla_intra_bwd(projp, w2p, gla_b, do_gla, gdqg, gdkd, gdvi, gdgl,
                                                    nc_seq=nc_seq, name="gla_intra_bwd")

    secs = (dcin, ddz, dgqk, dgv, dgr, dsa, dsb)
    d_wp_secs = []
    for idx, (sec, (_, wd)) in enumerate(zip(secs, SECTIONS)):
        (g_sec,) = _mm(h, sec, "tn", tm=d, tn=min(wd, 512), tk=tr, out_dtypes=(F32,), name=f"w_in_grad_{idx}")
        d_wp_secs.append(g_sec)
    dh0, d_norm1_g = _inproj_bwd(secs, wp, h0, norm1_g, dx1, tr=tt, name="in_proj_bwd")
    dh0 = dh0.reshape(bsz, t_seq, d)
    grad_x = dh0[:, CHUNK:]
    d_meta_rows = dh0[:, N_PAD:CHUNK].reshape(bsz * N_META, d)

    g_qkv, g_dz, g_gqk, g_gv, g_gr, g_sa, g_sb = d_wp_secs
    d_w_in = jnp.concatenate([g_qkv, g_dz, g_sa[:, 0:8], g_gqk, g_gv, g_gr, g_sb[:, 0:GLA_RANK]], axis=1)
    grads = dict(w_in=d_w_in, w_out=d_w_out, w_up=d_w_up, w_down=d_w_down, meta_rows=d_meta_rows,
                 norm1_g=d_norm1_g, conv_w=d_conv_w, a_log_tile=d_alog, dt_bias_tile=d_dtb, dn_norm_g=d_dn_norm_g,
                 gla_w2=d_w2p[0:GLA_RANK], gla_b=d_gla_b, gla_norm_g=d_gla_norm_g, norm2_g=d_norm2_g,
                 final_norm_g=d_final_g, loss_tile=loss_tile)
    return grad_x, grads


def _pad_layout(w_full):
    z = lambda k: jnp.zeros((w_full.shape[0], k), w_full.dtype)
    return jnp.concatenate([w_full[:, 0:2048], w_full[:, 2056:3592], w_full[:, 2048:2056], z(LANE - 8),
                            w_full[:, 3592:3608], z(LANE - GLA_RANK)], axis=1)


def _pack_small(g, bsz):
    assert bsz * N_META == 32
    row = jnp.concatenate([g["a_log_tile"], g["dt_bias_tile"], g["dn_norm_g"], g["gla_norm_g"], g["gla_b"],
                           g["loss_tile"], jnp.zeros((1, LANE), F32)], axis=1)
    return jnp.concatenate([g["meta_rows"], g["norm1_g"], g["conv_w"].reshape(6, D_MODEL), row,
                            g["gla_w2"].reshape(4, D_MODEL), g["norm2_g"], g["final_norm_g"],
                            jnp.zeros((2, D_MODEL), F32)], axis=0)


def kernel(x, meta_tokens, norm1_g, w_in, conv_w, a_log, dt_bias, dn_norm_g, gla_w2, gla_b, gla_norm_g, w_out, norm2_g, w_up, w_down, final_norm_g, loss_target, m_meta_tokens, m_norm1_g, m_w_in, m_conv_w, m_a_log, m_dt_bias, m_dn_norm_g, m_gla_w2, m_gla_b, m_gla_norm_g, m_w_out, m_norm2_g, m_w_up, m_w_down, m_final_norm_g, v_meta_tokens, v_norm1_g, v_w_in, v_conv_w, v_a_log, v_dt_bias, v_dn_norm_g, v_gla_w2, v_gla_b, v_gla_norm_g, v_w_out, v_norm2_g, v_w_up, v_w_down, v_final_norm_g):
    bsz = x.shape[0]
    chip = 2 * lax.axis_index("x") + lax.axis_index("y")

    shard_w = IN_WIDTH // N_CHIPS
    lane_pad = lambda a, wd: jnp.pad(a, ((0, 0), (0, wd - a.shape[1])))
    g_in, g_out, g_up, g_down, g_meta, g_conv, g_w2 = _gather_weights(
        lane_pad(w_in[0], D_MODEL), w_out[0], w_up[0], w_down[0], meta_tokens, conv_w[0], lane_pad(gla_w2[0], LANE))
    wp = _pad_layout(g_in[:, :, 0:shard_w].transpose(1, 0, 2).reshape(D_MODEL, IN_WIDTH))
    w_out_f = g_out.reshape(D_MODEL, D_MODEL)
    w_up_f = g_up.transpose(1, 0, 2).reshape(D_MODEL, D_FF)
    w_down_f = g_down.reshape(D_FF, D_MODEL)
    meta_f = g_meta.transpose(1, 0, 2).reshape(N_META, D_MODEL)
    conv_f = g_conv.transpose(1, 0, 2).reshape(4, QKV_W)
    w2_f = g_w2[:, :, 0:GQ_W // N_CHIPS].transpose(1, 0, 2).reshape(GLA_RANK, GQ_W)

    grad_x, g = _local_step(x, loss_target, meta_f, norm1_g, wp, conv_f, a_log, dt_bias, dn_norm_g, w2_f, gla_b,
                            gla_norm_g, w_out_f, norm2_g, w_up_f, w_down_f, final_norm_g.reshape(1, D_MODEL))

    shard_major = [
        jnp.pad(g["w_in"].reshape(D_MODEL, N_CHIPS, shard_w).transpose(1, 0, 2),
                ((0, 0), (0, 0), (0, D_MODEL - shard_w))),
        g["w_out"].reshape(N_CHIPS, D_MODEL // N_CHIPS, D_MODEL),
        g["w_up"].reshape(D_MODEL, N_CHIPS, D_FF // N_CHIPS).transpose(1, 0, 2),
        g["w_down"].reshape(N_CHIPS, D_FF // N_CHIPS, D_MODEL),
    ]
    where = jnp.stack([lax.axis_index("c"), chip]).astype(jnp.int32)
    theirs = _sibling_halves(shard_major)
    pair = [_pair_sum(where, a, b, name=f"pair_sum_{k}") for k, (a, b) in enumerate(zip(shard_major, theirs))]
    parts = _chip_exchange(pair)
    halves = [_chip_sum(where, p, q, name=f"chip_sum_{k}") for k, (p, q) in enumerate(zip(pair, parts))]
    gw_in, gw_out, gw_up, gw_down = _sibling_join(halves)
    gw_in = gw_in[:, 0:shard_w]

    red = _small_allreduce(_pack_small(g, bsz))
    g_meta_full = red[0:N_META]
    g_norm1 = red[32:33]
    g_conv_full = red[33:39].reshape(4, QKV_W)
    srow = red[39:40]
    g_alog, g_dtb = srow[:, 4:8], srow[:, LANE + 4:LANE + 8]
    g_dn_norm, g_gla_norm = srow[:, 2 * LANE:3 * LANE], srow[:, 3 * LANE:4 * LANE]
    g_gla_b = srow[:, 4 * LANE:6 * LANE]
    loss = srow[0, 6 * LANE]
    g_w2_full = red[40:44].reshape(GLA_RANK, GQ_W)
    g_norm2 = red[44:45]
    g_final = red[45:46]
    g_meta_sh = lax.dynamic_slice_in_dim(g_meta_full, chip * (D_MODEL // N_CHIPS), D_MODEL // N_CHIPS, axis=1)
    g_conv_sh = lax.dynamic_slice_in_dim(g_conv_full, chip * (QKV_W // N_CHIPS), QKV_W // N_CHIPS, axis=1)
    g_w2_sh = lax.dynamic_slice_in_dim(g_w2_full, chip * (GQ_W // N_CHIPS), GQ_W // N_CHIPS, axis=1)

    names = ["meta_tokens", "norm1_g", "w_in", "conv_w", "a_log", "dt_bias", "dn_norm_g", "gla_w2", "gla_b",
             "gla_norm_g", "w_out", "norm2_g", "w_up", "w_down", "final_norm_g"]
    weights = dict(meta_tokens=meta_tokens, norm1_g=norm1_g, w_in=w_in, conv_w=conv_w, a_log=a_log, dt_bias=dt_bias,
                   dn_norm_g=dn_norm_g, gla_w2=gla_w2, gla_b=gla_b, gla_norm_g=gla_norm_g, w_out=w_out,
                   norm2_g=norm2_g, w_up=w_up, w_down=w_down, final_norm_g=final_norm_g)
    ms = dict(meta_tokens=m_meta_tokens, norm1_g=m_norm1_g, w_in=m_w_in, conv_w=m_conv_w, a_log=m_a_log,
              dt_bias=m_dt_bias, dn_norm_g=m_dn_norm_g, gla_w2=m_gla_w2, gla_b=m_gla_b, gla_norm_g=m_gla_norm_g,
              w_out=m_w_out, norm2_g=m_norm2_g, w_up=m_w_up, w_down=m_w_down, final_norm_g=m_final_norm_g)
    vs = dict(meta_tokens=v_meta_tokens, norm1_g=v_norm1_g, w_in=v_w_in, conv_w=v_conv_w, a_log=v_a_log,
              dt_bias=v_dt_bias, dn_norm_g=v_dn_norm_g, gla_w2=v_gla_w2, gla_b=v_gla_b, gla_norm_g=v_gla_norm_g,
              w_out=v_w_out, norm2_g=v_norm2_g, w_up=v_w_up, w_down=v_w_down, final_norm_g=v_final_norm_g)
    grads2d = dict(meta_tokens=g_meta_sh, norm1_g=g_norm1, w_in=gw_in, conv_w=g_conv_sh, a_log=g_alog, dt_bias=g_dtb,
                   dn_norm_g=g_dn_norm, gla_w2=g_w2_sh, gla_b=g_gla_b, gla_norm_g=g_gla_norm, w_out=gw_out,
                   norm2_g=g_norm2, w_up=gw_up, w_down=gw_down, final_norm_g=g_final)
    out_g, out_d, out_m, out_v = [], [], [], []
    for nm in names:
        shape = weights[nm].shape
        g2 = grads2d[nm]
        as2d = lambda a: a.reshape(g2.shape)
        dlt, nm_, nv_ = _adamw(as2d(weights[nm]), g2, as2d(ms[nm]), as2d(vs[nm]), name=f"adamw_{nm}")
        out_g.append(g2.reshape(shape))
        out_d.append(dlt.reshape(shape))
        out_m.append(nm_.reshape(shape))
        out_v.append(nv_.reshape(shape))
    return (loss, grad_x, *out_g, *out_d, *out_m, *out_v)
```

```python
import functools

import jax
import jax.numpy as jnp
import numpy as np
from jax import lax
from jax.experimental import pallas as pl
from jax.experimental.pallas import tpu as pltpu

F32 = jnp.float32
BF16 = jnp.bfloat16
HI = lax.Precision.HIGHEST
MESH = pl.DeviceIdType.MESH

D_MODEL = 1024
N_META = 16
CHUNK = 64
N_PAD = CHUNK - N_META
NH = 4
DN_D = 128
GLA_DK = 64
GLA_DV = 128
GLA_RANK = 16
D_FF = 4 * D_MODEL
EPS = 1e-6
IN_WIDTH = 3608
C_QKV, C_DZ, C_GQK, C_GV, C_GR, C_SA, C_SB, PW = 0, 1536, 2048, 2560, 3072, 3584, 3712, 3840
LANE = 128
N_CHIPS = 4

ADAM_LR, ADAM_B1, ADAM_B2, ADAM_EPS, ADAM_WD, ADAM_STEP = 0.001, 0.9, 0.999, 1e-08, 0.01, 10

VMEM_BIG = 56 * 1024 * 1024


def _cp(vmem=None, sem=None):
    kw = {}
    if vmem is not None:
        kw["vmem_limit_bytes"] = vmem
    if sem is not None:
        kw["dimension_semantics"] = sem
    return pltpu.CompilerParams(**kw)


def _tile(n, target, mult=16):
    best = None
    for t in range(mult, min(n, target) + 1, mult):
        if n % t == 0:
            best = t
    assert best is not None, (n, target)
    return best


def _dot(a, b, dims, prec=None):
    return lax.dot_general(a, b, (dims, ((), ())), preferred_element_type=F32, precision=prec)


def _nn(a, b):
    return _dot(a.astype(BF16), b.astype(BF16), ((1,), (0,)))


def _nt(a, b):
    return _dot(a.astype(BF16), b.astype(BF16), ((1,), (1,)))


def _tn(a, b):
    return _dot(a.astype(BF16), b.astype(BF16), ((0,), (0,)))


def _nn_hi(a, b):
    return _dot(a, b, ((1,), (0,)), HI)


def _nt_hi(a, b):
    return _dot(a, b, ((1,), (1,)), HI)


def _tn_hi(a, b):
    return _dot(a, b, ((0,), (0,)), HI)


def _sigmoid(x):
    return 1.0 / (1.0 + jnp.exp(-x))


def _softplus(x):
    return jnp.maximum(x, 0.0) + jnp.log(1.0 + jnp.exp(-jnp.abs(x)))


def _logsigmoid(x):
    return -_softplus(-x)


def _iota2(shape, dim):
    return lax.broadcasted_iota(jnp.int32, shape, dim)


def _mm(a, b, mode, *, tm, tn, tk, out_dtypes, extras=(), epilogue=None, name, vmem=VMEM_BIG):
    if mode == "tn":
        K, M = a.shape
    else:
        M, K = a.shape
    N = b.shape[0] if mode == "nt" else b.shape[1]
    assert M % tm == 0 and N % tn == 0 and K % tk == 0, (name, M, N, K, tm, tn, tk)
    nk = K // tk
    n_ex, n_out = len(extras), len(out_dtypes)
    if mode == "tn":
        a_spec = pl.BlockSpec((tk, tm), lambda i, j, k: (k, i))
    else:
        a_spec = pl.BlockSpec((tm, tk), lambda i, j, k: (i, k))
    if mode == "nt":
        b_spec = pl.BlockSpec((tn, tk), lambda i, j, k: (j, k))
    else:
        b_spec = pl.BlockSpec((tk, tn), lambda i, j, k: (k, j))
    mn_spec = pl.BlockSpec((tm, tn), lambda i, j, k: (i, j))
    dims = {"nn": ((1,), (0,)), "nt": ((1,), (1,)), "tn": ((0,), (0,))}[mode]

    single = nk == 1
    direct = (not single) and epilogue is None and n_out == 1 and out_dtypes[0] == F32

    def body(*refs):
        a_ref, b_ref = refs[0], refs[1]
        ex_refs = refs[2:2 + n_ex]
        out_refs = refs[2 + n_ex:2 + n_ex + n_out]
        part = _dot(a_ref[...].astype(BF16), b_ref[...].astype(BF16), dims)

        def finish(acc):
            res = (acc,) if epilogue is None else epilogue(acc, *[e[...] for e in ex_refs])
            for o_ref, r in zip(out_refs, res):
                o_ref[...] = r.astype(o_ref.dtype)

        if single:
            finish(part)
            return
        acc_ref = out_refs[0] if direct else refs[2 + n_ex + n_out]
        k = pl.program_id(2)

        @pl.when(k == 0)
        def _():
            acc_ref[...] = part

        @pl.when(k > 0)
        def _():
            acc_ref[...] += part

        if not direct:
            @pl.when(k == nk - 1)
            def _():
                finish(acc_ref[...])

    outs = pl.pallas_call(
        body, name=name, grid=(M // tm, N // tn, nk),
        in_specs=[a_spec, b_spec] + [mn_spec] * n_ex,
        out_specs=[mn_spec] * n_out,
        out_shape=[jax.ShapeDtypeStruct((M, N), dt) for dt in out_dtypes],
        scratch_shapes=[] if (single or direct) else [pltpu.VMEM((tm, tn), F32)],
        compiler_params=_cp(vmem, ("parallel", "parallel", "arbitrary")),
    )(a, b, *extras)
    return tuple(outs)


def _grad_tn(a, secs, *, tk, name):
    kk, m = a.shape
    widths = [s.shape[1] for s in secs]
    total = sum(widths)
    nk = kk // tk

    def body(*refs):
        a_ref, sec_refs, o_ref = refs[0], refs[1:-1], refs[-1]
        cat = sec_refs[0][...] if len(sec_refs) == 1 else jnp.concatenate([s[...] for s in sec_refs], axis=1)
        part = _dot(a_ref[...].astype(BF16), cat.astype(BF16), ((0,), (0,)))
        k = pl.program_id(0)

        @pl.when(k == 0)
        def _():
            o_ref[...] = part

        @pl.when(k > 0)
        def _():
            o_ref[...] += part

    return pl.pallas_call(
        body, name=name, grid=(nk,),
        in_specs=[pl.BlockSpec((tk, m), lambda k: (k, 0))] + [pl.BlockSpec((tk, w), lambda k: (k, 0)) for w in widths],
        out_specs=pl.BlockSpec((m, total), lambda k: (0, 0)),
        out_shape=jax.ShapeDtypeStruct((m, total), F32),
        compiler_params=_cp(VMEM_BIG, ("arbitrary",)),
    )(a, *secs)


def _rms_fwd(x, g, *, tr, name):
    n, d = x.shape

    def body(x_ref, g_ref, o_ref):
        xv = x_ref[...]
        r = lax.rsqrt(jnp.mean(xv * xv, axis=-1, keepdims=True) + EPS)
        o_ref[...] = (xv * r * g_ref[...]).astype(o_ref.dtype)

    return pl.pallas_call(
        body, name=name, grid=(n // tr,),
        in_specs=[pl.BlockSpec((tr, d), lambda i: (i, 0)), pl.BlockSpec((1, d), lambda i: (0, 0))],
        out_specs=pl.BlockSpec((tr, d), lambda i: (i, 0)),
        out_shape=jax.ShapeDtypeStruct((n, d), BF16),
        compiler_params=_cp(VMEM_BIG),
    )(x, g)


def _rms_bwd_math(xv, g, dy):
    r = lax.rsqrt(jnp.mean(xv * xv, axis=-1, keepdims=True) + EPS)
    xh = xv * r
    gdy = dy * g
    dx = r * (gdy - xh * jnp.mean(xh * gdy, axis=-1, keepdims=True))
    return dx, jnp.sum(dy * xh, axis=0, keepdims=True)


def _rms_bwd_add(x, g, dy, res, *, tr, name):
    n, d = x.shape

    def body(x_ref, g_ref, dy_ref, res_ref, o_ref, ob_ref, dg_ref):
        dx, dg = _rms_bwd_math(x_ref[...], g_ref[...], dy_ref[...])
        tot = res_ref[...] + dx
        o_ref[...] = tot
        ob_ref[...] = tot.astype(BF16)

        @pl.when(pl.program_id(0) == 0)
        def _():
            dg_ref[...] = dg

        @pl.when(pl.program_id(0) > 0)
        def _():
            dg_ref[...] += dg

    row = pl.BlockSpec((tr, d), lambda i: (i, 0))
    vec = pl.BlockSpec((1, d), lambda i: (0, 0))
    return pl.pallas_call(
        body, name=name, grid=(n // tr,),
        in_specs=[row, vec, row, row], out_specs=[row, row, vec],
        out_shape=[jax.ShapeDtypeStruct((n, d), F32), jax.ShapeDtypeStruct((n, d), BF16),
                   jax.ShapeDtypeStruct((1, d), F32)],
        compiler_params=_cp(VMEM_BIG),
    )(x, g, dy, res)


def _final_loss(x2, gf, tgt, *, t_seq, tr, name):
    n, d = x2.shape
    per_seq = t_seq // tr

    def body(x_ref, g_ref, t_ref, dx_ref, dxb_ref, dg_ref, loss_ref):
        i = pl.program_id(0)
        xv = x_ref[...]
        g = g_ref[...]
        r = lax.rsqrt(jnp.mean(xv * xv, axis=-1, keepdims=True) + EPS)
        xh = xv * r
        pos = (i % per_seq) * tr + _iota2((tr, 1), 0)
        real = pos >= CHUNK
        err = jnp.where(real, xh * g - t_ref[...], 0.0)
        dy = err * (1.0 / d)
        gdy = dy * g
        dx = r * (gdy - xh * jnp.mean(xh * gdy, axis=-1, keepdims=True))
        dx_ref[...] = dx
        dxb_ref[...] = dx.astype(BF16)
        dg = jnp.sum(dy * xh, axis=0, keepdims=True)
        ls = 0.5 * jnp.sum(jnp.mean(err * err, axis=-1, keepdims=True), axis=0, keepdims=True)
        ls = jnp.where(_iota2((1, LANE), 1) == 0, ls, 0.0)

        @pl.when(i == 0)
        def _():
            dg_ref[...] = dg
            loss_ref[...] = ls

        @pl.when(i > 0)
        def _():
            dg_ref[...] += dg
            loss_ref[...] += ls

    row = pl.BlockSpec((tr, d), lambda i: (i, 0))
    vec = pl.BlockSpec((1, d), lambda i: (0, 0))
    one = pl.BlockSpec((1, LANE), lambda i: (0, 0))
    return pl.pallas_call(
        body, name=name, grid=(n // tr,),
        in_specs=[row, vec, row], out_specs=[row, row, vec, one],
        out_shape=[jax.ShapeDtypeStruct((n, d), F32), jax.ShapeDtypeStruct((n, d), BF16),
                   jax.ShapeDtypeStruct((1, d), F32), jax.ShapeDtypeStruct((1, LANE), F32)],
        compiler_params=_cp(VMEM_BIG),
    )(x2, gf, tgt)


def _gnorm_fwd(o_dn, o_gla, projp, g_dn, g_gla, *, tr, name):
    n = o_dn.shape[0]
    w = NH * DN_D

    def body(odn_ref, ogl_ref, z_ref, r_ref, gdn_ref, ggl_ref, mix_ref):
        for grp, (o_ref, gate_ref, gain_ref) in enumerate(((odn_ref, z_ref, gdn_ref), (ogl_ref, r_ref, ggl_ref))):
            gain = gain_ref[...]
            for h in range(NH):
                sl = slice(h * DN_D, (h + 1) * DN_D)
                o = o_ref[:, sl]
                z = gate_ref[:, sl]
                r = lax.rsqrt(jnp.mean(o * o, axis=-1, keepdims=True) + EPS)
                y = (o * r * gain) * (z * _sigmoid(z))
                mix_ref[:, grp * w + h * DN_D: grp * w + (h + 1) * DN_D] = y.astype(mix_ref.dtype)

    row = pl.BlockSpec((tr, w), lambda i: (i, 0))
    vec = pl.BlockSpec((1, DN_D), lambda i: (0, 0))
    return pl.pallas_call(
        body, name=name, grid=(n // tr,),
        in_specs=[row, row, pl.BlockSpec((tr, w), lambda i: (i, C_DZ // w)),
                  pl.BlockSpec((tr, w), lambda i: (i, C_GR // w)), vec, vec],
        out_specs=pl.BlockSpec((tr, 2 * w), lambda i: (i, 0)),
        out_shape=jax.ShapeDtypeStruct((n, 2 * w), BF16),
        compiler_params=_cp(VMEM_BIG),
    )(o_dn, o_gla, projp, projp, g_dn, g_gla)


def _gnorm_bwd(dmix, o_dn, o_gla, projp, g_dn, g_gla, *, tr, name):
    n = o_dn.shape[0]
    w = NH * DN_D

    def body(dm_ref, odn_ref, ogl_ref, z_ref, r_ref, gdn_ref, ggl_ref,
             dodn_ref, ddz_ref, dogl_ref, dgr_ref, dgdn_ref, dggl_ref):
        first = pl.program_id(0) == 0
        groups = ((odn_ref, z_ref, gdn_ref, dodn_ref, ddz_ref, dgdn_ref),
                  (ogl_ref, r_ref, ggl_ref, dogl_ref, dgr_ref, dggl_ref))
        for grp, (o_ref, gate_ref, gain_ref, do_ref, dgate_ref, dgain_ref) in enumerate(groups):
            gain = gain_ref[...]
            dgain = jnp.zeros((1, DN_D), F32)
            for h in range(NH):
                sl = slice(h * DN_D, (h + 1) * DN_D)
                o = o_ref[:, sl]
                z = gate_ref[:, sl]
                dm = dm_ref[:, grp * w + h * DN_D: grp * w + (h + 1) * DN_D]
                r = lax.rsqrt(jnp.mean(o * o, axis=-1, keepdims=True) + EPS)
                oh = o * r
                s = _sigmoid(z)
                dn = dm * (z * s)
                dgate_ref[:, sl] = (dm * (oh * gain) * (s * (1.0 + z * (1.0 - s)))).astype(dgate_ref.dtype)
                gdn = dn * gain
                do_ref[:, sl] = r * (gdn - oh * jnp.mean(oh * gdn, axis=-1, keepdims=True))
                dgain = dgain + jnp.sum(dn * oh, axis=0, keepdims=True)

            @pl.when(first)
            def _():
                dgain_ref[...] = dgain

            @pl.when(jnp.logical_not(first))
            def _():
                dgain_ref[...] += dgain

    row = pl.BlockSpec((tr, w), lambda i: (i, 0))
    vec = pl.BlockSpec((1, DN_D), lambda i: (0, 0))
    big = jax.ShapeDtypeStruct((n, w), F32)
    gate = jax.ShapeDtypeStruct((n, w), BF16)
    small = jax.ShapeDtypeStruct((1, DN_D), F32)
    return pl.pallas_call(
        body, name=name, grid=(n // tr,),
        in_specs=[pl.BlockSpec((tr, 2 * w), lambda i: (i, 0)), row, row,
                  pl.BlockSpec((tr, w), lambda i: (i, C_DZ // w)), pl.BlockSpec((tr, w), lambda i: (i, C_GR // w)), vec, vec],
        out_specs=[row, row, row, row, vec, vec],
        out_shape=[big, gate, big, gate, small, small],
        compiler_params=_cp(VMEM_BIG),
    )(dmix, o_dn, o_gla, projp, projp, g_dn, g_gla)


QKV_W = 3 * NH * DN_D
HALO = 8


def _conv_z(xs_ref, cw_ref, tt):
    z = cw_ref[0:1, :] * xs_ref[pl.ds(HALO - 3, tt), :]
    for j in range(1, 4):
        z = z + cw_ref[j:j + 1, :] * xs_ref[pl.ds(HALO - 3 + j, tt), :]
    return z


def _dnprep_fwd(projp, conv_w, *, bsz, t_seq, tt, name):
    n = bsz * t_seq
    per_seq = t_seq // tt
    hw = NH * DN_D

    def body(x_ref, halo_ref, cw_ref, q_ref, k_ref, v_ref, xs_ref):
        i = pl.program_id(1)
        xs_ref[0:HALO, :] = jnp.where(i == 0, 0.0, halo_ref[...])
        xs_ref[HALO:HALO + tt, :] = x_ref[...]
        z = _conv_z(xs_ref, cw_ref, tt)
        a = z * _sigmoid(z)
        for grp, o_ref in enumerate((q_ref, k_ref)):
            for h in range(NH):
                ah = a[:, grp * hw + h * DN_D: grp * hw + (h + 1) * DN_D]
                rs = lax.rsqrt(jnp.sum(ah * ah, axis=-1, keepdims=True) + EPS)
                o_ref[:, h * DN_D:(h + 1) * DN_D] = ah * rs
        v_ref[...] = a[:, 2 * hw:3 * hw]

    def halo_map(b, i):
        return (jnp.maximum((b * t_seq + i * tt) // HALO - 1, 0), 0)

    out = pl.BlockSpec((tt, hw), lambda b, i: (b * per_seq + i, 0))
    sds = jax.ShapeDtypeStruct((n, hw), F32)
    return pl.pallas_call(
        body, name=name, grid=(bsz, per_seq),
        in_specs=[pl.BlockSpec((tt, QKV_W), lambda b, i: (b * per_seq + i, 0)),
                  pl.BlockSpec((HALO, QKV_W), halo_map),
                  pl.BlockSpec((4, QKV_W), lambda b, i: (0, 0))],
        out_specs=[out, out, out], out_shape=[sds, sds, sds],
        scratch_shapes=[pltpu.VMEM((tt + HALO, QKV_W), F32)],
        compiler_params=_cp(VMEM_BIG),
    )(projp, projp, conv_w)


def _dnprep_bwd_a(projp, conv_w, dq, dk, dv, *, bsz, t_seq, tt, name):
    n = bsz * t_seq
    per_seq = t_seq // tt
    hw = NH * DN_D

    def body(x_ref, halo_ref, cw_ref, dq_ref, dk_ref, dv_ref, dz_ref, dcw_ref, xs_ref):
        b, i = pl.program_id(0), pl.program_id(1)
        xs_ref[0:HALO, :] = jnp.where(i == 0, 0.0, halo_ref[...])
        xs_ref[HALO:HALO + tt, :] = x_ref[...]
        z = _conv_z(xs_ref, cw_ref, tt)
        s = _sigmoid(z)
        a = z * s
        dsilu = s * (1.0 + z * (1.0 - s))
        for grp, d_ref in enumerate((dq_ref, dk_ref)):
            for h in range(NH):
                sl = slice(grp * hw + h * DN_D, grp * hw + (h + 1) * DN_D)
                ah = a[:, sl]
                rs = lax.rsqrt(jnp.sum(ah * ah, axis=-1, keepdims=True) + EPS)
                y = ah * rs
                dy = d_ref[:, h * DN_D:(h + 1) * DN_D]
                da = rs * (dy - y * jnp.sum(dy * y, axis=-1, keepdims=True))
                dz_ref[:, sl] = da * dsilu[:, sl]
        dz_ref[:, 2 * hw:3 * hw] = dv_ref[...] * dsilu[:, 2 * hw:3 * hw]
        dz = dz_ref[...]
        first = jnp.logical_and(b == 0, i == 0)
        for j in range(4):
            part = jnp.sum(dz * xs_ref[pl.ds(HALO - 3 + j, tt), :], axis=0, keepdims=True)

            @pl.when(first)
            def _():
                dcw_ref[j:j + 1, :] = part

            @pl.when(jnp.logical_not(first))
            def _():
                dcw_ref[j:j + 1, :] += part

    def halo_map(b, i):
        return (jnp.maximum((b * t_seq + i * tt) // HALO - 1, 0), 0)

    hrow = pl.BlockSpec((tt, hw), lambda b, i: (b * per_seq + i, 0))
    return pl.pallas_call(
        body, name=name, grid=(bsz, per_seq),
        in_specs=[pl.BlockSpec((tt, QKV_W), lambda b, i: (b * per_seq + i, 0)),
                  pl.BlockSpec((HALO, QKV_W), halo_map),
                  pl.BlockSpec((4, QKV_W), lambda b, i: (0, 0)), hrow, hrow, hrow],
        out_specs=[pl.BlockSpec((tt, QKV_W), lambda b, i: (b * per_seq + i, 0)),
                   pl.BlockSpec((4, QKV_W), lambda b, i: (0, 0))],
        out_shape=[jax.ShapeDtypeStruct((n, QKV_W), F32), jax.ShapeDtypeStruct((4, QKV_W), F32)],
        scratch_shapes=[pltpu.VMEM((tt + HALO, QKV_W), F32)],
        compiler_params=_cp(VMEM_BIG),
    )(projp, projp, conv_w, dq, dk, dv)


def _dnprep_bwd_b(dz, conv_w, *, bsz, t_seq, tt, name):
    n = bsz * t_seq
    per_seq = t_seq // tt
    last_blk = n // HALO - 1

    def body(dz_ref, halo_ref, cw_ref, dx_ref, ds_ref):
        i = pl.program_id(1)
        ds_ref[0:tt, :] = dz_ref[...]
        ds_ref[tt:tt + HALO, :] = jnp.where(i == per_seq - 1, 0.0, halo_ref[...])
        dx = cw_ref[0:1, :] * ds_ref[pl.ds(3, tt), :]
        for j in range(1, 4):
            dx = dx + cw_ref[j:j + 1, :] * ds_ref[pl.ds(3 - j, tt), :]
        dx_ref[...] = dx.astype(dx_ref.dtype)

    def halo_map(b, i):
        return (jnp.minimum((b * t_seq + (i + 1) * tt) // HALO, last_blk), 0)

    row = pl.BlockSpec((tt, QKV_W), lambda b, i: (b * per_seq + i, 0))
    return pl.pallas_call(
        body, name=name, grid=(bsz, per_seq),
        in_specs=[row, pl.BlockSpec((HALO, QKV_W), halo_map), pl.BlockSpec((4, QKV_W), lambda b, i: (0, 0))],
        out_specs=row, out_shape=jax.ShapeDtypeStruct((n, QKV_W), BF16),
        scratch_shapes=[pltpu.VMEM((tt + HALO, QKV_W), F32)],
        compiler_params=_cp(VMEM_BIG),
    )(dz, dz, conv_w)


def _masks64():
    r = _iota2((CHUNK, CHUNK), 0)
    c = _iota2((CHUNK, CHUNK), 1)
    return r, c


def _group(nc_seq, target=5):
    return max(g for g in range(1, target + 1) if nc_seq % g == 0)


def _round_robin(chains):
    live = list(chains)
    while live:
        nxt = []
        for ch in live:
            try:
                next(ch)
                nxt.append(ch)
            except StopIteration:
                pass
        live = nxt
        yield


def _run(chains):
    for _ in _round_robin(chains):
        pass


def _per_chunk(inner, kinds, grp):
    def body(*refs):
        chains = []
        for gi in range(grp):
            views = []
            for r, kind in zip(refs, kinds):
                if kind == "row":
                    views.append(r.at[pl.ds(gi * CHUNK, CHUNK)])
                elif kind == "lead":
                    views.append(r.at[pl.ds(gi, 1)])
                else:
                    views.append(r)
            chains.append(inner(gi, *views))
        _run(chains)
    return body


def _accumulate(ref, val, gi):
    if gi > 0:
        ref[...] += val
        return
    first = pl.program_id(0) == 0

    @pl.when(first)
    def _():
        ref[...] = val

    @pl.when(jnp.logical_not(first))
    def _():
        ref[...] += val


def _tri_inv(a_strict):
    r, c = _masks64()
    eye = (r == c).astype(F32)
    blk16 = (r // 16) == (c // 16)
    blk32 = (r // 32) == (c // 32)
    ld = jnp.where(blk16, a_strict, 0.0)
    x = eye - ld
    p = _nn(ld, ld)
    yield
    for step in range(3):
        xp = _nn(x, p)
        if step < 2:
            p = _nn(p, p)
        x = x + xp
        yield
    for lk in (jnp.where(jnp.logical_and(blk32, jnp.logical_not(blk16)), a_strict, 0.0),
               jnp.where(blk32, 0.0, a_strict)):
        y = x - eye
        s = lk + _nn(y, lk)
        yield
        x = x - s - _nn(s, y)
        yield
    return x


def _dn_gates(sa, alog, dtb, chunk_in_seq):
    rows = _iota2((CHUNK, LANE), 0)
    valid = jnp.logical_or(rows >= N_PAD, chunk_in_seq > 0)
    beta_t = _sigmoid(sa)
    ea = jnp.exp(alog)
    g_t = jnp.where(valid, -ea * _softplus(sa + dtb), 0.0)
    r, c = _masks64()
    ltri = (r >= c).astype(F32)
    gam_t = _nn_hi(ltri, g_t)
    return beta_t, g_t, gam_t, valid, ea


def _dn_intra_fwd(qn, kn, v, projp, alog_row, dtb_row, *, nc_seq, name):
    n = qn.shape[0]
    nct = n // CHUNK
    hw = NH * DN_D
    scale = DN_D ** -0.5

    grp = _group(nc_seq)

    def inner(gi, q_ref, k_ref, v_ref, sa_ref, al_ref, dt_ref, u_ref, w_ref, qg_ref, kd_ref, p_ref, t_ref, gl_ref):
        ci = (pl.program_id(0) * grp + gi) % nc_seq
        beta_t, _, gam_t, _, _ = _dn_gates(sa_ref[...], al_ref[...], dt_ref[...], ci)
        yield
        gam_tt = gam_t.T
        r, c = _masks64()
        incl = r >= c
        strict = r > c

        def head(h):
            sl = slice(h * DN_D, (h + 1) * DN_D)
            beta = beta_t[:, h:h + 1]
            gam = gam_t[:, 4 + h:5 + h]
            gam_row = gam_tt[4 + h:5 + h, :]
            gl = gam_t[CHUNK - 1:CHUNK, 4 + h:5 + h]
            dec = jnp.exp(jnp.where(incl, gam - gam_row, -jnp.inf))
            kh = k_ref[:, sl]
            qh = q_ref[:, sl] * scale
            vh = v_ref[:, sl]
            kk = _nt(kh, kh)
            qk = _nt(qh, kh)
            yield
            a = jnp.where(strict, beta * kk * dec, 0.0)
            tm = yield from _tri_inv(a)
            egam = jnp.exp(gam)
            u_ref[:, sl] = _nn(tm, beta * vh)
            w_ref[:, sl] = _nn(tm, (beta * egam) * kh)
            qg_ref[:, sl] = egam * qh
            kd_ref[:, sl] = jnp.exp(gl - gam) * kh
            p_ref[0, h] = qk * dec
            t_ref[0, h] = tm
            gl_ref[0, h:h + 1, :] = jnp.broadcast_to(jnp.exp(gl), (1, LANE))

        yield from _round_robin([head(h) for h in range(NH)])

    rows = grp * CHUNK
    row = pl.BlockSpec((rows, hw), lambda i: (i, 0))
    vec = pl.BlockSpec((1, LANE), lambda i: (0, 0))
    mat = pl.BlockSpec((grp, NH, CHUNK, CHUNK), lambda i: (i, 0, 0, 0))
    big = jax.ShapeDtypeStruct((n, hw), F32)
    msd = jax.ShapeDtypeStruct((nct, NH, CHUNK, CHUNK), F32)
    kinds = ["row"] * 4 + ["whole"] * 2 + ["row"] * 4 + ["lead"] * 3
    return pl.pallas_call(
        _per_chunk(inner, kinds, grp), name=name, grid=(nct // grp,),
        in_specs=[row, row, row, pl.BlockSpec((rows, LANE), lambda i: (i, C_SA // LANE)), vec, vec],
        out_specs=[row, row, row, row, mat, mat, pl.BlockSpec((grp, NH, LANE), lambda i: (i, 0, 0))],
        out_shape=[big, big, big, big, msd, msd, jax.ShapeDtypeStruct((nct, NH, LANE), F32)],
        compiler_params=_cp(VMEM_BIG),
    )(qn, kn, v, projp, alog_row, dtb_row)


def _dn_scan_fwd(u, w, qg, kd, p, gl, *, bsz, nc_seq, name):
    hw = NH * DN_D
    t_seq = nc_seq * CHUNK
    u, w, qg, kd = (z.reshape(bsz, t_seq, hw) for z in (u, w, qg, kd))
    p = p.reshape(bsz, nc_seq, NH, CHUNK, CHUNK)
    gl = gl.reshape(bsz, nc_seq, NH, LANE)

    def body(u_ref, w_ref, qg_ref, kd_ref, p_ref, gl_ref, o_ref, vn_ref, hist_ref, s_ref):
        @pl.when(pl.program_id(0) == 0)
        def _():
            s_ref[...] = jnp.zeros_like(s_ref)

        def chain(b, h):
            sl = slice(h * DN_D, (h + 1) * DN_D)
            s = s_ref[b, h]
            hist_ref[b, 0, h] = s
            ws = _nn(w_ref[b, :, sl], s)
            qs = _nn(qg_ref[b, :, sl], s)
            yield
            vn = u_ref[b, :, sl] - ws
            vn_ref[b, :, sl] = vn
            o_ref[b, :, sl] = qs + _nn(p_ref[b, 0, h], vn)
            s_ref[b, h] = gl_ref[b, 0, h:h + 1, :] * s + _tn(kd_ref[b, :, sl], vn)

        _run([chain(b, h) for b in range(bsz) for h in range(NH)])

    row = pl.BlockSpec((bsz, CHUNK, hw), lambda i: (0, i, 0))
    outs = pl.pallas_call(
        body, name=name, grid=(nc_seq,),
        in_specs=[row, row, row, row, pl.BlockSpec((bsz, 1, NH, CHUNK, CHUNK), lambda i: (0, i, 0, 0, 0)),
                  pl.BlockSpec((bsz, 1, NH, LANE), lambda i: (0, i, 0, 0))],
        out_specs=[row, row, pl.BlockSpec((bsz, 1, NH, DN_D, DN_D), lambda i: (0, i, 0, 0, 0))],
        out_shape=[jax.ShapeDtypeStruct((bsz, t_seq, hw), F32), jax.ShapeDtypeStruct((bsz, t_seq, hw), F32),
                   jax.ShapeDtypeStruct((bsz, nc_seq, NH, DN_D, DN_D), F32)],
        scratch_shapes=[pltpu.VMEM((bsz, NH, DN_D, DN_D), F32)],
        compiler_params=_cp(VMEM_BIG, ("arbitrary",)),
    )(u, w, qg, kd, p, gl)
    o, vn, hist = outs
    return o.reshape(bsz * t_seq, hw), vn.reshape(bsz * t_seq, hw), hist


def _dn_scan_bwd(do, w, qg, kd, vn, p, gl, hist, *, bsz, nc_seq, name):
    hw = NH * DN_D
    t_seq = nc_seq * CHUNK
    do, w, qg, kd, vn = (z.reshape(bsz, t_seq, hw) for z in (do, w, qg, kd, vn))
    p = p.reshape(bsz, nc_seq, NH, CHUNK, CHUNK)
    gl = gl.reshape(bsz, nc_seq, NH, LANE)

    def body(do_ref, w_ref, qg_ref, kd_ref, vn_ref, p_ref, gl_ref, hist_ref,
             du_ref, dw_ref, dqg_ref, dkd_ref, dp_ref, dgl_ref, ds_ref):
        @pl.when(pl.program_id(0) == 0)
        def _():
            ds_ref[...] = jnp.zeros_like(ds_ref)

        def chain(b, h):
            sl = slice(h * DN_D, (h + 1) * DN_D)
            s = hist_ref[b, 0, h]
            dsn = ds_ref[b, h]
            doh = do_ref[b, :, sl]
            vnh = vn_ref[b, :, sl]
            kdh = kd_ref[b, :, sl]
            dvn = _tn(p_ref[b, 0, h], doh) + _nn(kdh, dsn)
            du_ref[b, :, sl] = dvn
            dqg_ref[b, :, sl] = _nt(doh, s)
            dp_ref[b, 0, h] = _nt(doh, vnh)
            dkd_ref[b, :, sl] = _nt(vnh, dsn)
            ds_part = _tn(qg_ref[b, :, sl], doh) + gl_ref[b, 0, h:h + 1, :] * dsn
            dgl = jnp.sum(jnp.sum(dsn * s, axis=0, keepdims=True), axis=1, keepdims=True)
            dgl_ref[b, 0, h:h + 1, :] = jnp.broadcast_to(dgl, (1, LANE))
            yield
            dw_ref[b, :, sl] = -_nt(dvn, s)
            ds_ref[b, h] = ds_part - _tn(w_ref[b, :, sl], dvn)

        _run([chain(b, h) for b in range(bsz) for h in range(NH)])

    rev = lambda i: nc_seq - 1 - i
    row = pl.BlockSpec((bsz, CHUNK, hw), lambda i: (0, rev(i), 0))
    mat = pl.BlockSpec((bsz, 1, NH, CHUNK, CHUNK), lambda i: (0, rev(i), 0, 0, 0))
    glb = pl.BlockSpec((bsz, 1, NH, LANE), lambda i: (0, rev(i), 0, 0))
    big = jax.ShapeDtypeStruct((bsz, t_seq, hw), F32)
    outs = pl.pallas_call(
        body, name=name, grid=(nc_seq,),
        in_specs=[row, row, row, row, row, mat, glb,
                  pl.BlockSpec((bsz, 1, NH, DN_D, DN_D), lambda i: (0, rev(i), 0, 0, 0))],
        out_specs=[row, row, row, row, mat, glb],
        out_shape=[big, big, big, big, jax.ShapeDtypeStruct((bsz, nc_seq, NH, CHUNK, CHUNK), F32),
                   jax.ShapeDtypeStruct((bsz, nc_seq, NH, LANE), F32)],
        scratch_shapes=[pltpu.VMEM((bsz, NH, DN_D, DN_D), F32)],
        compiler_params=_cp(VMEM_BIG, ("arbitrary",)),
    )(do, w, qg, kd, vn, p, gl, hist)
    du, dw, dqg, dkd, dp, dgl = outs
    n = bsz * t_seq
    return (du.reshape(n, hw), dw.reshape(n, hw), dqg.reshape(n, hw), dkd.reshape(n, hw),
            dp.reshape(bsz * nc_seq, NH, CHUNK, CHUNK), dgl.reshape(bsz * nc_seq, NH, LANE))


def _dn_intra_bwd(qn, kn, v, projp, alog_row, dtb_row, u, w, tmat, du, dw, dqg, dkd, dp, dgl, *, nc_seq, name):
    n = qn.shape[0]
    nct = n // CHUNK
    hw = NH * DN_D
    scale = DN_D ** -0.5

    grp = _group(nc_seq)

    def inner(gi, q_ref, k_ref, v_ref, sa_ref, al_ref, dt_ref, u_ref, w_ref, t_ref, du_ref, dw_ref, dqg_ref, dkd_ref,
              dp_ref, dgl_ref, dq_ref, dk_ref, dv_ref, dsa_ref, dal_ref, ddt_ref):
        ci = (pl.program_id(0) * grp + gi) % nc_seq
        sa = sa_ref[...]
        beta_t, g_t, gam_t, valid, ea = _dn_gates(sa, al_ref[...], dt_ref[...], ci)
        yield
        gam_tt = gam_t.T
        r, c = _masks64()
        incl = r >= c
        strict = r > c
        lane = _iota2((CHUNK, LANE), 1)
        rows1 = _iota2((CHUNK, 1), 0)
        acc = [jnp.zeros((CHUNK, LANE), F32)]

        def head(h):
            sl = slice(h * DN_D, (h + 1) * DN_D)
            beta = beta_t[:, h:h + 1]
            gam = gam_t[:, 4 + h:5 + h]
            gam_row = gam_tt[4 + h:5 + h, :]
            gl = gam_t[CHUNK - 1:CHUNK, 4 + h:5 + h]
            dec = jnp.exp(jnp.where(incl, gam - gam_row, -jnp.inf))
            kh = k_ref[:, sl]
            qh = q_ref[:, sl] * scale
            vh = v_ref[:, sl]
            kk = _nt(kh, kh)
            qk = _nt(qh, kh)
            a = jnp.where(strict, beta * kk * dec, 0.0)
            pm = qk * dec
            tm = t_ref[0, h]
            uh = u_ref[:, sl]
            wh = w_ref[:, sl]
            egam = jnp.exp(gam)
            ekd = jnp.exp(gl - gam)
            dvb = _tn(tm, du_ref[:, sl])
            dkg = _tn(tm, dw_ref[:, sl])
            yield
            da = jnp.where(strict, -(_nt(dvb, uh) + _nt(dkg, wh)), 0.0)
            yield
            dad = da * dec
            dkk = beta * dad
            dbeta = jnp.sum(dad * kk, axis=1, keepdims=True)
            dpm = jnp.where(incl, dp_ref[0, h], 0.0)
            dqk = dpm * dec
            e = da * a + dpm * pm
            dgam = jnp.sum(e, axis=1, keepdims=True) - jnp.sum(e.T, axis=1, keepdims=True)
            dqgh = dqg_ref[:, sl]
            dkdh = dkd_ref[:, sl]
            dkh = (_nn(dkk, kh) + _tn(dkk, kh) + _tn(dqk, qh) + (beta * egam) * dkg + ekd * dkdh)
            dqh = _nn(dqk, kh) + egam * dqgh
            dbeta = dbeta + jnp.sum(dkg * (egam * kh), axis=1, keepdims=True) + jnp.sum(dvb * vh, axis=1, keepdims=True)
            rkd = jnp.sum(dkdh * (ekd * kh), axis=1, keepdims=True)
            dgam = (dgam + jnp.sum(dkg * ((beta * egam) * kh), axis=1, keepdims=True)
                    + jnp.sum(dqgh * (egam * qh), axis=1, keepdims=True) - rkd)
            dgam_last = jnp.sum(rkd, axis=0, keepdims=True) + dgl_ref[0, h:h + 1, 0:1] * jnp.exp(gl)
            dgam = dgam + jnp.where(rows1 == CHUNK - 1, dgam_last, 0.0)
            dq_ref[:, sl] = dqh * scale
            dk_ref[:, sl] = dkh
            dv_ref[:, sl] = beta * dvb
            acc[0] = acc[0] + jnp.where(lane == h, dbeta, 0.0) + jnp.where(lane == 4 + h, dgam, 0.0)

        yield from _round_robin([head(h) for h in range(NH)])
        acc_t = acc[0]
        utri = (r <= c).astype(F32)
        dg_t = _nn_hi(utri, acc_t)
        ddb = acc_t * beta_t * (1.0 - beta_t)
        dda = jnp.where(valid, dg_t * (-ea) * _sigmoid(sa + dt_ref[...]), 0.0)
        dsa_ref[...] = jnp.where(lane < 4, ddb, jnp.where(lane < 8, dda, 0.0)).astype(dsa_ref.dtype)
        in_g = jnp.logical_and(lane >= 4, lane < 8)
        dal = jnp.sum(jnp.where(in_g, dg_t * g_t, 0.0), axis=0, keepdims=True)
        ddt = jnp.sum(jnp.where(in_g, dda, 0.0), axis=0, keepdims=True)
        _accumulate(dal_ref, dal, gi)
        _accumulate(ddt_ref, ddt, gi)

    rows = grp * CHUNK
    row = pl.BlockSpec((rows, hw), lambda i: (i, 0))
    vec = pl.BlockSpec((1, LANE), lambda i: (0, 0))
    mat = pl.BlockSpec((grp, NH, CHUNK, CHUNK), lambda i: (i, 0, 0, 0))
    glb = pl.BlockSpec((grp, NH, LANE), lambda i: (i, 0, 0))
    big = jax.ShapeDtypeStruct((n, hw), F32)
    v128 = jax.ShapeDtypeStruct((1, LANE), F32)
    kinds = (["row"] * 4 + ["whole"] * 2 + ["row"] * 2 + ["lead"] + ["row"] * 4 + ["lead"] * 2
             + ["row"] * 4 + ["whole"] * 2)
    return pl.pallas_call(
        _per_chunk(inner, kinds, grp), name=name, grid=(nct // grp,),
        in_specs=[row, row, row, pl.BlockSpec((rows, LANE), lambda i: (i, C_SA // LANE)), vec, vec,
                  row, row, mat, row, row, row, row, mat, glb],
        out_specs=[row, row, row, pl.BlockSpec((rows, LANE), lambda i: (i, 0)), vec, vec],
        out_shape=[big, big, big, jax.ShapeDtypeStruct((n, LANE), BF16), v128, v128],
        compiler_params=_cp(VMEM_BIG, ("arbitrary",)),
    )(qn, kn, v, projp, alog_row, dtb_row, u, w, tmat, du, dw, dqg, dkd, dp, dgl)


GQ_W = NH * GLA_DK
GV_W = NH * GLA_DV
GLA_NORM = 16.0
MID = CHUNK // 2


def _gla_gates(sb, w2p, gb, chunk_in_seq):
    rows = _iota2((CHUNK, GQ_W), 0)
    valid = jnp.logical_or(rows >= N_PAD, chunk_in_seq > 0)
    graw = _nn_hi(sb, w2p) + gb
    yield
    g = jnp.where(valid, _logsigmoid(graw) * (1.0 / GLA_NORM), 0.0)
    r, c = _masks64()
    bcum = _nn_hi((r >= c).astype(F32), g)
    yield
    return graw, bcum, valid


def _head_mask(h):
    lane = _iota2((1, GQ_W), 1)
    return jnp.logical_and(lane >= h * GLA_DK, lane < (h + 1) * GLA_DK)


def _gla_intra_fwd(projp, w2p, gb, *, nc_seq, name):
    n = projp.shape[0]
    nct = n // CHUNK
    scale = GLA_DK ** -0.5

    grp = _group(nc_seq)
    rows = grp * CHUNK

    def inner(gi, qk_ref, v_ref, sb_ref, w2_ref, gb_ref, oi_ref, qg_ref, kd_ref, gl_ref):
        ci = (pl.program_id(0) * grp + gi) % nc_seq
        _, bc, _ = yield from _gla_gates(sb_ref[...], w2_ref[...], gb_ref[...], ci)
        bref = bc[MID:MID + 1, :]
        bl = bc[CHUNK - 1:CHUNK, :]
        q = qk_ref[:, 0:GQ_W] * scale
        k = qk_ref[:, GQ_W:2 * GQ_W]
        qi = q * jnp.exp(bc - bref)
        ki = k * jnp.exp(bref - bc)
        qg_ref[...] = q * jnp.exp(bc)
        kd_ref[...] = k * jnp.exp(bl - bc)
        gl_ref[0] = jnp.exp(bl)
        r, c = _masks64()
        incl = r >= c
        a = [jnp.where(incl, _nt(jnp.where(_head_mask(h), qi, 0.0), ki), 0.0) for h in range(NH)]
        yield
        for h in range(NH):
            oi_ref[:, h * GLA_DV:(h + 1) * GLA_DV] = _nn(a[h], v_ref[:, h * GLA_DV:(h + 1) * GLA_DV])

    kinds = ["row"] * 3 + ["whole"] * 2 + ["row"] * 3 + ["lead"]
    return pl.pallas_call(
        _per_chunk(inner, kinds, grp), name=name, grid=(nct // grp,),
        in_specs=[pl.BlockSpec((rows, 2 * GQ_W), lambda i: (i, C_GQK // (2 * GQ_W))),
                  pl.BlockSpec((rows, GV_W), lambda i: (i, C_GV // GV_W)),
                  pl.BlockSpec((rows, LANE), lambda i: (i, C_SB // LANE)),
                  pl.BlockSpec((LANE, GQ_W), lambda i: (0, 0)), pl.BlockSpec((1, GQ_W), lambda i: (0, 0))],
        out_specs=[pl.BlockSpec((rows, GV_W), lambda i: (i, 0)), pl.BlockSpec((rows, GQ_W), lambda i: (i, 0)),
                   pl.BlockSpec((rows, GQ_W), lambda i: (i, 0)), pl.BlockSpec((grp, 1, GQ_W), lambda i: (i, 0, 0))],
        out_shape=[jax.ShapeDtypeStruct((n, GV_W), F32), jax.ShapeDtypeStruct((n, GQ_W), F32),
                   jax.ShapeDtypeStruct((n, GQ_W), F32), jax.ShapeDtypeStruct((nct, 1, GQ_W), F32)],
        compiler_params=_cp(VMEM_BIG),
    )(projp, projp, projp, w2p, gb)


def _gla_scan_fwd(oi, qg, kd, gl, projp, *, bsz, nc_seq, name):
    t_seq = nc_seq * CHUNK
    oi = oi.reshape(bsz, t_seq, GV_W)
    qg, kd = qg.reshape(bsz, t_seq, GQ_W), kd.reshape(bsz, t_seq, GQ_W)
    gl = gl.reshape(bsz, nc_seq, 1, GQ_W)
    pj = projp.reshape(bsz, t_seq, PW)

    def body(oi_ref, qg_ref, kd_ref, gl_ref, v_ref, o_ref, hist_ref, st_ref):
        @pl.when(pl.program_id(0) == 0)
        def _():
            st_ref[...] = jnp.zeros_like(st_ref)

        for b in range(bsz):
            st = st_ref[b]
            hist_ref[b, 0] = st
            qgb = qg_ref[b]
            kdb = kd_ref[b]
            upd = jnp.zeros((GLA_DV, GQ_W), F32)
            for h in range(NH):
                sl = slice(h * GLA_DV, (h + 1) * GLA_DV)
                m = _head_mask(h)
                o_ref[b, :, sl] = oi_ref[b, :, sl] + _nt(jnp.where(m, qgb, 0.0), st)
                upd = upd + jnp.where(m, _tn(v_ref[b, :, sl], kdb), 0.0)
            st_ref[b] = gl_ref[b, 0] * st + upd

    outs = pl.pallas_call(
        body, name=name, grid=(nc_seq,),
        in_specs=[pl.BlockSpec((bsz, CHUNK, GV_W), lambda i: (0, i, 0)),
                  pl.BlockSpec((bsz, CHUNK, GQ_W), lambda i: (0, i, 0)),
                  pl.BlockSpec((bsz, CHUNK, GQ_W), lambda i: (0, i, 0)),
                  pl.BlockSpec((bsz, 1, 1, GQ_W), lambda i: (0, i, 0, 0)),
                  pl.BlockSpec((bsz, CHUNK, GV_W), lambda i: (0, i, C_GV // GV_W))],
        out_specs=[pl.BlockSpec((bsz, CHUNK, GV_W), lambda i: (0, i, 0)),
                   pl.BlockSpec((bsz, 1, GLA_DV, GQ_W), lambda i: (0, i, 0, 0))],
        out_shape=[jax.ShapeDtypeStruct((bsz, t_seq, GV_W), F32),
                   jax.ShapeDtypeStruct((bsz, nc_seq, GLA_DV, GQ_W), F32)],
        scratch_shapes=[pltpu.VMEM((bsz, GLA_DV, GQ_W), F32)],
        compiler_params=_cp(VMEM_BIG, ("arbitrary",)),
    )(oi, qg, kd, gl, pj)
    return outs[0].reshape(bsz * t_seq, GV_W), outs[1]


def _gla_scan_bwd(do, qg, kd, gl, projp, hist, *, bsz, nc_seq, name):
    t_seq = nc_seq * CHUNK
    do = do.reshape(bsz, t_seq, GV_W)
    qg, kd = qg.reshape(bsz, t_seq, GQ_W), kd.reshape(bsz, t_seq, GQ_W)
    gl = gl.reshape(bsz, nc_seq, 1, GQ_W)
    pj = projp.reshape(bsz, t_seq, PW)

    def body(do_ref, qg_ref, kd_ref, gl_ref, v_ref, hist_ref, dqg_ref, dkd_ref, dv_ref, dgl_ref, dst_ref):
        @pl.when(pl.program_id(0) == 0)
        def _():
            dst_ref[...] = jnp.zeros_like(dst_ref)

        for b in range(bsz):
            st = hist_ref[b, 0]
            dst = dst_ref[b]
            qgb = qg_ref[b]
            kdb = kd_ref[b]
            dqg = jnp.zeros((CHUNK, GQ_W), F32)
            dkd = jnp.zeros((CHUNK, GQ_W), F32)
            add = jnp.zeros((GLA_DV, GQ_W), F32)
            for h in range(NH):
                sl = slice(h * GLA_DV, (h + 1) * GLA_DV)
                m = _head_mask(h)
                doh = do_ref[b, :, sl]
                vh = v_ref[b, :, sl]
                dqg = dqg + jnp.where(m, _nn(doh, st), 0.0)
                dkd = dkd + jnp.where(m, _nn(vh, dst), 0.0)
                dv_ref[b, :, sl] = _nt(jnp.where(m, kdb, 0.0), dst)
                add = add + jnp.where(m, _tn(doh, qgb), 0.0)
            dqg_ref[b] = dqg
            dkd_ref[b] = dkd
            dgl_ref[b, 0] = jnp.sum(dst * st, axis=0, keepdims=True)
            dst_ref[b] = gl_ref[b, 0] * dst + add

    rev = lambda i: nc_seq - 1 - i
    outs = pl.pallas_call(
        body, name=name, grid=(nc_seq,),
        in_specs=[pl.BlockSpec((bsz, CHUNK, GV_W), lambda i: (0, rev(i), 0)),
                  pl.BlockSpec((bsz, CHUNK, GQ_W), lambda i: (0, rev(i), 0)),
                  pl.BlockSpec((bsz, CHUNK, GQ_W), lambda i: (0, rev(i), 0)),
                  pl.BlockSpec((bsz, 1, 1, GQ_W), lambda i: (0, rev(i), 0, 0)),
                  pl.BlockSpec((bsz, CHUNK, GV_W), lambda i: (0, rev(i), C_GV // GV_W)),
                  pl.BlockSpec((bsz, 1, GLA_DV, GQ_W), lambda i: (0, rev(i), 0, 0))],
        out_specs=[pl.BlockSpec((bsz, CHUNK, GQ_W), lambda i: (0, rev(i), 0)),
                   pl.BlockSpec((bsz, CHUNK, GQ_W), lambda i: (0, rev(i), 0)),
                   pl.BlockSpec((bsz, CHUNK, GV_W), lambda i: (0, rev(i), 0)),
                   pl.BlockSpec((bsz, 1, 1, GQ_W), lambda i: (0, rev(i), 0, 0))],
        out_shape=[jax.ShapeDtypeStruct((bsz, t_seq, GQ_W), F32), jax.ShapeDtypeStruct((bsz, t_seq, GQ_W), F32),
                   jax.ShapeDtypeStruct((bsz, t_seq, GV_W), F32), jax.ShapeDtypeStruct((bsz, nc_seq, 1, GQ_W), F32)],
        scratch_shapes=[pltpu.VMEM((bsz, GLA_DV, GQ_W), F32)],
        compiler_params=_cp(VMEM_BIG, ("arbitrary",)),
    )(do, qg, kd, gl, pj, hist)
    n = bsz * t_seq
    return (outs[0].reshape(n, GQ_W), outs[1].reshape(n, GQ_W), outs[2].reshape(n, GV_W),
            outs[3].reshape(bsz * nc_seq, 1, GQ_W))


def _gla_intra_bwd(projp, w2p, gb, do, dqg, dkd, dvi, dgl, *, nc_seq, name):
    n = projp.shape[0]
    nct = n // CHUNK
    scale = GLA_DK ** -0.5

    grp = _group(nc_seq)
    rows = grp * CHUNK

    def inner(gi, qk_ref, v_ref, sb_ref, w2_ref, gb_ref, do_ref, dqg_ref, dkd_ref, dvi_ref, dgl_ref,
              dqk_ref, dv_ref, dsb_ref, dw2_ref, dgb_ref):
        ci = (pl.program_id(0) * grp + gi) % nc_seq
        sb = sb_ref[...]
        w2 = w2_ref[...]
        graw, bc, valid = yield from _gla_gates(sb, w2, gb_ref[...], ci)
        bref = bc[MID:MID + 1, :]
        bl = bc[CHUNK - 1:CHUNK, :]
        q = qk_ref[:, 0:GQ_W] * scale
        k = qk_ref[:, GQ_W:2 * GQ_W]
        ex1 = jnp.exp(bc - bref)
        ex2 = jnp.exp(bref - bc)
        eb = jnp.exp(bc)
        ekd = jnp.exp(bl - bc)
        qi, ki = q * ex1, k * ex2
        r, c = _masks64()
        incl = r >= c
        upper = r <= c
        a_t, da, da_t = [], [], []
        for h in range(NH):
            sl = slice(h * GLA_DV, (h + 1) * GLA_DV)
            doh = do_ref[:, sl]
            vh = v_ref[:, sl]
            a_t.append(jnp.where(upper, _nt(jnp.where(_head_mask(h), ki, 0.0), qi), 0.0))
            da.append(jnp.where(incl, _nt(doh, vh), 0.0))
            da_t.append(jnp.where(upper, _nt(vh, doh), 0.0))
        yield
        dqi = jnp.zeros((CHUNK, GQ_W), F32)
        dki = jnp.zeros((CHUNK, GQ_W), F32)
        for h in range(NH):
            sl = slice(h * GLA_DV, (h + 1) * GLA_DV)
            m = _head_mask(h)
            dv_ref[:, sl] = (_nn(a_t[h], do_ref[:, sl]) + dvi_ref[:, sl]).astype(dv_ref.dtype)
            dqi = dqi + jnp.where(m, _nn(da[h], ki), 0.0)
            dki = dki + jnp.where(m, _nn(da_t[h], qi), 0.0)
        yield
        dqg = dqg_ref[...]
        dkd = dkd_ref[...]
        dqk_ref[:, 0:GQ_W] = ((dqi * ex1 + dqg * eb) * scale).astype(dqk_ref.dtype)
        dqk_ref[:, GQ_W:2 * GQ_W] = (dki * ex2 + dkd * ekd).astype(dqk_ref.dtype)
        t_qi, t_ki, t_kd = dqi * qi, dki * ki, dkd * (k * ekd)
        db = t_qi - t_ki + dqg * (q * eb) - t_kd
        dbref = jnp.sum(t_ki - t_qi, axis=0, keepdims=True)
        dbl = jnp.sum(t_kd, axis=0, keepdims=True) + dgl_ref[0] * jnp.exp(bl)
        rows = _iota2((CHUNK, GQ_W), 0)
        db = db + jnp.where(rows == MID, dbref, 0.0) + jnp.where(rows == CHUNK - 1, dbl, 0.0)
        dg = _nn_hi(upper.astype(F32), db)
        yield
        dgraw = jnp.where(valid, dg * (1.0 / GLA_NORM) * _sigmoid(-graw), 0.0)
        dsb_ref[...] = _nt_hi(dgraw, w2).astype(dsb_ref.dtype)
        dw2 = _tn_hi(sb, dgraw)
        dgb = jnp.sum(dgraw, axis=0, keepdims=True)
        _accumulate(dw2_ref, dw2, gi)
        _accumulate(dgb_ref, dgb, gi)

    rq = pl.BlockSpec((rows, GQ_W), lambda i: (i, 0))
    rv = pl.BlockSpec((rows, GV_W), lambda i: (i, 0))
    kinds = ["row"] * 3 + ["whole"] * 2 + ["row"] * 4 + ["lead"] + ["row"] * 3 + ["whole"] * 2
    return pl.pallas_call(
        _per_chunk(inner, kinds, grp), name=name, grid=(nct // grp,),
        in_specs=[pl.BlockSpec((rows, 2 * GQ_W), lambda i: (i, C_GQK // (2 * GQ_W))),
                  pl.BlockSpec((rows, GV_W), lambda i: (i, C_GV // GV_W)),
                  pl.BlockSpec((rows, LANE), lambda i: (i, C_SB // LANE)),
                  pl.BlockSpec((LANE, GQ_W), lambda i: (0, 0)), pl.BlockSpec((1, GQ_W), lambda i: (0, 0)),
                  rv, rq, rq, rv, pl.BlockSpec((grp, 1, GQ_W), lambda i: (i, 0, 0))],
        out_specs=[pl.BlockSpec((rows, 2 * GQ_W), lambda i: (i, 0)), rv, pl.BlockSpec((rows, LANE), lambda i: (i, 0)),
                   pl.BlockSpec((LANE, GQ_W), lambda i: (0, 0)), pl.BlockSpec((1, GQ_W), lambda i: (0, 0))],
        out_shape=[jax.ShapeDtypeStruct((n, 2 * GQ_W), BF16), jax.ShapeDtypeStruct((n, GV_W), BF16),
                   jax.ShapeDtypeStruct((n, LANE), BF16), jax.ShapeDtypeStruct((LANE, GQ_W), F32),
                   jax.ShapeDtypeStruct((1, GQ_W), F32)],
        compiler_params=_cp(VMEM_BIG, ("arbitrary",)),
    )(projp, projp, projp, w2p, gb, do, dqg, dkd, dvi, dgl)


SECTIONS = ((C_QKV, 1536), (C_DZ, 512), (C_GQK, 512), (C_GV, 512), (C_GR, 512), (C_SA, 128), (C_SB, 128))


def _inproj_bwd(secs, wp, h0, g1, dx1, *, tr, name):
    n, d = h0.shape

    def body(*refs):
        sec_refs = refs[:len(SECTIONS)]
        wp_ref, h0_ref, g_ref, dx1_ref, o_ref, dg_ref = refs[len(SECTIONS):]
        dh = None
        for s_ref, (off, wd) in zip(sec_refs, SECTIONS):
            part = _nt(s_ref[...], wp_ref[:, off:off + wd])
            dh = part if dh is None else dh + part
        dx, dg = _rms_bwd_math(h0_ref[...], g_ref[...], dh)
        o_ref[...] = dx1_ref[...] + dx

        @pl.when(pl.program_id(0) == 0)
        def _():
            dg_ref[...] = dg

        @pl.when(pl.program_id(0) > 0)
        def _():
            dg_ref[...] += dg

    row = pl.BlockSpec((tr, d), lambda i: (i, 0))
    vec = pl.BlockSpec((1, d), lambda i: (0, 0))
    return pl.pallas_call(
        body, name=name, grid=(n // tr,),
        in_specs=[pl.BlockSpec((tr, wd), lambda i: (i, 0)) for _, wd in SECTIONS]
        + [pl.BlockSpec((d, PW), lambda i: (0, 0)), row, vec, row],
        out_specs=[row, vec],
        out_shape=[jax.ShapeDtypeStruct((n, d), F32), jax.ShapeDtypeStruct((1, d), F32)],
        compiler_params=_cp(VMEM_BIG),
    )(*secs, wp, h0, g1, dx1)


def _adamw(w, g, m, v, *, name):
    r, c = w.shape
    tr = _tile(r, 256, 8) if r > 256 else r
    c1 = 1.0 - ADAM_B1 ** ADAM_STEP
    c2 = 1.0 - ADAM_B2 ** ADAM_STEP

    def body(w_ref, g_ref, m_ref, v_ref, d_ref, nm_ref, nv_ref):
        gv = g_ref[...]
        nm = ADAM_B1 * m_ref[...] + (1.0 - ADAM_B1) * gv
        nv = ADAM_B2 * v_ref[...] + (1.0 - ADAM_B2) * (gv * gv)
        d_ref[...] = -ADAM_LR * ((nm / c1) / (jnp.sqrt(nv / c2) + ADAM_EPS) + ADAM_WD * w_ref[...])
        nm_ref[...] = nm
        nv_ref[...] = nv

    blk = pl.BlockSpec((tr, c), lambda i: (i, 0))
    sds = jax.ShapeDtypeStruct((r, c), F32)
    return pl.pallas_call(
        body, name=name, grid=(r // tr,), in_specs=[blk] * 4, out_specs=[blk] * 3, out_shape=[sds] * 3,
        compiler_params=_cp(VMEM_BIG),
    )(w, g, m, v)


def _pair_sum(where, g, theirs, *, name):
    lead, r, cols = g.shape
    half = r // 2
    tr = _tile(half, 256, 16)
    nh = half // tr

    def body(w_ref, a_ref, b_ref, o_ref):
        o_ref[...] = (a_ref[...] + b_ref[...]).astype(o_ref.dtype)

    blk = pl.BlockSpec((1, tr, cols), lambda s, i, w: (s, i, 0))
    return pl.pallas_call(
        body, name=name,
        grid_spec=pltpu.PrefetchScalarGridSpec(
            num_scalar_prefetch=1, grid=(lead, nh),
            in_specs=[pl.BlockSpec((1, tr, cols), lambda s, i, w: (s, w[0] * nh + i, 0)), blk], out_specs=blk),
        out_shape=jax.ShapeDtypeStruct((lead, half, cols), BF16), compiler_params=_cp(VMEM_BIG),
    )(where, g, theirs)


def _chip_sum(where, pair, q, *, name):
    _, half, cols = pair.shape
    tr = _tile(half, 256, 16)
    nh = half // tr

    def body(w_ref, own_ref, q1_ref, q2_ref, q3_ref, o_ref):
        f = lambda ref: ref[0].astype(F32)
        o_ref[...] = ((f(own_ref) + f(q1_ref)) + f(q2_ref)) + f(q3_ref)

    def peer(d):
        return pl.BlockSpec((1, tr, cols), lambda i, w: ((w[1] + d) % N_CHIPS, i, 0))

    return pl.pallas_call(
        body, name=name,
        grid_spec=pltpu.PrefetchScalarGridSpec(
            num_scalar_prefetch=1, grid=(nh,),
            in_specs=[peer(0), peer(1), peer(2), peer(3)],
            out_specs=pl.BlockSpec((tr, cols), lambda i, w: (w[0] * nh + i, 0))),
        out_shape=jax.ShapeDtypeStruct((2 * half, cols), F32), compiler_params=_cp(VMEM_BIG),
    )(where, pair, q, q, q)


ANY = pl.BlockSpec(memory_space=pl.ANY)
VM = pl.BlockSpec(memory_space=pltpu.VMEM)
CAST_ROWS = 64


def _place():
    return lax.axis_index("x"), lax.axis_index("y"), lax.axis_index("c")


def _other_chips(x, y):
    return [(1 - x, y, 2 * (1 - x) + y), (x, 1 - y, 2 * x + 1 - y), (1 - x, 1 - y, 2 * (1 - x) + 1 - y)]


def _gather_weights(w_in, w_out, w_up, w_down, meta, conv, gw2):
    big = (w_in, w_out, w_up, w_down)
    small = (meta, conv, gw2)
    n_arr = len(big) + len(small)

    def body(*refs):
        ins = refs[:n_arr]
        outs = refs[n_arr:2 * n_arr]
        stage = refs[2 * n_arr:2 * n_arr + len(big)]
        send_sems, recv_sems, local_sems = refs[2 * n_arr + len(big):]
        x, y, c = _place()
        me = 2 * x + y
        srcs = []
        for k in range(n_arr):
            if k < len(big):
                src, dst = ins[k], stage[k]

                def cast_rows(i, carry, src=src, dst=dst):
                    rows = pl.ds(pl.multiple_of(i * CAST_ROWS, CAST_ROWS), CAST_ROWS)
                    dst[rows, :] = src[rows, :].astype(BF16)
                    return carry

                lax.fori_loop(0, src.shape[0] // CAST_ROWS, cast_rows, 0)
                srcs.append(stage[k])
            else:
                srcs.append(ins[k])
        local = [pltpu.make_async_copy(srcs[k], outs[k].at[me], local_sems.at[k]) for k in range(n_arr)]
        for cp in local:
            cp.start()
        def pieces(k):
            if k < len(big):
                return _row_chunks(srcs[k].shape[0], ICI_SPLIT)[0]
            return [(0, pl.ds(0, srcs[k].shape[0]))]

        sends = []
        for k in range(n_arr):
            for d, (px, py, _) in enumerate(_other_chips(x, y)):
                for s, rows in pieces(k):
                    cp = pltpu.make_async_remote_copy(
                        src_ref=srcs[k].at[rows, :], dst_ref=outs[k].at[me, rows, :], send_sem=send_sems.at[k, d, s],
                        recv_sem=recv_sems.at[k, d, s], device_id=(px, py, c), device_id_type=MESH)
                    cp.start()
                    sends.append(cp)
        for k in range(n_arr):
            for d, (px, py, pj) in enumerate(_other_chips(x, y)):
                for s, rows in pieces(k):
                    pltpu.make_async_remote_copy(
                        src_ref=srcs[k].at[rows, :], dst_ref=outs[k].at[pj, rows, :], send_sem=send_sems.at[k, d, s],
                        recv_sem=recv_sems.at[k, d, s], device_id=(px, py, c), device_id_type=MESH).wait_recv()
        for cp in sends:
            cp.wait_send()
        for cp in local:
            cp.wait()

    out_shape = [jax.ShapeDtypeStruct((N_CHIPS,) + a.shape, BF16) for a in big]
    out_shape += [jax.ShapeDtypeStruct((N_CHIPS,) + a.shape, F32) for a in small]
    sem = pltpu.SemaphoreType.DMA((n_arr, 3, ICI_SPLIT))
    return pl.pallas_call(
        body, name="gather_weights", in_specs=[VM] * n_arr, out_specs=[ANY] * n_arr, out_shape=out_shape,
        scratch_shapes=[pltpu.VMEM(a.shape, BF16) for a in big] + [sem, sem, pltpu.SemaphoreType.DMA((n_arr,))],
        compiler_params=_cp(VMEM_BIG),
    )(*big, *small)


def _row_chunks(rows, n_split):
    size = rows // n_split
    assert size * n_split == rows and size % 16 == 0, (rows, n_split)
    return [(s, pl.ds(s * size, size)) for s in range(n_split)], size


D2D_SPLIT = 4
ICI_SPLIT = 2


def _sibling_halves(grads):
    n_arr = len(grads)

    def body(*refs):
        ins = refs[:n_arr]
        theirs = refs[n_arr:2 * n_arr]
        send_sems, recv_sems = refs[2 * n_arr:]
        x, y, c = _place()
        copies = []
        for k in range(n_arr):
            half = ins[k].shape[1] // 2
            chunks, size = _row_chunks(half, D2D_SPLIT)
            for s, dst_rows in chunks:
                give = pltpu.make_async_remote_copy(
                    src_ref=ins[k].at[:, pl.ds((1 - c) * half + s * size, size), :], dst_ref=theirs[k].at[:, dst_rows, :],
                    send_sem=send_sems.at[k, s], recv_sem=recv_sems.at[k, s], device_id=(x, y, 1 - c),
                    device_id_type=MESH)
                give.start()
                copies.append(give)
        for give in copies:
            give.wait()

    halves = [jax.ShapeDtypeStruct((g.shape[0], g.shape[1] // 2, g.shape[2]), F32) for g in grads]
    sem = pltpu.SemaphoreType.DMA((n_arr, D2D_SPLIT))
    return pl.pallas_call(
        body, name="sibling_halves", in_specs=[ANY] * n_arr, out_specs=[ANY] * n_arr, out_shape=halves,
        scratch_shapes=[sem, sem],
    )(*grads)


def _chip_exchange(parts):
    n_arr = len(parts)

    def body(*refs):
        ins = refs[:n_arr]
        outs = refs[n_arr:2 * n_arr]
        send_sems, recv_sems = refs[2 * n_arr:]
        x, y, c = _place()
        me = 2 * x + y
        sends = []
        for k in range(n_arr):
            chunks, _ = _row_chunks(ins[k].shape[1], ICI_SPLIT)
            for d, (px, py, pj) in enumerate(_other_chips(x, y)):
                for s, rows in chunks:
                    cp = pltpu.make_async_remote_copy(
                        src_ref=ins[k].at[pj, rows, :], dst_ref=outs[k].at[me, rows, :], send_sem=send_sems.at[k, d, s],
                        recv_sem=recv_sems.at[k, d, s], device_id=(px, py, c), device_id_type=MESH)
                    cp.start()
                    sends.append(cp)
        for k in range(n_arr):
            chunks, _ = _row_chunks(ins[k].shape[1], ICI_SPLIT)
            for d, (px, py, pj) in enumerate(_other_chips(x, y)):
                for s, rows in chunks:
                    pltpu.make_async_remote_copy(
                        src_ref=ins[k].at[pj, rows, :], dst_ref=outs[k].at[pj, rows, :], send_sem=send_sems.at[k, d, s],
                        recv_sem=recv_sems.at[k, d, s], device_id=(px, py, c), device_id_type=MESH).wait_recv()
        for cp in sends:
            cp.wait_send()

    sem = pltpu.SemaphoreType.DMA((n_arr, 3, ICI_SPLIT))
    return pl.pallas_call(
        body, name="chip_exchange", in_specs=[ANY] * n_arr, out_specs=[ANY] * n_arr,
        out_shape=[jax.ShapeDtypeStruct(p.shape, p.dtype) for p in parts],
        scratch_shapes=[sem, sem],
    )(*parts)


def _sibling_join(bufs):
    n_arr = len(bufs)

    def body(*refs):
        bufs_out = refs[n_arr:2 * n_arr]
        send_sems, recv_sems = refs[2 * n_arr:]
        x, y, c = _place()
        copies = []
        for k in range(n_arr):
            half = bufs_out[k].shape[0] // 2
            chunks, size = _row_chunks(half, D2D_SPLIT)
            for s, _ in chunks:
                rows = pl.ds(c * half + s * size, size)
                give = pltpu.make_async_remote_copy(
                    src_ref=bufs_out[k].at[rows, :], dst_ref=bufs_out[k].at[rows, :], send_sem=send_sems.at[k, s],
                    recv_sem=recv_sems.at[k, s], device_id=(x, y, 1 - c), device_id_type=MESH)
                give.start()
                copies.append((k, s, half, size, give))
        for k, s, half, size, give in copies:
            rows = pl.ds((1 - c) * half + s * size, size)
            pltpu.make_async_remote_copy(
                src_ref=bufs_out[k].at[rows, :], dst_ref=bufs_out[k].at[rows, :], send_sem=send_sems.at[k, s],
                recv_sem=recv_sems.at[k, s], device_id=(x, y, 1 - c), device_id_type=MESH).wait_recv()
            give.wait_send()

    sem = pltpu.SemaphoreType.DMA((n_arr, D2D_SPLIT))
    return pl.pallas_call(
        body, name="sibling_join", in_specs=[ANY] * n_arr, out_specs=[ANY] * n_arr,
        out_shape=[jax.ShapeDtypeStruct(b.shape, F32) for b in bufs],
        input_output_aliases={k: k for k in range(n_arr)},
        scratch_shapes=[sem, sem],
    )(*bufs)


PACK_ROWS = 48


def _small_allreduce(pack):
    masks = [(dx, dy, dc) for dx in (0, 1) for dy in (0, 1) for dc in (0, 1)][1:]

    def body(p_ref, o_ref, buf, send_sems, recv_sems):
        x, y, c = _place()
        me = 4 * x + 2 * y + c
        buf[me] = p_ref[...]
        sends = []
        for k, (dx, dy, dc) in enumerate(masks):
            peer = (1 - x if dx else x, 1 - y if dy else y, 1 - c if dc else c)
            cp = pltpu.make_async_remote_copy(
                src_ref=p_ref, dst_ref=buf.at[me], send_sem=send_sems.at[k], recv_sem=recv_sems.at[k],
                device_id=peer, device_id_type=MESH)
            cp.start()
            sends.append(cp)
        for k, (dx, dy, dc) in enumerate(masks):
            peer = (1 - x if dx else x, 1 - y if dy else y, 1 - c if dc else c)
            pj = 4 * peer[0] + 2 * peer[1] + peer[2]
            pltpu.make_async_remote_copy(
                src_ref=p_ref, dst_ref=buf.at[pj], send_sem=send_sems.at[k], recv_sem=recv_sems.at[k],
                device_id=peer, device_id_type=MESH).wait_recv()
        for cp in sends:
            cp.wait_send()
        tot = buf[0]
        for k in range(1, 8):
            tot = tot + buf[k]
        o_ref[...] = tot
        o_ref[0:N_META, :] = tot[0:N_META] + tot[N_META:2 * N_META]

    return pl.pallas_call(
        body, name="small_allreduce", in_specs=[VM], out_specs=VM,
        out_shape=jax.ShapeDtypeStruct((PACK_ROWS, D_MODEL), F32),
        scratch_shapes=[pltpu.VMEM((8, PACK_ROWS, D_MODEL), F32), pltpu.SemaphoreType.DMA((7,)),
                        pltpu.SemaphoreType.DMA((7,))],
    )(pack)


def _pad_lanes(vec, offset):
    k = vec.shape[1]
    return jnp.concatenate([jnp.zeros((1, offset), F32), vec, jnp.zeros((1, LANE - offset - k), F32)], axis=1)


def _local_step(x, tgt, meta, norm1_g, wp, conv_w, a_log, dt_bias, dn_norm_g, gla_w2, gla_b, gla_norm_g,
                w_out, norm2_g, w_up, w_down, final_norm_g):
    bsz, s_len, d = x.shape
    t_seq = s_len + CHUNK
    nc_seq = t_seq // CHUNK
    n = bsz * t_seq
    tr = _tile(t_seq, 832)
    tt = _tile(t_seq, 416)

    lead = jnp.concatenate([jnp.zeros((N_PAD, d), F32), meta], axis=0)
    h0 = jnp.concatenate([jnp.broadcast_to(lead[None], (bsz, CHUNK, d)), x], axis=1).reshape(n, d)
    tgt_p = jnp.concatenate([jnp.zeros((bsz, CHUNK, d), F32), tgt], axis=1).reshape(n, d)
    alog_row = _pad_lanes(a_log, 4)
    dtb_row = _pad_lanes(dt_bias, 4)
    w2p = jnp.concatenate([gla_w2, jnp.zeros((LANE - GLA_RANK, GQ_W), F32)], axis=0)

    h = _rms_fwd(h0, norm1_g, tr=tr, name="norm1")
    (projp,) = _mm(h, wp, "nn", tm=tt, tn=PW, tk=d, out_dtypes=(F32,), name="in_proj")
    qn, kn, v = _dnprep_fwd(projp, conv_w, bsz=bsz, t_seq=t_seq, tt=tt, name="dn_prep")
    u, w, qg, kd, pmat, tmat, gl = _dn_intra_fwd(qn, kn, v, projp, alog_row, dtb_row, nc_seq=nc_seq, name="dn_intra")
    o_dn, vn, hist = _dn_scan_fwd(u, w, qg, kd, pmat, gl, bsz=bsz, nc_seq=nc_seq, name="dn_scan")
    oi, gqg, gkd, ggl = _gla_intra_fwd(projp, w2p, gla_b, nc_seq=nc_seq, name="gla_intra")
    o_gla, ghist = _gla_scan_fwd(oi, gqg, gkd, ggl, projp, bsz=bsz, nc_seq=nc_seq, name="gla_scan")
    mix = _gnorm_fwd(o_dn, o_gla, projp, dn_norm_g, gla_norm_g, tr=tr, name="gated_norm")
    (x1,) = _mm(mix, w_out, "nn", tm=tr, tn=d, tk=d, out_dtypes=(F32,), extras=(h0,),
                epilogue=lambda acc, res: (res + acc,), name="out_proj")
    h2 = _rms_fwd(x1, norm2_g, tr=tr, name="norm2")

    (act,) = _mm(h2, w_up, "nn", tm=tt, tn=D_FF, tk=d, out_dtypes=(BF16,),
                 epilogue=lambda acc: (jnp.square(jnp.maximum(acc, 0.0)),), name="mlp_up")
    (x2,) = _mm(act, w_down, "nn", tm=tr, tn=d, tk=D_FF, out_dtypes=(F32,), extras=(x1,),
                epilogue=lambda acc, res: (res + acc,), name="mlp_down")
    dx2, dx2b, d_final_g, loss_tile = _final_loss(x2, final_norm_g, tgt_p, t_seq=t_seq, tr=tr, name="final_loss")

    (dup,) = _mm(dx2b, w_down, "nt", tm=tt, tn=D_FF, tk=d, out_dtypes=(BF16,), extras=(act,),
                 epilogue=lambda acc, a: (acc * (2.0 * jnp.sqrt(a.astype(F32))),), name="mlp_down_bwd")
    (d_w_down,) = _mm(act, dx2b, "tn", tm=D_FF // 2, tn=d, tk=tr, out_dtypes=(F32,), name="w_down_grad")
    (d_w_up,) = _mm(h2, dup, "tn", tm=d, tn=D_FF // 2, tk=tr, out_dtypes=(F32,), name="w_up_grad")
    (dh2,) = _mm(dup, w_up, "nt", tm=tr, tn=d, tk=D_FF, out_dtypes=(F32,), name="mlp_up_bwd")
    dx1, dx1b, d_norm2_g = _rms_bwd_add(x1, norm2_g, dh2, dx2, tr=tr, name="norm2_bwd")

    (dmix,) = _mm(dx1b, w_out, "nt", tm=tr, tn=d, tk=d, out_dtypes=(F32,), name="out_proj_bwd")
    (d_w_out,) = _mm(mix, dx1b, "tn", tm=d, tn=d, tk=tr, out_dtypes=(F32,), name="w_out_grad")
    do_dn, ddz, do_gla, dgr, d_dn_norm_g, d_gla_norm_g = _gnorm_bwd(
        dmix, o_dn, o_gla, projp, dn_norm_g, gla_norm_g, tr=tr, name="gated_norm_bwd")
    du, dw, dqg, dkd, dpm, dgl = _dn_scan_bwd(do_dn, w, qg, kd, vn, pmat, gl, hist, bsz=bsz, nc_seq=nc_seq,
                                               name="dn_scan_bwd")
    dqn, dkn, dv, dsa, d_alog, d_dtb = _dn_intra_bwd(qn, kn, v, projp, alog_row, dtb_row, u, w, tmat,
                                                     du, dw, dqg, dkd, dpm, dgl, nc_seq=nc_seq, name="dn_intra_bwd")
    dz, d_conv_w = _dnprep_bwd_a(projp, conv_w, dqn, dkn, dv, bsz=bsz, t_seq=t_seq, tt=tt, name="dn_prep_bwd")
    dcin = _dnprep_bwd_b(dz, conv_w, bsz=bsz, t_seq=t_seq, tt=tt, name="conv_bwd")
    gdqg, gdkd, gdvi, gdgl = _gla_scan_bwd(do_gla, gqg, gkd, ggl, projp, ghist, bsz=bsz, nc_seq=nc_seq,
                                            name="gla_scan_bwd")
    dgqk, dgv, dsb, d_w2p, d_gla_b = _gla_intra_bwd(projp, w2p, gla_b, do_gla, gdqg, gdkd, gdvi, gdgl,
                                                    nc_seq=nc_seq, name="gla_intra_bwd")

    secs = (dcin, ddz, dgqk, dgv, dgr, dsa, dsb)
    g_lo = _grad_tn(h, secs[0:2], tk=tr, name="w_in_grad_lo")
    g_hi = _grad_tn(h, secs[2:7], tk=tr, name="w_in_grad_hi")
    dh0, d_norm1_g = _inproj_bwd(secs, wp, h0, norm1_g, dx1, tr=tt, name="in_proj_bwd")
    dh0 = dh0.reshape(bsz, t_seq, d)
    grad_x = dh0[:, CHUNK:]
    d_meta_rows = dh0[:, N_PAD:CHUNK].reshape(bsz * N_META, d)

    sa0, sb0 = C_SA - C_GQK, C_SB - C_GQK
    d_w_in = jnp.concatenate([g_lo, g_hi[:, sa0:sa0 + 8], g_hi[:, 0:sa0], g_hi[:, sb0:sb0 + GLA_RANK]], axis=1)
    grads = dict(w_in=d_w_in, w_out=d_w_out, w_up=d_w_up, w_down=d_w_down, meta_rows=d_meta_rows,
                 norm1_g=d_norm1_g, conv_w=d_conv_w, a_log_tile=d_alog, dt_bias_tile=d_dtb, dn_norm_g=d_dn_norm_g,
                 gla_w2=d_w2p[0:GLA_RANK], gla_b=d_gla_b, gla_norm_g=d_gla_norm_g, norm2_g=d_norm2_g,
                 final_norm_g=d_final_g, loss_tile=loss_tile)
    return grad_x, grads


def _pad_layout(w_full):
    z = lambda k: jnp.zeros((w_full.shape[0], k), w_full.dtype)
    return jnp.concatenate([w_full[:, 0:2048], w_full[:, 2056:3592], w_full[:, 2048:2056], z(LANE - 8),
                            w_full[:, 3592:3608], z(LANE - GLA_RANK)], axis=1)


def _pack_small(g, bsz):
    assert bsz * N_META == 32
    row = jnp.concatenate([g["a_log_tile"], g["dt_bias_tile"], g["dn_norm_g"], g["gla_norm_g"], g["gla_b"],
                           g["loss_tile"], jnp.zeros((1, LANE), F32)], axis=1)
    return jnp.concatenate([g["meta_rows"], g["norm1_g"], g["conv_w"].reshape(6, D_MODEL), row,
                            g["gla_w2"].reshape(4, D_MODEL), g["norm2_g"], g["final_norm_g"],
                            jnp.zeros((2, D_MODEL), F32)], axis=0)


def kernel(x, meta_tokens, norm1_g, w_in, conv_w, a_log, dt_bias, dn_norm_g, gla_w2, gla_b, gla_norm_g, w_out, norm2_g, w_up, w_down, final_norm_g, loss_target, m_meta_tokens, m_norm1_g, m_w_in, m_conv_w, m_a_log, m_dt_bias, m_dn_norm_g, m_gla_w2, m_gla_b, m_gla_norm_g, m_w_out, m_norm2_g, m_w_up, m_w_down, m_final_norm_g, v_meta_tokens, v_norm1_g, v_w_in, v_conv_w, v_a_log, v_dt_bias, v_dn_norm_g, v_gla_w2, v_gla_b, v_gla_norm_g, v_w_out, v_norm2_g, v_w_up, v_w_down, v_final_norm_g):
    bsz = x.shape[0]
    chip = 2 * lax.axis_index("x") + lax.axis_index("y")

    shard_w = IN_WIDTH // N_CHIPS
    lane_pad = lambda a, wd: jnp.pad(a, ((0, 0), (0, wd - a.shape[1])))
    g_in, g_out, g_up, g_down, g_meta, g_conv, g_w2 = _gather_weights(
        lane_pad(w_in[0], D_MODEL), w_out[0], w_up[0], w_down[0], meta_tokens, conv_w[0], lane_pad(gla_w2[0], LANE))
    wp = _pad_layout(g_in[:, :, 0:shard_w].transpose(1, 0, 2).reshape(D_MODEL, IN_WIDTH))
    w_out_f = g_out.reshape(D_MODEL, D_MODEL)
    w_up_f = g_up.transpose(1, 0, 2).reshape(D_MODEL, D_FF)
    w_down_f = g_down.reshape(D_FF, D_MODEL)
    meta_f = g_meta.transpose(1, 0, 2).reshape(N_META, D_MODEL)
    conv_f = g_conv.transpose(1, 0, 2).reshape(4, QKV_W)
    w2_f = g_w2[:, :, 0:GQ_W // N_CHIPS].transpose(1, 0, 2).reshape(GLA_RANK, GQ_W)

    grad_x, g = _local_step(x, loss_target, meta_f, norm1_g, wp, conv_f, a_log, dt_bias, dn_norm_g, w2_f, gla_b,
                            gla_norm_g, w_out_f, norm2_g, w_up_f, w_down_f, final_norm_g.reshape(1, D_MODEL))

    shard_major = [
        jnp.pad(g["w_in"].reshape(D_MODEL, N_CHIPS, shard_w).transpose(1, 0, 2),
                ((0, 0), (0, 0), (0, D_MODEL - shard_w))),
        g["w_out"].reshape(N_CHIPS, D_MODEL // N_CHIPS, D_MODEL),
        g["w_up"].reshape(D_MODEL, N_CHIPS, D_FF // N_CHIPS).transpose(1, 0, 2),
        g["w_down"].reshape(N_CHIPS, D_FF // N_CHIPS, D_MODEL),
    ]
    where = jnp.stack([lax.axis_index("c"), chip]).astype(jnp.int32)
    theirs = _sibling_halves(shard_major)
    pair = [_pair_sum(where, a, b, name=f"pair_sum_{k}") for k, (a, b) in enumerate(zip(shard_major, theirs))]
    parts = _chip_exchange(pair)
    halves = [_chip_sum(where, p, q, name=f"chip_sum_{k}") for k, (p, q) in enumerate(zip(pair, parts))]
    gw_in, gw_out, gw_up, gw_down = _sibling_join(halves)
    gw_in = gw_in[:, 0:shard_w]

    red = _small_allreduce(_pack_small(g, bsz))
    g_meta_full = red[0:N_META]
    g_norm1 = red[32:33]
    g_conv_full = red[33:39].reshape(4, QKV_W)
    srow = red[39:40]
    g_alog, g_dtb = srow[:, 4:8], srow[:, LANE + 4:LANE + 8]
    g_dn_norm, g_gla_norm = srow[:, 2 * LANE:3 * LANE], srow[:, 3 * LANE:4 * LANE]
    g_gla_b = srow[:, 4 * LANE:6 * LANE]
    loss = srow[0, 6 * LANE]
    g_w2_full = red[40:44].reshape(GLA_RANK, GQ_W)
    g_norm2 = red[44:45]
    g_final = red[45:46]
    g_meta_sh = lax.dynamic_slice_in_dim(g_meta_full, chip * (D_MODEL // N_CHIPS), D_MODEL // N_CHIPS, axis=1)
    g_conv_sh = lax.dynamic_slice_in_dim(g_conv_full, chip * (QKV_W // N_CHIPS), QKV_W // N_CHIPS, axis=1)
    g_w2_sh = lax.dynamic_slice_in_dim(g_w2_full, chip * (GQ_W // N_CHIPS), GQ_W // N_CHIPS, axis=1)

    names = ["meta_tokens", "norm1_g", "w_in", "conv_w", "a_log", "dt_bias", "dn_norm_g", "gla_w2", "gla_b",
             "gla_norm_g", "w_out", "norm2_g", "w_up", "w_down", "final_norm_g"]
    weights = dict(meta_tokens=meta_tokens, norm1_g=norm1_g, w_in=w_in, conv_w=conv_w, a_log=a_log, dt_bias=dt_bias,
                   dn_norm_g=dn_norm_g, gla_w2=gla_w2, gla_b=gla_b, gla_norm_g=gla_norm_g, w_out=w_out,
                   norm2_g=norm2_g, w_up=w_up, w_down=w_down, final_norm_g=final_norm_g)
    ms = dict(meta_tokens=m_meta_tokens, norm1_g=m_norm1_g, w_in=m_w_in, conv_w=m_conv_w, a_log=m_a_log,
              dt_bias=m_dt_bias, dn_norm_g=m_dn_norm_g, gla_w2=m_gla_w2, gla_b=m_gla_b, gla_norm_g=m_gla_norm_g,
              w_out=m_w_out, norm2_g=m_norm2_g, w_up=m_w_up, w_down=m_w_down, final_norm_g=m_final_norm_g)
    vs = dict(meta_tokens=v_meta_tokens, norm1_g=v_norm1_g, w_in=v_w_in, conv_w=v_conv_w, a_log=v_a_log,
              dt_bias=v_dt_bias, dn_norm_g=v_dn_norm_g, gla_w2=v_gla_w2, gla_b=v_gla_b, gla_norm_g=v_gla_norm_g,
              w_out=v_w_out, norm2_g=v_norm2_g, w_up=v_w_up, w_down=v_w_down, final_norm_g=v_final_norm_g)
    grads2d = dict(meta_tokens=g_meta_sh, norm1_g=g_norm1, w_in=gw_in, conv_w=g_conv_sh, a_log=g_alog, dt_bias=g_dtb,
                   dn_norm_g=g_dn_norm, gla_w2=g_w2_sh, gla_b=g_gla_b, gla_norm_g=g_gla_norm, w_out=gw_out,
                   norm2_g=g_norm2, w_up=gw_up, w_down=gw_down, final_norm_g=g_final)
    out_g, out_d, out_m, out_v = [], [], [], []
    for nm in names:
        shape = weights[nm].shape
        g2 = grads2d[nm]
        as2d = lambda a: a.reshape(g2.shape)
        dlt, nm_, nv_ = _adamw(as2d(weights[nm]), g2, as2d(ms[nm]), as2d(vs[nm]), name=f"adamw_{nm}")
        out_g.append(g2.reshape(shape))
        out_d.append(dlt.reshape(shape))
        out_m.append(nm_.reshape(shape))
        out_v.append(nv_.reshape(shape))
    return (loss, grad_x, *out_g, *out_d, *out_m, *out_v)
```

```python
import functools

import jax
import jax.numpy as jnp
import numpy as np
from jax import lax
from jax.experimental import pallas as pl
from jax.experimental.pallas import tpu as pltpu

F32 = jnp.float32
BF16 = jnp.bfloat16
HI = lax.Precision.HIGHEST
MESH = pl.DeviceIdType.MESH

D_MODEL = 1024
N_META = 16
CHUNK = 64
N_PAD = CHUNK - N_META
NH = 4
DN_D = 128
GLA_DK = 64
GLA_DV = 128
GLA_RANK = 16
D_FF = 4 * D_MODEL
EPS = 1e-6
IN_WIDTH = 3608
C_QKV, C_DZ, C_GQK, C_GV, C_GR, C_SA, C_SB, PW = 0, 1536, 2048, 2560, 3072, 3584, 3712, 3840
LANE = 128
N_CHIPS = 4

ADAM_LR, ADAM_B1, ADAM_B2, ADAM_EPS, ADAM_WD, ADAM_STEP = 0.001, 0.9, 0.999, 1e-08, 0.01, 10

VMEM_BIG = 56 * 1024 * 1024


def _cp(vmem=None, sem=None):
    kw = {}
    if vmem is not None:
        kw["vmem_limit_bytes"] = vmem
    if sem is not None:
        kw["dimension_semantics"] = sem
    return pltpu.CompilerParams(**kw)


def _tile(n, target, mult=16):
    best = None
    for t in range(mult, min(n, target) + 1, mult):
        if n % t == 0:
            best = t
    assert best is not None, (n, target)
    return best


def _dot(a, b, dims, prec=None):
    return lax.dot_general(a, b, (dims, ((), ())), preferred_element_type=F32, precision=prec)


def _nn(a, b):
    return _dot(a.astype(BF16), b.astype(BF16), ((1,), (0,)))


def _nt(a, b):
    return _dot(a.astype(BF16), b.astype(BF16), ((1,), (1,)))


def _tn(a, b):
    return _dot(a.astype(BF16), b.astype(BF16), ((0,), (0,)))


def _nn_hi(a, b):
    return _dot(a, b, ((1,), (0,)), HI)


def _nt_hi(a, b):
    return _dot(a, b, ((1,), (1,)), HI)


def _tn_hi(a, b):
    return _dot(a, b, ((0,), (0,)), HI)


def _sigmoid(x):
    return 1.0 / (1.0 + jnp.exp(-x))


def _softplus(x):
    return jnp.maximum(x, 0.0) + jnp.log(1.0 + jnp.exp(-jnp.abs(x)))


def _logsigmoid(x):
    return -_softplus(-x)


def _iota2(shape, dim):
    return lax.broadcasted_iota(jnp.int32, shape, dim)


def _mm(a, b, mode, *, tm, tn, tk, out_dtypes, extras=(), epilogue=None, name, vmem=VMEM_BIG):
    if mode == "tn":
        K, M = a.shape
    else:
        M, K = a.shape
    N = b.shape[0] if mode == "nt" else b.shape[1]
    assert M % tm == 0 and N % tn == 0 and K % tk == 0, (name, M, N, K, tm, tn, tk)
    nk = K // tk
    n_ex, n_out = len(extras), len(out_dtypes)
    if mode == "tn":
        a_spec = pl.BlockSpec((tk, tm), lambda i, j, k: (k, i))
    else:
        a_spec = pl.BlockSpec((tm, tk), lambda i, j, k: (i, k))
    if mode == "nt":
        b_spec = pl.BlockSpec((tn, tk), lambda i, j, k: (j, k))
    else:
        b_spec = pl.BlockSpec((tk, tn), lambda i, j, k: (k, j))
    mn_spec = pl.BlockSpec((tm, tn), lambda i, j, k: (i, j))
    dims = {"nn": ((1,), (0,)), "nt": ((1,), (1,)), "tn": ((0,), (0,))}[mode]

    single = nk == 1
    direct = (not single) and epilogue is None and n_out == 1 and out_dtypes[0] == F32

    def body(*refs):
        a_ref, b_ref = refs[0], refs[1]
        ex_refs = refs[2:2 + n_ex]
        out_refs = refs[2 + n_ex:2 + n_ex + n_out]
        part = _dot(a_ref[...].astype(BF16), b_ref[...].astype(BF16), dims)

        def finish(acc):
            res = (acc,) if epilogue is None else epilogue(acc, *[e[...] for e in ex_refs])
            for o_ref, r in zip(out_refs, res):
                o_ref[...] = r.astype(o_ref.dtype)

        if single:
            finish(part)
            return
        acc_ref = out_refs[0] if direct else refs[2 + n_ex + n_out]
        k = pl.program_id(2)

        @pl.when(k == 0)
        def _():
            acc_ref[...] = part

        @pl.when(k > 0)
        def _():
            acc_ref[...] += part

        if not direct:
            @pl.when(k == nk - 1)
            def _():
                finish(acc_ref[...])

    outs = pl.pallas_call(
        body, name=name, grid=(M // tm, N // tn, nk),
        in_specs=[a_spec, b_spec] + [mn_spec] * n_ex,
        out_specs=[mn_spec] * n_out,
        out_shape=[jax.ShapeDtypeStruct((M, N), dt) for dt in out_dtypes],
        scratch_shapes=[] if (single or direct) else [pltpu.VMEM((tm, tn), F32)],
        compiler_params=_cp(vmem, ("parallel", "parallel", "arbitrary")),
    )(a, b, *extras)
    return tuple(outs)


def _grad_tn(a, secs, *, tk, name):
    kk, m = a.shape
    widths = [s.shape[1] for s in secs]
    total = sum(widths)
    nk = kk // tk

    def body(*refs):
        a_ref, sec_refs, o_ref = refs[0], refs[1:-1], refs[-1]
        cat = sec_refs[0][...] if len(sec_refs) == 1 else jnp.concatenate([s[...] for s in sec_refs], axis=1)
        part = _dot(a_ref[...].astype(BF16), cat.astype(BF16), ((0,), (0,)))
        k = pl.program_id(0)

        @pl.when(k == 0)
        def _():
            o_ref[...] = part

        @pl.when(k > 0)
        def _():
            o_ref[...] += part

    return pl.pallas_call(
        body, name=name, grid=(nk,),
        in_specs=[pl.BlockSpec((tk, m), lambda k: (k, 0))] + [pl.BlockSpec((tk, w), lambda k: (k, 0)) for w in widths],
        out_specs=pl.BlockSpec((m, total), lambda k: (0, 0)),
        out_shape=jax.ShapeDtypeStruct((m, total), F32),
        compiler_params=_cp(VMEM_BIG, ("arbitrary",)),
    )(a, *secs)


def _rms_fwd(x, g, *, tr, name):
    n, d = x.shape

    def body(x_ref, g_ref, o_ref):
        xv = x_ref[...]
        r = lax.rsqrt(jnp.mean(xv * xv, axis=-1, keepdims=True) + EPS)
        o_ref[...] = (xv * r * g_ref[...]).astype(o_ref.dtype)

    return pl.pallas_call(
        body, name=name, grid=(n // tr,),
        in_specs=[pl.BlockSpec((tr, d), lambda i: (i, 0)), pl.BlockSpec((1, d), lambda i: (0, 0))],
        out_specs=pl.BlockSpec((tr, d), lambda i: (i, 0)),
        out_shape=jax.ShapeDtypeStruct((n, d), BF16),
        compiler_params=_cp(VMEM_BIG),
    )(x, g)


def _rms_bwd_math(xv, g, dy):
    r = lax.rsqrt(jnp.mean(xv * xv, axis=-1, keepdims=True) + EPS)
    xh = xv * r
    gdy = dy * g
    dx = r * (gdy - xh * jnp.mean(xh * gdy, axis=-1, keepdims=True))
    return dx, jnp.sum(dy * xh, axis=0, keepdims=True)


def _rms_bwd_add(x, g, dy, res, *, tr, name):
    n, d = x.shape

    def body(x_ref, g_ref, dy_ref, res_ref, o_ref, ob_ref, dg_ref):
        dx, dg = _rms_bwd_math(x_ref[...], g_ref[...], dy_ref[...])
        tot = res_ref[...] + dx
        o_ref[...] = tot
        ob_ref[...] = tot.astype(BF16)

        @pl.when(pl.program_id(0) == 0)
        def _():
            dg_ref[...] = dg

        @pl.when(pl.program_id(0) > 0)
        def _():
            dg_ref[...] += dg

    row = pl.BlockSpec((tr, d), lambda i: (i, 0))
    vec = pl.BlockSpec((1, d), lambda i: (0, 0))
    return pl.pallas_call(
        body, name=name, grid=(n // tr,),
        in_specs=[row, vec, row, row], out_specs=[row, row, vec],
        out_shape=[jax.ShapeDtypeStruct((n, d), F32), jax.ShapeDtypeStruct((n, d), BF16),
                   jax.ShapeDtypeStruct((1, d), F32)],
        compiler_params=_cp(VMEM_BIG),
    )(x, g, dy, res)


def _final_loss(x2, gf, tgt, *, t_seq, tr, name):
    n, d = x2.shape
    per_seq = t_seq // tr

    def body(x_ref, g_ref, t_ref, dx_ref, dxb_ref, dg_ref, loss_ref):
        i = pl.program_id(0)
        xv = x_ref[...]
        g = g_ref[...]
        r = lax.rsqrt(jnp.mean(xv * xv, axis=-1, keepdims=True) + EPS)
        xh = xv * r
        pos = (i % per_seq) * tr + _iota2((tr, 1), 0)
        real = pos >= CHUNK
        err = jnp.where(real, xh * g - t_ref[...], 0.0)
        dy = err * (1.0 / d)
        gdy = dy * g
        dx = r * (gdy - xh * jnp.mean(xh * gdy, axis=-1, keepdims=True))
        dx_ref[...] = dx
        dxb_ref[...] = dx.astype(BF16)
        dg = jnp.sum(dy * xh, axis=0, keepdims=True)
        ls = 0.5 * jnp.sum(jnp.mean(err * err, axis=-1, keepdims=True), axis=0, keepdims=True)
        ls = jnp.where(_iota2((1, LANE), 1) == 0, ls, 0.0)

        @pl.when(i == 0)
        def _():
            dg_ref[...] = dg
            loss_ref[...] = ls

        @pl.when(i > 0)
        def _():
            dg_ref[...] += dg
            loss_ref[...] += ls

    row = pl.BlockSpec((tr, d), lambda i: (i, 0))
    vec = pl.BlockSpec((1, d), lambda i: (0, 0))
    one = pl.BlockSpec((1, LANE), lambda i: (0, 0))
    return pl.pallas_call(
        body, name=name, grid=(n // tr,),
        in_specs=[row, vec, row], out_specs=[row, row, vec, one],
        out_shape=[jax.ShapeDtypeStruct((n, d), F32), jax.ShapeDtypeStruct((n, d), BF16),
                   jax.ShapeDtypeStruct((1, d), F32), jax.ShapeDtypeStruct((1, LANE), F32)],
        compiler_params=_cp(VMEM_BIG),
    )(x2, gf, tgt)


def _gnorm_fwd(o_dn, o_gla, projp, g_dn, g_gla, *, tr, name):
    n = o_dn.shape[0]
    w = NH * DN_D

    def body(odn_ref, ogl_ref, z_ref, r_ref, gdn_ref, ggl_ref, mix_ref):
        for grp, (o_ref, gate_ref, gain_ref) in enumerate(((odn_ref, z_ref, gdn_ref), (ogl_ref, r_ref, ggl_ref))):
            gain = gain_ref[...]
            for h in range(NH):
                sl = slice(h * DN_D, (h + 1) * DN_D)
                o = o_ref[:, sl]
                z = gate_ref[:, sl]
                r = lax.rsqrt(jnp.mean(o * o, axis=-1, keepdims=True) + EPS)
                y = (o * r * gain) * (z * _sigmoid(z))
                mix_ref[:, grp * w + h * DN_D: grp * w + (h + 1) * DN_D] = y.astype(mix_ref.dtype)

    row = pl.BlockSpec((tr, w), lambda i: (i, 0))
    vec = pl.BlockSpec((1, DN_D), lambda i: (0, 0))
    return pl.pallas_call(
        body, name=name, grid=(n // tr,),
        in_specs=[row, row, pl.BlockSpec((tr, w), lambda i: (i, C_DZ // w)),
                  pl.BlockSpec((tr, w), lambda i: (i, C_GR // w)), vec, vec],
        out_specs=pl.BlockSpec((tr, 2 * w), lambda i: (i, 0)),
        out_shape=jax.ShapeDtypeStruct((n, 2 * w), BF16),
        compiler_params=_cp(VMEM_BIG),
    )(o_dn, o_gla, projp, projp, g_dn, g_gla)


def _gnorm_bwd(dmix, o_dn, o_gla, projp, g_dn, g_gla, *, tr, name):
    n = o_dn.shape[0]
    w = NH * DN_D

    def body(dm_ref, odn_ref, ogl_ref, z_ref, r_ref, gdn_ref, ggl_ref,
             dodn_ref, ddz_ref, dogl_ref, dgr_ref, dgdn_ref, dggl_ref):
        first = pl.program_id(0) == 0
        groups = ((odn_ref, z_ref, gdn_ref, dodn_ref, ddz_ref, dgdn_ref),
                  (ogl_ref, r_ref, ggl_ref, dogl_ref, dgr_ref, dggl_ref))
        for grp, (o_ref, gate_ref, gain_ref, do_ref, dgate_ref, dgain_ref) in enumerate(groups):
            gain = gain_ref[...]
            dgain = jnp.zeros((1, DN_D), F32)
            for h in range(NH):
                sl = slice(h * DN_D, (h + 1) * DN_D)
                o = o_ref[:, sl]
                z = gate_ref[:, sl]
                dm = dm_ref[:, grp * w + h * DN_D: grp * w + (h + 1) * DN_D]
                r = lax.rsqrt(jnp.mean(o * o, axis=-1, keepdims=True) + EPS)
                oh = o * r
                s = _sigmoid(z)
                dn = dm * (z * s)
                dgate_ref[:, sl] = (dm * (oh * gain) * (s * (1.0 + z * (1.0 - s)))).astype(dgate_ref.dtype)
                gdn = dn * gain
                do_ref[:, sl] = r * (gdn - oh * jnp.mean(oh * gdn, axis=-1, keepdims=True))
                dgain = dgain + jnp.sum(dn * oh, axis=0, keepdims=True)

            @pl.when(first)
            def _():
                dgain_ref[...] = dgain

            @pl.when(jnp.logical_not(first))
            def _():
                dgain_ref[...] += dgain

    row = pl.BlockSpec((tr, w), lambda i: (i, 0))
    vec = pl.BlockSpec((1, DN_D), lambda i: (0, 0))
    big = jax.ShapeDtypeStruct((n, w), F32)
    gate = jax.ShapeDtypeStruct((n, w), BF16)
    small = jax.ShapeDtypeStruct((1, DN_D), F32)
    return pl.pallas_call(
        body, name=name, grid=(n // tr,),
        in_specs=[pl.BlockSpec((tr, 2 * w), lambda i: (i, 0)), row, row,
                  pl.BlockSpec((tr, w), lambda i: (i, C_DZ // w)), pl.BlockSpec((tr, w), lambda i: (i, C_GR // w)), vec, vec],
        out_specs=[row, row, row, row, vec, vec],
        out_shape=[big, gate, big, gate, small, small],
        compiler_params=_cp(VMEM_BIG),
    )(dmix, o_dn, o_gla, projp, projp, g_dn, g_gla)


QKV_W = 3 * NH * DN_D
HALO = 8


def _conv_z(xs_ref, cw_ref, tt):
    z = cw_ref[0:1, :] * xs_ref[pl.ds(HALO - 3, tt), :]
    for j in range(1, 4):
        z = z + cw_ref[j:j + 1, :] * xs_ref[pl.ds(HALO - 3 + j, tt), :]
    return z


def _dnprep_fwd(projp, conv_w, *, bsz, t_seq, tt, name):
    n = bsz * t_seq
    per_seq = t_seq // tt
    hw = NH * DN_D

    def body(x_ref, halo_ref, cw_ref, q_ref, k_ref, v_ref, xs_ref):
        i = pl.program_id(1)
        xs_ref[0:HALO, :] = jnp.where(i == 0, 0.0, halo_ref[...])
        xs_ref[HALO:HALO + tt, :] = x_ref[...]
        z = _conv_z(xs_ref, cw_ref, tt)
        a = z * _sigmoid(z)
        for grp, o_ref in enumerate((q_ref, k_ref)):
            for h in range(NH):
                ah = a[:, grp * hw + h * DN_D: grp * hw + (h + 1) * DN_D]
                rs = lax.rsqrt(jnp.sum(ah * ah, axis=-1, keepdims=True) + EPS)
                o_ref[:, h * DN_D:(h + 1) * DN_D] = ah * rs
        v_ref[...] = a[:, 2 * hw:3 * hw]

    def halo_map(b, i):
        return (jnp.maximum((b * t_seq + i * tt) // HALO - 1, 0), 0)

    out = pl.BlockSpec((tt, hw), lambda b, i: (b * per_seq + i, 0))
    sds = jax.ShapeDtypeStruct((n, hw), F32)
    return pl.pallas_call(
        body, name=name, grid=(bsz, per_seq),
        in_specs=[pl.BlockSpec((tt, QKV_W), lambda b, i: (b * per_seq + i, 0)),
                  pl.BlockSpec((HALO, QKV_W), halo_map),
                  pl.BlockSpec((4, QKV_W), lambda b, i: (0, 0))],
        out_specs=[out, out, out], out_shape=[sds, sds, sds],
        scratch_shapes=[pltpu.VMEM((tt + HALO, QKV_W), F32)],
        compiler_params=_cp(VMEM_BIG),
    )(projp, projp, conv_w)


def _dnprep_bwd_a(projp, conv_w, dq, dk, dv, *, bsz, t_seq, tt, name):
    n = bsz * t_seq
    per_seq = t_seq // tt
    hw = NH * DN_D

    def body(x_ref, halo_ref, cw_ref, dq_ref, dk_ref, dv_ref, dz_ref, dcw_ref, xs_ref):
        b, i = pl.program_id(0), pl.program_id(1)
        xs_ref[0:HALO, :] = jnp.where(i == 0, 0.0, halo_ref[...])
        xs_ref[HALO:HALO + tt, :] = x_ref[...]
        z = _conv_z(xs_ref, cw_ref, tt)
        s = _sigmoid(z)
        a = z * s
        dsilu = s * (1.0 + z * (1.0 - s))
        for grp, d_ref in enumerate((dq_ref, dk_ref)):
            for h in range(NH):
                sl = slice(grp * hw + h * DN_D, grp * hw + (h + 1) * DN_D)
                ah = a[:, sl]
                rs = lax.rsqrt(jnp.sum(ah * ah, axis=-1, keepdims=True) + EPS)
                y = ah * rs
                dy = d_ref[:, h * DN_D:(h + 1) * DN_D]
                da = rs * (dy - y * jnp.sum(dy * y, axis=-1, keepdims=True))
                dz_ref[:, sl] = da * dsilu[:, sl]
        dz_ref[:, 2 * hw:3 * hw] = dv_ref[...] * dsilu[:, 2 * hw:3 * hw]
        dz = dz_ref[...]
        first = jnp.logical_and(b == 0, i == 0)
        for j in range(4):
            part = jnp.sum(dz * xs_ref[pl.ds(HALO - 3 + j, tt), :], axis=0, keepdims=True)

            @pl.when(first)
            def _():
                dcw_ref[j:j + 1, :] = part

            @pl.when(jnp.logical_not(first))
            def _():
                dcw_ref[j:j + 1, :] += part

    def halo_map(b, i):
        return (jnp.maximum((b * t_seq + i * tt) // HALO - 1, 0), 0)

    hrow = pl.BlockSpec((tt, hw), lambda b, i: (b * per_seq + i, 0))
    return pl.pallas_call(
        body, name=name, grid=(bsz, per_seq),
        in_specs=[pl.BlockSpec((tt, QKV_W), lambda b, i: (b * per_seq + i, 0)),
                  pl.BlockSpec((HALO, QKV_W), halo_map),
                  pl.BlockSpec((4, QKV_W), lambda b, i: (0, 0)), hrow, hrow, hrow],
        out_specs=[pl.BlockSpec((tt, QKV_W), lambda b, i: (b * per_seq + i, 0)),
                   pl.BlockSpec((4, QKV_W), lambda b, i: (0, 0))],
        out_shape=[jax.ShapeDtypeStruct((n, QKV_W), F32), jax.ShapeDtypeStruct((4, QKV_W), F32)],
        scratch_shapes=[pltpu.VMEM((tt + HALO, QKV_W), F32)],
        compiler_params=_cp(VMEM_BIG),
    )(projp, projp, conv_w, dq, dk, dv)


def _dnprep_bwd_b(dz, conv_w, *, bsz, t_seq, tt, name):
    n = bsz * t_seq
    per_seq = t_seq // tt
    last_blk = n // HALO - 1

    def body(dz_ref, halo_ref, cw_ref, dx_ref, ds_ref):
        i = pl.program_id(1)
        ds_ref[0:tt, :] = dz_ref[...]
        ds_ref[tt:tt + HALO, :] = jnp.where(i == per_seq - 1, 0.0, halo_ref[...])
        dx = cw_ref[0:1, :] * ds_ref[pl.ds(3, tt), :]
        for j in range(1, 4):
            dx = dx + cw_ref[j:j + 1, :] * ds_ref[pl.ds(3 - j, tt), :]
        dx_ref[...] = dx.astype(dx_ref.dtype)

    def halo_map(b, i):
        return (jnp.minimum((b * t_seq + (i + 1) * tt) // HALO, last_blk), 0)

    row = pl.BlockSpec((tt, QKV_W), lambda b, i: (b * per_seq + i, 0))
    return pl.pallas_call(
        body, name=name, grid=(bsz, per_seq),
        in_specs=[row, pl.BlockSpec((HALO, QKV_W), halo_map), pl.BlockSpec((4, QKV_W), lambda b, i: (0, 0))],
        out_specs=row, out_shape=jax.ShapeDtypeStruct((n, QKV_W), BF16),
        scratch_shapes=[pltpu.VMEM((tt + HALO, QKV_W), F32)],
        compiler_params=_cp(VMEM_BIG),
    )(dz, dz, conv_w)


def _masks64():
    r = _iota2((CHUNK, CHUNK), 0)
    c = _iota2((CHUNK, CHUNK), 1)
    return r, c


def _group(nc_seq, target=5):
    return max(g for g in range(1, target + 1) if nc_seq % g == 0)


def _round_robin(chains):
    live = list(chains)
    while live:
        nxt = []
        for ch in live:
            try:
                next(ch)
                nxt.append(ch)
            except StopIteration:
                pass
        live = nxt
        yield


def _run(chains):
    for _ in _round_robin(chains):
        pass


def _per_chunk(inner, kinds, grp):
    def body(*refs):
        chains = []
        for gi in range(grp):
            views = []
            for r, kind in zip(refs, kinds):
                if kind == "row":
                    views.append(r.at[pl.ds(gi * CHUNK, CHUNK)])
                elif kind == "lead":
                    views.append(r.at[pl.ds(gi, 1)])
                else:
                    views.append(r)
            chains.append(inner(gi, *views))
        _run(chains)
    return body


def _accumulate(ref, val, gi):
    if gi > 0:
        ref[...] += val
        return
    first = pl.program_id(0) == 0

    @pl.when(first)
    def _():
        ref[...] = val

    @pl.when(jnp.logical_not(first))
    def _():
        ref[...] += val


def _tri_inv(a_strict):
    r, c = _masks64()
    eye = (r == c).astype(F32)
    blk16 = (r // 16) == (c // 16)
    blk32 = (r // 32) == (c // 32)
    ld = jnp.where(blk16, a_strict, 0.0)
    x = eye - ld
    p = _nn(ld, ld)
    yield
    for step in range(3):
        xp = _nn(x, p)
        if step < 2:
            p = _nn(p, p)
        x = x + xp
        yield
    for lk in (jnp.where(jnp.logical_and(blk32, jnp.logical_not(blk16)), a_strict, 0.0),
               jnp.where(blk32, 0.0, a_strict)):
        y = x - eye
        s = lk + _nn(y, lk)
        yield
        x = x - s - _nn(s, y)
        yield
    return x


def _dn_gates(sa, alog, dtb, chunk_in_seq):
    rows = _iota2((CHUNK, LANE), 0)
    valid = jnp.logical_or(rows >= N_PAD, chunk_in_seq > 0)
    beta_t = _sigmoid(sa)
    ea = jnp.exp(alog)
    g_t = jnp.where(valid, -ea * _softplus(sa + dtb), 0.0)
    r, c = _masks64()
    ltri = (r >= c).astype(F32)
    gam_t = _nn_hi(ltri, g_t)
    return beta_t, g_t, gam_t, valid, ea


def _dn_intra_fwd(qn, kn, v, projp, alog_row, dtb_row, *, nc_seq, name):
    n = qn.shape[0]
    nct = n // CHUNK
    hw = NH * DN_D
    scale = DN_D ** -0.5

    grp = _group(nc_seq)

    def inner(gi, q_ref, k_ref, v_ref, sa_ref, al_ref, dt_ref, u_ref, w_ref, qg_ref, kd_ref, p_ref, t_ref, gl_ref):
        ci = (pl.program_id(0) * grp + gi) % nc_seq
        beta_t, _, gam_t, _, _ = _dn_gates(sa_ref[...], al_ref[...], dt_ref[...], ci)
        yield
        gam_tt = gam_t.T
        r, c = _masks64()
        incl = r >= c
        strict = r > c

        def head(h):
            sl = slice(h * DN_D, (h + 1) * DN_D)
            beta = beta_t[:, h:h + 1]
            gam = gam_t[:, 4 + h:5 + h]
            gam_row = gam_tt[4 + h:5 + h, :]
            gl = gam_t[CHUNK - 1:CHUNK, 4 + h:5 + h]
            dec = jnp.exp(jnp.where(incl, gam - gam_row, -jnp.inf))
            kh = k_ref[:, sl]
            qh = q_ref[:, sl] * scale
            vh = v_ref[:, sl]
            kk = _nt(kh, kh)
            qk = _nt(qh, kh)
            yield
            a = jnp.where(strict, beta * kk * dec, 0.0)
            tm = yield from _tri_inv(a)
            egam = jnp.exp(gam)
            u_ref[:, sl] = _nn(tm, beta * vh)
            w_ref[:, sl] = _nn(tm, (beta * egam) * kh)
            qg_ref[:, sl] = egam * qh
            kd_ref[:, sl] = jnp.exp(gl - gam) * kh
            p_ref[0, h] = qk * dec
            t_ref[0, h] = tm
            gl_ref[0, h:h + 1, :] = jnp.broadcast_to(jnp.exp(gl), (1, LANE))

        yield from _round_robin([head(h) for h in range(NH)])

    rows = grp * CHUNK
    row = pl.BlockSpec((rows, hw), lambda i: (i, 0))
    vec = pl.BlockSpec((1, LANE), lambda i: (0, 0))
    mat = pl.BlockSpec((grp, NH, CHUNK, CHUNK), lambda i: (i, 0, 0, 0))
    big = jax.ShapeDtypeStruct((n, hw), F32)
    msd = jax.ShapeDtypeStruct((nct, NH, CHUNK, CHUNK), F32)
    kinds = ["row"] * 4 + ["whole"] * 2 + ["row"] * 4 + ["lead"] * 3
    return pl.pallas_call(
        _per_chunk(inner, kinds, grp), name=name, grid=(nct // grp,),
        in_specs=[row, row, row, pl.BlockSpec((rows, LANE), lambda i: (i, C_SA // LANE)), vec, vec],
        out_specs=[row, row, row, row, mat, mat, pl.BlockSpec((grp, NH, LANE), lambda i: (i, 0, 0))],
        out_shape=[big, big, big, big, msd, msd, jax.ShapeDtypeStruct((nct, NH, LANE), F32)],
        compiler_params=_cp(VMEM_BIG),
    )(qn, kn, v, projp, alog_row, dtb_row)


def _dn_scan_fwd(u, w, qg, kd, p, gl, *, bsz, nc_seq, name):
    hw = NH * DN_D
    t_seq = nc_seq * CHUNK
    u, w, qg, kd = (z.reshape(bsz, t_seq, hw) for z in (u, w, qg, kd))
    p = p.reshape(bsz, nc_seq, NH, CHUNK, CHUNK)
    gl = gl.reshape(bsz, nc_seq, NH, LANE)

    def body(u_ref, w_ref, qg_ref, kd_ref, p_ref, gl_ref, o_ref, vn_ref, hist_ref, s_ref):
        @pl.when(pl.program_id(0) == 0)
        def _():
            s_ref[...] = jnp.zeros_like(s_ref)

        def chain(b, h):
            sl = slice(h * DN_D, (h + 1) * DN_D)
            s = s_ref[b, h]
            hist_ref[b, 0, h] = s
            ws = _nn(w_ref[b, :, sl], s)
            qs = _nn(qg_ref[b, :, sl], s)
            yield
            vn = u_ref[b, :, sl] - ws
            vn_ref[b, :, sl] = vn
            o_ref[b, :, sl] = qs + _nn(p_ref[b, 0, h], vn)
            s_ref[b, h] = gl_ref[b, 0, h:h + 1, :] * s + _tn(kd_ref[b, :, sl], vn)

        _run([chain(b, h) for b in range(bsz) for h in range(NH)])

    row = pl.BlockSpec((bsz, CHUNK, hw), lambda i: (0, i, 0))
    outs = pl.pallas_call(
        body, name=name, grid=(nc_seq,),
        in_specs=[row, row, row, row, pl.BlockSpec((bsz, 1, NH, CHUNK, CHUNK), lambda i: (0, i, 0, 0, 0)),
                  pl.BlockSpec((bsz, 1, NH, LANE), lambda i: (0, i, 0, 0))],
        out_specs=[row, row, pl.BlockSpec((bsz, 1, NH, DN_D, DN_D), lambda i: (0, i, 0, 0, 0))],
        out_shape=[jax.ShapeDtypeStruct((bsz, t_seq, hw), F32), jax.ShapeDtypeStruct((bsz, t_seq, hw), F32),
                   jax.ShapeDtypeStruct((bsz, nc_seq, NH, DN_D, DN_D), F32)],
        scratch_shapes=[pltpu.VMEM((bsz, NH, DN_D, DN_D), F32)],
        compiler_params=_cp(VMEM_BIG, ("arbitrary",)),
    )(u, w, qg, kd, p, gl)
    o, vn, hist = outs
    return o.reshape(bsz * t_seq, hw), vn.reshape(bsz * t_seq, hw), hist


def _dn_scan_bwd(do, w, qg, kd, vn, p, gl, hist, *, bsz, nc_seq, name):
    hw = NH * DN_D
    t_seq = nc_seq * CHUNK
    do, w, qg, kd, vn = (z.reshape(bsz, t_seq, hw) for z in (do, w, qg, kd, vn))
    p = p.reshape(bsz, nc_seq, NH, CHUNK, CHUNK)
    gl = gl.reshape(bsz, nc_seq, NH, LANE)

    def body(do_ref, w_ref, qg_ref, kd_ref, vn_ref, p_ref, gl_ref, hist_ref,
             du_ref, dw_ref, dqg_ref, dkd_ref, dgl_ref, ds_ref):
        @pl.when(pl.program_id(0) == 0)
        def _():
            ds_ref[...] = jnp.zeros_like(ds_ref)

        def chain(b, h):
            sl = slice(h * DN_D, (h + 1) * DN_D)
            s = hist_ref[b, 0, h]
            dsn = ds_ref[b, h]
            doh = do_ref[b, :, sl]
            vnh = vn_ref[b, :, sl]
            kdh = kd_ref[b, :, sl]
            dvn = _tn(p_ref[b, 0, h], doh) + _nn(kdh, dsn)
            du_ref[b, :, sl] = dvn
            dqg_ref[b, :, sl] = _nt(doh, s)
            dkd_ref[b, :, sl] = _nt(vnh, dsn)
            ds_part = _tn(qg_ref[b, :, sl], doh) + gl_ref[b, 0, h:h + 1, :] * dsn
            dgl = jnp.sum(jnp.sum(dsn * s, axis=0, keepdims=True), axis=1, keepdims=True)
            dgl_ref[b, 0, h:h + 1, :] = jnp.broadcast_to(dgl, (1, LANE))
            yield
            dw_ref[b, :, sl] = -_nt(dvn, s)
            ds_ref[b, h] = ds_part - _tn(w_ref[b, :, sl], dvn)

        _run([chain(b, h) for b in range(bsz) for h in range(NH)])

    rev = lambda i: nc_seq - 1 - i
    row = pl.BlockSpec((bsz, CHUNK, hw), lambda i: (0, rev(i), 0))
    mat = pl.BlockSpec((bsz, 1, NH, CHUNK, CHUNK), lambda i: (0, rev(i), 0, 0, 0))
    glb = pl.BlockSpec((bsz, 1, NH, LANE), lambda i: (0, rev(i), 0, 0))
    big = jax.ShapeDtypeStruct((bsz, t_seq, hw), F32)
    outs = pl.pallas_call(
        body, name=name, grid=(nc_seq,),
        in_specs=[row, row, row, row, row, mat, glb,
                  pl.BlockSpec((bsz, 1, NH, DN_D, DN_D), lambda i: (0, rev(i), 0, 0, 0))],
        out_specs=[row, row, row, row, glb],
        out_shape=[big, big, big, big, jax.ShapeDtypeStruct((bsz, nc_seq, NH, LANE), F32)],
        scratch_shapes=[pltpu.VMEM((bsz, NH, DN_D, DN_D), F32)],
        compiler_params=_cp(VMEM_BIG, ("arbitrary",)),
    )(do, w, qg, kd, vn, p, gl, hist)
    du, dw, dqg, dkd, dgl = outs
    n = bsz * t_seq
    return (du.reshape(n, hw), dw.reshape(n, hw), dqg.reshape(n, hw), dkd.reshape(n, hw),
            dgl.reshape(bsz * nc_seq, NH, LANE))


def _dn_intra_bwd(qn, kn, v, projp, alog_row, dtb_row, u, w, tmat, du, dw, dqg, dkd, do, vn, dgl, *, nc_seq, name):
    n = qn.shape[0]
    nct = n // CHUNK
    hw = NH * DN_D
    scale = DN_D ** -0.5

    grp = _group(nc_seq)

    def inner(gi, q_ref, k_ref, v_ref, sa_ref, al_ref, dt_ref, u_ref, w_ref, t_ref, du_ref, dw_ref, dqg_ref, dkd_ref,
              do_ref, vn_ref, dgl_ref, dq_ref, dk_ref, dv_ref, dsa_ref, dal_ref, ddt_ref):
        ci = (pl.program_id(0) * grp + gi) % nc_seq
        sa = sa_ref[...]
        beta_t, g_t, gam_t, valid, ea = _dn_gates(sa, al_ref[...], dt_ref[...], ci)
        yield
        lane = _iota2((CHUNK, LANE), 1)
        gates_t = jnp.where(lane < 4, beta_t, gam_t).T
        r, c = _masks64()
        incl, strict, upper, supper = r >= c, r > c, r <= c, r < c
        rows1 = _iota2((CHUNK, 1), 0)
        acc = [jnp.zeros((CHUNK, LANE), F32)]

        def head(h):
            sl = slice(h * DN_D, (h + 1) * DN_D)
            beta = beta_t[:, h:h + 1]
            gam = gam_t[:, 4 + h:5 + h]
            beta_row = gates_t[h:h + 1, :]
            gam_row = gates_t[4 + h:5 + h, :]
            gl = gam_t[CHUNK - 1:CHUNK, 4 + h:5 + h]
            dec = jnp.exp(jnp.where(incl, gam - gam_row, -jnp.inf))
            dec_t = jnp.exp(jnp.where(upper, gam_row - gam, -jnp.inf))
            kh = k_ref[:, sl]
            qh = q_ref[:, sl] * scale
            vh = v_ref[:, sl]
            uh = u_ref[:, sl]
            wh = w_ref[:, sl]
            doh = do_ref[:, sl]
            vnh = vn_ref[:, sl]
            kk = _nt(kh, kh)
            qk = _nt(qh, kh)
            qk_t = _nt(kh, qh)
            dp = _nt(doh, vnh)
            dp_t = _nt(vnh, doh)
            tm_t = t_ref[0, h].T
            dvb = _nn(tm_t, du_ref[:, sl])
            dkg = _nn(tm_t, dw_ref[:, sl])
            yield
            m = _nt(dvb, uh) + _nt(dkg, wh)
            m_t = _nt(uh, dvb) + _nt(wh, dkg)
            yield
            da = jnp.where(strict, -m, 0.0)
            da_t = jnp.where(supper, -m_t, 0.0)
            a = jnp.where(strict, beta * kk * dec, 0.0)
            a_t = jnp.where(supper, beta_row * kk * dec_t, 0.0)
            dad = da * dec
            dad_t = da_t * dec_t
            dbeta = jnp.sum(dad * kk, axis=1, keepdims=True)
            dpm = jnp.where(incl, dp, 0.0)
            dpm_t = jnp.where(upper, dp_t, 0.0)
            e = da * a + dpm * (qk * dec)
            e_t = da_t * a_t + dpm_t * (qk_t * dec_t)
            dgam = jnp.sum(e, axis=1, keepdims=True) - jnp.sum(e_t, axis=1, keepdims=True)
            egam = jnp.exp(gam)
            ekd = jnp.exp(gl - gam)
            dqgh = dqg_ref[:, sl]
            dkdh = dkd_ref[:, sl]
            dkh = (_nn(beta * dad, kh) + _nn(beta_row * dad_t, kh) + _nn(dpm_t * dec_t, qh)
                   + (beta * egam) * dkg + ekd * dkdh)
            dqh = _nn(dpm * dec, kh) + egam * dqgh
            dbeta = dbeta + jnp.sum(dkg * (egam * kh), axis=1, keepdims=True) + jnp.sum(dvb * vh, axis=1, keepdims=True)
            rkd = jnp.sum(dkdh * (ekd * kh), axis=1, keepdims=True)
            dgam = (dgam + jnp.sum(dkg * ((beta * egam) * kh), axis=1, keepdims=True)
                    + jnp.sum(dqgh * (egam * qh), axis=1, keepdims=True) - rkd)
            dgam_last = jnp.sum(rkd, axis=0, keepdims=True) + dgl_ref[0, h:h + 1, 0:1] * jnp.exp(gl)
            dgam = dgam + jnp.where(rows1 == CHUNK - 1, dgam_last, 0.0)
            dq_ref[:, sl] = dqh * scale
            dk_ref[:, sl] = dkh
            dv_ref[:, sl] = beta * dvb
            acc[0] = acc[0] + jnp.where(lane == h, dbeta, 0.0) + jnp.where(lane == 4 + h, dgam, 0.0)

        yield from _round_robin([head(h) for h in range(NH)])
        acc_t = acc[0]
        dg_t = _nn_hi(upper.astype(F32), acc_t)
        ddb = acc_t * beta_t * (1.0 - beta_t)
        dda = jnp.where(valid, dg_t * (-ea) * _sigmoid(sa + dt_ref[...]), 0.0)
        dsa_ref[...] = jnp.where(lane < 4, ddb, jnp.where(lane < 8, dda, 0.0)).astype(dsa_ref.dtype)
        in_g = jnp.logical_and(lane >= 4, lane < 8)
        dal = jnp.sum(jnp.where(in_g, dg_t * g_t, 0.0), axis=0, keepdims=True)
        ddt = jnp.sum(jnp.where(in_g, dda, 0.0), axis=0, keepdims=True)
        _accumulate(dal_ref, dal, gi)
        _accumulate(ddt_ref, ddt, gi)

    rows = grp * CHUNK
    row = pl.BlockSpec((rows, hw), lambda i: (i, 0))
    vec = pl.BlockSpec((1, LANE), lambda i: (0, 0))
    mat = pl.BlockSpec((grp, NH, CHUNK, CHUNK), lambda i: (i, 0, 0, 0))
    glb = pl.BlockSpec((grp, NH, LANE), lambda i: (i, 0, 0))
    big = jax.ShapeDtypeStruct((n, hw), F32)
    v128 = jax.ShapeDtypeStruct((1, LANE), F32)
    kinds = (["row"] * 4 + ["whole"] * 2 + ["row"] * 2 + ["lead"] + ["row"] * 6 + ["lead"]
             + ["row"] * 4 + ["whole"] * 2)
    return pl.pallas_call(
        _per_chunk(inner, kinds, grp), name=name, grid=(nct // grp,),
        in_specs=[row, row, row, pl.BlockSpec((rows, LANE), lambda i: (i, C_SA // LANE)), vec, vec,
                  row, row, mat, row, row, row, row, row, row, glb],
        out_specs=[row, row, row, pl.BlockSpec((rows, LANE), lambda i: (i, 0)), vec, vec],
        out_shape=[big, big, big, jax.ShapeDtypeStruct((n, LANE), BF16), v128, v128],
        compiler_params=_cp(VMEM_BIG, ("arbitrary",)),
    )(qn, kn, v, projp, alog_row, dtb_row, u, w, tmat, du, dw, dqg, dkd, do, vn, dgl)


GQ_W = NH * GLA_DK
GV_W = NH * GLA_DV
GLA_NORM = 16.0
MID = CHUNK // 2


def _gla_gates(sb, w2p, gb, chunk_in_seq):
    rows = _iota2((CHUNK, GQ_W), 0)
    valid = jnp.logical_or(rows >= N_PAD, chunk_in_seq > 0)
    graw = _nn_hi(sb, w2p) + gb
    yield
    g = jnp.where(valid, _logsigmoid(graw) * (1.0 / GLA_NORM), 0.0)
    r, c = _masks64()
    bcum = _nn_hi((r >= c).astype(F32), g)
    yield
    return graw, bcum, valid


def _head_mask(h):
    lane = _iota2((1, GQ_W), 1)
    return jnp.logical_and(lane >= h * GLA_DK, lane < (h + 1) * GLA_DK)


def _gla_intra_fwd(projp, w2p, gb, *, nc_seq, name):
    n = projp.shape[0]
    nct = n // CHUNK
    scale = GLA_DK ** -0.5

    grp = _group(nc_seq)
    rows = grp * CHUNK

    def inner(gi, qk_ref, v_ref, sb_ref, w2_ref, gb_ref, oi_ref, qg_ref, kd_ref, gl_ref):
        ci = (pl.program_id(0) * grp + gi) % nc_seq
        _, bc, _ = yield from _gla_gates(sb_ref[...], w2_ref[...], gb_ref[...], ci)
        bref = bc[MID:MID + 1, :]
        bl = bc[CHUNK - 1:CHUNK, :]
        q = qk_ref[:, 0:GQ_W] * scale
        k = qk_ref[:, GQ_W:2 * GQ_W]
        qi = q * jnp.exp(bc - bref)
        ki = k * jnp.exp(bref - bc)
        qg_ref[...] = q * jnp.exp(bc)
        kd_ref[...] = k * jnp.exp(bl - bc)
        gl_ref[0] = jnp.exp(bl)
        r, c = _masks64()
        incl = r >= c
        a = [jnp.where(incl, _nt(jnp.where(_head_mask(h), qi, 0.0), ki), 0.0) for h in range(NH)]
        yield
        for h in range(NH):
            oi_ref[:, h * GLA_DV:(h + 1) * GLA_DV] = _nn(a[h], v_ref[:, h * GLA_DV:(h + 1) * GLA_DV])

    kinds = ["row"] * 3 + ["whole"] * 2 + ["row"] * 3 + ["lead"]
    return pl.pallas_call(
        _per_chunk(inner, kinds, grp), name=name, grid=(nct // grp,),
        in_specs=[pl.BlockSpec((rows, 2 * GQ_W), lambda i: (i, C_GQK // (2 * GQ_W))),
                  pl.BlockSpec((rows, GV_W), lambda i: (i, C_GV // GV_W)),
                  pl.BlockSpec((rows, LANE), lambda i: (i, C_SB // LANE)),
                  pl.BlockSpec((LANE, GQ_W), lambda i: (0, 0)), pl.BlockSpec((1, GQ_W), lambda i: (0, 0))],
        out_specs=[pl.BlockSpec((rows, GV_W), lambda i: (i, 0)), pl.BlockSpec((rows, GQ_W), lambda i: (i, 0)),
                   pl.BlockSpec((rows, GQ_W), lambda i: (i, 0)), pl.BlockSpec((grp, 1, GQ_W), lambda i: (i, 0, 0))],
        out_shape=[jax.ShapeDtypeStruct((n, GV_W), F32), jax.ShapeDtypeStruct((n, GQ_W), F32),
                   jax.ShapeDtypeStruct((n, GQ_W), F32), jax.ShapeDtypeStruct((nct, 1, GQ_W), F32)],
        compiler_params=_cp(VMEM_BIG),
    )(projp, projp, projp, w2p, gb)


def _gla_scan_fwd(oi, qg, kd, gl, projp, *, bsz, nc_seq, name):
    t_seq = nc_seq * CHUNK
    oi = oi.reshape(bsz, t_seq, GV_W)
    qg, kd = qg.reshape(bsz, t_seq, GQ_W), kd.reshape(bsz, t_seq, GQ_W)
    gl = gl.reshape(bsz, nc_seq, 1, GQ_W)
    pj = projp.reshape(bsz, t_seq, PW)

    def body(oi_ref, qg_ref, kd_ref, gl_ref, v_ref, o_ref, hist_ref, st_ref):
        @pl.when(pl.program_id(0) == 0)
        def _():
            st_ref[...] = jnp.zeros_like(st_ref)

        for b in range(bsz):
            st = st_ref[b]
            hist_ref[b, 0] = st
            qgb = qg_ref[b]
            kdb = kd_ref[b]
            upd = jnp.zeros((GLA_DV, GQ_W), F32)
            for h in range(NH):
                sl = slice(h * GLA_DV, (h + 1) * GLA_DV)
                m = _head_mask(h)
                o_ref[b, :, sl] = oi_ref[b, :, sl] + _nt(jnp.where(m, qgb, 0.0), st)
                upd = upd + jnp.where(m, _tn(v_ref[b, :, sl], kdb), 0.0)
            st_ref[b] = gl_ref[b, 0] * st + upd

    outs = pl.pallas_call(
        body, name=name, grid=(nc_seq,),
        in_specs=[pl.BlockSpec((bsz, CHUNK, GV_W), lambda i: (0, i, 0)),
                  pl.BlockSpec((bsz, CHUNK, GQ_W), lambda i: (0, i, 0)),
                  pl.BlockSpec((bsz, CHUNK, GQ_W), lambda i: (0, i, 0)),
                  pl.BlockSpec((bsz, 1, 1, GQ_W), lambda i: (0, i, 0, 0)),
                  pl.BlockSpec((bsz, CHUNK, GV_W), lambda i: (0, i, C_GV // GV_W))],
        out_specs=[pl.BlockSpec((bsz, CHUNK, GV_W), lambda i: (0, i, 0)),
                   pl.BlockSpec((bsz, 1, GLA_DV, GQ_W), lambda i: (0, i, 0, 0))],
        out_shape=[jax.ShapeDtypeStruct((bsz, t_seq, GV_W), F32),
                   jax.ShapeDtypeStruct((bsz, nc_seq, GLA_DV, GQ_W), F32)],
        scratch_shapes=[pltpu.VMEM((bsz, GLA_DV, GQ_W), F32)],
        compiler_params=_cp(VMEM_BIG, ("arbitrary",)),
    )(oi, qg, kd, gl, pj)
    return outs[0].reshape(bsz * t_seq, GV_W), outs[1]


def _gla_scan_bwd(do, qg, kd, gl, projp, hist, *, bsz, nc_seq, name):
    t_seq = nc_seq * CHUNK
    do = do.reshape(bsz, t_seq, GV_W)
    qg, kd = qg.reshape(bsz, t_seq, GQ_W), kd.reshape(bsz, t_seq, GQ_W)
    gl = gl.reshape(bsz, nc_seq, 1, GQ_W)
    pj = projp.reshape(bsz, t_seq, PW)

    def body(do_ref, qg_ref, kd_ref, gl_ref, v_ref, hist_ref, dqg_ref, dkd_ref, dv_ref, dgl_ref, dst_ref):
        @pl.when(pl.program_id(0) == 0)
        def _():
            dst_ref[...] = jnp.zeros_like(dst_ref)

        for b in range(bsz):
            st = hist_ref[b, 0]
            dst = dst_ref[b]
            qgb = qg_ref[b]
            kdb = kd_ref[b]
            dqg = jnp.zeros((CHUNK, GQ_W), F32)
            dkd = jnp.zeros((CHUNK, GQ_W), F32)
            add = jnp.zeros((GLA_DV, GQ_W), F32)
            for h in range(NH):
                sl = slice(h * GLA_DV, (h + 1) * GLA_DV)
                m = _head_mask(h)
                doh = do_ref[b, :, sl]
                vh = v_ref[b, :, sl]
                dqg = dqg + jnp.where(m, _nn(doh, st), 0.0)
                dkd = dkd + jnp.where(m, _nn(vh, dst), 0.0)
                dv_ref[b, :, sl] = _nt(jnp.where(m, kdb, 0.0), dst)
                add = add + jnp.where(m, _tn(doh, qgb), 0.0)
            dqg_ref[b] = dqg
            dkd_ref[b] = dkd
            dgl_ref[b, 0] = jnp.sum(dst * st, axis=0, keepdims=True)
            dst_ref[b] = gl_ref[b, 0] * dst + add

    rev = lambda i: nc_seq - 1 - i
    outs = pl.pallas_call(
        body, name=name, grid=(nc_seq,),
        in_specs=[pl.BlockSpec((bsz, CHUNK, GV_W), lambda i: (0, rev(i), 0)),
                  pl.BlockSpec((bsz, CHUNK, GQ_W), lambda i: (0, rev(i), 0)),
                  pl.BlockSpec((bsz, CHUNK, GQ_W), lambda i: (0, rev(i), 0)),
                  pl.BlockSpec((bsz, 1, 1, GQ_W), lambda i: (0, rev(i), 0, 0)),
                  pl.BlockSpec((bsz, CHUNK, GV_W), lambda i: (0, rev(i), C_GV // GV_W)),
                  pl.BlockSpec((bsz, 1, GLA_DV, GQ_W), lambda i: (0, rev(i), 0, 0))],
        out_specs=[pl.BlockSpec((bsz, CHUNK, GQ_W), lambda i: (0, rev(i), 0)),
                   pl.BlockSpec((bsz, CHUNK, GQ_W), lambda i: (0, rev(i), 0)),
                   pl.BlockSpec((bsz, CHUNK, GV_W), lambda i: (0, rev(i), 0)),
                   pl.BlockSpec((bsz, 1, 1, GQ_W), lambda i: (0, rev(i), 0, 0))],
        out_shape=[jax.ShapeDtypeStruct((bsz, t_seq, GQ_W), F32), jax.ShapeDtypeStruct((bsz, t_seq, GQ_W), F32),
                   jax.ShapeDtypeStruct((bsz, t_seq, GV_W), F32), jax.ShapeDtypeStruct((bsz, nc_seq, 1, GQ_W), F32)],
        scratch_shapes=[pltpu.VMEM((bsz, GLA_DV, GQ_W), F32)],
        compiler_params=_cp(VMEM_BIG, ("arbitrary",)),
    )(do, qg, kd, gl, pj, hist)
    n = bsz * t_seq
    return (outs[0].reshape(n, GQ_W), outs[1].reshape(n, GQ_W), outs[2].reshape(n, GV_W),
            outs[3].reshape(bsz * nc_seq, 1, GQ_W))


def _gla_intra_bwd(projp, w2p, gb, do, dqg, dkd, dvi, dgl, *, nc_seq, name):
    n = projp.shape[0]
    nct = n // CHUNK
    scale = GLA_DK ** -0.5

    grp = _group(nc_seq)
    rows = grp * CHUNK

    def inner(gi, qk_ref, v_ref, sb_ref, w2_ref, gb_ref, do_ref, dqg_ref, dkd_ref, dvi_ref, dgl_ref,
              dqk_ref, dv_ref, dsb_ref, dw2_ref, dgb_ref):
        ci = (pl.program_id(0) * grp + gi) % nc_seq
        sb = sb_ref[...]
        w2 = w2_ref[...]
        graw, bc, valid = yield from _gla_gates(sb, w2, gb_ref[...], ci)
        bref = bc[MID:MID + 1, :]
        bl = bc[CHUNK - 1:CHUNK, :]
        q = qk_ref[:, 0:GQ_W] * scale
        k = qk_ref[:, GQ_W:2 * GQ_W]
        ex1 = jnp.exp(bc - bref)
        ex2 = jnp.exp(bref - bc)
        eb = jnp.exp(bc)
        ekd = jnp.exp(bl - bc)
        qi, ki = q * ex1, k * ex2
        r, c = _masks64()
        incl = r >= c
        upper = r <= c
        a_t, da, da_t = [], [], []
        for h in range(NH):
            sl = slice(h * GLA_DV, (h + 1) * GLA_DV)
            doh = do_ref[:, sl]
            vh = v_ref[:, sl]
            a_t.append(jnp.where(upper, _nt(jnp.where(_head_mask(h), ki, 0.0), qi), 0.0))
            da.append(jnp.where(incl, _nt(doh, vh), 0.0))
            da_t.append(jnp.where(upper, _nt(vh, doh), 0.0))
        yield
        dqi = jnp.zeros((CHUNK, GQ_W), F32)
        dki = jnp.zeros((CHUNK, GQ_W), F32)
        for h in range(NH):
            sl = slice(h * GLA_DV, (h + 1) * GLA_DV)
            m = _head_mask(h)
            dv_ref[:, sl] = (_nn(a_t[h], do_ref[:, sl]) + dvi_ref[:, sl]).astype(dv_ref.dtype)
            dqi = dqi + jnp.where(m, _nn(da[h], ki), 0.0)
            dki = dki + jnp.where(m, _nn(da_t[h], qi), 0.0)
        yield
        dqg = dqg_ref[...]
        dkd = dkd_ref[...]
        dqk_ref[:, 0:GQ_W] = ((dqi * ex1 + dqg * eb) * scale).astype(dqk_ref.dtype)
        dqk_ref[:, GQ_W:2 * GQ_W] = (dki * ex2 + dkd * ekd).astype(dqk_ref.dtype)
        t_qi, t_ki, t_kd = dqi * qi, dki * ki, dkd * (k * ekd)
        db = t_qi - t_ki + dqg * (q * eb) - t_kd
        dbref = jnp.sum(t_ki - t_qi, axis=0, keepdims=True)
        dbl = jnp.sum(t_kd, axis=0, keepdims=True) + dgl_ref[0] * jnp.exp(bl)
        rows = _iota2((CHUNK, GQ_W), 0)
        db = db + jnp.where(rows == MID, dbref, 0.0) + jnp.where(rows == CHUNK - 1, dbl, 0.0)
        dg = _nn_hi(upper.astype(F32), db)
        yield
        dgraw = jnp.where(valid, dg * (1.0 / GLA_NORM) * _sigmoid(-graw), 0.0)
        dsb_ref[...] = _nt_hi(dgraw, w2).astype(dsb_ref.dtype)
        dw2 = _tn_hi(sb, dgraw)
        dgb = jnp.sum(dgraw, axis=0, keepdims=True)
        _accumulate(dw2_ref, dw2, gi)
        _accumulate(dgb_ref, dgb, gi)

    rq = pl.BlockSpec((rows, GQ_W), lambda i: (i, 0))
    rv = pl.BlockSpec((rows, GV_W), lambda i: (i, 0))
    kinds = ["row"] * 3 + ["whole"] * 2 + ["row"] * 4 + ["lead"] + ["row"] * 3 + ["whole"] * 2
    return pl.pallas_call(
        _per_chunk(inner, kinds, grp), name=name, grid=(nct // grp,),
        in_specs=[pl.BlockSpec((rows, 2 * GQ_W), lambda i: (i, C_GQK // (2 * GQ_W))),
                  pl.BlockSpec((rows, GV_W), lambda i: (i, C_GV // GV_W)),
                  pl.BlockSpec((rows, LANE), lambda i: (i, C_SB // LANE)),
                  pl.BlockSpec((LANE, GQ_W), lambda i: (0, 0)), pl.BlockSpec((1, GQ_W), lambda i: (0, 0)),
                  rv, rq, rq, rv, pl.BlockSpec((grp, 1, GQ_W), lambda i: (i, 0, 0))],
        out_specs=[pl.BlockSpec((rows, 2 * GQ_W), lambda i: (i, 0)), rv, pl.BlockSpec((rows, LANE), lambda i: (i, 0)),
                   pl.BlockSpec((LANE, GQ_W), lambda i: (0, 0)), pl.BlockSpec((1, GQ_W), lambda i: (0, 0))],
        out_shape=[jax.ShapeDtypeStruct((n, 2 * GQ_W), BF16), jax.ShapeDtypeStruct((n, GV_W), BF16),
                   jax.ShapeDtypeStruct((n, LANE), BF16), jax.ShapeDtypeStruct((LANE, GQ_W), F32),
                   jax.ShapeDtypeStruct((1, GQ_W), F32)],
        compiler_params=_cp(VMEM_BIG, ("arbitrary",)),
    )(projp, projp, projp, w2p, gb, do, dqg, dkd, dvi, dgl)


SECTIONS = ((C_QKV, 1536), (C_DZ, 512), (C_GQK, 512), (C_GV, 512), (C_GR, 512), (C_SA, 128), (C_SB, 128))


def _inproj_bwd(secs, wp, h0, g1, dx1, *, tr, name):
    n, d = h0.shape

    def body(*refs):
        sec_refs = refs[:len(SECTIONS)]
        wp_ref, h0_ref, g_ref, dx1_ref, o_ref, dg_ref = refs[len(SECTIONS):]
        dh = None
        for s_ref, (off, wd) in zip(sec_refs, SECTIONS):
            part = _nt(s_ref[...], wp_ref[:, off:off + wd])
            dh = part if dh is None else dh + part
        dx, dg = _rms_bwd_math(h0_ref[...], g_ref[...], dh)
        o_ref[...] = dx1_ref[...] + dx

        @pl.when(pl.program_id(0) == 0)
        def _():
            dg_ref[...] = dg

        @pl.when(pl.program_id(0) > 0)
        def _():
            dg_ref[...] += dg

    row = pl.BlockSpec((tr, d), lambda i: (i, 0))
    vec = pl.BlockSpec((1, d), lambda i: (0, 0))
    return pl.pallas_call(
        body, name=name, grid=(n // tr,),
        in_specs=[pl.BlockSpec((tr, wd), lambda i: (i, 0)) for _, wd in SECTIONS]
        + [pl.BlockSpec((d, PW), lambda i: (0, 0)), row, vec, row],
        out_specs=[row, vec],
        out_shape=[jax.ShapeDtypeStruct((n, d), F32), jax.ShapeDtypeStruct((1, d), F32)],
        compiler_params=_cp(VMEM_BIG),
    )(*secs, wp, h0, g1, dx1)


def _adamw(w, g, m, v, *, name):
    r, c = w.shape
    tr = _tile(r, 256, 8) if r > 256 else r
    c1 = 1.0 - ADAM_B1 ** ADAM_STEP
    c2 = 1.0 - ADAM_B2 ** ADAM_STEP

    def body(w_ref, g_ref, m_ref, v_ref, d_ref, nm_ref, nv_ref):
        gv = g_ref[...]
        nm = ADAM_B1 * m_ref[...] + (1.0 - ADAM_B1) * gv
        nv = ADAM_B2 * v_ref[...] + (1.0 - ADAM_B2) * (gv * gv)
        d_ref[...] = -ADAM_LR * ((nm / c1) / (jnp.sqrt(nv / c2) + ADAM_EPS) + ADAM_WD * w_ref[...])
        nm_ref[...] = nm
        nv_ref[...] = nv

    blk = pl.BlockSpec((tr, c), lambda i: (i, 0))
    sds = jax.ShapeDtypeStruct((r, c), F32)
    return pl.pallas_call(
        body, name=name, grid=(r // tr,), in_specs=[blk] * 4, out_specs=[blk] * 3, out_shape=[sds] * 3,
        compiler_params=_cp(VMEM_BIG),
    )(w, g, m, v)


def _pair_sum(where, g, theirs, *, name):
    lead, r, cols = g.shape
    half = r // 2
    tr = _tile(half, 256, 16)
    nh = half // tr

    def body(w_ref, a_ref, b_ref, o_ref):
        o_ref[...] = (a_ref[...] + b_ref[...]).astype(o_ref.dtype)

    blk = pl.BlockSpec((1, tr, cols), lambda s, i, w: (s, i, 0))
    return pl.pallas_call(
        body, name=name,
        grid_spec=pltpu.PrefetchScalarGridSpec(
            num_scalar_prefetch=1, grid=(lead, nh),
            in_specs=[pl.BlockSpec((1, tr, cols), lambda s, i, w: (s, w[0] * nh + i, 0)), blk], out_specs=blk),
        out_shape=jax.ShapeDtypeStruct((lead, half, cols), BF16), compiler_params=_cp(VMEM_BIG),
    )(where, g, theirs)


def _chip_sum(where, pair, q, *, name):
    _, half, cols = pair.shape
    tr = _tile(half, 256, 16)
    nh = half // tr

    def body(w_ref, own_ref, q1_ref, q2_ref, q3_ref, o_ref):
        f = lambda ref: ref[0].astype(F32)
        o_ref[...] = ((f(own_ref) + f(q1_ref)) + f(q2_ref)) + f(q3_ref)

    def peer(d):
        return pl.BlockSpec((1, tr, cols), lambda i, w: ((w[1] + d) % N_CHIPS, i, 0))

    return pl.pallas_call(
        body, name=name,
        grid_spec=pltpu.PrefetchScalarGridSpec(
            num_scalar_prefetch=1, grid=(nh,),
            in_specs=[peer(0), peer(1), peer(2), peer(3)],
            out_specs=pl.BlockSpec((tr, cols), lambda i, w: (w[0] * nh + i, 0))),
        out_shape=jax.ShapeDtypeStruct((2 * half, cols), F32), compiler_params=_cp(VMEM_BIG),
    )(where, pair, q, q, q)


ANY = pl.BlockSpec(memory_space=pl.ANY)
VM = pl.BlockSpec(memory_space=pltpu.VMEM)
CAST_ROWS = 64


def _place():
    return lax.axis_index("x"), lax.axis_index("y"), lax.axis_index("c")


def _other_chips(x, y):
    return [(1 - x, y, 2 * (1 - x) + y), (x, 1 - y, 2 * x + 1 - y), (1 - x, 1 - y, 2 * (1 - x) + 1 - y)]


def _gather_weights(w_in, w_out, w_up, w_down, meta, conv, gw2):
    big = (w_in, w_out, w_up, w_down)
    small = (meta, conv, gw2)
    n_arr = len(big) + len(small)

    def body(*refs):
        ins = refs[:n_arr]
        outs = refs[n_arr:2 * n_arr]
        stage = refs[2 * n_arr:2 * n_arr + len(big)]
        send_sems, recv_sems, fsend_sems, frecv_sems, local_sems = refs[2 * n_arr + len(big):]
        x, y, c = _place()
        me = 2 * x + y
        srcs = []
        for k in range(n_arr):
            if k < len(big):
                src, dst = ins[k], stage[k]

                def cast_rows(i, carry, src=src, dst=dst):
                    rows = pl.ds(pl.multiple_of(i * CAST_ROWS, CAST_ROWS), CAST_ROWS)
                    dst[rows, :] = src[rows, :].astype(BF16)
                    return carry

                lax.fori_loop(0, src.shape[0] // CAST_ROWS, cast_rows, 0)
                srcs.append(stage[k])
            else:
                srcs.append(ins[k])
        local = [pltpu.make_async_copy(srcs[k], outs[k].at[me], local_sems.at[k]) for k in range(n_arr)]
        for cp in local:
            cp.start()
        def my_rows(k):
            r = srcs[k].shape[0]
            return pl.ds(c * (r // 2), r // 2) if k < len(big) else pl.ds(0, r)

        def sibling_rows(k):
            r = srcs[k].shape[0]
            return pl.ds((1 - c) * (r // 2), r // 2)

        def ici(k, d, px, py, block):
            rows = my_rows(k)
            return pltpu.make_async_remote_copy(
                src_ref=srcs[k].at[rows, :], dst_ref=outs[k].at[block, rows, :], send_sem=send_sems.at[k, d],
                recv_sem=recv_sems.at[k, d], device_id=(px, py, c), device_id_type=MESH)

        def pass_on(k, d, block, rows):
            return pltpu.make_async_remote_copy(
                src_ref=outs[k].at[block, rows, :], dst_ref=outs[k].at[block, rows, :], send_sem=fsend_sems.at[k, d],
                recv_sem=frecv_sems.at[k, d], device_id=(x, y, 1 - c), device_id_type=MESH)

        sends = []
        for k in range(n_arr):
            for d, (px, py, _) in enumerate(_other_chips(x, y)):
                cp = ici(k, d, px, py, me)
                cp.start()
                sends.append(cp)
        for k in range(n_arr):
            for d, (px, py, pj) in enumerate(_other_chips(x, y)):
                ici(k, d, px, py, pj).wait_recv()
                if k < len(big):
                    cp = pass_on(k, d, pj, my_rows(k))
                    cp.start()
                    sends.append(cp)
        for k in range(len(big)):
            for d, (_, _, pj) in enumerate(_other_chips(x, y)):
                pass_on(k, d, pj, sibling_rows(k)).wait_recv()
        for cp in sends:
            cp.wait_send()
        for cp in local:
            cp.wait()

    out_shape = [jax.ShapeDtypeStruct((N_CHIPS,) + a.shape, BF16) for a in big]
    out_shape += [jax.ShapeDtypeStruct((N_CHIPS,) + a.shape, F32) for a in small]
    sem = pltpu.SemaphoreType.DMA((n_arr, 3))
    return pl.pallas_call(
        body, name="gather_weights", in_specs=[VM] * n_arr, out_specs=[ANY] * n_arr, out_shape=out_shape,
        scratch_shapes=[pltpu.VMEM(a.shape, BF16) for a in big] + [sem, sem, sem, sem, pltpu.SemaphoreType.DMA((n_arr,))],
        compiler_params=_cp(VMEM_BIG),
    )(*big, *small)


def _row_chunks(rows, n_split):
    size = rows // n_split
    assert size * n_split == rows and size % 16 == 0, (rows, n_split)
    return [(s, pl.ds(s * size, size)) for s in range(n_split)], size


D2D_SPLIT = 4
ICI_SPLIT = 2


def _sibling_halves(grads):
    n_arr = len(grads)

    def body(*refs):
        ins = refs[:n_arr]
        theirs = refs[n_arr:2 * n_arr]
        send_sems, recv_sems = refs[2 * n_arr:]
        x, y, c = _place()
        copies = []
        for k in range(n_arr):
            half = ins[k].shape[1] // 2
            chunks, size = _row_chunks(half, D2D_SPLIT)
            for s, dst_rows in chunks:
                give = pltpu.make_async_remote_copy(
                    src_ref=ins[k].at[:, pl.ds((1 - c) * half + s * size, size), :], dst_ref=theirs[k].at[:, dst_rows, :],
                    send_sem=send_sems.at[k, s], recv_sem=recv_sems.at[k, s], device_id=(x, y, 1 - c),
                    device_id_type=MESH)
                give.start()
                copies.append(give)
        for give in copies:
            give.wait()

    halves = [jax.ShapeDtypeStruct((g.shape[0], g.shape[1] // 2, g.shape[2]), F32) for g in grads]
    sem = pltpu.SemaphoreType.DMA((n_arr, D2D_SPLIT))
    return pl.pallas_call(
        body, name="sibling_halves", in_specs=[ANY] * n_arr, out_specs=[ANY] * n_arr, out_shape=halves,
        scratch_shapes=[sem, sem],
    )(*grads)


def _chip_exchange(parts):
    n_arr = len(parts)

    def body(*refs):
        ins = refs[:n_arr]
        outs = refs[n_arr:2 * n_arr]
        send_sems, recv_sems = refs[2 * n_arr:]
        x, y, c = _place()
        me = 2 * x + y
        sends = []
        for k in range(n_arr):
            chunks, _ = _row_chunks(ins[k].shape[1], ICI_SPLIT)
            for d, (px, py, pj) in enumerate(_other_chips(x, y)):
                for s, rows in chunks:
                    cp = pltpu.make_async_remote_copy(
                        src_ref=ins[k].at[pj, rows, :], dst_ref=outs[k].at[me, rows, :], send_sem=send_sems.at[k, d, s],
                        recv_sem=recv_sems.at[k, d, s], device_id=(px, py, c), device_id_type=MESH)
                    cp.start()
                    sends.append(cp)
        for k in range(n_arr):
            chunks, _ = _row_chunks(ins[k].shape[1], ICI_SPLIT)
            for d, (px, py, pj) in enumerate(_other_chips(x, y)):
                for s, rows in chunks:
                    pltpu.make_async_remote_copy(
                        src_ref=ins[k].at[pj, rows, :], dst_ref=outs[k].at[pj, rows, :], send_sem=send_sems.at[k, d, s],
                        recv_sem=recv_sems.at[k, d, s], device_id=(px, py, c), device_id_type=MESH).wait_recv()
        for cp in sends:
            cp.wait_send()

    sem = pltpu.SemaphoreType.DMA((n_arr, 3, ICI_SPLIT))
    return pl.pallas_call(
        body, name="chip_exchange", in_specs=[ANY] * n_arr, out_specs=[ANY] * n_arr,
        out_shape=[jax.ShapeDtypeStruct(p.shape, p.dtype) for p in parts],
        scratch_shapes=[sem, sem],
    )(*parts)


def _sibling_join(bufs):
    n_arr = len(bufs)

    def body(*refs):
        bufs_out = refs[n_arr:2 * n_arr]
        send_sems, recv_sems = refs[2 * n_arr:]
        x, y, c = _place()
        copies = []
        for k in range(n_arr):
            half = bufs_out[k].shape[0] // 2
            chunks, size = _row_chunks(half, D2D_SPLIT)
            for s, _ in chunks:
                rows = pl.ds(c * half + s * size, size)
                give = pltpu.make_async_remote_copy(
                    src_ref=bufs_out[k].at[rows, :], dst_ref=bufs_out[k].at[rows, :], send_sem=send_sems.at[k, s],
                    recv_sem=recv_sems.at[k, s], device_id=(x, y, 1 - c), device_id_type=MESH)
                give.start()
                copies.append((k, s, half, size, give))
        for k, s, half, size, give in copies:
            rows = pl.ds((1 - c) * half + s * size, size)
            pltpu.make_async_remote_copy(
                src_ref=bufs_out[k].at[rows, :], dst_ref=bufs_out[k].at[rows, :], send_sem=send_sems.at[k, s],
                recv_sem=recv_sems.at[k, s], device_id=(x, y, 1 - c), device_id_type=MESH).wait_recv()
            give.wait_send()

    sem = pltpu.SemaphoreType.DMA((n_arr, D2D_SPLIT))
    return pl.pallas_call(
        body, name="sibling_join", in_specs=[ANY] * n_arr, out_specs=[ANY] * n_arr,
        out_shape=[jax.ShapeDtypeStruct(b.shape, F32) for b in bufs],
        input_output_aliases={k: k for k in range(n_arr)},
        scratch_shapes=[sem, sem],
    )(*bufs)


PACK_ROWS = 48


def _small_allreduce(pack):
    masks = [(dx, dy, dc) for dx in (0, 1) for dy in (0, 1) for dc in (0, 1)][1:]

    def body(p_ref, o_ref, buf, send_sems, recv_sems):
        x, y, c = _place()
        me = 4 * x + 2 * y + c
        buf[me] = p_ref[...]
        sends = []
        for k, (dx, dy, dc) in enumerate(masks):
            peer = (1 - x if dx else x, 1 - y if dy else y, 1 - c if dc else c)
            cp = pltpu.make_async_remote_copy(
                src_ref=p_ref, dst_ref=buf.at[me], send_sem=send_sems.at[k], recv_sem=recv_sems.at[k],
                device_id=peer, device_id_type=MESH)
            cp.start()
            sends.append(cp)
        for k, (dx, dy, dc) in enumerate(masks):
            peer = (1 - x if dx else x, 1 - y if dy else y, 1 - c if dc else c)
            pj = 4 * peer[0] + 2 * peer[1] + peer[2]
            pltpu.make_async_remote_copy(
                src_ref=p_ref, dst_ref=buf.at[pj], send_sem=send_sems.at[k], recv_sem=recv_sems.at[k],
                device_id=peer, device_id_type=MESH).wait_recv()
        for cp in sends:
            cp.wait_send()
        tot = buf[0]
        for k in range(1, 8):
            tot = tot + buf[k]
        o_ref[...] = tot
        o_ref[0:N_META, :] = tot[0:N_META] + tot[N_META:2 * N_META]

    return pl.pallas_call(
        body, name="small_allreduce", in_specs=[VM], out_specs=VM,
        out_shape=jax.ShapeDtypeStruct((PACK_ROWS, D_MODEL), F32),
        scratch_shapes=[pltpu.VMEM((8, PACK_ROWS, D_MODEL), F32), pltpu.SemaphoreType.DMA((7,)),
                        pltpu.SemaphoreType.DMA((7,))],
    )(pack)


def _pad_lanes(vec, offset):
    k = vec.shape[1]
    return jnp.concatenate([jnp.zeros((1, offset), F32), vec, jnp.zeros((1, LANE - offset - k), F32)], axis=1)


def _local_step(x, tgt, meta, norm1_g, wp, conv_w, a_log, dt_bias, dn_norm_g, gla_w2, gla_b, gla_norm_g,
                w_out, norm2_g, w_up, w_down, final_norm_g):
    bsz, s_len, d = x.shape
    t_seq = s_len + CHUNK
    nc_seq = t_seq // CHUNK
    n = bsz * t_seq
    tr = _tile(t_seq, 832)
    tt = _tile(t_seq, 416)

    lead = jnp.concatenate([jnp.zeros((N_PAD, d), F32), meta], axis=0)
    h0 = jnp.concatenate([jnp.broadcast_to(lead[None], (bsz, CHUNK, d)), x], axis=1).reshape(n, d)
    tgt_p = jnp.concatenate([jnp.zeros((bsz, CHUNK, d), F32), tgt], axis=1).reshape(n, d)
    alog_row = _pad_lanes(a_log, 4)
    dtb_row = _pad_lanes(dt_bias, 4)
    w2p = jnp.concatenate([gla_w2, jnp.zeros((LANE - GLA_RANK, GQ_W), F32)], axis=0)

    h = _rms_fwd(h0, norm1_g, tr=tr, name="norm1")
    (projp,) = _mm(h, wp, "nn", tm=tt, tn=PW, tk=d, out_dtypes=(F32,), name="in_proj")
    qn, kn, v = _dnprep_fwd(projp, conv_w, bsz=bsz, t_seq=t_seq, tt=tt, name="dn_prep")
    u, w, qg, kd, pmat, tmat, gl = _dn_intra_fwd(qn, kn, v, projp, alog_row, dtb_row, nc_seq=nc_seq, name="dn_intra")
    o_dn, vn, hist = _dn_scan_fwd(u, w, qg, kd, pmat, gl, bsz=bsz, nc_seq=nc_seq, name="dn_scan")
    oi, gqg, gkd, ggl = _gla_intra_fwd(projp, w2p, gla_b, nc_seq=nc_seq, name="gla_intra")
    o_gla, ghist = _gla_scan_fwd(oi, gqg, gkd, ggl, projp, bsz=bsz, nc_seq=nc_seq, name="gla_scan")
    mix = _gnorm_fwd(o_dn, o_gla, projp, dn_norm_g, gla_norm_g, tr=tr, name="gated_norm")
    (x1,) = _mm(mix, w_out, "nn", tm=tr, tn=d, tk=d, out_dtypes=(F32,), extras=(h0,),
                epilogue=lambda acc, res: (res + acc,), name="out_proj")
    h2 = _rms_fwd(x1, norm2_g, tr=tr, name="norm2")

    (act,) = _mm(h2, w_up, "nn", tm=tt, tn=D_FF, tk=d, out_dtypes=(BF16,),
                 epilogue=lambda acc: (jnp.square(jnp.maximum(acc, 0.0)),), name="mlp_up")
    (x2,) = _mm(act, w_down, "nn", tm=tr, tn=d, tk=D_FF, out_dtypes=(F32,), extras=(x1,),
                epilogue=lambda acc, res: (res + acc,), name="mlp_down")
    dx2, dx2b, d_final_g, loss_tile = _final_loss(x2, final_norm_g, tgt_p, t_seq=t_seq, tr=tr, name="final_loss")

    (dup,) = _mm(dx2b, w_down, "nt", tm=tt, tn=D_FF, tk=d, out_dtypes=(BF16,), extras=(act,),
                 epilogue=lambda acc, a: (acc * (2.0 * jnp.sqrt(a.astype(F32))),), name="mlp_down_bwd")
    (d_w_down,) = _mm(act, dx2b, "tn", tm=D_FF // 2, tn=d, tk=tr, out_dtypes=(F32,), name="w_down_grad")
    (d_w_up,) = _mm(h2, dup, "tn", tm=d, tn=D_FF // 2, tk=tr, out_dtypes=(F32,), name="w_up_grad")
    (dh2,) = _mm(dup, w_up, "nt", tm=tr, tn=d, tk=D_FF, out_dtypes=(F32,), name="mlp_up_bwd")
    dx1, dx1b, d_norm2_g = _rms_bwd_add(x1, norm2_g, dh2, dx2, tr=tr, name="norm2_bwd")

    (dmix,) = _mm(dx1b, w_out, "nt", tm=tr, tn=d, tk=d, out_dtypes=(F32,), name="out_proj_bwd")
    (d_w_out,) = _mm(mix, dx1b, "tn", tm=d, tn=d, tk=tr, out_dtypes=(F32,), name="w_out_grad")
    do_dn, ddz, do_gla, dgr, d_dn_norm_g, d_gla_norm_g = _gnorm_bwd(
        dmix, o_dn, o_gla, projp, dn_norm_g, gla_norm_g, tr=tr, name="gated_norm_bwd")
    du, dw, dqg, dkd, dgl = _dn_scan_bwd(do_dn, w, qg, kd, vn, pmat, gl, hist, bsz=bsz, nc_seq=nc_seq,
                                          name="dn_scan_bwd")
    dqn, dkn, dv, dsa, d_alog, d_dtb = _dn_intra_bwd(qn, kn, v, projp, alog_row, dtb_row, u, w, tmat,
                                                     du, dw, dqg, dkd, do_dn, vn, dgl, nc_seq=nc_seq,
                                                     name="dn_intra_bwd")
    dz, d_conv_w = _dnprep_bwd_a(projp, conv_w, dqn, dkn, dv, bsz=bsz, t_seq=t_seq, tt=tt, name="dn_prep_bwd")
    dcin = _dnprep_bwd_b(dz, conv_w, bsz=bsz, t_seq=t_seq, tt=tt, name="conv_bwd")
    gdqg, gdkd, gdvi, gdgl = _gla_scan_bwd(do_gla, gqg, gkd, ggl, projp, ghist, bsz=bsz, nc_seq=nc_seq,
                                            name="gla_scan_bwd")
    dgqk, dgv, dsb, d_w2p, d_gla_b = _gla_intra_bwd(projp, w2p, gla_b, do_gla, gdqg, gdkd, gdvi, gdgl,
                                                    nc_seq=nc_seq, name="gla_intra_bwd")

    secs = (dcin, ddz, dgqk, dgv, dgr, dsa, dsb)
    g_lo = _grad_tn(h, secs[0:2], tk=tr, name="w_in_grad_lo")
    g_hi = _grad_tn(h, secs[2:7], tk=tr, name="w_in_grad_hi")
    dh0, d_norm1_g = _inproj_bwd(secs, wp, h0, norm1_g, dx1, tr=tt, name="in_proj_bwd")
    dh0 = dh0.reshape(bsz, t_seq, d)
    grad_x = dh0[:, CHUNK:]
    d_meta_rows = dh0[:, N_PAD:CHUNK].reshape(bsz * N_META, d)

    sa0, sb0 = C_SA - C_GQK, C_SB - C_GQK
    d_w_in = jnp.concatenate([g_lo, g_hi[:, sa0:sa0 + 8], g_hi[:, 0:sa0], g_hi[:, sb0:sb0 + GLA_RANK]], axis=1)
    grads = dict(w_in=d_w_in, w_out=d_w_out, w_up=d_w_up, w_down=d_w_down, meta_rows=d_meta_rows,
                 norm1_g=d_norm1_g, conv_w=d_conv_w, a_log_tile=d_alog, dt_bias_tile=d_dtb, dn_norm_g=d_dn_norm_g,
                 gla_w2=d_w2p[0:GLA_RANK], gla_b=d_gla_b, gla_norm_g=d_gla_norm_g, norm2_g=d_norm2_g,
                 final_norm_g=d_final_g, loss_tile=loss_tile)
    return grad_x, grads


def _pad_layout(w_full):
    z = lambda k: jnp.zeros((w_full.shape[0], k), w_full.dtype)
    return jnp.concatenate([w_full[:, 0:2048], w_full[:, 2056:3592], w_full[:, 2048:2056], z(LANE - 8),
                            w_full[:, 3592:3608], z(LANE - GLA_RANK)], axis=1)


def _pack_small(g, bsz):
    assert bsz * N_META == 32
    row = jnp.concatenate([g["a_log_tile"], g["dt_bias_tile"], g["dn_norm_g"], g["gla_norm_g"], g["gla_b"],
                           g["loss_tile"], jnp.zeros((1, LANE), F32)], axis=1)
    return jnp.concatenate([g["meta_rows"], g["norm1_g"], g["conv_w"].reshape(6, D_MODEL), row,
                            g["gla_w2"].reshape(4, D_MODEL), g["norm2_g"], g["final_norm_g"],
                            jnp.zeros((2, D_MODEL), F32)], axis=0)


def kernel(x, meta_tokens, norm1_g, w_in, conv_w, a_log, dt_bias, dn_norm_g, gla_w2, gla_b, gla_norm_g, w_out, norm2_g, w_up, w_down, final_norm_g, loss_target, m_meta_tokens, m_norm1_g, m_w_in, m_conv_w, m_a_log, m_dt_bias, m_dn_norm_g, m_gla_w2, m_gla_b, m_gla_norm_g, m_w_out, m_norm2_g, m_w_up, m_w_down, m_final_norm_g, v_meta_tokens, v_norm1_g, v_w_in, v_conv_w, v_a_log, v_dt_bias, v_dn_norm_g, v_gla_w2, v_gla_b, v_gla_norm_g, v_w_out, v_norm2_g, v_w_up, v_w_down, v_final_norm_g):
    bsz = x.shape[0]
    chip = 2 * lax.axis_index("x") + lax.axis_index("y")

    shard_w = IN_WIDTH // N_CHIPS
    lane_pad = lambda a, wd: jnp.pad(a, ((0, 0), (0, wd - a.shape[1])))
    g_in, g_out, g_up, g_down, g_meta, g_conv, g_w2 = _gather_weights(
        lane_pad(w_in[0], D_MODEL), w_out[0], w_up[0], w_down[0], meta_tokens, conv_w[0], lane_pad(gla_w2[0], LANE))
    wp = _pad_layout(g_in[:, :, 0:shard_w].transpose(1, 0, 2).reshape(D_MODEL, IN_WIDTH))
    w_out_f = g_out.reshape(D_MODEL, D_MODEL)
    w_up_f = g_up.transpose(1, 0, 2).reshape(D_MODEL, D_FF)
    w_down_f = g_down.reshape(D_FF, D_MODEL)
    meta_f = g_meta.transpose(1, 0, 2).reshape(N_META, D_MODEL)
    conv_f = g_conv.transpose(1, 0, 2).reshape(4, QKV_W)
    w2_f = g_w2[:, :, 0:GQ_W // N_CHIPS].transpose(1, 0, 2).reshape(GLA_RANK, GQ_W)

    grad_x, g = _local_step(x, loss_target, meta_f, norm1_g, wp, conv_f, a_log, dt_bias, dn_norm_g, w2_f, gla_b,
                            gla_norm_g, w_out_f, norm2_g, w_up_f, w_down_f, final_norm_g.reshape(1, D_MODEL))

    shard_major = [
        jnp.pad(g["w_in"].reshape(D_MODEL, N_CHIPS, shard_w).transpose(1, 0, 2),
                ((0, 0), (0, 0), (0, D_MODEL - shard_w))),
        g["w_out"].reshape(N_CHIPS, D_MODEL // N_CHIPS, D_MODEL),
        g["w_up"].reshape(D_MODEL, N_CHIPS, D_FF // N_CHIPS).transpose(1, 0, 2),
        g["w_down"].reshape(N_CHIPS, D_FF // N_CHIPS, D_MODEL),
    ]
    where = jnp.stack([lax.axis_index("c"), chip]).astype(jnp.int32)
    theirs = _sibling_halves(shard_major)
    pair = [_pair_sum(where, a, b, name=f"pair_sum_{k}") for k, (a, b) in enumerate(zip(shard_major, theirs))]
    parts = _chip_exchange(pair)
    halves = [_chip_sum(where, p, q, name=f"chip_sum_{k}") for k, (p, q) in enumerate(zip(pair, parts))]
    gw_in, gw_out, gw_up, gw_down = _sibling_join(halves)
    gw_in = gw_in[:, 0:shard_w]

    red = _small_allreduce(_pack_small(g, bsz))
    g_meta_full = red[0:N_META]
    g_norm1 = red[32:33]
    g_conv_full = red[33:39].reshape(4, QKV_W)
    srow = red[39:40]
    g_alog, g_dtb = srow[:, 4:8], srow[:, LANE + 4:LANE + 8]
    g_dn_norm, g_gla_norm = srow[:, 2 * LANE:3 * LANE], srow[:, 3 * LANE:4 * LANE]
    g_gla_b = srow[:, 4 * LANE:6 * LANE]
    loss = srow[0, 6 * LANE]
    g_w2_full = red[40:44].reshape(GLA_RANK, GQ_W)
    g_norm2 = red[44:45]
    g_final = red[45:46]
    g_meta_sh = lax.dynamic_slice_in_dim(g_meta_full, chip * (D_MODEL // N_CHIPS), D_MODEL // N_CHIPS, axis=1)
    g_conv_sh = lax.dynamic_slice_in_dim(g_conv_full, chip * (QKV_W // N_CHIPS), QKV_W // N_CHIPS, axis=1)
    g_w2_sh = lax.dynamic_slice_in_dim(g_w2_full, chip * (GQ_W // N_CHIPS), GQ_W // N_CHIPS, axis=1)

    names = ["meta_tokens", "norm1_g", "w_in", "conv_w", "a_log", "dt_bias", "dn_norm_g", "gla_w2", "gla_b",
             "gla_norm_g", "w_out", "norm2_g", "w_up", "w_down", "final_norm_g"]
    weights = dict(meta_tokens=meta_tokens, norm1_g=norm1_g, w_in=w_in, conv_w=conv_w, a_log=a_log, dt_bias=dt_bias,
                   dn_norm_g=dn_norm_g, gla_w2=gla_w2, gla_b=gla_b, gla_norm_g=gla_norm_g, w_out=w_out,
                   norm2_g=norm2_g, w_up=w_up, w_down=w_down, final_norm_g=final_norm_g)
    ms = dict(meta_tokens=m_meta_tokens, norm1_g=m_norm1_g, w_in=m_w_in, conv_w=m_conv_w, a_log=m_a_log,
              dt_bias=m_dt_bias, dn_norm_g=m_dn_norm_g, gla_w2=m_gla_w2, gla_b=m_gla_b, gla_norm_g=m_gla_norm_g,
              w_out=m_w_out, norm2_g=m_norm2_g, w_up=m_w_up, w_down=m_w_down, final_norm_g=m_final_norm_g)
    vs = dict(meta_tokens=v_meta_tokens, norm1_g=v_norm1_g, w_in=v_w_in, conv_w=v_conv_w, a_log=v_a_log,
              dt_bias=v_dt_bias, dn_norm_g=v_dn_norm_g, gla_w2=v_gla_w2, gla_b=v_gla_b, gla_norm_g=v_gla_norm_g,
              w_out=v_w_out, norm2_g=v_norm2_g, w_up=v_w_up, w_down=v_w_down, final_norm_g=v_final_norm_g)
    grads2d = dict(meta_tokens=g_meta_sh, norm1_g=g_norm1, w_in=gw_in, conv_w=g_conv_sh, a_log=g_alog, dt_bias=g_dtb,
                   dn_norm_g=g_dn_norm, gla_w2=g_w2_sh, gla_b=g_gla_b, gla_norm_g=g_gla_norm, w_out=gw_out,
                   norm2_g=g_norm2, w_up=gw_up, w_down=gw_down, final_norm_g=g_final)
    out_g, out_d, out_m, out_v = [], [], [], []
    for nm in names:
        shape = weights[nm].shape
        g2 = grads2d[nm]
        as2d = lambda a: a.reshape(g2.shape)
        dlt, nm_, nv_ = _adamw(as2d(weights[nm]), g2, as2d(ms[nm]), as2d(vs[nm]), name=f"adamw_{nm}")
        out_g.append(g2.reshape(shape))
        out_d.append(dlt.reshape(shape))
        out_m.append(nm_.reshape(shape))
        out_v.append(nv_.reshape(shape))
    return (loss, grad_x, *out_g, *out_d, *out_m, *out_v)
```

```python
import functools

import jax
import jax.numpy as jnp
import numpy as np
from jax import lax
from jax.experimental import pallas as pl
from jax.experimental.pallas import tpu as pltpu

F32 = jnp.float32
BF16 = jnp.bfloat16
HI = lax.Precision.HIGHEST
MESH = pl.DeviceIdType.MESH

D_MODEL = 1024
N_META = 16
CHUNK = 64
N_PAD = CHUNK - N_META
NH = 4
DN_D = 128
GLA_DK = 64
GLA_DV = 128
GLA_RANK = 16
D_FF = 4 * D_MODEL
EPS = 1e-6
IN_WIDTH = 3608
C_QKV, C_DZ, C_GQK, C_GV, C_GR, C_SA, C_SB, PW = 0, 1536, 2048, 2560, 3072, 3584, 3712, 3840
LANE = 128
N_CHIPS = 4

ADAM_LR, ADAM_B1, ADAM_B2, ADAM_EPS, ADAM_WD, ADAM_STEP = 0.001, 0.9, 0.999, 1e-08, 0.01, 10

VMEM_BIG = 56 * 1024 * 1024


def _cp(vmem=None, sem=None):
    kw = {}
    if vmem is not None:
        kw["vmem_limit_bytes"] = vmem
    if sem is not None:
        kw["dimension_semantics"] = sem
    return pltpu.CompilerParams(**kw)


def _tile(n, target, mult=16):
    best = None
    for t in range(mult, min(n, target) + 1, mult):
        if n % t == 0:
            best = t
    assert best is not None, (n, target)
    return best


def _dot(a, b, dims, prec=None):
    return lax.dot_general(a, b, (dims, ((), ())), preferred_element_type=F32, precision=prec)


def _nn(a, b):
    return _dot(a.astype(BF16), b.astype(BF16), ((1,), (0,)))


def _nt(a, b):
    return _dot(a.astype(BF16), b.astype(BF16), ((1,), (1,)))


def _tn(a, b):
    return _dot(a.astype(BF16), b.astype(BF16), ((0,), (0,)))


def _nn_hi(a, b):
    return _dot(a, b, ((1,), (0,)), HI)


def _nt_hi(a, b):
    return _dot(a, b, ((1,), (1,)), HI)


def _tn_hi(a, b):
    return _dot(a, b, ((0,), (0,)), HI)


def _sigmoid(x):
    return 0.5 * jnp.tanh(0.5 * x) + 0.5


def _softplus(x):
    return jnp.maximum(x, 0.0) + jnp.log(1.0 + jnp.exp(-jnp.abs(x)))


def _logsigmoid(x):
    return -_softplus(-x)


def _iota2(shape, dim):
    return lax.broadcasted_iota(jnp.int32, shape, dim)


def _mm(a, b, mode, *, tm, tn, tk, out_dtypes, extras=(), epilogue=None, name, vmem=VMEM_BIG):
    if mode == "tn":
        K, M = a.shape
    else:
        M, K = a.shape
    N = b.shape[0] if mode == "nt" else b.shape[1]
    assert M % tm == 0 and N % tn == 0 and K % tk == 0, (name, M, N, K, tm, tn, tk)
    nk = K // tk
    n_ex, n_out = len(extras), len(out_dtypes)
    if mode == "tn":
        a_spec = pl.BlockSpec((tk, tm), lambda i, j, k: (k, i))
    else:
        a_spec = pl.BlockSpec((tm, tk), lambda i, j, k: (i, k))
    if mode == "nt":
        b_spec = pl.BlockSpec((tn, tk), lambda i, j, k: (j, k))
    else:
        b_spec = pl.BlockSpec((tk, tn), lambda i, j, k: (k, j))
    mn_spec = pl.BlockSpec((tm, tn), lambda i, j, k: (i, j))
    dims = {"nn": ((1,), (0,)), "nt": ((1,), (1,)), "tn": ((0,), (0,))}[mode]

    single = nk == 1
    direct = (not single) and epilogue is None and n_out == 1 and out_dtypes[0] == F32

    def body(*refs):
        a_ref, b_ref = refs[0], refs[1]
        ex_refs = refs[2:2 + n_ex]
        out_refs = refs[2 + n_ex:2 + n_ex + n_out]
        part = _dot(a_ref[...].astype(BF16), b_ref[...].astype(BF16), dims)

        def finish(acc):
            res = (acc,) if epilogue is None else epilogue(acc, *[e[...] for e in ex_refs])
            for o_ref, r in zip(out_refs, res):
                o_ref[...] = r.astype(o_ref.dtype)

        if single:
            finish(part)
            return
        acc_ref = out_refs[0] if direct else refs[2 + n_ex + n_out]
        k = pl.program_id(2)

        @pl.when(k == 0)
        def _():
            acc_ref[...] = part

        @pl.when(k > 0)
        def _():
            acc_ref[...] += part

        if not direct:
            @pl.when(k == nk - 1)
            def _():
                finish(acc_ref[...])

    outs = pl.pallas_call(
        body, name=name, grid=(M // tm, N // tn, nk),
        in_specs=[a_spec, b_spec] + [mn_spec] * n_ex,
        out_specs=[mn_spec] * n_out,
        out_shape=[jax.ShapeDtypeStruct((M, N), dt) for dt in out_dtypes],
        scratch_shapes=[] if (single or direct) else [pltpu.VMEM((tm, tn), F32)],
        compiler_params=_cp(vmem, ("parallel", "parallel", "arbitrary")),
    )(a, b, *extras)
    return tuple(outs)


def _grad_tn(a, secs, *, tk, name):
    kk, m = a.shape
    widths = [s.shape[1] for s in secs]
    total = sum(widths)
    nk = kk // tk

    def body(*refs):
        a_ref, sec_refs, o_ref = refs[0], refs[1:-1], refs[-1]
        cat = sec_refs[0][...] if len(sec_refs) == 1 else jnp.concatenate([s[...] for s in sec_refs], axis=1)
        part = _dot(a_ref[...].astype(BF16), cat.astype(BF16), ((0,), (0,)))
        k = pl.program_id(0)

        @pl.when(k == 0)
        def _():
            o_ref[...] = part

        @pl.when(k > 0)
        def _():
            o_ref[...] += part

    return pl.pallas_call(
        body, name=name, grid=(nk,),
        in_specs=[pl.BlockSpec((tk, m), lambda k: (k, 0))] + [pl.BlockSpec((tk, w), lambda k: (k, 0)) for w in widths],
        out_specs=pl.BlockSpec((m, total), lambda k: (0, 0)),
        out_shape=jax.ShapeDtypeStruct((m, total), F32),
        compiler_params=_cp(VMEM_BIG, ("arbitrary",)),
    )(a, *secs)


def _rms_fwd(x, g, *, tr, name):
    n, d = x.shape

    def body(x_ref, g_ref, o_ref):
        xv = x_ref[...]
        r = lax.rsqrt(jnp.mean(xv * xv, axis=-1, keepdims=True) + EPS)
        o_ref[...] = (xv * r * g_ref[...]).astype(o_ref.dtype)

    return pl.pallas_call(
        body, name=name, grid=(n // tr,),
        in_specs=[pl.BlockSpec((tr, d), lambda i: (i, 0)), pl.BlockSpec((1, d), lambda i: (0, 0))],
        out_specs=pl.BlockSpec((tr, d), lambda i: (i, 0)),
        out_shape=jax.ShapeDtypeStruct((n, d), BF16),
        compiler_params=_cp(VMEM_BIG),
    )(x, g)


def _rms_bwd_math(xv, g, dy):
    r = lax.rsqrt(jnp.mean(xv * xv, axis=-1, keepdims=True) + EPS)
    xh = xv * r
    gdy = dy * g
    dx = r * (gdy - xh * jnp.mean(xh * gdy, axis=-1, keepdims=True))
    return dx, jnp.sum(dy * xh, axis=0, keepdims=True)


def _rms_bwd_add(x, g, dy, res, *, tr, name):
    n, d = x.shape

    def body(x_ref, g_ref, dy_ref, res_ref, o_ref, ob_ref, dg_ref):
        dx, dg = _rms_bwd_math(x_ref[...], g_ref[...], dy_ref[...])
        tot = res_ref[...] + dx
        o_ref[...] = tot
        ob_ref[...] = tot.astype(BF16)

        @pl.when(pl.program_id(0) == 0)
        def _():
            dg_ref[...] = dg

        @pl.when(pl.program_id(0) > 0)
        def _():
            dg_ref[...] += dg

    row = pl.BlockSpec((tr, d), lambda i: (i, 0))
    vec = pl.BlockSpec((1, d), lambda i: (0, 0))
    return pl.pallas_call(
        body, name=name, grid=(n // tr,),
        in_specs=[row, vec, row, row], out_specs=[row, row, vec],
        out_shape=[jax.ShapeDtypeStruct((n, d), F32), jax.ShapeDtypeStruct((n, d), BF16),
                   jax.ShapeDtypeStruct((1, d), F32)],
        compiler_params=_cp(VMEM_BIG),
    )(x, g, dy, res)


def _final_loss(x2, gf, tgt, *, t_seq, tr, name):
    n, d = x2.shape
    per_seq = t_seq // tr

    def body(x_ref, g_ref, t_ref, dx_ref, dxb_ref, dg_ref, loss_ref):
        i = pl.program_id(0)
        xv = x_ref[...]
        g = g_ref[...]
        r = lax.rsqrt(jnp.mean(xv * xv, axis=-1, keepdims=True) + EPS)
        xh = xv * r
        pos = (i % per_seq) * tr + _iota2((tr, 1), 0)
        real = pos >= CHUNK
        err = jnp.where(real, xh * g - t_ref[...], 0.0)
        dy = err * (1.0 / d)
        gdy = dy * g
        dx = r * (gdy - xh * jnp.mean(xh * gdy, axis=-1, keepdims=True))
        dx_ref[...] = dx
        dxb_ref[...] = dx.astype(BF16)
        dg = jnp.sum(dy * xh, axis=0, keepdims=True)
        ls = 0.5 * jnp.sum(jnp.mean(err * err, axis=-1, keepdims=True), axis=0, keepdims=True)
        ls = jnp.where(_iota2((1, LANE), 1) == 0, ls, 0.0)

        @pl.when(i == 0)
        def _():
            dg_ref[...] = dg
            loss_ref[...] = ls

        @pl.when(i > 0)
        def _():
            dg_ref[...] += dg
            loss_ref[...] += ls

    row = pl.BlockSpec((tr, d), lambda i: (i, 0))
    vec = pl.BlockSpec((1, d), lambda i: (0, 0))
    one = pl.BlockSpec((1, LANE), lambda i: (0, 0))
    return pl.pallas_call(
        body, name=name, grid=(n // tr,),
        in_specs=[row, vec, row], out_specs=[row, row, vec, one],
        out_shape=[jax.ShapeDtypeStruct((n, d), F32), jax.ShapeDtypeStruct((n, d), BF16),
                   jax.ShapeDtypeStruct((1, d), F32), jax.ShapeDtypeStruct((1, LANE), F32)],
        compiler_params=_cp(VMEM_BIG),
    )(x2, gf, tgt)


def _gnorm_fwd(o_dn, o_gla, projp, g_dn, g_gla, *, tr, name):
    n = o_dn.shape[0]
    w = NH * DN_D

    def body(odn_ref, ogl_ref, z_ref, r_ref, gdn_ref, ggl_ref, mix_ref):
        for grp, (o_ref, gate_ref, gain_ref) in enumerate(((odn_ref, z_ref, gdn_ref), (ogl_ref, r_ref, ggl_ref))):
            gain = gain_ref[...]
            for h in range(NH):
                sl = slice(h * DN_D, (h + 1) * DN_D)
                o = o_ref[:, sl]
                z = gate_ref[:, sl]
                r = lax.rsqrt(jnp.mean(o * o, axis=-1, keepdims=True) + EPS)
                y = (o * r * gain) * (z * _sigmoid(z))
                mix_ref[:, grp * w + h * DN_D: grp * w + (h + 1) * DN_D] = y.astype(mix_ref.dtype)

    row = pl.BlockSpec((tr, w), lambda i: (i, 0))
    vec = pl.BlockSpec((1, DN_D), lambda i: (0, 0))
    return pl.pallas_call(
        body, name=name, grid=(n // tr,),
        in_specs=[row, row, pl.BlockSpec((tr, w), lambda i: (i, C_DZ // w)),
                  pl.BlockSpec((tr, w), lambda i: (i, C_GR // w)), vec, vec],
        out_specs=pl.BlockSpec((tr, 2 * w), lambda i: (i, 0)),
        out_shape=jax.ShapeDtypeStruct((n, 2 * w), BF16),
        compiler_params=_cp(VMEM_BIG),
    )(o_dn, o_gla, projp, projp, g_dn, g_gla)


def _gnorm_bwd(dmix, o_dn, o_gla, projp, g_dn, g_gla, *, tr, name):
    n = o_dn.shape[0]
    w = NH * DN_D

    def body(dm_ref, odn_ref, ogl_ref, z_ref, r_ref, gdn_ref, ggl_ref,
             dodn_ref, ddz_ref, dogl_ref, dgr_ref, dgdn_ref, dggl_ref):
        first = pl.program_id(0) == 0
        groups = ((odn_ref, z_ref, gdn_ref, dodn_ref, ddz_ref, dgdn_ref),
                  (ogl_ref, r_ref, ggl_ref, dogl_ref, dgr_ref, dggl_ref))
        for grp, (o_ref, gate_ref, gain_ref, do_ref, dgate_ref, dgain_ref) in enumerate(groups):
            gain = gain_ref[...]
            dgain = jnp.zeros((1, DN_D), F32)
            for h in range(NH):
                sl = slice(h * DN_D, (h + 1) * DN_D)
                o = o_ref[:, sl]
                z = gate_ref[:, sl]
                dm = dm_ref[:, grp * w + h * DN_D: grp * w + (h + 1) * DN_D]
                r = lax.rsqrt(jnp.mean(o * o, axis=-1, keepdims=True) + EPS)
                oh = o * r
                s = _sigmoid(z)
                dn = dm * (z * s)
                dgate_ref[:, sl] = (dm * (oh * gain) * (s * (1.0 + z * (1.0 - s)))).astype(dgate_ref.dtype)
                gdn = dn * gain
                do_ref[:, sl] = r * (gdn - oh * jnp.mean(oh * gdn, axis=-1, keepdims=True))
                dgain = dgain + jnp.sum(dn * oh, axis=0, keepdims=True)

            @pl.when(first)
            def _():
                dgain_ref[...] = dgain

            @pl.when(jnp.logical_not(first))
            def _():
                dgain_ref[...] += dgain

    row = pl.BlockSpec((tr, w), lambda i: (i, 0))
    vec = pl.BlockSpec((1, DN_D), lambda i: (0, 0))
    big = jax.ShapeDtypeStruct((n, w), F32)
    gate = jax.ShapeDtypeStruct((n, w), BF16)
    small = jax.ShapeDtypeStruct((1, DN_D), F32)
    return pl.pallas_call(
        body, name=name, grid=(n // tr,),
        in_specs=[pl.BlockSpec((tr, 2 * w), lambda i: (i, 0)), row, row,
                  pl.BlockSpec((tr, w), lambda i: (i, C_DZ // w)), pl.BlockSpec((tr, w), lambda i: (i, C_GR // w)), vec, vec],
        out_specs=[row, row, row, row, vec, vec],
        out_shape=[big, gate, big, gate, small, small],
        compiler_params=_cp(VMEM_BIG),
    )(dmix, o_dn, o_gla, projp, projp, g_dn, g_gla)


QKV_W = 3 * NH * DN_D
HALO = 8


def _conv_z(xs_ref, cw_ref, tt):
    z = cw_ref[0:1, :] * xs_ref[pl.ds(HALO - 3, tt), :]
    for j in range(1, 4):
        z = z + cw_ref[j:j + 1, :] * xs_ref[pl.ds(HALO - 3 + j, tt), :]
    return z


def _dnprep_fwd(projp, conv_w, *, bsz, t_seq, tt, name):
    n = bsz * t_seq
    per_seq = t_seq // tt
    hw = NH * DN_D

    def body(x_ref, halo_ref, cw_ref, q_ref, k_ref, v_ref, xs_ref):
        i = pl.program_id(1)
        xs_ref[0:HALO, :] = jnp.where(i == 0, 0.0, halo_ref[...])
        xs_ref[HALO:HALO + tt, :] = x_ref[...]
        z = _conv_z(xs_ref, cw_ref, tt)
        a = z * _sigmoid(z)
        for grp, o_ref in enumerate((q_ref, k_ref)):
            for h in range(NH):
                ah = a[:, grp * hw + h * DN_D: grp * hw + (h + 1) * DN_D]
                rs = lax.rsqrt(jnp.sum(ah * ah, axis=-1, keepdims=True) + EPS)
                o_ref[:, h * DN_D:(h + 1) * DN_D] = ah * rs
        v_ref[...] = a[:, 2 * hw:3 * hw]

    def halo_map(b, i):
        return (jnp.maximum((b * t_seq + i * tt) // HALO - 1, 0), 0)

    out = pl.BlockSpec((tt, hw), lambda b, i: (b * per_seq + i, 0))
    sds = jax.ShapeDtypeStruct((n, hw), F32)
    return pl.pallas_call(
        body, name=name, grid=(bsz, per_seq),
        in_specs=[pl.BlockSpec((tt, QKV_W), lambda b, i: (b * per_seq + i, 0)),
                  pl.BlockSpec((HALO, QKV_W), halo_map),
                  pl.BlockSpec((4, QKV_W), lambda b, i: (0, 0))],
        out_specs=[out, out, out], out_shape=[sds, sds, sds],
        scratch_shapes=[pltpu.VMEM((tt + HALO, QKV_W), F32)],
        compiler_params=_cp(VMEM_BIG),
    )(projp, projp, conv_w)


def _dnprep_bwd_a(projp, conv_w, dq, dk, dv, *, bsz, t_seq, tt, name):
    n = bsz * t_seq
    per_seq = t_seq // tt
    hw = NH * DN_D

    def body(x_ref, halo_ref, cw_ref, dq_ref, dk_ref, dv_ref, dz_ref, dcw_ref, xs_ref):
        b, i = pl.program_id(0), pl.program_id(1)
        xs_ref[0:HALO, :] = jnp.where(i == 0, 0.0, halo_ref[...])
        xs_ref[HALO:HALO + tt, :] = x_ref[...]
        z = _conv_z(xs_ref, cw_ref, tt)
        s = _sigmoid(z)
        a = z * s
        dsilu = s * (1.0 + z * (1.0 - s))
        for grp, d_ref in enumerate((dq_ref, dk_ref)):
            for h in range(NH):
                sl = slice(grp * hw + h * DN_D, grp * hw + (h + 1) * DN_D)
                ah = a[:, sl]
                rs = lax.rsqrt(jnp.sum(ah * ah, axis=-1, keepdims=True) + EPS)
                y = ah * rs
                dy = d_ref[:, h * DN_D:(h + 1) * DN_D]
                da = rs * (dy - y * jnp.sum(dy * y, axis=-1, keepdims=True))
                dz_ref[:, sl] = da * dsilu[:, sl]
        dz_ref[:, 2 * hw:3 * hw] = dv_ref[...] * dsilu[:, 2 * hw:3 * hw]
        dz = dz_ref[...]
        first = jnp.logical_and(b == 0, i == 0)
        for j in range(4):
            part = jnp.sum(dz * xs_ref[pl.ds(HALO - 3 + j, tt), :], axis=0, keepdims=True)

            @pl.when(first)
            def _():
                dcw_ref[j:j + 1, :] = part

            @pl.when(jnp.logical_not(first))
            def _():
                dcw_ref[j:j + 1, :] += part

    def halo_map(b, i):
        return (jnp.maximum((b * t_seq + i * tt) // HALO - 1, 0), 0)

    hrow = pl.BlockSpec((tt, hw), lambda b, i: (b * per_seq + i, 0))
    return pl.pallas_call(
        body, name=name, grid=(bsz, per_seq),
        in_specs=[pl.BlockSpec((tt, QKV_W), lambda b, i: (b * per_seq + i, 0)),
                  pl.BlockSpec((HALO, QKV_W), halo_map),
                  pl.BlockSpec((4, QKV_W), lambda b, i: (0, 0)), hrow, hrow, hrow],
        out_specs=[pl.BlockSpec((tt, QKV_W), lambda b, i: (b * per_seq + i, 0)),
                   pl.BlockSpec((4, QKV_W), lambda b, i: (0, 0))],
        out_shape=[jax.ShapeDtypeStruct((n, QKV_W), F32), jax.ShapeDtypeStruct((4, QKV_W), F32)],
        scratch_shapes=[pltpu.VMEM((tt + HALO, QKV_W), F32)],
        compiler_params=_cp(VMEM_BIG),
    )(projp, projp, conv_w, dq, dk, dv)


def _dnprep_bwd_b(dz, conv_w, *, bsz, t_seq, tt, name):
    n = bsz * t_seq
    per_seq = t_seq // tt
    last_blk = n // HALO - 1

    def body(dz_ref, halo_ref, cw_ref, dx_ref, ds_ref):
        i = pl.program_id(1)
        ds_ref[0:tt, :] = dz_ref[...]
        ds_ref[tt:tt + HALO, :] = jnp.where(i == per_seq - 1, 0.0, halo_ref[...])
        dx = cw_ref[0:1, :] * ds_ref[pl.ds(3, tt), :]
        for j in range(1, 4):
            dx = dx + cw_ref[j:j + 1, :] * ds_ref[pl.ds(3 - j, tt), :]
        dx_ref[...] = dx.astype(dx_ref.dtype)

    def halo_map(b, i):
        return (jnp.minimum((b * t_seq + (i + 1) * tt) // HALO, last_blk), 0)

    row = pl.BlockSpec((tt, QKV_W), lambda b, i: (b * per_seq + i, 0))
    return pl.pallas_call(
        body, name=name, grid=(bsz, per_seq),
        in_specs=[row, pl.BlockSpec((HALO, QKV_W), halo_map), pl.BlockSpec((4, QKV_W), lambda b, i: (0, 0))],
        out_specs=row, out_shape=jax.ShapeDtypeStruct((n, QKV_W), BF16),
        scratch_shapes=[pltpu.VMEM((tt + HALO, QKV_W), F32)],
        compiler_params=_cp(VMEM_BIG),
    )(dz, dz, conv_w)


def _masks64():
    r = _iota2((CHUNK, CHUNK), 0)
    c = _iota2((CHUNK, CHUNK), 1)
    return r, c


def _group(nc_seq, target=5):
    return max(g for g in range(1, target + 1) if nc_seq % g == 0)


def _round_robin(chains):
    live = list(chains)
    while live:
        nxt = []
        for ch in live:
            try:
                next(ch)
                nxt.append(ch)
            except StopIteration:
                pass
        live = nxt
        yield


def _run(chains):
    for _ in _round_robin(chains):
        pass


def _per_chunk(inner, kinds, grp):
    def body(*refs):
        chains = []
        for gi in range(grp):
            views = []
            for r, kind in zip(refs, kinds):
                if kind == "row":
                    views.append(r.at[pl.ds(gi * CHUNK, CHUNK)])
                elif kind == "lead":
                    views.append(r.at[pl.ds(gi, 1)])
                else:
                    views.append(r)
            chains.append(inner(gi, *views))
        _run(chains)
    return body


def _accumulate(ref, val, gi):
    if gi > 0:
        ref[...] += val
        return
    first = pl.program_id(0) == 0

    @pl.when(first)
    def _():
        ref[...] = val

    @pl.when(jnp.logical_not(first))
    def _():
        ref[...] += val


def _tri_inv(a_strict):
    r, c = _masks64()
    eye = (r == c).astype(F32)
    blk16 = (r // 16) == (c // 16)
    blk32 = (r // 32) == (c // 32)
    ld = jnp.where(blk16, a_strict, 0.0)
    x = eye - ld
    p = _nn(ld, ld)
    yield
    for step in range(3):
        xp = _nn(x, p)
        if step < 2:
            p = _nn(p, p)
        x = x + xp
        yield
    for lk in (jnp.where(jnp.logical_and(blk32, jnp.logical_not(blk16)), a_strict, 0.0),
               jnp.where(blk32, 0.0, a_strict)):
        y = x - eye
        s = lk + _nn(y, lk)
        yield
        x = x - s - _nn(s, y)
        yield
    return x


def _dn_gates(sa, alog, dtb, chunk_in_seq):
    rows = _iota2((CHUNK, LANE), 0)
    valid = jnp.logical_or(rows >= N_PAD, chunk_in_seq > 0)
    beta_t = _sigmoid(sa)
    ea = jnp.exp(alog)
    g_t = jnp.where(valid, -ea * _softplus(sa + dtb), 0.0)
    r, c = _masks64()
    ltri = (r >= c).astype(F32)
    gam_t = _nn_hi(ltri, g_t)
    return beta_t, g_t, gam_t, valid, ea


def _dn_intra_fwd(qn, kn, v, projp, alog_row, dtb_row, *, nc_seq, name):
    n = qn.shape[0]
    nct = n // CHUNK
    hw = NH * DN_D
    scale = DN_D ** -0.5

    grp = _group(nc_seq)

    def inner(gi, q_ref, k_ref, v_ref, sa_ref, al_ref, dt_ref, u_ref, w_ref, qg_ref, kd_ref, p_ref, t_ref, gl_ref):
        ci = (pl.program_id(0) * grp + gi) % nc_seq
        beta_t, _, gam_t, _, _ = _dn_gates(sa_ref[...], al_ref[...], dt_ref[...], ci)
        yield
        gam_tt = gam_t.T
        r, c = _masks64()
        incl = r >= c
        strict = r > c

        def head(h):
            sl = slice(h * DN_D, (h + 1) * DN_D)
            beta = beta_t[:, h:h + 1]
            gam = gam_t[:, 4 + h:5 + h]
            gam_row = gam_tt[4 + h:5 + h, :]
            gl = gam_t[CHUNK - 1:CHUNK, 4 + h:5 + h]
            dec = jnp.exp(jnp.where(incl, gam - gam_row, -jnp.inf))
            kh = k_ref[:, sl]
            qh = q_ref[:, sl] * scale
            vh = v_ref[:, sl]
            kk = _nt(kh, kh)
            qk = _nt(qh, kh)
            yield
            a = jnp.where(strict, beta * kk * dec, 0.0)
            tm = yield from _tri_inv(a)
            egam = jnp.exp(gam)
            u_ref[:, sl] = _nn(tm, beta * vh)
            w_ref[:, sl] = _nn(tm, (beta * egam) * kh)
            qg_ref[:, sl] = egam * qh
            kd_ref[:, sl] = jnp.exp(gl - gam) * kh
            p_ref[0, h] = qk * dec
            t_ref[0, h] = tm
            gl_ref[0, h:h + 1, :] = jnp.broadcast_to(jnp.exp(gl), (1, LANE))

        yield from _round_robin([head(h) for h in range(NH)])

    rows = grp * CHUNK
    row = pl.BlockSpec((rows, hw), lambda i: (i, 0))
    vec = pl.BlockSpec((1, LANE), lambda i: (0, 0))
    mat = pl.BlockSpec((grp, NH, CHUNK, CHUNK), lambda i: (i, 0, 0, 0))
    big = jax.ShapeDtypeStruct((n, hw), F32)
    msd = jax.ShapeDtypeStruct((nct, NH, CHUNK, CHUNK), F32)
    kinds = ["row"] * 4 + ["whole"] * 2 + ["row"] * 4 + ["lead"] * 3
    return pl.pallas_call(
        _per_chunk(inner, kinds, grp), name=name, grid=(nct // grp,),
        in_specs=[row, row, row, pl.BlockSpec((rows, LANE), lambda i: (i, C_SA // LANE)), vec, vec],
        out_specs=[row, row, row, row, mat, mat, pl.BlockSpec((grp, NH, LANE), lambda i: (i, 0, 0))],
        out_shape=[big, big, big, big, msd, msd, jax.ShapeDtypeStruct((nct, NH, LANE), F32)],
        compiler_params=_cp(VMEM_BIG),
    )(qn, kn, v, projp, alog_row, dtb_row)


def _dn_scan_fwd(u, w, qg, kd, p, gl, *, bsz, nc_seq, name):
    hw = NH * DN_D
    t_seq = nc_seq * CHUNK
    u, w, qg, kd = (z.reshape(bsz, t_seq, hw) for z in (u, w, qg, kd))
    p = p.reshape(bsz, nc_seq, NH, CHUNK, CHUNK)
    gl = gl.reshape(bsz, nc_seq, NH, LANE)

    def body(u_ref, w_ref, qg_ref, kd_ref, p_ref, gl_ref, o_ref, vn_ref, hist_ref, s_ref):
        @pl.when(pl.program_id(0) == 0)
        def _():
            s_ref[...] = jnp.zeros_like(s_ref)

        def chain(b, h):
            sl = slice(h * DN_D, (h + 1) * DN_D)
            s = s_ref[b, h]
            hist_ref[b, 0, h] = s
            ws = _nn(w_ref[b, :, sl], s)
            qs = _nn(qg_ref[b, :, sl], s)
            yield
            vn = u_ref[b, :, sl] - ws
            vn_ref[b, :, sl] = vn
            o_ref[b, :, sl] = qs + _nn(p_ref[b, 0, h], vn)
            s_ref[b, h] = gl_ref[b, 0, h:h + 1, :] * s + _tn(kd_ref[b, :, sl], vn)

        _run([chain(b, h) for b in range(bsz) for h in range(NH)])

    row = pl.BlockSpec((bsz, CHUNK, hw), lambda i: (0, i, 0))
    outs = pl.pallas_call(
        body, name=name, grid=(nc_seq,),
        in_specs=[row, row, row, row, pl.BlockSpec((bsz, 1, NH, CHUNK, CHUNK), lambda i: (0, i, 0, 0, 0)),
                  pl.BlockSpec((bsz, 1, NH, LANE), lambda i: (0, i, 0, 0))],
        out_specs=[row, row, pl.BlockSpec((bsz, 1, NH, DN_D, DN_D), lambda i: (0, i, 0, 0, 0))],
        out_shape=[jax.ShapeDtypeStruct((bsz, t_seq, hw), F32), jax.ShapeDtypeStruct((bsz, t_seq, hw), F32),
                   jax.ShapeDtypeStruct((bsz, nc_seq, NH, DN_D, DN_D), F32)],
        scratch_shapes=[pltpu.VMEM((bsz, NH, DN_D, DN_D), F32)],
        compiler_params=_cp(VMEM_BIG, ("arbitrary",)),
    )(u, w, qg, kd, p, gl)
    o, vn, hist = outs
    return o.reshape(bsz * t_seq, hw), vn.reshape(bsz * t_seq, hw), hist


def _dn_scan_bwd(do, w, qg, kd, vn, p, gl, hist, *, bsz, nc_seq, name):
    hw = NH * DN_D
    t_seq = nc_seq * CHUNK
    do, w, qg, kd, vn = (z.reshape(bsz, t_seq, hw) for z in (do, w, qg, kd, vn))
    p = p.reshape(bsz, nc_seq, NH, CHUNK, CHUNK)
    gl = gl.reshape(bsz, nc_seq, NH, LANE)

    def body(do_ref, w_ref, qg_ref, kd_ref, vn_ref, p_ref, gl_ref, hist_ref,
             du_ref, dw_ref, dqg_ref, dkd_ref, dgl_ref, ds_ref):
        @pl.when(pl.program_id(0) == 0)
        def _():
            ds_ref[...] = jnp.zeros_like(ds_ref)

        def chain(b, h):
            sl = slice(h * DN_D, (h + 1) * DN_D)
            s = hist_ref[b, 0, h]
            dsn = ds_ref[b, h]
            doh = do_ref[b, :, sl]
            vnh = vn_ref[b, :, sl]
            kdh = kd_ref[b, :, sl]
            dvn = _tn(p_ref[b, 0, h], doh) + _nn(kdh, dsn)
            du_ref[b, :, sl] = dvn
            dqg_ref[b, :, sl] = _nt(doh, s)
            dkd_ref[b, :, sl] = _nt(vnh, dsn)
            ds_part = _tn(qg_ref[b, :, sl], doh) + gl_ref[b, 0, h:h + 1, :] * dsn
            dgl = jnp.sum(jnp.sum(dsn * s, axis=0, keepdims=True), axis=1, keepdims=True)
            dgl_ref[b, 0, h:h + 1, :] = jnp.broadcast_to(dgl, (1, LANE))
            yield
            dw_ref[b, :, sl] = -_nt(dvn, s)
            ds_ref[b, h] = ds_part - _tn(w_ref[b, :, sl], dvn)

        _run([chain(b, h) for b in range(bsz) for h in range(NH)])

    rev = lambda i: nc_seq - 1 - i
    row = pl.BlockSpec((bsz, CHUNK, hw), lambda i: (0, rev(i), 0))
    mat = pl.BlockSpec((bsz, 1, NH, CHUNK, CHUNK), lambda i: (0, rev(i), 0, 0, 0))
    glb = pl.BlockSpec((bsz, 1, NH, LANE), lambda i: (0, rev(i), 0, 0))
    big = jax.ShapeDtypeStruct((bsz, t_seq, hw), F32)
    outs = pl.pallas_call(
        body, name=name, grid=(nc_seq,),
        in_specs=[row, row, row, row, row, mat, glb,
                  pl.BlockSpec((bsz, 1, NH, DN_D, DN_D), lambda i: (0, rev(i), 0, 0, 0))],
        out_specs=[row, row, row, row, glb],
        out_shape=[big, big, big, big, jax.ShapeDtypeStruct((bsz, nc_seq, NH, LANE), F32)],
        scratch_shapes=[pltpu.VMEM((bsz, NH, DN_D, DN_D), F32)],
        compiler_params=_cp(VMEM_BIG, ("arbitrary",)),
    )(do, w, qg, kd, vn, p, gl, hist)
    du, dw, dqg, dkd, dgl = outs
    n = bsz * t_seq
    return (du.reshape(n, hw), dw.reshape(n, hw), dqg.reshape(n, hw), dkd.reshape(n, hw),
            dgl.reshape(bsz * nc_seq, NH, LANE))


def _dn_intra_bwd(qn, kn, v, projp, alog_row, dtb_row, u, w, tmat, du, dw, dqg, dkd, do, vn, dgl, *, nc_seq, name):
    n = qn.shape[0]
    nct = n // CHUNK
    hw = NH * DN_D
    scale = DN_D ** -0.5

    grp = _group(nc_seq)

    def inner(gi, q_ref, k_ref, v_ref, sa_ref, al_ref, dt_ref, u_ref, w_ref, t_ref, du_ref, dw_ref, dqg_ref, dkd_ref,
              do_ref, vn_ref, dgl_ref, dq_ref, dk_ref, dv_ref, dsa_ref, dal_ref, ddt_ref):
        ci = (pl.program_id(0) * grp + gi) % nc_seq
        sa = sa_ref[...]
        beta_t, g_t, gam_t, valid, ea = _dn_gates(sa, al_ref[...], dt_ref[...], ci)
        yield
        lane = _iota2((CHUNK, LANE), 1)
        gates_t = jnp.where(lane < 4, beta_t, gam_t).T
        r, c = _masks64()
        incl, strict, upper, supper = r >= c, r > c, r <= c, r < c
        rows1 = _iota2((CHUNK, 1), 0)
        acc = [jnp.zeros((CHUNK, LANE), F32)]

        def head(h):
            sl = slice(h * DN_D, (h + 1) * DN_D)
            beta = beta_t[:, h:h + 1]
            gam = gam_t[:, 4 + h:5 + h]
            beta_row = gates_t[h:h + 1, :]
            gam_row = gates_t[4 + h:5 + h, :]
            gl = gam_t[CHUNK - 1:CHUNK, 4 + h:5 + h]
            dec = jnp.exp(jnp.where(incl, gam - gam_row, -jnp.inf))
            dec_t = jnp.exp(jnp.where(upper, gam_row - gam, -jnp.inf))
            kh = k_ref[:, sl]
            qh = q_ref[:, sl] * scale
            vh = v_ref[:, sl]
            uh = u_ref[:, sl]
            wh = w_ref[:, sl]
            doh = do_ref[:, sl]
            vnh = vn_ref[:, sl]
            kk = _nt(kh, kh)
            qk = _nt(qh, kh)
            qk_t = _nt(kh, qh)
            dp = _nt(doh, vnh)
            dp_t = _nt(vnh, doh)
            tm_t = t_ref[0, h].T
            dvb = _nn(tm_t, du_ref[:, sl])
            dkg = _nn(tm_t, dw_ref[:, sl])
            yield
            m = _nt(dvb, uh) + _nt(dkg, wh)
            m_t = _nt(uh, dvb) + _nt(wh, dkg)
            yield
            da = jnp.where(strict, -m, 0.0)
            da_t = jnp.where(supper, -m_t, 0.0)
            a = jnp.where(strict, beta * kk * dec, 0.0)
            a_t = jnp.where(supper, beta_row * kk * dec_t, 0.0)
            dad = da * dec
            dad_t = da_t * dec_t
            dbeta = jnp.sum(dad * kk, axis=1, keepdims=True)
            dpm = jnp.where(incl, dp, 0.0)
            dpm_t = jnp.where(upper, dp_t, 0.0)
            e = da * a + dpm * (qk * dec)
            e_t = da_t * a_t + dpm_t * (qk_t * dec_t)
            dgam = jnp.sum(e, axis=1, keepdims=True) - jnp.sum(e_t, axis=1, keepdims=True)
            egam = jnp.exp(gam)
            ekd = jnp.exp(gl - gam)
            dqgh = dqg_ref[:, sl]
            dkdh = dkd_ref[:, sl]
            dkh = (_nn(beta * dad, kh) + _nn(beta_row * dad_t, kh) + _nn(dpm_t * dec_t, qh)
                   + (beta * egam) * dkg + ekd * dkdh)
            dqh = _nn(dpm * dec, kh) + egam * dqgh
            dbeta = dbeta + jnp.sum(dkg * (egam * kh), axis=1, keepdims=True) + jnp.sum(dvb * vh, axis=1, keepdims=True)
            rkd = jnp.sum(dkdh * (ekd * kh), axis=1, keepdims=True)
            dgam = (dgam + jnp.sum(dkg * ((beta * egam) * kh), axis=1, keepdims=True)
                    + jnp.sum(dqgh * (egam * qh), axis=1, keepdims=True) - rkd)
            dgam_last = jnp.sum(rkd, axis=0, keepdims=True) + dgl_ref[0, h:h + 1, 0:1] * jnp.exp(gl)
            dgam = dgam + jnp.where(rows1 == CHUNK - 1, dgam_last, 0.0)
            dq_ref[:, sl] = dqh * scale
            dk_ref[:, sl] = dkh
            dv_ref[:, sl] = beta * dvb
            acc[0] = acc[0] + jnp.where(lane == h, dbeta, 0.0) + jnp.where(lane == 4 + h, dgam, 0.0)

        yield from _round_robin([head(h) for h in range(NH)])
        acc_t = acc[0]
        dg_t = _nn_hi(upper.astype(F32), acc_t)
        ddb = acc_t * beta_t * (1.0 - beta_t)
        dda = jnp.where(valid, dg_t * (-ea) * _sigmoid(sa + dt_ref[...]), 0.0)
        dsa_ref[...] = jnp.where(lane < 4, ddb, jnp.where(lane < 8, dda, 0.0)).astype(dsa_ref.dtype)
        in_g = jnp.logical_and(lane >= 4, lane < 8)
        dal = jnp.sum(jnp.where(in_g, dg_t * g_t, 0.0), axis=0, keepdims=True)
        ddt = jnp.sum(jnp.where(in_g, dda, 0.0), axis=0, keepdims=True)
        _accumulate(dal_ref, dal, gi)
        _accumulate(ddt_ref, ddt, gi)

    rows = grp * CHUNK
    row = pl.BlockSpec((rows, hw), lambda i: (i, 0))
    vec = pl.BlockSpec((1, LANE), lambda i: (0, 0))
    mat = pl.BlockSpec((grp, NH, CHUNK, CHUNK), lambda i: (i, 0, 0, 0))
    glb = pl.BlockSpec((grp, NH, LANE), lambda i: (i, 0, 0))
    big = jax.ShapeDtypeStruct((n, hw), F32)
    v128 = jax.ShapeDtypeStruct((1, LANE), F32)
    kinds = (["row"] * 4 + ["whole"] * 2 + ["row"] * 2 + ["lead"] + ["row"] * 6 + ["lead"]
             + ["row"] * 4 + ["whole"] * 2)
    return pl.pallas_call(
        _per_chunk(inner, kinds, grp), name=name, grid=(nct // grp,),
        in_specs=[row, row, row, pl.BlockSpec((rows, LANE), lambda i: (i, C_SA // LANE)), vec, vec,
                  row, row, mat, row, row, row, row, row, row, glb],
        out_specs=[row, row, row, pl.BlockSpec((rows, LANE), lambda i: (i, 0)), vec, vec],
        out_shape=[big, big, big, jax.ShapeDtypeStruct((n, LANE), BF16), v128, v128],
        compiler_params=_cp(VMEM_BIG, ("arbitrary",)),
    )(qn, kn, v, projp, alog_row, dtb_row, u, w, tmat, du, dw, dqg, dkd, do, vn, dgl)


GQ_W = NH * GLA_DK
GV_W = NH * GLA_DV
GLA_NORM = 16.0
MID = CHUNK // 2


def _gla_gates(sb, w2p, gb, chunk_in_seq):
    rows = _iota2((CHUNK, GQ_W), 0)
    valid = jnp.logical_or(rows >= N_PAD, chunk_in_seq > 0)
    graw = _nn_hi(sb, w2p) + gb
    yield
    g = jnp.where(valid, _logsigmoid(graw) * (1.0 / GLA_NORM), 0.0)
    r, c = _masks64()
    bcum = _nn_hi((r >= c).astype(F32), g)
    yield
    return graw, bcum, valid


def _head_mask(h):
    lane = _iota2((1, GQ_W), 1)
    return jnp.logical_and(lane >= h * GLA_DK, lane < (h + 1) * GLA_DK)


def _gla_intra_fwd(projp, w2p, gb, *, nc_seq, name):
    n = projp.shape[0]
    nct = n // CHUNK
    scale = GLA_DK ** -0.5

    grp = _group(nc_seq)
    rows = grp * CHUNK

    def inner(gi, qk_ref, v_ref, sb_ref, w2_ref, gb_ref, oi_ref, qg_ref, kd_ref, gl_ref):
        ci = (pl.program_id(0) * grp + gi) % nc_seq
        _, bc, _ = yield from _gla_gates(sb_ref[...], w2_ref[...], gb_ref[...], ci)
        bref = bc[MID:MID + 1, :]
        bl = bc[CHUNK - 1:CHUNK, :]
        q = qk_ref[:, 0:GQ_W] * scale
        k = qk_ref[:, GQ_W:2 * GQ_W]
        qi = q * jnp.exp(bc - bref)
        ki = k * jnp.exp(bref - bc)
        qg_ref[...] = q * jnp.exp(bc)
        kd_ref[...] = k * jnp.exp(bl - bc)
        gl_ref[0] = jnp.exp(bl)
        r, c = _masks64()
        incl = r >= c
        a = [jnp.where(incl, _nt(jnp.where(_head_mask(h), qi, 0.0), ki), 0.0) for h in range(NH)]
        yield
        for h in range(NH):
            oi_ref[:, h * GLA_DV:(h + 1) * GLA_DV] = _nn(a[h], v_ref[:, h * GLA_DV:(h + 1) * GLA_DV])

    kinds = ["row"] * 3 + ["whole"] * 2 + ["row"] * 3 + ["lead"]
    return pl.pallas_call(
        _per_chunk(inner, kinds, grp), name=name, grid=(nct // grp,),
        in_specs=[pl.BlockSpec((rows, 2 * GQ_W), lambda i: (i, C_GQK // (2 * GQ_W))),
                  pl.BlockSpec((rows, GV_W), lambda i: (i, C_GV // GV_W)),
                  pl.BlockSpec((rows, LANE), lambda i: (i, C_SB // LANE)),
                  pl.BlockSpec((LANE, GQ_W), lambda i: (0, 0)), pl.BlockSpec((1, GQ_W), lambda i: (0, 0))],
        out_specs=[pl.BlockSpec((rows, GV_W), lambda i: (i, 0)), pl.BlockSpec((rows, GQ_W), lambda i: (i, 0)),
                   pl.BlockSpec((rows, GQ_W), lambda i: (i, 0)), pl.BlockSpec((grp, 1, GQ_W), lambda i: (i, 0, 0))],
        out_shape=[jax.ShapeDtypeStruct((n, GV_W), F32), jax.ShapeDtypeStruct((n, GQ_W), F32),
                   jax.ShapeDtypeStruct((n, GQ_W), F32), jax.ShapeDtypeStruct((nct, 1, GQ_W), F32)],
        compiler_params=_cp(VMEM_BIG),
    )(projp, projp, projp, w2p, gb)


def _gla_scan_fwd(oi, qg, kd, gl, projp, *, bsz, nc_seq, name):
    t_seq = nc_seq * CHUNK
    oi = oi.reshape(bsz, t_seq, GV_W)
    qg, kd = qg.reshape(bsz, t_seq, GQ_W), kd.reshape(bsz, t_seq, GQ_W)
    gl = gl.reshape(bsz, nc_seq, 1, GQ_W)
    pj = projp.reshape(bsz, t_seq, PW)

    def body(oi_ref, qg_ref, kd_ref, gl_ref, v_ref, o_ref, hist_ref, st_ref):
        @pl.when(pl.program_id(0) == 0)
        def _():
            st_ref[...] = jnp.zeros_like(st_ref)

        for b in range(bsz):
            st = st_ref[b]
            hist_ref[b, 0] = st
            qgb = qg_ref[b]
            kdb = kd_ref[b]
            upd = jnp.zeros((GLA_DV, GQ_W), F32)
            for h in range(NH):
                sl = slice(h * GLA_DV, (h + 1) * GLA_DV)
                m = _head_mask(h)
                o_ref[b, :, sl] = oi_ref[b, :, sl] + _nt(jnp.where(m, qgb, 0.0), st)
                upd = upd + jnp.where(m, _tn(v_ref[b, :, sl], kdb), 0.0)
            st_ref[b] = gl_ref[b, 0] * st + upd

    outs = pl.pallas_call(
        body, name=name, grid=(nc_seq,),
        in_specs=[pl.BlockSpec((bsz, CHUNK, GV_W), lambda i: (0, i, 0)),
                  pl.BlockSpec((bsz, CHUNK, GQ_W), lambda i: (0, i, 0)),
                  pl.BlockSpec((bsz, CHUNK, GQ_W), lambda i: (0, i, 0)),
                  pl.BlockSpec((bsz, 1, 1, GQ_W), lambda i: (0, i, 0, 0)),
                  pl.BlockSpec((bsz, CHUNK, GV_W), lambda i: (0, i, C_GV // GV_W))],
        out_specs=[pl.BlockSpec((bsz, CHUNK, GV_W), lambda i: (0, i, 0)),
                   pl.BlockSpec((bsz, 1, GLA_DV, GQ_W), lambda i: (0, i, 0, 0))],
        out_shape=[jax.ShapeDtypeStruct((bsz, t_seq, GV_W), F32),
                   jax.ShapeDtypeStruct((bsz, nc_seq, GLA_DV, GQ_W), F32)],
        scratch_shapes=[pltpu.VMEM((bsz, GLA_DV, GQ_W), F32)],
        compiler_params=_cp(VMEM_BIG, ("arbitrary",)),
    )(oi, qg, kd, gl, pj)
    return outs[0].reshape(bsz * t_seq, GV_W), outs[1]


def _gla_scan_bwd(do, qg, kd, gl, projp, hist, *, bsz, nc_seq, name):
    t_seq = nc_seq * CHUNK
    do = do.reshape(bsz, t_seq, GV_W)
    qg, kd = qg.reshape(bsz, t_seq, GQ_W), kd.reshape(bsz, t_seq, GQ_W)
    gl = gl.reshape(bsz, nc_seq, 1, GQ_W)
    pj = projp.reshape(bsz, t_seq, PW)

    def body(do_ref, qg_ref, kd_ref, gl_ref, v_ref, hist_ref, dqg_ref, dkd_ref, dv_ref, dgl_ref, dst_ref):
        @pl.when(pl.program_id(0) == 0)
        def _():
            dst_ref[...] = jnp.zeros_like(dst_ref)

        for b in range(bsz):
            st = hist_ref[b, 0]
            dst = dst_ref[b]
            qgb = qg_ref[b]
            kdb = kd_ref[b]
            dqg = jnp.zeros((CHUNK, GQ_W), F32)
            dkd = jnp.zeros((CHUNK, GQ_W), F32)
            add = jnp.zeros((GLA_DV, GQ_W), F32)
            for h in range(NH):
                sl = slice(h * GLA_DV, (h + 1) * GLA_DV)
                m = _head_mask(h)
                doh = do_ref[b, :, sl]
                vh = v_ref[b, :, sl]
                dqg = dqg + jnp.where(m, _nn(doh, st), 0.0)
                dkd = dkd + jnp.where(m, _nn(vh, dst), 0.0)
                dv_ref[b, :, sl] = _nt(jnp.where(m, kdb, 0.0), dst)
                add = add + jnp.where(m, _tn(doh, qgb), 0.0)
            dqg_ref[b] = dqg
            dkd_ref[b] = dkd
            dgl_ref[b, 0] = jnp.sum(dst * st, axis=0, keepdims=True)
            dst_ref[b] = gl_ref[b, 0] * dst + add

    rev = lambda i: nc_seq - 1 - i
    outs = pl.pallas_call(
        body, name=name, grid=(nc_seq,),
        in_specs=[pl.BlockSpec((bsz, CHUNK, GV_W), lambda i: (0, rev(i), 0)),
                  pl.BlockSpec((bsz, CHUNK, GQ_W), lambda i: (0, rev(i), 0)),
                  pl.BlockSpec((bsz, CHUNK, GQ_W), lambda i: (0, rev(i), 0)),
                  pl.BlockSpec((bsz, 1, 1, GQ_W), lambda i: (0, rev(i), 0, 0)),
                  pl.BlockSpec((bsz, CHUNK, GV_W), lambda i: (0, rev(i), C_GV // GV_W)),
                  pl.BlockSpec((bsz, 1, GLA_DV, GQ_W), lambda i: (0, rev(i), 0, 0))],
        out_specs=[pl.BlockSpec((bsz, CHUNK, GQ_W), lambda i: (0, rev(i), 0)),
                   pl.BlockSpec((bsz, CHUNK, GQ_W), lambda i: (0, rev(i), 0)),
                   pl.BlockSpec((bsz, CHUNK, GV_W), lambda i: (0, rev(i), 0)),
                   pl.BlockSpec((bsz, 1, 1, GQ_W), lambda i: (0, rev(i), 0, 0))],
        out_shape=[jax.ShapeDtypeStruct((bsz, t_seq, GQ_W), F32), jax.ShapeDtypeStruct((bsz, t_seq, GQ_W), F32),
                   jax.ShapeDtypeStruct((bsz, t_seq, GV_W), F32), jax.ShapeDtypeStruct((bsz, nc_seq, 1, GQ_W), F32)],
        scratch_shapes=[pltpu.VMEM((bsz, GLA_DV, GQ_W), F32)],
        compiler_params=_cp(VMEM_BIG, ("arbitrary",)),
    )(do, qg, kd, gl, pj, hist)
    n = bsz * t_seq
    return (outs[0].reshape(n, GQ_W), outs[1].reshape(n, GQ_W), outs[2].reshape(n, GV_W),
            outs[3].reshape(bsz * nc_seq, 1, GQ_W))


def _gla_intra_bwd(projp, w2p, gb, do, dqg, dkd, dvi, dgl, *, nc_seq, name):
    n = projp.shape[0]
    nct = n // CHUNK
    scale = GLA_DK ** -0.5

    grp = _group(nc_seq)
    rows = grp * CHUNK

    def inner(gi, qk_ref, v_ref, sb_ref, w2_ref, gb_ref, do_ref, dqg_ref, dkd_ref, dvi_ref, dgl_ref,
              dqk_ref, dv_ref, dsb_ref, dw2_ref, dgb_ref):
        ci = (pl.program_id(0) * grp + gi) % nc_seq
        sb = sb_ref[...]
        w2 = w2_ref[...]
        graw, bc, valid = yield from _gla_gates(sb, w2, gb_ref[...], ci)
        bref = bc[MID:MID + 1, :]
        bl = bc[CHUNK - 1:CHUNK, :]
        q = qk_ref[:, 0:GQ_W] * scale
        k = qk_ref[:, GQ_W:2 * GQ_W]
        ex1 = jnp.exp(bc - bref)
        ex2 = jnp.exp(bref - bc)
        eb = jnp.exp(bc)
        ekd = jnp.exp(bl - bc)
        qi, ki = q * ex1, k * ex2
        r, c = _masks64()
        incl = r >= c
        upper = r <= c
        a_t, da, da_t = [], [], []
        for h in range(NH):
            sl = slice(h * GLA_DV, (h + 1) * GLA_DV)
            doh = do_ref[:, sl]
            vh = v_ref[:, sl]
            a_t.append(jnp.where(upper, _nt(jnp.where(_head_mask(h), ki, 0.0), qi), 0.0))
            da.append(jnp.where(incl, _nt(doh, vh), 0.0))
            da_t.append(jnp.where(upper, _nt(vh, doh), 0.0))
        yield
        dqi = jnp.zeros((CHUNK, GQ_W), F32)
        dki = jnp.zeros((CHUNK, GQ_W), F32)
        for h in range(NH):
            sl = slice(h * GLA_DV, (h + 1) * GLA_DV)
            m = _head_mask(h)
            dv_ref[:, sl] = (_nn(a_t[h], do_ref[:, sl]) + dvi_ref[:, sl]).astype(dv_ref.dtype)
            dqi = dqi + jnp.where(m, _nn(da[h], ki), 0.0)
            dki = dki + jnp.where(m, _nn(da_t[h], qi), 0.0)
        yield
        dqg = dqg_ref[...]
        dkd = dkd_ref[...]
        dqk_ref[:, 0:GQ_W] = ((dqi * ex1 + dqg * eb) * scale).astype(dqk_ref.dtype)
        dqk_ref[:, GQ_W:2 * GQ_W] = (dki * ex2 + dkd * ekd).astype(dqk_ref.dtype)
        t_qi, t_ki, t_kd = dqi * qi, dki * ki, dkd * (k * ekd)
        db = t_qi - t_ki + dqg * (q * eb) - t_kd
        dbref = jnp.sum(t_ki - t_qi, axis=0, keepdims=True)
        dbl = jnp.sum(t_kd, axis=0, keepdims=True) + dgl_ref[0] * jnp.exp(bl)
        rows = _iota2((CHUNK, GQ_W), 0)
        db = db + jnp.where(rows == MID, dbref, 0.0) + jnp.where(rows == CHUNK - 1, dbl, 0.0)
        dg = _nn_hi(upper.astype(F32), db)
        yield
        dgraw = jnp.where(valid, dg * (1.0 / GLA_NORM) * _sigmoid(-graw), 0.0)
        dsb_ref[...] = _nt_hi(dgraw, w2).astype(dsb_ref.dtype)
        dw2 = _tn_hi(sb, dgraw)
        dgb = jnp.sum(dgraw, axis=0, keepdims=True)
        _accumulate(dw2_ref, dw2, gi)
        _accumulate(dgb_ref, dgb, gi)

    rq = pl.BlockSpec((rows, GQ_W), lambda i: (i, 0))
    rv = pl.BlockSpec((rows, GV_W), lambda i: (i, 0))
    kinds = ["row"] * 3 + ["whole"] * 2 + ["row"] * 4 + ["lead"] + ["row"] * 3 + ["whole"] * 2
    return pl.pallas_call(
        _per_chunk(inner, kinds, grp), name=name, grid=(nct // grp,),
        in_specs=[pl.BlockSpec((rows, 2 * GQ_W), lambda i: (i, C_GQK // (2 * GQ_W))),
                  pl.BlockSpec((rows, GV_W), lambda i: (i, C_GV // GV_W)),
                  pl.BlockSpec((rows, LANE), lambda i: (i, C_SB // LANE)),
                  pl.BlockSpec((LANE, GQ_W), lambda i: (0, 0)), pl.BlockSpec((1, GQ_W), lambda i: (0, 0)),
                  rv, rq, rq, rv, pl.BlockSpec((grp, 1, GQ_W), lambda i: (i, 0, 0))],
        out_specs=[pl.BlockSpec((rows, 2 * GQ_W), lambda i: (i, 0)), rv, pl.BlockSpec((rows, LANE), lambda i: (i, 0)),
                   pl.BlockSpec((LANE, GQ_W), lambda i: (0, 0)), pl.BlockSpec((1, GQ_W), lambda i: (0, 0))],
        out_shape=[jax.ShapeDtypeStruct((n, 2 * GQ_W), BF16), jax.ShapeDtypeStruct((n, GV_W), BF16),
                   jax.ShapeDtypeStruct((n, LANE), BF16), jax.ShapeDtypeStruct((LANE, GQ_W), F32),
                   jax.ShapeDtypeStruct((1, GQ_W), F32)],
        compiler_params=_cp(VMEM_BIG, ("arbitrary",)),
    )(projp, projp, projp, w2p, gb, do, dqg, dkd, dvi, dgl)


SECTIONS = ((C_QKV, 1536), (C_DZ, 512), (C_GQK, 512), (C_GV, 512), (C_GR, 512), (C_SA, 128), (C_SB, 128))


def _inproj_bwd(secs, wp, h0, g1, dx1, *, tr, name):
    n, d = h0.shape

    def body(*refs):
        sec_refs = refs[:len(SECTIONS)]
        wp_ref, h0_ref, g_ref, dx1_ref, o_ref, dg_ref = refs[len(SECTIONS):]
        dh = None
        for s_ref, (off, wd) in zip(sec_refs, SECTIONS):
            part = _nt(s_ref[...], wp_ref[:, off:off + wd])
            dh = part if dh is None else dh + part
        dx, dg = _rms_bwd_math(h0_ref[...], g_ref[...], dh)
        o_ref[...] = dx1_ref[...] + dx

        @pl.when(pl.program_id(0) == 0)
        def _():
            dg_ref[...] = dg

        @pl.when(pl.program_id(0) > 0)
        def _():
            dg_ref[...] += dg

    row = pl.BlockSpec((tr, d), lambda i: (i, 0))
    vec = pl.BlockSpec((1, d), lambda i: (0, 0))
    return pl.pallas_call(
        body, name=name, grid=(n // tr,),
        in_specs=[pl.BlockSpec((tr, wd), lambda i: (i, 0)) for _, wd in SECTIONS]
        + [pl.BlockSpec((d, PW), lambda i: (0, 0)), row, vec, row],
        out_specs=[row, vec],
        out_shape=[jax.ShapeDtypeStruct((n, d), F32), jax.ShapeDtypeStruct((1, d), F32)],
        compiler_params=_cp(VMEM_BIG),
    )(*secs, wp, h0, g1, dx1)


def _adamw(w, g, m, v, *, name, emit_grad=False):
    lead = w.ndim - 2
    r, c = w.shape[-2:]
    tr = _tile(r, 256, 8) if r > 256 else r
    c1 = 1.0 - ADAM_B1 ** ADAM_STEP
    c2 = 1.0 - ADAM_B2 ** ADAM_STEP
    n_out = 4 if emit_grad else 3

    def body(w_ref, g_ref, m_ref, v_ref, *out_refs):
        rd = (lambda ref: ref[0]) if lead else (lambda ref: ref[...])
        gv = g_ref[:, 0:c]
        nm = ADAM_B1 * rd(m_ref) + (1.0 - ADAM_B1) * gv
        nv = ADAM_B2 * rd(v_ref) + (1.0 - ADAM_B2) * (gv * gv)
        res = [-ADAM_LR * ((nm / c1) / (jnp.sqrt(nv / c2) + ADAM_EPS) + ADAM_WD * rd(w_ref)), nm, nv, gv]
        for o_ref, val in zip(out_refs, res):
            if lead:
                o_ref[0] = val
            else:
                o_ref[...] = val

    blk = pl.BlockSpec((1,) * lead + (tr, c), lambda i: (0,) * lead + (i, 0))
    gblk = pl.BlockSpec((tr, g.shape[1]), lambda i: (i, 0))
    sds = jax.ShapeDtypeStruct(w.shape, F32)
    return pl.pallas_call(
        body, name=name, grid=(r // tr,), in_specs=[blk, gblk, blk, blk], out_specs=[blk] * n_out,
        out_shape=[sds] * n_out, compiler_params=_cp(VMEM_BIG),
    )(w, g, m, v)


def _pair_sum(where, g, theirs, *, name):
    lead, r, cols = g.shape
    half = r // 2
    tr = _tile(half, 256, 16)
    nh = half // tr

    def body(w_ref, a_ref, b_ref, o_ref):
        o_ref[...] = (a_ref[...] + b_ref[...]).astype(o_ref.dtype)

    blk = pl.BlockSpec((1, tr, cols), lambda s, i, w: (s, i, 0))
    return pl.pallas_call(
        body, name=name,
        grid_spec=pltpu.PrefetchScalarGridSpec(
            num_scalar_prefetch=1, grid=(lead, nh),
            in_specs=[pl.BlockSpec((1, tr, cols), lambda s, i, w: (s, w[0] * nh + i, 0)), blk], out_specs=blk),
        out_shape=jax.ShapeDtypeStruct((lead, half, cols), BF16), compiler_params=_cp(VMEM_BIG),
    )(where, g, theirs)


def _chip_sum(where, pair, q, *, name):
    _, half, cols = pair.shape
    tr = _tile(half, 256, 16)
    nh = half // tr

    def body(w_ref, own_ref, q1_ref, q2_ref, q3_ref, o_ref):
        f = lambda ref: ref[0].astype(F32)
        o_ref[...] = ((f(own_ref) + f(q1_ref)) + f(q2_ref)) + f(q3_ref)

    def peer(d):
        return pl.BlockSpec((1, tr, cols), lambda i, w: ((w[1] + d) % N_CHIPS, i, 0))

    return pl.pallas_call(
        body, name=name,
        grid_spec=pltpu.PrefetchScalarGridSpec(
            num_scalar_prefetch=1, grid=(nh,),
            in_specs=[peer(0), peer(1), peer(2), peer(3)],
            out_specs=pl.BlockSpec((tr, cols), lambda i, w: (w[0] * nh + i, 0))),
        out_shape=jax.ShapeDtypeStruct((2 * half, cols), F32), compiler_params=_cp(VMEM_BIG),
    )(where, pair, q, q, q)


ANY = pl.BlockSpec(memory_space=pl.ANY)
VM = pl.BlockSpec(memory_space=pltpu.VMEM)
CAST_ROWS = 64


def _place():
    return lax.axis_index("x"), lax.axis_index("y"), lax.axis_index("c")


def _other_chips(x, y):
    return [(1 - x, y, 2 * (1 - x) + y), (x, 1 - y, 2 * x + 1 - y), (1 - x, 1 - y, 2 * (1 - x) + 1 - y)]


def _gather_weights(w_in, w_out, w_up, w_down, meta, conv, gw2):
    big = (w_in, w_out, w_up, w_down)
    small = (meta, conv, gw2)
    n_arr = len(big) + len(small)

    def body(*refs):
        ins = refs[:n_arr]
        outs = refs[n_arr:2 * n_arr]
        stage = refs[2 * n_arr:2 * n_arr + len(big)]
        send_sems, recv_sems, fsend_sems, frecv_sems, local_sems = refs[2 * n_arr + len(big):]
        x, y, c = _place()
        me = 2 * x + y
        srcs = []
        for k in range(n_arr):
            if k < len(big):
                src, dst = ins[k], stage[k]

                def cast_rows(i, carry, src=src, dst=dst):
                    rows = pl.ds(pl.multiple_of(i * CAST_ROWS, CAST_ROWS), CAST_ROWS)
                    dst[rows, :] = src[rows, :].astype(BF16)
                    return carry

                lax.fori_loop(0, src.shape[0] // CAST_ROWS, cast_rows, 0)
                srcs.append(stage[k])
            else:
                srcs.append(ins[k])
        local = [pltpu.make_async_copy(srcs[k], outs[k].at[me], local_sems.at[k]) for k in range(n_arr)]
        for cp in local:
            cp.start()
        def my_rows(k):
            r = srcs[k].shape[0]
            return pl.ds(c * (r // 2), r // 2) if k < len(big) else pl.ds(0, r)

        def sibling_rows(k):
            r = srcs[k].shape[0]
            return pl.ds((1 - c) * (r // 2), r // 2)

        def ici(k, d, px, py, block):
            rows = my_rows(k)
            return pltpu.make_async_remote_copy(
                src_ref=srcs[k].at[rows, :], dst_ref=outs[k].at[block, rows, :], send_sem=send_sems.at[k, d],
                recv_sem=recv_sems.at[k, d], device_id=(px, py, c), device_id_type=MESH)

        def pass_on(k, d, block, rows):
            return pltpu.make_async_remote_copy(
                src_ref=outs[k].at[block, rows, :], dst_ref=outs[k].at[block, rows, :], send_sem=fsend_sems.at[k, d],
                recv_sem=frecv_sems.at[k, d], device_id=(x, y, 1 - c), device_id_type=MESH)

        sends = []
        for k in range(n_arr):
            for d, (px, py, _) in enumerate(_other_chips(x, y)):
                cp = ici(k, d, px, py, me)
                cp.start()
                sends.append(cp)
        for k in range(n_arr):
            for d, (px, py, pj) in enumerate(_other_chips(x, y)):
                ici(k, d, px, py, pj).wait_recv()
                if k < len(big):
                    cp = pass_on(k, d, pj, my_rows(k))
                    cp.start()
                    sends.append(cp)
        for k in range(len(big)):
            for d, (_, _, pj) in enumerate(_other_chips(x, y)):
                pass_on(k, d, pj, sibling_rows(k)).wait_recv()
        for cp in sends:
            cp.wait_send()
        for cp in local:
            cp.wait()

    out_shape = [jax.ShapeDtypeStruct((N_CHIPS,) + a.shape, BF16) for a in big]
    out_shape += [jax.ShapeDtypeStruct((N_CHIPS,) + a.shape, F32) for a in small]
    sem = pltpu.SemaphoreType.DMA((n_arr, 3))
    return pl.pallas_call(
        body, name="gather_weights", in_specs=[VM] * n_arr, out_specs=[ANY] * n_arr, out_shape=out_shape,
        scratch_shapes=[pltpu.VMEM(a.shape, BF16) for a in big] + [sem, sem, sem, sem, pltpu.SemaphoreType.DMA((n_arr,))],
        compiler_params=_cp(VMEM_BIG),
    )(*big, *small)


def _row_chunks(rows, n_split):
    size = rows // n_split
    assert size * n_split == rows and size % 16 == 0, (rows, n_split)
    return [(s, pl.ds(s * size, size)) for s in range(n_split)], size


D2D_SPLIT = 4
ICI_SPLIT = 2


def _sibling_halves(grads):
    n_arr = len(grads)

    def body(*refs):
        ins = refs[:n_arr]
        theirs = refs[n_arr:2 * n_arr]
        send_sems, recv_sems = refs[2 * n_arr:]
        x, y, c = _place()
        copies = []
        for k in range(n_arr):
            half = ins[k].shape[1] // 2
            chunks, size = _row_chunks(half, D2D_SPLIT)
            for s, dst_rows in chunks:
                give = pltpu.make_async_remote_copy(
                    src_ref=ins[k].at[:, pl.ds((1 - c) * half + s * size, size), :], dst_ref=theirs[k].at[:, dst_rows, :],
                    send_sem=send_sems.at[k, s], recv_sem=recv_sems.at[k, s], device_id=(x, y, 1 - c),
                    device_id_type=MESH)
                give.start()
                copies.append(give)
        for give in copies:
            give.wait()

    halves = [jax.ShapeDtypeStruct((g.shape[0], g.shape[1] // 2, g.shape[2]), F32) for g in grads]
    sem = pltpu.SemaphoreType.DMA((n_arr, D2D_SPLIT))
    return pl.pallas_call(
        body, name="sibling_halves", in_specs=[ANY] * n_arr, out_specs=[ANY] * n_arr, out_shape=halves,
        scratch_shapes=[sem, sem],
    )(*grads)


def _chip_exchange(parts):
    n_arr = len(parts)

    def body(*refs):
        ins = refs[:n_arr]
        outs = refs[n_arr:2 * n_arr]
        send_sems, recv_sems = refs[2 * n_arr:]
        x, y, c = _place()
        me = 2 * x + y
        sends = []
        for k in range(n_arr):
            chunks, _ = _row_chunks(ins[k].shape[1], ICI_SPLIT)
            for d, (px, py, pj) in enumerate(_other_chips(x, y)):
                for s, rows in chunks:
                    cp = pltpu.make_async_remote_copy(
                        src_ref=ins[k].at[pj, rows, :], dst_ref=outs[k].at[me, rows, :], send_sem=send_sems.at[k, d, s],
                        recv_sem=recv_sems.at[k, d, s], device_id=(px, py, c), device_id_type=MESH)
                    cp.start()
                    sends.append(cp)
        for k in range(n_arr):
            chunks, _ = _row_chunks(ins[k].shape[1], ICI_SPLIT)
            for d, (px, py, pj) in enumerate(_other_chips(x, y)):
                for s, rows in chunks:
                    pltpu.make_async_remote_copy(
                        src_ref=ins[k].at[pj, rows, :], dst_ref=outs[k].at[pj, rows, :], send_sem=send_sems.at[k, d, s],
                        recv_sem=recv_sems.at[k, d, s], device_id=(px, py, c), device_id_type=MESH).wait_recv()
        for cp in sends:
            cp.wait_send()

    sem = pltpu.SemaphoreType.DMA((n_arr, 3, ICI_SPLIT))
    return pl.pallas_call(
        body, name="chip_exchange", in_specs=[ANY] * n_arr, out_specs=[ANY] * n_arr,
        out_shape=[jax.ShapeDtypeStruct(p.shape, p.dtype) for p in parts],
        scratch_shapes=[sem, sem],
    )(*parts)


def _sibling_join(bufs):
    n_arr = len(bufs)

    def body(*refs):
        bufs_out = refs[n_arr:2 * n_arr]
        send_sems, recv_sems = refs[2 * n_arr:]
        x, y, c = _place()
        copies = []
        for k in range(n_arr):
            half = bufs_out[k].shape[0] // 2
            chunks, size = _row_chunks(half, D2D_SPLIT)
            for s, _ in chunks:
                rows = pl.ds(c * half + s * size, size)
                give = pltpu.make_async_remote_copy(
                    src_ref=bufs_out[k].at[rows, :], dst_ref=bufs_out[k].at[rows, :], send_sem=send_sems.at[k, s],
                    recv_sem=recv_sems.at[k, s], device_id=(x, y, 1 - c), device_id_type=MESH)
                give.start()
                copies.append((k, s, half, size, give))
        for k, s, half, size, give in copies:
            rows = pl.ds((1 - c) * half + s * size, size)
            pltpu.make_async_remote_copy(
                src_ref=bufs_out[k].at[rows, :], dst_ref=bufs_out[k].at[rows, :], send_sem=send_sems.at[k, s],
                recv_sem=recv_sems.at[k, s], device_id=(x, y, 1 - c), device_id_type=MESH).wait_recv()
            give.wait_send()

    sem = pltpu.SemaphoreType.DMA((n_arr, D2D_SPLIT))
    return pl.pallas_call(
        body, name="sibling_join", in_specs=[ANY] * n_arr, out_specs=[ANY] * n_arr,
        out_shape=[jax.ShapeDtypeStruct(b.shape, F32) for b in bufs],
        input_output_aliases={k: k for k in range(n_arr)},
        scratch_shapes=[sem, sem],
    )(*bufs)


PACK_ROWS = 48


def _small_allreduce(pack):
    masks = [(dx, dy, dc) for dx in (0, 1) for dy in (0, 1) for dc in (0, 1)][1:]

    def body(p_ref, o_ref, buf, send_sems, recv_sems):
        x, y, c = _place()
        me = 4 * x + 2 * y + c
        buf[me] = p_ref[...]
        sends = []
        for k, (dx, dy, dc) in enumerate(masks):
            peer = (1 - x if dx else x, 1 - y if dy else y, 1 - c if dc else c)
            cp = pltpu.make_async_remote_copy(
                src_ref=p_ref, dst_ref=buf.at[me], send_sem=send_sems.at[k], recv_sem=recv_sems.at[k],
                device_id=peer, device_id_type=MESH)
            cp.start()
            sends.append(cp)
        for k, (dx, dy, dc) in enumerate(masks):
            peer = (1 - x if dx else x, 1 - y if dy else y, 1 - c if dc else c)
            pj = 4 * peer[0] + 2 * peer[1] + peer[2]
            pltpu.make_async_remote_copy(
                src_ref=p_ref, dst_ref=buf.at[pj], send_sem=send_sems.at[k], recv_sem=recv_sems.at[k],
                device_id=peer, device_id_type=MESH).wait_recv()
        for cp in sends:
            cp.wait_send()
        tot = buf[0]
        for k in range(1, 8):
            tot = tot + buf[k]
        o_ref[...] = tot
        o_ref[0:N_META, :] = tot[0:N_META] + tot[N_META:2 * N_META]

    return pl.pallas_call(
        body, name="small_allreduce", in_specs=[VM], out_specs=VM,
        out_shape=jax.ShapeDtypeStruct((PACK_ROWS, D_MODEL), F32),
        scratch_shapes=[pltpu.VMEM((8, PACK_ROWS, D_MODEL), F32), pltpu.SemaphoreType.DMA((7,)),
                        pltpu.SemaphoreType.DMA((7,))],
    )(pack)


def _pad_lanes(vec, offset):
    k = vec.shape[1]
    return jnp.concatenate([jnp.zeros((1, offset), F32), vec, jnp.zeros((1, LANE - offset - k), F32)], axis=1)


def _local_step(x, tgt, meta, norm1_g, wp, conv_w, a_log, dt_bias, dn_norm_g, gla_w2, gla_b, gla_norm_g,
                w_out, norm2_g, w_up, w_down, final_norm_g):
    bsz, s_len, d = x.shape
    t_seq = s_len + CHUNK
    nc_seq = t_seq // CHUNK
    n = bsz * t_seq
    tr = _tile(t_seq, 832)
    tt = _tile(t_seq, 416)

    lead = jnp.concatenate([jnp.zeros((N_PAD, d), F32), meta], axis=0)
    h0 = jnp.concatenate([jnp.broadcast_to(lead[None], (bsz, CHUNK, d)), x], axis=1).reshape(n, d)
    tgt_p = jnp.concatenate([jnp.zeros((bsz, CHUNK, d), F32), tgt], axis=1).reshape(n, d)
    alog_row = _pad_lanes(a_log, 4)
    dtb_row = _pad_lanes(dt_bias, 4)
    w2p = jnp.concatenate([gla_w2, jnp.zeros((LANE - GLA_RANK, GQ_W), F32)], axis=0)

    h = _rms_fwd(h0, norm1_g, tr=tr, name="norm1")
    (projp,) = _mm(h, wp, "nn", tm=tt, tn=PW, tk=d, out_dtypes=(F32,), name="in_proj")
    qn, kn, v = _dnprep_fwd(projp, conv_w, bsz=bsz, t_seq=t_seq, tt=tt, name="dn_prep")
    u, w, qg, kd, pmat, tmat, gl = _dn_intra_fwd(qn, kn, v, projp, alog_row, dtb_row, nc_seq=nc_seq, name="dn_intra")
    o_dn, vn, hist = _dn_scan_fwd(u, w, qg, kd, pmat, gl, bsz=bsz, nc_seq=nc_seq, name="dn_scan")
    oi, gqg, gkd, ggl = _gla_intra_fwd(projp, w2p, gla_b, nc_seq=nc_seq, name="gla_intra")
    o_gla, ghist = _gla_scan_fwd(oi, gqg, gkd, ggl, projp, bsz=bsz, nc_seq=nc_seq, name="gla_scan")
    mix = _gnorm_fwd(o_dn, o_gla, projp, dn_norm_g, gla_norm_g, tr=tr, name="gated_norm")
    (x1,) = _mm(mix, w_out, "nn", tm=tr, tn=d, tk=d, out_dtypes=(F32,), extras=(h0,),
                epilogue=lambda acc, res: (res + acc,), name="out_proj")
    h2 = _rms_fwd(x1, norm2_g, tr=tr, name="norm2")

    (act,) = _mm(h2, w_up, "nn", tm=tt, tn=D_FF, tk=d, out_dtypes=(BF16,),
                 epilogue=lambda acc: (jnp.square(jnp.maximum(acc, 0.0)),), name="mlp_up")
    (x2,) = _mm(act, w_down, "nn", tm=tr, tn=d, tk=D_FF, out_dtypes=(F32,), extras=(x1,),
                epilogue=lambda acc, res: (res + acc,), name="mlp_down")
    dx2, dx2b, d_final_g, loss_tile = _final_loss(x2, final_norm_g, tgt_p, t_seq=t_seq, tr=tr, name="final_loss")

    (dup,) = _mm(dx2b, w_down, "nt", tm=tt, tn=D_FF, tk=d, out_dtypes=(BF16,), extras=(act,),
                 epilogue=lambda acc, a: (acc * (2.0 * jnp.sqrt(a.astype(F32))),), name="mlp_down_bwd")
    (d_w_down,) = _mm(act, dx2b, "tn", tm=D_FF // 2, tn=d, tk=tr, out_dtypes=(F32,), name="w_down_grad")
    (d_w_up,) = _mm(h2, dup, "tn", tm=d, tn=D_FF // 2, tk=tr, out_dtypes=(F32,), name="w_up_grad")
    (dh2,) = _mm(dup, w_up, "nt", tm=tr, tn=d, tk=D_FF, out_dtypes=(F32,), name="mlp_up_bwd")
    dx1, dx1b, d_norm2_g = _rms_bwd_add(x1, norm2_g, dh2, dx2, tr=tr, name="norm2_bwd")

    (dmix,) = _mm(dx1b, w_out, "nt", tm=tr, tn=d, tk=d, out_dtypes=(F32,), name="out_proj_bwd")
    (d_w_out,) = _mm(mix, dx1b, "tn", tm=d, tn=d, tk=tr, out_dtypes=(F32,), name="w_out_grad")
    do_dn, ddz, do_gla, dgr, d_dn_norm_g, d_gla_norm_g = _gnorm_bwd(
        dmix, o_dn, o_gla, projp, dn_norm_g, gla_norm_g, tr=tr, name="gated_norm_bwd")
    du, dw, dqg, dkd, dgl = _dn_scan_bwd(do_dn, w, qg, kd, vn, pmat, gl, hist, bsz=bsz, nc_seq=nc_seq,
                                          name="dn_scan_bwd")
    dqn, dkn, dv, dsa, d_alog, d_dtb = _dn_intra_bwd(qn, kn, v, projp, alog_row, dtb_row, u, w, tmat,
                                                     du, dw, dqg, dkd, do_dn, vn, dgl, nc_seq=nc_seq,
                                                     name="dn_intra_bwd")
    dz, d_conv_w = _dnprep_bwd_a(projp, conv_w, dqn, dkn, dv, bsz=bsz, t_seq=t_seq, tt=tt, name="dn_prep_bwd")
    dcin = _dnprep_bwd_b(dz, conv_w, bsz=bsz, t_seq=t_seq, tt=tt, name="conv_bwd")
    gdqg, gdkd, gdvi, gdgl = _gla_scan_bwd(do_gla, gqg, gkd, ggl, projp, ghist, bsz=bsz, nc_seq=nc_seq,
                                            name="gla_scan_bwd")
    dgqk, dgv, dsb, d_w2p, d_gla_b = _gla_intra_bwd(projp, w2p, gla_b, do_gla, gdqg, gdkd, gdvi, gdgl,
                                                    nc_seq=nc_seq, name="gla_intra_bwd")

    secs = (dcin, ddz, dgqk, dgv, dgr, dsa, dsb)
    g_lo = _grad_tn(h, secs[0:2], tk=tr, name="w_in_grad_lo")
    g_hi = _grad_tn(h, secs[2:7], tk=tr, name="w_in_grad_hi")
    dh0, d_norm1_g = _inproj_bwd(secs, wp, h0, norm1_g, dx1, tr=tt, name="in_proj_bwd")
    dh0 = dh0.reshape(bsz, t_seq, d)
    grad_x = dh0[:, CHUNK:]
    d_meta_rows = dh0[:, N_PAD:CHUNK].reshape(bsz * N_META, d)

    grads = dict(w_in_lo=g_lo, w_in_hi=g_hi, w_out=d_w_out, w_up=d_w_up, w_down=d_w_down, meta_rows=d_meta_rows,
                 norm1_g=d_norm1_g, conv_w=d_conv_w, a_log_tile=d_alog, dt_bias_tile=d_dtb, dn_norm_g=d_dn_norm_g,
                 gla_w2=d_w2p[0:GLA_RANK], gla_b=d_gla_b, gla_norm_g=d_gla_norm_g, norm2_g=d_norm2_g,
                 final_norm_g=d_final_g, loss_tile=loss_tile)
    return grad_x, grads


SHARD_W = IN_WIDTH // N_CHIPS
PADDED_ORDER = ((0, 2048), (2056, 3592), (2048, 2056), LANE - 8, (3592, 3608), LANE - GLA_RANK)


def _pad_layout(w_full):
    pieces = [jnp.zeros((w_full.shape[0], seg), w_full.dtype) if isinstance(seg, int) else w_full[:, seg[0]:seg[1]]
              for seg in PADDED_ORDER]
    return jnp.concatenate(pieces, axis=1)


def _padded_from_shards(stack):
    pieces = []
    for seg in PADDED_ORDER:
        if isinstance(seg, int):
            pieces.append(jnp.zeros((stack.shape[1], seg), stack.dtype))
            continue
        for j in range(N_CHIPS):
            lo, hi = max(seg[0], j * SHARD_W), min(seg[1], (j + 1) * SHARD_W)
            if lo < hi:
                pieces.append(stack[j, :, lo - j * SHARD_W:hi - j * SHARD_W])
    return jnp.concatenate(pieces, axis=1)


def _shards_from_padded(g_lo, g_hi):
    split = g_lo.shape[1]
    starts, pos = [], 0
    for seg in PADDED_ORDER:
        width = seg if isinstance(seg, int) else seg[1] - seg[0]
        if not isinstance(seg, int):
            starts.append((seg[0], seg[1], pos))
        pos += width
    shards = []
    for j in range(N_CHIPS):
        pieces = []
        for a, b, p0 in sorted(starts):
            lo, hi = max(a, j * SHARD_W), min(b, (j + 1) * SHARD_W)
            if lo < hi:
                src, off = (g_lo, 0) if p0 < split else (g_hi, split)
                pieces.append(src[:, p0 + lo - a - off:p0 + hi - a - off])
        pieces.append(jnp.zeros((g_lo.shape[0], D_MODEL - SHARD_W), g_lo.dtype))
        shards.append(jnp.concatenate(pieces, axis=1))
    return jnp.stack(shards)


def _pack_small(g, bsz):
    assert bsz * N_META == 32
    row = jnp.concatenate([g["a_log_tile"], g["dt_bias_tile"], g["dn_norm_g"], g["gla_norm_g"], g["gla_b"],
                           g["loss_tile"], jnp.zeros((1, LANE), F32)], axis=1)
    return jnp.concatenate([g["meta_rows"], g["norm1_g"], g["conv_w"].reshape(6, D_MODEL), row,
                            g["gla_w2"].reshape(4, D_MODEL), g["norm2_g"], g["final_norm_g"],
                            jnp.zeros((2, D_MODEL), F32)], axis=0)


def kernel(x, meta_tokens, norm1_g, w_in, conv_w, a_log, dt_bias, dn_norm_g, gla_w2, gla_b, gla_norm_g, w_out, norm2_g, w_up, w_down, final_norm_g, loss_target, m_meta_tokens, m_norm1_g, m_w_in, m_conv_w, m_a_log, m_dt_bias, m_dn_norm_g, m_gla_w2, m_gla_b, m_gla_norm_g, m_w_out, m_norm2_g, m_w_up, m_w_down, m_final_norm_g, v_meta_tokens, v_norm1_g, v_w_in, v_conv_w, v_a_log, v_dt_bias, v_dn_norm_g, v_gla_w2, v_gla_b, v_gla_norm_g, v_w_out, v_norm2_g, v_w_up, v_w_down, v_final_norm_g):
    bsz = x.shape[0]
    chip = 2 * lax.axis_index("x") + lax.axis_index("y")

    lane_pad = lambda a, wd: jnp.pad(a, ((0, 0), (0, wd - a.shape[1])))
    g_in, g_out, g_up, g_down, g_meta, g_conv, g_w2 = _gather_weights(
        lane_pad(w_in[0], D_MODEL), w_out[0], w_up[0], w_down[0], meta_tokens, conv_w[0], lane_pad(gla_w2[0], LANE))
    wp = _padded_from_shards(g_in)
    w_out_f = g_out.reshape(D_MODEL, D_MODEL)
    w_up_f = g_up.transpose(1, 0, 2).reshape(D_MODEL, D_FF)
    w_down_f = g_down.reshape(D_FF, D_MODEL)
    meta_f = g_meta.transpose(1, 0, 2).reshape(N_META, D_MODEL)
    conv_f = g_conv.transpose(1, 0, 2).reshape(4, QKV_W)
    w2_f = g_w2[:, :, 0:GQ_W // N_CHIPS].transpose(1, 0, 2).reshape(GLA_RANK, GQ_W)

    grad_x, g = _local_step(x, loss_target, meta_f, norm1_g, wp, conv_f, a_log, dt_bias, dn_norm_g, w2_f, gla_b,
                            gla_norm_g, w_out_f, norm2_g, w_up_f, w_down_f, final_norm_g.reshape(1, D_MODEL))

    shard_major = [
        _shards_from_padded(g["w_in_lo"], g["w_in_hi"]),
        g["w_out"].reshape(N_CHIPS, D_MODEL // N_CHIPS, D_MODEL),
        g["w_up"].reshape(D_MODEL, N_CHIPS, D_FF // N_CHIPS).transpose(1, 0, 2),
        g["w_down"].reshape(N_CHIPS, D_FF // N_CHIPS, D_MODEL),
    ]
    where = jnp.stack([lax.axis_index("c"), chip]).astype(jnp.int32)
    theirs = _sibling_halves(shard_major)
    pair = [_pair_sum(where, a, b, name=f"pair_sum_{k}") for k, (a, b) in enumerate(zip(shard_major, theirs))]
    parts = _chip_exchange(pair)
    halves = [_chip_sum(where, p, q, name=f"chip_sum_{k}") for k, (p, q) in enumerate(zip(pair, parts))]
    gw_in, gw_out, gw_up, gw_down = _sibling_join(halves)

    red = _small_allreduce(_pack_small(g, bsz))
    g_meta_full = red[0:N_META]
    g_norm1 = red[32:33]
    g_conv_full = red[33:39].reshape(4, QKV_W)
    srow = red[39:40]
    g_alog, g_dtb = srow[:, 4:8], srow[:, LANE + 4:LANE + 8]
    g_dn_norm, g_gla_norm = srow[:, 2 * LANE:3 * LANE], srow[:, 3 * LANE:4 * LANE]
    g_gla_b = srow[:, 4 * LANE:6 * LANE]
    loss = srow[0, 6 * LANE]
    g_w2_full = red[40:44].reshape(GLA_RANK, GQ_W)
    g_norm2 = red[44:45]
    g_final = red[45:46]
    g_meta_sh = lax.dynamic_slice_in_dim(g_meta_full, chip * (D_MODEL // N_CHIPS), D_MODEL // N_CHIPS, axis=1)
    g_conv_sh = lax.dynamic_slice_in_dim(g_conv_full, chip * (QKV_W // N_CHIPS), QKV_W // N_CHIPS, axis=1)
    g_w2_sh = lax.dynamic_slice_in_dim(g_w2_full, chip * (GQ_W // N_CHIPS), GQ_W // N_CHIPS, axis=1)

    names = ["meta_tokens", "norm1_g", "w_in", "conv_w", "a_log", "dt_bias", "dn_norm_g", "gla_w2", "gla_b",
             "gla_norm_g", "w_out", "norm2_g", "w_up", "w_down", "final_norm_g"]
    weights = dict(meta_tokens=meta_tokens, norm1_g=norm1_g, w_in=w_in, conv_w=conv_w, a_log=a_log, dt_bias=dt_bias,
                   dn_norm_g=dn_norm_g, gla_w2=gla_w2, gla_b=gla_b, gla_norm_g=gla_norm_g, w_out=w_out,
                   norm2_g=norm2_g, w_up=w_up, w_down=w_down, final_norm_g=final_norm_g)
    ms = dict(meta_tokens=m_meta_tokens, norm1_g=m_norm1_g, w_in=m_w_in, conv_w=m_conv_w, a_log=m_a_log,
              dt_bias=m_dt_bias, dn_norm_g=m_dn_norm_g, gla_w2=m_gla_w2, gla_b=m_gla_b, gla_norm_g=m_gla_norm_g,
              w_out=m_w_out, norm2_g=m_norm2_g, w_up=m_w_up, w_down=m_w_down, final_norm_g=m_final_norm_g)
    vs = dict(meta_tokens=v_meta_tokens, norm1_g=v_norm1_g, w_in=v_w_in, conv_w=v_conv_w, a_log=v_a_log,
              dt_bias=v_dt_bias, dn_norm_g=v_dn_norm_g, gla_w2=v_gla_w2, gla_b=v_gla_b, gla_norm_g=v_gla_norm_g,
              w_out=v_w_out, norm2_g=v_norm2_g, w_up=v_w_up, w_down=v_w_down, final_norm_g=v_final_norm_g)
    grads2d = dict(meta_tokens=g_meta_sh, norm1_g=g_norm1, w_in=gw_in, conv_w=g_conv_sh, a_log=g_alog, dt_bias=g_dtb,
                   dn_norm_g=g_dn_norm, gla_w2=g_w2_sh, gla_b=g_gla_b, gla_norm_g=g_gla_norm, w_out=gw_out,
                   norm2_g=g_norm2, w_up=gw_up, w_down=gw_down, final_norm_g=g_final)
    out_g, out_d, out_m, out_v = [], [], [], []
    for nm in names:
        shape = weights[nm].shape
        g2 = grads2d[nm]
        if len(shape) == 3:
            res = _adamw(weights[nm], g2, ms[nm], vs[nm], name=f"adamw_{nm}", emit_grad=nm == "w_in")
            gout = res[3] if nm == "w_in" else g2.reshape(shape)
        else:
            as2d = lambda a: a.reshape(g2.shape)
            res = _adamw(as2d(weights[nm]), g2, as2d(ms[nm]), as2d(vs[nm]), name=f"adamw_{nm}")
            gout = g2.reshape(shape)
        out_g.append(gout)
        out_d.append(res[0].reshape(shape))
        out_m.append(res[1].reshape(shape))
        out_v.append(res[2].reshape(shape))
    return (loss, grad_x, *out_g, *out_d, *out_m, *out_v)
```

```python
import functools

import jax
import jax.numpy as jnp
import numpy as np
from jax import lax
from jax.experimental import pallas as pl
from jax.experimental.pallas import tpu as pltpu

F32 = jnp.float32
BF16 = jnp.bfloat16
HI = lax.Precision.HIGHEST
MESH = pl.DeviceIdType.MESH

D_MODEL = 1024
N_META = 16
CHUNK = 64
N_PAD = CHUNK - N_META
NH = 4
DN_D = 128
GLA_DK = 64
GLA_DV = 128
GLA_RANK = 16
D_FF = 4 * D_MODEL
EPS = 1e-6
IN_WIDTH = 3608
C_QKV, C_DZ, C_GQK, C_GV, C_GR, C_SA, C_SB, PW = 0, 1536, 2048, 2560, 3072, 3584, 3712, 3840
LANE = 128
N_CHIPS = 4

ADAM_LR, ADAM_B1, ADAM_B2, ADAM_EPS, ADAM_WD, ADAM_STEP = 0.001, 0.9, 0.999, 1e-08, 0.01, 10

VMEM_BIG = 56 * 1024 * 1024


def _cp(vmem=None, sem=None):
    kw = {}
    if vmem is not None:
        kw["vmem_limit_bytes"] = vmem
    if sem is not None:
        kw["dimension_semantics"] = sem
    return pltpu.CompilerParams(**kw)


def _tile(n, target, mult=16):
    best = None
    for t in range(mult, min(n, target) + 1, mult):
        if n % t == 0:
            best = t
    assert best is not None, (n, target)
    return best


def _dot(a, b, dims, prec=None):
    return lax.dot_general(a, b, (dims, ((), ())), preferred_element_type=F32, precision=prec)


def _nn(a, b):
    return _dot(a.astype(BF16), b.astype(BF16), ((1,), (0,)))


def _nt(a, b):
    return _dot(a.astype(BF16), b.astype(BF16), ((1,), (1,)))


def _tn(a, b):
    return _dot(a.astype(BF16), b.astype(BF16), ((0,), (0,)))


def _nn_hi(a, b):
    return _dot(a, b, ((1,), (0,)), HI)


def _nt_hi(a, b):
    return _dot(a, b, ((1,), (1,)), HI)


def _tn_hi(a, b):
    return _dot(a, b, ((0,), (0,)), HI)


def _sigmoid(x):
    return 0.5 * jnp.tanh(0.5 * x) + 0.5


def _softplus(x):
    return jnp.maximum(x, 0.0) + jnp.log(1.0 + jnp.exp(-jnp.abs(x)))


def _logsigmoid(x):
    return -_softplus(-x)


def _iota2(shape, dim):
    return lax.broadcasted_iota(jnp.int32, shape, dim)


def _mm(a, b, mode, *, tm, tn, tk, out_dtypes, extras=(), epilogue=None, name, vmem=VMEM_BIG):
    if mode == "tn":
        K, M = a.shape
    else:
        M, K = a.shape
    N = b.shape[0] if mode == "nt" else b.shape[1]
    assert M % tm == 0 and N % tn == 0 and K % tk == 0, (name, M, N, K, tm, tn, tk)
    nk = K // tk
    n_ex, n_out = len(extras), len(out_dtypes)
    if mode == "tn":
        a_spec = pl.BlockSpec((tk, tm), lambda i, j, k: (k, i))
    else:
        a_spec = pl.BlockSpec((tm, tk), lambda i, j, k: (i, k))
    if mode == "nt":
        b_spec = pl.BlockSpec((tn, tk), lambda i, j, k: (j, k))
    else:
        b_spec = pl.BlockSpec((tk, tn), lambda i, j, k: (k, j))
    mn_spec = pl.BlockSpec((tm, tn), lambda i, j, k: (i, j))
    dims = {"nn": ((1,), (0,)), "nt": ((1,), (1,)), "tn": ((0,), (0,))}[mode]

    single = nk == 1
    direct = (not single) and epilogue is None and n_out == 1 and out_dtypes[0] == F32

    def body(*refs):
        a_ref, b_ref = refs[0], refs[1]
        ex_refs = refs[2:2 + n_ex]
        out_refs = refs[2 + n_ex:2 + n_ex + n_out]
        part = _dot(a_ref[...].astype(BF16), b_ref[...].astype(BF16), dims)

        def finish(acc):
            res = (acc,) if epilogue is None else epilogue(acc, *[e[...] for e in ex_refs])
            for o_ref, r in zip(out_refs, res):
                o_ref[...] = r.astype(o_ref.dtype)

        if single:
            finish(part)
            return
        acc_ref = out_refs[0] if direct else refs[2 + n_ex + n_out]
        k = pl.program_id(2)

        @pl.when(k == 0)
        def _():
            acc_ref[...] = part

        @pl.when(k > 0)
        def _():
            acc_ref[...] += part

        if not direct:
            @pl.when(k == nk - 1)
            def _():
                finish(acc_ref[...])

    outs = pl.pallas_call(
        body, name=name, grid=(M // tm, N // tn, nk),
        in_specs=[a_spec, b_spec] + [mn_spec] * n_ex,
        out_specs=[mn_spec] * n_out,
        out_shape=[jax.ShapeDtypeStruct((M, N), dt) for dt in out_dtypes],
        scratch_shapes=[] if (single or direct) else [pltpu.VMEM((tm, tn), F32)],
        compiler_params=_cp(vmem, ("parallel", "parallel", "arbitrary")),
    )(a, b, *extras)
    return tuple(outs)


def _grad_tn(a, secs, *, tk, name):
    kk, m = a.shape
    widths = [s.shape[1] for s in secs]
    total = sum(widths)
    nk = kk // tk

    def body(*refs):
        a_ref, sec_refs, o_ref = refs[0], refs[1:-1], refs[-1]
        cat = sec_refs[0][...] if len(sec_refs) == 1 else jnp.concatenate([s[...] for s in sec_refs], axis=1)
        part = _dot(a_ref[...].astype(BF16), cat.astype(BF16), ((0,), (0,)))
        k = pl.program_id(0)

        @pl.when(k == 0)
        def _():
            o_ref[...] = part

        @pl.when(k > 0)
        def _():
            o_ref[...] += part

    return pl.pallas_call(
        body, name=name, grid=(nk,),
        in_specs=[pl.BlockSpec((tk, m), lambda k: (k, 0))] + [pl.BlockSpec((tk, w), lambda k: (k, 0)) for w in widths],
        out_specs=pl.BlockSpec((m, total), lambda k: (0, 0)),
        out_shape=jax.ShapeDtypeStruct((m, total), F32),
        compiler_params=_cp(VMEM_BIG, ("arbitrary",)),
    )(a, *secs)


def _rms_fwd(x, g, *, tr, name):
    n, d = x.shape

    def body(x_ref, g_ref, o_ref):
        xv = x_ref[...]
        r = lax.rsqrt(jnp.mean(xv * xv, axis=-1, keepdims=True) + EPS)
        o_ref[...] = (xv * r * g_ref[...]).astype(o_ref.dtype)

    return pl.pallas_call(
        body, name=name, grid=(n // tr,),
        in_specs=[pl.BlockSpec((tr, d), lambda i: (i, 0)), pl.BlockSpec((1, d), lambda i: (0, 0))],
        out_specs=pl.BlockSpec((tr, d), lambda i: (i, 0)),
        out_shape=jax.ShapeDtypeStruct((n, d), BF16),
        compiler_params=_cp(VMEM_BIG),
    )(x, g)


def _rms_bwd_math(xv, g, dy):
    r = lax.rsqrt(jnp.mean(xv * xv, axis=-1, keepdims=True) + EPS)
    xh = xv * r
    gdy = dy * g
    dx = r * (gdy - xh * jnp.mean(xh * gdy, axis=-1, keepdims=True))
    return dx, jnp.sum(dy * xh, axis=0, keepdims=True)


def _rms_bwd_add(x, g, dy, res, *, tr, name):
    n, d = x.shape

    def body(x_ref, g_ref, dy_ref, res_ref, o_ref, ob_ref, dg_ref):
        dx, dg = _rms_bwd_math(x_ref[...], g_ref[...], dy_ref[...])
        tot = res_ref[...] + dx
        o_ref[...] = tot
        ob_ref[...] = tot.astype(BF16)

        @pl.when(pl.program_id(0) == 0)
        def _():
            dg_ref[...] = dg

        @pl.when(pl.program_id(0) > 0)
        def _():
            dg_ref[...] += dg

    row = pl.BlockSpec((tr, d), lambda i: (i, 0))
    vec = pl.BlockSpec((1, d), lambda i: (0, 0))
    return pl.pallas_call(
        body, name=name, grid=(n // tr,),
        in_specs=[row, vec, row, row], out_specs=[row, row, vec],
        out_shape=[jax.ShapeDtypeStruct((n, d), F32), jax.ShapeDtypeStruct((n, d), BF16),
                   jax.ShapeDtypeStruct((1, d), F32)],
        compiler_params=_cp(VMEM_BIG),
    )(x, g, dy, res)


def _final_loss(x2, gf, tgt, *, t_seq, tr, name):
    n, d = x2.shape
    per_seq = t_seq // tr

    def body(x_ref, g_ref, t_ref, dx_ref, dxb_ref, dg_ref, loss_ref):
        i = pl.program_id(0)
        xv = x_ref[...]
        g = g_ref[...]
        r = lax.rsqrt(jnp.mean(xv * xv, axis=-1, keepdims=True) + EPS)
        xh = xv * r
        pos = (i % per_seq) * tr + _iota2((tr, 1), 0)
        real = pos >= CHUNK
        err = jnp.where(real, xh * g - t_ref[...], 0.0)
        dy = err * (1.0 / d)
        gdy = dy * g
        dx = r * (gdy - xh * jnp.mean(xh * gdy, axis=-1, keepdims=True))
        dx_ref[...] = dx
        dxb_ref[...] = dx.astype(BF16)
        dg = jnp.sum(dy * xh, axis=0, keepdims=True)
        ls = 0.5 * jnp.sum(jnp.mean(err * err, axis=-1, keepdims=True), axis=0, keepdims=True)
        ls = jnp.where(_iota2((1, LANE), 1) == 0, ls, 0.0)

        @pl.when(i == 0)
        def _():
            dg_ref[...] = dg
            loss_ref[...] = ls

        @pl.when(i > 0)
        def _():
            dg_ref[...] += dg
            loss_ref[...] += ls

    row = pl.BlockSpec((tr, d), lambda i: (i, 0))
    vec = pl.BlockSpec((1, d), lambda i: (0, 0))
    one = pl.BlockSpec((1, LANE), lambda i: (0, 0))
    return pl.pallas_call(
        body, name=name, grid=(n // tr,),
        in_specs=[row, vec, row], out_specs=[row, row, vec, one],
        out_shape=[jax.ShapeDtypeStruct((n, d), F32), jax.ShapeDtypeStruct((n, d), BF16),
                   jax.ShapeDtypeStruct((1, d), F32), jax.ShapeDtypeStruct((1, LANE), F32)],
        compiler_params=_cp(VMEM_BIG),
    )(x2, gf, tgt)


def _gnorm_fwd(o_dn, o_gla, projp, g_dn, g_gla, *, tr, name):
    n = o_dn.shape[0]
    w = NH * DN_D

    def body(odn_ref, ogl_ref, z_ref, r_ref, gdn_ref, ggl_ref, mix_ref):
        for grp, (o_ref, gate_ref, gain_ref) in enumerate(((odn_ref, z_ref, gdn_ref), (ogl_ref, r_ref, ggl_ref))):
            gain = gain_ref[...]
            for h in range(NH):
                sl = slice(h * DN_D, (h + 1) * DN_D)
                o = o_ref[:, sl]
                z = gate_ref[:, sl]
                r = lax.rsqrt(jnp.mean(o * o, axis=-1, keepdims=True) + EPS)
                y = (o * r * gain) * (z * _sigmoid(z))
                mix_ref[:, grp * w + h * DN_D: grp * w + (h + 1) * DN_D] = y.astype(mix_ref.dtype)

    row = pl.BlockSpec((tr, w), lambda i: (i, 0))
    vec = pl.BlockSpec((1, DN_D), lambda i: (0, 0))
    return pl.pallas_call(
        body, name=name, grid=(n // tr,),
        in_specs=[row, row, pl.BlockSpec((tr, w), lambda i: (i, C_DZ // w)),
                  pl.BlockSpec((tr, w), lambda i: (i, C_GR // w)), vec, vec],
        out_specs=pl.BlockSpec((tr, 2 * w), lambda i: (i, 0)),
        out_shape=jax.ShapeDtypeStruct((n, 2 * w), BF16),
        compiler_params=_cp(VMEM_BIG),
    )(o_dn, o_gla, projp, projp, g_dn, g_gla)


def _gnorm_bwd(dmix, o_dn, o_gla, projp, g_dn, g_gla, *, tr, name):
    n = o_dn.shape[0]
    w = NH * DN_D

    def body(dm_ref, odn_ref, ogl_ref, z_ref, r_ref, gdn_ref, ggl_ref,
             dodn_ref, ddz_ref, dogl_ref, dgr_ref, dgdn_ref, dggl_ref):
        first = pl.program_id(0) == 0
        groups = ((odn_ref, z_ref, gdn_ref, dodn_ref, ddz_ref, dgdn_ref),
                  (ogl_ref, r_ref, ggl_ref, dogl_ref, dgr_ref, dggl_ref))
        for grp, (o_ref, gate_ref, gain_ref, do_ref, dgate_ref, dgain_ref) in enumerate(groups):
            gain = gain_ref[...]
            dgain = jnp.zeros((1, DN_D), F32)
            for h in range(NH):
                sl = slice(h * DN_D, (h + 1) * DN_D)
                o = o_ref[:, sl]
                z = gate_ref[:, sl]
                dm = dm_ref[:, grp * w + h * DN_D: grp * w + (h + 1) * DN_D]
                r = lax.rsqrt(jnp.mean(o * o, axis=-1, keepdims=True) + EPS)
                oh = o * r
                s = _sigmoid(z)
                dn = dm * (z * s)
                dgate_ref[:, sl] = (dm * (oh * gain) * (s * (1.0 + z * (1.0 - s)))).astype(dgate_ref.dtype)
                gdn = dn * gain
                do_ref[:, sl] = r * (gdn - oh * jnp.mean(oh * gdn, axis=-1, keepdims=True))
                dgain = dgain + jnp.sum(dn * oh, axis=0, keepdims=True)

            @pl.when(first)
            def _():
                dgain_ref[...] = dgain

            @pl.when(jnp.logical_not(first))
            def _():
                dgain_ref[...] += dgain

    row = pl.BlockSpec((tr, w), lambda i: (i, 0))
    vec = pl.BlockSpec((1, DN_D), lambda i: (0, 0))
    big = jax.ShapeDtypeStruct((n, w), F32)
    gate = jax.ShapeDtypeStruct((n, w), BF16)
    small = jax.ShapeDtypeStruct((1, DN_D), F32)
    return pl.pallas_call(
        body, name=name, grid=(n // tr,),
        in_specs=[pl.BlockSpec((tr, 2 * w), lambda i: (i, 0)), row, row,
                  pl.BlockSpec((tr, w), lambda i: (i, C_DZ // w)), pl.BlockSpec((tr, w), lambda i: (i, C_GR // w)), vec, vec],
        out_specs=[row, row, row, row, vec, vec],
        out_shape=[big, gate, big, gate, small, small],
        compiler_params=_cp(VMEM_BIG),
    )(dmix, o_dn, o_gla, projp, projp, g_dn, g_gla)


QKV_W = 3 * NH * DN_D
HALO = 8


def _conv_z(xs_ref, cw_ref, tt):
    z = cw_ref[0:1, :] * xs_ref[pl.ds(HALO - 3, tt), :]
    for j in range(1, 4):
        z = z + cw_ref[j:j + 1, :] * xs_ref[pl.ds(HALO - 3 + j, tt), :]
    return z


def _dnprep_fwd(projp, conv_w, *, bsz, t_seq, tt, name):
    n = bsz * t_seq
    per_seq = t_seq // tt
    hw = NH * DN_D

    def body(x_ref, halo_ref, cw_ref, q_ref, k_ref, v_ref, xs_ref):
        i = pl.program_id(1)
        xs_ref[0:HALO, :] = jnp.where(i == 0, 0.0, halo_ref[...])
        xs_ref[HALO:HALO + tt, :] = x_ref[...]
        z = _conv_z(xs_ref, cw_ref, tt)
        a = z * _sigmoid(z)
        for grp, o_ref in enumerate((q_ref, k_ref)):
            for h in range(NH):
                ah = a[:, grp * hw + h * DN_D: grp * hw + (h + 1) * DN_D]
                rs = lax.rsqrt(jnp.sum(ah * ah, axis=-1, keepdims=True) + EPS)
                o_ref[:, h * DN_D:(h + 1) * DN_D] = ah * rs
        v_ref[...] = a[:, 2 * hw:3 * hw]

    def halo_map(b, i):
        return (jnp.maximum((b * t_seq + i * tt) // HALO - 1, 0), 0)

    out = pl.BlockSpec((tt, hw), lambda b, i: (b * per_seq + i, 0))
    sds = jax.ShapeDtypeStruct((n, hw), F32)
    return pl.pallas_call(
        body, name=name, grid=(bsz, per_seq),
        in_specs=[pl.BlockSpec((tt, QKV_W), lambda b, i: (b * per_seq + i, 0)),
                  pl.BlockSpec((HALO, QKV_W), halo_map),
                  pl.BlockSpec((4, QKV_W), lambda b, i: (0, 0))],
        out_specs=[out, out, out], out_shape=[sds, sds, sds],
        scratch_shapes=[pltpu.VMEM((tt + HALO, QKV_W), F32)],
        compiler_params=_cp(VMEM_BIG),
    )(projp, projp, conv_w)


def _dnprep_bwd_a(projp, conv_w, dq, dk, dv, *, bsz, t_seq, tt, name):
    n = bsz * t_seq
    per_seq = t_seq // tt
    hw = NH * DN_D

    def body(x_ref, halo_ref, cw_ref, dq_ref, dk_ref, dv_ref, dz_ref, dcw_ref, xs_ref):
        b, i = pl.program_id(0), pl.program_id(1)
        xs_ref[0:HALO, :] = jnp.where(i == 0, 0.0, halo_ref[...])
        xs_ref[HALO:HALO + tt, :] = x_ref[...]
        z = _conv_z(xs_ref, cw_ref, tt)
        s = _sigmoid(z)
        a = z * s
        dsilu = s * (1.0 + z * (1.0 - s))
        for grp, d_ref in enumerate((dq_ref, dk_ref)):
            for h in range(NH):
                sl = slice(grp * hw + h * DN_D, grp * hw + (h + 1) * DN_D)
                ah = a[:, sl]
                rs = lax.rsqrt(jnp.sum(ah * ah, axis=-1, keepdims=True) + EPS)
                y = ah * rs
                dy = d_ref[:, h * DN_D:(h + 1) * DN_D]
                da = rs * (dy - y * jnp.sum(dy * y, axis=-1, keepdims=True))
                dz_ref[:, sl] = da * dsilu[:, sl]
        dz_ref[:, 2 * hw:3 * hw] = dv_ref[...] * dsilu[:, 2 * hw:3 * hw]
        dz = dz_ref[...]
        first = jnp.logical_and(b == 0, i == 0)
        for j in range(4):
            part = jnp.sum(dz * xs_ref[pl.ds(HALO - 3 + j, tt), :], axis=0, keepdims=True)

            @pl.when(first)
            def _():
                dcw_ref[j:j + 1, :] = part

            @pl.when(jnp.logical_not(first))
            def _():
                dcw_ref[j:j + 1, :] += part

    def halo_map(b, i):
        return (jnp.maximum((b * t_seq + i * tt) // HALO - 1, 0), 0)

    hrow = pl.BlockSpec((tt, hw), lambda b, i: (b * per_seq + i, 0))
    return pl.pallas_call(
        body, name=name, grid=(bsz, per_seq),
        in_specs=[pl.BlockSpec((tt, QKV_W), lambda b, i: (b * per_seq + i, 0)),
                  pl.BlockSpec((HALO, QKV_W), halo_map),
                  pl.BlockSpec((4, QKV_W), lambda b, i: (0, 0)), hrow, hrow, hrow],
        out_specs=[pl.BlockSpec((tt, QKV_W), lambda b, i: (b * per_seq + i, 0)),
                   pl.BlockSpec((4, QKV_W), lambda b, i: (0, 0))],
        out_shape=[jax.ShapeDtypeStruct((n, QKV_W), F32), jax.ShapeDtypeStruct((4, QKV_W), F32)],
        scratch_shapes=[pltpu.VMEM((tt + HALO, QKV_W), F32)],
        compiler_params=_cp(VMEM_BIG),
    )(projp, projp, conv_w, dq, dk, dv)


def _dnprep_bwd_b(dz, conv_w, *, bsz, t_seq, tt, name):
    n = bsz * t_seq
    per_seq = t_seq // tt
    last_blk = n // HALO - 1

    def body(dz_ref, halo_ref, cw_ref, dx_ref, ds_ref):
        i = pl.program_id(1)
        ds_ref[0:tt, :] = dz_ref[...]
        ds_ref[tt:tt + HALO, :] = jnp.where(i == per_seq - 1, 0.0, halo_ref[...])
        dx = cw_ref[0:1, :] * ds_ref[pl.ds(3, tt), :]
        for j in range(1, 4):
            dx = dx + cw_ref[j:j + 1, :] * ds_ref[pl.ds(3 - j, tt), :]
        dx_ref[...] = dx.astype(dx_ref.dtype)

    def halo_map(b, i):
        return (jnp.minimum((b * t_seq + (i + 1) * tt) // HALO, last_blk), 0)

    row = pl.BlockSpec((tt, QKV_W), lambda b, i: (b * per_seq + i, 0))
    return pl.pallas_call(
        body, name=name, grid=(bsz, per_seq),
        in_specs=[row, pl.BlockSpec((HALO, QKV_W), halo_map), pl.BlockSpec((4, QKV_W), lambda b, i: (0, 0))],
        out_specs=row, out_shape=jax.ShapeDtypeStruct((n, QKV_W), BF16),
        scratch_shapes=[pltpu.VMEM((tt + HALO, QKV_W), F32)],
        compiler_params=_cp(VMEM_BIG),
    )(dz, dz, conv_w)


def _masks64():
    r = _iota2((CHUNK, CHUNK), 0)
    c = _iota2((CHUNK, CHUNK), 1)
    return r, c


def _group(nc_seq, target=5):
    return max(g for g in range(1, target + 1) if nc_seq % g == 0)


def _round_robin(chains):
    live = list(chains)
    while live:
        nxt = []
        for ch in live:
            try:
                next(ch)
                nxt.append(ch)
            except StopIteration:
                pass
        live = nxt
        yield


def _run(chains):
    for _ in _round_robin(chains):
        pass


def _per_chunk(inner, kinds, grp):
    def body(*refs):
        chains = []
        for gi in range(grp):
            views = []
            for r, kind in zip(refs, kinds):
                if kind == "row":
                    views.append(r.at[pl.ds(gi * CHUNK, CHUNK)])
                elif kind == "lead":
                    views.append(r.at[pl.ds(gi, 1)])
                else:
                    views.append(r)
            chains.append(inner(gi, *views))
        _run(chains)
    return body


def _accumulate(ref, val, gi):
    if gi > 0:
        ref[...] += val
        return
    first = pl.program_id(0) == 0

    @pl.when(first)
    def _():
        ref[...] = val

    @pl.when(jnp.logical_not(first))
    def _():
        ref[...] += val


ANY = pl.BlockSpec(memory_space=pl.ANY)


def _place():
    return lax.axis_index("x"), lax.axis_index("y"), lax.axis_index("c")


def _other_chips(x, y):
    return [(1 - x, y, 2 * (1 - x) + y), (x, 1 - y, 2 * x + 1 - y), (1 - x, 1 - y, 2 * (1 - x) + 1 - y)]


class _GatherRider:
    def __init__(self, bufs, split):
        self.inputs = list(bufs)
        self.split = list(split)
        self.out_shapes = [jax.ShapeDtypeStruct(b.shape, b.dtype) for b in bufs]
        self.aliases = {i: i for i in range(len(bufs))}
        self.sems = [pltpu.SemaphoreType.DMA((len(bufs), 3))] * 4

    def _rows(self, k, buf, c, mine=True):
        r = buf.shape[1]
        if not self.split[k]:
            return pl.ds(0, r)
        return pl.ds((c if mine else 1 - c) * (r // 2), r // 2)

    def _ici(self, k, d, bufs, sems, c, px, py, block):
        rows = self._rows(k, bufs[k], c)
        return pltpu.make_async_remote_copy(
            src_ref=bufs[k].at[block, rows, :], dst_ref=bufs[k].at[block, rows, :], send_sem=sems[0].at[k, d],
            recv_sem=sems[1].at[k, d], device_id=(px, py, c), device_id_type=MESH)

    def _pass(self, k, d, bufs, sems, x, y, c, block, mine):
        rows = self._rows(k, bufs[k], c, mine)
        return pltpu.make_async_remote_copy(
            src_ref=bufs[k].at[block, rows, :], dst_ref=bufs[k].at[block, rows, :], send_sem=sems[2].at[k, d],
            recv_sem=sems[3].at[k, d], device_id=(x, y, 1 - c), device_id_type=MESH)

    def first(self, in_refs, bufs, sems):
        x, y, c = _place()
        for k in range(len(bufs)):
            for d, (px, py, _) in enumerate(_other_chips(x, y)):
                self._ici(k, d, bufs, sems, c, px, py, 2 * x + y).start()

    def last(self, in_refs, bufs, sems):
        x, y, c = _place()
        chips = _other_chips(x, y)
        for k in range(len(bufs)):
            for d, (px, py, pj) in enumerate(chips):
                self._ici(k, d, bufs, sems, c, px, py, pj).wait_recv()
                if self.split[k]:
                    self._pass(k, d, bufs, sems, x, y, c, pj, True).start()
        for k in range(len(bufs)):
            for d, (px, py, pj) in enumerate(chips):
                if self.split[k]:
                    self._pass(k, d, bufs, sems, x, y, c, pj, False).wait_recv()
                    self._pass(k, d, bufs, sems, x, y, c, pj, True).wait_send()
                self._ici(k, d, bufs, sems, c, px, py, 2 * x + y).wait_send()


def _hosted_call(body, rider, *, name, grid, in_specs, out_specs, out_shape, scratch_shapes, compiler_params, args):
    if rider is None:
        outs = pl.pallas_call(body, name=name, grid=grid, in_specs=in_specs, out_specs=out_specs, out_shape=out_shape,
                              scratch_shapes=scratch_shapes, compiler_params=compiler_params)(*args)
        return list(outs), []
    n_in, n_out, n_scr = len(in_specs), len(out_specs), len(scratch_shapes)
    r_in, r_out = len(rider.inputs), len(rider.out_shapes)

    def full_body(*refs):
        ins = refs[:n_in]
        rins = refs[n_in:n_in + r_in]
        outs = refs[n_in + r_in:n_in + r_in + n_out]
        routs = refs[n_in + r_in + n_out:n_in + r_in + n_out + r_out]
        rest = refs[n_in + r_in + n_out + r_out:]
        scr, sems = rest[:n_scr], rest[n_scr:]
        ids = [pl.program_id(a) for a in range(len(grid))]
        is_first = functools.reduce(jnp.logical_and, [i == 0 for i in ids])
        is_last = functools.reduce(jnp.logical_and, [i == g - 1 for i, g in zip(ids, grid)])

        @pl.when(is_first)
        def _():
            rider.first(rins, routs, sems)

        body(*ins, *outs, *scr)

        @pl.when(is_last)
        def _():
            rider.last(rins, routs, sems)

    res = pl.pallas_call(
        full_body, name=name, grid=grid, in_specs=list(in_specs) + [ANY] * r_in,
        out_specs=list(out_specs) + [ANY] * r_out, out_shape=list(out_shape) + list(rider.out_shapes),
        input_output_aliases={n_in + i: n_out + o for i, o in rider.aliases.items()},
        scratch_shapes=list(scratch_shapes) + list(rider.sems), compiler_params=compiler_params,
    )(*args, *rider.inputs)
    return list(res[:n_out]), list(res[n_out:])


def _exchange_now(rider, *, name):
    r_in = len(rider.inputs)

    def body(*refs):
        rins = refs[:r_in]
        routs = refs[r_in:r_in + len(rider.out_shapes)]
        sems = refs[r_in + len(rider.out_shapes):]
        rider.first(rins, routs, sems)
        rider.last(rins, routs, sems)

    return pl.pallas_call(
        body, name=name, in_specs=[ANY] * r_in, out_specs=[ANY] * len(rider.out_shapes), out_shape=list(rider.out_shapes),
        input_output_aliases=dict(rider.aliases), scratch_shapes=list(rider.sems),
    )(*rider.inputs)


def _to_slot(where, a, dtype, *, name):
    r, cols = a.shape
    tr = _tile(r, 256, 16) if r > 256 else r

    def body(w_ref, a_ref, o_ref):
        o_ref[0] = a_ref[...].astype(o_ref.dtype)

    return pl.pallas_call(
        body, name=name,
        grid_spec=pltpu.PrefetchScalarGridSpec(
            num_scalar_prefetch=1, grid=(r // tr,),
            in_specs=[pl.BlockSpec((tr, cols), lambda i, w: (i, 0))],
            out_specs=pl.BlockSpec((1, tr, cols), lambda i, w: (w[1], i, 0))),
        out_shape=jax.ShapeDtypeStruct((N_CHIPS, r, cols), dtype), compiler_params=_cp(VMEM_BIG),
    )(where, a)


def _tri_inv(a_strict):
    r, c = _masks64()
    eye = (r == c).astype(F32)
    blk16 = (r // 16) == (c // 16)
    blk32 = (r // 32) == (c // 32)
    ld = jnp.where(blk16, a_strict, 0.0)
    x = eye - ld
    p = _nn(ld, ld)
    yield
    for step in range(3):
        xp = _nn(x, p)
        if step < 2:
            p = _nn(p, p)
        x = x + xp
        yield
    for lk in (jnp.where(jnp.logical_and(blk32, jnp.logical_not(blk16)), a_strict, 0.0),
               jnp.where(blk32, 0.0, a_strict)):
        y = x - eye
        s = lk + _nn(y, lk)
        yield
        x = x - s - _nn(s, y)
        yield
    return x


def _dn_gates(sa, alog, dtb, chunk_in_seq):
    rows = _iota2((CHUNK, LANE), 0)
    valid = jnp.logical_or(rows >= N_PAD, chunk_in_seq > 0)
    beta_t = _sigmoid(sa)
    ea = jnp.exp(alog)
    g_t = jnp.where(valid, -ea * _softplus(sa + dtb), 0.0)
    r, c = _masks64()
    ltri = (r >= c).astype(F32)
    gam_t = _nn_hi(ltri, g_t)
    return beta_t, g_t, gam_t, valid, ea


def _dn_intra_fwd(qn, kn, v, projp, alog_row, dtb_row, *, nc_seq, name):
    n = qn.shape[0]
    nct = n // CHUNK
    hw = NH * DN_D
    scale = DN_D ** -0.5

    grp = _group(nc_seq)

    def inner(gi, q_ref, k_ref, v_ref, sa_ref, al_ref, dt_ref, u_ref, w_ref, qg_ref, kd_ref, p_ref, t_ref, gl_ref):
        ci = (pl.program_id(0) * grp + gi) % nc_seq
        beta_t, _, gam_t, _, _ = _dn_gates(sa_ref[...], al_ref[...], dt_ref[...], ci)
        yield
        gam_tt = gam_t.T
        r, c = _masks64()
        incl = r >= c
        strict = r > c

        def head(h):
            sl = slice(h * DN_D, (h + 1) * DN_D)
            beta = beta_t[:, h:h + 1]
            gam = gam_t[:, 4 + h:5 + h]
            gam_row = gam_tt[4 + h:5 + h, :]
            gl = gam_t[CHUNK - 1:CHUNK, 4 + h:5 + h]
            dec = jnp.exp(jnp.where(incl, gam - gam_row, -jnp.inf))
            kh = k_ref[:, sl]
            qh = q_ref[:, sl] * scale
            vh = v_ref[:, sl]
            kk = _nt(kh, kh)
            qk = _nt(qh, kh)
            yield
            a = jnp.where(strict, beta * kk * dec, 0.0)
            tm = yield from _tri_inv(a)
            egam = jnp.exp(gam)
            u_ref[:, sl] = _nn(tm, beta * vh)
            w_ref[:, sl] = _nn(tm, (beta * egam) * kh)
            qg_ref[:, sl] = egam * qh
            kd_ref[:, sl] = jnp.exp(gl - gam) * kh
            p_ref[0, h] = qk * dec
            t_ref[0, h] = tm
            gl_ref[0, h:h + 1, :] = jnp.broadcast_to(jnp.exp(gl), (1, LANE))

        yield from _round_robin([head(h) for h in range(NH)])

    rows = grp * CHUNK
    row = pl.BlockSpec((rows, hw), lambda i: (i, 0))
    vec = pl.BlockSpec((1, LANE), lambda i: (0, 0))
    mat = pl.BlockSpec((grp, NH, CHUNK, CHUNK), lambda i: (i, 0, 0, 0))
    big = jax.ShapeDtypeStruct((n, hw), F32)
    msd = jax.ShapeDtypeStruct((nct, NH, CHUNK, CHUNK), F32)
    kinds = ["row"] * 4 + ["whole"] * 2 + ["row"] * 4 + ["lead"] * 3
    return pl.pallas_call(
        _per_chunk(inner, kinds, grp), name=name, grid=(nct // grp,),
        in_specs=[row, row, row, pl.BlockSpec((rows, LANE), lambda i: (i, C_SA // LANE)), vec, vec],
        out_specs=[row, row, row, row, mat, mat, pl.BlockSpec((grp, NH, LANE), lambda i: (i, 0, 0))],
        out_shape=[big, big, big, big, msd, msd, jax.ShapeDtypeStruct((nct, NH, LANE), F32)],
        compiler_params=_cp(VMEM_BIG),
    )(qn, kn, v, projp, alog_row, dtb_row)


def _dn_scan_fwd(u, w, qg, kd, p, gl, *, bsz, nc_seq, name, rider=None):
    hw = NH * DN_D
    t_seq = nc_seq * CHUNK
    u, w, qg, kd = (z.reshape(bsz, t_seq, hw) for z in (u, w, qg, kd))
    p = p.reshape(bsz, nc_seq, NH, CHUNK, CHUNK)
    gl = gl.reshape(bsz, nc_seq, NH, LANE)

    def body(u_ref, w_ref, qg_ref, kd_ref, p_ref, gl_ref, o_ref, vn_ref, hist_ref, s_ref):
        @pl.when(pl.program_id(0) == 0)
        def _():
            s_ref[...] = jnp.zeros_like(s_ref)

        def chain(b, h):
            sl = slice(h * DN_D, (h + 1) * DN_D)
            s = s_ref[b, h]
            hist_ref[b, 0, h] = s
            ws = _nn(w_ref[b, :, sl], s)
            qs = _nn(qg_ref[b, :, sl], s)
            yield
            vn = u_ref[b, :, sl] - ws
            vn_ref[b, :, sl] = vn
            o_ref[b, :, sl] = qs + _nn(p_ref[b, 0, h], vn)
            s_ref[b, h] = gl_ref[b, 0, h:h + 1, :] * s + _tn(kd_ref[b, :, sl], vn)

        _run([chain(b, h) for b in range(bsz) for h in range(NH)])

    row = pl.BlockSpec((bsz, CHUNK, hw), lambda i: (0, i, 0))
    outs, ridden = _hosted_call(
        body, rider, name=name, grid=(nc_seq,),
        in_specs=[row, row, row, row, pl.BlockSpec((bsz, 1, NH, CHUNK, CHUNK), lambda i: (0, i, 0, 0, 0)),
                  pl.BlockSpec((bsz, 1, NH, LANE), lambda i: (0, i, 0, 0))],
        out_specs=[row, row, pl.BlockSpec((bsz, 1, NH, DN_D, DN_D), lambda i: (0, i, 0, 0, 0))],
        out_shape=[jax.ShapeDtypeStruct((bsz, t_seq, hw), F32), jax.ShapeDtypeStruct((bsz, t_seq, hw), F32),
                   jax.ShapeDtypeStruct((bsz, nc_seq, NH, DN_D, DN_D), F32)],
        scratch_shapes=[pltpu.VMEM((bsz, NH, DN_D, DN_D), F32)],
        compiler_params=_cp(VMEM_BIG, ("arbitrary",)), args=(u, w, qg, kd, p, gl))
    o, vn, hist = outs
    return o.reshape(bsz * t_seq, hw), vn.reshape(bsz * t_seq, hw), hist, ridden


def _dn_scan_bwd(do, w, qg, kd, vn, p, gl, hist, *, bsz, nc_seq, name):
    hw = NH * DN_D
    t_seq = nc_seq * CHUNK
    do, w, qg, kd, vn = (z.reshape(bsz, t_seq, hw) for z in (do, w, qg, kd, vn))
    p = p.reshape(bsz, nc_seq, NH, CHUNK, CHUNK)
    gl = gl.reshape(bsz, nc_seq, NH, LANE)

    def body(do_ref, w_ref, qg_ref, kd_ref, vn_ref, p_ref, gl_ref, hist_ref,
             du_ref, dw_ref, dqg_ref, dkd_ref, dgl_ref, ds_ref):
        @pl.when(pl.program_id(0) == 0)
        def _():
            ds_ref[...] = jnp.zeros_like(ds_ref)

        def chain(b, h):
            sl = slice(h * DN_D, (h + 1) * DN_D)
            s = hist_ref[b, 0, h]
            dsn = ds_ref[b, h]
            doh = do_ref[b, :, sl]
            vnh = vn_ref[b, :, sl]
            kdh = kd_ref[b, :, sl]
            dvn = _tn(p_ref[b, 0, h], doh) + _nn(kdh, dsn)
            du_ref[b, :, sl] = dvn
            dqg_ref[b, :, sl] = _nt(doh, s)
            dkd_ref[b, :, sl] = _nt(vnh, dsn)
            ds_part = _tn(qg_ref[b, :, sl], doh) + gl_ref[b, 0, h:h + 1, :] * dsn
            dgl = jnp.sum(jnp.sum(dsn * s, axis=0, keepdims=True), axis=1, keepdims=True)
            dgl_ref[b, 0, h:h + 1, :] = jnp.broadcast_to(dgl, (1, LANE))
            yield
            dw_ref[b, :, sl] = -_nt(dvn, s)
            ds_ref[b, h] = ds_part - _tn(w_ref[b, :, sl], dvn)

        _run([chain(b, h) for b in range(bsz) for h in range(NH)])

    rev = lambda i: nc_seq - 1 - i
    row = pl.BlockSpec((bsz, CHUNK, hw), lambda i: (0, rev(i), 0))
    mat = pl.BlockSpec((bsz, 1, NH, CHUNK, CHUNK), lambda i: (0, rev(i), 0, 0, 0))
    glb = pl.BlockSpec((bsz, 1, NH, LANE), lambda i: (0, rev(i), 0, 0))
    big = jax.ShapeDtypeStruct((bsz, t_seq, hw), F32)
    outs = pl.pallas_call(
        body, name=name, grid=(nc_seq,),
        in_specs=[row, row, row, row, row, mat, glb,
                  pl.BlockSpec((bsz, 1, NH, DN_D, DN_D), lambda i: (0, rev(i), 0, 0, 0))],
        out_specs=[row, row, row, row, glb],
        out_shape=[big, big, big, big, jax.ShapeDtypeStruct((bsz, nc_seq, NH, LANE), F32)],
        scratch_shapes=[pltpu.VMEM((bsz, NH, DN_D, DN_D), F32)],
        compiler_params=_cp(VMEM_BIG, ("arbitrary",)),
    )(do, w, qg, kd, vn, p, gl, hist)
    du, dw, dqg, dkd, dgl = outs
    n = bsz * t_seq
    return (du.reshape(n, hw), dw.reshape(n, hw), dqg.reshape(n, hw), dkd.reshape(n, hw),
            dgl.reshape(bsz * nc_seq, NH, LANE))


def _dn_intra_bwd(qn, kn, v, projp, alog_row, dtb_row, u, w, tmat, du, dw, dqg, dkd, do, vn, dgl, *, nc_seq, name):
    n = qn.shape[0]
    nct = n // CHUNK
    hw = NH * DN_D
    scale = DN_D ** -0.5

    grp = _group(nc_seq)

    def inner(gi, q_ref, k_ref, v_ref, sa_ref, al_ref, dt_ref, u_ref, w_ref, t_ref, du_ref, dw_ref, dqg_ref, dkd_ref,
              do_ref, vn_ref, dgl_ref, dq_ref, dk_ref, dv_ref, dsa_ref, dal_ref, ddt_ref):
        ci = (pl.program_id(0) * grp + gi) % nc_seq
        sa = sa_ref[...]
        beta_t, g_t, gam_t, valid, ea = _dn_gates(sa, al_ref[...], dt_ref[...], ci)
        yield
        lane = _iota2((CHUNK, LANE), 1)
        gates_t = jnp.where(lane < 4, beta_t, gam_t).T
        r, c = _masks64()
        incl, strict, upper, supper = r >= c, r > c, r <= c, r < c
        rows1 = _iota2((CHUNK, 1), 0)
        acc = [jnp.zeros((CHUNK, LANE), F32)]

        def head(h):
            sl = slice(h * DN_D, (h + 1) * DN_D)
            beta = beta_t[:, h:h + 1]
            gam = gam_t[:, 4 + h:5 + h]
            beta_row = gates_t[h:h + 1, :]
            gam_row = gates_t[4 + h:5 + h, :]
            gl = gam_t[CHUNK - 1:CHUNK, 4 + h:5 + h]
            dec = jnp.exp(jnp.where(incl, gam - gam_row, -jnp.inf))
            dec_t = jnp.exp(jnp.where(upper, gam_row - gam, -jnp.inf))
            kh = k_ref[:, sl]
            qh = q_ref[:, sl] * scale
            vh = v_ref[:, sl]
            uh = u_ref[:, sl]
            wh = w_ref[:, sl]
            doh = do_ref[:, sl]
            vnh = vn_ref[:, sl]
            kk = _nt(kh, kh)
            qk = _nt(qh, kh)
            qk_t = _nt(kh, qh)
            dp = _nt(doh, vnh)
            dp_t = _nt(vnh, doh)
            tm_t = t_ref[0, h].T
            dvb = _nn(tm_t, du_ref[:, sl])
            dkg = _nn(tm_t, dw_ref[:, sl])
            yield
            m = _nt(dvb, uh) + _nt(dkg, wh)
            m_t = _nt(uh, dvb) + _nt(wh, dkg)
            yield
            da = jnp.where(strict, -m, 0.0)
            da_t = jnp.where(supper, -m_t, 0.0)
            a = jnp.where(strict, beta * kk * dec, 0.0)
            a_t = jnp.where(supper, beta_row * kk * dec_t, 0.0)
            dad = da * dec
            dad_t = da_t * dec_t
            dbeta = jnp.sum(dad * kk, axis=1, keepdims=True)
            dpm = jnp.where(incl, dp, 0.0)
            dpm_t = jnp.where(upper, dp_t, 0.0)
            e = da * a + dpm * (qk * dec)
            e_t = da_t * a_t + dpm_t * (qk_t * dec_t)
            dgam = jnp.sum(e, axis=1, keepdims=True) - jnp.sum(e_t, axis=1, keepdims=True)
            egam = jnp.exp(gam)
            ekd = jnp.exp(gl - gam)
            dqgh = dqg_ref[:, sl]
            dkdh = dkd_ref[:, sl]
            dkh = (_nn(beta * dad, kh) + _nn(beta_row * dad_t, kh) + _nn(dpm_t * dec_t, qh)
                   + (beta * egam) * dkg + ekd * dkdh)
            dqh = _nn(dpm * dec, kh) + egam * dqgh
            dbeta = dbeta + jnp.sum(dkg * (egam * kh), axis=1, keepdims=True) + jnp.sum(dvb * vh, axis=1, keepdims=True)
            rkd = jnp.sum(dkdh * (ekd * kh), axis=1, keepdims=True)
            dgam = (dgam + jnp.sum(dkg * ((beta * egam) * kh), axis=1, keepdims=True)
                    + jnp.sum(dqgh * (egam * qh), axis=1, keepdims=True) - rkd)
            dgam_last = jnp.sum(rkd, axis=0, keepdims=True) + dgl_ref[0, h:h + 1, 0:1] * jnp.exp(gl)
            dgam = dgam + jnp.where(rows1 == CHUNK - 1, dgam_last, 0.0)
            dq_ref[:, sl] = dqh * scale
            dk_ref[:, sl] = dkh
            dv_ref[:, sl] = beta * dvb
            acc[0] = acc[0] + jnp.where(lane == h, dbeta, 0.0) + jnp.where(lane == 4 + h, dgam, 0.0)

        yield from _round_robin([head(h) for h in range(NH)])
        acc_t = acc[0]
        dg_t = _nn_hi(upper.astype(F32), acc_t)
        ddb = acc_t * beta_t * (1.0 - beta_t)
        dda = jnp.where(valid, dg_t * (-ea) * _sigmoid(sa + dt_ref[...]), 0.0)
        dsa_ref[...] = jnp.where(lane < 4, ddb, jnp.where(lane < 8, dda, 0.0)).astype(dsa_ref.dtype)
        in_g = jnp.logical_and(lane >= 4, lane < 8)
        dal = jnp.sum(jnp.where(in_g, dg_t * g_t, 0.0), axis=0, keepdims=True)
        ddt = jnp.sum(jnp.where(in_g, dda, 0.0), axis=0, keepdims=True)
        _accumulate(dal_ref, dal, gi)
        _accumulate(ddt_ref, ddt, gi)

    rows = grp * CHUNK
    row = pl.BlockSpec((rows, hw), lambda i: (i, 0))
    vec = pl.BlockSpec((1, LANE), lambda i: (0, 0))
    mat = pl.BlockSpec((grp, NH, CHUNK, CHUNK), lambda i: (i, 0, 0, 0))
    glb = pl.BlockSpec((grp, NH, LANE), lambda i: (i, 0, 0))
    big = jax.ShapeDtypeStruct((n, hw), F32)
    v128 = jax.ShapeDtypeStruct((1, LANE), F32)
    kinds = (["row"] * 4 + ["whole"] * 2 + ["row"] * 2 + ["lead"] + ["row"] * 6 + ["lead"]
             + ["row"] * 4 + ["whole"] * 2)
    return pl.pallas_call(
        _per_chunk(inner, kinds, grp), name=name, grid=(nct // grp,),
        in_specs=[row, row, row, pl.BlockSpec((rows, LANE), lambda i: (i, C_SA // LANE)), vec, vec,
                  row, row, mat, row, row, row, row, row, row, glb],
        out_specs=[row, row, row, pl.BlockSpec((rows, LANE), lambda i: (i, 0)), vec, vec],
        out_shape=[big, big, big, jax.ShapeDtypeStruct((n, LANE), BF16), v128, v128],
        compiler_params=_cp(VMEM_BIG, ("arbitrary",)),
    )(qn, kn, v, projp, alog_row, dtb_row, u, w, tmat, du, dw, dqg, dkd, do, vn, dgl)


GQ_W = NH * GLA_DK
GV_W = NH * GLA_DV
GLA_NORM = 16.0
MID = CHUNK // 2


def _gla_gates(sb, w2p, gb, chunk_in_seq):
    rows = _iota2((CHUNK, GQ_W), 0)
    valid = jnp.logical_or(rows >= N_PAD, chunk_in_seq > 0)
    graw = _nn_hi(sb, w2p) + gb
    yield
    g = jnp.where(valid, _logsigmoid(graw) * (1.0 / GLA_NORM), 0.0)
    r, c = _masks64()
    bcum = _nn_hi((r >= c).astype(F32), g)
    yield
    return graw, bcum, valid


def _head_mask(h):
    lane = _iota2((1, GQ_W), 1)
    return jnp.logical_and(lane >= h * GLA_DK, lane < (h + 1) * GLA_DK)


def _gla_intra_fwd(projp, w2p, gb, *, nc_seq, name):
    n = projp.shape[0]
    nct = n // CHUNK
    scale = GLA_DK ** -0.5

    grp = _group(nc_seq)
    rows = grp * CHUNK

    def inner(gi, qk_ref, v_ref, sb_ref, w2_ref, gb_ref, oi_ref, qg_ref, kd_ref, gl_ref):
        ci = (pl.program_id(0) * grp + gi) % nc_seq
        _, bc, _ = yield from _gla_gates(sb_ref[...], w2_ref[...], gb_ref[...], ci)
        bref = bc[MID:MID + 1, :]
        bl = bc[CHUNK - 1:CHUNK, :]
        q = qk_ref[:, 0:GQ_W] * scale
        k = qk_ref[:, GQ_W:2 * GQ_W]
        qi = q * jnp.exp(bc - bref)
        ki = k * jnp.exp(bref - bc)
        qg_ref[...] = q * jnp.exp(bc)
        kd_ref[...] = k * jnp.exp(bl - bc)
        gl_ref[0] = jnp.exp(bl)
        r, c = _masks64()
        incl = r >= c
        a = [jnp.where(incl, _nt(jnp.where(_head_mask(h), qi, 0.0), ki), 0.0) for h in range(NH)]
        yield
        for h in range(NH):
            oi_ref[:, h * GLA_DV:(h + 1) * GLA_DV] = _nn(a[h], v_ref[:, h * GLA_DV:(h + 1) * GLA_DV])

    kinds = ["row"] * 3 + ["whole"] * 2 + ["row"] * 3 + ["lead"]
    return pl.pallas_call(
        _per_chunk(inner, kinds, grp), name=name, grid=(nct // grp,),
        in_specs=[pl.BlockSpec((rows, 2 * GQ_W), lambda i: (i, C_GQK // (2 * GQ_W))),
                  pl.BlockSpec((rows, GV_W), lambda i: (i, C_GV // GV_W)),
                  pl.BlockSpec((rows, LANE), lambda i: (i, C_SB // LANE)),
                  pl.BlockSpec((LANE, GQ_W), lambda i: (0, 0)), pl.BlockSpec((1, GQ_W), lambda i: (0, 0))],
        out_specs=[pl.BlockSpec((rows, GV_W), lambda i: (i, 0)), pl.BlockSpec((rows, GQ_W), lambda i: (i, 0)),
                   pl.BlockSpec((rows, GQ_W), lambda i: (i, 0)), pl.BlockSpec((grp, 1, GQ_W), lambda i: (i, 0, 0))],
        out_shape=[jax.ShapeDtypeStruct((n, GV_W), F32), jax.ShapeDtypeStruct((n, GQ_W), F32),
                   jax.ShapeDtypeStruct((n, GQ_W), F32), jax.ShapeDtypeStruct((nct, 1, GQ_W), F32)],
        compiler_params=_cp(VMEM_BIG),
    )(projp, projp, projp, w2p, gb)


def _gla_scan_fwd(oi, qg, kd, gl, projp, *, bsz, nc_seq, name, rider=None):
    t_seq = nc_seq * CHUNK
    oi = oi.reshape(bsz, t_seq, GV_W)
    qg, kd = qg.reshape(bsz, t_seq, GQ_W), kd.reshape(bsz, t_seq, GQ_W)
    gl = gl.reshape(bsz, nc_seq, 1, GQ_W)
    pj = projp.reshape(bsz, t_seq, PW)

    def body(oi_ref, qg_ref, kd_ref, gl_ref, v_ref, o_ref, hist_ref, st_ref):
        @pl.when(pl.program_id(0) == 0)
        def _():
            st_ref[...] = jnp.zeros_like(st_ref)

        for b in range(bsz):
            st = st_ref[b]
            hist_ref[b, 0] = st
            qgb = qg_ref[b]
            kdb = kd_ref[b]
            upd = jnp.zeros((GLA_DV, GQ_W), F32)
            for h in range(NH):
                sl = slice(h * GLA_DV, (h + 1) * GLA_DV)
                m = _head_mask(h)
                o_ref[b, :, sl] = oi_ref[b, :, sl] + _nt(jnp.where(m, qgb, 0.0), st)
                upd = upd + jnp.where(m, _tn(v_ref[b, :, sl], kdb), 0.0)
            st_ref[b] = gl_ref[b, 0] * st + upd

    outs, ridden = _hosted_call(
        body, rider, name=name, grid=(nc_seq,),
        in_specs=[pl.BlockSpec((bsz, CHUNK, GV_W), lambda i: (0, i, 0)),
                  pl.BlockSpec((bsz, CHUNK, GQ_W), lambda i: (0, i, 0)),
                  pl.BlockSpec((bsz, CHUNK, GQ_W), lambda i: (0, i, 0)),
                  pl.BlockSpec((bsz, 1, 1, GQ_W), lambda i: (0, i, 0, 0)),
                  pl.BlockSpec((bsz, CHUNK, GV_W), lambda i: (0, i, C_GV // GV_W))],
        out_specs=[pl.BlockSpec((bsz, CHUNK, GV_W), lambda i: (0, i, 0)),
                   pl.BlockSpec((bsz, 1, GLA_DV, GQ_W), lambda i: (0, i, 0, 0))],
        out_shape=[jax.ShapeDtypeStruct((bsz, t_seq, GV_W), F32),
                   jax.ShapeDtypeStruct((bsz, nc_seq, GLA_DV, GQ_W), F32)],
        scratch_shapes=[pltpu.VMEM((bsz, GLA_DV, GQ_W), F32)],
        compiler_params=_cp(VMEM_BIG, ("arbitrary",)), args=(oi, qg, kd, gl, pj))
    return outs[0].reshape(bsz * t_seq, GV_W), outs[1], ridden


def _gla_scan_bwd(do, qg, kd, gl, projp, hist, *, bsz, nc_seq, name):
    t_seq = nc_seq * CHUNK
    do = do.reshape(bsz, t_seq, GV_W)
    qg, kd = qg.reshape(bsz, t_seq, GQ_W), kd.reshape(bsz, t_seq, GQ_W)
    gl = gl.reshape(bsz, nc_seq, 1, GQ_W)
    pj = projp.reshape(bsz, t_seq, PW)

    def body(do_ref, qg_ref, kd_ref, gl_ref, v_ref, hist_ref, dqg_ref, dkd_ref, dv_ref, dgl_ref, dst_ref):
        @pl.when(pl.program_id(0) == 0)
        def _():
            dst_ref[...] = jnp.zeros_like(dst_ref)

        for b in range(bsz):
            st = hist_ref[b, 0]
            dst = dst_ref[b]
            qgb = qg_ref[b]
            kdb = kd_ref[b]
            dqg = jnp.zeros((CHUNK, GQ_W), F32)
            dkd = jnp.zeros((CHUNK, GQ_W), F32)
            add = jnp.zeros((GLA_DV, GQ_W), F32)
            for h in range(NH):
                sl = slice(h * GLA_DV, (h + 1) * GLA_DV)
                m = _head_mask(h)
                doh = do_ref[b, :, sl]
                vh = v_ref[b, :, sl]
                dqg = dqg + jnp.where(m, _nn(doh, st), 0.0)
                dkd = dkd + jnp.where(m, _nn(vh, dst), 0.0)
                dv_ref[b, :, sl] = _nt(jnp.where(m, kdb, 0.0), dst)
                add = add + jnp.where(m, _tn(doh, qgb), 0.0)
            dqg_ref[b] = dqg
            dkd_ref[b] = dkd
            dgl_ref[b, 0] = jnp.sum(dst * st, axis=0, keepdims=True)
            dst_ref[b] = gl_ref[b, 0] * dst + add

    rev = lambda i: nc_seq - 1 - i
    outs = pl.pallas_call(
        body, name=name, grid=(nc_seq,),
        in_specs=[pl.BlockSpec((bsz, CHUNK, GV_W), lambda i: (0, rev(i), 0)),
                  pl.BlockSpec((bsz, CHUNK, GQ_W), lambda i: (0, rev(i), 0)),
                  pl.BlockSpec((bsz, CHUNK, GQ_W), lambda i: (0, rev(i), 0)),
                  pl.BlockSpec((bsz, 1, 1, GQ_W), lambda i: (0, rev(i), 0, 0)),
                  pl.BlockSpec((bsz, CHUNK, GV_W), lambda i: (0, rev(i), C_GV // GV_W)),
                  pl.BlockSpec((bsz, 1, GLA_DV, GQ_W), lambda i: (0, rev(i), 0, 0))],
        out_specs=[pl.BlockSpec((bsz, CHUNK, GQ_W), lambda i: (0, rev(i), 0)),
                   pl.BlockSpec((bsz, CHUNK, GQ_W), lambda i: (0, rev(i), 0)),
                   pl.BlockSpec((bsz, CHUNK, GV_W), lambda i: (0, rev(i), 0)),
                   pl.BlockSpec((bsz, 1, 1, GQ_W), lambda i: (0, rev(i), 0, 0))],
        out_shape=[jax.ShapeDtypeStruct((bsz, t_seq, GQ_W), F32), jax.ShapeDtypeStruct((bsz, t_seq, GQ_W), F32),
                   jax.ShapeDtypeStruct((bsz, t_seq, GV_W), F32), jax.ShapeDtypeStruct((bsz, nc_seq, 1, GQ_W), F32)],
        scratch_shapes=[pltpu.VMEM((bsz, GLA_DV, GQ_W), F32)],
        compiler_params=_cp(VMEM_BIG, ("arbitrary",)),
    )(do, qg, kd, gl, pj, hist)
    n = bsz * t_seq
    return (outs[0].reshape(n, GQ_W), outs[1].reshape(n, GQ_W), outs[2].reshape(n, GV_W),
            outs[3].reshape(bsz * nc_seq, 1, GQ_W))


def _gla_intra_bwd(projp, w2p, gb, do, dqg, dkd, dvi, dgl, *, nc_seq, name):
    n = projp.shape[0]
    nct = n // CHUNK
    scale = GLA_DK ** -0.5

    grp = _group(nc_seq)
    rows = grp * CHUNK

    def inner(gi, qk_ref, v_ref, sb_ref, w2_ref, gb_ref, do_ref, dqg_ref, dkd_ref, dvi_ref, dgl_ref,
              dqk_ref, dv_ref, dsb_ref, dw2_ref, dgb_ref):
        ci = (pl.program_id(0) * grp + gi) % nc_seq
        sb = sb_ref[...]
        w2 = w2_ref[...]
        graw, bc, valid = yield from _gla_gates(sb, w2, gb_ref[...], ci)
        bref = bc[MID:MID + 1, :]
        bl = bc[CHUNK - 1:CHUNK, :]
        q = qk_ref[:, 0:GQ_W] * scale
        k = qk_ref[:, GQ_W:2 * GQ_W]
        ex1 = jnp.exp(bc - bref)
        ex2 = jnp.exp(bref - bc)
        eb = jnp.exp(bc)
        ekd = jnp.exp(bl - bc)
        qi, ki = q * ex1, k * ex2
        r, c = _masks64()
        incl = r >= c
        upper = r <= c
        a_t, da, da_t = [], [], []
        for h in range(NH):
            sl = slice(h * GLA_DV, (h + 1) * GLA_DV)
            doh = do_ref[:, sl]
            vh = v_ref[:, sl]
            a_t.append(jnp.where(upper, _nt(jnp.where(_head_mask(h), ki, 0.0), qi), 0.0))
            da.append(jnp.where(incl, _nt(doh, vh), 0.0))
            da_t.append(jnp.where(upper, _nt(vh, doh), 0.0))
        yield
        dqi = jnp.zeros((CHUNK, GQ_W), F32)
        dki = jnp.zeros((CHUNK, GQ_W), F32)
        for h in range(NH):
            sl = slice(h * GLA_DV, (h + 1) * GLA_DV)
            m = _head_mask(h)
            dv_ref[:, sl] = (_nn(a_t[h], do_ref[:, sl]) + dvi_ref[:, sl]).astype(dv_ref.dtype)
            dqi = dqi + jnp.where(m, _nn(da[h], ki), 0.0)
            dki = dki + jnp.where(m, _nn(da_t[h], qi), 0.0)
        yield
        dqg = dqg_ref[...]
        dkd = dkd_ref[...]
        dqk_ref[:, 0:GQ_W] = ((dqi * ex1 + dqg * eb) * scale).astype(dqk_ref.dtype)
        dqk_ref[:, GQ_W:2 * GQ_W] = (dki * ex2 + dkd * ekd).astype(dqk_ref.dtype)
        t_qi, t_ki, t_kd = dqi * qi, dki * ki, dkd * (k * ekd)
        db = t_qi - t_ki + dqg * (q * eb) - t_kd
        dbref = jnp.sum(t_ki - t_qi, axis=0, keepdims=True)
        dbl = jnp.sum(t_kd, axis=0, keepdims=True) + dgl_ref[0] * jnp.exp(bl)
        rows = _iota2((CHUNK, GQ_W), 0)
        db = db + jnp.where(rows == MID, dbref, 0.0) + jnp.where(rows == CHUNK - 1, dbl, 0.0)
        dg = _nn_hi(upper.astype(F32), db)
        yield
        dgraw = jnp.where(valid, dg * (1.0 / GLA_NORM) * _sigmoid(-graw), 0.0)
        dsb_ref[...] = _nt_hi(dgraw, w2).astype(dsb_ref.dtype)
        dw2 = _tn_hi(sb, dgraw)
        dgb = jnp.sum(dgraw, axis=0, keepdims=True)
        _accumulate(dw2_ref, dw2, gi)
        _accumulate(dgb_ref, dgb, gi)

    rq = pl.BlockSpec((rows, GQ_W), lambda i: (i, 0))
    rv = pl.BlockSpec((rows, GV_W), lambda i: (i, 0))
    kinds = ["row"] * 3 + ["whole"] * 2 + ["row"] * 4 + ["lead"] + ["row"] * 3 + ["whole"] * 2
    return pl.pallas_call(
        _per_chunk(inner, kinds, grp), name=name, grid=(nct // grp,),
        in_specs=[pl.BlockSpec((rows, 2 * GQ_W), lambda i: (i, C_GQK // (2 * GQ_W))),
                  pl.BlockSpec((rows, GV_W), lambda i: (i, C_GV // GV_W)),
                  pl.BlockSpec((rows, LANE), lambda i: (i, C_SB // LANE)),
                  pl.BlockSpec((LANE, GQ_W), lambda i: (0, 0)), pl.BlockSpec((1, GQ_W), lambda i: (0, 0)),
                  rv, rq, rq, rv, pl.BlockSpec((grp, 1, GQ_W), lambda i: (i, 0, 0))],
        out_specs=[pl.BlockSpec((rows, 2 * GQ_W), lambda i: (i, 0)), rv, pl.BlockSpec((rows, LANE), lambda i: (i, 0)),
                   pl.BlockSpec((LANE, GQ_W), lambda i: (0, 0)), pl.BlockSpec((1, GQ_W), lambda i: (0, 0))],
        out_shape=[jax.ShapeDtypeStruct((n, 2 * GQ_W), BF16), jax.ShapeDtypeStruct((n, GV_W), BF16),
                   jax.ShapeDtypeStruct((n, LANE), BF16), jax.ShapeDtypeStruct((LANE, GQ_W), F32),
                   jax.ShapeDtypeStruct((1, GQ_W), F32)],
        compiler_params=_cp(VMEM_BIG, ("arbitrary",)),
    )(projp, projp, projp, w2p, gb, do, dqg, dkd, dvi, dgl)


SECTIONS = ((C_QKV, 1536), (C_DZ, 512), (C_GQK, 512), (C_GV, 512), (C_GR, 512), (C_SA, 128), (C_SB, 128))


def _inproj_bwd(secs, wp, h0, g1, dx1, *, tr, name):
    n, d = h0.shape

    def body(*refs):
        sec_refs = refs[:len(SECTIONS)]
        wp_ref, h0_ref, g_ref, dx1_ref, o_ref, dg_ref = refs[len(SECTIONS):]
        dh = None
        for s_ref, (off, wd) in zip(sec_refs, SECTIONS):
            part = _nt(s_ref[...], wp_ref[:, off:off + wd])
            dh = part if dh is None else dh + part
        dx, dg = _rms_bwd_math(h0_ref[...], g_ref[...], dh)
        o_ref[...] = dx1_ref[...] + dx

        @pl.when(pl.program_id(0) == 0)
        def _():
            dg_ref[...] = dg

        @pl.when(pl.program_id(0) > 0)
        def _():
            dg_ref[...] += dg

    row = pl.BlockSpec((tr, d), lambda i: (i, 0))
    vec = pl.BlockSpec((1, d), lambda i: (0, 0))
    return pl.pallas_call(
        body, name=name, grid=(n // tr,),
        in_specs=[pl.BlockSpec((tr, wd), lambda i: (i, 0)) for _, wd in SECTIONS]
        + [pl.BlockSpec((d, PW), lambda i: (0, 0)), row, vec, row],
        out_specs=[row, vec],
        out_shape=[jax.ShapeDtypeStruct((n, d), F32), jax.ShapeDtypeStruct((1, d), F32)],
        compiler_params=_cp(VMEM_BIG),
    )(*secs, wp, h0, g1, dx1)


def _adamw(w, g, m, v, *, name, emit_grad=False):
    lead = w.ndim - 2
    r, c = w.shape[-2:]
    tr = _tile(r, 256, 8) if r > 256 else r
    c1 = 1.0 - ADAM_B1 ** ADAM_STEP
    c2 = 1.0 - ADAM_B2 ** ADAM_STEP
    n_out = 4 if emit_grad else 3

    def body(w_ref, g_ref, m_ref, v_ref, *out_refs):
        rd = (lambda ref: ref[0]) if lead else (lambda ref: ref[...])
        gv = g_ref[:, 0:c]
        nm = ADAM_B1 * rd(m_ref) + (1.0 - ADAM_B1) * gv
        nv = ADAM_B2 * rd(v_ref) + (1.0 - ADAM_B2) * (gv * gv)
        res = [-ADAM_LR * ((nm / c1) / (jnp.sqrt(nv / c2) + ADAM_EPS) + ADAM_WD * rd(w_ref)), nm, nv, gv]
        for o_ref, val in zip(out_refs, res):
            if lead:
                o_ref[0] = val
            else:
                o_ref[...] = val

    blk = pl.BlockSpec((1,) * lead + (tr, c), lambda i: (0,) * lead + (i, 0))
    gblk = pl.BlockSpec((tr, g.shape[1]), lambda i: (i, 0))
    sds = jax.ShapeDtypeStruct(w.shape, F32)
    return pl.pallas_call(
        body, name=name, grid=(r // tr,), in_specs=[blk, gblk, blk, blk], out_specs=[blk] * n_out,
        out_shape=[sds] * n_out, compiler_params=_cp(VMEM_BIG),
    )(w, g, m, v)


def _pair_sum(where, g, theirs, *, name):
    lead, r, cols = g.shape
    half = r // 2
    tr = _tile(half, 256, 16)
    nh = half // tr

    def body(w_ref, a_ref, b_ref, o_ref):
        o_ref[...] = (a_ref[...] + b_ref[...]).astype(o_ref.dtype)

    blk = pl.BlockSpec((1, tr, cols), lambda s, i, w: (s, i, 0))
    return pl.pallas_call(
        body, name=name,
        grid_spec=pltpu.PrefetchScalarGridSpec(
            num_scalar_prefetch=1, grid=(lead, nh),
            in_specs=[pl.BlockSpec((1, tr, cols), lambda s, i, w: (s, w[0] * nh + i, 0)), blk], out_specs=blk),
        out_shape=jax.ShapeDtypeStruct((lead, half, cols), BF16), compiler_params=_cp(VMEM_BIG),
    )(where, g, theirs)


def _chip_sum(where, pair, q, *, name):
    _, half, cols = pair.shape
    tr = _tile(half, 256, 16)
    nh = half // tr

    def body(w_ref, own_ref, q1_ref, q2_ref, q3_ref, o_ref):
        f = lambda ref: ref[0].astype(F32)
        o_ref[...] = ((f(own_ref) + f(q1_ref)) + f(q2_ref)) + f(q3_ref)

    def peer(d):
        return pl.BlockSpec((1, tr, cols), lambda i, w: ((w[1] + d) % N_CHIPS, i, 0))

    return pl.pallas_call(
        body, name=name,
        grid_spec=pltpu.PrefetchScalarGridSpec(
            num_scalar_prefetch=1, grid=(nh,),
            in_specs=[peer(0), peer(1), peer(2), peer(3)],
            out_specs=pl.BlockSpec((tr, cols), lambda i, w: (w[0] * nh + i, 0))),
        out_shape=jax.ShapeDtypeStruct((2 * half, cols), F32), compiler_params=_cp(VMEM_BIG),
    )(where, pair, q, q, q)


VM = pl.BlockSpec(memory_space=pltpu.VMEM)


def _row_chunks(rows, n_split):
    size = rows // n_split
    assert size * n_split == rows and size % 16 == 0, (rows, n_split)
    return [(s, pl.ds(s * size, size)) for s in range(n_split)], size


D2D_SPLIT = 4
ICI_SPLIT = 2


def _sibling_halves(grads):
    n_arr = len(grads)

    def body(*refs):
        ins = refs[:n_arr]
        theirs = refs[n_arr:2 * n_arr]
        send_sems, recv_sems = refs[2 * n_arr:]
        x, y, c = _place()
        copies = []
        for k in range(n_arr):
            half = ins[k].shape[1] // 2
            chunks, size = _row_chunks(half, D2D_SPLIT)
            for s, dst_rows in chunks:
                give = pltpu.make_async_remote_copy(
                    src_ref=ins[k].at[:, pl.ds((1 - c) * half + s * size, size), :], dst_ref=theirs[k].at[:, dst_rows, :],
                    send_sem=send_sems.at[k, s], recv_sem=recv_sems.at[k, s], device_id=(x, y, 1 - c),
                    device_id_type=MESH)
                give.start()
                copies.append(give)
        for give in copies:
            give.wait()

    halves = [jax.ShapeDtypeStruct((g.shape[0], g.shape[1] // 2, g.shape[2]), F32) for g in grads]
    sem = pltpu.SemaphoreType.DMA((n_arr, D2D_SPLIT))
    return pl.pallas_call(
        body, name="sibling_halves", in_specs=[ANY] * n_arr, out_specs=[ANY] * n_arr, out_shape=halves,
        scratch_shapes=[sem, sem],
    )(*grads)


def _chip_exchange(parts):
    n_arr = len(parts)

    def body(*refs):
        ins = refs[:n_arr]
        outs = refs[n_arr:2 * n_arr]
        send_sems, recv_sems = refs[2 * n_arr:]
        x, y, c = _place()
        me = 2 * x + y
        sends = []
        for k in range(n_arr):
            chunks, _ = _row_chunks(ins[k].shape[1], ICI_SPLIT)
            for d, (px, py, pj) in enumerate(_other_chips(x, y)):
                for s, rows in chunks:
                    cp = pltpu.make_async_remote_copy(
                        src_ref=ins[k].at[pj, rows, :], dst_ref=outs[k].at[me, rows, :], send_sem=send_sems.at[k, d, s],
                        recv_sem=recv_sems.at[k, d, s], device_id=(px, py, c), device_id_type=MESH)
                    cp.start()
                    sends.append(cp)
        for k in range(n_arr):
            chunks, _ = _row_chunks(ins[k].shape[1], ICI_SPLIT)
            for d, (px, py, pj) in enumerate(_other_chips(x, y)):
                for s, rows in chunks:
                    pltpu.make_async_remote_copy(
                        src_ref=ins[k].at[pj, rows, :], dst_ref=outs[k].at[pj, rows, :], send_sem=send_sems.at[k, d, s],
                        recv_sem=recv_sems.at[k, d, s], device_id=(px, py, c), device_id_type=MESH).wait_recv()
        for cp in sends:
            cp.wait_send()

    sem = pltpu.SemaphoreType.DMA((n_arr, 3, ICI_SPLIT))
    return pl.pallas_call(
        body, name="chip_exchange", in_specs=[ANY] * n_arr, out_specs=[ANY] * n_arr,
        out_shape=[jax.ShapeDtypeStruct(p.shape, p.dtype) for p in parts],
        scratch_shapes=[sem, sem],
    )(*parts)


def _sibling_join(bufs):
    n_arr = len(bufs)

    def body(*refs):
        bufs_out = refs[n_arr:2 * n_arr]
        send_sems, recv_sems = refs[2 * n_arr:]
        x, y, c = _place()
        copies = []
        for k in range(n_arr):
            half = bufs_out[k].shape[0] // 2
            chunks, size = _row_chunks(half, D2D_SPLIT)
            for s, _ in chunks:
                rows = pl.ds(c * half + s * size, size)
                give = pltpu.make_async_remote_copy(
                    src_ref=bufs_out[k].at[rows, :], dst_ref=bufs_out[k].at[rows, :], send_sem=send_sems.at[k, s],
                    recv_sem=recv_sems.at[k, s], device_id=(x, y, 1 - c), device_id_type=MESH)
                give.start()
                copies.append((k, s, half, size, give))
        for k, s, half, size, give in copies:
            rows = pl.ds((1 - c) * half + s * size, size)
            pltpu.make_async_remote_copy(
                src_ref=bufs_out[k].at[rows, :], dst_ref=bufs_out[k].at[rows, :], send_sem=send_sems.at[k, s],
                recv_sem=recv_sems.at[k, s], device_id=(x, y, 1 - c), device_id_type=MESH).wait_recv()
            give.wait_send()

    sem = pltpu.SemaphoreType.DMA((n_arr, D2D_SPLIT))
    return pl.pallas_call(
        body, name="sibling_join", in_specs=[ANY] * n_arr, out_specs=[ANY] * n_arr,
        out_shape=[jax.ShapeDtypeStruct(b.shape, F32) for b in bufs],
        input_output_aliases={k: k for k in range(n_arr)},
        scratch_shapes=[sem, sem],
    )(*bufs)


PACK_ROWS = 48


def _small_allreduce(pack):
    masks = [(dx, dy, dc) for dx in (0, 1) for dy in (0, 1) for dc in (0, 1)][1:]

    def body(p_ref, o_ref, buf, send_sems, recv_sems):
        x, y, c = _place()
        me = 4 * x + 2 * y + c
        buf[me] = p_ref[...]
        sends = []
        for k, (dx, dy, dc) in enumerate(masks):
            peer = (1 - x if dx else x, 1 - y if dy else y, 1 - c if dc else c)
            cp = pltpu.make_async_remote_copy(
                src_ref=p_ref, dst_ref=buf.at[me], send_sem=send_sems.at[k], recv_sem=recv_sems.at[k],
                device_id=peer, device_id_type=MESH)
            cp.start()
            sends.append(cp)
        for k, (dx, dy, dc) in enumerate(masks):
            peer = (1 - x if dx else x, 1 - y if dy else y, 1 - c if dc else c)
            pj = 4 * peer[0] + 2 * peer[1] + peer[2]
            pltpu.make_async_remote_copy(
                src_ref=p_ref, dst_ref=buf.at[pj], send_sem=send_sems.at[k], recv_sem=recv_sems.at[k],
                device_id=peer, device_id_type=MESH).wait_recv()
        for cp in sends:
            cp.wait_send()
        tot = buf[0]
        for k in range(1, 8):
            tot = tot + buf[k]
        o_ref[...] = tot
        o_ref[0:N_META, :] = tot[0:N_META] + tot[N_META:2 * N_META]

    return pl.pallas_call(
        body, name="small_allreduce", in_specs=[VM], out_specs=VM,
        out_shape=jax.ShapeDtypeStruct((PACK_ROWS, D_MODEL), F32),
        scratch_shapes=[pltpu.VMEM((8, PACK_ROWS, D_MODEL), F32), pltpu.SemaphoreType.DMA((7,)),
                        pltpu.SemaphoreType.DMA((7,))],
    )(pack)


def _pad_lanes(vec, offset):
    k = vec.shape[1]
    return jnp.concatenate([jnp.zeros((1, offset), F32), vec, jnp.zeros((1, LANE - offset - k), F32)], axis=1)


def _local_step(x, tgt, meta, norm1_g, wp, conv_w, a_log, dt_bias, dn_norm_g, gla_w2, gla_b, gla_norm_g,
                w_out, norm2_g, w_up, w_down, final_norm_g, late_gather=None):
    bsz, s_len, d = x.shape
    t_seq = s_len + CHUNK
    nc_seq = t_seq // CHUNK
    n = bsz * t_seq
    tr = _tile(t_seq, 832)
    tt = _tile(t_seq, 416)

    lead = jnp.concatenate([jnp.zeros((N_PAD, d), F32), meta], axis=0)
    h0 = jnp.concatenate([jnp.broadcast_to(lead[None], (bsz, CHUNK, d)), x], axis=1).reshape(n, d)
    tgt_p = jnp.concatenate([jnp.zeros((bsz, CHUNK, d), F32), tgt], axis=1).reshape(n, d)
    alog_row = _pad_lanes(a_log, 4)
    dtb_row = _pad_lanes(dt_bias, 4)
    w2p = jnp.concatenate([gla_w2, jnp.zeros((LANE - GLA_RANK, GQ_W), F32)], axis=0)

    h = _rms_fwd(h0, norm1_g, tr=tr, name="norm1")
    (projp,) = _mm(h, wp, "nn", tm=tt, tn=PW, tk=d, out_dtypes=(F32,), name="in_proj")
    qn, kn, v = _dnprep_fwd(projp, conv_w, bsz=bsz, t_seq=t_seq, tt=tt, name="dn_prep")
    u, w, qg, kd, pmat, tmat, gl = _dn_intra_fwd(qn, kn, v, projp, alog_row, dtb_row, nc_seq=nc_seq, name="dn_intra")
    ride_a, ride_b = late_gather if late_gather is not None else (None, None)
    o_dn, vn, hist, got_a = _dn_scan_fwd(u, w, qg, kd, pmat, gl, bsz=bsz, nc_seq=nc_seq, name="dn_scan", rider=ride_a)
    oi, gqg, gkd, ggl = _gla_intra_fwd(projp, w2p, gla_b, nc_seq=nc_seq, name="gla_intra")
    o_gla, ghist, got_b = _gla_scan_fwd(oi, gqg, gkd, ggl, projp, bsz=bsz, nc_seq=nc_seq, name="gla_scan", rider=ride_b)
    if late_gather is not None:
        w_out = got_a[0].reshape(d, d)
        w_up = got_a[1].transpose(1, 0, 2).reshape(d, D_FF)
        w_down = got_b[0].reshape(D_FF, d)
    mix = _gnorm_fwd(o_dn, o_gla, projp, dn_norm_g, gla_norm_g, tr=tr, name="gated_norm")
    (x1,) = _mm(mix, w_out, "nn", tm=tr, tn=d, tk=d, out_dtypes=(F32,), extras=(h0,),
                epilogue=lambda acc, res: (res + acc,), name="out_proj")
    h2 = _rms_fwd(x1, norm2_g, tr=tr, name="norm2")

    (act,) = _mm(h2, w_up, "nn", tm=tt, tn=D_FF, tk=d, out_dtypes=(BF16,),
                 epilogue=lambda acc: (jnp.square(jnp.maximum(acc, 0.0)),), name="mlp_up")
    (x2,) = _mm(act, w_down, "nn", tm=tr, tn=d, tk=D_FF, out_dtypes=(F32,), extras=(x1,),
                epilogue=lambda acc, res: (res + acc,), name="mlp_down")
    dx2, dx2b, d_final_g, loss_tile = _final_loss(x2, final_norm_g, tgt_p, t_seq=t_seq, tr=tr, name="final_loss")

    (dup,) = _mm(dx2b, w_down, "nt", tm=tt, tn=D_FF, tk=d, out_dtypes=(BF16,), extras=(act,),
                 epilogue=lambda acc, a: (acc * (2.0 * jnp.sqrt(a.astype(F32))),), name="mlp_down_bwd")
    (d_w_down,) = _mm(act, dx2b, "tn", tm=D_FF // 2, tn=d, tk=tr, out_dtypes=(F32,), name="w_down_grad")
    (d_w_up,) = _mm(h2, dup, "tn", tm=d, tn=D_FF // 2, tk=tr, out_dtypes=(F32,), name="w_up_grad")
    (dh2,) = _mm(dup, w_up, "nt", tm=tr, tn=d, tk=D_FF, out_dtypes=(F32,), name="mlp_up_bwd")
    dx1, dx1b, d_norm2_g = _rms_bwd_add(x1, norm2_g, dh2, dx2, tr=tr, name="norm2_bwd")

    (dmix,) = _mm(dx1b, w_out, "nt", tm=tr, tn=d, tk=d, out_dtypes=(F32,), name="out_proj_bwd")
    (d_w_out,) = _mm(mix, dx1b, "tn", tm=d, tn=d, tk=tr, out_dtypes=(F32,), name="w_out_grad")
    do_dn, ddz, do_gla, dgr, d_dn_norm_g, d_gla_norm_g = _gnorm_bwd(
        dmix, o_dn, o_gla, projp, dn_norm_g, gla_norm_g, tr=tr, name="gated_norm_bwd")
    du, dw, dqg, dkd, dgl = _dn_scan_bwd(do_dn, w, qg, kd, vn, pmat, gl, hist, bsz=bsz, nc_seq=nc_seq,
                                          name="dn_scan_bwd")
    dqn, dkn, dv, dsa, d_alog, d_dtb = _dn_intra_bwd(qn, kn, v, projp, alog_row, dtb_row, u, w, tmat,
                                                     du, dw, dqg, dkd, do_dn, vn, dgl, nc_seq=nc_seq,
                                                     name="dn_intra_bwd")
    dz, d_conv_w = _dnprep_bwd_a(projp, conv_w, dqn, dkn, dv, bsz=bsz, t_seq=t_seq, tt=tt, name="dn_prep_bwd")
    dcin = _dnprep_bwd_b(dz, conv_w, bsz=bsz, t_seq=t_seq, tt=tt, name="conv_bwd")
    gdqg, gdkd, gdvi, gdgl = _gla_scan_bwd(do_gla, gqg, gkd, ggl, projp, ghist, bsz=bsz, nc_seq=nc_seq,
                                            name="gla_scan_bwd")
    dgqk, dgv, dsb, d_w2p, d_gla_b = _gla_intra_bwd(projp, w2p, gla_b, do_gla, gdqg, gdkd, gdvi, gdgl,
                                                    nc_seq=nc_seq, name="gla_intra_bwd")

    secs = (dcin, ddz, dgqk, dgv, dgr, dsa, dsb)
    g_lo = _grad_tn(h, secs[0:2], tk=tr, name="w_in_grad_lo")
    g_hi = _grad_tn(h, secs[2:7], tk=tr, name="w_in_grad_hi")
    dh0, d_norm1_g = _inproj_bwd(secs, wp, h0, norm1_g, dx1, tr=tt, name="in_proj_bwd")
    dh0 = dh0.reshape(bsz, t_seq, d)
    grad_x = dh0[:, CHUNK:]
    d_meta_rows = dh0[:, N_PAD:CHUNK].reshape(bsz * N_META, d)

    grads = dict(w_in_lo=g_lo, w_in_hi=g_hi, w_out=d_w_out, w_up=d_w_up, w_down=d_w_down, meta_rows=d_meta_rows,
                 norm1_g=d_norm1_g, conv_w=d_conv_w, a_log_tile=d_alog, dt_bias_tile=d_dtb, dn_norm_g=d_dn_norm_g,
                 gla_w2=d_w2p[0:GLA_RANK], gla_b=d_gla_b, gla_norm_g=d_gla_norm_g, norm2_g=d_norm2_g,
                 final_norm_g=d_final_g, loss_tile=loss_tile)
    return grad_x, grads


SHARD_W = IN_WIDTH // N_CHIPS
PADDED_ORDER = ((0, 2048), (2056, 3592), (2048, 2056), LANE - 8, (3592, 3608), LANE - GLA_RANK)


def _pad_layout(w_full):
    pieces = [jnp.zeros((w_full.shape[0], seg), w_full.dtype) if isinstance(seg, int) else w_full[:, seg[0]:seg[1]]
              for seg in PADDED_ORDER]
    return jnp.concatenate(pieces, axis=1)


def _padded_from_shards(stack):
    pieces = []
    for seg in PADDED_ORDER:
        if isinstance(seg, int):
            pieces.append(jnp.zeros((stack.shape[1], seg), stack.dtype))
            continue
        for j in range(N_CHIPS):
            lo, hi = max(seg[0], j * SHARD_W), min(seg[1], (j + 1) * SHARD_W)
            if lo < hi:
                pieces.append(stack[j, :, lo - j * SHARD_W:hi - j * SHARD_W])
    return jnp.concatenate(pieces, axis=1)


def _shards_from_padded(g_lo, g_hi):
    split = g_lo.shape[1]
    starts, pos = [], 0
    for seg in PADDED_ORDER:
        width = seg if isinstance(seg, int) else seg[1] - seg[0]
        if not isinstance(seg, int):
            starts.append((seg[0], seg[1], pos))
        pos += width
    shards = []
    for j in range(N_CHIPS):
        pieces = []
        for a, b, p0 in sorted(starts):
            lo, hi = max(a, j * SHARD_W), min(b, (j + 1) * SHARD_W)
            if lo < hi:
                src, off = (g_lo, 0) if p0 < split else (g_hi, split)
                pieces.append(src[:, p0 + lo - a - off:p0 + hi - a - off])
        pieces.append(jnp.zeros((g_lo.shape[0], D_MODEL - SHARD_W), g_lo.dtype))
        shards.append(jnp.concatenate(pieces, axis=1))
    return jnp.stack(shards)


def _pack_small(g, bsz):
    assert bsz * N_META == 32
    row = jnp.concatenate([g["a_log_tile"], g["dt_bias_tile"], g["dn_norm_g"], g["gla_norm_g"], g["gla_b"],
                           g["loss_tile"], jnp.zeros((1, LANE), F32)], axis=1)
    return jnp.concatenate([g["meta_rows"], g["norm1_g"], g["conv_w"].reshape(6, D_MODEL), row,
                            g["gla_w2"].reshape(4, D_MODEL), g["norm2_g"], g["final_norm_g"],
                            jnp.zeros((2, D_MODEL), F32)], axis=0)


def kernel(x, meta_tokens, norm1_g, w_in, conv_w, a_log, dt_bias, dn_norm_g, gla_w2, gla_b, gla_norm_g, w_out, norm2_g, w_up, w_down, final_norm_g, loss_target, m_meta_tokens, m_norm1_g, m_w_in, m_conv_w, m_a_log, m_dt_bias, m_dn_norm_g, m_gla_w2, m_gla_b, m_gla_norm_g, m_w_out, m_norm2_g, m_w_up, m_w_down, m_final_norm_g, v_meta_tokens, v_norm1_g, v_w_in, v_conv_w, v_a_log, v_dt_bias, v_dn_norm_g, v_gla_w2, v_gla_b, v_gla_norm_g, v_w_out, v_norm2_g, v_w_up, v_w_down, v_final_norm_g):
    bsz = x.shape[0]
    chip = 2 * lax.axis_index("x") + lax.axis_index("y")

    lane_pad = lambda a, wd: jnp.pad(a, ((0, 0), (0, wd - a.shape[1])))
    where = jnp.stack([lax.axis_index("c"), chip]).astype(jnp.int32)
    slot = lambda a, dt, nm: _to_slot(where, a, dt, name="slot_" + nm)
    early = _GatherRider([slot(lane_pad(w_in[0], D_MODEL), BF16, "w_in"), slot(meta_tokens, F32, "meta"),
                          slot(conv_w[0], F32, "conv"), slot(lane_pad(gla_w2[0], LANE), F32, "gla_w2")],
                         [True, False, False, False])
    g_in, g_meta, g_conv, g_w2 = _exchange_now(early, name="gather_early")
    late = (_GatherRider([slot(w_out[0], BF16, "w_out"), slot(w_up[0], BF16, "w_up")], [True, True]),
            _GatherRider([slot(w_down[0], BF16, "w_down")], [True]))
    wp = _padded_from_shards(g_in)
    meta_f = g_meta.transpose(1, 0, 2).reshape(N_META, D_MODEL)
    conv_f = g_conv.transpose(1, 0, 2).reshape(4, QKV_W)
    w2_f = g_w2[:, :, 0:GQ_W // N_CHIPS].transpose(1, 0, 2).reshape(GLA_RANK, GQ_W)

    grad_x, g = _local_step(x, loss_target, meta_f, norm1_g, wp, conv_f, a_log, dt_bias, dn_norm_g, w2_f, gla_b,
                            gla_norm_g, None, norm2_g, None, None, final_norm_g.reshape(1, D_MODEL), late_gather=late)

    shard_major = [
        _shards_from_padded(g["w_in_lo"], g["w_in_hi"]),
        g["w_out"].reshape(N_CHIPS, D_MODEL // N_CHIPS, D_MODEL),
        g["w_up"].reshape(D_MODEL, N_CHIPS, D_FF // N_CHIPS).transpose(1, 0, 2),
        g["w_down"].reshape(N_CHIPS, D_FF // N_CHIPS, D_MODEL),
    ]
    theirs = _sibling_halves(shard_major)
    pair = [_pair_sum(where, a, b, name=f"pair_sum_{k}") for k, (a, b) in enumerate(zip(shard_major, theirs))]
    parts = _chip_exchange(pair)
    halves = [_chip_sum(where, p, q, name=f"chip_sum_{k}") for k, (p, q) in enumerate(zip(pair, parts))]
    gw_in, gw_out, gw_up, gw_down = _sibling_join(halves)

    red = _small_allreduce(_pack_small(g, bsz))
    g_meta_full = red[0:N_META]
    g_norm1 = red[32:33]
    g_conv_full = red[33:39].reshape(4, QKV_W)
    srow = red[39:40]
    g_alog, g_dtb = srow[:, 4:8], srow[:, LANE + 4:LANE + 8]
    g_dn_norm, g_gla_norm = srow[:, 2 * LANE:3 * LANE], srow[:, 3 * LANE:4 * LANE]
    g_gla_b = srow[:, 4 * LANE:6 * LANE]
    loss = srow[0, 6 * LANE]
    g_w2_full = red[40:44].reshape(GLA_RANK, GQ_W)
    g_norm2 = red[44:45]
    g_final = red[45:46]
    g_meta_sh = lax.dynamic_slice_in_dim(g_meta_full, chip * (D_MODEL // N_CHIPS), D_MODEL // N_CHIPS, axis=1)
    g_conv_sh = lax.dynamic_slice_in_dim(g_conv_full, chip * (QKV_W // N_CHIPS), QKV_W // N_CHIPS, axis=1)
    g_w2_sh = lax.dynamic_slice_in_dim(g_w2_full, chip * (GQ_W // N_CHIPS), GQ_W // N_CHIPS, axis=1)

    names = ["meta_tokens", "norm1_g", "w_in", "conv_w", "a_log", "dt_bias", "dn_norm_g", "gla_w2", "gla_b",
             "gla_norm_g", "w_out", "norm2_g", "w_up", "w_down", "final_norm_g"]
    weights = dict(meta_tokens=meta_tokens, norm1_g=norm1_g, w_in=w_in, conv_w=conv_w, a_log=a_log, dt_bias=dt_bias,
                   dn_norm_g=dn_norm_g, gla_w2=gla_w2, gla_b=gla_b, gla_norm_g=gla_norm_g, w_out=w_out,
                   norm2_g=norm2_g, w_up=w_up, w_down=w_down, final_norm_g=final_norm_g)
    ms = dict(meta_tokens=m_meta_tokens, norm1_g=m_norm1_g, w_in=m_w_in, conv_w=m_conv_w, a_log=m_a_log,
              dt_bias=m_dt_bias, dn_norm_g=m_dn_norm_g, gla_w2=m_gla_w2, gla_b=m_gla_b, gla_norm_g=m_gla_norm_g,
              w_out=m_w_out, norm2_g=m_norm2_g, w_up=m_w_up, w_down=m_w_down, final_norm_g=m_final_norm_g)
    vs = dict(meta_tokens=v_meta_tokens, norm1_g=v_norm1_g, w_in=v_w_in, conv_w=v_conv_w, a_log=v_a_log,
              dt_bias=v_dt_bias, dn_norm_g=v_dn_norm_g, gla_w2=v_gla_w2, gla_b=v_gla_b, gla_norm_g=v_gla_norm_g,
              w_out=v_w_out, norm2_g=v_norm2_g, w_up=v_w_up, w_down=v_w_down, final_norm_g=v_final_norm_g)
    grads2d = dict(meta_tokens=g_meta_sh, norm1_g=g_norm1, w_in=gw_in, conv_w=g_conv_sh, a_log=g_alog, dt_bias=g_dtb,
                   dn_norm_g=g_dn_norm, gla_w2=g_w2_sh, gla_b=g_gla_b, gla_norm_g=g_gla_norm, w_out=gw_out,
                   norm2_g=g_norm2, w_up=gw_up, w_down=gw_down, final_norm_g=g_final)
    out_g, out_d, out_m, out_v = [], [], [], []
    for nm in names:
        shape = weights[nm].shape
        g2 = grads2d[nm]
        if len(shape) == 3:
            res = _adamw(weights[nm], g2, ms[nm], vs[nm], name=f"adamw_{nm}", emit_grad=nm == "w_in")
            gout = res[3] if nm == "w_in" else g2.reshape(shape)
        else:
            as2d = lambda a: a.reshape(g2.shape)
            res = _adamw(as2d(weights[nm]), g2, as2d(ms[nm]), as2d(vs[nm]), name=f"adamw_{nm}")
            gout = g2.reshape(shape)
        out_g.append(gout)
        out_d.append(res[0].reshape(shape))
        out_m.append(res[1].reshape(shape))
        out_v.append(res[2].reshape(shape))
    return (loss, grad_x, *out_g, *out_d, *out_m, *out_v)
```

```python
import functools

import jax
import jax.numpy as jnp
import numpy as np
from jax import lax
from jax.experimental import pallas as pl
from jax.experimental.pallas import tpu as pltpu

F32 = jnp.float32
BF16 = jnp.bfloat16
HI = lax.Precision.HIGHEST
MESH = pl.DeviceIdType.MESH

D_MODEL = 1024
N_META = 16
CHUNK = 64
N_PAD = CHUNK - N_META
NH = 4
DN_D = 128
GLA_DK = 64
GLA_DV = 128
GLA_RANK = 16
D_FF = 4 * D_MODEL
EPS = 1e-6
IN_WIDTH = 3608
C_QKV, C_DZ, C_GQK, C_GV, C_GR, C_SA, C_SB, PW = 0, 1536, 2048, 2560, 3072, 3584, 3712, 3840
LANE = 128
N_CHIPS = 4

ADAM_LR, ADAM_B1, ADAM_B2, ADAM_EPS, ADAM_WD, ADAM_STEP = 0.001, 0.9, 0.999, 1e-08, 0.01, 10

VMEM_BIG = 56 * 1024 * 1024


def _cp(vmem=None, sem=None):
    kw = {}
    if vmem is not None:
        kw["vmem_limit_bytes"] = vmem
    if sem is not None:
        kw["dimension_semantics"] = sem
    return pltpu.CompilerParams(**kw)


def _tile(n, target, mult=16):
    best = None
    for t in range(mult, min(n, target) + 1, mult):
        if n % t == 0:
            best = t
    assert best is not None, (n, target)
    return best


def _dot(a, b, dims, prec=None):
    return lax.dot_general(a, b, (dims, ((), ())), preferred_element_type=F32, precision=prec)


def _nn(a, b):
    return _dot(a.astype(BF16), b.astype(BF16), ((1,), (0,)))


def _nt(a, b):
    return _dot(a.astype(BF16), b.astype(BF16), ((1,), (1,)))


def _tn(a, b):
    return _dot(a.astype(BF16), b.astype(BF16), ((0,), (0,)))


def _nn_hi(a, b):
    return _dot(a, b, ((1,), (0,)), HI)


def _nt_hi(a, b):
    return _dot(a, b, ((1,), (1,)), HI)


def _tn_hi(a, b):
    return _dot(a, b, ((0,), (0,)), HI)


def _sigmoid(x):
    return 0.5 * jnp.tanh(0.5 * x) + 0.5


def _softplus(x):
    return jnp.maximum(x, 0.0) + jnp.log(1.0 + jnp.exp(-jnp.abs(x)))


def _logsigmoid(x):
    return -_softplus(-x)


def _iota2(shape, dim):
    return lax.broadcasted_iota(jnp.int32, shape, dim)


def _mm(a, b, mode, *, tm, tn, tk, out_dtypes, extras=(), epilogue=None, name, vmem=VMEM_BIG, rider=None):
    if mode == "tn":
        K, M = a.shape
    else:
        M, K = a.shape
    N = b.shape[0] if mode == "nt" else b.shape[1]
    assert M % tm == 0 and N % tn == 0 and K % tk == 0, (name, M, N, K, tm, tn, tk)
    nk = K // tk
    n_ex, n_out = len(extras), len(out_dtypes)
    if mode == "tn":
        a_spec = pl.BlockSpec((tk, tm), lambda i, j, k: (k, i))
    else:
        a_spec = pl.BlockSpec((tm, tk), lambda i, j, k: (i, k))
    if mode == "nt":
        b_spec = pl.BlockSpec((tn, tk), lambda i, j, k: (j, k))
    else:
        b_spec = pl.BlockSpec((tk, tn), lambda i, j, k: (k, j))
    mn_spec = pl.BlockSpec((tm, tn), lambda i, j, k: (i, j))
    dims = {"nn": ((1,), (0,)), "nt": ((1,), (1,)), "tn": ((0,), (0,))}[mode]

    single = nk == 1
    direct = (not single) and epilogue is None and n_out == 1 and out_dtypes[0] == F32

    def body(*refs):
        a_ref, b_ref = refs[0], refs[1]
        ex_refs = refs[2:2 + n_ex]
        out_refs = refs[2 + n_ex:2 + n_ex + n_out]
        part = _dot(a_ref[...].astype(BF16), b_ref[...].astype(BF16), dims)

        def finish(acc):
            res = (acc,) if epilogue is None else epilogue(acc, *[e[...] for e in ex_refs])
            for o_ref, r in zip(out_refs, res):
                o_ref[...] = r.astype(o_ref.dtype)

        if single:
            finish(part)
            return
        acc_ref = out_refs[0] if direct else refs[2 + n_ex + n_out]
        k = pl.program_id(2)

        @pl.when(k == 0)
        def _():
            acc_ref[...] = part

        @pl.when(k > 0)
        def _():
            acc_ref[...] += part

        if not direct:
            @pl.when(k == nk - 1)
            def _():
                finish(acc_ref[...])

    outs, ridden = _hosted_call(
        body, rider, name=name, grid=(M // tm, N // tn, nk),
        in_specs=[a_spec, b_spec] + [mn_spec] * n_ex,
        out_specs=[mn_spec] * n_out,
        out_shape=[jax.ShapeDtypeStruct((M, N), dt) for dt in out_dtypes],
        scratch_shapes=[] if (single or direct) else [pltpu.VMEM((tm, tn), F32)],
        compiler_params=_cp(vmem, ("parallel", "parallel", "arbitrary")), args=(a, b, *extras))
    return tuple(outs) if rider is None else (tuple(outs), ridden)


def _grad_tn(a, secs, *, tk, name):
    kk, m = a.shape
    widths = [s.shape[1] for s in secs]
    total = sum(widths)
    nk = kk // tk

    def body(*refs):
        a_ref, sec_refs, o_ref = refs[0], refs[1:-1], refs[-1]
        cat = sec_refs[0][...] if len(sec_refs) == 1 else jnp.concatenate([s[...] for s in sec_refs], axis=1)
        part = _dot(a_ref[...].astype(BF16), cat.astype(BF16), ((0,), (0,)))
        k = pl.program_id(0)

        @pl.when(k == 0)
        def _():
            o_ref[...] = part

        @pl.when(k > 0)
        def _():
            o_ref[...] += part

    return pl.pallas_call(
        body, name=name, grid=(nk,),
        in_specs=[pl.BlockSpec((tk, m), lambda k: (k, 0))] + [pl.BlockSpec((tk, w), lambda k: (k, 0)) for w in widths],
        out_specs=pl.BlockSpec((m, total), lambda k: (0, 0)),
        out_shape=jax.ShapeDtypeStruct((m, total), F32),
        compiler_params=_cp(VMEM_BIG, ("arbitrary",)),
    )(a, *secs)


def _rms_fwd(x, g, *, tr, name):
    n, d = x.shape

    def body(x_ref, g_ref, o_ref):
        xv = x_ref[...]
        r = lax.rsqrt(jnp.mean(xv * xv, axis=-1, keepdims=True) + EPS)
        o_ref[...] = (xv * r * g_ref[...]).astype(o_ref.dtype)

    return pl.pallas_call(
        body, name=name, grid=(n // tr,),
        in_specs=[pl.BlockSpec((tr, d), lambda i: (i, 0)), pl.BlockSpec((1, d), lambda i: (0, 0))],
        out_specs=pl.BlockSpec((tr, d), lambda i: (i, 0)),
        out_shape=jax.ShapeDtypeStruct((n, d), BF16),
        compiler_params=_cp(VMEM_BIG),
    )(x, g)


def _rms_bwd_math(xv, g, dy):
    r = lax.rsqrt(jnp.mean(xv * xv, axis=-1, keepdims=True) + EPS)
    xh = xv * r
    gdy = dy * g
    dx = r * (gdy - xh * jnp.mean(xh * gdy, axis=-1, keepdims=True))
    return dx, jnp.sum(dy * xh, axis=0, keepdims=True)


def _rms_bwd_add(x, g, dy, res, *, tr, name):
    n, d = x.shape

    def body(x_ref, g_ref, dy_ref, res_ref, o_ref, ob_ref, dg_ref):
        dx, dg = _rms_bwd_math(x_ref[...], g_ref[...], dy_ref[...])
        tot = res_ref[...] + dx
        o_ref[...] = tot
        ob_ref[...] = tot.astype(BF16)

        @pl.when(pl.program_id(0) == 0)
        def _():
            dg_ref[...] = dg

        @pl.when(pl.program_id(0) > 0)
        def _():
            dg_ref[...] += dg

    row = pl.BlockSpec((tr, d), lambda i: (i, 0))
    vec = pl.BlockSpec((1, d), lambda i: (0, 0))
    return pl.pallas_call(
        body, name=name, grid=(n // tr,),
        in_specs=[row, vec, row, row], out_specs=[row, row, vec],
        out_shape=[jax.ShapeDtypeStruct((n, d), F32), jax.ShapeDtypeStruct((n, d), BF16),
                   jax.ShapeDtypeStruct((1, d), F32)],
        compiler_params=_cp(VMEM_BIG),
    )(x, g, dy, res)


def _final_loss(x2, gf, tgt, *, t_seq, tr, name):
    n, d = x2.shape
    per_seq = t_seq // tr

    def body(x_ref, g_ref, t_ref, dx_ref, dxb_ref, dg_ref, loss_ref):
        i = pl.program_id(0)
        xv = x_ref[...]
        g = g_ref[...]
        r = lax.rsqrt(jnp.mean(xv * xv, axis=-1, keepdims=True) + EPS)
        xh = xv * r
        pos = (i % per_seq) * tr + _iota2((tr, 1), 0)
        real = pos >= CHUNK
        err = jnp.where(real, xh * g - t_ref[...], 0.0)
        dy = err * (1.0 / d)
        gdy = dy * g
        dx = r * (gdy - xh * jnp.mean(xh * gdy, axis=-1, keepdims=True))
        dx_ref[...] = dx
        dxb_ref[...] = dx.astype(BF16)
        dg = jnp.sum(dy * xh, axis=0, keepdims=True)
        ls = 0.5 * jnp.sum(jnp.mean(err * err, axis=-1, keepdims=True), axis=0, keepdims=True)
        ls = jnp.where(_iota2((1, LANE), 1) == 0, ls, 0.0)

        @pl.when(i == 0)
        def _():
            dg_ref[...] = dg
            loss_ref[...] = ls

        @pl.when(i > 0)
        def _():
            dg_ref[...] += dg
            loss_ref[...] += ls

    row = pl.BlockSpec((tr, d), lambda i: (i, 0))
    vec = pl.BlockSpec((1, d), lambda i: (0, 0))
    one = pl.BlockSpec((1, LANE), lambda i: (0, 0))
    return pl.pallas_call(
        body, name=name, grid=(n // tr,),
        in_specs=[row, vec, row], out_specs=[row, row, vec, one],
        out_shape=[jax.ShapeDtypeStruct((n, d), F32), jax.ShapeDtypeStruct((n, d), BF16),
                   jax.ShapeDtypeStruct((1, d), F32), jax.ShapeDtypeStruct((1, LANE), F32)],
        compiler_params=_cp(VMEM_BIG),
    )(x2, gf, tgt)


def _gnorm_fwd(o_dn, o_gla, projp, g_dn, g_gla, *, tr, name):
    n = o_dn.shape[0]
    w = NH * DN_D

    def body(odn_ref, ogl_ref, z_ref, r_ref, gdn_ref, ggl_ref, mix_ref):
        for grp, (o_ref, gate_ref, gain_ref) in enumerate(((odn_ref, z_ref, gdn_ref), (ogl_ref, r_ref, ggl_ref))):
            gain = gain_ref[...]
            for h in range(NH):
                sl = slice(h * DN_D, (h + 1) * DN_D)
                o = o_ref[:, sl]
                z = gate_ref[:, sl]
                r = lax.rsqrt(jnp.mean(o * o, axis=-1, keepdims=True) + EPS)
                y = (o * r * gain) * (z * _sigmoid(z))
                mix_ref[:, grp * w + h * DN_D: grp * w + (h + 1) * DN_D] = y.astype(mix_ref.dtype)

    row = pl.BlockSpec((tr, w), lambda i: (i, 0))
    vec = pl.BlockSpec((1, DN_D), lambda i: (0, 0))
    return pl.pallas_call(
        body, name=name, grid=(n // tr,),
        in_specs=[row, row, pl.BlockSpec((tr, w), lambda i: (i, C_DZ // w)),
                  pl.BlockSpec((tr, w), lambda i: (i, C_GR // w)), vec, vec],
        out_specs=pl.BlockSpec((tr, 2 * w), lambda i: (i, 0)),
        out_shape=jax.ShapeDtypeStruct((n, 2 * w), BF16),
        compiler_params=_cp(VMEM_BIG),
    )(o_dn, o_gla, projp, projp, g_dn, g_gla)


def _gnorm_bwd(dmix, o_dn, o_gla, projp, g_dn, g_gla, *, tr, name):
    n = o_dn.shape[0]
    w = NH * DN_D

    def body(dm_ref, odn_ref, ogl_ref, z_ref, r_ref, gdn_ref, ggl_ref,
             dodn_ref, ddz_ref, dogl_ref, dgr_ref, dgdn_ref, dggl_ref):
        first = pl.program_id(0) == 0
        groups = ((odn_ref, z_ref, gdn_ref, dodn_ref, ddz_ref, dgdn_ref),
                  (ogl_ref, r_ref, ggl_ref, dogl_ref, dgr_ref, dggl_ref))
        for grp, (o_ref, gate_ref, gain_ref, do_ref, dgate_ref, dgain_ref) in enumerate(groups):
            gain = gain_ref[...]
            dgain = jnp.zeros((1, DN_D), F32)
            for h in range(NH):
                sl = slice(h * DN_D, (h + 1) * DN_D)
                o = o_ref[:, sl]
                z = gate_ref[:, sl]
                dm = dm_ref[:, grp * w + h * DN_D: grp * w + (h + 1) * DN_D]
                r = lax.rsqrt(jnp.mean(o * o, axis=-1, keepdims=True) + EPS)
                oh = o * r
                s = _sigmoid(z)
                dn = dm * (z * s)
                dgate_ref[:, sl] = (dm * (oh * gain) * (s * (1.0 + z * (1.0 - s)))).astype(dgate_ref.dtype)
                gdn = dn * gain
                do_ref[:, sl] = r * (gdn - oh * jnp.mean(oh * gdn, axis=-1, keepdims=True))
                dgain = dgain + jnp.sum(dn * oh, axis=0, keepdims=True)

            @pl.when(first)
            def _():
                dgain_ref[...] = dgain

            @pl.when(jnp.logical_not(first))
            def _():
                dgain_ref[...] += dgain

    row = pl.BlockSpec((tr, w), lambda i: (i, 0))
    vec = pl.BlockSpec((1, DN_D), lambda i: (0, 0))
    big = jax.ShapeDtypeStruct((n, w), F32)
    gate = jax.ShapeDtypeStruct((n, w), BF16)
    small = jax.ShapeDtypeStruct((1, DN_D), F32)
    return pl.pallas_call(
        body, name=name, grid=(n // tr,),
        in_specs=[pl.BlockSpec((tr, 2 * w), lambda i: (i, 0)), row, row,
                  pl.BlockSpec((tr, w), lambda i: (i, C_DZ // w)), pl.BlockSpec((tr, w), lambda i: (i, C_GR // w)), vec, vec],
        out_specs=[row, row, row, row, vec, vec],
        out_shape=[big, gate, big, gate, small, small],
        compiler_params=_cp(VMEM_BIG),
    )(dmix, o_dn, o_gla, projp, projp, g_dn, g_gla)


QKV_W = 3 * NH * DN_D
HALO = 8


def _conv_z(xs_ref, cw_ref, tt):
    z = cw_ref[0:1, :] * xs_ref[pl.ds(HALO - 3, tt), :]
    for j in range(1, 4):
        z = z + cw_ref[j:j + 1, :] * xs_ref[pl.ds(HALO - 3 + j, tt), :]
    return z


def _dnprep_fwd(projp, conv_w, *, bsz, t_seq, tt, name):
    n = bsz * t_seq
    per_seq = t_seq // tt
    hw = NH * DN_D

    def body(x_ref, halo_ref, cw_ref, q_ref, k_ref, v_ref, xs_ref):
        i = pl.program_id(1)
        xs_ref[0:HALO, :] = jnp.where(i == 0, 0.0, halo_ref[...])
        xs_ref[HALO:HALO + tt, :] = x_ref[...]
        z = _conv_z(xs_ref, cw_ref, tt)
        a = z * _sigmoid(z)
        for grp, o_ref in enumerate((q_ref, k_ref)):
            for h in range(NH):
                ah = a[:, grp * hw + h * DN_D: grp * hw + (h + 1) * DN_D]
                rs = lax.rsqrt(jnp.sum(ah * ah, axis=-1, keepdims=True) + EPS)
                o_ref[:, h * DN_D:(h + 1) * DN_D] = ah * rs
        v_ref[...] = a[:, 2 * hw:3 * hw]

    def halo_map(b, i):
        return (jnp.maximum((b * t_seq + i * tt) // HALO - 1, 0), 0)

    out = pl.BlockSpec((tt, hw), lambda b, i: (b * per_seq + i, 0))
    sds = jax.ShapeDtypeStruct((n, hw), F32)
    return pl.pallas_call(
        body, name=name, grid=(bsz, per_seq),
        in_specs=[pl.BlockSpec((tt, QKV_W), lambda b, i: (b * per_seq + i, 0)),
                  pl.BlockSpec((HALO, QKV_W), halo_map),
                  pl.BlockSpec((4, QKV_W), lambda b, i: (0, 0))],
        out_specs=[out, out, out], out_shape=[sds, sds, sds],
        scratch_shapes=[pltpu.VMEM((tt + HALO, QKV_W), F32)],
        compiler_params=_cp(VMEM_BIG),
    )(projp, projp, conv_w)


def _dnprep_bwd_a(projp, conv_w, dq, dk, dv, *, bsz, t_seq, tt, name):
    n = bsz * t_seq
    per_seq = t_seq // tt
    hw = NH * DN_D

    def body(x_ref, halo_ref, cw_ref, dq_ref, dk_ref, dv_ref, dz_ref, dcw_ref, xs_ref):
        b, i = pl.program_id(0), pl.program_id(1)
        xs_ref[0:HALO, :] = jnp.where(i == 0, 0.0, halo_ref[...])
        xs_ref[HALO:HALO + tt, :] = x_ref[...]
        z = _conv_z(xs_ref, cw_ref, tt)
        s = _sigmoid(z)
        a = z * s
        dsilu = s * (1.0 + z * (1.0 - s))
        for grp, d_ref in enumerate((dq_ref, dk_ref)):
            for h in range(NH):
                sl = slice(grp * hw + h * DN_D, grp * hw + (h + 1) * DN_D)
                ah = a[:, sl]
                rs = lax.rsqrt(jnp.sum(ah * ah, axis=-1, keepdims=True) + EPS)
                y = ah * rs
                dy = d_ref[:, h * DN_D:(h + 1) * DN_D]
                da = rs * (dy - y * jnp.sum(dy * y, axis=-1, keepdims=True))
                dz_ref[:, sl] = da * dsilu[:, sl]
        dz_ref[:, 2 * hw:3 * hw] = dv_ref[...] * dsilu[:, 2 * hw:3 * hw]
        dz = dz_ref[...]
        first = jnp.logical_and(b == 0, i == 0)
        for j in range(4):
            part = jnp.sum(dz * xs_ref[pl.ds(HALO - 3 + j, tt), :], axis=0, keepdims=True)

            @pl.when(first)
            def _():
                dcw_ref[j:j + 1, :] = part

            @pl.when(jnp.logical_not(first))
            def _():
                dcw_ref[j:j + 1, :] += part

    def halo_map(b, i):
        return (jnp.maximum((b * t_seq + i * tt) // HALO - 1, 0), 0)

    hrow = pl.BlockSpec((tt, hw), lambda b, i: (b * per_seq + i, 0))
    return pl.pallas_call(
        body, name=name, grid=(bsz, per_seq),
        in_specs=[pl.BlockSpec((tt, QKV_W), lambda b, i: (b * per_seq + i, 0)),
                  pl.BlockSpec((HALO, QKV_W), halo_map),
                  pl.BlockSpec((4, QKV_W), lambda b, i: (0, 0)), hrow, hrow, hrow],
        out_specs=[pl.BlockSpec((tt, QKV_W), lambda b, i: (b * per_seq + i, 0)),
                   pl.BlockSpec((4, QKV_W), lambda b, i: (0, 0))],
        out_shape=[jax.ShapeDtypeStruct((n, QKV_W), F32), jax.ShapeDtypeStruct((4, QKV_W), F32)],
        scratch_shapes=[pltpu.VMEM((tt + HALO, QKV_W), F32)],
        compiler_params=_cp(VMEM_BIG),
    )(projp, projp, conv_w, dq, dk, dv)


def _dnprep_bwd_b(dz, conv_w, *, bsz, t_seq, tt, name):
    n = bsz * t_seq
    per_seq = t_seq // tt
    last_blk = n // HALO - 1

    def body(dz_ref, halo_ref, cw_ref, dx_ref, ds_ref):
        i = pl.program_id(1)
        ds_ref[0:tt, :] = dz_ref[...]
        ds_ref[tt:tt + HALO, :] = jnp.where(i == per_seq - 1, 0.0, halo_ref[...])
        dx = cw_ref[0:1, :] * ds_ref[pl.ds(3, tt), :]
        for j in range(1, 4):
            dx = dx + cw_ref[j:j + 1, :] * ds_ref[pl.ds(3 - j, tt), :]
        dx_ref[...] = dx.astype(dx_ref.dtype)

    def halo_map(b, i):
        return (jnp.minimum((b * t_seq + (i + 1) * tt) // HALO, last_blk), 0)

    row = pl.BlockSpec((tt, QKV_W), lambda b, i: (b * per_seq + i, 0))
    return pl.pallas_call(
        body, name=name, grid=(bsz, per_seq),
        in_specs=[row, pl.BlockSpec((HALO, QKV_W), halo_map), pl.BlockSpec((4, QKV_W), lambda b, i: (0, 0))],
        out_specs=row, out_shape=jax.ShapeDtypeStruct((n, QKV_W), BF16),
        scratch_shapes=[pltpu.VMEM((tt + HALO, QKV_W), F32)],
        compiler_params=_cp(VMEM_BIG),
    )(dz, dz, conv_w)


def _masks64():
    r = _iota2((CHUNK, CHUNK), 0)
    c = _iota2((CHUNK, CHUNK), 1)
    return r, c


def _group(nc_seq, target=5):
    return max(g for g in range(1, target + 1) if nc_seq % g == 0)


def _round_robin(chains):
    live = list(chains)
    while live:
        nxt = []
        for ch in live:
            try:
                next(ch)
                nxt.append(ch)
            except StopIteration:
                pass
        live = nxt
        yield


def _run(chains):
    for _ in _round_robin(chains):
        pass


def _per_chunk(inner, kinds, grp):
    def body(*refs):
        chains = []
        for gi in range(grp):
            views = []
            for r, kind in zip(refs, kinds):
                if kind == "row":
                    views.append(r.at[pl.ds(gi * CHUNK, CHUNK)])
                elif kind == "lead":
                    views.append(r.at[pl.ds(gi, 1)])
                else:
                    views.append(r)
            chains.append(inner(gi, *views))
        _run(chains)
    return body


def _accumulate(ref, val, gi):
    if gi > 0:
        ref[...] += val
        return
    first = pl.program_id(0) == 0

    @pl.when(first)
    def _():
        ref[...] = val

    @pl.when(jnp.logical_not(first))
    def _():
        ref[...] += val


ANY = pl.BlockSpec(memory_space=pl.ANY)


def _place():
    return lax.axis_index("x"), lax.axis_index("y"), lax.axis_index("c")


def _other_chips(x, y):
    return [(1 - x, y, 2 * (1 - x) + y), (x, 1 - y, 2 * x + 1 - y), (1 - x, 1 - y, 2 * (1 - x) + 1 - y)]


class _GatherRider:
    def __init__(self, bufs, split):
        self.inputs = list(bufs)
        self.split = list(split)
        self.out_shapes = [jax.ShapeDtypeStruct(b.shape, b.dtype) for b in bufs]
        self.aliases = {i: i for i in range(len(bufs))}
        self.sems = [pltpu.SemaphoreType.DMA((len(bufs), 3))] * 4

    def _rows(self, k, buf, c, mine=True):
        r = buf.shape[1]
        if not self.split[k]:
            return pl.ds(0, r)
        return pl.ds((c if mine else 1 - c) * (r // 2), r // 2)

    def _ici(self, k, d, bufs, sems, c, px, py, block):
        rows = self._rows(k, bufs[k], c)
        return pltpu.make_async_remote_copy(
            src_ref=bufs[k].at[block, rows, :], dst_ref=bufs[k].at[block, rows, :], send_sem=sems[0].at[k, d],
            recv_sem=sems[1].at[k, d], device_id=(px, py, c), device_id_type=MESH)

    def _pass(self, k, d, bufs, sems, x, y, c, block, mine):
        rows = self._rows(k, bufs[k], c, mine)
        return pltpu.make_async_remote_copy(
            src_ref=bufs[k].at[block, rows, :], dst_ref=bufs[k].at[block, rows, :], send_sem=sems[2].at[k, d],
            recv_sem=sems[3].at[k, d], device_id=(x, y, 1 - c), device_id_type=MESH)

    def first(self, in_refs, bufs, sems):
        x, y, c = _place()
        for k in range(len(bufs)):
            for d, (px, py, _) in enumerate(_other_chips(x, y)):
                self._ici(k, d, bufs, sems, c, px, py, 2 * x + y).start()

    def last(self, in_refs, bufs, sems):
        x, y, c = _place()
        chips = _other_chips(x, y)
        for k in range(len(bufs)):
            for d, (px, py, pj) in enumerate(chips):
                self._ici(k, d, bufs, sems, c, px, py, pj).wait_recv()
                if self.split[k]:
                    self._pass(k, d, bufs, sems, x, y, c, pj, True).start()
        for k in range(len(bufs)):
            for d, (px, py, pj) in enumerate(chips):
                if self.split[k]:
                    self._pass(k, d, bufs, sems, x, y, c, pj, False).wait_recv()
                    self._pass(k, d, bufs, sems, x, y, c, pj, True).wait_send()
                self._ici(k, d, bufs, sems, c, px, py, 2 * x + y).wait_send()


def _hosted_call(body, rider, *, name, grid, in_specs, out_specs, out_shape, scratch_shapes, compiler_params, args):
    if rider is None:
        outs = pl.pallas_call(body, name=name, grid=grid, in_specs=in_specs, out_specs=out_specs, out_shape=out_shape,
                              scratch_shapes=scratch_shapes, compiler_params=compiler_params)(*args)
        return list(outs), []
    n_in, n_out, n_scr = len(in_specs), len(out_specs), len(scratch_shapes)
    r_in, r_out = len(rider.inputs), len(rider.out_shapes)
    compiler_params = _cp(compiler_params.vmem_limit_bytes, ("arbitrary",) * len(grid))

    def full_body(*refs):
        ins = refs[:n_in]
        rins = refs[n_in:n_in + r_in]
        outs = refs[n_in + r_in:n_in + r_in + n_out]
        routs = refs[n_in + r_in + n_out:n_in + r_in + n_out + r_out]
        rest = refs[n_in + r_in + n_out + r_out:]
        scr, sems = rest[:n_scr], rest[n_scr:]
        ids = [pl.program_id(a) for a in range(len(grid))]
        is_first = functools.reduce(jnp.logical_and, [i == 0 for i in ids])
        is_last = functools.reduce(jnp.logical_and, [i == g - 1 for i, g in zip(ids, grid)])

        @pl.when(is_first)
        def _():
            rider.first(rins, routs, sems)

        body(*ins, *outs, *scr)

        @pl.when(is_last)
        def _():
            rider.last(rins, routs, sems)

    res = pl.pallas_call(
        full_body, name=name, grid=grid, in_specs=list(in_specs) + [ANY] * r_in,
        out_specs=list(out_specs) + [ANY] * r_out, out_shape=list(out_shape) + list(rider.out_shapes),
        input_output_aliases={n_in + i: n_out + o for i, o in rider.aliases.items()},
        scratch_shapes=list(scratch_shapes) + list(rider.sems), compiler_params=compiler_params,
    )(*args, *rider.inputs)
    return list(res[:n_out]), list(res[n_out:])


def _exchange_now(rider, *, name):
    r_in = len(rider.inputs)

    def body(*refs):
        rins = refs[:r_in]
        routs = refs[r_in:r_in + len(rider.out_shapes)]
        sems = refs[r_in + len(rider.out_shapes):]
        rider.first(rins, routs, sems)
        rider.last(rins, routs, sems)

    return pl.pallas_call(
        body, name=name, in_specs=[ANY] * r_in, out_specs=[ANY] * len(rider.out_shapes), out_shape=list(rider.out_shapes),
        input_output_aliases=dict(rider.aliases), scratch_shapes=list(rider.sems),
    )(*rider.inputs)


def _to_slot(where, a, dtype, *, name):
    r, cols = a.shape
    tr = _tile(r, 256, 16) if r > 256 else r

    def body(w_ref, a_ref, o_ref):
        o_ref[0] = a_ref[...].astype(o_ref.dtype)

    return pl.pallas_call(
        body, name=name,
        grid_spec=pltpu.PrefetchScalarGridSpec(
            num_scalar_prefetch=1, grid=(r // tr,),
            in_specs=[pl.BlockSpec((tr, cols), lambda i, w: (i, 0))],
            out_specs=pl.BlockSpec((1, tr, cols), lambda i, w: (w[1], i, 0))),
        out_shape=jax.ShapeDtypeStruct((N_CHIPS, r, cols), dtype), compiler_params=_cp(VMEM_BIG),
    )(where, a)


def _tri_inv(a_strict):
    r, c = _masks64()
    eye = (r == c).astype(F32)
    blk16 = (r // 16) == (c // 16)
    blk32 = (r // 32) == (c // 32)
    ld = jnp.where(blk16, a_strict, 0.0)
    x = eye - ld
    p = _nn(ld, ld)
    yield
    for step in range(3):
        xp = _nn(x, p)
        if step < 2:
            p = _nn(p, p)
        x = x + xp
        yield
    for lk in (jnp.where(jnp.logical_and(blk32, jnp.logical_not(blk16)), a_strict, 0.0),
               jnp.where(blk32, 0.0, a_strict)):
        y = x - eye
        s = lk + _nn(y, lk)
        yield
        x = x - s - _nn(s, y)
        yield
    return x


def _dn_gates(sa, alog, dtb, chunk_in_seq):
    rows = _iota2((CHUNK, LANE), 0)
    valid = jnp.logical_or(rows >= N_PAD, chunk_in_seq > 0)
    beta_t = _sigmoid(sa)
    ea = jnp.exp(alog)
    g_t = jnp.where(valid, -ea * _softplus(sa + dtb), 0.0)
    r, c = _masks64()
    ltri = (r >= c).astype(F32)
    gam_t = _nn_hi(ltri, g_t)
    return beta_t, g_t, gam_t, valid, ea


def _dn_intra_fwd(qn, kn, v, projp, alog_row, dtb_row, *, nc_seq, name):
    n = qn.shape[0]
    nct = n // CHUNK
    hw = NH * DN_D
    scale = DN_D ** -0.5

    grp = _group(nc_seq)

    def inner(gi, q_ref, k_ref, v_ref, sa_ref, al_ref, dt_ref, u_ref, w_ref, qg_ref, kd_ref, p_ref, t_ref, gl_ref):
        ci = (pl.program_id(0) * grp + gi) % nc_seq
        beta_t, _, gam_t, _, _ = _dn_gates(sa_ref[...], al_ref[...], dt_ref[...], ci)
        yield
        gam_tt = gam_t.T
        r, c = _masks64()
        incl = r >= c
        strict = r > c

        def head(h):
            sl = slice(h * DN_D, (h + 1) * DN_D)
            beta = beta_t[:, h:h + 1]
            gam = gam_t[:, 4 + h:5 + h]
            gam_row = gam_tt[4 + h:5 + h, :]
            gl = gam_t[CHUNK - 1:CHUNK, 4 + h:5 + h]
            dec = jnp.exp(jnp.where(incl, gam - gam_row, -jnp.inf))
            kh = k_ref[:, sl]
            qh = q_ref[:, sl] * scale
            vh = v_ref[:, sl]
            kk = _nt(kh, kh)
            qk = _nt(qh, kh)
            yield
            a = jnp.where(strict, beta * kk * dec, 0.0)
            tm = yield from _tri_inv(a)
            egam = jnp.exp(gam)
            u_ref[:, sl] = _nn(tm, beta * vh)
            w_ref[:, sl] = _nn(tm, (beta * egam) * kh)
            qg_ref[:, sl] = egam * qh
            kd_ref[:, sl] = jnp.exp(gl - gam) * kh
            p_ref[0, h] = qk * dec
            t_ref[0, h] = tm
            gl_ref[0, h:h + 1, :] = jnp.broadcast_to(jnp.exp(gl), (1, LANE))

        yield from _round_robin([head(h) for h in range(NH)])

    rows = grp * CHUNK
    row = pl.BlockSpec((rows, hw), lambda i: (i, 0))
    vec = pl.BlockSpec((1, LANE), lambda i: (0, 0))
    mat = pl.BlockSpec((grp, NH, CHUNK, CHUNK), lambda i: (i, 0, 0, 0))
    big = jax.ShapeDtypeStruct((n, hw), F32)
    msd = jax.ShapeDtypeStruct((nct, NH, CHUNK, CHUNK), F32)
    kinds = ["row"] * 4 + ["whole"] * 2 + ["row"] * 4 + ["lead"] * 3
    return pl.pallas_call(
        _per_chunk(inner, kinds, grp), name=name, grid=(nct // grp,),
        in_specs=[row, row, row, pl.BlockSpec((rows, LANE), lambda i: (i, C_SA // LANE)), vec, vec],
        out_specs=[row, row, row, row, mat, mat, pl.BlockSpec((grp, NH, LANE), lambda i: (i, 0, 0))],
        out_shape=[big, big, big, big, msd, msd, jax.ShapeDtypeStruct((nct, NH, LANE), F32)],
        compiler_params=_cp(VMEM_BIG),
    )(qn, kn, v, projp, alog_row, dtb_row)


def _dn_scan_fwd(u, w, qg, kd, p, gl, *, bsz, nc_seq, name, rider=None):
    hw = NH * DN_D
    t_seq = nc_seq * CHUNK
    u, w, qg, kd = (z.reshape(bsz, t_seq, hw) for z in (u, w, qg, kd))
    p = p.reshape(bsz, nc_seq, NH, CHUNK, CHUNK)
    gl = gl.reshape(bsz, nc_seq, NH, LANE)

    def body(u_ref, w_ref, qg_ref, kd_ref, p_ref, gl_ref, o_ref, vn_ref, hist_ref, s_ref):
        @pl.when(pl.program_id(0) == 0)
        def _():
            s_ref[...] = jnp.zeros_like(s_ref)

        def chain(b, h):
            sl = slice(h * DN_D, (h + 1) * DN_D)
            s = s_ref[b, h]
            hist_ref[b, 0, h] = s
            ws = _nn(w_ref[b, :, sl], s)
            qs = _nn(qg_ref[b, :, sl], s)
            yield
            vn = u_ref[b, :, sl] - ws
            vn_ref[b, :, sl] = vn
            o_ref[b, :, sl] = qs + _nn(p_ref[b, 0, h], vn)
            s_ref[b, h] = gl_ref[b, 0, h:h + 1, :] * s + _tn(kd_ref[b, :, sl], vn)

        _run([chain(b, h) for b in range(bsz) for h in range(NH)])

    row = pl.BlockSpec((bsz, CHUNK, hw), lambda i: (0, i, 0))
    outs, ridden = _hosted_call(
        body, rider, name=name, grid=(nc_seq,),
        in_specs=[row, row, row, row, pl.BlockSpec((bsz, 1, NH, CHUNK, CHUNK), lambda i: (0, i, 0, 0, 0)),
                  pl.BlockSpec((bsz, 1, NH, LANE), lambda i: (0, i, 0, 0))],
        out_specs=[row, row, pl.BlockSpec((bsz, 1, NH, DN_D, DN_D), lambda i: (0, i, 0, 0, 0))],
        out_shape=[jax.ShapeDtypeStruct((bsz, t_seq, hw), F32), jax.ShapeDtypeStruct((bsz, t_seq, hw), F32),
                   jax.ShapeDtypeStruct((bsz, nc_seq, NH, DN_D, DN_D), F32)],
        scratch_shapes=[pltpu.VMEM((bsz, NH, DN_D, DN_D), F32)],
        compiler_params=_cp(VMEM_BIG, ("arbitrary",)), args=(u, w, qg, kd, p, gl))
    o, vn, hist = outs
    return o.reshape(bsz * t_seq, hw), vn.reshape(bsz * t_seq, hw), hist, ridden


def _dn_scan_bwd(do, w, qg, kd, vn, p, gl, hist, *, bsz, nc_seq, name):
    hw = NH * DN_D
    t_seq = nc_seq * CHUNK
    do, w, qg, kd, vn = (z.reshape(bsz, t_seq, hw) for z in (do, w, qg, kd, vn))
    p = p.reshape(bsz, nc_seq, NH, CHUNK, CHUNK)
    gl = gl.reshape(bsz, nc_seq, NH, LANE)

    def body(do_ref, w_ref, qg_ref, kd_ref, vn_ref, p_ref, gl_ref, hist_ref,
             du_ref, dw_ref, dqg_ref, dkd_ref, dgl_ref, ds_ref):
        @pl.when(pl.program_id(0) == 0)
        def _():
            ds_ref[...] = jnp.zeros_like(ds_ref)

        def chain(b, h):
            sl = slice(h * DN_D, (h + 1) * DN_D)
            s = hist_ref[b, 0, h]
            dsn = ds_ref[b, h]
            doh = do_ref[b, :, sl]
            vnh = vn_ref[b, :, sl]
            kdh = kd_ref[b, :, sl]
            dvn = _tn(p_ref[b, 0, h], doh) + _nn(kdh, dsn)
            du_ref[b, :, sl] = dvn
            dqg_ref[b, :, sl] = _nt(doh, s)
            dkd_ref[b, :, sl] = _nt(vnh, dsn)
            ds_part = _tn(qg_ref[b, :, sl], doh) + gl_ref[b, 0, h:h + 1, :] * dsn
            dgl = jnp.sum(jnp.sum(dsn * s, axis=0, keepdims=True), axis=1, keepdims=True)
            dgl_ref[b, 0, h:h + 1, :] = jnp.broadcast_to(dgl, (1, LANE))
            yield
            dw_ref[b, :, sl] = -_nt(dvn, s)
            ds_ref[b, h] = ds_part - _tn(w_ref[b, :, sl], dvn)

        _run([chain(b, h) for b in range(bsz) for h in range(NH)])

    rev = lambda i: nc_seq - 1 - i
    row = pl.BlockSpec((bsz, CHUNK, hw), lambda i: (0, rev(i), 0))
    mat = pl.BlockSpec((bsz, 1, NH, CHUNK, CHUNK), lambda i: (0, rev(i), 0, 0, 0))
    glb = pl.BlockSpec((bsz, 1, NH, LANE), lambda i: (0, rev(i), 0, 0))
    big = jax.ShapeDtypeStruct((bsz, t_seq, hw), F32)
    outs = pl.pallas_call(
        body, name=name, grid=(nc_seq,),
        in_specs=[row, row, row, row, row, mat, glb,
                  pl.BlockSpec((bsz, 1, NH, DN_D, DN_D), lambda i: (0, rev(i), 0, 0, 0))],
        out_specs=[row, row, row, row, glb],
        out_shape=[big, big, big, big, jax.ShapeDtypeStruct((bsz, nc_seq, NH, LANE), F32)],
        scratch_shapes=[pltpu.VMEM((bsz, NH, DN_D, DN_D), F32)],
        compiler_params=_cp(VMEM_BIG, ("arbitrary",)),
    )(do, w, qg, kd, vn, p, gl, hist)
    du, dw, dqg, dkd, dgl = outs
    n = bsz * t_seq
    return (du.reshape(n, hw), dw.reshape(n, hw), dqg.reshape(n, hw), dkd.reshape(n, hw),
            dgl.reshape(bsz * nc_seq, NH, LANE))


def _dn_intra_bwd(qn, kn, v, projp, alog_row, dtb_row, u, w, tmat, du, dw, dqg, dkd, do, vn, dgl, *, nc_seq, name,
                  rider=None):
    n = qn.shape[0]
    nct = n // CHUNK
    hw = NH * DN_D
    scale = DN_D ** -0.5

    grp = _group(nc_seq)

    def inner(gi, q_ref, k_ref, v_ref, sa_ref, al_ref, dt_ref, u_ref, w_ref, t_ref, du_ref, dw_ref, dqg_ref, dkd_ref,
              do_ref, vn_ref, dgl_ref, dq_ref, dk_ref, dv_ref, dsa_ref, dal_ref, ddt_ref):
        ci = (pl.program_id(0) * grp + gi) % nc_seq
        sa = sa_ref[...]
        beta_t, g_t, gam_t, valid, ea = _dn_gates(sa, al_ref[...], dt_ref[...], ci)
        yield
        lane = _iota2((CHUNK, LANE), 1)
        gates_t = jnp.where(lane < 4, beta_t, gam_t).T
        r, c = _masks64()
        incl, strict, upper, supper = r >= c, r > c, r <= c, r < c
        rows1 = _iota2((CHUNK, 1), 0)
        acc = [jnp.zeros((CHUNK, LANE), F32)]

        def head(h):
            sl = slice(h * DN_D, (h + 1) * DN_D)
            beta = beta_t[:, h:h + 1]
            gam = gam_t[:, 4 + h:5 + h]
            beta_row = gates_t[h:h + 1, :]
            gam_row = gates_t[4 + h:5 + h, :]
            gl = gam_t[CHUNK - 1:CHUNK, 4 + h:5 + h]
            dec = jnp.exp(jnp.where(incl, gam - gam_row, -jnp.inf))
            dec_t = jnp.exp(jnp.where(upper, gam_row - gam, -jnp.inf))
            kh = k_ref[:, sl]
            qh = q_ref[:, sl] * scale
            vh = v_ref[:, sl]
            uh = u_ref[:, sl]
            wh = w_ref[:, sl]
            doh = do_ref[:, sl]
            vnh = vn_ref[:, sl]
            kk = _nt(kh, kh)
            qk = _nt(qh, kh)
            qk_t = _nt(kh, qh)
            dp = _nt(doh, vnh)
            dp_t = _nt(vnh, doh)
            tm_t = t_ref[0, h].T
            dvb = _nn(tm_t, du_ref[:, sl])
            dkg = _nn(tm_t, dw_ref[:, sl])
            yield
            m = _nt(dvb, uh) + _nt(dkg, wh)
            m_t = _nt(uh, dvb) + _nt(wh, dkg)
            yield
            da = jnp.where(strict, -m, 0.0)
            da_t = jnp.where(supper, -m_t, 0.0)
            a = jnp.where(strict, beta * kk * dec, 0.0)
            a_t = jnp.where(supper, beta_row * kk * dec_t, 0.0)
            dad = da * dec
            dad_t = da_t * dec_t
            dbeta = jnp.sum(dad * kk, axis=1, keepdims=True)
            dpm = jnp.where(incl, dp, 0.0)
            dpm_t = jnp.where(upper, dp_t, 0.0)
            e = da * a + dpm * (qk * dec)
            e_t = da_t * a_t + dpm_t * (qk_t * dec_t)
            dgam = jnp.sum(e, axis=1, keepdims=True) - jnp.sum(e_t, axis=1, keepdims=True)
            egam = jnp.exp(gam)
            ekd = jnp.exp(gl - gam)
            dqgh = dqg_ref[:, sl]
            dkdh = dkd_ref[:, sl]
            dkh = (_nn(beta * dad, kh) + _nn(beta_row * dad_t, kh) + _nn(dpm_t * dec_t, qh)
                   + (beta * egam) * dkg + ekd * dkdh)
            dqh = _nn(dpm * dec, kh) + egam * dqgh
            dbeta = dbeta + jnp.sum(dkg * (egam * kh), axis=1, keepdims=True) + jnp.sum(dvb * vh, axis=1, keepdims=True)
            rkd = jnp.sum(dkdh * (ekd * kh), axis=1, keepdims=True)
            dgam = (dgam + jnp.sum(dkg * ((beta * egam) * kh), axis=1, keepdims=True)
                    + jnp.sum(dqgh * (egam * qh), axis=1, keepdims=True) - rkd)
            dgam_last = jnp.sum(rkd, axis=0, keepdims=True) + dgl_ref[0, h:h + 1, 0:1] * jnp.exp(gl)
            dgam = dgam + jnp.where(rows1 == CHUNK - 1, dgam_last, 0.0)
            dq_ref[:, sl] = dqh * scale
            dk_ref[:, sl] = dkh
            dv_ref[:, sl] = beta * dvb
            acc[0] = acc[0] + jnp.where(lane == h, dbeta, 0.0) + jnp.where(lane == 4 + h, dgam, 0.0)

        yield from _round_robin([head(h) for h in range(NH)])
        acc_t = acc[0]
        dg_t = _nn_hi(upper.astype(F32), acc_t)
        ddb = acc_t * beta_t * (1.0 - beta_t)
        dda = jnp.where(valid, dg_t * (-ea) * _sigmoid(sa + dt_ref[...]), 0.0)
        dsa_ref[...] = jnp.where(lane < 4, ddb, jnp.where(lane < 8, dda, 0.0)).astype(dsa_ref.dtype)
        in_g = jnp.logical_and(lane >= 4, lane < 8)
        dal = jnp.sum(jnp.where(in_g, dg_t * g_t, 0.0), axis=0, keepdims=True)
        ddt = jnp.sum(jnp.where(in_g, dda, 0.0), axis=0, keepdims=True)
        _accumulate(dal_ref, dal, gi)
        _accumulate(ddt_ref, ddt, gi)

    rows = grp * CHUNK
    row = pl.BlockSpec((rows, hw), lambda i: (i, 0))
    vec = pl.BlockSpec((1, LANE), lambda i: (0, 0))
    mat = pl.BlockSpec((grp, NH, CHUNK, CHUNK), lambda i: (i, 0, 0, 0))
    glb = pl.BlockSpec((grp, NH, LANE), lambda i: (i, 0, 0))
    big = jax.ShapeDtypeStruct((n, hw), F32)
    v128 = jax.ShapeDtypeStruct((1, LANE), F32)
    kinds = (["row"] * 4 + ["whole"] * 2 + ["row"] * 2 + ["lead"] + ["row"] * 6 + ["lead"]
             + ["row"] * 4 + ["whole"] * 2)
    outs, ridden = _hosted_call(
        _per_chunk(inner, kinds, grp), rider, name=name, grid=(nct // grp,),
        in_specs=[row, row, row, pl.BlockSpec((rows, LANE), lambda i: (i, C_SA // LANE)), vec, vec,
                  row, row, mat, row, row, row, row, row, row, glb],
        out_specs=[row, row, row, pl.BlockSpec((rows, LANE), lambda i: (i, 0)), vec, vec],
        out_shape=[big, big, big, jax.ShapeDtypeStruct((n, LANE), BF16), v128, v128],
        scratch_shapes=[], compiler_params=_cp(VMEM_BIG, ("arbitrary",)),
        args=(qn, kn, v, projp, alog_row, dtb_row, u, w, tmat, du, dw, dqg, dkd, do, vn, dgl))
    return (*outs, ridden)


GQ_W = NH * GLA_DK
GV_W = NH * GLA_DV
GLA_NORM = 16.0
MID = CHUNK // 2


def _gla_gates(sb, w2p, gb, chunk_in_seq):
    rows = _iota2((CHUNK, GQ_W), 0)
    valid = jnp.logical_or(rows >= N_PAD, chunk_in_seq > 0)
    graw = _nn_hi(sb, w2p) + gb
    yield
    g = jnp.where(valid, _logsigmoid(graw) * (1.0 / GLA_NORM), 0.0)
    r, c = _masks64()
    bcum = _nn_hi((r >= c).astype(F32), g)
    yield
    return graw, bcum, valid


def _head_mask(h):
    lane = _iota2((1, GQ_W), 1)
    return jnp.logical_and(lane >= h * GLA_DK, lane < (h + 1) * GLA_DK)


def _gla_intra_fwd(projp, w2p, gb, *, nc_seq, name):
    n = projp.shape[0]
    nct = n // CHUNK
    scale = GLA_DK ** -0.5

    grp = _group(nc_seq)
    rows = grp * CHUNK

    def inner(gi, qk_ref, v_ref, sb_ref, w2_ref, gb_ref, oi_ref, qg_ref, kd_ref, gl_ref):
        ci = (pl.program_id(0) * grp + gi) % nc_seq
        _, bc, _ = yield from _gla_gates(sb_ref[...], w2_ref[...], gb_ref[...], ci)
        bref = bc[MID:MID + 1, :]
        bl = bc[CHUNK - 1:CHUNK, :]
        q = qk_ref[:, 0:GQ_W] * scale
        k = qk_ref[:, GQ_W:2 * GQ_W]
        qi = q * jnp.exp(bc - bref)
        ki = k * jnp.exp(bref - bc)
        qg_ref[...] = q * jnp.exp(bc)
        kd_ref[...] = k * jnp.exp(bl - bc)
        gl_ref[0] = jnp.exp(bl)
        r, c = _masks64()
        incl = r >= c
        a = [jnp.where(incl, _nt(jnp.where(_head_mask(h), qi, 0.0), ki), 0.0) for h in range(NH)]
        yield
        for h in range(NH):
            oi_ref[:, h * GLA_DV:(h + 1) * GLA_DV] = _nn(a[h], v_ref[:, h * GLA_DV:(h + 1) * GLA_DV])

    kinds = ["row"] * 3 + ["whole"] * 2 + ["row"] * 3 + ["lead"]
    return pl.pallas_call(
        _per_chunk(inner, kinds, grp), name=name, grid=(nct // grp,),
        in_specs=[pl.BlockSpec((rows, 2 * GQ_W), lambda i: (i, C_GQK // (2 * GQ_W))),
                  pl.BlockSpec((rows, GV_W), lambda i: (i, C_GV // GV_W)),
                  pl.BlockSpec((rows, LANE), lambda i: (i, C_SB // LANE)),
                  pl.BlockSpec((LANE, GQ_W), lambda i: (0, 0)), pl.BlockSpec((1, GQ_W), lambda i: (0, 0))],
        out_specs=[pl.BlockSpec((rows, GV_W), lambda i: (i, 0)), pl.BlockSpec((rows, GQ_W), lambda i: (i, 0)),
                   pl.BlockSpec((rows, GQ_W), lambda i: (i, 0)), pl.BlockSpec((grp, 1, GQ_W), lambda i: (i, 0, 0))],
        out_shape=[jax.ShapeDtypeStruct((n, GV_W), F32), jax.ShapeDtypeStruct((n, GQ_W), F32),
                   jax.ShapeDtypeStruct((n, GQ_W), F32), jax.ShapeDtypeStruct((nct, 1, GQ_W), F32)],
        compiler_params=_cp(VMEM_BIG),
    )(projp, projp, projp, w2p, gb)


def _gla_scan_fwd(oi, qg, kd, gl, projp, *, bsz, nc_seq, name, rider=None):
    t_seq = nc_seq * CHUNK
    oi = oi.reshape(bsz, t_seq, GV_W)
    qg, kd = qg.reshape(bsz, t_seq, GQ_W), kd.reshape(bsz, t_seq, GQ_W)
    gl = gl.reshape(bsz, nc_seq, 1, GQ_W)
    pj = projp.reshape(bsz, t_seq, PW)

    def body(oi_ref, qg_ref, kd_ref, gl_ref, v_ref, o_ref, hist_ref, st_ref):
        @pl.when(pl.program_id(0) == 0)
        def _():
            st_ref[...] = jnp.zeros_like(st_ref)

        for b in range(bsz):
            st = st_ref[b]
            hist_ref[b, 0] = st
            qgb = qg_ref[b]
            kdb = kd_ref[b]
            upd = jnp.zeros((GLA_DV, GQ_W), F32)
            for h in range(NH):
                sl = slice(h * GLA_DV, (h + 1) * GLA_DV)
                m = _head_mask(h)
                o_ref[b, :, sl] = oi_ref[b, :, sl] + _nt(jnp.where(m, qgb, 0.0), st)
                upd = upd + jnp.where(m, _tn(v_ref[b, :, sl], kdb), 0.0)
            st_ref[b] = gl_ref[b, 0] * st + upd

    outs, ridden = _hosted_call(
        body, rider, name=name, grid=(nc_seq,),
        in_specs=[pl.BlockSpec((bsz, CHUNK, GV_W), lambda i: (0, i, 0)),
                  pl.BlockSpec((bsz, CHUNK, GQ_W), lambda i: (0, i, 0)),
                  pl.BlockSpec((bsz, CHUNK, GQ_W), lambda i: (0, i, 0)),
                  pl.BlockSpec((bsz, 1, 1, GQ_W), lambda i: (0, i, 0, 0)),
                  pl.BlockSpec((bsz, CHUNK, GV_W), lambda i: (0, i, C_GV // GV_W))],
        out_specs=[pl.BlockSpec((bsz, CHUNK, GV_W), lambda i: (0, i, 0)),
                   pl.BlockSpec((bsz, 1, GLA_DV, GQ_W), lambda i: (0, i, 0, 0))],
        out_shape=[jax.ShapeDtypeStruct((bsz, t_seq, GV_W), F32),
                   jax.ShapeDtypeStruct((bsz, nc_seq, GLA_DV, GQ_W), F32)],
        scratch_shapes=[pltpu.VMEM((bsz, GLA_DV, GQ_W), F32)],
        compiler_params=_cp(VMEM_BIG, ("arbitrary",)), args=(oi, qg, kd, gl, pj))
    return outs[0].reshape(bsz * t_seq, GV_W), outs[1], ridden


def _gla_scan_bwd(do, qg, kd, gl, projp, hist, *, bsz, nc_seq, name):
    t_seq = nc_seq * CHUNK
    do = do.reshape(bsz, t_seq, GV_W)
    qg, kd = qg.reshape(bsz, t_seq, GQ_W), kd.reshape(bsz, t_seq, GQ_W)
    gl = gl.reshape(bsz, nc_seq, 1, GQ_W)
    pj = projp.reshape(bsz, t_seq, PW)

    def body(do_ref, qg_ref, kd_ref, gl_ref, v_ref, hist_ref, dqg_ref, dkd_ref, dv_ref, dgl_ref, dst_ref):
        @pl.when(pl.program_id(0) == 0)
        def _():
            dst_ref[...] = jnp.zeros_like(dst_ref)

        for b in range(bsz):
            st = hist_ref[b, 0]
            dst = dst_ref[b]
            qgb = qg_ref[b]
            kdb = kd_ref[b]
            dqg = jnp.zeros((CHUNK, GQ_W), F32)
            dkd = jnp.zeros((CHUNK, GQ_W), F32)
            add = jnp.zeros((GLA_DV, GQ_W), F32)
            for h in range(NH):
                sl = slice(h * GLA_DV, (h + 1) * GLA_DV)
                m = _head_mask(h)
                doh = do_ref[b, :, sl]
                vh = v_ref[b, :, sl]
                dqg = dqg + jnp.where(m, _nn(doh, st), 0.0)
                dkd = dkd + jnp.where(m, _nn(vh, dst), 0.0)
                dv_ref[b, :, sl] = _nt(jnp.where(m, kdb, 0.0), dst)
                add = add + jnp.where(m, _tn(doh, qgb), 0.0)
            dqg_ref[b] = dqg
            dkd_ref[b] = dkd
            dgl_ref[b, 0] = jnp.sum(dst * st, axis=0, keepdims=True)
            dst_ref[b] = gl_ref[b, 0] * dst + add

    rev = lambda i: nc_seq - 1 - i
    outs = pl.pallas_call(
        body, name=name, grid=(nc_seq,),
        in_specs=[pl.BlockSpec((bsz, CHUNK, GV_W), lambda i: (0, rev(i), 0)),
                  pl.BlockSpec((bsz, CHUNK, GQ_W), lambda i: (0, rev(i), 0)),
                  pl.BlockSpec((bsz, CHUNK, GQ_W), lambda i: (0, rev(i), 0)),
                  pl.BlockSpec((bsz, 1, 1, GQ_W), lambda i: (0, rev(i), 0, 0)),
                  pl.BlockSpec((bsz, CHUNK, GV_W), lambda i: (0, rev(i), C_GV // GV_W)),
                  pl.BlockSpec((bsz, 1, GLA_DV, GQ_W), lambda i: (0, rev(i), 0, 0))],
        out_specs=[pl.BlockSpec((bsz, CHUNK, GQ_W), lambda i: (0, rev(i), 0)),
                   pl.BlockSpec((bsz, CHUNK, GQ_W), lambda i: (0, rev(i), 0)),
                   pl.BlockSpec((bsz, CHUNK, GV_W), lambda i: (0, rev(i), 0)),
                   pl.BlockSpec((bsz, 1, 1, GQ_W), lambda i: (0, rev(i), 0, 0))],
        out_shape=[jax.ShapeDtypeStruct((bsz, t_seq, GQ_W), F32), jax.ShapeDtypeStruct((bsz, t_seq, GQ_W), F32),
                   jax.ShapeDtypeStruct((bsz, t_seq, GV_W), F32), jax.ShapeDtypeStruct((bsz, nc_seq, 1, GQ_W), F32)],
        scratch_shapes=[pltpu.VMEM((bsz, GLA_DV, GQ_W), F32)],
        compiler_params=_cp(VMEM_BIG, ("arbitrary",)),
    )(do, qg, kd, gl, pj, hist)
    n = bsz * t_seq
    return (outs[0].reshape(n, GQ_W), outs[1].reshape(n, GQ_W), outs[2].reshape(n, GV_W),
            outs[3].reshape(bsz * nc_seq, 1, GQ_W))


def _gla_intra_bwd(projp, w2p, gb, do, dqg, dkd, dvi, dgl, *, nc_seq, name):
    n = projp.shape[0]
    nct = n // CHUNK
    scale = GLA_DK ** -0.5

    grp = _group(nc_seq)
    rows = grp * CHUNK

    def inner(gi, qk_ref, v_ref, sb_ref, w2_ref, gb_ref, do_ref, dqg_ref, dkd_ref, dvi_ref, dgl_ref,
              dqk_ref, dv_ref, dsb_ref, dw2_ref, dgb_ref):
        ci = (pl.program_id(0) * grp + gi) % nc_seq
        sb = sb_ref[...]
        w2 = w2_ref[...]
        graw, bc, valid = yield from _gla_gates(sb, w2, gb_ref[...], ci)
        bref = bc[MID:MID + 1, :]
        bl = bc[CHUNK - 1:CHUNK, :]
        q = qk_ref[:, 0:GQ_W] * scale
        k = qk_ref[:, GQ_W:2 * GQ_W]
        ex1 = jnp.exp(bc - bref)
        ex2 = jnp.exp(bref - bc)
        eb = jnp.exp(bc)
        ekd = jnp.exp(bl - bc)
        qi, ki = q * ex1, k * ex2
        r, c = _masks64()
        incl = r >= c
        upper = r <= c
        a_t, da, da_t = [], [], []
        for h in range(NH):
            sl = slice(h * GLA_DV, (h + 1) * GLA_DV)
            doh = do_ref[:, sl]
            vh = v_ref[:, sl]
            a_t.append(jnp.where(upper, _nt(jnp.where(_head_mask(h), ki, 0.0), qi), 0.0))
            da.append(jnp.where(incl, _nt(doh, vh), 0.0))
            da_t.append(jnp.where(upper, _nt(vh, doh), 0.0))
        yield
        dqi = jnp.zeros((CHUNK, GQ_W), F32)
        dki = jnp.zeros((CHUNK, GQ_W), F32)
        for h in range(NH):
            sl = slice(h * GLA_DV, (h + 1) * GLA_DV)
            m = _head_mask(h)
            dv_ref[:, sl] = (_nn(a_t[h], do_ref[:, sl]) + dvi_ref[:, sl]).astype(dv_ref.dtype)
            dqi = dqi + jnp.where(m, _nn(da[h], ki), 0.0)
            dki = dki + jnp.where(m, _nn(da_t[h], qi), 0.0)
        yield
        dqg = dqg_ref[...]
        dkd = dkd_ref[...]
        dqk_ref[:, 0:GQ_W] = ((dqi * ex1 + dqg * eb) * scale).astype(dqk_ref.dtype)
        dqk_ref[:, GQ_W:2 * GQ_W] = (dki * ex2 + dkd * ekd).astype(dqk_ref.dtype)
        t_qi, t_ki, t_kd = dqi * qi, dki * ki, dkd * (k * ekd)
        db = t_qi - t_ki + dqg * (q * eb) - t_kd
        dbref = jnp.sum(t_ki - t_qi, axis=0, keepdims=True)
        dbl = jnp.sum(t_kd, axis=0, keepdims=True) + dgl_ref[0] * jnp.exp(bl)
        rows = _iota2((CHUNK, GQ_W), 0)
        db = db + jnp.where(rows == MID, dbref, 0.0) + jnp.where(rows == CHUNK - 1, dbl, 0.0)
        dg = _nn_hi(upper.astype(F32), db)
        yield
        dgraw = jnp.where(valid, dg * (1.0 / GLA_NORM) * _sigmoid(-graw), 0.0)
        dsb_ref[...] = _nt_hi(dgraw, w2).astype(dsb_ref.dtype)
        dw2 = _tn_hi(sb, dgraw)
        dgb = jnp.sum(dgraw, axis=0, keepdims=True)
        _accumulate(dw2_ref, dw2, gi)
        _accumulate(dgb_ref, dgb, gi)

    rq = pl.BlockSpec((rows, GQ_W), lambda i: (i, 0))
    rv = pl.BlockSpec((rows, GV_W), lambda i: (i, 0))
    kinds = ["row"] * 3 + ["whole"] * 2 + ["row"] * 4 + ["lead"] + ["row"] * 3 + ["whole"] * 2
    return pl.pallas_call(
        _per_chunk(inner, kinds, grp), name=name, grid=(nct // grp,),
        in_specs=[pl.BlockSpec((rows, 2 * GQ_W), lambda i: (i, C_GQK // (2 * GQ_W))),
                  pl.BlockSpec((rows, GV_W), lambda i: (i, C_GV // GV_W)),
                  pl.BlockSpec((rows, LANE), lambda i: (i, C_SB // LANE)),
                  pl.BlockSpec((LANE, GQ_W), lambda i: (0, 0)), pl.BlockSpec((1, GQ_W), lambda i: (0, 0)),
                  rv, rq, rq, rv, pl.BlockSpec((grp, 1, GQ_W), lambda i: (i, 0, 0))],
        out_specs=[pl.BlockSpec((rows, 2 * GQ_W), lambda i: (i, 0)), rv, pl.BlockSpec((rows, LANE), lambda i: (i, 0)),
                   pl.BlockSpec((LANE, GQ_W), lambda i: (0, 0)), pl.BlockSpec((1, GQ_W), lambda i: (0, 0))],
        out_shape=[jax.ShapeDtypeStruct((n, 2 * GQ_W), BF16), jax.ShapeDtypeStruct((n, GV_W), BF16),
                   jax.ShapeDtypeStruct((n, LANE), BF16), jax.ShapeDtypeStruct((LANE, GQ_W), F32),
                   jax.ShapeDtypeStruct((1, GQ_W), F32)],
        compiler_params=_cp(VMEM_BIG, ("arbitrary",)),
    )(projp, projp, projp, w2p, gb, do, dqg, dkd, dvi, dgl)


SECTIONS = ((C_QKV, 1536), (C_DZ, 512), (C_GQK, 512), (C_GV, 512), (C_GR, 512), (C_SA, 128), (C_SB, 128))


def _inproj_bwd(secs, wp, h0, g1, dx1, *, tr, name):
    n, d = h0.shape

    def body(*refs):
        sec_refs = refs[:len(SECTIONS)]
        wp_ref, h0_ref, g_ref, dx1_ref, o_ref, dg_ref = refs[len(SECTIONS):]
        dh = None
        for s_ref, (off, wd) in zip(sec_refs, SECTIONS):
            part = _nt(s_ref[...], wp_ref[:, off:off + wd])
            dh = part if dh is None else dh + part
        dx, dg = _rms_bwd_math(h0_ref[...], g_ref[...], dh)
        o_ref[...] = dx1_ref[...] + dx

        @pl.when(pl.program_id(0) == 0)
        def _():
            dg_ref[...] = dg

        @pl.when(pl.program_id(0) > 0)
        def _():
            dg_ref[...] += dg

    row = pl.BlockSpec((tr, d), lambda i: (i, 0))
    vec = pl.BlockSpec((1, d), lambda i: (0, 0))
    return pl.pallas_call(
        body, name=name, grid=(n // tr,),
        in_specs=[pl.BlockSpec((tr, wd), lambda i: (i, 0)) for _, wd in SECTIONS]
        + [pl.BlockSpec((d, PW), lambda i: (0, 0)), row, vec, row],
        out_specs=[row, vec],
        out_shape=[jax.ShapeDtypeStruct((n, d), F32), jax.ShapeDtypeStruct((1, d), F32)],
        compiler_params=_cp(VMEM_BIG),
    )(*secs, wp, h0, g1, dx1)


def _adamw(w, g, m, v, *, name, emit_grad=False):
    lead = w.ndim - 2
    r, c = w.shape[-2:]
    tr = _tile(r, 256, 8) if r > 256 else r
    c1 = 1.0 - ADAM_B1 ** ADAM_STEP
    c2 = 1.0 - ADAM_B2 ** ADAM_STEP
    n_out = 4 if emit_grad else 3

    def body(w_ref, g_ref, m_ref, v_ref, *out_refs):
        rd = (lambda ref: ref[0]) if lead else (lambda ref: ref[...])
        gv = g_ref[:, 0:c]
        nm = ADAM_B1 * rd(m_ref) + (1.0 - ADAM_B1) * gv
        nv = ADAM_B2 * rd(v_ref) + (1.0 - ADAM_B2) * (gv * gv)
        res = [-ADAM_LR * ((nm / c1) / (jnp.sqrt(nv / c2) + ADAM_EPS) + ADAM_WD * rd(w_ref)), nm, nv, gv]
        for o_ref, val in zip(out_refs, res):
            if lead:
                o_ref[0] = val
            else:
                o_ref[...] = val

    blk = pl.BlockSpec((1,) * lead + (tr, c), lambda i: (0,) * lead + (i, 0))
    gblk = pl.BlockSpec((tr, g.shape[1]), lambda i: (i, 0))
    sds = jax.ShapeDtypeStruct(w.shape, F32)
    return pl.pallas_call(
        body, name=name, grid=(r // tr,), in_specs=[blk, gblk, blk, blk], out_specs=[blk] * n_out,
        out_shape=[sds] * n_out, compiler_params=_cp(VMEM_BIG),
    )(w, g, m, v)


def _pair_sum(where, g, theirs, *, name):
    lead, r, cols = g.shape
    half = r // 2
    tr = _tile(half, 256, 16)
    nh = half // tr

    def body(w_ref, a_ref, b_ref, o_ref):
        o_ref[...] = (a_ref[...] + b_ref[...]).astype(o_ref.dtype)

    blk = pl.BlockSpec((1, tr, cols), lambda s, i, w: (s, i, 0))
    return pl.pallas_call(
        body, name=name,
        grid_spec=pltpu.PrefetchScalarGridSpec(
            num_scalar_prefetch=1, grid=(lead, nh),
            in_specs=[pl.BlockSpec((1, tr, cols), lambda s, i, w: (s, w[0] * nh + i, 0)), blk], out_specs=blk),
        out_shape=jax.ShapeDtypeStruct((lead, half, cols), BF16), compiler_params=_cp(VMEM_BIG),
    )(where, g, theirs)


def _chip_sum(where, pair, q, *, name):
    _, half, cols = pair.shape
    tr = _tile(half, 256, 16)
    nh = half // tr

    def body(w_ref, own_ref, q1_ref, q2_ref, q3_ref, o_ref):
        f = lambda ref: ref[0].astype(F32)
        o_ref[...] = ((f(own_ref) + f(q1_ref)) + f(q2_ref)) + f(q3_ref)

    def peer(d):
        return pl.BlockSpec((1, tr, cols), lambda i, w: ((w[1] + d) % N_CHIPS, i, 0))

    return pl.pallas_call(
        body, name=name,
        grid_spec=pltpu.PrefetchScalarGridSpec(
            num_scalar_prefetch=1, grid=(nh,),
            in_specs=[peer(0), peer(1), peer(2), peer(3)],
            out_specs=pl.BlockSpec((tr, cols), lambda i, w: (w[0] * nh + i, 0))),
        out_shape=jax.ShapeDtypeStruct((2 * half, cols), F32), compiler_params=_cp(VMEM_BIG),
    )(where, pair, q, q, q)


VM = pl.BlockSpec(memory_space=pltpu.VMEM)


def _row_chunks(rows, n_split):
    size = rows // n_split
    assert size * n_split == rows and size % 16 == 0, (rows, n_split)
    return [(s, pl.ds(s * size, size)) for s in range(n_split)], size


D2D_SPLIT = 4
ICI_SPLIT = 2


def _sibling_halves(grads):
    n_arr = len(grads)

    def body(*refs):
        ins = refs[:n_arr]
        theirs = refs[n_arr:2 * n_arr]
        send_sems, recv_sems = refs[2 * n_arr:]
        x, y, c = _place()
        copies = []
        for k in range(n_arr):
            half = ins[k].shape[1] // 2
            chunks, size = _row_chunks(half, D2D_SPLIT)
            for s, dst_rows in chunks:
                give = pltpu.make_async_remote_copy(
                    src_ref=ins[k].at[:, pl.ds((1 - c) * half + s * size, size), :], dst_ref=theirs[k].at[:, dst_rows, :],
                    send_sem=send_sems.at[k, s], recv_sem=recv_sems.at[k, s], device_id=(x, y, 1 - c),
                    device_id_type=MESH)
                give.start()
                copies.append(give)
        for give in copies:
            give.wait()

    halves = [jax.ShapeDtypeStruct((g.shape[0], g.shape[1] // 2, g.shape[2]), F32) for g in grads]
    sem = pltpu.SemaphoreType.DMA((n_arr, D2D_SPLIT))
    return pl.pallas_call(
        body, name="sibling_halves", in_specs=[ANY] * n_arr, out_specs=[ANY] * n_arr, out_shape=halves,
        scratch_shapes=[sem, sem],
    )(*grads)


def _chip_exchange(parts):
    n_arr = len(parts)

    def body(*refs):
        ins = refs[:n_arr]
        outs = refs[n_arr:2 * n_arr]
        send_sems, recv_sems = refs[2 * n_arr:]
        x, y, c = _place()
        me = 2 * x + y
        sends = []
        for k in range(n_arr):
            chunks, _ = _row_chunks(ins[k].shape[1], ICI_SPLIT)
            for d, (px, py, pj) in enumerate(_other_chips(x, y)):
                for s, rows in chunks:
                    cp = pltpu.make_async_remote_copy(
                        src_ref=ins[k].at[pj, rows, :], dst_ref=outs[k].at[me, rows, :], send_sem=send_sems.at[k, d, s],
                        recv_sem=recv_sems.at[k, d, s], device_id=(px, py, c), device_id_type=MESH)
                    cp.start()
                    sends.append(cp)
        for k in range(n_arr):
            chunks, _ = _row_chunks(ins[k].shape[1], ICI_SPLIT)
            for d, (px, py, pj) in enumerate(_other_chips(x, y)):
                for s, rows in chunks:
                    pltpu.make_async_remote_copy(
                        src_ref=ins[k].at[pj, rows, :], dst_ref=outs[k].at[pj, rows, :], send_sem=send_sems.at[k, d, s],
                        recv_sem=recv_sems.at[k, d, s], device_id=(px, py, c), device_id_type=MESH).wait_recv()
        for cp in sends:
            cp.wait_send()

    sem = pltpu.SemaphoreType.DMA((n_arr, 3, ICI_SPLIT))
    return pl.pallas_call(
        body, name="chip_exchange", in_specs=[ANY] * n_arr, out_specs=[ANY] * n_arr,
        out_shape=[jax.ShapeDtypeStruct(p.shape, p.dtype) for p in parts],
        scratch_shapes=[sem, sem],
    )(*parts)


class _SiblingHalvesRider:
    def __init__(self, grads):
        self.inputs = list(grads)
        self.out_shapes = [jax.ShapeDtypeStruct((g.shape[0], g.shape[1] // 2, g.shape[2]), F32) for g in grads]
        self.aliases = {}
        self.sems = [pltpu.SemaphoreType.DMA((len(grads), D2D_SPLIT))] * 2

    def _copies(self, ins, outs, sems):
        x, y, c = _place()
        for k in range(len(ins)):
            half = ins[k].shape[1] // 2
            chunks, size = _row_chunks(half, D2D_SPLIT)
            for s, dst_rows in chunks:
                yield pltpu.make_async_remote_copy(
                    src_ref=ins[k].at[:, pl.ds((1 - c) * half + s * size, size), :], dst_ref=outs[k].at[:, dst_rows, :],
                    send_sem=sems[0].at[k, s], recv_sem=sems[1].at[k, s], device_id=(x, y, 1 - c), device_id_type=MESH)

    def first(self, ins, outs, sems):
        for cp in self._copies(ins, outs, sems):
            cp.start()

    def last(self, ins, outs, sems):
        for cp in self._copies(ins, outs, sems):
            cp.wait()


class _ChipExchangeRider:
    def __init__(self, parts):
        self.inputs = list(parts)
        self.out_shapes = [jax.ShapeDtypeStruct(p.shape, p.dtype) for p in parts]
        self.aliases = {}
        self.sems = [pltpu.SemaphoreType.DMA((len(parts), 3, ICI_SPLIT))] * 2

    def _copies(self, ins, outs, sems, receiving):
        x, y, c = _place()
        for k in range(len(ins)):
            chunks, _ = _row_chunks(ins[k].shape[1], ICI_SPLIT)
            for d, (px, py, pj) in enumerate(_other_chips(x, y)):
                for s, rows in chunks:
                    yield pltpu.make_async_remote_copy(
                        src_ref=ins[k].at[pj, rows, :], dst_ref=outs[k].at[pj if receiving else 2 * x + y, rows, :],
                        send_sem=sems[0].at[k, d, s], recv_sem=sems[1].at[k, d, s], device_id=(px, py, c),
                        device_id_type=MESH)

    def first(self, ins, outs, sems):
        for cp in self._copies(ins, outs, sems, False):
            cp.start()

    def last(self, ins, outs, sems):
        for cp in self._copies(ins, outs, sems, True):
            cp.wait_recv()
        for cp in self._copies(ins, outs, sems, False):
            cp.wait_send()


def _sibling_join(bufs):
    n_arr = len(bufs)

    def body(*refs):
        bufs_out = refs[n_arr:2 * n_arr]
        send_sems, recv_sems = refs[2 * n_arr:]
        x, y, c = _place()
        copies = []
        for k in range(n_arr):
            half = bufs_out[k].shape[0] // 2
            chunks, size = _row_chunks(half, D2D_SPLIT)
            for s, _ in chunks:
                rows = pl.ds(c * half + s * size, size)
                give = pltpu.make_async_remote_copy(
                    src_ref=bufs_out[k].at[rows, :], dst_ref=bufs_out[k].at[rows, :], send_sem=send_sems.at[k, s],
                    recv_sem=recv_sems.at[k, s], device_id=(x, y, 1 - c), device_id_type=MESH)
                give.start()
                copies.append((k, s, half, size, give))
        for k, s, half, size, give in copies:
            rows = pl.ds((1 - c) * half + s * size, size)
            pltpu.make_async_remote_copy(
                src_ref=bufs_out[k].at[rows, :], dst_ref=bufs_out[k].at[rows, :], send_sem=send_sems.at[k, s],
                recv_sem=recv_sems.at[k, s], device_id=(x, y, 1 - c), device_id_type=MESH).wait_recv()
            give.wait_send()

    sem = pltpu.SemaphoreType.DMA((n_arr, D2D_SPLIT))
    return pl.pallas_call(
        body, name="sibling_join", in_specs=[ANY] * n_arr, out_specs=[ANY] * n_arr,
        out_shape=[jax.ShapeDtypeStruct(b.shape, F32) for b in bufs],
        input_output_aliases={k: k for k in range(n_arr)},
        scratch_shapes=[sem, sem],
    )(*bufs)


PACK_ROWS = 48


def _small_allreduce(pack):
    masks = [(dx, dy, dc) for dx in (0, 1) for dy in (0, 1) for dc in (0, 1)][1:]

    def body(p_ref, o_ref, buf, send_sems, recv_sems):
        x, y, c = _place()
        me = 4 * x + 2 * y + c
        buf[me] = p_ref[...]
        sends = []
        for k, (dx, dy, dc) in enumerate(masks):
            peer = (1 - x if dx else x, 1 - y if dy else y, 1 - c if dc else c)
            cp = pltpu.make_async_remote_copy(
                src_ref=p_ref, dst_ref=buf.at[me], send_sem=send_sems.at[k], recv_sem=recv_sems.at[k],
                device_id=peer, device_id_type=MESH)
            cp.start()
            sends.append(cp)
        for k, (dx, dy, dc) in enumerate(masks):
            peer = (1 - x if dx else x, 1 - y if dy else y, 1 - c if dc else c)
            pj = 4 * peer[0] + 2 * peer[1] + peer[2]
            pltpu.make_async_remote_copy(
                src_ref=p_ref, dst_ref=buf.at[pj], send_sem=send_sems.at[k], recv_sem=recv_sems.at[k],
                device_id=peer, device_id_type=MESH).wait_recv()
        for cp in sends:
            cp.wait_send()
        tot = buf[0]
        for k in range(1, 8):
            tot = tot + buf[k]
        o_ref[...] = tot
        o_ref[0:N_META, :] = tot[0:N_META] + tot[N_META:2 * N_META]

    return pl.pallas_call(
        body, name="small_allreduce", in_specs=[VM], out_specs=VM,
        out_shape=jax.ShapeDtypeStruct((PACK_ROWS, D_MODEL), F32),
        scratch_shapes=[pltpu.VMEM((8, PACK_ROWS, D_MODEL), F32), pltpu.SemaphoreType.DMA((7,)),
                        pltpu.SemaphoreType.DMA((7,))],
    )(pack)


def _pad_lanes(vec, offset):
    k = vec.shape[1]
    return jnp.concatenate([jnp.zeros((1, offset), F32), vec, jnp.zeros((1, LANE - offset - k), F32)], axis=1)


def _local_step(x, tgt, meta, norm1_g, wp, conv_w, a_log, dt_bias, dn_norm_g, gla_w2, gla_b, gla_norm_g,
                w_out, norm2_g, w_up, w_down, final_norm_g, late_gather=None, where=None):
    bsz, s_len, d = x.shape
    t_seq = s_len + CHUNK
    nc_seq = t_seq // CHUNK
    n = bsz * t_seq
    tr = _tile(t_seq, 832)
    tt = _tile(t_seq, 416)

    lead = jnp.concatenate([jnp.zeros((N_PAD, d), F32), meta], axis=0)
    h0 = jnp.concatenate([jnp.broadcast_to(lead[None], (bsz, CHUNK, d)), x], axis=1).reshape(n, d)
    tgt_p = jnp.concatenate([jnp.zeros((bsz, CHUNK, d), F32), tgt], axis=1).reshape(n, d)
    alog_row = _pad_lanes(a_log, 4)
    dtb_row = _pad_lanes(dt_bias, 4)
    w2p = jnp.concatenate([gla_w2, jnp.zeros((LANE - GLA_RANK, GQ_W), F32)], axis=0)

    h = _rms_fwd(h0, norm1_g, tr=tr, name="norm1")
    (projp,) = _mm(h, wp, "nn", tm=tt, tn=PW, tk=d, out_dtypes=(F32,), name="in_proj")
    qn, kn, v = _dnprep_fwd(projp, conv_w, bsz=bsz, t_seq=t_seq, tt=tt, name="dn_prep")
    u, w, qg, kd, pmat, tmat, gl = _dn_intra_fwd(qn, kn, v, projp, alog_row, dtb_row, nc_seq=nc_seq, name="dn_intra")
    ride_a, ride_b = late_gather if late_gather is not None else (None, None)
    o_dn, vn, hist, got_a = _dn_scan_fwd(u, w, qg, kd, pmat, gl, bsz=bsz, nc_seq=nc_seq, name="dn_scan", rider=ride_a)
    oi, gqg, gkd, ggl = _gla_intra_fwd(projp, w2p, gla_b, nc_seq=nc_seq, name="gla_intra")
    o_gla, ghist, got_b = _gla_scan_fwd(oi, gqg, gkd, ggl, projp, bsz=bsz, nc_seq=nc_seq, name="gla_scan", rider=ride_b)
    if late_gather is not None:
        w_out = got_a[0].reshape(d, d)
        w_up = got_a[1].transpose(1, 0, 2).reshape(d, D_FF)
        w_down = got_b[0].reshape(D_FF, d)
    mix = _gnorm_fwd(o_dn, o_gla, projp, dn_norm_g, gla_norm_g, tr=tr, name="gated_norm")
    (x1,) = _mm(mix, w_out, "nn", tm=tr, tn=d, tk=d, out_dtypes=(F32,), extras=(h0,),
                epilogue=lambda acc, res: (res + acc,), name="out_proj")
    h2 = _rms_fwd(x1, norm2_g, tr=tr, name="norm2")

    (act,) = _mm(h2, w_up, "nn", tm=tt, tn=D_FF, tk=d, out_dtypes=(BF16,),
                 epilogue=lambda acc: (jnp.square(jnp.maximum(acc, 0.0)),), name="mlp_up")
    (x2,) = _mm(act, w_down, "nn", tm=tr, tn=d, tk=D_FF, out_dtypes=(F32,), extras=(x1,),
                epilogue=lambda acc, res: (res + acc,), name="mlp_down")
    dx2, dx2b, d_final_g, loss_tile = _final_loss(x2, final_norm_g, tgt_p, t_seq=t_seq, tr=tr, name="final_loss")

    (dup,) = _mm(dx2b, w_down, "nt", tm=tt, tn=D_FF, tk=d, out_dtypes=(BF16,), extras=(act,),
                 epilogue=lambda acc, a: (acc * (2.0 * jnp.sqrt(a.astype(F32))),), name="mlp_down_bwd")
    (d_w_down,) = _mm(act, dx2b, "tn", tm=D_FF // 2, tn=d, tk=tr, out_dtypes=(F32,), name="w_down_grad")
    (d_w_up,) = _mm(h2, dup, "tn", tm=d, tn=D_FF // 2, tk=tr, out_dtypes=(F32,), name="w_up_grad")
    mlp_sm = [d_w_up.reshape(d, N_CHIPS, D_FF // N_CHIPS).transpose(1, 0, 2), d_w_down.reshape(N_CHIPS, D_FF // N_CHIPS, d)]
    ride1 = _SiblingHalvesRider(mlp_sm) if where is not None else None
    res = _mm(dup, w_up, "nt", tm=tr, tn=d, tk=D_FF, out_dtypes=(F32,), name="mlp_up_bwd", rider=ride1)
    ((dh2,), theirs) = res if where is not None else (res, None)
    ride2 = None
    if where is not None:
        mlp_pair = [_pair_sum(where, a, b, name=f"pair_sum_mlp{k}") for k, (a, b) in enumerate(zip(mlp_sm, theirs))]
        ride2 = _ChipExchangeRider(mlp_pair)
    dx1, dx1b, d_norm2_g = _rms_bwd_add(x1, norm2_g, dh2, dx2, tr=tr, name="norm2_bwd")

    (dmix,) = _mm(dx1b, w_out, "nt", tm=tr, tn=d, tk=d, out_dtypes=(F32,), name="out_proj_bwd")
    (d_w_out,) = _mm(mix, dx1b, "tn", tm=d, tn=d, tk=tr, out_dtypes=(F32,), name="w_out_grad")
    do_dn, ddz, do_gla, dgr, d_dn_norm_g, d_gla_norm_g = _gnorm_bwd(
        dmix, o_dn, o_gla, projp, dn_norm_g, gla_norm_g, tr=tr, name="gated_norm_bwd")
    du, dw, dqg, dkd, dgl = _dn_scan_bwd(do_dn, w, qg, kd, vn, pmat, gl, hist, bsz=bsz, nc_seq=nc_seq,
                                          name="dn_scan_bwd")
    dqn, dkn, dv, dsa, d_alog, d_dtb, mlp_parts = _dn_intra_bwd(
        qn, kn, v, projp, alog_row, dtb_row, u, w, tmat, du, dw, dqg, dkd, do_dn, vn, dgl, nc_seq=nc_seq,
        name="dn_intra_bwd", rider=ride2)
    dz, d_conv_w = _dnprep_bwd_a(projp, conv_w, dqn, dkn, dv, bsz=bsz, t_seq=t_seq, tt=tt, name="dn_prep_bwd")
    dcin = _dnprep_bwd_b(dz, conv_w, bsz=bsz, t_seq=t_seq, tt=tt, name="conv_bwd")
    gdqg, gdkd, gdvi, gdgl = _gla_scan_bwd(do_gla, gqg, gkd, ggl, projp, ghist, bsz=bsz, nc_seq=nc_seq,
                                            name="gla_scan_bwd")
    dgqk, dgv, dsb, d_w2p, d_gla_b = _gla_intra_bwd(projp, w2p, gla_b, do_gla, gdqg, gdkd, gdvi, gdgl,
                                                    nc_seq=nc_seq, name="gla_intra_bwd")

    secs = (dcin, ddz, dgqk, dgv, dgr, dsa, dsb)
    g_lo = _grad_tn(h, secs[0:2], tk=tr, name="w_in_grad_lo")
    g_hi = _grad_tn(h, secs[2:7], tk=tr, name="w_in_grad_hi")
    dh0, d_norm1_g = _inproj_bwd(secs, wp, h0, norm1_g, dx1, tr=tt, name="in_proj_bwd")
    dh0 = dh0.reshape(bsz, t_seq, d)
    grad_x = dh0[:, CHUNK:]
    d_meta_rows = dh0[:, N_PAD:CHUNK].reshape(bsz * N_META, d)

    grads = dict(w_in_lo=g_lo, w_in_hi=g_hi, w_out=d_w_out, w_up=d_w_up, w_down=d_w_down, meta_rows=d_meta_rows,
                 norm1_g=d_norm1_g, conv_w=d_conv_w, a_log_tile=d_alog, dt_bias_tile=d_dtb, dn_norm_g=d_dn_norm_g,
                 gla_w2=d_w2p[0:GLA_RANK], gla_b=d_gla_b, gla_norm_g=d_gla_norm_g, norm2_g=d_norm2_g,
                 final_norm_g=d_final_g, loss_tile=loss_tile)
    if where is not None:
        grads["mlp_exchanged"] = (mlp_pair, mlp_parts)
    return grad_x, grads


SHARD_W = IN_WIDTH // N_CHIPS
PADDED_ORDER = ((0, 2048), (2056, 3592), (2048, 2056), LANE - 8, (3592, 3608), LANE - GLA_RANK)


def _pad_layout(w_full):
    pieces = [jnp.zeros((w_full.shape[0], seg), w_full.dtype) if isinstance(seg, int) else w_full[:, seg[0]:seg[1]]
              for seg in PADDED_ORDER]
    return jnp.concatenate(pieces, axis=1)


def _padded_from_shards(stack):
    pieces = []
    for seg in PADDED_ORDER:
        if isinstance(seg, int):
            pieces.append(jnp.zeros((stack.shape[1], seg), stack.dtype))
            continue
        for j in range(N_CHIPS):
            lo, hi = max(seg[0], j * SHARD_W), min(seg[1], (j + 1) * SHARD_W)
            if lo < hi:
                pieces.append(stack[j, :, lo - j * SHARD_W:hi - j * SHARD_W])
    return jnp.concatenate(pieces, axis=1)


def _shards_from_padded(g_lo, g_hi):
    split = g_lo.shape[1]
    starts, pos = [], 0
    for seg in PADDED_ORDER:
        width = seg if isinstance(seg, int) else seg[1] - seg[0]
        if not isinstance(seg, int):
            starts.append((seg[0], seg[1], pos))
        pos += width
    shards = []
    for j in range(N_CHIPS):
        pieces = []
        for a, b, p0 in sorted(starts):
            lo, hi = max(a, j * SHARD_W), min(b, (j + 1) * SHARD_W)
            if lo < hi:
                src, off = (g_lo, 0) if p0 < split else (g_hi, split)
                pieces.append(src[:, p0 + lo - a - off:p0 + hi - a - off])
        pieces.append(jnp.zeros((g_lo.shape[0], D_MODEL - SHARD_W), g_lo.dtype))
        shards.append(jnp.concatenate(pieces, axis=1))
    return jnp.stack(shards)


def _pack_small(g, bsz):
    assert bsz * N_META == 32
    row = jnp.concatenate([g["a_log_tile"], g["dt_bias_tile"], g["dn_norm_g"], g["gla_norm_g"], g["gla_b"],
                           g["loss_tile"], jnp.zeros((1, LANE), F32)], axis=1)
    return jnp.concatenate([g["meta_rows"], g["norm1_g"], g["conv_w"].reshape(6, D_MODEL), row,
                            g["gla_w2"].reshape(4, D_MODEL), g["norm2_g"], g["final_norm_g"],
                            jnp.zeros((2, D_MODEL), F32)], axis=0)


def kernel(x, meta_tokens, norm1_g, w_in, conv_w, a_log, dt_bias, dn_norm_g, gla_w2, gla_b, gla_norm_g, w_out, norm2_g, w_up, w_down, final_norm_g, loss_target, m_meta_tokens, m_norm1_g, m_w_in, m_conv_w, m_a_log, m_dt_bias, m_dn_norm_g, m_gla_w2, m_gla_b, m_gla_norm_g, m_w_out, m_norm2_g, m_w_up, m_w_down, m_final_norm_g, v_meta_tokens, v_norm1_g, v_w_in, v_conv_w, v_a_log, v_dt_bias, v_dn_norm_g, v_gla_w2, v_gla_b, v_gla_norm_g, v_w_out, v_norm2_g, v_w_up, v_w_down, v_final_norm_g):
    bsz = x.shape[0]
    chip = 2 * lax.axis_index("x") + lax.axis_index("y")

    lane_pad = lambda a, wd: jnp.pad(a, ((0, 0), (0, wd - a.shape[1])))
    where = jnp.stack([lax.axis_index("c"), chip]).astype(jnp.int32)
    slot = lambda a, dt, nm: _to_slot(where, a, dt, name="slot_" + nm)
    early = _GatherRider([slot(lane_pad(w_in[0], D_MODEL), BF16, "w_in"), slot(meta_tokens, F32, "meta"),
                          slot(conv_w[0], F32, "conv"), slot(lane_pad(gla_w2[0], LANE), F32, "gla_w2")],
                         [True, False, False, False])
    g_in, g_meta, g_conv, g_w2 = _exchange_now(early, name="gather_early")
    late = (_GatherRider([slot(w_out[0], BF16, "w_out"), slot(w_up[0], BF16, "w_up")], [True, True]),
            _GatherRider([slot(w_down[0], BF16, "w_down")], [True]))
    wp = _padded_from_shards(g_in)
    meta_f = g_meta.transpose(1, 0, 2).reshape(N_META, D_MODEL)
    conv_f = g_conv.transpose(1, 0, 2).reshape(4, QKV_W)
    w2_f = g_w2[:, :, 0:GQ_W // N_CHIPS].transpose(1, 0, 2).reshape(GLA_RANK, GQ_W)

    grad_x, g = _local_step(x, loss_target, meta_f, norm1_g, wp, conv_f, a_log, dt_bias, dn_norm_g, w2_f, gla_b,
                            gla_norm_g, None, norm2_g, None, None, final_norm_g.reshape(1, D_MODEL), late_gather=late, where=where)

    shard_major = [_shards_from_padded(g["w_in_lo"], g["w_in_hi"]), g["w_out"].reshape(N_CHIPS, D_MODEL // N_CHIPS, D_MODEL)]
    theirs = _exchange_now(_SiblingHalvesRider(shard_major), name="sibling_halves")
    pair = [_pair_sum(where, a, b, name=f"pair_sum_{k}") for k, (a, b) in enumerate(zip(shard_major, theirs))]
    parts = _exchange_now(_ChipExchangeRider(pair), name="chip_exchange")
    mlp_pair, mlp_parts = g["mlp_exchanged"]
    halves = [_chip_sum(where, p, q, name=f"chip_sum_{k}")
              for k, (p, q) in enumerate(zip(pair + mlp_pair, list(parts) + list(mlp_parts)))]
    gw_in, gw_out, gw_up, gw_down = _sibling_join(halves)

    red = _small_allreduce(_pack_small(g, bsz))
    g_meta_full = red[0:N_META]
    g_norm1 = red[32:33]
    g_conv_full = red[33:39].reshape(4, QKV_W)
    srow = red[39:40]
    g_alog, g_dtb = srow[:, 4:8], srow[:, LANE + 4:LANE + 8]
    g_dn_norm, g_gla_norm = srow[:, 2 * LANE:3 * LANE], srow[:, 3 * LANE:4 * LANE]
    g_gla_b = srow[:, 4 * LANE:6 * LANE]
    loss = srow[0, 6 * LANE]
    g_w2_full = red[40:44].reshape(GLA_RANK, GQ_W)
    g_norm2 = red[44:45]
    g_final = red[45:46]
    g_meta_sh = lax.dynamic_slice_in_dim(g_meta_full, chip * (D_MODEL // N_CHIPS), D_MODEL // N_CHIPS, axis=1)
    g_conv_sh = lax.dynamic_slice_in_dim(g_conv_full, chip * (QKV_W // N_CHIPS), QKV_W // N_CHIPS, axis=1)
    g_w2_sh = lax.dynamic_slice_in_dim(g_w2_full, chip * (GQ_W // N_CHIPS), GQ_W // N_CHIPS, axis=1)

    names = ["meta_tokens", "norm1_g", "w_in", "conv_w", "a_log", "dt_bias", "dn_norm_g", "gla_w2", "gla_b",
             "gla_norm_g", "w_out", "norm2_g", "w_up", "w_down", "final_norm_g"]
    weights = dict(meta_tokens=meta_tokens, norm1_g=norm1_g, w_in=w_in, conv_w=conv_w, a_log=a_log, dt_bias=dt_bias,
                   dn_norm_g=dn_norm_g, gla_w2=gla_w2, gla_b=gla_b, gla_norm_g=gla_norm_g, w_out=w_out,
                   norm2_g=norm2_g, w_up=w_up, w_down=w_down, final_norm_g=final_norm_g)
    ms = dict(meta_tokens=m_meta_tokens, norm1_g=m_norm1_g, w_in=m_w_in, conv_w=m_conv_w, a_log=m_a_log,
              dt_bias=m_dt_bias, dn_norm_g=m_dn_norm_g, gla_w2=m_gla_w2, gla_b=m_gla_b, gla_norm_g=m_gla_norm_g,
              w_out=m_w_out, norm2_g=m_norm2_g, w_up=m_w_up, w_down=m_w_down, final_norm_g=m_final_norm_g)
    vs = dict(meta_tokens=v_meta_tokens, norm1_g=v_norm1_g, w_in=v_w_in, conv_w=v_conv_w, a_log=v_a_log,
              dt_bias=v_dt_bias, dn_norm_g=v_dn_norm_g, gla_w2=v_gla_w2, gla_b=v_gla_b, gla_norm_g=v_gla_norm_g,
              w_out=v_w_out, norm2_g=v_norm2_g, w_up=v_w_up, w_down=v_w_down, final_norm_g=v_final_norm_g)
    grads2d = dict(meta_tokens=g_meta_sh, norm1_g=g_norm1, w_in=gw_in, conv_w=g_conv_sh, a_log=g_alog, dt_bias=g_dtb,
                   dn_norm_g=g_dn_norm, gla_w2=g_w2_sh, gla_b=g_gla_b, gla_norm_g=g_gla_norm, w_out=gw_out,
                   norm2_g=g_norm2, w_up=gw_up, w_down=gw_down, final_norm_g=g_final)
    out_g, out_d, out_m, out_v = [], [], [], []
    for nm in names:
        shape = weights[nm].shape
        g2 = grads2d[nm]
        if len(shape) == 3:
            res = _adamw(weights[nm], g2, ms[nm], vs[nm], name=f"adamw_{nm}", emit_grad=nm == "w_in")
            gout = res[3] if nm == "w_in" else g2.reshape(shape)
        else:
            as2d = lambda a: a.reshape(g2.shape)
            res = _adamw(as2d(weights[nm]), g2, as2d(ms[nm]), as2d(vs[nm]), name=f"adamw_{nm}")
            gout = g2.reshape(shape)
        out_g.append(gout)
        out_d.append(res[0].reshape(shape))
        out_m.append(res[1].reshape(shape))
        out_v.append(res[2].reshape(shape))
    return (loss, grad_x, *out_g, *out_d, *out_m, *out_v)
```

```python
import functools

import jax
import jax.numpy as jnp
import numpy as np
from jax import lax
from jax.experimental import pallas as pl
from jax.experimental.pallas import tpu as pltpu

F32 = jnp.float32
BF16 = jnp.bfloat16
HI = lax.Precision.HIGHEST
MESH = pl.DeviceIdType.MESH

D_MODEL = 1024
N_META = 16
CHUNK = 64
N_PAD = CHUNK - N_META
NH = 4
DN_D = 128
GLA_DK = 64
GLA_DV = 128
GLA_RANK = 16
D_FF = 4 * D_MODEL
EPS = 1e-6
IN_WIDTH = 3608
C_QKV, C_DZ, C_GQK, C_GV, C_GR, C_SA, C_SB, PW = 0, 1536, 2048, 2560, 3072, 3584, 3712, 3840
LANE = 128
N_CHIPS = 4

ADAM_LR, ADAM_B1, ADAM_B2, ADAM_EPS, ADAM_WD, ADAM_STEP = 0.001, 0.9, 0.999, 1e-08, 0.01, 10

VMEM_BIG = 56 * 1024 * 1024


def _cp(vmem=None, sem=None):
    kw = {}
    if vmem is not None:
        kw["vmem_limit_bytes"] = vmem
    if sem is not None:
        kw["dimension_semantics"] = sem
    return pltpu.CompilerParams(**kw)


def _tile(n, target, mult=16):
    best = None
    for t in range(mult, min(n, target) + 1, mult):
        if n % t == 0:
            best = t
    assert best is not None, (n, target)
    return best


def _dot(a, b, dims, prec=None):
    return lax.dot_general(a, b, (dims, ((), ())), preferred_element_type=F32, precision=prec)


def _nn(a, b):
    return _dot(a.astype(BF16), b.astype(BF16), ((1,), (0,)))


def _nt(a, b):
    return _dot(a.astype(BF16), b.astype(BF16), ((1,), (1,)))


def _tn(a, b):
    return _dot(a.astype(BF16), b.astype(BF16), ((0,), (0,)))


def _nn_hi(a, b):
    return _dot(a, b, ((1,), (0,)), HI)


def _nt_hi(a, b):
    return _dot(a, b, ((1,), (1,)), HI)


def _tn_hi(a, b):
    return _dot(a, b, ((0,), (0,)), HI)


def _sigmoid(x):
    return 0.5 * jnp.tanh(0.5 * x) + 0.5


def _softplus(x):
    return jnp.maximum(x, 0.0) + jnp.log(1.0 + jnp.exp(-jnp.abs(x)))


def _logsigmoid(x):
    return -_softplus(-x)


def _iota2(shape, dim):
    return lax.broadcasted_iota(jnp.int32, shape, dim)


def _mm(a, b, mode, *, tm, tn, tk, out_dtypes, extras=(), epilogue=None, name, vmem=VMEM_BIG, rider=None,
        out_widths=None):
    if mode == "tn":
        K, M = a.shape
    else:
        M, K = a.shape
    N = b.shape[0] if mode == "nt" else b.shape[1]
    assert M % tm == 0 and N % tn == 0 and K % tk == 0, (name, M, N, K, tm, tn, tk)
    nk = K // tk
    n_ex, n_out = len(extras), len(out_dtypes)
    if mode == "tn":
        a_spec = pl.BlockSpec((tk, tm), lambda i, j, k: (k, i))
    else:
        a_spec = pl.BlockSpec((tm, tk), lambda i, j, k: (i, k))
    if mode == "nt":
        b_spec = pl.BlockSpec((tn, tk), lambda i, j, k: (j, k))
    else:
        b_spec = pl.BlockSpec((tk, tn), lambda i, j, k: (k, j))
    mn_spec = pl.BlockSpec((tm, tn), lambda i, j, k: (i, j))
    if out_widths is None:
        o_specs = [mn_spec] * n_out
        o_shapes = [jax.ShapeDtypeStruct((M, N), dt) for dt in out_dtypes]
    else:
        assert tn == N
        o_specs = [pl.BlockSpec((tm, wd), lambda i, j, k: (i, 0)) for wd in out_widths]
        o_shapes = [jax.ShapeDtypeStruct((M, wd), dt) for wd, dt in zip(out_widths, out_dtypes)]
    dims = {"nn": ((1,), (0,)), "nt": ((1,), (1,)), "tn": ((0,), (0,))}[mode]

    single = nk == 1
    direct = (not single) and epilogue is None and n_out == 1 and out_dtypes[0] == F32

    def body(*refs):
        a_ref, b_ref = refs[0], refs[1]
        ex_refs = refs[2:2 + n_ex]
        out_refs = refs[2 + n_ex:2 + n_ex + n_out]
        part = _dot(a_ref[...].astype(BF16), b_ref[...].astype(BF16), dims)

        def finish(acc):
            res = (acc,) if epilogue is None else epilogue(acc, *[e[...] for e in ex_refs])
            for o_ref, r in zip(out_refs, res):
                o_ref[...] = r.astype(o_ref.dtype)

        if single:
            finish(part)
            return
        acc_ref = out_refs[0] if direct else refs[2 + n_ex + n_out]
        k = pl.program_id(2)

        @pl.when(k == 0)
        def _():
            acc_ref[...] = part

        @pl.when(k > 0)
        def _():
            acc_ref[...] += part

        if not direct:
            @pl.when(k == nk - 1)
            def _():
                finish(acc_ref[...])

    outs, ridden = _hosted_call(
        body, rider, name=name, grid=(M // tm, N // tn, nk),
        in_specs=[a_spec, b_spec] + [mn_spec] * n_ex,
        out_specs=o_specs, out_shape=o_shapes,
        scratch_shapes=[] if (single or direct) else [pltpu.VMEM((tm, tn), F32)],
        compiler_params=_cp(vmem, ("parallel", "parallel", "arbitrary")), args=(a, b, *extras))
    return tuple(outs) if rider is None else (tuple(outs), ridden)


def _grad_tn(a, secs, *, tk, name):
    kk, m = a.shape
    widths = [s.shape[1] for s in secs]
    total = sum(widths)
    nk = kk // tk

    def body(*refs):
        a_ref, sec_refs, o_ref = refs[0], refs[1:-1], refs[-1]
        cat = sec_refs[0][...] if len(sec_refs) == 1 else jnp.concatenate([s[...] for s in sec_refs], axis=1)
        part = _dot(a_ref[...].astype(BF16), cat.astype(BF16), ((0,), (0,)))
        k = pl.program_id(0)

        @pl.when(k == 0)
        def _():
            o_ref[...] = part

        @pl.when(k > 0)
        def _():
            o_ref[...] += part

    return pl.pallas_call(
        body, name=name, grid=(nk,),
        in_specs=[pl.BlockSpec((tk, m), lambda k: (k, 0))] + [pl.BlockSpec((tk, w), lambda k: (k, 0)) for w in widths],
        out_specs=pl.BlockSpec((m, total), lambda k: (0, 0)),
        out_shape=jax.ShapeDtypeStruct((m, total), F32),
        compiler_params=_cp(VMEM_BIG, ("arbitrary",)),
    )(a, *secs)


def _rms_fwd(x, g, *, tr, name):
    n, d = x.shape

    def body(x_ref, g_ref, o_ref):
        xv = x_ref[...]
        r = lax.rsqrt(jnp.mean(xv * xv, axis=-1, keepdims=True) + EPS)
        o_ref[...] = (xv * r * g_ref[...]).astype(o_ref.dtype)

    return pl.pallas_call(
        body, name=name, grid=(n // tr,),
        in_specs=[pl.BlockSpec((tr, d), lambda i: (i, 0)), pl.BlockSpec((1, d), lambda i: (0, 0))],
        out_specs=pl.BlockSpec((tr, d), lambda i: (i, 0)),
        out_shape=jax.ShapeDtypeStruct((n, d), BF16),
        compiler_params=_cp(VMEM_BIG),
    )(x, g)


def _rms_bwd_math(xv, g, dy):
    r = lax.rsqrt(jnp.mean(xv * xv, axis=-1, keepdims=True) + EPS)
    xh = xv * r
    gdy = dy * g
    dx = r * (gdy - xh * jnp.mean(xh * gdy, axis=-1, keepdims=True))
    return dx, jnp.sum(dy * xh, axis=0, keepdims=True)


def _rms_bwd_add(x, g, dy, res, *, tr, name):
    n, d = x.shape

    def body(x_ref, g_ref, dy_ref, res_ref, o_ref, ob_ref, dg_ref):
        dx, dg = _rms_bwd_math(x_ref[...], g_ref[...], dy_ref[...])
        tot = res_ref[...] + dx
        o_ref[...] = tot
        ob_ref[...] = tot.astype(BF16)

        @pl.when(pl.program_id(0) == 0)
        def _():
            dg_ref[...] = dg

        @pl.when(pl.program_id(0) > 0)
        def _():
            dg_ref[...] += dg

    row = pl.BlockSpec((tr, d), lambda i: (i, 0))
    vec = pl.BlockSpec((1, d), lambda i: (0, 0))
    return pl.pallas_call(
        body, name=name, grid=(n // tr,),
        in_specs=[row, vec, row, row], out_specs=[row, row, vec],
        out_shape=[jax.ShapeDtypeStruct((n, d), F32), jax.ShapeDtypeStruct((n, d), BF16),
                   jax.ShapeDtypeStruct((1, d), F32)],
        compiler_params=_cp(VMEM_BIG),
    )(x, g, dy, res)


def _final_loss(x2, gf, tgt, *, t_seq, tr, name):
    n, d = x2.shape
    per_seq = t_seq // tr

    def body(x_ref, g_ref, t_ref, dx_ref, dxb_ref, dg_ref, loss_ref):
        i = pl.program_id(0)
        xv = x_ref[...]
        g = g_ref[...]
        r = lax.rsqrt(jnp.mean(xv * xv, axis=-1, keepdims=True) + EPS)
        xh = xv * r
        pos = (i % per_seq) * tr + _iota2((tr, 1), 0)
        real = pos >= CHUNK
        err = jnp.where(real, xh * g - t_ref[...], 0.0)
        dy = err * (1.0 / d)
        gdy = dy * g
        dx = r * (gdy - xh * jnp.mean(xh * gdy, axis=-1, keepdims=True))
        dx_ref[...] = dx
        dxb_ref[...] = dx.astype(BF16)
        dg = jnp.sum(dy * xh, axis=0, keepdims=True)
        ls = 0.5 * jnp.sum(jnp.mean(err * err, axis=-1, keepdims=True), axis=0, keepdims=True)
        ls = jnp.where(_iota2((1, LANE), 1) == 0, ls, 0.0)

        @pl.when(i == 0)
        def _():
            dg_ref[...] = dg
            loss_ref[...] = ls

        @pl.when(i > 0)
        def _():
            dg_ref[...] += dg
            loss_ref[...] += ls

    row = pl.BlockSpec((tr, d), lambda i: (i, 0))
    vec = pl.BlockSpec((1, d), lambda i: (0, 0))
    one = pl.BlockSpec((1, LANE), lambda i: (0, 0))
    return pl.pallas_call(
        body, name=name, grid=(n // tr,),
        in_specs=[row, vec, row], out_specs=[row, row, vec, one],
        out_shape=[jax.ShapeDtypeStruct((n, d), F32), jax.ShapeDtypeStruct((n, d), BF16),
                   jax.ShapeDtypeStruct((1, d), F32), jax.ShapeDtypeStruct((1, LANE), F32)],
        compiler_params=_cp(VMEM_BIG),
    )(x2, gf, tgt)


def _gnorm_fwd(o_dn, o_gla, projp, g_dn, g_gla, *, tr, name):
    n = o_dn.shape[0]
    w = NH * DN_D

    def body(odn_ref, ogl_ref, z_ref, r_ref, gdn_ref, ggl_ref, mix_ref):
        for grp, (o_ref, gate_ref, gain_ref) in enumerate(((odn_ref, z_ref, gdn_ref), (ogl_ref, r_ref, ggl_ref))):
            gain = gain_ref[...]
            for h in range(NH):
                sl = slice(h * DN_D, (h + 1) * DN_D)
                o = o_ref[:, sl].astype(F32)
                z = gate_ref[:, sl].astype(F32)
                r = lax.rsqrt(jnp.mean(o * o, axis=-1, keepdims=True) + EPS)
                y = (o * r * gain) * (z * _sigmoid(z))
                mix_ref[:, grp * w + h * DN_D: grp * w + (h + 1) * DN_D] = y.astype(mix_ref.dtype)

    row = pl.BlockSpec((tr, w), lambda i: (i, 0))
    vec = pl.BlockSpec((1, DN_D), lambda i: (0, 0))
    return pl.pallas_call(
        body, name=name, grid=(n // tr,),
        in_specs=[row, row, pl.BlockSpec((tr, w), lambda i: (i, C_DZ // w)),
                  pl.BlockSpec((tr, w), lambda i: (i, C_GR // w)), vec, vec],
        out_specs=pl.BlockSpec((tr, 2 * w), lambda i: (i, 0)),
        out_shape=jax.ShapeDtypeStruct((n, 2 * w), BF16),
        compiler_params=_cp(VMEM_BIG),
    )(o_dn, o_gla, projp, projp, g_dn, g_gla)


def _gnorm_bwd(dmix, o_dn, o_gla, projp, g_dn, g_gla, *, tr, name):
    n = o_dn.shape[0]
    w = NH * DN_D

    def body(dm_ref, odn_ref, ogl_ref, z_ref, r_ref, gdn_ref, ggl_ref,
             dodn_ref, ddz_ref, dogl_ref, dgr_ref, dgdn_ref, dggl_ref):
        first = pl.program_id(0) == 0
        groups = ((odn_ref, z_ref, gdn_ref, dodn_ref, ddz_ref, dgdn_ref),
                  (ogl_ref, r_ref, ggl_ref, dogl_ref, dgr_ref, dggl_ref))
        for grp, (o_ref, gate_ref, gain_ref, do_ref, dgate_ref, dgain_ref) in enumerate(groups):
            gain = gain_ref[...]
            dgain = jnp.zeros((1, DN_D), F32)
            for h in range(NH):
                sl = slice(h * DN_D, (h + 1) * DN_D)
                o = o_ref[:, sl].astype(F32)
                z = gate_ref[:, sl].astype(F32)
                dm = dm_ref[:, grp * w + h * DN_D: grp * w + (h + 1) * DN_D].astype(F32)
                r = lax.rsqrt(jnp.mean(o * o, axis=-1, keepdims=True) + EPS)
                oh = o * r
                s = _sigmoid(z)
                dn = dm * (z * s)
                dgate_ref[:, sl] = (dm * (oh * gain) * (s * (1.0 + z * (1.0 - s)))).astype(dgate_ref.dtype)
                gdn = dn * gain
                do_ref[:, sl] = (r * (gdn - oh * jnp.mean(oh * gdn, axis=-1, keepdims=True))).astype(do_ref.dtype)
                dgain = dgain + jnp.sum(dn * oh, axis=0, keepdims=True)

            @pl.when(first)
            def _():
                dgain_ref[...] = dgain

            @pl.when(jnp.logical_not(first))
            def _():
                dgain_ref[...] += dgain

    row = pl.BlockSpec((tr, w), lambda i: (i, 0))
    vec = pl.BlockSpec((1, DN_D), lambda i: (0, 0))
    big = jax.ShapeDtypeStruct((n, w), F32)
    gate = jax.ShapeDtypeStruct((n, w), BF16)
    small = jax.ShapeDtypeStruct((1, DN_D), F32)
    return pl.pallas_call(
        body, name=name, grid=(n // tr,),
        in_specs=[pl.BlockSpec((tr, 2 * w), lambda i: (i, 0)), row, row,
                  pl.BlockSpec((tr, w), lambda i: (i, C_DZ // w)), pl.BlockSpec((tr, w), lambda i: (i, C_GR // w)), vec, vec],
        out_specs=[row, row, row, row, vec, vec],
        out_shape=[gate, gate, gate, gate, small, small],
        compiler_params=_cp(VMEM_BIG),
    )(dmix, o_dn, o_gla, projp, projp, g_dn, g_gla)


QKV_W = 3 * NH * DN_D
HALO = 8


def _conv_z(xs_ref, cw_ref, tt):
    z = cw_ref[0:1, :] * xs_ref[pl.ds(HALO - 3, tt), :]
    for j in range(1, 4):
        z = z + cw_ref[j:j + 1, :] * xs_ref[pl.ds(HALO - 3 + j, tt), :]
    return z


def _dnprep_fwd(projp, conv_w, *, bsz, t_seq, tt, name):
    n = bsz * t_seq
    per_seq = t_seq // tt
    hw = NH * DN_D

    def body(x_ref, halo_ref, cw_ref, q_ref, k_ref, v_ref, xs_ref):
        i = pl.program_id(1)
        xs_ref[0:HALO, :] = jnp.where(i == 0, 0.0, halo_ref[...].astype(F32))
        xs_ref[HALO:HALO + tt, :] = x_ref[...].astype(F32)
        z = _conv_z(xs_ref, cw_ref, tt)
        a = z * _sigmoid(z)
        for grp, o_ref in enumerate((q_ref, k_ref)):
            for h in range(NH):
                ah = a[:, grp * hw + h * DN_D: grp * hw + (h + 1) * DN_D]
                rs = lax.rsqrt(jnp.sum(ah * ah, axis=-1, keepdims=True) + EPS)
                o_ref[:, h * DN_D:(h + 1) * DN_D] = (ah * rs).astype(o_ref.dtype)
        v_ref[...] = a[:, 2 * hw:3 * hw].astype(v_ref.dtype)

    def halo_map(b, i):
        return (jnp.maximum((b * t_seq + i * tt) // HALO - 1, 0), 0)

    out = pl.BlockSpec((tt, hw), lambda b, i: (b * per_seq + i, 0))
    sds = jax.ShapeDtypeStruct((n, hw), BF16)
    return pl.pallas_call(
        body, name=name, grid=(bsz, per_seq),
        in_specs=[pl.BlockSpec((tt, QKV_W), lambda b, i: (b * per_seq + i, 0)),
                  pl.BlockSpec((HALO, QKV_W), halo_map),
                  pl.BlockSpec((4, QKV_W), lambda b, i: (0, 0))],
        out_specs=[out, out, out], out_shape=[sds, sds, sds],
        scratch_shapes=[pltpu.VMEM((tt + HALO, QKV_W), F32)],
        compiler_params=_cp(VMEM_BIG),
    )(projp, projp, conv_w)


def _dnprep_bwd_a(projp, conv_w, dq, dk, dv, *, bsz, t_seq, tt, name):
    n = bsz * t_seq
    per_seq = t_seq // tt
    hw = NH * DN_D

    def body(x_ref, halo_ref, cw_ref, dq_ref, dk_ref, dv_ref, dz_ref, dcw_ref, xs_ref):
        b, i = pl.program_id(0), pl.program_id(1)
        xs_ref[0:HALO, :] = jnp.where(i == 0, 0.0, halo_ref[...].astype(F32))
        xs_ref[HALO:HALO + tt, :] = x_ref[...].astype(F32)
        z = _conv_z(xs_ref, cw_ref, tt)
        s = _sigmoid(z)
        a = z * s
        dsilu = s * (1.0 + z * (1.0 - s))
        dzs = []
        for grp, d_ref in enumerate((dq_ref, dk_ref)):
            for h in range(NH):
                sl = slice(grp * hw + h * DN_D, grp * hw + (h + 1) * DN_D)
                ah = a[:, sl]
                rs = lax.rsqrt(jnp.sum(ah * ah, axis=-1, keepdims=True) + EPS)
                y = ah * rs
                dy = d_ref[:, h * DN_D:(h + 1) * DN_D].astype(F32)
                da = rs * (dy - y * jnp.sum(dy * y, axis=-1, keepdims=True))
                dzs.append(da * dsilu[:, sl])
        dzs.append(dv_ref[...].astype(F32) * dsilu[:, 2 * hw:3 * hw])
        dz = jnp.concatenate(dzs, axis=1)
        dz_ref[...] = dz.astype(dz_ref.dtype)
        first = jnp.logical_and(b == 0, i == 0)
        for j in range(4):
            part = jnp.sum(dz * xs_ref[pl.ds(HALO - 3 + j, tt), :], axis=0, keepdims=True)

            @pl.when(first)
            def _():
                dcw_ref[j:j + 1, :] = part

            @pl.when(jnp.logical_not(first))
            def _():
                dcw_ref[j:j + 1, :] += part

    def halo_map(b, i):
        return (jnp.maximum((b * t_seq + i * tt) // HALO - 1, 0), 0)

    hrow = pl.BlockSpec((tt, hw), lambda b, i: (b * per_seq + i, 0))
    return pl.pallas_call(
        body, name=name, grid=(bsz, per_seq),
        in_specs=[pl.BlockSpec((tt, QKV_W), lambda b, i: (b * per_seq + i, 0)),
                  pl.BlockSpec((HALO, QKV_W), halo_map),
                  pl.BlockSpec((4, QKV_W), lambda b, i: (0, 0)), hrow, hrow, hrow],
        out_specs=[pl.BlockSpec((tt, QKV_W), lambda b, i: (b * per_seq + i, 0)),
                   pl.BlockSpec((4, QKV_W), lambda b, i: (0, 0))],
        out_shape=[jax.ShapeDtypeStruct((n, QKV_W), BF16), jax.ShapeDtypeStruct((4, QKV_W), F32)],
        scratch_shapes=[pltpu.VMEM((tt + HALO, QKV_W), F32)],
        compiler_params=_cp(VMEM_BIG),
    )(projp, projp, conv_w, dq, dk, dv)


def _dnprep_bwd_b(dz, conv_w, *, bsz, t_seq, tt, name):
    n = bsz * t_seq
    per_seq = t_seq // tt
    last_blk = n // HALO - 1

    def body(dz_ref, halo_ref, cw_ref, dx_ref, ds_ref):
        i = pl.program_id(1)
        ds_ref[0:tt, :] = dz_ref[...].astype(F32)
        ds_ref[tt:tt + HALO, :] = jnp.where(i == per_seq - 1, 0.0, halo_ref[...].astype(F32))
        dx = cw_ref[0:1, :] * ds_ref[pl.ds(3, tt), :]
        for j in range(1, 4):
            dx = dx + cw_ref[j:j + 1, :] * ds_ref[pl.ds(3 - j, tt), :]
        dx_ref[...] = dx.astype(dx_ref.dtype)

    def halo_map(b, i):
        return (jnp.minimum((b * t_seq + (i + 1) * tt) // HALO, last_blk), 0)

    row = pl.BlockSpec((tt, QKV_W), lambda b, i: (b * per_seq + i, 0))
    return pl.pallas_call(
        body, name=name, grid=(bsz, per_seq),
        in_specs=[row, pl.BlockSpec((HALO, QKV_W), halo_map), pl.BlockSpec((4, QKV_W), lambda b, i: (0, 0))],
        out_specs=row, out_shape=jax.ShapeDtypeStruct((n, QKV_W), BF16),
        scratch_shapes=[pltpu.VMEM((tt + HALO, QKV_W), F32)],
        compiler_params=_cp(VMEM_BIG),
    )(dz, dz, conv_w)


def _masks64():
    r = _iota2((CHUNK, CHUNK), 0)
    c = _iota2((CHUNK, CHUNK), 1)
    return r, c


def _group(nc_seq, target=5):
    return max(g for g in range(1, target + 1) if nc_seq % g == 0)


def _round_robin(chains):
    live = list(chains)
    while live:
        nxt = []
        for ch in live:
            try:
                next(ch)
                nxt.append(ch)
            except StopIteration:
                pass
        live = nxt
        yield


def _run(chains):
    for _ in _round_robin(chains):
        pass


def _per_chunk(inner, kinds, grp):
    def body(*refs):
        chains = []
        for gi in range(grp):
            views = []
            for r, kind in zip(refs, kinds):
                if kind == "row":
                    views.append(r.at[pl.ds(gi * CHUNK, CHUNK)])
                elif kind == "lead":
                    views.append(r.at[pl.ds(gi, 1)])
                else:
                    views.append(r)
            chains.append(inner(gi, *views))
        _run(chains)
    return body


def _accumulate(ref, val, gi):
    if gi > 0:
        ref[...] += val
        return
    first = pl.program_id(0) == 0

    @pl.when(first)
    def _():
        ref[...] = val

    @pl.when(jnp.logical_not(first))
    def _():
        ref[...] += val


ANY = pl.BlockSpec(memory_space=pl.ANY)


def _place():
    return lax.axis_index("x"), lax.axis_index("y"), lax.axis_index("c")


def _other_chips(x, y):
    return [(1 - x, y, 2 * (1 - x) + y), (x, 1 - y, 2 * x + 1 - y), (1 - x, 1 - y, 2 * (1 - x) + 1 - y)]


class _GatherRider:
    def __init__(self, bufs, split):
        self.inputs = list(bufs)
        self.split = list(split)
        self.out_shapes = [jax.ShapeDtypeStruct(b.shape, b.dtype) for b in bufs]
        self.aliases = {i: i for i in range(len(bufs))}
        self.sems = [pltpu.SemaphoreType.DMA((len(bufs), 3))] * 4

    def _rows(self, k, buf, c, mine=True):
        r = buf.shape[1]
        if not self.split[k]:
            return pl.ds(0, r)
        return pl.ds((c if mine else 1 - c) * (r // 2), r // 2)

    def _ici(self, k, d, bufs, sems, c, px, py, block):
        rows = self._rows(k, bufs[k], c)
        return pltpu.make_async_remote_copy(
            src_ref=bufs[k].at[block, rows, :], dst_ref=bufs[k].at[block, rows, :], send_sem=sems[0].at[k, d],
            recv_sem=sems[1].at[k, d], device_id=(px, py, c), device_id_type=MESH)

    def _pass(self, k, d, bufs, sems, x, y, c, block, mine):
        rows = self._rows(k, bufs[k], c, mine)
        return pltpu.make_async_remote_copy(
            src_ref=bufs[k].at[block, rows, :], dst_ref=bufs[k].at[block, rows, :], send_sem=sems[2].at[k, d],
            recv_sem=sems[3].at[k, d], device_id=(x, y, 1 - c), device_id_type=MESH)

    def first(self, in_refs, bufs, sems):
        x, y, c = _place()
        for k in range(len(bufs)):
            for d, (px, py, _) in enumerate(_other_chips(x, y)):
                self._ici(k, d, bufs, sems, c, px, py, 2 * x + y).start()

    def last(self, in_refs, bufs, sems):
        x, y, c = _place()
        chips = _other_chips(x, y)
        for k in range(len(bufs)):
            for d, (px, py, pj) in enumerate(chips):
                self._ici(k, d, bufs, sems, c, px, py, pj).wait_recv()
                if self.split[k]:
                    self._pass(k, d, bufs, sems, x, y, c, pj, True).start()
        for k in range(len(bufs)):
            for d, (px, py, pj) in enumerate(chips):
                if self.split[k]:
                    self._pass(k, d, bufs, sems, x, y, c, pj, False).wait_recv()
                    self._pass(k, d, bufs, sems, x, y, c, pj, True).wait_send()
                self._ici(k, d, bufs, sems, c, px, py, 2 * x + y).wait_send()


def _hosted_call(body, rider, *, name, grid, in_specs, out_specs, out_shape, scratch_shapes, compiler_params, args):
    if rider is None:
        outs = pl.pallas_call(body, name=name, grid=grid, in_specs=in_specs, out_specs=out_specs, out_shape=out_shape,
                              scratch_shapes=scratch_shapes, compiler_params=compiler_params)(*args)
        return list(outs), []
    n_in, n_out, n_scr = len(in_specs), len(out_specs), len(scratch_shapes)
    r_in, r_out = len(rider.inputs), len(rider.out_shapes)
    compiler_params = _cp(compiler_params.vmem_limit_bytes, ("arbitrary",) * len(grid))

    def full_body(*refs):
        ins = refs[:n_in]
        rins = refs[n_in:n_in + r_in]
        outs = refs[n_in + r_in:n_in + r_in + n_out]
        routs = refs[n_in + r_in + n_out:n_in + r_in + n_out + r_out]
        rest = refs[n_in + r_in + n_out + r_out:]
        scr, sems = rest[:n_scr], rest[n_scr:]
        ids = [pl.program_id(a) for a in range(len(grid))]
        is_first = functools.reduce(jnp.logical_and, [i == 0 for i in ids])
        is_last = functools.reduce(jnp.logical_and, [i == g - 1 for i, g in zip(ids, grid)])

        @pl.when(is_first)
        def _():
            rider.first(rins, routs, sems)

        body(*ins, *outs, *scr)

        @pl.when(is_last)
        def _():
            rider.last(rins, routs, sems)

    res = pl.pallas_call(
        full_body, name=name, grid=grid, in_specs=list(in_specs) + [ANY] * r_in,
        out_specs=list(out_specs) + [ANY] * r_out, out_shape=list(out_shape) + list(rider.out_shapes),
        input_output_aliases={n_in + i: n_out + o for i, o in rider.aliases.items()},
        scratch_shapes=list(scratch_shapes) + list(rider.sems), compiler_params=compiler_params,
    )(*args, *rider.inputs)
    return list(res[:n_out]), list(res[n_out:])


def _exchange_now(rider, *, name):
    r_in = len(rider.inputs)

    def body(*refs):
        rins = refs[:r_in]
        routs = refs[r_in:r_in + len(rider.out_shapes)]
        sems = refs[r_in + len(rider.out_shapes):]
        rider.first(rins, routs, sems)
        rider.last(rins, routs, sems)

    return pl.pallas_call(
        body, name=name, in_specs=[ANY] * r_in, out_specs=[ANY] * len(rider.out_shapes), out_shape=list(rider.out_shapes),
        input_output_aliases=dict(rider.aliases), scratch_shapes=list(rider.sems),
    )(*rider.inputs)


def _to_slot(where, a, dtype, *, name):
    r, cols = a.shape
    tr = _tile(r, 256, 16) if r > 256 else r

    def body(w_ref, a_ref, o_ref):
        o_ref[0] = a_ref[...].astype(o_ref.dtype)

    return pl.pallas_call(
        body, name=name,
        grid_spec=pltpu.PrefetchScalarGridSpec(
            num_scalar_prefetch=1, grid=(r // tr,),
            in_specs=[pl.BlockSpec((tr, cols), lambda i, w: (i, 0))],
            out_specs=pl.BlockSpec((1, tr, cols), lambda i, w: (w[1], i, 0))),
        out_shape=jax.ShapeDtypeStruct((N_CHIPS, r, cols), dtype), compiler_params=_cp(VMEM_BIG),
    )(where, a)


def _tri_inv(a_strict):
    r, c = _masks64()
    eye = (r == c).astype(F32)
    blk16 = (r // 16) == (c // 16)
    blk32 = (r // 32) == (c // 32)
    ld = jnp.where(blk16, a_strict, 0.0)
    x = eye - ld
    p = _nn(ld, ld)
    yield
    for step in range(3):
        xp = _nn(x, p)
        if step < 2:
            p = _nn(p, p)
        x = x + xp
        yield
    for lk in (jnp.where(jnp.logical_and(blk32, jnp.logical_not(blk16)), a_strict, 0.0),
               jnp.where(blk32, 0.0, a_strict)):
        y = x - eye
        s = lk + _nn(y, lk)
        yield
        x = x - s - _nn(s, y)
        yield
    return x


def _dn_gates(sa, alog, dtb, chunk_in_seq):
    rows = _iota2((CHUNK, LANE), 0)
    valid = jnp.logical_or(rows >= N_PAD, chunk_in_seq > 0)
    beta_t = _sigmoid(sa)
    ea = jnp.exp(alog)
    g_t = jnp.where(valid, -ea * _softplus(sa + dtb), 0.0)
    r, c = _masks64()
    ltri = (r >= c).astype(F32)
    gam_t = _nn_hi(ltri, g_t)
    return beta_t, g_t, gam_t, valid, ea


def _dn_intra_fwd(qn, kn, v, projp, alog_row, dtb_row, *, nc_seq, name):
    n = qn.shape[0]
    nct = n // CHUNK
    hw = NH * DN_D
    scale = DN_D ** -0.5

    grp = _group(nc_seq)

    def inner(gi, q_ref, k_ref, v_ref, sa_ref, al_ref, dt_ref, u_ref, w_ref, qg_ref, kd_ref, p_ref, t_ref, gl_ref):
        ci = (pl.program_id(0) * grp + gi) % nc_seq
        beta_t, _, gam_t, _, _ = _dn_gates(sa_ref[...], al_ref[...], dt_ref[...], ci)
        yield
        gam_tt = gam_t.T
        r, c = _masks64()
        incl = r >= c
        strict = r > c

        def head(h):
            sl = slice(h * DN_D, (h + 1) * DN_D)
            beta = beta_t[:, h:h + 1]
            gam = gam_t[:, 4 + h:5 + h]
            gam_row = gam_tt[4 + h:5 + h, :]
            gl = gam_t[CHUNK - 1:CHUNK, 4 + h:5 + h]
            dec = jnp.exp(jnp.where(incl, gam - gam_row, -jnp.inf))
            kh = k_ref[:, sl].astype(F32)
            qh = q_ref[:, sl].astype(F32) * scale
            vh = v_ref[:, sl].astype(F32)
            kk = _nt(kh, kh)
            qk = _nt(qh, kh)
            yield
            a = jnp.where(strict, beta * kk * dec, 0.0)
            tm = yield from _tri_inv(a)
            egam = jnp.exp(gam)
            u_ref[:, sl] = _nn(tm, beta * vh).astype(u_ref.dtype)
            w_ref[:, sl] = _nn(tm, (beta * egam) * kh).astype(w_ref.dtype)
            qg_ref[:, sl] = (egam * qh).astype(qg_ref.dtype)
            kd_ref[:, sl] = (jnp.exp(gl - gam) * kh).astype(kd_ref.dtype)
            p_ref[0, h] = qk * dec
            t_ref[0, h] = tm
            gl_ref[0, h:h + 1, :] = jnp.broadcast_to(jnp.exp(gl), (1, LANE))

        yield from _round_robin([head(h) for h in range(NH)])

    rows = grp * CHUNK
    row = pl.BlockSpec((rows, hw), lambda i: (i, 0))
    vec = pl.BlockSpec((1, LANE), lambda i: (0, 0))
    mat = pl.BlockSpec((grp, NH, CHUNK, CHUNK), lambda i: (i, 0, 0, 0))
    big = jax.ShapeDtypeStruct((n, hw), BF16)
    msd = jax.ShapeDtypeStruct((nct, NH, CHUNK, CHUNK), F32)
    kinds = ["row"] * 4 + ["whole"] * 2 + ["row"] * 4 + ["lead"] * 3
    return pl.pallas_call(
        _per_chunk(inner, kinds, grp), name=name, grid=(nct // grp,),
        in_specs=[row, row, row, pl.BlockSpec((rows, LANE), lambda i: (i, 0)), vec, vec],
        out_specs=[row, row, row, row, mat, mat, pl.BlockSpec((grp, NH, LANE), lambda i: (i, 0, 0))],
        out_shape=[big, big, big, big, msd, msd, jax.ShapeDtypeStruct((nct, NH, LANE), F32)],
        compiler_params=_cp(VMEM_BIG),
    )(qn, kn, v, projp, alog_row, dtb_row)


def _dn_scan_fwd(u, w, qg, kd, p, gl, *, bsz, nc_seq, name, rider=None):
    hw = NH * DN_D
    t_seq = nc_seq * CHUNK
    u, w, qg, kd = (z.reshape(bsz, t_seq, hw) for z in (u, w, qg, kd))
    p = p.reshape(bsz, nc_seq, NH, CHUNK, CHUNK)
    gl = gl.reshape(bsz, nc_seq, NH, LANE)

    def body(u_ref, w_ref, qg_ref, kd_ref, p_ref, gl_ref, o_ref, vn_ref, hist_ref, s_ref):
        @pl.when(pl.program_id(0) == 0)
        def _():
            s_ref[...] = jnp.zeros_like(s_ref)

        def chain(b, h):
            sl = slice(h * DN_D, (h + 1) * DN_D)
            s = s_ref[b, h]
            hist_ref[b, 0, h] = s.astype(hist_ref.dtype)
            ws = _nn(w_ref[b, :, sl], s)
            qs = _nn(qg_ref[b, :, sl], s)
            yield
            vn = u_ref[b, :, sl] - ws
            vn_ref[b, :, sl] = vn.astype(vn_ref.dtype)
            o_ref[b, :, sl] = (qs + _nn(p_ref[b, 0, h], vn)).astype(o_ref.dtype)
            s_ref[b, h] = gl_ref[b, 0, h:h + 1, :] * s + _tn(kd_ref[b, :, sl], vn)

        _run([chain(b, h) for b in range(bsz) for h in range(NH)])

    row = pl.BlockSpec((bsz, CHUNK, hw), lambda i: (0, i, 0))
    outs, ridden = _hosted_call(
        body, rider, name=name, grid=(nc_seq,),
        in_specs=[row, row, row, row, pl.BlockSpec((bsz, 1, NH, CHUNK, CHUNK), lambda i: (0, i, 0, 0, 0)),
                  pl.BlockSpec((bsz, 1, NH, LANE), lambda i: (0, i, 0, 0))],
        out_specs=[row, row, pl.BlockSpec((bsz, 1, NH, DN_D, DN_D), lambda i: (0, i, 0, 0, 0))],
        out_shape=[jax.ShapeDtypeStruct((bsz, t_seq, hw), BF16), jax.ShapeDtypeStruct((bsz, t_seq, hw), BF16),
                   jax.ShapeDtypeStruct((bsz, nc_seq, NH, DN_D, DN_D), BF16)],
        scratch_shapes=[pltpu.VMEM((bsz, NH, DN_D, DN_D), F32)],
        compiler_params=_cp(VMEM_BIG, ("arbitrary",)), args=(u, w, qg, kd, p, gl))
    o, vn, hist = outs
    return o.reshape(bsz * t_seq, hw), vn.reshape(bsz * t_seq, hw), hist, ridden


def _dn_scan_bwd(do, w, qg, kd, vn, p, gl, hist, *, bsz, nc_seq, name):
    hw = NH * DN_D
    t_seq = nc_seq * CHUNK
    do, w, qg, kd, vn = (z.reshape(bsz, t_seq, hw) for z in (do, w, qg, kd, vn))
    p = p.reshape(bsz, nc_seq, NH, CHUNK, CHUNK)
    gl = gl.reshape(bsz, nc_seq, NH, LANE)

    def body(do_ref, w_ref, qg_ref, kd_ref, vn_ref, p_ref, gl_ref, hist_ref,
             du_ref, dw_ref, dqg_ref, dkd_ref, dgl_ref, ds_ref):
        @pl.when(pl.program_id(0) == 0)
        def _():
            ds_ref[...] = jnp.zeros_like(ds_ref)

        def chain(b, h):
            sl = slice(h * DN_D, (h + 1) * DN_D)
            s = hist_ref[b, 0, h]
            dsn = ds_ref[b, h]
            doh = do_ref[b, :, sl]
            vnh = vn_ref[b, :, sl]
            kdh = kd_ref[b, :, sl]
            dvn = _tn(p_ref[b, 0, h], doh) + _nn(kdh, dsn)
            du_ref[b, :, sl] = dvn.astype(du_ref.dtype)
            dqg_ref[b, :, sl] = _nt(doh, s).astype(dqg_ref.dtype)
            dkd_ref[b, :, sl] = _nt(vnh, dsn).astype(dkd_ref.dtype)
            ds_part = _tn(qg_ref[b, :, sl], doh) + gl_ref[b, 0, h:h + 1, :] * dsn
            dgl = jnp.sum(jnp.sum(dsn * s, axis=0, keepdims=True), axis=1, keepdims=True)
            dgl_ref[b, 0, h:h + 1, :] = jnp.broadcast_to(dgl, (1, LANE))
            yield
            dw_ref[b, :, sl] = (-_nt(dvn, s)).astype(dw_ref.dtype)
            ds_ref[b, h] = ds_part - _tn(w_ref[b, :, sl], dvn)

        _run([chain(b, h) for b in range(bsz) for h in range(NH)])

    rev = lambda i: nc_seq - 1 - i
    row = pl.BlockSpec((bsz, CHUNK, hw), lambda i: (0, rev(i), 0))
    mat = pl.BlockSpec((bsz, 1, NH, CHUNK, CHUNK), lambda i: (0, rev(i), 0, 0, 0))
    glb = pl.BlockSpec((bsz, 1, NH, LANE), lambda i: (0, rev(i), 0, 0))
    big = jax.ShapeDtypeStruct((bsz, t_seq, hw), BF16)
    outs = pl.pallas_call(
        body, name=name, grid=(nc_seq,),
        in_specs=[row, row, row, row, row, mat, glb,
                  pl.BlockSpec((bsz, 1, NH, DN_D, DN_D), lambda i: (0, rev(i), 0, 0, 0))],
        out_specs=[row, row, row, row, glb],
        out_shape=[big, big, big, big, jax.ShapeDtypeStruct((bsz, nc_seq, NH, LANE), F32)],
        scratch_shapes=[pltpu.VMEM((bsz, NH, DN_D, DN_D), F32)],
        compiler_params=_cp(VMEM_BIG, ("arbitrary",)),
    )(do, w, qg, kd, vn, p, gl, hist)
    du, dw, dqg, dkd, dgl = outs
    n = bsz * t_seq
    return (du.reshape(n, hw), dw.reshape(n, hw), dqg.reshape(n, hw), dkd.reshape(n, hw),
            dgl.reshape(bsz * nc_seq, NH, LANE))


def _dn_intra_bwd(qn, kn, v, projp, alog_row, dtb_row, u, w, tmat, du, dw, dqg, dkd, do, vn, dgl, *, nc_seq, name,
                  rider=None):
    n = qn.shape[0]
    nct = n // CHUNK
    hw = NH * DN_D
    scale = DN_D ** -0.5

    grp = _group(nc_seq)

    def inner(gi, q_ref, k_ref, v_ref, sa_ref, al_ref, dt_ref, u_ref, w_ref, t_ref, du_ref, dw_ref, dqg_ref, dkd_ref,
              do_ref, vn_ref, dgl_ref, dq_ref, dk_ref, dv_ref, dsa_ref, dal_ref, ddt_ref):
        ci = (pl.program_id(0) * grp + gi) % nc_seq
        sa = sa_ref[...]
        beta_t, g_t, gam_t, valid, ea = _dn_gates(sa, al_ref[...], dt_ref[...], ci)
        yield
        lane = _iota2((CHUNK, LANE), 1)
        gates_t = jnp.where(lane < 4, beta_t, gam_t).T
        r, c = _masks64()
        incl, strict, upper, supper = r >= c, r > c, r <= c, r < c
        rows1 = _iota2((CHUNK, 1), 0)
        acc = [jnp.zeros((CHUNK, LANE), F32)]

        def head(h):
            sl = slice(h * DN_D, (h + 1) * DN_D)
            beta = beta_t[:, h:h + 1]
            gam = gam_t[:, 4 + h:5 + h]
            beta_row = gates_t[h:h + 1, :]
            gam_row = gates_t[4 + h:5 + h, :]
            gl = gam_t[CHUNK - 1:CHUNK, 4 + h:5 + h]
            dec = jnp.exp(jnp.where(incl, gam - gam_row, -jnp.inf))
            dec_t = jnp.exp(jnp.where(upper, gam_row - gam, -jnp.inf))
            kh = k_ref[:, sl].astype(F32)
            qh = q_ref[:, sl].astype(F32) * scale
            vh = v_ref[:, sl].astype(F32)
            uh = u_ref[:, sl]
            wh = w_ref[:, sl]
            doh = do_ref[:, sl]
            vnh = vn_ref[:, sl]
            kk = _nt(kh, kh)
            qk = _nt(qh, kh)
            qk_t = _nt(kh, qh)
            dp = _nt(doh, vnh)
            dp_t = _nt(vnh, doh)
            tm_t = t_ref[0, h].T
            dvb = _nn(tm_t, du_ref[:, sl])
            dkg = _nn(tm_t, dw_ref[:, sl])
            yield
            m = _nt(dvb, uh) + _nt(dkg, wh)
            m_t = _nt(uh, dvb) + _nt(wh, dkg)
            yield
            da = jnp.where(strict, -m, 0.0)
            da_t = jnp.where(supper, -m_t, 0.0)
            a = jnp.where(strict, beta * kk * dec, 0.0)
            a_t = jnp.where(supper, beta_row * kk * dec_t, 0.0)
            dad = da * dec
            dad_t = da_t * dec_t
            dbeta = jnp.sum(dad * kk, axis=1, keepdims=True)
            dpm = jnp.where(incl, dp, 0.0)
            dpm_t = jnp.where(upper, dp_t, 0.0)
            e = da * a + dpm * (qk * dec)
            e_t = da_t * a_t + dpm_t * (qk_t * dec_t)
            dgam = jnp.sum(e, axis=1, keepdims=True) - jnp.sum(e_t, axis=1, keepdims=True)
            egam = jnp.exp(gam)
            ekd = jnp.exp(gl - gam)
            dqgh = dqg_ref[:, sl].astype(F32)
            dkdh = dkd_ref[:, sl].astype(F32)
            dkh = (_nn(beta * dad, kh) + _nn(beta_row * dad_t, kh) + _nn(dpm_t * dec_t, qh)
                   + (beta * egam) * dkg + ekd * dkdh)
            dqh = _nn(dpm * dec, kh) + egam * dqgh
            dbeta = dbeta + jnp.sum(dkg * (egam * kh), axis=1, keepdims=True) + jnp.sum(dvb * vh, axis=1, keepdims=True)
            rkd = jnp.sum(dkdh * (ekd * kh), axis=1, keepdims=True)
            dgam = (dgam + jnp.sum(dkg * ((beta * egam) * kh), axis=1, keepdims=True)
                    + jnp.sum(dqgh * (egam * qh), axis=1, keepdims=True) - rkd)
            dgam_last = jnp.sum(rkd, axis=0, keepdims=True) + dgl_ref[0, h:h + 1, 0:1] * jnp.exp(gl)
            dgam = dgam + jnp.where(rows1 == CHUNK - 1, dgam_last, 0.0)
            dq_ref[:, sl] = (dqh * scale).astype(dq_ref.dtype)
            dk_ref[:, sl] = dkh.astype(dk_ref.dtype)
            dv_ref[:, sl] = (beta * dvb).astype(dv_ref.dtype)
            acc[0] = acc[0] + jnp.where(lane == h, dbeta, 0.0) + jnp.where(lane == 4 + h, dgam, 0.0)

        yield from _round_robin([head(h) for h in range(NH)])
        acc_t = acc[0]
        dg_t = _nn_hi(upper.astype(F32), acc_t)
        ddb = acc_t * beta_t * (1.0 - beta_t)
        dda = jnp.where(valid, dg_t * (-ea) * _sigmoid(sa + dt_ref[...]), 0.0)
        dsa_ref[...] = jnp.where(lane < 4, ddb, jnp.where(lane < 8, dda, 0.0)).astype(dsa_ref.dtype)
        in_g = jnp.logical_and(lane >= 4, lane < 8)
        dal = jnp.sum(jnp.where(in_g, dg_t * g_t, 0.0), axis=0, keepdims=True)
        ddt = jnp.sum(jnp.where(in_g, dda, 0.0), axis=0, keepdims=True)
        _accumulate(dal_ref, dal, gi)
        _accumulate(ddt_ref, ddt, gi)

    rows = grp * CHUNK
    row = pl.BlockSpec((rows, hw), lambda i: (i, 0))
    vec = pl.BlockSpec((1, LANE), lambda i: (0, 0))
    mat = pl.BlockSpec((grp, NH, CHUNK, CHUNK), lambda i: (i, 0, 0, 0))
    glb = pl.BlockSpec((grp, NH, LANE), lambda i: (i, 0, 0))
    big = jax.ShapeDtypeStruct((n, hw), BF16)
    v128 = jax.ShapeDtypeStruct((1, LANE), F32)
    kinds = (["row"] * 4 + ["whole"] * 2 + ["row"] * 2 + ["lead"] + ["row"] * 6 + ["lead"]
             + ["row"] * 4 + ["whole"] * 2)
    outs, ridden = _hosted_call(
        _per_chunk(inner, kinds, grp), rider, name=name, grid=(nct // grp,),
        in_specs=[row, row, row, pl.BlockSpec((rows, LANE), lambda i: (i, 0)), vec, vec,
                  row, row, mat, row, row, row, row, row, row, glb],
        out_specs=[row, row, row, pl.BlockSpec((rows, LANE), lambda i: (i, 0)), vec, vec],
        out_shape=[big, big, big, jax.ShapeDtypeStruct((n, LANE), BF16), v128, v128],
        scratch_shapes=[], compiler_params=_cp(VMEM_BIG, ("arbitrary",)),
        args=(qn, kn, v, projp, alog_row, dtb_row, u, w, tmat, du, dw, dqg, dkd, do, vn, dgl))
    return (*outs, ridden)


GQ_W = NH * GLA_DK
GV_W = NH * GLA_DV
GLA_NORM = 16.0
MID = CHUNK // 2


def _gla_gates(sb, w2p, gb, chunk_in_seq):
    rows = _iota2((CHUNK, GQ_W), 0)
    valid = jnp.logical_or(rows >= N_PAD, chunk_in_seq > 0)
    graw = _nn_hi(sb, w2p) + gb
    yield
    g = jnp.where(valid, _logsigmoid(graw) * (1.0 / GLA_NORM), 0.0)
    r, c = _masks64()
    bcum = _nn_hi((r >= c).astype(F32), g)
    yield
    return graw, bcum, valid


def _head_mask(h):
    lane = _iota2((1, GQ_W), 1)
    return jnp.logical_and(lane >= h * GLA_DK, lane < (h + 1) * GLA_DK)


def _gla_intra_fwd(projp, gates, w2p, gb, *, nc_seq, name):
    n = projp.shape[0]
    nct = n // CHUNK
    scale = GLA_DK ** -0.5

    grp = _group(nc_seq)
    rows = grp * CHUNK

    def inner(gi, qk_ref, v_ref, sb_ref, w2_ref, gb_ref, oi_ref, qg_ref, kd_ref, gl_ref):
        ci = (pl.program_id(0) * grp + gi) % nc_seq
        _, bc, _ = yield from _gla_gates(sb_ref[...], w2_ref[...], gb_ref[...], ci)
        bref = bc[MID:MID + 1, :]
        bl = bc[CHUNK - 1:CHUNK, :]
        q = qk_ref[:, 0:GQ_W].astype(F32) * scale
        k = qk_ref[:, GQ_W:2 * GQ_W].astype(F32)
        qi = q * jnp.exp(bc - bref)
        ki = k * jnp.exp(bref - bc)
        qg_ref[...] = (q * jnp.exp(bc)).astype(qg_ref.dtype)
        kd_ref[...] = (k * jnp.exp(bl - bc)).astype(kd_ref.dtype)
        gl_ref[0] = jnp.exp(bl)
        r, c = _masks64()
        incl = r >= c
        a = [jnp.where(incl, _nt(jnp.where(_head_mask(h), qi, 0.0), ki), 0.0) for h in range(NH)]
        yield
        for h in range(NH):
            oi_ref[:, h * GLA_DV:(h + 1) * GLA_DV] = _nn(a[h], v_ref[:, h * GLA_DV:(h + 1) * GLA_DV]).astype(oi_ref.dtype)

    kinds = ["row"] * 3 + ["whole"] * 2 + ["row"] * 3 + ["lead"]
    return pl.pallas_call(
        _per_chunk(inner, kinds, grp), name=name, grid=(nct // grp,),
        in_specs=[pl.BlockSpec((rows, 2 * GQ_W), lambda i: (i, C_GQK // (2 * GQ_W))),
                  pl.BlockSpec((rows, GV_W), lambda i: (i, C_GV // GV_W)),
                  pl.BlockSpec((rows, LANE), lambda i: (i, 1)),
                  pl.BlockSpec((LANE, GQ_W), lambda i: (0, 0)), pl.BlockSpec((1, GQ_W), lambda i: (0, 0))],
        out_specs=[pl.BlockSpec((rows, GV_W), lambda i: (i, 0)), pl.BlockSpec((rows, GQ_W), lambda i: (i, 0)),
                   pl.BlockSpec((rows, GQ_W), lambda i: (i, 0)), pl.BlockSpec((grp, 1, GQ_W), lambda i: (i, 0, 0))],
        out_shape=[jax.ShapeDtypeStruct((n, GV_W), BF16), jax.ShapeDtypeStruct((n, GQ_W), BF16),
                   jax.ShapeDtypeStruct((n, GQ_W), BF16), jax.ShapeDtypeStruct((nct, 1, GQ_W), F32)],
        compiler_params=_cp(VMEM_BIG),
    )(projp, projp, gates, w2p, gb)


def _gla_scan_fwd(oi, qg, kd, gl, projp, *, bsz, nc_seq, name, rider=None):
    t_seq = nc_seq * CHUNK
    oi = oi.reshape(bsz, t_seq, GV_W)
    qg, kd = qg.reshape(bsz, t_seq, GQ_W), kd.reshape(bsz, t_seq, GQ_W)
    gl = gl.reshape(bsz, nc_seq, 1, GQ_W)
    pj = projp.reshape(bsz, t_seq, PW)

    def body(oi_ref, qg_ref, kd_ref, gl_ref, v_ref, o_ref, hist_ref, st_ref):
        @pl.when(pl.program_id(0) == 0)
        def _():
            st_ref[...] = jnp.zeros_like(st_ref)

        for b in range(bsz):
            st = st_ref[b]
            hist_ref[b, 0] = st.astype(hist_ref.dtype)
            qgb = qg_ref[b]
            kdb = kd_ref[b]
            upd = jnp.zeros((GLA_DV, GQ_W), F32)
            for h in range(NH):
                sl = slice(h * GLA_DV, (h + 1) * GLA_DV)
                m = _head_mask(h)
                o_ref[b, :, sl] = (oi_ref[b, :, sl] + _nt(jnp.where(m, qgb, 0.0), st)).astype(o_ref.dtype)
                upd = upd + jnp.where(m, _tn(v_ref[b, :, sl], kdb), 0.0)
            st_ref[b] = gl_ref[b, 0] * st + upd

    outs, ridden = _hosted_call(
        body, rider, name=name, grid=(nc_seq,),
        in_specs=[pl.BlockSpec((bsz, CHUNK, GV_W), lambda i: (0, i, 0)),
                  pl.BlockSpec((bsz, CHUNK, GQ_W), lambda i: (0, i, 0)),
                  pl.BlockSpec((bsz, CHUNK, GQ_W), lambda i: (0, i, 0)),
                  pl.BlockSpec((bsz, 1, 1, GQ_W), lambda i: (0, i, 0, 0)),
                  pl.BlockSpec((bsz, CHUNK, GV_W), lambda i: (0, i, C_GV // GV_W))],
        out_specs=[pl.BlockSpec((bsz, CHUNK, GV_W), lambda i: (0, i, 0)),
                   pl.BlockSpec((bsz, 1, GLA_DV, GQ_W), lambda i: (0, i, 0, 0))],
        out_shape=[jax.ShapeDtypeStruct((bsz, t_seq, GV_W), BF16),
                   jax.ShapeDtypeStruct((bsz, nc_seq, GLA_DV, GQ_W), BF16)],
        scratch_shapes=[pltpu.VMEM((bsz, GLA_DV, GQ_W), F32)],
        compiler_params=_cp(VMEM_BIG, ("arbitrary",)), args=(oi, qg, kd, gl, pj))
    return outs[0].reshape(bsz * t_seq, GV_W), outs[1], ridden


def _gla_scan_bwd(do, qg, kd, gl, projp, hist, *, bsz, nc_seq, name):
    t_seq = nc_seq * CHUNK
    do = do.reshape(bsz, t_seq, GV_W)
    qg, kd = qg.reshape(bsz, t_seq, GQ_W), kd.reshape(bsz, t_seq, GQ_W)
    gl = gl.reshape(bsz, nc_seq, 1, GQ_W)
    pj = projp.reshape(bsz, t_seq, PW)

    def body(do_ref, qg_ref, kd_ref, gl_ref, v_ref, hist_ref, dqg_ref, dkd_ref, dv_ref, dgl_ref, dst_ref):
        @pl.when(pl.program_id(0) == 0)
        def _():
            dst_ref[...] = jnp.zeros_like(dst_ref)

        for b in range(bsz):
            st = hist_ref[b, 0]
            dst = dst_ref[b]
            qgb = qg_ref[b]
            kdb = kd_ref[b]
            dqg = jnp.zeros((CHUNK, GQ_W), F32)
            dkd = jnp.zeros((CHUNK, GQ_W), F32)
            add = jnp.zeros((GLA_DV, GQ_W), F32)
            for h in range(NH):
                sl = slice(h * GLA_DV, (h + 1) * GLA_DV)
                m = _head_mask(h)
                doh = do_ref[b, :, sl]
                vh = v_ref[b, :, sl]
                dqg = dqg + jnp.where(m, _nn(doh, st), 0.0)
                dkd = dkd + jnp.where(m, _nn(vh, dst), 0.0)
                dv_ref[b, :, sl] = _nt(jnp.where(m, kdb, 0.0), dst).astype(dv_ref.dtype)
                add = add + jnp.where(m, _tn(doh, qgb), 0.0)
            dqg_ref[b] = dqg.astype(dqg_ref.dtype)
            dkd_ref[b] = dkd.astype(dkd_ref.dtype)
            dgl_ref[b, 0] = jnp.sum(dst * st, axis=0, keepdims=True)
            dst_ref[b] = gl_ref[b, 0] * dst + add

    rev = lambda i: nc_seq - 1 - i
    outs = pl.pallas_call(
        body, name=name, grid=(nc_seq,),
        in_specs=[pl.BlockSpec((bsz, CHUNK, GV_W), lambda i: (0, rev(i), 0)),
                  pl.BlockSpec((bsz, CHUNK, GQ_W), lambda i: (0, rev(i), 0)),
                  pl.BlockSpec((bsz, CHUNK, GQ_W), lambda i: (0, rev(i), 0)),
                  pl.BlockSpec((bsz, 1, 1, GQ_W), lambda i: (0, rev(i), 0, 0)),
                  pl.BlockSpec((bsz, CHUNK, GV_W), lambda i: (0, rev(i), C_GV // GV_W)),
                  pl.BlockSpec((bsz, 1, GLA_DV, GQ_W), lambda i: (0, rev(i), 0, 0))],
        out_specs=[pl.BlockSpec((bsz, CHUNK, GQ_W), lambda i: (0, rev(i), 0)),
                   pl.BlockSpec((bsz, CHUNK, GQ_W), lambda i: (0, rev(i), 0)),
                   pl.BlockSpec((bsz, CHUNK, GV_W), lambda i: (0, rev(i), 0)),
                   pl.BlockSpec((bsz, 1, 1, GQ_W), lambda i: (0, rev(i), 0, 0))],
        out_shape=[jax.ShapeDtypeStruct((bsz, t_seq, GQ_W), BF16), jax.ShapeDtypeStruct((bsz, t_seq, GQ_W), BF16),
                   jax.ShapeDtypeStruct((bsz, t_seq, GV_W), BF16), jax.ShapeDtypeStruct((bsz, nc_seq, 1, GQ_W), F32)],
        scratch_shapes=[pltpu.VMEM((bsz, GLA_DV, GQ_W), F32)],
        compiler_params=_cp(VMEM_BIG, ("arbitrary",)),
    )(do, qg, kd, gl, pj, hist)
    n = bsz * t_seq
    return (outs[0].reshape(n, GQ_W), outs[1].reshape(n, GQ_W), outs[2].reshape(n, GV_W),
            outs[3].reshape(bsz * nc_seq, 1, GQ_W))


def _gla_intra_bwd(projp, gates, w2p, gb, do, dqg, dkd, dvi, dgl, *, nc_seq, name):
    n = projp.shape[0]
    nct = n // CHUNK
    scale = GLA_DK ** -0.5

    grp = _group(nc_seq)
    rows = grp * CHUNK

    def inner(gi, qk_ref, v_ref, sb_ref, w2_ref, gb_ref, do_ref, dqg_ref, dkd_ref, dvi_ref, dgl_ref,
              dqk_ref, dv_ref, dsb_ref, dw2_ref, dgb_ref):
        ci = (pl.program_id(0) * grp + gi) % nc_seq
        sb = sb_ref[...]
        w2 = w2_ref[...]
        graw, bc, valid = yield from _gla_gates(sb, w2, gb_ref[...], ci)
        bref = bc[MID:MID + 1, :]
        bl = bc[CHUNK - 1:CHUNK, :]
        q = qk_ref[:, 0:GQ_W].astype(F32) * scale
        k = qk_ref[:, GQ_W:2 * GQ_W].astype(F32)
        ex1 = jnp.exp(bc - bref)
        ex2 = jnp.exp(bref - bc)
        eb = jnp.exp(bc)
        ekd = jnp.exp(bl - bc)
        qi, ki = q * ex1, k * ex2
        r, c = _masks64()
        incl = r >= c
        upper = r <= c
        a_t, da, da_t = [], [], []
        for h in range(NH):
            sl = slice(h * GLA_DV, (h + 1) * GLA_DV)
            doh = do_ref[:, sl]
            vh = v_ref[:, sl]
            a_t.append(jnp.where(upper, _nt(jnp.where(_head_mask(h), ki, 0.0), qi), 0.0))
            da.append(jnp.where(incl, _nt(doh, vh), 0.0))
            da_t.append(jnp.where(upper, _nt(vh, doh), 0.0))
        yield
        dqi = jnp.zeros((CHUNK, GQ_W), F32)
        dki = jnp.zeros((CHUNK, GQ_W), F32)
        for h in range(NH):
            sl = slice(h * GLA_DV, (h + 1) * GLA_DV)
            m = _head_mask(h)
            dv_ref[:, sl] = (_nn(a_t[h], do_ref[:, sl]) + dvi_ref[:, sl]).astype(dv_ref.dtype)
            dqi = dqi + jnp.where(m, _nn(da[h], ki), 0.0)
            dki = dki + jnp.where(m, _nn(da_t[h], qi), 0.0)
        yield
        dqg = dqg_ref[...].astype(F32)
        dkd = dkd_ref[...].astype(F32)
        dqk_ref[:, 0:GQ_W] = ((dqi * ex1 + dqg * eb) * scale).astype(dqk_ref.dtype)
        dqk_ref[:, GQ_W:2 * GQ_W] = (dki * ex2 + dkd * ekd).astype(dqk_ref.dtype)
        t_qi, t_ki, t_kd = dqi * qi, dki * ki, dkd * (k * ekd)
        db = t_qi - t_ki + dqg * (q * eb) - t_kd
        dbref = jnp.sum(t_ki - t_qi, axis=0, keepdims=True)
        dbl = jnp.sum(t_kd, axis=0, keepdims=True) + dgl_ref[0] * jnp.exp(bl)
        rows = _iota2((CHUNK, GQ_W), 0)
        db = db + jnp.where(rows == MID, dbref, 0.0) + jnp.where(rows == CHUNK - 1, dbl, 0.0)
        dg = _nn_hi(upper.astype(F32), db)
        yield
        dgraw = jnp.where(valid, dg * (1.0 / GLA_NORM) * _sigmoid(-graw), 0.0)
        dsb_ref[...] = _nt_hi(dgraw, w2).astype(dsb_ref.dtype)
        dw2 = _tn_hi(sb, dgraw)
        dgb = jnp.sum(dgraw, axis=0, keepdims=True)
        _accumulate(dw2_ref, dw2, gi)
        _accumulate(dgb_ref, dgb, gi)

    rq = pl.BlockSpec((rows, GQ_W), lambda i: (i, 0))
    rv = pl.BlockSpec((rows, GV_W), lambda i: (i, 0))
    kinds = ["row"] * 3 + ["whole"] * 2 + ["row"] * 4 + ["lead"] + ["row"] * 3 + ["whole"] * 2
    return pl.pallas_call(
        _per_chunk(inner, kinds, grp), name=name, grid=(nct // grp,),
        in_specs=[pl.BlockSpec((rows, 2 * GQ_W), lambda i: (i, C_GQK // (2 * GQ_W))),
                  pl.BlockSpec((rows, GV_W), lambda i: (i, C_GV // GV_W)),
                  pl.BlockSpec((rows, LANE), lambda i: (i, 1)),
                  pl.BlockSpec((LANE, GQ_W), lambda i: (0, 0)), pl.BlockSpec((1, GQ_W), lambda i: (0, 0)),
                  rv, rq, rq, rv, pl.BlockSpec((grp, 1, GQ_W), lambda i: (i, 0, 0))],
        out_specs=[pl.BlockSpec((rows, 2 * GQ_W), lambda i: (i, 0)), rv, pl.BlockSpec((rows, LANE), lambda i: (i, 0)),
                   pl.BlockSpec((LANE, GQ_W), lambda i: (0, 0)), pl.BlockSpec((1, GQ_W), lambda i: (0, 0))],
        out_shape=[jax.ShapeDtypeStruct((n, 2 * GQ_W), BF16), jax.ShapeDtypeStruct((n, GV_W), BF16),
                   jax.ShapeDtypeStruct((n, LANE), BF16), jax.ShapeDtypeStruct((LANE, GQ_W), F32),
                   jax.ShapeDtypeStruct((1, GQ_W), F32)],
        compiler_params=_cp(VMEM_BIG, ("arbitrary",)),
    )(projp, projp, gates, w2p, gb, do, dqg, dkd, dvi, dgl)


SECTIONS = ((C_QKV, 1536), (C_DZ, 512), (C_GQK, 512), (C_GV, 512), (C_GR, 512), (C_SA, 128), (C_SB, 128))


def _inproj_bwd(secs, wp, h0, g1, dx1, *, tr, name):
    n, d = h0.shape

    def body(*refs):
        sec_refs = refs[:len(SECTIONS)]
        wp_ref, h0_ref, g_ref, dx1_ref, o_ref, dg_ref = refs[len(SECTIONS):]
        dh = None
        for s_ref, (off, wd) in zip(sec_refs, SECTIONS):
            part = _nt(s_ref[...], wp_ref[:, off:off + wd])
            dh = part if dh is None else dh + part
        dx, dg = _rms_bwd_math(h0_ref[...], g_ref[...], dh)
        o_ref[...] = dx1_ref[...] + dx

        @pl.when(pl.program_id(0) == 0)
        def _():
            dg_ref[...] = dg

        @pl.when(pl.program_id(0) > 0)
        def _():
            dg_ref[...] += dg

    row = pl.BlockSpec((tr, d), lambda i: (i, 0))
    vec = pl.BlockSpec((1, d), lambda i: (0, 0))
    return pl.pallas_call(
        body, name=name, grid=(n // tr,),
        in_specs=[pl.BlockSpec((tr, wd), lambda i: (i, 0)) for _, wd in SECTIONS]
        + [pl.BlockSpec((d, PW), lambda i: (0, 0)), row, vec, row],
        out_specs=[row, vec],
        out_shape=[jax.ShapeDtypeStruct((n, d), F32), jax.ShapeDtypeStruct((1, d), F32)],
        compiler_params=_cp(VMEM_BIG),
    )(*secs, wp, h0, g1, dx1)


def _adamw(w, g, m, v, *, name, emit_grad=False):
    lead = w.ndim - 2
    r, c = w.shape[-2:]
    tr = _tile(r, 256, 8) if r > 256 else r
    c1 = 1.0 - ADAM_B1 ** ADAM_STEP
    c2 = 1.0 - ADAM_B2 ** ADAM_STEP
    n_out = 4 if emit_grad else 3

    def body(w_ref, g_ref, m_ref, v_ref, *out_refs):
        rd = (lambda ref: ref[0]) if lead else (lambda ref: ref[...])
        gv = g_ref[:, 0:c]
        nm = ADAM_B1 * rd(m_ref) + (1.0 - ADAM_B1) * gv
        nv = ADAM_B2 * rd(v_ref) + (1.0 - ADAM_B2) * (gv * gv)
        res = [-ADAM_LR * ((nm / c1) / (jnp.sqrt(nv / c2) + ADAM_EPS) + ADAM_WD * rd(w_ref)), nm, nv, gv]
        for o_ref, val in zip(out_refs, res):
            if lead:
                o_ref[0] = val
            else:
                o_ref[...] = val

    blk = pl.BlockSpec((1,) * lead + (tr, c), lambda i: (0,) * lead + (i, 0))
    gblk = pl.BlockSpec((tr, g.shape[1]), lambda i: (i, 0))
    sds = jax.ShapeDtypeStruct(w.shape, F32)
    return pl.pallas_call(
        body, name=name, grid=(r // tr,), in_specs=[blk, gblk, blk, blk], out_specs=[blk] * n_out,
        out_shape=[sds] * n_out, compiler_params=_cp(VMEM_BIG),
    )(w, g, m, v)


def _pair_sum(where, g, theirs, *, name):
    lead, r, cols = g.shape
    half = r // 2
    tr = _tile(half, 256, 16)
    nh = half // tr

    def body(w_ref, a_ref, b_ref, o_ref):
        o_ref[...] = (a_ref[...] + b_ref[...]).astype(o_ref.dtype)

    blk = pl.BlockSpec((1, tr, cols), lambda s, i, w: (s, i, 0))
    return pl.pallas_call(
        body, name=name,
        grid_spec=pltpu.PrefetchScalarGridSpec(
            num_scalar_prefetch=1, grid=(lead, nh),
            in_specs=[pl.BlockSpec((1, tr, cols), lambda s, i, w: (s, w[0] * nh + i, 0)), blk], out_specs=blk),
        out_shape=jax.ShapeDtypeStruct((lead, half, cols), BF16), compiler_params=_cp(VMEM_BIG),
    )(where, g, theirs)


def _chip_sum(where, pair, q, *, name):
    _, half, cols = pair.shape
    tr = _tile(half, 256, 16)
    nh = half // tr

    def body(w_ref, own_ref, q1_ref, q2_ref, q3_ref, o_ref):
        f = lambda ref: ref[0].astype(F32)
        o_ref[...] = ((f(own_ref) + f(q1_ref)) + f(q2_ref)) + f(q3_ref)

    def peer(d):
        return pl.BlockSpec((1, tr, cols), lambda i, w: ((w[1] + d) % N_CHIPS, i, 0))

    return pl.pallas_call(
        body, name=name,
        grid_spec=pltpu.PrefetchScalarGridSpec(
            num_scalar_prefetch=1, grid=(nh,),
            in_specs=[peer(0), peer(1), peer(2), peer(3)],
            out_specs=pl.BlockSpec((tr, cols), lambda i, w: (w[0] * nh + i, 0))),
        out_shape=jax.ShapeDtypeStruct((2 * half, cols), F32), compiler_params=_cp(VMEM_BIG),
    )(where, pair, q, q, q)


VM = pl.BlockSpec(memory_space=pltpu.VMEM)


def _row_chunks(rows, n_split):
    size = rows // n_split
    assert size * n_split == rows and size % 16 == 0, (rows, n_split)
    return [(s, pl.ds(s * size, size)) for s in range(n_split)], size


D2D_SPLIT = 4
ICI_SPLIT = 2


def _sibling_halves(grads):
    n_arr = len(grads)

    def body(*refs):
        ins = refs[:n_arr]
        theirs = refs[n_arr:2 * n_arr]
        send_sems, recv_sems = refs[2 * n_arr:]
        x, y, c = _place()
        copies = []
        for k in range(n_arr):
            half = ins[k].shape[1] // 2
            chunks, size = _row_chunks(half, D2D_SPLIT)
            for s, dst_rows in chunks:
                give = pltpu.make_async_remote_copy(
                    src_ref=ins[k].at[:, pl.ds((1 - c) * half + s * size, size), :], dst_ref=theirs[k].at[:, dst_rows, :],
                    send_sem=send_sems.at[k, s], recv_sem=recv_sems.at[k, s], device_id=(x, y, 1 - c),
                    device_id_type=MESH)
                give.start()
                copies.append(give)
        for give in copies:
            give.wait()

    halves = [jax.ShapeDtypeStruct((g.shape[0], g.shape[1] // 2, g.shape[2]), F32) for g in grads]
    sem = pltpu.SemaphoreType.DMA((n_arr, D2D_SPLIT))
    return pl.pallas_call(
        body, name="sibling_halves", in_specs=[ANY] * n_arr, out_specs=[ANY] * n_arr, out_shape=halves,
        scratch_shapes=[sem, sem],
    )(*grads)


def _chip_exchange(parts):
    n_arr = len(parts)

    def body(*refs):
        ins = refs[:n_arr]
        outs = refs[n_arr:2 * n_arr]
        send_sems, recv_sems = refs[2 * n_arr:]
        x, y, c = _place()
        me = 2 * x + y
        sends = []
        for k in range(n_arr):
            chunks, _ = _row_chunks(ins[k].shape[1], ICI_SPLIT)
            for d, (px, py, pj) in enumerate(_other_chips(x, y)):
                for s, rows in chunks:
                    cp = pltpu.make_async_remote_copy(
                        src_ref=ins[k].at[pj, rows, :], dst_ref=outs[k].at[me, rows, :], send_sem=send_sems.at[k, d, s],
                        recv_sem=recv_sems.at[k, d, s], device_id=(px, py, c), device_id_type=MESH)
                    cp.start()
                    sends.append(cp)
        for k in range(n_arr):
            chunks, _ = _row_chunks(ins[k].shape[1], ICI_SPLIT)
            for d, (px, py, pj) in enumerate(_other_chips(x, y)):
                for s, rows in chunks:
                    pltpu.make_async_remote_copy(
                        src_ref=ins[k].at[pj, rows, :], dst_ref=outs[k].at[pj, rows, :], send_sem=send_sems.at[k, d, s],
                        recv_sem=recv_sems.at[k, d, s], device_id=(px, py, c), device_id_type=MESH).wait_recv()
        for cp in sends:
            cp.wait_send()

    sem = pltpu.SemaphoreType.DMA((n_arr, 3, ICI_SPLIT))
    return pl.pallas_call(
        body, name="chip_exchange", in_specs=[ANY] * n_arr, out_specs=[ANY] * n_arr,
        out_shape=[jax.ShapeDtypeStruct(p.shape, p.dtype) for p in parts],
        scratch_shapes=[sem, sem],
    )(*parts)


class _SiblingHalvesRider:
    def __init__(self, grads):
        self.inputs = list(grads)
        self.out_shapes = [jax.ShapeDtypeStruct((g.shape[0], g.shape[1] // 2, g.shape[2]), F32) for g in grads]
        self.aliases = {}
        self.sems = [pltpu.SemaphoreType.DMA((len(grads), D2D_SPLIT))] * 2

    def _copies(self, ins, outs, sems):
        x, y, c = _place()
        for k in range(len(ins)):
            half = ins[k].shape[1] // 2
            chunks, size = _row_chunks(half, D2D_SPLIT)
            for s, dst_rows in chunks:
                yield pltpu.make_async_remote_copy(
                    src_ref=ins[k].at[:, pl.ds((1 - c) * half + s * size, size), :], dst_ref=outs[k].at[:, dst_rows, :],
                    send_sem=sems[0].at[k, s], recv_sem=sems[1].at[k, s], device_id=(x, y, 1 - c), device_id_type=MESH)

    def first(self, ins, outs, sems):
        for cp in self._copies(ins, outs, sems):
            cp.start()

    def last(self, ins, outs, sems):
        for cp in self._copies(ins, outs, sems):
            cp.wait()


class _ChipExchangeRider:
    def __init__(self, parts):
        self.inputs = list(parts)
        self.out_shapes = [jax.ShapeDtypeStruct(p.shape, p.dtype) for p in parts]
        self.aliases = {}
        self.sems = [pltpu.SemaphoreType.DMA((len(parts), 3, ICI_SPLIT))] * 2

    def _copies(self, ins, outs, sems, receiving):
        x, y, c = _place()
        for k in range(len(ins)):
            chunks, _ = _row_chunks(ins[k].shape[1], ICI_SPLIT)
            for d, (px, py, pj) in enumerate(_other_chips(x, y)):
                for s, rows in chunks:
                    yield pltpu.make_async_remote_copy(
                        src_ref=ins[k].at[pj, rows, :], dst_ref=outs[k].at[pj if receiving else 2 * x + y, rows, :],
                        send_sem=sems[0].at[k, d, s], recv_sem=sems[1].at[k, d, s], device_id=(px, py, c),
                        device_id_type=MESH)

    def first(self, ins, outs, sems):
        for cp in self._copies(ins, outs, sems, False):
            cp.start()

    def last(self, ins, outs, sems):
        for cp in self._copies(ins, outs, sems, True):
            cp.wait_recv()
        for cp in self._copies(ins, outs, sems, False):
            cp.wait_send()


def _sibling_join(bufs):
    n_arr = len(bufs)

    def body(*refs):
        bufs_out = refs[n_arr:2 * n_arr]
        send_sems, recv_sems = refs[2 * n_arr:]
        x, y, c = _place()
        copies = []
        for k in range(n_arr):
            half = bufs_out[k].shape[0] // 2
            chunks, size = _row_chunks(half, D2D_SPLIT)
            for s, _ in chunks:
                rows = pl.ds(c * half + s * size, size)
                give = pltpu.make_async_remote_copy(
                    src_ref=bufs_out[k].at[rows, :], dst_ref=bufs_out[k].at[rows, :], send_sem=send_sems.at[k, s],
                    recv_sem=recv_sems.at[k, s], device_id=(x, y, 1 - c), device_id_type=MESH)
                give.start()
                copies.append((k, s, half, size, give))
        for k, s, half, size, give in copies:
            rows = pl.ds((1 - c) * half + s * size, size)
            pltpu.make_async_remote_copy(
                src_ref=bufs_out[k].at[rows, :], dst_ref=bufs_out[k].at[rows, :], send_sem=send_sems.at[k, s],
                recv_sem=recv_sems.at[k, s], device_id=(x, y, 1 - c), device_id_type=MESH).wait_recv()
            give.wait_send()

    sem = pltpu.SemaphoreType.DMA((n_arr, D2D_SPLIT))
    return pl.pallas_call(
        body, name="sibling_join", in_specs=[ANY] * n_arr, out_specs=[ANY] * n_arr,
        out_shape=[jax.ShapeDtypeStruct(b.shape, F32) for b in bufs],
        input_output_aliases={k: k for k in range(n_arr)},
        scratch_shapes=[sem, sem],
    )(*bufs)


PACK_ROWS = 48


def _small_allreduce(pack):
    masks = [(dx, dy, dc) for dx in (0, 1) for dy in (0, 1) for dc in (0, 1)][1:]

    def body(p_ref, o_ref, buf, send_sems, recv_sems):
        x, y, c = _place()
        me = 4 * x + 2 * y + c
        buf[me] = p_ref[...]
        sends = []
        for k, (dx, dy, dc) in enumerate(masks):
            peer = (1 - x if dx else x, 1 - y if dy else y, 1 - c if dc else c)
            cp = pltpu.make_async_remote_copy(
                src_ref=p_ref, dst_ref=buf.at[me], send_sem=send_sems.at[k], recv_sem=recv_sems.at[k],
                device_id=peer, device_id_type=MESH)
            cp.start()
            sends.append(cp)
        for k, (dx, dy, dc) in enumerate(masks):
            peer = (1 - x if dx else x, 1 - y if dy else y, 1 - c if dc else c)
            pj = 4 * peer[0] + 2 * peer[1] + peer[2]
            pltpu.make_async_remote_copy(
                src_ref=p_ref, dst_ref=buf.at[pj], send_sem=send_sems.at[k], recv_sem=recv_sems.at[k],
                device_id=peer, device_id_type=MESH).wait_recv()
        for cp in sends:
            cp.wait_send()
        tot = buf[0]
        for k in range(1, 8):
            tot = tot + buf[k]
        o_ref[...] = tot
        o_ref[0:N_META, :] = tot[0:N_META] + tot[N_META:2 * N_META]

    return pl.pallas_call(
        body, name="small_allreduce", in_specs=[VM], out_specs=VM,
        out_shape=jax.ShapeDtypeStruct((PACK_ROWS, D_MODEL), F32),
        scratch_shapes=[pltpu.VMEM((8, PACK_ROWS, D_MODEL), F32), pltpu.SemaphoreType.DMA((7,)),
                        pltpu.SemaphoreType.DMA((7,))],
    )(pack)


def _pad_lanes(vec, offset):
    k = vec.shape[1]
    return jnp.concatenate([jnp.zeros((1, offset), F32), vec, jnp.zeros((1, LANE - offset - k), F32)], axis=1)


def _local_step(x, tgt, meta, norm1_g, wp, conv_w, a_log, dt_bias, dn_norm_g, gla_w2, gla_b, gla_norm_g,
                w_out, norm2_g, w_up, w_down, final_norm_g, late_gather=None, where=None):
    bsz, s_len, d = x.shape
    t_seq = s_len + CHUNK
    nc_seq = t_seq // CHUNK
    n = bsz * t_seq
    tr = _tile(t_seq, 832)
    tt = _tile(t_seq, 416)

    lead = jnp.concatenate([jnp.zeros((N_PAD, d), F32), meta], axis=0)
    h0 = jnp.concatenate([jnp.broadcast_to(lead[None], (bsz, CHUNK, d)), x], axis=1).reshape(n, d)
    tgt_p = jnp.concatenate([jnp.zeros((bsz, CHUNK, d), F32), tgt], axis=1).reshape(n, d)
    alog_row = _pad_lanes(a_log, 4)
    dtb_row = _pad_lanes(dt_bias, 4)
    w2p = jnp.concatenate([gla_w2, jnp.zeros((LANE - GLA_RANK, GQ_W), F32)], axis=0)

    h = _rms_fwd(h0, norm1_g, tr=tr, name="norm1")
    projp, gates = _mm(h, wp, "nn", tm=tt, tn=PW, tk=d, out_dtypes=(BF16, F32), out_widths=(PW, PW - C_SA),
                       epilogue=lambda acc: (acc, acc[:, C_SA:PW]), name="in_proj")
    qn, kn, v = _dnprep_fwd(projp, conv_w, bsz=bsz, t_seq=t_seq, tt=tt, name="dn_prep")
    u, w, qg, kd, pmat, tmat, gl = _dn_intra_fwd(qn, kn, v, gates, alog_row, dtb_row, nc_seq=nc_seq, name="dn_intra")
    ride_a, ride_b = late_gather if late_gather is not None else (None, None)
    o_dn, vn, hist, got_a = _dn_scan_fwd(u, w, qg, kd, pmat, gl, bsz=bsz, nc_seq=nc_seq, name="dn_scan", rider=ride_a)
    oi, gqg, gkd, ggl = _gla_intra_fwd(projp, gates, w2p, gla_b, nc_seq=nc_seq, name="gla_intra")
    o_gla, ghist, got_b = _gla_scan_fwd(oi, gqg, gkd, ggl, projp, bsz=bsz, nc_seq=nc_seq, name="gla_scan", rider=ride_b)
    if late_gather is not None:
        w_out = got_a[0].reshape(d, d)
        w_up = got_a[1].transpose(1, 0, 2).reshape(d, D_FF)
        w_down = got_b[0].reshape(D_FF, d)
    mix = _gnorm_fwd(o_dn, o_gla, projp, dn_norm_g, gla_norm_g, tr=tr, name="gated_norm")
    (x1,) = _mm(mix, w_out, "nn", tm=tr, tn=d, tk=d, out_dtypes=(F32,), extras=(h0,),
                epilogue=lambda acc, res: (res + acc,), name="out_proj")
    h2 = _rms_fwd(x1, norm2_g, tr=tr, name="norm2")

    (act,) = _mm(h2, w_up, "nn", tm=tt, tn=D_FF, tk=d, out_dtypes=(BF16,),
                 epilogue=lambda acc: (jnp.square(jnp.maximum(acc, 0.0)),), name="mlp_up")
    (x2,) = _mm(act, w_down, "nn", tm=tr, tn=d, tk=D_FF, out_dtypes=(F32,), extras=(x1,),
                epilogue=lambda acc, res: (res + acc,), name="mlp_down")
    dx2, dx2b, d_final_g, loss_tile = _final_loss(x2, final_norm_g, tgt_p, t_seq=t_seq, tr=tr, name="final_loss")

    (dup,) = _mm(dx2b, w_down, "nt", tm=tt, tn=D_FF, tk=d, out_dtypes=(BF16,), extras=(act,),
                 epilogue=lambda acc, a: (acc * (2.0 * jnp.sqrt(a.astype(F32))),), name="mlp_down_bwd")
    (d_w_down,) = _mm(act, dx2b, "tn", tm=D_FF // 2, tn=d, tk=tr, out_dtypes=(F32,), name="w_down_grad")
    (d_w_up,) = _mm(h2, dup, "tn", tm=d, tn=D_FF // 2, tk=tr, out_dtypes=(F32,), name="w_up_grad")
    mlp_sm = [d_w_up.reshape(d, N_CHIPS, D_FF // N_CHIPS).transpose(1, 0, 2), d_w_down.reshape(N_CHIPS, D_FF // N_CHIPS, d)]
    ride1 = _SiblingHalvesRider(mlp_sm) if where is not None else None
    res = _mm(dup, w_up, "nt", tm=tr, tn=d, tk=D_FF, out_dtypes=(F32,), name="mlp_up_bwd", rider=ride1)
    ((dh2,), theirs) = res if where is not None else (res, None)
    ride2 = None
    if where is not None:
        mlp_pair = [_pair_sum(where, a, b, name=f"pair_sum_mlp{k}") for k, (a, b) in enumerate(zip(mlp_sm, theirs))]
        ride2 = _ChipExchangeRider(mlp_pair)
    dx1, dx1b, d_norm2_g = _rms_bwd_add(x1, norm2_g, dh2, dx2, tr=tr, name="norm2_bwd")

    (dmix,) = _mm(dx1b, w_out, "nt", tm=tr, tn=d, tk=d, out_dtypes=(BF16,), name="out_proj_bwd")
    (d_w_out,) = _mm(mix, dx1b, "tn", tm=d, tn=d, tk=tr, out_dtypes=(F32,), name="w_out_grad")
    do_dn, ddz, do_gla, dgr, d_dn_norm_g, d_gla_norm_g = _gnorm_bwd(
        dmix, o_dn, o_gla, projp, dn_norm_g, gla_norm_g, tr=tr, name="gated_norm_bwd")
    du, dw, dqg, dkd, dgl = _dn_scan_bwd(do_dn, w, qg, kd, vn, pmat, gl, hist, bsz=bsz, nc_seq=nc_seq,
                                          name="dn_scan_bwd")
    dqn, dkn, dv, dsa, d_alog, d_dtb, mlp_parts = _dn_intra_bwd(
        qn, kn, v, gates, alog_row, dtb_row, u, w, tmat, du, dw, dqg, dkd, do_dn, vn, dgl, nc_seq=nc_seq,
        name="dn_intra_bwd", rider=ride2)
    dz, d_conv_w = _dnprep_bwd_a(projp, conv_w, dqn, dkn, dv, bsz=bsz, t_seq=t_seq, tt=tt, name="dn_prep_bwd")
    dcin = _dnprep_bwd_b(dz, conv_w, bsz=bsz, t_seq=t_seq, tt=tt, name="conv_bwd")
    gdqg, gdkd, gdvi, gdgl = _gla_scan_bwd(do_gla, gqg, gkd, ggl, projp, ghist, bsz=bsz, nc_seq=nc_seq,
                                            name="gla_scan_bwd")
    dgqk, dgv, dsb, d_w2p, d_gla_b = _gla_intra_bwd(projp, gates, w2p, gla_b, do_gla, gdqg, gdkd, gdvi, gdgl,
                                                    nc_seq=nc_seq, name="gla_intra_bwd")

    secs = (dcin, ddz, dgqk, dgv, dgr, dsa, dsb)
    g_lo = _grad_tn(h, secs[0:2], tk=tr, name="w_in_grad_lo")
    g_hi = _grad_tn(h, secs[2:7], tk=tr, name="w_in_grad_hi")
    dh0, d_norm1_g = _inproj_bwd(secs, wp, h0, norm1_g, dx1, tr=tt, name="in_proj_bwd")
    dh0 = dh0.reshape(bsz, t_seq, d)
    grad_x = dh0[:, CHUNK:]
    d_meta_rows = dh0[:, N_PAD:CHUNK].reshape(bsz * N_META, d)

    grads = dict(w_in_lo=g_lo, w_in_hi=g_hi, w_out=d_w_out, w_up=d_w_up, w_down=d_w_down, meta_rows=d_meta_rows,
                 norm1_g=d_norm1_g, conv_w=d_conv_w, a_log_tile=d_alog, dt_bias_tile=d_dtb, dn_norm_g=d_dn_norm_g,
                 gla_w2=d_w2p[0:GLA_RANK], gla_b=d_gla_b, gla_norm_g=d_gla_norm_g, norm2_g=d_norm2_g,
                 final_norm_g=d_final_g, loss_tile=loss_tile)
    if where is not None:
        grads["mlp_exchanged"] = (mlp_pair, mlp_parts)
    return grad_x, grads


SHARD_W = IN_WIDTH // N_CHIPS
PADDED_ORDER = ((0, 2048), (2056, 3592), (2048, 2056), LANE - 8, (3592, 3608), LANE - GLA_RANK)


def _pad_layout(w_full):
    pieces = [jnp.zeros((w_full.shape[0], seg), w_full.dtype) if isinstance(seg, int) else w_full[:, seg[0]:seg[1]]
              for seg in PADDED_ORDER]
    return jnp.concatenate(pieces, axis=1)


def _padded_from_shards(stack):
    pieces = []
    for seg in PADDED_ORDER:
        if isinstance(seg, int):
            pieces.append(jnp.zeros((stack.shape[1], seg), stack.dtype))
            continue
        for j in range(N_CHIPS):
            lo, hi = max(seg[0], j * SHARD_W), min(seg[1], (j + 1) * SHARD_W)
            if lo < hi:
                pieces.append(stack[j, :, lo - j * SHARD_W:hi - j * SHARD_W])
    return jnp.concatenate(pieces, axis=1)


def _shards_from_padded(g_lo, g_hi):
    split = g_lo.shape[1]
    starts, pos = [], 0
    for seg in PADDED_ORDER:
        width = seg if isinstance(seg, int) else seg[1] - seg[0]
        if not isinstance(seg, int):
            starts.append((seg[0], seg[1], pos))
        pos += width
    shards = []
    for j in range(N_CHIPS):
        pieces = []
        for a, b, p0 in sorted(starts):
            lo, hi = max(a, j * SHARD_W), min(b, (j + 1) * SHARD_W)
            if lo < hi:
                src, off = (g_lo, 0) if p0 < split else (g_hi, split)
                pieces.append(src[:, p0 + lo - a - off:p0 + hi - a - off])
        pieces.append(jnp.zeros((g_lo.shape[0], D_MODEL - SHARD_W), g_lo.dtype))
        shards.append(jnp.concatenate(pieces, axis=1))
    return jnp.stack(shards)


def _pack_small(g, bsz):
    assert bsz * N_META == 32
    row = jnp.concatenate([g["a_log_tile"], g["dt_bias_tile"], g["dn_norm_g"], g["gla_norm_g"], g["gla_b"],
                           g["loss_tile"], jnp.zeros((1, LANE), F32)], axis=1)
    return jnp.concatenate([g["meta_rows"], g["norm1_g"], g["conv_w"].reshape(6, D_MODEL), row,
                            g["gla_w2"].reshape(4, D_MODEL), g["norm2_g"], g["final_norm_g"],
                            jnp.zeros((2, D_MODEL), F32)], axis=0)


def kernel(x, meta_tokens, norm1_g, w_in, conv_w, a_log, dt_bias, dn_norm_g, gla_w2, gla_b, gla_norm_g, w_out, norm2_g, w_up, w_down, final_norm_g, loss_target, m_meta_tokens, m_norm1_g, m_w_in, m_conv_w, m_a_log, m_dt_bias, m_dn_norm_g, m_gla_w2, m_gla_b, m_gla_norm_g, m_w_out, m_norm2_g, m_w_up, m_w_down, m_final_norm_g, v_meta_tokens, v_norm1_g, v_w_in, v_conv_w, v_a_log, v_dt_bias, v_dn_norm_g, v_gla_w2, v_gla_b, v_gla_norm_g, v_w_out, v_norm2_g, v_w_up, v_w_down, v_final_norm_g):
    bsz = x.shape[0]
    chip = 2 * lax.axis_index("x") + lax.axis_index("y")

    lane_pad = lambda a, wd: jnp.pad(a, ((0, 0), (0, wd - a.shape[1])))
    where = jnp.stack([lax.axis_index("c"), chip]).astype(jnp.int32)
    slot = lambda a, dt, nm: _to_slot(where, a, dt, name="slot_" + nm)
    early = _GatherRider([slot(lane_pad(w_in[0], D_MODEL), BF16, "w_in"), slot(meta_tokens, F32, "meta"),
                          slot(conv_w[0], F32, "conv"), slot(lane_pad(gla_w2[0], LANE), F32, "gla_w2")],
                         [True, False, False, False])
    g_in, g_meta, g_conv, g_w2 = _exchange_now(early, name="gather_early")
    late = (_GatherRider([slot(w_out[0], BF16, "w_out"), slot(w_up[0], BF16, "w_up")], [True, True]),
            _GatherRider([slot(w_down[0], BF16, "w_down")], [True]))
    wp = _padded_from_shards(g_in)
    meta_f = g_meta.transpose(1, 0, 2).reshape(N_META, D_MODEL)
    conv_f = g_conv.transpose(1, 0, 2).reshape(4, QKV_W)
    w2_f = g_w2[:, :, 0:GQ_W // N_CHIPS].transpose(1, 0, 2).reshape(GLA_RANK, GQ_W)

    grad_x, g = _local_step(x, loss_target, meta_f, norm1_g, wp, conv_f, a_log, dt_bias, dn_norm_g, w2_f, gla_b,
                            gla_norm_g, None, norm2_g, None, None, final_norm_g.reshape(1, D_MODEL), late_gather=late, where=where)

    shard_major = [_shards_from_padded(g["w_in_lo"], g["w_in_hi"]), g["w_out"].reshape(N_CHIPS, D_MODEL // N_CHIPS, D_MODEL)]
    theirs = _exchange_now(_SiblingHalvesRider(shard_major), name="sibling_halves")
    pair = [_pair_sum(where, a, b, name=f"pair_sum_{k}") for k, (a, b) in enumerate(zip(shard_major, theirs))]
    parts = _exchange_now(_ChipExchangeRider(pair), name="chip_exchange")
    mlp_pair, mlp_parts = g["mlp_exchanged"]
    halves = [_chip_sum(where, p, q, name=f"chip_sum_{k}")
              for k, (p, q) in enumerate(zip(pair + mlp_pair, list(parts) + list(mlp_parts)))]
    gw_in, gw_out, gw_up, gw_down = _sibling_join(halves)

    red = _small_allreduce(_pack_small(g, bsz))
    g_meta_full = red[0:N_META]
    g_norm1 = red[32:33]
    g_conv_full = red[33:39].reshape(4, QKV_W)
    srow = red[39:40]
    g_alog, g_dtb = srow[:, 4:8], srow[:, LANE + 4:LANE + 8]
    g_dn_norm, g_gla_norm = srow[:, 2 * LANE:3 * LANE], srow[:, 3 * LANE:4 * LANE]
    g_gla_b = srow[:, 4 * LANE:6 * LANE]
    loss = srow[0, 6 * LANE]
    g_w2_full = red[40:44].reshape(GLA_RANK, GQ_W)
    g_norm2 = red[44:45]
    g_final = red[45:46]
    g_meta_sh = lax.dynamic_slice_in_dim(g_meta_full, chip * (D_MODEL // N_CHIPS), D_MODEL // N_CHIPS, axis=1)
    g_conv_sh = lax.dynamic_slice_in_dim(g_conv_full, chip * (QKV_W // N_CHIPS), QKV_W // N_CHIPS, axis=1)
    g_w2_sh = lax.dynamic_slice_in_dim(g_w2_full, chip * (GQ_W // N_CHIPS), GQ_W // N_CHIPS, axis=1)

    names = ["meta_tokens", "norm1_g", "w_in", "conv_w", "a_log", "dt_bias", "dn_norm_g", "gla_w2", "gla_b",
             "gla_norm_g", "w_out", "norm2_g", "w_up", "w_down", "final_norm_g"]
    weights = dict(meta_tokens=meta_tokens, norm1_g=norm1_g, w_in=w_in, conv_w=conv_w, a_log=a_log, dt_bias=dt_bias,
                   dn_norm_g=dn_norm_g, gla_w2=gla_w2, gla_b=gla_b, gla_norm_g=gla_norm_g, w_out=w_out,
                   norm2_g=norm2_g, w_up=w_up, w_down=w_down, final_norm_g=final_norm_g)
    ms = dict(meta_tokens=m_meta_tokens, norm1_g=m_norm1_g, w_in=m_w_in, conv_w=m_conv_w, a_log=m_a_log,
              dt_bias=m_dt_bias, dn_norm_g=m_dn_norm_g, gla_w2=m_gla_w2, gla_b=m_gla_b, gla_norm_g=m_gla_norm_g,
              w_out=m_w_out, norm2_g=m_norm2_g, w_up=m_w_up, w_down=m_w_down, final_norm_g=m_final_norm_g)
    vs = dict(meta_tokens=v_meta_tokens, norm1_g=v_norm1_g, w_in=v_w_in, conv_w=v_conv_w, a_log=v_a_log,
              dt_bias=v_dt_bias, dn_norm_g=v_dn_norm_g, gla_w2=v_gla_w2, gla_b=v_gla_b, gla_norm_g=v_gla_norm_g,
              w_out=v_w_out, norm2_g=v_norm2_g, w_up=v_w_up, w_down=v_w_down, final_norm_g=v_final_norm_g)
    grads2d = dict(meta_tokens=g_meta_sh, norm1_g=g_norm1, w_in=gw_in, conv_w=g_conv_sh, a_log=g_alog, dt_bias=g_dtb,
                   dn_norm_g=g_dn_norm, gla_w2=g_w2_sh, gla_b=g_gla_b, gla_norm_g=g_gla_norm, w_out=gw_out,
                   norm2_g=g_norm2, w_up=gw_up, w_down=gw_down, final_norm_g=g_final)
    out_g, out_d, out_m, out_v = [], [], [], []
    for nm in names:
        shape = weights[nm].shape
        g2 = grads2d[nm]
        if len(shape) == 3:
            res = _adamw(weights[nm], g2, ms[nm], vs[nm], name=f"adamw_{nm}", emit_grad=nm == "w_in")
            gout = res[3] if nm == "w_in" else g2.reshape(shape)
        else:
            as2d = lambda a: a.reshape(g2.shape)
            res = _adamw(as2d(weights[nm]), g2, as2d(ms[nm]), as2d(vs[nm]), name=f"adamw_{nm}")
            gout = g2.reshape(shape)
        out_g.append(gout)
        out_d.append(res[0].reshape(shape))
        out_m.append(res[1].reshape(shape))
        out_v.append(res[2].reshape(shape))
    return (loss, grad_x, *out_g, *out_d, *out_m, *out_v)
```

```python
import functools

import jax
import jax.numpy as jnp
import numpy as np
from jax import lax
from jax.experimental import pallas as pl
from jax.experimental.pallas import tpu as pltpu

F32 = jnp.float32
BF16 = jnp.bfloat16
HI = lax.Precision.HIGHEST
MESH = pl.DeviceIdType.MESH

D_MODEL = 1024
N_META = 16
CHUNK = 64
N_PAD = CHUNK - N_META
NH = 4
DN_D = 128
GLA_DK = 64
GLA_DV = 128
GLA_RANK = 16
D_FF = 4 * D_MODEL
EPS = 1e-6
IN_WIDTH = 3608
C_QKV, C_DZ, C_GQK, C_GV, C_GR, C_SA, C_SB, PW = 0, 1536, 2048, 2560, 3072, 3584, 3712, 3840
LANE = 128
N_CHIPS = 4

ADAM_LR, ADAM_B1, ADAM_B2, ADAM_EPS, ADAM_WD, ADAM_STEP = 0.001, 0.9, 0.999, 1e-08, 0.01, 10

VMEM_BIG = 56 * 1024 * 1024


def _cp(vmem=None, sem=None):
    kw = {}
    if vmem is not None:
        kw["vmem_limit_bytes"] = vmem
    if sem is not None:
        kw["dimension_semantics"] = sem
    return pltpu.CompilerParams(**kw)


def _tile(n, target, mult=16):
    best = None
    for t in range(mult, min(n, target) + 1, mult):
        if n % t == 0:
            best = t
    assert best is not None, (n, target)
    return best


def _dot(a, b, dims, prec=None):
    return lax.dot_general(a, b, (dims, ((), ())), preferred_element_type=F32, precision=prec)


def _nn(a, b):
    return _dot(a.astype(BF16), b.astype(BF16), ((1,), (0,)))


def _nt(a, b):
    return _dot(a.astype(BF16), b.astype(BF16), ((1,), (1,)))


def _tn(a, b):
    return _dot(a.astype(BF16), b.astype(BF16), ((0,), (0,)))


def _nn_hi(a, b):
    return _dot(a, b, ((1,), (0,)), HI)


def _nt_hi(a, b):
    return _dot(a, b, ((1,), (1,)), HI)


def _tn_hi(a, b):
    return _dot(a, b, ((0,), (0,)), HI)


def _sigmoid(x):
    return 0.5 * jnp.tanh(0.5 * x) + 0.5


def _softplus(x):
    return jnp.maximum(x, 0.0) + jnp.log(1.0 + jnp.exp(-jnp.abs(x)))


def _logsigmoid(x):
    return -_softplus(-x)


def _iota2(shape, dim):
    return lax.broadcasted_iota(jnp.int32, shape, dim)


def _mm(a, b, mode, *, tm, tn, tk, out_dtypes, extras=(), epilogue=None, name, vmem=VMEM_BIG, rider=None,
        out_widths=None):
    if mode == "tn":
        K, M = a.shape
    else:
        M, K = a.shape
    N = b.shape[0] if mode == "nt" else b.shape[1]
    assert M % tm == 0 and N % tn == 0 and K % tk == 0, (name, M, N, K, tm, tn, tk)
    nk = K // tk
    n_ex, n_out = len(extras), len(out_dtypes)
    if mode == "tn":
        a_spec = pl.BlockSpec((tk, tm), lambda i, j, k: (k, i))
    else:
        a_spec = pl.BlockSpec((tm, tk), lambda i, j, k: (i, k))
    if mode == "nt":
        b_spec = pl.BlockSpec((tn, tk), lambda i, j, k: (j, k))
    else:
        b_spec = pl.BlockSpec((tk, tn), lambda i, j, k: (k, j))
    mn_spec = pl.BlockSpec((tm, tn), lambda i, j, k: (i, j))
    if out_widths is None:
        o_specs = [mn_spec] * n_out
        o_shapes = [jax.ShapeDtypeStruct((M, N), dt) for dt in out_dtypes]
    else:
        assert tn == N
        o_specs = [pl.BlockSpec((tm, wd), lambda i, j, k: (i, 0)) for wd in out_widths]
        o_shapes = [jax.ShapeDtypeStruct((M, wd), dt) for wd, dt in zip(out_widths, out_dtypes)]
    dims = {"nn": ((1,), (0,)), "nt": ((1,), (1,)), "tn": ((0,), (0,))}[mode]

    single = nk == 1
    direct = (not single) and epilogue is None and n_out == 1 and out_dtypes[0] == F32

    def body(*refs):
        a_ref, b_ref = refs[0], refs[1]
        ex_refs = refs[2:2 + n_ex]
        out_refs = refs[2 + n_ex:2 + n_ex + n_out]
        part = _dot(a_ref[...].astype(BF16), b_ref[...].astype(BF16), dims)

        def finish(acc):
            res = (acc,) if epilogue is None else epilogue(acc, *[e[...] for e in ex_refs])
            for o_ref, r in zip(out_refs, res):
                o_ref[...] = r.astype(o_ref.dtype)

        if single:
            finish(part)
            return
        acc_ref = out_refs[0] if direct else refs[2 + n_ex + n_out]
        k = pl.program_id(2)

        @pl.when(k == 0)
        def _():
            acc_ref[...] = part

        @pl.when(k > 0)
        def _():
            acc_ref[...] += part

        if not direct:
            @pl.when(k == nk - 1)
            def _():
                finish(acc_ref[...])

    outs, ridden = _hosted_call(
        body, rider, name=name, grid=(M // tm, N // tn, nk),
        in_specs=[a_spec, b_spec] + [mn_spec] * n_ex,
        out_specs=o_specs, out_shape=o_shapes,
        scratch_shapes=[] if (single or direct) else [pltpu.VMEM((tm, tn), F32)],
        compiler_params=_cp(vmem, ("parallel", "parallel", "arbitrary")), args=(a, b, *extras))
    return tuple(outs) if rider is None else (tuple(outs), ridden)


def _grad_tn(a, secs, *, tk, name):
    kk, m = a.shape
    widths = [s.shape[1] for s in secs]
    total = sum(widths)
    nk = kk // tk

    def body(*refs):
        a_ref, sec_refs, o_ref = refs[0], refs[1:-1], refs[-1]
        cat = sec_refs[0][...] if len(sec_refs) == 1 else jnp.concatenate([s[...] for s in sec_refs], axis=1)
        part = _dot(a_ref[...].astype(BF16), cat.astype(BF16), ((0,), (0,)))
        k = pl.program_id(0)

        @pl.when(k == 0)
        def _():
            o_ref[...] = part

        @pl.when(k > 0)
        def _():
            o_ref[...] += part

    return pl.pallas_call(
        body, name=name, grid=(nk,),
        in_specs=[pl.BlockSpec((tk, m), lambda k: (k, 0))] + [pl.BlockSpec((tk, w), lambda k: (k, 0)) for w in widths],
        out_specs=pl.BlockSpec((m, total), lambda k: (0, 0)),
        out_shape=jax.ShapeDtypeStruct((m, total), F32),
        compiler_params=_cp(VMEM_BIG, ("arbitrary",)),
    )(a, *secs)


def _rms_fwd(x, g, *, tr, name):
    n, d = x.shape

    def body(x_ref, g_ref, o_ref):
        xv = x_ref[...]
        r = lax.rsqrt(jnp.mean(xv * xv, axis=-1, keepdims=True) + EPS)
        o_ref[...] = (xv * r * g_ref[...]).astype(o_ref.dtype)

    return pl.pallas_call(
        body, name=name, grid=(n // tr,),
        in_specs=[pl.BlockSpec((tr, d), lambda i: (i, 0)), pl.BlockSpec((1, d), lambda i: (0, 0))],
        out_specs=pl.BlockSpec((tr, d), lambda i: (i, 0)),
        out_shape=jax.ShapeDtypeStruct((n, d), BF16),
        compiler_params=_cp(VMEM_BIG),
    )(x, g)


def _rms_bwd_math(xv, g, dy):
    r = lax.rsqrt(jnp.mean(xv * xv, axis=-1, keepdims=True) + EPS)
    xh = xv * r
    gdy = dy * g
    dx = r * (gdy - xh * jnp.mean(xh * gdy, axis=-1, keepdims=True))
    return dx, jnp.sum(dy * xh, axis=0, keepdims=True)


def _rms_bwd_add(x, g, dy, res, *, tr, name):
    n, d = x.shape

    def body(x_ref, g_ref, dy_ref, res_ref, o_ref, ob_ref, dg_ref):
        dx, dg = _rms_bwd_math(x_ref[...], g_ref[...], dy_ref[...])
        tot = res_ref[...] + dx
        o_ref[...] = tot
        ob_ref[...] = tot.astype(BF16)

        @pl.when(pl.program_id(0) == 0)
        def _():
            dg_ref[...] = dg

        @pl.when(pl.program_id(0) > 0)
        def _():
            dg_ref[...] += dg

    row = pl.BlockSpec((tr, d), lambda i: (i, 0))
    vec = pl.BlockSpec((1, d), lambda i: (0, 0))
    return pl.pallas_call(
        body, name=name, grid=(n // tr,),
        in_specs=[row, vec, row, row], out_specs=[row, row, vec],
        out_shape=[jax.ShapeDtypeStruct((n, d), F32), jax.ShapeDtypeStruct((n, d), BF16),
                   jax.ShapeDtypeStruct((1, d), F32)],
        compiler_params=_cp(VMEM_BIG),
    )(x, g, dy, res)


def _final_loss(x2, gf, tgt, *, t_seq, tr, name):
    n, d = x2.shape
    per_seq = t_seq // tr

    def body(x_ref, g_ref, t_ref, dx_ref, dxb_ref, dg_ref, loss_ref):
        i = pl.program_id(0)
        xv = x_ref[...]
        g = g_ref[...]
        r = lax.rsqrt(jnp.mean(xv * xv, axis=-1, keepdims=True) + EPS)
        xh = xv * r
        pos = (i % per_seq) * tr + _iota2((tr, 1), 0)
        real = pos >= CHUNK
        err = jnp.where(real, xh * g - t_ref[...], 0.0)
        dy = err * (1.0 / d)
        gdy = dy * g
        dx = r * (gdy - xh * jnp.mean(xh * gdy, axis=-1, keepdims=True))
        dx_ref[...] = dx
        dxb_ref[...] = dx.astype(BF16)
        dg = jnp.sum(dy * xh, axis=0, keepdims=True)
        ls = 0.5 * jnp.sum(jnp.mean(err * err, axis=-1, keepdims=True), axis=0, keepdims=True)
        ls = jnp.where(_iota2((1, LANE), 1) == 0, ls, 0.0)

        @pl.when(i == 0)
        def _():
            dg_ref[...] = dg
            loss_ref[...] = ls

        @pl.when(i > 0)
        def _():
            dg_ref[...] += dg
            loss_ref[...] += ls

    row = pl.BlockSpec((tr, d), lambda i: (i, 0))
    vec = pl.BlockSpec((1, d), lambda i: (0, 0))
    one = pl.BlockSpec((1, LANE), lambda i: (0, 0))
    return pl.pallas_call(
        body, name=name, grid=(n // tr,),
        in_specs=[row, vec, row], out_specs=[row, row, vec, one],
        out_shape=[jax.ShapeDtypeStruct((n, d), F32), jax.ShapeDtypeStruct((n, d), BF16),
                   jax.ShapeDtypeStruct((1, d), F32), jax.ShapeDtypeStruct((1, LANE), F32)],
        compiler_params=_cp(VMEM_BIG),
    )(x2, gf, tgt)


def _gnorm_fwd(o_dn, o_gla, projp, g_dn, g_gla, *, tr, name):
    n = o_dn.shape[0]
    w = NH * DN_D

    def body(odn_ref, ogl_ref, z_ref, r_ref, gdn_ref, ggl_ref, mix_ref):
        for grp, (o_ref, gate_ref, gain_ref) in enumerate(((odn_ref, z_ref, gdn_ref), (ogl_ref, r_ref, ggl_ref))):
            gain = gain_ref[...]
            for h in range(NH):
                sl = slice(h * DN_D, (h + 1) * DN_D)
                o = o_ref[:, sl].astype(F32)
                z = gate_ref[:, sl].astype(F32)
                r = lax.rsqrt(jnp.mean(o * o, axis=-1, keepdims=True) + EPS)
                y = (o * r * gain) * (z * _sigmoid(z))
                mix_ref[:, grp * w + h * DN_D: grp * w + (h + 1) * DN_D] = y.astype(mix_ref.dtype)

    row = pl.BlockSpec((tr, w), lambda i: (i, 0))
    vec = pl.BlockSpec((1, DN_D), lambda i: (0, 0))
    return pl.pallas_call(
        body, name=name, grid=(n // tr,),
        in_specs=[row, row, pl.BlockSpec((tr, w), lambda i: (i, C_DZ // w)),
                  pl.BlockSpec((tr, w), lambda i: (i, C_GR // w)), vec, vec],
        out_specs=pl.BlockSpec((tr, 2 * w), lambda i: (i, 0)),
        out_shape=jax.ShapeDtypeStruct((n, 2 * w), BF16),
        compiler_params=_cp(VMEM_BIG),
    )(o_dn, o_gla, projp, projp, g_dn, g_gla)


def _gnorm_bwd(dmix, o_dn, o_gla, projp, g_dn, g_gla, *, tr, name):
    n = o_dn.shape[0]
    w = NH * DN_D

    def body(dm_ref, odn_ref, ogl_ref, z_ref, r_ref, gdn_ref, ggl_ref,
             dodn_ref, ddz_ref, dogl_ref, dgr_ref, dgdn_ref, dggl_ref):
        first = pl.program_id(0) == 0
        groups = ((odn_ref, z_ref, gdn_ref, dodn_ref, ddz_ref, dgdn_ref),
                  (ogl_ref, r_ref, ggl_ref, dogl_ref, dgr_ref, dggl_ref))
        for grp, (o_ref, gate_ref, gain_ref, do_ref, dgate_ref, dgain_ref) in enumerate(groups):
            gain = gain_ref[...]
            dgain = jnp.zeros((1, DN_D), F32)
            for h in range(NH):
                sl = slice(h * DN_D, (h + 1) * DN_D)
                o = o_ref[:, sl].astype(F32)
                z = gate_ref[:, sl].astype(F32)
                dm = dm_ref[:, grp * w + h * DN_D: grp * w + (h + 1) * DN_D].astype(F32)
                r = lax.rsqrt(jnp.mean(o * o, axis=-1, keepdims=True) + EPS)
                oh = o * r
                s = _sigmoid(z)
                dn = dm * (z * s)
                dgate_ref[:, sl] = (dm * (oh * gain) * (s * (1.0 + z * (1.0 - s)))).astype(dgate_ref.dtype)
                gdn = dn * gain
                do_ref[:, sl] = (r * (gdn - oh * jnp.mean(oh * gdn, axis=-1, keepdims=True))).astype(do_ref.dtype)
                dgain = dgain + jnp.sum(dn * oh, axis=0, keepdims=True)

            @pl.when(first)
            def _():
                dgain_ref[...] = dgain

            @pl.when(jnp.logical_not(first))
            def _():
                dgain_ref[...] += dgain

    row = pl.BlockSpec((tr, w), lambda i: (i, 0))
    vec = pl.BlockSpec((1, DN_D), lambda i: (0, 0))
    big = jax.ShapeDtypeStruct((n, w), F32)
    gate = jax.ShapeDtypeStruct((n, w), BF16)
    small = jax.ShapeDtypeStruct((1, DN_D), F32)
    return pl.pallas_call(
        body, name=name, grid=(n // tr,),
        in_specs=[pl.BlockSpec((tr, 2 * w), lambda i: (i, 0)), row, row,
                  pl.BlockSpec((tr, w), lambda i: (i, C_DZ // w)), pl.BlockSpec((tr, w), lambda i: (i, C_GR // w)), vec, vec],
        out_specs=[row, row, row, row, vec, vec],
        out_shape=[gate, gate, gate, gate, small, small],
        compiler_params=_cp(VMEM_BIG),
    )(dmix, o_dn, o_gla, projp, projp, g_dn, g_gla)


QKV_W = 3 * NH * DN_D
HALO = 8


def _conv_z(xs_ref, cw_ref, tt):
    z = cw_ref[0:1, :] * xs_ref[pl.ds(HALO - 3, tt), :]
    for j in range(1, 4):
        z = z + cw_ref[j:j + 1, :] * xs_ref[pl.ds(HALO - 3 + j, tt), :]
    return z


def _dnprep_fwd(projp, conv_w, *, bsz, t_seq, tt, name):
    n = bsz * t_seq
    per_seq = t_seq // tt
    hw = NH * DN_D

    def body(x_ref, halo_ref, cw_ref, q_ref, k_ref, v_ref, xs_ref):
        i = pl.program_id(1)
        xs_ref[0:HALO, :] = jnp.where(i == 0, 0.0, halo_ref[...].astype(F32))
        xs_ref[HALO:HALO + tt, :] = x_ref[...].astype(F32)
        z = _conv_z(xs_ref, cw_ref, tt)
        a = z * _sigmoid(z)
        for grp, o_ref in enumerate((q_ref, k_ref)):
            for h in range(NH):
                ah = a[:, grp * hw + h * DN_D: grp * hw + (h + 1) * DN_D]
                rs = lax.rsqrt(jnp.sum(ah * ah, axis=-1, keepdims=True) + EPS)
                o_ref[:, h * DN_D:(h + 1) * DN_D] = (ah * rs).astype(o_ref.dtype)
        v_ref[...] = a[:, 2 * hw:3 * hw].astype(v_ref.dtype)

    def halo_map(b, i):
        return (jnp.maximum((b * t_seq + i * tt) // HALO - 1, 0), 0)

    out = pl.BlockSpec((tt, hw), lambda b, i: (b * per_seq + i, 0))
    sds = jax.ShapeDtypeStruct((n, hw), BF16)
    return pl.pallas_call(
        body, name=name, grid=(bsz, per_seq),
        in_specs=[pl.BlockSpec((tt, QKV_W), lambda b, i: (b * per_seq + i, 0)),
                  pl.BlockSpec((HALO, QKV_W), halo_map),
                  pl.BlockSpec((4, QKV_W), lambda b, i: (0, 0))],
        out_specs=[out, out, out], out_shape=[sds, sds, sds],
        scratch_shapes=[pltpu.VMEM((tt + HALO, QKV_W), F32)],
        compiler_params=_cp(VMEM_BIG),
    )(projp, projp, conv_w)


def _dnprep_bwd_a(projp, conv_w, dq, dk, dv, *, bsz, t_seq, tt, name):
    n = bsz * t_seq
    per_seq = t_seq // tt
    hw = NH * DN_D

    def body(x_ref, halo_ref, cw_ref, dq_ref, dk_ref, dv_ref, dz_ref, dcw_ref, xs_ref):
        b, i = pl.program_id(0), pl.program_id(1)
        xs_ref[0:HALO, :] = jnp.where(i == 0, 0.0, halo_ref[...].astype(F32))
        xs_ref[HALO:HALO + tt, :] = x_ref[...].astype(F32)
        z = _conv_z(xs_ref, cw_ref, tt)
        s = _sigmoid(z)
        a = z * s
        dsilu = s * (1.0 + z * (1.0 - s))
        for grp, d_ref in enumerate((dq_ref, dk_ref)):
            for h in range(NH):
                sl = slice(grp * hw + h * DN_D, grp * hw + (h + 1) * DN_D)
                ah = a[:, sl]
                rs = lax.rsqrt(jnp.sum(ah * ah, axis=-1, keepdims=True) + EPS)
                y = ah * rs
                dy = d_ref[:, h * DN_D:(h + 1) * DN_D]
                da = rs * (dy - y * jnp.sum(dy * y, axis=-1, keepdims=True))
                dz_ref[:, sl] = da * dsilu[:, sl]
        dz_ref[:, 2 * hw:3 * hw] = dv_ref[...] * dsilu[:, 2 * hw:3 * hw]
        dz = dz_ref[...]
        first = jnp.logical_and(b == 0, i == 0)
        for j in range(4):
            part = jnp.sum(dz * xs_ref[pl.ds(HALO - 3 + j, tt), :], axis=0, keepdims=True)

            @pl.when(first)
            def _():
                dcw_ref[j:j + 1, :] = part

            @pl.when(jnp.logical_not(first))
            def _():
                dcw_ref[j:j + 1, :] += part

    def halo_map(b, i):
        return (jnp.maximum((b * t_seq + i * tt) // HALO - 1, 0), 0)

    hrow = pl.BlockSpec((tt, hw), lambda b, i: (b * per_seq + i, 0))
    return pl.pallas_call(
        body, name=name, grid=(bsz, per_seq),
        in_specs=[pl.BlockSpec((tt, QKV_W), lambda b, i: (b * per_seq + i, 0)),
                  pl.BlockSpec((HALO, QKV_W), halo_map),
                  pl.BlockSpec((4, QKV_W), lambda b, i: (0, 0)), hrow, hrow, hrow],
        out_specs=[pl.BlockSpec((tt, QKV_W), lambda b, i: (b * per_seq + i, 0)),
                   pl.BlockSpec((4, QKV_W), lambda b, i: (0, 0))],
        out_shape=[jax.ShapeDtypeStruct((n, QKV_W), F32), jax.ShapeDtypeStruct((4, QKV_W), F32)],
        scratch_shapes=[pltpu.VMEM((tt + HALO, QKV_W), F32)],
        compiler_params=_cp(VMEM_BIG),
    )(projp, projp, conv_w, dq, dk, dv)


def _dnprep_bwd_b(dz, conv_w, *, bsz, t_seq, tt, name):
    n = bsz * t_seq
    per_seq = t_seq // tt
    last_blk = n // HALO - 1

    def body(dz_ref, halo_ref, cw_ref, dx_ref, ds_ref):
        i = pl.program_id(1)
        ds_ref[0:tt, :] = dz_ref[...].astype(F32)
        ds_ref[tt:tt + HALO, :] = jnp.where(i == per_seq - 1, 0.0, halo_ref[...].astype(F32))
        dx = cw_ref[0:1, :] * ds_ref[pl.ds(3, tt), :]
        for j in range(1, 4):
            dx = dx + cw_ref[j:j + 1, :] * ds_ref[pl.ds(3 - j, tt), :]
        dx_ref[...] = dx.astype(dx_ref.dtype)

    def halo_map(b, i):
        return (jnp.minimum((b * t_seq + (i + 1) * tt) // HALO, last_blk), 0)

    row = pl.BlockSpec((tt, QKV_W), lambda b, i: (b * per_seq + i, 0))
    return pl.pallas_call(
        body, name=name, grid=(bsz, per_seq),
        in_specs=[row, pl.BlockSpec((HALO, QKV_W), halo_map), pl.BlockSpec((4, QKV_W), lambda b, i: (0, 0))],
        out_specs=row, out_shape=jax.ShapeDtypeStruct((n, QKV_W), BF16),
        scratch_shapes=[pltpu.VMEM((tt + HALO, QKV_W), F32)],
        compiler_params=_cp(VMEM_BIG),
    )(dz, dz, conv_w)


def _masks64():
    r = _iota2((CHUNK, CHUNK), 0)
    c = _iota2((CHUNK, CHUNK), 1)
    return r, c


def _group(nc_seq, target=5):
    return max(g for g in range(1, target + 1) if nc_seq % g == 0)


def _round_robin(chains):
    live = list(chains)
    while live:
        nxt = []
        for ch in live:
            try:
                next(ch)
                nxt.append(ch)
            except StopIteration:
                pass
        live = nxt
        yield


def _run(chains):
    for _ in _round_robin(chains):
        pass


def _per_chunk(inner, kinds, grp):
    def body(*refs):
        chains = []
        for gi in range(grp):
            views = []
            for r, kind in zip(refs, kinds):
                if kind == "row":
                    views.append(r.at[pl.ds(gi * CHUNK, CHUNK)])
                elif kind == "lead":
                    views.append(r.at[pl.ds(gi, 1)])
                else:
                    views.append(r)
            chains.append(inner(gi, *views))
        _run(chains)
    return body


def _accumulate(ref, val, gi):
    if gi > 0:
        ref[...] += val
        return
    first = pl.program_id(0) == 0

    @pl.when(first)
    def _():
        ref[...] = val

    @pl.when(jnp.logical_not(first))
    def _():
        ref[...] += val


ANY = pl.BlockSpec(memory_space=pl.ANY)


def _place():
    return lax.axis_index("x"), lax.axis_index("y"), lax.axis_index("c")


def _other_chips(x, y):
    return [(1 - x, y, 2 * (1 - x) + y), (x, 1 - y, 2 * x + 1 - y), (1 - x, 1 - y, 2 * (1 - x) + 1 - y)]


class _GatherRider:
    def __init__(self, bufs, split):
        self.inputs = list(bufs)
        self.split = list(split)
        self.out_shapes = [jax.ShapeDtypeStruct(b.shape, b.dtype) for b in bufs]
        self.aliases = {i: i for i in range(len(bufs))}
        self.sems = [pltpu.SemaphoreType.DMA((len(bufs), 3))] * 4

    def _rows(self, k, buf, c, mine=True):
        r = buf.shape[1]
        if not self.split[k]:
            return pl.ds(0, r)
        return pl.ds((c if mine else 1 - c) * (r // 2), r // 2)

    def _ici(self, k, d, bufs, sems, c, px, py, block):
        rows = self._rows(k, bufs[k], c)
        return pltpu.make_async_remote_copy(
            src_ref=bufs[k].at[block, rows, :], dst_ref=bufs[k].at[block, rows, :], send_sem=sems[0].at[k, d],
            recv_sem=sems[1].at[k, d], device_id=(px, py, c), device_id_type=MESH)

    def _pass(self, k, d, bufs, sems, x, y, c, block, mine):
        rows = self._rows(k, bufs[k], c, mine)
        return pltpu.make_async_remote_copy(
            src_ref=bufs[k].at[block, rows, :], dst_ref=bufs[k].at[block, rows, :], send_sem=sems[2].at[k, d],
            recv_sem=sems[3].at[k, d], device_id=(x, y, 1 - c), device_id_type=MESH)

    def first(self, in_refs, bufs, sems):
        x, y, c = _place()
        for k in range(len(bufs)):
            for d, (px, py, _) in enumerate(_other_chips(x, y)):
                self._ici(k, d, bufs, sems, c, px, py, 2 * x + y).start()

    def last(self, in_refs, bufs, sems):
        x, y, c = _place()
        chips = _other_chips(x, y)
        for k in range(len(bufs)):
            for d, (px, py, pj) in enumerate(chips):
                self._ici(k, d, bufs, sems, c, px, py, pj).wait_recv()
                if self.split[k]:
                    self._pass(k, d, bufs, sems, x, y, c, pj, True).start()
        for k in range(len(bufs)):
            for d, (px, py, pj) in enumerate(chips):
                if self.split[k]:
                    self._pass(k, d, bufs, sems, x, y, c, pj, False).wait_recv()
                    self._pass(k, d, bufs, sems, x, y, c, pj, True).wait_send()
                self._ici(k, d, bufs, sems, c, px, py, 2 * x + y).wait_send()


def _hosted_call(body, rider, *, name, grid, in_specs, out_specs, out_shape, scratch_shapes, compiler_params, args):
    if rider is None:
        outs = pl.pallas_call(body, name=name, grid=grid, in_specs=in_specs, out_specs=out_specs, out_shape=out_shape,
                              scratch_shapes=scratch_shapes, compiler_params=compiler_params)(*args)
        return list(outs), []
    n_in, n_out, n_scr = len(in_specs), len(out_specs), len(scratch_shapes)
    r_in, r_out = len(rider.inputs), len(rider.out_shapes)
    compiler_params = _cp(compiler_params.vmem_limit_bytes, ("arbitrary",) * len(grid))

    def full_body(*refs):
        ins = refs[:n_in]
        rins = refs[n_in:n_in + r_in]
        outs = refs[n_in + r_in:n_in + r_in + n_out]
        routs = refs[n_in + r_in + n_out:n_in + r_in + n_out + r_out]
        rest = refs[n_in + r_in + n_out + r_out:]
        scr, sems = rest[:n_scr], rest[n_scr:]
        ids = [pl.program_id(a) for a in range(len(grid))]
        is_first = functools.reduce(jnp.logical_and, [i == 0 for i in ids])
        is_last = functools.reduce(jnp.logical_and, [i == g - 1 for i, g in zip(ids, grid)])

        @pl.when(is_first)
        def _():
            rider.first(rins, routs, sems)

        body(*ins, *outs, *scr)

        @pl.when(is_last)
        def _():
            rider.last(rins, routs, sems)

    res = pl.pallas_call(
        full_body, name=name, grid=grid, in_specs=list(in_specs) + [ANY] * r_in,
        out_specs=list(out_specs) + [ANY] * r_out, out_shape=list(out_shape) + list(rider.out_shapes),
        input_output_aliases={n_in + i: n_out + o for i, o in rider.aliases.items()},
        scratch_shapes=list(scratch_shapes) + list(rider.sems), compiler_params=compiler_params,
    )(*args, *rider.inputs)
    return list(res[:n_out]), list(res[n_out:])


def _exchange_now(rider, *, name):
    r_in = len(rider.inputs)

    def body(*refs):
        rins = refs[:r_in]
        routs = refs[r_in:r_in + len(rider.out_shapes)]
        sems = refs[r_in + len(rider.out_shapes):]
        rider.first(rins, routs, sems)
        rider.last(rins, routs, sems)

    return pl.pallas_call(
        body, name=name, in_specs=[ANY] * r_in, out_specs=[ANY] * len(rider.out_shapes), out_shape=list(rider.out_shapes),
        input_output_aliases=dict(rider.aliases), scratch_shapes=list(rider.sems),
    )(*rider.inputs)


def _to_slot(where, a, dtype, *, name):
    r, cols = a.shape
    tr = _tile(r, 256, 16) if r > 256 else r

    def body(w_ref, a_ref, o_ref):
        o_ref[0] = a_ref[...].astype(o_ref.dtype)

    return pl.pallas_call(
        body, name=name,
        grid_spec=pltpu.PrefetchScalarGridSpec(
            num_scalar_prefetch=1, grid=(r // tr,),
            in_specs=[pl.BlockSpec((tr, cols), lambda i, w: (i, 0))],
            out_specs=pl.BlockSpec((1, tr, cols), lambda i, w: (w[1], i, 0))),
        out_shape=jax.ShapeDtypeStruct((N_CHIPS, r, cols), dtype), compiler_params=_cp(VMEM_BIG),
    )(where, a)


def _tri_inv(a_strict):
    r, c = _masks64()
    eye = (r == c).astype(F32)
    blk16 = (r // 16) == (c // 16)
    blk32 = (r // 32) == (c // 32)
    ld = jnp.where(blk16, a_strict, 0.0)
    x = eye - ld
    p = _nn(ld, ld)
    yield
    for step in range(3):
        xp = _nn(x, p)
        if step < 2:
            p = _nn(p, p)
        x = x + xp
        yield
    for lk in (jnp.where(jnp.logical_and(blk32, jnp.logical_not(blk16)), a_strict, 0.0),
               jnp.where(blk32, 0.0, a_strict)):
        y = x - eye
        s = lk + _nn(y, lk)
        yield
        x = x - s - _nn(s, y)
        yield
    return x


def _dn_gates(sa, alog, dtb, chunk_in_seq):
    rows = _iota2((CHUNK, LANE), 0)
    valid = jnp.logical_or(rows >= N_PAD, chunk_in_seq > 0)
    beta_t = _sigmoid(sa)
    ea = jnp.exp(alog)
    g_t = jnp.where(valid, -ea * _softplus(sa + dtb), 0.0)
    r, c = _masks64()
    ltri = (r >= c).astype(F32)
    gam_t = _nn_hi(ltri, g_t)
    return beta_t, g_t, gam_t, valid, ea


def _dn_intra_fwd(qn, kn, v, projp, alog_row, dtb_row, *, nc_seq, name):
    n = qn.shape[0]
    nct = n // CHUNK
    hw = NH * DN_D
    scale = DN_D ** -0.5

    grp = _group(nc_seq)

    def inner(gi, q_ref, k_ref, v_ref, sa_ref, al_ref, dt_ref, u_ref, w_ref, qg_ref, kd_ref, p_ref, t_ref, gl_ref):
        ci = (pl.program_id(0) * grp + gi) % nc_seq
        beta_t, _, gam_t, _, _ = _dn_gates(sa_ref[...], al_ref[...], dt_ref[...], ci)
        yield
        gam_tt = gam_t.T
        r, c = _masks64()
        incl = r >= c
        strict = r > c

        def head(h):
            sl = slice(h * DN_D, (h + 1) * DN_D)
            beta_w = jnp.broadcast_to(beta_t[:, h:h + 1], (CHUNK, DN_D))
            gam_w = jnp.broadcast_to(gam_t[:, 4 + h:5 + h], (CHUNK, DN_D))
            gam_row = gam_tt[4 + h:5 + h, :]
            gl = gam_t[CHUNK - 1:CHUNK, 4 + h:5 + h]
            dec = jnp.exp(jnp.where(incl, gam_w[:, 0:CHUNK] - gam_row, -jnp.inf))
            kh = k_ref[:, sl].astype(F32)
            qh = q_ref[:, sl].astype(F32) * scale
            vh = v_ref[:, sl].astype(F32)
            kk = _nt(kh, kh)
            qk = _nt(qh, kh)
            yield
            a = jnp.where(strict, beta_w[:, 0:CHUNK] * kk * dec, 0.0)
            tm = yield from _tri_inv(a)
            egam_w = jnp.exp(gam_w)
            u_ref[:, sl] = _nn(tm, beta_w * vh).astype(u_ref.dtype)
            w_ref[:, sl] = _nn(tm, (beta_w * egam_w) * kh).astype(w_ref.dtype)
            qg_ref[:, sl] = (egam_w * qh).astype(qg_ref.dtype)
            kd_ref[:, sl] = (jnp.exp(gl - gam_w) * kh).astype(kd_ref.dtype)
            p_ref[0, h] = qk * dec
            t_ref[0, h] = tm
            gl_ref[0, h:h + 1, :] = jnp.broadcast_to(jnp.exp(gl), (1, LANE))

        yield from _round_robin([head(h) for h in range(NH)])

    rows = grp * CHUNK
    row = pl.BlockSpec((rows, hw), lambda i: (i, 0))
    vec = pl.BlockSpec((1, LANE), lambda i: (0, 0))
    mat = pl.BlockSpec((grp, NH, CHUNK, CHUNK), lambda i: (i, 0, 0, 0))
    big = jax.ShapeDtypeStruct((n, hw), BF16)
    msd = jax.ShapeDtypeStruct((nct, NH, CHUNK, CHUNK), F32)
    kinds = ["row"] * 4 + ["whole"] * 2 + ["row"] * 4 + ["lead"] * 3
    return pl.pallas_call(
        _per_chunk(inner, kinds, grp), name=name, grid=(nct // grp,),
        in_specs=[row, row, row, pl.BlockSpec((rows, LANE), lambda i: (i, 0)), vec, vec],
        out_specs=[row, row, row, row, mat, mat, pl.BlockSpec((grp, NH, LANE), lambda i: (i, 0, 0))],
        out_shape=[big, big, big, big, msd, msd, jax.ShapeDtypeStruct((nct, NH, LANE), F32)],
        compiler_params=_cp(VMEM_BIG),
    )(qn, kn, v, projp, alog_row, dtb_row)


def _dn_scan_fwd(u, w, qg, kd, p, gl, *, bsz, nc_seq, name, rider=None):
    hw = NH * DN_D
    t_seq = nc_seq * CHUNK
    u, w, qg, kd = (z.reshape(bsz, t_seq, hw) for z in (u, w, qg, kd))
    p = p.reshape(bsz, nc_seq, NH, CHUNK, CHUNK)
    gl = gl.reshape(bsz, nc_seq, NH, LANE)

    def body(u_ref, w_ref, qg_ref, kd_ref, p_ref, gl_ref, o_ref, vn_ref, hist_ref, s_ref):
        @pl.when(pl.program_id(0) == 0)
        def _():
            s_ref[...] = jnp.zeros_like(s_ref)

        def chain(b, h):
            sl = slice(h * DN_D, (h + 1) * DN_D)
            s = s_ref[b, h]
            hist_ref[b, 0, h] = s.astype(hist_ref.dtype)
            ws = _nn(w_ref[b, :, sl], s)
            qs = _nn(qg_ref[b, :, sl], s)
            yield
            vn = u_ref[b, :, sl] - ws
            vn_ref[b, :, sl] = vn.astype(vn_ref.dtype)
            o_ref[b, :, sl] = (qs + _nn(p_ref[b, 0, h], vn)).astype(o_ref.dtype)
            s_ref[b, h] = gl_ref[b, 0, h:h + 1, :] * s + _tn(kd_ref[b, :, sl], vn)

        _run([chain(b, h) for b in range(bsz) for h in range(NH)])

    row = pl.BlockSpec((bsz, CHUNK, hw), lambda i: (0, i, 0))
    outs, ridden = _hosted_call(
        body, rider, name=name, grid=(nc_seq,),
        in_specs=[row, row, row, row, pl.BlockSpec((bsz, 1, NH, CHUNK, CHUNK), lambda i: (0, i, 0, 0, 0)),
                  pl.BlockSpec((bsz, 1, NH, LANE), lambda i: (0, i, 0, 0))],
        out_specs=[row, row, pl.BlockSpec((bsz, 1, NH, DN_D, DN_D), lambda i: (0, i, 0, 0, 0))],
        out_shape=[jax.ShapeDtypeStruct((bsz, t_seq, hw), BF16), jax.ShapeDtypeStruct((bsz, t_seq, hw), BF16),
                   jax.ShapeDtypeStruct((bsz, nc_seq, NH, DN_D, DN_D), BF16)],
        scratch_shapes=[pltpu.VMEM((bsz, NH, DN_D, DN_D), F32)],
        compiler_params=_cp(VMEM_BIG, ("arbitrary",)), args=(u, w, qg, kd, p, gl))
    o, vn, hist = outs
    return o.reshape(bsz * t_seq, hw), vn.reshape(bsz * t_seq, hw), hist, ridden


def _dn_scan_bwd(do, w, qg, kd, vn, p, gl, hist, *, bsz, nc_seq, name):
    hw = NH * DN_D
    t_seq = nc_seq * CHUNK
    do, w, qg, kd, vn = (z.reshape(bsz, t_seq, hw) for z in (do, w, qg, kd, vn))
    p = p.reshape(bsz, nc_seq, NH, CHUNK, CHUNK)
    gl = gl.reshape(bsz, nc_seq, NH, LANE)

    def body(do_ref, w_ref, qg_ref, kd_ref, vn_ref, p_ref, gl_ref, hist_ref,
             du_ref, dw_ref, dqg_ref, dkd_ref, dgl_ref, ds_ref):
        @pl.when(pl.program_id(0) == 0)
        def _():
            ds_ref[...] = jnp.zeros_like(ds_ref)

        def chain(b, h):
            sl = slice(h * DN_D, (h + 1) * DN_D)
            s = hist_ref[b, 0, h]
            dsn = ds_ref[b, h]
            doh = do_ref[b, :, sl]
            vnh = vn_ref[b, :, sl]
            kdh = kd_ref[b, :, sl]
            dvn = _tn(p_ref[b, 0, h], doh) + _nn(kdh, dsn)
            du_ref[b, :, sl] = dvn.astype(du_ref.dtype)
            dqg_ref[b, :, sl] = _nt(doh, s).astype(dqg_ref.dtype)
            dkd_ref[b, :, sl] = _nt(vnh, dsn).astype(dkd_ref.dtype)
            ds_part = _tn(qg_ref[b, :, sl], doh) + gl_ref[b, 0, h:h + 1, :] * dsn
            dgl = jnp.sum(jnp.sum(dsn * s, axis=0, keepdims=True), axis=1, keepdims=True)
            dgl_ref[b, 0, h:h + 1, :] = jnp.broadcast_to(dgl, (1, LANE))
            yield
            dw_ref[b, :, sl] = (-_nt(dvn, s)).astype(dw_ref.dtype)
            ds_ref[b, h] = ds_part - _tn(w_ref[b, :, sl], dvn)

        _run([chain(b, h) for b in range(bsz) for h in range(NH)])

    rev = lambda i: nc_seq - 1 - i
    row = pl.BlockSpec((bsz, CHUNK, hw), lambda i: (0, rev(i), 0))
    mat = pl.BlockSpec((bsz, 1, NH, CHUNK, CHUNK), lambda i: (0, rev(i), 0, 0, 0))
    glb = pl.BlockSpec((bsz, 1, NH, LANE), lambda i: (0, rev(i), 0, 0))
    big = jax.ShapeDtypeStruct((bsz, t_seq, hw), BF16)
    outs = pl.pallas_call(
        body, name=name, grid=(nc_seq,),
        in_specs=[row, row, row, row, row, mat, glb,
                  pl.BlockSpec((bsz, 1, NH, DN_D, DN_D), lambda i: (0, rev(i), 0, 0, 0))],
        out_specs=[row, row, row, row, glb],
        out_shape=[big, big, big, big, jax.ShapeDtypeStruct((bsz, nc_seq, NH, LANE), F32)],
        scratch_shapes=[pltpu.VMEM((bsz, NH, DN_D, DN_D), F32)],
        compiler_params=_cp(VMEM_BIG, ("arbitrary",)),
    )(do, w, qg, kd, vn, p, gl, hist)
    du, dw, dqg, dkd, dgl = outs
    n = bsz * t_seq
    return (du.reshape(n, hw), dw.reshape(n, hw), dqg.reshape(n, hw), dkd.reshape(n, hw),
            dgl.reshape(bsz * nc_seq, NH, LANE))


def _dn_intra_bwd(qn, kn, v, projp, alog_row, dtb_row, u, w, tmat, du, dw, dqg, dkd, do, vn, dgl, *, nc_seq, name,
                  rider=None):
    n = qn.shape[0]
    nct = n // CHUNK
    hw = NH * DN_D
    scale = DN_D ** -0.5

    grp = _group(nc_seq)

    def inner(gi, q_ref, k_ref, v_ref, sa_ref, al_ref, dt_ref, u_ref, w_ref, t_ref, du_ref, dw_ref, dqg_ref, dkd_ref,
              do_ref, vn_ref, dgl_ref, dq_ref, dk_ref, dv_ref, dsa_ref, dal_ref, ddt_ref):
        ci = (pl.program_id(0) * grp + gi) % nc_seq
        sa = sa_ref[...]
        beta_t, g_t, gam_t, valid, ea = _dn_gates(sa, al_ref[...], dt_ref[...], ci)
        yield
        lane = _iota2((CHUNK, LANE), 1)
        gates_t = jnp.where(lane < 4, beta_t, gam_t).T
        r, c = _masks64()
        incl, strict, upper, supper = r >= c, r > c, r <= c, r < c
        rows1 = _iota2((CHUNK, 1), 0)
        acc = [jnp.zeros((CHUNK, LANE), F32)]

        def head(h):
            sl = slice(h * DN_D, (h + 1) * DN_D)
            beta_w = jnp.broadcast_to(beta_t[:, h:h + 1], (CHUNK, DN_D))
            gam_w = jnp.broadcast_to(gam_t[:, 4 + h:5 + h], (CHUNK, DN_D))
            beta_s, gam_s = beta_w[:, 0:CHUNK], gam_w[:, 0:CHUNK]
            beta_row = gates_t[h:h + 1, :]
            gam_row = gates_t[4 + h:5 + h, :]
            gl = gam_t[CHUNK - 1:CHUNK, 4 + h:5 + h]
            dec = jnp.exp(jnp.where(incl, gam_s - gam_row, -jnp.inf))
            dec_t = jnp.exp(jnp.where(upper, gam_row - gam_s, -jnp.inf))
            egam_w = jnp.exp(gam_w)
            ekd_w = jnp.exp(gl - gam_w)
            kh = k_ref[:, sl].astype(F32)
            qh = q_ref[:, sl].astype(F32) * scale
            vh = v_ref[:, sl].astype(F32)
            uh = u_ref[:, sl]
            wh = w_ref[:, sl]
            doh = do_ref[:, sl]
            vnh = vn_ref[:, sl]
            kk = _nt(kh, kh)
            qk = _nt(qh, kh)
            qk_t = _nt(kh, qh)
            dp = _nt(doh, vnh)
            dp_t = _nt(vnh, doh)
            tm_t = t_ref[0, h].T
            dvb = _nn(tm_t, du_ref[:, sl])
            dkg = _nn(tm_t, dw_ref[:, sl])
            yield
            m = _nt(dvb, uh) + _nt(dkg, wh)
            m_t = _nt(uh, dvb) + _nt(wh, dkg)
            yield
            da = jnp.where(strict, -m, 0.0)
            da_t = jnp.where(supper, -m_t, 0.0)
            a = jnp.where(strict, beta_s * kk * dec, 0.0)
            a_t = jnp.where(supper, beta_row * kk * dec_t, 0.0)
            dad = da * dec
            dad_t = da_t * dec_t
            dpm = jnp.where(incl, dp, 0.0)
            dpm_t = jnp.where(upper, dp_t, 0.0)
            e = da * a + dpm * (qk * dec)
            e_t = da_t * a_t + dpm_t * (qk_t * dec_t)
            dqgh = dqg_ref[:, sl].astype(F32)
            dkdh = dkd_ref[:, sl].astype(F32)
            bg_w = beta_w * egam_w
            dkh = (_nn(beta_s * dad, kh) + _nn(beta_row * dad_t, kh) + _nn(dpm_t * dec_t, qh)
                   + bg_w * dkg + ekd_w * dkdh)
            dqh = _nn(dpm * dec, kh) + egam_w * dqgh
            t_kd = dkdh * (ekd_w * kh)
            dbeta = (jnp.sum(dad * kk, axis=1, keepdims=True)
                     + jnp.sum(dkg * (egam_w * kh) + dvb * vh, axis=1, keepdims=True))
            dgam = (jnp.sum(e - e_t, axis=1, keepdims=True)
                    + jnp.sum(dkg * (bg_w * kh) + dqgh * (egam_w * qh) - t_kd, axis=1, keepdims=True))
            dgam_last = (jnp.sum(jnp.sum(t_kd, axis=0, keepdims=True), axis=1, keepdims=True)
                         + dgl_ref[0, h:h + 1, 0:1] * jnp.exp(gl))
            dgam = dgam + jnp.where(rows1 == CHUNK - 1, dgam_last, 0.0)
            dq_ref[:, sl] = (dqh * scale).astype(dq_ref.dtype)
            dk_ref[:, sl] = dkh.astype(dk_ref.dtype)
            dv_ref[:, sl] = (beta_w * dvb).astype(dv_ref.dtype)
            acc[0] = acc[0] + jnp.where(lane == h, dbeta, 0.0) + jnp.where(lane == 4 + h, dgam, 0.0)

        yield from _round_robin([head(h) for h in range(NH)])
        acc_t = acc[0]
        dg_t = _nn_hi(upper.astype(F32), acc_t)
        ddb = acc_t * beta_t * (1.0 - beta_t)
        dda = jnp.where(valid, dg_t * (-ea) * _sigmoid(sa + dt_ref[...]), 0.0)
        dsa_ref[...] = jnp.where(lane < 4, ddb, jnp.where(lane < 8, dda, 0.0)).astype(dsa_ref.dtype)
        in_g = jnp.logical_and(lane >= 4, lane < 8)
        dal = jnp.sum(jnp.where(in_g, dg_t * g_t, 0.0), axis=0, keepdims=True)
        ddt = jnp.sum(jnp.where(in_g, dda, 0.0), axis=0, keepdims=True)
        _accumulate(dal_ref, dal, gi)
        _accumulate(ddt_ref, ddt, gi)

    rows = grp * CHUNK
    row = pl.BlockSpec((rows, hw), lambda i: (i, 0))
    vec = pl.BlockSpec((1, LANE), lambda i: (0, 0))
    mat = pl.BlockSpec((grp, NH, CHUNK, CHUNK), lambda i: (i, 0, 0, 0))
    glb = pl.BlockSpec((grp, NH, LANE), lambda i: (i, 0, 0))
    big = jax.ShapeDtypeStruct((n, hw), F32)
    v128 = jax.ShapeDtypeStruct((1, LANE), F32)
    kinds = (["row"] * 4 + ["whole"] * 2 + ["row"] * 2 + ["lead"] + ["row"] * 6 + ["lead"]
             + ["row"] * 4 + ["whole"] * 2)
    outs, ridden = _hosted_call(
        _per_chunk(inner, kinds, grp), rider, name=name, grid=(nct // grp,),
        in_specs=[row, row, row, pl.BlockSpec((rows, LANE), lambda i: (i, 0)), vec, vec,
                  row, row, mat, row, row, row, row, row, row, glb],
        out_specs=[row, row, row, pl.BlockSpec((rows, LANE), lambda i: (i, 0)), vec, vec],
        out_shape=[big, big, big, jax.ShapeDtypeStruct((n, LANE), BF16), v128, v128],
        scratch_shapes=[], compiler_params=_cp(VMEM_BIG, ("arbitrary",)),
        args=(qn, kn, v, projp, alog_row, dtb_row, u, w, tmat, du, dw, dqg, dkd, do, vn, dgl))
    return (*outs, ridden)


GQ_W = NH * GLA_DK
GV_W = NH * GLA_DV
GLA_NORM = 16.0
MID = CHUNK // 2


def _gla_gates(sb, w2p, gb, chunk_in_seq):
    rows = _iota2((CHUNK, GQ_W), 0)
    valid = jnp.logical_or(rows >= N_PAD, chunk_in_seq > 0)
    graw = _nn_hi(sb, w2p) + gb
    yield
    g = jnp.where(valid, _logsigmoid(graw) * (1.0 / GLA_NORM), 0.0)
    r, c = _masks64()
    bcum = _nn_hi((r >= c).astype(F32), g)
    yield
    return graw, bcum, valid


def _head_mask(h):
    lane = _iota2((1, GQ_W), 1)
    return jnp.logical_and(lane >= h * GLA_DK, lane < (h + 1) * GLA_DK)


def _gla_intra_fwd(projp, gates, w2p, gb, *, nc_seq, name):
    n = projp.shape[0]
    nct = n // CHUNK
    scale = GLA_DK ** -0.5

    grp = _group(nc_seq)
    rows = grp * CHUNK

    def inner(gi, qk_ref, v_ref, sb_ref, w2_ref, gb_ref, oi_ref, qg_ref, kd_ref, gl_ref):
        ci = (pl.program_id(0) * grp + gi) % nc_seq
        _, bc, _ = yield from _gla_gates(sb_ref[...], w2_ref[...], gb_ref[...], ci)
        bref = bc[MID:MID + 1, :]
        bl = bc[CHUNK - 1:CHUNK, :]
        q = qk_ref[:, 0:GQ_W].astype(F32) * scale
        k = qk_ref[:, GQ_W:2 * GQ_W].astype(F32)
        qi = q * jnp.exp(bc - bref)
        ki = k * jnp.exp(bref - bc)
        qg_ref[...] = (q * jnp.exp(bc)).astype(qg_ref.dtype)
        kd_ref[...] = (k * jnp.exp(bl - bc)).astype(kd_ref.dtype)
        gl_ref[0] = jnp.exp(bl)
        r, c = _masks64()
        incl = r >= c
        a = [jnp.where(incl, _nt(jnp.where(_head_mask(h), qi, 0.0), ki), 0.0) for h in range(NH)]
        yield
        for h in range(NH):
            oi_ref[:, h * GLA_DV:(h + 1) * GLA_DV] = _nn(a[h], v_ref[:, h * GLA_DV:(h + 1) * GLA_DV]).astype(oi_ref.dtype)

    kinds = ["row"] * 3 + ["whole"] * 2 + ["row"] * 3 + ["lead"]
    return pl.pallas_call(
        _per_chunk(inner, kinds, grp), name=name, grid=(nct // grp,),
        in_specs=[pl.BlockSpec((rows, 2 * GQ_W), lambda i: (i, C_GQK // (2 * GQ_W))),
                  pl.BlockSpec((rows, GV_W), lambda i: (i, C_GV // GV_W)),
                  pl.BlockSpec((rows, LANE), lambda i: (i, 1)),
                  pl.BlockSpec((LANE, GQ_W), lambda i: (0, 0)), pl.BlockSpec((1, GQ_W), lambda i: (0, 0))],
        out_specs=[pl.BlockSpec((rows, GV_W), lambda i: (i, 0)), pl.BlockSpec((rows, GQ_W), lambda i: (i, 0)),
                   pl.BlockSpec((rows, GQ_W), lambda i: (i, 0)), pl.BlockSpec((grp, 1, GQ_W), lambda i: (i, 0, 0))],
        out_shape=[jax.ShapeDtypeStruct((n, GV_W), BF16), jax.ShapeDtypeStruct((n, GQ_W), BF16),
                   jax.ShapeDtypeStruct((n, GQ_W), BF16), jax.ShapeDtypeStruct((nct, 1, GQ_W), F32)],
        compiler_params=_cp(VMEM_BIG),
    )(projp, projp, gates, w2p, gb)


def _gla_scan_fwd(oi, qg, kd, gl, projp, *, bsz, nc_seq, name, rider=None):
    t_seq = nc_seq * CHUNK
    oi = oi.reshape(bsz, t_seq, GV_W)
    qg, kd = qg.reshape(bsz, t_seq, GQ_W), kd.reshape(bsz, t_seq, GQ_W)
    gl = gl.reshape(bsz, nc_seq, 1, GQ_W)
    pj = projp.reshape(bsz, t_seq, PW)

    def body(oi_ref, qg_ref, kd_ref, gl_ref, v_ref, o_ref, hist_ref, st_ref):
        @pl.when(pl.program_id(0) == 0)
        def _():
            st_ref[...] = jnp.zeros_like(st_ref)

        for b in range(bsz):
            st = st_ref[b]
            hist_ref[b, 0] = st.astype(hist_ref.dtype)
            qgb = qg_ref[b]
            kdb = kd_ref[b]
            upd = jnp.zeros((GLA_DV, GQ_W), F32)
            for h in range(NH):
                sl = slice(h * GLA_DV, (h + 1) * GLA_DV)
                m = _head_mask(h)
                o_ref[b, :, sl] = (oi_ref[b, :, sl] + _nt(jnp.where(m, qgb, 0.0), st)).astype(o_ref.dtype)
                upd = upd + jnp.where(m, _tn(v_ref[b, :, sl], kdb), 0.0)
            st_ref[b] = gl_ref[b, 0] * st + upd

    outs, ridden = _hosted_call(
        body, rider, name=name, grid=(nc_seq,),
        in_specs=[pl.BlockSpec((bsz, CHUNK, GV_W), lambda i: (0, i, 0)),
                  pl.BlockSpec((bsz, CHUNK, GQ_W), lambda i: (0, i, 0)),
                  pl.BlockSpec((bsz, CHUNK, GQ_W), lambda i: (0, i, 0)),
                  pl.BlockSpec((bsz, 1, 1, GQ_W), lambda i: (0, i, 0, 0)),
                  pl.BlockSpec((bsz, CHUNK, GV_W), lambda i: (0, i, C_GV // GV_W))],
        out_specs=[pl.BlockSpec((bsz, CHUNK, GV_W), lambda i: (0, i, 0)),
                   pl.BlockSpec((bsz, 1, GLA_DV, GQ_W), lambda i: (0, i, 0, 0))],
        out_shape=[jax.ShapeDtypeStruct((bsz, t_seq, GV_W), BF16),
                   jax.ShapeDtypeStruct((bsz, nc_seq, GLA_DV, GQ_W), BF16)],
        scratch_shapes=[pltpu.VMEM((bsz, GLA_DV, GQ_W), F32)],
        compiler_params=_cp(VMEM_BIG, ("arbitrary",)), args=(oi, qg, kd, gl, pj))
    return outs[0].reshape(bsz * t_seq, GV_W), outs[1], ridden


def _gla_scan_bwd(do, qg, kd, gl, projp, hist, *, bsz, nc_seq, name):
    t_seq = nc_seq * CHUNK
    do = do.reshape(bsz, t_seq, GV_W)
    qg, kd = qg.reshape(bsz, t_seq, GQ_W), kd.reshape(bsz, t_seq, GQ_W)
    gl = gl.reshape(bsz, nc_seq, 1, GQ_W)
    pj = projp.reshape(bsz, t_seq, PW)

    def body(do_ref, qg_ref, kd_ref, gl_ref, v_ref, hist_ref, dqg_ref, dkd_ref, dv_ref, dgl_ref, dst_ref):
        @pl.when(pl.program_id(0) == 0)
        def _():
            dst_ref[...] = jnp.zeros_like(dst_ref)

        for b in range(bsz):
            st = hist_ref[b, 0]
            dst = dst_ref[b]
            qgb = qg_ref[b]
            kdb = kd_ref[b]
            dqg = jnp.zeros((CHUNK, GQ_W), F32)
            dkd = jnp.zeros((CHUNK, GQ_W), F32)
            add = jnp.zeros((GLA_DV, GQ_W), F32)
            for h in range(NH):
                sl = slice(h * GLA_DV, (h + 1) * GLA_DV)
                m = _head_mask(h)
                doh = do_ref[b, :, sl]
                vh = v_ref[b, :, sl]
                dqg = dqg + jnp.where(m, _nn(doh, st), 0.0)
                dkd = dkd + jnp.where(m, _nn(vh, dst), 0.0)
                dv_ref[b, :, sl] = _nt(jnp.where(m, kdb, 0.0), dst).astype(dv_ref.dtype)
                add = add + jnp.where(m, _tn(doh, qgb), 0.0)
            dqg_ref[b] = dqg.astype(dqg_ref.dtype)
            dkd_ref[b] = dkd.astype(dkd_ref.dtype)
            dgl_ref[b, 0] = jnp.sum(dst * st, axis=0, keepdims=True)
            dst_ref[b] = gl_ref[b, 0] * dst + add

    rev = lambda i: nc_seq - 1 - i
    outs = pl.pallas_call(
        body, name=name, grid=(nc_seq,),
        in_specs=[pl.BlockSpec((bsz, CHUNK, GV_W), lambda i: (0, rev(i), 0)),
                  pl.BlockSpec((bsz, CHUNK, GQ_W), lambda i: (0, rev(i), 0)),
                  pl.BlockSpec((bsz, CHUNK, GQ_W), lambda i: (0, rev(i), 0)),
                  pl.BlockSpec((bsz, 1, 1, GQ_W), lambda i: (0, rev(i), 0, 0)),
                  pl.BlockSpec((bsz, CHUNK, GV_W), lambda i: (0, rev(i), C_GV // GV_W)),
                  pl.BlockSpec((bsz, 1, GLA_DV, GQ_W), lambda i: (0, rev(i), 0, 0))],
        out_specs=[pl.BlockSpec((bsz, CHUNK, GQ_W), lambda i: (0, rev(i), 0)),
                   pl.BlockSpec((bsz, CHUNK, GQ_W), lambda i: (0, rev(i), 0)),
                   pl.BlockSpec((bsz, CHUNK, GV_W), lambda i: (0, rev(i), 0)),
                   pl.BlockSpec((bsz, 1, 1, GQ_W), lambda i: (0, rev(i), 0, 0))],
        out_shape=[jax.ShapeDtypeStruct((bsz, t_seq, GQ_W), BF16), jax.ShapeDtypeStruct((bsz, t_seq, GQ_W), BF16),
                   jax.ShapeDtypeStruct((bsz, t_seq, GV_W), BF16), jax.ShapeDtypeStruct((bsz, nc_seq, 1, GQ_W), F32)],
        scratch_shapes=[pltpu.VMEM((bsz, GLA_DV, GQ_W), F32)],
        compiler_params=_cp(VMEM_BIG, ("arbitrary",)),
    )(do, qg, kd, gl, pj, hist)
    n = bsz * t_seq
    return (outs[0].reshape(n, GQ_W), outs[1].reshape(n, GQ_W), outs[2].reshape(n, GV_W),
            outs[3].reshape(bsz * nc_seq, 1, GQ_W))


def _gla_intra_bwd(projp, gates, w2p, gb, do, dqg, dkd, dvi, dgl, *, nc_seq, name):
    n = projp.shape[0]
    nct = n // CHUNK
    scale = GLA_DK ** -0.5

    grp = _group(nc_seq)
    rows = grp * CHUNK

    def inner(gi, qk_ref, v_ref, sb_ref, w2_ref, gb_ref, do_ref, dqg_ref, dkd_ref, dvi_ref, dgl_ref,
              dqk_ref, dv_ref, dsb_ref, dw2_ref, dgb_ref):
        ci = (pl.program_id(0) * grp + gi) % nc_seq
        sb = sb_ref[...]
        w2 = w2_ref[...]
        graw, bc, valid = yield from _gla_gates(sb, w2, gb_ref[...], ci)
        bref = bc[MID:MID + 1, :]
        bl = bc[CHUNK - 1:CHUNK, :]
        q = qk_ref[:, 0:GQ_W].astype(F32) * scale
        k = qk_ref[:, GQ_W:2 * GQ_W].astype(F32)
        ex1 = jnp.exp(bc - bref)
        ex2 = jnp.exp(bref - bc)
        eb = jnp.exp(bc)
        ekd = jnp.exp(bl - bc)
        qi, ki = q * ex1, k * ex2
        r, c = _masks64()
        incl = r >= c
        upper = r <= c
        a_t, da, da_t = [], [], []
        for h in range(NH):
            sl = slice(h * GLA_DV, (h + 1) * GLA_DV)
            doh = do_ref[:, sl]
            vh = v_ref[:, sl]
            a_t.append(jnp.where(upper, _nt(jnp.where(_head_mask(h), ki, 0.0), qi), 0.0))
            da.append(jnp.where(incl, _nt(doh, vh), 0.0))
            da_t.append(jnp.where(upper, _nt(vh, doh), 0.0))
        yield
        dqi = jnp.zeros((CHUNK, GQ_W), F32)
        dki = jnp.zeros((CHUNK, GQ_W), F32)
        for h in range(NH):
            sl = slice(h * GLA_DV, (h + 1) * GLA_DV)
            m = _head_mask(h)
            dv_ref[:, sl] = (_nn(a_t[h], do_ref[:, sl]) + dvi_ref[:, sl]).astype(dv_ref.dtype)
            dqi = dqi + jnp.where(m, _nn(da[h], ki), 0.0)
            dki = dki + jnp.where(m, _nn(da_t[h], qi), 0.0)
        yield
        dqg = dqg_ref[...].astype(F32)
        dkd = dkd_ref[...].astype(F32)
        dqk_ref[:, 0:GQ_W] = ((dqi * ex1 + dqg * eb) * scale).astype(dqk_ref.dtype)
        dqk_ref[:, GQ_W:2 * GQ_W] = (dki * ex2 + dkd * ekd).astype(dqk_ref.dtype)
        t_qi, t_ki, t_kd = dqi * qi, dki * ki, dkd * (k * ekd)
        db = t_qi - t_ki + dqg * (q * eb) - t_kd
        dbref = jnp.sum(t_ki - t_qi, axis=0, keepdims=True)
        dbl = jnp.sum(t_kd, axis=0, keepdims=True) + dgl_ref[0] * jnp.exp(bl)
        rows = _iota2((CHUNK, GQ_W), 0)
        db = db + jnp.where(rows == MID, dbref, 0.0) + jnp.where(rows == CHUNK - 1, dbl, 0.0)
        dg = _nn_hi(upper.astype(F32), db)
        yield
        dgraw = jnp.where(valid, dg * (1.0 / GLA_NORM) * _sigmoid(-graw), 0.0)
        dsb_ref[...] = _nt_hi(dgraw, w2).astype(dsb_ref.dtype)
        dw2 = _tn_hi(sb, dgraw)
        dgb = jnp.sum(dgraw, axis=0, keepdims=True)
        _accumulate(dw2_ref, dw2, gi)
        _accumulate(dgb_ref, dgb, gi)

    rq = pl.BlockSpec((rows, GQ_W), lambda i: (i, 0))
    rv = pl.BlockSpec((rows, GV_W), lambda i: (i, 0))
    kinds = ["row"] * 3 + ["whole"] * 2 + ["row"] * 4 + ["lead"] + ["row"] * 3 + ["whole"] * 2
    return pl.pallas_call(
        _per_chunk(inner, kinds, grp), name=name, grid=(nct // grp,),
        in_specs=[pl.BlockSpec((rows, 2 * GQ_W), lambda i: (i, C_GQK // (2 * GQ_W))),
                  pl.BlockSpec((rows, GV_W), lambda i: (i, C_GV // GV_W)),
                  pl.BlockSpec((rows, LANE), lambda i: (i, 1)),
                  pl.BlockSpec((LANE, GQ_W), lambda i: (0, 0)), pl.BlockSpec((1, GQ_W), lambda i: (0, 0)),
                  rv, rq, rq, rv, pl.BlockSpec((grp, 1, GQ_W), lambda i: (i, 0, 0))],
        out_specs=[pl.BlockSpec((rows, 2 * GQ_W), lambda i: (i, 0)), rv, pl.BlockSpec((rows, LANE), lambda i: (i, 0)),
                   pl.BlockSpec((LANE, GQ_W), lambda i: (0, 0)), pl.BlockSpec((1, GQ_W), lambda i: (0, 0))],
        out_shape=[jax.ShapeDtypeStruct((n, 2 * GQ_W), BF16), jax.ShapeDtypeStruct((n, GV_W), BF16),
                   jax.ShapeDtypeStruct((n, LANE), BF16), jax.ShapeDtypeStruct((LANE, GQ_W), F32),
                   jax.ShapeDtypeStruct((1, GQ_W), F32)],
        compiler_params=_cp(VMEM_BIG, ("arbitrary",)),
    )(projp, projp, gates, w2p, gb, do, dqg, dkd, dvi, dgl)


SECTIONS = ((C_QKV, 1536), (C_DZ, 512), (C_GQK, 512), (C_GV, 512), (C_GR, 512), (C_SA, 128), (C_SB, 128))


def _inproj_bwd(secs, wp, h0, g1, dx1, *, tr, name):
    n, d = h0.shape

    def body(*refs):
        sec_refs = refs[:len(SECTIONS)]
        wp_ref, h0_ref, g_ref, dx1_ref, o_ref, dg_ref = refs[len(SECTIONS):]
        dh = None
        for s_ref, (off, wd) in zip(sec_refs, SECTIONS):
            part = _nt(s_ref[...], wp_ref[:, off:off + wd])
            dh = part if dh is None else dh + part
        dx, dg = _rms_bwd_math(h0_ref[...], g_ref[...], dh)
        o_ref[...] = dx1_ref[...] + dx

        @pl.when(pl.program_id(0) == 0)
        def _():
            dg_ref[...] = dg

        @pl.when(pl.program_id(0) > 0)
        def _():
            dg_ref[...] += dg

    row = pl.BlockSpec((tr, d), lambda i: (i, 0))
    vec = pl.BlockSpec((1, d), lambda i: (0, 0))
    return pl.pallas_call(
        body, name=name, grid=(n // tr,),
        in_specs=[pl.BlockSpec((tr, wd), lambda i: (i, 0)) for _, wd in SECTIONS]
        + [pl.BlockSpec((d, PW), lambda i: (0, 0)), row, vec, row],
        out_specs=[row, vec],
        out_shape=[jax.ShapeDtypeStruct((n, d), F32), jax.ShapeDtypeStruct((1, d), F32)],
        compiler_params=_cp(VMEM_BIG),
    )(*secs, wp, h0, g1, dx1)


def _adamw(w, g, m, v, *, name, emit_grad=False):
    lead = w.ndim - 2
    r, c = w.shape[-2:]
    tr = _tile(r, 256, 8) if r > 256 else r
    c1 = 1.0 - ADAM_B1 ** ADAM_STEP
    c2 = 1.0 - ADAM_B2 ** ADAM_STEP
    n_out = 4 if emit_grad else 3

    def body(w_ref, g_ref, m_ref, v_ref, *out_refs):
        rd = (lambda ref: ref[0]) if lead else (lambda ref: ref[...])
        gv = g_ref[:, 0:c]
        nm = ADAM_B1 * rd(m_ref) + (1.0 - ADAM_B1) * gv
        nv = ADAM_B2 * rd(v_ref) + (1.0 - ADAM_B2) * (gv * gv)
        res = [-ADAM_LR * ((nm / c1) / (jnp.sqrt(nv / c2) + ADAM_EPS) + ADAM_WD * rd(w_ref)), nm, nv, gv]
        for o_ref, val in zip(out_refs, res):
            if lead:
                o_ref[0] = val
            else:
                o_ref[...] = val

    blk = pl.BlockSpec((1,) * lead + (tr, c), lambda i: (0,) * lead + (i, 0))
    gblk = pl.BlockSpec((tr, g.shape[1]), lambda i: (i, 0))
    sds = jax.ShapeDtypeStruct(w.shape, F32)
    return pl.pallas_call(
        body, name=name, grid=(r // tr,), in_specs=[blk, gblk, blk, blk], out_specs=[blk] * n_out,
        out_shape=[sds] * n_out, compiler_params=_cp(VMEM_BIG),
    )(w, g, m, v)


def _pair_sum(where, g, theirs, *, name):
    lead, r, cols = g.shape
    half = r // 2
    tr = _tile(half, 256, 16)
    nh = half // tr

    def body(w_ref, a_ref, b_ref, o_ref):
        o_ref[...] = (a_ref[...] + b_ref[...]).astype(o_ref.dtype)

    blk = pl.BlockSpec((1, tr, cols), lambda s, i, w: (s, i, 0))
    return pl.pallas_call(
        body, name=name,
        grid_spec=pltpu.PrefetchScalarGridSpec(
            num_scalar_prefetch=1, grid=(lead, nh),
            in_specs=[pl.BlockSpec((1, tr, cols), lambda s, i, w: (s, w[0] * nh + i, 0)), blk], out_specs=blk),
        out_shape=jax.ShapeDtypeStruct((lead, half, cols), BF16), compiler_params=_cp(VMEM_BIG),
    )(where, g, theirs)


def _chip_sum(where, pair, q, *, name):
    _, half, cols = pair.shape
    tr = _tile(half, 256, 16)
    nh = half // tr

    def body(w_ref, own_ref, q1_ref, q2_ref, q3_ref, o_ref):
        f = lambda ref: ref[0].astype(F32)
        o_ref[...] = ((f(own_ref) + f(q1_ref)) + f(q2_ref)) + f(q3_ref)

    def peer(d):
        return pl.BlockSpec((1, tr, cols), lambda i, w: ((w[1] + d) % N_CHIPS, i, 0))

    return pl.pallas_call(
        body, name=name,
        grid_spec=pltpu.PrefetchScalarGridSpec(
            num_scalar_prefetch=1, grid=(nh,),
            in_specs=[peer(0), peer(1), peer(2), peer(3)],
            out_specs=pl.BlockSpec((tr, cols), lambda i, w: (w[0] * nh + i, 0))),
        out_shape=jax.ShapeDtypeStruct((2 * half, cols), F32), compiler_params=_cp(VMEM_BIG),
    )(where, pair, q, q, q)


VM = pl.BlockSpec(memory_space=pltpu.VMEM)


def _row_chunks(rows, n_split):
    size = rows // n_split
    assert size * n_split == rows and size % 16 == 0, (rows, n_split)
    return [(s, pl.ds(s * size, size)) for s in range(n_split)], size


D2D_SPLIT = 4
ICI_SPLIT = 2


def _sibling_halves(grads):
    n_arr = len(grads)

    def body(*refs):
        ins = refs[:n_arr]
        theirs = refs[n_arr:2 * n_arr]
        send_sems, recv_sems = refs[2 * n_arr:]
        x, y, c = _place()
        copies = []
        for k in range(n_arr):
            half = ins[k].shape[1] // 2
            chunks, size = _row_chunks(half, D2D_SPLIT)
            for s, dst_rows in chunks:
                give = pltpu.make_async_remote_copy(
                    src_ref=ins[k].at[:, pl.ds((1 - c) * half + s * size, size), :], dst_ref=theirs[k].at[:, dst_rows, :],
                    send_sem=send_sems.at[k, s], recv_sem=recv_sems.at[k, s], device_id=(x, y, 1 - c),
                    device_id_type=MESH)
                give.start()
                copies.append(give)
        for give in copies:
            give.wait()

    halves = [jax.ShapeDtypeStruct((g.shape[0], g.shape[1] // 2, g.shape[2]), F32) for g in grads]
    sem = pltpu.SemaphoreType.DMA((n_arr, D2D_SPLIT))
    return pl.pallas_call(
        body, name="sibling_halves", in_specs=[ANY] * n_arr, out_specs=[ANY] * n_arr, out_shape=halves,
        scratch_shapes=[sem, sem],
    )(*grads)


def _chip_exchange(parts):
    n_arr = len(parts)

    def body(*refs):
        ins = refs[:n_arr]
        outs = refs[n_arr:2 * n_arr]
        send_sems, recv_sems = refs[2 * n_arr:]
        x, y, c = _place()
        me = 2 * x + y
        sends = []
        for k in range(n_arr):
            chunks, _ = _row_chunks(ins[k].shape[1], ICI_SPLIT)
            for d, (px, py, pj) in enumerate(_other_chips(x, y)):
                for s, rows in chunks:
                    cp = pltpu.make_async_remote_copy(
                        src_ref=ins[k].at[pj, rows, :], dst_ref=outs[k].at[me, rows, :], send_sem=send_sems.at[k, d, s],
                        recv_sem=recv_sems.at[k, d, s], device_id=(px, py, c), device_id_type=MESH)
                    cp.start()
                    sends.append(cp)
        for k in range(n_arr):
            chunks, _ = _row_chunks(ins[k].shape[1], ICI_SPLIT)
            for d, (px, py, pj) in enumerate(_other_chips(x, y)):
                for s, rows in chunks:
                    pltpu.make_async_remote_copy(
                        src_ref=ins[k].at[pj, rows, :], dst_ref=outs[k].at[pj, rows, :], send_sem=send_sems.at[k, d, s],
                        recv_sem=recv_sems.at[k, d, s], device_id=(px, py, c), device_id_type=MESH).wait_recv()
        for cp in sends:
            cp.wait_send()

    sem = pltpu.SemaphoreType.DMA((n_arr, 3, ICI_SPLIT))
    return pl.pallas_call(
        body, name="chip_exchange", in_specs=[ANY] * n_arr, out_specs=[ANY] * n_arr,
        out_shape=[jax.ShapeDtypeStruct(p.shape, p.dtype) for p in parts],
        scratch_shapes=[sem, sem],
    )(*parts)


class _SiblingHalvesRider:
    def __init__(self, grads):
        self.inputs = list(grads)
        self.out_shapes = [jax.ShapeDtypeStruct((g.shape[0], g.shape[1] // 2, g.shape[2]), F32) for g in grads]
        self.aliases = {}
        self.sems = [pltpu.SemaphoreType.DMA((len(grads), D2D_SPLIT))] * 2

    def _copies(self, ins, outs, sems):
        x, y, c = _place()
        for k in range(len(ins)):
            half = ins[k].shape[1] // 2
            chunks, size = _row_chunks(half, D2D_SPLIT)
            for s, dst_rows in chunks:
                yield pltpu.make_async_remote_copy(
                    src_ref=ins[k].at[:, pl.ds((1 - c) * half + s * size, size), :], dst_ref=outs[k].at[:, dst_rows, :],
                    send_sem=sems[0].at[k, s], recv_sem=sems[1].at[k, s], device_id=(x, y, 1 - c), device_id_type=MESH)

    def first(self, ins, outs, sems):
        for cp in self._copies(ins, outs, sems):
            cp.start()

    def last(self, ins, outs, sems):
        for cp in self._copies(ins, outs, sems):
            cp.wait()


class _ChipExchangeRider:
    def __init__(self, parts):
        self.inputs = list(parts)
        self.out_shapes = [jax.ShapeDtypeStruct(p.shape, p.dtype) for p in parts]
        self.aliases = {}
        self.sems = [pltpu.SemaphoreType.DMA((len(parts), 3, ICI_SPLIT))] * 2

    def _copies(self, ins, outs, sems, receiving):
        x, y, c = _place()
        for k in range(len(ins)):
            chunks, _ = _row_chunks(ins[k].shape[1], ICI_SPLIT)
            for d, (px, py, pj) in enumerate(_other_chips(x, y)):
                for s, rows in chunks:
                    yield pltpu.make_async_remote_copy(
                        src_ref=ins[k].at[pj, rows, :], dst_ref=outs[k].at[pj if receiving else 2 * x + y, rows, :],
                        send_sem=sems[0].at[k, d, s], recv_sem=sems[1].at[k, d, s], device_id=(px, py, c),
                        device_id_type=MESH)

    def first(self, ins, outs, sems):
        for cp in self._copies(ins, outs, sems, False):
            cp.start()

    def last(self, ins, outs, sems):
        for cp in self._copies(ins, outs, sems, True):
            cp.wait_recv()
        for cp in self._copies(ins, outs, sems, False):
            cp.wait_send()


def _sibling_join(bufs):
    n_arr = len(bufs)

    def body(*refs):
        bufs_out = refs[n_arr:2 * n_arr]
        send_sems, recv_sems = refs[2 * n_arr:]
        x, y, c = _place()
        copies = []
        for k in range(n_arr):
            half = bufs_out[k].shape[0] // 2
            chunks, size = _row_chunks(half, D2D_SPLIT)
            for s, _ in chunks:
                rows = pl.ds(c * half + s * size, size)
                give = pltpu.make_async_remote_copy(
                    src_ref=bufs_out[k].at[rows, :], dst_ref=bufs_out[k].at[rows, :], send_sem=send_sems.at[k, s],
                    recv_sem=recv_sems.at[k, s], device_id=(x, y, 1 - c), device_id_type=MESH)
                give.start()
                copies.append((k, s, half, size, give))
        for k, s, half, size, give in copies:
            rows = pl.ds((1 - c) * half + s * size, size)
            pltpu.make_async_remote_copy(
                src_ref=bufs_out[k].at[rows, :], dst_ref=bufs_out[k].at[rows, :], send_sem=send_sems.at[k, s],
                recv_sem=recv_sems.at[k, s], device_id=(x, y, 1 - c), device_id_type=MESH).wait_recv()
            give.wait_send()

    sem = pltpu.SemaphoreType.DMA((n_arr, D2D_SPLIT))
    return pl.pallas_call(
        body, name="sibling_join", in_specs=[ANY] * n_arr, out_specs=[ANY] * n_arr,
        out_shape=[jax.ShapeDtypeStruct(b.shape, F32) for b in bufs],
        input_output_aliases={k: k for k in range(n_arr)},
        scratch_shapes=[sem, sem],
    )(*bufs)


PACK_ROWS = 48


def _small_allreduce(pack):
    masks = [(dx, dy, dc) for dx in (0, 1) for dy in (0, 1) for dc in (0, 1)][1:]

    def body(p_ref, o_ref, buf, send_sems, recv_sems):
        x, y, c = _place()
        me = 4 * x + 2 * y + c
        buf[me] = p_ref[...]
        sends = []
        for k, (dx, dy, dc) in enumerate(masks):
            peer = (1 - x if dx else x, 1 - y if dy else y, 1 - c if dc else c)
            cp = pltpu.make_async_remote_copy(
                src_ref=p_ref, dst_ref=buf.at[me], send_sem=send_sems.at[k], recv_sem=recv_sems.at[k],
                device_id=peer, device_id_type=MESH)
            cp.start()
            sends.append(cp)
        for k, (dx, dy, dc) in enumerate(masks):
            peer = (1 - x if dx else x, 1 - y if dy else y, 1 - c if dc else c)
            pj = 4 * peer[0] + 2 * peer[1] + peer[2]
            pltpu.make_async_remote_copy(
                src_ref=p_ref, dst_ref=buf.at[pj], send_sem=send_sems.at[k], recv_sem=recv_sems.at[k],
                device_id=peer, device_id_type=MESH).wait_recv()
        for cp in sends:
            cp.wait_send()
        tot = buf[0]
        for k in range(1, 8):
            tot = tot + buf[k]
        o_ref[...] = tot
        o_ref[0:N_META, :] = tot[0:N_META] + tot[N_META:2 * N_META]

    return pl.pallas_call(
        body, name="small_allreduce", in_specs=[VM], out_specs=VM,
        out_shape=jax.ShapeDtypeStruct((PACK_ROWS, D_MODEL), F32),
        scratch_shapes=[pltpu.VMEM((8, PACK_ROWS, D_MODEL), F32), pltpu.SemaphoreType.DMA((7,)),
                        pltpu.SemaphoreType.DMA((7,))],
    )(pack)


def _pad_lanes(vec, offset):
    k = vec.shape[1]
    return jnp.concatenate([jnp.zeros((1, offset), F32), vec, jnp.zeros((1, LANE - offset - k), F32)], axis=1)


def _local_step(x, tgt, meta, norm1_g, wp, conv_w, a_log, dt_bias, dn_norm_g, gla_w2, gla_b, gla_norm_g,
                w_out, norm2_g, w_up, w_down, final_norm_g, late_gather=None, where=None):
    bsz, s_len, d = x.shape
    t_seq = s_len + CHUNK
    nc_seq = t_seq // CHUNK
    n = bsz * t_seq
    tr = _tile(t_seq, 832)
    tt = _tile(t_seq, 416)

    lead = jnp.concatenate([jnp.zeros((N_PAD, d), F32), meta], axis=0)
    h0 = jnp.concatenate([jnp.broadcast_to(lead[None], (bsz, CHUNK, d)), x], axis=1).reshape(n, d)
    tgt_p = jnp.concatenate([jnp.zeros((bsz, CHUNK, d), F32), tgt], axis=1).reshape(n, d)
    alog_row = _pad_lanes(a_log, 4)
    dtb_row = _pad_lanes(dt_bias, 4)
    w2p = jnp.concatenate([gla_w2, jnp.zeros((LANE - GLA_RANK, GQ_W), F32)], axis=0)

    h = _rms_fwd(h0, norm1_g, tr=tr, name="norm1")
    projp, gates = _mm(h, wp, "nn", tm=tt, tn=PW, tk=d, out_dtypes=(BF16, F32), out_widths=(PW, PW - C_SA),
                       epilogue=lambda acc: (acc, acc[:, C_SA:PW]), name="in_proj")
    qn, kn, v = _dnprep_fwd(projp, conv_w, bsz=bsz, t_seq=t_seq, tt=tt, name="dn_prep")
    u, w, qg, kd, pmat, tmat, gl = _dn_intra_fwd(qn, kn, v, gates, alog_row, dtb_row, nc_seq=nc_seq, name="dn_intra")
    ride_a, ride_b = late_gather if late_gather is not None else (None, None)
    o_dn, vn, hist, got_a = _dn_scan_fwd(u, w, qg, kd, pmat, gl, bsz=bsz, nc_seq=nc_seq, name="dn_scan", rider=ride_a)
    oi, gqg, gkd, ggl = _gla_intra_fwd(projp, gates, w2p, gla_b, nc_seq=nc_seq, name="gla_intra")
    o_gla, ghist, got_b = _gla_scan_fwd(oi, gqg, gkd, ggl, projp, bsz=bsz, nc_seq=nc_seq, name="gla_scan", rider=ride_b)
    if late_gather is not None:
        w_out = got_a[0].reshape(d, d)
        w_up = got_a[1].transpose(1, 0, 2).reshape(d, D_FF)
        w_down = got_b[0].reshape(D_FF, d)
    mix = _gnorm_fwd(o_dn, o_gla, projp, dn_norm_g, gla_norm_g, tr=tr, name="gated_norm")
    (x1,) = _mm(mix, w_out, "nn", tm=tr, tn=d, tk=d, out_dtypes=(F32,), extras=(h0,),
                epilogue=lambda acc, res: (res + acc,), name="out_proj")
    h2 = _rms_fwd(x1, norm2_g, tr=tr, name="norm2")

    (act,) = _mm(h2, w_up, "nn", tm=tt, tn=D_FF, tk=d, out_dtypes=(BF16,),
                 epilogue=lambda acc: (jnp.square(jnp.maximum(acc, 0.0)),), name="mlp_up")
    (x2,) = _mm(act, w_down, "nn", tm=tr, tn=d, tk=D_FF, out_dtypes=(F32,), extras=(x1,),
                epilogue=lambda acc, res: (res + acc,), name="mlp_down")
    dx2, dx2b, d_final_g, loss_tile = _final_loss(x2, final_norm_g, tgt_p, t_seq=t_seq, tr=tr, name="final_loss")

    (dup,) = _mm(dx2b, w_down, "nt", tm=tt, tn=D_FF, tk=d, out_dtypes=(BF16,), extras=(act,),
                 epilogue=lambda acc, a: (acc * (2.0 * jnp.sqrt(a.astype(F32))),), name="mlp_down_bwd")
    (d_w_down,) = _mm(act, dx2b, "tn", tm=D_FF // 2, tn=d, tk=tr, out_dtypes=(F32,), name="w_down_grad")
    (d_w_up,) = _mm(h2, dup, "tn", tm=d, tn=D_FF // 2, tk=tr, out_dtypes=(F32,), name="w_up_grad")
    mlp_sm = [d_w_up.reshape(d, N_CHIPS, D_FF // N_CHIPS).transpose(1, 0, 2), d_w_down.reshape(N_CHIPS, D_FF // N_CHIPS, d)]
    ride1 = _SiblingHalvesRider(mlp_sm) if where is not None else None
    res = _mm(dup, w_up, "nt", tm=tr, tn=d, tk=D_FF, out_dtypes=(F32,), name="mlp_up_bwd", rider=ride1)
    ((dh2,), theirs) = res if where is not None else (res, None)
    ride2 = None
    if where is not None:
        mlp_pair = [_pair_sum(where, a, b, name=f"pair_sum_mlp{k}") for k, (a, b) in enumerate(zip(mlp_sm, theirs))]
        ride2 = _ChipExchangeRider(mlp_pair)
    dx1, dx1b, d_norm2_g = _rms_bwd_add(x1, norm2_g, dh2, dx2, tr=tr, name="norm2_bwd")

    (dmix,) = _mm(dx1b, w_out, "nt", tm=tr, tn=d, tk=d, out_dtypes=(BF16,), name="out_proj_bwd")
    (d_w_out,) = _mm(mix, dx1b, "tn", tm=d, tn=d, tk=tr, out_dtypes=(F32,), name="w_out_grad")
    do_dn, ddz, do_gla, dgr, d_dn_norm_g, d_gla_norm_g = _gnorm_bwd(
        dmix, o_dn, o_gla, projp, dn_norm_g, gla_norm_g, tr=tr, name="gated_norm_bwd")
    du, dw, dqg, dkd, dgl = _dn_scan_bwd(do_dn, w, qg, kd, vn, pmat, gl, hist, bsz=bsz, nc_seq=nc_seq,
                                          name="dn_scan_bwd")
    dqn, dkn, dv, dsa, d_alog, d_dtb, mlp_parts = _dn_intra_bwd(
        qn, kn, v, gates, alog_row, dtb_row, u, w, tmat, du, dw, dqg, dkd, do_dn, vn, dgl, nc_seq=nc_seq,
        name="dn_intra_bwd", rider=ride2)
    dz, d_conv_w = _dnprep_bwd_a(projp, conv_w, dqn, dkn, dv, bsz=bsz, t_seq=t_seq, tt=tt, name="dn_prep_bwd")
    dcin = _dnprep_bwd_b(dz, conv_w, bsz=bsz, t_seq=t_seq, tt=tt, name="conv_bwd")
    gdqg, gdkd, gdvi, gdgl = _gla_scan_bwd(do_gla, gqg, gkd, ggl, projp, ghist, bsz=bsz, nc_seq=nc_seq,
                                            name="gla_scan_bwd")
    dgqk, dgv, dsb, d_w2p, d_gla_b = _gla_intra_bwd(projp, gates, w2p, gla_b, do_gla, gdqg, gdkd, gdvi, gdgl,
                                                    nc_seq=nc_seq, name="gla_intra_bwd")

    secs = (dcin, ddz, dgqk, dgv, dgr, dsa, dsb)
    g_lo = _grad_tn(h, secs[0:2], tk=tr, name="w_in_grad_lo")
    g_hi = _grad_tn(h, secs[2:7], tk=tr, name="w_in_grad_hi")
    dh0, d_norm1_g = _inproj_bwd(secs, wp, h0, norm1_g, dx1, tr=tt, name="in_proj_bwd")
    dh0 = dh0.reshape(bsz, t_seq, d)
    grad_x = dh0[:, CHUNK:]
    d_meta_rows = dh0[:, N_PAD:CHUNK].reshape(bsz * N_META, d)

    grads = dict(w_in_lo=g_lo, w_in_hi=g_hi, w_out=d_w_out, w_up=d_w_up, w_down=d_w_down, meta_rows=d_meta_rows,
                 norm1_g=d_norm1_g, conv_w=d_conv_w, a_log_tile=d_alog, dt_bias_tile=d_dtb, dn_norm_g=d_dn_norm_g,
                 gla_w2=d_w2p[0:GLA_RANK], gla_b=d_gla_b, gla_norm_g=d_gla_norm_g, norm2_g=d_norm2_g,
                 final_norm_g=d_final_g, loss_tile=loss_tile)
    if where is not None:
        grads["mlp_exchanged"] = (mlp_pair, mlp_parts)
    return grad_x, grads


SHARD_W = IN_WIDTH // N_CHIPS
PADDED_ORDER = ((0, 2048), (2056, 3592), (2048, 2056), LANE - 8, (3592, 3608), LANE - GLA_RANK)


def _pad_layout(w_full):
    pieces = [jnp.zeros((w_full.shape[0], seg), w_full.dtype) if isinstance(seg, int) else w_full[:, seg[0]:seg[1]]
              for seg in PADDED_ORDER]
    return jnp.concatenate(pieces, axis=1)


def _padded_from_shards(stack):
    pieces = []
    for seg in PADDED_ORDER:
        if isinstance(seg, int):
            pieces.append(jnp.zeros((stack.shape[1], seg), stack.dtype))
            continue
        for j in range(N_CHIPS):
            lo, hi = max(seg[0], j * SHARD_W), min(seg[1], (j + 1) * SHARD_W)
            if lo < hi:
                pieces.append(stack[j, :, lo - j * SHARD_W:hi - j * SHARD_W])
    return jnp.concatenate(pieces, axis=1)


def _shards_from_padded(g_lo, g_hi):
    split = g_lo.shape[1]
    starts, pos = [], 0
    for seg in PADDED_ORDER:
        width = seg if isinstance(seg, int) else seg[1] - seg[0]
        if not isinstance(seg, int):
            starts.append((seg[0], seg[1], pos))
        pos += width
    shards = []
    for j in range(N_CHIPS):
        pieces = []
        for a, b, p0 in sorted(starts):
            lo, hi = max(a, j * SHARD_W), min(b, (j + 1) * SHARD_W)
            if lo < hi:
                src, off = (g_lo, 0) if p0 < split else (g_hi, split)
                pieces.append(src[:, p0 + lo - a - off:p0 + hi - a - off])
        pieces.append(jnp.zeros((g_lo.shape[0], D_MODEL - SHARD_W), g_lo.dtype))
        shards.append(jnp.concatenate(pieces, axis=1))
    return jnp.stack(shards)


def _pack_small(g, bsz):
    assert bsz * N_META == 32
    row = jnp.concatenate([g["a_log_tile"], g["dt_bias_tile"], g["dn_norm_g"], g["gla_norm_g"], g["gla_b"],
                           g["loss_tile"], jnp.zeros((1, LANE), F32)], axis=1)
    return jnp.concatenate([g["meta_rows"], g["norm1_g"], g["conv_w"].reshape(6, D_MODEL), row,
                            g["gla_w2"].reshape(4, D_MODEL), g["norm2_g"], g["final_norm_g"],
                            jnp.zeros((2, D_MODEL), F32)], axis=0)


def kernel(x, meta_tokens, norm1_g, w_in, conv_w, a_log, dt_bias, dn_norm_g, gla_w2, gla_b, gla_norm_g, w_out, norm2_g, w_up, w_down, final_norm_g, loss_target, m_meta_tokens, m_norm1_g, m_w_in, m_conv_w, m_a_log, m_dt_bias, m_dn_norm_g, m_gla_w2, m_gla_b, m_gla_norm_g, m_w_out, m_norm2_g, m_w_up, m_w_down, m_final_norm_g, v_meta_tokens, v_norm1_g, v_w_in, v_conv_w, v_a_log, v_dt_bias, v_dn_norm_g, v_gla_w2, v_gla_b, v_gla_norm_g, v_w_out, v_norm2_g, v_w_up, v_w_down, v_final_norm_g):
    bsz = x.shape[0]
    chip = 2 * lax.axis_index("x") + lax.axis_index("y")

    lane_pad = lambda a, wd: jnp.pad(a, ((0, 0), (0, wd - a.shape[1])))
    where = jnp.stack([lax.axis_index("c"), chip]).astype(jnp.int32)
    slot = lambda a, dt, nm: _to_slot(where, a, dt, name="slot_" + nm)
    early = _GatherRider([slot(lane_pad(w_in[0], D_MODEL), BF16, "w_in"), slot(meta_tokens, F32, "meta"),
                          slot(conv_w[0], F32, "conv"), slot(lane_pad(gla_w2[0], LANE), F32, "gla_w2")],
                         [True, False, False, False])
    g_in, g_meta, g_conv, g_w2 = _exchange_now(early, name="gather_early")
    late = (_GatherRider([slot(w_out[0], BF16, "w_out"), slot(w_up[0], BF16, "w_up")], [True, True]),
            _GatherRider([slot(w_down[0], BF16, "w_down")], [True]))
    wp = _padded_from_shards(g_in)
    meta_f = g_meta.transpose(1, 0, 2).reshape(N_META, D_MODEL)
    conv_f = g_conv.transpose(1, 0, 2).reshape(4, QKV_W)
    w2_f = g_w2[:, :, 0:GQ_W // N_CHIPS].transpose(1, 0, 2).reshape(GLA_RANK, GQ_W)

    grad_x, g = _local_step(x, loss_target, meta_f, norm1_g, wp, conv_f, a_log, dt_bias, dn_norm_g, w2_f, gla_b,
                            gla_norm_g, None, norm2_g, None, None, final_norm_g.reshape(1, D_MODEL), late_gather=late, where=where)

    shard_major = [_shards_from_padded(g["w_in_lo"], g["w_in_hi"]), g["w_out"].reshape(N_CHIPS, D_MODEL // N_CHIPS, D_MODEL)]
    theirs = _exchange_now(_SiblingHalvesRider(shard_major), name="sibling_halves")
    pair = [_pair_sum(where, a, b, name=f"pair_sum_{k}") for k, (a, b) in enumerate(zip(shard_major, theirs))]
    parts = _exchange_now(_ChipExchangeRider(pair), name="chip_exchange")
    mlp_pair, mlp_parts = g["mlp_exchanged"]
    halves = [_chip_sum(where, p, q, name=f"chip_sum_{k}")
              for k, (p, q) in enumerate(zip(pair + mlp_pair, list(parts) + list(mlp_parts)))]
    gw_in, gw_out, gw_up, gw_down = _sibling_join(halves)

    red = _small_allreduce(_pack_small(g, bsz))
    g_meta_full = red[0:N_META]
    g_norm1 = red[32:33]
    g_conv_full = red[33:39].reshape(4, QKV_W)
    srow = red[39:40]
    g_alog, g_dtb = srow[:, 4:8], srow[:, LANE + 4:LANE + 8]
    g_dn_norm, g_gla_norm = srow[:, 2 * LANE:3 * LANE], srow[:, 3 * LANE:4 * LANE]
    g_gla_b = srow[:, 4 * LANE:6 * LANE]
    loss = srow[0, 6 * LANE]
    g_w2_full = red[40:44].reshape(GLA_RANK, GQ_W)
    g_norm2 = red[44:45]
    g_final = red[45:46]
    g_meta_sh = lax.dynamic_slice_in_dim(g_meta_full, chip * (D_MODEL // N_CHIPS), D_MODEL // N_CHIPS, axis=1)
    g_conv_sh = lax.dynamic_slice_in_dim(g_conv_full, chip * (QKV_W // N_CHIPS), QKV_W // N_CHIPS, axis=1)
    g_w2_sh = lax.dynamic_slice_in_dim(g_w2_full, chip * (GQ_W // N_CHIPS), GQ_W // N_CHIPS, axis=1)

    names = ["meta_tokens", "norm1_g", "w_in", "conv_w", "a_log", "dt_bias", "dn_norm_g", "gla_w2", "gla_b",
             "gla_norm_g", "w_out", "norm2_g", "w_up", "w_down", "final_norm_g"]
    weights = dict(meta_tokens=meta_tokens, norm1_g=norm1_g, w_in=w_in, conv_w=conv_w, a_log=a_log, dt_bias=dt_bias,
                   dn_norm_g=dn_norm_g, gla_w2=gla_w2, gla_b=gla_b, gla_norm_g=gla_norm_g, w_out=w_out,
                   norm2_g=norm2_g, w_up=w_up, w_down=w_down, final_norm_g=final_norm_g)
    ms = dict(meta_tokens=m_meta_tokens, norm1_g=m_norm1_g, w_in=m_w_in, conv_w=m_conv_w, a_log=m_a_log,
              dt_bias=m_dt_bias, dn_norm_g=m_dn_norm_g, gla_w2=m_gla_w2, gla_b=m_gla_b, gla_norm_g=m_gla_norm_g,
              w_out=m_w_out, norm2_g=m_norm2_g, w_up=m_w_up, w_down=m_w_down, final_norm_g=m_final_norm_g)
    vs = dict(meta_tokens=v_meta_tokens, norm1_g=v_norm1_g, w_in=v_w_in, conv_w=v_conv_w, a_log=v_a_log,
              dt_bias=v_dt_bias, dn_norm_g=v_dn_norm_g, gla_w2=v_gla_w2, gla_b=v_gla_b, gla_norm_g=v_gla_norm_g,
              w_out=v_w_out, norm2_g=v_norm2_g, w_up=v_w_up, w_down=v_w_down, final_norm_g=v_final_norm_g)
    grads2d = dict(meta_tokens=g_meta_sh, norm1_g=g_norm1, w_in=gw_in, conv_w=g_conv_sh, a_log=g_alog, dt_bias=g_dtb,
                   dn_norm_g=g_dn_norm, gla_w2=g_w2_sh, gla_b=g_gla_b, gla_norm_g=g_gla_norm, w_out=gw_out,
                   norm2_g=g_norm2, w_up=gw_up, w_down=gw_down, final_norm_g=g_final)
    out_g, out_d, out_m, out_v = [], [], [], []
    for nm in names:
        shape = weights[nm].shape
        g2 = grads2d[nm]
        if len(shape) == 3:
            res = _adamw(weights[nm], g2, ms[nm], vs[nm], name=f"adamw_{nm}", emit_grad=nm == "w_in")
            gout = res[3] if nm == "w_in" else g2.reshape(shape)
        else:
            as2d = lambda a: a.reshape(g2.shape)
            res = _adamw(as2d(weights[nm]), g2, as2d(ms[nm]), as2d(vs[nm]), name=f"adamw_{nm}")
            gout = g2.reshape(shape)
        out_g.append(gout)
        out_d.append(res[0].reshape(shape))
        out_m.append(res[1].reshape(shape))
        out_v.append(res[2].reshape(shape))
    return (loss, grad_x, *out_g, *out_d, *out_m, *out_v)
```

```python
import functools

import jax
import jax.numpy as jnp
import numpy as np
from jax import lax
from jax.experimental import pallas as pl
from jax.experimental.pallas import tpu as pltpu

F32 = jnp.float32
BF16 = jnp.bfloat16
HI = lax.Precision.HIGHEST
MESH = pl.DeviceIdType.MESH

D_MODEL = 1024
N_META = 16
CHUNK = 64
N_PAD = CHUNK - N_META
NH = 4
DN_D = 128
GLA_DK = 64
GLA_DV = 128
GLA_RANK = 16
D_FF = 4 * D_MODEL
EPS = 1e-6
IN_WIDTH = 3608
C_QKV, C_DZ, C_GQK, C_GV, C_GR, C_SA, C_SB, PW = 0, 1536, 2048, 2560, 3072, 3584, 3712, 3840
LANE = 128
N_CHIPS = 4

ADAM_LR, ADAM_B1, ADAM_B2, ADAM_EPS, ADAM_WD, ADAM_STEP = 0.001, 0.9, 0.999, 1e-08, 0.01, 10

VMEM_BIG = 56 * 1024 * 1024


def _cp(vmem=None, sem=None):
    kw = {}
    if vmem is not None:
        kw["vmem_limit_bytes"] = vmem
    if sem is not None:
        kw["dimension_semantics"] = sem
    return pltpu.CompilerParams(**kw)


def _tile(n, target, mult=16):
    best = None
    for t in range(mult, min(n, target) + 1, mult):
        if n % t == 0:
            best = t
    assert best is not None, (n, target)
    return best


def _dot(a, b, dims, prec=None):
    return lax.dot_general(a, b, (dims, ((), ())), preferred_element_type=F32, precision=prec)


def _nn(a, b):
    return _dot(a.astype(BF16), b.astype(BF16), ((1,), (0,)))


def _nt(a, b):
    return _dot(a.astype(BF16), b.astype(BF16), ((1,), (1,)))


def _tn(a, b):
    return _dot(a.astype(BF16), b.astype(BF16), ((0,), (0,)))


def _tri_sum(tri, x):
    t = tri.astype(BF16)
    hi = x.astype(BF16)
    r1 = x - hi.astype(F32)
    mid = r1.astype(BF16)
    lo = (r1 - mid.astype(F32)).astype(BF16)
    nn = ((1,), (0,))
    return _dot(t, hi, nn) + _dot(t, mid, nn) + _dot(t, lo, nn)


def _sigmoid(x):
    return 0.5 * jnp.tanh(0.5 * x) + 0.5


def _softplus(x):
    return jnp.maximum(x, 0.0) + jnp.log(1.0 + jnp.exp(-jnp.abs(x)))


def _logsigmoid(x):
    return -_softplus(-x)


def _iota2(shape, dim):
    return lax.broadcasted_iota(jnp.int32, shape, dim)


def _mm(a, b, mode, *, tm, tn, tk, out_dtypes, extras=(), epilogue=None, name, vmem=VMEM_BIG, rider=None,
        out_widths=None):
    if mode == "tn":
        K, M = a.shape
    else:
        M, K = a.shape
    N = b.shape[0] if mode == "nt" else b.shape[1]
    assert M % tm == 0 and N % tn == 0 and K % tk == 0, (name, M, N, K, tm, tn, tk)
    nk = K // tk
    n_ex, n_out = len(extras), len(out_dtypes)
    if mode == "tn":
        a_spec = pl.BlockSpec((tk, tm), lambda i, j, k: (k, i))
    else:
        a_spec = pl.BlockSpec((tm, tk), lambda i, j, k: (i, k))
    if mode == "nt":
        b_spec = pl.BlockSpec((tn, tk), lambda i, j, k: (j, k))
    else:
        b_spec = pl.BlockSpec((tk, tn), lambda i, j, k: (k, j))
    mn_spec = pl.BlockSpec((tm, tn), lambda i, j, k: (i, j))
    if out_widths is None:
        o_specs = [mn_spec] * n_out
        o_shapes = [jax.ShapeDtypeStruct((M, N), dt) for dt in out_dtypes]
    else:
        assert tn == N
        o_specs = [pl.BlockSpec((tm, wd), lambda i, j, k: (i, 0)) for wd in out_widths]
        o_shapes = [jax.ShapeDtypeStruct((M, wd), dt) for wd, dt in zip(out_widths, out_dtypes)]
    dims = {"nn": ((1,), (0,)), "nt": ((1,), (1,)), "tn": ((0,), (0,))}[mode]

    single = nk == 1
    direct = (not single) and epilogue is None and n_out == 1 and out_dtypes[0] == F32

    def body(*refs):
        a_ref, b_ref = refs[0], refs[1]
        ex_refs = refs[2:2 + n_ex]
        out_refs = refs[2 + n_ex:2 + n_ex + n_out]
        part = _dot(a_ref[...].astype(BF16), b_ref[...].astype(BF16), dims)

        def finish(acc):
            res = (acc,) if epilogue is None else epilogue(acc, *[e[...] for e in ex_refs])
            for o_ref, r in zip(out_refs, res):
                o_ref[...] = r.astype(o_ref.dtype)

        if single:
            finish(part)
            return
        acc_ref = out_refs[0] if direct else refs[2 + n_ex + n_out]
        k = pl.program_id(2)

        @pl.when(k == 0)
        def _():
            acc_ref[...] = part

        @pl.when(k > 0)
        def _():
            acc_ref[...] += part

        if not direct:
            @pl.when(k == nk - 1)
            def _():
                finish(acc_ref[...])

    outs, ridden = _hosted_call(
        body, rider, name=name, grid=(M // tm, N // tn, nk),
        in_specs=[a_spec, b_spec] + [mn_spec] * n_ex,
        out_specs=o_specs, out_shape=o_shapes,
        scratch_shapes=[] if (single or direct) else [pltpu.VMEM((tm, tn), F32)],
        compiler_params=_cp(vmem, ("parallel", "parallel", "arbitrary")), args=(a, b, *extras))
    return tuple(outs) if rider is None else (tuple(outs), ridden)


def _grad_tn(a, secs, *, tk, name):
    kk, m = a.shape
    widths = [s.shape[1] for s in secs]
    total = sum(widths)
    nk = kk // tk

    def body(*refs):
        a_ref, sec_refs, o_ref = refs[0], refs[1:-1], refs[-1]
        cat = sec_refs[0][...] if len(sec_refs) == 1 else jnp.concatenate([s[...] for s in sec_refs], axis=1)
        part = _dot(a_ref[...].astype(BF16), cat.astype(BF16), ((0,), (0,)))
        k = pl.program_id(0)

        @pl.when(k == 0)
        def _():
            o_ref[...] = part

        @pl.when(k > 0)
        def _():
            o_ref[...] += part

    return pl.pallas_call(
        body, name=name, grid=(nk,),
        in_specs=[pl.BlockSpec((tk, m), lambda k: (k, 0))] + [pl.BlockSpec((tk, w), lambda k: (k, 0)) for w in widths],
        out_specs=pl.BlockSpec((m, total), lambda k: (0, 0)),
        out_shape=jax.ShapeDtypeStruct((m, total), F32),
        compiler_params=_cp(VMEM_BIG, ("arbitrary",)),
    )(a, *secs)


def _rms_fwd(x, g, *, tr, name):
    n, d = x.shape

    def body(x_ref, g_ref, o_ref):
        xv = x_ref[...]
        r = lax.rsqrt(jnp.mean(xv * xv, axis=-1, keepdims=True) + EPS)
        o_ref[...] = (xv * r * g_ref[...]).astype(o_ref.dtype)

    return pl.pallas_call(
        body, name=name, grid=(n // tr,),
        in_specs=[pl.BlockSpec((tr, d), lambda i: (i, 0)), pl.BlockSpec((1, d), lambda i: (0, 0))],
        out_specs=pl.BlockSpec((tr, d), lambda i: (i, 0)),
        out_shape=jax.ShapeDtypeStruct((n, d), BF16),
        compiler_params=_cp(VMEM_BIG),
    )(x, g)


def _rms_bwd_math(xv, g, dy):
    r = lax.rsqrt(jnp.mean(xv * xv, axis=-1, keepdims=True) + EPS)
    xh = xv * r
    gdy = dy * g
    dx = r * (gdy - xh * jnp.mean(xh * gdy, axis=-1, keepdims=True))
    return dx, jnp.sum(dy * xh, axis=0, keepdims=True)


def _mlp_up_bwd_norm(dup, w_up, x, g, res, *, tr, name, rider=None):
    n, d = x.shape
    ff = dup.shape[1]

    def body(dup_ref, w_ref, x_ref, g_ref, res_ref, o_ref, ob_ref, dg_ref):
        dh = _nt(dup_ref[...], w_ref[...])
        dx, dg = _rms_bwd_math(x_ref[...], g_ref[...], dh)
        tot = res_ref[...] + dx
        o_ref[...] = tot
        ob_ref[...] = tot.astype(BF16)

        @pl.when(pl.program_id(0) == 0)
        def _():
            dg_ref[...] = dg

        @pl.when(pl.program_id(0) > 0)
        def _():
            dg_ref[...] += dg

    row = pl.BlockSpec((tr, d), lambda i: (i, 0))
    vec = pl.BlockSpec((1, d), lambda i: (0, 0))
    outs, ridden = _hosted_call(
        body, rider, name=name, grid=(n // tr,),
        in_specs=[pl.BlockSpec((tr, ff), lambda i: (i, 0)), pl.BlockSpec((d, ff), lambda i: (0, 0)), row, vec, row],
        out_specs=[row, row, vec],
        out_shape=[jax.ShapeDtypeStruct((n, d), F32), jax.ShapeDtypeStruct((n, d), BF16),
                   jax.ShapeDtypeStruct((1, d), F32)],
        scratch_shapes=[], compiler_params=_cp(VMEM_BIG, ("arbitrary",)), args=(dup, w_up, x, g, res))
    return (*outs, ridden)


def _mlp_down_loss(act, w_down, x1, gf, tgt, *, t_seq, tr, name):
    n, d = x1.shape
    ff = act.shape[1]
    per_seq = t_seq // tr

    def body(a_ref, w_ref, x_ref, g_ref, t_ref, dx_ref, dxb_ref, dg_ref, loss_ref):
        i = pl.program_id(0)
        xv = x_ref[...] + _nn(a_ref[...], w_ref[...])
        g = g_ref[...]
        r = lax.rsqrt(jnp.mean(xv * xv, axis=-1, keepdims=True) + EPS)
        xh = xv * r
        pos = (i % per_seq) * tr + _iota2((tr, 1), 0)
        real = pos >= CHUNK
        err = jnp.where(real, xh * g - t_ref[...], 0.0)
        dy = err * (1.0 / d)
        gdy = dy * g
        dx = r * (gdy - xh * jnp.mean(xh * gdy, axis=-1, keepdims=True))
        dx_ref[...] = dx
        dxb_ref[...] = dx.astype(BF16)
        dg = jnp.sum(dy * xh, axis=0, keepdims=True)
        ls = 0.5 * jnp.sum(jnp.mean(err * err, axis=-1, keepdims=True), axis=0, keepdims=True)
        ls = jnp.where(_iota2((1, LANE), 1) == 0, ls, 0.0)

        @pl.when(i == 0)
        def _():
            dg_ref[...] = dg
            loss_ref[...] = ls

        @pl.when(i > 0)
        def _():
            dg_ref[...] += dg
            loss_ref[...] += ls

    row = pl.BlockSpec((tr, d), lambda i: (i, 0))
    vec = pl.BlockSpec((1, d), lambda i: (0, 0))
    one = pl.BlockSpec((1, LANE), lambda i: (0, 0))
    return pl.pallas_call(
        body, name=name, grid=(n // tr,),
        in_specs=[pl.BlockSpec((tr, ff), lambda i: (i, 0)), pl.BlockSpec((ff, d), lambda i: (0, 0)), row, vec, row],
        out_specs=[row, row, vec, one],
        out_shape=[jax.ShapeDtypeStruct((n, d), F32), jax.ShapeDtypeStruct((n, d), BF16),
                   jax.ShapeDtypeStruct((1, d), F32), jax.ShapeDtypeStruct((1, LANE), F32)],
        compiler_params=_cp(VMEM_BIG, ("arbitrary",)),
    )(act, w_down, x1, gf, tgt)


def _gnorm_fwd(o_dn, o_gla, projp, g_dn, g_gla, *, tr, name):
    n = o_dn.shape[0]
    w = NH * DN_D

    def body(odn_ref, ogl_ref, z_ref, r_ref, gdn_ref, ggl_ref, mix_ref):
        for grp, (o_ref, gate_ref, gain_ref) in enumerate(((odn_ref, z_ref, gdn_ref), (ogl_ref, r_ref, ggl_ref))):
            gain = gain_ref[...]
            for h in range(NH):
                sl = slice(h * DN_D, (h + 1) * DN_D)
                o = o_ref[:, sl].astype(F32)
                z = gate_ref[:, sl].astype(F32)
                r = lax.rsqrt(jnp.mean(o * o, axis=-1, keepdims=True) + EPS)
                y = (o * r * gain) * (z * _sigmoid(z))
                mix_ref[:, grp * w + h * DN_D: grp * w + (h + 1) * DN_D] = y.astype(mix_ref.dtype)

    row = pl.BlockSpec((tr, w), lambda i: (i, 0))
    vec = pl.BlockSpec((1, DN_D), lambda i: (0, 0))
    return pl.pallas_call(
        body, name=name, grid=(n // tr,),
        in_specs=[row, row, pl.BlockSpec((tr, w), lambda i: (i, C_DZ // w)),
                  pl.BlockSpec((tr, w), lambda i: (i, C_GR // w)), vec, vec],
        out_specs=pl.BlockSpec((tr, 2 * w), lambda i: (i, 0)),
        out_shape=jax.ShapeDtypeStruct((n, 2 * w), BF16),
        compiler_params=_cp(VMEM_BIG),
    )(o_dn, o_gla, projp, projp, g_dn, g_gla)


def _gnorm_bwd(dmix, o_dn, o_gla, projp, g_dn, g_gla, *, tr, name):
    n = o_dn.shape[0]
    w = NH * DN_D

    def body(dm_ref, odn_ref, ogl_ref, z_ref, r_ref, gdn_ref, ggl_ref,
             dodn_ref, ddz_ref, dogl_ref, dgr_ref, dgdn_ref, dggl_ref):
        first = pl.program_id(0) == 0
        groups = ((odn_ref, z_ref, gdn_ref, dodn_ref, ddz_ref, dgdn_ref),
                  (ogl_ref, r_ref, ggl_ref, dogl_ref, dgr_ref, dggl_ref))
        for grp, (o_ref, gate_ref, gain_ref, do_ref, dgate_ref, dgain_ref) in enumerate(groups):
            gain = gain_ref[...]
            dgain = jnp.zeros((1, DN_D), F32)
            for h in range(NH):
                sl = slice(h * DN_D, (h + 1) * DN_D)
                o = o_ref[:, sl].astype(F32)
                z = gate_ref[:, sl].astype(F32)
                dm = dm_ref[:, grp * w + h * DN_D: grp * w + (h + 1) * DN_D].astype(F32)
                r = lax.rsqrt(jnp.mean(o * o, axis=-1, keepdims=True) + EPS)
                oh = o * r
                s = _sigmoid(z)
                dn = dm * (z * s)
                dgate_ref[:, sl] = (dm * (oh * gain) * (s * (1.0 + z * (1.0 - s)))).astype(dgate_ref.dtype)
                gdn = dn * gain
                do_ref[:, sl] = (r * (gdn - oh * jnp.mean(oh * gdn, axis=-1, keepdims=True))).astype(do_ref.dtype)
                dgain = dgain + jnp.sum(dn * oh, axis=0, keepdims=True)

            @pl.when(first)
            def _():
                dgain_ref[...] = dgain

            @pl.when(jnp.logical_not(first))
            def _():
                dgain_ref[...] += dgain

    row = pl.BlockSpec((tr, w), lambda i: (i, 0))
    vec = pl.BlockSpec((1, DN_D), lambda i: (0, 0))
    big = jax.ShapeDtypeStruct((n, w), F32)
    gate = jax.ShapeDtypeStruct((n, w), BF16)
    small = jax.ShapeDtypeStruct((1, DN_D), F32)
    return pl.pallas_call(
        body, name=name, grid=(n // tr,),
        in_specs=[pl.BlockSpec((tr, 2 * w), lambda i: (i, 0)), row, row,
                  pl.BlockSpec((tr, w), lambda i: (i, C_DZ // w)), pl.BlockSpec((tr, w), lambda i: (i, C_GR // w)), vec, vec],
        out_specs=[row, row, row, row, vec, vec],
        out_shape=[gate, gate, gate, gate, small, small],
        compiler_params=_cp(VMEM_BIG),
    )(dmix, o_dn, o_gla, projp, projp, g_dn, g_gla)


QKV_W = 3 * NH * DN_D
HALO = 8


def _conv_z(xs_ref, cw_ref, tt):
    z = cw_ref[0:1, :] * xs_ref[pl.ds(HALO - 3, tt), :]
    for j in range(1, 4):
        z = z + cw_ref[j:j + 1, :] * xs_ref[pl.ds(HALO - 3 + j, tt), :]
    return z


def _dnprep_fwd(projp, conv_w, *, bsz, t_seq, tt, name):
    n = bsz * t_seq
    per_seq = t_seq // tt
    hw = NH * DN_D

    def body(x_ref, halo_ref, cw_ref, q_ref, k_ref, v_ref, xs_ref):
        i = pl.program_id(1)
        xs_ref[0:HALO, :] = jnp.where(i == 0, 0.0, halo_ref[...].astype(F32))
        xs_ref[HALO:HALO + tt, :] = x_ref[...].astype(F32)
        z = _conv_z(xs_ref, cw_ref, tt)
        a = z * _sigmoid(z)
        for grp, o_ref in enumerate((q_ref, k_ref)):
            for h in range(NH):
                ah = a[:, grp * hw + h * DN_D: grp * hw + (h + 1) * DN_D]
                rs = lax.rsqrt(jnp.sum(ah * ah, axis=-1, keepdims=True) + EPS)
                o_ref[:, h * DN_D:(h + 1) * DN_D] = (ah * rs).astype(o_ref.dtype)
        v_ref[...] = a[:, 2 * hw:3 * hw].astype(v_ref.dtype)

    def halo_map(b, i):
        return (jnp.maximum((b * t_seq + i * tt) // HALO - 1, 0), 0)

    out = pl.BlockSpec((tt, hw), lambda b, i: (b * per_seq + i, 0))
    sds = jax.ShapeDtypeStruct((n, hw), BF16)
    return pl.pallas_call(
        body, name=name, grid=(bsz, per_seq),
        in_specs=[pl.BlockSpec((tt, QKV_W), lambda b, i: (b * per_seq + i, 0)),
                  pl.BlockSpec((HALO, QKV_W), halo_map),
                  pl.BlockSpec((4, QKV_W), lambda b, i: (0, 0))],
        out_specs=[out, out, out], out_shape=[sds, sds, sds],
        scratch_shapes=[pltpu.VMEM((tt + HALO, QKV_W), F32)],
        compiler_params=_cp(VMEM_BIG),
    )(projp, projp, conv_w)


def _dnprep_bwd_a(projp, conv_w, dq, dk, dv, *, bsz, t_seq, tt, name):
    n = bsz * t_seq
    per_seq = t_seq // tt
    hw = NH * DN_D

    def body(x_ref, halo_ref, cw_ref, dq_ref, dk_ref, dv_ref, dz_ref, dcw_ref, xs_ref):
        b, i = pl.program_id(0), pl.program_id(1)
        xs_ref[0:HALO, :] = jnp.where(i == 0, 0.0, halo_ref[...].astype(F32))
        xs_ref[HALO:HALO + tt, :] = x_ref[...].astype(F32)
        z = _conv_z(xs_ref, cw_ref, tt)
        s = _sigmoid(z)
        a = z * s
        dsilu = s * (1.0 + z * (1.0 - s))
        for grp, d_ref in enumerate((dq_ref, dk_ref)):
            for h in range(NH):
                sl = slice(grp * hw + h * DN_D, grp * hw + (h + 1) * DN_D)
                ah = a[:, sl]
                rs = lax.rsqrt(jnp.sum(ah * ah, axis=-1, keepdims=True) + EPS)
                y = ah * rs
                dy = d_ref[:, h * DN_D:(h + 1) * DN_D]
                da = rs * (dy - y * jnp.sum(dy * y, axis=-1, keepdims=True))
                dz_ref[:, sl] = da * dsilu[:, sl]
        dz_ref[:, 2 * hw:3 * hw] = dv_ref[...] * dsilu[:, 2 * hw:3 * hw]
        dz = dz_ref[...]
        first = jnp.logical_and(b == 0, i == 0)
        for j in range(4):
            part = jnp.sum(dz * xs_ref[pl.ds(HALO - 3 + j, tt), :], axis=0, keepdims=True)

            @pl.when(first)
            def _():
                dcw_ref[j:j + 1, :] = part

            @pl.when(jnp.logical_not(first))
            def _():
                dcw_ref[j:j + 1, :] += part

    def halo_map(b, i):
        return (jnp.maximum((b * t_seq + i * tt) // HALO - 1, 0), 0)

    hrow = pl.BlockSpec((tt, hw), lambda b, i: (b * per_seq + i, 0))
    return pl.pallas_call(
        body, name=name, grid=(bsz, per_seq),
        in_specs=[pl.BlockSpec((tt, QKV_W), lambda b, i: (b * per_seq + i, 0)),
                  pl.BlockSpec((HALO, QKV_W), halo_map),
                  pl.BlockSpec((4, QKV_W), lambda b, i: (0, 0)), hrow, hrow, hrow],
        out_specs=[pl.BlockSpec((tt, QKV_W), lambda b, i: (b * per_seq + i, 0)),
                   pl.BlockSpec((4, QKV_W), lambda b, i: (0, 0))],
        out_shape=[jax.ShapeDtypeStruct((n, QKV_W), F32), jax.ShapeDtypeStruct((4, QKV_W), F32)],
        scratch_shapes=[pltpu.VMEM((tt + HALO, QKV_W), F32)],
        compiler_params=_cp(VMEM_BIG),
    )(projp, projp, conv_w, dq, dk, dv)


def _dnprep_bwd_b(dz, conv_w, *, bsz, t_seq, tt, name):
    n = bsz * t_seq
    per_seq = t_seq // tt
    last_blk = n // HALO - 1

    def body(dz_ref, halo_ref, cw_ref, dx_ref, ds_ref):
        i = pl.program_id(1)
        ds_ref[0:tt, :] = dz_ref[...].astype(F32)
        ds_ref[tt:tt + HALO, :] = jnp.where(i == per_seq - 1, 0.0, halo_ref[...].astype(F32))
        dx = cw_ref[0:1, :] * ds_ref[pl.ds(3, tt), :]
        for j in range(1, 4):
            dx = dx + cw_ref[j:j + 1, :] * ds_ref[pl.ds(3 - j, tt), :]
        dx_ref[...] = dx.astype(dx_ref.dtype)

    def halo_map(b, i):
        return (jnp.minimum((b * t_seq + (i + 1) * tt) // HALO, last_blk), 0)

    row = pl.BlockSpec((tt, QKV_W), lambda b, i: (b * per_seq + i, 0))
    return pl.pallas_call(
        body, name=name, grid=(bsz, per_seq),
        in_specs=[row, pl.BlockSpec((HALO, QKV_W), halo_map), pl.BlockSpec((4, QKV_W), lambda b, i: (0, 0))],
        out_specs=row, out_shape=jax.ShapeDtypeStruct((n, QKV_W), BF16),
        scratch_shapes=[pltpu.VMEM((tt + HALO, QKV_W), F32)],
        compiler_params=_cp(VMEM_BIG),
    )(dz, dz, conv_w)


def _masks64():
    r = _iota2((CHUNK, CHUNK), 0)
    c = _iota2((CHUNK, CHUNK), 1)
    return r, c


def _group(nc_seq, target=5):
    return max(g for g in range(1, target + 1) if nc_seq % g == 0)


def _round_robin(chains):
    live = list(chains)
    while live:
        nxt = []
        for ch in live:
            try:
                next(ch)
                nxt.append(ch)
            except StopIteration:
                pass
        live = nxt
        yield


def _run(chains):
    for _ in _round_robin(chains):
        pass


def _per_chunk(inner, kinds, grp):
    def body(*refs):
        chains = []
        for gi in range(grp):
            views = []
            for r, kind in zip(refs, kinds):
                if kind == "row":
                    views.append(r.at[pl.ds(gi * CHUNK, CHUNK)])
                elif kind == "lead":
                    views.append(r.at[pl.ds(gi, 1)])
                else:
                    views.append(r)
            chains.append(inner(gi, *views))
        _run(chains)
    return body


def _accumulate(ref, val, gi):
    if gi > 0:
        ref[...] += val
        return
    first = pl.program_id(0) == 0

    @pl.when(first)
    def _():
        ref[...] = val

    @pl.when(jnp.logical_not(first))
    def _():
        ref[...] += val


ANY = pl.BlockSpec(memory_space=pl.ANY)


def _place():
    return lax.axis_index("x"), lax.axis_index("y"), lax.axis_index("c")


def _other_chips(x, y):
    return [(1 - x, y, 2 * (1 - x) + y), (x, 1 - y, 2 * x + 1 - y), (1 - x, 1 - y, 2 * (1 - x) + 1 - y)]


class _GatherRider:
    def __init__(self, bufs, split):
        self.inputs = list(bufs)
        self.split = list(split)
        self.out_shapes = [jax.ShapeDtypeStruct(b.shape, b.dtype) for b in bufs]
        self.aliases = {i: i for i in range(len(bufs))}
        self.sems = [pltpu.SemaphoreType.DMA((len(bufs), 3))] * 4

    def _rows(self, k, buf, c, mine=True):
        r = buf.shape[1]
        if not self.split[k]:
            return pl.ds(0, r)
        return pl.ds((c if mine else 1 - c) * (r // 2), r // 2)

    def _ici(self, k, d, bufs, sems, c, px, py, block):
        rows = self._rows(k, bufs[k], c)
        return pltpu.make_async_remote_copy(
            src_ref=bufs[k].at[block, rows, :], dst_ref=bufs[k].at[block, rows, :], send_sem=sems[0].at[k, d],
            recv_sem=sems[1].at[k, d], device_id=(px, py, c), device_id_type=MESH)

    def _pass(self, k, d, bufs, sems, x, y, c, block, mine):
        rows = self._rows(k, bufs[k], c, mine)
        return pltpu.make_async_remote_copy(
            src_ref=bufs[k].at[block, rows, :], dst_ref=bufs[k].at[block, rows, :], send_sem=sems[2].at[k, d],
            recv_sem=sems[3].at[k, d], device_id=(x, y, 1 - c), device_id_type=MESH)

    def first(self, in_refs, bufs, sems):
        x, y, c = _place()
        for k in range(len(bufs)):
            for d, (px, py, _) in enumerate(_other_chips(x, y)):
                self._ici(k, d, bufs, sems, c, px, py, 2 * x + y).start()

    def last(self, in_refs, bufs, sems):
        x, y, c = _place()
        chips = _other_chips(x, y)
        for k in range(len(bufs)):
            for d, (px, py, pj) in enumerate(chips):
                self._ici(k, d, bufs, sems, c, px, py, pj).wait_recv()
                if self.split[k]:
                    self._pass(k, d, bufs, sems, x, y, c, pj, True).start()
        for k in range(len(bufs)):
            for d, (px, py, pj) in enumerate(chips):
                if self.split[k]:
                    self._pass(k, d, bufs, sems, x, y, c, pj, False).wait_recv()
                    self._pass(k, d, bufs, sems, x, y, c, pj, True).wait_send()
                self._ici(k, d, bufs, sems, c, px, py, 2 * x + y).wait_send()


def _hosted_call(body, rider, *, name, grid, in_specs, out_specs, out_shape, scratch_shapes, compiler_params, args):
    if rider is None:
        outs = pl.pallas_call(body, name=name, grid=grid, in_specs=in_specs, out_specs=out_specs, out_shape=out_shape,
                              scratch_shapes=scratch_shapes, compiler_params=compiler_params)(*args)
        return list(outs), []
    n_in, n_out, n_scr = len(in_specs), len(out_specs), len(scratch_shapes)
    r_in, r_out = len(rider.inputs), len(rider.out_shapes)
    compiler_params = _cp(compiler_params.vmem_limit_bytes, ("arbitrary",) * len(grid))

    def full_body(*refs):
        ins = refs[:n_in]
        rins = refs[n_in:n_in + r_in]
        outs = refs[n_in + r_in:n_in + r_in + n_out]
        routs = refs[n_in + r_in + n_out:n_in + r_in + n_out + r_out]
        rest = refs[n_in + r_in + n_out + r_out:]
        scr, sems = rest[:n_scr], rest[n_scr:]
        ids = [pl.program_id(a) for a in range(len(grid))]
        is_first = functools.reduce(jnp.logical_and, [i == 0 for i in ids])
        is_last = functools.reduce(jnp.logical_and, [i == g - 1 for i, g in zip(ids, grid)])

        @pl.when(is_first)
        def _():
            rider.first(rins, routs, sems)

        body(*ins, *outs, *scr)

        @pl.when(is_last)
        def _():
            rider.last(rins, routs, sems)

    res = pl.pallas_call(
        full_body, name=name, grid=grid, in_specs=list(in_specs) + [ANY] * r_in,
        out_specs=list(out_specs) + [ANY] * r_out, out_shape=list(out_shape) + list(rider.out_shapes),
        input_output_aliases={n_in + i: n_out + o for i, o in rider.aliases.items()},
        scratch_shapes=list(scratch_shapes) + list(rider.sems), compiler_params=compiler_params,
    )(*args, *rider.inputs)
    return list(res[:n_out]), list(res[n_out:])


def _exchange_now(rider, *, name):
    r_in = len(rider.inputs)

    def body(*refs):
        rins = refs[:r_in]
        routs = refs[r_in:r_in + len(rider.out_shapes)]
        sems = refs[r_in + len(rider.out_shapes):]
        rider.first(rins, routs, sems)
        rider.last(rins, routs, sems)

    return pl.pallas_call(
        body, name=name, in_specs=[ANY] * r_in, out_specs=[ANY] * len(rider.out_shapes), out_shape=list(rider.out_shapes),
        input_output_aliases=dict(rider.aliases), scratch_shapes=list(rider.sems),
    )(*rider.inputs)


def _to_slot(where, a, dtype, *, name):
    r, cols = a.shape
    tr = _tile(r, 256, 16) if r > 256 else r

    def body(w_ref, a_ref, o_ref):
        o_ref[0] = a_ref[...].astype(o_ref.dtype)

    return pl.pallas_call(
        body, name=name,
        grid_spec=pltpu.PrefetchScalarGridSpec(
            num_scalar_prefetch=1, grid=(r // tr,),
            in_specs=[pl.BlockSpec((tr, cols), lambda i, w: (i, 0))],
            out_specs=pl.BlockSpec((1, tr, cols), lambda i, w: (w[1], i, 0))),
        out_shape=jax.ShapeDtypeStruct((N_CHIPS, r, cols), dtype), compiler_params=_cp(VMEM_BIG),
    )(where, a)


def _tri_inv(a_strict):
    r, c = _masks64()
    eye = (r == c).astype(F32)
    blk16 = (r // 16) == (c // 16)
    blk32 = (r // 32) == (c // 32)
    ld = jnp.where(blk16, a_strict, 0.0)
    x = eye - ld
    p = _nn(ld, ld)
    yield
    for step in range(3):
        xp = _nn(x, p)
        if step < 2:
            p = _nn(p, p)
        x = x + xp
        yield
    for lk in (jnp.where(jnp.logical_and(blk32, jnp.logical_not(blk16)), a_strict, 0.0),
               jnp.where(blk32, 0.0, a_strict)):
        y = x - eye
        s = lk + _nn(y, lk)
        yield
        x = x - s - _nn(s, y)
        yield
    return x


def _dn_gates(sa, alog, dtb, chunk_in_seq):
    rows = _iota2((CHUNK, LANE), 0)
    valid = jnp.logical_or(rows >= N_PAD, chunk_in_seq > 0)
    beta_t = _sigmoid(sa)
    ea = jnp.exp(alog)
    g_t = jnp.where(valid, -ea * _softplus(sa + dtb), 0.0)
    r, c = _masks64()
    ltri = (r >= c).astype(F32)
    gam_t = _tri_sum(ltri, g_t)
    return beta_t, g_t, gam_t, valid, ea


def _dn_intra_fwd(qn, kn, v, projp, alog_row, dtb_row, *, nc_seq, name):
    n = qn.shape[0]
    nct = n // CHUNK
    hw = NH * DN_D
    scale = DN_D ** -0.5

    grp = _group(nc_seq)

    def inner(gi, q_ref, k_ref, v_ref, sa_ref, al_ref, dt_ref, u_ref, w_ref, qg_ref, kd_ref, p_ref, t_ref, gl_ref):
        ci = (pl.program_id(0) * grp + gi) % nc_seq
        beta_t, _, gam_t, _, _ = _dn_gates(sa_ref[...], al_ref[...], dt_ref[...], ci)
        yield
        gam_tt = gam_t.T
        r, c = _masks64()
        incl = r >= c
        strict = r > c

        def head(h):
            sl = slice(h * DN_D, (h + 1) * DN_D)
            beta_w = jnp.broadcast_to(beta_t[:, h:h + 1], (CHUNK, DN_D))
            gam_w = jnp.broadcast_to(gam_t[:, 4 + h:5 + h], (CHUNK, DN_D))
            gam_row = gam_tt[4 + h:5 + h, :]
            gl = gam_t[CHUNK - 1:CHUNK, 4 + h:5 + h]
            dec = jnp.exp(jnp.where(incl, gam_w[:, 0:CHUNK] - gam_row, -jnp.inf))
            kh = k_ref[:, sl].astype(F32)
            qh = q_ref[:, sl].astype(F32) * scale
            vh = v_ref[:, sl].astype(F32)
            kk = _nt(kh, kh)
            qk = _nt(qh, kh)
            yield
            a = jnp.where(strict, beta_w[:, 0:CHUNK] * kk * dec, 0.0)
            tm = yield from _tri_inv(a)
            egam_w = jnp.exp(gam_w)
            u_ref[:, sl] = _nn(tm, beta_w * vh).astype(u_ref.dtype)
            w_ref[:, sl] = _nn(tm, (beta_w * egam_w) * kh).astype(w_ref.dtype)
            qg_ref[:, sl] = (egam_w * qh).astype(qg_ref.dtype)
            kd_ref[:, sl] = (jnp.exp(gl - gam_w) * kh).astype(kd_ref.dtype)
            p_ref[0, h] = qk * dec
            t_ref[0, h] = tm
            gl_ref[0, h:h + 1, :] = jnp.broadcast_to(jnp.exp(gl), (1, LANE))

        yield from _round_robin([head(h) for h in range(NH)])

    rows = grp * CHUNK
    row = pl.BlockSpec((rows, hw), lambda i: (i, 0))
    vec = pl.BlockSpec((1, LANE), lambda i: (0, 0))
    mat = pl.BlockSpec((grp, NH, CHUNK, CHUNK), lambda i: (i, 0, 0, 0))
    big = jax.ShapeDtypeStruct((n, hw), BF16)
    msd = jax.ShapeDtypeStruct((nct, NH, CHUNK, CHUNK), F32)
    kinds = ["row"] * 4 + ["whole"] * 2 + ["row"] * 4 + ["lead"] * 3
    return pl.pallas_call(
        _per_chunk(inner, kinds, grp), name=name, grid=(nct // grp,),
        in_specs=[row, row, row, pl.BlockSpec((rows, LANE), lambda i: (i, 0)), vec, vec],
        out_specs=[row, row, row, row, mat, mat, pl.BlockSpec((grp, NH, LANE), lambda i: (i, 0, 0))],
        out_shape=[big, big, big, big, msd, msd, jax.ShapeDtypeStruct((nct, NH, LANE), F32)],
        compiler_params=_cp(VMEM_BIG),
    )(qn, kn, v, projp, alog_row, dtb_row)


def _dn_scan_fwd(u, w, qg, kd, p, gl, *, bsz, nc_seq, name, rider=None):
    hw = NH * DN_D
    t_seq = nc_seq * CHUNK
    u, w, qg, kd = (z.reshape(bsz, t_seq, hw) for z in (u, w, qg, kd))
    p = p.reshape(bsz, nc_seq, NH, CHUNK, CHUNK)
    gl = gl.reshape(bsz, nc_seq, NH, LANE)

    def body(u_ref, w_ref, qg_ref, kd_ref, p_ref, gl_ref, o_ref, vn_ref, hist_ref, s_ref):
        @pl.when(pl.program_id(0) == 0)
        def _():
            s_ref[...] = jnp.zeros_like(s_ref)

        def chain(b, h):
            sl = slice(h * DN_D, (h + 1) * DN_D)
            s = s_ref[b, h]
            hist_ref[b, 0, h] = s.astype(hist_ref.dtype)
            ws = _nn(w_ref[b, :, sl], s)
            qs = _nn(qg_ref[b, :, sl], s)
            yield
            vn = u_ref[b, :, sl] - ws
            vn_ref[b, :, sl] = vn.astype(vn_ref.dtype)
            o_ref[b, :, sl] = (qs + _nn(p_ref[b, 0, h], vn)).astype(o_ref.dtype)
            s_ref[b, h] = gl_ref[b, 0, h:h + 1, :] * s + _tn(kd_ref[b, :, sl], vn)

        _run([chain(b, h) for b in range(bsz) for h in range(NH)])

    row = pl.BlockSpec((bsz, CHUNK, hw), lambda i: (0, i, 0))
    outs, ridden = _hosted_call(
        body, rider, name=name, grid=(nc_seq,),
        in_specs=[row, row, row, row, pl.BlockSpec((bsz, 1, NH, CHUNK, CHUNK), lambda i: (0, i, 0, 0, 0)),
                  pl.BlockSpec((bsz, 1, NH, LANE), lambda i: (0, i, 0, 0))],
        out_specs=[row, row, pl.BlockSpec((bsz, 1, NH, DN_D, DN_D), lambda i: (0, i, 0, 0, 0))],
        out_shape=[jax.ShapeDtypeStruct((bsz, t_seq, hw), BF16), jax.ShapeDtypeStruct((bsz, t_seq, hw), BF16),
                   jax.ShapeDtypeStruct((bsz, nc_seq, NH, DN_D, DN_D), BF16)],
        scratch_shapes=[pltpu.VMEM((bsz, NH, DN_D, DN_D), F32)],
        compiler_params=_cp(VMEM_BIG, ("arbitrary",)), args=(u, w, qg, kd, p, gl))
    o, vn, hist = outs
    return o.reshape(bsz * t_seq, hw), vn.reshape(bsz * t_seq, hw), hist, ridden


def _dn_scan_bwd(do, w, qg, kd, vn, p, gl, hist, *, bsz, nc_seq, name):
    hw = NH * DN_D
    t_seq = nc_seq * CHUNK
    do, w, qg, kd, vn = (z.reshape(bsz, t_seq, hw) for z in (do, w, qg, kd, vn))
    p = p.reshape(bsz, nc_seq, NH, CHUNK, CHUNK)
    gl = gl.reshape(bsz, nc_seq, NH, LANE)

    def body(do_ref, w_ref, qg_ref, kd_ref, vn_ref, p_ref, gl_ref, hist_ref,
             du_ref, dw_ref, dqg_ref, dkd_ref, dgl_ref, ds_ref):
        @pl.when(pl.program_id(0) == 0)
        def _():
            ds_ref[...] = jnp.zeros_like(ds_ref)

        def chain(b, h):
            sl = slice(h * DN_D, (h + 1) * DN_D)
            s = hist_ref[b, 0, h]
            dsn = ds_ref[b, h]
            doh = do_ref[b, :, sl]
            vnh = vn_ref[b, :, sl]
            kdh = kd_ref[b, :, sl]
            dvn = _tn(p_ref[b, 0, h], doh) + _nn(kdh, dsn)
            du_ref[b, :, sl] = dvn.astype(du_ref.dtype)
            dqg_ref[b, :, sl] = _nt(doh, s).astype(dqg_ref.dtype)
            dkd_ref[b, :, sl] = _nt(vnh, dsn).astype(dkd_ref.dtype)
            ds_part = _tn(qg_ref[b, :, sl], doh) + gl_ref[b, 0, h:h + 1, :] * dsn
            dgl = jnp.sum(jnp.sum(dsn * s, axis=0, keepdims=True), axis=1, keepdims=True)
            dgl_ref[b, 0, h:h + 1, :] = jnp.broadcast_to(dgl, (1, LANE))
            yield
            dw_ref[b, :, sl] = (-_nt(dvn, s)).astype(dw_ref.dtype)
            ds_ref[b, h] = ds_part - _tn(w_ref[b, :, sl], dvn)

        _run([chain(b, h) for b in range(bsz) for h in range(NH)])

    rev = lambda i: nc_seq - 1 - i
    row = pl.BlockSpec((bsz, CHUNK, hw), lambda i: (0, rev(i), 0))
    mat = pl.BlockSpec((bsz, 1, NH, CHUNK, CHUNK), lambda i: (0, rev(i), 0, 0, 0))
    glb = pl.BlockSpec((bsz, 1, NH, LANE), lambda i: (0, rev(i), 0, 0))
    big = jax.ShapeDtypeStruct((bsz, t_seq, hw), BF16)
    outs = pl.pallas_call(
        body, name=name, grid=(nc_seq,),
        in_specs=[row, row, row, row, row, mat, glb,
                  pl.BlockSpec((bsz, 1, NH, DN_D, DN_D), lambda i: (0, rev(i), 0, 0, 0))],
        out_specs=[row, row, row, row, glb],
        out_shape=[big, big, jax.ShapeDtypeStruct(big.shape, F32), jax.ShapeDtypeStruct(big.shape, F32),
                   jax.ShapeDtypeStruct((bsz, nc_seq, NH, LANE), F32)],
        scratch_shapes=[pltpu.VMEM((bsz, NH, DN_D, DN_D), F32)],
        compiler_params=_cp(VMEM_BIG, ("arbitrary",)),
    )(do, w, qg, kd, vn, p, gl, hist)
    du, dw, dqg, dkd, dgl = outs
    n = bsz * t_seq
    return (du.reshape(n, hw), dw.reshape(n, hw), dqg.reshape(n, hw), dkd.reshape(n, hw),
            dgl.reshape(bsz * nc_seq, NH, LANE))


def _dn_intra_bwd(qn, kn, v, projp, alog_row, dtb_row, u, w, tmat, du, dw, dqg, dkd, do, vn, dgl, *, nc_seq, name,
                  rider=None):
    n = qn.shape[0]
    nct = n // CHUNK
    hw = NH * DN_D
    scale = DN_D ** -0.5

    grp = _group(nc_seq)

    def inner(gi, q_ref, k_ref, v_ref, sa_ref, al_ref, dt_ref, u_ref, w_ref, t_ref, du_ref, dw_ref, dqg_ref, dkd_ref,
              do_ref, vn_ref, dgl_ref, dq_ref, dk_ref, dv_ref, dsa_ref, dal_ref, ddt_ref):
        ci = (pl.program_id(0) * grp + gi) % nc_seq
        sa = sa_ref[...]
        beta_t, g_t, gam_t, valid, ea = _dn_gates(sa, al_ref[...], dt_ref[...], ci)
        yield
        lane = _iota2((CHUNK, LANE), 1)
        gates_t = jnp.where(lane < 4, beta_t, gam_t).T
        r, c = _masks64()
        incl, strict, upper, supper = r >= c, r > c, r <= c, r < c
        rows1 = _iota2((CHUNK, 1), 0)
        acc = [jnp.zeros((CHUNK, LANE), F32)]

        def head(h):
            sl = slice(h * DN_D, (h + 1) * DN_D)
            beta_w = jnp.broadcast_to(beta_t[:, h:h + 1], (CHUNK, DN_D))
            gam_w = jnp.broadcast_to(gam_t[:, 4 + h:5 + h], (CHUNK, DN_D))
            beta_s, gam_s = beta_w[:, 0:CHUNK], gam_w[:, 0:CHUNK]
            beta_row = gates_t[h:h + 1, :]
            gam_row = gates_t[4 + h:5 + h, :]
            gl = gam_t[CHUNK - 1:CHUNK, 4 + h:5 + h]
            dec = jnp.exp(jnp.where(incl, gam_s - gam_row, -jnp.inf))
            dec_t = jnp.exp(jnp.where(upper, gam_row - gam_s, -jnp.inf))
            egam_w = jnp.exp(gam_w)
            ekd_w = jnp.exp(gl - gam_w)
            kh = k_ref[:, sl].astype(F32)
            qh = q_ref[:, sl].astype(F32) * scale
            vh = v_ref[:, sl].astype(F32)
            uh = u_ref[:, sl]
            wh = w_ref[:, sl]
            doh = do_ref[:, sl]
            vnh = vn_ref[:, sl]
            kk = _nt(kh, kh)
            qk = _nt(qh, kh)
            qk_t = _nt(kh, qh)
            dp = _nt(doh, vnh)
            dp_t = _nt(vnh, doh)
            tm_t = t_ref[0, h].T
            dvb = _nn(tm_t, du_ref[:, sl])
            dkg = _nn(tm_t, dw_ref[:, sl])
            yield
            m = _nt(dvb, uh) + _nt(dkg, wh)
            m_t = _nt(uh, dvb) + _nt(wh, dkg)
            yield
            da = jnp.where(strict, -m, 0.0)
            da_t = jnp.where(supper, -m_t, 0.0)
            a = jnp.where(strict, beta_s * kk * dec, 0.0)
            a_t = jnp.where(supper, beta_row * kk * dec_t, 0.0)
            dad = da * dec
            dad_t = da_t * dec_t
            dpm = jnp.where(incl, dp, 0.0)
            dpm_t = jnp.where(upper, dp_t, 0.0)
            e = da * a + dpm * (qk * dec)
            e_t = da_t * a_t + dpm_t * (qk_t * dec_t)
            dqgh = dqg_ref[:, sl].astype(F32)
            dkdh = dkd_ref[:, sl].astype(F32)
            bg_w = beta_w * egam_w
            dkh = (_nn(beta_s * dad, kh) + _nn(beta_row * dad_t, kh) + _nn(dpm_t * dec_t, qh)
                   + bg_w * dkg + ekd_w * dkdh)
            dqh = _nn(dpm * dec, kh) + egam_w * dqgh
            t_kd = dkdh * (ekd_w * kh)
            dbeta = (jnp.sum(dad * kk, axis=1, keepdims=True)
                     + jnp.sum(dkg * (egam_w * kh) + dvb * vh, axis=1, keepdims=True))
            dgam = (jnp.sum(e - e_t, axis=1, keepdims=True)
                    + jnp.sum(dkg * (bg_w * kh) + dqgh * (egam_w * qh) - t_kd, axis=1, keepdims=True))
            dgam_last = (jnp.sum(jnp.sum(t_kd, axis=0, keepdims=True), axis=1, keepdims=True)
                         + dgl_ref[0, h:h + 1, 0:1] * jnp.exp(gl))
            dgam = dgam + jnp.where(rows1 == CHUNK - 1, dgam_last, 0.0)
            dq_ref[:, sl] = (dqh * scale).astype(dq_ref.dtype)
            dk_ref[:, sl] = dkh.astype(dk_ref.dtype)
            dv_ref[:, sl] = (beta_w * dvb).astype(dv_ref.dtype)
            acc[0] = acc[0] + jnp.where(lane == h, dbeta, 0.0) + jnp.where(lane == 4 + h, dgam, 0.0)

        yield from _round_robin([head(h) for h in range(NH)])
        acc_t = acc[0]
        dg_t = _tri_sum(upper, acc_t)
        ddb = acc_t * beta_t * (1.0 - beta_t)
        dda = jnp.where(valid, dg_t * (-ea) * _sigmoid(sa + dt_ref[...]), 0.0)
        dsa_ref[...] = jnp.where(lane < 4, ddb, jnp.where(lane < 8, dda, 0.0)).astype(dsa_ref.dtype)
        in_g = jnp.logical_and(lane >= 4, lane < 8)
        dal = jnp.sum(jnp.where(in_g, dg_t * g_t, 0.0), axis=0, keepdims=True)
        ddt = jnp.sum(jnp.where(in_g, dda, 0.0), axis=0, keepdims=True)
        _accumulate(dal_ref, dal, gi)
        _accumulate(ddt_ref, ddt, gi)

    rows = grp * CHUNK
    row = pl.BlockSpec((rows, hw), lambda i: (i, 0))
    vec = pl.BlockSpec((1, LANE), lambda i: (0, 0))
    mat = pl.BlockSpec((grp, NH, CHUNK, CHUNK), lambda i: (i, 0, 0, 0))
    glb = pl.BlockSpec((grp, NH, LANE), lambda i: (i, 0, 0))
    big = jax.ShapeDtypeStruct((n, hw), F32)
    v128 = jax.ShapeDtypeStruct((1, LANE), F32)
    kinds = (["row"] * 4 + ["whole"] * 2 + ["row"] * 2 + ["lead"] + ["row"] * 6 + ["lead"]
             + ["row"] * 4 + ["whole"] * 2)
    outs, ridden = _hosted_call(
        _per_chunk(inner, kinds, grp), rider, name=name, grid=(nct // grp,),
        in_specs=[row, row, row, pl.BlockSpec((rows, LANE), lambda i: (i, 0)), vec, vec,
                  row, row, mat, row, row, row, row, row, row, glb],
        out_specs=[row, row, row, pl.BlockSpec((rows, LANE), lambda i: (i, 0)), vec, vec],
        out_shape=[big, big, big, jax.ShapeDtypeStruct((n, LANE), BF16), v128, v128],
        scratch_shapes=[], compiler_params=_cp(VMEM_BIG, ("arbitrary",)),
        args=(qn, kn, v, projp, alog_row, dtb_row, u, w, tmat, du, dw, dqg, dkd, do, vn, dgl))
    return (*outs, ridden)


GQ_W = NH * GLA_DK
GV_W = NH * GLA_DV
GLA_NORM = 16.0
MID = CHUNK // 2


def _gla_gates(sb, w2p, gb, chunk_in_seq):
    rows = _iota2((CHUNK, GQ_W), 0)
    valid = jnp.logical_or(rows >= N_PAD, chunk_in_seq > 0)
    graw = _nn(sb, w2p) + gb
    yield
    g = jnp.where(valid, _logsigmoid(graw) * (1.0 / GLA_NORM), 0.0)
    r, c = _masks64()
    bcum = _tri_sum(r >= c, g)
    yield
    return graw, bcum, valid


def _head_mask(h):
    lane = _iota2((1, GQ_W), 1)
    return jnp.logical_and(lane >= h * GLA_DK, lane < (h + 1) * GLA_DK)


def _gla_intra_fwd(projp, gates, w2p, gb, *, nc_seq, name):
    n = projp.shape[0]
    nct = n // CHUNK
    scale = GLA_DK ** -0.5

    grp = _group(nc_seq)
    rows = grp * CHUNK

    def inner(gi, qk_ref, v_ref, sb_ref, w2_ref, gb_ref, oi_ref, qg_ref, kd_ref, gl_ref):
        ci = (pl.program_id(0) * grp + gi) % nc_seq
        _, bc, _ = yield from _gla_gates(sb_ref[...], w2_ref[...], gb_ref[...], ci)
        bref = bc[MID:MID + 1, :]
        bl = bc[CHUNK - 1:CHUNK, :]
        q = qk_ref[:, 0:GQ_W].astype(F32) * scale
        k = qk_ref[:, GQ_W:2 * GQ_W].astype(F32)
        qi = q * jnp.exp(bc - bref)
        ki = k * jnp.exp(bref - bc)
        qg_ref[...] = (q * jnp.exp(bc)).astype(qg_ref.dtype)
        kd_ref[...] = (k * jnp.exp(bl - bc)).astype(kd_ref.dtype)
        gl_ref[0] = jnp.exp(bl)
        r, c = _masks64()
        incl = r >= c
        a = [jnp.where(incl, _nt(jnp.where(_head_mask(h), qi, 0.0), ki), 0.0) for h in range(NH)]
        yield
        for h in range(NH):
            oi_ref[:, h * GLA_DV:(h + 1) * GLA_DV] = _nn(a[h], v_ref[:, h * GLA_DV:(h + 1) * GLA_DV]).astype(oi_ref.dtype)

    kinds = ["row"] * 3 + ["whole"] * 2 + ["row"] * 3 + ["lead"]
    return pl.pallas_call(
        _per_chunk(inner, kinds, grp), name=name, grid=(nct // grp,),
        in_specs=[pl.BlockSpec((rows, 2 * GQ_W), lambda i: (i, C_GQK // (2 * GQ_W))),
                  pl.BlockSpec((rows, GV_W), lambda i: (i, C_GV // GV_W)),
                  pl.BlockSpec((rows, LANE), lambda i: (i, 1)),
                  pl.BlockSpec((LANE, GQ_W), lambda i: (0, 0)), pl.BlockSpec((1, GQ_W), lambda i: (0, 0))],
        out_specs=[pl.BlockSpec((rows, GV_W), lambda i: (i, 0)), pl.BlockSpec((rows, GQ_W), lambda i: (i, 0)),
                   pl.BlockSpec((rows, GQ_W), lambda i: (i, 0)), pl.BlockSpec((grp, 1, GQ_W), lambda i: (i, 0, 0))],
        out_shape=[jax.ShapeDtypeStruct((n, GV_W), BF16), jax.ShapeDtypeStruct((n, GQ_W), BF16),
                   jax.ShapeDtypeStruct((n, GQ_W), BF16), jax.ShapeDtypeStruct((nct, 1, GQ_W), F32)],
        compiler_params=_cp(VMEM_BIG),
    )(projp, projp, gates, w2p, gb)


def _gla_scan_fwd(oi, qg, kd, gl, projp, *, bsz, nc_seq, name, rider=None):
    t_seq = nc_seq * CHUNK
    oi = oi.reshape(bsz, t_seq, GV_W)
    qg, kd = qg.reshape(bsz, t_seq, GQ_W), kd.reshape(bsz, t_seq, GQ_W)
    gl = gl.reshape(bsz, nc_seq, 1, GQ_W)
    pj = projp.reshape(bsz, t_seq, PW)

    def body(oi_ref, qg_ref, kd_ref, gl_ref, v_ref, o_ref, hist_ref, st_ref):
        @pl.when(pl.program_id(0) == 0)
        def _():
            st_ref[...] = jnp.zeros_like(st_ref)

        for b in range(bsz):
            st = st_ref[b]
            hist_ref[b, 0] = st.astype(hist_ref.dtype)
            qgb = qg_ref[b]
            kdb = kd_ref[b]
            upd = jnp.zeros((GLA_DV, GQ_W), F32)
            for h in range(NH):
                sl = slice(h * GLA_DV, (h + 1) * GLA_DV)
                m = _head_mask(h)
                o_ref[b, :, sl] = (oi_ref[b, :, sl] + _nt(jnp.where(m, qgb, 0.0), st)).astype(o_ref.dtype)
                upd = upd + jnp.where(m, _tn(v_ref[b, :, sl], kdb), 0.0)
            st_ref[b] = gl_ref[b, 0] * st + upd

    outs, ridden = _hosted_call(
        body, rider, name=name, grid=(nc_seq,),
        in_specs=[pl.BlockSpec((bsz, CHUNK, GV_W), lambda i: (0, i, 0)),
                  pl.BlockSpec((bsz, CHUNK, GQ_W), lambda i: (0, i, 0)),
                  pl.BlockSpec((bsz, CHUNK, GQ_W), lambda i: (0, i, 0)),
                  pl.BlockSpec((bsz, 1, 1, GQ_W), lambda i: (0, i, 0, 0)),
                  pl.BlockSpec((bsz, CHUNK, GV_W), lambda i: (0, i, C_GV // GV_W))],
        out_specs=[pl.BlockSpec((bsz, CHUNK, GV_W), lambda i: (0, i, 0)),
                   pl.BlockSpec((bsz, 1, GLA_DV, GQ_W), lambda i: (0, i, 0, 0))],
        out_shape=[jax.ShapeDtypeStruct((bsz, t_seq, GV_W), BF16),
                   jax.ShapeDtypeStruct((bsz, nc_seq, GLA_DV, GQ_W), BF16)],
        scratch_shapes=[pltpu.VMEM((bsz, GLA_DV, GQ_W), F32)],
        compiler_params=_cp(VMEM_BIG, ("arbitrary",)), args=(oi, qg, kd, gl, pj))
    return outs[0].reshape(bsz * t_seq, GV_W), outs[1], ridden


def _gla_scan_bwd(do, qg, kd, gl, projp, hist, *, bsz, nc_seq, name):
    t_seq = nc_seq * CHUNK
    do = do.reshape(bsz, t_seq, GV_W)
    qg, kd = qg.reshape(bsz, t_seq, GQ_W), kd.reshape(bsz, t_seq, GQ_W)
    gl = gl.reshape(bsz, nc_seq, 1, GQ_W)
    pj = projp.reshape(bsz, t_seq, PW)

    def body(do_ref, qg_ref, kd_ref, gl_ref, v_ref, hist_ref, dqg_ref, dkd_ref, dv_ref, dgl_ref, dst_ref):
        @pl.when(pl.program_id(0) == 0)
        def _():
            dst_ref[...] = jnp.zeros_like(dst_ref)

        for b in range(bsz):
            st = hist_ref[b, 0]
            dst = dst_ref[b]
            qgb = qg_ref[b]
            kdb = kd_ref[b]
            dqg = jnp.zeros((CHUNK, GQ_W), F32)
            dkd = jnp.zeros((CHUNK, GQ_W), F32)
            add = jnp.zeros((GLA_DV, GQ_W), F32)
            for h in range(NH):
                sl = slice(h * GLA_DV, (h + 1) * GLA_DV)
                m = _head_mask(h)
                doh = do_ref[b, :, sl]
                vh = v_ref[b, :, sl]
                dqg = dqg + jnp.where(m, _nn(doh, st), 0.0)
                dkd = dkd + jnp.where(m, _nn(vh, dst), 0.0)
                dv_ref[b, :, sl] = _nt(jnp.where(m, kdb, 0.0), dst).astype(dv_ref.dtype)
                add = add + jnp.where(m, _tn(doh, qgb), 0.0)
            dqg_ref[b] = dqg.astype(dqg_ref.dtype)
            dkd_ref[b] = dkd.astype(dkd_ref.dtype)
            dgl_ref[b, 0] = jnp.sum(dst * st, axis=0, keepdims=True)
            dst_ref[b] = gl_ref[b, 0] * dst + add

    rev = lambda i: nc_seq - 1 - i
    outs = pl.pallas_call(
        body, name=name, grid=(nc_seq,),
        in_specs=[pl.BlockSpec((bsz, CHUNK, GV_W), lambda i: (0, rev(i), 0)),
                  pl.BlockSpec((bsz, CHUNK, GQ_W), lambda i: (0, rev(i), 0)),
                  pl.BlockSpec((bsz, CHUNK, GQ_W), lambda i: (0, rev(i), 0)),
                  pl.BlockSpec((bsz, 1, 1, GQ_W), lambda i: (0, rev(i), 0, 0)),
                  pl.BlockSpec((bsz, CHUNK, GV_W), lambda i: (0, rev(i), C_GV // GV_W)),
                  pl.BlockSpec((bsz, 1, GLA_DV, GQ_W), lambda i: (0, rev(i), 0, 0))],
        out_specs=[pl.BlockSpec((bsz, CHUNK, GQ_W), lambda i: (0, rev(i), 0)),
                   pl.BlockSpec((bsz, CHUNK, GQ_W), lambda i: (0, rev(i), 0)),
                   pl.BlockSpec((bsz, CHUNK, GV_W), lambda i: (0, rev(i), 0)),
                   pl.BlockSpec((bsz, 1, 1, GQ_W), lambda i: (0, rev(i), 0, 0))],
        out_shape=[jax.ShapeDtypeStruct((bsz, t_seq, GQ_W), F32), jax.ShapeDtypeStruct((bsz, t_seq, GQ_W), F32),
                   jax.ShapeDtypeStruct((bsz, t_seq, GV_W), BF16), jax.ShapeDtypeStruct((bsz, nc_seq, 1, GQ_W), F32)],
        scratch_shapes=[pltpu.VMEM((bsz, GLA_DV, GQ_W), F32)],
        compiler_params=_cp(VMEM_BIG, ("arbitrary",)),
    )(do, qg, kd, gl, pj, hist)
    n = bsz * t_seq
    return (outs[0].reshape(n, GQ_W), outs[1].reshape(n, GQ_W), outs[2].reshape(n, GV_W),
            outs[3].reshape(bsz * nc_seq, 1, GQ_W))


def _gla_intra_bwd(projp, gates, w2p, gb, do, dqg, dkd, dvi, dgl, *, nc_seq, name):
    n = projp.shape[0]
    nct = n // CHUNK
    scale = GLA_DK ** -0.5

    grp = _group(nc_seq)
    rows = grp * CHUNK

    def inner(gi, qk_ref, v_ref, sb_ref, w2_ref, gb_ref, do_ref, dqg_ref, dkd_ref, dvi_ref, dgl_ref,
              dqk_ref, dv_ref, dsb_ref, dw2_ref, dgb_ref):
        ci = (pl.program_id(0) * grp + gi) % nc_seq
        sb = sb_ref[...]
        w2 = w2_ref[...]
        graw, bc, valid = yield from _gla_gates(sb, w2, gb_ref[...], ci)
        bref = bc[MID:MID + 1, :]
        bl = bc[CHUNK - 1:CHUNK, :]
        q = qk_ref[:, 0:GQ_W].astype(F32) * scale
        k = qk_ref[:, GQ_W:2 * GQ_W].astype(F32)
        ex1 = jnp.exp(bc - bref)
        ex2 = jnp.exp(bref - bc)
        eb = jnp.exp(bc)
        ekd = jnp.exp(bl - bc)
        qi, ki = q * ex1, k * ex2
        r, c = _masks64()
        incl = r >= c
        upper = r <= c
        a_t, da, da_t = [], [], []
        for h in range(NH):
            sl = slice(h * GLA_DV, (h + 1) * GLA_DV)
            doh = do_ref[:, sl]
            vh = v_ref[:, sl]
            a_t.append(jnp.where(upper, _nt(jnp.where(_head_mask(h), ki, 0.0), qi), 0.0))
            da.append(jnp.where(incl, _nt(doh, vh), 0.0))
            da_t.append(jnp.where(upper, _nt(vh, doh), 0.0))
        yield
        dqi = jnp.zeros((CHUNK, GQ_W), F32)
        dki = jnp.zeros((CHUNK, GQ_W), F32)
        for h in range(NH):
            sl = slice(h * GLA_DV, (h + 1) * GLA_DV)
            m = _head_mask(h)
            dv_ref[:, sl] = (_nn(a_t[h], do_ref[:, sl]) + dvi_ref[:, sl]).astype(dv_ref.dtype)
            dqi = dqi + jnp.where(m, _nn(da[h], ki), 0.0)
            dki = dki + jnp.where(m, _nn(da_t[h], qi), 0.0)
        yield
        dqg = dqg_ref[...].astype(F32)
        dkd = dkd_ref[...].astype(F32)
        dqk_ref[:, 0:GQ_W] = ((dqi * ex1 + dqg * eb) * scale).astype(dqk_ref.dtype)
        dqk_ref[:, GQ_W:2 * GQ_W] = (dki * ex2 + dkd * ekd).astype(dqk_ref.dtype)
        t_qi, t_ki, t_kd = dqi * qi, dki * ki, dkd * (k * ekd)
        db = t_qi - t_ki + dqg * (q * eb) - t_kd
        dbref = jnp.sum(t_ki - t_qi, axis=0, keepdims=True)
        dbl = jnp.sum(t_kd, axis=0, keepdims=True) + dgl_ref[0] * jnp.exp(bl)
        rows = _iota2((CHUNK, GQ_W), 0)
        db = db + jnp.where(rows == MID, dbref, 0.0) + jnp.where(rows == CHUNK - 1, dbl, 0.0)
        dg = _tri_sum(upper, db)
        yield
        dgraw = jnp.where(valid, dg * (1.0 / GLA_NORM) * _sigmoid(-graw), 0.0)
        dsb_ref[...] = _nt(dgraw, w2).astype(dsb_ref.dtype)
        dw2 = _tn(sb, dgraw)
        dgb = jnp.sum(dgraw, axis=0, keepdims=True)
        _accumulate(dw2_ref, dw2, gi)
        _accumulate(dgb_ref, dgb, gi)

    rq = pl.BlockSpec((rows, GQ_W), lambda i: (i, 0))
    rv = pl.BlockSpec((rows, GV_W), lambda i: (i, 0))
    kinds = ["row"] * 3 + ["whole"] * 2 + ["row"] * 4 + ["lead"] + ["row"] * 3 + ["whole"] * 2
    return pl.pallas_call(
        _per_chunk(inner, kinds, grp), name=name, grid=(nct // grp,),
        in_specs=[pl.BlockSpec((rows, 2 * GQ_W), lambda i: (i, C_GQK // (2 * GQ_W))),
                  pl.BlockSpec((rows, GV_W), lambda i: (i, C_GV // GV_W)),
                  pl.BlockSpec((rows, LANE), lambda i: (i, 1)),
                  pl.BlockSpec((LANE, GQ_W), lambda i: (0, 0)), pl.BlockSpec((1, GQ_W), lambda i: (0, 0)),
                  rv, rq, rq, rv, pl.BlockSpec((grp, 1, GQ_W), lambda i: (i, 0, 0))],
        out_specs=[pl.BlockSpec((rows, 2 * GQ_W), lambda i: (i, 0)), rv, pl.BlockSpec((rows, LANE), lambda i: (i, 0)),
                   pl.BlockSpec((LANE, GQ_W), lambda i: (0, 0)), pl.BlockSpec((1, GQ_W), lambda i: (0, 0))],
        out_shape=[jax.ShapeDtypeStruct((n, 2 * GQ_W), BF16), jax.ShapeDtypeStruct((n, GV_W), BF16),
                   jax.ShapeDtypeStruct((n, LANE), BF16), jax.ShapeDtypeStruct((LANE, GQ_W), F32),
                   jax.ShapeDtypeStruct((1, GQ_W), F32)],
        compiler_params=_cp(VMEM_BIG, ("arbitrary",)),
    )(projp, projp, gates, w2p, gb, do, dqg, dkd, dvi, dgl)


SECTIONS = ((C_QKV, 1536), (C_DZ, 512), (C_GQK, 512), (C_GV, 512), (C_GR, 512), (C_SA, 128), (C_SB, 128))


def _inproj_bwd(secs, wp, h0, g1, dx1, *, tr, name):
    n, d = h0.shape

    def body(*refs):
        sec_refs = refs[:len(SECTIONS)]
        wp_ref, h0_ref, g_ref, dx1_ref, o_ref, dg_ref = refs[len(SECTIONS):]
        dh = None
        for s_ref, (off, wd) in zip(sec_refs, SECTIONS):
            part = _nt(s_ref[...], wp_ref[:, off:off + wd])
            dh = part if dh is None else dh + part
        dx, dg = _rms_bwd_math(h0_ref[...], g_ref[...], dh)
        o_ref[...] = dx1_ref[...] + dx

        @pl.when(pl.program_id(0) == 0)
        def _():
            dg_ref[...] = dg

        @pl.when(pl.program_id(0) > 0)
        def _():
            dg_ref[...] += dg

    row = pl.BlockSpec((tr, d), lambda i: (i, 0))
    vec = pl.BlockSpec((1, d), lambda i: (0, 0))
    return pl.pallas_call(
        body, name=name, grid=(n // tr,),
        in_specs=[pl.BlockSpec((tr, wd), lambda i: (i, 0)) for _, wd in SECTIONS]
        + [pl.BlockSpec((d, PW), lambda i: (0, 0)), row, vec, row],
        out_specs=[row, vec],
        out_shape=[jax.ShapeDtypeStruct((n, d), F32), jax.ShapeDtypeStruct((1, d), F32)],
        compiler_params=_cp(VMEM_BIG),
    )(*secs, wp, h0, g1, dx1)


def _adamw(w, g, m, v, *, name, emit_grad=False):
    lead = w.ndim - 2
    r, c = w.shape[-2:]
    tr = _tile(r, 256, 8) if r > 256 else r
    c1 = 1.0 - ADAM_B1 ** ADAM_STEP
    c2 = 1.0 - ADAM_B2 ** ADAM_STEP
    n_out = 4 if emit_grad else 3

    def body(w_ref, g_ref, m_ref, v_ref, *out_refs):
        rd = (lambda ref: ref[0]) if lead else (lambda ref: ref[...])
        gv = g_ref[:, 0:c]
        nm = ADAM_B1 * rd(m_ref) + (1.0 - ADAM_B1) * gv
        nv = ADAM_B2 * rd(v_ref) + (1.0 - ADAM_B2) * (gv * gv)
        res = [-ADAM_LR * ((nm / c1) / (jnp.sqrt(nv / c2) + ADAM_EPS) + ADAM_WD * rd(w_ref)), nm, nv, gv]
        for o_ref, val in zip(out_refs, res):
            if lead:
                o_ref[0] = val
            else:
                o_ref[...] = val

    blk = pl.BlockSpec((1,) * lead + (tr, c), lambda i: (0,) * lead + (i, 0))
    gblk = pl.BlockSpec((tr, g.shape[1]), lambda i: (i, 0))
    sds = jax.ShapeDtypeStruct(w.shape, F32)
    return pl.pallas_call(
        body, name=name, grid=(r // tr,), in_specs=[blk, gblk, blk, blk], out_specs=[blk] * n_out,
        out_shape=[sds] * n_out, compiler_params=_cp(VMEM_BIG),
    )(w, g, m, v)


def _pair_sum(where, g, theirs, *, name):
    lead, r, cols = g.shape
    half = r // 2
    tr = _tile(half, 256, 16)
    nh = half // tr

    def body(w_ref, a_ref, b_ref, o_ref):
        o_ref[...] = (a_ref[...] + b_ref[...]).astype(o_ref.dtype)

    blk = pl.BlockSpec((1, tr, cols), lambda s, i, w: (s, i, 0))
    return pl.pallas_call(
        body, name=name,
        grid_spec=pltpu.PrefetchScalarGridSpec(
            num_scalar_prefetch=1, grid=(lead, nh),
            in_specs=[pl.BlockSpec((1, tr, cols), lambda s, i, w: (s, w[0] * nh + i, 0)), blk], out_specs=blk),
        out_shape=jax.ShapeDtypeStruct((lead, half, cols), BF16), compiler_params=_cp(VMEM_BIG),
    )(where, g, theirs)


def _chip_sum(where, pair, q, *, name):
    _, half, cols = pair.shape
    tr = _tile(half, 256, 16)
    nh = half // tr

    def body(w_ref, own_ref, q1_ref, q2_ref, q3_ref, o_ref):
        f = lambda ref: ref[0].astype(F32)
        o_ref[...] = ((f(own_ref) + f(q1_ref)) + f(q2_ref)) + f(q3_ref)

    def peer(d):
        return pl.BlockSpec((1, tr, cols), lambda i, w: ((w[1] + d) % N_CHIPS, i, 0))

    return pl.pallas_call(
        body, name=name,
        grid_spec=pltpu.PrefetchScalarGridSpec(
            num_scalar_prefetch=1, grid=(nh,),
            in_specs=[peer(0), peer(1), peer(2), peer(3)],
            out_specs=pl.BlockSpec((tr, cols), lambda i, w: (w[0] * nh + i, 0))),
        out_shape=jax.ShapeDtypeStruct((2 * half, cols), F32), compiler_params=_cp(VMEM_BIG),
    )(where, pair, q, q, q)


VM = pl.BlockSpec(memory_space=pltpu.VMEM)


def _row_chunks(rows, n_split):
    size = rows // n_split
    assert size * n_split == rows and size % 16 == 0, (rows, n_split)
    return [(s, pl.ds(s * size, size)) for s in range(n_split)], size


D2D_SPLIT = 4
ICI_SPLIT = 2


def _sibling_halves(grads):
    n_arr = len(grads)

    def body(*refs):
        ins = refs[:n_arr]
        theirs = refs[n_arr:2 * n_arr]
        send_sems, recv_sems = refs[2 * n_arr:]
        x, y, c = _place()
        copies = []
        for k in range(n_arr):
            half = ins[k].shape[1] // 2
            chunks, size = _row_chunks(half, D2D_SPLIT)
            for s, dst_rows in chunks:
                give = pltpu.make_async_remote_copy(
                    src_ref=ins[k].at[:, pl.ds((1 - c) * half + s * size, size), :], dst_ref=theirs[k].at[:, dst_rows, :],
                    send_sem=send_sems.at[k, s], recv_sem=recv_sems.at[k, s], device_id=(x, y, 1 - c),
                    device_id_type=MESH)
                give.start()
                copies.append(give)
        for give in copies:
            give.wait()

    halves = [jax.ShapeDtypeStruct((g.shape[0], g.shape[1] // 2, g.shape[2]), F32) for g in grads]
    sem = pltpu.SemaphoreType.DMA((n_arr, D2D_SPLIT))
    return pl.pallas_call(
        body, name="sibling_halves", in_specs=[ANY] * n_arr, out_specs=[ANY] * n_arr, out_shape=halves,
        scratch_shapes=[sem, sem],
    )(*grads)


def _chip_exchange(parts):
    n_arr = len(parts)

    def body(*refs):
        ins = refs[:n_arr]
        outs = refs[n_arr:2 * n_arr]
        send_sems, recv_sems = refs[2 * n_arr:]
        x, y, c = _place()
        me = 2 * x + y
        sends = []
        for k in range(n_arr):
            chunks, _ = _row_chunks(ins[k].shape[1], ICI_SPLIT)
            for d, (px, py, pj) in enumerate(_other_chips(x, y)):
                for s, rows in chunks:
                    cp = pltpu.make_async_remote_copy(
                        src_ref=ins[k].at[pj, rows, :], dst_ref=outs[k].at[me, rows, :], send_sem=send_sems.at[k, d, s],
                        recv_sem=recv_sems.at[k, d, s], device_id=(px, py, c), device_id_type=MESH)
                    cp.start()
                    sends.append(cp)
        for k in range(n_arr):
            chunks, _ = _row_chunks(ins[k].shape[1], ICI_SPLIT)
            for d, (px, py, pj) in enumerate(_other_chips(x, y)):
                for s, rows in chunks:
                    pltpu.make_async_remote_copy(
                        src_ref=ins[k].at[pj, rows, :], dst_ref=outs[k].at[pj, rows, :], send_sem=send_sems.at[k, d, s],
                        recv_sem=recv_sems.at[k, d, s], device_id=(px, py, c), device_id_type=MESH).wait_recv()
        for cp in sends:
            cp.wait_send()

    sem = pltpu.SemaphoreType.DMA((n_arr, 3, ICI_SPLIT))
    return pl.pallas_call(
        body, name="chip_exchange", in_specs=[ANY] * n_arr, out_specs=[ANY] * n_arr,
        out_shape=[jax.ShapeDtypeStruct(p.shape, p.dtype) for p in parts],
        scratch_shapes=[sem, sem],
    )(*parts)


class _SiblingHalvesRider:
    def __init__(self, grads):
        self.inputs = list(grads)
        self.out_shapes = [jax.ShapeDtypeStruct((g.shape[0], g.shape[1] // 2, g.shape[2]), F32) for g in grads]
        self.aliases = {}
        self.sems = [pltpu.SemaphoreType.DMA((len(grads), D2D_SPLIT))] * 2

    def _copies(self, ins, outs, sems):
        x, y, c = _place()
        for k in range(len(ins)):
            half = ins[k].shape[1] // 2
            chunks, size = _row_chunks(half, D2D_SPLIT)
            for s, dst_rows in chunks:
                yield pltpu.make_async_remote_copy(
                    src_ref=ins[k].at[:, pl.ds((1 - c) * half + s * size, size), :], dst_ref=outs[k].at[:, dst_rows, :],
                    send_sem=sems[0].at[k, s], recv_sem=sems[1].at[k, s], device_id=(x, y, 1 - c), device_id_type=MESH)

    def first(self, ins, outs, sems):
        for cp in self._copies(ins, outs, sems):
            cp.start()

    def last(self, ins, outs, sems):
        for cp in self._copies(ins, outs, sems):
            cp.wait()


class _ChipExchangeRider:
    def __init__(self, parts):
        self.inputs = list(parts)
        self.out_shapes = [jax.ShapeDtypeStruct(p.shape, p.dtype) for p in parts]
        self.aliases = {}
        self.sems = [pltpu.SemaphoreType.DMA((len(parts), 3, ICI_SPLIT))] * 2

    def _copies(self, ins, outs, sems, receiving):
        x, y, c = _place()
        for k in range(len(ins)):
            chunks, _ = _row_chunks(ins[k].shape[1], ICI_SPLIT)
            for d, (px, py, pj) in enumerate(_other_chips(x, y)):
                for s, rows in chunks:
                    yield pltpu.make_async_remote_copy(
                        src_ref=ins[k].at[pj, rows, :], dst_ref=outs[k].at[pj if receiving else 2 * x + y, rows, :],
                        send_sem=sems[0].at[k, d, s], recv_sem=sems[1].at[k, d, s], device_id=(px, py, c),
                        device_id_type=MESH)

    def first(self, ins, outs, sems):
        for cp in self._copies(ins, outs, sems, False):
            cp.start()

    def last(self, ins, outs, sems):
        for cp in self._copies(ins, outs, sems, True):
            cp.wait_recv()
        for cp in self._copies(ins, outs, sems, False):
            cp.wait_send()


def _sibling_join(bufs):
    n_arr = len(bufs)

    def body(*refs):
        bufs_out = refs[n_arr:2 * n_arr]
        send_sems, recv_sems = refs[2 * n_arr:]
        x, y, c = _place()
        copies = []
        for k in range(n_arr):
            half = bufs_out[k].shape[0] // 2
            chunks, size = _row_chunks(half, D2D_SPLIT)
            for s, _ in chunks:
                rows = pl.ds(c * half + s * size, size)
                give = pltpu.make_async_remote_copy(
                    src_ref=bufs_out[k].at[rows, :], dst_ref=bufs_out[k].at[rows, :], send_sem=send_sems.at[k, s],
                    recv_sem=recv_sems.at[k, s], device_id=(x, y, 1 - c), device_id_type=MESH)
                give.start()
                copies.append((k, s, half, size, give))
        for k, s, half, size, give in copies:
            rows = pl.ds((1 - c) * half + s * size, size)
            pltpu.make_async_remote_copy(
                src_ref=bufs_out[k].at[rows, :], dst_ref=bufs_out[k].at[rows, :], send_sem=send_sems.at[k, s],
                recv_sem=recv_sems.at[k, s], device_id=(x, y, 1 - c), device_id_type=MESH).wait_recv()
            give.wait_send()

    sem = pltpu.SemaphoreType.DMA((n_arr, D2D_SPLIT))
    return pl.pallas_call(
        body, name="sibling_join", in_specs=[ANY] * n_arr, out_specs=[ANY] * n_arr,
        out_shape=[jax.ShapeDtypeStruct(b.shape, F32) for b in bufs],
        input_output_aliases={k: k for k in range(n_arr)},
        scratch_shapes=[sem, sem],
    )(*bufs)


PACK_ROWS = 48


def _small_allreduce(pack):
    masks = [(dx, dy, dc) for dx in (0, 1) for dy in (0, 1) for dc in (0, 1)][1:]

    def body(p_ref, o_ref, buf, send_sems, recv_sems):
        x, y, c = _place()
        me = 4 * x + 2 * y + c
        buf[me] = p_ref[...]
        sends = []
        for k, (dx, dy, dc) in enumerate(masks):
            peer = (1 - x if dx else x, 1 - y if dy else y, 1 - c if dc else c)
            cp = pltpu.make_async_remote_copy(
                src_ref=p_ref, dst_ref=buf.at[me], send_sem=send_sems.at[k], recv_sem=recv_sems.at[k],
                device_id=peer, device_id_type=MESH)
            cp.start()
            sends.append(cp)
        for k, (dx, dy, dc) in enumerate(masks):
            peer = (1 - x if dx else x, 1 - y if dy else y, 1 - c if dc else c)
            pj = 4 * peer[0] + 2 * peer[1] + peer[2]
            pltpu.make_async_remote_copy(
                src_ref=p_ref, dst_ref=buf.at[pj], send_sem=send_sems.at[k], recv_sem=recv_sems.at[k],
                device_id=peer, device_id_type=MESH).wait_recv()
        for cp in sends:
            cp.wait_send()
        tot = buf[0]
        for k in range(1, 8):
            tot = tot + buf[k]
        o_ref[...] = tot
        o_ref[0:N_META, :] = tot[0:N_META] + tot[N_META:2 * N_META]

    return pl.pallas_call(
        body, name="small_allreduce", in_specs=[VM], out_specs=VM,
        out_shape=jax.ShapeDtypeStruct((PACK_ROWS, D_MODEL), F32),
        scratch_shapes=[pltpu.VMEM((8, PACK_ROWS, D_MODEL), F32), pltpu.SemaphoreType.DMA((7,)),
                        pltpu.SemaphoreType.DMA((7,))],
    )(pack)


def _pad_lanes(vec, offset):
    k = vec.shape[1]
    return jnp.concatenate([jnp.zeros((1, offset), F32), vec, jnp.zeros((1, LANE - offset - k), F32)], axis=1)


def _local_step(x, tgt, meta, norm1_g, wp, conv_w, a_log, dt_bias, dn_norm_g, gla_w2, gla_b, gla_norm_g,
                w_out, norm2_g, w_up, w_down, final_norm_g, late_gather=None, where=None):
    bsz, s_len, d = x.shape
    t_seq = s_len + CHUNK
    nc_seq = t_seq // CHUNK
    n = bsz * t_seq
    tr = _tile(t_seq, 832)
    tt = _tile(t_seq, 416)

    lead = jnp.concatenate([jnp.zeros((N_PAD, d), F32), meta], axis=0)
    h0 = jnp.concatenate([jnp.broadcast_to(lead[None], (bsz, CHUNK, d)), x], axis=1).reshape(n, d)
    tgt_p = jnp.concatenate([jnp.zeros((bsz, CHUNK, d), F32), tgt], axis=1).reshape(n, d)
    alog_row = _pad_lanes(a_log, 4)
    dtb_row = _pad_lanes(dt_bias, 4)
    w2p = jnp.concatenate([gla_w2, jnp.zeros((LANE - GLA_RANK, GQ_W), F32)], axis=0)

    h = _rms_fwd(h0, norm1_g, tr=tr, name="norm1")
    projp, gates = _mm(h, wp, "nn", tm=tt, tn=PW, tk=d, out_dtypes=(BF16, F32), out_widths=(PW, PW - C_SA),
                       epilogue=lambda acc: (acc, acc[:, C_SA:PW]), name="in_proj")
    qn, kn, v = _dnprep_fwd(projp, conv_w, bsz=bsz, t_seq=t_seq, tt=tt, name="dn_prep")
    u, w, qg, kd, pmat, tmat, gl = _dn_intra_fwd(qn, kn, v, gates, alog_row, dtb_row, nc_seq=nc_seq, name="dn_intra")
    ride_a, ride_b = late_gather if late_gather is not None else (None, None)
    o_dn, vn, hist, got_a = _dn_scan_fwd(u, w, qg, kd, pmat, gl, bsz=bsz, nc_seq=nc_seq, name="dn_scan", rider=ride_a)
    oi, gqg, gkd, ggl = _gla_intra_fwd(projp, gates, w2p, gla_b, nc_seq=nc_seq, name="gla_intra")
    o_gla, ghist, got_b = _gla_scan_fwd(oi, gqg, gkd, ggl, projp, bsz=bsz, nc_seq=nc_seq, name="gla_scan", rider=ride_b)
    if late_gather is not None:
        w_out = got_a[0].reshape(d, d)
        w_up = got_a[1].transpose(1, 0, 2).reshape(d, D_FF)
        w_down = got_b[0].reshape(D_FF, d)
    mix = _gnorm_fwd(o_dn, o_gla, projp, dn_norm_g, gla_norm_g, tr=tr, name="gated_norm")
    (x1,) = _mm(mix, w_out, "nn", tm=tr, tn=d, tk=d, out_dtypes=(F32,), extras=(h0,),
                epilogue=lambda acc, res: (res + acc,), name="out_proj")
    h2 = _rms_fwd(x1, norm2_g, tr=tr, name="norm2")

    (act,) = _mm(h2, w_up, "nn", tm=tt, tn=D_FF, tk=d, out_dtypes=(BF16,),
                 epilogue=lambda acc: (jnp.square(jnp.maximum(acc, 0.0)),), name="mlp_up")
    dx2, dx2b, d_final_g, loss_tile = _mlp_down_loss(act, w_down, x1, final_norm_g, tgt_p, t_seq=t_seq, tr=tt,
                                                     name="mlp_down_loss")

    (dup,) = _mm(dx2b, w_down, "nt", tm=tt, tn=D_FF, tk=d, out_dtypes=(BF16,), extras=(act,),
                 epilogue=lambda acc, a: (acc * (2.0 * jnp.sqrt(a.astype(F32))),), name="mlp_down_bwd")
    (d_w_down,) = _mm(act, dx2b, "tn", tm=D_FF // 2, tn=d, tk=tr, out_dtypes=(F32,), name="w_down_grad")
    (d_w_up,) = _mm(h2, dup, "tn", tm=d, tn=D_FF // 2, tk=tr, out_dtypes=(F32,), name="w_up_grad")
    mlp_sm = [d_w_up.reshape(d, N_CHIPS, D_FF // N_CHIPS).transpose(1, 0, 2), d_w_down.reshape(N_CHIPS, D_FF // N_CHIPS, d)]
    ride1 = _SiblingHalvesRider(mlp_sm) if where is not None else None
    dx1, dx1b, d_norm2_g, theirs = _mlp_up_bwd_norm(dup, w_up, x1, norm2_g, dx2, tr=tt, name="mlp_up_bwd_norm",
                                                    rider=ride1)
    ride2 = None
    if where is not None:
        mlp_pair = [_pair_sum(where, a, b, name=f"pair_sum_mlp{k}") for k, (a, b) in enumerate(zip(mlp_sm, theirs))]
        ride2 = _ChipExchangeRider(mlp_pair)

    (dmix,) = _mm(dx1b, w_out, "nt", tm=tr, tn=d, tk=d, out_dtypes=(BF16,), name="out_proj_bwd")
    (d_w_out,) = _mm(mix, dx1b, "tn", tm=d, tn=d, tk=tr, out_dtypes=(F32,), name="w_out_grad")
    do_dn, ddz, do_gla, dgr, d_dn_norm_g, d_gla_norm_g = _gnorm_bwd(
        dmix, o_dn, o_gla, projp, dn_norm_g, gla_norm_g, tr=tr, name="gated_norm_bwd")
    du, dw, dqg, dkd, dgl = _dn_scan_bwd(do_dn, w, qg, kd, vn, pmat, gl, hist, bsz=bsz, nc_seq=nc_seq,
                                          name="dn_scan_bwd")
    dqn, dkn, dv, dsa, d_alog, d_dtb, mlp_parts = _dn_intra_bwd(
        qn, kn, v, gates, alog_row, dtb_row, u, w, tmat, du, dw, dqg, dkd, do_dn, vn, dgl, nc_seq=nc_seq,
        name="dn_intra_bwd", rider=ride2)
    dz, d_conv_w = _dnprep_bwd_a(projp, conv_w, dqn, dkn, dv, bsz=bsz, t_seq=t_seq, tt=tt, name="dn_prep_bwd")
    dcin = _dnprep_bwd_b(dz, conv_w, bsz=bsz, t_seq=t_seq, tt=tt, name="conv_bwd")
    gdqg, gdkd, gdvi, gdgl = _gla_scan_bwd(do_gla, gqg, gkd, ggl, projp, ghist, bsz=bsz, nc_seq=nc_seq,
                                            name="gla_scan_bwd")
    dgqk, dgv, dsb, d_w2p, d_gla_b = _gla_intra_bwd(projp, gates, w2p, gla_b, do_gla, gdqg, gdkd, gdvi, gdgl,
                                                    nc_seq=nc_seq, name="gla_intra_bwd")

    secs = (dcin, ddz, dgqk, dgv, dgr, dsa, dsb)
    g_lo = _grad_tn(h, secs[0:2], tk=tr, name="w_in_grad_lo")
    g_hi = _grad_tn(h, secs[2:7], tk=tr, name="w_in_grad_hi")
    dh0, d_norm1_g = _inproj_bwd(secs, wp, h0, norm1_g, dx1, tr=tt, name="in_proj_bwd")
    dh0 = dh0.reshape(bsz, t_seq, d)
    grad_x = dh0[:, CHUNK:]
    d_meta_rows = dh0[:, N_PAD:CHUNK].reshape(bsz * N_META, d)

    grads = dict(w_in_lo=g_lo, w_in_hi=g_hi, w_out=d_w_out, w_up=d_w_up, w_down=d_w_down, meta_rows=d_meta_rows,
                 norm1_g=d_norm1_g, conv_w=d_conv_w, a_log_tile=d_alog, dt_bias_tile=d_dtb, dn_norm_g=d_dn_norm_g,
                 gla_w2=d_w2p[0:GLA_RANK], gla_b=d_gla_b, gla_norm_g=d_gla_norm_g, norm2_g=d_norm2_g,
                 final_norm_g=d_final_g, loss_tile=loss_tile)
    if where is not None:
        grads["mlp_exchanged"] = (mlp_pair, mlp_parts)
    return grad_x, grads


SHARD_W = IN_WIDTH // N_CHIPS
PADDED_ORDER = ((0, 2048), (2056, 3592), (2048, 2056), LANE - 8, (3592, 3608), LANE - GLA_RANK)


def _pad_layout(w_full):
    pieces = [jnp.zeros((w_full.shape[0], seg), w_full.dtype) if isinstance(seg, int) else w_full[:, seg[0]:seg[1]]
              for seg in PADDED_ORDER]
    return jnp.concatenate(pieces, axis=1)


def _padded_from_shards(stack):
    pieces = []
    for seg in PADDED_ORDER:
        if isinstance(seg, int):
            pieces.append(jnp.zeros((stack.shape[1], seg), stack.dtype))
            continue
        for j in range(N_CHIPS):
            lo, hi = max(seg[0], j * SHARD_W), min(seg[1], (j + 1) * SHARD_W)
            if lo < hi:
                pieces.append(stack[j, :, lo - j * SHARD_W:hi - j * SHARD_W])
    return jnp.concatenate(pieces, axis=1)


def _shards_from_padded(g_lo, g_hi):
    split = g_lo.shape[1]
    starts, pos = [], 0
    for seg in PADDED_ORDER:
        width = seg if isinstance(seg, int) else seg[1] - seg[0]
        if not isinstance(seg, int):
            starts.append((seg[0], seg[1], pos))
        pos += width
    shards = []
    for j in range(N_CHIPS):
        pieces = []
        for a, b, p0 in sorted(starts):
            lo, hi = max(a, j * SHARD_W), min(b, (j + 1) * SHARD_W)
            if lo < hi:
                src, off = (g_lo, 0) if p0 < split else (g_hi, split)
                pieces.append(src[:, p0 + lo - a - off:p0 + hi - a - off])
        pieces.append(jnp.zeros((g_lo.shape[0], D_MODEL - SHARD_W), g_lo.dtype))
        shards.append(jnp.concatenate(pieces, axis=1))
    return jnp.stack(shards)


def _pack_small(g, bsz):
    assert bsz * N_META == 32
    row = jnp.concatenate([g["a_log_tile"], g["dt_bias_tile"], g["dn_norm_g"], g["gla_norm_g"], g["gla_b"],
                           g["loss_tile"], jnp.zeros((1, LANE), F32)], axis=1)
    return jnp.concatenate([g["meta_rows"], g["norm1_g"], g["conv_w"].reshape(6, D_MODEL), row,
                            g["gla_w2"].reshape(4, D_MODEL), g["norm2_g"], g["final_norm_g"],
                            jnp.zeros((2, D_MODEL), F32)], axis=0)


def kernel(x, meta_tokens, norm1_g, w_in, conv_w, a_log, dt_bias, dn_norm_g, gla_w2, gla_b, gla_norm_g, w_out, norm2_g, w_up, w_down, final_norm_g, loss_target, m_meta_tokens, m_norm1_g, m_w_in, m_conv_w, m_a_log, m_dt_bias, m_dn_norm_g, m_gla_w2, m_gla_b, m_gla_norm_g, m_w_out, m_norm2_g, m_w_up, m_w_down, m_final_norm_g, v_meta_tokens, v_norm1_g, v_w_in, v_conv_w, v_a_log, v_dt_bias, v_dn_norm_g, v_gla_w2, v_gla_b, v_gla_norm_g, v_w_out, v_norm2_g, v_w_up, v_w_down, v_final_norm_g):
    bsz = x.shape[0]
    chip = 2 * lax.axis_index("x") + lax.axis_index("y")

    lane_pad = lambda a, wd: jnp.pad(a, ((0, 0), (0, wd - a.shape[1])))
    where = jnp.stack([lax.axis_index("c"), chip]).astype(jnp.int32)
    slot = lambda a, dt, nm: _to_slot(where, a, dt, name="slot_" + nm)
    early = _GatherRider([slot(lane_pad(w_in[0], D_MODEL), BF16, "w_in"), slot(meta_tokens, F32, "meta"),
                          slot(conv_w[0], F32, "conv"), slot(lane_pad(gla_w2[0], LANE), F32, "gla_w2")],
                         [True, False, False, False])
    g_in, g_meta, g_conv, g_w2 = _exchange_now(early, name="gather_early")
    late = (_GatherRider([slot(w_out[0], BF16, "w_out"), slot(w_up[0], BF16, "w_up")], [True, True]),
            _GatherRider([slot(w_down[0], BF16, "w_down")], [True]))
    wp = _padded_from_shards(g_in)
    meta_f = g_meta.transpose(1, 0, 2).reshape(N_META, D_MODEL)
    conv_f = g_conv.transpose(1, 0, 2).reshape(4, QKV_W)
    w2_f = g_w2[:, :, 0:GQ_W // N_CHIPS].transpose(1, 0, 2).reshape(GLA_RANK, GQ_W)

    grad_x, g = _local_step(x, loss_target, meta_f, norm1_g, wp, conv_f, a_log, dt_bias, dn_norm_g, w2_f, gla_b,
                            gla_norm_g, None, norm2_g, None, None, final_norm_g.reshape(1, D_MODEL), late_gather=late, where=where)

    shard_major = [_shards_from_padded(g["w_in_lo"], g["w_in_hi"]), g["w_out"].reshape(N_CHIPS, D_MODEL // N_CHIPS, D_MODEL)]
    theirs = _exchange_now(_SiblingHalvesRider(shard_major), name="sibling_halves")
    pair = [_pair_sum(where, a, b, name=f"pair_sum_{k}") for k, (a, b) in enumerate(zip(shard_major, theirs))]
    parts = _exchange_now(_ChipExchangeRider(pair), name="chip_exchange")
    mlp_pair, mlp_parts = g["mlp_exchanged"]
    halves = [_chip_sum(where, p, q, name=f"chip_sum_{k}")
              for k, (p, q) in enumerate(zip(pair + mlp_pair, list(parts) + list(mlp_parts)))]
    gw_in, gw_out, gw_up, gw_down = _sibling_join(halves)

    red = _small_allreduce(_pack_small(g, bsz))
    g_meta_full = red[0:N_META]
    g_norm1 = red[32:33]
    g_conv_full = red[33:39].reshape(4, QKV_W)
    srow = red[39:40]
    g_alog, g_dtb = srow[:, 4:8], srow[:, LANE + 4:LANE + 8]
    g_dn_norm, g_gla_norm = srow[:, 2 * LANE:3 * LANE], srow[:, 3 * LANE:4 * LANE]
    g_gla_b = srow[:, 4 * LANE:6 * LANE]
    loss = srow[0, 6 * LANE]
    g_w2_full = red[40:44].reshape(GLA_RANK, GQ_W)
    g_norm2 = red[44:45]
    g_final = red[45:46]
    g_meta_sh = lax.dynamic_slice_in_dim(g_meta_full, chip * (D_MODEL // N_CHIPS), D_MODEL // N_CHIPS, axis=1)
    g_conv_sh = lax.dynamic_slice_in_dim(g_conv_full, chip * (QKV_W // N_CHIPS), QKV_W // N_CHIPS, axis=1)
    g_w2_sh = lax.dynamic_slice_in_dim(g_w2_full, chip * (GQ_W // N_CHIPS), GQ_W // N_CHIPS, axis=1)

    names = ["meta_tokens", "norm1_g", "w_in", "conv_w", "a_log", "dt_bias", "dn_norm_g", "gla_w2", "gla_b",
             "gla_norm_g", "w_out", "norm2_g", "w_up", "w_down", "final_norm_g"]
    weights = dict(meta_tokens=meta_tokens, norm1_g=norm1_g, w_in=w_in, conv_w=conv_w, a_log=a_log, dt_bias=dt_bias,
                   dn_norm_g=dn_norm_g, gla_w2=gla_w2, gla_b=gla_b, gla_norm_g=gla_norm_g, w_out=w_out,
                   norm2_g=norm2_g, w_up=w_up, w_down=w_down, final_norm_g=final_norm_g)
    ms = dict(meta_tokens=m_meta_tokens, norm1_g=m_norm1_g, w_in=m_w_in, conv_w=m_conv_w, a_log=m_a_log,
              dt_bias=m_dt_bias, dn_norm_g=m_dn_norm_g, gla_w2=m_gla_w2, gla_b=m_gla_b, gla_norm_g=m_gla_norm_g,
              w_out=m_w_out, norm2_g=m_norm2_g, w_up=m_w_up, w_down=m_w_down, final_norm_g=m_final_norm_g)
    vs = dict(meta_tokens=v_meta_tokens, norm1_g=v_norm1_g, w_in=v_w_in, conv_w=v_conv_w, a_log=v_a_log,
              dt_bias=v_dt_bias, dn_norm_g=v_dn_norm_g, gla_w2=v_gla_w2, gla_b=v_gla_b, gla_norm_g=v_gla_norm_g,
              w_out=v_w_out, norm2_g=v_norm2_g, w_up=v_w_up, w_down=v_w_down, final_norm_g=v_final_norm_g)
    grads2d = dict(meta_tokens=g_meta_sh, norm1_g=g_norm1, w_in=gw_in, conv_w=g_conv_sh, a_log=g_alog, dt_bias=g_dtb,
                   dn_norm_g=g_dn_norm, gla_w2=g_w2_sh, gla_b=g_gla_b, gla_norm_g=g_gla_norm, w_out=gw_out,
                   norm2_g=g_norm2, w_up=gw_up, w_down=gw_down, final_norm_g=g_final)
    out_g, out_d, out_m, out_v = [], [], [], []
    for nm in names:
        shape = weights[nm].shape
        g2 = grads2d[nm]
        if len(shape) == 3:
            res = _adamw(weights[nm], g2, ms[nm], vs[nm], name=f"adamw_{nm}", emit_grad=nm == "w_in")
            gout = res[3] if nm == "w_in" else g2.reshape(shape)
        else:
            as2d = lambda a: a.reshape(g2.shape)
            res = _adamw(as2d(weights[nm]), g2, as2d(ms[nm]), as2d(vs[nm]), name=f"adamw_{nm}")
            gout = g2.reshape(shape)
        out_g.append(gout)
        out_d.append(res[0].reshape(shape))
        out_m.append(res[1].reshape(shape))
        out_v.append(res[2].reshape(shape))
    return (loss, grad_x, *out_g, *out_d, *out_m, *out_v)
```

```python
import functools

import jax
import jax.numpy as jnp
import numpy as np
from jax import lax
from jax.experimental import pallas as pl
from jax.experimental.pallas import tpu as pltpu

F32 = jnp.float32
BF16 = jnp.bfloat16
HI = lax.Precision.HIGHEST
MESH = pl.DeviceIdType.MESH

D_MODEL = 1024
N_META = 16
CHUNK = 64
N_PAD = CHUNK - N_META
NH = 4
DN_D = 128
GLA_DK = 64
GLA_DV = 128
GLA_RANK = 16
D_FF = 4 * D_MODEL
EPS = 1e-6
IN_WIDTH = 3608
C_QKV, C_DZ, C_GQK, C_GV, C_GR, C_SA, C_SB, PW = 0, 1536, 2048, 2560, 3072, 3584, 3712, 3840
LANE = 128
N_CHIPS = 4

ADAM_LR, ADAM_B1, ADAM_B2, ADAM_EPS, ADAM_WD, ADAM_STEP = 0.001, 0.9, 0.999, 1e-08, 0.01, 10

VMEM_BIG = 56 * 1024 * 1024


def _cp(vmem=None, sem=None):
    kw = {}
    if vmem is not None:
        kw["vmem_limit_bytes"] = vmem
    if sem is not None:
        kw["dimension_semantics"] = sem
    return pltpu.CompilerParams(**kw)


def _tile(n, target, mult=16):
    best = None
    for t in range(mult, min(n, target) + 1, mult):
        if n % t == 0:
            best = t
    assert best is not None, (n, target)
    return best


def _dot(a, b, dims, prec=None):
    return lax.dot_general(a, b, (dims, ((), ())), preferred_element_type=F32, precision=prec)


def _nn(a, b):
    return _dot(a.astype(BF16), b.astype(BF16), ((1,), (0,)))


def _nt(a, b):
    return _dot(a.astype(BF16), b.astype(BF16), ((1,), (1,)))


def _tn(a, b):
    return _dot(a.astype(BF16), b.astype(BF16), ((0,), (0,)))


def _tri_sum(tri, x):
    t = tri.astype(BF16)
    hi = x.astype(BF16)
    r1 = x - hi.astype(F32)
    mid = r1.astype(BF16)
    lo = (r1 - mid.astype(F32)).astype(BF16)
    nn = ((1,), (0,))
    return _dot(t, hi, nn) + _dot(t, mid, nn) + _dot(t, lo, nn)


def _sigmoid(x):
    return 0.5 * jnp.tanh(0.5 * x) + 0.5


def _softplus(x):
    return jnp.maximum(x, 0.0) + jnp.log(1.0 + jnp.exp(-jnp.abs(x)))


def _logsigmoid(x):
    return -_softplus(-x)


def _iota2(shape, dim):
    return lax.broadcasted_iota(jnp.int32, shape, dim)


def _mm(a, b, mode, *, tm, tn, tk, out_dtypes, extras=(), epilogue=None, name, vmem=VMEM_BIG, rider=None,
        out_widths=None):
    if mode == "tn":
        K, M = a.shape
    else:
        M, K = a.shape
    N = b.shape[0] if mode == "nt" else b.shape[1]
    assert M % tm == 0 and N % tn == 0 and K % tk == 0, (name, M, N, K, tm, tn, tk)
    nk = K // tk
    n_ex, n_out = len(extras), len(out_dtypes)
    if mode == "tn":
        a_spec = pl.BlockSpec((tk, tm), lambda i, j, k: (k, i))
    else:
        a_spec = pl.BlockSpec((tm, tk), lambda i, j, k: (i, k))
    if mode == "nt":
        b_spec = pl.BlockSpec((tn, tk), lambda i, j, k: (j, k))
    else:
        b_spec = pl.BlockSpec((tk, tn), lambda i, j, k: (k, j))
    mn_spec = pl.BlockSpec((tm, tn), lambda i, j, k: (i, j))
    if out_widths is None:
        o_specs = [mn_spec] * n_out
        o_shapes = [jax.ShapeDtypeStruct((M, N), dt) for dt in out_dtypes]
    else:
        assert tn == N
        o_specs = [pl.BlockSpec((tm, wd), lambda i, j, k: (i, 0)) for wd in out_widths]
        o_shapes = [jax.ShapeDtypeStruct((M, wd), dt) for wd, dt in zip(out_widths, out_dtypes)]
    dims = {"nn": ((1,), (0,)), "nt": ((1,), (1,)), "tn": ((0,), (0,))}[mode]

    single = nk == 1
    direct = (not single) and epilogue is None and n_out == 1 and out_dtypes[0] == F32

    def body(*refs):
        a_ref, b_ref = refs[0], refs[1]
        ex_refs = refs[2:2 + n_ex]
        out_refs = refs[2 + n_ex:2 + n_ex + n_out]
        part = _dot(a_ref[...].astype(BF16), b_ref[...].astype(BF16), dims)

        def finish(acc):
            res = (acc,) if epilogue is None else epilogue(acc, *[e[...] for e in ex_refs])
            for o_ref, r in zip(out_refs, res):
                o_ref[...] = r.astype(o_ref.dtype)

        if single:
            finish(part)
            return
        acc_ref = out_refs[0] if direct else refs[2 + n_ex + n_out]
        k = pl.program_id(2)

        @pl.when(k == 0)
        def _():
            acc_ref[...] = part

        @pl.when(k > 0)
        def _():
            acc_ref[...] += part

        if not direct:
            @pl.when(k == nk - 1)
            def _():
                finish(acc_ref[...])

    outs, ridden = _hosted_call(
        body, rider, name=name, grid=(M // tm, N // tn, nk),
        in_specs=[a_spec, b_spec] + [mn_spec] * n_ex,
        out_specs=o_specs, out_shape=o_shapes,
        scratch_shapes=[] if (single or direct) else [pltpu.VMEM((tm, tn), F32)],
        compiler_params=_cp(vmem, ("parallel", "parallel", "arbitrary")), args=(a, b, *extras))
    return tuple(outs) if rider is None else (tuple(outs), ridden)


def _grad_tn(a, secs, *, tk, name):
    kk, m = a.shape
    widths = [s.shape[1] for s in secs]
    total = sum(widths)
    nk = kk // tk

    def body(*refs):
        a_ref, sec_refs, o_ref = refs[0], refs[1:-1], refs[-1]
        cat = sec_refs[0][...] if len(sec_refs) == 1 else jnp.concatenate([s[...] for s in sec_refs], axis=1)
        part = _dot(a_ref[...].astype(BF16), cat.astype(BF16), ((0,), (0,)))
        k = pl.program_id(0)

        @pl.when(k == 0)
        def _():
            o_ref[...] = part

        @pl.when(k > 0)
        def _():
            o_ref[...] += part

    return pl.pallas_call(
        body, name=name, grid=(nk,),
        in_specs=[pl.BlockSpec((tk, m), lambda k: (k, 0))] + [pl.BlockSpec((tk, w), lambda k: (k, 0)) for w in widths],
        out_specs=pl.BlockSpec((m, total), lambda k: (0, 0)),
        out_shape=jax.ShapeDtypeStruct((m, total), F32),
        compiler_params=_cp(VMEM_BIG, ("arbitrary",)),
    )(a, *secs)


def _rms_fwd(x, g, *, tr, name):
    n, d = x.shape

    def body(x_ref, g_ref, o_ref):
        xv = x_ref[...]
        r = lax.rsqrt(jnp.mean(xv * xv, axis=-1, keepdims=True) + EPS)
        o_ref[...] = (xv * r * g_ref[...]).astype(o_ref.dtype)

    return pl.pallas_call(
        body, name=name, grid=(n // tr,),
        in_specs=[pl.BlockSpec((tr, d), lambda i: (i, 0)), pl.BlockSpec((1, d), lambda i: (0, 0))],
        out_specs=pl.BlockSpec((tr, d), lambda i: (i, 0)),
        out_shape=jax.ShapeDtypeStruct((n, d), BF16),
        compiler_params=_cp(VMEM_BIG),
    )(x, g)


def _rms_bwd_math(xv, g, dy):
    r = lax.rsqrt(jnp.mean(xv * xv, axis=-1, keepdims=True) + EPS)
    xh = xv * r
    gdy = dy * g
    dx = r * (gdy - xh * jnp.mean(xh * gdy, axis=-1, keepdims=True))
    return dx, jnp.sum(dy * xh, axis=0, keepdims=True)


def _mlp_up_bwd_norm(dup, w_up, x, g, res, *, tr, name, rider=None):
    n, d = x.shape
    ff = dup.shape[1]

    def body(dup_ref, w_ref, x_ref, g_ref, res_ref, o_ref, ob_ref, dg_ref):
        dh = _nt(dup_ref[...], w_ref[...])
        dx, dg = _rms_bwd_math(x_ref[...], g_ref[...], dh)
        tot = res_ref[...] + dx
        o_ref[...] = tot
        ob_ref[...] = tot.astype(BF16)

        @pl.when(pl.program_id(0) == 0)
        def _():
            dg_ref[...] = dg

        @pl.when(pl.program_id(0) > 0)
        def _():
            dg_ref[...] += dg

    row = pl.BlockSpec((tr, d), lambda i: (i, 0))
    vec = pl.BlockSpec((1, d), lambda i: (0, 0))
    outs, ridden = _hosted_call(
        body, rider, name=name, grid=(n // tr,),
        in_specs=[pl.BlockSpec((tr, ff), lambda i: (i, 0)), pl.BlockSpec((d, ff), lambda i: (0, 0)), row, vec, row],
        out_specs=[row, row, vec],
        out_shape=[jax.ShapeDtypeStruct((n, d), F32), jax.ShapeDtypeStruct((n, d), BF16),
                   jax.ShapeDtypeStruct((1, d), F32)],
        scratch_shapes=[], compiler_params=_cp(VMEM_BIG, ("arbitrary",)), args=(dup, w_up, x, g, res))
    return (*outs, ridden)


def _mlp_down_loss(act, w_down, x1, gf, tgt, *, t_seq, tr, name):
    n, d = x1.shape
    ff = act.shape[1]
    per_seq = t_seq // tr

    def body(a_ref, w_ref, x_ref, g_ref, t_ref, dx_ref, dxb_ref, dg_ref, loss_ref):
        i = pl.program_id(0)
        xv = x_ref[...] + _nn(a_ref[...], w_ref[...])
        g = g_ref[...]
        r = lax.rsqrt(jnp.mean(xv * xv, axis=-1, keepdims=True) + EPS)
        xh = xv * r
        pos = (i % per_seq) * tr + _iota2((tr, 1), 0)
        real = pos >= CHUNK
        err = jnp.where(real, xh * g - t_ref[...], 0.0)
        dy = err * (1.0 / d)
        gdy = dy * g
        dx = r * (gdy - xh * jnp.mean(xh * gdy, axis=-1, keepdims=True))
        dx_ref[...] = dx
        dxb_ref[...] = dx.astype(BF16)
        dg = jnp.sum(dy * xh, axis=0, keepdims=True)
        ls = 0.5 * jnp.sum(jnp.mean(err * err, axis=-1, keepdims=True), axis=0, keepdims=True)
        ls = jnp.where(_iota2((1, LANE), 1) == 0, ls, 0.0)

        @pl.when(i == 0)
        def _():
            dg_ref[...] = dg
            loss_ref[...] = ls

        @pl.when(i > 0)
        def _():
            dg_ref[...] += dg
            loss_ref[...] += ls

    row = pl.BlockSpec((tr, d), lambda i: (i, 0))
    vec = pl.BlockSpec((1, d), lambda i: (0, 0))
    one = pl.BlockSpec((1, LANE), lambda i: (0, 0))
    return pl.pallas_call(
        body, name=name, grid=(n // tr,),
        in_specs=[pl.BlockSpec((tr, ff), lambda i: (i, 0)), pl.BlockSpec((ff, d), lambda i: (0, 0)), row, vec, row],
        out_specs=[row, row, vec, one],
        out_shape=[jax.ShapeDtypeStruct((n, d), F32), jax.ShapeDtypeStruct((n, d), BF16),
                   jax.ShapeDtypeStruct((1, d), F32), jax.ShapeDtypeStruct((1, LANE), F32)],
        compiler_params=_cp(VMEM_BIG, ("arbitrary",)),
    )(act, w_down, x1, gf, tgt)


def _gnorm_fwd(o_dn, o_gla, projp, g_dn, g_gla, *, tr, name):
    n = o_dn.shape[0]
    w = NH * DN_D

    def body(odn_ref, ogl_ref, z_ref, r_ref, gdn_ref, ggl_ref, mix_ref):
        for grp, (o_ref, gate_ref, gain_ref) in enumerate(((odn_ref, z_ref, gdn_ref), (ogl_ref, r_ref, ggl_ref))):
            gain = gain_ref[...]
            for h in range(NH):
                sl = slice(h * DN_D, (h + 1) * DN_D)
                o = o_ref[:, sl].astype(F32)
                z = gate_ref[:, sl].astype(F32)
                r = lax.rsqrt(jnp.mean(o * o, axis=-1, keepdims=True) + EPS)
                y = (o * r * gain) * (z * _sigmoid(z))
                mix_ref[:, grp * w + h * DN_D: grp * w + (h + 1) * DN_D] = y.astype(mix_ref.dtype)

    row = pl.BlockSpec((tr, w), lambda i: (i, 0))
    vec = pl.BlockSpec((1, DN_D), lambda i: (0, 0))
    return pl.pallas_call(
        body, name=name, grid=(n // tr,),
        in_specs=[row, row, pl.BlockSpec((tr, w), lambda i: (i, C_DZ // w)),
                  pl.BlockSpec((tr, w), lambda i: (i, C_GR // w)), vec, vec],
        out_specs=pl.BlockSpec((tr, 2 * w), lambda i: (i, 0)),
        out_shape=jax.ShapeDtypeStruct((n, 2 * w), BF16),
        compiler_params=_cp(VMEM_BIG),
    )(o_dn, o_gla, projp, projp, g_dn, g_gla)


def _gnorm_bwd(dmix, o_dn, o_gla, projp, g_dn, g_gla, *, tr, name):
    n = o_dn.shape[0]
    w = NH * DN_D

    def body(dm_ref, odn_ref, ogl_ref, z_ref, r_ref, gdn_ref, ggl_ref,
             dodn_ref, ddz_ref, dogl_ref, dgr_ref, dgdn_ref, dggl_ref):
        first = pl.program_id(0) == 0
        groups = ((odn_ref, z_ref, gdn_ref, dodn_ref, ddz_ref, dgdn_ref),
                  (ogl_ref, r_ref, ggl_ref, dogl_ref, dgr_ref, dggl_ref))
        for grp, (o_ref, gate_ref, gain_ref, do_ref, dgate_ref, dgain_ref) in enumerate(groups):
            gain = gain_ref[...]
            dgain = jnp.zeros((1, DN_D), F32)
            for h in range(NH):
                sl = slice(h * DN_D, (h + 1) * DN_D)
                o = o_ref[:, sl].astype(F32)
                z = gate_ref[:, sl].astype(F32)
                dm = dm_ref[:, grp * w + h * DN_D: grp * w + (h + 1) * DN_D].astype(F32)
                r = lax.rsqrt(jnp.mean(o * o, axis=-1, keepdims=True) + EPS)
                oh = o * r
                s = _sigmoid(z)
                dn = dm * (z * s)
                dgate_ref[:, sl] = (dm * (oh * gain) * (s * (1.0 + z * (1.0 - s)))).astype(dgate_ref.dtype)
                gdn = dn * gain
                do_ref[:, sl] = (r * (gdn - oh * jnp.mean(oh * gdn, axis=-1, keepdims=True))).astype(do_ref.dtype)
                dgain = dgain + jnp.sum(dn * oh, axis=0, keepdims=True)

            @pl.when(first)
            def _():
                dgain_ref[...] = dgain

            @pl.when(jnp.logical_not(first))
            def _():
                dgain_ref[...] += dgain

    row = pl.BlockSpec((tr, w), lambda i: (i, 0))
    vec = pl.BlockSpec((1, DN_D), lambda i: (0, 0))
    big = jax.ShapeDtypeStruct((n, w), F32)
    gate = jax.ShapeDtypeStruct((n, w), BF16)
    small = jax.ShapeDtypeStruct((1, DN_D), F32)
    return pl.pallas_call(
        body, name=name, grid=(n // tr,),
        in_specs=[pl.BlockSpec((tr, 2 * w), lambda i: (i, 0)), row, row,
                  pl.BlockSpec((tr, w), lambda i: (i, C_DZ // w)), pl.BlockSpec((tr, w), lambda i: (i, C_GR // w)), vec, vec],
        out_specs=[row, row, row, row, vec, vec],
        out_shape=[gate, gate, gate, gate, small, small],
        compiler_params=_cp(VMEM_BIG),
    )(dmix, o_dn, o_gla, projp, projp, g_dn, g_gla)


QKV_W = 3 * NH * DN_D
HALO = 8


def _conv_z(xs_ref, cw_ref, tt):
    z = cw_ref[0:1, :] * xs_ref[pl.ds(HALO - 3, tt), :]
    for j in range(1, 4):
        z = z + cw_ref[j:j + 1, :] * xs_ref[pl.ds(HALO - 3 + j, tt), :]
    return z


def _dnprep_fwd(projp, conv_w, *, bsz, t_seq, tt, name):
    n = bsz * t_seq
    per_seq = t_seq // tt
    hw = NH * DN_D

    def body(x_ref, halo_ref, cw_ref, q_ref, k_ref, v_ref, xs_ref):
        i = pl.program_id(1)
        xs_ref[0:HALO, :] = jnp.where(i == 0, 0.0, halo_ref[...].astype(F32))
        xs_ref[HALO:HALO + tt, :] = x_ref[...].astype(F32)
        z = _conv_z(xs_ref, cw_ref, tt)
        a = z * _sigmoid(z)
        for grp, o_ref in enumerate((q_ref, k_ref)):
            for h in range(NH):
                ah = a[:, grp * hw + h * DN_D: grp * hw + (h + 1) * DN_D]
                rs = lax.rsqrt(jnp.sum(ah * ah, axis=-1, keepdims=True) + EPS)
                o_ref[:, h * DN_D:(h + 1) * DN_D] = (ah * rs).astype(o_ref.dtype)
        v_ref[...] = a[:, 2 * hw:3 * hw].astype(v_ref.dtype)

    def halo_map(b, i):
        return (jnp.maximum((b * t_seq + i * tt) // HALO - 1, 0), 0)

    out = pl.BlockSpec((tt, hw), lambda b, i: (b * per_seq + i, 0))
    sds = jax.ShapeDtypeStruct((n, hw), BF16)
    return pl.pallas_call(
        body, name=name, grid=(bsz, per_seq),
        in_specs=[pl.BlockSpec((tt, QKV_W), lambda b, i: (b * per_seq + i, 0)),
                  pl.BlockSpec((HALO, QKV_W), halo_map),
                  pl.BlockSpec((4, QKV_W), lambda b, i: (0, 0))],
        out_specs=[out, out, out], out_shape=[sds, sds, sds],
        scratch_shapes=[pltpu.VMEM((tt + HALO, QKV_W), F32)],
        compiler_params=_cp(VMEM_BIG),
    )(projp, projp, conv_w)


def _dnprep_bwd_a(projp, conv_w, dq, dk, dv, *, bsz, t_seq, tt, name):
    n = bsz * t_seq
    per_seq = t_seq // tt
    hw = NH * DN_D

    def body(x_ref, halo_ref, cw_ref, dq_ref, dk_ref, dv_ref, dz_ref, dcw_ref, xs_ref):
        b, i = pl.program_id(0), pl.program_id(1)
        xs_ref[0:HALO, :] = jnp.where(i == 0, 0.0, halo_ref[...].astype(F32))
        xs_ref[HALO:HALO + tt, :] = x_ref[...].astype(F32)
        z = _conv_z(xs_ref, cw_ref, tt)
        s = _sigmoid(z)
        a = z * s
        dsilu = s * (1.0 + z * (1.0 - s))
        for grp, d_ref in enumerate((dq_ref, dk_ref)):
            for h in range(NH):
                sl = slice(grp * hw + h * DN_D, grp * hw + (h + 1) * DN_D)
                ah = a[:, sl]
                rs = lax.rsqrt(jnp.sum(ah * ah, axis=-1, keepdims=True) + EPS)
                y = ah * rs
                dy = d_ref[:, h * DN_D:(h + 1) * DN_D]
                da = rs * (dy - y * jnp.sum(dy * y, axis=-1, keepdims=True))
                dz_ref[:, sl] = da * dsilu[:, sl]
        dz_ref[:, 2 * hw:3 * hw] = dv_ref[...] * dsilu[:, 2 * hw:3 * hw]
        dz = dz_ref[...]
        first = jnp.logical_and(b == 0, i == 0)
        for j in range(4):
            part = jnp.sum(dz * xs_ref[pl.ds(HALO - 3 + j, tt), :], axis=0, keepdims=True)

            @pl.when(first)
            def _():
                dcw_ref[j:j + 1, :] = part

            @pl.when(jnp.logical_not(first))
            def _():
                dcw_ref[j:j + 1, :] += part

    def halo_map(b, i):
        return (jnp.maximum((b * t_seq + i * tt) // HALO - 1, 0), 0)

    hrow = pl.BlockSpec((tt, hw), lambda b, i: (b * per_seq + i, 0))
    return pl.pallas_call(
        body, name=name, grid=(bsz, per_seq),
        in_specs=[pl.BlockSpec((tt, QKV_W), lambda b, i: (b * per_seq + i, 0)),
                  pl.BlockSpec((HALO, QKV_W), halo_map),
                  pl.BlockSpec((4, QKV_W), lambda b, i: (0, 0)), hrow, hrow, hrow],
        out_specs=[pl.BlockSpec((tt, QKV_W), lambda b, i: (b * per_seq + i, 0)),
                   pl.BlockSpec((4, QKV_W), lambda b, i: (0, 0))],
        out_shape=[jax.ShapeDtypeStruct((n, QKV_W), F32), jax.ShapeDtypeStruct((4, QKV_W), F32)],
        scratch_shapes=[pltpu.VMEM((tt + HALO, QKV_W), F32)],
        compiler_params=_cp(VMEM_BIG),
    )(projp, projp, conv_w, dq, dk, dv)


def _dnprep_bwd_b(dz, conv_w, *, bsz, t_seq, tt, name):
    n = bsz * t_seq
    per_seq = t_seq // tt
    last_blk = n // HALO - 1

    def body(dz_ref, halo_ref, cw_ref, dx_ref, ds_ref):
        i = pl.program_id(1)
        ds_ref[0:tt, :] = dz_ref[...].astype(F32)
        ds_ref[tt:tt + HALO, :] = jnp.where(i == per_seq - 1, 0.0, halo_ref[...].astype(F32))
        dx = cw_ref[0:1, :] * ds_ref[pl.ds(3, tt), :]
        for j in range(1, 4):
            dx = dx + cw_ref[j:j + 1, :] * ds_ref[pl.ds(3 - j, tt), :]
        dx_ref[...] = dx.astype(dx_ref.dtype)

    def halo_map(b, i):
        return (jnp.minimum((b * t_seq + (i + 1) * tt) // HALO, last_blk), 0)

    row = pl.BlockSpec((tt, QKV_W), lambda b, i: (b * per_seq + i, 0))
    return pl.pallas_call(
        body, name=name, grid=(bsz, per_seq),
        in_specs=[row, pl.BlockSpec((HALO, QKV_W), halo_map), pl.BlockSpec((4, QKV_W), lambda b, i: (0, 0))],
        out_specs=row, out_shape=jax.ShapeDtypeStruct((n, QKV_W), BF16),
        scratch_shapes=[pltpu.VMEM((tt + HALO, QKV_W), F32)],
        compiler_params=_cp(VMEM_BIG),
    )(dz, dz, conv_w)


def _masks64():
    r = _iota2((CHUNK, CHUNK), 0)
    c = _iota2((CHUNK, CHUNK), 1)
    return r, c


def _group(nc_seq, target=5):
    return max(g for g in range(1, target + 1) if nc_seq % g == 0)


def _round_robin(chains):
    live = list(chains)
    while live:
        nxt = []
        for ch in live:
            try:
                next(ch)
                nxt.append(ch)
            except StopIteration:
                pass
        live = nxt
        yield


def _run(chains):
    for _ in _round_robin(chains):
        pass


def _per_chunk(inner, kinds, grp):
    def body(*refs):
        chains = []
        for gi in range(grp):
            views = []
            for r, kind in zip(refs, kinds):
                if kind == "row":
                    views.append(r.at[pl.ds(gi * CHUNK, CHUNK)])
                elif kind == "lead":
                    views.append(r.at[pl.ds(gi, 1)])
                else:
                    views.append(r)
            chains.append(inner(gi, *views))
        _run(chains)
    return body


def _accumulate(ref, val, gi):
    if gi > 0:
        ref[...] += val
        return
    first = pl.program_id(0) == 0

    @pl.when(first)
    def _():
        ref[...] = val

    @pl.when(jnp.logical_not(first))
    def _():
        ref[...] += val


ANY = pl.BlockSpec(memory_space=pl.ANY)


def _place():
    return lax.axis_index("x"), lax.axis_index("y"), lax.axis_index("c")


def _other_chips(x, y):
    return [(1 - x, y, 2 * (1 - x) + y), (x, 1 - y, 2 * x + 1 - y), (1 - x, 1 - y, 2 * (1 - x) + 1 - y)]


class _GatherRider:
    def __init__(self, bufs, split):
        self.inputs = list(bufs)
        self.split = list(split)
        self.out_shapes = [jax.ShapeDtypeStruct(b.shape, b.dtype) for b in bufs]
        self.aliases = {i: i for i in range(len(bufs))}
        self.sems = [pltpu.SemaphoreType.DMA((len(bufs), 3))] * 4

    def _rows(self, k, buf, c, mine=True):
        r = buf.shape[1]
        if not self.split[k]:
            return pl.ds(0, r)
        return pl.ds((c if mine else 1 - c) * (r // 2), r // 2)

    def _ici(self, k, d, bufs, sems, c, px, py, block):
        rows = self._rows(k, bufs[k], c)
        return pltpu.make_async_remote_copy(
            src_ref=bufs[k].at[block, rows, :], dst_ref=bufs[k].at[block, rows, :], send_sem=sems[0].at[k, d],
            recv_sem=sems[1].at[k, d], device_id=(px, py, c), device_id_type=MESH)

    def _pass(self, k, d, bufs, sems, x, y, c, block, mine):
        rows = self._rows(k, bufs[k], c, mine)
        return pltpu.make_async_remote_copy(
            src_ref=bufs[k].at[block, rows, :], dst_ref=bufs[k].at[block, rows, :], send_sem=sems[2].at[k, d],
            recv_sem=sems[3].at[k, d], device_id=(x, y, 1 - c), device_id_type=MESH)

    def first(self, in_refs, bufs, sems):
        x, y, c = _place()
        for k in range(len(bufs)):
            for d, (px, py, _) in enumerate(_other_chips(x, y)):
                self._ici(k, d, bufs, sems, c, px, py, 2 * x + y).start()

    def last(self, in_refs, bufs, sems):
        x, y, c = _place()
        chips = _other_chips(x, y)
        for k in range(len(bufs)):
            for d, (px, py, pj) in enumerate(chips):
                self._ici(k, d, bufs, sems, c, px, py, pj).wait_recv()
                if self.split[k]:
                    self._pass(k, d, bufs, sems, x, y, c, pj, True).start()
        for k in range(len(bufs)):
            for d, (px, py, pj) in enumerate(chips):
                if self.split[k]:
                    self._pass(k, d, bufs, sems, x, y, c, pj, False).wait_recv()
                    self._pass(k, d, bufs, sems, x, y, c, pj, True).wait_send()
                self._ici(k, d, bufs, sems, c, px, py, 2 * x + y).wait_send()


def _hosted_call(body, rider, *, name, grid, in_specs, out_specs, out_shape, scratch_shapes, compiler_params, args):
    if rider is None:
        outs = pl.pallas_call(body, name=name, grid=grid, in_specs=in_specs, out_specs=out_specs, out_shape=out_shape,
                              scratch_shapes=scratch_shapes, compiler_params=compiler_params)(*args)
        return list(outs), []
    n_in, n_out, n_scr = len(in_specs), len(out_specs), len(scratch_shapes)
    r_in, r_out = len(rider.inputs), len(rider.out_shapes)
    compiler_params = _cp(compiler_params.vmem_limit_bytes, ("arbitrary",) * len(grid))

    def full_body(*refs):
        ins = refs[:n_in]
        rins = refs[n_in:n_in + r_in]
        outs = refs[n_in + r_in:n_in + r_in + n_out]
        routs = refs[n_in + r_in + n_out:n_in + r_in + n_out + r_out]
        rest = refs[n_in + r_in + n_out + r_out:]
        scr, sems = rest[:n_scr], rest[n_scr:]
        ids = [pl.program_id(a) for a in range(len(grid))]
        is_first = functools.reduce(jnp.logical_and, [i == 0 for i in ids])
        is_last = functools.reduce(jnp.logical_and, [i == g - 1 for i, g in zip(ids, grid)])

        @pl.when(is_first)
        def _():
            rider.first(rins, routs, sems)

        body(*ins, *outs, *scr)

        @pl.when(is_last)
        def _():
            rider.last(rins, routs, sems)

    res = pl.pallas_call(
        full_body, name=name, grid=grid, in_specs=list(in_specs) + [ANY] * r_in,
        out_specs=list(out_specs) + [ANY] * r_out, out_shape=list(out_shape) + list(rider.out_shapes),
        input_output_aliases={n_in + i: n_out + o for i, o in rider.aliases.items()},
        scratch_shapes=list(scratch_shapes) + list(rider.sems), compiler_params=compiler_params,
    )(*args, *rider.inputs)
    return list(res[:n_out]), list(res[n_out:])


def _exchange_now(rider, *, name):
    r_in = len(rider.inputs)

    def body(*refs):
        rins = refs[:r_in]
        routs = refs[r_in:r_in + len(rider.out_shapes)]
        sems = refs[r_in + len(rider.out_shapes):]
        rider.first(rins, routs, sems)
        rider.last(rins, routs, sems)

    return pl.pallas_call(
        body, name=name, in_specs=[ANY] * r_in, out_specs=[ANY] * len(rider.out_shapes), out_shape=list(rider.out_shapes),
        input_output_aliases=dict(rider.aliases), scratch_shapes=list(rider.sems),
    )(*rider.inputs)


def _to_slot(where, a, dtype, *, name):
    r, cols = a.shape
    tr = _tile(r, 256, 16) if r > 256 else r

    def body(w_ref, a_ref, o_ref):
        o_ref[0] = a_ref[...].astype(o_ref.dtype)

    return pl.pallas_call(
        body, name=name,
        grid_spec=pltpu.PrefetchScalarGridSpec(
            num_scalar_prefetch=1, grid=(r // tr,),
            in_specs=[pl.BlockSpec((tr, cols), lambda i, w: (i, 0))],
            out_specs=pl.BlockSpec((1, tr, cols), lambda i, w: (w[1], i, 0))),
        out_shape=jax.ShapeDtypeStruct((N_CHIPS, r, cols), dtype), compiler_params=_cp(VMEM_BIG),
    )(where, a)


def _tri_inv(a_strict):
    r, c = _masks64()
    eye = (r == c).astype(F32)
    blk16 = (r // 16) == (c // 16)
    blk32 = (r // 32) == (c // 32)
    ld = jnp.where(blk16, a_strict, 0.0)
    x = eye - ld
    p = _nn(ld, ld)
    yield
    for step in range(3):
        xp = _nn(x, p)
        if step < 2:
            p = _nn(p, p)
        x = x + xp
        yield
    for lk in (jnp.where(jnp.logical_and(blk32, jnp.logical_not(blk16)), a_strict, 0.0),
               jnp.where(blk32, 0.0, a_strict)):
        y = x - eye
        s = lk + _nn(y, lk)
        yield
        x = x - s - _nn(s, y)
        yield
    return x


def _dn_gates(sa, alog, dtb, chunk_in_seq):
    rows = _iota2((CHUNK, LANE), 0)
    valid = jnp.logical_or(rows >= N_PAD, chunk_in_seq > 0)
    beta_t = _sigmoid(sa)
    ea = jnp.exp(alog)
    g_t = jnp.where(valid, -ea * _softplus(sa + dtb), 0.0)
    r, c = _masks64()
    ltri = (r >= c).astype(F32)
    gam_t = _tri_sum(ltri, g_t)
    return beta_t, g_t, gam_t, valid, ea


def _dn_intra_fwd(qn, kn, v, projp, alog_row, dtb_row, *, nc_seq, name):
    n = qn.shape[0]
    nct = n // CHUNK
    hw = NH * DN_D
    scale = DN_D ** -0.5

    grp = _group(nc_seq)

    def inner(gi, q_ref, k_ref, v_ref, sa_ref, al_ref, dt_ref, u_ref, w_ref, qg_ref, kd_ref, p_ref, t_ref, gl_ref):
        ci = (pl.program_id(0) * grp + gi) % nc_seq
        beta_t, _, gam_t, _, _ = _dn_gates(sa_ref[...], al_ref[...], dt_ref[...], ci)
        yield
        gam_tt = gam_t.T
        r, c = _masks64()
        incl = r >= c
        strict = r > c

        def head(h):
            sl = slice(h * DN_D, (h + 1) * DN_D)
            beta_w = jnp.broadcast_to(beta_t[:, h:h + 1], (CHUNK, DN_D))
            gam_w = jnp.broadcast_to(gam_t[:, 4 + h:5 + h], (CHUNK, DN_D))
            gam_row = gam_tt[4 + h:5 + h, :]
            gl = gam_t[CHUNK - 1:CHUNK, 4 + h:5 + h]
            dec = jnp.exp(jnp.where(incl, gam_w[:, 0:CHUNK] - gam_row, -jnp.inf))
            kh = k_ref[:, sl].astype(F32)
            qh = q_ref[:, sl].astype(F32) * scale
            vh = v_ref[:, sl].astype(F32)
            kk = _nt(kh, kh)
            qk = _nt(qh, kh)
            yield
            a = jnp.where(strict, beta_w[:, 0:CHUNK] * kk * dec, 0.0)
            tm = yield from _tri_inv(a)
            egam_w = jnp.exp(gam_w)
            u_ref[:, sl] = _nn(tm, beta_w * vh).astype(u_ref.dtype)
            w_ref[:, sl] = _nn(tm, (beta_w * egam_w) * kh).astype(w_ref.dtype)
            qg_ref[:, sl] = (egam_w * qh).astype(qg_ref.dtype)
            kd_ref[:, sl] = (jnp.exp(gl - gam_w) * kh).astype(kd_ref.dtype)
            p_ref[0, h] = qk * dec
            t_ref[0, h] = tm
            gl_ref[0, h:h + 1, :] = jnp.broadcast_to(jnp.exp(gl), (1, LANE))

        yield from _round_robin([head(h) for h in range(NH)])

    rows = grp * CHUNK
    row = pl.BlockSpec((rows, hw), lambda i: (i, 0))
    vec = pl.BlockSpec((1, LANE), lambda i: (0, 0))
    mat = pl.BlockSpec((grp, NH, CHUNK, CHUNK), lambda i: (i, 0, 0, 0))
    big = jax.ShapeDtypeStruct((n, hw), BF16)
    msd = jax.ShapeDtypeStruct((nct, NH, CHUNK, CHUNK), F32)
    kinds = ["row"] * 4 + ["whole"] * 2 + ["row"] * 4 + ["lead"] * 3
    return pl.pallas_call(
        _per_chunk(inner, kinds, grp), name=name, grid=(nct // grp,),
        in_specs=[row, row, row, pl.BlockSpec((rows, LANE), lambda i: (i, 0)), vec, vec],
        out_specs=[row, row, row, row, mat, mat, pl.BlockSpec((grp, NH, LANE), lambda i: (i, 0, 0))],
        out_shape=[big, big, big, big, msd, msd, jax.ShapeDtypeStruct((nct, NH, LANE), F32)],
        compiler_params=_cp(VMEM_BIG),
    )(qn, kn, v, projp, alog_row, dtb_row)


def _dn_scan_fwd(u, w, qg, kd, p, gl, *, bsz, nc_seq, name, rider=None):
    hw = NH * DN_D
    t_seq = nc_seq * CHUNK
    u, w, qg, kd = (z.reshape(bsz, t_seq, hw) for z in (u, w, qg, kd))
    p = p.reshape(bsz, nc_seq, NH, CHUNK, CHUNK)
    gl = gl.reshape(bsz, nc_seq, NH, LANE)
    grp = _group(nc_seq)

    def body(u_ref, w_ref, qg_ref, kd_ref, p_ref, gl_ref, o_ref, vn_ref, hist_ref, s_ref):
        @pl.when(pl.program_id(0) == 0)
        def _():
            s_ref[...] = jnp.zeros_like(s_ref)

        def chain(b, h, gi):
            sl = slice(h * DN_D, (h + 1) * DN_D)
            rows = pl.ds(gi * CHUNK, CHUNK)
            s = s_ref[b, h]
            hist_ref[b, gi, h] = s.astype(hist_ref.dtype)
            ws = _nn(w_ref[b, rows, sl], s)
            qs = _nn(qg_ref[b, rows, sl], s)
            yield
            vn = u_ref[b, rows, sl] - ws
            vn_ref[b, rows, sl] = vn.astype(vn_ref.dtype)
            o_ref[b, rows, sl] = (qs + _nn(p_ref[b, gi, h], vn)).astype(o_ref.dtype)
            s_ref[b, h] = gl_ref[b, gi, h:h + 1, :] * s + _tn(kd_ref[b, rows, sl], vn)

        for gi in range(grp):
            _run([chain(b, h, gi) for b in range(bsz) for h in range(NH)])

    row = pl.BlockSpec((bsz, grp * CHUNK, hw), lambda i: (0, i, 0))
    outs, ridden = _hosted_call(
        body, rider, name=name, grid=(nc_seq // grp,),
        in_specs=[row, row, row, row, pl.BlockSpec((bsz, grp, NH, CHUNK, CHUNK), lambda i: (0, i, 0, 0, 0)),
                  pl.BlockSpec((bsz, grp, NH, LANE), lambda i: (0, i, 0, 0))],
        out_specs=[row, row, pl.BlockSpec((bsz, grp, NH, DN_D, DN_D), lambda i: (0, i, 0, 0, 0))],
        out_shape=[jax.ShapeDtypeStruct((bsz, t_seq, hw), BF16), jax.ShapeDtypeStruct((bsz, t_seq, hw), BF16),
                   jax.ShapeDtypeStruct((bsz, nc_seq, NH, DN_D, DN_D), BF16)],
        scratch_shapes=[pltpu.VMEM((bsz, NH, DN_D, DN_D), F32)],
        compiler_params=_cp(VMEM_BIG, ("arbitrary",)), args=(u, w, qg, kd, p, gl))
    o, vn, hist = outs
    return o.reshape(bsz * t_seq, hw), vn.reshape(bsz * t_seq, hw), hist, ridden


def _dn_scan_bwd(do, w, qg, kd, vn, p, gl, hist, *, bsz, nc_seq, name):
    hw = NH * DN_D
    t_seq = nc_seq * CHUNK
    do, w, qg, kd, vn = (z.reshape(bsz, t_seq, hw) for z in (do, w, qg, kd, vn))
    p = p.reshape(bsz, nc_seq, NH, CHUNK, CHUNK)
    gl = gl.reshape(bsz, nc_seq, NH, LANE)
    grp = _group(nc_seq)

    def body(do_ref, w_ref, qg_ref, kd_ref, vn_ref, p_ref, gl_ref, hist_ref,
             du_ref, dw_ref, dqg_ref, dkd_ref, dgl_ref, ds_ref):
        @pl.when(pl.program_id(0) == 0)
        def _():
            ds_ref[...] = jnp.zeros_like(ds_ref)

        def chain(b, h, gi):
            sl = slice(h * DN_D, (h + 1) * DN_D)
            rows = pl.ds(gi * CHUNK, CHUNK)
            s = hist_ref[b, gi, h]
            dsn = ds_ref[b, h]
            doh = do_ref[b, rows, sl]
            vnh = vn_ref[b, rows, sl]
            kdh = kd_ref[b, rows, sl]
            dvn = _tn(p_ref[b, gi, h], doh) + _nn(kdh, dsn)
            du_ref[b, rows, sl] = dvn.astype(du_ref.dtype)
            dqg_ref[b, rows, sl] = _nt(doh, s).astype(dqg_ref.dtype)
            dkd_ref[b, rows, sl] = _nt(vnh, dsn).astype(dkd_ref.dtype)
            ds_part = _tn(qg_ref[b, rows, sl], doh) + gl_ref[b, gi, h:h + 1, :] * dsn
            dgl = jnp.sum(jnp.sum(dsn * s, axis=0, keepdims=True), axis=1, keepdims=True)
            dgl_ref[b, gi, h:h + 1, :] = jnp.broadcast_to(dgl, (1, LANE))
            yield
            dw_ref[b, rows, sl] = (-_nt(dvn, s)).astype(dw_ref.dtype)
            ds_ref[b, h] = ds_part - _tn(w_ref[b, rows, sl], dvn)

        for gi in reversed(range(grp)):
            _run([chain(b, h, gi) for b in range(bsz) for h in range(NH)])

    steps = nc_seq // grp
    rev = lambda i: steps - 1 - i
    row = pl.BlockSpec((bsz, grp * CHUNK, hw), lambda i: (0, rev(i), 0))
    mat = pl.BlockSpec((bsz, grp, NH, CHUNK, CHUNK), lambda i: (0, rev(i), 0, 0, 0))
    glb = pl.BlockSpec((bsz, grp, NH, LANE), lambda i: (0, rev(i), 0, 0))
    big = jax.ShapeDtypeStruct((bsz, t_seq, hw), BF16)
    outs = pl.pallas_call(
        body, name=name, grid=(steps,),
        in_specs=[row, row, row, row, row, mat, glb,
                  pl.BlockSpec((bsz, grp, NH, DN_D, DN_D), lambda i: (0, rev(i), 0, 0, 0))],
        out_specs=[row, row, row, row, glb],
        out_shape=[big, big, jax.ShapeDtypeStruct(big.shape, F32), jax.ShapeDtypeStruct(big.shape, F32),
                   jax.ShapeDtypeStruct((bsz, nc_seq, NH, LANE), F32)],
        scratch_shapes=[pltpu.VMEM((bsz, NH, DN_D, DN_D), F32)],
        compiler_params=_cp(VMEM_BIG, ("arbitrary",)),
    )(do, w, qg, kd, vn, p, gl, hist)
    du, dw, dqg, dkd, dgl = outs
    n = bsz * t_seq
    return (du.reshape(n, hw), dw.reshape(n, hw), dqg.reshape(n, hw), dkd.reshape(n, hw),
            dgl.reshape(bsz * nc_seq, NH, LANE))


def _dn_intra_bwd(qn, kn, v, projp, alog_row, dtb_row, u, w, tmat, du, dw, dqg, dkd, do, vn, dgl, *, nc_seq, name,
                  rider=None):
    n = qn.shape[0]
    nct = n // CHUNK
    hw = NH * DN_D
    scale = DN_D ** -0.5

    grp = _group(nc_seq)

    def inner(gi, q_ref, k_ref, v_ref, sa_ref, al_ref, dt_ref, u_ref, w_ref, t_ref, du_ref, dw_ref, dqg_ref, dkd_ref,
              do_ref, vn_ref, dgl_ref, dq_ref, dk_ref, dv_ref, dsa_ref, dal_ref, ddt_ref):
        ci = (pl.program_id(0) * grp + gi) % nc_seq
        sa = sa_ref[...]
        beta_t, g_t, gam_t, valid, ea = _dn_gates(sa, al_ref[...], dt_ref[...], ci)
        yield
        lane = _iota2((CHUNK, LANE), 1)
        gates_t = jnp.where(lane < 4, beta_t, gam_t).T
        r, c = _masks64()
        incl, strict, upper, supper = r >= c, r > c, r <= c, r < c
        rows1 = _iota2((CHUNK, 1), 0)
        acc = [jnp.zeros((CHUNK, LANE), F32)]

        def head(h):
            sl = slice(h * DN_D, (h + 1) * DN_D)
            beta_w = jnp.broadcast_to(beta_t[:, h:h + 1], (CHUNK, DN_D))
            gam_w = jnp.broadcast_to(gam_t[:, 4 + h:5 + h], (CHUNK, DN_D))
            beta_s, gam_s = beta_w[:, 0:CHUNK], gam_w[:, 0:CHUNK]
            beta_row = gates_t[h:h + 1, :]
            gam_row = gates_t[4 + h:5 + h, :]
            gl = gam_t[CHUNK - 1:CHUNK, 4 + h:5 + h]
            dec = jnp.exp(jnp.where(incl, gam_s - gam_row, -jnp.inf))
            dec_t = jnp.exp(jnp.where(upper, gam_row - gam_s, -jnp.inf))
            egam_w = jnp.exp(gam_w)
            ekd_w = jnp.exp(gl - gam_w)
            kh = k_ref[:, sl].astype(F32)
            qh = q_ref[:, sl].astype(F32) * scale
            vh = v_ref[:, sl].astype(F32)
            uh = u_ref[:, sl]
            wh = w_ref[:, sl]
            doh = do_ref[:, sl]
            vnh = vn_ref[:, sl]
            kk = _nt(kh, kh)
            qk = _nt(qh, kh)
            qk_t = _nt(kh, qh)
            dp = _nt(doh, vnh)
            dp_t = _nt(vnh, doh)
            tm_t = t_ref[0, h].T
            dvb = _nn(tm_t, du_ref[:, sl])
            dkg = _nn(tm_t, dw_ref[:, sl])
            yield
            m = _nt(dvb, uh) + _nt(dkg, wh)
            m_t = _nt(uh, dvb) + _nt(wh, dkg)
            yield
            da = jnp.where(strict, -m, 0.0)
            da_t = jnp.where(supper, -m_t, 0.0)
            a = jnp.where(strict, beta_s * kk * dec, 0.0)
            a_t = jnp.where(supper, beta_row * kk * dec_t, 0.0)
            dad = da * dec
            dad_t = da_t * dec_t
            dpm = jnp.where(incl, dp, 0.0)
            dpm_t = jnp.where(upper, dp_t, 0.0)
            e = da * a + dpm * (qk * dec)
            e_t = da_t * a_t + dpm_t * (qk_t * dec_t)
            dqgh = dqg_ref[:, sl].astype(F32)
            dkdh = dkd_ref[:, sl].astype(F32)
            bg_w = beta_w * egam_w
            dkh = (_nn(beta_s * dad, kh) + _nn(beta_row * dad_t, kh) + _nn(dpm_t * dec_t, qh)
                   + bg_w * dkg + ekd_w * dkdh)
            dqh = _nn(dpm * dec, kh) + egam_w * dqgh
            t_kd = dkdh * (ekd_w * kh)
            dbeta = (jnp.sum(dad * kk, axis=1, keepdims=True)
                     + jnp.sum(dkg * (egam_w * kh) + dvb * vh, axis=1, keepdims=True))
            dgam = (jnp.sum(e - e_t, axis=1, keepdims=True)
                    + jnp.sum(dkg * (bg_w * kh) + dqgh * (egam_w * qh) - t_kd, axis=1, keepdims=True))
            dgam_last = (jnp.sum(jnp.sum(t_kd, axis=0, keepdims=True), axis=1, keepdims=True)
                         + dgl_ref[0, h:h + 1, 0:1] * jnp.exp(gl))
            dgam = dgam + jnp.where(rows1 == CHUNK - 1, dgam_last, 0.0)
            dq_ref[:, sl] = (dqh * scale).astype(dq_ref.dtype)
            dk_ref[:, sl] = dkh.astype(dk_ref.dtype)
            dv_ref[:, sl] = (beta_w * dvb).astype(dv_ref.dtype)
            acc[0] = acc[0] + jnp.where(lane == h, dbeta, 0.0) + jnp.where(lane == 4 + h, dgam, 0.0)

        yield from _round_robin([head(h) for h in range(NH)])
        acc_t = acc[0]
        dg_t = _tri_sum(upper, acc_t)
        ddb = acc_t * beta_t * (1.0 - beta_t)
        dda = jnp.where(valid, dg_t * (-ea) * _sigmoid(sa + dt_ref[...]), 0.0)
        dsa_ref[...] = jnp.where(lane < 4, ddb, jnp.where(lane < 8, dda, 0.0)).astype(dsa_ref.dtype)
        in_g = jnp.logical_and(lane >= 4, lane < 8)
        dal = jnp.sum(jnp.where(in_g, dg_t * g_t, 0.0), axis=0, keepdims=True)
        ddt = jnp.sum(jnp.where(in_g, dda, 0.0), axis=0, keepdims=True)
        _accumulate(dal_ref, dal, gi)
        _accumulate(ddt_ref, ddt, gi)

    rows = grp * CHUNK
    row = pl.BlockSpec((rows, hw), lambda i: (i, 0))
    vec = pl.BlockSpec((1, LANE), lambda i: (0, 0))
    mat = pl.BlockSpec((grp, NH, CHUNK, CHUNK), lambda i: (i, 0, 0, 0))
    glb = pl.BlockSpec((grp, NH, LANE), lambda i: (i, 0, 0))
    big = jax.ShapeDtypeStruct((n, hw), F32)
    v128 = jax.ShapeDtypeStruct((1, LANE), F32)
    kinds = (["row"] * 4 + ["whole"] * 2 + ["row"] * 2 + ["lead"] + ["row"] * 6 + ["lead"]
             + ["row"] * 4 + ["whole"] * 2)
    outs, ridden = _hosted_call(
        _per_chunk(inner, kinds, grp), rider, name=name, grid=(nct // grp,),
        in_specs=[row, row, row, pl.BlockSpec((rows, LANE), lambda i: (i, 0)), vec, vec,
                  row, row, mat, row, row, row, row, row, row, glb],
        out_specs=[row, row, row, pl.BlockSpec((rows, LANE), lambda i: (i, 0)), vec, vec],
        out_shape=[big, big, big, jax.ShapeDtypeStruct((n, LANE), BF16), v128, v128],
        scratch_shapes=[], compiler_params=_cp(VMEM_BIG, ("arbitrary",)),
        args=(qn, kn, v, projp, alog_row, dtb_row, u, w, tmat, du, dw, dqg, dkd, do, vn, dgl))
    return (*outs, ridden)


GQ_W = NH * GLA_DK
GV_W = NH * GLA_DV
GLA_NORM = 16.0
MID = CHUNK // 2


def _gla_gates(sb, w2p, gb, chunk_in_seq):
    rows = _iota2((CHUNK, GQ_W), 0)
    valid = jnp.logical_or(rows >= N_PAD, chunk_in_seq > 0)
    graw = _nn(sb, w2p) + gb
    yield
    g = jnp.where(valid, _logsigmoid(graw) * (1.0 / GLA_NORM), 0.0)
    r, c = _masks64()
    bcum = _tri_sum(r >= c, g)
    yield
    return graw, bcum, valid


def _head_mask(h):
    lane = _iota2((1, GQ_W), 1)
    return jnp.logical_and(lane >= h * GLA_DK, lane < (h + 1) * GLA_DK)


def _gla_intra_fwd(projp, gates, w2p, gb, *, nc_seq, name):
    n = projp.shape[0]
    nct = n // CHUNK
    scale = GLA_DK ** -0.5

    grp = _group(nc_seq)
    rows = grp * CHUNK

    def inner(gi, qk_ref, v_ref, sb_ref, w2_ref, gb_ref, oi_ref, qg_ref, kd_ref, gl_ref):
        ci = (pl.program_id(0) * grp + gi) % nc_seq
        _, bc, _ = yield from _gla_gates(sb_ref[...], w2_ref[...], gb_ref[...], ci)
        bref = bc[MID:MID + 1, :]
        bl = bc[CHUNK - 1:CHUNK, :]
        q = qk_ref[:, 0:GQ_W].astype(F32) * scale
        k = qk_ref[:, GQ_W:2 * GQ_W].astype(F32)
        qi = q * jnp.exp(bc - bref)
        ki = k * jnp.exp(bref - bc)
        qg_ref[...] = (q * jnp.exp(bc)).astype(qg_ref.dtype)
        kd_ref[...] = (k * jnp.exp(bl - bc)).astype(kd_ref.dtype)
        gl_ref[0] = jnp.exp(bl)
        r, c = _masks64()
        incl = r >= c
        a = [jnp.where(incl, _nt(jnp.where(_head_mask(h), qi, 0.0), ki), 0.0) for h in range(NH)]
        yield
        for h in range(NH):
            oi_ref[:, h * GLA_DV:(h + 1) * GLA_DV] = _nn(a[h], v_ref[:, h * GLA_DV:(h + 1) * GLA_DV]).astype(oi_ref.dtype)

    kinds = ["row"] * 3 + ["whole"] * 2 + ["row"] * 3 + ["lead"]
    return pl.pallas_call(
        _per_chunk(inner, kinds, grp), name=name, grid=(nct // grp,),
        in_specs=[pl.BlockSpec((rows, 2 * GQ_W), lambda i: (i, C_GQK // (2 * GQ_W))),
                  pl.BlockSpec((rows, GV_W), lambda i: (i, C_GV // GV_W)),
                  pl.BlockSpec((rows, LANE), lambda i: (i, 1)),
                  pl.BlockSpec((LANE, GQ_W), lambda i: (0, 0)), pl.BlockSpec((1, GQ_W), lambda i: (0, 0))],
        out_specs=[pl.BlockSpec((rows, GV_W), lambda i: (i, 0)), pl.BlockSpec((rows, GQ_W), lambda i: (i, 0)),
                   pl.BlockSpec((rows, GQ_W), lambda i: (i, 0)), pl.BlockSpec((grp, 1, GQ_W), lambda i: (i, 0, 0))],
        out_shape=[jax.ShapeDtypeStruct((n, GV_W), BF16), jax.ShapeDtypeStruct((n, GQ_W), BF16),
                   jax.ShapeDtypeStruct((n, GQ_W), BF16), jax.ShapeDtypeStruct((nct, 1, GQ_W), F32)],
        compiler_params=_cp(VMEM_BIG),
    )(projp, projp, gates, w2p, gb)


def _gla_scan_fwd(oi, qg, kd, gl, projp, *, bsz, nc_seq, name, rider=None):
    t_seq = nc_seq * CHUNK
    oi = oi.reshape(bsz, t_seq, GV_W)
    qg, kd = qg.reshape(bsz, t_seq, GQ_W), kd.reshape(bsz, t_seq, GQ_W)
    gl = gl.reshape(bsz, nc_seq, 1, GQ_W)
    pj = projp.reshape(bsz, t_seq, PW)
    grp = _group(nc_seq)

    def body(oi_ref, qg_ref, kd_ref, gl_ref, v_ref, o_ref, hist_ref, st_ref):
        @pl.when(pl.program_id(0) == 0)
        def _():
            st_ref[...] = jnp.zeros_like(st_ref)

        for gi in range(grp):
            rows = pl.ds(gi * CHUNK, CHUNK)
            for b in range(bsz):
                st = st_ref[b]
                hist_ref[b, gi] = st.astype(hist_ref.dtype)
                qgb = qg_ref[b, rows, :]
                kdb = kd_ref[b, rows, :]
                upd = jnp.zeros((GLA_DV, GQ_W), F32)
                for h in range(NH):
                    sl = slice(h * GLA_DV, (h + 1) * GLA_DV)
                    m = _head_mask(h)
                    o_ref[b, rows, sl] = (oi_ref[b, rows, sl] + _nt(jnp.where(m, qgb, 0.0), st)).astype(o_ref.dtype)
                    upd = upd + jnp.where(m, _tn(v_ref[b, rows, sl], kdb), 0.0)
                st_ref[b] = gl_ref[b, gi] * st + upd

    rws = grp * CHUNK
    outs, ridden = _hosted_call(
        body, rider, name=name, grid=(nc_seq // grp,),
        in_specs=[pl.BlockSpec((bsz, rws, GV_W), lambda i: (0, i, 0)),
                  pl.BlockSpec((bsz, rws, GQ_W), lambda i: (0, i, 0)),
                  pl.BlockSpec((bsz, rws, GQ_W), lambda i: (0, i, 0)),
                  pl.BlockSpec((bsz, grp, 1, GQ_W), lambda i: (0, i, 0, 0)),
                  pl.BlockSpec((bsz, rws, GV_W), lambda i: (0, i, C_GV // GV_W))],
        out_specs=[pl.BlockSpec((bsz, rws, GV_W), lambda i: (0, i, 0)),
                   pl.BlockSpec((bsz, grp, GLA_DV, GQ_W), lambda i: (0, i, 0, 0))],
        out_shape=[jax.ShapeDtypeStruct((bsz, t_seq, GV_W), BF16),
                   jax.ShapeDtypeStruct((bsz, nc_seq, GLA_DV, GQ_W), BF16)],
        scratch_shapes=[pltpu.VMEM((bsz, GLA_DV, GQ_W), F32)],
        compiler_params=_cp(VMEM_BIG, ("arbitrary",)), args=(oi, qg, kd, gl, pj))
    return outs[0].reshape(bsz * t_seq, GV_W), outs[1], ridden


def _gla_scan_bwd(do, qg, kd, gl, projp, hist, *, bsz, nc_seq, name):
    t_seq = nc_seq * CHUNK
    do = do.reshape(bsz, t_seq, GV_W)
    qg, kd = qg.reshape(bsz, t_seq, GQ_W), kd.reshape(bsz, t_seq, GQ_W)
    gl = gl.reshape(bsz, nc_seq, 1, GQ_W)
    pj = projp.reshape(bsz, t_seq, PW)
    grp = _group(nc_seq)

    def body(do_ref, qg_ref, kd_ref, gl_ref, v_ref, hist_ref, dqg_ref, dkd_ref, dv_ref, dgl_ref, dst_ref):
        @pl.when(pl.program_id(0) == 0)
        def _():
            dst_ref[...] = jnp.zeros_like(dst_ref)

        for gi in reversed(range(grp)):
            rows = pl.ds(gi * CHUNK, CHUNK)
            for b in range(bsz):
                st = hist_ref[b, gi]
                dst = dst_ref[b]
                qgb = qg_ref[b, rows, :]
                kdb = kd_ref[b, rows, :]
                dqg = jnp.zeros((CHUNK, GQ_W), F32)
                dkd = jnp.zeros((CHUNK, GQ_W), F32)
                add = jnp.zeros((GLA_DV, GQ_W), F32)
                for h in range(NH):
                    sl = slice(h * GLA_DV, (h + 1) * GLA_DV)
                    m = _head_mask(h)
                    doh = do_ref[b, rows, sl]
                    vh = v_ref[b, rows, sl]
                    dqg = dqg + jnp.where(m, _nn(doh, st), 0.0)
                    dkd = dkd + jnp.where(m, _nn(vh, dst), 0.0)
                    dv_ref[b, rows, sl] = _nt(jnp.where(m, kdb, 0.0), dst).astype(dv_ref.dtype)
                    add = add + jnp.where(m, _tn(doh, qgb), 0.0)
                dqg_ref[b, rows, :] = dqg.astype(dqg_ref.dtype)
                dkd_ref[b, rows, :] = dkd.astype(dkd_ref.dtype)
                dgl_ref[b, gi] = jnp.sum(dst * st, axis=0, keepdims=True)
                dst_ref[b] = gl_ref[b, gi] * dst + add

    steps = nc_seq // grp
    rws = grp * CHUNK
    rev = lambda i: steps - 1 - i
    outs = pl.pallas_call(
        body, name=name, grid=(steps,),
        in_specs=[pl.BlockSpec((bsz, rws, GV_W), lambda i: (0, rev(i), 0)),
                  pl.BlockSpec((bsz, rws, GQ_W), lambda i: (0, rev(i), 0)),
                  pl.BlockSpec((bsz, rws, GQ_W), lambda i: (0, rev(i), 0)),
                  pl.BlockSpec((bsz, grp, 1, GQ_W), lambda i: (0, rev(i), 0, 0)),
                  pl.BlockSpec((bsz, rws, GV_W), lambda i: (0, rev(i), C_GV // GV_W)),
                  pl.BlockSpec((bsz, grp, GLA_DV, GQ_W), lambda i: (0, rev(i), 0, 0))],
        out_specs=[pl.BlockSpec((bsz, rws, GQ_W), lambda i: (0, rev(i), 0)),
                   pl.BlockSpec((bsz, rws, GQ_W), lambda i: (0, rev(i), 0)),
                   pl.BlockSpec((bsz, rws, GV_W), lambda i: (0, rev(i), 0)),
                   pl.BlockSpec((bsz, grp, 1, GQ_W), lambda i: (0, rev(i), 0, 0))],
        out_shape=[jax.ShapeDtypeStruct((bsz, t_seq, GQ_W), F32), jax.ShapeDtypeStruct((bsz, t_seq, GQ_W), F32),
                   jax.ShapeDtypeStruct((bsz, t_seq, GV_W), BF16), jax.ShapeDtypeStruct((bsz, nc_seq, 1, GQ_W), F32)],
        scratch_shapes=[pltpu.VMEM((bsz, GLA_DV, GQ_W), F32)],
        compiler_params=_cp(VMEM_BIG, ("arbitrary",)),
    )(do, qg, kd, gl, pj, hist)
    n = bsz * t_seq
    return (outs[0].reshape(n, GQ_W), outs[1].reshape(n, GQ_W), outs[2].reshape(n, GV_W),
            outs[3].reshape(bsz * nc_seq, 1, GQ_W))


def _gla_intra_bwd(projp, gates, w2p, gb, do, dqg, dkd, dvi, dgl, *, nc_seq, name):
    n = projp.shape[0]
    nct = n // CHUNK
    scale = GLA_DK ** -0.5

    grp = _group(nc_seq)
    rows = grp * CHUNK

    def inner(gi, qk_ref, v_ref, sb_ref, w2_ref, gb_ref, do_ref, dqg_ref, dkd_ref, dvi_ref, dgl_ref,
              dqk_ref, dv_ref, dsb_ref, dw2_ref, dgb_ref):
        ci = (pl.program_id(0) * grp + gi) % nc_seq
        sb = sb_ref[...]
        w2 = w2_ref[...]
        graw, bc, valid = yield from _gla_gates(sb, w2, gb_ref[...], ci)
        bref = bc[MID:MID + 1, :]
        bl = bc[CHUNK - 1:CHUNK, :]
        q = qk_ref[:, 0:GQ_W].astype(F32) * scale
        k = qk_ref[:, GQ_W:2 * GQ_W].astype(F32)
        ex1 = jnp.exp(bc - bref)
        ex2 = jnp.exp(bref - bc)
        eb = jnp.exp(bc)
        ekd = jnp.exp(bl - bc)
        qi, ki = q * ex1, k * ex2
        r, c = _masks64()
        incl = r >= c
        upper = r <= c
        a_t, da, da_t = [], [], []
        for h in range(NH):
            sl = slice(h * GLA_DV, (h + 1) * GLA_DV)
            doh = do_ref[:, sl]
            vh = v_ref[:, sl]
            a_t.append(jnp.where(upper, _nt(jnp.where(_head_mask(h), ki, 0.0), qi), 0.0))
            da.append(jnp.where(incl, _nt(doh, vh), 0.0))
            da_t.append(jnp.where(upper, _nt(vh, doh), 0.0))
        yield
        dqi = jnp.zeros((CHUNK, GQ_W), F32)
        dki = jnp.zeros((CHUNK, GQ_W), F32)
        for h in range(NH):
            sl = slice(h * GLA_DV, (h + 1) * GLA_DV)
            m = _head_mask(h)
            dv_ref[:, sl] = (_nn(a_t[h], do_ref[:, sl]) + dvi_ref[:, sl]).astype(dv_ref.dtype)
            dqi = dqi + jnp.where(m, _nn(da[h], ki), 0.0)
            dki = dki + jnp.where(m, _nn(da_t[h], qi), 0.0)
        yield
        dqg = dqg_ref[...].astype(F32)
        dkd = dkd_ref[...].astype(F32)
        dqk_ref[:, 0:GQ_W] = ((dqi * ex1 + dqg * eb) * scale).astype(dqk_ref.dtype)
        dqk_ref[:, GQ_W:2 * GQ_W] = (dki * ex2 + dkd * ekd).astype(dqk_ref.dtype)
        t_qi, t_ki, t_kd = dqi * qi, dki * ki, dkd * (k * ekd)
        db = t_qi - t_ki + dqg * (q * eb) - t_kd
        dbref = jnp.sum(t_ki - t_qi, axis=0, keepdims=True)
        dbl = jnp.sum(t_kd, axis=0, keepdims=True) + dgl_ref[0] * jnp.exp(bl)
        rows = _iota2((CHUNK, GQ_W), 0)
        db = db + jnp.where(rows == MID, dbref, 0.0) + jnp.where(rows == CHUNK - 1, dbl, 0.0)
        dg = _tri_sum(upper, db)
        yield
        dgraw = jnp.where(valid, dg * (1.0 / GLA_NORM) * _sigmoid(-graw), 0.0)
        dsb_ref[...] = _nt(dgraw, w2).astype(dsb_ref.dtype)
        dw2 = _tn(sb, dgraw)
        dgb = jnp.sum(dgraw, axis=0, keepdims=True)
        _accumulate(dw2_ref, dw2, gi)
        _accumulate(dgb_ref, dgb, gi)

    rq = pl.BlockSpec((rows, GQ_W), lambda i: (i, 0))
    rv = pl.BlockSpec((rows, GV_W), lambda i: (i, 0))
    kinds = ["row"] * 3 + ["whole"] * 2 + ["row"] * 4 + ["lead"] + ["row"] * 3 + ["whole"] * 2
    return pl.pallas_call(
        _per_chunk(inner, kinds, grp), name=name, grid=(nct // grp,),
        in_specs=[pl.BlockSpec((rows, 2 * GQ_W), lambda i: (i, C_GQK // (2 * GQ_W))),
                  pl.BlockSpec((rows, GV_W), lambda i: (i, C_GV // GV_W)),
                  pl.BlockSpec((rows, LANE), lambda i: (i, 1)),
                  pl.BlockSpec((LANE, GQ_W), lambda i: (0, 0)), pl.BlockSpec((1, GQ_W), lambda i: (0, 0)),
                  rv, rq, rq, rv, pl.BlockSpec((grp, 1, GQ_W), lambda i: (i, 0, 0))],
        out_specs=[pl.BlockSpec((rows, 2 * GQ_W), lambda i: (i, 0)), rv, pl.BlockSpec((rows, LANE), lambda i: (i, 0)),
                   pl.BlockSpec((LANE, GQ_W), lambda i: (0, 0)), pl.BlockSpec((1, GQ_W), lambda i: (0, 0))],
        out_shape=[jax.ShapeDtypeStruct((n, 2 * GQ_W), BF16), jax.ShapeDtypeStruct((n, GV_W), BF16),
                   jax.ShapeDtypeStruct((n, LANE), BF16), jax.ShapeDtypeStruct((LANE, GQ_W), F32),
                   jax.ShapeDtypeStruct((1, GQ_W), F32)],
        compiler_params=_cp(VMEM_BIG, ("arbitrary",)),
    )(projp, projp, gates, w2p, gb, do, dqg, dkd, dvi, dgl)


SECTIONS = ((C_QKV, 1536), (C_DZ, 512), (C_GQK, 512), (C_GV, 512), (C_GR, 512), (C_SA, 128), (C_SB, 128))


def _inproj_bwd(secs, wp, h0, g1, dx1, *, tr, name):
    n, d = h0.shape

    def body(*refs):
        sec_refs = refs[:len(SECTIONS)]
        wp_ref, h0_ref, g_ref, dx1_ref, o_ref, dg_ref = refs[len(SECTIONS):]
        dh = None
        for s_ref, (off, wd) in zip(sec_refs, SECTIONS):
            part = _nt(s_ref[...], wp_ref[:, off:off + wd])
            dh = part if dh is None else dh + part
        dx, dg = _rms_bwd_math(h0_ref[...], g_ref[...], dh)
        o_ref[...] = dx1_ref[...] + dx

        @pl.when(pl.program_id(0) == 0)
        def _():
            dg_ref[...] = dg

        @pl.when(pl.program_id(0) > 0)
        def _():
            dg_ref[...] += dg

    row = pl.BlockSpec((tr, d), lambda i: (i, 0))
    vec = pl.BlockSpec((1, d), lambda i: (0, 0))
    return pl.pallas_call(
        body, name=name, grid=(n // tr,),
        in_specs=[pl.BlockSpec((tr, wd), lambda i: (i, 0)) for _, wd in SECTIONS]
        + [pl.BlockSpec((d, PW), lambda i: (0, 0)), row, vec, row],
        out_specs=[row, vec],
        out_shape=[jax.ShapeDtypeStruct((n, d), F32), jax.ShapeDtypeStruct((1, d), F32)],
        compiler_params=_cp(VMEM_BIG),
    )(*secs, wp, h0, g1, dx1)


def _adamw(w, g, m, v, *, name, emit_grad=False):
    lead = w.ndim - 2
    r, c = w.shape[-2:]
    tr = _tile(r, 256, 8) if r > 256 else r
    c1 = 1.0 - ADAM_B1 ** ADAM_STEP
    c2 = 1.0 - ADAM_B2 ** ADAM_STEP
    n_out = 4 if emit_grad else 3

    def body(w_ref, g_ref, m_ref, v_ref, *out_refs):
        rd = (lambda ref: ref[0]) if lead else (lambda ref: ref[...])
        gv = g_ref[:, 0:c]
        nm = ADAM_B1 * rd(m_ref) + (1.0 - ADAM_B1) * gv
        nv = ADAM_B2 * rd(v_ref) + (1.0 - ADAM_B2) * (gv * gv)
        res = [-ADAM_LR * ((nm / c1) / (jnp.sqrt(nv / c2) + ADAM_EPS) + ADAM_WD * rd(w_ref)), nm, nv, gv]
        for o_ref, val in zip(out_refs, res):
            if lead:
                o_ref[0] = val
            else:
                o_ref[...] = val

    blk = pl.BlockSpec((1,) * lead + (tr, c), lambda i: (0,) * lead + (i, 0))
    gblk = pl.BlockSpec((tr, g.shape[1]), lambda i: (i, 0))
    sds = jax.ShapeDtypeStruct(w.shape, F32)
    return pl.pallas_call(
        body, name=name, grid=(r // tr,), in_specs=[blk, gblk, blk, blk], out_specs=[blk] * n_out,
        out_shape=[sds] * n_out, compiler_params=_cp(VMEM_BIG),
    )(w, g, m, v)


def _pair_sum(where, g, theirs, *, name):
    lead, r, cols = g.shape
    half = r // 2
    tr = _tile(half, 256, 16)
    nh = half // tr

    def body(w_ref, a_ref, b_ref, o_ref):
        o_ref[...] = (a_ref[...] + b_ref[...]).astype(o_ref.dtype)

    blk = pl.BlockSpec((1, tr, cols), lambda s, i, w: (s, i, 0))
    return pl.pallas_call(
        body, name=name,
        grid_spec=pltpu.PrefetchScalarGridSpec(
            num_scalar_prefetch=1, grid=(lead, nh),
            in_specs=[pl.BlockSpec((1, tr, cols), lambda s, i, w: (s, w[0] * nh + i, 0)), blk], out_specs=blk),
        out_shape=jax.ShapeDtypeStruct((lead, half, cols), BF16), compiler_params=_cp(VMEM_BIG),
    )(where, g, theirs)


def _chip_sum(where, pair, q, *, name):
    _, half, cols = pair.shape
    tr = _tile(half, 256, 16)
    nh = half // tr

    def body(w_ref, own_ref, q1_ref, q2_ref, q3_ref, o_ref):
        f = lambda ref: ref[0].astype(F32)
        o_ref[...] = ((f(own_ref) + f(q1_ref)) + f(q2_ref)) + f(q3_ref)

    def peer(d):
        return pl.BlockSpec((1, tr, cols), lambda i, w: ((w[1] + d) % N_CHIPS, i, 0))

    return pl.pallas_call(
        body, name=name,
        grid_spec=pltpu.PrefetchScalarGridSpec(
            num_scalar_prefetch=1, grid=(nh,),
            in_specs=[peer(0), peer(1), peer(2), peer(3)],
            out_specs=pl.BlockSpec((tr, cols), lambda i, w: (w[0] * nh + i, 0))),
        out_shape=jax.ShapeDtypeStruct((2 * half, cols), F32), compiler_params=_cp(VMEM_BIG),
    )(where, pair, q, q, q)


VM = pl.BlockSpec(memory_space=pltpu.VMEM)


def _row_chunks(rows, n_split):
    size = rows // n_split
    assert size * n_split == rows and size % 16 == 0, (rows, n_split)
    return [(s, pl.ds(s * size, size)) for s in range(n_split)], size


D2D_SPLIT = 4
ICI_SPLIT = 2


def _sibling_halves(grads):
    n_arr = len(grads)

    def body(*refs):
        ins = refs[:n_arr]
        theirs = refs[n_arr:2 * n_arr]
        send_sems, recv_sems = refs[2 * n_arr:]
        x, y, c = _place()
        copies = []
        for k in range(n_arr):
            half = ins[k].shape[1] // 2
            chunks, size = _row_chunks(half, D2D_SPLIT)
            for s, dst_rows in chunks:
                give = pltpu.make_async_remote_copy(
                    src_ref=ins[k].at[:, pl.ds((1 - c) * half + s * size, size), :], dst_ref=theirs[k].at[:, dst_rows, :],
                    send_sem=send_sems.at[k, s], recv_sem=recv_sems.at[k, s], device_id=(x, y, 1 - c),
                    device_id_type=MESH)
                give.start()
                copies.append(give)
        for give in copies:
            give.wait()

    halves = [jax.ShapeDtypeStruct((g.shape[0], g.shape[1] // 2, g.shape[2]), F32) for g in grads]
    sem = pltpu.SemaphoreType.DMA((n_arr, D2D_SPLIT))
    return pl.pallas_call(
        body, name="sibling_halves", in_specs=[ANY] * n_arr, out_specs=[ANY] * n_arr, out_shape=halves,
        scratch_shapes=[sem, sem],
    )(*grads)


def _chip_exchange(parts):
    n_arr = len(parts)

    def body(*refs):
        ins = refs[:n_arr]
        outs = refs[n_arr:2 * n_arr]
        send_sems, recv_sems = refs[2 * n_arr:]
        x, y, c = _place()
        me = 2 * x + y
        sends = []
        for k in range(n_arr):
            chunks, _ = _row_chunks(ins[k].shape[1], ICI_SPLIT)
            for d, (px, py, pj) in enumerate(_other_chips(x, y)):
                for s, rows in chunks:
                    cp = pltpu.make_async_remote_copy(
                        src_ref=ins[k].at[pj, rows, :], dst_ref=outs[k].at[me, rows, :], send_sem=send_sems.at[k, d, s],
                        recv_sem=recv_sems.at[k, d, s], device_id=(px, py, c), device_id_type=MESH)
                    cp.start()
                    sends.append(cp)
        for k in range(n_arr):
            chunks, _ = _row_chunks(ins[k].shape[1], ICI_SPLIT)
            for d, (px, py, pj) in enumerate(_other_chips(x, y)):
                for s, rows in chunks:
                    pltpu.make_async_remote_copy(
                        src_ref=ins[k].at[pj, rows, :], dst_ref=outs[k].at[pj, rows, :], send_sem=send_sems.at[k, d, s],
                        recv_sem=recv_sems.at[k, d, s], device_id=(px, py, c), device_id_type=MESH).wait_recv()
        for cp in sends:
            cp.wait_send()

    sem = pltpu.SemaphoreType.DMA((n_arr, 3, ICI_SPLIT))
    return pl.pallas_call(
        body, name="chip_exchange", in_specs=[ANY] * n_arr, out_specs=[ANY] * n_arr,
        out_shape=[jax.ShapeDtypeStruct(p.shape, p.dtype) for p in parts],
        scratch_shapes=[sem, sem],
    )(*parts)


class _SiblingHalvesRider:
    def __init__(self, grads):
        self.inputs = list(grads)
        self.out_shapes = [jax.ShapeDtypeStruct((g.shape[0], g.shape[1] // 2, g.shape[2]), F32) for g in grads]
        self.aliases = {}
        self.sems = [pltpu.SemaphoreType.DMA((len(grads), D2D_SPLIT))] * 2

    def _copies(self, ins, outs, sems):
        x, y, c = _place()
        for k in range(len(ins)):
            half = ins[k].shape[1] // 2
            chunks, size = _row_chunks(half, D2D_SPLIT)
            for s, dst_rows in chunks:
                yield pltpu.make_async_remote_copy(
                    src_ref=ins[k].at[:, pl.ds((1 - c) * half + s * size, size), :], dst_ref=outs[k].at[:, dst_rows, :],
                    send_sem=sems[0].at[k, s], recv_sem=sems[1].at[k, s], device_id=(x, y, 1 - c), device_id_type=MESH)

    def first(self, ins, outs, sems):
        for cp in self._copies(ins, outs, sems):
            cp.start()

    def last(self, ins, outs, sems):
        for cp in self._copies(ins, outs, sems):
            cp.wait()


class _ChipExchangeRider:
    def __init__(self, parts):
        self.inputs = list(parts)
        self.out_shapes = [jax.ShapeDtypeStruct(p.shape, p.dtype) for p in parts]
        self.aliases = {}
        self.sems = [pltpu.SemaphoreType.DMA((len(parts), 3, ICI_SPLIT))] * 2

    def _copies(self, ins, outs, sems, receiving):
        x, y, c = _place()
        for k in range(len(ins)):
            chunks, _ = _row_chunks(ins[k].shape[1], ICI_SPLIT)
            for d, (px, py, pj) in enumerate(_other_chips(x, y)):
                for s, rows in chunks:
                    yield pltpu.make_async_remote_copy(
                        src_ref=ins[k].at[pj, rows, :], dst_ref=outs[k].at[pj if receiving else 2 * x + y, rows, :],
                        send_sem=sems[0].at[k, d, s], recv_sem=sems[1].at[k, d, s], device_id=(px, py, c),
                        device_id_type=MESH)

    def first(self, ins, outs, sems):
        for cp in self._copies(ins, outs, sems, False):
            cp.start()

    def last(self, ins, outs, sems):
        for cp in self._copies(ins, outs, sems, True):
            cp.wait_recv()
        for cp in self._copies(ins, outs, sems, False):
            cp.wait_send()


def _sibling_join(bufs):
    n_arr = len(bufs)

    def body(*refs):
        bufs_out = refs[n_arr:2 * n_arr]
        send_sems, recv_sems = refs[2 * n_arr:]
        x, y, c = _place()
        copies = []
        for k in range(n_arr):
            half = bufs_out[k].shape[0] // 2
            chunks, size = _row_chunks(half, D2D_SPLIT)
            for s, _ in chunks:
                rows = pl.ds(c * half + s * size, size)
                give = pltpu.make_async_remote_copy(
                    src_ref=bufs_out[k].at[rows, :], dst_ref=bufs_out[k].at[rows, :], send_sem=send_sems.at[k, s],
                    recv_sem=recv_sems.at[k, s], device_id=(x, y, 1 - c), device_id_type=MESH)
                give.start()
                copies.append((k, s, half, size, give))
        for k, s, half, size, give in copies:
            rows = pl.ds((1 - c) * half + s * size, size)
            pltpu.make_async_remote_copy(
                src_ref=bufs_out[k].at[rows, :], dst_ref=bufs_out[k].at[rows, :], send_sem=send_sems.at[k, s],
                recv_sem=recv_sems.at[k, s], device_id=(x, y, 1 - c), device_id_type=MESH).wait_recv()
            give.wait_send()

    sem = pltpu.SemaphoreType.DMA((n_arr, D2D_SPLIT))
    return pl.pallas_call(
        body, name="sibling_join", in_specs=[ANY] * n_arr, out_specs=[ANY] * n_arr,
        out_shape=[jax.ShapeDtypeStruct(b.shape, F32) for b in bufs],
        input_output_aliases={k: k for k in range(n_arr)},
        scratch_shapes=[sem, sem],
    )(*bufs)


PACK_ROWS = 48


def _small_allreduce(pack):
    masks = [(dx, dy, dc) for dx in (0, 1) for dy in (0, 1) for dc in (0, 1)][1:]

    def body(p_ref, o_ref, buf, send_sems, recv_sems):
        x, y, c = _place()
        me = 4 * x + 2 * y + c
        buf[me] = p_ref[...]
        sends = []
        for k, (dx, dy, dc) in enumerate(masks):
            peer = (1 - x if dx else x, 1 - y if dy else y, 1 - c if dc else c)
            cp = pltpu.make_async_remote_copy(
                src_ref=p_ref, dst_ref=buf.at[me], send_sem=send_sems.at[k], recv_sem=recv_sems.at[k],
                device_id=peer, device_id_type=MESH)
            cp.start()
            sends.append(cp)
        for k, (dx, dy, dc) in enumerate(masks):
            peer = (1 - x if dx else x, 1 - y if dy else y, 1 - c if dc else c)
            pj = 4 * peer[0] + 2 * peer[1] + peer[2]
            pltpu.make_async_remote_copy(
                src_ref=p_ref, dst_ref=buf.at[pj], send_sem=send_sems.at[k], recv_sem=recv_sems.at[k],
                device_id=peer, device_id_type=MESH).wait_recv()
        for cp in sends:
            cp.wait_send()
        tot = buf[0]
        for k in range(1, 8):
            tot = tot + buf[k]
        o_ref[...] = tot
        o_ref[0:N_META, :] = tot[0:N_META] + tot[N_META:2 * N_META]

    return pl.pallas_call(
        body, name="small_allreduce", in_specs=[VM], out_specs=VM,
        out_shape=jax.ShapeDtypeStruct((PACK_ROWS, D_MODEL), F32),
        scratch_shapes=[pltpu.VMEM((8, PACK_ROWS, D_MODEL), F32), pltpu.SemaphoreType.DMA((7,)),
                        pltpu.SemaphoreType.DMA((7,))],
    )(pack)


def _pad_lanes(vec, offset):
    k = vec.shape[1]
    return jnp.concatenate([jnp.zeros((1, offset), F32), vec, jnp.zeros((1, LANE - offset - k), F32)], axis=1)


def _local_step(x, tgt, meta, norm1_g, wp, conv_w, a_log, dt_bias, dn_norm_g, gla_w2, gla_b, gla_norm_g,
                w_out, norm2_g, w_up, w_down, final_norm_g, late_gather=None, where=None):
    bsz, s_len, d = x.shape
    t_seq = s_len + CHUNK
    nc_seq = t_seq // CHUNK
    n = bsz * t_seq
    tr = _tile(t_seq, 832)
    tt = _tile(t_seq, 416)

    lead = jnp.concatenate([jnp.zeros((N_PAD, d), F32), meta], axis=0)
    h0 = jnp.concatenate([jnp.broadcast_to(lead[None], (bsz, CHUNK, d)), x], axis=1).reshape(n, d)
    tgt_p = jnp.concatenate([jnp.zeros((bsz, CHUNK, d), F32), tgt], axis=1).reshape(n, d)
    alog_row = _pad_lanes(a_log, 4)
    dtb_row = _pad_lanes(dt_bias, 4)
    w2p = jnp.concatenate([gla_w2, jnp.zeros((LANE - GLA_RANK, GQ_W), F32)], axis=0)

    h = _rms_fwd(h0, norm1_g, tr=tr, name="norm1")
    projp, gates = _mm(h, wp, "nn", tm=tt, tn=PW, tk=d, out_dtypes=(BF16, F32), out_widths=(PW, PW - C_SA),
                       epilogue=lambda acc: (acc, acc[:, C_SA:PW]), name="in_proj")
    qn, kn, v = _dnprep_fwd(projp, conv_w, bsz=bsz, t_seq=t_seq, tt=tt, name="dn_prep")
    u, w, qg, kd, pmat, tmat, gl = _dn_intra_fwd(qn, kn, v, gates, alog_row, dtb_row, nc_seq=nc_seq, name="dn_intra")
    ride_a, ride_b = late_gather if late_gather is not None else (None, None)
    o_dn, vn, hist, got_a = _dn_scan_fwd(u, w, qg, kd, pmat, gl, bsz=bsz, nc_seq=nc_seq, name="dn_scan", rider=ride_a)
    oi, gqg, gkd, ggl = _gla_intra_fwd(projp, gates, w2p, gla_b, nc_seq=nc_seq, name="gla_intra")
    o_gla, ghist, got_b = _gla_scan_fwd(oi, gqg, gkd, ggl, projp, bsz=bsz, nc_seq=nc_seq, name="gla_scan", rider=ride_b)
    if late_gather is not None:
        w_out = got_a[0].reshape(d, d)
        w_up = got_a[1].transpose(1, 0, 2).reshape(d, D_FF)
        w_down = got_b[0].reshape(D_FF, d)
    mix = _gnorm_fwd(o_dn, o_gla, projp, dn_norm_g, gla_norm_g, tr=tr, name="gated_norm")
    (x1,) = _mm(mix, w_out, "nn", tm=tr, tn=d, tk=d, out_dtypes=(F32,), extras=(h0,),
                epilogue=lambda acc, res: (res + acc,), name="out_proj")
    h2 = _rms_fwd(x1, norm2_g, tr=tr, name="norm2")

    (act,) = _mm(h2, w_up, "nn", tm=tt, tn=D_FF, tk=d, out_dtypes=(BF16,),
                 epilogue=lambda acc: (jnp.square(jnp.maximum(acc, 0.0)),), name="mlp_up")
    dx2, dx2b, d_final_g, loss_tile = _mlp_down_loss(act, w_down, x1, final_norm_g, tgt_p, t_seq=t_seq, tr=tt,
                                                     name="mlp_down_loss")

    (dup,) = _mm(dx2b, w_down, "nt", tm=tt, tn=D_FF, tk=d, out_dtypes=(BF16,), extras=(act,),
                 epilogue=lambda acc, a: (acc * (2.0 * jnp.sqrt(a.astype(F32))),), name="mlp_down_bwd")
    (d_w_down,) = _mm(act, dx2b, "tn", tm=D_FF // 2, tn=d, tk=tr, out_dtypes=(F32,), name="w_down_grad")
    (d_w_up,) = _mm(h2, dup, "tn", tm=d, tn=D_FF // 2, tk=tr, out_dtypes=(F32,), name="w_up_grad")
    mlp_sm = [d_w_up.reshape(d, N_CHIPS, D_FF // N_CHIPS).transpose(1, 0, 2), d_w_down.reshape(N_CHIPS, D_FF // N_CHIPS, d)]
    ride1 = _SiblingHalvesRider(mlp_sm) if where is not None else None
    dx1, dx1b, d_norm2_g, theirs = _mlp_up_bwd_norm(dup, w_up, x1, norm2_g, dx2, tr=tt, name="mlp_up_bwd_norm",
                                                    rider=ride1)
    ride2 = None
    if where is not None:
        mlp_pair = [_pair_sum(where, a, b, name=f"pair_sum_mlp{k}") for k, (a, b) in enumerate(zip(mlp_sm, theirs))]
        ride2 = _ChipExchangeRider(mlp_pair)

    (dmix,) = _mm(dx1b, w_out, "nt", tm=tr, tn=d, tk=d, out_dtypes=(BF16,), name="out_proj_bwd")
    (d_w_out,) = _mm(mix, dx1b, "tn", tm=d, tn=d, tk=tr, out_dtypes=(F32,), name="w_out_grad")
    do_dn, ddz, do_gla, dgr, d_dn_norm_g, d_gla_norm_g = _gnorm_bwd(
        dmix, o_dn, o_gla, projp, dn_norm_g, gla_norm_g, tr=tr, name="gated_norm_bwd")
    du, dw, dqg, dkd, dgl = _dn_scan_bwd(do_dn, w, qg, kd, vn, pmat, gl, hist, bsz=bsz, nc_seq=nc_seq,
                                          name="dn_scan_bwd")
    dqn, dkn, dv, dsa, d_alog, d_dtb, mlp_parts = _dn_intra_bwd(
        qn, kn, v, gates, alog_row, dtb_row, u, w, tmat, du, dw, dqg, dkd, do_dn, vn, dgl, nc_seq=nc_seq,
        name="dn_intra_bwd", rider=ride2)
    dz, d_conv_w = _dnprep_bwd_a(projp, conv_w, dqn, dkn, dv, bsz=bsz, t_seq=t_seq, tt=tt, name="dn_prep_bwd")
    dcin = _dnprep_bwd_b(dz, conv_w, bsz=bsz, t_seq=t_seq, tt=tt, name="conv_bwd")
    gdqg, gdkd, gdvi, gdgl = _gla_scan_bwd(do_gla, gqg, gkd, ggl, projp, ghist, bsz=bsz, nc_seq=nc_seq,
                                            name="gla_scan_bwd")
    dgqk, dgv, dsb, d_w2p, d_gla_b = _gla_intra_bwd(projp, gates, w2p, gla_b, do_gla, gdqg, gdkd, gdvi, gdgl,
                                                    nc_seq=nc_seq, name="gla_intra_bwd")

    secs = (dcin, ddz, dgqk, dgv, dgr, dsa, dsb)
    g_lo = _grad_tn(h, secs[0:2], tk=tr, name="w_in_grad_lo")
    g_hi = _grad_tn(h, secs[2:7], tk=tr, name="w_in_grad_hi")
    dh0, d_norm1_g = _inproj_bwd(secs, wp, h0, norm1_g, dx1, tr=tt, name="in_proj_bwd")
    dh0 = dh0.reshape(bsz, t_seq, d)
    grad_x = dh0[:, CHUNK:]
    d_meta_rows = dh0[:, N_PAD:CHUNK].reshape(bsz * N_META, d)

    grads = dict(w_in_lo=g_lo, w_in_hi=g_hi, w_out=d_w_out, w_up=d_w_up, w_down=d_w_down, meta_rows=d_meta_rows,
                 norm1_g=d_norm1_g, conv_w=d_conv_w, a_log_tile=d_alog, dt_bias_tile=d_dtb, dn_norm_g=d_dn_norm_g,
                 gla_w2=d_w2p[0:GLA_RANK], gla_b=d_gla_b, gla_norm_g=d_gla_norm_g, norm2_g=d_norm2_g,
                 final_norm_g=d_final_g, loss_tile=loss_tile)
    if where is not None:
        grads["mlp_exchanged"] = (mlp_pair, mlp_parts)
    return grad_x, grads


SHARD_W = IN_WIDTH // N_CHIPS
PADDED_ORDER = ((0, 2048), (2056, 3592), (2048, 2056), LANE - 8, (3592, 3608), LANE - GLA_RANK)


def _pad_layout(w_full):
    pieces = [jnp.zeros((w_full.shape[0], seg), w_full.dtype) if isinstance(seg, int) else w_full[:, seg[0]:seg[1]]
              for seg in PADDED_ORDER]
    return jnp.concatenate(pieces, axis=1)


def _padded_from_shards(stack):
    pieces = []
    for seg in PADDED_ORDER:
        if isinstance(seg, int):
            pieces.append(jnp.zeros((stack.shape[1], seg), stack.dtype))
            continue
        for j in range(N_CHIPS):
            lo, hi = max(seg[0], j * SHARD_W), min(seg[1], (j + 1) * SHARD_W)
            if lo < hi:
                pieces.append(stack[j, :, lo - j * SHARD_W:hi - j * SHARD_W])
    return jnp.concatenate(pieces, axis=1)


def _shards_from_padded(g_lo, g_hi):
    split = g_lo.shape[1]
    starts, pos = [], 0
    for seg in PADDED_ORDER:
        width = seg if isinstance(seg, int) else seg[1] - seg[0]
        if not isinstance(seg, int):
            starts.append((seg[0], seg[1], pos))
        pos += width
    shards = []
    for j in range(N_CHIPS):
        pieces = []
        for a, b, p0 in sorted(starts):
            lo, hi = max(a, j * SHARD_W), min(b, (j + 1) * SHARD_W)
            if lo < hi:
                src, off = (g_lo, 0) if p0 < split else (g_hi, split)
                pieces.append(src[:, p0 + lo - a - off:p0 + hi - a - off])
        pieces.append(jnp.zeros((g_lo.shape[0], D_MODEL - SHARD_W), g_lo.dtype))
        shards.append(jnp.concatenate(pieces, axis=1))
    return jnp.stack(shards)


def _pack_small(g, bsz):
    assert bsz * N_META == 32
    row = jnp.concatenate([g["a_log_tile"], g["dt_bias_tile"], g["dn_norm_g"], g["gla_norm_g"], g["gla_b"],
                           g["loss_tile"], jnp.zeros((1, LANE), F32)], axis=1)
    return jnp.concatenate([g["meta_rows"], g["norm1_g"], g["conv_w"].reshape(6, D_MODEL), row,
                            g["gla_w2"].reshape(4, D_MODEL), g["norm2_g"], g["final_norm_g"],
                            jnp.zeros((2, D_MODEL), F32)], axis=0)


def kernel(x, meta_tokens, norm1_g, w_in, conv_w, a_log, dt_bias, dn_norm_g, gla_w2, gla_b, gla_norm_g, w_out, norm2_g, w_up, w_down, final_norm_g, loss_target, m_meta_tokens, m_norm1_g, m_w_in, m_conv_w, m_a_log, m_dt_bias, m_dn_norm_g, m_gla_w2, m_gla_b, m_gla_norm_g, m_w_out, m_norm2_g, m_w_up, m_w_down, m_final_norm_g, v_meta_tokens, v_norm1_g, v_w_in, v_conv_w, v_a_log, v_dt_bias, v_dn_norm_g, v_gla_w2, v_gla_b, v_gla_norm_g, v_w_out, v_norm2_g, v_w_up, v_w_down, v_final_norm_g):
    bsz = x.shape[0]
    chip = 2 * lax.axis_index("x") + lax.axis_index("y")

    lane_pad = lambda a, wd: jnp.pad(a, ((0, 0), (0, wd - a.shape[1])))
    where = jnp.stack([lax.axis_index("c"), chip]).astype(jnp.int32)
    slot = lambda a, dt, nm: _to_slot(where, a, dt, name="slot_" + nm)
    early = _GatherRider([slot(lane_pad(w_in[0], D_MODEL), BF16, "w_in"), slot(meta_tokens, F32, "meta"),
                          slot(conv_w[0], F32, "conv"), slot(lane_pad(gla_w2[0], LANE), F32, "gla_w2")],
                         [True, False, False, False])
    g_in, g_meta, g_conv, g_w2 = _exchange_now(early, name="gather_early")
    late = (_GatherRider([slot(w_out[0], BF16, "w_out"), slot(w_up[0], BF16, "w_up")], [True, True]),
            _GatherRider([slot(w_down[0], BF16, "w_down")], [True]))
    wp = _padded_from_shards(g_in)
    meta_f = g_meta.transpose(1, 0, 2).reshape(N_META, D_MODEL)
    conv_f = g_conv.transpose(1, 0, 2).reshape(4, QKV_W)
    w2_f = g_w2[:, :, 0:GQ_W // N_CHIPS].transpose(1, 0, 2).reshape(GLA_RANK, GQ_W)

    grad_x, g = _local_step(x, loss_target, meta_f, norm1_g, wp, conv_f, a_log, dt_bias, dn_norm_g, w2_f, gla_b,
                            gla_norm_g, None, norm2_g, None, None, final_norm_g.reshape(1, D_MODEL), late_gather=late, where=where)

    shard_major = [_shards_from_padded(g["w_in_lo"], g["w_in_hi"]), g["w_out"].reshape(N_CHIPS, D_MODEL // N_CHIPS, D_MODEL)]
    theirs = _exchange_now(_SiblingHalvesRider(shard_major), name="sibling_halves")
    pair = [_pair_sum(where, a, b, name=f"pair_sum_{k}") for k, (a, b) in enumerate(zip(shard_major, theirs))]
    parts = _exchange_now(_ChipExchangeRider(pair), name="chip_exchange")
    mlp_pair, mlp_parts = g["mlp_exchanged"]
    halves = [_chip_sum(where, p, q, name=f"chip_sum_{k}")
              for k, (p, q) in enumerate(zip(pair + mlp_pair, list(parts) + list(mlp_parts)))]
    gw_in, gw_out, gw_up, gw_down = _sibling_join(halves)

    red = _small_allreduce(_pack_small(g, bsz))
    g_meta_full = red[0:N_META]
    g_norm1 = red[32:33]
    g_conv_full = red[33:39].reshape(4, QKV_W)
    srow = red[39:40]
    g_alog, g_dtb = srow[:, 4:8], srow[:, LANE + 4:LANE + 8]
    g_dn_norm, g_gla_norm = srow[:, 2 * LANE:3 * LANE], srow[:, 3 * LANE:4 * LANE]
    g_gla_b = srow[:, 4 * LANE:6 * LANE]
    loss = srow[0, 6 * LANE]
    g_w2_full = red[40:44].reshape(GLA_RANK, GQ_W)
    g_norm2 = red[44:45]
    g_final = red[45:46]
    g_meta_sh = lax.dynamic_slice_in_dim(g_meta_full, chip * (D_MODEL // N_CHIPS), D_MODEL // N_CHIPS, axis=1)
    g_conv_sh = lax.dynamic_slice_in_dim(g_conv_full, chip * (QKV_W // N_CHIPS), QKV_W // N_CHIPS, axis=1)
    g_w2_sh = lax.dynamic_slice_in_dim(g_w2_full, chip * (GQ_W // N_CHIPS), GQ_W // N_CHIPS, axis=1)

    names = ["meta_tokens", "norm1_g", "w_in", "conv_w", "a_log", "dt_bias", "dn_norm_g", "gla_w2", "gla_b",
             "gla_norm_g", "w_out", "norm2_g", "w_up", "w_down", "final_norm_g"]
    weights = dict(meta_tokens=meta_tokens, norm1_g=norm1_g, w_in=w_in, conv_w=conv_w, a_log=a_log, dt_bias=dt_bias,
                   dn_norm_g=dn_norm_g, gla_w2=gla_w2, gla_b=gla_b, gla_norm_g=gla_norm_g, w_out=w_out,
                   norm2_g=norm2_g, w_up=w_up, w_down=w_down, final_norm_g=final_norm_g)
    ms = dict(meta_tokens=m_meta_tokens, norm1_g=m_norm1_g, w_in=m_w_in, conv_w=m_conv_w, a_log=m_a_log,
              dt_bias=m_dt_bias, dn_norm_g=m_dn_norm_g, gla_w2=m_gla_w2, gla_b=m_gla_b, gla_norm_g=m_gla_norm_g,
              w_out=m_w_out, norm2_g=m_norm2_g, w_up=m_w_up, w_down=m_w_down, final_norm_g=m_final_norm_g)
    vs = dict(meta_tokens=v_meta_tokens, norm1_g=v_norm1_g, w_in=v_w_in, conv_w=v_conv_w, a_log=v_a_log,
              dt_bias=v_dt_bias, dn_norm_g=v_dn_norm_g, gla_w2=v_gla_w2, gla_b=v_gla_b, gla_norm_g=v_gla_norm_g,
              w_out=v_w_out, norm2_g=v_norm2_g, w_up=v_w_up, w_down=v_w_down, final_norm_g=v_final_norm_g)
    grads2d = dict(meta_tokens=g_meta_sh, norm1_g=g_norm1, w_in=gw_in, conv_w=g_conv_sh, a_log=g_alog, dt_bias=g_dtb,
                   dn_norm_g=g_dn_norm, gla_w2=g_w2_sh, gla_b=g_gla_b, gla_norm_g=g_gla_norm, w_out=gw_out,
                   norm2_g=g_norm2, w_up=gw_up, w_down=gw_down, final_norm_g=g_final)
    out_g, out_d, out_m, out_v = [], [], [], []
    for nm in names:
        shape = weights[nm].shape
        g2 = grads2d[nm]
        if len(shape) == 3:
            res = _adamw(weights[nm], g2, ms[nm], vs[nm], name=f"adamw_{nm}", emit_grad=nm == "w_in")
            gout = res[3] if nm == "w_in" else g2.reshape(shape)
        else:
            as2d = lambda a: a.reshape(g2.shape)
            res = _adamw(as2d(weights[nm]), g2, as2d(ms[nm]), as2d(vs[nm]), name=f"adamw_{nm}")
            gout = g2.reshape(shape)
        out_g.append(gout)
        out_d.append(res[0].reshape(shape))
        out_m.append(res[1].reshape(shape))
        out_v.append(res[2].reshape(shape))
    return (loss, grad_x, *out_g, *out_d, *out_m, *out_v)
```

```python
import functools

import jax
import jax.numpy as jnp
import numpy as np
from jax import lax
from jax.experimental import pallas as pl
from jax.experimental.pallas import tpu as pltpu

F32 = jnp.float32
BF16 = jnp.bfloat16
HI = lax.Precision.HIGHEST
MESH = pl.DeviceIdType.MESH

D_MODEL = 1024
N_META = 16
CHUNK = 64
N_PAD = CHUNK - N_META
NH = 4
DN_D = 128
GLA_DK = 64
GLA_DV = 128
GLA_RANK = 16
D_FF = 4 * D_MODEL
EPS = 1e-6
IN_WIDTH = 3608
C_QKV, C_DZ, C_GQK, C_GV, C_GR, C_SA, C_SB, PW = 0, 1536, 2048, 2560, 3072, 3584, 3712, 3840
LANE = 128
N_CHIPS = 4

ADAM_LR, ADAM_B1, ADAM_B2, ADAM_EPS, ADAM_WD, ADAM_STEP = 0.001, 0.9, 0.999, 1e-08, 0.01, 10

VMEM_BIG = 56 * 1024 * 1024


def _cp(vmem=None, sem=None):
    kw = {}
    if vmem is not None:
        kw["vmem_limit_bytes"] = vmem
    if sem is not None:
        kw["dimension_semantics"] = sem
    return pltpu.CompilerParams(**kw)


def _tile(n, target, mult=16):
    best = None
    for t in range(mult, min(n, target) + 1, mult):
        if n % t == 0:
            best = t
    assert best is not None, (n, target)
    return best


def _dot(a, b, dims, prec=None):
    return lax.dot_general(a, b, (dims, ((), ())), preferred_element_type=F32, precision=prec)


def _nn(a, b):
    return _dot(a.astype(BF16), b.astype(BF16), ((1,), (0,)))


def _nt(a, b):
    return _dot(a.astype(BF16), b.astype(BF16), ((1,), (1,)))


def _tn(a, b):
    return _dot(a.astype(BF16), b.astype(BF16), ((0,), (0,)))


def _tri_sum(tri, x):
    t = tri.astype(BF16)
    hi = x.astype(BF16)
    r1 = x - hi.astype(F32)
    mid = r1.astype(BF16)
    lo = (r1 - mid.astype(F32)).astype(BF16)
    nn = ((1,), (0,))
    return _dot(t, hi, nn) + _dot(t, mid, nn) + _dot(t, lo, nn)


def _sigmoid(x):
    return 0.5 * jnp.tanh(0.5 * x) + 0.5


def _softplus(x):
    return jnp.maximum(x, 0.0) + jnp.log(1.0 + jnp.exp(-jnp.abs(x)))


def _logsigmoid(x):
    return -_softplus(-x)


def _iota2(shape, dim):
    return lax.broadcasted_iota(jnp.int32, shape, dim)


def _mm(a, b, mode, *, tm, tn, tk, out_dtypes, extras=(), epilogue=None, name, vmem=VMEM_BIG, rider=None,
        out_widths=None):
    if mode == "tn":
        K, M = a.shape
    else:
        M, K = a.shape
    N = b.shape[0] if mode == "nt" else b.shape[1]
    assert M % tm == 0 and N % tn == 0 and K % tk == 0, (name, M, N, K, tm, tn, tk)
    nk = K // tk
    n_ex, n_out = len(extras), len(out_dtypes)
    if mode == "tn":
        a_spec = pl.BlockSpec((tk, tm), lambda i, j, k: (k, i))
    else:
        a_spec = pl.BlockSpec((tm, tk), lambda i, j, k: (i, k))
    if mode == "nt":
        b_spec = pl.BlockSpec((tn, tk), lambda i, j, k: (j, k))
    else:
        b_spec = pl.BlockSpec((tk, tn), lambda i, j, k: (k, j))
    mn_spec = pl.BlockSpec((tm, tn), lambda i, j, k: (i, j))
    if out_widths is None:
        o_specs = [mn_spec] * n_out
        o_shapes = [jax.ShapeDtypeStruct((M, N), dt) for dt in out_dtypes]
    else:
        assert tn == N
        o_specs = [pl.BlockSpec((tm, wd), lambda i, j, k: (i, 0)) for wd in out_widths]
        o_shapes = [jax.ShapeDtypeStruct((M, wd), dt) for wd, dt in zip(out_widths, out_dtypes)]
    dims = {"nn": ((1,), (0,)), "nt": ((1,), (1,)), "tn": ((0,), (0,))}[mode]

    single = nk == 1
    direct = (not single) and epilogue is None and n_out == 1 and out_dtypes[0] == F32

    def body(*refs):
        a_ref, b_ref = refs[0], refs[1]
        ex_refs = refs[2:2 + n_ex]
        out_refs = refs[2 + n_ex:2 + n_ex + n_out]
        part = _dot(a_ref[...].astype(BF16), b_ref[...].astype(BF16), dims)

        def finish(acc):
            res = (acc,) if epilogue is None else epilogue(acc, *[e[...] for e in ex_refs])
            for o_ref, r in zip(out_refs, res):
                o_ref[...] = r.astype(o_ref.dtype)

        if single:
            finish(part)
            return
        acc_ref = out_refs[0] if direct else refs[2 + n_ex + n_out]
        k = pl.program_id(2)

        @pl.when(k == 0)
        def _():
            acc_ref[...] = part

        @pl.when(k > 0)
        def _():
            acc_ref[...] += part

        if not direct:
            @pl.when(k == nk - 1)
            def _():
                finish(acc_ref[...])

    outs, ridden = _hosted_call(
        body, rider, name=name, grid=(M // tm, N // tn, nk),
        in_specs=[a_spec, b_spec] + [mn_spec] * n_ex,
        out_specs=o_specs, out_shape=o_shapes,
        scratch_shapes=[] if (single or direct) else [pltpu.VMEM((tm, tn), F32)],
        compiler_params=_cp(vmem, ("parallel", "parallel", "arbitrary")), args=(a, b, *extras))
    return tuple(outs) if rider is None else (tuple(outs), ridden)


def _grad_tn(a, secs, *, tk, name):
    kk, m = a.shape
    widths = [s.shape[1] for s in secs]
    total = sum(widths)
    nk = kk // tk

    def body(*refs):
        a_ref, sec_refs, o_ref = refs[0], refs[1:-1], refs[-1]
        cat = sec_refs[0][...] if len(sec_refs) == 1 else jnp.concatenate([s[...] for s in sec_refs], axis=1)
        part = _dot(a_ref[...].astype(BF16), cat.astype(BF16), ((0,), (0,)))
        k = pl.program_id(0)

        @pl.when(k == 0)
        def _():
            o_ref[...] = part

        @pl.when(k > 0)
        def _():
            o_ref[...] += part

    return pl.pallas_call(
        body, name=name, grid=(nk,),
        in_specs=[pl.BlockSpec((tk, m), lambda k: (k, 0))] + [pl.BlockSpec((tk, w), lambda k: (k, 0)) for w in widths],
        out_specs=pl.BlockSpec((m, total), lambda k: (0, 0)),
        out_shape=jax.ShapeDtypeStruct((m, total), F32),
        compiler_params=_cp(VMEM_BIG, ("arbitrary",)),
    )(a, *secs)


def _rms_fwd(x, g, *, tr, name):
    n, d = x.shape

    def body(x_ref, g_ref, o_ref):
        xv = x_ref[...]
        r = lax.rsqrt(jnp.mean(xv * xv, axis=-1, keepdims=True) + EPS)
        o_ref[...] = (xv * r * g_ref[...]).astype(o_ref.dtype)

    return pl.pallas_call(
        body, name=name, grid=(n // tr,),
        in_specs=[pl.BlockSpec((tr, d), lambda i: (i, 0)), pl.BlockSpec((1, d), lambda i: (0, 0))],
        out_specs=pl.BlockSpec((tr, d), lambda i: (i, 0)),
        out_shape=jax.ShapeDtypeStruct((n, d), BF16),
        compiler_params=_cp(VMEM_BIG),
    )(x, g)


def _rms_bwd_math(xv, g, dy):
    r = lax.rsqrt(jnp.mean(xv * xv, axis=-1, keepdims=True) + EPS)
    xh = xv * r
    gdy = dy * g
    dx = r * (gdy - xh * jnp.mean(xh * gdy, axis=-1, keepdims=True))
    return dx, jnp.sum(dy * xh, axis=0, keepdims=True)


def _mlp_up_bwd_norm(dup, w_up, x, g, res, *, tr, name, rider=None):
    n, d = x.shape
    ff = dup.shape[1]

    def body(dup_ref, w_ref, x_ref, g_ref, res_ref, o_ref, ob_ref, dg_ref):
        dh = _nt(dup_ref[...], w_ref[...])
        dx, dg = _rms_bwd_math(x_ref[...], g_ref[...], dh)
        tot = res_ref[...] + dx
        o_ref[...] = tot
        ob_ref[...] = tot.astype(BF16)

        @pl.when(pl.program_id(0) == 0)
        def _():
            dg_ref[...] = dg

        @pl.when(pl.program_id(0) > 0)
        def _():
            dg_ref[...] += dg

    row = pl.BlockSpec((tr, d), lambda i: (i, 0))
    vec = pl.BlockSpec((1, d), lambda i: (0, 0))
    outs, ridden = _hosted_call(
        body, rider, name=name, grid=(n // tr,),
        in_specs=[pl.BlockSpec((tr, ff), lambda i: (i, 0)), pl.BlockSpec((d, ff), lambda i: (0, 0)), row, vec, row],
        out_specs=[row, row, vec],
        out_shape=[jax.ShapeDtypeStruct((n, d), F32), jax.ShapeDtypeStruct((n, d), BF16),
                   jax.ShapeDtypeStruct((1, d), F32)],
        scratch_shapes=[], compiler_params=_cp(VMEM_BIG, ("arbitrary",)), args=(dup, w_up, x, g, res))
    return (*outs, ridden)


def _mlp_down_loss(act, w_down, x1, gf, tgt, *, t_seq, tr, name):
    n, d = x1.shape
    ff = act.shape[1]
    per_seq = t_seq // tr

    def body(a_ref, w_ref, x_ref, g_ref, t_ref, dx_ref, dxb_ref, dg_ref, loss_ref):
        i = pl.program_id(0)
        xv = x_ref[...] + _nn(a_ref[...], w_ref[...])
        g = g_ref[...]
        r = lax.rsqrt(jnp.mean(xv * xv, axis=-1, keepdims=True) + EPS)
        xh = xv * r
        pos = (i % per_seq) * tr + _iota2((tr, 1), 0)
        real = pos >= CHUNK
        err = jnp.where(real, xh * g - t_ref[...], 0.0)
        dy = err * (1.0 / d)
        gdy = dy * g
        dx = r * (gdy - xh * jnp.mean(xh * gdy, axis=-1, keepdims=True))
        dx_ref[...] = dx
        dxb_ref[...] = dx.astype(BF16)
        dg = jnp.sum(dy * xh, axis=0, keepdims=True)
        ls = 0.5 * jnp.sum(jnp.mean(err * err, axis=-1, keepdims=True), axis=0, keepdims=True)
        ls = jnp.where(_iota2((1, LANE), 1) == 0, ls, 0.0)

        @pl.when(i == 0)
        def _():
            dg_ref[...] = dg
            loss_ref[...] = ls

        @pl.when(i > 0)
        def _():
            dg_ref[...] += dg
            loss_ref[...] += ls

    row = pl.BlockSpec((tr, d), lambda i: (i, 0))
    vec = pl.BlockSpec((1, d), lambda i: (0, 0))
    one = pl.BlockSpec((1, LANE), lambda i: (0, 0))
    return pl.pallas_call(
        body, name=name, grid=(n // tr,),
        in_specs=[pl.BlockSpec((tr, ff), lambda i: (i, 0)), pl.BlockSpec((ff, d), lambda i: (0, 0)), row, vec, row],
        out_specs=[row, row, vec, one],
        out_shape=[jax.ShapeDtypeStruct((n, d), F32), jax.ShapeDtypeStruct((n, d), BF16),
                   jax.ShapeDtypeStruct((1, d), F32), jax.ShapeDtypeStruct((1, LANE), F32)],
        compiler_params=_cp(VMEM_BIG, ("arbitrary",)),
    )(act, w_down, x1, gf, tgt)


def _gnorm_fwd(o_dn, o_gla, projp, g_dn, g_gla, *, tr, name):
    n = o_dn.shape[0]
    w = NH * DN_D

    def body(odn_ref, ogl_ref, z_ref, r_ref, gdn_ref, ggl_ref, mix_ref):
        for grp, (o_ref, gate_ref, gain_ref) in enumerate(((odn_ref, z_ref, gdn_ref), (ogl_ref, r_ref, ggl_ref))):
            gain = gain_ref[...]
            for h in range(NH):
                sl = slice(h * DN_D, (h + 1) * DN_D)
                o = o_ref[:, sl].astype(F32)
                z = gate_ref[:, sl].astype(F32)
                r = lax.rsqrt(jnp.mean(o * o, axis=-1, keepdims=True) + EPS)
                y = (o * r * gain) * (z * _sigmoid(z))
                mix_ref[:, grp * w + h * DN_D: grp * w + (h + 1) * DN_D] = y.astype(mix_ref.dtype)

    row = pl.BlockSpec((tr, w), lambda i: (i, 0))
    vec = pl.BlockSpec((1, DN_D), lambda i: (0, 0))
    return pl.pallas_call(
        body, name=name, grid=(n // tr,),
        in_specs=[row, row, pl.BlockSpec((tr, w), lambda i: (i, C_DZ // w)),
                  pl.BlockSpec((tr, w), lambda i: (i, C_GR // w)), vec, vec],
        out_specs=pl.BlockSpec((tr, 2 * w), lambda i: (i, 0)),
        out_shape=jax.ShapeDtypeStruct((n, 2 * w), BF16),
        compiler_params=_cp(VMEM_BIG),
    )(o_dn, o_gla, projp, projp, g_dn, g_gla)


def _gnorm_bwd(dmix, o_dn, o_gla, projp, g_dn, g_gla, *, tr, name):
    n = o_dn.shape[0]
    w = NH * DN_D

    def body(dm_ref, odn_ref, ogl_ref, z_ref, r_ref, gdn_ref, ggl_ref,
             dodn_ref, ddz_ref, dogl_ref, dgr_ref, dgdn_ref, dggl_ref):
        first = pl.program_id(0) == 0
        groups = ((odn_ref, z_ref, gdn_ref, dodn_ref, ddz_ref, dgdn_ref),
                  (ogl_ref, r_ref, ggl_ref, dogl_ref, dgr_ref, dggl_ref))
        for grp, (o_ref, gate_ref, gain_ref, do_ref, dgate_ref, dgain_ref) in enumerate(groups):
            gain = gain_ref[...]
            dgain = jnp.zeros((1, DN_D), F32)
            for h in range(NH):
                sl = slice(h * DN_D, (h + 1) * DN_D)
                o = o_ref[:, sl].astype(F32)
                z = gate_ref[:, sl].astype(F32)
                dm = dm_ref[:, grp * w + h * DN_D: grp * w + (h + 1) * DN_D].astype(F32)
                r = lax.rsqrt(jnp.mean(o * o, axis=-1, keepdims=True) + EPS)
                oh = o * r
                s = _sigmoid(z)
                dn = dm * (z * s)
                dgate_ref[:, sl] = (dm * (oh * gain) * (s * (1.0 + z * (1.0 - s)))).astype(dgate_ref.dtype)
                gdn = dn * gain
                do_ref[:, sl] = (r * (gdn - oh * jnp.mean(oh * gdn, axis=-1, keepdims=True))).astype(do_ref.dtype)
                dgain = dgain + jnp.sum(dn * oh, axis=0, keepdims=True)

            @pl.when(first)
            def _():
                dgain_ref[...] = dgain

            @pl.when(jnp.logical_not(first))
            def _():
                dgain_ref[...] += dgain

    row = pl.BlockSpec((tr, w), lambda i: (i, 0))
    vec = pl.BlockSpec((1, DN_D), lambda i: (0, 0))
    big = jax.ShapeDtypeStruct((n, w), F32)
    gate = jax.ShapeDtypeStruct((n, w), BF16)
    small = jax.ShapeDtypeStruct((1, DN_D), F32)
    return pl.pallas_call(
        body, name=name, grid=(n // tr,),
        in_specs=[pl.BlockSpec((tr, 2 * w), lambda i: (i, 0)), row, row,
                  pl.BlockSpec((tr, w), lambda i: (i, C_DZ // w)), pl.BlockSpec((tr, w), lambda i: (i, C_GR // w)), vec, vec],
        out_specs=[row, row, row, row, vec, vec],
        out_shape=[gate, gate, gate, gate, small, small],
        compiler_params=_cp(VMEM_BIG),
    )(dmix, o_dn, o_gla, projp, projp, g_dn, g_gla)


QKV_W = 3 * NH * DN_D
HALO = 8


def _conv_z(xs_ref, cw_ref, tt):
    z = cw_ref[0:1, :] * xs_ref[pl.ds(HALO - 3, tt), :]
    for j in range(1, 4):
        z = z + cw_ref[j:j + 1, :] * xs_ref[pl.ds(HALO - 3 + j, tt), :]
    return z


def _dnprep_fwd(projp, conv_w, *, bsz, t_seq, tt, name):
    n = bsz * t_seq
    per_seq = t_seq // tt
    hw = NH * DN_D

    def body(x_ref, halo_ref, cw_ref, q_ref, k_ref, v_ref, xs_ref):
        i = pl.program_id(1)
        xs_ref[0:HALO, :] = jnp.where(i == 0, 0.0, halo_ref[...].astype(F32))
        xs_ref[HALO:HALO + tt, :] = x_ref[...].astype(F32)
        z = _conv_z(xs_ref, cw_ref, tt)
        a = z * _sigmoid(z)
        for grp, o_ref in enumerate((q_ref, k_ref)):
            for h in range(NH):
                ah = a[:, grp * hw + h * DN_D: grp * hw + (h + 1) * DN_D]
                rs = lax.rsqrt(jnp.sum(ah * ah, axis=-1, keepdims=True) + EPS)
                o_ref[:, h * DN_D:(h + 1) * DN_D] = (ah * rs).astype(o_ref.dtype)
        v_ref[...] = a[:, 2 * hw:3 * hw].astype(v_ref.dtype)

    def halo_map(b, i):
        return (jnp.maximum((b * t_seq + i * tt) // HALO - 1, 0), 0)

    out = pl.BlockSpec((tt, hw), lambda b, i: (b * per_seq + i, 0))
    sds = jax.ShapeDtypeStruct((n, hw), BF16)
    return pl.pallas_call(
        body, name=name, grid=(bsz, per_seq),
        in_specs=[pl.BlockSpec((tt, QKV_W), lambda b, i: (b * per_seq + i, 0)),
                  pl.BlockSpec((HALO, QKV_W), halo_map),
                  pl.BlockSpec((4, QKV_W), lambda b, i: (0, 0))],
        out_specs=[out, out, out], out_shape=[sds, sds, sds],
        scratch_shapes=[pltpu.VMEM((tt + HALO, QKV_W), F32)],
        compiler_params=_cp(VMEM_BIG),
    )(projp, projp, conv_w)


def _dnprep_bwd_a(projp, conv_w, dq, dk, dv, *, bsz, t_seq, tt, name):
    n = bsz * t_seq
    per_seq = t_seq // tt
    hw = NH * DN_D

    def body(x_ref, halo_ref, cw_ref, dq_ref, dk_ref, dv_ref, dz_ref, dcw_ref, xs_ref):
        b, i = pl.program_id(0), pl.program_id(1)
        xs_ref[0:HALO, :] = jnp.where(i == 0, 0.0, halo_ref[...].astype(F32))
        xs_ref[HALO:HALO + tt, :] = x_ref[...].astype(F32)
        z = _conv_z(xs_ref, cw_ref, tt)
        s = _sigmoid(z)
        a = z * s
        dsilu = s * (1.0 + z * (1.0 - s))
        for grp, d_ref in enumerate((dq_ref, dk_ref)):
            for h in range(NH):
                sl = slice(grp * hw + h * DN_D, grp * hw + (h + 1) * DN_D)
                ah = a[:, sl]
                rs = lax.rsqrt(jnp.sum(ah * ah, axis=-1, keepdims=True) + EPS)
                y = ah * rs
                dy = d_ref[:, h * DN_D:(h + 1) * DN_D]
                da = rs * (dy - y * jnp.sum(dy * y, axis=-1, keepdims=True))
                dz_ref[:, sl] = da * dsilu[:, sl]
        dz_ref[:, 2 * hw:3 * hw] = dv_ref[...] * dsilu[:, 2 * hw:3 * hw]
        dz = dz_ref[...]
        first = jnp.logical_and(b == 0, i == 0)
        for j in range(4):
            part = jnp.sum(dz * xs_ref[pl.ds(HALO - 3 + j, tt), :], axis=0, keepdims=True)

            @pl.when(first)
            def _():
                dcw_ref[j:j + 1, :] = part

            @pl.when(jnp.logical_not(first))
            def _():
                dcw_ref[j:j + 1, :] += part

    def halo_map(b, i):
        return (jnp.maximum((b * t_seq + i * tt) // HALO - 1, 0), 0)

    hrow = pl.BlockSpec((tt, hw), lambda b, i: (b * per_seq + i, 0))
    return pl.pallas_call(
        body, name=name, grid=(bsz, per_seq),
        in_specs=[pl.BlockSpec((tt, QKV_W), lambda b, i: (b * per_seq + i, 0)),
                  pl.BlockSpec((HALO, QKV_W), halo_map),
                  pl.BlockSpec((4, QKV_W), lambda b, i: (0, 0)), hrow, hrow, hrow],
        out_specs=[pl.BlockSpec((tt, QKV_W), lambda b, i: (b * per_seq + i, 0)),
                   pl.BlockSpec((4, QKV_W), lambda b, i: (0, 0))],
        out_shape=[jax.ShapeDtypeStruct((n, QKV_W), F32), jax.ShapeDtypeStruct((4, QKV_W), F32)],
        scratch_shapes=[pltpu.VMEM((tt + HALO, QKV_W), F32)],
        compiler_params=_cp(VMEM_BIG),
    )(projp, projp, conv_w, dq, dk, dv)


def _dnprep_bwd_b(dz, conv_w, *, bsz, t_seq, tt, name):
    n = bsz * t_seq
    per_seq = t_seq // tt
    last_blk = n // HALO - 1

    def body(dz_ref, halo_ref, cw_ref, dx_ref, ds_ref):
        i = pl.program_id(1)
        ds_ref[0:tt, :] = dz_ref[...].astype(F32)
        ds_ref[tt:tt + HALO, :] = jnp.where(i == per_seq - 1, 0.0, halo_ref[...].astype(F32))
        dx = cw_ref[0:1, :] * ds_ref[pl.ds(3, tt), :]
        for j in range(1, 4):
            dx = dx + cw_ref[j:j + 1, :] * ds_ref[pl.ds(3 - j, tt), :]
        dx_ref[...] = dx.astype(dx_ref.dtype)

    def halo_map(b, i):
        return (jnp.minimum((b * t_seq + (i + 1) * tt) // HALO, last_blk), 0)

    row = pl.BlockSpec((tt, QKV_W), lambda b, i: (b * per_seq + i, 0))
    return pl.pallas_call(
        body, name=name, grid=(bsz, per_seq),
        in_specs=[row, pl.BlockSpec((HALO, QKV_W), halo_map), pl.BlockSpec((4, QKV_W), lambda b, i: (0, 0))],
        out_specs=row, out_shape=jax.ShapeDtypeStruct((n, QKV_W), BF16),
        scratch_shapes=[pltpu.VMEM((tt + HALO, QKV_W), F32)],
        compiler_params=_cp(VMEM_BIG),
    )(dz, dz, conv_w)


def _masks64():
    r = _iota2((CHUNK, CHUNK), 0)
    c = _iota2((CHUNK, CHUNK), 1)
    return r, c


def _group(nc_seq, target=5):
    return max(g for g in range(1, target + 1) if nc_seq % g == 0)


def _round_robin(chains):
    live = list(chains)
    while live:
        nxt = []
        for ch in live:
            try:
                next(ch)
                nxt.append(ch)
            except StopIteration:
                pass
        live = nxt
        yield


def _run(chains):
    for _ in _round_robin(chains):
        pass


def _per_chunk(inner, kinds, grp):
    def body(*refs):
        chains = []
        for gi in range(grp):
            views = []
            for r, kind in zip(refs, kinds):
                if kind == "row":
                    views.append(r.at[pl.ds(gi * CHUNK, CHUNK)])
                elif kind == "lead":
                    views.append(r.at[pl.ds(gi, 1)])
                else:
                    views.append(r)
            chains.append(inner(gi, *views))
        _run(chains)
    return body


def _accumulate(ref, val, gi):
    if gi > 0:
        ref[...] += val
        return
    first = pl.program_id(0) == 0

    @pl.when(first)
    def _():
        ref[...] = val

    @pl.when(jnp.logical_not(first))
    def _():
        ref[...] += val


ANY = pl.BlockSpec(memory_space=pl.ANY)


def _place():
    return lax.axis_index("x"), lax.axis_index("y"), lax.axis_index("c")


def _other_chips(x, y):
    return [(1 - x, y, 2 * (1 - x) + y), (x, 1 - y, 2 * x + 1 - y), (1 - x, 1 - y, 2 * (1 - x) + 1 - y)]


class _GatherRider:
    def __init__(self, bufs, split):
        self.inputs = list(bufs)
        self.split = list(split)
        self.out_shapes = [jax.ShapeDtypeStruct(b.shape, b.dtype) for b in bufs]
        self.aliases = {i: i for i in range(len(bufs))}
        self.sems = [pltpu.SemaphoreType.DMA((len(bufs), 3))] * 4

    def _rows(self, k, buf, c, mine=True):
        r = buf.shape[1]
        if not self.split[k]:
            return pl.ds(0, r)
        return pl.ds((c if mine else 1 - c) * (r // 2), r // 2)

    def _ici(self, k, d, bufs, sems, c, px, py, block):
        rows = self._rows(k, bufs[k], c)
        return pltpu.make_async_remote_copy(
            src_ref=bufs[k].at[block, rows, :], dst_ref=bufs[k].at[block, rows, :], send_sem=sems[0].at[k, d],
            recv_sem=sems[1].at[k, d], device_id=(px, py, c), device_id_type=MESH)

    def _pass(self, k, d, bufs, sems, x, y, c, block, mine):
        rows = self._rows(k, bufs[k], c, mine)
        return pltpu.make_async_remote_copy(
            src_ref=bufs[k].at[block, rows, :], dst_ref=bufs[k].at[block, rows, :], send_sem=sems[2].at[k, d],
            recv_sem=sems[3].at[k, d], device_id=(x, y, 1 - c), device_id_type=MESH)

    def first(self, in_refs, bufs, sems):
        x, y, c = _place()
        for k in range(len(bufs)):
            for d, (px, py, _) in enumerate(_other_chips(x, y)):
                self._ici(k, d, bufs, sems, c, px, py, 2 * x + y).start()

    def last(self, in_refs, bufs, sems):
        x, y, c = _place()
        chips = _other_chips(x, y)
        for k in range(len(bufs)):
            for d, (px, py, pj) in enumerate(chips):
                self._ici(k, d, bufs, sems, c, px, py, pj).wait_recv()
                if self.split[k]:
                    self._pass(k, d, bufs, sems, x, y, c, pj, True).start()
        for k in range(len(bufs)):
            for d, (px, py, pj) in enumerate(chips):
                if self.split[k]:
                    self._pass(k, d, bufs, sems, x, y, c, pj, False).wait_recv()
                    self._pass(k, d, bufs, sems, x, y, c, pj, True).wait_send()
                self._ici(k, d, bufs, sems, c, px, py, 2 * x + y).wait_send()


def _hosted_call(body, rider, *, name, grid, in_specs, out_specs, out_shape, scratch_shapes, compiler_params, args):
    if rider is None:
        outs = pl.pallas_call(body, name=name, grid=grid, in_specs=in_specs, out_specs=out_specs, out_shape=out_shape,
                              scratch_shapes=scratch_shapes, compiler_params=compiler_params)(*args)
        return list(outs), []
    n_in, n_out, n_scr = len(in_specs), len(out_specs), len(scratch_shapes)
    r_in, r_out = len(rider.inputs), len(rider.out_shapes)
    compiler_params = _cp(compiler_params.vmem_limit_bytes, ("arbitrary",) * len(grid))

    def full_body(*refs):
        ins = refs[:n_in]
        rins = refs[n_in:n_in + r_in]
        outs = refs[n_in + r_in:n_in + r_in + n_out]
        routs = refs[n_in + r_in + n_out:n_in + r_in + n_out + r_out]
        rest = refs[n_in + r_in + n_out + r_out:]
        scr, sems = rest[:n_scr], rest[n_scr:]
        ids = [pl.program_id(a) for a in range(len(grid))]
        is_first = functools.reduce(jnp.logical_and, [i == 0 for i in ids])
        is_last = functools.reduce(jnp.logical_and, [i == g - 1 for i, g in zip(ids, grid)])

        @pl.when(is_first)
        def _():
            rider.first(rins, routs, sems)

        body(*ins, *outs, *scr)

        @pl.when(is_last)
        def _():
            rider.last(rins, routs, sems)

    res = pl.pallas_call(
        full_body, name=name, grid=grid, in_specs=list(in_specs) + [ANY] * r_in,
        out_specs=list(out_specs) + [ANY] * r_out, out_shape=list(out_shape) + list(rider.out_shapes),
        input_output_aliases={n_in + i: n_out + o for i, o in rider.aliases.items()},
        scratch_shapes=list(scratch_shapes) + list(rider.sems), compiler_params=compiler_params,
    )(*args, *rider.inputs)
    return list(res[:n_out]), list(res[n_out:])


def _exchange_now(rider, *, name):
    r_in = len(rider.inputs)

    def body(*refs):
        rins = refs[:r_in]
        routs = refs[r_in:r_in + len(rider.out_shapes)]
        sems = refs[r_in + len(rider.out_shapes):]
        rider.first(rins, routs, sems)
        rider.last(rins, routs, sems)

    return pl.pallas_call(
        body, name=name, in_specs=[ANY] * r_in, out_specs=[ANY] * len(rider.out_shapes), out_shape=list(rider.out_shapes),
        input_output_aliases=dict(rider.aliases), scratch_shapes=list(rider.sems),
    )(*rider.inputs)


def _to_slot(where, a, dtype, *, name):
    r, cols = a.shape
    tr = _tile(r, 256, 16) if r > 256 else r

    def body(w_ref, a_ref, o_ref):
        o_ref[0] = a_ref[...].astype(o_ref.dtype)

    return pl.pallas_call(
        body, name=name,
        grid_spec=pltpu.PrefetchScalarGridSpec(
            num_scalar_prefetch=1, grid=(r // tr,),
            in_specs=[pl.BlockSpec((tr, cols), lambda i, w: (i, 0))],
            out_specs=pl.BlockSpec((1, tr, cols), lambda i, w: (w[1], i, 0))),
        out_shape=jax.ShapeDtypeStruct((N_CHIPS, r, cols), dtype), compiler_params=_cp(VMEM_BIG),
    )(where, a)


def _tri_inv(a_strict):
    r, c = _masks64()
    eye = (r == c).astype(F32)
    blk16 = (r // 16) == (c // 16)
    blk32 = (r // 32) == (c // 32)
    ld = jnp.where(blk16, a_strict, 0.0)
    x = eye - ld
    p = _nn(ld, ld)
    yield
    for step in range(3):
        xp = _nn(x, p)
        if step < 2:
            p = _nn(p, p)
        x = x + xp
        yield
    for lk in (jnp.where(jnp.logical_and(blk32, jnp.logical_not(blk16)), a_strict, 0.0),
               jnp.where(blk32, 0.0, a_strict)):
        y = x - eye
        s = lk + _nn(y, lk)
        yield
        x = x - s - _nn(s, y)
        yield
    return x


def _dn_gates(sa, alog, dtb, chunk_in_seq):
    rows = _iota2((CHUNK, LANE), 0)
    valid = jnp.logical_or(rows >= N_PAD, chunk_in_seq > 0)
    beta_t = _sigmoid(sa)
    ea = jnp.exp(alog)
    g_t = jnp.where(valid, -ea * _softplus(sa + dtb), 0.0)
    r, c = _masks64()
    ltri = (r >= c).astype(F32)
    gam_t = _tri_sum(ltri, g_t)
    return beta_t, g_t, gam_t, valid, ea


def _dn_intra_fwd(qn, kn, v, projp, alog_row, dtb_row, *, nc_seq, name, rider=None):
    n = qn.shape[0]
    nct = n // CHUNK
    hw = NH * DN_D
    scale = DN_D ** -0.5

    grp = _group(nc_seq)

    def inner(gi, q_ref, k_ref, v_ref, sa_ref, al_ref, dt_ref, u_ref, w_ref, qg_ref, kd_ref, p_ref, t_ref, gl_ref):
        ci = (pl.program_id(0) * grp + gi) % nc_seq
        beta_t, _, gam_t, _, _ = _dn_gates(sa_ref[...], al_ref[...], dt_ref[...], ci)
        yield
        gam_tt = gam_t.T
        r, c = _masks64()
        incl = r >= c
        strict = r > c

        def head(h):
            sl = slice(h * DN_D, (h + 1) * DN_D)
            beta_w = jnp.broadcast_to(beta_t[:, h:h + 1], (CHUNK, DN_D))
            gam_w = jnp.broadcast_to(gam_t[:, 4 + h:5 + h], (CHUNK, DN_D))
            gam_row = gam_tt[4 + h:5 + h, :]
            gl = gam_t[CHUNK - 1:CHUNK, 4 + h:5 + h]
            dec = jnp.exp(jnp.where(incl, gam_w[:, 0:CHUNK] - gam_row, -jnp.inf))
            kh = k_ref[:, sl].astype(F32)
            qh = q_ref[:, sl].astype(F32) * scale
            vh = v_ref[:, sl].astype(F32)
            kk = _nt(kh, kh)
            qk = _nt(qh, kh)
            yield
            a = jnp.where(strict, beta_w[:, 0:CHUNK] * kk * dec, 0.0)
            tm = yield from _tri_inv(a)
            egam_w = jnp.exp(gam_w)
            u_ref[:, sl] = _nn(tm, beta_w * vh).astype(u_ref.dtype)
            w_ref[:, sl] = _nn(tm, (beta_w * egam_w) * kh).astype(w_ref.dtype)
            qg_ref[:, sl] = (egam_w * qh).astype(qg_ref.dtype)
            kd_ref[:, sl] = (jnp.exp(gl - gam_w) * kh).astype(kd_ref.dtype)
            p_ref[0, h] = qk * dec
            t_ref[0, h] = tm
            gl_ref[0, h:h + 1, :] = jnp.broadcast_to(jnp.exp(gl), (1, LANE))

        yield from _round_robin([head(h) for h in range(NH)])

    rows = grp * CHUNK
    row = pl.BlockSpec((rows, hw), lambda i: (i, 0))
    vec = pl.BlockSpec((1, LANE), lambda i: (0, 0))
    mat = pl.BlockSpec((grp, NH, CHUNK, CHUNK), lambda i: (i, 0, 0, 0))
    big = jax.ShapeDtypeStruct((n, hw), BF16)
    msd = jax.ShapeDtypeStruct((nct, NH, CHUNK, CHUNK), F32)
    kinds = ["row"] * 4 + ["whole"] * 2 + ["row"] * 4 + ["lead"] * 3
    outs, ridden = _hosted_call(
        _per_chunk(inner, kinds, grp), rider, name=name, grid=(nct // grp,),
        in_specs=[row, row, row, pl.BlockSpec((rows, LANE), lambda i: (i, 0)), vec, vec],
        out_specs=[row, row, row, row, mat, mat, pl.BlockSpec((grp, NH, LANE), lambda i: (i, 0, 0))],
        out_shape=[big, big, big, big, msd, msd, jax.ShapeDtypeStruct((nct, NH, LANE), F32)],
        scratch_shapes=[], compiler_params=_cp(VMEM_BIG, ("arbitrary",)),
        args=(qn, kn, v, projp, alog_row, dtb_row))
    return (*outs, ridden)


def _dn_scan_fwd(u, w, qg, kd, p, gl, *, bsz, nc_seq, name, rider=None):
    hw = NH * DN_D
    t_seq = nc_seq * CHUNK
    u, w, qg, kd = (z.reshape(bsz, t_seq, hw) for z in (u, w, qg, kd))
    p = p.reshape(bsz, nc_seq, NH, CHUNK, CHUNK)
    gl = gl.reshape(bsz, nc_seq, NH, LANE)
    grp = _group(nc_seq)

    def body(u_ref, w_ref, qg_ref, kd_ref, p_ref, gl_ref, o_ref, vn_ref, hist_ref, s_ref):
        @pl.when(pl.program_id(0) == 0)
        def _():
            s_ref[...] = jnp.zeros_like(s_ref)

        def chain(b, h, gi):
            sl = slice(h * DN_D, (h + 1) * DN_D)
            rows = pl.ds(gi * CHUNK, CHUNK)
            s = s_ref[b, h]
            hist_ref[b, gi, h] = s.astype(hist_ref.dtype)
            ws = _nn(w_ref[b, rows, sl], s)
            qs = _nn(qg_ref[b, rows, sl], s)
            yield
            vn = u_ref[b, rows, sl] - ws
            vn_ref[b, rows, sl] = vn.astype(vn_ref.dtype)
            o_ref[b, rows, sl] = (qs + _nn(p_ref[b, gi, h], vn)).astype(o_ref.dtype)
            s_ref[b, h] = gl_ref[b, gi, h:h + 1, :] * s + _tn(kd_ref[b, rows, sl], vn)

        for gi in range(grp):
            _run([chain(b, h, gi) for b in range(bsz) for h in range(NH)])

    row = pl.BlockSpec((bsz, grp * CHUNK, hw), lambda i: (0, i, 0))
    outs, ridden = _hosted_call(
        body, rider, name=name, grid=(nc_seq // grp,),
        in_specs=[row, row, row, row, pl.BlockSpec((bsz, grp, NH, CHUNK, CHUNK), lambda i: (0, i, 0, 0, 0)),
                  pl.BlockSpec((bsz, grp, NH, LANE), lambda i: (0, i, 0, 0))],
        out_specs=[row, row, pl.BlockSpec((bsz, grp, NH, DN_D, DN_D), lambda i: (0, i, 0, 0, 0))],
        out_shape=[jax.ShapeDtypeStruct((bsz, t_seq, hw), BF16), jax.ShapeDtypeStruct((bsz, t_seq, hw), BF16),
                   jax.ShapeDtypeStruct((bsz, nc_seq, NH, DN_D, DN_D), BF16)],
        scratch_shapes=[pltpu.VMEM((bsz, NH, DN_D, DN_D), F32)],
        compiler_params=_cp(VMEM_BIG, ("arbitrary",)), args=(u, w, qg, kd, p, gl))
    o, vn, hist = outs
    return o.reshape(bsz * t_seq, hw), vn.reshape(bsz * t_seq, hw), hist, ridden


def _dn_scan_bwd(do, w, qg, kd, vn, p, gl, hist, *, bsz, nc_seq, name):
    hw = NH * DN_D
    t_seq = nc_seq * CHUNK
    do, w, qg, kd, vn = (z.reshape(bsz, t_seq, hw) for z in (do, w, qg, kd, vn))
    p = p.reshape(bsz, nc_seq, NH, CHUNK, CHUNK)
    gl = gl.reshape(bsz, nc_seq, NH, LANE)
    grp = _group(nc_seq)

    def body(do_ref, w_ref, qg_ref, kd_ref, vn_ref, p_ref, gl_ref, hist_ref,
             du_ref, dw_ref, dqg_ref, dkd_ref, dgl_ref, ds_ref):
        @pl.when(pl.program_id(0) == 0)
        def _():
            ds_ref[...] = jnp.zeros_like(ds_ref)

        def chain(b, h, gi):
            sl = slice(h * DN_D, (h + 1) * DN_D)
            rows = pl.ds(gi * CHUNK, CHUNK)
            s = hist_ref[b, gi, h]
            dsn = ds_ref[b, h]
            doh = do_ref[b, rows, sl]
            vnh = vn_ref[b, rows, sl]
            kdh = kd_ref[b, rows, sl]
            dvn = _tn(p_ref[b, gi, h], doh) + _nn(kdh, dsn)
            du_ref[b, rows, sl] = dvn.astype(du_ref.dtype)
            dqg_ref[b, rows, sl] = _nt(doh, s).astype(dqg_ref.dtype)
            dkd_ref[b, rows, sl] = _nt(vnh, dsn).astype(dkd_ref.dtype)
            ds_part = _tn(qg_ref[b, rows, sl], doh) + gl_ref[b, gi, h:h + 1, :] * dsn
            dgl = jnp.sum(jnp.sum(dsn * s, axis=0, keepdims=True), axis=1, keepdims=True)
            dgl_ref[b, gi, h:h + 1, :] = jnp.broadcast_to(dgl, (1, LANE))
            yield
            dw_ref[b, rows, sl] = (-_nt(dvn, s)).astype(dw_ref.dtype)
            ds_ref[b, h] = ds_part - _tn(w_ref[b, rows, sl], dvn)

        for gi in reversed(range(grp)):
            _run([chain(b, h, gi) for b in range(bsz) for h in range(NH)])

    steps = nc_seq // grp
    rev = lambda i: steps - 1 - i
    row = pl.BlockSpec((bsz, grp * CHUNK, hw), lambda i: (0, rev(i), 0))
    mat = pl.BlockSpec((bsz, grp, NH, CHUNK, CHUNK), lambda i: (0, rev(i), 0, 0, 0))
    glb = pl.BlockSpec((bsz, grp, NH, LANE), lambda i: (0, rev(i), 0, 0))
    big = jax.ShapeDtypeStruct((bsz, t_seq, hw), BF16)
    outs = pl.pallas_call(
        body, name=name, grid=(steps,),
        in_specs=[row, row, row, row, row, mat, glb,
                  pl.BlockSpec((bsz, grp, NH, DN_D, DN_D), lambda i: (0, rev(i), 0, 0, 0))],
        out_specs=[row, row, row, row, glb],
        out_shape=[big, big, jax.ShapeDtypeStruct(big.shape, F32), jax.ShapeDtypeStruct(big.shape, F32),
                   jax.ShapeDtypeStruct((bsz, nc_seq, NH, LANE), F32)],
        scratch_shapes=[pltpu.VMEM((bsz, NH, DN_D, DN_D), F32)],
        compiler_params=_cp(VMEM_BIG, ("arbitrary",)),
    )(do, w, qg, kd, vn, p, gl, hist)
    du, dw, dqg, dkd, dgl = outs
    n = bsz * t_seq
    return (du.reshape(n, hw), dw.reshape(n, hw), dqg.reshape(n, hw), dkd.reshape(n, hw),
            dgl.reshape(bsz * nc_seq, NH, LANE))


def _dn_intra_bwd(qn, kn, v, projp, alog_row, dtb_row, u, w, tmat, du, dw, dqg, dkd, do, vn, dgl, *, nc_seq, name,
                  rider=None):
    n = qn.shape[0]
    nct = n // CHUNK
    hw = NH * DN_D
    scale = DN_D ** -0.5

    grp = _group(nc_seq)

    def inner(gi, q_ref, k_ref, v_ref, sa_ref, al_ref, dt_ref, u_ref, w_ref, t_ref, du_ref, dw_ref, dqg_ref, dkd_ref,
              do_ref, vn_ref, dgl_ref, dq_ref, dk_ref, dv_ref, dsa_ref, dal_ref, ddt_ref):
        ci = (pl.program_id(0) * grp + gi) % nc_seq
        sa = sa_ref[...]
        beta_t, g_t, gam_t, valid, ea = _dn_gates(sa, al_ref[...], dt_ref[...], ci)
        yield
        lane = _iota2((CHUNK, LANE), 1)
        gates_t = jnp.where(lane < 4, beta_t, gam_t).T
        r, c = _masks64()
        incl, strict, upper, supper = r >= c, r > c, r <= c, r < c
        rows1 = _iota2((CHUNK, 1), 0)
        acc = [jnp.zeros((CHUNK, LANE), F32)]

        def head(h):
            sl = slice(h * DN_D, (h + 1) * DN_D)
            beta_w = jnp.broadcast_to(beta_t[:, h:h + 1], (CHUNK, DN_D))
            gam_w = jnp.broadcast_to(gam_t[:, 4 + h:5 + h], (CHUNK, DN_D))
            beta_s, gam_s = beta_w[:, 0:CHUNK], gam_w[:, 0:CHUNK]
            beta_row = gates_t[h:h + 1, :]
            gam_row = gates_t[4 + h:5 + h, :]
            gl = gam_t[CHUNK - 1:CHUNK, 4 + h:5 + h]
            dec = jnp.exp(jnp.where(incl, gam_s - gam_row, -jnp.inf))
            dec_t = jnp.exp(jnp.where(upper, gam_row - gam_s, -jnp.inf))
            egam_w = jnp.exp(gam_w)
            ekd_w = jnp.exp(gl - gam_w)
            kh = k_ref[:, sl].astype(F32)
            qh = q_ref[:, sl].astype(F32) * scale
            vh = v_ref[:, sl].astype(F32)
            uh = u_ref[:, sl]
            wh = w_ref[:, sl]
            doh = do_ref[:, sl]
            vnh = vn_ref[:, sl]
            kk = _nt(kh, kh)
            qk = _nt(qh, kh)
            qk_t = _nt(kh, qh)
            dp = _nt(doh, vnh)
            dp_t = _nt(vnh, doh)
            tm_t = t_ref[0, h].T
            dvb = _nn(tm_t, du_ref[:, sl])
            dkg = _nn(tm_t, dw_ref[:, sl])
            yield
            m = _nt(dvb, uh) + _nt(dkg, wh)
            m_t = _nt(uh, dvb) + _nt(wh, dkg)
            yield
            da = jnp.where(strict, -m, 0.0)
            da_t = jnp.where(supper, -m_t, 0.0)
            a = jnp.where(strict, beta_s * kk * dec, 0.0)
            a_t = jnp.where(supper, beta_row * kk * dec_t, 0.0)
            dad = da * dec
            dad_t = da_t * dec_t
            dpm = jnp.where(incl, dp, 0.0)
            dpm_t = jnp.where(upper, dp_t, 0.0)
            e = da * a + dpm * (qk * dec)
            e_t = da_t * a_t + dpm_t * (qk_t * dec_t)
            dqgh = dqg_ref[:, sl].astype(F32)
            dkdh = dkd_ref[:, sl].astype(F32)
            bg_w = beta_w * egam_w
            dkh = (_nn(beta_s * dad, kh) + _nn(beta_row * dad_t, kh) + _nn(dpm_t * dec_t, qh)
                   + bg_w * dkg + ekd_w * dkdh)
            dqh = _nn(dpm * dec, kh) + egam_w * dqgh
            t_kd = dkdh * (ekd_w * kh)
            dbeta = (jnp.sum(dad * kk, axis=1, keepdims=True)
                     + jnp.sum(dkg * (egam_w * kh) + dvb * vh, axis=1, keepdims=True))
            dgam = (jnp.sum(e - e_t, axis=1, keepdims=True)
                    + jnp.sum(dkg * (bg_w * kh) + dqgh * (egam_w * qh) - t_kd, axis=1, keepdims=True))
            dgam_last = (jnp.sum(jnp.sum(t_kd, axis=0, keepdims=True), axis=1, keepdims=True)
                         + dgl_ref[0, h:h + 1, 0:1] * jnp.exp(gl))
            dgam = dgam + jnp.where(rows1 == CHUNK - 1, dgam_last, 0.0)
            dq_ref[:, sl] = (dqh * scale).astype(dq_ref.dtype)
            dk_ref[:, sl] = dkh.astype(dk_ref.dtype)
            dv_ref[:, sl] = (beta_w * dvb).astype(dv_ref.dtype)
            acc[0] = acc[0] + jnp.where(lane == h, dbeta, 0.0) + jnp.where(lane == 4 + h, dgam, 0.0)

        yield from _round_robin([head(h) for h in range(NH)])
        acc_t = acc[0]
        dg_t = _tri_sum(upper, acc_t)
        ddb = acc_t * beta_t * (1.0 - beta_t)
        dda = jnp.where(valid, dg_t * (-ea) * _sigmoid(sa + dt_ref[...]), 0.0)
        dsa_ref[...] = jnp.where(lane < 4, ddb, jnp.where(lane < 8, dda, 0.0)).astype(dsa_ref.dtype)
        in_g = jnp.logical_and(lane >= 4, lane < 8)
        dal = jnp.sum(jnp.where(in_g, dg_t * g_t, 0.0), axis=0, keepdims=True)
        ddt = jnp.sum(jnp.where(in_g, dda, 0.0), axis=0, keepdims=True)
        _accumulate(dal_ref, dal, gi)
        _accumulate(ddt_ref, ddt, gi)

    rows = grp * CHUNK
    row = pl.BlockSpec((rows, hw), lambda i: (i, 0))
    vec = pl.BlockSpec((1, LANE), lambda i: (0, 0))
    mat = pl.BlockSpec((grp, NH, CHUNK, CHUNK), lambda i: (i, 0, 0, 0))
    glb = pl.BlockSpec((grp, NH, LANE), lambda i: (i, 0, 0))
    big = jax.ShapeDtypeStruct((n, hw), F32)
    v128 = jax.ShapeDtypeStruct((1, LANE), F32)
    kinds = (["row"] * 4 + ["whole"] * 2 + ["row"] * 2 + ["lead"] + ["row"] * 6 + ["lead"]
             + ["row"] * 4 + ["whole"] * 2)
    outs, ridden = _hosted_call(
        _per_chunk(inner, kinds, grp), rider, name=name, grid=(nct // grp,),
        in_specs=[row, row, row, pl.BlockSpec((rows, LANE), lambda i: (i, 0)), vec, vec,
                  row, row, mat, row, row, row, row, row, row, glb],
        out_specs=[row, row, row, pl.BlockSpec((rows, LANE), lambda i: (i, 0)), vec, vec],
        out_shape=[big, big, big, jax.ShapeDtypeStruct((n, LANE), BF16), v128, v128],
        scratch_shapes=[], compiler_params=_cp(VMEM_BIG, ("arbitrary",)),
        args=(qn, kn, v, projp, alog_row, dtb_row, u, w, tmat, du, dw, dqg, dkd, do, vn, dgl))
    return (*outs, ridden)


GQ_W = NH * GLA_DK
GV_W = NH * GLA_DV
GLA_NORM = 16.0
MID = CHUNK // 2


def _gla_gates(sb, w2p, gb, chunk_in_seq):
    rows = _iota2((CHUNK, GQ_W), 0)
    valid = jnp.logical_or(rows >= N_PAD, chunk_in_seq > 0)
    graw = _nn(sb, w2p) + gb
    yield
    g = jnp.where(valid, _logsigmoid(graw) * (1.0 / GLA_NORM), 0.0)
    r, c = _masks64()
    bcum = _tri_sum(r >= c, g)
    yield
    return graw, bcum, valid


def _head_mask(h):
    lane = _iota2((1, GQ_W), 1)
    return jnp.logical_and(lane >= h * GLA_DK, lane < (h + 1) * GLA_DK)


def _gla_intra_fwd(projp, gates, w2p, gb, *, nc_seq, name):
    n = projp.shape[0]
    nct = n // CHUNK
    scale = GLA_DK ** -0.5

    grp = _group(nc_seq)
    rows = grp * CHUNK

    def inner(gi, qk_ref, v_ref, sb_ref, w2_ref, gb_ref, oi_ref, qg_ref, kd_ref, gl_ref):
        ci = (pl.program_id(0) * grp + gi) % nc_seq
        _, bc, _ = yield from _gla_gates(sb_ref[...], w2_ref[...], gb_ref[...], ci)
        bref = bc[MID:MID + 1, :]
        bl = bc[CHUNK - 1:CHUNK, :]
        q = qk_ref[:, 0:GQ_W].astype(F32) * scale
        k = qk_ref[:, GQ_W:2 * GQ_W].astype(F32)
        qi = q * jnp.exp(bc - bref)
        ki = k * jnp.exp(bref - bc)
        qg_ref[...] = (q * jnp.exp(bc)).astype(qg_ref.dtype)
        kd_ref[...] = (k * jnp.exp(bl - bc)).astype(kd_ref.dtype)
        gl_ref[0] = jnp.exp(bl)
        r, c = _masks64()
        incl = r >= c
        a = [jnp.where(incl, _nt(jnp.where(_head_mask(h), qi, 0.0), ki), 0.0) for h in range(NH)]
        yield
        for h in range(NH):
            oi_ref[:, h * GLA_DV:(h + 1) * GLA_DV] = _nn(a[h], v_ref[:, h * GLA_DV:(h + 1) * GLA_DV]).astype(oi_ref.dtype)

    kinds = ["row"] * 3 + ["whole"] * 2 + ["row"] * 3 + ["lead"]
    return pl.pallas_call(
        _per_chunk(inner, kinds, grp), name=name, grid=(nct // grp,),
        in_specs=[pl.BlockSpec((rows, 2 * GQ_W), lambda i: (i, C_GQK // (2 * GQ_W))),
                  pl.BlockSpec((rows, GV_W), lambda i: (i, C_GV // GV_W)),
                  pl.BlockSpec((rows, LANE), lambda i: (i, 1)),
                  pl.BlockSpec((LANE, GQ_W), lambda i: (0, 0)), pl.BlockSpec((1, GQ_W), lambda i: (0, 0))],
        out_specs=[pl.BlockSpec((rows, GV_W), lambda i: (i, 0)), pl.BlockSpec((rows, GQ_W), lambda i: (i, 0)),
                   pl.BlockSpec((rows, GQ_W), lambda i: (i, 0)), pl.BlockSpec((grp, 1, GQ_W), lambda i: (i, 0, 0))],
        out_shape=[jax.ShapeDtypeStruct((n, GV_W), BF16), jax.ShapeDtypeStruct((n, GQ_W), BF16),
                   jax.ShapeDtypeStruct((n, GQ_W), BF16), jax.ShapeDtypeStruct((nct, 1, GQ_W), F32)],
        compiler_params=_cp(VMEM_BIG),
    )(projp, projp, gates, w2p, gb)


def _gla_scan_fwd(oi, qg, kd, gl, projp, *, bsz, nc_seq, name, rider=None):
    t_seq = nc_seq * CHUNK
    oi = oi.reshape(bsz, t_seq, GV_W)
    qg, kd = qg.reshape(bsz, t_seq, GQ_W), kd.reshape(bsz, t_seq, GQ_W)
    gl = gl.reshape(bsz, nc_seq, 1, GQ_W)
    pj = projp.reshape(bsz, t_seq, PW)
    grp = _group(nc_seq)

    def body(oi_ref, qg_ref, kd_ref, gl_ref, v_ref, o_ref, hist_ref, st_ref):
        @pl.when(pl.program_id(0) == 0)
        def _():
            st_ref[...] = jnp.zeros_like(st_ref)

        for gi in range(grp):
            rows = pl.ds(gi * CHUNK, CHUNK)
            for b in range(bsz):
                st = st_ref[b]
                hist_ref[b, gi] = st.astype(hist_ref.dtype)
                qgb = qg_ref[b, rows, :]
                kdb = kd_ref[b, rows, :]
                upd = jnp.zeros((GLA_DV, GQ_W), F32)
                for h in range(NH):
                    sl = slice(h * GLA_DV, (h + 1) * GLA_DV)
                    m = _head_mask(h)
                    o_ref[b, rows, sl] = (oi_ref[b, rows, sl] + _nt(jnp.where(m, qgb, 0.0), st)).astype(o_ref.dtype)
                    upd = upd + jnp.where(m, _tn(v_ref[b, rows, sl], kdb), 0.0)
                st_ref[b] = gl_ref[b, gi] * st + upd

    rws = grp * CHUNK
    outs, ridden = _hosted_call(
        body, rider, name=name, grid=(nc_seq // grp,),
        in_specs=[pl.BlockSpec((bsz, rws, GV_W), lambda i: (0, i, 0)),
                  pl.BlockSpec((bsz, rws, GQ_W), lambda i: (0, i, 0)),
                  pl.BlockSpec((bsz, rws, GQ_W), lambda i: (0, i, 0)),
                  pl.BlockSpec((bsz, grp, 1, GQ_W), lambda i: (0, i, 0, 0)),
                  pl.BlockSpec((bsz, rws, GV_W), lambda i: (0, i, C_GV // GV_W))],
        out_specs=[pl.BlockSpec((bsz, rws, GV_W), lambda i: (0, i, 0)),
                   pl.BlockSpec((bsz, grp, GLA_DV, GQ_W), lambda i: (0, i, 0, 0))],
        out_shape=[jax.ShapeDtypeStruct((bsz, t_seq, GV_W), BF16),
                   jax.ShapeDtypeStruct((bsz, nc_seq, GLA_DV, GQ_W), BF16)],
        scratch_shapes=[pltpu.VMEM((bsz, GLA_DV, GQ_W), F32)],
        compiler_params=_cp(VMEM_BIG, ("arbitrary",)), args=(oi, qg, kd, gl, pj))
    return outs[0].reshape(bsz * t_seq, GV_W), outs[1], ridden


def _gla_scan_bwd(do, qg, kd, gl, projp, hist, *, bsz, nc_seq, name):
    t_seq = nc_seq * CHUNK
    do = do.reshape(bsz, t_seq, GV_W)
    qg, kd = qg.reshape(bsz, t_seq, GQ_W), kd.reshape(bsz, t_seq, GQ_W)
    gl = gl.reshape(bsz, nc_seq, 1, GQ_W)
    pj = projp.reshape(bsz, t_seq, PW)
    grp = _group(nc_seq)

    def body(do_ref, qg_ref, kd_ref, gl_ref, v_ref, hist_ref, dqg_ref, dkd_ref, dv_ref, dgl_ref, dst_ref):
        @pl.when(pl.program_id(0) == 0)
        def _():
            dst_ref[...] = jnp.zeros_like(dst_ref)

        for gi in reversed(range(grp)):
            rows = pl.ds(gi * CHUNK, CHUNK)
            for b in range(bsz):
                st = hist_ref[b, gi]
                dst = dst_ref[b]
                qgb = qg_ref[b, rows, :]
                kdb = kd_ref[b, rows, :]
                dqg = jnp.zeros((CHUNK, GQ_W), F32)
                dkd = jnp.zeros((CHUNK, GQ_W), F32)
                add = jnp.zeros((GLA_DV, GQ_W), F32)
                for h in range(NH):
                    sl = slice(h * GLA_DV, (h + 1) * GLA_DV)
                    m = _head_mask(h)
                    doh = do_ref[b, rows, sl]
                    vh = v_ref[b, rows, sl]
                    dqg = dqg + jnp.where(m, _nn(doh, st), 0.0)
                    dkd = dkd + jnp.where(m, _nn(vh, dst), 0.0)
                    dv_ref[b, rows, sl] = _nt(jnp.where(m, kdb, 0.0), dst).astype(dv_ref.dtype)
                    add = add + jnp.where(m, _tn(doh, qgb), 0.0)
                dqg_ref[b, rows, :] = dqg.astype(dqg_ref.dtype)
                dkd_ref[b, rows, :] = dkd.astype(dkd_ref.dtype)
                dgl_ref[b, gi] = jnp.sum(dst * st, axis=0, keepdims=True)
                dst_ref[b] = gl_ref[b, gi] * dst + add

    steps = nc_seq // grp
    rws = grp * CHUNK
    rev = lambda i: steps - 1 - i
    outs = pl.pallas_call(
        body, name=name, grid=(steps,),
        in_specs=[pl.BlockSpec((bsz, rws, GV_W), lambda i: (0, rev(i), 0)),
                  pl.BlockSpec((bsz, rws, GQ_W), lambda i: (0, rev(i), 0)),
                  pl.BlockSpec((bsz, rws, GQ_W), lambda i: (0, rev(i), 0)),
                  pl.BlockSpec((bsz, grp, 1, GQ_W), lambda i: (0, rev(i), 0, 0)),
                  pl.BlockSpec((bsz, rws, GV_W), lambda i: (0, rev(i), C_GV // GV_W)),
                  pl.BlockSpec((bsz, grp, GLA_DV, GQ_W), lambda i: (0, rev(i), 0, 0))],
        out_specs=[pl.BlockSpec((bsz, rws, GQ_W), lambda i: (0, rev(i), 0)),
                   pl.BlockSpec((bsz, rws, GQ_W), lambda i: (0, rev(i), 0)),
                   pl.BlockSpec((bsz, rws, GV_W), lambda i: (0, rev(i), 0)),
                   pl.BlockSpec((bsz, grp, 1, GQ_W), lambda i: (0, rev(i), 0, 0))],
        out_shape=[jax.ShapeDtypeStruct((bsz, t_seq, GQ_W), F32), jax.ShapeDtypeStruct((bsz, t_seq, GQ_W), F32),
                   jax.ShapeDtypeStruct((bsz, t_seq, GV_W), BF16), jax.ShapeDtypeStruct((bsz, nc_seq, 1, GQ_W), F32)],
        scratch_shapes=[pltpu.VMEM((bsz, GLA_DV, GQ_W), F32)],
        compiler_params=_cp(VMEM_BIG, ("arbitrary",)),
    )(do, qg, kd, gl, pj, hist)
    n = bsz * t_seq
    return (outs[0].reshape(n, GQ_W), outs[1].reshape(n, GQ_W), outs[2].reshape(n, GV_W),
            outs[3].reshape(bsz * nc_seq, 1, GQ_W))


def _gla_intra_bwd(projp, gates, w2p, gb, do, dqg, dkd, dvi, dgl, *, nc_seq, name):
    n = projp.shape[0]
    nct = n // CHUNK
    scale = GLA_DK ** -0.5

    grp = _group(nc_seq)
    rows = grp * CHUNK

    def inner(gi, qk_ref, v_ref, sb_ref, w2_ref, gb_ref, do_ref, dqg_ref, dkd_ref, dvi_ref, dgl_ref,
              dqk_ref, dv_ref, dsb_ref, dw2_ref, dgb_ref):
        ci = (pl.program_id(0) * grp + gi) % nc_seq
        sb = sb_ref[...]
        w2 = w2_ref[...]
        graw, bc, valid = yield from _gla_gates(sb, w2, gb_ref[...], ci)
        bref = bc[MID:MID + 1, :]
        bl = bc[CHUNK - 1:CHUNK, :]
        q = qk_ref[:, 0:GQ_W].astype(F32) * scale
        k = qk_ref[:, GQ_W:2 * GQ_W].astype(F32)
        ex1 = jnp.exp(bc - bref)
        ex2 = jnp.exp(bref - bc)
        eb = jnp.exp(bc)
        ekd = jnp.exp(bl - bc)
        qi, ki = q * ex1, k * ex2
        r, c = _masks64()
        incl = r >= c
        upper = r <= c
        a_t, da, da_t = [], [], []
        for h in range(NH):
            sl = slice(h * GLA_DV, (h + 1) * GLA_DV)
            doh = do_ref[:, sl]
            vh = v_ref[:, sl]
            a_t.append(jnp.where(upper, _nt(jnp.where(_head_mask(h), ki, 0.0), qi), 0.0))
            da.append(jnp.where(incl, _nt(doh, vh), 0.0))
            da_t.append(jnp.where(upper, _nt(vh, doh), 0.0))
        yield
        dqi = jnp.zeros((CHUNK, GQ_W), F32)
        dki = jnp.zeros((CHUNK, GQ_W), F32)
        for h in range(NH):
            sl = slice(h * GLA_DV, (h + 1) * GLA_DV)
            m = _head_mask(h)
            dv_ref[:, sl] = (_nn(a_t[h], do_ref[:, sl]) + dvi_ref[:, sl]).astype(dv_ref.dtype)
            dqi = dqi + jnp.where(m, _nn(da[h], ki), 0.0)
            dki = dki + jnp.where(m, _nn(da_t[h], qi), 0.0)
        yield
        dqg = dqg_ref[...].astype(F32)
        dkd = dkd_ref[...].astype(F32)
        dqk_ref[:, 0:GQ_W] = ((dqi * ex1 + dqg * eb) * scale).astype(dqk_ref.dtype)
        dqk_ref[:, GQ_W:2 * GQ_W] = (dki * ex2 + dkd * ekd).astype(dqk_ref.dtype)
        t_qi, t_ki, t_kd = dqi * qi, dki * ki, dkd * (k * ekd)
        db = t_qi - t_ki + dqg * (q * eb) - t_kd
        dbref = jnp.sum(t_ki - t_qi, axis=0, keepdims=True)
        dbl = jnp.sum(t_kd, axis=0, keepdims=True) + dgl_ref[0] * jnp.exp(bl)
        rows = _iota2((CHUNK, GQ_W), 0)
        db = db + jnp.where(rows == MID, dbref, 0.0) + jnp.where(rows == CHUNK - 1, dbl, 0.0)
        dg = _tri_sum(upper, db)
        yield
        dgraw = jnp.where(valid, dg * (1.0 / GLA_NORM) * _sigmoid(-graw), 0.0)
        dsb_ref[...] = _nt(dgraw, w2).astype(dsb_ref.dtype)
        dw2 = _tn(sb, dgraw)
        dgb = jnp.sum(dgraw, axis=0, keepdims=True)
        _accumulate(dw2_ref, dw2, gi)
        _accumulate(dgb_ref, dgb, gi)

    rq = pl.BlockSpec((rows, GQ_W), lambda i: (i, 0))
    rv = pl.BlockSpec((rows, GV_W), lambda i: (i, 0))
    kinds = ["row"] * 3 + ["whole"] * 2 + ["row"] * 4 + ["lead"] + ["row"] * 3 + ["whole"] * 2
    return pl.pallas_call(
        _per_chunk(inner, kinds, grp), name=name, grid=(nct // grp,),
        in_specs=[pl.BlockSpec((rows, 2 * GQ_W), lambda i: (i, C_GQK // (2 * GQ_W))),
                  pl.BlockSpec((rows, GV_W), lambda i: (i, C_GV // GV_W)),
                  pl.BlockSpec((rows, LANE), lambda i: (i, 1)),
                  pl.BlockSpec((LANE, GQ_W), lambda i: (0, 0)), pl.BlockSpec((1, GQ_W), lambda i: (0, 0)),
                  rv, rq, rq, rv, pl.BlockSpec((grp, 1, GQ_W), lambda i: (i, 0, 0))],
        out_specs=[pl.BlockSpec((rows, 2 * GQ_W), lambda i: (i, 0)), rv, pl.BlockSpec((rows, LANE), lambda i: (i, 0)),
                   pl.BlockSpec((LANE, GQ_W), lambda i: (0, 0)), pl.BlockSpec((1, GQ_W), lambda i: (0, 0))],
        out_shape=[jax.ShapeDtypeStruct((n, 2 * GQ_W), BF16), jax.ShapeDtypeStruct((n, GV_W), BF16),
                   jax.ShapeDtypeStruct((n, LANE), BF16), jax.ShapeDtypeStruct((LANE, GQ_W), F32),
                   jax.ShapeDtypeStruct((1, GQ_W), F32)],
        compiler_params=_cp(VMEM_BIG, ("arbitrary",)),
    )(projp, projp, gates, w2p, gb, do, dqg, dkd, dvi, dgl)


SECTIONS = ((C_QKV, 1536), (C_DZ, 512), (C_GQK, 512), (C_GV, 512), (C_GR, 512), (C_SA, 128), (C_SB, 128))


def _inproj_bwd(secs, wp, h0, g1, dx1, *, tr, name):
    n, d = h0.shape

    def body(*refs):
        sec_refs = refs[:len(SECTIONS)]
        wp_ref, h0_ref, g_ref, dx1_ref, o_ref, dg_ref = refs[len(SECTIONS):]
        dh = None
        for s_ref, (off, wd) in zip(sec_refs, SECTIONS):
            part = _nt(s_ref[...], wp_ref[:, off:off + wd])
            dh = part if dh is None else dh + part
        dx, dg = _rms_bwd_math(h0_ref[...], g_ref[...], dh)
        o_ref[...] = dx1_ref[...] + dx

        @pl.when(pl.program_id(0) == 0)
        def _():
            dg_ref[...] = dg

        @pl.when(pl.program_id(0) > 0)
        def _():
            dg_ref[...] += dg

    row = pl.BlockSpec((tr, d), lambda i: (i, 0))
    vec = pl.BlockSpec((1, d), lambda i: (0, 0))
    return pl.pallas_call(
        body, name=name, grid=(n // tr,),
        in_specs=[pl.BlockSpec((tr, wd), lambda i: (i, 0)) for _, wd in SECTIONS]
        + [pl.BlockSpec((d, PW), lambda i: (0, 0)), row, vec, row],
        out_specs=[row, vec],
        out_shape=[jax.ShapeDtypeStruct((n, d), F32), jax.ShapeDtypeStruct((1, d), F32)],
        compiler_params=_cp(VMEM_BIG),
    )(*secs, wp, h0, g1, dx1)


def _adamw(w, g, m, v, *, name, emit_grad=False):
    lead = w.ndim - 2
    r, c = w.shape[-2:]
    tr = _tile(r, 256, 8) if r > 256 else r
    c1 = 1.0 - ADAM_B1 ** ADAM_STEP
    c2 = 1.0 - ADAM_B2 ** ADAM_STEP
    n_out = 4 if emit_grad else 3

    def body(w_ref, g_ref, m_ref, v_ref, *out_refs):
        rd = (lambda ref: ref[0]) if lead else (lambda ref: ref[...])
        gv = g_ref[:, 0:c]
        nm = ADAM_B1 * rd(m_ref) + (1.0 - ADAM_B1) * gv
        nv = ADAM_B2 * rd(v_ref) + (1.0 - ADAM_B2) * (gv * gv)
        res = [-ADAM_LR * ((nm / c1) / (jnp.sqrt(nv / c2) + ADAM_EPS) + ADAM_WD * rd(w_ref)), nm, nv, gv]
        for o_ref, val in zip(out_refs, res):
            if lead:
                o_ref[0] = val
            else:
                o_ref[...] = val

    blk = pl.BlockSpec((1,) * lead + (tr, c), lambda i: (0,) * lead + (i, 0))
    gblk = pl.BlockSpec((tr, g.shape[1]), lambda i: (i, 0))
    sds = jax.ShapeDtypeStruct(w.shape, F32)
    return pl.pallas_call(
        body, name=name, grid=(r // tr,), in_specs=[blk, gblk, blk, blk], out_specs=[blk] * n_out,
        out_shape=[sds] * n_out, compiler_params=_cp(VMEM_BIG),
    )(w, g, m, v)


def _pair_sum(where, g, theirs, *, name):
    lead, r, cols = g.shape
    half = r // 2
    tr = _tile(half, 256, 16)
    nh = half // tr

    def body(w_ref, a_ref, b_ref, o_ref):
        o_ref[...] = (a_ref[...] + b_ref[...]).astype(o_ref.dtype)

    blk = pl.BlockSpec((1, tr, cols), lambda s, i, w: (s, i, 0))
    return pl.pallas_call(
        body, name=name,
        grid_spec=pltpu.PrefetchScalarGridSpec(
            num_scalar_prefetch=1, grid=(lead, nh),
            in_specs=[pl.BlockSpec((1, tr, cols), lambda s, i, w: (s, w[0] * nh + i, 0)), blk], out_specs=blk),
        out_shape=jax.ShapeDtypeStruct((lead, half, cols), BF16), compiler_params=_cp(VMEM_BIG),
    )(where, g, theirs)


def _chip_sum(where, pair, q, *, name):
    _, half, cols = pair.shape
    tr = _tile(half, 256, 16)
    nh = half // tr

    def body(w_ref, own_ref, q1_ref, q2_ref, q3_ref, o_ref):
        f = lambda ref: ref[0].astype(F32)
        o_ref[...] = ((f(own_ref) + f(q1_ref)) + f(q2_ref)) + f(q3_ref)

    def peer(d):
        return pl.BlockSpec((1, tr, cols), lambda i, w: ((w[1] + d) % N_CHIPS, i, 0))

    return pl.pallas_call(
        body, name=name,
        grid_spec=pltpu.PrefetchScalarGridSpec(
            num_scalar_prefetch=1, grid=(nh,),
            in_specs=[peer(0), peer(1), peer(2), peer(3)],
            out_specs=pl.BlockSpec((tr, cols), lambda i, w: (w[0] * nh + i, 0))),
        out_shape=jax.ShapeDtypeStruct((2 * half, cols), F32), compiler_params=_cp(VMEM_BIG),
    )(where, pair, q, q, q)


VM = pl.BlockSpec(memory_space=pltpu.VMEM)


def _row_chunks(rows, n_split):
    size = rows // n_split
    assert size * n_split == rows and size % 16 == 0, (rows, n_split)
    return [(s, pl.ds(s * size, size)) for s in range(n_split)], size


D2D_SPLIT = 4
ICI_SPLIT = 2


def _sibling_halves(grads):
    n_arr = len(grads)

    def body(*refs):
        ins = refs[:n_arr]
        theirs = refs[n_arr:2 * n_arr]
        send_sems, recv_sems = refs[2 * n_arr:]
        x, y, c = _place()
        copies = []
        for k in range(n_arr):
            half = ins[k].shape[1] // 2
            chunks, size = _row_chunks(half, D2D_SPLIT)
            for s, dst_rows in chunks:
                give = pltpu.make_async_remote_copy(
                    src_ref=ins[k].at[:, pl.ds((1 - c) * half + s * size, size), :], dst_ref=theirs[k].at[:, dst_rows, :],
                    send_sem=send_sems.at[k, s], recv_sem=recv_sems.at[k, s], device_id=(x, y, 1 - c),
                    device_id_type=MESH)
                give.start()
                copies.append(give)
        for give in copies:
            give.wait()

    halves = [jax.ShapeDtypeStruct((g.shape[0], g.shape[1] // 2, g.shape[2]), F32) for g in grads]
    sem = pltpu.SemaphoreType.DMA((n_arr, D2D_SPLIT))
    return pl.pallas_call(
        body, name="sibling_halves", in_specs=[ANY] * n_arr, out_specs=[ANY] * n_arr, out_shape=halves,
        scratch_shapes=[sem, sem],
    )(*grads)


def _chip_exchange(parts):
    n_arr = len(parts)

    def body(*refs):
        ins = refs[:n_arr]
        outs = refs[n_arr:2 * n_arr]
        send_sems, recv_sems = refs[2 * n_arr:]
        x, y, c = _place()
        me = 2 * x + y
        sends = []
        for k in range(n_arr):
            chunks, _ = _row_chunks(ins[k].shape[1], ICI_SPLIT)
            for d, (px, py, pj) in enumerate(_other_chips(x, y)):
                for s, rows in chunks:
                    cp = pltpu.make_async_remote_copy(
                        src_ref=ins[k].at[pj, rows, :], dst_ref=outs[k].at[me, rows, :], send_sem=send_sems.at[k, d, s],
                        recv_sem=recv_sems.at[k, d, s], device_id=(px, py, c), device_id_type=MESH)
                    cp.start()
                    sends.append(cp)
        for k in range(n_arr):
            chunks, _ = _row_chunks(ins[k].shape[1], ICI_SPLIT)
            for d, (px, py, pj) in enumerate(_other_chips(x, y)):
                for s, rows in chunks:
                    pltpu.make_async_remote_copy(
                        src_ref=ins[k].at[pj, rows, :], dst_ref=outs[k].at[pj, rows, :], send_sem=send_sems.at[k, d, s],
                        recv_sem=recv_sems.at[k, d, s], device_id=(px, py, c), device_id_type=MESH).wait_recv()
        for cp in sends:
            cp.wait_send()

    sem = pltpu.SemaphoreType.DMA((n_arr, 3, ICI_SPLIT))
    return pl.pallas_call(
        body, name="chip_exchange", in_specs=[ANY] * n_arr, out_specs=[ANY] * n_arr,
        out_shape=[jax.ShapeDtypeStruct(p.shape, p.dtype) for p in parts],
        scratch_shapes=[sem, sem],
    )(*parts)


class _SiblingHalvesRider:
    def __init__(self, grads):
        self.inputs = list(grads)
        self.out_shapes = [jax.ShapeDtypeStruct((g.shape[0], g.shape[1] // 2, g.shape[2]), F32) for g in grads]
        self.aliases = {}
        self.sems = [pltpu.SemaphoreType.DMA((len(grads), D2D_SPLIT))] * 2

    def _copies(self, ins, outs, sems):
        x, y, c = _place()
        for k in range(len(ins)):
            half = ins[k].shape[1] // 2
            chunks, size = _row_chunks(half, D2D_SPLIT)
            for s, dst_rows in chunks:
                yield pltpu.make_async_remote_copy(
                    src_ref=ins[k].at[:, pl.ds((1 - c) * half + s * size, size), :], dst_ref=outs[k].at[:, dst_rows, :],
                    send_sem=sems[0].at[k, s], recv_sem=sems[1].at[k, s], device_id=(x, y, 1 - c), device_id_type=MESH)

    def first(self, ins, outs, sems):
        for cp in self._copies(ins, outs, sems):
            cp.start()

    def last(self, ins, outs, sems):
        for cp in self._copies(ins, outs, sems):
            cp.wait()


class _ChipExchangeRider:
    def __init__(self, parts):
        self.inputs = list(parts)
        self.out_shapes = [jax.ShapeDtypeStruct(p.shape, p.dtype) for p in parts]
        self.aliases = {}
        self.sems = [pltpu.SemaphoreType.DMA((len(parts), 3, ICI_SPLIT))] * 2

    def _copies(self, ins, outs, sems, receiving):
        x, y, c = _place()
        for k in range(len(ins)):
            chunks, _ = _row_chunks(ins[k].shape[1], ICI_SPLIT)
            for d, (px, py, pj) in enumerate(_other_chips(x, y)):
                for s, rows in chunks:
                    yield pltpu.make_async_remote_copy(
                        src_ref=ins[k].at[pj, rows, :], dst_ref=outs[k].at[pj if receiving else 2 * x + y, rows, :],
                        send_sem=sems[0].at[k, d, s], recv_sem=sems[1].at[k, d, s], device_id=(px, py, c),
                        device_id_type=MESH)

    def first(self, ins, outs, sems):
        for cp in self._copies(ins, outs, sems, False):
            cp.start()

    def last(self, ins, outs, sems):
        for cp in self._copies(ins, outs, sems, True):
            cp.wait_recv()
        for cp in self._copies(ins, outs, sems, False):
            cp.wait_send()


def _sibling_join(bufs):
    n_arr = len(bufs)

    def body(*refs):
        bufs_out = refs[n_arr:2 * n_arr]
        send_sems, recv_sems = refs[2 * n_arr:]
        x, y, c = _place()
        copies = []
        for k in range(n_arr):
            half = bufs_out[k].shape[0] // 2
            chunks, size = _row_chunks(half, D2D_SPLIT)
            for s, _ in chunks:
                rows = pl.ds(c * half + s * size, size)
                give = pltpu.make_async_remote_copy(
                    src_ref=bufs_out[k].at[rows, :], dst_ref=bufs_out[k].at[rows, :], send_sem=send_sems.at[k, s],
                    recv_sem=recv_sems.at[k, s], device_id=(x, y, 1 - c), device_id_type=MESH)
                give.start()
                copies.append((k, s, half, size, give))
        for k, s, half, size, give in copies:
            rows = pl.ds((1 - c) * half + s * size, size)
            pltpu.make_async_remote_copy(
                src_ref=bufs_out[k].at[rows, :], dst_ref=bufs_out[k].at[rows, :], send_sem=send_sems.at[k, s],
                recv_sem=recv_sems.at[k, s], device_id=(x, y, 1 - c), device_id_type=MESH).wait_recv()
            give.wait_send()

    sem = pltpu.SemaphoreType.DMA((n_arr, D2D_SPLIT))
    return pl.pallas_call(
        body, name="sibling_join", in_specs=[ANY] * n_arr, out_specs=[ANY] * n_arr,
        out_shape=[jax.ShapeDtypeStruct(b.shape, F32) for b in bufs],
        input_output_aliases={k: k for k in range(n_arr)},
        scratch_shapes=[sem, sem],
    )(*bufs)


PACK_ROWS = 48


def _small_allreduce(pack):
    masks = [(dx, dy, dc) for dx in (0, 1) for dy in (0, 1) for dc in (0, 1)][1:]

    def body(p_ref, o_ref, buf, send_sems, recv_sems):
        x, y, c = _place()
        me = 4 * x + 2 * y + c
        buf[me] = p_ref[...]
        sends = []
        for k, (dx, dy, dc) in enumerate(masks):
            peer = (1 - x if dx else x, 1 - y if dy else y, 1 - c if dc else c)
            cp = pltpu.make_async_remote_copy(
                src_ref=p_ref, dst_ref=buf.at[me], send_sem=send_sems.at[k], recv_sem=recv_sems.at[k],
                device_id=peer, device_id_type=MESH)
            cp.start()
            sends.append(cp)
        for k, (dx, dy, dc) in enumerate(masks):
            peer = (1 - x if dx else x, 1 - y if dy else y, 1 - c if dc else c)
            pj = 4 * peer[0] + 2 * peer[1] + peer[2]
            pltpu.make_async_remote_copy(
                src_ref=p_ref, dst_ref=buf.at[pj], send_sem=send_sems.at[k], recv_sem=recv_sems.at[k],
                device_id=peer, device_id_type=MESH).wait_recv()
        for cp in sends:
            cp.wait_send()
        tot = buf[0]
        for k in range(1, 8):
            tot = tot + buf[k]
        o_ref[...] = tot
        o_ref[0:N_META, :] = tot[0:N_META] + tot[N_META:2 * N_META]

    return pl.pallas_call(
        body, name="small_allreduce", in_specs=[VM], out_specs=VM,
        out_shape=jax.ShapeDtypeStruct((PACK_ROWS, D_MODEL), F32),
        scratch_shapes=[pltpu.VMEM((8, PACK_ROWS, D_MODEL), F32), pltpu.SemaphoreType.DMA((7,)),
                        pltpu.SemaphoreType.DMA((7,))],
    )(pack)


def _pad_lanes(vec, offset):
    k = vec.shape[1]
    return jnp.concatenate([jnp.zeros((1, offset), F32), vec, jnp.zeros((1, LANE - offset - k), F32)], axis=1)


def _local_step(x, tgt, meta, norm1_g, wp, conv_w, a_log, dt_bias, dn_norm_g, gla_w2, gla_b, gla_norm_g,
                w_out, norm2_g, w_up, w_down, final_norm_g, late_gather=None, where=None):
    bsz, s_len, d = x.shape
    t_seq = s_len + CHUNK
    nc_seq = t_seq // CHUNK
    n = bsz * t_seq
    tr = _tile(t_seq, 832)
    tt = _tile(t_seq, 416)

    lead = jnp.concatenate([jnp.zeros((N_PAD, d), F32), meta], axis=0)
    h0 = jnp.concatenate([jnp.broadcast_to(lead[None], (bsz, CHUNK, d)), x], axis=1).reshape(n, d)
    tgt_p = jnp.concatenate([jnp.zeros((bsz, CHUNK, d), F32), tgt], axis=1).reshape(n, d)
    alog_row = _pad_lanes(a_log, 4)
    dtb_row = _pad_lanes(dt_bias, 4)
    w2p = jnp.concatenate([gla_w2, jnp.zeros((LANE - GLA_RANK, GQ_W), F32)], axis=0)

    h = _rms_fwd(h0, norm1_g, tr=tr, name="norm1")
    ride_up, ride_down, ride_out = late_gather if late_gather is not None else (None, None, None)
    res = _mm(h, wp, "nn", tm=tt, tn=PW, tk=d, out_dtypes=(BF16, F32), out_widths=(PW, PW - C_SA),
              epilogue=lambda acc: (acc, acc[:, C_SA:PW]), name="in_proj", rider=ride_up)
    ((projp, gates), got_up) = res if ride_up is not None else (res, None)
    qn, kn, v = _dnprep_fwd(projp, conv_w, bsz=bsz, t_seq=t_seq, tt=tt, name="dn_prep")
    u, w, qg, kd, pmat, tmat, gl, got_down = _dn_intra_fwd(qn, kn, v, gates, alog_row, dtb_row, nc_seq=nc_seq,
                                                           name="dn_intra", rider=ride_down)
    o_dn, vn, hist, got_out = _dn_scan_fwd(u, w, qg, kd, pmat, gl, bsz=bsz, nc_seq=nc_seq, name="dn_scan",
                                           rider=ride_out)
    oi, gqg, gkd, ggl = _gla_intra_fwd(projp, gates, w2p, gla_b, nc_seq=nc_seq, name="gla_intra")
    o_gla, ghist, _ = _gla_scan_fwd(oi, gqg, gkd, ggl, projp, bsz=bsz, nc_seq=nc_seq, name="gla_scan")
    if late_gather is not None:
        w_out = got_out[0].reshape(d, d)
        w_up = got_up[0].transpose(1, 0, 2).reshape(d, D_FF)
        w_down = got_down[0].reshape(D_FF, d)
    mix = _gnorm_fwd(o_dn, o_gla, projp, dn_norm_g, gla_norm_g, tr=tr, name="gated_norm")
    (x1,) = _mm(mix, w_out, "nn", tm=tr, tn=d, tk=d, out_dtypes=(F32,), extras=(h0,),
                epilogue=lambda acc, res: (res + acc,), name="out_proj")
    h2 = _rms_fwd(x1, norm2_g, tr=tr, name="norm2")

    (act,) = _mm(h2, w_up, "nn", tm=tt, tn=D_FF, tk=d, out_dtypes=(BF16,),
                 epilogue=lambda acc: (jnp.square(jnp.maximum(acc, 0.0)),), name="mlp_up")
    dx2, dx2b, d_final_g, loss_tile = _mlp_down_loss(act, w_down, x1, final_norm_g, tgt_p, t_seq=t_seq, tr=tt,
                                                     name="mlp_down_loss")

    (dup,) = _mm(dx2b, w_down, "nt", tm=tt, tn=D_FF, tk=d, out_dtypes=(BF16,), extras=(act,),
                 epilogue=lambda acc, a: (acc * (2.0 * jnp.sqrt(a.astype(F32))),), name="mlp_down_bwd")
    (d_w_down,) = _mm(act, dx2b, "tn", tm=D_FF // 2, tn=d, tk=tr, out_dtypes=(F32,), name="w_down_grad")
    (d_w_up,) = _mm(h2, dup, "tn", tm=d, tn=D_FF // 2, tk=tr, out_dtypes=(F32,), name="w_up_grad")
    mlp_sm = [d_w_up.reshape(d, N_CHIPS, D_FF // N_CHIPS).transpose(1, 0, 2), d_w_down.reshape(N_CHIPS, D_FF // N_CHIPS, d)]
    ride1 = _SiblingHalvesRider(mlp_sm) if where is not None else None
    dx1, dx1b, d_norm2_g, theirs = _mlp_up_bwd_norm(dup, w_up, x1, norm2_g, dx2, tr=tt, name="mlp_up_bwd_norm",
                                                    rider=ride1)
    ride2 = None
    if where is not None:
        mlp_pair = [_pair_sum(where, a, b, name=f"pair_sum_mlp{k}") for k, (a, b) in enumerate(zip(mlp_sm, theirs))]
        ride2 = _ChipExchangeRider(mlp_pair)

    (dmix,) = _mm(dx1b, w_out, "nt", tm=tr, tn=d, tk=d, out_dtypes=(BF16,), name="out_proj_bwd")
    (d_w_out,) = _mm(mix, dx1b, "tn", tm=d, tn=d, tk=tr, out_dtypes=(F32,), name="w_out_grad")
    do_dn, ddz, do_gla, dgr, d_dn_norm_g, d_gla_norm_g = _gnorm_bwd(
        dmix, o_dn, o_gla, projp, dn_norm_g, gla_norm_g, tr=tr, name="gated_norm_bwd")
    du, dw, dqg, dkd, dgl = _dn_scan_bwd(do_dn, w, qg, kd, vn, pmat, gl, hist, bsz=bsz, nc_seq=nc_seq,
                                          name="dn_scan_bwd")
    dqn, dkn, dv, dsa, d_alog, d_dtb, mlp_parts = _dn_intra_bwd(
        qn, kn, v, gates, alog_row, dtb_row, u, w, tmat, du, dw, dqg, dkd, do_dn, vn, dgl, nc_seq=nc_seq,
        name="dn_intra_bwd", rider=ride2)
    dz, d_conv_w = _dnprep_bwd_a(projp, conv_w, dqn, dkn, dv, bsz=bsz, t_seq=t_seq, tt=tt, name="dn_prep_bwd")
    dcin = _dnprep_bwd_b(dz, conv_w, bsz=bsz, t_seq=t_seq, tt=tt, name="conv_bwd")
    gdqg, gdkd, gdvi, gdgl = _gla_scan_bwd(do_gla, gqg, gkd, ggl, projp, ghist, bsz=bsz, nc_seq=nc_seq,
                                            name="gla_scan_bwd")
    dgqk, dgv, dsb, d_w2p, d_gla_b = _gla_intra_bwd(projp, gates, w2p, gla_b, do_gla, gdqg, gdkd, gdvi, gdgl,
                                                    nc_seq=nc_seq, name="gla_intra_bwd")

    secs = (dcin, ddz, dgqk, dgv, dgr, dsa, dsb)
    g_lo = _grad_tn(h, secs[0:2], tk=tr, name="w_in_grad_lo")
    g_hi = _grad_tn(h, secs[2:7], tk=tr, name="w_in_grad_hi")
    dh0, d_norm1_g = _inproj_bwd(secs, wp, h0, norm1_g, dx1, tr=tt, name="in_proj_bwd")
    dh0 = dh0.reshape(bsz, t_seq, d)
    grad_x = dh0[:, CHUNK:]
    d_meta_rows = dh0[:, N_PAD:CHUNK].reshape(bsz * N_META, d)

    grads = dict(w_in_lo=g_lo, w_in_hi=g_hi, w_out=d_w_out, w_up=d_w_up, w_down=d_w_down, meta_rows=d_meta_rows,
                 norm1_g=d_norm1_g, conv_w=d_conv_w, a_log_tile=d_alog, dt_bias_tile=d_dtb, dn_norm_g=d_dn_norm_g,
                 gla_w2=d_w2p[0:GLA_RANK], gla_b=d_gla_b, gla_norm_g=d_gla_norm_g, norm2_g=d_norm2_g,
                 final_norm_g=d_final_g, loss_tile=loss_tile)
    if where is not None:
        grads["mlp_exchanged"] = (mlp_pair, mlp_parts)
    return grad_x, grads


SHARD_W = IN_WIDTH // N_CHIPS
PADDED_ORDER = ((0, 2048), (2056, 3592), (2048, 2056), LANE - 8, (3592, 3608), LANE - GLA_RANK)


def _pad_layout(w_full):
    pieces = [jnp.zeros((w_full.shape[0], seg), w_full.dtype) if isinstance(seg, int) else w_full[:, seg[0]:seg[1]]
              for seg in PADDED_ORDER]
    return jnp.concatenate(pieces, axis=1)


def _padded_from_shards(stack):
    pieces = []
    for seg in PADDED_ORDER:
        if isinstance(seg, int):
            pieces.append(jnp.zeros((stack.shape[1], seg), stack.dtype))
            continue
        for j in range(N_CHIPS):
            lo, hi = max(seg[0], j * SHARD_W), min(seg[1], (j + 1) * SHARD_W)
            if lo < hi:
                pieces.append(stack[j, :, lo - j * SHARD_W:hi - j * SHARD_W])
    return jnp.concatenate(pieces, axis=1)


def _shards_from_padded(g_lo, g_hi):
    split = g_lo.shape[1]
    starts, pos = [], 0
    for seg in PADDED_ORDER:
        width = seg if isinstance(seg, int) else seg[1] - seg[0]
        if not isinstance(seg, int):
            starts.append((seg[0], seg[1], pos))
        pos += width
    shards = []
    for j in range(N_CHIPS):
        pieces = []
        for a, b, p0 in sorted(starts):
            lo, hi = max(a, j * SHARD_W), min(b, (j + 1) * SHARD_W)
            if lo < hi:
                src, off = (g_lo, 0) if p0 < split else (g_hi, split)
                pieces.append(src[:, p0 + lo - a - off:p0 + hi - a - off])
        pieces.append(jnp.zeros((g_lo.shape[0], D_MODEL - SHARD_W), g_lo.dtype))
        shards.append(jnp.concatenate(pieces, axis=1))
    return jnp.stack(shards)


def _pack_small(g, bsz):
    assert bsz * N_META == 32
    row = jnp.concatenate([g["a_log_tile"], g["dt_bias_tile"], g["dn_norm_g"], g["gla_norm_g"], g["gla_b"],
                           g["loss_tile"], jnp.zeros((1, LANE), F32)], axis=1)
    return jnp.concatenate([g["meta_rows"], g["norm1_g"], g["conv_w"].reshape(6, D_MODEL), row,
                            g["gla_w2"].reshape(4, D_MODEL), g["norm2_g"], g["final_norm_g"],
                            jnp.zeros((2, D_MODEL), F32)], axis=0)


def kernel(x, meta_tokens, norm1_g, w_in, conv_w, a_log, dt_bias, dn_norm_g, gla_w2, gla_b, gla_norm_g, w_out, norm2_g, w_up, w_down, final_norm_g, loss_target, m_meta_tokens, m_norm1_g, m_w_in, m_conv_w, m_a_log, m_dt_bias, m_dn_norm_g, m_gla_w2, m_gla_b, m_gla_norm_g, m_w_out, m_norm2_g, m_w_up, m_w_down, m_final_norm_g, v_meta_tokens, v_norm1_g, v_w_in, v_conv_w, v_a_log, v_dt_bias, v_dn_norm_g, v_gla_w2, v_gla_b, v_gla_norm_g, v_w_out, v_norm2_g, v_w_up, v_w_down, v_final_norm_g):
    bsz = x.shape[0]
    chip = 2 * lax.axis_index("x") + lax.axis_index("y")

    lane_pad = lambda a, wd: jnp.pad(a, ((0, 0), (0, wd - a.shape[1])))
    where = jnp.stack([lax.axis_index("c"), chip]).astype(jnp.int32)
    slot = lambda a, dt, nm: _to_slot(where, a, dt, name="slot_" + nm)
    early = _GatherRider([slot(lane_pad(w_in[0], D_MODEL), BF16, "w_in"), slot(meta_tokens, F32, "meta"),
                          slot(conv_w[0], F32, "conv"), slot(lane_pad(gla_w2[0], LANE), F32, "gla_w2")],
                         [True, False, False, False])
    g_in, g_meta, g_conv, g_w2 = _exchange_now(early, name="gather_early")
    late = (_GatherRider([slot(w_up[0], BF16, "w_up")], [True]), _GatherRider([slot(w_down[0], BF16, "w_down")], [True]),
            _GatherRider([slot(w_out[0], BF16, "w_out")], [True]))
    wp = _padded_from_shards(g_in)
    meta_f = g_meta.transpose(1, 0, 2).reshape(N_META, D_MODEL)
    conv_f = g_conv.transpose(1, 0, 2).reshape(4, QKV_W)
    w2_f = g_w2[:, :, 0:GQ_W // N_CHIPS].transpose(1, 0, 2).reshape(GLA_RANK, GQ_W)

    grad_x, g = _local_step(x, loss_target, meta_f, norm1_g, wp, conv_f, a_log, dt_bias, dn_norm_g, w2_f, gla_b,
                            gla_norm_g, None, norm2_g, None, None, final_norm_g.reshape(1, D_MODEL), late_gather=late, where=where)

    shard_major = [_shards_from_padded(g["w_in_lo"], g["w_in_hi"]), g["w_out"].reshape(N_CHIPS, D_MODEL // N_CHIPS, D_MODEL)]
    theirs = _exchange_now(_SiblingHalvesRider(shard_major), name="sibling_halves")
    pair = [_pair_sum(where, a, b, name=f"pair_sum_{k}") for k, (a, b) in enumerate(zip(shard_major, theirs))]
    parts = _exchange_now(_ChipExchangeRider(pair), name="chip_exchange")
    mlp_pair, mlp_parts = g["mlp_exchanged"]
    halves = [_chip_sum(where, p, q, name=f"chip_sum_{k}")
              for k, (p, q) in enumerate(zip(pair + mlp_pair, list(parts) + list(mlp_parts)))]
    gw_in, gw_out, gw_up, gw_down = _sibling_join(halves)

    red = _small_allreduce(_pack_small(g, bsz))
    g_meta_full = red[0:N_META]
    g_norm1 = red[32:33]
    g_conv_full = red[33:39].reshape(4, QKV_W)
    srow = red[39:40]
    g_alog, g_dtb = srow[:, 4:8], srow[:, LANE + 4:LANE + 8]
    g_dn_norm, g_gla_norm = srow[:, 2 * LANE:3 * LANE], srow[:, 3 * LANE:4 * LANE]
    g_gla_b = srow[:, 4 * LANE:6 * LANE]
    loss = srow[0, 6 * LANE]
    g_w2_full = red[40:44].reshape(GLA_RANK, GQ_W)
    g_norm2 = red[44:45]
    g_final = red[45:46]
    g_meta_sh = lax.dynamic_slice_in_dim(g_meta_full, chip * (D_MODEL // N_CHIPS), D_MODEL // N_CHIPS, axis=1)
    g_conv_sh = lax.dynamic_slice_in_dim(g_conv_full, chip * (QKV_W // N_CHIPS), QKV_W // N_CHIPS, axis=1)
    g_w2_sh = lax.dynamic_slice_in_dim(g_w2_full, chip * (GQ_W // N_CHIPS), GQ_W // N_CHIPS, axis=1)

    names = ["meta_tokens", "norm1_g", "w_in", "conv_w", "a_log", "dt_bias", "dn_norm_g", "gla_w2", "gla_b",
             "gla_norm_g", "w_out", "norm2_g", "w_up", "w_down", "final_norm_g"]
    weights = dict(meta_tokens=meta_tokens, norm1_g=norm1_g, w_in=w_in, conv_w=conv_w, a_log=a_log, dt_bias=dt_bias,
                   dn_norm_g=dn_norm_g, gla_w2=gla_w2, gla_b=gla_b, gla_norm_g=gla_norm_g, w_out=w_out,
                   norm2_g=norm2_g, w_up=w_up, w_down=w_down, final_norm_g=final_norm_g)
    ms = dict(meta_tokens=m_meta_tokens, norm1_g=m_norm1_g, w_in=m_w_in, conv_w=m_conv_w, a_log=m_a_log,
              dt_bias=m_dt_bias, dn_norm_g=m_dn_norm_g, gla_w2=m_gla_w2, gla_b=m_gla_b, gla_norm_g=m_gla_norm_g,
              w_out=m_w_out, norm2_g=m_norm2_g, w_up=m_w_up, w_down=m_w_down, final_norm_g=m_final_norm_g)
    vs = dict(meta_tokens=v_meta_tokens, norm1_g=v_norm1_g, w_in=v_w_in, conv_w=v_conv_w, a_log=v_a_log,
              dt_bias=v_dt_bias, dn_norm_g=v_dn_norm_g, gla_w2=v_gla_w2, gla_b=v_gla_b, gla_norm_g=v_gla_norm_g,
              w_out=v_w_out, norm2_g=v_norm2_g, w_up=v_w_up, w_down=v_w_down, final_norm_g=v_final_norm_g)
    grads2d = dict(meta_tokens=g_meta_sh, norm1_g=g_norm1, w_in=gw_in, conv_w=g_conv_sh, a_log=g_alog, dt_bias=g_dtb,
                   dn_norm_g=g_dn_norm, gla_w2=g_w2_sh, gla_b=g_gla_b, gla_norm_g=g_gla_norm, w_out=gw_out,
                   norm2_g=g_norm2, w_up=gw_up, w_down=gw_down, final_norm_g=g_final)
    out_g, out_d, out_m, out_v = [], [], [], []
    for nm in names:
        shape = weights[nm].shape
        g2 = grads2d[nm]
        if len(shape) == 3:
            res = _adamw(weights[nm], g2, ms[nm], vs[nm], name=f"adamw_{nm}", emit_grad=nm == "w_in")
            gout = res[3] if nm == "w_in" else g2.reshape(shape)
        else:
            as2d = lambda a: a.reshape(g2.shape)
            res = _adamw(as2d(weights[nm]), g2, as2d(ms[nm]), as2d(vs[nm]), name=f"adamw_{nm}")
            gout = g2.reshape(shape)
        out_g.append(gout)
        out_d.append(res[0].reshape(shape))
        out_m.append(res[1].reshape(shape))
        out_v.append(res[2].reshape(shape))
    return (loss, grad_x, *out_g, *out_d, *out_m, *out_v)
```

```python
import functools

import jax
import jax.numpy as jnp
import numpy as np
from jax import lax
from jax.experimental import pallas as pl
from jax.experimental.pallas import tpu as pltpu

F32 = jnp.float32
BF16 = jnp.bfloat16
HI = lax.Precision.HIGHEST
MESH = pl.DeviceIdType.MESH

D_MODEL = 1024
N_META = 16
CHUNK = 64
N_PAD = CHUNK - N_META
NH = 4
DN_D = 128
GLA_DK = 64
GLA_DV = 128
GLA_RANK = 16
D_FF = 4 * D_MODEL
EPS = 1e-6
IN_WIDTH = 3608
C_QKV, C_DZ, C_GQK, C_GV, C_GR, C_SA, C_SB, PW = 0, 1536, 2048, 2560, 3072, 3584, 3712, 3840
LANE = 128
N_CHIPS = 4

ADAM_LR, ADAM_B1, ADAM_B2, ADAM_EPS, ADAM_WD, ADAM_STEP = 0.001, 0.9, 0.999, 1e-08, 0.01, 10

VMEM_BIG = 56 * 1024 * 1024


def _cp(vmem=None, sem=None):
    kw = {}
    if vmem is not None:
        kw["vmem_limit_bytes"] = vmem
    if sem is not None:
        kw["dimension_semantics"] = sem
    return pltpu.CompilerParams(**kw)


def _tile(n, target, mult=16):
    best = None
    for t in range(mult, min(n, target) + 1, mult):
        if n % t == 0:
            best = t
    assert best is not None, (n, target)
    return best


def _dot(a, b, dims, prec=None):
    return lax.dot_general(a, b, (dims, ((), ())), preferred_element_type=F32, precision=prec)


def _nn(a, b):
    return _dot(a.astype(BF16), b.astype(BF16), ((1,), (0,)))


def _nt(a, b):
    return _dot(a.astype(BF16), b.astype(BF16), ((1,), (1,)))


def _tn(a, b):
    return _dot(a.astype(BF16), b.astype(BF16), ((0,), (0,)))


def _tri_sum(tri, x):
    t = tri.astype(BF16)
    hi = x.astype(BF16)
    r1 = x - hi.astype(F32)
    mid = r1.astype(BF16)
    lo = (r1 - mid.astype(F32)).astype(BF16)
    nn = ((1,), (0,))
    return _dot(t, hi, nn) + _dot(t, mid, nn) + _dot(t, lo, nn)


def _sigmoid(x):
    return 0.5 * jnp.tanh(0.5 * x) + 0.5


def _softplus(x):
    return jnp.maximum(x, 0.0) + jnp.log(1.0 + jnp.exp(-jnp.abs(x)))


def _logsigmoid(x):
    return -_softplus(-x)


def _iota2(shape, dim):
    return lax.broadcasted_iota(jnp.int32, shape, dim)


def _mm(a, b, mode, *, tm, tn, tk, out_dtypes, extras=(), epilogue=None, name, vmem=VMEM_BIG, rider=None,
        out_widths=None, n_chunk=None):
    if mode == "tn":
        K, M = a.shape
    else:
        M, K = a.shape
    N = b.shape[0] if mode == "nt" else b.shape[1]
    assert M % tm == 0 and N % tn == 0 and K % tk == 0, (name, M, N, K, tm, tn, tk)
    nk = K // tk
    n_ex, n_out = len(extras), len(out_dtypes)
    if mode == "tn":
        a_spec = pl.BlockSpec((tk, tm), lambda i, j, k: (k, i))
    else:
        a_spec = pl.BlockSpec((tm, tk), lambda i, j, k: (i, k))
    if mode == "nt":
        b_spec = pl.BlockSpec((tn, tk), lambda i, j, k: (j, k))
    else:
        b_spec = pl.BlockSpec((tk, tn), lambda i, j, k: (k, j))
    mn_spec = pl.BlockSpec((tm, tn), lambda i, j, k: (i, j))
    if out_widths is None:
        o_specs = [mn_spec] * n_out
        o_shapes = [jax.ShapeDtypeStruct((M, N), dt) for dt in out_dtypes]
    else:
        assert tn == N
        o_specs = [pl.BlockSpec((tm, wd), lambda i, j, k: (i, 0)) for wd in out_widths]
        o_shapes = [jax.ShapeDtypeStruct((M, wd), dt) for wd, dt in zip(out_widths, out_dtypes)]
    dims = {"nn": ((1,), (0,)), "nt": ((1,), (1,)), "tn": ((0,), (0,))}[mode]

    single = nk == 1
    direct = (not single) and epilogue is None and n_out == 1 and out_dtypes[0] == F32

    def body(*refs):
        a_ref, b_ref = refs[0], refs[1]
        ex_refs = refs[2:2 + n_ex]
        out_refs = refs[2 + n_ex:2 + n_ex + n_out]
        if n_chunk is not None:
            assert single and out_widths is None and mode != "tn" and tn % n_chunk == 0
            av = a_ref[...].astype(BF16)
            for j in range(tn // n_chunk):
                cols = slice(j * n_chunk, (j + 1) * n_chunk)
                bv = b_ref[cols, :] if mode == "nt" else b_ref[:, cols]
                acc = _dot(av, bv.astype(BF16), dims)
                res = (acc,) if epilogue is None else epilogue(acc, *[e[:, cols] for e in ex_refs])
                for o_ref, r in zip(out_refs, res):
                    o_ref[:, cols] = r.astype(o_ref.dtype)
            return
        part = _dot(a_ref[...].astype(BF16), b_ref[...].astype(BF16), dims)

        def finish(acc):
            res = (acc,) if epilogue is None else epilogue(acc, *[e[...] for e in ex_refs])
            for o_ref, r in zip(out_refs, res):
                o_ref[...] = r.astype(o_ref.dtype)

        if single:
            finish(part)
            return
        acc_ref = out_refs[0] if direct else refs[2 + n_ex + n_out]
        k = pl.program_id(2)

        @pl.when(k == 0)
        def _():
            acc_ref[...] = part

        @pl.when(k > 0)
        def _():
            acc_ref[...] += part

        if not direct:
            @pl.when(k == nk - 1)
            def _():
                finish(acc_ref[...])

    outs, ridden = _hosted_call(
        body, rider, name=name, grid=(M // tm, N // tn, nk),
        in_specs=[a_spec, b_spec] + [mn_spec] * n_ex,
        out_specs=o_specs, out_shape=o_shapes,
        scratch_shapes=[] if (single or direct) else [pltpu.VMEM((tm, tn), F32)],
        compiler_params=_cp(vmem, ("parallel", "parallel", "arbitrary")), args=(a, b, *extras))
    return tuple(outs) if rider is None else (tuple(outs), ridden)


def _grad_tn(a, secs, *, tk, name):
    kk, m = a.shape
    widths = [s.shape[1] for s in secs]
    total = sum(widths)
    nk = kk // tk

    def body(*refs):
        a_ref, sec_refs, o_ref = refs[0], refs[1:-1], refs[-1]
        cat = sec_refs[0][...] if len(sec_refs) == 1 else jnp.concatenate([s[...] for s in sec_refs], axis=1)
        part = _dot(a_ref[...].astype(BF16), cat.astype(BF16), ((0,), (0,)))
        k = pl.program_id(0)

        @pl.when(k == 0)
        def _():
            o_ref[...] = part

        @pl.when(k > 0)
        def _():
            o_ref[...] += part

    return pl.pallas_call(
        body, name=name, grid=(nk,),
        in_specs=[pl.BlockSpec((tk, m), lambda k: (k, 0))] + [pl.BlockSpec((tk, w), lambda k: (k, 0)) for w in widths],
        out_specs=pl.BlockSpec((m, total), lambda k: (0, 0)),
        out_shape=jax.ShapeDtypeStruct((m, total), F32),
        compiler_params=_cp(VMEM_BIG, ("arbitrary",)),
    )(a, *secs)


def _rms_fwd(x, g, *, tr, name):
    n, d = x.shape

    def body(x_ref, g_ref, o_ref):
        xv = x_ref[...]
        r = lax.rsqrt(jnp.mean(xv * xv, axis=-1, keepdims=True) + EPS)
        o_ref[...] = (xv * r * g_ref[...]).astype(o_ref.dtype)

    return pl.pallas_call(
        body, name=name, grid=(n // tr,),
        in_specs=[pl.BlockSpec((tr, d), lambda i: (i, 0)), pl.BlockSpec((1, d), lambda i: (0, 0))],
        out_specs=pl.BlockSpec((tr, d), lambda i: (i, 0)),
        out_shape=jax.ShapeDtypeStruct((n, d), BF16),
        compiler_params=_cp(VMEM_BIG),
    )(x, g)


def _rms_bwd_math(xv, g, dy):
    r = lax.rsqrt(jnp.mean(xv * xv, axis=-1, keepdims=True) + EPS)
    xh = xv * r
    gdy = dy * g
    dx = r * (gdy - xh * jnp.mean(xh * gdy, axis=-1, keepdims=True))
    return dx, jnp.sum(dy * xh, axis=0, keepdims=True)


def _mlp_up_bwd_norm(dup, w_up, x, g, res, *, tr, name, rider=None):
    n, d = x.shape
    ff = dup.shape[1]

    def body(dup_ref, w_ref, x_ref, g_ref, res_ref, o_ref, ob_ref, dg_ref):
        dh = _nt(dup_ref[...], w_ref[...])
        dx, dg = _rms_bwd_math(x_ref[...], g_ref[...], dh)
        tot = res_ref[...] + dx
        o_ref[...] = tot
        ob_ref[...] = tot.astype(BF16)

        @pl.when(pl.program_id(0) == 0)
        def _():
            dg_ref[...] = dg

        @pl.when(pl.program_id(0) > 0)
        def _():
            dg_ref[...] += dg

    row = pl.BlockSpec((tr, d), lambda i: (i, 0))
    vec = pl.BlockSpec((1, d), lambda i: (0, 0))
    outs, ridden = _hosted_call(
        body, rider, name=name, grid=(n // tr,),
        in_specs=[pl.BlockSpec((tr, ff), lambda i: (i, 0)), pl.BlockSpec((d, ff), lambda i: (0, 0)), row, vec, row],
        out_specs=[row, row, vec],
        out_shape=[jax.ShapeDtypeStruct((n, d), F32), jax.ShapeDtypeStruct((n, d), BF16),
                   jax.ShapeDtypeStruct((1, d), F32)],
        scratch_shapes=[], compiler_params=_cp(VMEM_BIG, ("arbitrary",)), args=(dup, w_up, x, g, res))
    return (*outs, ridden)


def _mlp_down_loss(act, w_down, x1, gf, tgt, *, t_seq, tr, name):
    n, d = x1.shape
    ff = act.shape[1]
    per_seq = t_seq // tr

    def body(a_ref, w_ref, x_ref, g_ref, t_ref, dx_ref, dxb_ref, dg_ref, loss_ref):
        i = pl.program_id(0)
        xv = x_ref[...] + _nn(a_ref[...], w_ref[...])
        g = g_ref[...]
        r = lax.rsqrt(jnp.mean(xv * xv, axis=-1, keepdims=True) + EPS)
        xh = xv * r
        pos = (i % per_seq) * tr + _iota2((tr, 1), 0)
        real = pos >= CHUNK
        err = jnp.where(real, xh * g - t_ref[...], 0.0)
        dy = err * (1.0 / d)
        gdy = dy * g
        dx = r * (gdy - xh * jnp.mean(xh * gdy, axis=-1, keepdims=True))
        dx_ref[...] = dx
        dxb_ref[...] = dx.astype(BF16)
        dg = jnp.sum(dy * xh, axis=0, keepdims=True)
        ls = 0.5 * jnp.sum(jnp.mean(err * err, axis=-1, keepdims=True), axis=0, keepdims=True)
        ls = jnp.where(_iota2((1, LANE), 1) == 0, ls, 0.0)

        @pl.when(i == 0)
        def _():
            dg_ref[...] = dg
            loss_ref[...] = ls

        @pl.when(i > 0)
        def _():
            dg_ref[...] += dg
            loss_ref[...] += ls

    row = pl.BlockSpec((tr, d), lambda i: (i, 0))
    vec = pl.BlockSpec((1, d), lambda i: (0, 0))
    one = pl.BlockSpec((1, LANE), lambda i: (0, 0))
    return pl.pallas_call(
        body, name=name, grid=(n // tr,),
        in_specs=[pl.BlockSpec((tr, ff), lambda i: (i, 0)), pl.BlockSpec((ff, d), lambda i: (0, 0)), row, vec, row],
        out_specs=[row, row, vec, one],
        out_shape=[jax.ShapeDtypeStruct((n, d), F32), jax.ShapeDtypeStruct((n, d), BF16),
                   jax.ShapeDtypeStruct((1, d), F32), jax.ShapeDtypeStruct((1, LANE), F32)],
        compiler_params=_cp(VMEM_BIG, ("arbitrary",)),
    )(act, w_down, x1, gf, tgt)


def _gnorm_fwd(o_dn, o_gla, projp, g_dn, g_gla, *, tr, name):
    n = o_dn.shape[0]
    w = NH * DN_D

    def body(odn_ref, ogl_ref, z_ref, r_ref, gdn_ref, ggl_ref, mix_ref):
        for grp, (o_ref, gate_ref, gain_ref) in enumerate(((odn_ref, z_ref, gdn_ref), (ogl_ref, r_ref, ggl_ref))):
            gain = gain_ref[...]
            for h in range(NH):
                sl = slice(h * DN_D, (h + 1) * DN_D)
                o = o_ref[:, sl].astype(F32)
                z = gate_ref[:, sl].astype(F32)
                r = lax.rsqrt(jnp.mean(o * o, axis=-1, keepdims=True) + EPS)
                y = (o * r * gain) * (z * _sigmoid(z))
                mix_ref[:, grp * w + h * DN_D: grp * w + (h + 1) * DN_D] = y.astype(mix_ref.dtype)

    row = pl.BlockSpec((tr, w), lambda i: (i, 0))
    vec = pl.BlockSpec((1, DN_D), lambda i: (0, 0))
    return pl.pallas_call(
        body, name=name, grid=(n // tr,),
        in_specs=[row, row, pl.BlockSpec((tr, w), lambda i: (i, C_DZ // w)),
                  pl.BlockSpec((tr, w), lambda i: (i, C_GR // w)), vec, vec],
        out_specs=pl.BlockSpec((tr, 2 * w), lambda i: (i, 0)),
        out_shape=jax.ShapeDtypeStruct((n, 2 * w), BF16),
        compiler_params=_cp(VMEM_BIG),
    )(o_dn, o_gla, projp, projp, g_dn, g_gla)


def _gnorm_bwd(dmix, o_dn, o_gla, projp, g_dn, g_gla, *, tr, name):
    n = o_dn.shape[0]
    w = NH * DN_D

    def body(dm_ref, odn_ref, ogl_ref, z_ref, r_ref, gdn_ref, ggl_ref,
             dodn_ref, ddz_ref, dogl_ref, dgr_ref, dgdn_ref, dggl_ref):
        first = pl.program_id(0) == 0
        groups = ((odn_ref, z_ref, gdn_ref, dodn_ref, ddz_ref, dgdn_ref),
                  (ogl_ref, r_ref, ggl_ref, dogl_ref, dgr_ref, dggl_ref))
        for grp, (o_ref, gate_ref, gain_ref, do_ref, dgate_ref, dgain_ref) in enumerate(groups):
            gain = gain_ref[...]
            dgain = jnp.zeros((1, DN_D), F32)
            for h in range(NH):
                sl = slice(h * DN_D, (h + 1) * DN_D)
                o = o_ref[:, sl].astype(F32)
                z = gate_ref[:, sl].astype(F32)
                dm = dm_ref[:, grp * w + h * DN_D: grp * w + (h + 1) * DN_D].astype(F32)
                r = lax.rsqrt(jnp.mean(o * o, axis=-1, keepdims=True) + EPS)
                oh = o * r
                s = _sigmoid(z)
                dn = dm * (z * s)
                dgate_ref[:, sl] = (dm * (oh * gain) * (s * (1.0 + z * (1.0 - s)))).astype(dgate_ref.dtype)
                gdn = dn * gain
                do_ref[:, sl] = (r * (gdn - oh * jnp.mean(oh * gdn, axis=-1, keepdims=True))).astype(do_ref.dtype)
                dgain = dgain + jnp.sum(dn * oh, axis=0, keepdims=True)

            @pl.when(first)
            def _():
                dgain_ref[...] = dgain

            @pl.when(jnp.logical_not(first))
            def _():
                dgain_ref[...] += dgain

    row = pl.BlockSpec((tr, w), lambda i: (i, 0))
    vec = pl.BlockSpec((1, DN_D), lambda i: (0, 0))
    big = jax.ShapeDtypeStruct((n, w), F32)
    gate = jax.ShapeDtypeStruct((n, w), BF16)
    small = jax.ShapeDtypeStruct((1, DN_D), F32)
    return pl.pallas_call(
        body, name=name, grid=(n // tr,),
        in_specs=[pl.BlockSpec((tr, 2 * w), lambda i: (i, 0)), row, row,
                  pl.BlockSpec((tr, w), lambda i: (i, C_DZ // w)), pl.BlockSpec((tr, w), lambda i: (i, C_GR // w)), vec, vec],
        out_specs=[row, row, row, row, vec, vec],
        out_shape=[gate, gate, gate, gate, small, small],
        compiler_params=_cp(VMEM_BIG),
    )(dmix, o_dn, o_gla, projp, projp, g_dn, g_gla)


QKV_W = 3 * NH * DN_D
HALO = 8


def _conv_z(xs_ref, cw_ref, tt):
    z = cw_ref[0:1, :] * xs_ref[pl.ds(HALO - 3, tt), :]
    for j in range(1, 4):
        z = z + cw_ref[j:j + 1, :] * xs_ref[pl.ds(HALO - 3 + j, tt), :]
    return z


def _dnprep_fwd(projp, conv_w, *, bsz, t_seq, tt, name):
    n = bsz * t_seq
    per_seq = t_seq // tt
    hw = NH * DN_D

    def body(x_ref, halo_ref, cw_ref, q_ref, k_ref, v_ref, xs_ref):
        i = pl.program_id(1)
        xs_ref[0:HALO, :] = jnp.where(i == 0, 0.0, halo_ref[...].astype(F32))
        xs_ref[HALO:HALO + tt, :] = x_ref[...].astype(F32)
        z = _conv_z(xs_ref, cw_ref, tt)
        a = z * _sigmoid(z)
        for grp, o_ref in enumerate((q_ref, k_ref)):
            for h in range(NH):
                ah = a[:, grp * hw + h * DN_D: grp * hw + (h + 1) * DN_D]
                rs = lax.rsqrt(jnp.sum(ah * ah, axis=-1, keepdims=True) + EPS)
                o_ref[:, h * DN_D:(h + 1) * DN_D] = (ah * rs).astype(o_ref.dtype)
        v_ref[...] = a[:, 2 * hw:3 * hw].astype(v_ref.dtype)

    def halo_map(b, i):
        return (jnp.maximum((b * t_seq + i * tt) // HALO - 1, 0), 0)

    out = pl.BlockSpec((tt, hw), lambda b, i: (b * per_seq + i, 0))
    sds = jax.ShapeDtypeStruct((n, hw), BF16)
    return pl.pallas_call(
        body, name=name, grid=(bsz, per_seq),
        in_specs=[pl.BlockSpec((tt, QKV_W), lambda b, i: (b * per_seq + i, 0)),
                  pl.BlockSpec((HALO, QKV_W), halo_map),
                  pl.BlockSpec((4, QKV_W), lambda b, i: (0, 0))],
        out_specs=[out, out, out], out_shape=[sds, sds, sds],
        scratch_shapes=[pltpu.VMEM((tt + HALO, QKV_W), F32)],
        compiler_params=_cp(VMEM_BIG),
    )(projp, projp, conv_w)


def _dnprep_bwd_a(projp, conv_w, dq, dk, dv, *, bsz, t_seq, tt, name):
    n = bsz * t_seq
    per_seq = t_seq // tt
    hw = NH * DN_D

    def body(x_ref, halo_ref, cw_ref, dq_ref, dk_ref, dv_ref, dz_ref, dcw_ref, xs_ref):
        b, i = pl.program_id(0), pl.program_id(1)
        xs_ref[0:HALO, :] = jnp.where(i == 0, 0.0, halo_ref[...].astype(F32))
        xs_ref[HALO:HALO + tt, :] = x_ref[...].astype(F32)
        z = _conv_z(xs_ref, cw_ref, tt)
        s = _sigmoid(z)
        a = z * s
        dsilu = s * (1.0 + z * (1.0 - s))
        for grp, d_ref in enumerate((dq_ref, dk_ref)):
            for h in range(NH):
                sl = slice(grp * hw + h * DN_D, grp * hw + (h + 1) * DN_D)
                ah = a[:, sl]
                rs = lax.rsqrt(jnp.sum(ah * ah, axis=-1, keepdims=True) + EPS)
                y = ah * rs
                dy = d_ref[:, h * DN_D:(h + 1) * DN_D]
                da = rs * (dy - y * jnp.sum(dy * y, axis=-1, keepdims=True))
                dz_ref[:, sl] = da * dsilu[:, sl]
        dz_ref[:, 2 * hw:3 * hw] = dv_ref[...] * dsilu[:, 2 * hw:3 * hw]
        dz = dz_ref[...]
        first = jnp.logical_and(b == 0, i == 0)
        for j in range(4):
            part = jnp.sum(dz * xs_ref[pl.ds(HALO - 3 + j, tt), :], axis=0, keepdims=True)

            @pl.when(first)
            def _():
                dcw_ref[j:j + 1, :] = part

            @pl.when(jnp.logical_not(first))
            def _():
                dcw_ref[j:j + 1, :] += part

    def halo_map(b, i):
        return (jnp.maximum((b * t_seq + i * tt) // HALO - 1, 0), 0)

    hrow = pl.BlockSpec((tt, hw), lambda b, i: (b * per_seq + i, 0))
    return pl.pallas_call(
        body, name=name, grid=(bsz, per_seq),
        in_specs=[pl.BlockSpec((tt, QKV_W), lambda b, i: (b * per_seq + i, 0)),
                  pl.BlockSpec((HALO, QKV_W), halo_map),
                  pl.BlockSpec((4, QKV_W), lambda b, i: (0, 0)), hrow, hrow, hrow],
        out_specs=[pl.BlockSpec((tt, QKV_W), lambda b, i: (b * per_seq + i, 0)),
                   pl.BlockSpec((4, QKV_W), lambda b, i: (0, 0))],
        out_shape=[jax.ShapeDtypeStruct((n, QKV_W), F32), jax.ShapeDtypeStruct((4, QKV_W), F32)],
        scratch_shapes=[pltpu.VMEM((tt + HALO, QKV_W), F32)],
        compiler_params=_cp(VMEM_BIG),
    )(projp, projp, conv_w, dq, dk, dv)


def _dnprep_bwd_b(dz, conv_w, *, bsz, t_seq, tt, name):
    n = bsz * t_seq
    per_seq = t_seq // tt
    last_blk = n // HALO - 1

    def body(dz_ref, halo_ref, cw_ref, dx_ref, ds_ref):
        i = pl.program_id(1)
        ds_ref[0:tt, :] = dz_ref[...].astype(F32)
        ds_ref[tt:tt + HALO, :] = jnp.where(i == per_seq - 1, 0.0, halo_ref[...].astype(F32))
        dx = cw_ref[0:1, :] * ds_ref[pl.ds(3, tt), :]
        for j in range(1, 4):
            dx = dx + cw_ref[j:j + 1, :] * ds_ref[pl.ds(3 - j, tt), :]
        dx_ref[...] = dx.astype(dx_ref.dtype)

    def halo_map(b, i):
        return (jnp.minimum((b * t_seq + (i + 1) * tt) // HALO, last_blk), 0)

    row = pl.BlockSpec((tt, QKV_W), lambda b, i: (b * per_seq + i, 0))
    return pl.pallas_call(
        body, name=name, grid=(bsz, per_seq),
        in_specs=[row, pl.BlockSpec((HALO, QKV_W), halo_map), pl.BlockSpec((4, QKV_W), lambda b, i: (0, 0))],
        out_specs=row, out_shape=jax.ShapeDtypeStruct((n, QKV_W), BF16),
        scratch_shapes=[pltpu.VMEM((tt + HALO, QKV_W), F32)],
        compiler_params=_cp(VMEM_BIG),
    )(dz, dz, conv_w)


def _masks64():
    r = _iota2((CHUNK, CHUNK), 0)
    c = _iota2((CHUNK, CHUNK), 1)
    return r, c


def _group(nc_seq, target=5):
    return max(g for g in range(1, target + 1) if nc_seq % g == 0)


def _round_robin(chains):
    live = list(chains)
    while live:
        nxt = []
        for ch in live:
            try:
                next(ch)
                nxt.append(ch)
            except StopIteration:
                pass
        live = nxt
        yield


def _run(chains):
    for _ in _round_robin(chains):
        pass


def _per_chunk(inner, kinds, grp):
    def body(*refs):
        chains = []
        for gi in range(grp):
            views = []
            for r, kind in zip(refs, kinds):
                if kind == "row":
                    views.append(r.at[pl.ds(gi * CHUNK, CHUNK)])
                elif kind == "lead":
                    views.append(r.at[pl.ds(gi, 1)])
                else:
                    views.append(r)
            chains.append(inner(gi, *views))
        _run(chains)
    return body


def _accumulate(ref, val, gi):
    if gi > 0:
        ref[...] += val
        return
    first = pl.program_id(0) == 0

    @pl.when(first)
    def _():
        ref[...] = val

    @pl.when(jnp.logical_not(first))
    def _():
        ref[...] += val


ANY = pl.BlockSpec(memory_space=pl.ANY)


def _place():
    return lax.axis_index("x"), lax.axis_index("y"), lax.axis_index("c")


def _other_chips(x, y):
    return [(1 - x, y, 2 * (1 - x) + y), (x, 1 - y, 2 * x + 1 - y), (1 - x, 1 - y, 2 * (1 - x) + 1 - y)]


class _GatherRider:
    def __init__(self, bufs, split):
        self.inputs = list(bufs)
        self.split = list(split)
        self.out_shapes = [jax.ShapeDtypeStruct(b.shape, b.dtype) for b in bufs]
        self.aliases = {i: i for i in range(len(bufs))}
        self.sems = [pltpu.SemaphoreType.DMA((len(bufs), 3))] * 4

    def _rows(self, k, buf, c, mine=True):
        r = buf.shape[1]
        if not self.split[k]:
            return pl.ds(0, r)
        return pl.ds((c if mine else 1 - c) * (r // 2), r // 2)

    def _ici(self, k, d, bufs, sems, c, px, py, block):
        rows = self._rows(k, bufs[k], c)
        return pltpu.make_async_remote_copy(
            src_ref=bufs[k].at[block, rows, :], dst_ref=bufs[k].at[block, rows, :], send_sem=sems[0].at[k, d],
            recv_sem=sems[1].at[k, d], device_id=(px, py, c), device_id_type=MESH)

    def _pass(self, k, d, bufs, sems, x, y, c, block, mine):
        rows = self._rows(k, bufs[k], c, mine)
        return pltpu.make_async_remote_copy(
            src_ref=bufs[k].at[block, rows, :], dst_ref=bufs[k].at[block, rows, :], send_sem=sems[2].at[k, d],
            recv_sem=sems[3].at[k, d], device_id=(x, y, 1 - c), device_id_type=MESH)

    def first(self, in_refs, bufs, sems):
        x, y, c = _place()
        for k in range(len(bufs)):
            for d, (px, py, _) in enumerate(_other_chips(x, y)):
                self._ici(k, d, bufs, sems, c, px, py, 2 * x + y).start()

    def last(self, in_refs, bufs, sems):
        x, y, c = _place()
        chips = _other_chips(x, y)
        for k in range(len(bufs)):
            for d, (px, py, pj) in enumerate(chips):
                self._ici(k, d, bufs, sems, c, px, py, pj).wait_recv()
                if self.split[k]:
                    self._pass(k, d, bufs, sems, x, y, c, pj, True).start()
        for k in range(len(bufs)):
            for d, (px, py, pj) in enumerate(chips):
                if self.split[k]:
                    self._pass(k, d, bufs, sems, x, y, c, pj, False).wait_recv()
                    self._pass(k, d, bufs, sems, x, y, c, pj, True).wait_send()
                self._ici(k, d, bufs, sems, c, px, py, 2 * x + y).wait_send()


def _hosted_call(body, rider, *, name, grid, in_specs, out_specs, out_shape, scratch_shapes, compiler_params, args):
    if rider is None:
        outs = pl.pallas_call(body, name=name, grid=grid, in_specs=in_specs, out_specs=out_specs, out_shape=out_shape,
                              scratch_shapes=scratch_shapes, compiler_params=compiler_params)(*args)
        return list(outs), []
    n_in, n_out, n_scr = len(in_specs), len(out_specs), len(scratch_shapes)
    r_in, r_out = len(rider.inputs), len(rider.out_shapes)
    compiler_params = _cp(compiler_params.vmem_limit_bytes, ("arbitrary",) * len(grid))

    def full_body(*refs):
        ins = refs[:n_in]
        rins = refs[n_in:n_in + r_in]
        outs = refs[n_in + r_in:n_in + r_in + n_out]
        routs = refs[n_in + r_in + n_out:n_in + r_in + n_out + r_out]
        rest = refs[n_in + r_in + n_out + r_out:]
        scr, sems = rest[:n_scr], rest[n_scr:]
        ids = [pl.program_id(a) for a in range(len(grid))]
        is_first = functools.reduce(jnp.logical_and, [i == 0 for i in ids])
        is_last = functools.reduce(jnp.logical_and, [i == g - 1 for i, g in zip(ids, grid)])

        @pl.when(is_first)
        def _():
            rider.first(rins, routs, sems)

        body(*ins, *outs, *scr)

        @pl.when(is_last)
        def _():
            rider.last(rins, routs, sems)

    res = pl.pallas_call(
        full_body, name=name, grid=grid, in_specs=list(in_specs) + [ANY] * r_in,
        out_specs=list(out_specs) + [ANY] * r_out, out_shape=list(out_shape) + list(rider.out_shapes),
        input_output_aliases={n_in + i: n_out + o for i, o in rider.aliases.items()},
        scratch_shapes=list(scratch_shapes) + list(rider.sems), compiler_params=compiler_params,
    )(*args, *rider.inputs)
    return list(res[:n_out]), list(res[n_out:])


def _exchange_now(rider, *, name):
    r_in = len(rider.inputs)

    def body(*refs):
        rins = refs[:r_in]
        routs = refs[r_in:r_in + len(rider.out_shapes)]
        sems = refs[r_in + len(rider.out_shapes):]
        rider.first(rins, routs, sems)
        rider.last(rins, routs, sems)

    return pl.pallas_call(
        body, name=name, in_specs=[ANY] * r_in, out_specs=[ANY] * len(rider.out_shapes), out_shape=list(rider.out_shapes),
        input_output_aliases=dict(rider.aliases), scratch_shapes=list(rider.sems),
    )(*rider.inputs)


def _to_slot(where, a, dtype, *, name):
    r, cols = a.shape
    tr = _tile(r, 256, 16) if r > 256 else r

    def body(w_ref, a_ref, o_ref):
        o_ref[0] = a_ref[...].astype(o_ref.dtype)

    return pl.pallas_call(
        body, name=name,
        grid_spec=pltpu.PrefetchScalarGridSpec(
            num_scalar_prefetch=1, grid=(r // tr,),
            in_specs=[pl.BlockSpec((tr, cols), lambda i, w: (i, 0))],
            out_specs=pl.BlockSpec((1, tr, cols), lambda i, w: (w[1], i, 0))),
        out_shape=jax.ShapeDtypeStruct((N_CHIPS, r, cols), dtype), compiler_params=_cp(VMEM_BIG),
    )(where, a)


def _tri_inv(a_strict):
    r, c = _masks64()
    eye = (r == c).astype(F32)
    blk16 = (r // 16) == (c // 16)
    blk32 = (r // 32) == (c // 32)
    ld = jnp.where(blk16, a_strict, 0.0)
    x = eye - ld
    p = _nn(ld, ld)
    yield
    for step in range(3):
        xp = _nn(x, p)
        if step < 2:
            p = _nn(p, p)
        x = x + xp
        yield
    for lk in (jnp.where(jnp.logical_and(blk32, jnp.logical_not(blk16)), a_strict, 0.0),
               jnp.where(blk32, 0.0, a_strict)):
        y = x - eye
        s = lk + _nn(y, lk)
        yield
        x = x - s - _nn(s, y)
        yield
    return x


def _dn_gates(sa, alog, dtb, chunk_in_seq):
    rows = _iota2((CHUNK, LANE), 0)
    valid = jnp.logical_or(rows >= N_PAD, chunk_in_seq > 0)
    beta_t = _sigmoid(sa)
    ea = jnp.exp(alog)
    g_t = jnp.where(valid, -ea * _softplus(sa + dtb), 0.0)
    r, c = _masks64()
    ltri = (r >= c).astype(F32)
    gam_t = _tri_sum(ltri, g_t)
    return beta_t, g_t, gam_t, valid, ea


def _dn_intra_fwd(qn, kn, v, projp, alog_row, dtb_row, *, nc_seq, name, rider=None):
    n = qn.shape[0]
    nct = n // CHUNK
    hw = NH * DN_D
    scale = DN_D ** -0.5

    grp = _group(nc_seq)

    def inner(gi, q_ref, k_ref, v_ref, sa_ref, al_ref, dt_ref, u_ref, w_ref, qg_ref, kd_ref, p_ref, t_ref, gl_ref):
        ci = (pl.program_id(0) * grp + gi) % nc_seq
        beta_t, _, gam_t, _, _ = _dn_gates(sa_ref[...], al_ref[...], dt_ref[...], ci)
        yield
        gam_tt = gam_t.T
        r, c = _masks64()
        incl = r >= c
        strict = r > c

        def head(h):
            sl = slice(h * DN_D, (h + 1) * DN_D)
            beta_w = jnp.broadcast_to(beta_t[:, h:h + 1], (CHUNK, DN_D))
            gam_w = jnp.broadcast_to(gam_t[:, 4 + h:5 + h], (CHUNK, DN_D))
            gam_row = gam_tt[4 + h:5 + h, :]
            gl = gam_t[CHUNK - 1:CHUNK, 4 + h:5 + h]
            dec = jnp.exp(jnp.where(incl, gam_w[:, 0:CHUNK] - gam_row, -jnp.inf))
            kh = k_ref[:, sl].astype(F32)
            qh = q_ref[:, sl].astype(F32) * scale
            vh = v_ref[:, sl].astype(F32)
            kk = _nt(kh, kh)
            qk = _nt(qh, kh)
            yield
            a = jnp.where(strict, beta_w[:, 0:CHUNK] * kk * dec, 0.0)
            tm = yield from _tri_inv(a)
            egam_w = jnp.exp(gam_w)
            u_ref[:, sl] = _nn(tm, beta_w * vh).astype(u_ref.dtype)
            w_ref[:, sl] = _nn(tm, (beta_w * egam_w) * kh).astype(w_ref.dtype)
            qg_ref[:, sl] = (egam_w * qh).astype(qg_ref.dtype)
            kd_ref[:, sl] = (jnp.exp(gl - gam_w) * kh).astype(kd_ref.dtype)
            p_ref[0, h] = qk * dec
            t_ref[0, h] = tm
            gl_ref[0, h:h + 1, :] = jnp.broadcast_to(jnp.exp(gl), (1, LANE))

        yield from _round_robin([head(h) for h in range(NH)])

    rows = grp * CHUNK
    row = pl.BlockSpec((rows, hw), lambda i: (i, 0))
    vec = pl.BlockSpec((1, LANE), lambda i: (0, 0))
    mat = pl.BlockSpec((grp, NH, CHUNK, CHUNK), lambda i: (i, 0, 0, 0))
    big = jax.ShapeDtypeStruct((n, hw), BF16)
    msd = jax.ShapeDtypeStruct((nct, NH, CHUNK, CHUNK), F32)
    kinds = ["row"] * 4 + ["whole"] * 2 + ["row"] * 4 + ["lead"] * 3
    outs, ridden = _hosted_call(
        _per_chunk(inner, kinds, grp), rider, name=name, grid=(nct // grp,),
        in_specs=[row, row, row, pl.BlockSpec((rows, LANE), lambda i: (i, 0)), vec, vec],
        out_specs=[row, row, row, row, mat, mat, pl.BlockSpec((grp, NH, LANE), lambda i: (i, 0, 0))],
        out_shape=[big, big, big, big, msd, msd, jax.ShapeDtypeStruct((nct, NH, LANE), F32)],
        scratch_shapes=[], compiler_params=_cp(VMEM_BIG, ("arbitrary",)),
        args=(qn, kn, v, projp, alog_row, dtb_row))
    return (*outs, ridden)


def _dn_scan_fwd(u, w, qg, kd, p, gl, *, bsz, nc_seq, name, rider=None):
    hw = NH * DN_D
    t_seq = nc_seq * CHUNK
    u, w, qg, kd = (z.reshape(bsz, t_seq, hw) for z in (u, w, qg, kd))
    p = p.reshape(bsz, nc_seq, NH, CHUNK, CHUNK)
    gl = gl.reshape(bsz, nc_seq, NH, LANE)
    grp = _group(nc_seq)

    def body(u_ref, w_ref, qg_ref, kd_ref, p_ref, gl_ref, o_ref, vn_ref, hist_ref, s_ref):
        @pl.when(pl.program_id(0) == 0)
        def _():
            s_ref[...] = jnp.zeros_like(s_ref)

        def chain(b, h, gi):
            sl = slice(h * DN_D, (h + 1) * DN_D)
            rows = pl.ds(gi * CHUNK, CHUNK)
            s = s_ref[b, h]
            hist_ref[b, gi, h] = s.astype(hist_ref.dtype)
            ws = _nn(w_ref[b, rows, sl], s)
            qs = _nn(qg_ref[b, rows, sl], s)
            yield
            vn = u_ref[b, rows, sl] - ws
            vn_ref[b, rows, sl] = vn.astype(vn_ref.dtype)
            o_ref[b, rows, sl] = (qs + _nn(p_ref[b, gi, h], vn)).astype(o_ref.dtype)
            s_ref[b, h] = gl_ref[b, gi, h:h + 1, :] * s + _tn(kd_ref[b, rows, sl], vn)

        for gi in range(grp):
            _run([chain(b, h, gi) for b in range(bsz) for h in range(NH)])

    row = pl.BlockSpec((bsz, grp * CHUNK, hw), lambda i: (0, i, 0))
    outs, ridden = _hosted_call(
        body, rider, name=name, grid=(nc_seq // grp,),
        in_specs=[row, row, row, row, pl.BlockSpec((bsz, grp, NH, CHUNK, CHUNK), lambda i: (0, i, 0, 0, 0)),
                  pl.BlockSpec((bsz, grp, NH, LANE), lambda i: (0, i, 0, 0))],
        out_specs=[row, row, pl.BlockSpec((bsz, grp, NH, DN_D, DN_D), lambda i: (0, i, 0, 0, 0))],
        out_shape=[jax.ShapeDtypeStruct((bsz, t_seq, hw), BF16), jax.ShapeDtypeStruct((bsz, t_seq, hw), BF16),
                   jax.ShapeDtypeStruct((bsz, nc_seq, NH, DN_D, DN_D), BF16)],
        scratch_shapes=[pltpu.VMEM((bsz, NH, DN_D, DN_D), F32)],
        compiler_params=_cp(VMEM_BIG, ("arbitrary",)), args=(u, w, qg, kd, p, gl))
    o, vn, hist = outs
    return o.reshape(bsz * t_seq, hw), vn.reshape(bsz * t_seq, hw), hist, ridden


def _dn_scan_bwd(do, w, qg, kd, vn, p, gl, hist, *, bsz, nc_seq, name):
    hw = NH * DN_D
    t_seq = nc_seq * CHUNK
    do, w, qg, kd, vn = (z.reshape(bsz, t_seq, hw) for z in (do, w, qg, kd, vn))
    p = p.reshape(bsz, nc_seq, NH, CHUNK, CHUNK)
    gl = gl.reshape(bsz, nc_seq, NH, LANE)
    grp = _group(nc_seq)

    def body(do_ref, w_ref, qg_ref, kd_ref, vn_ref, p_ref, gl_ref, hist_ref,
             du_ref, dw_ref, dqg_ref, dkd_ref, dgl_ref, ds_ref):
        @pl.when(pl.program_id(0) == 0)
        def _():
            ds_ref[...] = jnp.zeros_like(ds_ref)

        def chain(b, h, gi):
            sl = slice(h * DN_D, (h + 1) * DN_D)
            rows = pl.ds(gi * CHUNK, CHUNK)
            s = hist_ref[b, gi, h]
            dsn = ds_ref[b, h]
            doh = do_ref[b, rows, sl]
            vnh = vn_ref[b, rows, sl]
            kdh = kd_ref[b, rows, sl]
            dvn = _tn(p_ref[b, gi, h], doh) + _nn(kdh, dsn)
            du_ref[b, rows, sl] = dvn.astype(du_ref.dtype)
            dqg_ref[b, rows, sl] = _nt(doh, s).astype(dqg_ref.dtype)
            dkd_ref[b, rows, sl] = _nt(vnh, dsn).astype(dkd_ref.dtype)
            ds_part = _tn(qg_ref[b, rows, sl], doh) + gl_ref[b, gi, h:h + 1, :] * dsn
            dgl = jnp.sum(jnp.sum(dsn * s, axis=0, keepdims=True), axis=1, keepdims=True)
            dgl_ref[b, gi, h:h + 1, :] = jnp.broadcast_to(dgl, (1, LANE))
            yield
            dw_ref[b, rows, sl] = (-_nt(dvn, s)).astype(dw_ref.dtype)
            ds_ref[b, h] = ds_part - _tn(w_ref[b, rows, sl], dvn)

        for gi in reversed(range(grp)):
            _run([chain(b, h, gi) for b in range(bsz) for h in range(NH)])

    steps = nc_seq // grp
    rev = lambda i: steps - 1 - i
    row = pl.BlockSpec((bsz, grp * CHUNK, hw), lambda i: (0, rev(i), 0))
    mat = pl.BlockSpec((bsz, grp, NH, CHUNK, CHUNK), lambda i: (0, rev(i), 0, 0, 0))
    glb = pl.BlockSpec((bsz, grp, NH, LANE), lambda i: (0, rev(i), 0, 0))
    big = jax.ShapeDtypeStruct((bsz, t_seq, hw), BF16)
    outs = pl.pallas_call(
        body, name=name, grid=(steps,),
        in_specs=[row, row, row, row, row, mat, glb,
                  pl.BlockSpec((bsz, grp, NH, DN_D, DN_D), lambda i: (0, rev(i), 0, 0, 0))],
        out_specs=[row, row, row, row, glb],
        out_shape=[big, big, jax.ShapeDtypeStruct(big.shape, F32), jax.ShapeDtypeStruct(big.shape, F32),
                   jax.ShapeDtypeStruct((bsz, nc_seq, NH, LANE), F32)],
        scratch_shapes=[pltpu.VMEM((bsz, NH, DN_D, DN_D), F32)],
        compiler_params=_cp(VMEM_BIG, ("arbitrary",)),
    )(do, w, qg, kd, vn, p, gl, hist)
    du, dw, dqg, dkd, dgl = outs
    n = bsz * t_seq
    return (du.reshape(n, hw), dw.reshape(n, hw), dqg.reshape(n, hw), dkd.reshape(n, hw),
            dgl.reshape(bsz * nc_seq, NH, LANE))


def _dn_intra_bwd(qn, kn, v, projp, alog_row, dtb_row, u, w, tmat, du, dw, dqg, dkd, do, vn, dgl, *, nc_seq, name,
                  rider=None):
    n = qn.shape[0]
    nct = n // CHUNK
    hw = NH * DN_D
    scale = DN_D ** -0.5

    grp = _group(nc_seq)

    def inner(gi, q_ref, k_ref, v_ref, sa_ref, al_ref, dt_ref, u_ref, w_ref, t_ref, du_ref, dw_ref, dqg_ref, dkd_ref,
              do_ref, vn_ref, dgl_ref, dq_ref, dk_ref, dv_ref, dsa_ref, dal_ref, ddt_ref):
        ci = (pl.program_id(0) * grp + gi) % nc_seq
        sa = sa_ref[...]
        beta_t, g_t, gam_t, valid, ea = _dn_gates(sa, al_ref[...], dt_ref[...], ci)
        yield
        lane = _iota2((CHUNK, LANE), 1)
        gates_t = jnp.where(lane < 4, beta_t, gam_t).T
        r, c = _masks64()
        incl, strict, upper, supper = r >= c, r > c, r <= c, r < c
        rows1 = _iota2((CHUNK, 1), 0)
        acc = [jnp.zeros((CHUNK, LANE), F32)]

        def head(h):
            sl = slice(h * DN_D, (h + 1) * DN_D)
            beta_w = jnp.broadcast_to(beta_t[:, h:h + 1], (CHUNK, DN_D))
            gam_w = jnp.broadcast_to(gam_t[:, 4 + h:5 + h], (CHUNK, DN_D))
            beta_s, gam_s = beta_w[:, 0:CHUNK], gam_w[:, 0:CHUNK]
            beta_row = gates_t[h:h + 1, :]
            gam_row = gates_t[4 + h:5 + h, :]
            gl = gam_t[CHUNK - 1:CHUNK, 4 + h:5 + h]
            dec = jnp.exp(jnp.where(incl, gam_s - gam_row, -jnp.inf))
            dec_t = jnp.exp(jnp.where(upper, gam_row - gam_s, -jnp.inf))
            egam_w = jnp.exp(gam_w)
            ekd_w = jnp.exp(gl - gam_w)
            kh = k_ref[:, sl].astype(F32)
            qh = q_ref[:, sl].astype(F32) * scale
            vh = v_ref[:, sl].astype(F32)
            uh = u_ref[:, sl]
            wh = w_ref[:, sl]
            doh = do_ref[:, sl]
            vnh = vn_ref[:, sl]
            kk = _nt(kh, kh)
            qk = _nt(qh, kh)
            qk_t = _nt(kh, qh)
            dp = _nt(doh, vnh)
            dp_t = _nt(vnh, doh)
            tm_t = t_ref[0, h].T
            dvb = _nn(tm_t, du_ref[:, sl])
            dkg = _nn(tm_t, dw_ref[:, sl])
            yield
            m = _nt(dvb, uh) + _nt(dkg, wh)
            m_t = _nt(uh, dvb) + _nt(wh, dkg)
            yield
            da = jnp.where(strict, -m, 0.0)
            da_t = jnp.where(supper, -m_t, 0.0)
            a = jnp.where(strict, beta_s * kk * dec, 0.0)
            a_t = jnp.where(supper, beta_row * kk * dec_t, 0.0)
            dad = da * dec
            dad_t = da_t * dec_t
            dpm = jnp.where(incl, dp, 0.0)
            dpm_t = jnp.where(upper, dp_t, 0.0)
            e = da * a + dpm * (qk * dec)
            e_t = da_t * a_t + dpm_t * (qk_t * dec_t)
            dqgh = dqg_ref[:, sl].astype(F32)
            dkdh = dkd_ref[:, sl].astype(F32)
            bg_w = beta_w * egam_w
            dkh = (_nn(beta_s * dad, kh) + _nn(beta_row * dad_t, kh) + _nn(dpm_t * dec_t, qh)
                   + bg_w * dkg + ekd_w * dkdh)
            dqh = _nn(dpm * dec, kh) + egam_w * dqgh
            t_kd = dkdh * (ekd_w * kh)
            dbeta = (jnp.sum(dad * kk, axis=1, keepdims=True)
                     + jnp.sum(dkg * (egam_w * kh) + dvb * vh, axis=1, keepdims=True))
            dgam = (jnp.sum(e - e_t, axis=1, keepdims=True)
                    + jnp.sum(dkg * (bg_w * kh) + dqgh * (egam_w * qh) - t_kd, axis=1, keepdims=True))
            dgam_last = (jnp.sum(jnp.sum(t_kd, axis=0, keepdims=True), axis=1, keepdims=True)
                         + dgl_ref[0, h:h + 1, 0:1] * jnp.exp(gl))
            dgam = dgam + jnp.where(rows1 == CHUNK - 1, dgam_last, 0.0)
            dq_ref[:, sl] = (dqh * scale).astype(dq_ref.dtype)
            dk_ref[:, sl] = dkh.astype(dk_ref.dtype)
            dv_ref[:, sl] = (beta_w * dvb).astype(dv_ref.dtype)
            acc[0] = acc[0] + jnp.where(lane == h, dbeta, 0.0) + jnp.where(lane == 4 + h, dgam, 0.0)

        yield from _round_robin([head(h) for h in range(NH)])
        acc_t = acc[0]
        dg_t = _tri_sum(upper, acc_t)
        ddb = acc_t * beta_t * (1.0 - beta_t)
        dda = jnp.where(valid, dg_t * (-ea) * _sigmoid(sa + dt_ref[...]), 0.0)
        dsa_ref[...] = jnp.where(lane < 4, ddb, jnp.where(lane < 8, dda, 0.0)).astype(dsa_ref.dtype)
        in_g = jnp.logical_and(lane >= 4, lane < 8)
        dal = jnp.sum(jnp.where(in_g, dg_t * g_t, 0.0), axis=0, keepdims=True)
        ddt = jnp.sum(jnp.where(in_g, dda, 0.0), axis=0, keepdims=True)
        _accumulate(dal_ref, dal, gi)
        _accumulate(ddt_ref, ddt, gi)

    rows = grp * CHUNK
    row = pl.BlockSpec((rows, hw), lambda i: (i, 0))
    vec = pl.BlockSpec((1, LANE), lambda i: (0, 0))
    mat = pl.BlockSpec((grp, NH, CHUNK, CHUNK), lambda i: (i, 0, 0, 0))
    glb = pl.BlockSpec((grp, NH, LANE), lambda i: (i, 0, 0))
    big = jax.ShapeDtypeStruct((n, hw), F32)
    v128 = jax.ShapeDtypeStruct((1, LANE), F32)
    kinds = (["row"] * 4 + ["whole"] * 2 + ["row"] * 2 + ["lead"] + ["row"] * 6 + ["lead"]
             + ["row"] * 4 + ["whole"] * 2)
    outs, ridden = _hosted_call(
        _per_chunk(inner, kinds, grp), rider, name=name, grid=(nct // grp,),
        in_specs=[row, row, row, pl.BlockSpec((rows, LANE), lambda i: (i, 0)), vec, vec,
                  row, row, mat, row, row, row, row, row, row, glb],
        out_specs=[row, row, row, pl.BlockSpec((rows, LANE), lambda i: (i, 0)), vec, vec],
        out_shape=[big, big, big, jax.ShapeDtypeStruct((n, LANE), BF16), v128, v128],
        scratch_shapes=[], compiler_params=_cp(VMEM_BIG, ("arbitrary",)),
        args=(qn, kn, v, projp, alog_row, dtb_row, u, w, tmat, du, dw, dqg, dkd, do, vn, dgl))
    return (*outs, ridden)


GQ_W = NH * GLA_DK
GV_W = NH * GLA_DV
GLA_NORM = 16.0
MID = CHUNK // 2


def _gla_gates(sb, w2p, gb, chunk_in_seq):
    rows = _iota2((CHUNK, GQ_W), 0)
    valid = jnp.logical_or(rows >= N_PAD, chunk_in_seq > 0)
    graw = _nn(sb, w2p) + gb
    yield
    g = jnp.where(valid, _logsigmoid(graw) * (1.0 / GLA_NORM), 0.0)
    r, c = _masks64()
    bcum = _tri_sum(r >= c, g)
    yield
    return graw, bcum, valid


def _head_mask(h):
    lane = _iota2((1, GQ_W), 1)
    return jnp.logical_and(lane >= h * GLA_DK, lane < (h + 1) * GLA_DK)


def _gla_intra_fwd(projp, gates, w2p, gb, *, nc_seq, name):
    n = projp.shape[0]
    nct = n // CHUNK
    scale = GLA_DK ** -0.5

    grp = _group(nc_seq)
    rows = grp * CHUNK

    def inner(gi, qk_ref, v_ref, sb_ref, w2_ref, gb_ref, oi_ref, qg_ref, kd_ref, gl_ref):
        ci = (pl.program_id(0) * grp + gi) % nc_seq
        _, bc, _ = yield from _gla_gates(sb_ref[...], w2_ref[...], gb_ref[...], ci)
        bref = bc[MID:MID + 1, :]
        bl = bc[CHUNK - 1:CHUNK, :]
        q = qk_ref[:, 0:GQ_W].astype(F32) * scale
        k = qk_ref[:, GQ_W:2 * GQ_W].astype(F32)
        qi = q * jnp.exp(bc - bref)
        ki = k * jnp.exp(bref - bc)
        qg_ref[...] = (q * jnp.exp(bc)).astype(qg_ref.dtype)
        kd_ref[...] = (k * jnp.exp(bl - bc)).astype(kd_ref.dtype)
        gl_ref[0] = jnp.exp(bl)
        r, c = _masks64()
        incl = r >= c
        a = [jnp.where(incl, _nt(jnp.where(_head_mask(h), qi, 0.0), ki), 0.0) for h in range(NH)]
        yield
        for h in range(NH):
            oi_ref[:, h * GLA_DV:(h + 1) * GLA_DV] = _nn(a[h], v_ref[:, h * GLA_DV:(h + 1) * GLA_DV]).astype(oi_ref.dtype)

    kinds = ["row"] * 3 + ["whole"] * 2 + ["row"] * 3 + ["lead"]
    return pl.pallas_call(
        _per_chunk(inner, kinds, grp), name=name, grid=(nct // grp,),
        in_specs=[pl.BlockSpec((rows, 2 * GQ_W), lambda i: (i, C_GQK // (2 * GQ_W))),
                  pl.BlockSpec((rows, GV_W), lambda i: (i, C_GV // GV_W)),
                  pl.BlockSpec((rows, LANE), lambda i: (i, 1)),
                  pl.BlockSpec((LANE, GQ_W), lambda i: (0, 0)), pl.BlockSpec((1, GQ_W), lambda i: (0, 0))],
        out_specs=[pl.BlockSpec((rows, GV_W), lambda i: (i, 0)), pl.BlockSpec((rows, GQ_W), lambda i: (i, 0)),
                   pl.BlockSpec((rows, GQ_W), lambda i: (i, 0)), pl.BlockSpec((grp, 1, GQ_W), lambda i: (i, 0, 0))],
        out_shape=[jax.ShapeDtypeStruct((n, GV_W), BF16), jax.ShapeDtypeStruct((n, GQ_W), BF16),
                   jax.ShapeDtypeStruct((n, GQ_W), BF16), jax.ShapeDtypeStruct((nct, 1, GQ_W), F32)],
        compiler_params=_cp(VMEM_BIG),
    )(projp, projp, gates, w2p, gb)


def _gla_scan_fwd(oi, qg, kd, gl, projp, *, bsz, nc_seq, name, rider=None):
    t_seq = nc_seq * CHUNK
    oi = oi.reshape(bsz, t_seq, GV_W)
    qg, kd = qg.reshape(bsz, t_seq, GQ_W), kd.reshape(bsz, t_seq, GQ_W)
    gl = gl.reshape(bsz, nc_seq, 1, GQ_W)
    pj = projp.reshape(bsz, t_seq, PW)
    grp = _group(nc_seq)

    def body(oi_ref, qg_ref, kd_ref, gl_ref, v_ref, o_ref, hist_ref, st_ref):
        @pl.when(pl.program_id(0) == 0)
        def _():
            st_ref[...] = jnp.zeros_like(st_ref)

        for gi in range(grp):
            rows = pl.ds(gi * CHUNK, CHUNK)
            for b in range(bsz):
                st = st_ref[b]
                hist_ref[b, gi] = st.astype(hist_ref.dtype)
                qgb = qg_ref[b, rows, :]
                kdb = kd_ref[b, rows, :]
                upd = jnp.zeros((GLA_DV, GQ_W), F32)
                for h in range(NH):
                    sl = slice(h * GLA_DV, (h + 1) * GLA_DV)
                    m = _head_mask(h)
                    o_ref[b, rows, sl] = (oi_ref[b, rows, sl] + _nt(jnp.where(m, qgb, 0.0), st)).astype(o_ref.dtype)
                    upd = upd + jnp.where(m, _tn(v_ref[b, rows, sl], kdb), 0.0)
                st_ref[b] = gl_ref[b, gi] * st + upd

    rws = grp * CHUNK
    outs, ridden = _hosted_call(
        body, rider, name=name, grid=(nc_seq // grp,),
        in_specs=[pl.BlockSpec((bsz, rws, GV_W), lambda i: (0, i, 0)),
                  pl.BlockSpec((bsz, rws, GQ_W), lambda i: (0, i, 0)),
                  pl.BlockSpec((bsz, rws, GQ_W), lambda i: (0, i, 0)),
                  pl.BlockSpec((bsz, grp, 1, GQ_W), lambda i: (0, i, 0, 0)),
                  pl.BlockSpec((bsz, rws, GV_W), lambda i: (0, i, C_GV // GV_W))],
        out_specs=[pl.BlockSpec((bsz, rws, GV_W), lambda i: (0, i, 0)),
                   pl.BlockSpec((bsz, grp, GLA_DV, GQ_W), lambda i: (0, i, 0, 0))],
        out_shape=[jax.ShapeDtypeStruct((bsz, t_seq, GV_W), BF16),
                   jax.ShapeDtypeStruct((bsz, nc_seq, GLA_DV, GQ_W), BF16)],
        scratch_shapes=[pltpu.VMEM((bsz, GLA_DV, GQ_W), F32)],
        compiler_params=_cp(VMEM_BIG, ("arbitrary",)), args=(oi, qg, kd, gl, pj))
    return outs[0].reshape(bsz * t_seq, GV_W), outs[1], ridden


def _gla_scan_bwd(do, qg, kd, gl, projp, hist, *, bsz, nc_seq, name):
    t_seq = nc_seq * CHUNK
    do = do.reshape(bsz, t_seq, GV_W)
    qg, kd = qg.reshape(bsz, t_seq, GQ_W), kd.reshape(bsz, t_seq, GQ_W)
    gl = gl.reshape(bsz, nc_seq, 1, GQ_W)
    pj = projp.reshape(bsz, t_seq, PW)
    grp = _group(nc_seq)

    def body(do_ref, qg_ref, kd_ref, gl_ref, v_ref, hist_ref, dqg_ref, dkd_ref, dv_ref, dgl_ref, dst_ref):
        @pl.when(pl.program_id(0) == 0)
        def _():
            dst_ref[...] = jnp.zeros_like(dst_ref)

        for gi in reversed(range(grp)):
            rows = pl.ds(gi * CHUNK, CHUNK)
            for b in range(bsz):
                st = hist_ref[b, gi]
                dst = dst_ref[b]
                qgb = qg_ref[b, rows, :]
                kdb = kd_ref[b, rows, :]
                dqg = jnp.zeros((CHUNK, GQ_W), F32)
                dkd = jnp.zeros((CHUNK, GQ_W), F32)
                add = jnp.zeros((GLA_DV, GQ_W), F32)
                for h in range(NH):
                    sl = slice(h * GLA_DV, (h + 1) * GLA_DV)
                    m = _head_mask(h)
                    doh = do_ref[b, rows, sl]
                    vh = v_ref[b, rows, sl]
                    dqg = dqg + jnp.where(m, _nn(doh, st), 0.0)
                    dkd = dkd + jnp.where(m, _nn(vh, dst), 0.0)
                    dv_ref[b, rows, sl] = _nt(jnp.where(m, kdb, 0.0), dst).astype(dv_ref.dtype)
                    add = add + jnp.where(m, _tn(doh, qgb), 0.0)
                dqg_ref[b, rows, :] = dqg.astype(dqg_ref.dtype)
                dkd_ref[b, rows, :] = dkd.astype(dkd_ref.dtype)
                dgl_ref[b, gi] = jnp.sum(dst * st, axis=0, keepdims=True)
                dst_ref[b] = gl_ref[b, gi] * dst + add

    steps = nc_seq // grp
    rws = grp * CHUNK
    rev = lambda i: steps - 1 - i
    outs = pl.pallas_call(
        body, name=name, grid=(steps,),
        in_specs=[pl.BlockSpec((bsz, rws, GV_W), lambda i: (0, rev(i), 0)),
                  pl.BlockSpec((bsz, rws, GQ_W), lambda i: (0, rev(i), 0)),
                  pl.BlockSpec((bsz, rws, GQ_W), lambda i: (0, rev(i), 0)),
                  pl.BlockSpec((bsz, grp, 1, GQ_W), lambda i: (0, rev(i), 0, 0)),
                  pl.BlockSpec((bsz, rws, GV_W), lambda i: (0, rev(i), C_GV // GV_W)),
                  pl.BlockSpec((bsz, grp, GLA_DV, GQ_W), lambda i: (0, rev(i), 0, 0))],
        out_specs=[pl.BlockSpec((bsz, rws, GQ_W), lambda i: (0, rev(i), 0)),
                   pl.BlockSpec((bsz, rws, GQ_W), lambda i: (0, rev(i), 0)),
                   pl.BlockSpec((bsz, rws, GV_W), lambda i: (0, rev(i), 0)),
                   pl.BlockSpec((bsz, grp, 1, GQ_W), lambda i: (0, rev(i), 0, 0))],
        out_shape=[jax.ShapeDtypeStruct((bsz, t_seq, GQ_W), F32), jax.ShapeDtypeStruct((bsz, t_seq, GQ_W), F32),
                   jax.ShapeDtypeStruct((bsz, t_seq, GV_W), BF16), jax.ShapeDtypeStruct((bsz, nc_seq, 1, GQ_W), F32)],
        scratch_shapes=[pltpu.VMEM((bsz, GLA_DV, GQ_W), F32)],
        compiler_params=_cp(VMEM_BIG, ("arbitrary",)),
    )(do, qg, kd, gl, pj, hist)
    n = bsz * t_seq
    return (outs[0].reshape(n, GQ_W), outs[1].reshape(n, GQ_W), outs[2].reshape(n, GV_W),
            outs[3].reshape(bsz * nc_seq, 1, GQ_W))


def _gla_intra_bwd(projp, gates, w2p, gb, do, dqg, dkd, dvi, dgl, *, nc_seq, name):
    n = projp.shape[0]
    nct = n // CHUNK
    scale = GLA_DK ** -0.5

    grp = _group(nc_seq)
    rows = grp * CHUNK

    def inner(gi, qk_ref, v_ref, sb_ref, w2_ref, gb_ref, do_ref, dqg_ref, dkd_ref, dvi_ref, dgl_ref,
              dqk_ref, dv_ref, dsb_ref, dw2_ref, dgb_ref):
        ci = (pl.program_id(0) * grp + gi) % nc_seq
        sb = sb_ref[...]
        w2 = w2_ref[...]
        graw, bc, valid = yield from _gla_gates(sb, w2, gb_ref[...], ci)
        bref = bc[MID:MID + 1, :]
        bl = bc[CHUNK - 1:CHUNK, :]
        q = qk_ref[:, 0:GQ_W].astype(F32) * scale
        k = qk_ref[:, GQ_W:2 * GQ_W].astype(F32)
        ex1 = jnp.exp(bc - bref)
        ex2 = jnp.exp(bref - bc)
        eb = jnp.exp(bc)
        ekd = jnp.exp(bl - bc)
        qi, ki = q * ex1, k * ex2
        r, c = _masks64()
        incl = r >= c
        upper = r <= c
        a_t, da, da_t = [], [], []
        for h in range(NH):
            sl = slice(h * GLA_DV, (h + 1) * GLA_DV)
            doh = do_ref[:, sl]
            vh = v_ref[:, sl]
            a_t.append(jnp.where(upper, _nt(jnp.where(_head_mask(h), ki, 0.0), qi), 0.0))
            da.append(jnp.where(incl, _nt(doh, vh), 0.0))
            da_t.append(jnp.where(upper, _nt(vh, doh), 0.0))
        yield
        dqi = jnp.zeros((CHUNK, GQ_W), F32)
        dki = jnp.zeros((CHUNK, GQ_W), F32)
        for h in range(NH):
            sl = slice(h * GLA_DV, (h + 1) * GLA_DV)
            m = _head_mask(h)
            dv_ref[:, sl] = (_nn(a_t[h], do_ref[:, sl]) + dvi_ref[:, sl]).astype(dv_ref.dtype)
            dqi = dqi + jnp.where(m, _nn(da[h], ki), 0.0)
            dki = dki + jnp.where(m, _nn(da_t[h], qi), 0.0)
        yield
        dqg = dqg_ref[...].astype(F32)
        dkd = dkd_ref[...].astype(F32)
        dqk_ref[:, 0:GQ_W] = ((dqi * ex1 + dqg * eb) * scale).astype(dqk_ref.dtype)
        dqk_ref[:, GQ_W:2 * GQ_W] = (dki * ex2 + dkd * ekd).astype(dqk_ref.dtype)
        t_qi, t_ki, t_kd = dqi * qi, dki * ki, dkd * (k * ekd)
        db = t_qi - t_ki + dqg * (q * eb) - t_kd
        dbref = jnp.sum(t_ki - t_qi, axis=0, keepdims=True)
        dbl = jnp.sum(t_kd, axis=0, keepdims=True) + dgl_ref[0] * jnp.exp(bl)
        rows = _iota2((CHUNK, GQ_W), 0)
        db = db + jnp.where(rows == MID, dbref, 0.0) + jnp.where(rows == CHUNK - 1, dbl, 0.0)
        dg = _tri_sum(upper, db)
        yield
        dgraw = jnp.where(valid, dg * (1.0 / GLA_NORM) * _sigmoid(-graw), 0.0)
        dsb_ref[...] = _nt(dgraw, w2).astype(dsb_ref.dtype)
        dw2 = _tn(sb, dgraw)
        dgb = jnp.sum(dgraw, axis=0, keepdims=True)
        _accumulate(dw2_ref, dw2, gi)
        _accumulate(dgb_ref, dgb, gi)

    rq = pl.BlockSpec((rows, GQ_W), lambda i: (i, 0))
    rv = pl.BlockSpec((rows, GV_W), lambda i: (i, 0))
    kinds = ["row"] * 3 + ["whole"] * 2 + ["row"] * 4 + ["lead"] + ["row"] * 3 + ["whole"] * 2
    return pl.pallas_call(
        _per_chunk(inner, kinds, grp), name=name, grid=(nct // grp,),
        in_specs=[pl.BlockSpec((rows, 2 * GQ_W), lambda i: (i, C_GQK // (2 * GQ_W))),
                  pl.BlockSpec((rows, GV_W), lambda i: (i, C_GV // GV_W)),
                  pl.BlockSpec((rows, LANE), lambda i: (i, 1)),
                  pl.BlockSpec((LANE, GQ_W), lambda i: (0, 0)), pl.BlockSpec((1, GQ_W), lambda i: (0, 0)),
                  rv, rq, rq, rv, pl.BlockSpec((grp, 1, GQ_W), lambda i: (i, 0, 0))],
        out_specs=[pl.BlockSpec((rows, 2 * GQ_W), lambda i: (i, 0)), rv, pl.BlockSpec((rows, LANE), lambda i: (i, 0)),
                   pl.BlockSpec((LANE, GQ_W), lambda i: (0, 0)), pl.BlockSpec((1, GQ_W), lambda i: (0, 0))],
        out_shape=[jax.ShapeDtypeStruct((n, 2 * GQ_W), BF16), jax.ShapeDtypeStruct((n, GV_W), BF16),
                   jax.ShapeDtypeStruct((n, LANE), BF16), jax.ShapeDtypeStruct((LANE, GQ_W), F32),
                   jax.ShapeDtypeStruct((1, GQ_W), F32)],
        compiler_params=_cp(VMEM_BIG, ("arbitrary",)),
    )(projp, projp, gates, w2p, gb, do, dqg, dkd, dvi, dgl)


SECTIONS = ((C_QKV, 1536), (C_DZ, 512), (C_GQK, 512), (C_GV, 512), (C_GR, 512), (C_SA, 128), (C_SB, 128))


def _inproj_bwd(secs, wp, h0, g1, dx1, *, tr, name):
    n, d = h0.shape

    def body(*refs):
        sec_refs = refs[:len(SECTIONS)]
        wp_ref, h0_ref, g_ref, dx1_ref, o_ref, dg_ref = refs[len(SECTIONS):]
        dh = None
        for s_ref, (off, wd) in zip(sec_refs, SECTIONS):
            part = _nt(s_ref[...], wp_ref[:, off:off + wd])
            dh = part if dh is None else dh + part
        dx, dg = _rms_bwd_math(h0_ref[...], g_ref[...], dh)
        o_ref[...] = dx1_ref[...] + dx

        @pl.when(pl.program_id(0) == 0)
        def _():
            dg_ref[...] = dg

        @pl.when(pl.program_id(0) > 0)
        def _():
            dg_ref[...] += dg

    row = pl.BlockSpec((tr, d), lambda i: (i, 0))
    vec = pl.BlockSpec((1, d), lambda i: (0, 0))
    return pl.pallas_call(
        body, name=name, grid=(n // tr,),
        in_specs=[pl.BlockSpec((tr, wd), lambda i: (i, 0)) for _, wd in SECTIONS]
        + [pl.BlockSpec((d, PW), lambda i: (0, 0)), row, vec, row],
        out_specs=[row, vec],
        out_shape=[jax.ShapeDtypeStruct((n, d), F32), jax.ShapeDtypeStruct((1, d), F32)],
        compiler_params=_cp(VMEM_BIG),
    )(*secs, wp, h0, g1, dx1)


def _adamw(w, g, m, v, *, name, emit_grad=False):
    lead = w.ndim - 2
    r, c = w.shape[-2:]
    tr = _tile(r, 256, 8) if r > 256 else r
    c1 = 1.0 - ADAM_B1 ** ADAM_STEP
    c2 = 1.0 - ADAM_B2 ** ADAM_STEP
    n_out = 4 if emit_grad else 3

    def body(w_ref, g_ref, m_ref, v_ref, *out_refs):
        rd = (lambda ref: ref[0]) if lead else (lambda ref: ref[...])
        gv = g_ref[:, 0:c]
        nm = ADAM_B1 * rd(m_ref) + (1.0 - ADAM_B1) * gv
        nv = ADAM_B2 * rd(v_ref) + (1.0 - ADAM_B2) * (gv * gv)
        res = [-ADAM_LR * ((nm / c1) / (jnp.sqrt(nv / c2) + ADAM_EPS) + ADAM_WD * rd(w_ref)), nm, nv, gv]
        for o_ref, val in zip(out_refs, res):
            if lead:
                o_ref[0] = val
            else:
                o_ref[...] = val

    blk = pl.BlockSpec((1,) * lead + (tr, c), lambda i: (0,) * lead + (i, 0))
    gblk = pl.BlockSpec((tr, g.shape[1]), lambda i: (i, 0))
    sds = jax.ShapeDtypeStruct(w.shape, F32)
    return pl.pallas_call(
        body, name=name, grid=(r // tr,), in_specs=[blk, gblk, blk, blk], out_specs=[blk] * n_out,
        out_shape=[sds] * n_out, compiler_params=_cp(VMEM_BIG),
    )(w, g, m, v)


def _pair_sum(where, g, theirs, *, name):
    lead, r, cols = g.shape
    half = r // 2
    tr = _tile(half, 256, 16)
    nh = half // tr

    def body(w_ref, a_ref, b_ref, o_ref):
        o_ref[...] = (a_ref[...] + b_ref[...]).astype(o_ref.dtype)

    blk = pl.BlockSpec((1, tr, cols), lambda s, i, w: (s, i, 0))
    return pl.pallas_call(
        body, name=name,
        grid_spec=pltpu.PrefetchScalarGridSpec(
            num_scalar_prefetch=1, grid=(lead, nh),
            in_specs=[pl.BlockSpec((1, tr, cols), lambda s, i, w: (s, w[0] * nh + i, 0)), blk], out_specs=blk),
        out_shape=jax.ShapeDtypeStruct((lead, half, cols), BF16), compiler_params=_cp(VMEM_BIG),
    )(where, g, theirs)


def _chip_sum(where, pair, q, *, name):
    _, half, cols = pair.shape
    tr = _tile(half, 256, 16)
    nh = half // tr

    def body(w_ref, own_ref, q1_ref, q2_ref, q3_ref, o_ref):
        f = lambda ref: ref[0].astype(F32)
        o_ref[...] = ((f(own_ref) + f(q1_ref)) + f(q2_ref)) + f(q3_ref)

    def peer(d):
        return pl.BlockSpec((1, tr, cols), lambda i, w: ((w[1] + d) % N_CHIPS, i, 0))

    return pl.pallas_call(
        body, name=name,
        grid_spec=pltpu.PrefetchScalarGridSpec(
            num_scalar_prefetch=1, grid=(nh,),
            in_specs=[peer(0), peer(1), peer(2), peer(3)],
            out_specs=pl.BlockSpec((tr, cols), lambda i, w: (w[0] * nh + i, 0))),
        out_shape=jax.ShapeDtypeStruct((2 * half, cols), F32), compiler_params=_cp(VMEM_BIG),
    )(where, pair, q, q, q)


VM = pl.BlockSpec(memory_space=pltpu.VMEM)


def _row_chunks(rows, n_split):
    size = rows // n_split
    assert size * n_split == rows and size % 16 == 0, (rows, n_split)
    return [(s, pl.ds(s * size, size)) for s in range(n_split)], size


D2D_SPLIT = 4
ICI_SPLIT = 2


def _sibling_halves(grads):
    n_arr = len(grads)

    def body(*refs):
        ins = refs[:n_arr]
        theirs = refs[n_arr:2 * n_arr]
        send_sems, recv_sems = refs[2 * n_arr:]
        x, y, c = _place()
        copies = []
        for k in range(n_arr):
            half = ins[k].shape[1] // 2
            chunks, size = _row_chunks(half, D2D_SPLIT)
            for s, dst_rows in chunks:
                give = pltpu.make_async_remote_copy(
                    src_ref=ins[k].at[:, pl.ds((1 - c) * half + s * size, size), :], dst_ref=theirs[k].at[:, dst_rows, :],
                    send_sem=send_sems.at[k, s], recv_sem=recv_sems.at[k, s], device_id=(x, y, 1 - c),
                    device_id_type=MESH)
                give.start()
                copies.append(give)
        for give in copies:
            give.wait()

    halves = [jax.ShapeDtypeStruct((g.shape[0], g.shape[1] // 2, g.shape[2]), F32) for g in grads]
    sem = pltpu.SemaphoreType.DMA((n_arr, D2D_SPLIT))
    return pl.pallas_call(
        body, name="sibling_halves", in_specs=[ANY] * n_arr, out_specs=[ANY] * n_arr, out_shape=halves,
        scratch_shapes=[sem, sem],
    )(*grads)


def _chip_exchange(parts):
    n_arr = len(parts)

    def body(*refs):
        ins = refs[:n_arr]
        outs = refs[n_arr:2 * n_arr]
        send_sems, recv_sems = refs[2 * n_arr:]
        x, y, c = _place()
        me = 2 * x + y
        sends = []
        for k in range(n_arr):
            chunks, _ = _row_chunks(ins[k].shape[1], ICI_SPLIT)
            for d, (px, py, pj) in enumerate(_other_chips(x, y)):
                for s, rows in chunks:
                    cp = pltpu.make_async_remote_copy(
                        src_ref=ins[k].at[pj, rows, :], dst_ref=outs[k].at[me, rows, :], send_sem=send_sems.at[k, d, s],
                        recv_sem=recv_sems.at[k, d, s], device_id=(px, py, c), device_id_type=MESH)
                    cp.start()
                    sends.append(cp)
        for k in range(n_arr):
            chunks, _ = _row_chunks(ins[k].shape[1], ICI_SPLIT)
            for d, (px, py, pj) in enumerate(_other_chips(x, y)):
                for s, rows in chunks:
                    pltpu.make_async_remote_copy(
                        src_ref=ins[k].at[pj, rows, :], dst_ref=outs[k].at[pj, rows, :], send_sem=send_sems.at[k, d, s],
                        recv_sem=recv_sems.at[k, d, s], device_id=(px, py, c), device_id_type=MESH).wait_recv()
        for cp in sends:
            cp.wait_send()

    sem = pltpu.SemaphoreType.DMA((n_arr, 3, ICI_SPLIT))
    return pl.pallas_call(
        body, name="chip_exchange", in_specs=[ANY] * n_arr, out_specs=[ANY] * n_arr,
        out_shape=[jax.ShapeDtypeStruct(p.shape, p.dtype) for p in parts],
        scratch_shapes=[sem, sem],
    )(*parts)


class _SiblingHalvesRider:
    def __init__(self, grads):
        self.inputs = list(grads)
        self.out_shapes = [jax.ShapeDtypeStruct((g.shape[0], g.shape[1] // 2, g.shape[2]), F32) for g in grads]
        self.aliases = {}
        self.sems = [pltpu.SemaphoreType.DMA((len(grads), D2D_SPLIT))] * 2

    def _copies(self, ins, outs, sems):
        x, y, c = _place()
        for k in range(len(ins)):
            half = ins[k].shape[1] // 2
            chunks, size = _row_chunks(half, D2D_SPLIT)
            for s, dst_rows in chunks:
                yield pltpu.make_async_remote_copy(
                    src_ref=ins[k].at[:, pl.ds((1 - c) * half + s * size, size), :], dst_ref=outs[k].at[:, dst_rows, :],
                    send_sem=sems[0].at[k, s], recv_sem=sems[1].at[k, s], device_id=(x, y, 1 - c), device_id_type=MESH)

    def first(self, ins, outs, sems):
        for cp in self._copies(ins, outs, sems):
            cp.start()

    def last(self, ins, outs, sems):
        for cp in self._copies(ins, outs, sems):
            cp.wait()


class _ChipExchangeRider:
    def __init__(self, parts):
        self.inputs = list(parts)
        self.out_shapes = [jax.ShapeDtypeStruct(p.shape, p.dtype) for p in parts]
        self.aliases = {}
        self.sems = [pltpu.SemaphoreType.DMA((len(parts), 3, ICI_SPLIT))] * 2

    def _copies(self, ins, outs, sems, receiving):
        x, y, c = _place()
        for k in range(len(ins)):
            chunks, _ = _row_chunks(ins[k].shape[1], ICI_SPLIT)
            for d, (px, py, pj) in enumerate(_other_chips(x, y)):
                for s, rows in chunks:
                    yield pltpu.make_async_remote_copy(
                        src_ref=ins[k].at[pj, rows, :], dst_ref=outs[k].at[pj if receiving else 2 * x + y, rows, :],
                        send_sem=sems[0].at[k, d, s], recv_sem=sems[1].at[k, d, s], device_id=(px, py, c),
                        device_id_type=MESH)

    def first(self, ins, outs, sems):
        for cp in self._copies(ins, outs, sems, False):
            cp.start()

    def last(self, ins, outs, sems):
        for cp in self._copies(ins, outs, sems, True):
            cp.wait_recv()
        for cp in self._copies(ins, outs, sems, False):
            cp.wait_send()


def _sibling_join(bufs):
    n_arr = len(bufs)

    def body(*refs):
        bufs_out = refs[n_arr:2 * n_arr]
        send_sems, recv_sems = refs[2 * n_arr:]
        x, y, c = _place()
        copies = []
        for k in range(n_arr):
            half = bufs_out[k].shape[0] // 2
            chunks, size = _row_chunks(half, D2D_SPLIT)
            for s, _ in chunks:
                rows = pl.ds(c * half + s * size, size)
                give = pltpu.make_async_remote_copy(
                    src_ref=bufs_out[k].at[rows, :], dst_ref=bufs_out[k].at[rows, :], send_sem=send_sems.at[k, s],
                    recv_sem=recv_sems.at[k, s], device_id=(x, y, 1 - c), device_id_type=MESH)
                give.start()
                copies.append((k, s, half, size, give))
        for k, s, half, size, give in copies:
            rows = pl.ds((1 - c) * half + s * size, size)
            pltpu.make_async_remote_copy(
                src_ref=bufs_out[k].at[rows, :], dst_ref=bufs_out[k].at[rows, :], send_sem=send_sems.at[k, s],
                recv_sem=recv_sems.at[k, s], device_id=(x, y, 1 - c), device_id_type=MESH).wait_recv()
            give.wait_send()

    sem = pltpu.SemaphoreType.DMA((n_arr, D2D_SPLIT))
    return pl.pallas_call(
        body, name="sibling_join", in_specs=[ANY] * n_arr, out_specs=[ANY] * n_arr,
        out_shape=[jax.ShapeDtypeStruct(b.shape, F32) for b in bufs],
        input_output_aliases={k: k for k in range(n_arr)},
        scratch_shapes=[sem, sem],
    )(*bufs)


PACK_ROWS = 48


def _small_allreduce(pack):
    masks = [(dx, dy, dc) for dx in (0, 1) for dy in (0, 1) for dc in (0, 1)][1:]

    def body(p_ref, o_ref, buf, send_sems, recv_sems):
        x, y, c = _place()
        me = 4 * x + 2 * y + c
        buf[me] = p_ref[...]
        sends = []
        for k, (dx, dy, dc) in enumerate(masks):
            peer = (1 - x if dx else x, 1 - y if dy else y, 1 - c if dc else c)
            cp = pltpu.make_async_remote_copy(
                src_ref=p_ref, dst_ref=buf.at[me], send_sem=send_sems.at[k], recv_sem=recv_sems.at[k],
                device_id=peer, device_id_type=MESH)
            cp.start()
            sends.append(cp)
        for k, (dx, dy, dc) in enumerate(masks):
            peer = (1 - x if dx else x, 1 - y if dy else y, 1 - c if dc else c)
            pj = 4 * peer[0] + 2 * peer[1] + peer[2]
            pltpu.make_async_remote_copy(
                src_ref=p_ref, dst_ref=buf.at[pj], send_sem=send_sems.at[k], recv_sem=recv_sems.at[k],
                device_id=peer, device_id_type=MESH).wait_recv()
        for cp in sends:
            cp.wait_send()
        tot = buf[0]
        for k in range(1, 8):
            tot = tot + buf[k]
        o_ref[...] = tot
        o_ref[0:N_META, :] = tot[0:N_META] + tot[N_META:2 * N_META]

    return pl.pallas_call(
        body, name="small_allreduce", in_specs=[VM], out_specs=VM,
        out_shape=jax.ShapeDtypeStruct((PACK_ROWS, D_MODEL), F32),
        scratch_shapes=[pltpu.VMEM((8, PACK_ROWS, D_MODEL), F32), pltpu.SemaphoreType.DMA((7,)),
                        pltpu.SemaphoreType.DMA((7,))],
    )(pack)


def _pad_lanes(vec, offset):
    k = vec.shape[1]
    return jnp.concatenate([jnp.zeros((1, offset), F32), vec, jnp.zeros((1, LANE - offset - k), F32)], axis=1)


def _local_step(x, tgt, meta, norm1_g, wp, conv_w, a_log, dt_bias, dn_norm_g, gla_w2, gla_b, gla_norm_g,
                w_out, norm2_g, w_up, w_down, final_norm_g, late_gather=None, where=None):
    bsz, s_len, d = x.shape
    t_seq = s_len + CHUNK
    nc_seq = t_seq // CHUNK
    n = bsz * t_seq
    tr = _tile(t_seq, 832)
    tt = _tile(t_seq, 416)

    lead = jnp.concatenate([jnp.zeros((N_PAD, d), F32), meta], axis=0)
    h0 = jnp.concatenate([jnp.broadcast_to(lead[None], (bsz, CHUNK, d)), x], axis=1).reshape(n, d)
    tgt_p = jnp.concatenate([jnp.zeros((bsz, CHUNK, d), F32), tgt], axis=1).reshape(n, d)
    alog_row = _pad_lanes(a_log, 4)
    dtb_row = _pad_lanes(dt_bias, 4)
    w2p = jnp.concatenate([gla_w2, jnp.zeros((LANE - GLA_RANK, GQ_W), F32)], axis=0)

    h = _rms_fwd(h0, norm1_g, tr=tr, name="norm1")
    ride_up, ride_down, ride_out = late_gather if late_gather is not None else (None, None, None)
    res = _mm(h, wp, "nn", tm=tt, tn=PW, tk=d, out_dtypes=(BF16, F32), out_widths=(PW, PW - C_SA),
              epilogue=lambda acc: (acc, acc[:, C_SA:PW]), name="in_proj", rider=ride_up)
    ((projp, gates), got_up) = res if ride_up is not None else (res, None)
    qn, kn, v = _dnprep_fwd(projp, conv_w, bsz=bsz, t_seq=t_seq, tt=tt, name="dn_prep")
    u, w, qg, kd, pmat, tmat, gl, got_down = _dn_intra_fwd(qn, kn, v, gates, alog_row, dtb_row, nc_seq=nc_seq,
                                                           name="dn_intra", rider=ride_down)
    o_dn, vn, hist, got_out = _dn_scan_fwd(u, w, qg, kd, pmat, gl, bsz=bsz, nc_seq=nc_seq, name="dn_scan",
                                           rider=ride_out)
    oi, gqg, gkd, ggl = _gla_intra_fwd(projp, gates, w2p, gla_b, nc_seq=nc_seq, name="gla_intra")
    o_gla, ghist, _ = _gla_scan_fwd(oi, gqg, gkd, ggl, projp, bsz=bsz, nc_seq=nc_seq, name="gla_scan")
    if late_gather is not None:
        w_out = got_out[0].reshape(d, d)
        w_up = got_up[0].transpose(1, 0, 2).reshape(d, D_FF)
        w_down = got_down[0].reshape(D_FF, d)
    mix = _gnorm_fwd(o_dn, o_gla, projp, dn_norm_g, gla_norm_g, tr=tr, name="gated_norm")
    (x1,) = _mm(mix, w_out, "nn", tm=tr, tn=d, tk=d, out_dtypes=(F32,), extras=(h0,),
                epilogue=lambda acc, res: (res + acc,), name="out_proj")
    h2 = _rms_fwd(x1, norm2_g, tr=tr, name="norm2")

    (act,) = _mm(h2, w_up, "nn", tm=tt, tn=D_FF, tk=d, out_dtypes=(BF16,),
                 epilogue=lambda acc: (jnp.square(jnp.maximum(acc, 0.0)),), name="mlp_up", n_chunk=1024)
    dx2, dx2b, d_final_g, loss_tile = _mlp_down_loss(act, w_down, x1, final_norm_g, tgt_p, t_seq=t_seq, tr=tt,
                                                     name="mlp_down_loss")

    (dup,) = _mm(dx2b, w_down, "nt", tm=tt, tn=D_FF, tk=d, out_dtypes=(BF16,), extras=(act,),
                 epilogue=lambda acc, a: (acc * (2.0 * jnp.sqrt(a.astype(F32))),), name="mlp_down_bwd", n_chunk=1024)
    tk2 = 2 * tr if n % (2 * tr) == 0 else tr
    (d_w_down,) = _mm(act, dx2b, "tn", tm=D_FF // 2, tn=d, tk=tk2, out_dtypes=(F32,), name="w_down_grad")
    (d_w_up,) = _mm(h2, dup, "tn", tm=d, tn=D_FF // 2, tk=tk2, out_dtypes=(F32,), name="w_up_grad")
    mlp_sm = [d_w_up.reshape(d, N_CHIPS, D_FF // N_CHIPS).transpose(1, 0, 2), d_w_down.reshape(N_CHIPS, D_FF // N_CHIPS, d)]
    ride1 = _SiblingHalvesRider(mlp_sm) if where is not None else None
    dx1, dx1b, d_norm2_g, theirs = _mlp_up_bwd_norm(dup, w_up, x1, norm2_g, dx2, tr=tt, name="mlp_up_bwd_norm",
                                                    rider=ride1)
    ride2 = None
    if where is not None:
        mlp_pair = [_pair_sum(where, a, b, name=f"pair_sum_mlp{k}") for k, (a, b) in enumerate(zip(mlp_sm, theirs))]
        ride2 = _ChipExchangeRider(mlp_pair)

    (dmix,) = _mm(dx1b, w_out, "nt", tm=tr, tn=d, tk=d, out_dtypes=(BF16,), name="out_proj_bwd")
    (d_w_out,) = _mm(mix, dx1b, "tn", tm=d, tn=d, tk=tr, out_dtypes=(F32,), name="w_out_grad")
    do_dn, ddz, do_gla, dgr, d_dn_norm_g, d_gla_norm_g = _gnorm_bwd(
        dmix, o_dn, o_gla, projp, dn_norm_g, gla_norm_g, tr=tr, name="gated_norm_bwd")
    du, dw, dqg, dkd, dgl = _dn_scan_bwd(do_dn, w, qg, kd, vn, pmat, gl, hist, bsz=bsz, nc_seq=nc_seq,
                                          name="dn_scan_bwd")
    dqn, dkn, dv, dsa, d_alog, d_dtb, mlp_parts = _dn_intra_bwd(
        qn, kn, v, gates, alog_row, dtb_row, u, w, tmat, du, dw, dqg, dkd, do_dn, vn, dgl, nc_seq=nc_seq,
        name="dn_intra_bwd", rider=ride2)
    dz, d_conv_w = _dnprep_bwd_a(projp, conv_w, dqn, dkn, dv, bsz=bsz, t_seq=t_seq, tt=tt, name="dn_prep_bwd")
    dcin = _dnprep_bwd_b(dz, conv_w, bsz=bsz, t_seq=t_seq, tt=tt, name="conv_bwd")
    gdqg, gdkd, gdvi, gdgl = _gla_scan_bwd(do_gla, gqg, gkd, ggl, projp, ghist, bsz=bsz, nc_seq=nc_seq,
                                            name="gla_scan_bwd")
    dgqk, dgv, dsb, d_w2p, d_gla_b = _gla_intra_bwd(projp, gates, w2p, gla_b, do_gla, gdqg, gdkd, gdvi, gdgl,
                                                    nc_seq=nc_seq, name="gla_intra_bwd")

    secs = (dcin, ddz, dgqk, dgv, dgr, dsa, dsb)
    g_lo = _grad_tn(h, secs[0:2], tk=tr, name="w_in_grad_lo")
    g_hi = _grad_tn(h, secs[2:7], tk=tr, name="w_in_grad_hi")
    dh0, d_norm1_g = _inproj_bwd(secs, wp, h0, norm1_g, dx1, tr=tt, name="in_proj_bwd")
    dh0 = dh0.reshape(bsz, t_seq, d)
    grad_x = dh0[:, CHUNK:]
    d_meta_rows = dh0[:, N_PAD:CHUNK].reshape(bsz * N_META, d)

    grads = dict(w_in_lo=g_lo, w_in_hi=g_hi, w_out=d_w_out, w_up=d_w_up, w_down=d_w_down, meta_rows=d_meta_rows,
                 norm1_g=d_norm1_g, conv_w=d_conv_w, a_log_tile=d_alog, dt_bias_tile=d_dtb, dn_norm_g=d_dn_norm_g,
                 gla_w2=d_w2p[0:GLA_RANK], gla_b=d_gla_b, gla_norm_g=d_gla_norm_g, norm2_g=d_norm2_g,
                 final_norm_g=d_final_g, loss_tile=loss_tile)
    if where is not None:
        grads["mlp_exchanged"] = (mlp_pair, mlp_parts)
    return grad_x, grads


SHARD_W = IN_WIDTH // N_CHIPS
PADDED_ORDER = ((0, 2048), (2056, 3592), (2048, 2056), LANE - 8, (3592, 3608), LANE - GLA_RANK)


def _pad_layout(w_full):
    pieces = [jnp.zeros((w_full.shape[0], seg), w_full.dtype) if isinstance(seg, int) else w_full[:, seg[0]:seg[1]]
              for seg in PADDED_ORDER]
    return jnp.concatenate(pieces, axis=1)


def _padded_from_shards(stack):
    pieces = []
    for seg in PADDED_ORDER:
        if isinstance(seg, int):
            pieces.append(jnp.zeros((stack.shape[1], seg), stack.dtype))
            continue
        for j in range(N_CHIPS):
            lo, hi = max(seg[0], j * SHARD_W), min(seg[1], (j + 1) * SHARD_W)
            if lo < hi:
                pieces.append(stack[j, :, lo - j * SHARD_W:hi - j * SHARD_W])
    return jnp.concatenate(pieces, axis=1)


def _shards_from_padded(g_lo, g_hi):
    split = g_lo.shape[1]
    starts, pos = [], 0
    for seg in PADDED_ORDER:
        width = seg if isinstance(seg, int) else seg[1] - seg[0]
        if not isinstance(seg, int):
            starts.append((seg[0], seg[1], pos))
        pos += width
    shards = []
    for j in range(N_CHIPS):
        pieces = []
        for a, b, p0 in sorted(starts):
            lo, hi = max(a, j * SHARD_W), min(b, (j + 1) * SHARD_W)
            if lo < hi:
                src, off = (g_lo, 0) if p0 < split else (g_hi, split)
                pieces.append(src[:, p0 + lo - a - off:p0 + hi - a - off])
        pieces.append(jnp.zeros((g_lo.shape[0], D_MODEL - SHARD_W), g_lo.dtype))
        shards.append(jnp.concatenate(pieces, axis=1))
    return jnp.stack(shards)


def _pack_small(g, bsz):
    assert bsz * N_META == 32
    row = jnp.concatenate([g["a_log_tile"], g["dt_bias_tile"], g["dn_norm_g"], g["gla_norm_g"], g["gla_b"],
                           g["loss_tile"], jnp.zeros((1, LANE), F32)], axis=1)
    return jnp.concatenate([g["meta_rows"], g["norm1_g"], g["conv_w"].reshape(6, D_MODEL), row,
                            g["gla_w2"].reshape(4, D_MODEL), g["norm2_g"], g["final_norm_g"],
                            jnp.zeros((2, D_MODEL), F32)], axis=0)


def kernel(x, meta_tokens, norm1_g, w_in, conv_w, a_log, dt_bias, dn_norm_g, gla_w2, gla_b, gla_norm_g, w_out, norm2_g, w_up, w_down, final_norm_g, loss_target, m_meta_tokens, m_norm1_g, m_w_in, m_conv_w, m_a_log, m_dt_bias, m_dn_norm_g, m_gla_w2, m_gla_b, m_gla_norm_g, m_w_out, m_norm2_g, m_w_up, m_w_down, m_final_norm_g, v_meta_tokens, v_norm1_g, v_w_in, v_conv_w, v_a_log, v_dt_bias, v_dn_norm_g, v_gla_w2, v_gla_b, v_gla_norm_g, v_w_out, v_norm2_g, v_w_up, v_w_down, v_final_norm_g):
    bsz = x.shape[0]
    chip = 2 * lax.axis_index("x") + lax.axis_index("y")

    lane_pad = lambda a, wd: jnp.pad(a, ((0, 0), (0, wd - a.shape[1])))
    where = jnp.stack([lax.axis_index("c"), chip]).astype(jnp.int32)
    slot = lambda a, dt, nm: _to_slot(where, a, dt, name="slot_" + nm)
    early = _GatherRider([slot(lane_pad(w_in[0], D_MODEL), BF16, "w_in"), slot(meta_tokens, F32, "meta"),
                          slot(conv_w[0], F32, "conv"), slot(lane_pad(gla_w2[0], LANE), F32, "gla_w2")],
                         [True, False, False, False])
    g_in, g_meta, g_conv, g_w2 = _exchange_now(early, name="gather_early")
    late = (_GatherRider([slot(w_up[0], BF16, "w_up")], [True]), _GatherRider([slot(w_down[0], BF16, "w_down")], [True]),
            _GatherRider([slot(w_out[0], BF16, "w_out")], [True]))
    wp = _padded_from_shards(g_in)
    meta_f = g_meta.transpose(1, 0, 2).reshape(N_META, D_MODEL)
    conv_f = g_conv.transpose(1, 0, 2).reshape(4, QKV_W)
    w2_f = g_w2[:, :, 0:GQ_W // N_CHIPS].transpose(1, 0, 2).reshape(GLA_RANK, GQ_W)

    grad_x, g = _local_step(x, loss_target, meta_f, norm1_g, wp, conv_f, a_log, dt_bias, dn_norm_g, w2_f, gla_b,
                            gla_norm_g, None, norm2_g, None, None, final_norm_g.reshape(1, D_MODEL), late_gather=late, where=where)

    shard_major = [_shards_from_padded(g["w_in_lo"], g["w_in_hi"]), g["w_out"].reshape(N_CHIPS, D_MODEL // N_CHIPS, D_MODEL)]
    theirs = _exchange_now(_SiblingHalvesRider(shard_major), name="sibling_halves")
    pair = [_pair_sum(where, a, b, name=f"pair_sum_{k}") for k, (a, b) in enumerate(zip(shard_major, theirs))]
    parts = _exchange_now(_ChipExchangeRider(pair), name="chip_exchange")
    mlp_pair, mlp_parts = g["mlp_exchanged"]
    halves = [_chip_sum(where, p, q, name=f"chip_sum_{k}")
              for k, (p, q) in enumerate(zip(pair + mlp_pair, list(parts) + list(mlp_parts)))]
    gw_in, gw_out, gw_up, gw_down = _sibling_join(halves)

    red = _small_allreduce(_pack_small(g, bsz))
    g_meta_full = red[0:N_META]
    g_norm1 = red[32:33]
    g_conv_full = red[33:39].reshape(4, QKV_W)
    srow = red[39:40]
    g_alog, g_dtb = srow[:, 4:8], srow[:, LANE + 4:LANE + 8]
    g_dn_norm, g_gla_norm = srow[:, 2 * LANE:3 * LANE], srow[:, 3 * LANE:4 * LANE]
    g_gla_b = srow[:, 4 * LANE:6 * LANE]
    loss = srow[0, 6 * LANE]
    g_w2_full = red[40:44].reshape(GLA_RANK, GQ_W)
    g_norm2 = red[44:45]
    g_final = red[45:46]
    g_meta_sh = lax.dynamic_slice_in_dim(g_meta_full, chip * (D_MODEL // N_CHIPS), D_MODEL // N_CHIPS, axis=1)
    g_conv_sh = lax.dynamic_slice_in_dim(g_conv_full, chip * (QKV_W // N_CHIPS), QKV_W // N_CHIPS, axis=1)
    g_w2_sh = lax.dynamic_slice_in_dim(g_w2_full, chip * (GQ_W // N_CHIPS), GQ_W // N_CHIPS, axis=1)

    names = ["meta_tokens", "norm1_g", "w_in", "conv_w", "a_log", "dt_bias", "dn_norm_g", "gla_w2", "gla_b",
             "gla_norm_g", "w_out", "norm2_g", "w_up", "w_down", "final_norm_g"]
    weights = dict(meta_tokens=meta_tokens, norm1_g=norm1_g, w_in=w_in, conv_w=conv_w, a_log=a_log, dt_bias=dt_bias,
                   dn_norm_g=dn_norm_g, gla_w2=gla_w2, gla_b=gla_b, gla_norm_g=gla_norm_g, w_out=w_out,
                   norm2_g=norm2_g, w_up=w_up, w_down=w_down, final_norm_g=final_norm_g)
    ms = dict(meta_tokens=m_meta_tokens, norm1_g=m_norm1_g, w_in=m_w_in, conv_w=m_conv_w, a_log=m_a_log,
              dt_bias=m_dt_bias, dn_norm_g=m_dn_norm_g, gla_w2=m_gla_w2, gla_b=m_gla_b, gla_norm_g=m_gla_norm_g,
              w_out=m_w_out, norm2_g=m_norm2_g, w_up=m_w_up, w_down=m_w_down, final_norm_g=m_final_norm_g)
    vs = dict(meta_tokens=v_meta_tokens, norm1_g=v_norm1_g, w_in=v_w_in, conv_w=v_conv_w, a_log=v_a_log,
              dt_bias=v_dt_bias, dn_norm_g=v_dn_norm_g, gla_w2=v_gla_w2, gla_b=v_gla_b, gla_norm_g=v_gla_norm_g,
              w_out=v_w_out, norm2_g=v_norm2_g, w_up=v_w_up, w_down=v_w_down, final_norm_g=v_final_norm_g)
    grads2d = dict(meta_tokens=g_meta_sh, norm1_g=g_norm1, w_in=gw_in, conv_w=g_conv_sh, a_log=g_alog, dt_bias=g_dtb,
                   dn_norm_g=g_dn_norm, gla_w2=g_w2_sh, gla_b=g_gla_b, gla_norm_g=g_gla_norm, w_out=gw_out,
                   norm2_g=g_norm2, w_up=gw_up, w_down=gw_down, final_norm_g=g_final)
    out_g, out_d, out_m, out_v = [], [], [], []
    for nm in names:
        shape = weights[nm].shape
        g2 = grads2d[nm]
        if len(shape) == 3:
            res = _adamw(weights[nm], g2, ms[nm], vs[nm], name=f"adamw_{nm}", emit_grad=nm == "w_in")
            gout = res[3] if nm == "w_in" else g2.reshape(shape)
        else:
            as2d = lambda a: a.reshape(g2.shape)
            res = _adamw(as2d(weights[nm]), g2, as2d(ms[nm]), as2d(vs[nm]), name=f"adamw_{nm}")
            gout = g2.reshape(shape)
        out_g.append(gout)
        out_d.append(res[0].reshape(shape))
        out_m.append(res[1].reshape(shape))
        out_v.append(res[2].reshape(shape))
    return (loss, grad_x, *out_g, *out_d, *out_m, *out_v)
```

```python
import functools

import jax
import jax.numpy as jnp
import numpy as np
from jax import lax
from jax.experimental import pallas as pl
from jax.experimental.pallas import tpu as pltpu

F32 = jnp.float32
BF16 = jnp.bfloat16
HI = lax.Precision.HIGHEST
MESH = pl.DeviceIdType.MESH

D_MODEL = 1024
N_META = 16
CHUNK = 64
N_PAD = CHUNK - N_META
NH = 4
DN_D = 128
GLA_DK = 64
GLA_DV = 128
GLA_RANK = 16
D_FF = 4 * D_MODEL
EPS = 1e-6
IN_WIDTH = 3608
C_QKV, C_DZ, C_GQK, C_GV, C_GR, C_SA, C_SB, PW = 0, 1536, 2048, 2560, 3072, 3584, 3712, 3840
LANE = 128
N_CHIPS = 4

ADAM_LR, ADAM_B1, ADAM_B2, ADAM_EPS, ADAM_WD, ADAM_STEP = 0.001, 0.9, 0.999, 1e-08, 0.01, 10

VMEM_BIG = 56 * 1024 * 1024


def _cp(vmem=None, sem=None):
    kw = {}
    if vmem is not None:
        kw["vmem_limit_bytes"] = vmem
    if sem is not None:
        kw["dimension_semantics"] = sem
    return pltpu.CompilerParams(**kw)


def _tile(n, target, mult=16):
    best = None
    for t in range(mult, min(n, target) + 1, mult):
        if n % t == 0:
            best = t
    assert best is not None, (n, target)
    return best


def _dot(a, b, dims, prec=None):
    return lax.dot_general(a, b, (dims, ((), ())), preferred_element_type=F32, precision=prec)


def _nn(a, b):
    return _dot(a.astype(BF16), b.astype(BF16), ((1,), (0,)))


def _nt(a, b):
    return _dot(a.astype(BF16), b.astype(BF16), ((1,), (1,)))


def _tn(a, b):
    return _dot(a.astype(BF16), b.astype(BF16), ((0,), (0,)))


def _tri_sum(tri, x):
    t = tri.astype(BF16)
    hi = x.astype(BF16)
    r1 = x - hi.astype(F32)
    mid = r1.astype(BF16)
    lo = (r1 - mid.astype(F32)).astype(BF16)
    nn = ((1,), (0,))
    return _dot(t, hi, nn) + _dot(t, mid, nn) + _dot(t, lo, nn)


def _sigmoid(x):
    return 0.5 * jnp.tanh(0.5 * x) + 0.5


def _softplus(x):
    return jnp.maximum(x, 0.0) + jnp.log(1.0 + jnp.exp(-jnp.abs(x)))


def _logsigmoid(x):
    return -_softplus(-x)


def _iota2(shape, dim):
    return lax.broadcasted_iota(jnp.int32, shape, dim)


def _mm(a, b, mode, *, tm, tn, tk, out_dtypes, extras=(), epilogue=None, name, vmem=VMEM_BIG, rider=None,
        out_widths=None, n_chunk=None):
    if mode == "tn":
        K, M = a.shape
    else:
        M, K = a.shape
    N = b.shape[0] if mode == "nt" else b.shape[1]
    assert M % tm == 0 and N % tn == 0 and K % tk == 0, (name, M, N, K, tm, tn, tk)
    nk = K // tk
    n_ex, n_out = len(extras), len(out_dtypes)
    if mode == "tn":
        a_spec = pl.BlockSpec((tk, tm), lambda i, j, k: (k, i))
    else:
        a_spec = pl.BlockSpec((tm, tk), lambda i, j, k: (i, k))
    if mode == "nt":
        b_spec = pl.BlockSpec((tn, tk), lambda i, j, k: (j, k))
    else:
        b_spec = pl.BlockSpec((tk, tn), lambda i, j, k: (k, j))
    mn_spec = pl.BlockSpec((tm, tn), lambda i, j, k: (i, j))
    if out_widths is None:
        o_specs = [mn_spec] * n_out
        o_shapes = [jax.ShapeDtypeStruct((M, N), dt) for dt in out_dtypes]
    else:
        assert tn == N
        o_specs = [pl.BlockSpec((tm, wd), lambda i, j, k: (i, 0)) for wd in out_widths]
        o_shapes = [jax.ShapeDtypeStruct((M, wd), dt) for wd, dt in zip(out_widths, out_dtypes)]
    dims = {"nn": ((1,), (0,)), "nt": ((1,), (1,)), "tn": ((0,), (0,))}[mode]

    single = nk == 1
    direct = (not single) and epilogue is None and n_out == 1 and out_dtypes[0] == F32

    def body(*refs):
        a_ref, b_ref = refs[0], refs[1]
        ex_refs = refs[2:2 + n_ex]
        out_refs = refs[2 + n_ex:2 + n_ex + n_out]
        if n_chunk is not None:
            assert single and out_widths is None and mode != "tn" and tn % n_chunk == 0
            av = a_ref[...].astype(BF16)
            for j in range(tn // n_chunk):
                cols = slice(j * n_chunk, (j + 1) * n_chunk)
                bv = b_ref[cols, :] if mode == "nt" else b_ref[:, cols]
                acc = _dot(av, bv.astype(BF16), dims)
                res = (acc,) if epilogue is None else epilogue(acc, *[e[:, cols] for e in ex_refs])
                for o_ref, r in zip(out_refs, res):
                    o_ref[:, cols] = r.astype(o_ref.dtype)
            return
        part = _dot(a_ref[...].astype(BF16), b_ref[...].astype(BF16), dims)

        def finish(acc):
            res = (acc,) if epilogue is None else epilogue(acc, *[e[...] for e in ex_refs])
            for o_ref, r in zip(out_refs, res):
                o_ref[...] = r.astype(o_ref.dtype)

        if single:
            finish(part)
            return
        acc_ref = out_refs[0] if direct else refs[2 + n_ex + n_out]
        k = pl.program_id(2)

        @pl.when(k == 0)
        def _():
            acc_ref[...] = part

        @pl.when(k > 0)
        def _():
            acc_ref[...] += part

        if not direct:
            @pl.when(k == nk - 1)
            def _():
                finish(acc_ref[...])

    outs, ridden = _hosted_call(
        body, rider, name=name, grid=(M // tm, N // tn, nk),
        in_specs=[a_spec, b_spec] + [mn_spec] * n_ex,
        out_specs=o_specs, out_shape=o_shapes,
        scratch_shapes=[] if (single or direct) else [pltpu.VMEM((tm, tn), F32)],
        compiler_params=_cp(vmem, ("parallel", "parallel", "arbitrary")), args=(a, b, *extras))
    return tuple(outs) if rider is None else (tuple(outs), ridden)


def _grad_tn(a, secs, *, tk, name):
    kk, m = a.shape
    widths = [s.shape[1] for s in secs]
    total = sum(widths)
    nk = kk // tk

    def body(*refs):
        a_ref, sec_refs, o_ref = refs[0], refs[1:-1], refs[-1]
        cat = sec_refs[0][...] if len(sec_refs) == 1 else jnp.concatenate([s[...] for s in sec_refs], axis=1)
        part = _dot(a_ref[...].astype(BF16), cat.astype(BF16), ((0,), (0,)))
        k = pl.program_id(0)

        @pl.when(k == 0)
        def _():
            o_ref[...] = part

        @pl.when(k > 0)
        def _():
            o_ref[...] += part

    return pl.pallas_call(
        body, name=name, grid=(nk,),
        in_specs=[pl.BlockSpec((tk, m), lambda k: (k, 0))] + [pl.BlockSpec((tk, w), lambda k: (k, 0)) for w in widths],
        out_specs=pl.BlockSpec((m, total), lambda k: (0, 0)),
        out_shape=jax.ShapeDtypeStruct((m, total), F32),
        compiler_params=_cp(VMEM_BIG, ("arbitrary",)),
    )(a, *secs)


def _rms_fwd(x, g, *, tr, name):
    n, d = x.shape

    def body(x_ref, g_ref, o_ref):
        xv = x_ref[...]
        r = lax.rsqrt(jnp.mean(xv * xv, axis=-1, keepdims=True) + EPS)
        o_ref[...] = (xv * r * g_ref[...]).astype(o_ref.dtype)

    return pl.pallas_call(
        body, name=name, grid=(n // tr,),
        in_specs=[pl.BlockSpec((tr, d), lambda i: (i, 0)), pl.BlockSpec((1, d), lambda i: (0, 0))],
        out_specs=pl.BlockSpec((tr, d), lambda i: (i, 0)),
        out_shape=jax.ShapeDtypeStruct((n, d), BF16),
        compiler_params=_cp(VMEM_BIG),
    )(x, g)


def _rms_bwd_math(xv, g, dy):
    r = lax.rsqrt(jnp.mean(xv * xv, axis=-1, keepdims=True) + EPS)
    xh = xv * r
    gdy = dy * g
    dx = r * (gdy - xh * jnp.mean(xh * gdy, axis=-1, keepdims=True))
    return dx, jnp.sum(dy * xh, axis=0, keepdims=True)


def _mlp_up_bwd_norm(dup, w_up, x, g, res, *, tr, name, rider=None):
    n, d = x.shape
    ff = dup.shape[1]

    def body(dup_ref, w_ref, x_ref, g_ref, res_ref, o_ref, ob_ref, dg_ref):
        dh = _nt(dup_ref[...], w_ref[...])
        dx, dg = _rms_bwd_math(x_ref[...], g_ref[...], dh)
        tot = res_ref[...] + dx
        o_ref[...] = tot
        ob_ref[...] = tot.astype(BF16)

        @pl.when(pl.program_id(0) == 0)
        def _():
            dg_ref[...] = dg

        @pl.when(pl.program_id(0) > 0)
        def _():
            dg_ref[...] += dg

    row = pl.BlockSpec((tr, d), lambda i: (i, 0))
    vec = pl.BlockSpec((1, d), lambda i: (0, 0))
    outs, ridden = _hosted_call(
        body, rider, name=name, grid=(n // tr,),
        in_specs=[pl.BlockSpec((tr, ff), lambda i: (i, 0)), pl.BlockSpec((d, ff), lambda i: (0, 0)), row, vec, row],
        out_specs=[row, row, vec],
        out_shape=[jax.ShapeDtypeStruct((n, d), F32), jax.ShapeDtypeStruct((n, d), BF16),
                   jax.ShapeDtypeStruct((1, d), F32)],
        scratch_shapes=[], compiler_params=_cp(VMEM_BIG, ("arbitrary",)), args=(dup, w_up, x, g, res))
    return (*outs, ridden)


def _mlp_down_loss(act, w_down, x1, gf, tgt, *, t_seq, tr, name):
    n, d = x1.shape
    ff = act.shape[1]
    per_seq = t_seq // tr

    def body(a_ref, w_ref, x_ref, g_ref, t_ref, dx_ref, dxb_ref, dg_ref, loss_ref):
        i = pl.program_id(0)
        xv = x_ref[...] + _nn(a_ref[...], w_ref[...])
        g = g_ref[...]
        r = lax.rsqrt(jnp.mean(xv * xv, axis=-1, keepdims=True) + EPS)
        xh = xv * r
        pos = (i % per_seq) * tr + _iota2((tr, 1), 0)
        real = pos >= CHUNK
        err = jnp.where(real, xh * g - t_ref[...], 0.0)
        dy = err * (1.0 / d)
        gdy = dy * g
        dx = r * (gdy - xh * jnp.mean(xh * gdy, axis=-1, keepdims=True))
        dx_ref[...] = dx
        dxb_ref[...] = dx.astype(BF16)
        dg = jnp.sum(dy * xh, axis=0, keepdims=True)
        ls = 0.5 * jnp.sum(jnp.mean(err * err, axis=-1, keepdims=True), axis=0, keepdims=True)
        ls = jnp.where(_iota2((1, LANE), 1) == 0, ls, 0.0)

        @pl.when(i == 0)
        def _():
            dg_ref[...] = dg
            loss_ref[...] = ls

        @pl.when(i > 0)
        def _():
            dg_ref[...] += dg
            loss_ref[...] += ls

    row = pl.BlockSpec((tr, d), lambda i: (i, 0))
    vec = pl.BlockSpec((1, d), lambda i: (0, 0))
    one = pl.BlockSpec((1, LANE), lambda i: (0, 0))
    return pl.pallas_call(
        body, name=name, grid=(n // tr,),
        in_specs=[pl.BlockSpec((tr, ff), lambda i: (i, 0)), pl.BlockSpec((ff, d), lambda i: (0, 0)), row, vec, row],
        out_specs=[row, row, vec, one],
        out_shape=[jax.ShapeDtypeStruct((n, d), F32), jax.ShapeDtypeStruct((n, d), BF16),
                   jax.ShapeDtypeStruct((1, d), F32), jax.ShapeDtypeStruct((1, LANE), F32)],
        compiler_params=_cp(VMEM_BIG, ("arbitrary",)),
    )(act, w_down, x1, gf, tgt)


def _gnorm_fwd(o_dn, o_gla, projp, g_dn, g_gla, *, tr, name):
    n = o_dn.shape[0]
    w = NH * DN_D

    def body(odn_ref, ogl_ref, z_ref, r_ref, gdn_ref, ggl_ref, mix_ref):
        for grp, (o_ref, gate_ref, gain_ref) in enumerate(((odn_ref, z_ref, gdn_ref), (ogl_ref, r_ref, ggl_ref))):
            gain = gain_ref[...]
            for h in range(NH):
                sl = slice(h * DN_D, (h + 1) * DN_D)
                o = o_ref[:, sl].astype(F32)
                z = gate_ref[:, sl].astype(F32)
                r = lax.rsqrt(jnp.mean(o * o, axis=-1, keepdims=True) + EPS)
                y = (o * r * gain) * (z * _sigmoid(z))
                mix_ref[:, grp * w + h * DN_D: grp * w + (h + 1) * DN_D] = y.astype(mix_ref.dtype)

    row = pl.BlockSpec((tr, w), lambda i: (i, 0))
    vec = pl.BlockSpec((1, DN_D), lambda i: (0, 0))
    return pl.pallas_call(
        body, name=name, grid=(n // tr,),
        in_specs=[row, row, pl.BlockSpec((tr, w), lambda i: (i, C_DZ // w)),
                  pl.BlockSpec((tr, w), lambda i: (i, C_GR // w)), vec, vec],
        out_specs=pl.BlockSpec((tr, 2 * w), lambda i: (i, 0)),
        out_shape=jax.ShapeDtypeStruct((n, 2 * w), BF16),
        compiler_params=_cp(VMEM_BIG),
    )(o_dn, o_gla, projp, projp, g_dn, g_gla)


def _gnorm_bwd(dmix, o_dn, o_gla, projp, g_dn, g_gla, *, tr, name):
    n = o_dn.shape[0]
    w = NH * DN_D

    def body(dm_ref, odn_ref, ogl_ref, z_ref, r_ref, gdn_ref, ggl_ref,
             dodn_ref, ddz_ref, dogl_ref, dgr_ref, dgdn_ref, dggl_ref):
        first = pl.program_id(0) == 0
        groups = ((odn_ref, z_ref, gdn_ref, dodn_ref, ddz_ref, dgdn_ref),
                  (ogl_ref, r_ref, ggl_ref, dogl_ref, dgr_ref, dggl_ref))
        for grp, (o_ref, gate_ref, gain_ref, do_ref, dgate_ref, dgain_ref) in enumerate(groups):
            gain = gain_ref[...]
            dgain = jnp.zeros((1, DN_D), F32)
            for h in range(NH):
                sl = slice(h * DN_D, (h + 1) * DN_D)
                o = o_ref[:, sl].astype(F32)
                z = gate_ref[:, sl].astype(F32)
                dm = dm_ref[:, grp * w + h * DN_D: grp * w + (h + 1) * DN_D].astype(F32)
                r = lax.rsqrt(jnp.mean(o * o, axis=-1, keepdims=True) + EPS)
                oh = o * r
                s = _sigmoid(z)
                dn = dm * (z * s)
                dgate_ref[:, sl] = (dm * (oh * gain) * (s * (1.0 + z * (1.0 - s)))).astype(dgate_ref.dtype)
                gdn = dn * gain
                do_ref[:, sl] = (r * (gdn - oh * jnp.mean(oh * gdn, axis=-1, keepdims=True))).astype(do_ref.dtype)
                dgain = dgain + jnp.sum(dn * oh, axis=0, keepdims=True)

            @pl.when(first)
            def _():
                dgain_ref[...] = dgain

            @pl.when(jnp.logical_not(first))
            def _():
                dgain_ref[...] += dgain

    row = pl.BlockSpec((tr, w), lambda i: (i, 0))
    vec = pl.BlockSpec((1, DN_D), lambda i: (0, 0))
    big = jax.ShapeDtypeStruct((n, w), F32)
    gate = jax.ShapeDtypeStruct((n, w), BF16)
    small = jax.ShapeDtypeStruct((1, DN_D), F32)
    return pl.pallas_call(
        body, name=name, grid=(n // tr,),
        in_specs=[pl.BlockSpec((tr, 2 * w), lambda i: (i, 0)), row, row,
                  pl.BlockSpec((tr, w), lambda i: (i, C_DZ // w)), pl.BlockSpec((tr, w), lambda i: (i, C_GR // w)), vec, vec],
        out_specs=[row, row, row, row, vec, vec],
        out_shape=[gate, gate, gate, gate, small, small],
        compiler_params=_cp(VMEM_BIG),
    )(dmix, o_dn, o_gla, projp, projp, g_dn, g_gla)


QKV_W = 3 * NH * DN_D
HALO = 8


def _conv_z(xs_ref, cw_ref, tt):
    z = cw_ref[0:1, :] * xs_ref[pl.ds(HALO - 3, tt), :]
    for j in range(1, 4):
        z = z + cw_ref[j:j + 1, :] * xs_ref[pl.ds(HALO - 3 + j, tt), :]
    return z


def _dnprep_fwd(projp, conv_w, *, bsz, t_seq, tt, name):
    n = bsz * t_seq
    per_seq = t_seq // tt
    hw = NH * DN_D

    def body(x_ref, halo_ref, cw_ref, q_ref, k_ref, v_ref, xs_ref):
        i = pl.program_id(1)
        xs_ref[0:HALO, :] = jnp.where(i == 0, 0.0, halo_ref[...].astype(F32))
        xs_ref[HALO:HALO + tt, :] = x_ref[...].astype(F32)
        z = _conv_z(xs_ref, cw_ref, tt)
        a = z * _sigmoid(z)
        for grp, o_ref in enumerate((q_ref, k_ref)):
            for h in range(NH):
                ah = a[:, grp * hw + h * DN_D: grp * hw + (h + 1) * DN_D]
                rs = lax.rsqrt(jnp.sum(ah * ah, axis=-1, keepdims=True) + EPS)
                o_ref[:, h * DN_D:(h + 1) * DN_D] = (ah * rs).astype(o_ref.dtype)
        v_ref[...] = a[:, 2 * hw:3 * hw].astype(v_ref.dtype)

    def halo_map(b, i):
        return (jnp.maximum((b * t_seq + i * tt) // HALO - 1, 0), 0)

    out = pl.BlockSpec((tt, hw), lambda b, i: (b * per_seq + i, 0))
    sds = jax.ShapeDtypeStruct((n, hw), BF16)
    return pl.pallas_call(
        body, name=name, grid=(bsz, per_seq),
        in_specs=[pl.BlockSpec((tt, QKV_W), lambda b, i: (b * per_seq + i, 0)),
                  pl.BlockSpec((HALO, QKV_W), halo_map),
                  pl.BlockSpec((4, QKV_W), lambda b, i: (0, 0))],
        out_specs=[out, out, out], out_shape=[sds, sds, sds],
        scratch_shapes=[pltpu.VMEM((tt + HALO, QKV_W), F32)],
        compiler_params=_cp(VMEM_BIG),
    )(projp, projp, conv_w)


def _dnprep_bwd_a(projp, conv_w, dq, dk, dv, *, bsz, t_seq, tt, name):
    n = bsz * t_seq
    per_seq = t_seq // tt
    hw = NH * DN_D

    def body(x_ref, halo_ref, cw_ref, dq_ref, dk_ref, dv_ref, dz_ref, dcw_ref, xs_ref):
        b, i = pl.program_id(0), pl.program_id(1)
        xs_ref[0:HALO, :] = jnp.where(i == 0, 0.0, halo_ref[...].astype(F32))
        xs_ref[HALO:HALO + tt, :] = x_ref[...].astype(F32)
        z = _conv_z(xs_ref, cw_ref, tt)
        s = _sigmoid(z)
        a = z * s
        dsilu = s * (1.0 + z * (1.0 - s))
        for grp, d_ref in enumerate((dq_ref, dk_ref)):
            for h in range(NH):
                sl = slice(grp * hw + h * DN_D, grp * hw + (h + 1) * DN_D)
                ah = a[:, sl]
                rs = lax.rsqrt(jnp.sum(ah * ah, axis=-1, keepdims=True) + EPS)
                y = ah * rs
                dy = d_ref[:, h * DN_D:(h + 1) * DN_D]
                da = rs * (dy - y * jnp.sum(dy * y, axis=-1, keepdims=True))
                dz_ref[:, sl] = da * dsilu[:, sl]
        dz_ref[:, 2 * hw:3 * hw] = dv_ref[...] * dsilu[:, 2 * hw:3 * hw]
        dz = dz_ref[...]
        first = jnp.logical_and(b == 0, i == 0)
        for j in range(4):
            part = jnp.sum(dz * xs_ref[pl.ds(HALO - 3 + j, tt), :], axis=0, keepdims=True)

            @pl.when(first)
            def _():
                dcw_ref[j:j + 1, :] = part

            @pl.when(jnp.logical_not(first))
            def _():
                dcw_ref[j:j + 1, :] += part

    def halo_map(b, i):
        return (jnp.maximum((b * t_seq + i * tt) // HALO - 1, 0), 0)

    hrow = pl.BlockSpec((tt, hw), lambda b, i: (b * per_seq + i, 0))
    return pl.pallas_call(
        body, name=name, grid=(bsz, per_seq),
        in_specs=[pl.BlockSpec((tt, QKV_W), lambda b, i: (b * per_seq + i, 0)),
                  pl.BlockSpec((HALO, QKV_W), halo_map),
                  pl.BlockSpec((4, QKV_W), lambda b, i: (0, 0)), hrow, hrow, hrow],
        out_specs=[pl.BlockSpec((tt, QKV_W), lambda b, i: (b * per_seq + i, 0)),
                   pl.BlockSpec((4, QKV_W), lambda b, i: (0, 0))],
        out_shape=[jax.ShapeDtypeStruct((n, QKV_W), F32), jax.ShapeDtypeStruct((4, QKV_W), F32)],
        scratch_shapes=[pltpu.VMEM((tt + HALO, QKV_W), F32)],
        compiler_params=_cp(VMEM_BIG),
    )(projp, projp, conv_w, dq, dk, dv)


def _dnprep_bwd_b(dz, conv_w, *, bsz, t_seq, tt, name):
    n = bsz * t_seq
    per_seq = t_seq // tt
    last_blk = n // HALO - 1

    def body(dz_ref, halo_ref, cw_ref, dx_ref, ds_ref):
        i = pl.program_id(1)
        ds_ref[0:tt, :] = dz_ref[...].astype(F32)
        ds_ref[tt:tt + HALO, :] = jnp.where(i == per_seq - 1, 0.0, halo_ref[...].astype(F32))
        dx = cw_ref[0:1, :] * ds_ref[pl.ds(3, tt), :]
        for j in range(1, 4):
            dx = dx + cw_ref[j:j + 1, :] * ds_ref[pl.ds(3 - j, tt), :]
        dx_ref[...] = dx.astype(dx_ref.dtype)

    def halo_map(b, i):
        return (jnp.minimum((b * t_seq + (i + 1) * tt) // HALO, last_blk), 0)

    row = pl.BlockSpec((tt, QKV_W), lambda b, i: (b * per_seq + i, 0))
    return pl.pallas_call(
        body, name=name, grid=(bsz, per_seq),
        in_specs=[row, pl.BlockSpec((HALO, QKV_W), halo_map), pl.BlockSpec((4, QKV_W), lambda b, i: (0, 0))],
        out_specs=row, out_shape=jax.ShapeDtypeStruct((n, QKV_W), BF16),
        scratch_shapes=[pltpu.VMEM((tt + HALO, QKV_W), F32)],
        compiler_params=_cp(VMEM_BIG),
    )(dz, dz, conv_w)


def _masks64():
    r = _iota2((CHUNK, CHUNK), 0)
    c = _iota2((CHUNK, CHUNK), 1)
    return r, c


def _group(nc_seq, target=5):
    return max(g for g in range(1, target + 1) if nc_seq % g == 0)


def _round_robin(chains):
    live = list(chains)
    while live:
        nxt = []
        for ch in live:
            try:
                next(ch)
                nxt.append(ch)
            except StopIteration:
                pass
        live = nxt
        yield


def _run(chains):
    for _ in _round_robin(chains):
        pass


def _per_chunk(inner, kinds, grp):
    def body(*refs):
        chains = []
        for gi in range(grp):
            views = []
            for r, kind in zip(refs, kinds):
                if kind == "row":
                    views.append(r.at[pl.ds(gi * CHUNK, CHUNK)])
                elif kind == "lead":
                    views.append(r.at[pl.ds(gi, 1)])
                else:
                    views.append(r)
            chains.append(inner(gi, *views))
        _run(chains)
    return body


def _accumulate(ref, val, gi):
    if gi > 0:
        ref[...] += val
        return
    first = pl.program_id(0) == 0

    @pl.when(first)
    def _():
        ref[...] = val

    @pl.when(jnp.logical_not(first))
    def _():
        ref[...] += val


ANY = pl.BlockSpec(memory_space=pl.ANY)


def _place():
    return lax.axis_index("x"), lax.axis_index("y"), lax.axis_index("c")


def _other_chips(x, y):
    return [(1 - x, y, 2 * (1 - x) + y), (x, 1 - y, 2 * x + 1 - y), (1 - x, 1 - y, 2 * (1 - x) + 1 - y)]


class _GatherRider:
    def __init__(self, bufs, split):
        self.inputs = list(bufs)
        self.split = list(split)
        self.out_shapes = [jax.ShapeDtypeStruct(b.shape, b.dtype) for b in bufs]
        self.aliases = {i: i for i in range(len(bufs))}
        self.sems = [pltpu.SemaphoreType.DMA((len(bufs), 3))] * 4

    def _rows(self, k, buf, c, mine=True):
        r = buf.shape[1]
        if not self.split[k]:
            return pl.ds(0, r)
        return pl.ds((c if mine else 1 - c) * (r // 2), r // 2)

    def _ici(self, k, d, bufs, sems, c, px, py, block):
        rows = self._rows(k, bufs[k], c)
        return pltpu.make_async_remote_copy(
            src_ref=bufs[k].at[block, rows, :], dst_ref=bufs[k].at[block, rows, :], send_sem=sems[0].at[k, d],
            recv_sem=sems[1].at[k, d], device_id=(px, py, c), device_id_type=MESH)

    def _pass(self, k, d, bufs, sems, x, y, c, block, mine):
        rows = self._rows(k, bufs[k], c, mine)
        return pltpu.make_async_remote_copy(
            src_ref=bufs[k].at[block, rows, :], dst_ref=bufs[k].at[block, rows, :], send_sem=sems[2].at[k, d],
            recv_sem=sems[3].at[k, d], device_id=(x, y, 1 - c), device_id_type=MESH)

    def first(self, in_refs, bufs, sems):
        x, y, c = _place()
        for k in range(len(bufs)):
            for d, (px, py, _) in enumerate(_other_chips(x, y)):
                self._ici(k, d, bufs, sems, c, px, py, 2 * x + y).start()

    def last(self, in_refs, bufs, sems):
        x, y, c = _place()
        chips = _other_chips(x, y)
        for k in range(len(bufs)):
            for d, (px, py, pj) in enumerate(chips):
                self._ici(k, d, bufs, sems, c, px, py, pj).wait_recv()
                if self.split[k]:
                    self._pass(k, d, bufs, sems, x, y, c, pj, True).start()
        for k in range(len(bufs)):
            for d, (px, py, pj) in enumerate(chips):
                if self.split[k]:
                    self._pass(k, d, bufs, sems, x, y, c, pj, False).wait_recv()
                    self._pass(k, d, bufs, sems, x, y, c, pj, True).wait_send()
                self._ici(k, d, bufs, sems, c, px, py, 2 * x + y).wait_send()


def _hosted_call(body, rider, *, name, grid, in_specs, out_specs, out_shape, scratch_shapes, compiler_params, args):
    if rider is None:
        outs = pl.pallas_call(body, name=name, grid=grid, in_specs=in_specs, out_specs=out_specs, out_shape=out_shape,
                              scratch_shapes=scratch_shapes, compiler_params=compiler_params)(*args)
        return list(outs), []
    n_in, n_out, n_scr = len(in_specs), len(out_specs), len(scratch_shapes)
    r_in, r_out = len(rider.inputs), len(rider.out_shapes)
    compiler_params = _cp(compiler_params.vmem_limit_bytes, ("arbitrary",) * len(grid))

    def full_body(*refs):
        ins = refs[:n_in]
        rins = refs[n_in:n_in + r_in]
        outs = refs[n_in + r_in:n_in + r_in + n_out]
        routs = refs[n_in + r_in + n_out:n_in + r_in + n_out + r_out]
        rest = refs[n_in + r_in + n_out + r_out:]
        scr, sems = rest[:n_scr], rest[n_scr:]
        ids = [pl.program_id(a) for a in range(len(grid))]
        is_first = functools.reduce(jnp.logical_and, [i == 0 for i in ids])
        is_last = functools.reduce(jnp.logical_and, [i == g - 1 for i, g in zip(ids, grid)])

        @pl.when(is_first)
        def _():
            rider.first(rins, routs, sems)

        body(*ins, *outs, *scr)

        @pl.when(is_last)
        def _():
            rider.last(rins, routs, sems)

    res = pl.pallas_call(
        full_body, name=name, grid=grid, in_specs=list(in_specs) + [ANY] * r_in,
        out_specs=list(out_specs) + [ANY] * r_out, out_shape=list(out_shape) + list(rider.out_shapes),
        input_output_aliases={n_in + i: n_out + o for i, o in rider.aliases.items()},
        scratch_shapes=list(scratch_shapes) + list(rider.sems), compiler_params=compiler_params,
    )(*args, *rider.inputs)
    return list(res[:n_out]), list(res[n_out:])


def _exchange_now(rider, *, name):
    r_in = len(rider.inputs)

    def body(*refs):
        rins = refs[:r_in]
        routs = refs[r_in:r_in + len(rider.out_shapes)]
        sems = refs[r_in + len(rider.out_shapes):]
        rider.first(rins, routs, sems)
        rider.last(rins, routs, sems)

    return pl.pallas_call(
        body, name=name, in_specs=[ANY] * r_in, out_specs=[ANY] * len(rider.out_shapes), out_shape=list(rider.out_shapes),
        input_output_aliases=dict(rider.aliases), scratch_shapes=list(rider.sems),
    )(*rider.inputs)


def _to_slot(where, a, dtype, *, name):
    r, cols = a.shape
    tr = _tile(r, 256, 16) if r > 256 else r

    def body(w_ref, a_ref, o_ref):
        o_ref[0] = a_ref[...].astype(o_ref.dtype)

    return pl.pallas_call(
        body, name=name,
        grid_spec=pltpu.PrefetchScalarGridSpec(
            num_scalar_prefetch=1, grid=(r // tr,),
            in_specs=[pl.BlockSpec((tr, cols), lambda i, w: (i, 0))],
            out_specs=pl.BlockSpec((1, tr, cols), lambda i, w: (w[1], i, 0))),
        out_shape=jax.ShapeDtypeStruct((N_CHIPS, r, cols), dtype), compiler_params=_cp(VMEM_BIG),
    )(where, a)


def _tri_inv(a_strict):
    r, c = _masks64()
    eye = (r == c).astype(F32)
    blk16 = (r // 16) == (c // 16)
    blk32 = (r // 32) == (c // 32)
    ld = jnp.where(blk16, a_strict, 0.0)
    x = eye - ld
    p = _nn(ld, ld)
    yield
    for step in range(3):
        xp = _nn(x, p)
        if step < 2:
            p = _nn(p, p)
        x = x + xp
        yield
    for lk in (jnp.where(jnp.logical_and(blk32, jnp.logical_not(blk16)), a_strict, 0.0),
               jnp.where(blk32, 0.0, a_strict)):
        y = x - eye
        s = lk + _nn(y, lk)
        yield
        x = x - s - _nn(s, y)
        yield
    return x


def _dn_gates(sa, alog, dtb, chunk_in_seq):
    rows = _iota2((CHUNK, LANE), 0)
    valid = jnp.logical_or(rows >= N_PAD, chunk_in_seq > 0)
    beta_t = _sigmoid(sa)
    ea = jnp.exp(alog)
    g_t = jnp.where(valid, -ea * _softplus(sa + dtb), 0.0)
    r, c = _masks64()
    ltri = (r >= c).astype(F32)
    gam_t = _tri_sum(ltri, g_t)
    return beta_t, g_t, gam_t, valid, ea


def _dn_intra_fwd(qn, kn, v, projp, alog_row, dtb_row, *, nc_seq, name, rider=None):
    n = qn.shape[0]
    nct = n // CHUNK
    hw = NH * DN_D
    scale = DN_D ** -0.5

    grp = _group(nc_seq)

    def inner(gi, q_ref, k_ref, v_ref, sa_ref, al_ref, dt_ref, u_ref, w_ref, qg_ref, kd_ref, p_ref, t_ref, gl_ref):
        ci = (pl.program_id(0) * grp + gi) % nc_seq
        beta_t, _, gam_t, _, _ = _dn_gates(sa_ref[...], al_ref[...], dt_ref[...], ci)
        yield
        gam_tt = gam_t.T
        r, c = _masks64()
        incl = r >= c
        strict = r > c

        def head(h):
            sl = slice(h * DN_D, (h + 1) * DN_D)
            beta_w = jnp.broadcast_to(beta_t[:, h:h + 1], (CHUNK, DN_D))
            gam_w = jnp.broadcast_to(gam_t[:, 4 + h:5 + h], (CHUNK, DN_D))
            gam_row = gam_tt[4 + h:5 + h, :]
            gl = gam_t[CHUNK - 1:CHUNK, 4 + h:5 + h]
            dec = jnp.exp(jnp.where(incl, gam_w[:, 0:CHUNK] - gam_row, -jnp.inf))
            kh = k_ref[:, sl].astype(F32)
            qh = q_ref[:, sl].astype(F32) * scale
            vh = v_ref[:, sl].astype(F32)
            kk = _nt(kh, kh)
            qk = _nt(qh, kh)
            yield
            a = jnp.where(strict, beta_w[:, 0:CHUNK] * kk * dec, 0.0)
            tm = yield from _tri_inv(a)
            egam_w = jnp.exp(gam_w)
            u_ref[:, sl] = _nn(tm, beta_w * vh).astype(u_ref.dtype)
            w_ref[:, sl] = _nn(tm, (beta_w * egam_w) * kh).astype(w_ref.dtype)
            qg_ref[:, sl] = (egam_w * qh).astype(qg_ref.dtype)
            kd_ref[:, sl] = (jnp.exp(gl - gam_w) * kh).astype(kd_ref.dtype)
            p_ref[0, h] = qk * dec
            t_ref[0, h] = tm
            gl_ref[0, h:h + 1, :] = jnp.broadcast_to(jnp.exp(gl), (1, LANE))

        yield from _round_robin([head(h) for h in range(NH)])

    rows = grp * CHUNK
    row = pl.BlockSpec((rows, hw), lambda i: (i, 0))
    vec = pl.BlockSpec((1, LANE), lambda i: (0, 0))
    mat = pl.BlockSpec((grp, NH, CHUNK, CHUNK), lambda i: (i, 0, 0, 0))
    big = jax.ShapeDtypeStruct((n, hw), BF16)
    msd = jax.ShapeDtypeStruct((nct, NH, CHUNK, CHUNK), F32)
    kinds = ["row"] * 4 + ["whole"] * 2 + ["row"] * 4 + ["lead"] * 3
    outs, ridden = _hosted_call(
        _per_chunk(inner, kinds, grp), rider, name=name, grid=(nct // grp,),
        in_specs=[row, row, row, pl.BlockSpec((rows, LANE), lambda i: (i, 0)), vec, vec],
        out_specs=[row, row, row, row, mat, mat, pl.BlockSpec((grp, NH, LANE), lambda i: (i, 0, 0))],
        out_shape=[big, big, big, big, msd, msd, jax.ShapeDtypeStruct((nct, NH, LANE), F32)],
        scratch_shapes=[], compiler_params=_cp(VMEM_BIG, ("arbitrary",)),
        args=(qn, kn, v, projp, alog_row, dtb_row))
    return (*outs, ridden)


def _dn_scan_fwd(u, w, qg, kd, p, gl, *, bsz, nc_seq, name, rider=None):
    hw = NH * DN_D
    t_seq = nc_seq * CHUNK
    u, w, qg, kd = (z.reshape(bsz, t_seq, hw) for z in (u, w, qg, kd))
    p = p.reshape(bsz, nc_seq, NH, CHUNK, CHUNK)
    gl = gl.reshape(bsz, nc_seq, NH, LANE)
    grp = _group(nc_seq)

    def body(u_ref, w_ref, qg_ref, kd_ref, p_ref, gl_ref, o_ref, vn_ref, hist_ref, s_ref):
        @pl.when(pl.program_id(0) == 0)
        def _():
            s_ref[...] = jnp.zeros_like(s_ref)

        def chain(b, h, gi):
            sl = slice(h * DN_D, (h + 1) * DN_D)
            rows = pl.ds(gi * CHUNK, CHUNK)
            s = s_ref[b, h]
            hist_ref[b, gi, h] = s.astype(hist_ref.dtype)
            ws = _nn(w_ref[b, rows, sl], s)
            qs = _nn(qg_ref[b, rows, sl], s)
            yield
            vn = u_ref[b, rows, sl] - ws
            vn_ref[b, rows, sl] = vn.astype(vn_ref.dtype)
            o_ref[b, rows, sl] = (qs + _nn(p_ref[b, gi, h], vn)).astype(o_ref.dtype)
            s_ref[b, h] = gl_ref[b, gi, h:h + 1, :] * s + _tn(kd_ref[b, rows, sl], vn)

        for gi in range(grp):
            _run([chain(b, h, gi) for b in range(bsz) for h in range(NH)])

    row = pl.BlockSpec((bsz, grp * CHUNK, hw), lambda i: (0, i, 0))
    outs, ridden = _hosted_call(
        body, rider, name=name, grid=(nc_seq // grp,),
        in_specs=[row, row, row, row, pl.BlockSpec((bsz, grp, NH, CHUNK, CHUNK), lambda i: (0, i, 0, 0, 0)),
                  pl.BlockSpec((bsz, grp, NH, LANE), lambda i: (0, i, 0, 0))],
        out_specs=[row, row, pl.BlockSpec((bsz, grp, NH, DN_D, DN_D), lambda i: (0, i, 0, 0, 0))],
        out_shape=[jax.ShapeDtypeStruct((bsz, t_seq, hw), BF16), jax.ShapeDtypeStruct((bsz, t_seq, hw), BF16),
                   jax.ShapeDtypeStruct((bsz, nc_seq, NH, DN_D, DN_D), BF16)],
        scratch_shapes=[pltpu.VMEM((bsz, NH, DN_D, DN_D), F32)],
        compiler_params=_cp(VMEM_BIG, ("arbitrary",)), args=(u, w, qg, kd, p, gl))
    o, vn, hist = outs
    return o.reshape(bsz * t_seq, hw), vn.reshape(bsz * t_seq, hw), hist, ridden


def _dn_scan_bwd(do, w, qg, kd, vn, p, gl, hist, *, bsz, nc_seq, name):
    hw = NH * DN_D
    t_seq = nc_seq * CHUNK
    do, w, qg, kd, vn = (z.reshape(bsz, t_seq, hw) for z in (do, w, qg, kd, vn))
    p = p.reshape(bsz, nc_seq, NH, CHUNK, CHUNK)
    gl = gl.reshape(bsz, nc_seq, NH, LANE)
    grp = _group(nc_seq)

    def body(do_ref, w_ref, qg_ref, kd_ref, vn_ref, p_ref, gl_ref, hist_ref,
             du_ref, dw_ref, dqg_ref, dkd_ref, dgl_ref, ds_ref):
        @pl.when(pl.program_id(0) == 0)
        def _():
            ds_ref[...] = jnp.zeros_like(ds_ref)

        def chain(b, h, gi):
            sl = slice(h * DN_D, (h + 1) * DN_D)
            rows = pl.ds(gi * CHUNK, CHUNK)
            s = hist_ref[b, gi, h]
            dsn = ds_ref[b, h]
            doh = do_ref[b, rows, sl]
            vnh = vn_ref[b, rows, sl]
            kdh = kd_ref[b, rows, sl]
            dvn = _tn(p_ref[b, gi, h], doh) + _nn(kdh, dsn)
            du_ref[b, rows, sl] = dvn.astype(du_ref.dtype)
            dqg_ref[b, rows, sl] = _nt(doh, s).astype(dqg_ref.dtype)
            dkd_ref[b, rows, sl] = _nt(vnh, dsn).astype(dkd_ref.dtype)
            ds_part = _tn(qg_ref[b, rows, sl], doh) + gl_ref[b, gi, h:h + 1, :] * dsn
            dgl = jnp.sum(jnp.sum(dsn * s, axis=0, keepdims=True), axis=1, keepdims=True)
            dgl_ref[b, gi, h:h + 1, :] = jnp.broadcast_to(dgl, (1, LANE))
            yield
            dw_ref[b, rows, sl] = (-_nt(dvn, s)).astype(dw_ref.dtype)
            ds_ref[b, h] = ds_part - _tn(w_ref[b, rows, sl], dvn)

        for gi in reversed(range(grp)):
            _run([chain(b, h, gi) for b in range(bsz) for h in range(NH)])

    steps = nc_seq // grp
    rev = lambda i: steps - 1 - i
    row = pl.BlockSpec((bsz, grp * CHUNK, hw), lambda i: (0, rev(i), 0))
    mat = pl.BlockSpec((bsz, grp, NH, CHUNK, CHUNK), lambda i: (0, rev(i), 0, 0, 0))
    glb = pl.BlockSpec((bsz, grp, NH, LANE), lambda i: (0, rev(i), 0, 0))
    big = jax.ShapeDtypeStruct((bsz, t_seq, hw), BF16)
    outs = pl.pallas_call(
        body, name=name, grid=(steps,),
        in_specs=[row, row, row, row, row, mat, glb,
                  pl.BlockSpec((bsz, grp, NH, DN_D, DN_D), lambda i: (0, rev(i), 0, 0, 0))],
        out_specs=[row, row, row, row, glb],
        out_shape=[big, big, jax.ShapeDtypeStruct(big.shape, F32), jax.ShapeDtypeStruct(big.shape, F32),
                   jax.ShapeDtypeStruct((bsz, nc_seq, NH, LANE), F32)],
        scratch_shapes=[pltpu.VMEM((bsz, NH, DN_D, DN_D), F32)],
        compiler_params=_cp(VMEM_BIG, ("arbitrary",)),
    )(do, w, qg, kd, vn, p, gl, hist)
    du, dw, dqg, dkd, dgl = outs
    n = bsz * t_seq
    return (du.reshape(n, hw), dw.reshape(n, hw), dqg.reshape(n, hw), dkd.reshape(n, hw),
            dgl.reshape(bsz * nc_seq, NH, LANE))


def _dn_intra_bwd(qn, kn, v, projp, alog_row, dtb_row, u, w, tmat, du, dw, dqg, dkd, do, vn, dgl, *, nc_seq, name,
                  rider=None):
    n = qn.shape[0]
    nct = n // CHUNK
    hw = NH * DN_D
    scale = DN_D ** -0.5

    grp = _group(nc_seq)

    def inner(gi, q_ref, k_ref, v_ref, sa_ref, al_ref, dt_ref, u_ref, w_ref, t_ref, du_ref, dw_ref, dqg_ref, dkd_ref,
              do_ref, vn_ref, dgl_ref, dq_ref, dk_ref, dv_ref, dsa_ref, dal_ref, ddt_ref):
        ci = (pl.program_id(0) * grp + gi) % nc_seq
        sa = sa_ref[...]
        beta_t, g_t, gam_t, valid, ea = _dn_gates(sa, al_ref[...], dt_ref[...], ci)
        yield
        lane = _iota2((CHUNK, LANE), 1)
        gates_t = jnp.where(lane < 4, beta_t, gam_t).T
        r, c = _masks64()
        incl, strict, upper, supper = r >= c, r > c, r <= c, r < c
        rows1 = _iota2((CHUNK, 1), 0)
        acc = [jnp.zeros((CHUNK, LANE), F32)]

        def head(h):
            sl = slice(h * DN_D, (h + 1) * DN_D)
            beta_w = jnp.broadcast_to(beta_t[:, h:h + 1], (CHUNK, DN_D))
            gam_w = jnp.broadcast_to(gam_t[:, 4 + h:5 + h], (CHUNK, DN_D))
            beta_s, gam_s = beta_w[:, 0:CHUNK], gam_w[:, 0:CHUNK]
            beta_row = gates_t[h:h + 1, :]
            gam_row = gates_t[4 + h:5 + h, :]
            gl = gam_t[CHUNK - 1:CHUNK, 4 + h:5 + h]
            dec = jnp.exp(jnp.where(incl, gam_s - gam_row, -jnp.inf))
            dec_t = jnp.exp(jnp.where(upper, gam_row - gam_s, -jnp.inf))
            egam_w = jnp.exp(gam_w)
            ekd_w = jnp.exp(gl - gam_w)
            kh = k_ref[:, sl].astype(F32)
            qh = q_ref[:, sl].astype(F32) * scale
            vh = v_ref[:, sl].astype(F32)
            uh = u_ref[:, sl]
            wh = w_ref[:, sl]
            doh = do_ref[:, sl]
            vnh = vn_ref[:, sl]
            kk = _nt(kh, kh)
            qk = _nt(qh, kh)
            qk_t = _nt(kh, qh)
            dp = _nt(doh, vnh)
            dp_t = _nt(vnh, doh)
            tm_t = t_ref[0, h].T
            dvb = _nn(tm_t, du_ref[:, sl])
            dkg = _nn(tm_t, dw_ref[:, sl])
            yield
            m = _nt(dvb, uh) + _nt(dkg, wh)
            m_t = _nt(uh, dvb) + _nt(wh, dkg)
            yield
            da = jnp.where(strict, -m, 0.0)
            da_t = jnp.where(supper, -m_t, 0.0)
            a = jnp.where(strict, beta_s * kk * dec, 0.0)
            a_t = jnp.where(supper, beta_row * kk * dec_t, 0.0)
            dad = da * dec
            dad_t = da_t * dec_t
            dpm = jnp.where(incl, dp, 0.0)
            dpm_t = jnp.where(upper, dp_t, 0.0)
            e = da * a + dpm * (qk * dec)
            e_t = da_t * a_t + dpm_t * (qk_t * dec_t)
            dqgh = dqg_ref[:, sl].astype(F32)
            dkdh = dkd_ref[:, sl].astype(F32)
            bg_w = beta_w * egam_w
            dkh = (_nn(beta_s * dad, kh) + _nn(beta_row * dad_t, kh) + _nn(dpm_t * dec_t, qh)
                   + bg_w * dkg + ekd_w * dkdh)
            dqh = _nn(dpm * dec, kh) + egam_w * dqgh
            t_kd = dkdh * (ekd_w * kh)
            dbeta = (jnp.sum(dad * kk, axis=1, keepdims=True)
                     + jnp.sum(dkg * (egam_w * kh) + dvb * vh, axis=1, keepdims=True))
            dgam = (jnp.sum(e - e_t, axis=1, keepdims=True)
                    + jnp.sum(dkg * (bg_w * kh) + dqgh * (egam_w * qh) - t_kd, axis=1, keepdims=True))
            dgam_last = (jnp.sum(jnp.sum(t_kd, axis=0, keepdims=True), axis=1, keepdims=True)
                         + dgl_ref[0, h:h + 1, 0:1] * jnp.exp(gl))
            dgam = dgam + jnp.where(rows1 == CHUNK - 1, dgam_last, 0.0)
            dq_ref[:, sl] = (dqh * scale).astype(dq_ref.dtype)
            dk_ref[:, sl] = dkh.astype(dk_ref.dtype)
            dv_ref[:, sl] = (beta_w * dvb).astype(dv_ref.dtype)
            acc[0] = acc[0] + jnp.where(lane == h, dbeta, 0.0) + jnp.where(lane == 4 + h, dgam, 0.0)

        yield from _round_robin([head(h) for h in range(NH)])
        acc_t = acc[0]
        dg_t = _tri_sum(upper, acc_t)
        ddb = acc_t * beta_t * (1.0 - beta_t)
        dda = jnp.where(valid, dg_t * (-ea) * _sigmoid(sa + dt_ref[...]), 0.0)
        dsa_ref[...] = jnp.where(lane < 4, ddb, jnp.where(lane < 8, dda, 0.0)).astype(dsa_ref.dtype)
        in_g = jnp.logical_and(lane >= 4, lane < 8)
        dal = jnp.sum(jnp.where(in_g, dg_t * g_t, 0.0), axis=0, keepdims=True)
        ddt = jnp.sum(jnp.where(in_g, dda, 0.0), axis=0, keepdims=True)
        _accumulate(dal_ref, dal, gi)
        _accumulate(ddt_ref, ddt, gi)

    rows = grp * CHUNK
    row = pl.BlockSpec((rows, hw), lambda i: (i, 0))
    vec = pl.BlockSpec((1, LANE), lambda i: (0, 0))
    mat = pl.BlockSpec((grp, NH, CHUNK, CHUNK), lambda i: (i, 0, 0, 0))
    glb = pl.BlockSpec((grp, NH, LANE), lambda i: (i, 0, 0))
    big = jax.ShapeDtypeStruct((n, hw), F32)
    v128 = jax.ShapeDtypeStruct((1, LANE), F32)
    kinds = (["row"] * 4 + ["whole"] * 2 + ["row"] * 2 + ["lead"] + ["row"] * 6 + ["lead"]
             + ["row"] * 4 + ["whole"] * 2)
    outs, ridden = _hosted_call(
        _per_chunk(inner, kinds, grp), rider, name=name, grid=(nct // grp,),
        in_specs=[row, row, row, pl.BlockSpec((rows, LANE), lambda i: (i, 0)), vec, vec,
                  row, row, mat, row, row, row, row, row, row, glb],
        out_specs=[row, row, row, pl.BlockSpec((rows, LANE), lambda i: (i, 0)), vec, vec],
        out_shape=[big, big, big, jax.ShapeDtypeStruct((n, LANE), BF16), v128, v128],
        scratch_shapes=[], compiler_params=_cp(VMEM_BIG, ("arbitrary",)),
        args=(qn, kn, v, projp, alog_row, dtb_row, u, w, tmat, du, dw, dqg, dkd, do, vn, dgl))
    return (*outs, ridden)


GQ_W = NH * GLA_DK
GV_W = NH * GLA_DV
GLA_NORM = 16.0
MID = CHUNK // 2


def _gla_gates(sb, w2p, gb, chunk_in_seq):
    rows = _iota2((CHUNK, GQ_W), 0)
    valid = jnp.logical_or(rows >= N_PAD, chunk_in_seq > 0)
    graw = _nn(sb, w2p) + gb
    yield
    g = jnp.where(valid, _logsigmoid(graw) * (1.0 / GLA_NORM), 0.0)
    r, c = _masks64()
    bcum = _tri_sum(r >= c, g)
    yield
    return graw, bcum, valid


def _head_mask(h):
    lane = _iota2((1, GQ_W), 1)
    return jnp.logical_and(lane >= h * GLA_DK, lane < (h + 1) * GLA_DK)


def _gla_intra_fwd(projp, gates, w2p, gb, *, nc_seq, name):
    n = projp.shape[0]
    nct = n // CHUNK
    scale = GLA_DK ** -0.5

    grp = _group(nc_seq)
    rows = grp * CHUNK

    def inner(gi, qk_ref, v_ref, sb_ref, w2_ref, gb_ref, oi_ref, qg_ref, kd_ref, gl_ref):
        ci = (pl.program_id(0) * grp + gi) % nc_seq
        _, bc, _ = yield from _gla_gates(sb_ref[...], w2_ref[...], gb_ref[...], ci)
        bref = bc[MID:MID + 1, :]
        bl = bc[CHUNK - 1:CHUNK, :]
        q = qk_ref[:, 0:GQ_W].astype(F32) * scale
        k = qk_ref[:, GQ_W:2 * GQ_W].astype(F32)
        qi = q * jnp.exp(bc - bref)
        ki = k * jnp.exp(bref - bc)
        qg_ref[...] = (q * jnp.exp(bc)).astype(qg_ref.dtype)
        kd_ref[...] = (k * jnp.exp(bl - bc)).astype(kd_ref.dtype)
        gl_ref[0] = jnp.exp(bl)
        r, c = _masks64()
        incl = r >= c
        a = [jnp.where(incl, _nt(jnp.where(_head_mask(h), qi, 0.0), ki), 0.0) for h in range(NH)]
        yield
        for h in range(NH):
            oi_ref[:, h * GLA_DV:(h + 1) * GLA_DV] = _nn(a[h], v_ref[:, h * GLA_DV:(h + 1) * GLA_DV]).astype(oi_ref.dtype)

    kinds = ["row"] * 3 + ["whole"] * 2 + ["row"] * 3 + ["lead"]
    return pl.pallas_call(
        _per_chunk(inner, kinds, grp), name=name, grid=(nct // grp,),
        in_specs=[pl.BlockSpec((rows, 2 * GQ_W), lambda i: (i, C_GQK // (2 * GQ_W))),
                  pl.BlockSpec((rows, GV_W), lambda i: (i, C_GV // GV_W)),
                  pl.BlockSpec((rows, LANE), lambda i: (i, 1)),
                  pl.BlockSpec((LANE, GQ_W), lambda i: (0, 0)), pl.BlockSpec((1, GQ_W), lambda i: (0, 0))],
        out_specs=[pl.BlockSpec((rows, GV_W), lambda i: (i, 0)), pl.BlockSpec((rows, GQ_W), lambda i: (i, 0)),
                   pl.BlockSpec((rows, GQ_W), lambda i: (i, 0)), pl.BlockSpec((grp, 1, GQ_W), lambda i: (i, 0, 0))],
        out_shape=[jax.ShapeDtypeStruct((n, GV_W), BF16), jax.ShapeDtypeStruct((n, GQ_W), BF16),
                   jax.ShapeDtypeStruct((n, GQ_W), BF16), jax.ShapeDtypeStruct((nct, 1, GQ_W), F32)],
        compiler_params=_cp(VMEM_BIG),
    )(projp, projp, gates, w2p, gb)


def _gla_scan_fwd(oi, qg, kd, gl, projp, *, bsz, nc_seq, name, rider=None):
    t_seq = nc_seq * CHUNK
    oi = oi.reshape(bsz, t_seq, GV_W)
    qg, kd = qg.reshape(bsz, t_seq, GQ_W), kd.reshape(bsz, t_seq, GQ_W)
    gl = gl.reshape(bsz, nc_seq, 1, GQ_W)
    pj = projp.reshape(bsz, t_seq, PW)
    grp = _group(nc_seq)

    def body(oi_ref, qg_ref, kd_ref, gl_ref, v_ref, o_ref, hist_ref, st_ref):
        @pl.when(pl.program_id(0) == 0)
        def _():
            st_ref[...] = jnp.zeros_like(st_ref)

        for gi in range(grp):
            rows = pl.ds(gi * CHUNK, CHUNK)
            for b in range(bsz):
                st = st_ref[b]
                hist_ref[b, gi] = st.astype(hist_ref.dtype)
                qgb = qg_ref[b, rows, :]
                kdb = kd_ref[b, rows, :]
                upd = jnp.zeros((GLA_DV, GQ_W), F32)
                for h in range(NH):
                    sl = slice(h * GLA_DV, (h + 1) * GLA_DV)
                    m = _head_mask(h)
                    o_ref[b, rows, sl] = (oi_ref[b, rows, sl] + _nt(jnp.where(m, qgb, 0.0), st)).astype(o_ref.dtype)
                    upd = upd + jnp.where(m, _tn(v_ref[b, rows, sl], kdb), 0.0)
                st_ref[b] = gl_ref[b, gi] * st + upd

    rws = grp * CHUNK
    outs, ridden = _hosted_call(
        body, rider, name=name, grid=(nc_seq // grp,),
        in_specs=[pl.BlockSpec((bsz, rws, GV_W), lambda i: (0, i, 0)),
                  pl.BlockSpec((bsz, rws, GQ_W), lambda i: (0, i, 0)),
                  pl.BlockSpec((bsz, rws, GQ_W), lambda i: (0, i, 0)),
                  pl.BlockSpec((bsz, grp, 1, GQ_W), lambda i: (0, i, 0, 0)),
                  pl.BlockSpec((bsz, rws, GV_W), lambda i: (0, i, C_GV // GV_W))],
        out_specs=[pl.BlockSpec((bsz, rws, GV_W), lambda i: (0, i, 0)),
                   pl.BlockSpec((bsz, grp, GLA_DV, GQ_W), lambda i: (0, i, 0, 0))],
        out_shape=[jax.ShapeDtypeStruct((bsz, t_seq, GV_W), BF16),
                   jax.ShapeDtypeStruct((bsz, nc_seq, GLA_DV, GQ_W), BF16)],
        scratch_shapes=[pltpu.VMEM((bsz, GLA_DV, GQ_W), F32)],
        compiler_params=_cp(VMEM_BIG, ("arbitrary",)), args=(oi, qg, kd, gl, pj))
    return outs[0].reshape(bsz * t_seq, GV_W), outs[1], ridden


def _gla_scan_bwd(do, qg, kd, gl, projp, hist, *, bsz, nc_seq, name):
    t_seq = nc_seq * CHUNK
    do = do.reshape(bsz, t_seq, GV_W)
    qg, kd = qg.reshape(bsz, t_seq, GQ_W), kd.reshape(bsz, t_seq, GQ_W)
    gl = gl.reshape(bsz, nc_seq, 1, GQ_W)
    pj = projp.reshape(bsz, t_seq, PW)
    grp = _group(nc_seq)

    def body(do_ref, qg_ref, kd_ref, gl_ref, v_ref, hist_ref, dqg_ref, dkd_ref, dv_ref, dgl_ref, dst_ref):
        @pl.when(pl.program_id(0) == 0)
        def _():
            dst_ref[...] = jnp.zeros_like(dst_ref)

        for gi in reversed(range(grp)):
            rows = pl.ds(gi * CHUNK, CHUNK)
            for b in range(bsz):
                st = hist_ref[b, gi]
                dst = dst_ref[b]
                qgb = qg_ref[b, rows, :]
                kdb = kd_ref[b, rows, :]
                dqg = jnp.zeros((CHUNK, GQ_W), F32)
                dkd = jnp.zeros((CHUNK, GQ_W), F32)
                add = jnp.zeros((GLA_DV, GQ_W), F32)
                for h in range(NH):
                    sl = slice(h * GLA_DV, (h + 1) * GLA_DV)
                    m = _head_mask(h)
                    doh = do_ref[b, rows, sl]
                    vh = v_ref[b, rows, sl]
                    dqg = dqg + jnp.where(m, _nn(doh, st), 0.0)
                    dkd = dkd + jnp.where(m, _nn(vh, dst), 0.0)
                    dv_ref[b, rows, sl] = _nt(jnp.where(m, kdb, 0.0), dst).astype(dv_ref.dtype)
                    add = add + jnp.where(m, _tn(doh, qgb), 0.0)
                dqg_ref[b, rows, :] = dqg.astype(dqg_ref.dtype)
                dkd_ref[b, rows, :] = dkd.astype(dkd_ref.dtype)
                dgl_ref[b, gi] = jnp.sum(dst * st, axis=0, keepdims=True)
                dst_ref[b] = gl_ref[b, gi] * dst + add

    steps = nc_seq // grp
    rws = grp * CHUNK
    rev = lambda i: steps - 1 - i
    outs = pl.pallas_call(
        body, name=name, grid=(steps,),
        in_specs=[pl.BlockSpec((bsz, rws, GV_W), lambda i: (0, rev(i), 0)),
                  pl.BlockSpec((bsz, rws, GQ_W), lambda i: (0, rev(i), 0)),
                  pl.BlockSpec((bsz, rws, GQ_W), lambda i: (0, rev(i), 0)),
                  pl.BlockSpec((bsz, grp, 1, GQ_W), lambda i: (0, rev(i), 0, 0)),
                  pl.BlockSpec((bsz, rws, GV_W), lambda i: (0, rev(i), C_GV // GV_W)),
                  pl.BlockSpec((bsz, grp, GLA_DV, GQ_W), lambda i: (0, rev(i), 0, 0))],
        out_specs=[pl.BlockSpec((bsz, rws, GQ_W), lambda i: (0, rev(i), 0)),
                   pl.BlockSpec((bsz, rws, GQ_W), lambda i: (0, rev(i), 0)),
                   pl.BlockSpec((bsz, rws, GV_W), lambda i: (0, rev(i), 0)),
                   pl.BlockSpec((bsz, grp, 1, GQ_W), lambda i: (0, rev(i), 0, 0))],
        out_shape=[jax.ShapeDtypeStruct((bsz, t_seq, GQ_W), F32), jax.ShapeDtypeStruct((bsz, t_seq, GQ_W), F32),
                   jax.ShapeDtypeStruct((bsz, t_seq, GV_W), BF16), jax.ShapeDtypeStruct((bsz, nc_seq, 1, GQ_W), F32)],
        scratch_shapes=[pltpu.VMEM((bsz, GLA_DV, GQ_W), F32)],
        compiler_params=_cp(VMEM_BIG, ("arbitrary",)),
    )(do, qg, kd, gl, pj, hist)
    n = bsz * t_seq
    return (outs[0].reshape(n, GQ_W), outs[1].reshape(n, GQ_W), outs[2].reshape(n, GV_W),
            outs[3].reshape(bsz * nc_seq, 1, GQ_W))


def _gla_intra_bwd(projp, gates, w2p, gb, do, dqg, dkd, dvi, dgl, *, nc_seq, name):
    n = projp.shape[0]
    nct = n // CHUNK
    scale = GLA_DK ** -0.5

    grp = _group(nc_seq)
    rows = grp * CHUNK

    def inner(gi, qk_ref, v_ref, sb_ref, w2_ref, gb_ref, do_ref, dqg_ref, dkd_ref, dvi_ref, dgl_ref,
              dqk_ref, dv_ref, dsb_ref, dw2_ref, dgb_ref):
        ci = (pl.program_id(0) * grp + gi) % nc_seq
        sb = sb_ref[...]
        w2 = w2_ref[...]
        graw, bc, valid = yield from _gla_gates(sb, w2, gb_ref[...], ci)
        bref = bc[MID:MID + 1, :]
        bl = bc[CHUNK - 1:CHUNK, :]
        q = qk_ref[:, 0:GQ_W].astype(F32) * scale
        k = qk_ref[:, GQ_W:2 * GQ_W].astype(F32)
        ex1 = jnp.exp(bc - bref)
        ex2 = jnp.exp(bref - bc)
        eb = jnp.exp(bc)
        ekd = jnp.exp(bl - bc)
        qi, ki = q * ex1, k * ex2
        r, c = _masks64()
        incl = r >= c
        upper = r <= c
        a_t, da, da_t = [], [], []
        for h in range(NH):
            sl = slice(h * GLA_DV, (h + 1) * GLA_DV)
            doh = do_ref[:, sl]
            vh = v_ref[:, sl]
            a_t.append(jnp.where(upper, _nt(jnp.where(_head_mask(h), ki, 0.0), qi), 0.0))
            da.append(jnp.where(incl, _nt(doh, vh), 0.0))
            da_t.append(jnp.where(upper, _nt(vh, doh), 0.0))
        yield
        dqi = jnp.zeros((CHUNK, GQ_W), F32)
        dki = jnp.zeros((CHUNK, GQ_W), F32)
        for h in range(NH):
            sl = slice(h * GLA_DV, (h + 1) * GLA_DV)
            m = _head_mask(h)
            dv_ref[:, sl] = (_nn(a_t[h], do_ref[:, sl]) + dvi_ref[:, sl]).astype(dv_ref.dtype)
            dqi = dqi + jnp.where(m, _nn(da[h], ki), 0.0)
            dki = dki + jnp.where(m, _nn(da_t[h], qi), 0.0)
        yield
        dqg = dqg_ref[...].astype(F32)
        dkd = dkd_ref[...].astype(F32)
        dqk_ref[:, 0:GQ_W] = ((dqi * ex1 + dqg * eb) * scale).astype(dqk_ref.dtype)
        dqk_ref[:, GQ_W:2 * GQ_W] = (dki * ex2 + dkd * ekd).astype(dqk_ref.dtype)
        t_qi, t_ki, t_kd = dqi * qi, dki * ki, dkd * (k * ekd)
        db = t_qi - t_ki + dqg * (q * eb) - t_kd
        dbref = jnp.sum(t_ki - t_qi, axis=0, keepdims=True)
        dbl = jnp.sum(t_kd, axis=0, keepdims=True) + dgl_ref[0] * jnp.exp(bl)
        rows = _iota2((CHUNK, GQ_W), 0)
        db = db + jnp.where(rows == MID, dbref, 0.0) + jnp.where(rows == CHUNK - 1, dbl, 0.0)
        dg = _tri_sum(upper, db)
        yield
        dgraw = jnp.where(valid, dg * (1.0 / GLA_NORM) * _sigmoid(-graw), 0.0)
        dsb_ref[...] = _nt(dgraw, w2).astype(dsb_ref.dtype)
        dw2 = _tn(sb, dgraw)
        dgb = jnp.sum(dgraw, axis=0, keepdims=True)
        _accumulate(dw2_ref, dw2, gi)
        _accumulate(dgb_ref, dgb, gi)

    rq = pl.BlockSpec((rows, GQ_W), lambda i: (i, 0))
    rv = pl.BlockSpec((rows, GV_W), lambda i: (i, 0))
    kinds = ["row"] * 3 + ["whole"] * 2 + ["row"] * 4 + ["lead"] + ["row"] * 3 + ["whole"] * 2
    return pl.pallas_call(
        _per_chunk(inner, kinds, grp), name=name, grid=(nct // grp,),
        in_specs=[pl.BlockSpec((rows, 2 * GQ_W), lambda i: (i, C_GQK // (2 * GQ_W))),
                  pl.BlockSpec((rows, GV_W), lambda i: (i, C_GV // GV_W)),
                  pl.BlockSpec((rows, LANE), lambda i: (i, 1)),
                  pl.BlockSpec((LANE, GQ_W), lambda i: (0, 0)), pl.BlockSpec((1, GQ_W), lambda i: (0, 0)),
                  rv, rq, rq, rv, pl.BlockSpec((grp, 1, GQ_W), lambda i: (i, 0, 0))],
        out_specs=[pl.BlockSpec((rows, 2 * GQ_W), lambda i: (i, 0)), rv, pl.BlockSpec((rows, LANE), lambda i: (i, 0)),
                   pl.BlockSpec((LANE, GQ_W), lambda i: (0, 0)), pl.BlockSpec((1, GQ_W), lambda i: (0, 0))],
        out_shape=[jax.ShapeDtypeStruct((n, 2 * GQ_W), BF16), jax.ShapeDtypeStruct((n, GV_W), BF16),
                   jax.ShapeDtypeStruct((n, LANE), BF16), jax.ShapeDtypeStruct((LANE, GQ_W), F32),
                   jax.ShapeDtypeStruct((1, GQ_W), F32)],
        compiler_params=_cp(VMEM_BIG, ("arbitrary",)),
    )(projp, projp, gates, w2p, gb, do, dqg, dkd, dvi, dgl)


SECTIONS = ((C_QKV, 1536), (C_DZ, 512), (C_GQK, 512), (C_GV, 512), (C_GR, 512), (C_SA, 128), (C_SB, 128))


def _inproj_bwd(secs, wp, h0, g1, dx1, *, tr, name):
    n, d = h0.shape

    def body(*refs):
        sec_refs = refs[:len(SECTIONS)]
        wp_ref, h0_ref, g_ref, dx1_ref, o_ref, dg_ref = refs[len(SECTIONS):]
        dh = None
        for s_ref, (off, wd) in zip(sec_refs, SECTIONS):
            part = _nt(s_ref[...], wp_ref[:, off:off + wd])
            dh = part if dh is None else dh + part
        dx, dg = _rms_bwd_math(h0_ref[...], g_ref[...], dh)
        o_ref[...] = dx1_ref[...] + dx

        @pl.when(pl.program_id(0) == 0)
        def _():
            dg_ref[...] = dg

        @pl.when(pl.program_id(0) > 0)
        def _():
            dg_ref[...] += dg

    row = pl.BlockSpec((tr, d), lambda i: (i, 0))
    vec = pl.BlockSpec((1, d), lambda i: (0, 0))
    return pl.pallas_call(
        body, name=name, grid=(n // tr,),
        in_specs=[pl.BlockSpec((tr, wd), lambda i: (i, 0)) for _, wd in SECTIONS]
        + [pl.BlockSpec((d, PW), lambda i: (0, 0)), row, vec, row],
        out_specs=[row, vec],
        out_shape=[jax.ShapeDtypeStruct((n, d), F32), jax.ShapeDtypeStruct((1, d), F32)],
        compiler_params=_cp(VMEM_BIG),
    )(*secs, wp, h0, g1, dx1)


def _adamw(w, g, m, v, *, name, emit_grad=False, col_tile=None):
    lead = w.ndim - 2
    r, c = w.shape[-2:]
    tr = r if col_tile is not None else (_tile(r, 256, 8) if r > 256 else r)
    tc = col_tile if col_tile is not None else c
    c1 = 1.0 - ADAM_B1 ** ADAM_STEP
    c2 = 1.0 - ADAM_B2 ** ADAM_STEP
    n_out = 4 if emit_grad else 3

    def body(w_ref, g_ref, m_ref, v_ref, *out_refs):
        rd = (lambda ref: ref[0]) if lead else (lambda ref: ref[...])
        gv = g_ref[:, 0:tc]
        nm = ADAM_B1 * rd(m_ref) + (1.0 - ADAM_B1) * gv
        nv = ADAM_B2 * rd(v_ref) + (1.0 - ADAM_B2) * (gv * gv)
        res = [-ADAM_LR * ((nm / c1) / (jnp.sqrt(nv / c2) + ADAM_EPS) + ADAM_WD * rd(w_ref)), nm, nv, gv]
        for o_ref, val in zip(out_refs, res):
            if lead:
                o_ref[0] = val
            else:
                o_ref[...] = val

    if col_tile is None:
        blk = pl.BlockSpec((1,) * lead + (tr, c), lambda i: (0,) * lead + (i, 0))
        gblk = pl.BlockSpec((tr, g.shape[1]), lambda i: (i, 0))
        steps = r // tr
    else:
        blk = pl.BlockSpec((1,) * lead + (r, tc), lambda j: (0,) * lead + (0, j))
        gblk = pl.BlockSpec((r, tc), lambda j: (0, j))
        steps = c // tc
    sds = jax.ShapeDtypeStruct(w.shape, F32)
    return pl.pallas_call(
        body, name=name, grid=(steps,), in_specs=[blk, gblk, blk, blk], out_specs=[blk] * n_out,
        out_shape=[sds] * n_out, compiler_params=_cp(VMEM_BIG),
    )(w, g, m, v)


def _adamw_rows(w, g, m, v, *, name):
    r, _, c = w.shape
    tr = max(t for t in range(1, 129) if r % t == 0)
    c1 = 1.0 - ADAM_B1 ** ADAM_STEP
    c2 = 1.0 - ADAM_B2 ** ADAM_STEP

    def body(w_ref, g_ref, m_ref, v_ref, d_ref, nm_ref, nv_ref, go_ref):
        gv = g_ref[...]
        nm = ADAM_B1 * m_ref[...] + (1.0 - ADAM_B1) * gv
        nv = ADAM_B2 * v_ref[...] + (1.0 - ADAM_B2) * (gv * gv)
        d_ref[...] = -ADAM_LR * ((nm / c1) / (jnp.sqrt(nv / c2) + ADAM_EPS) + ADAM_WD * w_ref[...])
        nm_ref[...] = nm
        nv_ref[...] = nv
        go_ref[...] = gv

    blk = pl.BlockSpec((tr, 1, c), lambda i: (i, 0, 0))
    sds = jax.ShapeDtypeStruct(w.shape, F32)
    return pl.pallas_call(
        body, name=name, grid=(r // tr,), in_specs=[blk] * 4, out_specs=[blk] * 4, out_shape=[sds] * 4,
        compiler_params=_cp(VMEM_BIG),
    )(w, g, m, v)


def _pair_sum(where, g, theirs, *, name):
    lead, r, cols = g.shape
    half = r // 2
    tr = _tile(half, 256, 16)
    nh = half // tr

    def body(w_ref, a_ref, b_ref, o_ref):
        o_ref[...] = (a_ref[...] + b_ref[...]).astype(o_ref.dtype)

    blk = pl.BlockSpec((1, tr, cols), lambda s, i, w: (s, i, 0))
    return pl.pallas_call(
        body, name=name,
        grid_spec=pltpu.PrefetchScalarGridSpec(
            num_scalar_prefetch=1, grid=(lead, nh),
            in_specs=[pl.BlockSpec((1, tr, cols), lambda s, i, w: (s, w[0] * nh + i, 0)), blk], out_specs=blk),
        out_shape=jax.ShapeDtypeStruct((lead, half, cols), BF16), compiler_params=_cp(VMEM_BIG),
    )(where, g, theirs)


def _chip_sum(where, pair, q, *, name):
    _, half, cols = pair.shape
    tr = _tile(half, 256, 16)
    nh = half // tr

    def body(w_ref, own_ref, q1_ref, q2_ref, q3_ref, o_ref):
        f = lambda ref: ref[0].astype(F32)
        o_ref[...] = ((f(own_ref) + f(q1_ref)) + f(q2_ref)) + f(q3_ref)

    def peer(d):
        return pl.BlockSpec((1, tr, cols), lambda i, w: ((w[1] + d) % N_CHIPS, i, 0))

    return pl.pallas_call(
        body, name=name,
        grid_spec=pltpu.PrefetchScalarGridSpec(
            num_scalar_prefetch=1, grid=(nh,),
            in_specs=[peer(0), peer(1), peer(2), peer(3)],
            out_specs=pl.BlockSpec((tr, cols), lambda i, w: (w[0] * nh + i, 0))),
        out_shape=jax.ShapeDtypeStruct((2 * half, cols), F32), compiler_params=_cp(VMEM_BIG),
    )(where, pair, q, q, q)


VM = pl.BlockSpec(memory_space=pltpu.VMEM)


def _row_chunks(rows, n_split):
    size = rows // n_split
    assert size * n_split == rows and size % 16 == 0, (rows, n_split)
    return [(s, pl.ds(s * size, size)) for s in range(n_split)], size


D2D_SPLIT = 4
ICI_SPLIT = 2


def _sibling_halves(grads):
    n_arr = len(grads)

    def body(*refs):
        ins = refs[:n_arr]
        theirs = refs[n_arr:2 * n_arr]
        send_sems, recv_sems = refs[2 * n_arr:]
        x, y, c = _place()
        copies = []
        for k in range(n_arr):
            half = ins[k].shape[1] // 2
            chunks, size = _row_chunks(half, D2D_SPLIT)
            for s, dst_rows in chunks:
                give = pltpu.make_async_remote_copy(
                    src_ref=ins[k].at[:, pl.ds((1 - c) * half + s * size, size), :], dst_ref=theirs[k].at[:, dst_rows, :],
                    send_sem=send_sems.at[k, s], recv_sem=recv_sems.at[k, s], device_id=(x, y, 1 - c),
                    device_id_type=MESH)
                give.start()
                copies.append(give)
        for give in copies:
            give.wait()

    halves = [jax.ShapeDtypeStruct((g.shape[0], g.shape[1] // 2, g.shape[2]), F32) for g in grads]
    sem = pltpu.SemaphoreType.DMA((n_arr, D2D_SPLIT))
    return pl.pallas_call(
        body, name="sibling_halves", in_specs=[ANY] * n_arr, out_specs=[ANY] * n_arr, out_shape=halves,
        scratch_shapes=[sem, sem],
    )(*grads)


def _chip_exchange(parts):
    n_arr = len(parts)

    def body(*refs):
        ins = refs[:n_arr]
        outs = refs[n_arr:2 * n_arr]
        send_sems, recv_sems = refs[2 * n_arr:]
        x, y, c = _place()
        me = 2 * x + y
        sends = []
        for k in range(n_arr):
            chunks, _ = _row_chunks(ins[k].shape[1], ICI_SPLIT)
            for d, (px, py, pj) in enumerate(_other_chips(x, y)):
                for s, rows in chunks:
                    cp = pltpu.make_async_remote_copy(
                        src_ref=ins[k].at[pj, rows, :], dst_ref=outs[k].at[me, rows, :], send_sem=send_sems.at[k, d, s],
                        recv_sem=recv_sems.at[k, d, s], device_id=(px, py, c), device_id_type=MESH)
                    cp.start()
                    sends.append(cp)
        for k in range(n_arr):
            chunks, _ = _row_chunks(ins[k].shape[1], ICI_SPLIT)
            for d, (px, py, pj) in enumerate(_other_chips(x, y)):
                for s, rows in chunks:
                    pltpu.make_async_remote_copy(
                        src_ref=ins[k].at[pj, rows, :], dst_ref=outs[k].at[pj, rows, :], send_sem=send_sems.at[k, d, s],
                        recv_sem=recv_sems.at[k, d, s], device_id=(px, py, c), device_id_type=MESH).wait_recv()
        for cp in sends:
            cp.wait_send()

    sem = pltpu.SemaphoreType.DMA((n_arr, 3, ICI_SPLIT))
    return pl.pallas_call(
        body, name="chip_exchange", in_specs=[ANY] * n_arr, out_specs=[ANY] * n_arr,
        out_shape=[jax.ShapeDtypeStruct(p.shape, p.dtype) for p in parts],
        scratch_shapes=[sem, sem],
    )(*parts)


class _SiblingHalvesRider:
    def __init__(self, grads):
        self.inputs = list(grads)
        self.out_shapes = [jax.ShapeDtypeStruct((g.shape[0], g.shape[1] // 2, g.shape[2]), F32) for g in grads]
        self.aliases = {}
        self.sems = [pltpu.SemaphoreType.DMA((len(grads), D2D_SPLIT))] * 2

    def _copies(self, ins, outs, sems):
        x, y, c = _place()
        for k in range(len(ins)):
            half = ins[k].shape[1] // 2
            chunks, size = _row_chunks(half, D2D_SPLIT)
            for s, dst_rows in chunks:
                yield pltpu.make_async_remote_copy(
                    src_ref=ins[k].at[:, pl.ds((1 - c) * half + s * size, size), :], dst_ref=outs[k].at[:, dst_rows, :],
                    send_sem=sems[0].at[k, s], recv_sem=sems[1].at[k, s], device_id=(x, y, 1 - c), device_id_type=MESH)

    def first(self, ins, outs, sems):
        for cp in self._copies(ins, outs, sems):
            cp.start()

    def last(self, ins, outs, sems):
        for cp in self._copies(ins, outs, sems):
            cp.wait()


class _ChipExchangeRider:
    def __init__(self, parts):
        self.inputs = list(parts)
        self.out_shapes = [jax.ShapeDtypeStruct(p.shape, p.dtype) for p in parts]
        self.aliases = {}
        self.sems = [pltpu.SemaphoreType.DMA((len(parts), 3, ICI_SPLIT))] * 2

    def _copies(self, ins, outs, sems, receiving):
        x, y, c = _place()
        for k in range(len(ins)):
            chunks, _ = _row_chunks(ins[k].shape[1], ICI_SPLIT)
            for d, (px, py, pj) in enumerate(_other_chips(x, y)):
                for s, rows in chunks:
                    yield pltpu.make_async_remote_copy(
                        src_ref=ins[k].at[pj, rows, :], dst_ref=outs[k].at[pj if receiving else 2 * x + y, rows, :],
                        send_sem=sems[0].at[k, d, s], recv_sem=sems[1].at[k, d, s], device_id=(px, py, c),
                        device_id_type=MESH)

    def first(self, ins, outs, sems):
        for cp in self._copies(ins, outs, sems, False):
            cp.start()

    def last(self, ins, outs, sems):
        for cp in self._copies(ins, outs, sems, True):
            cp.wait_recv()
        for cp in self._copies(ins, outs, sems, False):
            cp.wait_send()


def _sibling_join(bufs):
    n_arr = len(bufs)

    def body(*refs):
        bufs_out = refs[n_arr:2 * n_arr]
        send_sems, recv_sems = refs[2 * n_arr:]
        x, y, c = _place()
        copies = []
        for k in range(n_arr):
            half = bufs_out[k].shape[0] // 2
            chunks, size = _row_chunks(half, D2D_SPLIT)
            for s, _ in chunks:
                rows = pl.ds(c * half + s * size, size)
                give = pltpu.make_async_remote_copy(
                    src_ref=bufs_out[k].at[rows, :], dst_ref=bufs_out[k].at[rows, :], send_sem=send_sems.at[k, s],
                    recv_sem=recv_sems.at[k, s], device_id=(x, y, 1 - c), device_id_type=MESH)
                give.start()
                copies.append((k, s, half, size, give))
        for k, s, half, size, give in copies:
            rows = pl.ds((1 - c) * half + s * size, size)
            pltpu.make_async_remote_copy(
                src_ref=bufs_out[k].at[rows, :], dst_ref=bufs_out[k].at[rows, :], send_sem=send_sems.at[k, s],
                recv_sem=recv_sems.at[k, s], device_id=(x, y, 1 - c), device_id_type=MESH).wait_recv()
            give.wait_send()

    sem = pltpu.SemaphoreType.DMA((n_arr, D2D_SPLIT))
    return pl.pallas_call(
        body, name="sibling_join", in_specs=[ANY] * n_arr, out_specs=[ANY] * n_arr,
        out_shape=[jax.ShapeDtypeStruct(b.shape, F32) for b in bufs],
        input_output_aliases={k: k for k in range(n_arr)},
        scratch_shapes=[sem, sem],
    )(*bufs)


PACK_ROWS = 48


def _small_allreduce(pack):
    masks = [(dx, dy, dc) for dx in (0, 1) for dy in (0, 1) for dc in (0, 1)][1:]

    def body(p_ref, o_ref, buf, send_sems, recv_sems):
        x, y, c = _place()
        me = 4 * x + 2 * y + c
        buf[me] = p_ref[...]
        sends = []
        for k, (dx, dy, dc) in enumerate(masks):
            peer = (1 - x if dx else x, 1 - y if dy else y, 1 - c if dc else c)
            cp = pltpu.make_async_remote_copy(
                src_ref=p_ref, dst_ref=buf.at[me], send_sem=send_sems.at[k], recv_sem=recv_sems.at[k],
                device_id=peer, device_id_type=MESH)
            cp.start()
            sends.append(cp)
        for k, (dx, dy, dc) in enumerate(masks):
            peer = (1 - x if dx else x, 1 - y if dy else y, 1 - c if dc else c)
            pj = 4 * peer[0] + 2 * peer[1] + peer[2]
            pltpu.make_async_remote_copy(
                src_ref=p_ref, dst_ref=buf.at[pj], send_sem=send_sems.at[k], recv_sem=recv_sems.at[k],
                device_id=peer, device_id_type=MESH).wait_recv()
        for cp in sends:
            cp.wait_send()
        tot = buf[0]
        for k in range(1, 8):
            tot = tot + buf[k]
        o_ref[...] = tot
        o_ref[0:N_META, :] = tot[0:N_META] + tot[N_META:2 * N_META]

    return pl.pallas_call(
        body, name="small_allreduce", in_specs=[VM], out_specs=VM,
        out_shape=jax.ShapeDtypeStruct((PACK_ROWS, D_MODEL), F32),
        scratch_shapes=[pltpu.VMEM((8, PACK_ROWS, D_MODEL), F32), pltpu.SemaphoreType.DMA((7,)),
                        pltpu.SemaphoreType.DMA((7,))],
    )(pack)


def _pad_lanes(vec, offset):
    k = vec.shape[1]
    return jnp.concatenate([jnp.zeros((1, offset), F32), vec, jnp.zeros((1, LANE - offset - k), F32)], axis=1)


def _local_step(x, tgt, meta, norm1_g, wp, conv_w, a_log, dt_bias, dn_norm_g, gla_w2, gla_b, gla_norm_g,
                w_out, norm2_g, w_up, w_down, final_norm_g, late_gather=None, where=None):
    bsz, s_len, d = x.shape
    t_seq = s_len + CHUNK
    nc_seq = t_seq // CHUNK
    n = bsz * t_seq
    tr = _tile(t_seq, 832)
    tt = _tile(t_seq, 416)

    lead = jnp.concatenate([jnp.zeros((N_PAD, d), F32), meta], axis=0)
    h0 = jnp.concatenate([jnp.broadcast_to(lead[None], (bsz, CHUNK, d)), x], axis=1).reshape(n, d)
    tgt_p = jnp.concatenate([jnp.zeros((bsz, CHUNK, d), F32), tgt], axis=1).reshape(n, d)
    alog_row = _pad_lanes(a_log, 4)
    dtb_row = _pad_lanes(dt_bias, 4)
    w2p = jnp.concatenate([gla_w2, jnp.zeros((LANE - GLA_RANK, GQ_W), F32)], axis=0)

    h = _rms_fwd(h0, norm1_g, tr=tr, name="norm1")
    ride_up, ride_down, ride_out = late_gather if late_gather is not None else (None, None, None)
    res = _mm(h, wp, "nn", tm=tt, tn=PW, tk=d, out_dtypes=(BF16, F32), out_widths=(PW, PW - C_SA),
              epilogue=lambda acc: (acc, acc[:, C_SA:PW]), name="in_proj", rider=ride_up)
    ((projp, gates), got_up) = res if ride_up is not None else (res, None)
    qn, kn, v = _dnprep_fwd(projp, conv_w, bsz=bsz, t_seq=t_seq, tt=tt, name="dn_prep")
    u, w, qg, kd, pmat, tmat, gl, got_down = _dn_intra_fwd(qn, kn, v, gates, alog_row, dtb_row, nc_seq=nc_seq,
                                                           name="dn_intra", rider=ride_down)
    o_dn, vn, hist, got_out = _dn_scan_fwd(u, w, qg, kd, pmat, gl, bsz=bsz, nc_seq=nc_seq, name="dn_scan",
                                           rider=ride_out)
    oi, gqg, gkd, ggl = _gla_intra_fwd(projp, gates, w2p, gla_b, nc_seq=nc_seq, name="gla_intra")
    o_gla, ghist, _ = _gla_scan_fwd(oi, gqg, gkd, ggl, projp, bsz=bsz, nc_seq=nc_seq, name="gla_scan")
    if late_gather is not None:
        w_out = got_out[0].reshape(d, d)
        w_up = got_up[0].transpose(1, 0, 2).reshape(d, D_FF)
        w_down = got_down[0].reshape(D_FF, d)
    mix = _gnorm_fwd(o_dn, o_gla, projp, dn_norm_g, gla_norm_g, tr=tr, name="gated_norm")
    (x1,) = _mm(mix, w_out, "nn", tm=tr, tn=d, tk=d, out_dtypes=(F32,), extras=(h0,),
                epilogue=lambda acc, res: (res + acc,), name="out_proj")
    h2 = _rms_fwd(x1, norm2_g, tr=tr, name="norm2")

    (act,) = _mm(h2, w_up, "nn", tm=tt, tn=D_FF, tk=d, out_dtypes=(BF16,),
                 epilogue=lambda acc: (jnp.square(jnp.maximum(acc, 0.0)),), name="mlp_up", n_chunk=1024)
    dx2, dx2b, d_final_g, loss_tile = _mlp_down_loss(act, w_down, x1, final_norm_g, tgt_p, t_seq=t_seq, tr=tt,
                                                     name="mlp_down_loss")

    (dup,) = _mm(dx2b, w_down, "nt", tm=tt, tn=D_FF, tk=d, out_dtypes=(BF16,), extras=(act,),
                 epilogue=lambda acc, a: (acc * (2.0 * jnp.sqrt(a.astype(F32))),), name="mlp_down_bwd", n_chunk=1024)
    tk2 = 2 * tr if n % (2 * tr) == 0 else tr
    (d_w_down,) = _mm(act, dx2b, "tn", tm=D_FF // 2, tn=d, tk=tk2, out_dtypes=(F32,), name="w_down_grad")
    (d_w_up,) = _mm(h2, dup, "tn", tm=d, tn=D_FF // 2, tk=tk2, out_dtypes=(F32,), name="w_up_grad")
    mlp_sm = [d_w_up.reshape(d, N_CHIPS, D_FF // N_CHIPS).transpose(1, 0, 2), d_w_down.reshape(N_CHIPS, D_FF // N_CHIPS, d)]
    ride1 = _SiblingHalvesRider(mlp_sm) if where is not None else None
    dx1, dx1b, d_norm2_g, theirs = _mlp_up_bwd_norm(dup, w_up, x1, norm2_g, dx2, tr=tt, name="mlp_up_bwd_norm",
                                                    rider=ride1)
    ride2 = None
    if where is not None:
        mlp_pair = [_pair_sum(where, a, b, name=f"pair_sum_mlp{k}") for k, (a, b) in enumerate(zip(mlp_sm, theirs))]
        ride2 = _ChipExchangeRider(mlp_pair)

    (dmix,) = _mm(dx1b, w_out, "nt", tm=tr, tn=d, tk=d, out_dtypes=(BF16,), name="out_proj_bwd")
    (d_w_out,) = _mm(mix, dx1b, "tn", tm=d, tn=d, tk=tr, out_dtypes=(F32,), name="w_out_grad")
    do_dn, ddz, do_gla, dgr, d_dn_norm_g, d_gla_norm_g = _gnorm_bwd(
        dmix, o_dn, o_gla, projp, dn_norm_g, gla_norm_g, tr=tr, name="gated_norm_bwd")
    du, dw, dqg, dkd, dgl = _dn_scan_bwd(do_dn, w, qg, kd, vn, pmat, gl, hist, bsz=bsz, nc_seq=nc_seq,
                                          name="dn_scan_bwd")
    dqn, dkn, dv, dsa, d_alog, d_dtb, mlp_parts = _dn_intra_bwd(
        qn, kn, v, gates, alog_row, dtb_row, u, w, tmat, du, dw, dqg, dkd, do_dn, vn, dgl, nc_seq=nc_seq,
        name="dn_intra_bwd", rider=ride2)
    dz, d_conv_w = _dnprep_bwd_a(projp, conv_w, dqn, dkn, dv, bsz=bsz, t_seq=t_seq, tt=tt, name="dn_prep_bwd")
    dcin = _dnprep_bwd_b(dz, conv_w, bsz=bsz, t_seq=t_seq, tt=tt, name="conv_bwd")
    gdqg, gdkd, gdvi, gdgl = _gla_scan_bwd(do_gla, gqg, gkd, ggl, projp, ghist, bsz=bsz, nc_seq=nc_seq,
                                            name="gla_scan_bwd")
    dgqk, dgv, dsb, d_w2p, d_gla_b = _gla_intra_bwd(projp, gates, w2p, gla_b, do_gla, gdqg, gdkd, gdvi, gdgl,
                                                    nc_seq=nc_seq, name="gla_intra_bwd")

    secs = (dcin, ddz, dgqk, dgv, dgr, dsa, dsb)
    g_lo = _grad_tn(h, secs[0:2], tk=tr, name="w_in_grad_lo")
    g_hi = _grad_tn(h, secs[2:7], tk=tr, name="w_in_grad_hi")
    dh0, d_norm1_g = _inproj_bwd(secs, wp, h0, norm1_g, dx1, tr=tt, name="in_proj_bwd")
    dh0 = dh0.reshape(bsz, t_seq, d)
    grad_x = dh0[:, CHUNK:]
    d_meta_rows = dh0[:, N_PAD:CHUNK].reshape(bsz * N_META, d)

    grads = dict(w_in_lo=g_lo, w_in_hi=g_hi, w_out=d_w_out, w_up=d_w_up, w_down=d_w_down, meta_rows=d_meta_rows,
                 norm1_g=d_norm1_g, conv_w=d_conv_w, a_log_tile=d_alog, dt_bias_tile=d_dtb, dn_norm_g=d_dn_norm_g,
                 gla_w2=d_w2p[0:GLA_RANK], gla_b=d_gla_b, gla_norm_g=d_gla_norm_g, norm2_g=d_norm2_g,
                 final_norm_g=d_final_g, loss_tile=loss_tile)
    if where is not None:
        grads["mlp_exchanged"] = (mlp_pair, mlp_parts)
    return grad_x, grads


SHARD_W = IN_WIDTH // N_CHIPS
PADDED_ORDER = ((0, 2048), (2056, 3592), (2048, 2056), LANE - 8, (3592, 3608), LANE - GLA_RANK)


def _pad_layout(w_full):
    pieces = [jnp.zeros((w_full.shape[0], seg), w_full.dtype) if isinstance(seg, int) else w_full[:, seg[0]:seg[1]]
              for seg in PADDED_ORDER]
    return jnp.concatenate(pieces, axis=1)


def _padded_from_shards(stack):
    pieces = []
    for seg in PADDED_ORDER:
        if isinstance(seg, int):
            pieces.append(jnp.zeros((stack.shape[1], seg), stack.dtype))
            continue
        for j in range(N_CHIPS):
            lo, hi = max(seg[0], j * SHARD_W), min(seg[1], (j + 1) * SHARD_W)
            if lo < hi:
                pieces.append(stack[j, :, lo - j * SHARD_W:hi - j * SHARD_W])
    return jnp.concatenate(pieces, axis=1)


def _shards_from_padded(g_lo, g_hi):
    split = g_lo.shape[1]
    starts, pos = [], 0
    for seg in PADDED_ORDER:
        width = seg if isinstance(seg, int) else seg[1] - seg[0]
        if not isinstance(seg, int):
            starts.append((seg[0], seg[1], pos))
        pos += width
    shards = []
    for j in range(N_CHIPS):
        pieces = []
        for a, b, p0 in sorted(starts):
            lo, hi = max(a, j * SHARD_W), min(b, (j + 1) * SHARD_W)
            if lo < hi:
                src, off = (g_lo, 0) if p0 < split else (g_hi, split)
                pieces.append(src[:, p0 + lo - a - off:p0 + hi - a - off])
        pieces.append(jnp.zeros((g_lo.shape[0], D_MODEL - SHARD_W), g_lo.dtype))
        shards.append(jnp.concatenate(pieces, axis=1))
    return jnp.stack(shards)


def _pack_small(g, bsz):
    assert bsz * N_META == 32
    row = jnp.concatenate([g["a_log_tile"], g["dt_bias_tile"], g["dn_norm_g"], g["gla_norm_g"], g["gla_b"],
                           g["loss_tile"], jnp.zeros((1, LANE), F32)], axis=1)
    return jnp.concatenate([g["meta_rows"], g["norm1_g"], g["conv_w"].reshape(6, D_MODEL), row,
                            g["gla_w2"].reshape(4, D_MODEL), g["norm2_g"], g["final_norm_g"],
                            jnp.zeros((2, D_MODEL), F32)], axis=0)


def kernel(x, meta_tokens, norm1_g, w_in, conv_w, a_log, dt_bias, dn_norm_g, gla_w2, gla_b, gla_norm_g, w_out, norm2_g, w_up, w_down, final_norm_g, loss_target, m_meta_tokens, m_norm1_g, m_w_in, m_conv_w, m_a_log, m_dt_bias, m_dn_norm_g, m_gla_w2, m_gla_b, m_gla_norm_g, m_w_out, m_norm2_g, m_w_up, m_w_down, m_final_norm_g, v_meta_tokens, v_norm1_g, v_w_in, v_conv_w, v_a_log, v_dt_bias, v_dn_norm_g, v_gla_w2, v_gla_b, v_gla_norm_g, v_w_out, v_norm2_g, v_w_up, v_w_down, v_final_norm_g):
    bsz = x.shape[0]
    chip = 2 * lax.axis_index("x") + lax.axis_index("y")

    lane_pad = lambda a, wd: jnp.pad(a, ((0, 0), (0, wd - a.shape[1])))
    where = jnp.stack([lax.axis_index("c"), chip]).astype(jnp.int32)
    slot = lambda a, dt, nm: _to_slot(where, a, dt, name="slot_" + nm)
    early = _GatherRider([slot(lane_pad(w_in[0], D_MODEL), BF16, "w_in"), slot(meta_tokens, F32, "meta"),
                          slot(conv_w[0], F32, "conv"), slot(lane_pad(gla_w2[0], LANE), F32, "gla_w2")],
                         [True, False, False, False])
    g_in, g_meta, g_conv, g_w2 = _exchange_now(early, name="gather_early")
    late = (_GatherRider([slot(w_up[0], BF16, "w_up")], [True]), _GatherRider([slot(w_down[0], BF16, "w_down")], [True]),
            _GatherRider([slot(w_out[0], BF16, "w_out")], [True]))
    wp = _padded_from_shards(g_in)
    meta_f = g_meta.transpose(1, 0, 2).reshape(N_META, D_MODEL)
    conv_f = g_conv.transpose(1, 0, 2).reshape(4, QKV_W)
    w2_f = g_w2[:, :, 0:GQ_W // N_CHIPS].transpose(1, 0, 2).reshape(GLA_RANK, GQ_W)

    grad_x, g = _local_step(x, loss_target, meta_f, norm1_g, wp, conv_f, a_log, dt_bias, dn_norm_g, w2_f, gla_b,
                            gla_norm_g, None, norm2_g, None, None, final_norm_g.reshape(1, D_MODEL), late_gather=late, where=where)

    shard_major = [_shards_from_padded(g["w_in_lo"], g["w_in_hi"]), g["w_out"].reshape(N_CHIPS, D_MODEL // N_CHIPS, D_MODEL)]
    theirs = _exchange_now(_SiblingHalvesRider(shard_major), name="sibling_halves")
    pair = [_pair_sum(where, a, b, name=f"pair_sum_{k}") for k, (a, b) in enumerate(zip(shard_major, theirs))]
    parts = _exchange_now(_ChipExchangeRider(pair), name="chip_exchange")
    mlp_pair, mlp_parts = g["mlp_exchanged"]
    halves = [_chip_sum(where, p, q, name=f"chip_sum_{k}")
              for k, (p, q) in enumerate(zip(pair + mlp_pair, list(parts) + list(mlp_parts)))]
    gw_in, gw_out, gw_up, gw_down = _sibling_join(halves)

    red = _small_allreduce(_pack_small(g, bsz))
    g_meta_full = red[0:N_META]
    g_norm1 = red[32:33]
    g_conv_full = red[33:39].reshape(4, QKV_W)
    srow = red[39:40]
    g_alog, g_dtb = srow[:, 4:8], srow[:, LANE + 4:LANE + 8]
    g_dn_norm, g_gla_norm = srow[:, 2 * LANE:3 * LANE], srow[:, 3 * LANE:4 * LANE]
    g_gla_b = srow[:, 4 * LANE:6 * LANE]
    loss = srow[0, 6 * LANE]
    g_w2_full = red[40:44].reshape(GLA_RANK, GQ_W)
    g_norm2 = red[44:45]
    g_final = red[45:46]
    g_meta_sh = lax.dynamic_slice_in_dim(g_meta_full, chip * (D_MODEL // N_CHIPS), D_MODEL // N_CHIPS, axis=1)
    g_conv_sh = lax.dynamic_slice_in_dim(g_conv_full, chip * (QKV_W // N_CHIPS), QKV_W // N_CHIPS, axis=1)
    g_w2_sh = lax.dynamic_slice_in_dim(g_w2_full, chip * (GQ_W // N_CHIPS), GQ_W // N_CHIPS, axis=1)

    names = ["meta_tokens", "norm1_g", "w_in", "conv_w", "a_log", "dt_bias", "dn_norm_g", "gla_w2", "gla_b",
             "gla_norm_g", "w_out", "norm2_g", "w_up", "w_down", "final_norm_g"]
    weights = dict(meta_tokens=meta_tokens, norm1_g=norm1_g, w_in=w_in, conv_w=conv_w, a_log=a_log, dt_bias=dt_bias,
                   dn_norm_g=dn_norm_g, gla_w2=gla_w2, gla_b=gla_b, gla_norm_g=gla_norm_g, w_out=w_out,
                   norm2_g=norm2_g, w_up=w_up, w_down=w_down, final_norm_g=final_norm_g)
    ms = dict(meta_tokens=m_meta_tokens, norm1_g=m_norm1_g, w_in=m_w_in, conv_w=m_conv_w, a_log=m_a_log,
              dt_bias=m_dt_bias, dn_norm_g=m_dn_norm_g, gla_w2=m_gla_w2, gla_b=m_gla_b, gla_norm_g=m_gla_norm_g,
              w_out=m_w_out, norm2_g=m_norm2_g, w_up=m_w_up, w_down=m_w_down, final_norm_g=m_final_norm_g)
    vs = dict(meta_tokens=v_meta_tokens, norm1_g=v_norm1_g, w_in=v_w_in, conv_w=v_conv_w, a_log=v_a_log,
              dt_bias=v_dt_bias, dn_norm_g=v_dn_norm_g, gla_w2=v_gla_w2, gla_b=v_gla_b, gla_norm_g=v_gla_norm_g,
              w_out=v_w_out, norm2_g=v_norm2_g, w_up=v_w_up, w_down=v_w_down, final_norm_g=v_final_norm_g)
    grads2d = dict(meta_tokens=g_meta_sh, norm1_g=g_norm1, w_in=gw_in, conv_w=g_conv_sh, a_log=g_alog, dt_bias=g_dtb,
                   dn_norm_g=g_dn_norm, gla_w2=g_w2_sh, gla_b=g_gla_b, gla_norm_g=g_gla_norm, w_out=gw_out,
                   norm2_g=g_norm2, w_up=gw_up, w_down=gw_down, final_norm_g=g_final)
    out_g, out_d, out_m, out_v = [], [], [], []
    for nm in names:
        shape = weights[nm].shape
        g2 = grads2d[nm]
        if nm == "w_in":
            tview = lambda a: jnp.transpose(a, (2, 0, 1))
            res = _adamw_rows(tview(weights[nm]), g2[:, 0:SHARD_W].T.reshape(SHARD_W, 1, D_MODEL), tview(ms[nm]),
                              tview(vs[nm]), name=f"adamw_{nm}")
            res = [jnp.transpose(a, (1, 2, 0)) for a in res]
            gout = res[3]
        elif len(shape) == 3:
            res = _adamw(weights[nm], g2, ms[nm], vs[nm], name=f"adamw_{nm}")
            gout = g2.reshape(shape)
        else:
            as2d = lambda a: a.reshape(g2.shape)
            res = _adamw(as2d(weights[nm]), g2, as2d(ms[nm]), as2d(vs[nm]), name=f"adamw_{nm}")
            gout = g2.reshape(shape)
        out_g.append(gout)
        out_d.append(res[0].reshape(shape))
        out_m.append(res[1].reshape(shape))
        out_v.append(res[2].reshape(shape))
    return (loss, grad_x, *out_g, *out_d, *out_m, *out_v)
```

```python
import functools

import jax
import jax.numpy as jnp
import numpy as np
from jax import lax
from jax.experimental import pallas as pl
from jax.experimental.pallas import tpu as pltpu

F32 = jnp.float32
BF16 = jnp.bfloat16
HI = lax.Precision.HIGHEST
MESH = pl.DeviceIdType.MESH

D_MODEL = 1024
N_META = 16
CHUNK = 64
N_PAD = CHUNK - N_META
NH = 4
DN_D = 128
GLA_DK = 64
GLA_DV = 128
GLA_RANK = 16
D_FF = 4 * D_MODEL
EPS = 1e-6
IN_WIDTH = 3608
C_QKV, C_DZ, C_GQK, C_GV, C_GR, C_SA, C_SB, PW = 0, 1536, 2048, 2560, 3072, 3584, 3712, 3840
LANE = 128
N_CHIPS = 4

ADAM_LR, ADAM_B1, ADAM_B2, ADAM_EPS, ADAM_WD, ADAM_STEP = 0.001, 0.9, 0.999, 1e-08, 0.01, 10

VMEM_BIG = 56 * 1024 * 1024


def _cp(vmem=None, sem=None):
    kw = {}
    if vmem is not None:
        kw["vmem_limit_bytes"] = vmem
    if sem is not None:
        kw["dimension_semantics"] = sem
    return pltpu.CompilerParams(**kw)


def _tile(n, target, mult=16):
    best = None
    for t in range(mult, min(n, target) + 1, mult):
        if n % t == 0:
            best = t
    assert best is not None, (n, target)
    return best


def _dot(a, b, dims, prec=None):
    return lax.dot_general(a, b, (dims, ((), ())), preferred_element_type=F32, precision=prec)


def _nn(a, b):
    return _dot(a.astype(BF16), b.astype(BF16), ((1,), (0,)))


def _nt(a, b):
    return _dot(a.astype(BF16), b.astype(BF16), ((1,), (1,)))


def _tn(a, b):
    return _dot(a.astype(BF16), b.astype(BF16), ((0,), (0,)))


def _split(x):
    hi = x.astype(BF16)
    return hi, (x - hi.astype(F32)).astype(BF16)


def _tri_sum(tri, x):
    t = tri.astype(BF16)
    hi = x.astype(BF16)
    r1 = x - hi.astype(F32)
    mid = r1.astype(BF16)
    lo = (r1 - mid.astype(F32)).astype(BF16)
    nn = ((1,), (0,))
    return _dot(t, hi, nn) + _dot(t, mid, nn) + _dot(t, lo, nn)


def _sigmoid(x):
    return 0.5 * jnp.tanh(0.5 * x) + 0.5


def _softplus(x):
    return jnp.maximum(x, 0.0) + jnp.log(1.0 + jnp.exp(-jnp.abs(x)))


def _logsigmoid(x):
    return -_softplus(-x)


def _iota2(shape, dim):
    return lax.broadcasted_iota(jnp.int32, shape, dim)


def _mm(a, b, mode, *, tm, tn, tk, out_dtypes, extras=(), epilogue=None, name, vmem=VMEM_BIG, rider=None,
        out_widths=None, n_chunk=None):
    if mode == "tn":
        K, M = a.shape
    else:
        M, K = a.shape
    N = b.shape[0] if mode == "nt" else b.shape[1]
    assert M % tm == 0 and N % tn == 0 and K % tk == 0, (name, M, N, K, tm, tn, tk)
    nk = K // tk
    n_ex, n_out = len(extras), len(out_dtypes)
    if mode == "tn":
        a_spec = pl.BlockSpec((tk, tm), lambda i, j, k: (k, i))
    else:
        a_spec = pl.BlockSpec((tm, tk), lambda i, j, k: (i, k))
    if mode == "nt":
        b_spec = pl.BlockSpec((tn, tk), lambda i, j, k: (j, k))
    else:
        b_spec = pl.BlockSpec((tk, tn), lambda i, j, k: (k, j))
    mn_spec = pl.BlockSpec((tm, tn), lambda i, j, k: (i, j))
    if out_widths is None:
        o_specs = [mn_spec] * n_out
        o_shapes = [jax.ShapeDtypeStruct((M, N), dt) for dt in out_dtypes]
    else:
        assert tn == N
        o_specs = [pl.BlockSpec((tm, wd), lambda i, j, k: (i, 0)) for wd in out_widths]
        o_shapes = [jax.ShapeDtypeStruct((M, wd), dt) for wd, dt in zip(out_widths, out_dtypes)]
    dims = {"nn": ((1,), (0,)), "nt": ((1,), (1,)), "tn": ((0,), (0,))}[mode]

    single = nk == 1
    direct = (not single) and epilogue is None and n_out == 1 and out_dtypes[0] == F32

    def body(*refs):
        a_ref, b_ref = refs[0], refs[1]
        ex_refs = refs[2:2 + n_ex]
        out_refs = refs[2 + n_ex:2 + n_ex + n_out]
        if n_chunk is not None:
            assert single and out_widths is None and mode != "tn" and tn % n_chunk == 0
            av = a_ref[...].astype(BF16)
            for j in range(tn // n_chunk):
                cols = slice(j * n_chunk, (j + 1) * n_chunk)
                bv = b_ref[cols, :] if mode == "nt" else b_ref[:, cols]
                acc = _dot(av, bv.astype(BF16), dims)
                res = (acc,) if epilogue is None else epilogue(acc, *[e[:, cols] for e in ex_refs])
                for o_ref, r in zip(out_refs, res):
                    o_ref[:, cols] = r.astype(o_ref.dtype)
            return
        part = _dot(a_ref[...].astype(BF16), b_ref[...].astype(BF16), dims)

        def finish(acc):
            res = (acc,) if epilogue is None else epilogue(acc, *[e[...] for e in ex_refs])
            for o_ref, r in zip(out_refs, res):
                o_ref[...] = r.astype(o_ref.dtype)

        if single:
            finish(part)
            return
        acc_ref = out_refs[0] if direct else refs[2 + n_ex + n_out]
        k = pl.program_id(2)

        @pl.when(k == 0)
        def _():
            acc_ref[...] = part

        @pl.when(k > 0)
        def _():
            acc_ref[...] += part

        if not direct:
            @pl.when(k == nk - 1)
            def _():
                finish(acc_ref[...])

    outs, ridden = _hosted_call(
        body, rider, name=name, grid=(M // tm, N // tn, nk),
        in_specs=[a_spec, b_spec] + [mn_spec] * n_ex,
        out_specs=o_specs, out_shape=o_shapes,
        scratch_shapes=[] if (single or direct) else [pltpu.VMEM((tm, tn), F32)],
        compiler_params=_cp(vmem, ("parallel", "parallel", "arbitrary")), args=(a, b, *extras))
    return tuple(outs) if rider is None else (tuple(outs), ridden)


def _grad_tn(a, secs, *, tk, name):
    kk, m = a.shape
    widths = [s.shape[1] for s in secs]
    total = sum(widths)
    nk = kk // tk

    def body(*refs):
        a_ref, sec_refs, o_ref = refs[0], refs[1:-1], refs[-1]
        cat = sec_refs[0][...] if len(sec_refs) == 1 else jnp.concatenate([s[...] for s in sec_refs], axis=1)
        part = _dot(a_ref[...].astype(BF16), cat.astype(BF16), ((0,), (0,)))
        k = pl.program_id(0)

        @pl.when(k == 0)
        def _():
            o_ref[...] = part

        @pl.when(k > 0)
        def _():
            o_ref[...] += part

    return pl.pallas_call(
        body, name=name, grid=(nk,),
        in_specs=[pl.BlockSpec((tk, m), lambda k: (k, 0))] + [pl.BlockSpec((tk, w), lambda k: (k, 0)) for w in widths],
        out_specs=pl.BlockSpec((m, total), lambda k: (0, 0)),
        out_shape=jax.ShapeDtypeStruct((m, total), F32),
        compiler_params=_cp(VMEM_BIG, ("arbitrary",)),
    )(a, *secs)


def _rms_fwd(x, g, *, tr, name):
    n, d = x.shape

    def body(x_ref, g_ref, o_ref):
        xv = x_ref[...]
        r = lax.rsqrt(jnp.mean(xv * xv, axis=-1, keepdims=True) + EPS)
        o_ref[...] = (xv * r * g_ref[...]).astype(o_ref.dtype)

    return pl.pallas_call(
        body, name=name, grid=(n // tr,),
        in_specs=[pl.BlockSpec((tr, d), lambda i: (i, 0)), pl.BlockSpec((1, d), lambda i: (0, 0))],
        out_specs=pl.BlockSpec((tr, d), lambda i: (i, 0)),
        out_shape=jax.ShapeDtypeStruct((n, d), BF16),
        compiler_params=_cp(VMEM_BIG),
    )(x, g)


def _rms_bwd_math(xv, g, dy):
    r = lax.rsqrt(jnp.mean(xv * xv, axis=-1, keepdims=True) + EPS)
    xh = xv * r
    gdy = dy * g
    dx = r * (gdy - xh * jnp.mean(xh * gdy, axis=-1, keepdims=True))
    return dx, jnp.sum(dy * xh, axis=0, keepdims=True)


def _mlp_up_bwd_norm(dup, w_up, x, g, res, *, tr, name, rider=None):
    n, d = x.shape
    ff = dup.shape[1]

    def body(dup_ref, w_ref, x_ref, g_ref, res_ref, o_ref, ob_ref, dg_ref):
        dh = _nt(dup_ref[...], w_ref[...])
        dx, dg = _rms_bwd_math(x_ref[...], g_ref[...], dh)
        tot = res_ref[...] + dx
        o_ref[...] = tot
        ob_ref[...] = tot.astype(BF16)

        @pl.when(pl.program_id(0) == 0)
        def _():
            dg_ref[...] = dg

        @pl.when(pl.program_id(0) > 0)
        def _():
            dg_ref[...] += dg

    row = pl.BlockSpec((tr, d), lambda i: (i, 0))
    vec = pl.BlockSpec((1, d), lambda i: (0, 0))
    outs, ridden = _hosted_call(
        body, rider, name=name, grid=(n // tr,),
        in_specs=[pl.BlockSpec((tr, ff), lambda i: (i, 0)), pl.BlockSpec((d, ff), lambda i: (0, 0)), row, vec, row],
        out_specs=[row, row, vec],
        out_shape=[jax.ShapeDtypeStruct((n, d), F32), jax.ShapeDtypeStruct((n, d), BF16),
                   jax.ShapeDtypeStruct((1, d), F32)],
        scratch_shapes=[], compiler_params=_cp(VMEM_BIG, ("arbitrary",)), args=(dup, w_up, x, g, res))
    return (*outs, ridden)


def _mlp_down_loss(act, w_down, x1, gf, tgt, *, t_seq, tr, name):
    n, d = x1.shape
    ff = act.shape[1]
    per_seq = t_seq // tr

    def body(a_ref, w_ref, x_ref, g_ref, t_ref, dx_ref, dxb_ref, dg_ref, loss_ref):
        i = pl.program_id(0)
        xv = x_ref[...] + _nn(a_ref[...], w_ref[...])
        g = g_ref[...]
        r = lax.rsqrt(jnp.mean(xv * xv, axis=-1, keepdims=True) + EPS)
        xh = xv * r
        pos = (i % per_seq) * tr + _iota2((tr, 1), 0)
        real = pos >= CHUNK
        err = jnp.where(real, xh * g - t_ref[...], 0.0)
        dy = err * (1.0 / d)
        gdy = dy * g
        dx = r * (gdy - xh * jnp.mean(xh * gdy, axis=-1, keepdims=True))
        dx_ref[...] = dx
        dxb_ref[...] = dx.astype(BF16)
        dg = jnp.sum(dy * xh, axis=0, keepdims=True)
        ls = 0.5 * jnp.sum(jnp.mean(err * err, axis=-1, keepdims=True), axis=0, keepdims=True)
        ls = jnp.where(_iota2((1, LANE), 1) == 0, ls, 0.0)

        @pl.when(i == 0)
        def _():
            dg_ref[...] = dg
            loss_ref[...] = ls

        @pl.when(i > 0)
        def _():
            dg_ref[...] += dg
            loss_ref[...] += ls

    row = pl.BlockSpec((tr, d), lambda i: (i, 0))
    vec = pl.BlockSpec((1, d), lambda i: (0, 0))
    one = pl.BlockSpec((1, LANE), lambda i: (0, 0))
    return pl.pallas_call(
        body, name=name, grid=(n // tr,),
        in_specs=[pl.BlockSpec((tr, ff), lambda i: (i, 0)), pl.BlockSpec((ff, d), lambda i: (0, 0)), row, vec, row],
        out_specs=[row, row, vec, one],
        out_shape=[jax.ShapeDtypeStruct((n, d), F32), jax.ShapeDtypeStruct((n, d), BF16),
                   jax.ShapeDtypeStruct((1, d), F32), jax.ShapeDtypeStruct((1, LANE), F32)],
        compiler_params=_cp(VMEM_BIG, ("arbitrary",)),
    )(act, w_down, x1, gf, tgt)


def _gnorm_fwd(o_dn, o_gla, projp, g_dn, g_gla, *, tr, name):
    n = o_dn.shape[0]
    w = NH * DN_D

    def body(odn_ref, ogl_ref, z_ref, r_ref, gdn_ref, ggl_ref, mix_ref):
        for grp, (o_ref, gate_ref, gain_ref) in enumerate(((odn_ref, z_ref, gdn_ref), (ogl_ref, r_ref, ggl_ref))):
            gain = gain_ref[...]
            for h in range(NH):
                sl = slice(h * DN_D, (h + 1) * DN_D)
                o = o_ref[:, sl].astype(F32)
                z = gate_ref[:, sl].astype(F32)
                r = lax.rsqrt(jnp.mean(o * o, axis=-1, keepdims=True) + EPS)
                y = (o * r * gain) * (z * _sigmoid(z))
                mix_ref[:, grp * w + h * DN_D: grp * w + (h + 1) * DN_D] = y.astype(mix_ref.dtype)

    row = pl.BlockSpec((tr, w), lambda i: (i, 0))
    vec = pl.BlockSpec((1, DN_D), lambda i: (0, 0))
    return pl.pallas_call(
        body, name=name, grid=(n // tr,),
        in_specs=[row, row, pl.BlockSpec((tr, w), lambda i: (i, C_DZ // w)),
                  pl.BlockSpec((tr, w), lambda i: (i, C_GR // w)), vec, vec],
        out_specs=pl.BlockSpec((tr, 2 * w), lambda i: (i, 0)),
        out_shape=jax.ShapeDtypeStruct((n, 2 * w), BF16),
        compiler_params=_cp(VMEM_BIG),
    )(o_dn, o_gla, projp, projp, g_dn, g_gla)


def _gnorm_bwd(dmix, o_dn, o_gla, projp, g_dn, g_gla, *, tr, name):
    n = o_dn.shape[0]
    w = NH * DN_D

    def body(dm_ref, odn_ref, ogl_ref, z_ref, r_ref, gdn_ref, ggl_ref,
             dodn_ref, ddz_ref, dogl_ref, dgr_ref, dgdn_ref, dggl_ref):
        first = pl.program_id(0) == 0
        groups = ((odn_ref, z_ref, gdn_ref, dodn_ref, ddz_ref, dgdn_ref),
                  (ogl_ref, r_ref, ggl_ref, dogl_ref, dgr_ref, dggl_ref))
        for grp, (o_ref, gate_ref, gain_ref, do_ref, dgate_ref, dgain_ref) in enumerate(groups):
            gain = gain_ref[...]
            dgain = jnp.zeros((1, DN_D), F32)
            for h in range(NH):
                sl = slice(h * DN_D, (h + 1) * DN_D)
                o = o_ref[:, sl].astype(F32)
                z = gate_ref[:, sl].astype(F32)
                dm = dm_ref[:, grp * w + h * DN_D: grp * w + (h + 1) * DN_D].astype(F32)
                r = lax.rsqrt(jnp.mean(o * o, axis=-1, keepdims=True) + EPS)
                oh = o * r
                s = _sigmoid(z)
                dn = dm * (z * s)
                dgate_ref[:, sl] = (dm * (oh * gain) * (s * (1.0 + z * (1.0 - s)))).astype(dgate_ref.dtype)
                gdn = dn * gain
                do_ref[:, sl] = (r * (gdn - oh * jnp.mean(oh * gdn, axis=-1, keepdims=True))).astype(do_ref.dtype)
                dgain = dgain + jnp.sum(dn * oh, axis=0, keepdims=True)

            @pl.when(first)
            def _():
                dgain_ref[...] = dgain

            @pl.when(jnp.logical_not(first))
            def _():
                dgain_ref[...] += dgain

    row = pl.BlockSpec((tr, w), lambda i: (i, 0))
    vec = pl.BlockSpec((1, DN_D), lambda i: (0, 0))
    big = jax.ShapeDtypeStruct((n, w), F32)
    gate = jax.ShapeDtypeStruct((n, w), BF16)
    small = jax.ShapeDtypeStruct((1, DN_D), F32)
    return pl.pallas_call(
        body, name=name, grid=(n // tr,),
        in_specs=[pl.BlockSpec((tr, 2 * w), lambda i: (i, 0)), row, row,
                  pl.BlockSpec((tr, w), lambda i: (i, C_DZ // w)), pl.BlockSpec((tr, w), lambda i: (i, C_GR // w)), vec, vec],
        out_specs=[row, row, row, row, vec, vec],
        out_shape=[gate, gate, gate, gate, small, small],
        compiler_params=_cp(VMEM_BIG),
    )(dmix, o_dn, o_gla, projp, projp, g_dn, g_gla)


QKV_W = 3 * NH * DN_D
HALO = 8


def _conv_z(xs_ref, cw_ref, tt):
    z = cw_ref[0:1, :] * xs_ref[pl.ds(HALO - 3, tt), :]
    for j in range(1, 4):
        z = z + cw_ref[j:j + 1, :] * xs_ref[pl.ds(HALO - 3 + j, tt), :]
    return z


def _dnprep_fwd(projp, conv_w, *, bsz, t_seq, tt, name):
    n = bsz * t_seq
    per_seq = t_seq // tt
    hw = NH * DN_D

    def body(x_ref, halo_ref, cw_ref, q_ref, k_ref, v_ref, xs_ref):
        i = pl.program_id(1)
        xs_ref[0:HALO, :] = jnp.where(i == 0, 0.0, halo_ref[...].astype(F32))
        xs_ref[HALO:HALO + tt, :] = x_ref[...].astype(F32)
        z = _conv_z(xs_ref, cw_ref, tt)
        a = z * _sigmoid(z)
        for grp, o_ref in enumerate((q_ref, k_ref)):
            for h in range(NH):
                ah = a[:, grp * hw + h * DN_D: grp * hw + (h + 1) * DN_D]
                rs = lax.rsqrt(jnp.sum(ah * ah, axis=-1, keepdims=True) + EPS)
                o_ref[:, h * DN_D:(h + 1) * DN_D] = (ah * rs).astype(o_ref.dtype)
        v_ref[...] = a[:, 2 * hw:3 * hw].astype(v_ref.dtype)

    def halo_map(b, i):
        return (jnp.maximum((b * t_seq + i * tt) // HALO - 1, 0), 0)

    out = pl.BlockSpec((tt, hw), lambda b, i: (b * per_seq + i, 0))
    sds = jax.ShapeDtypeStruct((n, hw), BF16)
    return pl.pallas_call(
        body, name=name, grid=(bsz, per_seq),
        in_specs=[pl.BlockSpec((tt, QKV_W), lambda b, i: (b * per_seq + i, 0)),
                  pl.BlockSpec((HALO, QKV_W), halo_map),
                  pl.BlockSpec((4, QKV_W), lambda b, i: (0, 0))],
        out_specs=[out, out, out], out_shape=[sds, sds, sds],
        scratch_shapes=[pltpu.VMEM((tt + HALO, QKV_W), F32)],
        compiler_params=_cp(VMEM_BIG),
    )(projp, projp, conv_w)


def _dnprep_bwd_a(projp, conv_w, dq, dk, dv, *, bsz, t_seq, tt, name):
    n = bsz * t_seq
    per_seq = t_seq // tt
    hw = NH * DN_D

    def body(x_ref, halo_ref, cw_ref, dq_ref, dk_ref, dv_ref, dz_ref, dcw_ref, xs_ref):
        b, i = pl.program_id(0), pl.program_id(1)
        xs_ref[0:HALO, :] = jnp.where(i == 0, 0.0, halo_ref[...].astype(F32))
        xs_ref[HALO:HALO + tt, :] = x_ref[...].astype(F32)
        z = _conv_z(xs_ref, cw_ref, tt)
        s = _sigmoid(z)
        a = z * s
        dsilu = s * (1.0 + z * (1.0 - s))
        for grp, d_ref in enumerate((dq_ref, dk_ref)):
            for h in range(NH):
                sl = slice(grp * hw + h * DN_D, grp * hw + (h + 1) * DN_D)
                ah = a[:, sl]
                rs = lax.rsqrt(jnp.sum(ah * ah, axis=-1, keepdims=True) + EPS)
                y = ah * rs
                dy = d_ref[:, h * DN_D:(h + 1) * DN_D]
                da = rs * (dy - y * jnp.sum(dy * y, axis=-1, keepdims=True))
                dz_ref[:, sl] = da * dsilu[:, sl]
        dz_ref[:, 2 * hw:3 * hw] = dv_ref[...] * dsilu[:, 2 * hw:3 * hw]
        dz = dz_ref[...]
        first = jnp.logical_and(b == 0, i == 0)
        for j in range(4):
            part = jnp.sum(dz * xs_ref[pl.ds(HALO - 3 + j, tt), :], axis=0, keepdims=True)

            @pl.when(first)
            def _():
                dcw_ref[j:j + 1, :] = part

            @pl.when(jnp.logical_not(first))
            def _():
                dcw_ref[j:j + 1, :] += part

    def halo_map(b, i):
        return (jnp.maximum((b * t_seq + i * tt) // HALO - 1, 0), 0)

    hrow = pl.BlockSpec((tt, hw), lambda b, i: (b * per_seq + i, 0))
    return pl.pallas_call(
        body, name=name, grid=(bsz, per_seq),
        in_specs=[pl.BlockSpec((tt, QKV_W), lambda b, i: (b * per_seq + i, 0)),
                  pl.BlockSpec((HALO, QKV_W), halo_map),
                  pl.BlockSpec((4, QKV_W), lambda b, i: (0, 0)), hrow, hrow, hrow],
        out_specs=[pl.BlockSpec((tt, QKV_W), lambda b, i: (b * per_seq + i, 0)),
                   pl.BlockSpec((4, QKV_W), lambda b, i: (0, 0))],
        out_shape=[jax.ShapeDtypeStruct((n, QKV_W), F32), jax.ShapeDtypeStruct((4, QKV_W), F32)],
        scratch_shapes=[pltpu.VMEM((tt + HALO, QKV_W), F32)],
        compiler_params=_cp(VMEM_BIG),
    )(projp, projp, conv_w, dq, dk, dv)


def _dnprep_bwd_b(dz, conv_w, *, bsz, t_seq, tt, name):
    n = bsz * t_seq
    per_seq = t_seq // tt
    last_blk = n // HALO - 1

    def body(dz_ref, halo_ref, cw_ref, dx_ref, ds_ref):
        i = pl.program_id(1)
        ds_ref[0:tt, :] = dz_ref[...].astype(F32)
        ds_ref[tt:tt + HALO, :] = jnp.where(i == per_seq - 1, 0.0, halo_ref[...].astype(F32))
        dx = cw_ref[0:1, :] * ds_ref[pl.ds(3, tt), :]
        for j in range(1, 4):
            dx = dx + cw_ref[j:j + 1, :] * ds_ref[pl.ds(3 - j, tt), :]
        dx_ref[...] = dx.astype(dx_ref.dtype)

    def halo_map(b, i):
        return (jnp.minimum((b * t_seq + (i + 1) * tt) // HALO, last_blk), 0)

    row = pl.BlockSpec((tt, QKV_W), lambda b, i: (b * per_seq + i, 0))
    return pl.pallas_call(
        body, name=name, grid=(bsz, per_seq),
        in_specs=[row, pl.BlockSpec((HALO, QKV_W), halo_map), pl.BlockSpec((4, QKV_W), lambda b, i: (0, 0))],
        out_specs=row, out_shape=jax.ShapeDtypeStruct((n, QKV_W), BF16),
        scratch_shapes=[pltpu.VMEM((tt + HALO, QKV_W), F32)],
        compiler_params=_cp(VMEM_BIG),
    )(dz, dz, conv_w)


def _masks64():
    r = _iota2((CHUNK, CHUNK), 0)
    c = _iota2((CHUNK, CHUNK), 1)
    return r, c


def _group(nc_seq, target=5):
    return max(g for g in range(1, target + 1) if nc_seq % g == 0)


def _round_robin(chains):
    live = list(chains)
    while live:
        nxt = []
        for ch in live:
            try:
                next(ch)
                nxt.append(ch)
            except StopIteration:
                pass
        live = nxt
        yield


def _run(chains):
    for _ in _round_robin(chains):
        pass


def _per_chunk(inner, kinds, grp):
    def body(*refs):
        chains = []
        for gi in range(grp):
            views = []
            for r, kind in zip(refs, kinds):
                if kind == "row":
                    views.append(r.at[pl.ds(gi * CHUNK, CHUNK)])
                elif kind == "lead":
                    views.append(r.at[pl.ds(gi, 1)])
                else:
                    views.append(r)
            chains.append(inner(gi, *views))
        _run(chains)
    return body


def _accumulate(ref, val, gi):
    if gi > 0:
        ref[...] += val
        return
    first = pl.program_id(0) == 0

    @pl.when(first)
    def _():
        ref[...] = val

    @pl.when(jnp.logical_not(first))
    def _():
        ref[...] += val


ANY = pl.BlockSpec(memory_space=pl.ANY)


def _place():
    return lax.axis_index("x"), lax.axis_index("y"), lax.axis_index("c")


def _other_chips(x, y):
    return [(1 - x, y, 2 * (1 - x) + y), (x, 1 - y, 2 * x + 1 - y), (1 - x, 1 - y, 2 * (1 - x) + 1 - y)]


class _GatherRider:
    def __init__(self, bufs, split):
        self.inputs = list(bufs)
        self.split = list(split)
        self.out_shapes = [jax.ShapeDtypeStruct(b.shape, b.dtype) for b in bufs]
        self.aliases = {i: i for i in range(len(bufs))}
        self.sems = [pltpu.SemaphoreType.DMA((len(bufs), 3))] * 4

    def _rows(self, k, buf, c, mine=True):
        r = buf.shape[1]
        if not self.split[k]:
            return pl.ds(0, r)
        return pl.ds((c if mine else 1 - c) * (r // 2), r // 2)

    def _ici(self, k, d, bufs, sems, c, px, py, block):
        rows = self._rows(k, bufs[k], c)
        return pltpu.make_async_remote_copy(
            src_ref=bufs[k].at[block, rows, :], dst_ref=bufs[k].at[block, rows, :], send_sem=sems[0].at[k, d],
            recv_sem=sems[1].at[k, d], device_id=(px, py, c), device_id_type=MESH)

    def _pass(self, k, d, bufs, sems, x, y, c, block, mine):
        rows = self._rows(k, bufs[k], c, mine)
        return pltpu.make_async_remote_copy(
            src_ref=bufs[k].at[block, rows, :], dst_ref=bufs[k].at[block, rows, :], send_sem=sems[2].at[k, d],
            recv_sem=sems[3].at[k, d], device_id=(x, y, 1 - c), device_id_type=MESH)

    def first(self, in_refs, bufs, sems):
        x, y, c = _place()
        for k in range(len(bufs)):
            for d, (px, py, _) in enumerate(_other_chips(x, y)):
                self._ici(k, d, bufs, sems, c, px, py, 2 * x + y).start()

    def last(self, in_refs, bufs, sems):
        x, y, c = _place()
        chips = _other_chips(x, y)
        for k in range(len(bufs)):
            for d, (px, py, pj) in enumerate(chips):
                self._ici(k, d, bufs, sems, c, px, py, pj).wait_recv()
                if self.split[k]:
                    self._pass(k, d, bufs, sems, x, y, c, pj, True).start()
        for k in range(len(bufs)):
            for d, (px, py, pj) in enumerate(chips):
                if self.split[k]:
                    self._pass(k, d, bufs, sems, x, y, c, pj, False).wait_recv()
                    self._pass(k, d, bufs, sems, x, y, c, pj, True).wait_send()
                self._ici(k, d, bufs, sems, c, px, py, 2 * x + y).wait_send()


def _hosted_call(body, rider, *, name, grid, in_specs, out_specs, out_shape, scratch_shapes, compiler_params, args):
    if rider is None:
        outs = pl.pallas_call(body, name=name, grid=grid, in_specs=in_specs, out_specs=out_specs, out_shape=out_shape,
                              scratch_shapes=scratch_shapes, compiler_params=compiler_params)(*args)
        return list(outs), []
    n_in, n_out, n_scr = len(in_specs), len(out_specs), len(scratch_shapes)
    r_in, r_out = len(rider.inputs), len(rider.out_shapes)
    compiler_params = _cp(compiler_params.vmem_limit_bytes, ("arbitrary",) * len(grid))

    def full_body(*refs):
        ins = refs[:n_in]
        rins = refs[n_in:n_in + r_in]
        outs = refs[n_in + r_in:n_in + r_in + n_out]
        routs = refs[n_in + r_in + n_out:n_in + r_in + n_out + r_out]
        rest = refs[n_in + r_in + n_out + r_out:]
        scr, sems = rest[:n_scr], rest[n_scr:]
        ids = [pl.program_id(a) for a in range(len(grid))]
        is_first = functools.reduce(jnp.logical_and, [i == 0 for i in ids])
        is_last = functools.reduce(jnp.logical_and, [i == g - 1 for i, g in zip(ids, grid)])

        @pl.when(is_first)
        def _():
            rider.first(rins, routs, sems)

        body(*ins, *outs, *scr)

        @pl.when(is_last)
        def _():
            rider.last(rins, routs, sems)

    res = pl.pallas_call(
        full_body, name=name, grid=grid, in_specs=list(in_specs) + [ANY] * r_in,
        out_specs=list(out_specs) + [ANY] * r_out, out_shape=list(out_shape) + list(rider.out_shapes),
        input_output_aliases={n_in + i: n_out + o for i, o in rider.aliases.items()},
        scratch_shapes=list(scratch_shapes) + list(rider.sems), compiler_params=compiler_params,
    )(*args, *rider.inputs)
    return list(res[:n_out]), list(res[n_out:])


def _exchange_now(rider, *, name):
    r_in = len(rider.inputs)

    def body(*refs):
        rins = refs[:r_in]
        routs = refs[r_in:r_in + len(rider.out_shapes)]
        sems = refs[r_in + len(rider.out_shapes):]
        rider.first(rins, routs, sems)
        rider.last(rins, routs, sems)

    return pl.pallas_call(
        body, name=name, in_specs=[ANY] * r_in, out_specs=[ANY] * len(rider.out_shapes), out_shape=list(rider.out_shapes),
        input_output_aliases=dict(rider.aliases), scratch_shapes=list(rider.sems),
    )(*rider.inputs)


def _to_slot(where, a, dtype, *, name):
    r, cols = a.shape
    tr = _tile(r, 256, 16) if r > 256 else r

    def body(w_ref, a_ref, o_ref):
        o_ref[0] = a_ref[...].astype(o_ref.dtype)

    return pl.pallas_call(
        body, name=name,
        grid_spec=pltpu.PrefetchScalarGridSpec(
            num_scalar_prefetch=1, grid=(r // tr,),
            in_specs=[pl.BlockSpec((tr, cols), lambda i, w: (i, 0))],
            out_specs=pl.BlockSpec((1, tr, cols), lambda i, w: (w[1], i, 0))),
        out_shape=jax.ShapeDtypeStruct((N_CHIPS, r, cols), dtype), compiler_params=_cp(VMEM_BIG),
    )(where, a)


def _tri_inv(a_strict):
    r, c = _masks64()
    eye = (r == c).astype(F32)
    blk16 = (r // 16) == (c // 16)
    blk32 = (r // 32) == (c // 32)
    ld = jnp.where(blk16, a_strict, 0.0)
    x = eye - ld
    p = _nn(ld, ld)
    yield
    for step in range(3):
        xp = _nn(x, p)
        if step < 2:
            p = _nn(p, p)
        x = x + xp
        yield
    for lk in (jnp.where(jnp.logical_and(blk32, jnp.logical_not(blk16)), a_strict, 0.0),
               jnp.where(blk32, 0.0, a_strict)):
        y = x - eye
        s = lk + _nn(y, lk)
        yield
        x = x - s - _nn(s, y)
        yield
    return x


def _dn_gates(sa, alog, dtb, chunk_in_seq):
    rows = _iota2((CHUNK, LANE), 0)
    valid = jnp.logical_or(rows >= N_PAD, chunk_in_seq > 0)
    beta_t = _sigmoid(sa)
    ea = jnp.exp(alog)
    g_t = jnp.where(valid, -ea * _softplus(sa + dtb), 0.0)
    r, c = _masks64()
    ltri = (r >= c).astype(F32)
    gam_t = _tri_sum(ltri, g_t)
    return beta_t, g_t, gam_t, valid, ea


def _dn_intra_fwd(qn, kn, v, projp, alog_row, dtb_row, *, nc_seq, name, rider=None):
    n = qn.shape[0]
    nct = n // CHUNK
    hw = NH * DN_D
    scale = DN_D ** -0.5

    grp = _group(nc_seq)

    def inner(gi, q_ref, k_ref, v_ref, sa_ref, al_ref, dt_ref, u_ref, w_ref, qg_ref, kd_ref, p_ref, t_ref, gl_ref):
        ci = (pl.program_id(0) * grp + gi) % nc_seq
        beta_t, _, gam_t, _, _ = _dn_gates(sa_ref[...], al_ref[...], dt_ref[...], ci)
        yield
        gam_tt = gam_t.T
        r, c = _masks64()
        incl = r >= c
        strict = r > c

        def head(h):
            sl = slice(h * DN_D, (h + 1) * DN_D)
            beta_w = jnp.broadcast_to(beta_t[:, h:h + 1], (CHUNK, DN_D))
            gam_w = jnp.broadcast_to(gam_t[:, 4 + h:5 + h], (CHUNK, DN_D))
            gam_row = gam_tt[4 + h:5 + h, :]
            gl = gam_t[CHUNK - 1:CHUNK, 4 + h:5 + h]
            dec = jnp.exp(jnp.where(incl, gam_w[:, 0:CHUNK] - gam_row, -jnp.inf))
            kh = k_ref[:, sl].astype(F32)
            qh = q_ref[:, sl].astype(F32) * scale
            vh = v_ref[:, sl].astype(F32)
            kk = _nt(kh, kh)
            qk = _nt(qh, kh)
            yield
            a = jnp.where(strict, beta_w[:, 0:CHUNK] * kk * dec, 0.0)
            tm = yield from _tri_inv(a)
            egam_w = jnp.exp(gam_w)
            u_ref[:, sl] = _nn(tm, beta_w * vh).astype(u_ref.dtype)
            w_ref[:, sl] = _nn(tm, (beta_w * egam_w) * kh).astype(w_ref.dtype)
            qg_ref[:, sl] = (egam_w * qh).astype(qg_ref.dtype)
            kd_ref[:, sl] = (jnp.exp(gl - gam_w) * kh).astype(kd_ref.dtype)
            p_ref[0, h] = qk * dec
            t_ref[0, h] = tm
            gl_ref[0, h:h + 1, :] = jnp.broadcast_to(jnp.exp(gl), (1, LANE))

        yield from _round_robin([head(h) for h in range(NH)])

    rows = grp * CHUNK
    row = pl.BlockSpec((rows, hw), lambda i: (i, 0))
    vec = pl.BlockSpec((1, LANE), lambda i: (0, 0))
    mat = pl.BlockSpec((grp, NH, CHUNK, CHUNK), lambda i: (i, 0, 0, 0))
    big = jax.ShapeDtypeStruct((n, hw), BF16)
    msd = jax.ShapeDtypeStruct((nct, NH, CHUNK, CHUNK), F32)
    kinds = ["row"] * 4 + ["whole"] * 2 + ["row"] * 4 + ["lead"] * 3
    outs, ridden = _hosted_call(
        _per_chunk(inner, kinds, grp), rider, name=name, grid=(nct // grp,),
        in_specs=[row, row, row, pl.BlockSpec((rows, LANE), lambda i: (i, 0)), vec, vec],
        out_specs=[row, row, row, row, mat, mat, pl.BlockSpec((grp, NH, LANE), lambda i: (i, 0, 0))],
        out_shape=[big, big, big, big, msd, msd, jax.ShapeDtypeStruct((nct, NH, LANE), F32)],
        scratch_shapes=[], compiler_params=_cp(VMEM_BIG, ("arbitrary",)),
        args=(qn, kn, v, projp, alog_row, dtb_row))
    return (*outs, ridden)


def _dn_scan_fwd(u, w, qg, kd, p, gl, *, bsz, nc_seq, name, rider=None):
    hw = NH * DN_D
    t_seq = nc_seq * CHUNK
    u, w, qg, kd = (z.reshape(bsz, t_seq, hw) for z in (u, w, qg, kd))
    p = p.reshape(bsz, nc_seq, NH, CHUNK, CHUNK)
    gl = gl.reshape(bsz, nc_seq, NH, LANE)
    grp = _group(nc_seq)

    def body(u_ref, w_ref, qg_ref, kd_ref, p_ref, gl_ref, o_ref, vn_ref, hist_ref, s_ref):
        @pl.when(pl.program_id(0) == 0)
        def _():
            s_ref[...] = jnp.zeros_like(s_ref)

        def chain(b, h, gi):
            sl = slice(h * DN_D, (h + 1) * DN_D)
            rows = pl.ds(gi * CHUNK, CHUNK)
            s = s_ref[b, h]
            hist_ref[b, gi, h] = s.astype(hist_ref.dtype)
            ws = _nn(w_ref[b, rows, sl], s)
            qs = _nn(qg_ref[b, rows, sl], s)
            yield
            vn = u_ref[b, rows, sl] - ws
            vn_ref[b, rows, sl] = vn.astype(vn_ref.dtype)
            o_ref[b, rows, sl] = (qs + _nn(p_ref[b, gi, h], vn)).astype(o_ref.dtype)
            s_ref[b, h] = gl_ref[b, gi, h:h + 1, :] * s + _tn(kd_ref[b, rows, sl], vn)

        for gi in range(grp):
            _run([chain(b, h, gi) for b in range(bsz) for h in range(NH)])

    row = pl.BlockSpec((bsz, grp * CHUNK, hw), lambda i: (0, i, 0))
    outs, ridden = _hosted_call(
        body, rider, name=name, grid=(nc_seq // grp,),
        in_specs=[row, row, row, row, pl.BlockSpec((bsz, grp, NH, CHUNK, CHUNK), lambda i: (0, i, 0, 0, 0)),
                  pl.BlockSpec((bsz, grp, NH, LANE), lambda i: (0, i, 0, 0))],
        out_specs=[row, row, pl.BlockSpec((bsz, grp, NH, DN_D, DN_D), lambda i: (0, i, 0, 0, 0))],
        out_shape=[jax.ShapeDtypeStruct((bsz, t_seq, hw), BF16), jax.ShapeDtypeStruct((bsz, t_seq, hw), BF16),
                   jax.ShapeDtypeStruct((bsz, nc_seq, NH, DN_D, DN_D), F32)],
        scratch_shapes=[pltpu.VMEM((bsz, NH, DN_D, DN_D), F32)],
        compiler_params=_cp(VMEM_BIG, ("arbitrary",)), args=(u, w, qg, kd, p, gl))
    o, vn, hist = outs
    return o.reshape(bsz * t_seq, hw), vn.reshape(bsz * t_seq, hw), hist, ridden


def _dn_scan_bwd(do, w, qg, kd, vn, p, gl, hist, *, bsz, nc_seq, name):
    hw = NH * DN_D
    t_seq = nc_seq * CHUNK
    do, w, qg, kd, vn = (z.reshape(bsz, t_seq, hw) for z in (do, w, qg, kd, vn))
    p = p.reshape(bsz, nc_seq, NH, CHUNK, CHUNK)
    gl = gl.reshape(bsz, nc_seq, NH, LANE)
    grp = _group(nc_seq)

    def body(do_ref, w_ref, qg_ref, kd_ref, vn_ref, p_ref, gl_ref, hist_ref,
             du_ref, dw_ref, dqg_ref, dkd_ref, dgl_ref, ds_ref):
        @pl.when(pl.program_id(0) == 0)
        def _():
            ds_ref[...] = jnp.zeros_like(ds_ref)

        def chain(b, h, gi):
            sl = slice(h * DN_D, (h + 1) * DN_D)
            rows = pl.ds(gi * CHUNK, CHUNK)
            s = hist_ref[b, gi, h]
            dsn = ds_ref[b, h]
            doh = do_ref[b, rows, sl]
            vnh = vn_ref[b, rows, sl]
            kdh = kd_ref[b, rows, sl]
            dvn = _tn(p_ref[b, gi, h], doh) + _nn(kdh, dsn)
            du_ref[b, rows, sl] = dvn.astype(du_ref.dtype)
            dqg_ref[b, rows, sl] = _nt(doh, s).astype(dqg_ref.dtype)
            dkd_ref[b, rows, sl] = _nt(vnh, dsn).astype(dkd_ref.dtype)
            ds_part = _tn(qg_ref[b, rows, sl], doh) + gl_ref[b, gi, h:h + 1, :] * dsn
            dgl = jnp.sum(jnp.sum(dsn * s, axis=0, keepdims=True), axis=1, keepdims=True)
            dgl_ref[b, gi, h:h + 1, :] = jnp.broadcast_to(dgl, (1, LANE))
            yield
            dw_ref[b, rows, sl] = (-_nt(dvn, s)).astype(dw_ref.dtype)
            ds_ref[b, h] = ds_part - _tn(w_ref[b, rows, sl], dvn)

        for gi in reversed(range(grp)):
            _run([chain(b, h, gi) for b in range(bsz) for h in range(NH)])

    steps = nc_seq // grp
    rev = lambda i: steps - 1 - i
    row = pl.BlockSpec((bsz, grp * CHUNK, hw), lambda i: (0, rev(i), 0))
    mat = pl.BlockSpec((bsz, grp, NH, CHUNK, CHUNK), lambda i: (0, rev(i), 0, 0, 0))
    glb = pl.BlockSpec((bsz, grp, NH, LANE), lambda i: (0, rev(i), 0, 0))
    big = jax.ShapeDtypeStruct((bsz, t_seq, hw), BF16)
    outs = pl.pallas_call(
        body, name=name, grid=(steps,),
        in_specs=[row, row, row, row, row, mat, glb,
                  pl.BlockSpec((bsz, grp, NH, DN_D, DN_D), lambda i: (0, rev(i), 0, 0, 0))],
        out_specs=[row, row, row, row, glb],
        out_shape=[big, big, jax.ShapeDtypeStruct(big.shape, F32), jax.ShapeDtypeStruct(big.shape, F32),
                   jax.ShapeDtypeStruct((bsz, nc_seq, NH, LANE), F32)],
        scratch_shapes=[pltpu.VMEM((bsz, NH, DN_D, DN_D), F32)],
        compiler_params=_cp(VMEM_BIG, ("arbitrary",)),
    )(do, w, qg, kd, vn, p, gl, hist)
    du, dw, dqg, dkd, dgl = outs
    n = bsz * t_seq
    return (du.reshape(n, hw), dw.reshape(n, hw), dqg.reshape(n, hw), dkd.reshape(n, hw),
            dgl.reshape(bsz * nc_seq, NH, LANE))


def _dn_intra_bwd(qn, kn, v, projp, alog_row, dtb_row, u, w, tmat, du, dw, dqg, dkd, do, vn, dgl, *, nc_seq, name,
                  rider=None):
    n = qn.shape[0]
    nct = n // CHUNK
    hw = NH * DN_D
    scale = DN_D ** -0.5

    grp = _group(nc_seq)

    def inner(gi, q_ref, k_ref, v_ref, sa_ref, al_ref, dt_ref, u_ref, w_ref, t_ref, du_ref, dw_ref, dqg_ref, dkd_ref,
              do_ref, vn_ref, dgl_ref, dq_ref, dk_ref, dv_ref, dsa_ref, dal_ref, ddt_ref):
        ci = (pl.program_id(0) * grp + gi) % nc_seq
        sa = sa_ref[...]
        beta_t, g_t, gam_t, valid, ea = _dn_gates(sa, al_ref[...], dt_ref[...], ci)
        yield
        lane = _iota2((CHUNK, LANE), 1)
        gates_t = jnp.where(lane < 4, beta_t, gam_t).T
        r, c = _masks64()
        incl, strict, upper, supper = r >= c, r > c, r <= c, r < c
        rows1 = _iota2((CHUNK, 1), 0)
        acc = [jnp.zeros((CHUNK, LANE), F32)]

        def head(h):
            sl = slice(h * DN_D, (h + 1) * DN_D)
            beta_w = jnp.broadcast_to(beta_t[:, h:h + 1], (CHUNK, DN_D))
            gam_w = jnp.broadcast_to(gam_t[:, 4 + h:5 + h], (CHUNK, DN_D))
            beta_s, gam_s = beta_w[:, 0:CHUNK], gam_w[:, 0:CHUNK]
            beta_row = gates_t[h:h + 1, :]
            gam_row = gates_t[4 + h:5 + h, :]
            gl = gam_t[CHUNK - 1:CHUNK, 4 + h:5 + h]
            dec = jnp.exp(jnp.where(incl, gam_s - gam_row, -jnp.inf))
            dec_t = jnp.exp(jnp.where(upper, gam_row - gam_s, -jnp.inf))
            egam_w = jnp.exp(gam_w)
            ekd_w = jnp.exp(gl - gam_w)
            kh = k_ref[:, sl].astype(F32)
            qh = q_ref[:, sl].astype(F32) * scale
            vh = v_ref[:, sl].astype(F32)
            uh = u_ref[:, sl]
            wh = w_ref[:, sl]
            doh = do_ref[:, sl]
            vnh = vn_ref[:, sl]
            kk = _nt(kh, kh)
            qk = _nt(qh, kh)
            qk_t = _nt(kh, qh)
            dp = _nt(doh, vnh)
            dp_t = _nt(vnh, doh)
            t_hi, t_lo = _split(t_ref[0, h].T)
            duh, dwh = du_ref[:, sl], dw_ref[:, sl]
            dvb = _nn(t_hi, duh) + _nn(t_lo, duh)
            dkg = _nn(t_hi, dwh) + _nn(t_lo, dwh)
            yield
            dvb_hi, dvb_lo = _split(dvb)
            dkg_hi, dkg_lo = _split(dkg)
            m = (_nt(dvb_hi, uh) + _nt(dvb_lo, uh)) + (_nt(dkg_hi, wh) + _nt(dkg_lo, wh))
            m_t = (_nt(uh, dvb_hi) + _nt(uh, dvb_lo)) + (_nt(wh, dkg_hi) + _nt(wh, dkg_lo))
            yield
            da = jnp.where(strict, -m, 0.0)
            da_t = jnp.where(supper, -m_t, 0.0)
            a = jnp.where(strict, beta_s * kk * dec, 0.0)
            a_t = jnp.where(supper, beta_row * kk * dec_t, 0.0)
            dad = da * dec
            dad_t = da_t * dec_t
            dpm = jnp.where(incl, dp, 0.0)
            dpm_t = jnp.where(upper, dp_t, 0.0)
            e = da * a + dpm * (qk * dec)
            e_t = da_t * a_t + dpm_t * (qk_t * dec_t)
            dqgh = dqg_ref[:, sl].astype(F32)
            dkdh = dkd_ref[:, sl].astype(F32)
            bg_w = beta_w * egam_w
            dkh = (_nn(beta_s * dad, kh) + _nn(beta_row * dad_t, kh) + _nn(dpm_t * dec_t, qh)
                   + bg_w * dkg + ekd_w * dkdh)
            dqh = _nn(dpm * dec, kh) + egam_w * dqgh
            t_kd = dkdh * (ekd_w * kh)
            dbeta = (jnp.sum(dad * kk, axis=1, keepdims=True)
                     + jnp.sum(dkg * (egam_w * kh) + dvb * vh, axis=1, keepdims=True))
            dgam = (jnp.sum(e - e_t, axis=1, keepdims=True)
                    + jnp.sum(dkg * (bg_w * kh) + dqgh * (egam_w * qh) - t_kd, axis=1, keepdims=True))
            dgam_last = (jnp.sum(jnp.sum(t_kd, axis=0, keepdims=True), axis=1, keepdims=True)
                         + dgl_ref[0, h:h + 1, 0:1] * jnp.exp(gl))
            dgam = dgam + jnp.where(rows1 == CHUNK - 1, dgam_last, 0.0)
            dq_ref[:, sl] = (dqh * scale).astype(dq_ref.dtype)
            dk_ref[:, sl] = dkh.astype(dk_ref.dtype)
            dv_ref[:, sl] = (beta_w * dvb).astype(dv_ref.dtype)
            acc[0] = acc[0] + jnp.where(lane == h, dbeta, 0.0) + jnp.where(lane == 4 + h, dgam, 0.0)

        yield from _round_robin([head(h) for h in range(NH)])
        acc_t = acc[0]
        dg_t = _tri_sum(upper, acc_t)
        ddb = acc_t * beta_t * (1.0 - beta_t)
        dda = jnp.where(valid, dg_t * (-ea) * _sigmoid(sa + dt_ref[...]), 0.0)
        dsa_ref[...] = jnp.where(lane < 4, ddb, jnp.where(lane < 8, dda, 0.0)).astype(dsa_ref.dtype)
        in_g = jnp.logical_and(lane >= 4, lane < 8)
        dal = jnp.sum(jnp.where(in_g, dg_t * g_t, 0.0), axis=0, keepdims=True)
        ddt = jnp.sum(jnp.where(in_g, dda, 0.0), axis=0, keepdims=True)
        _accumulate(dal_ref, dal, gi)
        _accumulate(ddt_ref, ddt, gi)

    rows = grp * CHUNK
    row = pl.BlockSpec((rows, hw), lambda i: (i, 0))
    vec = pl.BlockSpec((1, LANE), lambda i: (0, 0))
    mat = pl.BlockSpec((grp, NH, CHUNK, CHUNK), lambda i: (i, 0, 0, 0))
    glb = pl.BlockSpec((grp, NH, LANE), lambda i: (i, 0, 0))
    big = jax.ShapeDtypeStruct((n, hw), F32)
    v128 = jax.ShapeDtypeStruct((1, LANE), F32)
    kinds = (["row"] * 4 + ["whole"] * 2 + ["row"] * 2 + ["lead"] + ["row"] * 6 + ["lead"]
             + ["row"] * 4 + ["whole"] * 2)
    outs, ridden = _hosted_call(
        _per_chunk(inner, kinds, grp), rider, name=name, grid=(nct // grp,),
        in_specs=[row, row, row, pl.BlockSpec((rows, LANE), lambda i: (i, 0)), vec, vec,
                  row, row, mat, row, row, row, row, row, row, glb],
        out_specs=[row, row, row, pl.BlockSpec((rows, LANE), lambda i: (i, 0)), vec, vec],
        out_shape=[big, big, big, jax.ShapeDtypeStruct((n, LANE), BF16), v128, v128],
        scratch_shapes=[], compiler_params=_cp(VMEM_BIG, ("arbitrary",)),
        args=(qn, kn, v, projp, alog_row, dtb_row, u, w, tmat, du, dw, dqg, dkd, do, vn, dgl))
    return (*outs, ridden)


GQ_W = NH * GLA_DK
GV_W = NH * GLA_DV
GLA_NORM = 16.0
MID = CHUNK // 2


def _gla_gates(sb, w2p, gb, chunk_in_seq):
    rows = _iota2((CHUNK, GQ_W), 0)
    valid = jnp.logical_or(rows >= N_PAD, chunk_in_seq > 0)
    graw = _nn(sb, w2p) + gb
    yield
    g = jnp.where(valid, _logsigmoid(graw) * (1.0 / GLA_NORM), 0.0)
    r, c = _masks64()
    bcum = _tri_sum(r >= c, g)
    yield
    return graw, bcum, valid


def _head_mask(h):
    lane = _iota2((1, GQ_W), 1)
    return jnp.logical_and(lane >= h * GLA_DK, lane < (h + 1) * GLA_DK)


def _gla_intra_fwd(projp, gates, w2p, gb, *, nc_seq, name):
    n = projp.shape[0]
    nct = n // CHUNK
    scale = GLA_DK ** -0.5

    grp = _group(nc_seq)
    rows = grp * CHUNK

    def inner(gi, qk_ref, v_ref, sb_ref, w2_ref, gb_ref, oi_ref, qg_ref, kd_ref, gl_ref):
        ci = (pl.program_id(0) * grp + gi) % nc_seq
        _, bc, _ = yield from _gla_gates(sb_ref[...], w2_ref[...], gb_ref[...], ci)
        bref = bc[MID:MID + 1, :]
        bl = bc[CHUNK - 1:CHUNK, :]
        q = qk_ref[:, 0:GQ_W].astype(F32) * scale
        k = qk_ref[:, GQ_W:2 * GQ_W].astype(F32)
        qi = q * jnp.exp(bc - bref)
        ki = k * jnp.exp(bref - bc)
        qg_ref[...] = (q * jnp.exp(bc)).astype(qg_ref.dtype)
        kd_ref[...] = (k * jnp.exp(bl - bc)).astype(kd_ref.dtype)
        gl_ref[0] = jnp.exp(bl)
        r, c = _masks64()
        incl = r >= c
        a = [jnp.where(incl, _nt(jnp.where(_head_mask(h), qi, 0.0), ki), 0.0) for h in range(NH)]
        yield
        for h in range(NH):
            oi_ref[:, h * GLA_DV:(h + 1) * GLA_DV] = _nn(a[h], v_ref[:, h * GLA_DV:(h + 1) * GLA_DV]).astype(oi_ref.dtype)

    kinds = ["row"] * 3 + ["whole"] * 2 + ["row"] * 3 + ["lead"]
    return pl.pallas_call(
        _per_chunk(inner, kinds, grp), name=name, grid=(nct // grp,),
        in_specs=[pl.BlockSpec((rows, 2 * GQ_W), lambda i: (i, C_GQK // (2 * GQ_W))),
                  pl.BlockSpec((rows, GV_W), lambda i: (i, C_GV // GV_W)),
                  pl.BlockSpec((rows, LANE), lambda i: (i, 1)),
                  pl.BlockSpec((LANE, GQ_W), lambda i: (0, 0)), pl.BlockSpec((1, GQ_W), lambda i: (0, 0))],
        out_specs=[pl.BlockSpec((rows, GV_W), lambda i: (i, 0)), pl.BlockSpec((rows, GQ_W), lambda i: (i, 0)),
                   pl.BlockSpec((rows, GQ_W), lambda i: (i, 0)), pl.BlockSpec((grp, 1, GQ_W), lambda i: (i, 0, 0))],
        out_shape=[jax.ShapeDtypeStruct((n, GV_W), BF16), jax.ShapeDtypeStruct((n, GQ_W), BF16),
                   jax.ShapeDtypeStruct((n, GQ_W), BF16), jax.ShapeDtypeStruct((nct, 1, GQ_W), F32)],
        compiler_params=_cp(VMEM_BIG),
    )(projp, projp, gates, w2p, gb)


def _gla_scan_fwd(oi, qg, kd, gl, projp, *, bsz, nc_seq, name, rider=None):
    t_seq = nc_seq * CHUNK
    oi = oi.reshape(bsz, t_seq, GV_W)
    qg, kd = qg.reshape(bsz, t_seq, GQ_W), kd.reshape(bsz, t_seq, GQ_W)
    gl = gl.reshape(bsz, nc_seq, 1, GQ_W)
    pj = projp.reshape(bsz, t_seq, PW)
    grp = _group(nc_seq)

    def body(oi_ref, qg_ref, kd_ref, gl_ref, v_ref, o_ref, hist_ref, st_ref):
        @pl.when(pl.program_id(0) == 0)
        def _():
            st_ref[...] = jnp.zeros_like(st_ref)

        for gi in range(grp):
            rows = pl.ds(gi * CHUNK, CHUNK)
            for b in range(bsz):
                st = st_ref[b]
                hist_ref[b, gi] = st.astype(hist_ref.dtype)
                qgb = qg_ref[b, rows, :]
                kdb = kd_ref[b, rows, :]
                upd = jnp.zeros((GLA_DV, GQ_W), F32)
                for h in range(NH):
                    sl = slice(h * GLA_DV, (h + 1) * GLA_DV)
                    m = _head_mask(h)
                    o_ref[b, rows, sl] = (oi_ref[b, rows, sl] + _nt(jnp.where(m, qgb, 0.0), st)).astype(o_ref.dtype)
                    upd = upd + jnp.where(m, _tn(v_ref[b, rows, sl], kdb), 0.0)
                st_ref[b] = gl_ref[b, gi] * st + upd

    rws = grp * CHUNK
    outs, ridden = _hosted_call(
        body, rider, name=name, grid=(nc_seq // grp,),
        in_specs=[pl.BlockSpec((bsz, rws, GV_W), lambda i: (0, i, 0)),
                  pl.BlockSpec((bsz, rws, GQ_W), lambda i: (0, i, 0)),
                  pl.BlockSpec((bsz, rws, GQ_W), lambda i: (0, i, 0)),
                  pl.BlockSpec((bsz, grp, 1, GQ_W), lambda i: (0, i, 0, 0)),
                  pl.BlockSpec((bsz, rws, GV_W), lambda i: (0, i, C_GV // GV_W))],
        out_specs=[pl.BlockSpec((bsz, rws, GV_W), lambda i: (0, i, 0)),
                   pl.BlockSpec((bsz, grp, GLA_DV, GQ_W), lambda i: (0, i, 0, 0))],
        out_shape=[jax.ShapeDtypeStruct((bsz, t_seq, GV_W), BF16),
                   jax.ShapeDtypeStruct((bsz, nc_seq, GLA_DV, GQ_W), F32)],
        scratch_shapes=[pltpu.VMEM((bsz, GLA_DV, GQ_W), F32)],
        compiler_params=_cp(VMEM_BIG, ("arbitrary",)), args=(oi, qg, kd, gl, pj))
    return outs[0].reshape(bsz * t_seq, GV_W), outs[1], ridden


def _gla_scan_bwd(do, qg, kd, gl, projp, hist, *, bsz, nc_seq, name):
    t_seq = nc_seq * CHUNK
    do = do.reshape(bsz, t_seq, GV_W)
    qg, kd = qg.reshape(bsz, t_seq, GQ_W), kd.reshape(bsz, t_seq, GQ_W)
    gl = gl.reshape(bsz, nc_seq, 1, GQ_W)
    pj = projp.reshape(bsz, t_seq, PW)
    grp = _group(nc_seq)

    def body(do_ref, qg_ref, kd_ref, gl_ref, v_ref, hist_ref, dqg_ref, dkd_ref, dv_ref, dgl_ref, dst_ref):
        @pl.when(pl.program_id(0) == 0)
        def _():
            dst_ref[...] = jnp.zeros_like(dst_ref)

        for gi in reversed(range(grp)):
            rows = pl.ds(gi * CHUNK, CHUNK)
            for b in range(bsz):
                st = hist_ref[b, gi]
                dst = dst_ref[b]
                qgb = qg_ref[b, rows, :]
                kdb = kd_ref[b, rows, :]
                dqg = jnp.zeros((CHUNK, GQ_W), F32)
                dkd = jnp.zeros((CHUNK, GQ_W), F32)
                add = jnp.zeros((GLA_DV, GQ_W), F32)
                for h in range(NH):
                    sl = slice(h * GLA_DV, (h + 1) * GLA_DV)
                    m = _head_mask(h)
                    doh = do_ref[b, rows, sl]
                    vh = v_ref[b, rows, sl]
                    dqg = dqg + jnp.where(m, _nn(doh, st), 0.0)
                    dkd = dkd + jnp.where(m, _nn(vh, dst), 0.0)
                    dv_ref[b, rows, sl] = _nt(jnp.where(m, kdb, 0.0), dst).astype(dv_ref.dtype)
                    add = add + jnp.where(m, _tn(doh, qgb), 0.0)
                dqg_ref[b, rows, :] = dqg.astype(dqg_ref.dtype)
                dkd_ref[b, rows, :] = dkd.astype(dkd_ref.dtype)
                dgl_ref[b, gi] = jnp.sum(dst * st, axis=0, keepdims=True)
                dst_ref[b] = gl_ref[b, gi] * dst + add

    steps = nc_seq // grp
    rws = grp * CHUNK
    rev = lambda i: steps - 1 - i
    outs = pl.pallas_call(
        body, name=name, grid=(steps,),
        in_specs=[pl.BlockSpec((bsz, rws, GV_W), lambda i: (0, rev(i), 0)),
                  pl.BlockSpec((bsz, rws, GQ_W), lambda i: (0, rev(i), 0)),
                  pl.BlockSpec((bsz, rws, GQ_W), lambda i: (0, rev(i), 0)),
                  pl.BlockSpec((bsz, grp, 1, GQ_W), lambda i: (0, rev(i), 0, 0)),
                  pl.BlockSpec((bsz, rws, GV_W), lambda i: (0, rev(i), C_GV // GV_W)),
                  pl.BlockSpec((bsz, grp, GLA_DV, GQ_W), lambda i: (0, rev(i), 0, 0))],
        out_specs=[pl.BlockSpec((bsz, rws, GQ_W), lambda i: (0, rev(i), 0)),
                   pl.BlockSpec((bsz, rws, GQ_W), lambda i: (0, rev(i), 0)),
                   pl.BlockSpec((bsz, rws, GV_W), lambda i: (0, rev(i), 0)),
                   pl.BlockSpec((bsz, grp, 1, GQ_W), lambda i: (0, rev(i), 0, 0))],
        out_shape=[jax.ShapeDtypeStruct((bsz, t_seq, GQ_W), F32), jax.ShapeDtypeStruct((bsz, t_seq, GQ_W), F32),
                   jax.ShapeDtypeStruct((bsz, t_seq, GV_W), BF16), jax.ShapeDtypeStruct((bsz, nc_seq, 1, GQ_W), F32)],
        scratch_shapes=[pltpu.VMEM((bsz, GLA_DV, GQ_W), F32)],
        compiler_params=_cp(VMEM_BIG, ("arbitrary",)),
    )(do, qg, kd, gl, pj, hist)
    n = bsz * t_seq
    return (outs[0].reshape(n, GQ_W), outs[1].reshape(n, GQ_W), outs[2].reshape(n, GV_W),
            outs[3].reshape(bsz * nc_seq, 1, GQ_W))


def _gla_intra_bwd(projp, gates, w2p, gb, do, dqg, dkd, dvi, dgl, *, nc_seq, name):
    n = projp.shape[0]
    nct = n // CHUNK
    scale = GLA_DK ** -0.5

    grp = _group(nc_seq)
    rows = grp * CHUNK

    def inner(gi, qk_ref, v_ref, sb_ref, w2_ref, gb_ref, do_ref, dqg_ref, dkd_ref, dvi_ref, dgl_ref,
              dqk_ref, dv_ref, dsb_ref, dw2_ref, dgb_ref):
        ci = (pl.program_id(0) * grp + gi) % nc_seq
        sb = sb_ref[...]
        w2 = w2_ref[...]
        graw, bc, valid = yield from _gla_gates(sb, w2, gb_ref[...], ci)
        bref = bc[MID:MID + 1, :]
        bl = bc[CHUNK - 1:CHUNK, :]
        q = qk_ref[:, 0:GQ_W].astype(F32) * scale
        k = qk_ref[:, GQ_W:2 * GQ_W].astype(F32)
        ex1 = jnp.exp(bc - bref)
        ex2 = jnp.exp(bref - bc)
        eb = jnp.exp(bc)
        ekd = jnp.exp(bl - bc)
        qi, ki = q * ex1, k * ex2
        r, c = _masks64()
        incl = r >= c
        upper = r <= c
        a_t, da, da_t = [], [], []
        for h in range(NH):
            sl = slice(h * GLA_DV, (h + 1) * GLA_DV)
            doh = do_ref[:, sl]
            vh = v_ref[:, sl]
            a_t.append(jnp.where(upper, _nt(jnp.where(_head_mask(h), ki, 0.0), qi), 0.0))
            da.append(jnp.where(incl, _nt(doh, vh), 0.0))
            da_t.append(jnp.where(upper, _nt(vh, doh), 0.0))
        yield
        dqi = jnp.zeros((CHUNK, GQ_W), F32)
        dki = jnp.zeros((CHUNK, GQ_W), F32)
        for h in range(NH):
            sl = slice(h * GLA_DV, (h + 1) * GLA_DV)
            m = _head_mask(h)
            dv_ref[:, sl] = (_nn(a_t[h], do_ref[:, sl]) + dvi_ref[:, sl]).astype(dv_ref.dtype)
            dqi = dqi + jnp.where(m, _nn(da[h], ki), 0.0)
            dki = dki + jnp.where(m, _nn(da_t[h], qi), 0.0)
        yield
        dqg = dqg_ref[...].astype(F32)
        dkd = dkd_ref[...].astype(F32)
        dqk_ref[:, 0:GQ_W] = ((dqi * ex1 + dqg * eb) * scale).astype(dqk_ref.dtype)
        dqk_ref[:, GQ_W:2 * GQ_W] = (dki * ex2 + dkd * ekd).astype(dqk_ref.dtype)
        t_qi, t_ki, t_kd = dqi * qi, dki * ki, dkd * (k * ekd)
        db = t_qi - t_ki + dqg * (q * eb) - t_kd
        dbref = jnp.sum(t_ki - t_qi, axis=0, keepdims=True)
        dbl = jnp.sum(t_kd, axis=0, keepdims=True) + dgl_ref[0] * jnp.exp(bl)
        rows = _iota2((CHUNK, GQ_W), 0)
        db = db + jnp.where(rows == MID, dbref, 0.0) + jnp.where(rows == CHUNK - 1, dbl, 0.0)
        dg = _tri_sum(upper, db)
        yield
        dgraw = jnp.where(valid, dg * (1.0 / GLA_NORM) * _sigmoid(-graw), 0.0)
        dsb_ref[...] = _nt(dgraw, w2).astype(dsb_ref.dtype)
        dw2 = _tn(sb, dgraw)
        dgb = jnp.sum(dgraw, axis=0, keepdims=True)
        _accumulate(dw2_ref, dw2, gi)
        _accumulate(dgb_ref, dgb, gi)

    rq = pl.BlockSpec((rows, GQ_W), lambda i: (i, 0))
    rv = pl.BlockSpec((rows, GV_W), lambda i: (i, 0))
    kinds = ["row"] * 3 + ["whole"] * 2 + ["row"] * 4 + ["lead"] + ["row"] * 3 + ["whole"] * 2
    return pl.pallas_call(
        _per_chunk(inner, kinds, grp), name=name, grid=(nct // grp,),
        in_specs=[pl.BlockSpec((rows, 2 * GQ_W), lambda i: (i, C_GQK // (2 * GQ_W))),
                  pl.BlockSpec((rows, GV_W), lambda i: (i, C_GV // GV_W)),
                  pl.BlockSpec((rows, LANE), lambda i: (i, 1)),
                  pl.BlockSpec((LANE, GQ_W), lambda i: (0, 0)), pl.BlockSpec((1, GQ_W), lambda i: (0, 0)),
                  rv, rq, rq, rv, pl.BlockSpec((grp, 1, GQ_W), lambda i: (i, 0, 0))],
        out_specs=[pl.BlockSpec((rows, 2 * GQ_W), lambda i: (i, 0)), rv, pl.BlockSpec((rows, LANE), lambda i: (i, 0)),
                   pl.BlockSpec((LANE, GQ_W), lambda i: (0, 0)), pl.BlockSpec((1, GQ_W), lambda i: (0, 0))],
        out_shape=[jax.ShapeDtypeStruct((n, 2 * GQ_W), BF16), jax.ShapeDtypeStruct((n, GV_W), BF16),
                   jax.ShapeDtypeStruct((n, LANE), BF16), jax.ShapeDtypeStruct((LANE, GQ_W), F32),
                   jax.ShapeDtypeStruct((1, GQ_W), F32)],
        compiler_params=_cp(VMEM_BIG, ("arbitrary",)),
    )(projp, projp, gates, w2p, gb, do, dqg, dkd, dvi, dgl)


SECTIONS = ((C_QKV, 1536), (C_DZ, 512), (C_GQK, 512), (C_GV, 512), (C_GR, 512), (C_SA, 128), (C_SB, 128))


def _inproj_bwd(secs, wp, h0, g1, dx1, *, tr, name):
    n, d = h0.shape

    def body(*refs):
        sec_refs = refs[:len(SECTIONS)]
        wp_ref, h0_ref, g_ref, dx1_ref, o_ref, dg_ref = refs[len(SECTIONS):]
        dh = None
        for s_ref, (off, wd) in zip(sec_refs, SECTIONS):
            part = _nt(s_ref[...], wp_ref[:, off:off + wd])
            dh = part if dh is None else dh + part
        dx, dg = _rms_bwd_math(h0_ref[...], g_ref[...], dh)
        o_ref[...] = dx1_ref[...] + dx

        @pl.when(pl.program_id(0) == 0)
        def _():
            dg_ref[...] = dg

        @pl.when(pl.program_id(0) > 0)
        def _():
            dg_ref[...] += dg

    row = pl.BlockSpec((tr, d), lambda i: (i, 0))
    vec = pl.BlockSpec((1, d), lambda i: (0, 0))
    return pl.pallas_call(
        body, name=name, grid=(n // tr,),
        in_specs=[pl.BlockSpec((tr, wd), lambda i: (i, 0)) for _, wd in SECTIONS]
        + [pl.BlockSpec((d, PW), lambda i: (0, 0)), row, vec, row],
        out_specs=[row, vec],
        out_shape=[jax.ShapeDtypeStruct((n, d), F32), jax.ShapeDtypeStruct((1, d), F32)],
        compiler_params=_cp(VMEM_BIG),
    )(*secs, wp, h0, g1, dx1)


def _adamw(w, g, m, v, *, name, emit_grad=False, col_tile=None):
    lead = w.ndim - 2
    r, c = w.shape[-2:]
    tr = r if col_tile is not None else (_tile(r, 256, 8) if r > 256 else r)
    tc = col_tile if col_tile is not None else c
    c1 = 1.0 - ADAM_B1 ** ADAM_STEP
    c2 = 1.0 - ADAM_B2 ** ADAM_STEP
    n_out = 4 if emit_grad else 3

    def body(w_ref, g_ref, m_ref, v_ref, *out_refs):
        rd = (lambda ref: ref[0]) if lead else (lambda ref: ref[...])
        gv = g_ref[:, 0:tc]
        nm = ADAM_B1 * rd(m_ref) + (1.0 - ADAM_B1) * gv
        nv = ADAM_B2 * rd(v_ref) + (1.0 - ADAM_B2) * (gv * gv)
        res = [-ADAM_LR * ((nm / c1) / (jnp.sqrt(nv / c2) + ADAM_EPS) + ADAM_WD * rd(w_ref)), nm, nv, gv]
        for o_ref, val in zip(out_refs, res):
            if lead:
                o_ref[0] = val
            else:
                o_ref[...] = val

    if col_tile is None:
        blk = pl.BlockSpec((1,) * lead + (tr, c), lambda i: (0,) * lead + (i, 0))
        gblk = pl.BlockSpec((tr, g.shape[1]), lambda i: (i, 0))
        steps = r // tr
    else:
        blk = pl.BlockSpec((1,) * lead + (r, tc), lambda j: (0,) * lead + (0, j))
        gblk = pl.BlockSpec((r, tc), lambda j: (0, j))
        steps = c // tc
    sds = jax.ShapeDtypeStruct(w.shape, F32)
    return pl.pallas_call(
        body, name=name, grid=(steps,), in_specs=[blk, gblk, blk, blk], out_specs=[blk] * n_out,
        out_shape=[sds] * n_out, compiler_params=_cp(VMEM_BIG),
    )(w, g, m, v)


def _adamw_rows(w, g, m, v, *, name):
    r, _, c = w.shape
    tr = max(t for t in range(1, 129) if r % t == 0)
    c1 = 1.0 - ADAM_B1 ** ADAM_STEP
    c2 = 1.0 - ADAM_B2 ** ADAM_STEP

    def body(w_ref, g_ref, m_ref, v_ref, d_ref, nm_ref, nv_ref, go_ref):
        gv = g_ref[...]
        nm = ADAM_B1 * m_ref[...] + (1.0 - ADAM_B1) * gv
        nv = ADAM_B2 * v_ref[...] + (1.0 - ADAM_B2) * (gv * gv)
        d_ref[...] = -ADAM_LR * ((nm / c1) / (jnp.sqrt(nv / c2) + ADAM_EPS) + ADAM_WD * w_ref[...])
        nm_ref[...] = nm
        nv_ref[...] = nv
        go_ref[...] = gv

    blk = pl.BlockSpec((tr, 1, c), lambda i: (i, 0, 0))
    sds = jax.ShapeDtypeStruct(w.shape, F32)
    return pl.pallas_call(
        body, name=name, grid=(r // tr,), in_specs=[blk] * 4, out_specs=[blk] * 4, out_shape=[sds] * 4,
        compiler_params=_cp(VMEM_BIG),
    )(w, g, m, v)


def _pair_sum(where, g, theirs, *, name):
    lead, r, cols = g.shape
    half = r // 2
    tr = _tile(half, 256, 16)
    nh = half // tr

    def body(w_ref, a_ref, b_ref, o_ref):
        o_ref[...] = (a_ref[...] + b_ref[...]).astype(o_ref.dtype)

    blk = pl.BlockSpec((1, tr, cols), lambda s, i, w: (s, i, 0))
    return pl.pallas_call(
        body, name=name,
        grid_spec=pltpu.PrefetchScalarGridSpec(
            num_scalar_prefetch=1, grid=(lead, nh),
            in_specs=[pl.BlockSpec((1, tr, cols), lambda s, i, w: (s, w[0] * nh + i, 0)), blk], out_specs=blk),
        out_shape=jax.ShapeDtypeStruct((lead, half, cols), BF16), compiler_params=_cp(VMEM_BIG),
    )(where, g, theirs)


def _chip_sum(where, pair, q, *, name):
    _, half, cols = pair.shape
    tr = _tile(half, 256, 16)
    nh = half // tr

    def body(w_ref, own_ref, q1_ref, q2_ref, q3_ref, o_ref):
        f = lambda ref: ref[0].astype(F32)
        o_ref[...] = ((f(own_ref) + f(q1_ref)) + f(q2_ref)) + f(q3_ref)

    def peer(d):
        return pl.BlockSpec((1, tr, cols), lambda i, w: ((w[1] + d) % N_CHIPS, i, 0))

    return pl.pallas_call(
        body, name=name,
        grid_spec=pltpu.PrefetchScalarGridSpec(
            num_scalar_prefetch=1, grid=(nh,),
            in_specs=[peer(0), peer(1), peer(2), peer(3)],
            out_specs=pl.BlockSpec((tr, cols), lambda i, w: (w[0] * nh + i, 0))),
        out_shape=jax.ShapeDtypeStruct((2 * half, cols), F32), compiler_params=_cp(VMEM_BIG),
    )(where, pair, q, q, q)


VM = pl.BlockSpec(memory_space=pltpu.VMEM)


def _row_chunks(rows, n_split):
    size = rows // n_split
    assert size * n_split == rows and size % 16 == 0, (rows, n_split)
    return [(s, pl.ds(s * size, size)) for s in range(n_split)], size


D2D_SPLIT = 4
ICI_SPLIT = 2


def _sibling_halves(grads):
    n_arr = len(grads)

    def body(*refs):
        ins = refs[:n_arr]
        theirs = refs[n_arr:2 * n_arr]
        send_sems, recv_sems = refs[2 * n_arr:]
        x, y, c = _place()
        copies = []
        for k in range(n_arr):
            half = ins[k].shape[1] // 2
            chunks, size = _row_chunks(half, D2D_SPLIT)
            for s, dst_rows in chunks:
                give = pltpu.make_async_remote_copy(
                    src_ref=ins[k].at[:, pl.ds((1 - c) * half + s * size, size), :], dst_ref=theirs[k].at[:, dst_rows, :],
                    send_sem=send_sems.at[k, s], recv_sem=recv_sems.at[k, s], device_id=(x, y, 1 - c),
                    device_id_type=MESH)
                give.start()
                copies.append(give)
        for give in copies:
            give.wait()

    halves = [jax.ShapeDtypeStruct((g.shape[0], g.shape[1] // 2, g.shape[2]), F32) for g in grads]
    sem = pltpu.SemaphoreType.DMA((n_arr, D2D_SPLIT))
    return pl.pallas_call(
        body, name="sibling_halves", in_specs=[ANY] * n_arr, out_specs=[ANY] * n_arr, out_shape=halves,
        scratch_shapes=[sem, sem],
    )(*grads)


def _chip_exchange(parts):
    n_arr = len(parts)

    def body(*refs):
        ins = refs[:n_arr]
        outs = refs[n_arr:2 * n_arr]
        send_sems, recv_sems = refs[2 * n_arr:]
        x, y, c = _place()
        me = 2 * x + y
        sends = []
        for k in range(n_arr):
            chunks, _ = _row_chunks(ins[k].shape[1], ICI_SPLIT)
            for d, (px, py, pj) in enumerate(_other_chips(x, y)):
                for s, rows in chunks:
                    cp = pltpu.make_async_remote_copy(
                        src_ref=ins[k].at[pj, rows, :], dst_ref=outs[k].at[me, rows, :], send_sem=send_sems.at[k, d, s],
                        recv_sem=recv_sems.at[k, d, s], device_id=(px, py, c), device_id_type=MESH)
                    cp.start()
                    sends.append(cp)
        for k in range(n_arr):
            chunks, _ = _row_chunks(ins[k].shape[1], ICI_SPLIT)
            for d, (px, py, pj) in enumerate(_other_chips(x, y)):
                for s, rows in chunks:
                    pltpu.make_async_remote_copy(
                        src_ref=ins[k].at[pj, rows, :], dst_ref=outs[k].at[pj, rows, :], send_sem=send_sems.at[k, d, s],
                        recv_sem=recv_sems.at[k, d, s], device_id=(px, py, c), device_id_type=MESH).wait_recv()
        for cp in sends:
            cp.wait_send()

    sem = pltpu.SemaphoreType.DMA((n_arr, 3, ICI_SPLIT))
    return pl.pallas_call(
        body, name="chip_exchange", in_specs=[ANY] * n_arr, out_specs=[ANY] * n_arr,
        out_shape=[jax.ShapeDtypeStruct(p.shape, p.dtype) for p in parts],
        scratch_shapes=[sem, sem],
    )(*parts)


class _SiblingHalvesRider:
    def __init__(self, grads):
        self.inputs = list(grads)
        self.out_shapes = [jax.ShapeDtypeStruct((g.shape[0], g.shape[1] // 2, g.shape[2]), F32) for g in grads]
        self.aliases = {}
        self.sems = [pltpu.SemaphoreType.DMA((len(grads), D2D_SPLIT))] * 2

    def _copies(self, ins, outs, sems):
        x, y, c = _place()
        for k in range(len(ins)):
            half = ins[k].shape[1] // 2
            chunks, size = _row_chunks(half, D2D_SPLIT)
            for s, dst_rows in chunks:
                yield pltpu.make_async_remote_copy(
                    src_ref=ins[k].at[:, pl.ds((1 - c) * half + s * size, size), :], dst_ref=outs[k].at[:, dst_rows, :],
                    send_sem=sems[0].at[k, s], recv_sem=sems[1].at[k, s], device_id=(x, y, 1 - c), device_id_type=MESH)

    def first(self, ins, outs, sems):
        for cp in self._copies(ins, outs, sems):
            cp.start()

    def last(self, ins, outs, sems):
        for cp in self._copies(ins, outs, sems):
            cp.wait()


class _ChipExchangeRider:
    def __init__(self, parts):
        self.inputs = list(parts)
        self.out_shapes = [jax.ShapeDtypeStruct(p.shape, p.dtype) for p in parts]
        self.aliases = {}
        self.sems = [pltpu.SemaphoreType.DMA((len(parts), 3, ICI_SPLIT))] * 2

    def _copies(self, ins, outs, sems, receiving):
        x, y, c = _place()
        for k in range(len(ins)):
            chunks, _ = _row_chunks(ins[k].shape[1], ICI_SPLIT)
            for d, (px, py, pj) in enumerate(_other_chips(x, y)):
                for s, rows in chunks:
                    yield pltpu.make_async_remote_copy(
                        src_ref=ins[k].at[pj, rows, :], dst_ref=outs[k].at[pj if receiving else 2 * x + y, rows, :],
                        send_sem=sems[0].at[k, d, s], recv_sem=sems[1].at[k, d, s], device_id=(px, py, c),
                        device_id_type=MESH)

    def first(self, ins, outs, sems):
        for cp in self._copies(ins, outs, sems, False):
            cp.start()

    def last(self, ins, outs, sems):
        for cp in self._copies(ins, outs, sems, True):
            cp.wait_recv()
        for cp in self._copies(ins, outs, sems, False):
            cp.wait_send()


def _sibling_join(bufs):
    n_arr = len(bufs)

    def body(*refs):
        bufs_out = refs[n_arr:2 * n_arr]
        send_sems, recv_sems = refs[2 * n_arr:]
        x, y, c = _place()
        copies = []
        for k in range(n_arr):
            half = bufs_out[k].shape[0] // 2
            chunks, size = _row_chunks(half, D2D_SPLIT)
            for s, _ in chunks:
                rows = pl.ds(c * half + s * size, size)
                give = pltpu.make_async_remote_copy(
                    src_ref=bufs_out[k].at[rows, :], dst_ref=bufs_out[k].at[rows, :], send_sem=send_sems.at[k, s],
                    recv_sem=recv_sems.at[k, s], device_id=(x, y, 1 - c), device_id_type=MESH)
                give.start()
                copies.append((k, s, half, size, give))
        for k, s, half, size, give in copies:
            rows = pl.ds((1 - c) * half + s * size, size)
            pltpu.make_async_remote_copy(
                src_ref=bufs_out[k].at[rows, :], dst_ref=bufs_out[k].at[rows, :], send_sem=send_sems.at[k, s],
                recv_sem=recv_sems.at[k, s], device_id=(x, y, 1 - c), device_id_type=MESH).wait_recv()
            give.wait_send()

    sem = pltpu.SemaphoreType.DMA((n_arr, D2D_SPLIT))
    return pl.pallas_call(
        body, name="sibling_join", in_specs=[ANY] * n_arr, out_specs=[ANY] * n_arr,
        out_shape=[jax.ShapeDtypeStruct(b.shape, F32) for b in bufs],
        input_output_aliases={k: k for k in range(n_arr)},
        scratch_shapes=[sem, sem],
    )(*bufs)


PACK_ROWS = 48


def _small_allreduce(pack):
    masks = [(dx, dy, dc) for dx in (0, 1) for dy in (0, 1) for dc in (0, 1)][1:]

    def body(p_ref, o_ref, buf, send_sems, recv_sems):
        x, y, c = _place()
        me = 4 * x + 2 * y + c
        buf[me] = p_ref[...]
        sends = []
        for k, (dx, dy, dc) in enumerate(masks):
            peer = (1 - x if dx else x, 1 - y if dy else y, 1 - c if dc else c)
            cp = pltpu.make_async_remote_copy(
                src_ref=p_ref, dst_ref=buf.at[me], send_sem=send_sems.at[k], recv_sem=recv_sems.at[k],
                device_id=peer, device_id_type=MESH)
            cp.start()
            sends.append(cp)
        for k, (dx, dy, dc) in enumerate(masks):
            peer = (1 - x if dx else x, 1 - y if dy else y, 1 - c if dc else c)
            pj = 4 * peer[0] + 2 * peer[1] + peer[2]
            pltpu.make_async_remote_copy(
                src_ref=p_ref, dst_ref=buf.at[pj], send_sem=send_sems.at[k], recv_sem=recv_sems.at[k],
                device_id=peer, device_id_type=MESH).wait_recv()
        for cp in sends:
            cp.wait_send()
        tot = buf[0]
        for k in range(1, 8):
            tot = tot + buf[k]
        o_ref[...] = tot
        o_ref[0:N_META, :] = tot[0:N_META] + tot[N_META:2 * N_META]

    return pl.pallas_call(
        body, name="small_allreduce", in_specs=[VM], out_specs=VM,
        out_shape=jax.ShapeDtypeStruct((PACK_ROWS, D_MODEL), F32),
        scratch_shapes=[pltpu.VMEM((8, PACK_ROWS, D_MODEL), F32), pltpu.SemaphoreType.DMA((7,)),
                        pltpu.SemaphoreType.DMA((7,))],
    )(pack)


def _pad_lanes(vec, offset):
    k = vec.shape[1]
    return jnp.concatenate([jnp.zeros((1, offset), F32), vec, jnp.zeros((1, LANE - offset - k), F32)], axis=1)


def _local_step(x, tgt, meta, norm1_g, wp, conv_w, a_log, dt_bias, dn_norm_g, gla_w2, gla_b, gla_norm_g,
                w_out, norm2_g, w_up, w_down, final_norm_g, late_gather=None, where=None):
    bsz, s_len, d = x.shape
    t_seq = s_len + CHUNK
    nc_seq = t_seq // CHUNK
    n = bsz * t_seq
    tr = _tile(t_seq, 832)
    tt = _tile(t_seq, 416)

    lead = jnp.concatenate([jnp.zeros((N_PAD, d), F32), meta], axis=0)
    h0 = jnp.concatenate([jnp.broadcast_to(lead[None], (bsz, CHUNK, d)), x], axis=1).reshape(n, d)
    tgt_p = jnp.concatenate([jnp.zeros((bsz, CHUNK, d), F32), tgt], axis=1).reshape(n, d)
    alog_row = _pad_lanes(a_log, 4)
    dtb_row = _pad_lanes(dt_bias, 4)
    w2p = jnp.concatenate([gla_w2, jnp.zeros((LANE - GLA_RANK, GQ_W), F32)], axis=0)

    h = _rms_fwd(h0, norm1_g, tr=tr, name="norm1")
    ride_up, ride_down, ride_out = late_gather if late_gather is not None else (None, None, None)
    res = _mm(h, wp, "nn", tm=tt, tn=PW, tk=d, out_dtypes=(BF16, F32), out_widths=(PW, PW - C_SA),
              epilogue=lambda acc: (acc, acc[:, C_SA:PW]), name="in_proj", rider=ride_up)
    ((projp, gates), got_up) = res if ride_up is not None else (res, None)
    qn, kn, v = _dnprep_fwd(projp, conv_w, bsz=bsz, t_seq=t_seq, tt=tt, name="dn_prep")
    u, w, qg, kd, pmat, tmat, gl, got_down = _dn_intra_fwd(qn, kn, v, gates, alog_row, dtb_row, nc_seq=nc_seq,
                                                           name="dn_intra", rider=ride_down)
    o_dn, vn, hist, got_out = _dn_scan_fwd(u, w, qg, kd, pmat, gl, bsz=bsz, nc_seq=nc_seq, name="dn_scan",
                                           rider=ride_out)
    oi, gqg, gkd, ggl = _gla_intra_fwd(projp, gates, w2p, gla_b, nc_seq=nc_seq, name="gla_intra")
    o_gla, ghist, _ = _gla_scan_fwd(oi, gqg, gkd, ggl, projp, bsz=bsz, nc_seq=nc_seq, name="gla_scan")
    if late_gather is not None:
        w_out = got_out[0].reshape(d, d)
        w_up = got_up[0].transpose(1, 0, 2).reshape(d, D_FF)
        w_down = got_down[0].reshape(D_FF, d)
    mix = _gnorm_fwd(o_dn, o_gla, projp, dn_norm_g, gla_norm_g, tr=tr, name="gated_norm")
    (x1,) = _mm(mix, w_out, "nn", tm=tr, tn=d, tk=d, out_dtypes=(F32,), extras=(h0,),
                epilogue=lambda acc, res: (res + acc,), name="out_proj")
    h2 = _rms_fwd(x1, norm2_g, tr=tr, name="norm2")

    (act,) = _mm(h2, w_up, "nn", tm=tt, tn=D_FF, tk=d, out_dtypes=(BF16,),
                 epilogue=lambda acc: (jnp.square(jnp.maximum(acc, 0.0)),), name="mlp_up", n_chunk=1024)
    dx2, dx2b, d_final_g, loss_tile = _mlp_down_loss(act, w_down, x1, final_norm_g, tgt_p, t_seq=t_seq, tr=tt,
                                                     name="mlp_down_loss")

    (dup,) = _mm(dx2b, w_down, "nt", tm=tt, tn=D_FF, tk=d, out_dtypes=(BF16,), extras=(act,),
                 epilogue=lambda acc, a: (acc * (2.0 * jnp.sqrt(a.astype(F32))),), name="mlp_down_bwd", n_chunk=1024)
    tk2 = 2 * tr if n % (2 * tr) == 0 else tr
    (d_w_down,) = _mm(act, dx2b, "tn", tm=D_FF // 2, tn=d, tk=tk2, out_dtypes=(F32,), name="w_down_grad")
    (d_w_up,) = _mm(h2, dup, "tn", tm=d, tn=D_FF // 2, tk=tk2, out_dtypes=(F32,), name="w_up_grad")
    mlp_sm = [d_w_up.reshape(d, N_CHIPS, D_FF // N_CHIPS).transpose(1, 0, 2), d_w_down.reshape(N_CHIPS, D_FF // N_CHIPS, d)]
    ride1 = _SiblingHalvesRider(mlp_sm) if where is not None else None
    dx1, dx1b, d_norm2_g, theirs = _mlp_up_bwd_norm(dup, w_up, x1, norm2_g, dx2, tr=tt, name="mlp_up_bwd_norm",
                                                    rider=ride1)
    ride2 = None
    if where is not None:
        mlp_pair = [_pair_sum(where, a, b, name=f"pair_sum_mlp{k}") for k, (a, b) in enumerate(zip(mlp_sm, theirs))]
        ride2 = _ChipExchangeRider(mlp_pair)

    (dmix,) = _mm(dx1b, w_out, "nt", tm=tr, tn=d, tk=d, out_dtypes=(BF16,), name="out_proj_bwd")
    (d_w_out,) = _mm(mix, dx1b, "tn", tm=d, tn=d, tk=tr, out_dtypes=(F32,), name="w_out_grad")
    do_dn, ddz, do_gla, dgr, d_dn_norm_g, d_gla_norm_g = _gnorm_bwd(
        dmix, o_dn, o_gla, projp, dn_norm_g, gla_norm_g, tr=tr, name="gated_norm_bwd")
    du, dw, dqg, dkd, dgl = _dn_scan_bwd(do_dn, w, qg, kd, vn, pmat, gl, hist, bsz=bsz, nc_seq=nc_seq,
                                          name="dn_scan_bwd")
    dqn, dkn, dv, dsa, d_alog, d_dtb, mlp_parts = _dn_intra_bwd(
        qn, kn, v, gates, alog_row, dtb_row, u, w, tmat, du, dw, dqg, dkd, do_dn, vn, dgl, nc_seq=nc_seq,
        name="dn_intra_bwd", rider=ride2)
    dz, d_conv_w = _dnprep_bwd_a(projp, conv_w, dqn, dkn, dv, bsz=bsz, t_seq=t_seq, tt=tt, name="dn_prep_bwd")
    dcin = _dnprep_bwd_b(dz, conv_w, bsz=bsz, t_seq=t_seq, tt=tt, name="conv_bwd")
    gdqg, gdkd, gdvi, gdgl = _gla_scan_bwd(do_gla, gqg, gkd, ggl, projp, ghist, bsz=bsz, nc_seq=nc_seq,
                                            name="gla_scan_bwd")
    dgqk, dgv, dsb, d_w2p, d_gla_b = _gla_intra_bwd(projp, gates, w2p, gla_b, do_gla, gdqg, gdkd, gdvi, gdgl,
                                                    nc_seq=nc_seq, name="gla_intra_bwd")

    secs = (dcin, ddz, dgqk, dgv, dgr, dsa, dsb)
    g_lo = _grad_tn(h, secs[0:2], tk=tr, name="w_in_grad_lo")
    g_hi = _grad_tn(h, secs[2:7], tk=tr, name="w_in_grad_hi")
    dh0, d_norm1_g = _inproj_bwd(secs, wp, h0, norm1_g, dx1, tr=tt, name="in_proj_bwd")
    dh0 = dh0.reshape(bsz, t_seq, d)
    grad_x = dh0[:, CHUNK:]
    d_meta_rows = dh0[:, N_PAD:CHUNK].reshape(bsz * N_META, d)

    grads = dict(w_in_lo=g_lo, w_in_hi=g_hi, w_out=d_w_out, w_up=d_w_up, w_down=d_w_down, meta_rows=d_meta_rows,
                 norm1_g=d_norm1_g, conv_w=d_conv_w, a_log_tile=d_alog, dt_bias_tile=d_dtb, dn_norm_g=d_dn_norm_g,
                 gla_w2=d_w2p[0:GLA_RANK], gla_b=d_gla_b, gla_norm_g=d_gla_norm_g, norm2_g=d_norm2_g,
                 final_norm_g=d_final_g, loss_tile=loss_tile)
    if where is not None:
        grads["mlp_exchanged"] = (mlp_pair, mlp_parts)
    return grad_x, grads


SHARD_W = IN_WIDTH // N_CHIPS
PADDED_ORDER = ((0, 2048), (2056, 3592), (2048, 2056), LANE - 8, (3592, 3608), LANE - GLA_RANK)


def _pad_layout(w_full):
    pieces = [jnp.zeros((w_full.shape[0], seg), w_full.dtype) if isinstance(seg, int) else w_full[:, seg[0]:seg[1]]
              for seg in PADDED_ORDER]
    return jnp.concatenate(pieces, axis=1)


def _padded_from_shards(stack):
    pieces = []
    for seg in PADDED_ORDER:
        if isinstance(seg, int):
            pieces.append(jnp.zeros((stack.shape[1], seg), stack.dtype))
            continue
        for j in range(N_CHIPS):
            lo, hi = max(seg[0], j * SHARD_W), min(seg[1], (j + 1) * SHARD_W)
            if lo < hi:
                pieces.append(stack[j, :, lo - j * SHARD_W:hi - j * SHARD_W])
    return jnp.concatenate(pieces, axis=1)


def _shards_from_padded(g_lo, g_hi):
    split = g_lo.shape[1]
    starts, pos = [], 0
    for seg in PADDED_ORDER:
        width = seg if isinstance(seg, int) else seg[1] - seg[0]
        if not isinstance(seg, int):
            starts.append((seg[0], seg[1], pos))
        pos += width
    shards = []
    for j in range(N_CHIPS):
        pieces = []
        for a, b, p0 in sorted(starts):
            lo, hi = max(a, j * SHARD_W), min(b, (j + 1) * SHARD_W)
            if lo < hi:
                src, off = (g_lo, 0) if p0 < split else (g_hi, split)
                pieces.append(src[:, p0 + lo - a - off:p0 + hi - a - off])
        pieces.append(jnp.zeros((g_lo.shape[0], D_MODEL - SHARD_W), g_lo.dtype))
        shards.append(jnp.concatenate(pieces, axis=1))
    return jnp.stack(shards)


def _pack_small(g, bsz):
    assert bsz * N_META == 32
    row = jnp.concatenate([g["a_log_tile"], g["dt_bias_tile"], g["dn_norm_g"], g["gla_norm_g"], g["gla_b"],
                           g["loss_tile"], jnp.zeros((1, LANE), F32)], axis=1)
    return jnp.concatenate([g["meta_rows"], g["norm1_g"], g["conv_w"].reshape(6, D_MODEL), row,
                            g["gla_w2"].reshape(4, D_MODEL), g["norm2_g"], g["final_norm_g"],
                            jnp.zeros((2, D_MODEL), F32)], axis=0)


def kernel(x, meta_tokens, norm1_g, w_in, conv_w, a_log, dt_bias, dn_norm_g, gla_w2, gla_b, gla_norm_g, w_out, norm2_g, w_up, w_down, final_norm_g, loss_target, m_meta_tokens, m_norm1_g, m_w_in, m_conv_w, m_a_log, m_dt_bias, m_dn_norm_g, m_gla_w2, m_gla_b, m_gla_norm_g, m_w_out, m_norm2_g, m_w_up, m_w_down, m_final_norm_g, v_meta_tokens, v_norm1_g, v_w_in, v_conv_w, v_a_log, v_dt_bias, v_dn_norm_g, v_gla_w2, v_gla_b, v_gla_norm_g, v_w_out, v_norm2_g, v_w_up, v_w_down, v_final_norm_g):
    bsz = x.shape[0]
    chip = 2 * lax.axis_index("x") + lax.axis_index("y")

    lane_pad = lambda a, wd: jnp.pad(a, ((0, 0), (0, wd - a.shape[1])))
    where = jnp.stack([lax.axis_index("c"), chip]).astype(jnp.int32)
    slot = lambda a, dt, nm: _to_slot(where, a, dt, name="slot_" + nm)
    early = _GatherRider([slot(lane_pad(w_in[0], D_MODEL), BF16, "w_in"), slot(meta_tokens, F32, "meta"),
                          slot(conv_w[0], F32, "conv"), slot(lane_pad(gla_w2[0], LANE), F32, "gla_w2")],
                         [True, False, False, False])
    g_in, g_meta, g_conv, g_w2 = _exchange_now(early, name="gather_early")
    late = (_GatherRider([slot(w_up[0], BF16, "w_up")], [True]), _GatherRider([slot(w_down[0], BF16, "w_down")], [True]),
            _GatherRider([slot(w_out[0], BF16, "w_out")], [True]))
    wp = _padded_from_shards(g_in)
    meta_f = g_meta.transpose(1, 0, 2).reshape(N_META, D_MODEL)
    conv_f = g_conv.transpose(1, 0, 2).reshape(4, QKV_W)
    w2_f = g_w2[:, :, 0:GQ_W // N_CHIPS].transpose(1, 0, 2).reshape(GLA_RANK, GQ_W)

    grad_x, g = _local_step(x, loss_target, meta_f, norm1_g, wp, conv_f, a_log, dt_bias, dn_norm_g, w2_f, gla_b,
                            gla_norm_g, None, norm2_g, None, None, final_norm_g.reshape(1, D_MODEL), late_gather=late, where=where)

    shard_major = [_shards_from_padded(g["w_in_lo"], g["w_in_hi"]), g["w_out"].reshape(N_CHIPS, D_MODEL // N_CHIPS, D_MODEL)]
    theirs = _exchange_now(_SiblingHalvesRider(shard_major), name="sibling_halves")
    pair = [_pair_sum(where, a, b, name=f"pair_sum_{k}") for k, (a, b) in enumerate(zip(shard_major, theirs))]
    parts = _exchange_now(_ChipExchangeRider(pair), name="chip_exchange")
    mlp_pair, mlp_parts = g["mlp_exchanged"]
    halves = [_chip_sum(where, p, q, name=f"chip_sum_{k}")
              for k, (p, q) in enumerate(zip(pair + mlp_pair, list(parts) + list(mlp_parts)))]
    gw_in, gw_out, gw_up, gw_down = _sibling_join(halves)

    red = _small_allreduce(_pack_small(g, bsz))
    g_meta_full = red[0:N_META]
    g_norm1 = red[32:33]
    g_conv_full = red[33:39].reshape(4, QKV_W)
    srow = red[39:40]
    g_alog, g_dtb = srow[:, 4:8], srow[:, LANE + 4:LANE + 8]
    g_dn_norm, g_gla_norm = srow[:, 2 * LANE:3 * LANE], srow[:, 3 * LANE:4 * LANE]
    g_gla_b = srow[:, 4 * LANE:6 * LANE]
    loss = srow[0, 6 * LANE]
    g_w2_full = red[40:44].reshape(GLA_RANK, GQ_W)
    g_norm2 = red[44:45]
    g_final = red[45:46]
    g_meta_sh = lax.dynamic_slice_in_dim(g_meta_full, chip * (D_MODEL // N_CHIPS), D_MODEL // N_CHIPS, axis=1)
    g_conv_sh = lax.dynamic_slice_in_dim(g_conv_full, chip * (QKV_W // N_CHIPS), QKV_W // N_CHIPS, axis=1)
    g_w2_sh = lax.dynamic_slice_in_dim(g_w2_full, chip * (GQ_W // N_CHIPS), GQ_W // N_CHIPS, axis=1)

    names = ["meta_tokens", "norm1_g", "w_in", "conv_w", "a_log", "dt_bias", "dn_norm_g", "gla_w2", "gla_b",
             "gla_norm_g", "w_out", "norm2_g", "w_up", "w_down", "final_norm_g"]
    weights = dict(meta_tokens=meta_tokens, norm1_g=norm1_g, w_in=w_in, conv_w=conv_w, a_log=a_log, dt_bias=dt_bias,
                   dn_norm_g=dn_norm_g, gla_w2=gla_w2, gla_b=gla_b, gla_norm_g=gla_norm_g, w_out=w_out,
                   norm2_g=norm2_g, w_up=w_up, w_down=w_down, final_norm_g=final_norm_g)
    ms = dict(meta_tokens=m_meta_tokens, norm1_g=m_norm1_g, w_in=m_w_in, conv_w=m_conv_w, a_log=m_a_log,
              dt_bias=m_dt_bias, dn_norm_g=m_dn_norm_g, gla_w2=m_gla_w2, gla_b=m_gla_b, gla_norm_g=m_gla_norm_g,
              w_out=m_w_out, norm2_g=m_norm2_g, w_up=m_w_up, w_down=m_w_down, final_norm_g=m_final_norm_g)
    vs = dict(meta_tokens=v_meta_tokens, norm1_g=v_norm1_g, w_in=v_w_in, conv_w=v_conv_w, a_log=v_a_log,
              dt_bias=v_dt_bias, dn_norm_g=v_dn_norm_g, gla_w2=v_gla_w2, gla_b=v_gla_b, gla_norm_g=v_gla_norm_g,
              w_out=v_w_out, norm2_g=v_norm2_g, w_up=v_w_up, w_down=v_w_down, final_norm_g=v_final_norm_g)
    grads2d = dict(meta_tokens=g_meta_sh, norm1_g=g_norm1, w_in=gw_in, conv_w=g_conv_sh, a_log=g_alog, dt_bias=g_dtb,
                   dn_norm_g=g_dn_norm, gla_w2=g_w2_sh, gla_b=g_gla_b, gla_norm_g=g_gla_norm, w_out=gw_out,
                   norm2_g=g_norm2, w_up=gw_up, w_down=gw_down, final_norm_g=g_final)
    out_g, out_d, out_m, out_v = [], [], [], []
    for nm in names:
        shape = weights[nm].shape
        g2 = grads2d[nm]
        if nm == "w_in":
            tview = lambda a: jnp.transpose(a, (2, 0, 1))
            res = _adamw_rows(tview(weights[nm]), g2[:, 0:SHARD_W].T.reshape(SHARD_W, 1, D_MODEL), tview(ms[nm]),
                              tview(vs[nm]), name=f"adamw_{nm}")
            res = [jnp.transpose(a, (1, 2, 0)) for a in res]
            gout = res[3]
        elif len(shape) == 3:
            res = _adamw(weights[nm], g2, ms[nm], vs[nm], name=f"adamw_{nm}")
            gout = g2.reshape(shape)
        else:
            as2d = lambda a: a.reshape(g2.shape)
            res = _adamw(as2d(weights[nm]), g2, as2d(ms[nm]), as2d(vs[nm]), name=f"adamw_{nm}")
            gout = g2.reshape(shape)
        out_g.append(gout)
        out_d.append(res[0].reshape(shape))
        out_m.append(res[1].reshape(shape))
        out_v.append(res[2].reshape(shape))
    return (loss, grad_x, *out_g, *out_d, *out_m, *out_v)
```

```python
import functools

import jax
import jax.numpy as jnp
import numpy as np
from jax import lax
from jax.experimental import pallas as pl
from jax.experimental.pallas import tpu as pltpu

F32 = jnp.float32
BF16 = jnp.bfloat16
HI = lax.Precision.HIGHEST
MESH = pl.DeviceIdType.MESH

D_MODEL = 1024
N_META = 16
CHUNK = 64
N_PAD = CHUNK - N_META
NH = 4
DN_D = 128
GLA_DK = 64
GLA_DV = 128
GLA_RANK = 16
D_FF = 4 * D_MODEL
EPS = 1e-6
IN_WIDTH = 3608
C_QKV, C_DZ, C_GQK, C_GV, C_GR, C_SA, C_SB, PW = 0, 1536, 2048, 2560, 3072, 3584, 3712, 3840
LANE = 128
N_CHIPS = 4

ADAM_LR, ADAM_B1, ADAM_B2, ADAM_EPS, ADAM_WD, ADAM_STEP = 0.001, 0.9, 0.999, 1e-08, 0.01, 10

VMEM_BIG = 56 * 1024 * 1024


def _cp(vmem=None, sem=None):
    kw = {}
    if vmem is not None:
        kw["vmem_limit_bytes"] = vmem
    if sem is not None:
        kw["dimension_semantics"] = sem
    return pltpu.CompilerParams(**kw)


def _tile(n, target, mult=16):
    best = None
    for t in range(mult, min(n, target) + 1, mult):
        if n % t == 0:
            best = t
    assert best is not None, (n, target)
    return best


def _dot(a, b, dims, prec=None):
    return lax.dot_general(a, b, (dims, ((), ())), preferred_element_type=F32, precision=prec)


def _nn(a, b):
    return _dot(a.astype(BF16), b.astype(BF16), ((1,), (0,)))


def _nt(a, b):
    return _dot(a.astype(BF16), b.astype(BF16), ((1,), (1,)))


def _tn(a, b):
    return _dot(a.astype(BF16), b.astype(BF16), ((0,), (0,)))


def _split(x):
    hi = x.astype(BF16)
    return hi, (x - hi.astype(F32)).astype(BF16)


def _tri_sum(tri, x):
    t = tri.astype(BF16)
    hi = x.astype(BF16)
    r1 = x - hi.astype(F32)
    mid = r1.astype(BF16)
    lo = (r1 - mid.astype(F32)).astype(BF16)
    nn = ((1,), (0,))
    return _dot(t, hi, nn) + _dot(t, mid, nn) + _dot(t, lo, nn)


def _sigmoid(x):
    return 0.5 * jnp.tanh(0.5 * x) + 0.5


def _softplus(x):
    return jnp.maximum(x, 0.0) + jnp.log(1.0 + jnp.exp(-jnp.abs(x)))


def _logsigmoid(x):
    return -_softplus(-x)


def _iota2(shape, dim):
    return lax.broadcasted_iota(jnp.int32, shape, dim)


def _mm(a, b, mode, *, tm, tn, tk, out_dtypes, extras=(), epilogue=None, name, vmem=VMEM_BIG, rider=None,
        out_widths=None, n_chunk=None):
    if mode == "tn":
        K, M = a.shape
    else:
        M, K = a.shape
    N = b.shape[0] if mode == "nt" else b.shape[1]
    assert M % tm == 0 and N % tn == 0 and K % tk == 0, (name, M, N, K, tm, tn, tk)
    nk = K // tk
    n_ex, n_out = len(extras), len(out_dtypes)
    if mode == "tn":
        a_spec = pl.BlockSpec((tk, tm), lambda i, j, k: (k, i))
    else:
        a_spec = pl.BlockSpec((tm, tk), lambda i, j, k: (i, k))
    if mode == "nt":
        b_spec = pl.BlockSpec((tn, tk), lambda i, j, k: (j, k))
    else:
        b_spec = pl.BlockSpec((tk, tn), lambda i, j, k: (k, j))
    mn_spec = pl.BlockSpec((tm, tn), lambda i, j, k: (i, j))
    if out_widths is None:
        o_specs = [mn_spec] * n_out
        o_shapes = [jax.ShapeDtypeStruct((M, N), dt) for dt in out_dtypes]
    else:
        assert tn == N
        o_specs = [pl.BlockSpec((tm, wd), lambda i, j, k: (i, 0)) for wd in out_widths]
        o_shapes = [jax.ShapeDtypeStruct((M, wd), dt) for wd, dt in zip(out_widths, out_dtypes)]
    dims = {"nn": ((1,), (0,)), "nt": ((1,), (1,)), "tn": ((0,), (0,))}[mode]

    single = nk == 1
    direct = (not single) and epilogue is None and n_out == 1 and out_dtypes[0] == F32

    def body(*refs):
        a_ref, b_ref = refs[0], refs[1]
        ex_refs = refs[2:2 + n_ex]
        out_refs = refs[2 + n_ex:2 + n_ex + n_out]
        if n_chunk is not None:
            assert single and out_widths is None and mode != "tn" and tn % n_chunk == 0
            av = a_ref[...].astype(BF16)
            for j in range(tn // n_chunk):
                cols = slice(j * n_chunk, (j + 1) * n_chunk)
                bv = b_ref[cols, :] if mode == "nt" else b_ref[:, cols]
                acc = _dot(av, bv.astype(BF16), dims)
                res = (acc,) if epilogue is None else epilogue(acc, *[e[:, cols] for e in ex_refs])
                for o_ref, r in zip(out_refs, res):
                    o_ref[:, cols] = r.astype(o_ref.dtype)
            return
        part = _dot(a_ref[...].astype(BF16), b_ref[...].astype(BF16), dims)

        def finish(acc):
            res = (acc,) if epilogue is None else epilogue(acc, *[e[...] for e in ex_refs])
            for o_ref, r in zip(out_refs, res):
                o_ref[...] = r.astype(o_ref.dtype)

        if single:
            finish(part)
            return
        acc_ref = out_refs[0] if direct else refs[2 + n_ex + n_out]
        k = pl.program_id(2)

        @pl.when(k == 0)
        def _():
            acc_ref[...] = part

        @pl.when(k > 0)
        def _():
            acc_ref[...] += part

        if not direct:
            @pl.when(k == nk - 1)
            def _():
                finish(acc_ref[...])

    outs, ridden = _hosted_call(
        body, rider, name=name, grid=(M // tm, N // tn, nk),
        in_specs=[a_spec, b_spec] + [mn_spec] * n_ex,
        out_specs=o_specs, out_shape=o_shapes,
        scratch_shapes=[] if (single or direct) else [pltpu.VMEM((tm, tn), F32)],
        compiler_params=_cp(vmem, ("parallel", "parallel", "arbitrary")), args=(a, b, *extras))
    return tuple(outs) if rider is None else (tuple(outs), ridden)


def _grad_tn(a, secs, *, tk, name):
    kk, m = a.shape
    widths = [s.shape[1] for s in secs]
    total = sum(widths)
    nk = kk // tk

    def body(*refs):
        a_ref, sec_refs, o_ref = refs[0], refs[1:-1], refs[-1]
        cat = sec_refs[0][...] if len(sec_refs) == 1 else jnp.concatenate([s[...] for s in sec_refs], axis=1)
        part = _dot(a_ref[...].astype(BF16), cat.astype(BF16), ((0,), (0,)))
        k = pl.program_id(0)

        @pl.when(k == 0)
        def _():
            o_ref[...] = part

        @pl.when(k > 0)
        def _():
            o_ref[...] += part

    return pl.pallas_call(
        body, name=name, grid=(nk,),
        in_specs=[pl.BlockSpec((tk, m), lambda k: (k, 0))] + [pl.BlockSpec((tk, w), lambda k: (k, 0)) for w in widths],
        out_specs=pl.BlockSpec((m, total), lambda k: (0, 0)),
        out_shape=jax.ShapeDtypeStruct((m, total), F32),
        compiler_params=_cp(VMEM_BIG, ("arbitrary",)),
    )(a, *secs)


class _ShiftedRows:
    def __init__(self, src, buf, sems, *, per_seq, tt, steps):
        self.src, self.buf, self.sems = src, buf, sems
        self.per_seq, self.tt, self.steps = per_seq, tt, steps

    def _do(self, step, slot, act):
        b, j = step // self.per_seq, step % self.per_seq
        tt = self.tt

        @pl.when(j == 0)
        def _():
            act(pltpu.make_async_copy(self.src.at[b, pl.ds(0, tt - CHUNK), :],
                                      self.buf.at[slot, pl.ds(CHUNK, tt - CHUNK), :], self.sems.at[slot]))

        if self.per_seq > 1:
            @pl.when(j > 0)
            def _():
                act(pltpu.make_async_copy(self.src.at[b, pl.ds(j * tt - CHUNK, tt), :], self.buf.at[slot],
                                          self.sems.at[slot]))

    def tile(self, i):
        slot = i % 2

        @pl.when(i == 0)
        def _():
            self._do(i, slot, lambda cp: cp.start())

        self._do(i, slot, lambda cp: cp.wait())

        @pl.when(i + 1 < self.steps)
        def _():
            self._do(i + 1, 1 - slot, lambda cp: cp.start())

        return slot


def _embed_norm(x, lead, g, *, tr, name):
    bsz, s_len, d = x.shape
    t_seq = s_len + CHUNK
    per_seq = t_seq // tr
    steps = bsz * per_seq
    n = bsz * t_seq

    def body(x_hbm, lead_ref, g_ref, h0_ref, h_ref, buf, sems):
        i = pl.program_id(0)
        slot = _ShiftedRows(x_hbm, buf, sems, per_seq=per_seq, tt=tr, steps=steps).tile(i)

        @pl.when(i % per_seq == 0)
        def _():
            buf[slot, 0:CHUNK, :] = lead_ref[...]

        xv = buf[slot]
        h0_ref[...] = xv
        r = lax.rsqrt(jnp.mean(xv * xv, axis=-1, keepdims=True) + EPS)
        h_ref[...] = (xv * r * g_ref[...]).astype(h_ref.dtype)

    row = pl.BlockSpec((tr, d), lambda i: (i, 0))
    return pl.pallas_call(
        body, name=name, grid=(steps,),
        in_specs=[ANY, pl.BlockSpec((CHUNK, d), lambda i: (0, 0)), pl.BlockSpec((1, d), lambda i: (0, 0))],
        out_specs=[row, row],
        out_shape=[jax.ShapeDtypeStruct((n, d), F32), jax.ShapeDtypeStruct((n, d), BF16)],
        scratch_shapes=[pltpu.VMEM((2, tr, d), F32), pltpu.SemaphoreType.DMA((2,))],
        compiler_params=_cp(VMEM_BIG, ("arbitrary",)),
    )(x, lead, g)


def _rms_fwd(x, g, *, tr, name):
    n, d = x.shape

    def body(x_ref, g_ref, o_ref):
        xv = x_ref[...]
        r = lax.rsqrt(jnp.mean(xv * xv, axis=-1, keepdims=True) + EPS)
        o_ref[...] = (xv * r * g_ref[...]).astype(o_ref.dtype)

    return pl.pallas_call(
        body, name=name, grid=(n // tr,),
        in_specs=[pl.BlockSpec((tr, d), lambda i: (i, 0)), pl.BlockSpec((1, d), lambda i: (0, 0))],
        out_specs=pl.BlockSpec((tr, d), lambda i: (i, 0)),
        out_shape=jax.ShapeDtypeStruct((n, d), BF16),
        compiler_params=_cp(VMEM_BIG),
    )(x, g)


def _rms_bwd_math(xv, g, dy):
    r = lax.rsqrt(jnp.mean(xv * xv, axis=-1, keepdims=True) + EPS)
    xh = xv * r
    gdy = dy * g
    dx = r * (gdy - xh * jnp.mean(xh * gdy, axis=-1, keepdims=True))
    return dx, jnp.sum(dy * xh, axis=0, keepdims=True)


def _mlp_up_bwd_norm(dup, w_up, x, g, res, *, tr, name, rider=None):
    n, d = x.shape
    ff = dup.shape[1]

    def body(dup_ref, w_ref, x_ref, g_ref, res_ref, o_ref, ob_ref, dg_ref):
        dh = _nt(dup_ref[...], w_ref[...])
        dx, dg = _rms_bwd_math(x_ref[...], g_ref[...], dh)
        tot = res_ref[...] + dx
        o_ref[...] = tot
        ob_ref[...] = tot.astype(BF16)

        @pl.when(pl.program_id(0) == 0)
        def _():
            dg_ref[...] = dg

        @pl.when(pl.program_id(0) > 0)
        def _():
            dg_ref[...] += dg

    row = pl.BlockSpec((tr, d), lambda i: (i, 0))
    vec = pl.BlockSpec((1, d), lambda i: (0, 0))
    outs, ridden = _hosted_call(
        body, rider, name=name, grid=(n // tr,),
        in_specs=[pl.BlockSpec((tr, ff), lambda i: (i, 0)), pl.BlockSpec((d, ff), lambda i: (0, 0)), row, vec, row],
        out_specs=[row, row, vec],
        out_shape=[jax.ShapeDtypeStruct((n, d), F32), jax.ShapeDtypeStruct((n, d), BF16),
                   jax.ShapeDtypeStruct((1, d), F32)],
        scratch_shapes=[], compiler_params=_cp(VMEM_BIG, ("arbitrary",)), args=(dup, w_up, x, g, res))
    return (*outs, ridden)


def _mlp_down_loss(act, w_down, x1, gf, tgt, *, t_seq, tr, name):
    n, d = x1.shape
    ff = act.shape[1]
    per_seq = t_seq // tr
    steps = n // tr

    def body(a_ref, w_ref, x_ref, g_ref, t_hbm, dx_ref, dxb_ref, dg_ref, loss_ref, tbuf, tsems):
        i = pl.program_id(0)
        slot = _ShiftedRows(t_hbm, tbuf, tsems, per_seq=per_seq, tt=tr, steps=steps).tile(i)

        @pl.when(i % per_seq == 0)
        def _():
            tbuf[slot, 0:CHUNK, :] = jnp.zeros((CHUNK, d), F32)

        t_ref = tbuf.at[slot]
        xv = x_ref[...] + _nn(a_ref[...], w_ref[...])
        g = g_ref[...]
        r = lax.rsqrt(jnp.mean(xv * xv, axis=-1, keepdims=True) + EPS)
        xh = xv * r
        pos = (i % per_seq) * tr + _iota2((tr, 1), 0)
        real = pos >= CHUNK
        err = jnp.where(real, xh * g - t_ref[...], 0.0)
        dy = err * (1.0 / d)
        gdy = dy * g
        dx = r * (gdy - xh * jnp.mean(xh * gdy, axis=-1, keepdims=True))
        dx_ref[...] = dx
        dxb_ref[...] = dx.astype(BF16)
        dg = jnp.sum(dy * xh, axis=0, keepdims=True)
        ls = 0.5 * jnp.sum(jnp.mean(err * err, axis=-1, keepdims=True), axis=0, keepdims=True)
        ls = jnp.where(_iota2((1, LANE), 1) == 0, ls, 0.0)

        @pl.when(i == 0)
        def _():
            dg_ref[...] = dg
            loss_ref[...] = ls

        @pl.when(i > 0)
        def _():
            dg_ref[...] += dg
            loss_ref[...] += ls

    row = pl.BlockSpec((tr, d), lambda i: (i, 0))
    vec = pl.BlockSpec((1, d), lambda i: (0, 0))
    one = pl.BlockSpec((1, LANE), lambda i: (0, 0))
    return pl.pallas_call(
        body, name=name, grid=(n // tr,),
        in_specs=[pl.BlockSpec((tr, ff), lambda i: (i, 0)), pl.BlockSpec((ff, d), lambda i: (0, 0)), row, vec, ANY],
        out_specs=[row, row, vec, one],
        out_shape=[jax.ShapeDtypeStruct((n, d), F32), jax.ShapeDtypeStruct((n, d), BF16),
                   jax.ShapeDtypeStruct((1, d), F32), jax.ShapeDtypeStruct((1, LANE), F32)],
        scratch_shapes=[pltpu.VMEM((2, tr, d), F32), pltpu.SemaphoreType.DMA((2,))],
        compiler_params=_cp(VMEM_BIG, ("arbitrary",)),
    )(act, w_down, x1, gf, tgt)


def _gnorm_fwd(o_dn, o_gla, projp, g_dn, g_gla, *, tr, name):
    n = o_dn.shape[0]
    w = NH * DN_D

    def body(odn_ref, ogl_ref, z_ref, r_ref, gdn_ref, ggl_ref, mix_ref):
        for grp, (o_ref, gate_ref, gain_ref) in enumerate(((odn_ref, z_ref, gdn_ref), (ogl_ref, r_ref, ggl_ref))):
            gain = gain_ref[...]
            for h in range(NH):
                sl = slice(h * DN_D, (h + 1) * DN_D)
                o = o_ref[:, sl].astype(F32)
                z = gate_ref[:, sl].astype(F32)
                r = lax.rsqrt(jnp.mean(o * o, axis=-1, keepdims=True) + EPS)
                y = (o * r * gain) * (z * _sigmoid(z))
                mix_ref[:, grp * w + h * DN_D: grp * w + (h + 1) * DN_D] = y.astype(mix_ref.dtype)

    row = pl.BlockSpec((tr, w), lambda i: (i, 0))
    vec = pl.BlockSpec((1, DN_D), lambda i: (0, 0))
    return pl.pallas_call(
        body, name=name, grid=(n // tr,),
        in_specs=[row, row, pl.BlockSpec((tr, w), lambda i: (i, C_DZ // w)),
                  pl.BlockSpec((tr, w), lambda i: (i, C_GR // w)), vec, vec],
        out_specs=pl.BlockSpec((tr, 2 * w), lambda i: (i, 0)),
        out_shape=jax.ShapeDtypeStruct((n, 2 * w), BF16),
        compiler_params=_cp(VMEM_BIG),
    )(o_dn, o_gla, projp, projp, g_dn, g_gla)


def _gnorm_bwd(dmix, o_dn, o_gla, projp, g_dn, g_gla, *, tr, name):
    n = o_dn.shape[0]
    w = NH * DN_D

    def body(dm_ref, odn_ref, ogl_ref, z_ref, r_ref, gdn_ref, ggl_ref,
             dodn_ref, ddz_ref, dogl_ref, dgr_ref, dgdn_ref, dggl_ref):
        first = pl.program_id(0) == 0
        groups = ((odn_ref, z_ref, gdn_ref, dodn_ref, ddz_ref, dgdn_ref),
                  (ogl_ref, r_ref, ggl_ref, dogl_ref, dgr_ref, dggl_ref))
        for grp, (o_ref, gate_ref, gain_ref, do_ref, dgate_ref, dgain_ref) in enumerate(groups):
            gain = gain_ref[...]
            dgain = jnp.zeros((1, DN_D), F32)
            for h in range(NH):
                sl = slice(h * DN_D, (h + 1) * DN_D)
                o = o_ref[:, sl].astype(F32)
                z = gate_ref[:, sl].astype(F32)
                dm = dm_ref[:, grp * w + h * DN_D: grp * w + (h + 1) * DN_D].astype(F32)
                r = lax.rsqrt(jnp.mean(o * o, axis=-1, keepdims=True) + EPS)
                oh = o * r
                s = _sigmoid(z)
                dn = dm * (z * s)
                dgate_ref[:, sl] = (dm * (oh * gain) * (s * (1.0 + z * (1.0 - s)))).astype(dgate_ref.dtype)
                gdn = dn * gain
                do_ref[:, sl] = (r * (gdn - oh * jnp.mean(oh * gdn, axis=-1, keepdims=True))).astype(do_ref.dtype)
                dgain = dgain + jnp.sum(dn * oh, axis=0, keepdims=True)

            @pl.when(first)
            def _():
                dgain_ref[...] = dgain

            @pl.when(jnp.logical_not(first))
            def _():
                dgain_ref[...] += dgain

    row = pl.BlockSpec((tr, w), lambda i: (i, 0))
    vec = pl.BlockSpec((1, DN_D), lambda i: (0, 0))
    big = jax.ShapeDtypeStruct((n, w), F32)
    gate = jax.ShapeDtypeStruct((n, w), BF16)
    small = jax.ShapeDtypeStruct((1, DN_D), F32)
    return pl.pallas_call(
        body, name=name, grid=(n // tr,),
        in_specs=[pl.BlockSpec((tr, 2 * w), lambda i: (i, 0)), row, row,
                  pl.BlockSpec((tr, w), lambda i: (i, C_DZ // w)), pl.BlockSpec((tr, w), lambda i: (i, C_GR // w)), vec, vec],
        out_specs=[row, row, row, row, vec, vec],
        out_shape=[gate, gate, gate, gate, small, small],
        compiler_params=_cp(VMEM_BIG),
    )(dmix, o_dn, o_gla, projp, projp, g_dn, g_gla)


QKV_W = 3 * NH * DN_D
HALO = 8


def _conv_z(xs_ref, cw_ref, tt):
    z = cw_ref[0:1, :] * xs_ref[pl.ds(HALO - 3, tt), :]
    for j in range(1, 4):
        z = z + cw_ref[j:j + 1, :] * xs_ref[pl.ds(HALO - 3 + j, tt), :]
    return z


def _dnprep_fwd(projp, conv_w, *, bsz, t_seq, tt, name):
    n = bsz * t_seq
    per_seq = t_seq // tt
    hw = NH * DN_D

    def body(x_ref, halo_ref, cw_ref, q_ref, k_ref, v_ref, xs_ref):
        i = pl.program_id(1)
        xs_ref[0:HALO, :] = jnp.where(i == 0, 0.0, halo_ref[...].astype(F32))
        xs_ref[HALO:HALO + tt, :] = x_ref[...].astype(F32)
        z = _conv_z(xs_ref, cw_ref, tt)
        a = z * _sigmoid(z)
        for grp, o_ref in enumerate((q_ref, k_ref)):
            for h in range(NH):
                ah = a[:, grp * hw + h * DN_D: grp * hw + (h + 1) * DN_D]
                rs = lax.rsqrt(jnp.sum(ah * ah, axis=-1, keepdims=True) + EPS)
                o_ref[:, h * DN_D:(h + 1) * DN_D] = (ah * rs).astype(o_ref.dtype)
        v_ref[...] = a[:, 2 * hw:3 * hw].astype(v_ref.dtype)

    def halo_map(b, i):
        return (jnp.maximum((b * t_seq + i * tt) // HALO - 1, 0), 0)

    out = pl.BlockSpec((tt, hw), lambda b, i: (b * per_seq + i, 0))
    sds = jax.ShapeDtypeStruct((n, hw), BF16)
    return pl.pallas_call(
        body, name=name, grid=(bsz, per_seq),
        in_specs=[pl.BlockSpec((tt, QKV_W), lambda b, i: (b * per_seq + i, 0)),
                  pl.BlockSpec((HALO, QKV_W), halo_map),
                  pl.BlockSpec((4, QKV_W), lambda b, i: (0, 0))],
        out_specs=[out, out, out], out_shape=[sds, sds, sds],
        scratch_shapes=[pltpu.VMEM((tt + HALO, QKV_W), F32)],
        compiler_params=_cp(VMEM_BIG),
    )(projp, projp, conv_w)


def _dnprep_bwd_a(projp, conv_w, dq, dk, dv, *, bsz, t_seq, tt, name):
    n = bsz * t_seq
    per_seq = t_seq // tt
    hw = NH * DN_D

    def body(x_ref, halo_ref, cw_ref, dq_ref, dk_ref, dv_ref, dz_ref, dcw_ref, xs_ref):
        b, i = pl.program_id(0), pl.program_id(1)
        xs_ref[0:HALO, :] = jnp.where(i == 0, 0.0, halo_ref[...].astype(F32))
        xs_ref[HALO:HALO + tt, :] = x_ref[...].astype(F32)
        z = _conv_z(xs_ref, cw_ref, tt)
        s = _sigmoid(z)
        a = z * s
        dsilu = s * (1.0 + z * (1.0 - s))
        for grp, d_ref in enumerate((dq_ref, dk_ref)):
            for h in range(NH):
                sl = slice(grp * hw + h * DN_D, grp * hw + (h + 1) * DN_D)
                ah = a[:, sl]
                rs = lax.rsqrt(jnp.sum(ah * ah, axis=-1, keepdims=True) + EPS)
                y = ah * rs
                dy = d_ref[:, h * DN_D:(h + 1) * DN_D]
                da = rs * (dy - y * jnp.sum(dy * y, axis=-1, keepdims=True))
                dz_ref[:, sl] = da * dsilu[:, sl]
        dz_ref[:, 2 * hw:3 * hw] = dv_ref[...] * dsilu[:, 2 * hw:3 * hw]
        dz = dz_ref[...]
        first = jnp.logical_and(b == 0, i == 0)
        for j in range(4):
            part = jnp.sum(dz * xs_ref[pl.ds(HALO - 3 + j, tt), :], axis=0, keepdims=True)

            @pl.when(first)
            def _():
                dcw_ref[j:j + 1, :] = part

            @pl.when(jnp.logical_not(first))
            def _():
                dcw_ref[j:j + 1, :] += part

    def halo_map(b, i):
        return (jnp.maximum((b * t_seq + i * tt) // HALO - 1, 0), 0)

    hrow = pl.BlockSpec((tt, hw), lambda b, i: (b * per_seq + i, 0))
    return pl.pallas_call(
        body, name=name, grid=(bsz, per_seq),
        in_specs=[pl.BlockSpec((tt, QKV_W), lambda b, i: (b * per_seq + i, 0)),
                  pl.BlockSpec((HALO, QKV_W), halo_map),
                  pl.BlockSpec((4, QKV_W), lambda b, i: (0, 0)), hrow, hrow, hrow],
        out_specs=[pl.BlockSpec((tt, QKV_W), lambda b, i: (b * per_seq + i, 0)),
                   pl.BlockSpec((4, QKV_W), lambda b, i: (0, 0))],
        out_shape=[jax.ShapeDtypeStruct((n, QKV_W), F32), jax.ShapeDtypeStruct((4, QKV_W), F32)],
        scratch_shapes=[pltpu.VMEM((tt + HALO, QKV_W), F32)],
        compiler_params=_cp(VMEM_BIG),
    )(projp, projp, conv_w, dq, dk, dv)


def _dnprep_bwd_b(dz, conv_w, *, bsz, t_seq, tt, name):
    n = bsz * t_seq
    per_seq = t_seq // tt
    last_blk = n // HALO - 1

    def body(dz_ref, halo_ref, cw_ref, dx_ref, ds_ref):
        i = pl.program_id(1)
        ds_ref[0:tt, :] = dz_ref[...].astype(F32)
        ds_ref[tt:tt + HALO, :] = jnp.where(i == per_seq - 1, 0.0, halo_ref[...].astype(F32))
        dx = cw_ref[0:1, :] * ds_ref[pl.ds(3, tt), :]
        for j in range(1, 4):
            dx = dx + cw_ref[j:j + 1, :] * ds_ref[pl.ds(3 - j, tt), :]
        dx_ref[...] = dx.astype(dx_ref.dtype)

    def halo_map(b, i):
        return (jnp.minimum((b * t_seq + (i + 1) * tt) // HALO, last_blk), 0)

    row = pl.BlockSpec((tt, QKV_W), lambda b, i: (b * per_seq + i, 0))
    return pl.pallas_call(
        body, name=name, grid=(bsz, per_seq),
        in_specs=[row, pl.BlockSpec((HALO, QKV_W), halo_map), pl.BlockSpec((4, QKV_W), lambda b, i: (0, 0))],
        out_specs=row, out_shape=jax.ShapeDtypeStruct((n, QKV_W), BF16),
        scratch_shapes=[pltpu.VMEM((tt + HALO, QKV_W), F32)],
        compiler_params=_cp(VMEM_BIG),
    )(dz, dz, conv_w)


def _masks64():
    r = _iota2((CHUNK, CHUNK), 0)
    c = _iota2((CHUNK, CHUNK), 1)
    return r, c


def _group(nc_seq, target=5):
    return max(g for g in range(1, target + 1) if nc_seq % g == 0)


def _round_robin(chains):
    live = list(chains)
    while live:
        nxt = []
        for ch in live:
            try:
                next(ch)
                nxt.append(ch)
            except StopIteration:
                pass
        live = nxt
        yield


def _run(chains):
    for _ in _round_robin(chains):
        pass


def _per_chunk(inner, kinds, grp):
    def body(*refs):
        chains = []
        for gi in range(grp):
            views = []
            for r, kind in zip(refs, kinds):
                if kind == "row":
                    views.append(r.at[pl.ds(gi * CHUNK, CHUNK)])
                elif kind == "lead":
                    views.append(r.at[pl.ds(gi, 1)])
                else:
                    views.append(r)
            chains.append(inner(gi, *views))
        _run(chains)
    return body


def _accumulate(ref, val, gi):
    if gi > 0:
        ref[...] += val
        return
    first = pl.program_id(0) == 0

    @pl.when(first)
    def _():
        ref[...] = val

    @pl.when(jnp.logical_not(first))
    def _():
        ref[...] += val


ANY = pl.BlockSpec(memory_space=pl.ANY)


def _place():
    return lax.axis_index("x"), lax.axis_index("y"), lax.axis_index("c")


def _other_chips(x, y):
    return [(1 - x, y, 2 * (1 - x) + y), (x, 1 - y, 2 * x + 1 - y), (1 - x, 1 - y, 2 * (1 - x) + 1 - y)]


class _GatherRider:
    def __init__(self, bufs, split):
        self.inputs = list(bufs)
        self.split = list(split)
        self.out_shapes = [jax.ShapeDtypeStruct(b.shape, b.dtype) for b in bufs]
        self.aliases = {i: i for i in range(len(bufs))}
        self.sems = [pltpu.SemaphoreType.DMA((len(bufs), 3))] * 4

    def _rows(self, k, buf, c, mine=True):
        r = buf.shape[1]
        if not self.split[k]:
            return pl.ds(0, r)
        return pl.ds((c if mine else 1 - c) * (r // 2), r // 2)

    def _ici(self, k, d, bufs, sems, c, px, py, block):
        rows = self._rows(k, bufs[k], c)
        return pltpu.make_async_remote_copy(
            src_ref=bufs[k].at[block, rows, :], dst_ref=bufs[k].at[block, rows, :], send_sem=sems[0].at[k, d],
            recv_sem=sems[1].at[k, d], device_id=(px, py, c), device_id_type=MESH)

    def _pass(self, k, d, bufs, sems, x, y, c, block, mine):
        rows = self._rows(k, bufs[k], c, mine)
        return pltpu.make_async_remote_copy(
            src_ref=bufs[k].at[block, rows, :], dst_ref=bufs[k].at[block, rows, :], send_sem=sems[2].at[k, d],
            recv_sem=sems[3].at[k, d], device_id=(x, y, 1 - c), device_id_type=MESH)

    def first(self, in_refs, bufs, sems):
        x, y, c = _place()
        for k in range(len(bufs)):
            for d, (px, py, _) in enumerate(_other_chips(x, y)):
                self._ici(k, d, bufs, sems, c, px, py, 2 * x + y).start()

    def last(self, in_refs, bufs, sems):
        x, y, c = _place()
        chips = _other_chips(x, y)
        for k in range(len(bufs)):
            for d, (px, py, pj) in enumerate(chips):
                self._ici(k, d, bufs, sems, c, px, py, pj).wait_recv()
                if self.split[k]:
                    self._pass(k, d, bufs, sems, x, y, c, pj, True).start()
        for k in range(len(bufs)):
            for d, (px, py, pj) in enumerate(chips):
                if self.split[k]:
                    self._pass(k, d, bufs, sems, x, y, c, pj, False).wait_recv()
                    self._pass(k, d, bufs, sems, x, y, c, pj, True).wait_send()
                self._ici(k, d, bufs, sems, c, px, py, 2 * x + y).wait_send()


def _hosted_call(body, rider, *, name, grid, in_specs, out_specs, out_shape, scratch_shapes, compiler_params, args):
    if rider is None:
        outs = pl.pallas_call(body, name=name, grid=grid, in_specs=in_specs, out_specs=out_specs, out_shape=out_shape,
                              scratch_shapes=scratch_shapes, compiler_params=compiler_params)(*args)
        return list(outs), []
    n_in, n_out, n_scr = len(in_specs), len(out_specs), len(scratch_shapes)
    r_in, r_out = len(rider.inputs), len(rider.out_shapes)
    compiler_params = _cp(compiler_params.vmem_limit_bytes, ("arbitrary",) * len(grid))

    def full_body(*refs):
        ins = refs[:n_in]
        rins = refs[n_in:n_in + r_in]
        outs = refs[n_in + r_in:n_in + r_in + n_out]
        routs = refs[n_in + r_in + n_out:n_in + r_in + n_out + r_out]
        rest = refs[n_in + r_in + n_out + r_out:]
        scr, sems = rest[:n_scr], rest[n_scr:]
        ids = [pl.program_id(a) for a in range(len(grid))]
        is_first = functools.reduce(jnp.logical_and, [i == 0 for i in ids])
        is_last = functools.reduce(jnp.logical_and, [i == g - 1 for i, g in zip(ids, grid)])

        @pl.when(is_first)
        def _():
            rider.first(rins, routs, sems)

        body(*ins, *outs, *scr)

        @pl.when(is_last)
        def _():
            rider.last(rins, routs, sems)

    res = pl.pallas_call(
        full_body, name=name, grid=grid, in_specs=list(in_specs) + [ANY] * r_in,
        out_specs=list(out_specs) + [ANY] * r_out, out_shape=list(out_shape) + list(rider.out_shapes),
        input_output_aliases={n_in + i: n_out + o for i, o in rider.aliases.items()},
        scratch_shapes=list(scratch_shapes) + list(rider.sems), compiler_params=compiler_params,
    )(*args, *rider.inputs)
    return list(res[:n_out]), list(res[n_out:])


def _exchange_now(rider, *, name):
    r_in = len(rider.inputs)

    def body(*refs):
        rins = refs[:r_in]
        routs = refs[r_in:r_in + len(rider.out_shapes)]
        sems = refs[r_in + len(rider.out_shapes):]
        rider.first(rins, routs, sems)
        rider.last(rins, routs, sems)

    return pl.pallas_call(
        body, name=name, in_specs=[ANY] * r_in, out_specs=[ANY] * len(rider.out_shapes), out_shape=list(rider.out_shapes),
        input_output_aliases=dict(rider.aliases), scratch_shapes=list(rider.sems),
    )(*rider.inputs)


def _to_slot(where, a, dtype, *, name):
    r, cols = a.shape
    tr = _tile(r, 256, 16) if r > 256 else r

    def body(w_ref, a_ref, o_ref):
        o_ref[0] = a_ref[...].astype(o_ref.dtype)

    return pl.pallas_call(
        body, name=name,
        grid_spec=pltpu.PrefetchScalarGridSpec(
            num_scalar_prefetch=1, grid=(r // tr,),
            in_specs=[pl.BlockSpec((tr, cols), lambda i, w: (i, 0))],
            out_specs=pl.BlockSpec((1, tr, cols), lambda i, w: (w[1], i, 0))),
        out_shape=jax.ShapeDtypeStruct((N_CHIPS, r, cols), dtype), compiler_params=_cp(VMEM_BIG),
    )(where, a)


def _tri_inv(a_strict):
    r, c = _masks64()
    eye = (r == c).astype(F32)
    blk16 = (r // 16) == (c // 16)
    blk32 = (r // 32) == (c // 32)
    ld = jnp.where(blk16, a_strict, 0.0)
    x = eye - ld
    p = _nn(ld, ld)
    yield
    for step in range(3):
        xp = _nn(x, p)
        if step < 2:
            p = _nn(p, p)
        x = x + xp
        yield
    for lk in (jnp.where(jnp.logical_and(blk32, jnp.logical_not(blk16)), a_strict, 0.0),
               jnp.where(blk32, 0.0, a_strict)):
        y = x - eye
        s = lk + _nn(y, lk)
        yield
        x = x - s - _nn(s, y)
        yield
    return x


def _dn_gates(sa, alog, dtb, chunk_in_seq):
    rows = _iota2((CHUNK, LANE), 0)
    valid = jnp.logical_or(rows >= N_PAD, chunk_in_seq > 0)
    beta_t = _sigmoid(sa)
    ea = jnp.exp(alog)
    g_t = jnp.where(valid, -ea * _softplus(sa + dtb), 0.0)
    r, c = _masks64()
    ltri = (r >= c).astype(F32)
    gam_t = _tri_sum(ltri, g_t)
    return beta_t, g_t, gam_t, valid, ea


def _dn_intra_fwd(qn, kn, v, projp, alog_row, dtb_row, *, nc_seq, name, rider=None):
    n = qn.shape[0]
    nct = n // CHUNK
    hw = NH * DN_D
    scale = DN_D ** -0.5

    grp = _group(nc_seq)

    def inner(gi, q_ref, k_ref, v_ref, sa_ref, al_ref, dt_ref, u_ref, w_ref, qg_ref, kd_ref, p_ref, t_ref, gl_ref):
        ci = (pl.program_id(0) * grp + gi) % nc_seq
        beta_t, _, gam_t, _, _ = _dn_gates(sa_ref[...], al_ref[...], dt_ref[...], ci)
        yield
        gam_tt = gam_t.T
        r, c = _masks64()
        incl = r >= c
        strict = r > c

        def head(h):
            sl = slice(h * DN_D, (h + 1) * DN_D)
            beta_w = jnp.broadcast_to(beta_t[:, h:h + 1], (CHUNK, DN_D))
            gam_w = jnp.broadcast_to(gam_t[:, 4 + h:5 + h], (CHUNK, DN_D))
            gam_row = gam_tt[4 + h:5 + h, :]
            gl = gam_t[CHUNK - 1:CHUNK, 4 + h:5 + h]
            dec = jnp.exp(jnp.where(incl, gam_w[:, 0:CHUNK] - gam_row, -jnp.inf))
            kh = k_ref[:, sl].astype(F32)
            qh = q_ref[:, sl].astype(F32) * scale
            vh = v_ref[:, sl].astype(F32)
            kk = _nt(kh, kh)
            qk = _nt(qh, kh)
            yield
            a = jnp.where(strict, beta_w[:, 0:CHUNK] * kk * dec, 0.0)
            tm = yield from _tri_inv(a)
            egam_w = jnp.exp(gam_w)
            u_ref[:, sl] = _nn(tm, beta_w * vh).astype(u_ref.dtype)
            w_ref[:, sl] = _nn(tm, (beta_w * egam_w) * kh).astype(w_ref.dtype)
            qg_ref[:, sl] = (egam_w * qh).astype(qg_ref.dtype)
            kd_ref[:, sl] = (jnp.exp(gl - gam_w) * kh).astype(kd_ref.dtype)
            p_ref[0, h] = qk * dec
            t_ref[0, h] = tm
            gl_ref[0, h:h + 1, :] = jnp.broadcast_to(jnp.exp(gl), (1, LANE))

        yield from _round_robin([head(h) for h in range(NH)])

    rows = grp * CHUNK
    row = pl.BlockSpec((rows, hw), lambda i: (i, 0))
    vec = pl.BlockSpec((1, LANE), lambda i: (0, 0))
    mat = pl.BlockSpec((grp, NH, CHUNK, CHUNK), lambda i: (i, 0, 0, 0))
    big = jax.ShapeDtypeStruct((n, hw), BF16)
    msd = jax.ShapeDtypeStruct((nct, NH, CHUNK, CHUNK), F32)
    kinds = ["row"] * 4 + ["whole"] * 2 + ["row"] * 4 + ["lead"] * 3
    outs, ridden = _hosted_call(
        _per_chunk(inner, kinds, grp), rider, name=name, grid=(nct // grp,),
        in_specs=[row, row, row, pl.BlockSpec((rows, LANE), lambda i: (i, 0)), vec, vec],
        out_specs=[row, row, row, row, mat, mat, pl.BlockSpec((grp, NH, LANE), lambda i: (i, 0, 0))],
        out_shape=[big, big, big, big, msd, msd, jax.ShapeDtypeStruct((nct, NH, LANE), F32)],
        scratch_shapes=[], compiler_params=_cp(VMEM_BIG, ("arbitrary",)),
        args=(qn, kn, v, projp, alog_row, dtb_row))
    return (*outs, ridden)


def _dn_scan_fwd(u, w, qg, kd, p, gl, *, bsz, nc_seq, name, rider=None):
    hw = NH * DN_D
    t_seq = nc_seq * CHUNK
    u, w, qg, kd = (z.reshape(bsz, t_seq, hw) for z in (u, w, qg, kd))
    p = p.reshape(bsz, nc_seq, NH, CHUNK, CHUNK)
    gl = gl.reshape(bsz, nc_seq, NH, LANE)
    grp = _group(nc_seq)

    def body(u_ref, w_ref, qg_ref, kd_ref, p_ref, gl_ref, o_ref, vn_ref, hist_ref, s_ref):
        @pl.when(pl.program_id(0) == 0)
        def _():
            s_ref[...] = jnp.zeros_like(s_ref)

        def chain(b, h, gi):
            sl = slice(h * DN_D, (h + 1) * DN_D)
            rows = pl.ds(gi * CHUNK, CHUNK)
            s = s_ref[b, h]
            hist_ref[b, gi, h] = s.astype(hist_ref.dtype)
            ws = _nn(w_ref[b, rows, sl], s)
            qs = _nn(qg_ref[b, rows, sl], s)
            yield
            vn = u_ref[b, rows, sl] - ws
            vn_ref[b, rows, sl] = vn.astype(vn_ref.dtype)
            o_ref[b, rows, sl] = (qs + _nn(p_ref[b, gi, h], vn)).astype(o_ref.dtype)
            s_ref[b, h] = gl_ref[b, gi, h:h + 1, :] * s + _tn(kd_ref[b, rows, sl], vn)

        for gi in range(grp):
            _run([chain(b, h, gi) for b in range(bsz) for h in range(NH)])

    row = pl.BlockSpec((bsz, grp * CHUNK, hw), lambda i: (0, i, 0))
    outs, ridden = _hosted_call(
        body, rider, name=name, grid=(nc_seq // grp,),
        in_specs=[row, row, row, row, pl.BlockSpec((bsz, grp, NH, CHUNK, CHUNK), lambda i: (0, i, 0, 0, 0)),
                  pl.BlockSpec((bsz, grp, NH, LANE), lambda i: (0, i, 0, 0))],
        out_specs=[row, row, pl.BlockSpec((bsz, grp, NH, DN_D, DN_D), lambda i: (0, i, 0, 0, 0))],
        out_shape=[jax.ShapeDtypeStruct((bsz, t_seq, hw), BF16), jax.ShapeDtypeStruct((bsz, t_seq, hw), BF16),
                   jax.ShapeDtypeStruct((bsz, nc_seq, NH, DN_D, DN_D), F32)],
        scratch_shapes=[pltpu.VMEM((bsz, NH, DN_D, DN_D), F32)],
        compiler_params=_cp(VMEM_BIG, ("arbitrary",)), args=(u, w, qg, kd, p, gl))
    o, vn, hist = outs
    return o.reshape(bsz * t_seq, hw), vn.reshape(bsz * t_seq, hw), hist, ridden


def _dn_scan_bwd(do, w, qg, kd, vn, p, gl, hist, *, bsz, nc_seq, name):
    hw = NH * DN_D
    t_seq = nc_seq * CHUNK
    do, w, qg, kd, vn = (z.reshape(bsz, t_seq, hw) for z in (do, w, qg, kd, vn))
    p = p.reshape(bsz, nc_seq, NH, CHUNK, CHUNK)
    gl = gl.reshape(bsz, nc_seq, NH, LANE)
    grp = _group(nc_seq)

    def body(do_ref, w_ref, qg_ref, kd_ref, vn_ref, p_ref, gl_ref, hist_ref,
             du_ref, dw_ref, dqg_ref, dkd_ref, dgl_ref, ds_ref):
        @pl.when(pl.program_id(0) == 0)
        def _():
            ds_ref[...] = jnp.zeros_like(ds_ref)

        def chain(b, h, gi):
            sl = slice(h * DN_D, (h + 1) * DN_D)
            rows = pl.ds(gi * CHUNK, CHUNK)
            s = hist_ref[b, gi, h]
            dsn = ds_ref[b, h]
            doh = do_ref[b, rows, sl]
            vnh = vn_ref[b, rows, sl]
            kdh = kd_ref[b, rows, sl]
            dvn = _tn(p_ref[b, gi, h], doh) + _nn(kdh, dsn)
            du_ref[b, rows, sl] = dvn.astype(du_ref.dtype)
            dqg_ref[b, rows, sl] = _nt(doh, s).astype(dqg_ref.dtype)
            dkd_ref[b, rows, sl] = _nt(vnh, dsn).astype(dkd_ref.dtype)
            ds_part = _tn(qg_ref[b, rows, sl], doh) + gl_ref[b, gi, h:h + 1, :] * dsn
            dgl = jnp.sum(jnp.sum(dsn * s, axis=0, keepdims=True), axis=1, keepdims=True)
            dgl_ref[b, gi, h:h + 1, :] = jnp.broadcast_to(dgl, (1, LANE))
            yield
            dw_ref[b, rows, sl] = (-_nt(dvn, s)).astype(dw_ref.dtype)
            ds_ref[b, h] = ds_part - _tn(w_ref[b, rows, sl], dvn)

        for gi in reversed(range(grp)):
            _run([chain(b, h, gi) for b in range(bsz) for h in range(NH)])

    steps = nc_seq // grp
    rev = lambda i: steps - 1 - i
    row = pl.BlockSpec((bsz, grp * CHUNK, hw), lambda i: (0, rev(i), 0))
    mat = pl.BlockSpec((bsz, grp, NH, CHUNK, CHUNK), lambda i: (0, rev(i), 0, 0, 0))
    glb = pl.BlockSpec((bsz, grp, NH, LANE), lambda i: (0, rev(i), 0, 0))
    big = jax.ShapeDtypeStruct((bsz, t_seq, hw), BF16)
    outs = pl.pallas_call(
        body, name=name, grid=(steps,),
        in_specs=[row, row, row, row, row, mat, glb,
                  pl.BlockSpec((bsz, grp, NH, DN_D, DN_D), lambda i: (0, rev(i), 0, 0, 0))],
        out_specs=[row, row, row, row, glb],
        out_shape=[big, big, jax.ShapeDtypeStruct(big.shape, F32), jax.ShapeDtypeStruct(big.shape, F32),
                   jax.ShapeDtypeStruct((bsz, nc_seq, NH, LANE), F32)],
        scratch_shapes=[pltpu.VMEM((bsz, NH, DN_D, DN_D), F32)],
        compiler_params=_cp(VMEM_BIG, ("arbitrary",)),
    )(do, w, qg, kd, vn, p, gl, hist)
    du, dw, dqg, dkd, dgl = outs
    n = bsz * t_seq
    return (du.reshape(n, hw), dw.reshape(n, hw), dqg.reshape(n, hw), dkd.reshape(n, hw),
            dgl.reshape(bsz * nc_seq, NH, LANE))


def _dn_intra_bwd(qn, kn, v, projp, alog_row, dtb_row, u, w, tmat, du, dw, dqg, dkd, do, vn, dgl, *, nc_seq, name,
                  rider=None):
    n = qn.shape[0]
    nct = n // CHUNK
    hw = NH * DN_D
    scale = DN_D ** -0.5

    grp = _group(nc_seq)

    def inner(gi, q_ref, k_ref, v_ref, sa_ref, al_ref, dt_ref, u_ref, w_ref, t_ref, du_ref, dw_ref, dqg_ref, dkd_ref,
              do_ref, vn_ref, dgl_ref, dq_ref, dk_ref, dv_ref, dsa_ref, dal_ref, ddt_ref):
        ci = (pl.program_id(0) * grp + gi) % nc_seq
        sa = sa_ref[...]
        beta_t, g_t, gam_t, valid, ea = _dn_gates(sa, al_ref[...], dt_ref[...], ci)
        yield
        lane = _iota2((CHUNK, LANE), 1)
        gates_t = jnp.where(lane < 4, beta_t, gam_t).T
        r, c = _masks64()
        incl, strict, upper, supper = r >= c, r > c, r <= c, r < c
        rows1 = _iota2((CHUNK, 1), 0)
        acc = [jnp.zeros((CHUNK, LANE), F32)]

        def head(h):
            sl = slice(h * DN_D, (h + 1) * DN_D)
            beta_w = jnp.broadcast_to(beta_t[:, h:h + 1], (CHUNK, DN_D))
            gam_w = jnp.broadcast_to(gam_t[:, 4 + h:5 + h], (CHUNK, DN_D))
            beta_s, gam_s = beta_w[:, 0:CHUNK], gam_w[:, 0:CHUNK]
            beta_row = gates_t[h:h + 1, :]
            gam_row = gates_t[4 + h:5 + h, :]
            gl = gam_t[CHUNK - 1:CHUNK, 4 + h:5 + h]
            dec = jnp.exp(jnp.where(incl, gam_s - gam_row, -jnp.inf))
            dec_t = jnp.exp(jnp.where(upper, gam_row - gam_s, -jnp.inf))
            egam_w = jnp.exp(gam_w)
            ekd_w = jnp.exp(gl - gam_w)
            kh = k_ref[:, sl].astype(F32)
            qh = q_ref[:, sl].astype(F32) * scale
            vh = v_ref[:, sl].astype(F32)
            uh = u_ref[:, sl]
            wh = w_ref[:, sl]
            doh = do_ref[:, sl]
            vnh = vn_ref[:, sl]
            kk = _nt(kh, kh)
            qk = _nt(qh, kh)
            qk_t = _nt(kh, qh)
            dp = _nt(doh, vnh)
            dp_t = _nt(vnh, doh)
            t_hi, t_lo = _split(t_ref[0, h].T)
            duh, dwh = du_ref[:, sl], dw_ref[:, sl]
            dvb = _nn(t_hi, duh) + _nn(t_lo, duh)
            dkg = _nn(t_hi, dwh) + _nn(t_lo, dwh)
            yield
            dvb_hi, dvb_lo = _split(dvb)
            dkg_hi, dkg_lo = _split(dkg)
            m = (_nt(dvb_hi, uh) + _nt(dvb_lo, uh)) + (_nt(dkg_hi, wh) + _nt(dkg_lo, wh))
            m_t = (_nt(uh, dvb_hi) + _nt(uh, dvb_lo)) + (_nt(wh, dkg_hi) + _nt(wh, dkg_lo))
            yield
            da = jnp.where(strict, -m, 0.0)
            da_t = jnp.where(supper, -m_t, 0.0)
            a = jnp.where(strict, beta_s * kk * dec, 0.0)
            a_t = jnp.where(supper, beta_row * kk * dec_t, 0.0)
            dad = da * dec
            dad_t = da_t * dec_t
            dpm = jnp.where(incl, dp, 0.0)
            dpm_t = jnp.where(upper, dp_t, 0.0)
            e = da * a + dpm * (qk * dec)
            e_t = da_t * a_t + dpm_t * (qk_t * dec_t)
            dqgh = dqg_ref[:, sl].astype(F32)
            dkdh = dkd_ref[:, sl].astype(F32)
            bg_w = beta_w * egam_w
            dkh = (_nn(beta_s * dad, kh) + _nn(beta_row * dad_t, kh) + _nn(dpm_t * dec_t, qh)
                   + bg_w * dkg + ekd_w * dkdh)
            dqh = _nn(dpm * dec, kh) + egam_w * dqgh
            t_kd = dkdh * (ekd_w * kh)
            dbeta = (jnp.sum(dad * kk, axis=1, keepdims=True)
                     + jnp.sum(dkg * (egam_w * kh) + dvb * vh, axis=1, keepdims=True))
            dgam = (jnp.sum(e - e_t, axis=1, keepdims=True)
                    + jnp.sum(dkg * (bg_w * kh) + dqgh * (egam_w * qh) - t_kd, axis=1, keepdims=True))
            dgam_last = (jnp.sum(jnp.sum(t_kd, axis=0, keepdims=True), axis=1, keepdims=True)
                         + dgl_ref[0, h:h + 1, 0:1] * jnp.exp(gl))
            dgam = dgam + jnp.where(rows1 == CHUNK - 1, dgam_last, 0.0)
            dq_ref[:, sl] = (dqh * scale).astype(dq_ref.dtype)
            dk_ref[:, sl] = dkh.astype(dk_ref.dtype)
            dv_ref[:, sl] = (beta_w * dvb).astype(dv_ref.dtype)
            acc[0] = acc[0] + jnp.where(lane == h, dbeta, 0.0) + jnp.where(lane == 4 + h, dgam, 0.0)

        yield from _round_robin([head(h) for h in range(NH)])
        acc_t = acc[0]
        dg_t = _tri_sum(upper, acc_t)
        ddb = acc_t * beta_t * (1.0 - beta_t)
        dda = jnp.where(valid, dg_t * (-ea) * _sigmoid(sa + dt_ref[...]), 0.0)
        dsa_ref[...] = jnp.where(lane < 4, ddb, jnp.where(lane < 8, dda, 0.0)).astype(dsa_ref.dtype)
        in_g = jnp.logical_and(lane >= 4, lane < 8)
        dal = jnp.sum(jnp.where(in_g, dg_t * g_t, 0.0), axis=0, keepdims=True)
        ddt = jnp.sum(jnp.where(in_g, dda, 0.0), axis=0, keepdims=True)
        _accumulate(dal_ref, dal, gi)
        _accumulate(ddt_ref, ddt, gi)

    rows = grp * CHUNK
    row = pl.BlockSpec((rows, hw), lambda i: (i, 0))
    vec = pl.BlockSpec((1, LANE), lambda i: (0, 0))
    mat = pl.BlockSpec((grp, NH, CHUNK, CHUNK), lambda i: (i, 0, 0, 0))
    glb = pl.BlockSpec((grp, NH, LANE), lambda i: (i, 0, 0))
    big = jax.ShapeDtypeStruct((n, hw), F32)
    v128 = jax.ShapeDtypeStruct((1, LANE), F32)
    kinds = (["row"] * 4 + ["whole"] * 2 + ["row"] * 2 + ["lead"] + ["row"] * 6 + ["lead"]
             + ["row"] * 4 + ["whole"] * 2)
    outs, ridden = _hosted_call(
        _per_chunk(inner, kinds, grp), rider, name=name, grid=(nct // grp,),
        in_specs=[row, row, row, pl.BlockSpec((rows, LANE), lambda i: (i, 0)), vec, vec,
                  row, row, mat, row, row, row, row, row, row, glb],
        out_specs=[row, row, row, pl.BlockSpec((rows, LANE), lambda i: (i, 0)), vec, vec],
        out_shape=[big, big, big, jax.ShapeDtypeStruct((n, LANE), BF16), v128, v128],
        scratch_shapes=[], compiler_params=_cp(VMEM_BIG, ("arbitrary",)),
        args=(qn, kn, v, projp, alog_row, dtb_row, u, w, tmat, du, dw, dqg, dkd, do, vn, dgl))
    return (*outs, ridden)


GQ_W = NH * GLA_DK
GV_W = NH * GLA_DV
GLA_NORM = 16.0
MID = CHUNK // 2


def _gla_gates(sb, w2p, gb, chunk_in_seq):
    rows = _iota2((CHUNK, GQ_W), 0)
    valid = jnp.logical_or(rows >= N_PAD, chunk_in_seq > 0)
    graw = _nn(sb, w2p) + gb
    yield
    g = jnp.where(valid, _logsigmoid(graw) * (1.0 / GLA_NORM), 0.0)
    r, c = _masks64()
    bcum = _tri_sum(r >= c, g)
    yield
    return graw, bcum, valid


def _head_mask(h):
    lane = _iota2((1, GQ_W), 1)
    return jnp.logical_and(lane >= h * GLA_DK, lane < (h + 1) * GLA_DK)


def _gla_intra_fwd(projp, gates, w2p, gb, *, nc_seq, name):
    n = projp.shape[0]
    nct = n // CHUNK
    scale = GLA_DK ** -0.5

    grp = _group(nc_seq)
    rows = grp * CHUNK

    def inner(gi, qk_ref, v_ref, sb_ref, w2_ref, gb_ref, oi_ref, qg_ref, kd_ref, gl_ref):
        ci = (pl.program_id(0) * grp + gi) % nc_seq
        _, bc, _ = yield from _gla_gates(sb_ref[...], w2_ref[...], gb_ref[...], ci)
        bref = bc[MID:MID + 1, :]
        bl = bc[CHUNK - 1:CHUNK, :]
        q = qk_ref[:, 0:GQ_W].astype(F32) * scale
        k = qk_ref[:, GQ_W:2 * GQ_W].astype(F32)
        qi = q * jnp.exp(bc - bref)
        ki = k * jnp.exp(bref - bc)
        qg_ref[...] = (q * jnp.exp(bc)).astype(qg_ref.dtype)
        kd_ref[...] = (k * jnp.exp(bl - bc)).astype(kd_ref.dtype)
        gl_ref[0] = jnp.exp(bl)
        r, c = _masks64()
        incl = r >= c
        a = [jnp.where(incl, _nt(jnp.where(_head_mask(h), qi, 0.0), ki), 0.0) for h in range(NH)]
        yield
        for h in range(NH):
            oi_ref[:, h * GLA_DV:(h + 1) * GLA_DV] = _nn(a[h], v_ref[:, h * GLA_DV:(h + 1) * GLA_DV]).astype(oi_ref.dtype)

    kinds = ["row"] * 3 + ["whole"] * 2 + ["row"] * 3 + ["lead"]
    return pl.pallas_call(
        _per_chunk(inner, kinds, grp), name=name, grid=(nct // grp,),
        in_specs=[pl.BlockSpec((rows, 2 * GQ_W), lambda i: (i, C_GQK // (2 * GQ_W))),
                  pl.BlockSpec((rows, GV_W), lambda i: (i, C_GV // GV_W)),
                  pl.BlockSpec((rows, LANE), lambda i: (i, 1)),
                  pl.BlockSpec((LANE, GQ_W), lambda i: (0, 0)), pl.BlockSpec((1, GQ_W), lambda i: (0, 0))],
        out_specs=[pl.BlockSpec((rows, GV_W), lambda i: (i, 0)), pl.BlockSpec((rows, GQ_W), lambda i: (i, 0)),
                   pl.BlockSpec((rows, GQ_W), lambda i: (i, 0)), pl.BlockSpec((grp, 1, GQ_W), lambda i: (i, 0, 0))],
        out_shape=[jax.ShapeDtypeStruct((n, GV_W), BF16), jax.ShapeDtypeStruct((n, GQ_W), BF16),
                   jax.ShapeDtypeStruct((n, GQ_W), BF16), jax.ShapeDtypeStruct((nct, 1, GQ_W), F32)],
        compiler_params=_cp(VMEM_BIG),
    )(projp, projp, gates, w2p, gb)


def _gla_scan_fwd(oi, qg, kd, gl, projp, *, bsz, nc_seq, name, rider=None):
    t_seq = nc_seq * CHUNK
    oi = oi.reshape(bsz, t_seq, GV_W)
    qg, kd = qg.reshape(bsz, t_seq, GQ_W), kd.reshape(bsz, t_seq, GQ_W)
    gl = gl.reshape(bsz, nc_seq, 1, GQ_W)
    pj = projp.reshape(bsz, t_seq, PW)
    grp = _group(nc_seq)

    def body(oi_ref, qg_ref, kd_ref, gl_ref, v_ref, o_ref, hist_ref, st_ref):
        @pl.when(pl.program_id(0) == 0)
        def _():
            st_ref[...] = jnp.zeros_like(st_ref)

        for gi in range(grp):
            rows = pl.ds(gi * CHUNK, CHUNK)
            for b in range(bsz):
                st = st_ref[b]
                hist_ref[b, gi] = st.astype(hist_ref.dtype)
                qgb = qg_ref[b, rows, :]
                kdb = kd_ref[b, rows, :]
                upd = jnp.zeros((GLA_DV, GQ_W), F32)
                for h in range(NH):
                    sl = slice(h * GLA_DV, (h + 1) * GLA_DV)
                    m = _head_mask(h)
                    o_ref[b, rows, sl] = (oi_ref[b, rows, sl] + _nt(jnp.where(m, qgb, 0.0), st)).astype(o_ref.dtype)
                    upd = upd + jnp.where(m, _tn(v_ref[b, rows, sl], kdb), 0.0)
                st_ref[b] = gl_ref[b, gi] * st + upd

    rws = grp * CHUNK
    outs, ridden = _hosted_call(
        body, rider, name=name, grid=(nc_seq // grp,),
        in_specs=[pl.BlockSpec((bsz, rws, GV_W), lambda i: (0, i, 0)),
                  pl.BlockSpec((bsz, rws, GQ_W), lambda i: (0, i, 0)),
                  pl.BlockSpec((bsz, rws, GQ_W), lambda i: (0, i, 0)),
                  pl.BlockSpec((bsz, grp, 1, GQ_W), lambda i: (0, i, 0, 0)),
                  pl.BlockSpec((bsz, rws, GV_W), lambda i: (0, i, C_GV // GV_W))],
        out_specs=[pl.BlockSpec((bsz, rws, GV_W), lambda i: (0, i, 0)),
                   pl.BlockSpec((bsz, grp, GLA_DV, GQ_W), lambda i: (0, i, 0, 0))],
        out_shape=[jax.ShapeDtypeStruct((bsz, t_seq, GV_W), BF16),
                   jax.ShapeDtypeStruct((bsz, nc_seq, GLA_DV, GQ_W), F32)],
        scratch_shapes=[pltpu.VMEM((bsz, GLA_DV, GQ_W), F32)],
        compiler_params=_cp(VMEM_BIG, ("arbitrary",)), args=(oi, qg, kd, gl, pj))
    return outs[0].reshape(bsz * t_seq, GV_W), outs[1], ridden


def _gla_scan_bwd(do, qg, kd, gl, projp, hist, *, bsz, nc_seq, name):
    t_seq = nc_seq * CHUNK
    do = do.reshape(bsz, t_seq, GV_W)
    qg, kd = qg.reshape(bsz, t_seq, GQ_W), kd.reshape(bsz, t_seq, GQ_W)
    gl = gl.reshape(bsz, nc_seq, 1, GQ_W)
    pj = projp.reshape(bsz, t_seq, PW)
    grp = _group(nc_seq)

    def body(do_ref, qg_ref, kd_ref, gl_ref, v_ref, hist_ref, dqg_ref, dkd_ref, dv_ref, dgl_ref, dst_ref):
        @pl.when(pl.program_id(0) == 0)
        def _():
            dst_ref[...] = jnp.zeros_like(dst_ref)

        for gi in reversed(range(grp)):
            rows = pl.ds(gi * CHUNK, CHUNK)
            for b in range(bsz):
                st = hist_ref[b, gi]
                dst = dst_ref[b]
                qgb = qg_ref[b, rows, :]
                kdb = kd_ref[b, rows, :]
                dqg = jnp.zeros((CHUNK, GQ_W), F32)
                dkd = jnp.zeros((CHUNK, GQ_W), F32)
                add = jnp.zeros((GLA_DV, GQ_W), F32)
                for h in range(NH):
                    sl = slice(h * GLA_DV, (h + 1) * GLA_DV)
                    m = _head_mask(h)
                    doh = do_ref[b, rows, sl]
                    vh = v_ref[b, rows, sl]
                    dqg = dqg + jnp.where(m, _nn(doh, st), 0.0)
                    dkd = dkd + jnp.where(m, _nn(vh, dst), 0.0)
                    dv_ref[b, rows, sl] = _nt(jnp.where(m, kdb, 0.0), dst).astype(dv_ref.dtype)
                    add = add + jnp.where(m, _tn(doh, qgb), 0.0)
                dqg_ref[b, rows, :] = dqg.astype(dqg_ref.dtype)
                dkd_ref[b, rows, :] = dkd.astype(dkd_ref.dtype)
                dgl_ref[b, gi] = jnp.sum(dst * st, axis=0, keepdims=True)
                dst_ref[b] = gl_ref[b, gi] * dst + add

    steps = nc_seq // grp
    rws = grp * CHUNK
    rev = lambda i: steps - 1 - i
    outs = pl.pallas_call(
        body, name=name, grid=(steps,),
        in_specs=[pl.BlockSpec((bsz, rws, GV_W), lambda i: (0, rev(i), 0)),
                  pl.BlockSpec((bsz, rws, GQ_W), lambda i: (0, rev(i), 0)),
                  pl.BlockSpec((bsz, rws, GQ_W), lambda i: (0, rev(i), 0)),
                  pl.BlockSpec((bsz, grp, 1, GQ_W), lambda i: (0, rev(i), 0, 0)),
                  pl.BlockSpec((bsz, rws, GV_W), lambda i: (0, rev(i), C_GV // GV_W)),
                  pl.BlockSpec((bsz, grp, GLA_DV, GQ_W), lambda i: (0, rev(i), 0, 0))],
        out_specs=[pl.BlockSpec((bsz, rws, GQ_W), lambda i: (0, rev(i), 0)),
                   pl.BlockSpec((bsz, rws, GQ_W), lambda i: (0, rev(i), 0)),
                   pl.BlockSpec((bsz, rws, GV_W), lambda i: (0, rev(i), 0)),
                   pl.BlockSpec((bsz, grp, 1, GQ_W), lambda i: (0, rev(i), 0, 0))],
        out_shape=[jax.ShapeDtypeStruct((bsz, t_seq, GQ_W), F32), jax.ShapeDtypeStruct((bsz, t_seq, GQ_W), F32),
                   jax.ShapeDtypeStruct((bsz, t_seq, GV_W), BF16), jax.ShapeDtypeStruct((bsz, nc_seq, 1, GQ_W), F32)],
        scratch_shapes=[pltpu.VMEM((bsz, GLA_DV, GQ_W), F32)],
        compiler_params=_cp(VMEM_BIG, ("arbitrary",)),
    )(do, qg, kd, gl, pj, hist)
    n = bsz * t_seq
    return (outs[0].reshape(n, GQ_W), outs[1].reshape(n, GQ_W), outs[2].reshape(n, GV_W),
            outs[3].reshape(bsz * nc_seq, 1, GQ_W))


def _gla_intra_bwd(projp, gates, w2p, gb, do, dqg, dkd, dvi, dgl, *, nc_seq, name):
    n = projp.shape[0]
    nct = n // CHUNK
    scale = GLA_DK ** -0.5

    grp = _group(nc_seq)
    rows = grp * CHUNK

    def inner(gi, qk_ref, v_ref, sb_ref, w2_ref, gb_ref, do_ref, dqg_ref, dkd_ref, dvi_ref, dgl_ref,
              dqk_ref, dv_ref, dsb_ref, dw2_ref, dgb_ref):
        ci = (pl.program_id(0) * grp + gi) % nc_seq
        sb = sb_ref[...]
        w2 = w2_ref[...]
        graw, bc, valid = yield from _gla_gates(sb, w2, gb_ref[...], ci)
        bref = bc[MID:MID + 1, :]
        bl = bc[CHUNK - 1:CHUNK, :]
        q = qk_ref[:, 0:GQ_W].astype(F32) * scale
        k = qk_ref[:, GQ_W:2 * GQ_W].astype(F32)
        ex1 = jnp.exp(bc - bref)
        ex2 = jnp.exp(bref - bc)
        eb = jnp.exp(bc)
        ekd = jnp.exp(bl - bc)
        qi, ki = q * ex1, k * ex2
        r, c = _masks64()
        incl = r >= c
        upper = r <= c
        a_t, da, da_t = [], [], []
        for h in range(NH):
            sl = slice(h * GLA_DV, (h + 1) * GLA_DV)
            doh = do_ref[:, sl]
            vh = v_ref[:, sl]
            a_t.append(jnp.where(upper, _nt(jnp.where(_head_mask(h), ki, 0.0), qi), 0.0))
            da.append(jnp.where(incl, _nt(doh, vh), 0.0))
            da_t.append(jnp.where(upper, _nt(vh, doh), 0.0))
        yield
        dqi = jnp.zeros((CHUNK, GQ_W), F32)
        dki = jnp.zeros((CHUNK, GQ_W), F32)
        for h in range(NH):
            sl = slice(h * GLA_DV, (h + 1) * GLA_DV)
            m = _head_mask(h)
            dv_ref[:, sl] = (_nn(a_t[h], do_ref[:, sl]) + dvi_ref[:, sl]).astype(dv_ref.dtype)
            dqi = dqi + jnp.where(m, _nn(da[h], ki), 0.0)
            dki = dki + jnp.where(m, _nn(da_t[h], qi), 0.0)
        yield
        dqg = dqg_ref[...].astype(F32)
        dkd = dkd_ref[...].astype(F32)
        dqk_ref[:, 0:GQ_W] = ((dqi * ex1 + dqg * eb) * scale).astype(dqk_ref.dtype)
        dqk_ref[:, GQ_W:2 * GQ_W] = (dki * ex2 + dkd * ekd).astype(dqk_ref.dtype)
        t_qi, t_ki, t_kd = dqi * qi, dki * ki, dkd * (k * ekd)
        db = t_qi - t_ki + dqg * (q * eb) - t_kd
        dbref = jnp.sum(t_ki - t_qi, axis=0, keepdims=True)
        dbl = jnp.sum(t_kd, axis=0, keepdims=True) + dgl_ref[0] * jnp.exp(bl)
        rows = _iota2((CHUNK, GQ_W), 0)
        db = db + jnp.where(rows == MID, dbref, 0.0) + jnp.where(rows == CHUNK - 1, dbl, 0.0)
        dg = _tri_sum(upper, db)
        yield
        dgraw = jnp.where(valid, dg * (1.0 / GLA_NORM) * _sigmoid(-graw), 0.0)
        dsb_ref[...] = _nt(dgraw, w2).astype(dsb_ref.dtype)
        dw2 = _tn(sb, dgraw)
        dgb = jnp.sum(dgraw, axis=0, keepdims=True)
        _accumulate(dw2_ref, dw2, gi)
        _accumulate(dgb_ref, dgb, gi)

    rq = pl.BlockSpec((rows, GQ_W), lambda i: (i, 0))
    rv = pl.BlockSpec((rows, GV_W), lambda i: (i, 0))
    kinds = ["row"] * 3 + ["whole"] * 2 + ["row"] * 4 + ["lead"] + ["row"] * 3 + ["whole"] * 2
    return pl.pallas_call(
        _per_chunk(inner, kinds, grp), name=name, grid=(nct // grp,),
        in_specs=[pl.BlockSpec((rows, 2 * GQ_W), lambda i: (i, C_GQK // (2 * GQ_W))),
                  pl.BlockSpec((rows, GV_W), lambda i: (i, C_GV // GV_W)),
                  pl.BlockSpec((rows, LANE), lambda i: (i, 1)),
                  pl.BlockSpec((LANE, GQ_W), lambda i: (0, 0)), pl.BlockSpec((1, GQ_W), lambda i: (0, 0)),
                  rv, rq, rq, rv, pl.BlockSpec((grp, 1, GQ_W), lambda i: (i, 0, 0))],
        out_specs=[pl.BlockSpec((rows, 2 * GQ_W), lambda i: (i, 0)), rv, pl.BlockSpec((rows, LANE), lambda i: (i, 0)),
                   pl.BlockSpec((LANE, GQ_W), lambda i: (0, 0)), pl.BlockSpec((1, GQ_W), lambda i: (0, 0))],
        out_shape=[jax.ShapeDtypeStruct((n, 2 * GQ_W), BF16), jax.ShapeDtypeStruct((n, GV_W), BF16),
                   jax.ShapeDtypeStruct((n, LANE), BF16), jax.ShapeDtypeStruct((LANE, GQ_W), F32),
                   jax.ShapeDtypeStruct((1, GQ_W), F32)],
        compiler_params=_cp(VMEM_BIG, ("arbitrary",)),
    )(projp, projp, gates, w2p, gb, do, dqg, dkd, dvi, dgl)


SECTIONS = ((C_QKV, 1536), (C_DZ, 512), (C_GQK, 512), (C_GV, 512), (C_GR, 512), (C_SA, 128), (C_SB, 128))


def _inproj_bwd(secs, wp, h0, g1, dx1, *, bsz, t_seq, tr, name):
    n, d = h0.shape
    per_seq = t_seq // tr
    steps = n // tr
    s_len = t_seq - CHUNK

    def body(*refs):
        sec_refs = refs[:len(SECTIONS)]
        wp_ref, h0_ref, g_ref, dx1_ref, gx_hbm, meta_ref, dg_ref, obuf, sems = refs[len(SECTIONS):]
        i = pl.program_id(0)
        slot = i % 2

        def put(step, slot_, act):
            b, j = step // per_seq, step % per_seq

            @pl.when(j == 0)
            def _():
                act(pltpu.make_async_copy(obuf.at[slot_, pl.ds(CHUNK, tr - CHUNK), :],
                                          gx_hbm.at[b, pl.ds(0, tr - CHUNK), :], sems.at[slot_]))

            if per_seq > 1:
                @pl.when(j > 0)
                def _():
                    act(pltpu.make_async_copy(obuf.at[slot_], gx_hbm.at[b, pl.ds(j * tr - CHUNK, tr), :],
                                              sems.at[slot_]))

        dh = None
        for s_ref, (off, wd) in zip(sec_refs, SECTIONS):
            part = _nt(s_ref[...], wp_ref[:, off:off + wd])
            dh = part if dh is None else dh + part
        dx, dg = _rms_bwd_math(h0_ref[...], g_ref[...], dh)
        tot = dx1_ref[...] + dx

        @pl.when(i >= 2)
        def _():
            put(i - 2, slot, lambda cp: cp.wait())

        obuf[slot] = tot
        put(i, slot, lambda cp: cp.start())

        @pl.when(i % per_seq == 0)
        def _():
            meta_ref[...] = tot[N_PAD:CHUNK, :]

        @pl.when(i == steps - 1)
        def _():
            if steps > 1:
                put(i - 1, 1 - slot, lambda cp: cp.wait())
            put(i, slot, lambda cp: cp.wait())

        @pl.when(i == 0)
        def _():
            dg_ref[...] = dg

        @pl.when(i > 0)
        def _():
            dg_ref[...] += dg

    row = pl.BlockSpec((tr, d), lambda i: (i, 0))
    vec = pl.BlockSpec((1, d), lambda i: (0, 0))
    return pl.pallas_call(
        body, name=name, grid=(steps,),
        in_specs=[pl.BlockSpec((tr, wd), lambda i: (i, 0)) for _, wd in SECTIONS]
        + [pl.BlockSpec((d, PW), lambda i: (0, 0)), row, vec, row],
        out_specs=[ANY, pl.BlockSpec((N_META, d), lambda i: (i // per_seq, 0)), vec],
        out_shape=[jax.ShapeDtypeStruct((bsz, s_len, d), F32), jax.ShapeDtypeStruct((bsz * N_META, d), F32),
                   jax.ShapeDtypeStruct((1, d), F32)],
        scratch_shapes=[pltpu.VMEM((2, tr, d), F32), pltpu.SemaphoreType.DMA((2,))],
        compiler_params=_cp(VMEM_BIG, ("arbitrary",)),
    )(*secs, wp, h0, g1, dx1)


def _adamw(w, g, m, v, *, name, emit_grad=False, col_tile=None):
    lead = w.ndim - 2
    r, c = w.shape[-2:]
    tr = r if col_tile is not None else (_tile(r, 256, 8) if r > 256 else r)
    tc = col_tile if col_tile is not None else c
    c1 = 1.0 - ADAM_B1 ** ADAM_STEP
    c2 = 1.0 - ADAM_B2 ** ADAM_STEP
    n_out = 4 if emit_grad else 3

    def body(w_ref, g_ref, m_ref, v_ref, *out_refs):
        rd = (lambda ref: ref[0]) if lead else (lambda ref: ref[...])
        gv = g_ref[:, 0:tc]
        nm = ADAM_B1 * rd(m_ref) + (1.0 - ADAM_B1) * gv
        nv = ADAM_B2 * rd(v_ref) + (1.0 - ADAM_B2) * (gv * gv)
        res = [-ADAM_LR * ((nm / c1) / (jnp.sqrt(nv / c2) + ADAM_EPS) + ADAM_WD * rd(w_ref)), nm, nv, gv]
        for o_ref, val in zip(out_refs, res):
            if lead:
                o_ref[0] = val
            else:
                o_ref[...] = val

    if col_tile is None:
        blk = pl.BlockSpec((1,) * lead + (tr, c), lambda i: (0,) * lead + (i, 0))
        gblk = pl.BlockSpec((tr, g.shape[1]), lambda i: (i, 0))
        steps = r // tr
    else:
        blk = pl.BlockSpec((1,) * lead + (r, tc), lambda j: (0,) * lead + (0, j))
        gblk = pl.BlockSpec((r, tc), lambda j: (0, j))
        steps = c // tc
    sds = jax.ShapeDtypeStruct(w.shape, F32)
    return pl.pallas_call(
        body, name=name, grid=(steps,), in_specs=[blk, gblk, blk, blk], out_specs=[blk] * n_out,
        out_shape=[sds] * n_out, compiler_params=_cp(VMEM_BIG),
    )(w, g, m, v)


def _adamw_rows(w, g, m, v, *, name):
    r, _, c = w.shape
    tr = max(t for t in range(1, 129) if r % t == 0)
    c1 = 1.0 - ADAM_B1 ** ADAM_STEP
    c2 = 1.0 - ADAM_B2 ** ADAM_STEP

    def body(w_ref, g_ref, m_ref, v_ref, d_ref, nm_ref, nv_ref, go_ref):
        gv = g_ref[...]
        nm = ADAM_B1 * m_ref[...] + (1.0 - ADAM_B1) * gv
        nv = ADAM_B2 * v_ref[...] + (1.0 - ADAM_B2) * (gv * gv)
        d_ref[...] = -ADAM_LR * ((nm / c1) / (jnp.sqrt(nv / c2) + ADAM_EPS) + ADAM_WD * w_ref[...])
        nm_ref[...] = nm
        nv_ref[...] = nv
        go_ref[...] = gv

    blk = pl.BlockSpec((tr, 1, c), lambda i: (i, 0, 0))
    sds = jax.ShapeDtypeStruct(w.shape, F32)
    return pl.pallas_call(
        body, name=name, grid=(r // tr,), in_specs=[blk] * 4, out_specs=[blk] * 4, out_shape=[sds] * 4,
        compiler_params=_cp(VMEM_BIG),
    )(w, g, m, v)


def _pair_sum(where, g, theirs, *, name):
    lead, r, cols = g.shape
    half = r // 2
    tr = _tile(half, 256, 16)
    nh = half // tr

    def body(w_ref, a_ref, b_ref, o_ref):
        o_ref[...] = (a_ref[...] + b_ref[...]).astype(o_ref.dtype)

    blk = pl.BlockSpec((1, tr, cols), lambda s, i, w: (s, i, 0))
    return pl.pallas_call(
        body, name=name,
        grid_spec=pltpu.PrefetchScalarGridSpec(
            num_scalar_prefetch=1, grid=(lead, nh),
            in_specs=[pl.BlockSpec((1, tr, cols), lambda s, i, w: (s, w[0] * nh + i, 0)), blk], out_specs=blk),
        out_shape=jax.ShapeDtypeStruct((lead, half, cols), BF16), compiler_params=_cp(VMEM_BIG),
    )(where, g, theirs)


def _chip_sum(where, pair, q, *, name):
    _, half, cols = pair.shape
    tr = _tile(half, 256, 16)
    nh = half // tr

    def body(w_ref, own_ref, q1_ref, q2_ref, q3_ref, o_ref):
        f = lambda ref: ref[0].astype(F32)
        o_ref[...] = ((f(own_ref) + f(q1_ref)) + f(q2_ref)) + f(q3_ref)

    def peer(d):
        return pl.BlockSpec((1, tr, cols), lambda i, w: ((w[1] + d) % N_CHIPS, i, 0))

    return pl.pallas_call(
        body, name=name,
        grid_spec=pltpu.PrefetchScalarGridSpec(
            num_scalar_prefetch=1, grid=(nh,),
            in_specs=[peer(0), peer(1), peer(2), peer(3)],
            out_specs=pl.BlockSpec((tr, cols), lambda i, w: (w[0] * nh + i, 0))),
        out_shape=jax.ShapeDtypeStruct((2 * half, cols), F32), compiler_params=_cp(VMEM_BIG),
    )(where, pair, q, q, q)


VM = pl.BlockSpec(memory_space=pltpu.VMEM)


def _row_chunks(rows, n_split):
    size = rows // n_split
    assert size * n_split == rows and size % 16 == 0, (rows, n_split)
    return [(s, pl.ds(s * size, size)) for s in range(n_split)], size


D2D_SPLIT = 4
ICI_SPLIT = 2


def _sibling_halves(grads):
    n_arr = len(grads)

    def body(*refs):
        ins = refs[:n_arr]
        theirs = refs[n_arr:2 * n_arr]
        send_sems, recv_sems = refs[2 * n_arr:]
        x, y, c = _place()
        copies = []
        for k in range(n_arr):
            half = ins[k].shape[1] // 2
            chunks, size = _row_chunks(half, D2D_SPLIT)
            for s, dst_rows in chunks:
                give = pltpu.make_async_remote_copy(
                    src_ref=ins[k].at[:, pl.ds((1 - c) * half + s * size, size), :], dst_ref=theirs[k].at[:, dst_rows, :],
                    send_sem=send_sems.at[k, s], recv_sem=recv_sems.at[k, s], device_id=(x, y, 1 - c),
                    device_id_type=MESH)
                give.start()
                copies.append(give)
        for give in copies:
            give.wait()

    halves = [jax.ShapeDtypeStruct((g.shape[0], g.shape[1] // 2, g.shape[2]), F32) for g in grads]
    sem = pltpu.SemaphoreType.DMA((n_arr, D2D_SPLIT))
    return pl.pallas_call(
        body, name="sibling_halves", in_specs=[ANY] * n_arr, out_specs=[ANY] * n_arr, out_shape=halves,
        scratch_shapes=[sem, sem],
    )(*grads)


def _chip_exchange(parts):
    n_arr = len(parts)

    def body(*refs):
        ins = refs[:n_arr]
        outs = refs[n_arr:2 * n_arr]
        send_sems, recv_sems = refs[2 * n_arr:]
        x, y, c = _place()
        me = 2 * x + y
        sends = []
        for k in range(n_arr):
            chunks, _ = _row_chunks(ins[k].shape[1], ICI_SPLIT)
            for d, (px, py, pj) in enumerate(_other_chips(x, y)):
                for s, rows in chunks:
                    cp = pltpu.make_async_remote_copy(
                        src_ref=ins[k].at[pj, rows, :], dst_ref=outs[k].at[me, rows, :], send_sem=send_sems.at[k, d, s],
                        recv_sem=recv_sems.at[k, d, s], device_id=(px, py, c), device_id_type=MESH)
                    cp.start()
                    sends.append(cp)
        for k in range(n_arr):
            chunks, _ = _row_chunks(ins[k].shape[1], ICI_SPLIT)
            for d, (px, py, pj) in enumerate(_other_chips(x, y)):
                for s, rows in chunks:
                    pltpu.make_async_remote_copy(
                        src_ref=ins[k].at[pj, rows, :], dst_ref=outs[k].at[pj, rows, :], send_sem=send_sems.at[k, d, s],
                        recv_sem=recv_sems.at[k, d, s], device_id=(px, py, c), device_id_type=MESH).wait_recv()
        for cp in sends:
            cp.wait_send()

    sem = pltpu.SemaphoreType.DMA((n_arr, 3, ICI_SPLIT))
    return pl.pallas_call(
        body, name="chip_exchange", in_specs=[ANY] * n_arr, out_specs=[ANY] * n_arr,
        out_shape=[jax.ShapeDtypeStruct(p.shape, p.dtype) for p in parts],
        scratch_shapes=[sem, sem],
    )(*parts)


class _SiblingHalvesRider:
    def __init__(self, grads):
        self.inputs = list(grads)
        self.out_shapes = [jax.ShapeDtypeStruct((g.shape[0], g.shape[1] // 2, g.shape[2]), F32) for g in grads]
        self.aliases = {}
        self.sems = [pltpu.SemaphoreType.DMA((len(grads), D2D_SPLIT))] * 2

    def _copies(self, ins, outs, sems):
        x, y, c = _place()
        for k in range(len(ins)):
            half = ins[k].shape[1] // 2
            chunks, size = _row_chunks(half, D2D_SPLIT)
            for s, dst_rows in chunks:
                yield pltpu.make_async_remote_copy(
                    src_ref=ins[k].at[:, pl.ds((1 - c) * half + s * size, size), :], dst_ref=outs[k].at[:, dst_rows, :],
                    send_sem=sems[0].at[k, s], recv_sem=sems[1].at[k, s], device_id=(x, y, 1 - c), device_id_type=MESH)

    def first(self, ins, outs, sems):
        for cp in self._copies(ins, outs, sems):
            cp.start()

    def last(self, ins, outs, sems):
        for cp in self._copies(ins, outs, sems):
            cp.wait()


class _ChipExchangeRider:
    def __init__(self, parts):
        self.inputs = list(parts)
        self.out_shapes = [jax.ShapeDtypeStruct(p.shape, p.dtype) for p in parts]
        self.aliases = {}
        self.sems = [pltpu.SemaphoreType.DMA((len(parts), 3, ICI_SPLIT))] * 2

    def _copies(self, ins, outs, sems, receiving):
        x, y, c = _place()
        for k in range(len(ins)):
            chunks, _ = _row_chunks(ins[k].shape[1], ICI_SPLIT)
            for d, (px, py, pj) in enumerate(_other_chips(x, y)):
                for s, rows in chunks:
                    yield pltpu.make_async_remote_copy(
                        src_ref=ins[k].at[pj, rows, :], dst_ref=outs[k].at[pj if receiving else 2 * x + y, rows, :],
                        send_sem=sems[0].at[k, d, s], recv_sem=sems[1].at[k, d, s], device_id=(px, py, c),
                        device_id_type=MESH)

    def first(self, ins, outs, sems):
        for cp in self._copies(ins, outs, sems, False):
            cp.start()

    def last(self, ins, outs, sems):
        for cp in self._copies(ins, outs, sems, True):
            cp.wait_recv()
        for cp in self._copies(ins, outs, sems, False):
            cp.wait_send()


def _sibling_join(bufs):
    n_arr = len(bufs)

    def body(*refs):
        bufs_out = refs[n_arr:2 * n_arr]
        send_sems, recv_sems = refs[2 * n_arr:]
        x, y, c = _place()
        copies = []
        for k in range(n_arr):
            half = bufs_out[k].shape[0] // 2
            chunks, size = _row_chunks(half, D2D_SPLIT)
            for s, _ in chunks:
                rows = pl.ds(c * half + s * size, size)
                give = pltpu.make_async_remote_copy(
                    src_ref=bufs_out[k].at[rows, :], dst_ref=bufs_out[k].at[rows, :], send_sem=send_sems.at[k, s],
                    recv_sem=recv_sems.at[k, s], device_id=(x, y, 1 - c), device_id_type=MESH)
                give.start()
                copies.append((k, s, half, size, give))
        for k, s, half, size, give in copies:
            rows = pl.ds((1 - c) * half + s * size, size)
            pltpu.make_async_remote_copy(
                src_ref=bufs_out[k].at[rows, :], dst_ref=bufs_out[k].at[rows, :], send_sem=send_sems.at[k, s],
                recv_sem=recv_sems.at[k, s], device_id=(x, y, 1 - c), device_id_type=MESH).wait_recv()
            give.wait_send()

    sem = pltpu.SemaphoreType.DMA((n_arr, D2D_SPLIT))
    return pl.pallas_call(
        body, name="sibling_join", in_specs=[ANY] * n_arr, out_specs=[ANY] * n_arr,
        out_shape=[jax.ShapeDtypeStruct(b.shape, F32) for b in bufs],
        input_output_aliases={k: k for k in range(n_arr)},
        scratch_shapes=[sem, sem],
    )(*bufs)


PACK_ROWS = 48


def _small_allreduce(pack):
    masks = [(dx, dy, dc) for dx in (0, 1) for dy in (0, 1) for dc in (0, 1)][1:]

    def body(p_ref, o_ref, buf, send_sems, recv_sems):
        x, y, c = _place()
        me = 4 * x + 2 * y + c
        buf[me] = p_ref[...]
        sends = []
        for k, (dx, dy, dc) in enumerate(masks):
            peer = (1 - x if dx else x, 1 - y if dy else y, 1 - c if dc else c)
            cp = pltpu.make_async_remote_copy(
                src_ref=p_ref, dst_ref=buf.at[me], send_sem=send_sems.at[k], recv_sem=recv_sems.at[k],
                device_id=peer, device_id_type=MESH)
            cp.start()
            sends.append(cp)
        for k, (dx, dy, dc) in enumerate(masks):
            peer = (1 - x if dx else x, 1 - y if dy else y, 1 - c if dc else c)
            pj = 4 * peer[0] + 2 * peer[1] + peer[2]
            pltpu.make_async_remote_copy(
                src_ref=p_ref, dst_ref=buf.at[pj], send_sem=send_sems.at[k], recv_sem=recv_sems.at[k],
                device_id=peer, device_id_type=MESH).wait_recv()
        for cp in sends:
            cp.wait_send()
        tot = buf[0]
        for k in range(1, 8):
            tot = tot + buf[k]
        o_ref[...] = tot
        o_ref[0:N_META, :] = tot[0:N_META] + tot[N_META:2 * N_META]

    return pl.pallas_call(
        body, name="small_allreduce", in_specs=[VM], out_specs=VM,
        out_shape=jax.ShapeDtypeStruct((PACK_ROWS, D_MODEL), F32),
        scratch_shapes=[pltpu.VMEM((8, PACK_ROWS, D_MODEL), F32), pltpu.SemaphoreType.DMA((7,)),
                        pltpu.SemaphoreType.DMA((7,))],
    )(pack)


def _pad_lanes(vec, offset):
    k = vec.shape[1]
    return jnp.concatenate([jnp.zeros((1, offset), F32), vec, jnp.zeros((1, LANE - offset - k), F32)], axis=1)


def _local_step(x, tgt, meta, norm1_g, wp, conv_w, a_log, dt_bias, dn_norm_g, gla_w2, gla_b, gla_norm_g,
                w_out, norm2_g, w_up, w_down, final_norm_g, late_gather=None, where=None):
    bsz, s_len, d = x.shape
    t_seq = s_len + CHUNK
    nc_seq = t_seq // CHUNK
    n = bsz * t_seq
    tr = _tile(t_seq, 832)
    tt = _tile(t_seq, 416)

    lead = jnp.concatenate([jnp.zeros((N_PAD, d), F32), meta], axis=0)
    alog_row = _pad_lanes(a_log, 4)
    dtb_row = _pad_lanes(dt_bias, 4)
    w2p = jnp.concatenate([gla_w2, jnp.zeros((LANE - GLA_RANK, GQ_W), F32)], axis=0)

    h0, h = _embed_norm(x, lead, norm1_g, tr=tr, name="embed_norm1")
    ride_up, ride_down, ride_out = late_gather if late_gather is not None else (None, None, None)
    res = _mm(h, wp, "nn", tm=tt, tn=PW, tk=d, out_dtypes=(BF16, F32), out_widths=(PW, PW - C_SA),
              epilogue=lambda acc: (acc, acc[:, C_SA:PW]), name="in_proj", rider=ride_up)
    ((projp, gates), got_up) = res if ride_up is not None else (res, None)
    qn, kn, v = _dnprep_fwd(projp, conv_w, bsz=bsz, t_seq=t_seq, tt=tt, name="dn_prep")
    u, w, qg, kd, pmat, tmat, gl, got_down = _dn_intra_fwd(qn, kn, v, gates, alog_row, dtb_row, nc_seq=nc_seq,
                                                           name="dn_intra", rider=ride_down)
    o_dn, vn, hist, got_out = _dn_scan_fwd(u, w, qg, kd, pmat, gl, bsz=bsz, nc_seq=nc_seq, name="dn_scan",
                                           rider=ride_out)
    oi, gqg, gkd, ggl = _gla_intra_fwd(projp, gates, w2p, gla_b, nc_seq=nc_seq, name="gla_intra")
    o_gla, ghist, _ = _gla_scan_fwd(oi, gqg, gkd, ggl, projp, bsz=bsz, nc_seq=nc_seq, name="gla_scan")
    if late_gather is not None:
        w_out = got_out[0].reshape(d, d)
        w_up = got_up[0].transpose(1, 0, 2).reshape(d, D_FF)
        w_down = got_down[0].reshape(D_FF, d)
    mix = _gnorm_fwd(o_dn, o_gla, projp, dn_norm_g, gla_norm_g, tr=tr, name="gated_norm")
    (x1,) = _mm(mix, w_out, "nn", tm=tr, tn=d, tk=d, out_dtypes=(F32,), extras=(h0,),
                epilogue=lambda acc, res: (res + acc,), name="out_proj")
    h2 = _rms_fwd(x1, norm2_g, tr=tr, name="norm2")

    (act,) = _mm(h2, w_up, "nn", tm=tt, tn=D_FF, tk=d, out_dtypes=(BF16,),
                 epilogue=lambda acc: (jnp.square(jnp.maximum(acc, 0.0)),), name="mlp_up", n_chunk=1024)
    dx2, dx2b, d_final_g, loss_tile = _mlp_down_loss(act, w_down, x1, final_norm_g, tgt, t_seq=t_seq, tr=tt,
                                                     name="mlp_down_loss")

    (dup,) = _mm(dx2b, w_down, "nt", tm=tt, tn=D_FF, tk=d, out_dtypes=(BF16,), extras=(act,),
                 epilogue=lambda acc, a: (acc * (2.0 * jnp.sqrt(a.astype(F32))),), name="mlp_down_bwd", n_chunk=1024)
    tk2 = 2 * tr if n % (2 * tr) == 0 else tr
    (d_w_down,) = _mm(act, dx2b, "tn", tm=D_FF // 2, tn=d, tk=tk2, out_dtypes=(F32,), name="w_down_grad")
    (d_w_up,) = _mm(h2, dup, "tn", tm=d, tn=D_FF // 2, tk=tk2, out_dtypes=(F32,), name="w_up_grad")
    mlp_sm = [d_w_up.reshape(d, N_CHIPS, D_FF // N_CHIPS).transpose(1, 0, 2), d_w_down.reshape(N_CHIPS, D_FF // N_CHIPS, d)]
    ride1 = _SiblingHalvesRider(mlp_sm) if where is not None else None
    dx1, dx1b, d_norm2_g, theirs = _mlp_up_bwd_norm(dup, w_up, x1, norm2_g, dx2, tr=tt, name="mlp_up_bwd_norm",
                                                    rider=ride1)
    ride2 = None
    if where is not None:
        mlp_pair = [_pair_sum(where, a, b, name=f"pair_sum_mlp{k}") for k, (a, b) in enumerate(zip(mlp_sm, theirs))]
        ride2 = _ChipExchangeRider(mlp_pair)

    (dmix,) = _mm(dx1b, w_out, "nt", tm=tr, tn=d, tk=d, out_dtypes=(BF16,), name="out_proj_bwd")
    (d_w_out,) = _mm(mix, dx1b, "tn", tm=d, tn=d, tk=tr, out_dtypes=(F32,), name="w_out_grad")
    do_dn, ddz, do_gla, dgr, d_dn_norm_g, d_gla_norm_g = _gnorm_bwd(
        dmix, o_dn, o_gla, projp, dn_norm_g, gla_norm_g, tr=tr, name="gated_norm_bwd")
    du, dw, dqg, dkd, dgl = _dn_scan_bwd(do_dn, w, qg, kd, vn, pmat, gl, hist, bsz=bsz, nc_seq=nc_seq,
                                          name="dn_scan_bwd")
    dqn, dkn, dv, dsa, d_alog, d_dtb, mlp_parts = _dn_intra_bwd(
        qn, kn, v, gates, alog_row, dtb_row, u, w, tmat, du, dw, dqg, dkd, do_dn, vn, dgl, nc_seq=nc_seq,
        name="dn_intra_bwd", rider=ride2)
    dz, d_conv_w = _dnprep_bwd_a(projp, conv_w, dqn, dkn, dv, bsz=bsz, t_seq=t_seq, tt=tt, name="dn_prep_bwd")
    dcin = _dnprep_bwd_b(dz, conv_w, bsz=bsz, t_seq=t_seq, tt=tt, name="conv_bwd")
    gdqg, gdkd, gdvi, gdgl = _gla_scan_bwd(do_gla, gqg, gkd, ggl, projp, ghist, bsz=bsz, nc_seq=nc_seq,
                                            name="gla_scan_bwd")
    dgqk, dgv, dsb, d_w2p, d_gla_b = _gla_intra_bwd(projp, gates, w2p, gla_b, do_gla, gdqg, gdkd, gdvi, gdgl,
                                                    nc_seq=nc_seq, name="gla_intra_bwd")

    secs = (dcin, ddz, dgqk, dgv, dgr, dsa, dsb)
    g_lo = _grad_tn(h, secs[0:2], tk=tr, name="w_in_grad_lo")
    g_hi = _grad_tn(h, secs[2:7], tk=tr, name="w_in_grad_hi")
    grad_x, d_meta_rows, d_norm1_g = _inproj_bwd(secs, wp, h0, norm1_g, dx1, bsz=bsz, t_seq=t_seq, tr=tt,
                                                 name="in_proj_bwd")

    grads = dict(w_in_lo=g_lo, w_in_hi=g_hi, w_out=d_w_out, w_up=d_w_up, w_down=d_w_down, meta_rows=d_meta_rows,
                 norm1_g=d_norm1_g, conv_w=d_conv_w, a_log_tile=d_alog, dt_bias_tile=d_dtb, dn_norm_g=d_dn_norm_g,
                 gla_w2=d_w2p[0:GLA_RANK], gla_b=d_gla_b, gla_norm_g=d_gla_norm_g, norm2_g=d_norm2_g,
                 final_norm_g=d_final_g, loss_tile=loss_tile)
    if where is not None:
        grads["mlp_exchanged"] = (mlp_pair, mlp_parts)
    return grad_x, grads


SHARD_W = IN_WIDTH // N_CHIPS
PADDED_ORDER = ((0, 2048), (2056, 3592), (2048, 2056), LANE - 8, (3592, 3608), LANE - GLA_RANK)


def _pad_layout(w_full):
    pieces = [jnp.zeros((w_full.shape[0], seg), w_full.dtype) if isinstance(seg, int) else w_full[:, seg[0]:seg[1]]
              for seg in PADDED_ORDER]
    return jnp.concatenate(pieces, axis=1)


def _padded_from_shards(stack):
    pieces = []
    for seg in PADDED_ORDER:
        if isinstance(seg, int):
            pieces.append(jnp.zeros((stack.shape[1], seg), stack.dtype))
            continue
        for j in range(N_CHIPS):
            lo, hi = max(seg[0], j * SHARD_W), min(seg[1], (j + 1) * SHARD_W)
            if lo < hi:
                pieces.append(stack[j, :, lo - j * SHARD_W:hi - j * SHARD_W])
    return jnp.concatenate(pieces, axis=1)


def _shards_from_padded(g_lo, g_hi):
    split = g_lo.shape[1]
    starts, pos = [], 0
    for seg in PADDED_ORDER:
        width = seg if isinstance(seg, int) else seg[1] - seg[0]
        if not isinstance(seg, int):
            starts.append((seg[0], seg[1], pos))
        pos += width
    shards = []
    for j in range(N_CHIPS):
        pieces = []
        for a, b, p0 in sorted(starts):
            lo, hi = max(a, j * SHARD_W), min(b, (j + 1) * SHARD_W)
            if lo < hi:
                src, off = (g_lo, 0) if p0 < split else (g_hi, split)
                pieces.append(src[:, p0 + lo - a - off:p0 + hi - a - off])
        pieces.append(jnp.zeros((g_lo.shape[0], D_MODEL - SHARD_W), g_lo.dtype))
        shards.append(jnp.concatenate(pieces, axis=1))
    return jnp.stack(shards)


def _pack_small(g, bsz):
    assert bsz * N_META == 32
    row = jnp.concatenate([g["a_log_tile"], g["dt_bias_tile"], g["dn_norm_g"], g["gla_norm_g"], g["gla_b"],
                           g["loss_tile"], jnp.zeros((1, LANE), F32)], axis=1)
    return jnp.concatenate([g["meta_rows"], g["norm1_g"], g["conv_w"].reshape(6, D_MODEL), row,
                            g["gla_w2"].reshape(4, D_MODEL), g["norm2_g"], g["final_norm_g"],
                            jnp.zeros((2, D_MODEL), F32)], axis=0)


def kernel(x, meta_tokens, norm1_g, w_in, conv_w, a_log, dt_bias, dn_norm_g, gla_w2, gla_b, gla_norm_g, w_out, norm2_g, w_up, w_down, final_norm_g, loss_target, m_meta_tokens, m_norm1_g, m_w_in, m_conv_w, m_a_log, m_dt_bias, m_dn_norm_g, m_gla_w2, m_gla_b, m_gla_norm_g, m_w_out, m_norm2_g, m_w_up, m_w_down, m_final_norm_g, v_meta_tokens, v_norm1_g, v_w_in, v_conv_w, v_a_log, v_dt_bias, v_dn_norm_g, v_gla_w2, v_gla_b, v_gla_norm_g, v_w_out, v_norm2_g, v_w_up, v_w_down, v_final_norm_g):
    bsz = x.shape[0]
    chip = 2 * lax.axis_index("x") + lax.axis_index("y")

    lane_pad = lambda a, wd: jnp.pad(a, ((0, 0), (0, wd - a.shape[1])))
    where = jnp.stack([lax.axis_index("c"), chip]).astype(jnp.int32)
    slot = lambda a, dt, nm: _to_slot(where, a, dt, name="slot_" + nm)
    early = _GatherRider([slot(lane_pad(w_in[0], D_MODEL), BF16, "w_in"), slot(meta_tokens, F32, "meta"),
                          slot(conv_w[0], F32, "conv"), slot(lane_pad(gla_w2[0], LANE), F32, "gla_w2")],
                         [True, False, False, False])
    g_in, g_meta, g_conv, g_w2 = _exchange_now(early, name="gather_early")
    late = (_GatherRider([slot(w_up[0], BF16, "w_up")], [True]), _GatherRider([slot(w_down[0], BF16, "w_down")], [True]),
            _GatherRider([slot(w_out[0], BF16, "w_out")], [True]))
    wp = _padded_from_shards(g_in)
    meta_f = g_meta.transpose(1, 0, 2).reshape(N_META, D_MODEL)
    conv_f = g_conv.transpose(1, 0, 2).reshape(4, QKV_W)
    w2_f = g_w2[:, :, 0:GQ_W // N_CHIPS].transpose(1, 0, 2).reshape(GLA_RANK, GQ_W)

    grad_x, g = _local_step(x, loss_target, meta_f, norm1_g, wp, conv_f, a_log, dt_bias, dn_norm_g, w2_f, gla_b,
                            gla_norm_g, None, norm2_g, None, None, final_norm_g.reshape(1, D_MODEL), late_gather=late, where=where)

    shard_major = [_shards_from_padded(g["w_in_lo"], g["w_in_hi"]), g["w_out"].reshape(N_CHIPS, D_MODEL // N_CHIPS, D_MODEL)]
    theirs = _exchange_now(_SiblingHalvesRider(shard_major), name="sibling_halves")
    pair = [_pair_sum(where, a, b, name=f"pair_sum_{k}") for k, (a, b) in enumerate(zip(shard_major, theirs))]
    parts = _exchange_now(_ChipExchangeRider(pair), name="chip_exchange")
    mlp_pair, mlp_parts = g["mlp_exchanged"]
    halves = [_chip_sum(where, p, q, name=f"chip_sum_{k}")
              for k, (p, q) in enumerate(zip(pair + mlp_pair, list(parts) + list(mlp_parts)))]
    gw_in, gw_out, gw_up, gw_down = _sibling_join(halves)

    red = _small_allreduce(_pack_small(g, bsz))
    g_meta_full = red[0:N_META]
    g_norm1 = red[32:33]
    g_conv_full = red[33:39].reshape(4, QKV_W)
    srow = red[39:40]
    g_alog, g_dtb = srow[:, 4:8], srow[:, LANE + 4:LANE + 8]
    g_dn_norm, g_gla_norm = srow[:, 2 * LANE:3 * LANE], srow[:, 3 * LANE:4 * LANE]
    g_gla_b = srow[:, 4 * LANE:6 * LANE]
    loss = srow[0, 6 * LANE]
    g_w2_full = red[40:44].reshape(GLA_RANK, GQ_W)
    g_norm2 = red[44:45]
    g_final = red[45:46]
    g_meta_sh = lax.dynamic_slice_in_dim(g_meta_full, chip * (D_MODEL // N_CHIPS), D_MODEL // N_CHIPS, axis=1)
    g_conv_sh = lax.dynamic_slice_in_dim(g_conv_full, chip * (QKV_W // N_CHIPS), QKV_W // N_CHIPS, axis=1)
    g_w2_sh = lax.dynamic_slice_in_dim(g_w2_full, chip * (GQ_W // N_CHIPS), GQ_W // N_CHIPS, axis=1)

    names = ["meta_tokens", "norm1_g", "w_in", "conv_w", "a_log", "dt_bias", "dn_norm_g", "gla_w2", "gla_b",
             "gla_norm_g", "w_out", "norm2_g", "w_up", "w_down", "final_norm_g"]
    weights = dict(meta_tokens=meta_tokens, norm1_g=norm1_g, w_in=w_in, conv_w=conv_w, a_log=a_log, dt_bias=dt_bias,
                   dn_norm_g=dn_norm_g, gla_w2=gla_w2, gla_b=gla_b, gla_norm_g=gla_norm_g, w_out=w_out,
                   norm2_g=norm2_g, w_up=w_up, w_down=w_down, final_norm_g=final_norm_g)
    ms = dict(meta_tokens=m_meta_tokens, norm1_g=m_norm1_g, w_in=m_w_in, conv_w=m_conv_w, a_log=m_a_log,
              dt_bias=m_dt_bias, dn_norm_g=m_dn_norm_g, gla_w2=m_gla_w2, gla_b=m_gla_b, gla_norm_g=m_gla_norm_g,
              w_out=m_w_out, norm2_g=m_norm2_g, w_up=m_w_up, w_down=m_w_down, final_norm_g=m_final_norm_g)
    vs = dict(meta_tokens=v_meta_tokens, norm1_g=v_norm1_g, w_in=v_w_in, conv_w=v_conv_w, a_log=v_a_log,
              dt_bias=v_dt_bias, dn_norm_g=v_dn_norm_g, gla_w2=v_gla_w2, gla_b=v_gla_b, gla_norm_g=v_gla_norm_g,
              w_out=v_w_out, norm2_g=v_norm2_g, w_up=v_w_up, w_down=v_w_down, final_norm_g=v_final_norm_g)
    grads2d = dict(meta_tokens=g_meta_sh, norm1_g=g_norm1, w_in=gw_in, conv_w=g_conv_sh, a_log=g_alog, dt_bias=g_dtb,
                   dn_norm_g=g_dn_norm, gla_w2=g_w2_sh, gla_b=g_gla_b, gla_norm_g=g_gla_norm, w_out=gw_out,
                   norm2_g=g_norm2, w_up=gw_up, w_down=gw_down, final_norm_g=g_final)
    out_g, out_d, out_m, out_v = [], [], [], []
    for nm in names:
        shape = weights[nm].shape
        g2 = grads2d[nm]
        if nm == "w_in":
            tview = lambda a: jnp.transpose(a, (2, 0, 1))
            res = _adamw_rows(tview(weights[nm]), g2[:, 0:SHARD_W].T.reshape(SHARD_W, 1, D_MODEL), tview(ms[nm]),
                              tview(vs[nm]), name=f"adamw_{nm}")
            res = [jnp.transpose(a, (1, 2, 0)) for a in res]
            gout = res[3]
        elif len(shape) == 3:
            res = _adamw(weights[nm], g2, ms[nm], vs[nm], name=f"adamw_{nm}")
            gout = g2.reshape(shape)
        else:
            as2d = lambda a: a.reshape(g2.shape)
            res = _adamw(as2d(weights[nm]), g2, as2d(ms[nm]), as2d(vs[nm]), name=f"adamw_{nm}")
            gout = g2.reshape(shape)
        out_g.append(gout)
        out_d.append(res[0].reshape(shape))
        out_m.append(res[1].reshape(shape))
        out_v.append(res[2].reshape(shape))
    return (loss, grad_x, *out_g, *out_d, *out_m, *out_v)
```

```python
import functools

import jax
import jax.numpy as jnp
import numpy as np
from jax import lax
from jax.experimental import pallas as pl
from jax.experimental.pallas import tpu as pltpu

F32 = jnp.float32
BF16 = jnp.bfloat16
HI = lax.Precision.HIGHEST
MESH = pl.DeviceIdType.MESH

D_MODEL = 1024
N_META = 16
CHUNK = 64
N_PAD = CHUNK - N_META
NH = 4
DN_D = 128
GLA_DK = 64
GLA_DV = 128
GLA_RANK = 16
D_FF = 4 * D_MODEL
EPS = 1e-6
IN_WIDTH = 3608
C_QKV, C_DZ, C_GQK, C_GV, C_GR, C_SA, C_SB, PW = 0, 1536, 2048, 2560, 3072, 3584, 3712, 3840
LANE = 128
N_CHIPS = 4

ADAM_LR, ADAM_B1, ADAM_B2, ADAM_EPS, ADAM_WD, ADAM_STEP = 0.001, 0.9, 0.999, 1e-08, 0.01, 10

VMEM_BIG = 56 * 1024 * 1024


def _cp(vmem=None, sem=None):
    kw = {}
    if vmem is not None:
        kw["vmem_limit_bytes"] = vmem
    if sem is not None:
        kw["dimension_semantics"] = sem
    return pltpu.CompilerParams(**kw)


def _tile(n, target, mult=16):
    best = None
    for t in range(mult, min(n, target) + 1, mult):
        if n % t == 0:
            best = t
    assert best is not None, (n, target)
    return best


def _dot(a, b, dims, prec=None):
    return lax.dot_general(a, b, (dims, ((), ())), preferred_element_type=F32, precision=prec)


def _nn(a, b):
    return _dot(a.astype(BF16), b.astype(BF16), ((1,), (0,)))


def _nt(a, b):
    return _dot(a.astype(BF16), b.astype(BF16), ((1,), (1,)))


def _tn(a, b):
    return _dot(a.astype(BF16), b.astype(BF16), ((0,), (0,)))


def _split(x):
    hi = x.astype(BF16)
    return hi, (x - hi.astype(F32)).astype(BF16)


def _tri_sum(tri, x):
    t = tri.astype(BF16)
    hi = x.astype(BF16)
    r1 = x - hi.astype(F32)
    mid = r1.astype(BF16)
    lo = (r1 - mid.astype(F32)).astype(BF16)
    nn = ((1,), (0,))
    return _dot(t, hi, nn) + _dot(t, mid, nn) + _dot(t, lo, nn)


def _sigmoid(x):
    return 0.5 * jnp.tanh(0.5 * x) + 0.5


def _softplus(x):
    return jnp.maximum(x, 0.0) + jnp.log(1.0 + jnp.exp(-jnp.abs(x)))


def _logsigmoid(x):
    return -_softplus(-x)


def _iota2(shape, dim):
    return lax.broadcasted_iota(jnp.int32, shape, dim)


def _mm(a, b, mode, *, tm, tn, tk, out_dtypes, extras=(), epilogue=None, name, vmem=VMEM_BIG, rider=None,
        out_widths=None, n_chunk=None):
    if mode == "tn":
        K, M = a.shape
    else:
        M, K = a.shape
    N = b.shape[0] if mode == "nt" else b.shape[1]
    assert M % tm == 0 and N % tn == 0 and K % tk == 0, (name, M, N, K, tm, tn, tk)
    nk = K // tk
    n_ex, n_out = len(extras), len(out_dtypes)
    if mode == "tn":
        a_spec = pl.BlockSpec((tk, tm), lambda i, j, k: (k, i))
    else:
        a_spec = pl.BlockSpec((tm, tk), lambda i, j, k: (i, k))
    if mode == "nt":
        b_spec = pl.BlockSpec((tn, tk), lambda i, j, k: (j, k))
    else:
        b_spec = pl.BlockSpec((tk, tn), lambda i, j, k: (k, j))
    mn_spec = pl.BlockSpec((tm, tn), lambda i, j, k: (i, j))
    if out_widths is None:
        o_specs = [mn_spec] * n_out
        o_shapes = [jax.ShapeDtypeStruct((M, N), dt) for dt in out_dtypes]
    else:
        assert tn == N
        o_specs = [pl.BlockSpec((tm, wd), lambda i, j, k: (i, 0)) for wd in out_widths]
        o_shapes = [jax.ShapeDtypeStruct((M, wd), dt) for wd, dt in zip(out_widths, out_dtypes)]
    dims = {"nn": ((1,), (0,)), "nt": ((1,), (1,)), "tn": ((0,), (0,))}[mode]

    single = nk == 1
    direct = (not single) and epilogue is None and n_out == 1 and out_dtypes[0] == F32

    def body(*refs):
        a_ref, b_ref = refs[0], refs[1]
        ex_refs = refs[2:2 + n_ex]
        out_refs = refs[2 + n_ex:2 + n_ex + n_out]
        if n_chunk is not None:
            assert single and out_widths is None and mode != "tn" and tn % n_chunk == 0
            av = a_ref[...].astype(BF16)
            for j in range(tn // n_chunk):
                cols = slice(j * n_chunk, (j + 1) * n_chunk)
                bv = b_ref[cols, :] if mode == "nt" else b_ref[:, cols]
                acc = _dot(av, bv.astype(BF16), dims)
                res = (acc,) if epilogue is None else epilogue(acc, *[e[:, cols] for e in ex_refs])
                for o_ref, r in zip(out_refs, res):
                    o_ref[:, cols] = r.astype(o_ref.dtype)
            return
        part = _dot(a_ref[...].astype(BF16), b_ref[...].astype(BF16), dims)

        def finish(acc):
            res = (acc,) if epilogue is None else epilogue(acc, *[e[...] for e in ex_refs])
            for o_ref, r in zip(out_refs, res):
                o_ref[...] = r.astype(o_ref.dtype)

        if single:
            finish(part)
            return
        acc_ref = out_refs[0] if direct else refs[2 + n_ex + n_out]
        k = pl.program_id(2)

        @pl.when(k == 0)
        def _():
            acc_ref[...] = part

        @pl.when(k > 0)
        def _():
            acc_ref[...] += part

        if not direct:
            @pl.when(k == nk - 1)
            def _():
                finish(acc_ref[...])

    outs, ridden = _hosted_call(
        body, rider, name=name, grid=(M // tm, N // tn, nk),
        in_specs=[a_spec, b_spec] + [mn_spec] * n_ex,
        out_specs=o_specs, out_shape=o_shapes,
        scratch_shapes=[] if (single or direct) else [pltpu.VMEM((tm, tn), F32)],
        compiler_params=_cp(vmem, ("parallel", "parallel", "arbitrary")), args=(a, b, *extras))
    return tuple(outs) if rider is None else (tuple(outs), ridden)


def _grad_tn(a, secs, *, tk, name):
    kk, m = a.shape
    widths = [s.shape[1] for s in secs]
    total = sum(widths)
    nk = kk // tk

    def body(*refs):
        a_ref, sec_refs, o_ref = refs[0], refs[1:-1], refs[-1]
        cat = sec_refs[0][...] if len(sec_refs) == 1 else jnp.concatenate([s[...] for s in sec_refs], axis=1)
        part = _dot(a_ref[...].astype(BF16), cat.astype(BF16), ((0,), (0,)))
        k = pl.program_id(0)

        @pl.when(k == 0)
        def _():
            o_ref[...] = part

        @pl.when(k > 0)
        def _():
            o_ref[...] += part

    return pl.pallas_call(
        body, name=name, grid=(nk,),
        in_specs=[pl.BlockSpec((tk, m), lambda k: (k, 0))] + [pl.BlockSpec((tk, w), lambda k: (k, 0)) for w in widths],
        out_specs=pl.BlockSpec((m, total), lambda k: (0, 0)),
        out_shape=jax.ShapeDtypeStruct((m, total), F32),
        compiler_params=_cp(VMEM_BIG, ("arbitrary",)),
    )(a, *secs)


class _ShiftedRows:
    def __init__(self, src, buf, sems, *, per_seq, tt, steps):
        self.src, self.buf, self.sems = src, buf, sems
        self.per_seq, self.tt, self.steps = per_seq, tt, steps

    def _do(self, step, slot, act):
        b, j = step // self.per_seq, step % self.per_seq
        tt = self.tt

        @pl.when(j == 0)
        def _():
            act(pltpu.make_async_copy(self.src.at[b, pl.ds(0, tt - CHUNK), :],
                                      self.buf.at[slot, pl.ds(CHUNK, tt - CHUNK), :], self.sems.at[slot]))

        if self.per_seq > 1:
            @pl.when(j > 0)
            def _():
                act(pltpu.make_async_copy(self.src.at[b, pl.ds(j * tt - CHUNK, tt), :], self.buf.at[slot],
                                          self.sems.at[slot]))

    def tile(self, i):
        slot = i % 2

        @pl.when(i == 0)
        def _():
            self._do(i, slot, lambda cp: cp.start())

        self._do(i, slot, lambda cp: cp.wait())

        @pl.when(i + 1 < self.steps)
        def _():
            self._do(i + 1, 1 - slot, lambda cp: cp.start())

        return slot


def _embed_norm(x, lead, g, *, tr, name, rider=None):
    bsz, s_len, d = x.shape
    t_seq = s_len + CHUNK
    per_seq = t_seq // tr
    steps = bsz * per_seq
    n = bsz * t_seq

    def body(x_hbm, lead_ref, g_ref, h0_ref, h_ref, buf, sems):
        i = pl.program_id(0)
        slot = _ShiftedRows(x_hbm, buf, sems, per_seq=per_seq, tt=tr, steps=steps).tile(i)

        @pl.when(i % per_seq == 0)
        def _():
            buf[slot, 0:CHUNK, :] = lead_ref[...]

        xv = buf[slot]
        h0_ref[...] = xv
        r = lax.rsqrt(jnp.mean(xv * xv, axis=-1, keepdims=True) + EPS)
        h_ref[...] = (xv * r * g_ref[...]).astype(h_ref.dtype)

    row = pl.BlockSpec((tr, d), lambda i: (i, 0))
    outs, ridden = _hosted_call(
        body, rider, name=name, grid=(steps,),
        in_specs=[ANY, pl.BlockSpec((CHUNK, d), lambda i: (0, 0)), pl.BlockSpec((1, d), lambda i: (0, 0))],
        out_specs=[row, row],
        out_shape=[jax.ShapeDtypeStruct((n, d), F32), jax.ShapeDtypeStruct((n, d), BF16)],
        scratch_shapes=[pltpu.VMEM((2, tr, d), F32), pltpu.SemaphoreType.DMA((2,))],
        compiler_params=_cp(VMEM_BIG, ("arbitrary",)), args=(x, lead, g))
    return (*outs, ridden)


def _rms_fwd(x, g, *, tr, name):
    n, d = x.shape

    def body(x_ref, g_ref, o_ref):
        xv = x_ref[...]
        r = lax.rsqrt(jnp.mean(xv * xv, axis=-1, keepdims=True) + EPS)
        o_ref[...] = (xv * r * g_ref[...]).astype(o_ref.dtype)

    return pl.pallas_call(
        body, name=name, grid=(n // tr,),
        in_specs=[pl.BlockSpec((tr, d), lambda i: (i, 0)), pl.BlockSpec((1, d), lambda i: (0, 0))],
        out_specs=pl.BlockSpec((tr, d), lambda i: (i, 0)),
        out_shape=jax.ShapeDtypeStruct((n, d), BF16),
        compiler_params=_cp(VMEM_BIG),
    )(x, g)


def _rms_bwd_math(xv, g, dy):
    r = lax.rsqrt(jnp.mean(xv * xv, axis=-1, keepdims=True) + EPS)
    xh = xv * r
    gdy = dy * g
    dx = r * (gdy - xh * jnp.mean(xh * gdy, axis=-1, keepdims=True))
    return dx, jnp.sum(dy * xh, axis=0, keepdims=True)


def _mlp_up_bwd_norm(dup, w_up, x, g, res, *, tr, name, rider=None):
    n, d = x.shape
    ff = dup.shape[1]

    def body(dup_ref, w_ref, x_ref, g_ref, res_ref, o_ref, ob_ref, dg_ref):
        dh = _nt(dup_ref[...], w_ref[...])
        dx, dg = _rms_bwd_math(x_ref[...], g_ref[...], dh)
        tot = res_ref[...] + dx
        o_ref[...] = tot
        ob_ref[...] = tot.astype(BF16)

        @pl.when(pl.program_id(0) == 0)
        def _():
            dg_ref[...] = dg

        @pl.when(pl.program_id(0) > 0)
        def _():
            dg_ref[...] += dg

    row = pl.BlockSpec((tr, d), lambda i: (i, 0))
    vec = pl.BlockSpec((1, d), lambda i: (0, 0))
    outs, ridden = _hosted_call(
        body, rider, name=name, grid=(n // tr,),
        in_specs=[pl.BlockSpec((tr, ff), lambda i: (i, 0)), pl.BlockSpec((d, ff), lambda i: (0, 0)), row, vec, row],
        out_specs=[row, row, vec],
        out_shape=[jax.ShapeDtypeStruct((n, d), F32), jax.ShapeDtypeStruct((n, d), BF16),
                   jax.ShapeDtypeStruct((1, d), F32)],
        scratch_shapes=[], compiler_params=_cp(VMEM_BIG, ("arbitrary",)), args=(dup, w_up, x, g, res))
    return (*outs, ridden)


def _mlp_down_loss(act, w_down, x1, gf, tgt, *, t_seq, tr, name):
    n, d = x1.shape
    ff = act.shape[1]
    per_seq = t_seq // tr
    steps = n // tr

    def body(a_ref, w_ref, x_ref, g_ref, t_hbm, dx_ref, dxb_ref, dg_ref, loss_ref, tbuf, tsems):
        i = pl.program_id(0)
        slot = _ShiftedRows(t_hbm, tbuf, tsems, per_seq=per_seq, tt=tr, steps=steps).tile(i)

        @pl.when(i % per_seq == 0)
        def _():
            tbuf[slot, 0:CHUNK, :] = jnp.zeros((CHUNK, d), F32)

        t_ref = tbuf.at[slot]
        xv = x_ref[...] + _nn(a_ref[...], w_ref[...])
        g = g_ref[...]
        r = lax.rsqrt(jnp.mean(xv * xv, axis=-1, keepdims=True) + EPS)
        xh = xv * r
        pos = (i % per_seq) * tr + _iota2((tr, 1), 0)
        real = pos >= CHUNK
        err = jnp.where(real, xh * g - t_ref[...], 0.0)
        dy = err * (1.0 / d)
        gdy = dy * g
        dx = r * (gdy - xh * jnp.mean(xh * gdy, axis=-1, keepdims=True))
        dx_ref[...] = dx
        dxb_ref[...] = dx.astype(BF16)
        dg = jnp.sum(dy * xh, axis=0, keepdims=True)
        ls = 0.5 * jnp.sum(jnp.mean(err * err, axis=-1, keepdims=True), axis=0, keepdims=True)
        ls = jnp.where(_iota2((1, LANE), 1) == 0, ls, 0.0)

        @pl.when(i == 0)
        def _():
            dg_ref[...] = dg
            loss_ref[...] = ls

        @pl.when(i > 0)
        def _():
            dg_ref[...] += dg
            loss_ref[...] += ls

    row = pl.BlockSpec((tr, d), lambda i: (i, 0))
    vec = pl.BlockSpec((1, d), lambda i: (0, 0))
    one = pl.BlockSpec((1, LANE), lambda i: (0, 0))
    return pl.pallas_call(
        body, name=name, grid=(n // tr,),
        in_specs=[pl.BlockSpec((tr, ff), lambda i: (i, 0)), pl.BlockSpec((ff, d), lambda i: (0, 0)), row, vec, ANY],
        out_specs=[row, row, vec, one],
        out_shape=[jax.ShapeDtypeStruct((n, d), F32), jax.ShapeDtypeStruct((n, d), BF16),
                   jax.ShapeDtypeStruct((1, d), F32), jax.ShapeDtypeStruct((1, LANE), F32)],
        scratch_shapes=[pltpu.VMEM((2, tr, d), F32), pltpu.SemaphoreType.DMA((2,))],
        compiler_params=_cp(VMEM_BIG, ("arbitrary",)),
    )(act, w_down, x1, gf, tgt)


def _gnorm_fwd(o_dn, o_gla, projp, g_dn, g_gla, *, tr, name):
    n = o_dn.shape[0]
    w = NH * DN_D

    def body(odn_ref, ogl_ref, z_ref, r_ref, gdn_ref, ggl_ref, mix_ref):
        for grp, (o_ref, gate_ref, gain_ref) in enumerate(((odn_ref, z_ref, gdn_ref), (ogl_ref, r_ref, ggl_ref))):
            gain = gain_ref[...]
            for h in range(NH):
                sl = slice(h * DN_D, (h + 1) * DN_D)
                o = o_ref[:, sl].astype(F32)
                z = gate_ref[:, sl].astype(F32)
                r = lax.rsqrt(jnp.mean(o * o, axis=-1, keepdims=True) + EPS)
                y = (o * r * gain) * (z * _sigmoid(z))
                mix_ref[:, grp * w + h * DN_D: grp * w + (h + 1) * DN_D] = y.astype(mix_ref.dtype)

    row = pl.BlockSpec((tr, w), lambda i: (i, 0))
    vec = pl.BlockSpec((1, DN_D), lambda i: (0, 0))
    return pl.pallas_call(
        body, name=name, grid=(n // tr,),
        in_specs=[row, row, pl.BlockSpec((tr, w), lambda i: (i, C_DZ // w)),
                  pl.BlockSpec((tr, w), lambda i: (i, C_GR // w)), vec, vec],
        out_specs=pl.BlockSpec((tr, 2 * w), lambda i: (i, 0)),
        out_shape=jax.ShapeDtypeStruct((n, 2 * w), BF16),
        compiler_params=_cp(VMEM_BIG),
    )(o_dn, o_gla, projp, projp, g_dn, g_gla)


def _gnorm_bwd(dmix, o_dn, o_gla, projp, g_dn, g_gla, *, tr, name):
    n = o_dn.shape[0]
    w = NH * DN_D

    def body(dm_ref, odn_ref, ogl_ref, z_ref, r_ref, gdn_ref, ggl_ref,
             dodn_ref, ddz_ref, dogl_ref, dgr_ref, dgdn_ref, dggl_ref):
        first = pl.program_id(0) == 0
        groups = ((odn_ref, z_ref, gdn_ref, dodn_ref, ddz_ref, dgdn_ref),
                  (ogl_ref, r_ref, ggl_ref, dogl_ref, dgr_ref, dggl_ref))
        for grp, (o_ref, gate_ref, gain_ref, do_ref, dgate_ref, dgain_ref) in enumerate(groups):
            gain = gain_ref[...]
            dgain = jnp.zeros((1, DN_D), F32)
            for h in range(NH):
                sl = slice(h * DN_D, (h + 1) * DN_D)
                o = o_ref[:, sl].astype(F32)
                z = gate_ref[:, sl].astype(F32)
                dm = dm_ref[:, grp * w + h * DN_D: grp * w + (h + 1) * DN_D].astype(F32)
                r = lax.rsqrt(jnp.mean(o * o, axis=-1, keepdims=True) + EPS)
                oh = o * r
                s = _sigmoid(z)
                dn = dm * (z * s)
                dgate_ref[:, sl] = (dm * (oh * gain) * (s * (1.0 + z * (1.0 - s)))).astype(dgate_ref.dtype)
                gdn = dn * gain
                do_ref[:, sl] = (r * (gdn - oh * jnp.mean(oh * gdn, axis=-1, keepdims=True))).astype(do_ref.dtype)
                dgain = dgain + jnp.sum(dn * oh, axis=0, keepdims=True)

            @pl.when(first)
            def _():
                dgain_ref[...] = dgain

            @pl.when(jnp.logical_not(first))
            def _():
                dgain_ref[...] += dgain

    row = pl.BlockSpec((tr, w), lambda i: (i, 0))
    vec = pl.BlockSpec((1, DN_D), lambda i: (0, 0))
    big = jax.ShapeDtypeStruct((n, w), F32)
    gate = jax.ShapeDtypeStruct((n, w), BF16)
    small = jax.ShapeDtypeStruct((1, DN_D), F32)
    return pl.pallas_call(
        body, name=name, grid=(n // tr,),
        in_specs=[pl.BlockSpec((tr, 2 * w), lambda i: (i, 0)), row, row,
                  pl.BlockSpec((tr, w), lambda i: (i, C_DZ // w)), pl.BlockSpec((tr, w), lambda i: (i, C_GR // w)), vec, vec],
        out_specs=[row, row, row, row, vec, vec],
        out_shape=[gate, gate, gate, gate, small, small],
        compiler_params=_cp(VMEM_BIG),
    )(dmix, o_dn, o_gla, projp, projp, g_dn, g_gla)


QKV_W = 3 * NH * DN_D
HALO = 8


def _conv_z(xs_ref, cw_ref, tt):
    z = cw_ref[0:1, :] * xs_ref[pl.ds(HALO - 3, tt), :]
    for j in range(1, 4):
        z = z + cw_ref[j:j + 1, :] * xs_ref[pl.ds(HALO - 3 + j, tt), :]
    return z


def _dnprep_fwd(projp, conv_w, *, bsz, t_seq, tt, name):
    n = bsz * t_seq
    per_seq = t_seq // tt
    hw = NH * DN_D

    def body(x_ref, halo_ref, cw_ref, q_ref, k_ref, v_ref, xs_ref):
        i = pl.program_id(1)
        xs_ref[0:HALO, :] = jnp.where(i == 0, 0.0, halo_ref[...].astype(F32))
        xs_ref[HALO:HALO + tt, :] = x_ref[...].astype(F32)
        z = _conv_z(xs_ref, cw_ref, tt)
        a = z * _sigmoid(z)
        for grp, o_ref in enumerate((q_ref, k_ref)):
            for h in range(NH):
                ah = a[:, grp * hw + h * DN_D: grp * hw + (h + 1) * DN_D]
                rs = lax.rsqrt(jnp.sum(ah * ah, axis=-1, keepdims=True) + EPS)
                o_ref[:, h * DN_D:(h + 1) * DN_D] = (ah * rs).astype(o_ref.dtype)
        v_ref[...] = a[:, 2 * hw:3 * hw].astype(v_ref.dtype)

    def halo_map(b, i):
        return (jnp.maximum((b * t_seq + i * tt) // HALO - 1, 0), 0)

    out = pl.BlockSpec((tt, hw), lambda b, i: (b * per_seq + i, 0))
    sds = jax.ShapeDtypeStruct((n, hw), BF16)
    return pl.pallas_call(
        body, name=name, grid=(bsz, per_seq),
        in_specs=[pl.BlockSpec((tt, QKV_W), lambda b, i: (b * per_seq + i, 0)),
                  pl.BlockSpec((HALO, QKV_W), halo_map),
                  pl.BlockSpec((4, QKV_W), lambda b, i: (0, 0))],
        out_specs=[out, out, out], out_shape=[sds, sds, sds],
        scratch_shapes=[pltpu.VMEM((tt + HALO, QKV_W), F32)],
        compiler_params=_cp(VMEM_BIG),
    )(projp, projp, conv_w)


def _dnprep_bwd_a(projp, conv_w, dq, dk, dv, *, bsz, t_seq, tt, name):
    n = bsz * t_seq
    per_seq = t_seq // tt
    hw = NH * DN_D

    def body(x_ref, halo_ref, cw_ref, dq_ref, dk_ref, dv_ref, dz_ref, dcw_ref, xs_ref):
        b, i = pl.program_id(0), pl.program_id(1)
        xs_ref[0:HALO, :] = jnp.where(i == 0, 0.0, halo_ref[...].astype(F32))
        xs_ref[HALO:HALO + tt, :] = x_ref[...].astype(F32)
        z = _conv_z(xs_ref, cw_ref, tt)
        s = _sigmoid(z)
        a = z * s
        dsilu = s * (1.0 + z * (1.0 - s))
        for grp, d_ref in enumerate((dq_ref, dk_ref)):
            for h in range(NH):
                sl = slice(grp * hw + h * DN_D, grp * hw + (h + 1) * DN_D)
                ah = a[:, sl]
                rs = lax.rsqrt(jnp.sum(ah * ah, axis=-1, keepdims=True) + EPS)
                y = ah * rs
                dy = d_ref[:, h * DN_D:(h + 1) * DN_D]
                da = rs * (dy - y * jnp.sum(dy * y, axis=-1, keepdims=True))
                dz_ref[:, sl] = da * dsilu[:, sl]
        dz_ref[:, 2 * hw:3 * hw] = dv_ref[...] * dsilu[:, 2 * hw:3 * hw]
        dz = dz_ref[...]
        first = jnp.logical_and(b == 0, i == 0)
        for j in range(4):
            part = jnp.sum(dz * xs_ref[pl.ds(HALO - 3 + j, tt), :], axis=0, keepdims=True)

            @pl.when(first)
            def _():
                dcw_ref[j:j + 1, :] = part

            @pl.when(jnp.logical_not(first))
            def _():
                dcw_ref[j:j + 1, :] += part

    def halo_map(b, i):
        return (jnp.maximum((b * t_seq + i * tt) // HALO - 1, 0), 0)

    hrow = pl.BlockSpec((tt, hw), lambda b, i: (b * per_seq + i, 0))
    return pl.pallas_call(
        body, name=name, grid=(bsz, per_seq),
        in_specs=[pl.BlockSpec((tt, QKV_W), lambda b, i: (b * per_seq + i, 0)),
                  pl.BlockSpec((HALO, QKV_W), halo_map),
                  pl.BlockSpec((4, QKV_W), lambda b, i: (0, 0)), hrow, hrow, hrow],
        out_specs=[pl.BlockSpec((tt, QKV_W), lambda b, i: (b * per_seq + i, 0)),
                   pl.BlockSpec((4, QKV_W), lambda b, i: (0, 0))],
        out_shape=[jax.ShapeDtypeStruct((n, QKV_W), F32), jax.ShapeDtypeStruct((4, QKV_W), F32)],
        scratch_shapes=[pltpu.VMEM((tt + HALO, QKV_W), F32)],
        compiler_params=_cp(VMEM_BIG),
    )(projp, projp, conv_w, dq, dk, dv)


def _dnprep_bwd_b(dz, conv_w, *, bsz, t_seq, tt, name):
    n = bsz * t_seq
    per_seq = t_seq // tt
    last_blk = n // HALO - 1

    def body(dz_ref, halo_ref, cw_ref, dx_ref, ds_ref):
        i = pl.program_id(1)
        ds_ref[0:tt, :] = dz_ref[...].astype(F32)
        ds_ref[tt:tt + HALO, :] = jnp.where(i == per_seq - 1, 0.0, halo_ref[...].astype(F32))
        dx = cw_ref[0:1, :] * ds_ref[pl.ds(3, tt), :]
        for j in range(1, 4):
            dx = dx + cw_ref[j:j + 1, :] * ds_ref[pl.ds(3 - j, tt), :]
        dx_ref[...] = dx.astype(dx_ref.dtype)

    def halo_map(b, i):
        return (jnp.minimum((b * t_seq + (i + 1) * tt) // HALO, last_blk), 0)

    row = pl.BlockSpec((tt, QKV_W), lambda b, i: (b * per_seq + i, 0))
    return pl.pallas_call(
        body, name=name, grid=(bsz, per_seq),
        in_specs=[row, pl.BlockSpec((HALO, QKV_W), halo_map), pl.BlockSpec((4, QKV_W), lambda b, i: (0, 0))],
        out_specs=row, out_shape=jax.ShapeDtypeStruct((n, QKV_W), BF16),
        scratch_shapes=[pltpu.VMEM((tt + HALO, QKV_W), F32)],
        compiler_params=_cp(VMEM_BIG),
    )(dz, dz, conv_w)


def _masks64():
    r = _iota2((CHUNK, CHUNK), 0)
    c = _iota2((CHUNK, CHUNK), 1)
    return r, c


def _group(nc_seq, target=5):
    return max(g for g in range(1, target + 1) if nc_seq % g == 0)


def _round_robin(chains):
    live = list(chains)
    while live:
        nxt = []
        for ch in live:
            try:
                next(ch)
                nxt.append(ch)
            except StopIteration:
                pass
        live = nxt
        yield


def _run(chains):
    for _ in _round_robin(chains):
        pass


def _per_chunk(inner, kinds, grp):
    def body(*refs):
        chains = []
        for gi in range(grp):
            views = []
            for r, kind in zip(refs, kinds):
                if kind == "row":
                    views.append(r.at[pl.ds(gi * CHUNK, CHUNK)])
                elif kind == "lead":
                    views.append(r.at[pl.ds(gi, 1)])
                else:
                    views.append(r)
            chains.append(inner(gi, *views))
        _run(chains)
    return body


def _accumulate(ref, val, gi):
    if gi > 0:
        ref[...] += val
        return
    first = pl.program_id(0) == 0

    @pl.when(first)
    def _():
        ref[...] = val

    @pl.when(jnp.logical_not(first))
    def _():
        ref[...] += val


ANY = pl.BlockSpec(memory_space=pl.ANY)


def _place():
    return lax.axis_index("x"), lax.axis_index("y"), lax.axis_index("c")


def _other_chips(x, y):
    return [(1 - x, y, 2 * (1 - x) + y), (x, 1 - y, 2 * x + 1 - y), (1 - x, 1 - y, 2 * (1 - x) + 1 - y)]


class _GatherRider:
    def __init__(self, bufs, split):
        self.inputs = list(bufs)
        self.split = list(split)
        self.out_shapes = [jax.ShapeDtypeStruct(b.shape, b.dtype) for b in bufs]
        self.aliases = {i: i for i in range(len(bufs))}
        self.sems = [pltpu.SemaphoreType.DMA((len(bufs), 3))] * 4

    def _rows(self, k, buf, c, mine=True):
        r = buf.shape[1]
        if not self.split[k]:
            return pl.ds(0, r)
        return pl.ds((c if mine else 1 - c) * (r // 2), r // 2)

    def _ici(self, k, d, bufs, sems, c, px, py, block):
        rows = self._rows(k, bufs[k], c)
        return pltpu.make_async_remote_copy(
            src_ref=bufs[k].at[block, rows, :], dst_ref=bufs[k].at[block, rows, :], send_sem=sems[0].at[k, d],
            recv_sem=sems[1].at[k, d], device_id=(px, py, c), device_id_type=MESH)

    def _pass(self, k, d, bufs, sems, x, y, c, block, mine):
        rows = self._rows(k, bufs[k], c, mine)
        return pltpu.make_async_remote_copy(
            src_ref=bufs[k].at[block, rows, :], dst_ref=bufs[k].at[block, rows, :], send_sem=sems[2].at[k, d],
            recv_sem=sems[3].at[k, d], device_id=(x, y, 1 - c), device_id_type=MESH)

    def first(self, in_refs, bufs, sems):
        x, y, c = _place()
        for k in range(len(bufs)):
            for d, (px, py, _) in enumerate(_other_chips(x, y)):
                self._ici(k, d, bufs, sems, c, px, py, 2 * x + y).start()

    def last(self, in_refs, bufs, sems):
        x, y, c = _place()
        chips = _other_chips(x, y)
        for k in range(len(bufs)):
            for d, (px, py, pj) in enumerate(chips):
                self._ici(k, d, bufs, sems, c, px, py, pj).wait_recv()
                if self.split[k]:
                    self._pass(k, d, bufs, sems, x, y, c, pj, True).start()
        for k in range(len(bufs)):
            for d, (px, py, pj) in enumerate(chips):
                if self.split[k]:
                    self._pass(k, d, bufs, sems, x, y, c, pj, False).wait_recv()
                    self._pass(k, d, bufs, sems, x, y, c, pj, True).wait_send()
                self._ici(k, d, bufs, sems, c, px, py, 2 * x + y).wait_send()


def _hosted_call(body, rider, *, name, grid, in_specs, out_specs, out_shape, scratch_shapes, compiler_params, args):
    if rider is None:
        outs = pl.pallas_call(body, name=name, grid=grid, in_specs=in_specs, out_specs=out_specs, out_shape=out_shape,
                              scratch_shapes=scratch_shapes, compiler_params=compiler_params)(*args)
        return list(outs), []
    n_in, n_out, n_scr = len(in_specs), len(out_specs), len(scratch_shapes)
    r_in, r_out = len(rider.inputs), len(rider.out_shapes)
    compiler_params = _cp(compiler_params.vmem_limit_bytes, ("arbitrary",) * len(grid))

    def full_body(*refs):
        ins = refs[:n_in]
        rins = refs[n_in:n_in + r_in]
        outs = refs[n_in + r_in:n_in + r_in + n_out]
        routs = refs[n_in + r_in + n_out:n_in + r_in + n_out + r_out]
        rest = refs[n_in + r_in + n_out + r_out:]
        scr, sems = rest[:n_scr], rest[n_scr:]
        ids = [pl.program_id(a) for a in range(len(grid))]
        is_first = functools.reduce(jnp.logical_and, [i == 0 for i in ids])
        is_last = functools.reduce(jnp.logical_and, [i == g - 1 for i, g in zip(ids, grid)])

        @pl.when(is_first)
        def _():
            rider.first(rins, routs, sems)

        body(*ins, *outs, *scr)

        @pl.when(is_last)
        def _():
            rider.last(rins, routs, sems)

    res = pl.pallas_call(
        full_body, name=name, grid=grid, in_specs=list(in_specs) + [ANY] * r_in,
        out_specs=list(out_specs) + [ANY] * r_out, out_shape=list(out_shape) + list(rider.out_shapes),
        input_output_aliases={n_in + i: n_out + o for i, o in rider.aliases.items()},
        scratch_shapes=list(scratch_shapes) + list(rider.sems), compiler_params=compiler_params,
    )(*args, *rider.inputs)
    return list(res[:n_out]), list(res[n_out:])


def _exchange_now(rider, *, name):
    r_in = len(rider.inputs)

    def body(*refs):
        rins = refs[:r_in]
        routs = refs[r_in:r_in + len(rider.out_shapes)]
        sems = refs[r_in + len(rider.out_shapes):]
        rider.first(rins, routs, sems)
        rider.last(rins, routs, sems)

    return pl.pallas_call(
        body, name=name, in_specs=[ANY] * r_in, out_specs=[ANY] * len(rider.out_shapes), out_shape=list(rider.out_shapes),
        input_output_aliases=dict(rider.aliases), scratch_shapes=list(rider.sems),
    )(*rider.inputs)


def _to_slot(where, a, dtype, *, name):
    r, cols = a.shape
    tr = _tile(r, 256, 16) if r > 256 else r

    def body(w_ref, a_ref, o_ref):
        o_ref[0] = a_ref[...].astype(o_ref.dtype)

    return pl.pallas_call(
        body, name=name,
        grid_spec=pltpu.PrefetchScalarGridSpec(
            num_scalar_prefetch=1, grid=(r // tr,),
            in_specs=[pl.BlockSpec((tr, cols), lambda i, w: (i, 0))],
            out_specs=pl.BlockSpec((1, tr, cols), lambda i, w: (w[1], i, 0))),
        out_shape=jax.ShapeDtypeStruct((N_CHIPS, r, cols), dtype), compiler_params=_cp(VMEM_BIG),
    )(where, a)


def _tri_inv(a_strict):
    r, c = _masks64()
    eye = (r == c).astype(F32)
    blk16 = (r // 16) == (c // 16)
    blk32 = (r // 32) == (c // 32)
    ld = jnp.where(blk16, a_strict, 0.0)
    x = eye - ld
    p = _nn(ld, ld)
    yield
    for step in range(3):
        xp = _nn(x, p)
        if step < 2:
            p = _nn(p, p)
        x = x + xp
        yield
    for lk in (jnp.where(jnp.logical_and(blk32, jnp.logical_not(blk16)), a_strict, 0.0),
               jnp.where(blk32, 0.0, a_strict)):
        y = x - eye
        s = lk + _nn(y, lk)
        yield
        x = x - s - _nn(s, y)
        yield
    return x


def _dn_gates(sa, alog, dtb, chunk_in_seq):
    rows = _iota2((CHUNK, LANE), 0)
    valid = jnp.logical_or(rows >= N_PAD, chunk_in_seq > 0)
    beta_t = _sigmoid(sa)
    ea = jnp.exp(alog)
    g_t = jnp.where(valid, -ea * _softplus(sa + dtb), 0.0)
    r, c = _masks64()
    ltri = (r >= c).astype(F32)
    gam_t = _tri_sum(ltri, g_t)
    return beta_t, g_t, gam_t, valid, ea


def _dn_intra_fwd(qn, kn, v, projp, alog_row, dtb_row, *, nc_seq, name, rider=None):
    n = qn.shape[0]
    nct = n // CHUNK
    hw = NH * DN_D
    scale = DN_D ** -0.5

    grp = _group(nc_seq)

    def inner(gi, q_ref, k_ref, v_ref, sa_ref, al_ref, dt_ref, u_ref, w_ref, qg_ref, kd_ref, p_ref, t_ref, gl_ref):
        ci = (pl.program_id(0) * grp + gi) % nc_seq
        beta_t, _, gam_t, _, _ = _dn_gates(sa_ref[...], al_ref[...], dt_ref[...], ci)
        yield
        gam_tt = gam_t.T
        r, c = _masks64()
        incl = r >= c
        strict = r > c

        def head(h):
            sl = slice(h * DN_D, (h + 1) * DN_D)
            beta_w = jnp.broadcast_to(beta_t[:, h:h + 1], (CHUNK, DN_D))
            gam_w = jnp.broadcast_to(gam_t[:, 4 + h:5 + h], (CHUNK, DN_D))
            gam_row = gam_tt[4 + h:5 + h, :]
            gl = gam_t[CHUNK - 1:CHUNK, 4 + h:5 + h]
            dec = jnp.exp(jnp.where(incl, gam_w[:, 0:CHUNK] - gam_row, -jnp.inf))
            kh = k_ref[:, sl].astype(F32)
            qh = q_ref[:, sl].astype(F32) * scale
            vh = v_ref[:, sl].astype(F32)
            kk = _nt(kh, kh)
            qk = _nt(qh, kh)
            yield
            a = jnp.where(strict, beta_w[:, 0:CHUNK] * kk * dec, 0.0)
            tm = yield from _tri_inv(a)
            egam_w = jnp.exp(gam_w)
            u_ref[:, sl] = _nn(tm, beta_w * vh).astype(u_ref.dtype)
            w_ref[:, sl] = _nn(tm, (beta_w * egam_w) * kh).astype(w_ref.dtype)
            qg_ref[:, sl] = (egam_w * qh).astype(qg_ref.dtype)
            kd_ref[:, sl] = (jnp.exp(gl - gam_w) * kh).astype(kd_ref.dtype)
            p_ref[0, h] = qk * dec
            t_ref[0, h] = tm
            gl_ref[0, h:h + 1, :] = jnp.broadcast_to(jnp.exp(gl), (1, LANE))

        yield from _round_robin([head(h) for h in range(NH)])

    rows = grp * CHUNK
    row = pl.BlockSpec((rows, hw), lambda i: (i, 0))
    vec = pl.BlockSpec((1, LANE), lambda i: (0, 0))
    mat = pl.BlockSpec((grp, NH, CHUNK, CHUNK), lambda i: (i, 0, 0, 0))
    big = jax.ShapeDtypeStruct((n, hw), BF16)
    msd = jax.ShapeDtypeStruct((nct, NH, CHUNK, CHUNK), F32)
    kinds = ["row"] * 4 + ["whole"] * 2 + ["row"] * 4 + ["lead"] * 3
    outs, ridden = _hosted_call(
        _per_chunk(inner, kinds, grp), rider, name=name, grid=(nct // grp,),
        in_specs=[row, row, row, pl.BlockSpec((rows, LANE), lambda i: (i, 0)), vec, vec],
        out_specs=[row, row, row, row, mat, mat, pl.BlockSpec((grp, NH, LANE), lambda i: (i, 0, 0))],
        out_shape=[big, big, big, big, msd, msd, jax.ShapeDtypeStruct((nct, NH, LANE), F32)],
        scratch_shapes=[], compiler_params=_cp(VMEM_BIG, ("arbitrary",)),
        args=(qn, kn, v, projp, alog_row, dtb_row))
    return (*outs, ridden)


def _dn_scan_fwd(u, w, qg, kd, p, gl, *, bsz, nc_seq, name, rider=None):
    hw = NH * DN_D
    t_seq = nc_seq * CHUNK
    u, w, qg, kd = (z.reshape(bsz, t_seq, hw) for z in (u, w, qg, kd))
    p = p.reshape(bsz, nc_seq, NH, CHUNK, CHUNK)
    gl = gl.reshape(bsz, nc_seq, NH, LANE)
    grp = _group(nc_seq)

    def body(u_ref, w_ref, qg_ref, kd_ref, p_ref, gl_ref, o_ref, vn_ref, hist_ref, s_ref):
        @pl.when(pl.program_id(0) == 0)
        def _():
            s_ref[...] = jnp.zeros_like(s_ref)

        def chain(b, h, gi):
            sl = slice(h * DN_D, (h + 1) * DN_D)
            rows = pl.ds(gi * CHUNK, CHUNK)
            s = s_ref[b, h]
            hist_ref[b, gi, h] = s.astype(hist_ref.dtype)
            ws = _nn(w_ref[b, rows, sl], s)
            qs = _nn(qg_ref[b, rows, sl], s)
            yield
            vn = u_ref[b, rows, sl] - ws
            vn_ref[b, rows, sl] = vn.astype(vn_ref.dtype)
            o_ref[b, rows, sl] = (qs + _nn(p_ref[b, gi, h], vn)).astype(o_ref.dtype)
            s_ref[b, h] = gl_ref[b, gi, h:h + 1, :] * s + _tn(kd_ref[b, rows, sl], vn)

        for gi in range(grp):
            _run([chain(b, h, gi) for b in range(bsz) for h in range(NH)])

    row = pl.BlockSpec((bsz, grp * CHUNK, hw), lambda i: (0, i, 0))
    outs, ridden = _hosted_call(
        body, rider, name=name, grid=(nc_seq // grp,),
        in_specs=[row, row, row, row, pl.BlockSpec((bsz, grp, NH, CHUNK, CHUNK), lambda i: (0, i, 0, 0, 0)),
                  pl.BlockSpec((bsz, grp, NH, LANE), lambda i: (0, i, 0, 0))],
        out_specs=[row, row, pl.BlockSpec((bsz, grp, NH, DN_D, DN_D), lambda i: (0, i, 0, 0, 0))],
        out_shape=[jax.ShapeDtypeStruct((bsz, t_seq, hw), BF16), jax.ShapeDtypeStruct((bsz, t_seq, hw), BF16),
                   jax.ShapeDtypeStruct((bsz, nc_seq, NH, DN_D, DN_D), F32)],
        scratch_shapes=[pltpu.VMEM((bsz, NH, DN_D, DN_D), F32)],
        compiler_params=_cp(VMEM_BIG, ("arbitrary",)), args=(u, w, qg, kd, p, gl))
    o, vn, hist = outs
    return o.reshape(bsz * t_seq, hw), vn.reshape(bsz * t_seq, hw), hist, ridden


def _dn_scan_bwd(do, w, qg, kd, vn, p, gl, hist, *, bsz, nc_seq, name):
    hw = NH * DN_D
    t_seq = nc_seq * CHUNK
    do, w, qg, kd, vn = (z.reshape(bsz, t_seq, hw) for z in (do, w, qg, kd, vn))
    p = p.reshape(bsz, nc_seq, NH, CHUNK, CHUNK)
    gl = gl.reshape(bsz, nc_seq, NH, LANE)
    grp = _group(nc_seq)

    def body(do_ref, w_ref, qg_ref, kd_ref, vn_ref, p_ref, gl_ref, hist_ref,
             du_ref, dw_ref, dqg_ref, dkd_ref, dgl_ref, ds_ref):
        @pl.when(pl.program_id(0) == 0)
        def _():
            ds_ref[...] = jnp.zeros_like(ds_ref)

        def chain(b, h, gi):
            sl = slice(h * DN_D, (h + 1) * DN_D)
            rows = pl.ds(gi * CHUNK, CHUNK)
            s = hist_ref[b, gi, h]
            dsn = ds_ref[b, h]
            doh = do_ref[b, rows, sl]
            vnh = vn_ref[b, rows, sl]
            kdh = kd_ref[b, rows, sl]
            dvn = _tn(p_ref[b, gi, h], doh) + _nn(kdh, dsn)
            du_ref[b, rows, sl] = dvn.astype(du_ref.dtype)
            dqg_ref[b, rows, sl] = _nt(doh, s).astype(dqg_ref.dtype)
            dkd_ref[b, rows, sl] = _nt(vnh, dsn).astype(dkd_ref.dtype)
            ds_part = _tn(qg_ref[b, rows, sl], doh) + gl_ref[b, gi, h:h + 1, :] * dsn
            dgl = jnp.sum(jnp.sum(dsn * s, axis=0, keepdims=True), axis=1, keepdims=True)
            dgl_ref[b, gi, h:h + 1, :] = jnp.broadcast_to(dgl, (1, LANE))
            yield
            dw_ref[b, rows, sl] = (-_nt(dvn, s)).astype(dw_ref.dtype)
            ds_ref[b, h] = ds_part - _tn(w_ref[b, rows, sl], dvn)

        for gi in reversed(range(grp)):
            _run([chain(b, h, gi) for b in range(bsz) for h in range(NH)])

    steps = nc_seq // grp
    rev = lambda i: steps - 1 - i
    row = pl.BlockSpec((bsz, grp * CHUNK, hw), lambda i: (0, rev(i), 0))
    mat = pl.BlockSpec((bsz, grp, NH, CHUNK, CHUNK), lambda i: (0, rev(i), 0, 0, 0))
    glb = pl.BlockSpec((bsz, grp, NH, LANE), lambda i: (0, rev(i), 0, 0))
    big = jax.ShapeDtypeStruct((bsz, t_seq, hw), BF16)
    outs = pl.pallas_call(
        body, name=name, grid=(steps,),
        in_specs=[row, row, row, row, row, mat, glb,
                  pl.BlockSpec((bsz, grp, NH, DN_D, DN_D), lambda i: (0, rev(i), 0, 0, 0))],
        out_specs=[row, row, row, row, glb],
        out_shape=[big, big, jax.ShapeDtypeStruct(big.shape, F32), jax.ShapeDtypeStruct(big.shape, F32),
                   jax.ShapeDtypeStruct((bsz, nc_seq, NH, LANE), F32)],
        scratch_shapes=[pltpu.VMEM((bsz, NH, DN_D, DN_D), F32)],
        compiler_params=_cp(VMEM_BIG, ("arbitrary",)),
    )(do, w, qg, kd, vn, p, gl, hist)
    du, dw, dqg, dkd, dgl = outs
    n = bsz * t_seq
    return (du.reshape(n, hw), dw.reshape(n, hw), dqg.reshape(n, hw), dkd.reshape(n, hw),
            dgl.reshape(bsz * nc_seq, NH, LANE))


def _dn_intra_bwd(qn, kn, v, projp, alog_row, dtb_row, u, w, tmat, du, dw, dqg, dkd, do, vn, dgl, *, nc_seq, name,
                  rider=None):
    n = qn.shape[0]
    nct = n // CHUNK
    hw = NH * DN_D
    scale = DN_D ** -0.5

    grp = _group(nc_seq)

    def inner(gi, q_ref, k_ref, v_ref, sa_ref, al_ref, dt_ref, u_ref, w_ref, t_ref, du_ref, dw_ref, dqg_ref, dkd_ref,
              do_ref, vn_ref, dgl_ref, dq_ref, dk_ref, dv_ref, dsa_ref, dal_ref, ddt_ref):
        ci = (pl.program_id(0) * grp + gi) % nc_seq
        sa = sa_ref[...]
        beta_t, g_t, gam_t, valid, ea = _dn_gates(sa, al_ref[...], dt_ref[...], ci)
        yield
        lane = _iota2((CHUNK, LANE), 1)
        gates_t = jnp.where(lane < 4, beta_t, gam_t).T
        r, c = _masks64()
        incl, strict, upper, supper = r >= c, r > c, r <= c, r < c
        rows1 = _iota2((CHUNK, 1), 0)
        acc = [jnp.zeros((CHUNK, LANE), F32)]

        def head(h):
            sl = slice(h * DN_D, (h + 1) * DN_D)
            beta_w = jnp.broadcast_to(beta_t[:, h:h + 1], (CHUNK, DN_D))
            gam_w = jnp.broadcast_to(gam_t[:, 4 + h:5 + h], (CHUNK, DN_D))
            beta_s, gam_s = beta_w[:, 0:CHUNK], gam_w[:, 0:CHUNK]
            beta_row = gates_t[h:h + 1, :]
            gam_row = gates_t[4 + h:5 + h, :]
            gl = gam_t[CHUNK - 1:CHUNK, 4 + h:5 + h]
            dec = jnp.exp(jnp.where(incl, gam_s - gam_row, -jnp.inf))
            dec_t = jnp.exp(jnp.where(upper, gam_row - gam_s, -jnp.inf))
            egam_w = jnp.exp(gam_w)
            ekd_w = jnp.exp(gl - gam_w)
            kh = k_ref[:, sl].astype(F32)
            qh = q_ref[:, sl].astype(F32) * scale
            vh = v_ref[:, sl].astype(F32)
            uh = u_ref[:, sl]
            wh = w_ref[:, sl]
            doh = do_ref[:, sl]
            vnh = vn_ref[:, sl]
            kk = _nt(kh, kh)
            qk = _nt(qh, kh)
            qk_t = _nt(kh, qh)
            dp = _nt(doh, vnh)
            dp_t = _nt(vnh, doh)
            t_hi, t_lo = _split(t_ref[0, h].T)
            duh, dwh = du_ref[:, sl], dw_ref[:, sl]
            dvb = _nn(t_hi, duh) + _nn(t_lo, duh)
            dkg = _nn(t_hi, dwh) + _nn(t_lo, dwh)
            yield
            dvb_hi, dvb_lo = _split(dvb)
            dkg_hi, dkg_lo = _split(dkg)
            m = (_nt(dvb_hi, uh) + _nt(dvb_lo, uh)) + (_nt(dkg_hi, wh) + _nt(dkg_lo, wh))
            m_t = (_nt(uh, dvb_hi) + _nt(uh, dvb_lo)) + (_nt(wh, dkg_hi) + _nt(wh, dkg_lo))
            yield
            da = jnp.where(strict, -m, 0.0)
            da_t = jnp.where(supper, -m_t, 0.0)
            a = jnp.where(strict, beta_s * kk * dec, 0.0)
            a_t = jnp.where(supper, beta_row * kk * dec_t, 0.0)
            dad = da * dec
            dad_t = da_t * dec_t
            dpm = jnp.where(incl, dp, 0.0)
            dpm_t = jnp.where(upper, dp_t, 0.0)
            e = da * a + dpm * (qk * dec)
            e_t = da_t * a_t + dpm_t * (qk_t * dec_t)
            dqgh = dqg_ref[:, sl].astype(F32)
            dkdh = dkd_ref[:, sl].astype(F32)
            bg_w = beta_w * egam_w
            dkh = (_nn(beta_s * dad, kh) + _nn(beta_row * dad_t, kh) + _nn(dpm_t * dec_t, qh)
                   + bg_w * dkg + ekd_w * dkdh)
            dqh = _nn(dpm * dec, kh) + egam_w * dqgh
            t_kd = dkdh * (ekd_w * kh)
            dbeta = (jnp.sum(dad * kk, axis=1, keepdims=True)
                     + jnp.sum(dkg * (egam_w * kh) + dvb * vh, axis=1, keepdims=True))
            dgam = (jnp.sum(e - e_t, axis=1, keepdims=True)
                    + jnp.sum(dkg * (bg_w * kh) + dqgh * (egam_w * qh) - t_kd, axis=1, keepdims=True))
            dgam_last = (jnp.sum(jnp.sum(t_kd, axis=0, keepdims=True), axis=1, keepdims=True)
                         + dgl_ref[0, h:h + 1, 0:1] * jnp.exp(gl))
            dgam = dgam + jnp.where(rows1 == CHUNK - 1, dgam_last, 0.0)
            dq_ref[:, sl] = (dqh * scale).astype(dq_ref.dtype)
            dk_ref[:, sl] = dkh.astype(dk_ref.dtype)
            dv_ref[:, sl] = (beta_w * dvb).astype(dv_ref.dtype)
            acc[0] = acc[0] + jnp.where(lane == h, dbeta, 0.0) + jnp.where(lane == 4 + h, dgam, 0.0)

        yield from _round_robin([head(h) for h in range(NH)])
        acc_t = acc[0]
        dg_t = _tri_sum(upper, acc_t)
        ddb = acc_t * beta_t * (1.0 - beta_t)
        dda = jnp.where(valid, dg_t * (-ea) * _sigmoid(sa + dt_ref[...]), 0.0)
        dsa_ref[...] = jnp.where(lane < 4, ddb, jnp.where(lane < 8, dda, 0.0)).astype(dsa_ref.dtype)
        in_g = jnp.logical_and(lane >= 4, lane < 8)
        dal = jnp.sum(jnp.where(in_g, dg_t * g_t, 0.0), axis=0, keepdims=True)
        ddt = jnp.sum(jnp.where(in_g, dda, 0.0), axis=0, keepdims=True)
        _accumulate(dal_ref, dal, gi)
        _accumulate(ddt_ref, ddt, gi)

    rows = grp * CHUNK
    row = pl.BlockSpec((rows, hw), lambda i: (i, 0))
    vec = pl.BlockSpec((1, LANE), lambda i: (0, 0))
    mat = pl.BlockSpec((grp, NH, CHUNK, CHUNK), lambda i: (i, 0, 0, 0))
    glb = pl.BlockSpec((grp, NH, LANE), lambda i: (i, 0, 0))
    big = jax.ShapeDtypeStruct((n, hw), F32)
    v128 = jax.ShapeDtypeStruct((1, LANE), F32)
    kinds = (["row"] * 4 + ["whole"] * 2 + ["row"] * 2 + ["lead"] + ["row"] * 6 + ["lead"]
             + ["row"] * 4 + ["whole"] * 2)
    outs, ridden = _hosted_call(
        _per_chunk(inner, kinds, grp), rider, name=name, grid=(nct // grp,),
        in_specs=[row, row, row, pl.BlockSpec((rows, LANE), lambda i: (i, 0)), vec, vec,
                  row, row, mat, row, row, row, row, row, row, glb],
        out_specs=[row, row, row, pl.BlockSpec((rows, LANE), lambda i: (i, 0)), vec, vec],
        out_shape=[big, big, big, jax.ShapeDtypeStruct((n, LANE), BF16), v128, v128],
        scratch_shapes=[], compiler_params=_cp(VMEM_BIG, ("arbitrary",)),
        args=(qn, kn, v, projp, alog_row, dtb_row, u, w, tmat, du, dw, dqg, dkd, do, vn, dgl))
    return (*outs, ridden)


GQ_W = NH * GLA_DK
GV_W = NH * GLA_DV
GLA_NORM = 16.0
MID = CHUNK // 2


def _gla_gates(sb, w2p, gb, chunk_in_seq):
    rows = _iota2((CHUNK, GQ_W), 0)
    valid = jnp.logical_or(rows >= N_PAD, chunk_in_seq > 0)
    graw = _nn(sb, w2p) + gb
    yield
    g = jnp.where(valid, _logsigmoid(graw) * (1.0 / GLA_NORM), 0.0)
    r, c = _masks64()
    bcum = _tri_sum(r >= c, g)
    yield
    return graw, bcum, valid


def _head_mask(h):
    lane = _iota2((1, GQ_W), 1)
    return jnp.logical_and(lane >= h * GLA_DK, lane < (h + 1) * GLA_DK)


def _gla_intra_fwd(projp, gates, w2p, gb, *, nc_seq, name):
    n = projp.shape[0]
    nct = n // CHUNK
    scale = GLA_DK ** -0.5

    grp = _group(nc_seq)
    rows = grp * CHUNK

    def inner(gi, qk_ref, v_ref, sb_ref, w2_ref, gb_ref, oi_ref, qg_ref, kd_ref, gl_ref):
        ci = (pl.program_id(0) * grp + gi) % nc_seq
        _, bc, _ = yield from _gla_gates(sb_ref[...], w2_ref[...], gb_ref[...], ci)
        bref = bc[MID:MID + 1, :]
        bl = bc[CHUNK - 1:CHUNK, :]
        q = qk_ref[:, 0:GQ_W].astype(F32) * scale
        k = qk_ref[:, GQ_W:2 * GQ_W].astype(F32)
        qi = q * jnp.exp(bc - bref)
        ki = k * jnp.exp(bref - bc)
        qg_ref[...] = (q * jnp.exp(bc)).astype(qg_ref.dtype)
        kd_ref[...] = (k * jnp.exp(bl - bc)).astype(kd_ref.dtype)
        gl_ref[0] = jnp.exp(bl)
        r, c = _masks64()
        incl = r >= c
        a = [jnp.where(incl, _nt(jnp.where(_head_mask(h), qi, 0.0), ki), 0.0) for h in range(NH)]
        yield
        for h in range(NH):
            oi_ref[:, h * GLA_DV:(h + 1) * GLA_DV] = _nn(a[h], v_ref[:, h * GLA_DV:(h + 1) * GLA_DV]).astype(oi_ref.dtype)

    kinds = ["row"] * 3 + ["whole"] * 2 + ["row"] * 3 + ["lead"]
    return pl.pallas_call(
        _per_chunk(inner, kinds, grp), name=name, grid=(nct // grp,),
        in_specs=[pl.BlockSpec((rows, 2 * GQ_W), lambda i: (i, C_GQK // (2 * GQ_W))),
                  pl.BlockSpec((rows, GV_W), lambda i: (i, C_GV // GV_W)),
                  pl.BlockSpec((rows, LANE), lambda i: (i, 1)),
                  pl.BlockSpec((LANE, GQ_W), lambda i: (0, 0)), pl.BlockSpec((1, GQ_W), lambda i: (0, 0))],
        out_specs=[pl.BlockSpec((rows, GV_W), lambda i: (i, 0)), pl.BlockSpec((rows, GQ_W), lambda i: (i, 0)),
                   pl.BlockSpec((rows, GQ_W), lambda i: (i, 0)), pl.BlockSpec((grp, 1, GQ_W), lambda i: (i, 0, 0))],
        out_shape=[jax.ShapeDtypeStruct((n, GV_W), BF16), jax.ShapeDtypeStruct((n, GQ_W), BF16),
                   jax.ShapeDtypeStruct((n, GQ_W), BF16), jax.ShapeDtypeStruct((nct, 1, GQ_W), F32)],
        compiler_params=_cp(VMEM_BIG),
    )(projp, projp, gates, w2p, gb)


def _gla_scan_fwd(oi, qg, kd, gl, projp, *, bsz, nc_seq, name, rider=None):
    t_seq = nc_seq * CHUNK
    oi = oi.reshape(bsz, t_seq, GV_W)
    qg, kd = qg.reshape(bsz, t_seq, GQ_W), kd.reshape(bsz, t_seq, GQ_W)
    gl = gl.reshape(bsz, nc_seq, 1, GQ_W)
    pj = projp.reshape(bsz, t_seq, PW)
    grp = _group(nc_seq)

    def body(oi_ref, qg_ref, kd_ref, gl_ref, v_ref, o_ref, hist_ref, st_ref):
        @pl.when(pl.program_id(0) == 0)
        def _():
            st_ref[...] = jnp.zeros_like(st_ref)

        for gi in range(grp):
            rows = pl.ds(gi * CHUNK, CHUNK)
            for b in range(bsz):
                st = st_ref[b]
                hist_ref[b, gi] = st.astype(hist_ref.dtype)
                qgb = qg_ref[b, rows, :]
                kdb = kd_ref[b, rows, :]
                upd = jnp.zeros((GLA_DV, GQ_W), F32)
                for h in range(NH):
                    sl = slice(h * GLA_DV, (h + 1) * GLA_DV)
                    m = _head_mask(h)
                    o_ref[b, rows, sl] = (oi_ref[b, rows, sl] + _nt(jnp.where(m, qgb, 0.0), st)).astype(o_ref.dtype)
                    upd = upd + jnp.where(m, _tn(v_ref[b, rows, sl], kdb), 0.0)
                st_ref[b] = gl_ref[b, gi] * st + upd

    rws = grp * CHUNK
    outs, ridden = _hosted_call(
        body, rider, name=name, grid=(nc_seq // grp,),
        in_specs=[pl.BlockSpec((bsz, rws, GV_W), lambda i: (0, i, 0)),
                  pl.BlockSpec((bsz, rws, GQ_W), lambda i: (0, i, 0)),
                  pl.BlockSpec((bsz, rws, GQ_W), lambda i: (0, i, 0)),
                  pl.BlockSpec((bsz, grp, 1, GQ_W), lambda i: (0, i, 0, 0)),
                  pl.BlockSpec((bsz, rws, GV_W), lambda i: (0, i, C_GV // GV_W))],
        out_specs=[pl.BlockSpec((bsz, rws, GV_W), lambda i: (0, i, 0)),
                   pl.BlockSpec((bsz, grp, GLA_DV, GQ_W), lambda i: (0, i, 0, 0))],
        out_shape=[jax.ShapeDtypeStruct((bsz, t_seq, GV_W), BF16),
                   jax.ShapeDtypeStruct((bsz, nc_seq, GLA_DV, GQ_W), F32)],
        scratch_shapes=[pltpu.VMEM((bsz, GLA_DV, GQ_W), F32)],
        compiler_params=_cp(VMEM_BIG, ("arbitrary",)), args=(oi, qg, kd, gl, pj))
    return outs[0].reshape(bsz * t_seq, GV_W), outs[1], ridden


def _gla_scan_bwd(do, qg, kd, gl, projp, hist, *, bsz, nc_seq, name):
    t_seq = nc_seq * CHUNK
    do = do.reshape(bsz, t_seq, GV_W)
    qg, kd = qg.reshape(bsz, t_seq, GQ_W), kd.reshape(bsz, t_seq, GQ_W)
    gl = gl.reshape(bsz, nc_seq, 1, GQ_W)
    pj = projp.reshape(bsz, t_seq, PW)
    grp = _group(nc_seq)

    def body(do_ref, qg_ref, kd_ref, gl_ref, v_ref, hist_ref, dqg_ref, dkd_ref, dv_ref, dgl_ref, dst_ref):
        @pl.when(pl.program_id(0) == 0)
        def _():
            dst_ref[...] = jnp.zeros_like(dst_ref)

        for gi in reversed(range(grp)):
            rows = pl.ds(gi * CHUNK, CHUNK)
            for b in range(bsz):
                st = hist_ref[b, gi]
                dst = dst_ref[b]
                qgb = qg_ref[b, rows, :]
                kdb = kd_ref[b, rows, :]
                dqg = jnp.zeros((CHUNK, GQ_W), F32)
                dkd = jnp.zeros((CHUNK, GQ_W), F32)
                add = jnp.zeros((GLA_DV, GQ_W), F32)
                for h in range(NH):
                    sl = slice(h * GLA_DV, (h + 1) * GLA_DV)
                    m = _head_mask(h)
                    doh = do_ref[b, rows, sl]
                    vh = v_ref[b, rows, sl]
                    dqg = dqg + jnp.where(m, _nn(doh, st), 0.0)
                    dkd = dkd + jnp.where(m, _nn(vh, dst), 0.0)
                    dv_ref[b, rows, sl] = _nt(jnp.where(m, kdb, 0.0), dst).astype(dv_ref.dtype)
                    add = add + jnp.where(m, _tn(doh, qgb), 0.0)
                dqg_ref[b, rows, :] = dqg.astype(dqg_ref.dtype)
                dkd_ref[b, rows, :] = dkd.astype(dkd_ref.dtype)
                dgl_ref[b, gi] = jnp.sum(dst * st, axis=0, keepdims=True)
                dst_ref[b] = gl_ref[b, gi] * dst + add

    steps = nc_seq // grp
    rws = grp * CHUNK
    rev = lambda i: steps - 1 - i
    outs = pl.pallas_call(
        body, name=name, grid=(steps,),
        in_specs=[pl.BlockSpec((bsz, rws, GV_W), lambda i: (0, rev(i), 0)),
                  pl.BlockSpec((bsz, rws, GQ_W), lambda i: (0, rev(i), 0)),
                  pl.BlockSpec((bsz, rws, GQ_W), lambda i: (0, rev(i), 0)),
                  pl.BlockSpec((bsz, grp, 1, GQ_W), lambda i: (0, rev(i), 0, 0)),
                  pl.BlockSpec((bsz, rws, GV_W), lambda i: (0, rev(i), C_GV // GV_W)),
                  pl.BlockSpec((bsz, grp, GLA_DV, GQ_W), lambda i: (0, rev(i), 0, 0))],
        out_specs=[pl.BlockSpec((bsz, rws, GQ_W), lambda i: (0, rev(i), 0)),
                   pl.BlockSpec((bsz, rws, GQ_W), lambda i: (0, rev(i), 0)),
                   pl.BlockSpec((bsz, rws, GV_W), lambda i: (0, rev(i), 0)),
                   pl.BlockSpec((bsz, grp, 1, GQ_W), lambda i: (0, rev(i), 0, 0))],
        out_shape=[jax.ShapeDtypeStruct((bsz, t_seq, GQ_W), F32), jax.ShapeDtypeStruct((bsz, t_seq, GQ_W), F32),
                   jax.ShapeDtypeStruct((bsz, t_seq, GV_W), BF16), jax.ShapeDtypeStruct((bsz, nc_seq, 1, GQ_W), F32)],
        scratch_shapes=[pltpu.VMEM((bsz, GLA_DV, GQ_W), F32)],
        compiler_params=_cp(VMEM_BIG, ("arbitrary",)),
    )(do, qg, kd, gl, pj, hist)
    n = bsz * t_seq
    return (outs[0].reshape(n, GQ_W), outs[1].reshape(n, GQ_W), outs[2].reshape(n, GV_W),
            outs[3].reshape(bsz * nc_seq, 1, GQ_W))


def _gla_intra_bwd(projp, gates, w2p, gb, do, dqg, dkd, dvi, dgl, *, nc_seq, name):
    n = projp.shape[0]
    nct = n // CHUNK
    scale = GLA_DK ** -0.5

    grp = _group(nc_seq)
    rows = grp * CHUNK

    def inner(gi, qk_ref, v_ref, sb_ref, w2_ref, gb_ref, do_ref, dqg_ref, dkd_ref, dvi_ref, dgl_ref,
              dqk_ref, dv_ref, dsb_ref, dw2_ref, dgb_ref):
        ci = (pl.program_id(0) * grp + gi) % nc_seq
        sb = sb_ref[...]
        w2 = w2_ref[...]
        graw, bc, valid = yield from _gla_gates(sb, w2, gb_ref[...], ci)
        bref = bc[MID:MID + 1, :]
        bl = bc[CHUNK - 1:CHUNK, :]
        q = qk_ref[:, 0:GQ_W].astype(F32) * scale
        k = qk_ref[:, GQ_W:2 * GQ_W].astype(F32)
        ex1 = jnp.exp(bc - bref)
        ex2 = jnp.exp(bref - bc)
        eb = jnp.exp(bc)
        ekd = jnp.exp(bl - bc)
        qi, ki = q * ex1, k * ex2
        r, c = _masks64()
        incl = r >= c
        upper = r <= c
        a_t, da, da_t = [], [], []
        for h in range(NH):
            sl = slice(h * GLA_DV, (h + 1) * GLA_DV)
            doh = do_ref[:, sl]
            vh = v_ref[:, sl]
            a_t.append(jnp.where(upper, _nt(jnp.where(_head_mask(h), ki, 0.0), qi), 0.0))
            da.append(jnp.where(incl, _nt(doh, vh), 0.0))
            da_t.append(jnp.where(upper, _nt(vh, doh), 0.0))
        yield
        dqi = jnp.zeros((CHUNK, GQ_W), F32)
        dki = jnp.zeros((CHUNK, GQ_W), F32)
        for h in range(NH):
            sl = slice(h * GLA_DV, (h + 1) * GLA_DV)
            m = _head_mask(h)
            dv_ref[:, sl] = (_nn(a_t[h], do_ref[:, sl]) + dvi_ref[:, sl]).astype(dv_ref.dtype)
            dqi = dqi + jnp.where(m, _nn(da[h], ki), 0.0)
            dki = dki + jnp.where(m, _nn(da_t[h], qi), 0.0)
        yield
        dqg = dqg_ref[...].astype(F32)
        dkd = dkd_ref[...].astype(F32)
        dqk_ref[:, 0:GQ_W] = ((dqi * ex1 + dqg * eb) * scale).astype(dqk_ref.dtype)
        dqk_ref[:, GQ_W:2 * GQ_W] = (dki * ex2 + dkd * ekd).astype(dqk_ref.dtype)
        t_qi, t_ki, t_kd = dqi * qi, dki * ki, dkd * (k * ekd)
        db = t_qi - t_ki + dqg * (q * eb) - t_kd
        dbref = jnp.sum(t_ki - t_qi, axis=0, keepdims=True)
        dbl = jnp.sum(t_kd, axis=0, keepdims=True) + dgl_ref[0] * jnp.exp(bl)
        rows = _iota2((CHUNK, GQ_W), 0)
        db = db + jnp.where(rows == MID, dbref, 0.0) + jnp.where(rows == CHUNK - 1, dbl, 0.0)
        dg = _tri_sum(upper, db)
        yield
        dgraw = jnp.where(valid, dg * (1.0 / GLA_NORM) * _sigmoid(-graw), 0.0)
        dsb_ref[...] = _nt(dgraw, w2).astype(dsb_ref.dtype)
        dw2 = _tn(sb, dgraw)
        dgb = jnp.sum(dgraw, axis=0, keepdims=True)
        _accumulate(dw2_ref, dw2, gi)
        _accumulate(dgb_ref, dgb, gi)

    rq = pl.BlockSpec((rows, GQ_W), lambda i: (i, 0))
    rv = pl.BlockSpec((rows, GV_W), lambda i: (i, 0))
    kinds = ["row"] * 3 + ["whole"] * 2 + ["row"] * 4 + ["lead"] + ["row"] * 3 + ["whole"] * 2
    return pl.pallas_call(
        _per_chunk(inner, kinds, grp), name=name, grid=(nct // grp,),
        in_specs=[pl.BlockSpec((rows, 2 * GQ_W), lambda i: (i, C_GQK // (2 * GQ_W))),
                  pl.BlockSpec((rows, GV_W), lambda i: (i, C_GV // GV_W)),
                  pl.BlockSpec((rows, LANE), lambda i: (i, 1)),
                  pl.BlockSpec((LANE, GQ_W), lambda i: (0, 0)), pl.BlockSpec((1, GQ_W), lambda i: (0, 0)),
                  rv, rq, rq, rv, pl.BlockSpec((grp, 1, GQ_W), lambda i: (i, 0, 0))],
        out_specs=[pl.BlockSpec((rows, 2 * GQ_W), lambda i: (i, 0)), rv, pl.BlockSpec((rows, LANE), lambda i: (i, 0)),
                   pl.BlockSpec((LANE, GQ_W), lambda i: (0, 0)), pl.BlockSpec((1, GQ_W), lambda i: (0, 0))],
        out_shape=[jax.ShapeDtypeStruct((n, 2 * GQ_W), BF16), jax.ShapeDtypeStruct((n, GV_W), BF16),
                   jax.ShapeDtypeStruct((n, LANE), BF16), jax.ShapeDtypeStruct((LANE, GQ_W), F32),
                   jax.ShapeDtypeStruct((1, GQ_W), F32)],
        compiler_params=_cp(VMEM_BIG, ("arbitrary",)),
    )(projp, projp, gates, w2p, gb, do, dqg, dkd, dvi, dgl)


SECTIONS = ((C_QKV, 1536), (C_DZ, 512), (C_GQK, 512), (C_GV, 512), (C_GR, 512), (C_SA, 128), (C_SB, 128))


def _inproj_bwd(secs, wp, h0, g1, dx1, *, bsz, t_seq, tr, name, rider=None):
    n, d = h0.shape
    per_seq = t_seq // tr
    steps = n // tr
    s_len = t_seq - CHUNK

    def body(*refs):
        sec_refs = refs[:len(SECTIONS)]
        wp_ref, h0_ref, g_ref, dx1_ref, gx_hbm, meta_ref, dg_ref, obuf, sems = refs[len(SECTIONS):]
        i = pl.program_id(0)
        slot = i % 2

        def put(step, slot_, act):
            b, j = step // per_seq, step % per_seq

            @pl.when(j == 0)
            def _():
                act(pltpu.make_async_copy(obuf.at[slot_, pl.ds(CHUNK, tr - CHUNK), :],
                                          gx_hbm.at[b, pl.ds(0, tr - CHUNK), :], sems.at[slot_]))

            if per_seq > 1:
                @pl.when(j > 0)
                def _():
                    act(pltpu.make_async_copy(obuf.at[slot_], gx_hbm.at[b, pl.ds(j * tr - CHUNK, tr), :],
                                              sems.at[slot_]))

        dh = None
        for s_ref, (off, wd) in zip(sec_refs, SECTIONS):
            part = _nt(s_ref[...], wp_ref[:, off:off + wd])
            dh = part if dh is None else dh + part
        dx, dg = _rms_bwd_math(h0_ref[...], g_ref[...], dh)
        tot = dx1_ref[...] + dx

        @pl.when(i >= 2)
        def _():
            put(i - 2, slot, lambda cp: cp.wait())

        obuf[slot] = tot
        put(i, slot, lambda cp: cp.start())

        @pl.when(i % per_seq == 0)
        def _():
            meta_ref[...] = tot[N_PAD:CHUNK, :]

        @pl.when(i == steps - 1)
        def _():
            if steps > 1:
                put(i - 1, 1 - slot, lambda cp: cp.wait())
            put(i, slot, lambda cp: cp.wait())

        @pl.when(i == 0)
        def _():
            dg_ref[...] = dg

        @pl.when(i > 0)
        def _():
            dg_ref[...] += dg

    row = pl.BlockSpec((tr, d), lambda i: (i, 0))
    vec = pl.BlockSpec((1, d), lambda i: (0, 0))
    outs, ridden = _hosted_call(
        body, rider, name=name, grid=(steps,),
        in_specs=[pl.BlockSpec((tr, wd), lambda i: (i, 0)) for _, wd in SECTIONS]
        + [pl.BlockSpec((d, PW), lambda i: (0, 0)), row, vec, row],
        out_specs=[ANY, pl.BlockSpec((N_META, d), lambda i: (i // per_seq, 0)), vec],
        out_shape=[jax.ShapeDtypeStruct((bsz, s_len, d), F32), jax.ShapeDtypeStruct((bsz * N_META, d), F32),
                   jax.ShapeDtypeStruct((1, d), F32)],
        scratch_shapes=[pltpu.VMEM((2, tr, d), F32), pltpu.SemaphoreType.DMA((2,))],
        compiler_params=_cp(VMEM_BIG, ("arbitrary",)), args=(*secs, wp, h0, g1, dx1))
    return (*outs, ridden)


def _adamw(w, g, m, v, *, name, emit_grad=False, col_tile=None):
    lead = w.ndim - 2
    r, c = w.shape[-2:]
    tr = r if col_tile is not None else (_tile(r, 256, 8) if r > 256 else r)
    tc = col_tile if col_tile is not None else c
    c1 = 1.0 - ADAM_B1 ** ADAM_STEP
    c2 = 1.0 - ADAM_B2 ** ADAM_STEP
    n_out = 4 if emit_grad else 3

    def body(w_ref, g_ref, m_ref, v_ref, *out_refs):
        rd = (lambda ref: ref[0]) if lead else (lambda ref: ref[...])
        gv = g_ref[:, 0:tc]
        nm = ADAM_B1 * rd(m_ref) + (1.0 - ADAM_B1) * gv
        nv = ADAM_B2 * rd(v_ref) + (1.0 - ADAM_B2) * (gv * gv)
        res = [-ADAM_LR * ((nm / c1) / (jnp.sqrt(nv / c2) + ADAM_EPS) + ADAM_WD * rd(w_ref)), nm, nv, gv]
        for o_ref, val in zip(out_refs, res):
            if lead:
                o_ref[0] = val
            else:
                o_ref[...] = val

    if col_tile is None:
        blk = pl.BlockSpec((1,) * lead + (tr, c), lambda i: (0,) * lead + (i, 0))
        gblk = pl.BlockSpec((tr, g.shape[1]), lambda i: (i, 0))
        steps = r // tr
    else:
        blk = pl.BlockSpec((1,) * lead + (r, tc), lambda j: (0,) * lead + (0, j))
        gblk = pl.BlockSpec((r, tc), lambda j: (0, j))
        steps = c // tc
    sds = jax.ShapeDtypeStruct(w.shape, F32)
    return pl.pallas_call(
        body, name=name, grid=(steps,), in_specs=[blk, gblk, blk, blk], out_specs=[blk] * n_out,
        out_shape=[sds] * n_out, compiler_params=_cp(VMEM_BIG),
    )(w, g, m, v)


def _adamw_rows(w, g, m, v, *, name):
    r, _, c = w.shape
    tr = max(t for t in range(1, 129) if r % t == 0)
    c1 = 1.0 - ADAM_B1 ** ADAM_STEP
    c2 = 1.0 - ADAM_B2 ** ADAM_STEP

    def body(w_ref, g_ref, m_ref, v_ref, d_ref, nm_ref, nv_ref, go_ref):
        gv = g_ref[...]
        nm = ADAM_B1 * m_ref[...] + (1.0 - ADAM_B1) * gv
        nv = ADAM_B2 * v_ref[...] + (1.0 - ADAM_B2) * (gv * gv)
        d_ref[...] = -ADAM_LR * ((nm / c1) / (jnp.sqrt(nv / c2) + ADAM_EPS) + ADAM_WD * w_ref[...])
        nm_ref[...] = nm
        nv_ref[...] = nv
        go_ref[...] = gv

    blk = pl.BlockSpec((tr, 1, c), lambda i: (i, 0, 0))
    sds = jax.ShapeDtypeStruct(w.shape, F32)
    return pl.pallas_call(
        body, name=name, grid=(r // tr,), in_specs=[blk] * 4, out_specs=[blk] * 4, out_shape=[sds] * 4,
        compiler_params=_cp(VMEM_BIG),
    )(w, g, m, v)


def _pair_sum(where, g, theirs, *, name):
    lead, r, cols = g.shape
    half = r // 2
    tr = _tile(half, 256, 16)
    nh = half // tr

    def body(w_ref, a_ref, b_ref, o_ref):
        o_ref[...] = (a_ref[...] + b_ref[...]).astype(o_ref.dtype)

    blk = pl.BlockSpec((1, tr, cols), lambda s, i, w: (s, i, 0))
    return pl.pallas_call(
        body, name=name,
        grid_spec=pltpu.PrefetchScalarGridSpec(
            num_scalar_prefetch=1, grid=(lead, nh),
            in_specs=[pl.BlockSpec((1, tr, cols), lambda s, i, w: (s, w[0] * nh + i, 0)), blk], out_specs=blk),
        out_shape=jax.ShapeDtypeStruct((lead, half, cols), BF16), compiler_params=_cp(VMEM_BIG),
    )(where, g, theirs)


def _chip_sum(where, pair, q, *, name):
    _, half, cols = pair.shape
    tr = _tile(half, 256, 16)
    nh = half // tr

    def body(w_ref, own_ref, q1_ref, q2_ref, q3_ref, o_ref):
        f = lambda ref: ref[0].astype(F32)
        o_ref[...] = ((f(own_ref) + f(q1_ref)) + f(q2_ref)) + f(q3_ref)

    def peer(d):
        return pl.BlockSpec((1, tr, cols), lambda i, w: ((w[1] + d) % N_CHIPS, i, 0))

    return pl.pallas_call(
        body, name=name,
        grid_spec=pltpu.PrefetchScalarGridSpec(
            num_scalar_prefetch=1, grid=(nh,),
            in_specs=[peer(0), peer(1), peer(2), peer(3)],
            out_specs=pl.BlockSpec((tr, cols), lambda i, w: (w[0] * nh + i, 0))),
        out_shape=jax.ShapeDtypeStruct((2 * half, cols), F32), compiler_params=_cp(VMEM_BIG),
    )(where, pair, q, q, q)


VM = pl.BlockSpec(memory_space=pltpu.VMEM)


def _row_chunks(rows, n_split):
    size = rows // n_split
    assert size * n_split == rows and size % 16 == 0, (rows, n_split)
    return [(s, pl.ds(s * size, size)) for s in range(n_split)], size


D2D_SPLIT = 4
ICI_SPLIT = 2


def _sibling_halves(grads):
    n_arr = len(grads)

    def body(*refs):
        ins = refs[:n_arr]
        theirs = refs[n_arr:2 * n_arr]
        send_sems, recv_sems = refs[2 * n_arr:]
        x, y, c = _place()
        copies = []
        for k in range(n_arr):
            half = ins[k].shape[1] // 2
            chunks, size = _row_chunks(half, D2D_SPLIT)
            for s, dst_rows in chunks:
                give = pltpu.make_async_remote_copy(
                    src_ref=ins[k].at[:, pl.ds((1 - c) * half + s * size, size), :], dst_ref=theirs[k].at[:, dst_rows, :],
                    send_sem=send_sems.at[k, s], recv_sem=recv_sems.at[k, s], device_id=(x, y, 1 - c),
                    device_id_type=MESH)
                give.start()
                copies.append(give)
        for give in copies:
            give.wait()

    halves = [jax.ShapeDtypeStruct((g.shape[0], g.shape[1] // 2, g.shape[2]), F32) for g in grads]
    sem = pltpu.SemaphoreType.DMA((n_arr, D2D_SPLIT))
    return pl.pallas_call(
        body, name="sibling_halves", in_specs=[ANY] * n_arr, out_specs=[ANY] * n_arr, out_shape=halves,
        scratch_shapes=[sem, sem],
    )(*grads)


def _chip_exchange(parts):
    n_arr = len(parts)

    def body(*refs):
        ins = refs[:n_arr]
        outs = refs[n_arr:2 * n_arr]
        send_sems, recv_sems = refs[2 * n_arr:]
        x, y, c = _place()
        me = 2 * x + y
        sends = []
        for k in range(n_arr):
            chunks, _ = _row_chunks(ins[k].shape[1], ICI_SPLIT)
            for d, (px, py, pj) in enumerate(_other_chips(x, y)):
                for s, rows in chunks:
                    cp = pltpu.make_async_remote_copy(
                        src_ref=ins[k].at[pj, rows, :], dst_ref=outs[k].at[me, rows, :], send_sem=send_sems.at[k, d, s],
                        recv_sem=recv_sems.at[k, d, s], device_id=(px, py, c), device_id_type=MESH)
                    cp.start()
                    sends.append(cp)
        for k in range(n_arr):
            chunks, _ = _row_chunks(ins[k].shape[1], ICI_SPLIT)
            for d, (px, py, pj) in enumerate(_other_chips(x, y)):
                for s, rows in chunks:
                    pltpu.make_async_remote_copy(
                        src_ref=ins[k].at[pj, rows, :], dst_ref=outs[k].at[pj, rows, :], send_sem=send_sems.at[k, d, s],
                        recv_sem=recv_sems.at[k, d, s], device_id=(px, py, c), device_id_type=MESH).wait_recv()
        for cp in sends:
            cp.wait_send()

    sem = pltpu.SemaphoreType.DMA((n_arr, 3, ICI_SPLIT))
    return pl.pallas_call(
        body, name="chip_exchange", in_specs=[ANY] * n_arr, out_specs=[ANY] * n_arr,
        out_shape=[jax.ShapeDtypeStruct(p.shape, p.dtype) for p in parts],
        scratch_shapes=[sem, sem],
    )(*parts)


class _SiblingHalvesRider:
    def __init__(self, grads):
        self.inputs = list(grads)
        self.out_shapes = [jax.ShapeDtypeStruct((g.shape[0], g.shape[1] // 2, g.shape[2]), F32) for g in grads]
        self.aliases = {}
        self.sems = [pltpu.SemaphoreType.DMA((len(grads), D2D_SPLIT))] * 2

    def _copies(self, ins, outs, sems):
        x, y, c = _place()
        for k in range(len(ins)):
            half = ins[k].shape[1] // 2
            chunks, size = _row_chunks(half, D2D_SPLIT)
            for s, dst_rows in chunks:
                yield pltpu.make_async_remote_copy(
                    src_ref=ins[k].at[:, pl.ds((1 - c) * half + s * size, size), :], dst_ref=outs[k].at[:, dst_rows, :],
                    send_sem=sems[0].at[k, s], recv_sem=sems[1].at[k, s], device_id=(x, y, 1 - c), device_id_type=MESH)

    def first(self, ins, outs, sems):
        for cp in self._copies(ins, outs, sems):
            cp.start()

    def last(self, ins, outs, sems):
        for cp in self._copies(ins, outs, sems):
            cp.wait()


class _ChipExchangeRider:
    def __init__(self, parts):
        self.inputs = list(parts)
        self.out_shapes = [jax.ShapeDtypeStruct(p.shape, p.dtype) for p in parts]
        self.aliases = {}
        self.sems = [pltpu.SemaphoreType.DMA((len(parts), 3, ICI_SPLIT))] * 2

    def _copies(self, ins, outs, sems, receiving):
        x, y, c = _place()
        for k in range(len(ins)):
            chunks, _ = _row_chunks(ins[k].shape[1], ICI_SPLIT)
            for d, (px, py, pj) in enumerate(_other_chips(x, y)):
                for s, rows in chunks:
                    yield pltpu.make_async_remote_copy(
                        src_ref=ins[k].at[pj, rows, :], dst_ref=outs[k].at[pj if receiving else 2 * x + y, rows, :],
                        send_sem=sems[0].at[k, d, s], recv_sem=sems[1].at[k, d, s], device_id=(px, py, c),
                        device_id_type=MESH)

    def first(self, ins, outs, sems):
        for cp in self._copies(ins, outs, sems, False):
            cp.start()

    def last(self, ins, outs, sems):
        for cp in self._copies(ins, outs, sems, True):
            cp.wait_recv()
        for cp in self._copies(ins, outs, sems, False):
            cp.wait_send()


def _sibling_join(bufs):
    n_arr = len(bufs)

    def body(*refs):
        bufs_out = refs[n_arr:2 * n_arr]
        send_sems, recv_sems = refs[2 * n_arr:]
        x, y, c = _place()
        copies = []
        for k in range(n_arr):
            half = bufs_out[k].shape[0] // 2
            chunks, size = _row_chunks(half, D2D_SPLIT)
            for s, _ in chunks:
                rows = pl.ds(c * half + s * size, size)
                give = pltpu.make_async_remote_copy(
                    src_ref=bufs_out[k].at[rows, :], dst_ref=bufs_out[k].at[rows, :], send_sem=send_sems.at[k, s],
                    recv_sem=recv_sems.at[k, s], device_id=(x, y, 1 - c), device_id_type=MESH)
                give.start()
                copies.append((k, s, half, size, give))
        for k, s, half, size, give in copies:
            rows = pl.ds((1 - c) * half + s * size, size)
            pltpu.make_async_remote_copy(
                src_ref=bufs_out[k].at[rows, :], dst_ref=bufs_out[k].at[rows, :], send_sem=send_sems.at[k, s],
                recv_sem=recv_sems.at[k, s], device_id=(x, y, 1 - c), device_id_type=MESH).wait_recv()
            give.wait_send()

    sem = pltpu.SemaphoreType.DMA((n_arr, D2D_SPLIT))
    return pl.pallas_call(
        body, name="sibling_join", in_specs=[ANY] * n_arr, out_specs=[ANY] * n_arr,
        out_shape=[jax.ShapeDtypeStruct(b.shape, F32) for b in bufs],
        input_output_aliases={k: k for k in range(n_arr)},
        scratch_shapes=[sem, sem],
    )(*bufs)


PACK_ROWS = 48


def _small_allreduce(pack):
    masks = [(dx, dy, dc) for dx in (0, 1) for dy in (0, 1) for dc in (0, 1)][1:]

    def body(p_ref, o_ref, buf, send_sems, recv_sems):
        x, y, c = _place()
        me = 4 * x + 2 * y + c
        buf[me] = p_ref[...]
        sends = []
        for k, (dx, dy, dc) in enumerate(masks):
            peer = (1 - x if dx else x, 1 - y if dy else y, 1 - c if dc else c)
            cp = pltpu.make_async_remote_copy(
                src_ref=p_ref, dst_ref=buf.at[me], send_sem=send_sems.at[k], recv_sem=recv_sems.at[k],
                device_id=peer, device_id_type=MESH)
            cp.start()
            sends.append(cp)
        for k, (dx, dy, dc) in enumerate(masks):
            peer = (1 - x if dx else x, 1 - y if dy else y, 1 - c if dc else c)
            pj = 4 * peer[0] + 2 * peer[1] + peer[2]
            pltpu.make_async_remote_copy(
                src_ref=p_ref, dst_ref=buf.at[pj], send_sem=send_sems.at[k], recv_sem=recv_sems.at[k],
                device_id=peer, device_id_type=MESH).wait_recv()
        for cp in sends:
            cp.wait_send()
        tot = buf[0]
        for k in range(1, 8):
            tot = tot + buf[k]
        o_ref[...] = tot
        o_ref[0:N_META, :] = tot[0:N_META] + tot[N_META:2 * N_META]

    return pl.pallas_call(
        body, name="small_allreduce", in_specs=[VM], out_specs=VM,
        out_shape=jax.ShapeDtypeStruct((PACK_ROWS, D_MODEL), F32),
        scratch_shapes=[pltpu.VMEM((8, PACK_ROWS, D_MODEL), F32), pltpu.SemaphoreType.DMA((7,)),
                        pltpu.SemaphoreType.DMA((7,))],
    )(pack)


def _pad_lanes(vec, offset):
    k = vec.shape[1]
    return jnp.concatenate([jnp.zeros((1, offset), F32), vec, jnp.zeros((1, LANE - offset - k), F32)], axis=1)


def _local_step(x, tgt, meta, norm1_g, wp, conv_w, a_log, dt_bias, dn_norm_g, gla_w2, gla_b, gla_norm_g,
                w_out, norm2_g, w_up, w_down, final_norm_g, late_gather=None, where=None, early_gather=None):
    bsz, s_len, d = x.shape
    t_seq = s_len + CHUNK
    nc_seq = t_seq // CHUNK
    n = bsz * t_seq
    tr = _tile(t_seq, 832)
    tt = _tile(t_seq, 416)

    lead = jnp.concatenate([jnp.zeros((N_PAD, d), F32), meta], axis=0)
    alog_row = _pad_lanes(a_log, 4)
    dtb_row = _pad_lanes(dt_bias, 4)

    h0, h, got_early = _embed_norm(x, lead, norm1_g, tr=tr, name="embed_norm1", rider=early_gather)
    if early_gather is not None:
        wp = _padded_from_shards(got_early[0])
        conv_w = got_early[1].transpose(1, 0, 2).reshape(4, QKV_W)
        gla_w2 = got_early[2][:, :, 0:GQ_W // N_CHIPS].transpose(1, 0, 2).reshape(GLA_RANK, GQ_W)
    w2p = jnp.concatenate([gla_w2, jnp.zeros((LANE - GLA_RANK, GQ_W), F32)], axis=0)
    ride_up, ride_down, ride_out = late_gather if late_gather is not None else (None, None, None)
    res = _mm(h, wp, "nn", tm=tt, tn=PW, tk=d, out_dtypes=(BF16, F32), out_widths=(PW, PW - C_SA),
              epilogue=lambda acc: (acc, acc[:, C_SA:PW]), name="in_proj", rider=ride_up)
    ((projp, gates), got_up) = res if ride_up is not None else (res, None)
    qn, kn, v = _dnprep_fwd(projp, conv_w, bsz=bsz, t_seq=t_seq, tt=tt, name="dn_prep")
    u, w, qg, kd, pmat, tmat, gl, got_down = _dn_intra_fwd(qn, kn, v, gates, alog_row, dtb_row, nc_seq=nc_seq,
                                                           name="dn_intra", rider=ride_down)
    o_dn, vn, hist, got_out = _dn_scan_fwd(u, w, qg, kd, pmat, gl, bsz=bsz, nc_seq=nc_seq, name="dn_scan",
                                           rider=ride_out)
    oi, gqg, gkd, ggl = _gla_intra_fwd(projp, gates, w2p, gla_b, nc_seq=nc_seq, name="gla_intra")
    o_gla, ghist, _ = _gla_scan_fwd(oi, gqg, gkd, ggl, projp, bsz=bsz, nc_seq=nc_seq, name="gla_scan")
    if late_gather is not None:
        w_out = got_out[0].reshape(d, d)
        w_up = got_up[0].transpose(1, 0, 2).reshape(d, D_FF)
        w_down = got_down[0].reshape(D_FF, d)
    mix = _gnorm_fwd(o_dn, o_gla, projp, dn_norm_g, gla_norm_g, tr=tr, name="gated_norm")
    (x1,) = _mm(mix, w_out, "nn", tm=tr, tn=d, tk=d, out_dtypes=(F32,), extras=(h0,),
                epilogue=lambda acc, res: (res + acc,), name="out_proj")
    h2 = _rms_fwd(x1, norm2_g, tr=tr, name="norm2")

    (act,) = _mm(h2, w_up, "nn", tm=tt, tn=D_FF, tk=d, out_dtypes=(BF16,),
                 epilogue=lambda acc: (jnp.square(jnp.maximum(acc, 0.0)),), name="mlp_up", n_chunk=1024)
    dx2, dx2b, d_final_g, loss_tile = _mlp_down_loss(act, w_down, x1, final_norm_g, tgt, t_seq=t_seq, tr=tt,
                                                     name="mlp_down_loss")

    (dup,) = _mm(dx2b, w_down, "nt", tm=tt, tn=D_FF, tk=d, out_dtypes=(BF16,), extras=(act,),
                 epilogue=lambda acc, a: (acc * (2.0 * jnp.sqrt(a.astype(F32))),), name="mlp_down_bwd", n_chunk=1024)
    tk2 = 2 * tr if n % (2 * tr) == 0 else tr
    (d_w_down,) = _mm(act, dx2b, "tn", tm=D_FF // 2, tn=d, tk=tk2, out_dtypes=(F32,), name="w_down_grad")
    (d_w_up,) = _mm(h2, dup, "tn", tm=d, tn=D_FF // 2, tk=tk2, out_dtypes=(F32,), name="w_up_grad")
    mlp_sm = [d_w_up.reshape(d, N_CHIPS, D_FF // N_CHIPS).transpose(1, 0, 2), d_w_down.reshape(N_CHIPS, D_FF // N_CHIPS, d)]
    ride1 = _SiblingHalvesRider(mlp_sm) if where is not None else None
    dx1, dx1b, d_norm2_g, theirs = _mlp_up_bwd_norm(dup, w_up, x1, norm2_g, dx2, tr=tt, name="mlp_up_bwd_norm",
                                                    rider=ride1)
    ride2 = None
    if where is not None:
        mlp_pair = [_pair_sum(where, a, b, name=f"pair_sum_mlp{k}") for k, (a, b) in enumerate(zip(mlp_sm, theirs))]
        ride2 = _ChipExchangeRider(mlp_pair)

    (dmix,) = _mm(dx1b, w_out, "nt", tm=tr, tn=d, tk=d, out_dtypes=(BF16,), name="out_proj_bwd")
    (d_w_out,) = _mm(mix, dx1b, "tn", tm=d, tn=d, tk=tr, out_dtypes=(F32,), name="w_out_grad")
    do_dn, ddz, do_gla, dgr, d_dn_norm_g, d_gla_norm_g = _gnorm_bwd(
        dmix, o_dn, o_gla, projp, dn_norm_g, gla_norm_g, tr=tr, name="gated_norm_bwd")
    du, dw, dqg, dkd, dgl = _dn_scan_bwd(do_dn, w, qg, kd, vn, pmat, gl, hist, bsz=bsz, nc_seq=nc_seq,
                                          name="dn_scan_bwd")
    dqn, dkn, dv, dsa, d_alog, d_dtb, mlp_parts = _dn_intra_bwd(
        qn, kn, v, gates, alog_row, dtb_row, u, w, tmat, du, dw, dqg, dkd, do_dn, vn, dgl, nc_seq=nc_seq,
        name="dn_intra_bwd", rider=ride2)
    dz, d_conv_w = _dnprep_bwd_a(projp, conv_w, dqn, dkn, dv, bsz=bsz, t_seq=t_seq, tt=tt, name="dn_prep_bwd")
    dcin = _dnprep_bwd_b(dz, conv_w, bsz=bsz, t_seq=t_seq, tt=tt, name="conv_bwd")
    gdqg, gdkd, gdvi, gdgl = _gla_scan_bwd(do_gla, gqg, gkd, ggl, projp, ghist, bsz=bsz, nc_seq=nc_seq,
                                            name="gla_scan_bwd")
    dgqk, dgv, dsb, d_w2p, d_gla_b = _gla_intra_bwd(projp, gates, w2p, gla_b, do_gla, gdqg, gdkd, gdvi, gdgl,
                                                    nc_seq=nc_seq, name="gla_intra_bwd")

    secs = (dcin, ddz, dgqk, dgv, dgr, dsa, dsb)
    g_lo = _grad_tn(h, secs[0:2], tk=tr, name="w_in_grad_lo")
    g_hi = _grad_tn(h, secs[2:7], tk=tr, name="w_in_grad_hi")
    ride3 = None
    if where is not None:
        late_sm = [_shards_from_padded(g_lo, g_hi), d_w_out.reshape(N_CHIPS, d // N_CHIPS, d)]
        late_theirs = _exchange_now(_SiblingHalvesRider(late_sm), name="sibling_halves")
        late_pair = [_pair_sum(where, a, b, name=f"pair_sum_{k}") for k, (a, b) in enumerate(zip(late_sm, late_theirs))]
        ride3 = _ChipExchangeRider(late_pair)
    grad_x, d_meta_rows, d_norm1_g, late_parts = _inproj_bwd(secs, wp, h0, norm1_g, dx1, bsz=bsz, t_seq=t_seq, tr=tt,
                                                             name="in_proj_bwd", rider=ride3)

    grads = dict(w_in_lo=g_lo, w_in_hi=g_hi, w_out=d_w_out, w_up=d_w_up, w_down=d_w_down, meta_rows=d_meta_rows,
                 norm1_g=d_norm1_g, conv_w=d_conv_w, a_log_tile=d_alog, dt_bias_tile=d_dtb, dn_norm_g=d_dn_norm_g,
                 gla_w2=d_w2p[0:GLA_RANK], gla_b=d_gla_b, gla_norm_g=d_gla_norm_g, norm2_g=d_norm2_g,
                 final_norm_g=d_final_g, loss_tile=loss_tile)
    if where is not None:
        grads["exchanged"] = (late_pair + mlp_pair, list(late_parts) + list(mlp_parts))
    return grad_x, grads


SHARD_W = IN_WIDTH // N_CHIPS
PADDED_ORDER = ((0, 2048), (2056, 3592), (2048, 2056), LANE - 8, (3592, 3608), LANE - GLA_RANK)


def _pad_layout(w_full):
    pieces = [jnp.zeros((w_full.shape[0], seg), w_full.dtype) if isinstance(seg, int) else w_full[:, seg[0]:seg[1]]
              for seg in PADDED_ORDER]
    return jnp.concatenate(pieces, axis=1)


def _padded_from_shards(stack):
    pieces = []
    for seg in PADDED_ORDER:
        if isinstance(seg, int):
            pieces.append(jnp.zeros((stack.shape[1], seg), stack.dtype))
            continue
        for j in range(N_CHIPS):
            lo, hi = max(seg[0], j * SHARD_W), min(seg[1], (j + 1) * SHARD_W)
            if lo < hi:
                pieces.append(stack[j, :, lo - j * SHARD_W:hi - j * SHARD_W])
    return jnp.concatenate(pieces, axis=1)


def _shards_from_padded(g_lo, g_hi):
    split = g_lo.shape[1]
    starts, pos = [], 0
    for seg in PADDED_ORDER:
        width = seg if isinstance(seg, int) else seg[1] - seg[0]
        if not isinstance(seg, int):
            starts.append((seg[0], seg[1], pos))
        pos += width
    shards = []
    for j in range(N_CHIPS):
        pieces = []
        for a, b, p0 in sorted(starts):
            lo, hi = max(a, j * SHARD_W), min(b, (j + 1) * SHARD_W)
            if lo < hi:
                src, off = (g_lo, 0) if p0 < split else (g_hi, split)
                pieces.append(src[:, p0 + lo - a - off:p0 + hi - a - off])
        pieces.append(jnp.zeros((g_lo.shape[0], D_MODEL - SHARD_W), g_lo.dtype))
        shards.append(jnp.concatenate(pieces, axis=1))
    return jnp.stack(shards)


def _pack_small(g, bsz):
    assert bsz * N_META == 32
    row = jnp.concatenate([g["a_log_tile"], g["dt_bias_tile"], g["dn_norm_g"], g["gla_norm_g"], g["gla_b"],
                           g["loss_tile"], jnp.zeros((1, LANE), F32)], axis=1)
    return jnp.concatenate([g["meta_rows"], g["norm1_g"], g["conv_w"].reshape(6, D_MODEL), row,
                            g["gla_w2"].reshape(4, D_MODEL), g["norm2_g"], g["final_norm_g"],
                            jnp.zeros((2, D_MODEL), F32)], axis=0)


def kernel(x, meta_tokens, norm1_g, w_in, conv_w, a_log, dt_bias, dn_norm_g, gla_w2, gla_b, gla_norm_g, w_out, norm2_g, w_up, w_down, final_norm_g, loss_target, m_meta_tokens, m_norm1_g, m_w_in, m_conv_w, m_a_log, m_dt_bias, m_dn_norm_g, m_gla_w2, m_gla_b, m_gla_norm_g, m_w_out, m_norm2_g, m_w_up, m_w_down, m_final_norm_g, v_meta_tokens, v_norm1_g, v_w_in, v_conv_w, v_a_log, v_dt_bias, v_dn_norm_g, v_gla_w2, v_gla_b, v_gla_norm_g, v_w_out, v_norm2_g, v_w_up, v_w_down, v_final_norm_g):
    bsz = x.shape[0]
    chip = 2 * lax.axis_index("x") + lax.axis_index("y")

    lane_pad = lambda a, wd: jnp.pad(a, ((0, 0), (0, wd - a.shape[1])))
    where = jnp.stack([lax.axis_index("c"), chip]).astype(jnp.int32)
    slot = lambda a, dt, nm: _to_slot(where, a, dt, name="slot_" + nm)
    (g_meta,) = _exchange_now(_GatherRider([slot(meta_tokens, F32, "meta")], [False]), name="gather_meta")
    early = _GatherRider([slot(lane_pad(w_in[0], D_MODEL), BF16, "w_in"), slot(conv_w[0], F32, "conv"),
                          slot(lane_pad(gla_w2[0], LANE), F32, "gla_w2")], [True, False, False])
    late = (_GatherRider([slot(w_up[0], BF16, "w_up")], [True]), _GatherRider([slot(w_down[0], BF16, "w_down")], [True]),
            _GatherRider([slot(w_out[0], BF16, "w_out")], [True]))
    meta_f = g_meta.transpose(1, 0, 2).reshape(N_META, D_MODEL)

    grad_x, g = _local_step(x, loss_target, meta_f, norm1_g, None, None, a_log, dt_bias, dn_norm_g, None, gla_b,
                            gla_norm_g, None, norm2_g, None, None, final_norm_g.reshape(1, D_MODEL), late_gather=late,
                            where=where, early_gather=early)

    pair, parts = g["exchanged"]
    halves = [_chip_sum(where, p, q, name=f"chip_sum_{k}") for k, (p, q) in enumerate(zip(pair, parts))]
    gw_in, gw_out, gw_up, gw_down = _sibling_join(halves)

    red = _small_allreduce(_pack_small(g, bsz))
    g_meta_full = red[0:N_META]
    g_norm1 = red[32:33]
    g_conv_full = red[33:39].reshape(4, QKV_W)
    srow = red[39:40]
    g_alog, g_dtb = srow[:, 4:8], srow[:, LANE + 4:LANE + 8]
    g_dn_norm, g_gla_norm = srow[:, 2 * LANE:3 * LANE], srow[:, 3 * LANE:4 * LANE]
    g_gla_b = srow[:, 4 * LANE:6 * LANE]
    loss = srow[0, 6 * LANE]
    g_w2_full = red[40:44].reshape(GLA_RANK, GQ_W)
    g_norm2 = red[44:45]
    g_final = red[45:46]
    g_meta_sh = lax.dynamic_slice_in_dim(g_meta_full, chip * (D_MODEL // N_CHIPS), D_MODEL // N_CHIPS, axis=1)
    g_conv_sh = lax.dynamic_slice_in_dim(g_conv_full, chip * (QKV_W // N_CHIPS), QKV_W // N_CHIPS, axis=1)
    g_w2_sh = lax.dynamic_slice_in_dim(g_w2_full, chip * (GQ_W // N_CHIPS), GQ_W // N_CHIPS, axis=1)

    names = ["meta_tokens", "norm1_g", "w_in", "conv_w", "a_log", "dt_bias", "dn_norm_g", "gla_w2", "gla_b",
             "gla_norm_g", "w_out", "norm2_g", "w_up", "w_down", "final_norm_g"]
    weights = dict(meta_tokens=meta_tokens, norm1_g=norm1_g, w_in=w_in, conv_w=conv_w, a_log=a_log, dt_bias=dt_bias,
                   dn_norm_g=dn_norm_g, gla_w2=gla_w2, gla_b=gla_b, gla_norm_g=gla_norm_g, w_out=w_out,
                   norm2_g=norm2_g, w_up=w_up, w_down=w_down, final_norm_g=final_norm_g)
    ms = dict(meta_tokens=m_meta_tokens, norm1_g=m_norm1_g, w_in=m_w_in, conv_w=m_conv_w, a_log=m_a_log,
              dt_bias=m_dt_bias, dn_norm_g=m_dn_norm_g, gla_w2=m_gla_w2, gla_b=m_gla_b, gla_norm_g=m_gla_norm_g,
              w_out=m_w_out, norm2_g=m_norm2_g, w_up=m_w_up, w_down=m_w_down, final_norm_g=m_final_norm_g)
    vs = dict(meta_tokens=v_meta_tokens, norm1_g=v_norm1_g, w_in=v_w_in, conv_w=v_conv_w, a_log=v_a_log,
              dt_bias=v_dt_bias, dn_norm_g=v_dn_norm_g, gla_w2=v_gla_w2, gla_b=v_gla_b, gla_norm_g=v_gla_norm_g,
              w_out=v_w_out, norm2_g=v_norm2_g, w_up=v_w_up, w_down=v_w_down, final_norm_g=v_final_norm_g)
    grads2d = dict(meta_tokens=g_meta_sh, norm1_g=g_norm1, w_in=gw_in, conv_w=g_conv_sh, a_log=g_alog, dt_bias=g_dtb,
                   dn_norm_g=g_dn_norm, gla_w2=g_w2_sh, gla_b=g_gla_b, gla_norm_g=g_gla_norm, w_out=gw_out,
                   norm2_g=g_norm2, w_up=gw_up, w_down=gw_down, final_norm_g=g_final)
    out_g, out_d, out_m, out_v = [], [], [], []
    for nm in names:
        shape = weights[nm].shape
        g2 = grads2d[nm]
        if nm == "w_in":
            tview = lambda a: jnp.transpose(a, (2, 0, 1))
            res = _adamw_rows(tview(weights[nm]), g2[:, 0:SHARD_W].T.reshape(SHARD_W, 1, D_MODEL), tview(ms[nm]),
                              tview(vs[nm]), name=f"adamw_{nm}")
            res = [jnp.transpose(a, (1, 2, 0)) for a in res]
            gout = res[3]
        elif len(shape) == 3:
            res = _adamw(weights[nm], g2, ms[nm], vs[nm], name=f"adamw_{nm}")
            gout = g2.reshape(shape)
        else:
            as2d = lambda a: a.reshape(g2.shape)
            res = _adamw(as2d(weights[nm]), g2, as2d(ms[nm]), as2d(vs[nm]), name=f"adamw_{nm}")
            gout = g2.reshape(shape)
        out_g.append(gout)
        out_d.append(res[0].reshape(shape))
        out_m.append(res[1].reshape(shape))
        out_v.append(res[2].reshape(shape))
    return (loss, grad_x, *out_g, *out_d, *out_m, *out_v)
```

```python
import functools

import jax
import jax.numpy as jnp
import numpy as np
from jax import lax
from jax.experimental import pallas as pl
from jax.experimental.pallas import tpu as pltpu

F32 = jnp.float32
BF16 = jnp.bfloat16
HI = lax.Precision.HIGHEST
MESH = pl.DeviceIdType.MESH

D_MODEL = 1024
N_META = 16
CHUNK = 64
N_PAD = CHUNK - N_META
NH = 4
DN_D = 128
GLA_DK = 64
GLA_DV = 128
GLA_RANK = 16
D_FF = 4 * D_MODEL
EPS = 1e-6
IN_WIDTH = 3608
C_QKV, C_DZ, C_GQK, C_GV, C_GR, C_SA, C_SB, PW = 0, 1536, 2048, 2560, 3072, 3584, 3712, 3840
LANE = 128
N_CHIPS = 4

ADAM_LR, ADAM_B1, ADAM_B2, ADAM_EPS, ADAM_WD, ADAM_STEP = 0.001, 0.9, 0.999, 1e-08, 0.01, 10

VMEM_BIG = 56 * 1024 * 1024


def _cp(vmem=None, sem=None):
    kw = {}
    if vmem is not None:
        kw["vmem_limit_bytes"] = vmem
    if sem is not None:
        kw["dimension_semantics"] = sem
    return pltpu.CompilerParams(**kw)


def _tile(n, target, mult=16):
    best = None
    for t in range(mult, min(n, target) + 1, mult):
        if n % t == 0:
            best = t
    assert best is not None, (n, target)
    return best


def _dot(a, b, dims, prec=None):
    return lax.dot_general(a, b, (dims, ((), ())), preferred_element_type=F32, precision=prec)


def _nn(a, b):
    return _dot(a.astype(BF16), b.astype(BF16), ((1,), (0,)))


def _nt(a, b):
    return _dot(a.astype(BF16), b.astype(BF16), ((1,), (1,)))


def _tn(a, b):
    return _dot(a.astype(BF16), b.astype(BF16), ((0,), (0,)))


def _split(x):
    hi = x.astype(BF16)
    return hi, (x - hi.astype(F32)).astype(BF16)


def _tri_sum(tri, x):
    t = tri.astype(BF16)
    hi = x.astype(BF16)
    r1 = x - hi.astype(F32)
    mid = r1.astype(BF16)
    lo = (r1 - mid.astype(F32)).astype(BF16)
    nn = ((1,), (0,))
    return _dot(t, hi, nn) + _dot(t, mid, nn) + _dot(t, lo, nn)


def _sigmoid(x):
    return 0.5 * jnp.tanh(0.5 * x) + 0.5


def _softplus(x):
    return jnp.maximum(x, 0.0) + jnp.log(1.0 + jnp.exp(-jnp.abs(x)))


def _logsigmoid(x):
    return -_softplus(-x)


def _iota2(shape, dim):
    return lax.broadcasted_iota(jnp.int32, shape, dim)


def _mm(a, b, mode, *, tm, tn, tk, out_dtypes, extras=(), epilogue=None, name, vmem=VMEM_BIG, rider=None,
        out_widths=None, n_chunk=None, vec_extras=(), shard_cols=None):
    if mode == "tn":
        K, M = a.shape
    else:
        M, K = a.shape
    N = b.shape[0] if mode == "nt" else b.shape[1]
    assert M % tm == 0 and N % tn == 0 and K % tk == 0, (name, M, N, K, tm, tn, tk)
    nk = K // tk
    n_ex, n_out, n_vec = len(extras), len(out_dtypes), len(vec_extras)
    if mode == "tn":
        a_spec = pl.BlockSpec((tk, tm), lambda i, j, k: (k, i))
    else:
        a_spec = pl.BlockSpec((tm, tk), lambda i, j, k: (i, k))
    if mode == "nt":
        b_spec = pl.BlockSpec((tn, tk), lambda i, j, k: (j, k))
    else:
        b_spec = pl.BlockSpec((tk, tn), lambda i, j, k: (k, j))
    mn_spec = pl.BlockSpec((tm, tn), lambda i, j, k: (i, j))
    if out_widths is None:
        o_specs = [mn_spec] * n_out
        o_shapes = [jax.ShapeDtypeStruct((M, N), dt) for dt in out_dtypes]
    else:
        assert tn == N
        o_specs = [pl.BlockSpec((tm, wd), lambda i, j, k: (i, 0)) for wd in out_widths]
        o_shapes = [jax.ShapeDtypeStruct((M, wd), dt) for wd, dt in zip(out_widths, out_dtypes)]
    if shard_cols is not None:
        o_specs = [pl.BlockSpec((tn // shard_cols, tm, shard_cols), lambda i, j, k: (j, i, 0))]
        o_shapes = [jax.ShapeDtypeStruct((N // shard_cols, M, shard_cols), F32)]
    dims = {"nn": ((1,), (0,)), "nt": ((1,), (1,)), "tn": ((0,), (0,))}[mode]

    single = nk == 1
    direct = (not single) and epilogue is None and n_out == 1 and out_dtypes[0] == F32

    def body(*refs):
        a_ref, b_ref = refs[0], refs[1]
        ex_refs = refs[2:2 + n_ex]
        vec_refs = refs[2 + n_ex:2 + n_ex + n_vec]
        out_refs = refs[2 + n_ex + n_vec:2 + n_ex + n_vec + n_out]
        if n_chunk is not None:
            assert single and out_widths is None and mode != "tn" and tn % n_chunk == 0
            av = a_ref[...].astype(BF16)
            for j in range(tn // n_chunk):
                cols = slice(j * n_chunk, (j + 1) * n_chunk)
                bv = b_ref[cols, :] if mode == "nt" else b_ref[:, cols]
                acc = _dot(av, bv.astype(BF16), dims)
                res = (acc,) if epilogue is None else epilogue(acc, *[e[:, cols] for e in ex_refs])
                for o_ref, r in zip(out_refs, res):
                    o_ref[:, cols] = r.astype(o_ref.dtype)
            return
        part = _dot(a_ref[...].astype(BF16), b_ref[...].astype(BF16), dims)

        def finish(acc):
            res = (acc,) if epilogue is None else epilogue(acc, *[e[...] for e in ex_refs], *[v[...] for v in vec_refs])
            for o_ref, r in zip(out_refs, res):
                o_ref[...] = r.astype(o_ref.dtype)

        if single:
            if shard_cols is not None:
                for sh in range(tn // shard_cols):
                    out_refs[0][sh] = part[:, sh * shard_cols:(sh + 1) * shard_cols]
            else:
                finish(part)
            return
        acc_ref = out_refs[0] if direct else refs[2 + n_ex + n_vec + n_out]
        k = pl.program_id(2)
        if shard_cols is not None:
            assert direct
            for sh in range(tn // shard_cols):
                piece = part[:, sh * shard_cols:(sh + 1) * shard_cols]

                @pl.when(k == 0)
                def _():
                    acc_ref[sh] = piece

                @pl.when(k > 0)
                def _():
                    acc_ref[sh] += piece
            return

        @pl.when(k == 0)
        def _():
            acc_ref[...] = part

        @pl.when(k > 0)
        def _():
            acc_ref[...] += part

        if not direct:
            @pl.when(k == nk - 1)
            def _():
                finish(acc_ref[...])

    outs, ridden = _hosted_call(
        body, rider, name=name, grid=(M // tm, N // tn, nk),
        in_specs=[a_spec, b_spec] + [mn_spec] * n_ex + [pl.BlockSpec((1, tn), lambda i, j, k: (0, j))] * n_vec,
        out_specs=o_specs, out_shape=o_shapes,
        scratch_shapes=[] if (single or direct) else [pltpu.VMEM((tm, tn), F32)],
        compiler_params=_cp(vmem, ("parallel", "parallel", "arbitrary")), args=(a, b, *extras, *vec_extras))
    return tuple(outs) if rider is None else (tuple(outs), ridden)


def _grad_tn(a, secs, *, tk, name):
    kk, m = a.shape
    widths = [s.shape[1] for s in secs]
    total = sum(widths)
    nk = kk // tk

    def body(*refs):
        a_ref, sec_refs, o_ref = refs[0], refs[1:-1], refs[-1]
        cat = sec_refs[0][...] if len(sec_refs) == 1 else jnp.concatenate([s[...] for s in sec_refs], axis=1)
        part = _dot(a_ref[...].astype(BF16), cat.astype(BF16), ((0,), (0,)))
        k = pl.program_id(0)

        @pl.when(k == 0)
        def _():
            o_ref[...] = part

        @pl.when(k > 0)
        def _():
            o_ref[...] += part

    return pl.pallas_call(
        body, name=name, grid=(nk,),
        in_specs=[pl.BlockSpec((tk, m), lambda k: (k, 0))] + [pl.BlockSpec((tk, w), lambda k: (k, 0)) for w in widths],
        out_specs=pl.BlockSpec((m, total), lambda k: (0, 0)),
        out_shape=jax.ShapeDtypeStruct((m, total), F32),
        compiler_params=_cp(VMEM_BIG, ("arbitrary",)),
    )(a, *secs)


class _ShiftedRows:
    def __init__(self, src, buf, sems, *, per_seq, tt, steps):
        self.src, self.buf, self.sems = src, buf, sems
        self.per_seq, self.tt, self.steps = per_seq, tt, steps

    def _do(self, step, slot, act):
        b, j = step // self.per_seq, step % self.per_seq
        tt = self.tt

        @pl.when(j == 0)
        def _():
            act(pltpu.make_async_copy(self.src.at[b, pl.ds(0, tt - CHUNK), :],
                                      self.buf.at[slot, pl.ds(CHUNK, tt - CHUNK), :], self.sems.at[slot]))

        if self.per_seq > 1:
            @pl.when(j > 0)
            def _():
                act(pltpu.make_async_copy(self.src.at[b, pl.ds(j * tt - CHUNK, tt), :], self.buf.at[slot],
                                          self.sems.at[slot]))

    def tile(self, i):
        slot = i % 2

        @pl.when(i == 0)
        def _():
            self._do(i, slot, lambda cp: cp.start())

        self._do(i, slot, lambda cp: cp.wait())

        @pl.when(i + 1 < self.steps)
        def _():
            self._do(i + 1, 1 - slot, lambda cp: cp.start())

        return slot


def _embed_norm(x, lead, g, *, tr, name, rider=None):
    bsz, s_len, d = x.shape
    t_seq = s_len + CHUNK
    per_seq = t_seq // tr
    steps = bsz * per_seq
    n = bsz * t_seq

    def body(x_hbm, lead_ref, g_ref, h0_ref, h_ref, buf, sems):
        i = pl.program_id(0)
        slot = _ShiftedRows(x_hbm, buf, sems, per_seq=per_seq, tt=tr, steps=steps).tile(i)

        @pl.when(i % per_seq == 0)
        def _():
            buf[slot, 0:CHUNK, :] = lead_ref[...]

        xv = buf[slot]
        h0_ref[...] = xv
        r = lax.rsqrt(jnp.mean(xv * xv, axis=-1, keepdims=True) + EPS)
        h_ref[...] = (xv * r * g_ref[...]).astype(h_ref.dtype)

    row = pl.BlockSpec((tr, d), lambda i: (i, 0))
    outs, ridden = _hosted_call(
        body, rider, name=name, grid=(steps,),
        in_specs=[ANY, pl.BlockSpec((CHUNK, d), lambda i: (0, 0)), pl.BlockSpec((1, d), lambda i: (0, 0))],
        out_specs=[row, row],
        out_shape=[jax.ShapeDtypeStruct((n, d), F32), jax.ShapeDtypeStruct((n, d), BF16)],
        scratch_shapes=[pltpu.VMEM((2, tr, d), F32), pltpu.SemaphoreType.DMA((2,))],
        compiler_params=_cp(VMEM_BIG, ("arbitrary",)), args=(x, lead, g))
    return (*outs, ridden)


def _rms_fwd(x, g, *, tr, name):
    n, d = x.shape

    def body(x_ref, g_ref, o_ref):
        xv = x_ref[...]
        r = lax.rsqrt(jnp.mean(xv * xv, axis=-1, keepdims=True) + EPS)
        o_ref[...] = (xv * r * g_ref[...]).astype(o_ref.dtype)

    return pl.pallas_call(
        body, name=name, grid=(n // tr,),
        in_specs=[pl.BlockSpec((tr, d), lambda i: (i, 0)), pl.BlockSpec((1, d), lambda i: (0, 0))],
        out_specs=pl.BlockSpec((tr, d), lambda i: (i, 0)),
        out_shape=jax.ShapeDtypeStruct((n, d), BF16),
        compiler_params=_cp(VMEM_BIG),
    )(x, g)


def _rms_bwd_math(xv, g, dy):
    r = lax.rsqrt(jnp.mean(xv * xv, axis=-1, keepdims=True) + EPS)
    xh = xv * r
    gdy = dy * g
    dx = r * (gdy - xh * jnp.mean(xh * gdy, axis=-1, keepdims=True))
    return dx, jnp.sum(dy * xh, axis=0, keepdims=True)


def _mlp_up_bwd_norm(dup, w_up, x, g, res, *, tr, name, rider=None):
    n, d = x.shape
    ff = dup.shape[1]

    def body(dup_ref, w_ref, x_ref, g_ref, res_ref, o_ref, ob_ref, dg_ref):
        dh = _nt(dup_ref[...], w_ref[...])
        dx, dg = _rms_bwd_math(x_ref[...], g_ref[...], dh)
        tot = res_ref[...] + dx
        o_ref[...] = tot
        ob_ref[...] = tot.astype(BF16)

        @pl.when(pl.program_id(0) == 0)
        def _():
            dg_ref[...] = dg

        @pl.when(pl.program_id(0) > 0)
        def _():
            dg_ref[...] += dg

    row = pl.BlockSpec((tr, d), lambda i: (i, 0))
    vec = pl.BlockSpec((1, d), lambda i: (0, 0))
    outs, ridden = _hosted_call(
        body, rider, name=name, grid=(n // tr,),
        in_specs=[pl.BlockSpec((tr, ff), lambda i: (i, 0)), pl.BlockSpec((d, ff), lambda i: (0, 0)), row, vec, row],
        out_specs=[row, row, vec],
        out_shape=[jax.ShapeDtypeStruct((n, d), F32), jax.ShapeDtypeStruct((n, d), BF16),
                   jax.ShapeDtypeStruct((1, d), F32)],
        scratch_shapes=[], compiler_params=_cp(VMEM_BIG, ("arbitrary",)), args=(dup, w_up, x, g, res))
    return (*outs, ridden)


def _mlp_down_loss(act, w_down, x1, gf, tgt, *, t_seq, tr, name):
    n, d = x1.shape
    ff = act.shape[1]
    per_seq = t_seq // tr
    steps = n // tr

    def body(a_ref, w_ref, x_ref, g_ref, t_hbm, dx_ref, dxb_ref, dg_ref, loss_ref, tbuf, tsems):
        i = pl.program_id(0)
        slot = _ShiftedRows(t_hbm, tbuf, tsems, per_seq=per_seq, tt=tr, steps=steps).tile(i)

        @pl.when(i % per_seq == 0)
        def _():
            tbuf[slot, 0:CHUNK, :] = jnp.zeros((CHUNK, d), F32)

        t_ref = tbuf.at[slot]
        xv = x_ref[...] + _nn(a_ref[...], w_ref[...])
        g = g_ref[...]
        r = lax.rsqrt(jnp.mean(xv * xv, axis=-1, keepdims=True) + EPS)
        xh = xv * r
        pos = (i % per_seq) * tr + _iota2((tr, 1), 0)
        real = pos >= CHUNK
        err = jnp.where(real, xh * g - t_ref[...], 0.0)
        dy = err * (1.0 / d)
        gdy = dy * g
        dx = r * (gdy - xh * jnp.mean(xh * gdy, axis=-1, keepdims=True))
        dx_ref[...] = dx
        dxb_ref[...] = dx.astype(BF16)
        dg = jnp.sum(dy * xh, axis=0, keepdims=True)
        ls = 0.5 * jnp.sum(jnp.mean(err * err, axis=-1, keepdims=True), axis=0, keepdims=True)
        ls = jnp.where(_iota2((1, LANE), 1) == 0, ls, 0.0)

        @pl.when(i == 0)
        def _():
            dg_ref[...] = dg
            loss_ref[...] = ls

        @pl.when(i > 0)
        def _():
            dg_ref[...] += dg
            loss_ref[...] += ls

    row = pl.BlockSpec((tr, d), lambda i: (i, 0))
    vec = pl.BlockSpec((1, d), lambda i: (0, 0))
    one = pl.BlockSpec((1, LANE), lambda i: (0, 0))
    return pl.pallas_call(
        body, name=name, grid=(n // tr,),
        in_specs=[pl.BlockSpec((tr, ff), lambda i: (i, 0)), pl.BlockSpec((ff, d), lambda i: (0, 0)), row, vec, ANY],
        out_specs=[row, row, vec, one],
        out_shape=[jax.ShapeDtypeStruct((n, d), F32), jax.ShapeDtypeStruct((n, d), BF16),
                   jax.ShapeDtypeStruct((1, d), F32), jax.ShapeDtypeStruct((1, LANE), F32)],
        scratch_shapes=[pltpu.VMEM((2, tr, d), F32), pltpu.SemaphoreType.DMA((2,))],
        compiler_params=_cp(VMEM_BIG, ("arbitrary",)),
    )(act, w_down, x1, gf, tgt)


def _gnorm_fwd(o_dn, o_gla, projp, g_dn, g_gla, *, tr, name):
    n = o_dn.shape[0]
    w = NH * DN_D

    def body(odn_ref, ogl_ref, z_ref, r_ref, gdn_ref, ggl_ref, mix_ref):
        for grp, (o_ref, gate_ref, gain_ref) in enumerate(((odn_ref, z_ref, gdn_ref), (ogl_ref, r_ref, ggl_ref))):
            gain = gain_ref[...]
            for h in range(NH):
                sl = slice(h * DN_D, (h + 1) * DN_D)
                o = o_ref[:, sl].astype(F32)
                z = gate_ref[:, sl].astype(F32)
                r = lax.rsqrt(jnp.mean(o * o, axis=-1, keepdims=True) + EPS)
                y = (o * r * gain) * (z * _sigmoid(z))
                mix_ref[:, grp * w + h * DN_D: grp * w + (h + 1) * DN_D] = y.astype(mix_ref.dtype)

    row = pl.BlockSpec((tr, w), lambda i: (i, 0))
    vec = pl.BlockSpec((1, DN_D), lambda i: (0, 0))
    return pl.pallas_call(
        body, name=name, grid=(n // tr,),
        in_specs=[row, row, pl.BlockSpec((tr, w), lambda i: (i, C_DZ // w)),
                  pl.BlockSpec((tr, w), lambda i: (i, C_GR // w)), vec, vec],
        out_specs=pl.BlockSpec((tr, 2 * w), lambda i: (i, 0)),
        out_shape=jax.ShapeDtypeStruct((n, 2 * w), BF16),
        compiler_params=_cp(VMEM_BIG),
    )(o_dn, o_gla, projp, projp, g_dn, g_gla)


def _gnorm_bwd(dmix, o_dn, o_gla, projp, g_dn, g_gla, *, tr, name):
    n = o_dn.shape[0]
    w = NH * DN_D

    def body(dm_ref, odn_ref, ogl_ref, z_ref, r_ref, gdn_ref, ggl_ref,
             dodn_ref, ddz_ref, dogl_ref, dgr_ref, dgdn_ref, dggl_ref):
        first = pl.program_id(0) == 0
        groups = ((odn_ref, z_ref, gdn_ref, dodn_ref, ddz_ref, dgdn_ref),
                  (ogl_ref, r_ref, ggl_ref, dogl_ref, dgr_ref, dggl_ref))
        for grp, (o_ref, gate_ref, gain_ref, do_ref, dgate_ref, dgain_ref) in enumerate(groups):
            gain = gain_ref[...]
            dgain = jnp.zeros((1, DN_D), F32)
            for h in range(NH):
                sl = slice(h * DN_D, (h + 1) * DN_D)
                o = o_ref[:, sl].astype(F32)
                z = gate_ref[:, sl].astype(F32)
                dm = dm_ref[:, grp * w + h * DN_D: grp * w + (h + 1) * DN_D].astype(F32)
                r = lax.rsqrt(jnp.mean(o * o, axis=-1, keepdims=True) + EPS)
                oh = o * r
                s = _sigmoid(z)
                dn = dm * (z * s)
                dgate_ref[:, sl] = (dm * (oh * gain) * (s * (1.0 + z * (1.0 - s)))).astype(dgate_ref.dtype)
                gdn = dn * gain
                do_ref[:, sl] = (r * (gdn - oh * jnp.mean(oh * gdn, axis=-1, keepdims=True))).astype(do_ref.dtype)
                dgain = dgain + jnp.sum(dn * oh, axis=0, keepdims=True)

            @pl.when(first)
            def _():
                dgain_ref[...] = dgain

            @pl.when(jnp.logical_not(first))
            def _():
                dgain_ref[...] += dgain

    row = pl.BlockSpec((tr, w), lambda i: (i, 0))
    vec = pl.BlockSpec((1, DN_D), lambda i: (0, 0))
    big = jax.ShapeDtypeStruct((n, w), F32)
    gate = jax.ShapeDtypeStruct((n, w), BF16)
    small = jax.ShapeDtypeStruct((1, DN_D), F32)
    return pl.pallas_call(
        body, name=name, grid=(n // tr,),
        in_specs=[pl.BlockSpec((tr, 2 * w), lambda i: (i, 0)), row, row,
                  pl.BlockSpec((tr, w), lambda i: (i, C_DZ // w)), pl.BlockSpec((tr, w), lambda i: (i, C_GR // w)), vec, vec],
        out_specs=[row, row, row, row, vec, vec],
        out_shape=[gate, gate, gate, gate, small, small],
        compiler_params=_cp(VMEM_BIG),
    )(dmix, o_dn, o_gla, projp, projp, g_dn, g_gla)


QKV_W = 3 * NH * DN_D
HALO = 8


def _conv_z(xs_ref, cw_ref, tt):
    z = cw_ref[0:1, :] * xs_ref[pl.ds(HALO - 3, tt), :]
    for j in range(1, 4):
        z = z + cw_ref[j:j + 1, :] * xs_ref[pl.ds(HALO - 3 + j, tt), :]
    return z


def _dnprep_fwd(projp, conv_w, *, bsz, t_seq, tt, name):
    n = bsz * t_seq
    per_seq = t_seq // tt
    hw = NH * DN_D

    def body(x_ref, halo_ref, cw_ref, q_ref, k_ref, v_ref, xs_ref):
        i = pl.program_id(1)
        xs_ref[0:HALO, :] = jnp.where(i == 0, 0.0, halo_ref[...].astype(F32))
        xs_ref[HALO:HALO + tt, :] = x_ref[...].astype(F32)
        z = _conv_z(xs_ref, cw_ref, tt)
        a = z * _sigmoid(z)
        for grp, o_ref in enumerate((q_ref, k_ref)):
            for h in range(NH):
                ah = a[:, grp * hw + h * DN_D: grp * hw + (h + 1) * DN_D]
                rs = lax.rsqrt(jnp.sum(ah * ah, axis=-1, keepdims=True) + EPS)
                o_ref[:, h * DN_D:(h + 1) * DN_D] = (ah * rs).astype(o_ref.dtype)
        v_ref[...] = a[:, 2 * hw:3 * hw].astype(v_ref.dtype)

    def halo_map(b, i):
        return (jnp.maximum((b * t_seq + i * tt) // HALO - 1, 0), 0)

    out = pl.BlockSpec((tt, hw), lambda b, i: (b * per_seq + i, 0))
    sds = jax.ShapeDtypeStruct((n, hw), BF16)
    return pl.pallas_call(
        body, name=name, grid=(bsz, per_seq),
        in_specs=[pl.BlockSpec((tt, QKV_W), lambda b, i: (b * per_seq + i, 0)),
                  pl.BlockSpec((HALO, QKV_W), halo_map),
                  pl.BlockSpec((4, QKV_W), lambda b, i: (0, 0))],
        out_specs=[out, out, out], out_shape=[sds, sds, sds],
        scratch_shapes=[pltpu.VMEM((tt + HALO, QKV_W), F32)],
        compiler_params=_cp(VMEM_BIG),
    )(projp, projp, conv_w)


def _dnprep_bwd_a(projp, conv_w, dq, dk, dv, *, bsz, t_seq, tt, name):
    n = bsz * t_seq
    per_seq = t_seq // tt
    hw = NH * DN_D

    def body(x_ref, halo_ref, cw_ref, dq_ref, dk_ref, dv_ref, dz_ref, dcw_ref, xs_ref):
        b, i = pl.program_id(0), pl.program_id(1)
        xs_ref[0:HALO, :] = jnp.where(i == 0, 0.0, halo_ref[...].astype(F32))
        xs_ref[HALO:HALO + tt, :] = x_ref[...].astype(F32)
        z = _conv_z(xs_ref, cw_ref, tt)
        s = _sigmoid(z)
        a = z * s
        dsilu = s * (1.0 + z * (1.0 - s))
        for grp, d_ref in enumerate((dq_ref, dk_ref)):
            for h in range(NH):
                sl = slice(grp * hw + h * DN_D, grp * hw + (h + 1) * DN_D)
                ah = a[:, sl]
                rs = lax.rsqrt(jnp.sum(ah * ah, axis=-1, keepdims=True) + EPS)
                y = ah * rs
                dy = d_ref[:, h * DN_D:(h + 1) * DN_D]
                da = rs * (dy - y * jnp.sum(dy * y, axis=-1, keepdims=True))
                dz_ref[:, sl] = da * dsilu[:, sl]
        dz_ref[:, 2 * hw:3 * hw] = dv_ref[...] * dsilu[:, 2 * hw:3 * hw]
        dz = dz_ref[...]
        first = jnp.logical_and(b == 0, i == 0)
        for j in range(4):
            part = jnp.sum(dz * xs_ref[pl.ds(HALO - 3 + j, tt), :], axis=0, keepdims=True)

            @pl.when(first)
            def _():
                dcw_ref[j:j + 1, :] = part

            @pl.when(jnp.logical_not(first))
            def _():
                dcw_ref[j:j + 1, :] += part

    def halo_map(b, i):
        return (jnp.maximum((b * t_seq + i * tt) // HALO - 1, 0), 0)

    hrow = pl.BlockSpec((tt, hw), lambda b, i: (b * per_seq + i, 0))
    return pl.pallas_call(
        body, name=name, grid=(bsz, per_seq),
        in_specs=[pl.BlockSpec((tt, QKV_W), lambda b, i: (b * per_seq + i, 0)),
                  pl.BlockSpec((HALO, QKV_W), halo_map),
                  pl.BlockSpec((4, QKV_W), lambda b, i: (0, 0)), hrow, hrow, hrow],
        out_specs=[pl.BlockSpec((tt, QKV_W), lambda b, i: (b * per_seq + i, 0)),
                   pl.BlockSpec((4, QKV_W), lambda b, i: (0, 0))],
        out_shape=[jax.ShapeDtypeStruct((n, QKV_W), F32), jax.ShapeDtypeStruct((4, QKV_W), F32)],
        scratch_shapes=[pltpu.VMEM((tt + HALO, QKV_W), F32)],
        compiler_params=_cp(VMEM_BIG),
    )(projp, projp, conv_w, dq, dk, dv)


def _dnprep_bwd_b(dz, conv_w, *, bsz, t_seq, tt, name):
    n = bsz * t_seq
    per_seq = t_seq // tt
    last_blk = n // HALO - 1

    def body(dz_ref, halo_ref, cw_ref, dx_ref, ds_ref):
        i = pl.program_id(1)
        ds_ref[0:tt, :] = dz_ref[...].astype(F32)
        ds_ref[tt:tt + HALO, :] = jnp.where(i == per_seq - 1, 0.0, halo_ref[...].astype(F32))
        dx = cw_ref[0:1, :] * ds_ref[pl.ds(3, tt), :]
        for j in range(1, 4):
            dx = dx + cw_ref[j:j + 1, :] * ds_ref[pl.ds(3 - j, tt), :]
        dx_ref[...] = dx.astype(dx_ref.dtype)

    def halo_map(b, i):
        return (jnp.minimum((b * t_seq + (i + 1) * tt) // HALO, last_blk), 0)

    row = pl.BlockSpec((tt, QKV_W), lambda b, i: (b * per_seq + i, 0))
    return pl.pallas_call(
        body, name=name, grid=(bsz, per_seq),
        in_specs=[row, pl.BlockSpec((HALO, QKV_W), halo_map), pl.BlockSpec((4, QKV_W), lambda b, i: (0, 0))],
        out_specs=row, out_shape=jax.ShapeDtypeStruct((n, QKV_W), BF16),
        scratch_shapes=[pltpu.VMEM((tt + HALO, QKV_W), F32)],
        compiler_params=_cp(VMEM_BIG),
    )(dz, dz, conv_w)


def _masks64():
    r = _iota2((CHUNK, CHUNK), 0)
    c = _iota2((CHUNK, CHUNK), 1)
    return r, c


def _group(nc_seq, target=5):
    return max(g for g in range(1, target + 1) if nc_seq % g == 0)


def _round_robin(chains):
    live = list(chains)
    while live:
        nxt = []
        for ch in live:
            try:
                next(ch)
                nxt.append(ch)
            except StopIteration:
                pass
        live = nxt
        yield


def _run(chains):
    for _ in _round_robin(chains):
        pass


def _per_chunk(inner, kinds, grp):
    def body(*refs):
        chains = []
        for gi in range(grp):
            views = []
            for r, kind in zip(refs, kinds):
                if kind == "row":
                    views.append(r.at[pl.ds(gi * CHUNK, CHUNK)])
                elif kind == "lead":
                    views.append(r.at[pl.ds(gi, 1)])
                else:
                    views.append(r)
            chains.append(inner(gi, *views))
        _run(chains)
    return body


def _accumulate(ref, val, gi):
    if gi > 0:
        ref[...] += val
        return
    first = pl.program_id(0) == 0

    @pl.when(first)
    def _():
        ref[...] = val

    @pl.when(jnp.logical_not(first))
    def _():
        ref[...] += val


ANY = pl.BlockSpec(memory_space=pl.ANY)


def _place():
    return lax.axis_index("x"), lax.axis_index("y"), lax.axis_index("c")


def _other_chips(x, y):
    return [(1 - x, y, 2 * (1 - x) + y), (x, 1 - y, 2 * x + 1 - y), (1 - x, 1 - y, 2 * (1 - x) + 1 - y)]


class _GatherRider:
    def __init__(self, bufs, split):
        self.inputs = list(bufs)
        self.split = list(split)
        self.out_shapes = [jax.ShapeDtypeStruct(b.shape, b.dtype) for b in bufs]
        self.aliases = {i: i for i in range(len(bufs))}
        self.sems = [pltpu.SemaphoreType.DMA((len(bufs), 3))] * 4

    def _rows(self, k, buf, c, mine=True):
        r = buf.shape[1]
        if not self.split[k]:
            return pl.ds(0, r)
        return pl.ds((c if mine else 1 - c) * (r // 2), r // 2)

    def _ici(self, k, d, bufs, sems, c, px, py, block):
        rows = self._rows(k, bufs[k], c)
        return pltpu.make_async_remote_copy(
            src_ref=bufs[k].at[block, rows, :], dst_ref=bufs[k].at[block, rows, :], send_sem=sems[0].at[k, d],
            recv_sem=sems[1].at[k, d], device_id=(px, py, c), device_id_type=MESH)

    def _pass(self, k, d, bufs, sems, x, y, c, block, mine):
        rows = self._rows(k, bufs[k], c, mine)
        return pltpu.make_async_remote_copy(
            src_ref=bufs[k].at[block, rows, :], dst_ref=bufs[k].at[block, rows, :], send_sem=sems[2].at[k, d],
            recv_sem=sems[3].at[k, d], device_id=(x, y, 1 - c), device_id_type=MESH)

    def first(self, in_refs, bufs, sems):
        x, y, c = _place()
        for k in range(len(bufs)):
            for d, (px, py, _) in enumerate(_other_chips(x, y)):
                self._ici(k, d, bufs, sems, c, px, py, 2 * x + y).start()

    def last(self, in_refs, bufs, sems):
        x, y, c = _place()
        chips = _other_chips(x, y)
        for k in range(len(bufs)):
            for d, (px, py, pj) in enumerate(chips):
                self._ici(k, d, bufs, sems, c, px, py, pj).wait_recv()
                if self.split[k]:
                    self._pass(k, d, bufs, sems, x, y, c, pj, True).start()
        for k in range(len(bufs)):
            for d, (px, py, pj) in enumerate(chips):
                if self.split[k]:
                    self._pass(k, d, bufs, sems, x, y, c, pj, False).wait_recv()
                    self._pass(k, d, bufs, sems, x, y, c, pj, True).wait_send()
                self._ici(k, d, bufs, sems, c, px, py, 2 * x + y).wait_send()


def _hosted_call(body, rider, *, name, grid, in_specs, out_specs, out_shape, scratch_shapes, compiler_params, args):
    if rider is None:
        outs = pl.pallas_call(body, name=name, grid=grid, in_specs=in_specs, out_specs=out_specs, out_shape=out_shape,
                              scratch_shapes=scratch_shapes, compiler_params=compiler_params)(*args)
        return list(outs), []
    n_in, n_out, n_scr = len(in_specs), len(out_specs), len(scratch_shapes)
    r_in, r_out = len(rider.inputs), len(rider.out_shapes)
    compiler_params = _cp(compiler_params.vmem_limit_bytes, ("arbitrary",) * len(grid))

    def full_body(*refs):
        ins = refs[:n_in]
        rins = refs[n_in:n_in + r_in]
        outs = refs[n_in + r_in:n_in + r_in + n_out]
        routs = refs[n_in + r_in + n_out:n_in + r_in + n_out + r_out]
        rest = refs[n_in + r_in + n_out + r_out:]
        scr, sems = rest[:n_scr], rest[n_scr:]
        ids = [pl.program_id(a) for a in range(len(grid))]
        is_first = functools.reduce(jnp.logical_and, [i == 0 for i in ids])
        is_last = functools.reduce(jnp.logical_and, [i == g - 1 for i, g in zip(ids, grid)])

        @pl.when(is_first)
        def _():
            rider.first(rins, routs, sems)

        body(*ins, *outs, *scr)

        @pl.when(is_last)
        def _():
            rider.last(rins, routs, sems)

    res = pl.pallas_call(
        full_body, name=name, grid=grid, in_specs=list(in_specs) + [ANY] * r_in,
        out_specs=list(out_specs) + [ANY] * r_out, out_shape=list(out_shape) + list(rider.out_shapes),
        input_output_aliases={n_in + i: n_out + o for i, o in rider.aliases.items()},
        scratch_shapes=list(scratch_shapes) + list(rider.sems), compiler_params=compiler_params,
    )(*args, *rider.inputs)
    return list(res[:n_out]), list(res[n_out:])


def _exchange_now(rider, *, name):
    r_in = len(rider.inputs)

    def body(*refs):
        rins = refs[:r_in]
        routs = refs[r_in:r_in + len(rider.out_shapes)]
        sems = refs[r_in + len(rider.out_shapes):]
        rider.first(rins, routs, sems)
        rider.last(rins, routs, sems)

    return pl.pallas_call(
        body, name=name, in_specs=[ANY] * r_in, out_specs=[ANY] * len(rider.out_shapes), out_shape=list(rider.out_shapes),
        input_output_aliases=dict(rider.aliases), scratch_shapes=list(rider.sems),
    )(*rider.inputs)


def _to_slot(where, a, dtype, *, name):
    r, cols = a.shape
    tr = _tile(r, 256, 16) if r > 256 else r

    def body(w_ref, a_ref, o_ref):
        o_ref[0] = a_ref[...].astype(o_ref.dtype)

    return pl.pallas_call(
        body, name=name,
        grid_spec=pltpu.PrefetchScalarGridSpec(
            num_scalar_prefetch=1, grid=(r // tr,),
            in_specs=[pl.BlockSpec((tr, cols), lambda i, w: (i, 0))],
            out_specs=pl.BlockSpec((1, tr, cols), lambda i, w: (w[1], i, 0))),
        out_shape=jax.ShapeDtypeStruct((N_CHIPS, r, cols), dtype), compiler_params=_cp(VMEM_BIG),
    )(where, a)


def _tri_inv(a_strict):
    r, c = _masks64()
    eye = (r == c).astype(F32)
    blk16 = (r // 16) == (c // 16)
    blk32 = (r // 32) == (c // 32)
    ld = jnp.where(blk16, a_strict, 0.0)
    x = eye - ld
    p = _nn(ld, ld)
    yield
    for step in range(3):
        xp = _nn(x, p)
        if step < 2:
            p = _nn(p, p)
        x = x + xp
        yield
    for lk in (jnp.where(jnp.logical_and(blk32, jnp.logical_not(blk16)), a_strict, 0.0),
               jnp.where(blk32, 0.0, a_strict)):
        y = x - eye
        s = lk + _nn(y, lk)
        yield
        x = x - s - _nn(s, y)
        yield
    return x


def _dn_gates(sa, alog, dtb, chunk_in_seq):
    rows = _iota2((CHUNK, LANE), 0)
    valid = jnp.logical_or(rows >= N_PAD, chunk_in_seq > 0)
    beta_t = _sigmoid(sa)
    ea = jnp.exp(alog)
    g_t = jnp.where(valid, -ea * _softplus(sa + dtb), 0.0)
    r, c = _masks64()
    ltri = (r >= c).astype(F32)
    gam_t = _tri_sum(ltri, g_t)
    return beta_t, g_t, gam_t, valid, ea


def _dn_intra_fwd(qn, kn, v, projp, alog_row, dtb_row, *, nc_seq, name, rider=None):
    n = qn.shape[0]
    nct = n // CHUNK
    hw = NH * DN_D
    scale = DN_D ** -0.5

    grp = _group(nc_seq)

    def inner(gi, q_ref, k_ref, v_ref, sa_ref, al_ref, dt_ref, u_ref, w_ref, qg_ref, kd_ref, p_ref, t_ref, gl_ref):
        ci = (pl.program_id(0) * grp + gi) % nc_seq
        beta_t, _, gam_t, _, _ = _dn_gates(sa_ref[...], al_ref[...], dt_ref[...], ci)
        yield
        gam_tt = gam_t.T
        r, c = _masks64()
        incl = r >= c
        strict = r > c

        def head(h):
            sl = slice(h * DN_D, (h + 1) * DN_D)
            beta_w = jnp.broadcast_to(beta_t[:, h:h + 1], (CHUNK, DN_D))
            gam_w = jnp.broadcast_to(gam_t[:, 4 + h:5 + h], (CHUNK, DN_D))
            gam_row = gam_tt[4 + h:5 + h, :]
            gl = gam_t[CHUNK - 1:CHUNK, 4 + h:5 + h]
            dec = jnp.exp(jnp.where(incl, gam_w[:, 0:CHUNK] - gam_row, -jnp.inf))
            kh = k_ref[:, sl].astype(F32)
            qh = q_ref[:, sl].astype(F32) * scale
            vh = v_ref[:, sl].astype(F32)
            kk = _nt(kh, kh)
            qk = _nt(qh, kh)
            yield
            a = jnp.where(strict, beta_w[:, 0:CHUNK] * kk * dec, 0.0)
            tm = yield from _tri_inv(a)
            egam_w = jnp.exp(gam_w)
            u_ref[:, sl] = _nn(tm, beta_w * vh).astype(u_ref.dtype)
            w_ref[:, sl] = _nn(tm, (beta_w * egam_w) * kh).astype(w_ref.dtype)
            qg_ref[:, sl] = (egam_w * qh).astype(qg_ref.dtype)
            kd_ref[:, sl] = (jnp.exp(gl - gam_w) * kh).astype(kd_ref.dtype)
            p_ref[0, h] = qk * dec
            t_ref[0, h] = tm
            gl_ref[0, h:h + 1, :] = jnp.broadcast_to(jnp.exp(gl), (1, LANE))

        yield from _round_robin([head(h) for h in range(NH)])

    rows = grp * CHUNK
    row = pl.BlockSpec((rows, hw), lambda i: (i, 0))
    vec = pl.BlockSpec((1, LANE), lambda i: (0, 0))
    mat = pl.BlockSpec((grp, NH, CHUNK, CHUNK), lambda i: (i, 0, 0, 0))
    big = jax.ShapeDtypeStruct((n, hw), BF16)
    msd = jax.ShapeDtypeStruct((nct, NH, CHUNK, CHUNK), F32)
    kinds = ["row"] * 4 + ["whole"] * 2 + ["row"] * 4 + ["lead"] * 3
    outs, ridden = _hosted_call(
        _per_chunk(inner, kinds, grp), rider, name=name, grid=(nct // grp,),
        in_specs=[row, row, row, pl.BlockSpec((rows, LANE), lambda i: (i, 0)), vec, vec],
        out_specs=[row, row, row, row, mat, mat, pl.BlockSpec((grp, NH, LANE), lambda i: (i, 0, 0))],
        out_shape=[big, big, big, big, msd, msd, jax.ShapeDtypeStruct((nct, NH, LANE), F32)],
        scratch_shapes=[], compiler_params=_cp(VMEM_BIG, ("arbitrary",)),
        args=(qn, kn, v, projp, alog_row, dtb_row))
    return (*outs, ridden)


def _dn_scan_fwd(u, w, qg, kd, p, gl, *, bsz, nc_seq, name, rider=None):
    hw = NH * DN_D
    t_seq = nc_seq * CHUNK
    u, w, qg, kd = (z.reshape(bsz, t_seq, hw) for z in (u, w, qg, kd))
    p = p.reshape(bsz, nc_seq, NH, CHUNK, CHUNK)
    gl = gl.reshape(bsz, nc_seq, NH, LANE)
    grp = _group(nc_seq)

    def body(u_ref, w_ref, qg_ref, kd_ref, p_ref, gl_ref, o_ref, vn_ref, hist_ref, s_ref):
        @pl.when(pl.program_id(0) == 0)
        def _():
            s_ref[...] = jnp.zeros_like(s_ref)

        def chain(b, h, gi):
            sl = slice(h * DN_D, (h + 1) * DN_D)
            rows = pl.ds(gi * CHUNK, CHUNK)
            s = s_ref[b, h]
            hist_ref[b, gi, h] = s.astype(hist_ref.dtype)
            ws = _nn(w_ref[b, rows, sl], s)
            qs = _nn(qg_ref[b, rows, sl], s)
            yield
            vn = u_ref[b, rows, sl] - ws
            vn_ref[b, rows, sl] = vn.astype(vn_ref.dtype)
            o_ref[b, rows, sl] = (qs + _nn(p_ref[b, gi, h], vn)).astype(o_ref.dtype)
            s_ref[b, h] = gl_ref[b, gi, h:h + 1, :] * s + _tn(kd_ref[b, rows, sl], vn)

        for gi in range(grp):
            _run([chain(b, h, gi) for b in range(bsz) for h in range(NH)])

    row = pl.BlockSpec((bsz, grp * CHUNK, hw), lambda i: (0, i, 0))
    outs, ridden = _hosted_call(
        body, rider, name=name, grid=(nc_seq // grp,),
        in_specs=[row, row, row, row, pl.BlockSpec((bsz, grp, NH, CHUNK, CHUNK), lambda i: (0, i, 0, 0, 0)),
                  pl.BlockSpec((bsz, grp, NH, LANE), lambda i: (0, i, 0, 0))],
        out_specs=[row, row, pl.BlockSpec((bsz, grp, NH, DN_D, DN_D), lambda i: (0, i, 0, 0, 0))],
        out_shape=[jax.ShapeDtypeStruct((bsz, t_seq, hw), BF16), jax.ShapeDtypeStruct((bsz, t_seq, hw), BF16),
                   jax.ShapeDtypeStruct((bsz, nc_seq, NH, DN_D, DN_D), F32)],
        scratch_shapes=[pltpu.VMEM((bsz, NH, DN_D, DN_D), F32)],
        compiler_params=_cp(VMEM_BIG, ("arbitrary",)), args=(u, w, qg, kd, p, gl))
    o, vn, hist = outs
    return o.reshape(bsz * t_seq, hw), vn.reshape(bsz * t_seq, hw), hist, ridden


def _dn_scan_bwd(do, w, qg, kd, vn, p, gl, hist, *, bsz, nc_seq, name):
    hw = NH * DN_D
    t_seq = nc_seq * CHUNK
    do, w, qg, kd, vn = (z.reshape(bsz, t_seq, hw) for z in (do, w, qg, kd, vn))
    p = p.reshape(bsz, nc_seq, NH, CHUNK, CHUNK)
    gl = gl.reshape(bsz, nc_seq, NH, LANE)
    grp = _group(nc_seq)

    def body(do_ref, w_ref, qg_ref, kd_ref, vn_ref, p_ref, gl_ref, hist_ref,
             du_ref, dw_ref, dqg_ref, dkd_ref, dgl_ref, ds_ref):
        @pl.when(pl.program_id(0) == 0)
        def _():
            ds_ref[...] = jnp.zeros_like(ds_ref)

        def chain(b, h, gi):
            sl = slice(h * DN_D, (h + 1) * DN_D)
            rows = pl.ds(gi * CHUNK, CHUNK)
            s = hist_ref[b, gi, h]
            dsn = ds_ref[b, h]
            doh = do_ref[b, rows, sl]
            vnh = vn_ref[b, rows, sl]
            kdh = kd_ref[b, rows, sl]
            dvn = _tn(p_ref[b, gi, h], doh) + _nn(kdh, dsn)
            du_ref[b, rows, sl] = dvn.astype(du_ref.dtype)
            dqg_ref[b, rows, sl] = _nt(doh, s).astype(dqg_ref.dtype)
            dkd_ref[b, rows, sl] = _nt(vnh, dsn).astype(dkd_ref.dtype)
            ds_part = _tn(qg_ref[b, rows, sl], doh) + gl_ref[b, gi, h:h + 1, :] * dsn
            dgl = jnp.sum(jnp.sum(dsn * s, axis=0, keepdims=True), axis=1, keepdims=True)
            dgl_ref[b, gi, h:h + 1, :] = jnp.broadcast_to(dgl, (1, LANE))
            yield
            dw_ref[b, rows, sl] = (-_nt(dvn, s)).astype(dw_ref.dtype)
            ds_ref[b, h] = ds_part - _tn(w_ref[b, rows, sl], dvn)

        for gi in reversed(range(grp)):
            _run([chain(b, h, gi) for b in range(bsz) for h in range(NH)])

    steps = nc_seq // grp
    rev = lambda i: steps - 1 - i
    row = pl.BlockSpec((bsz, grp * CHUNK, hw), lambda i: (0, rev(i), 0))
    mat = pl.BlockSpec((bsz, grp, NH, CHUNK, CHUNK), lambda i: (0, rev(i), 0, 0, 0))
    glb = pl.BlockSpec((bsz, grp, NH, LANE), lambda i: (0, rev(i), 0, 0))
    big = jax.ShapeDtypeStruct((bsz, t_seq, hw), BF16)
    outs = pl.pallas_call(
        body, name=name, grid=(steps,),
        in_specs=[row, row, row, row, row, mat, glb,
                  pl.BlockSpec((bsz, grp, NH, DN_D, DN_D), lambda i: (0, rev(i), 0, 0, 0))],
        out_specs=[row, row, row, row, glb],
        out_shape=[big, big, jax.ShapeDtypeStruct(big.shape, F32), jax.ShapeDtypeStruct(big.shape, F32),
                   jax.ShapeDtypeStruct((bsz, nc_seq, NH, LANE), F32)],
        scratch_shapes=[pltpu.VMEM((bsz, NH, DN_D, DN_D), F32)],
        compiler_params=_cp(VMEM_BIG, ("arbitrary",)),
    )(do, w, qg, kd, vn, p, gl, hist)
    du, dw, dqg, dkd, dgl = outs
    n = bsz * t_seq
    return (du.reshape(n, hw), dw.reshape(n, hw), dqg.reshape(n, hw), dkd.reshape(n, hw),
            dgl.reshape(bsz * nc_seq, NH, LANE))


def _dn_intra_bwd(qn, kn, v, projp, alog_row, dtb_row, u, w, tmat, du, dw, dqg, dkd, do, vn, dgl, *, nc_seq, name,
                  rider=None):
    n = qn.shape[0]
    nct = n // CHUNK
    hw = NH * DN_D
    scale = DN_D ** -0.5

    grp = _group(nc_seq)

    def inner(gi, q_ref, k_ref, v_ref, sa_ref, al_ref, dt_ref, u_ref, w_ref, t_ref, du_ref, dw_ref, dqg_ref, dkd_ref,
              do_ref, vn_ref, dgl_ref, dq_ref, dk_ref, dv_ref, dsa_ref, dal_ref, ddt_ref):
        ci = (pl.program_id(0) * grp + gi) % nc_seq
        sa = sa_ref[...]
        beta_t, g_t, gam_t, valid, ea = _dn_gates(sa, al_ref[...], dt_ref[...], ci)
        yield
        lane = _iota2((CHUNK, LANE), 1)
        gates_t = jnp.where(lane < 4, beta_t, gam_t).T
        r, c = _masks64()
        incl, strict, upper, supper = r >= c, r > c, r <= c, r < c
        rows1 = _iota2((CHUNK, 1), 0)
        acc = [jnp.zeros((CHUNK, LANE), F32)]

        def head(h):
            sl = slice(h * DN_D, (h + 1) * DN_D)
            beta_w = jnp.broadcast_to(beta_t[:, h:h + 1], (CHUNK, DN_D))
            gam_w = jnp.broadcast_to(gam_t[:, 4 + h:5 + h], (CHUNK, DN_D))
            beta_s, gam_s = beta_w[:, 0:CHUNK], gam_w[:, 0:CHUNK]
            beta_row = gates_t[h:h + 1, :]
            gam_row = gates_t[4 + h:5 + h, :]
            gl = gam_t[CHUNK - 1:CHUNK, 4 + h:5 + h]
            dec = jnp.exp(jnp.where(incl, gam_s - gam_row, -jnp.inf))
            dec_t = jnp.exp(jnp.where(upper, gam_row - gam_s, -jnp.inf))
            egam_w = jnp.exp(gam_w)
            ekd_w = jnp.exp(gl - gam_w)
            kh = k_ref[:, sl].astype(F32)
            qh = q_ref[:, sl].astype(F32) * scale
            vh = v_ref[:, sl].astype(F32)
            uh = u_ref[:, sl]
            wh = w_ref[:, sl]
            doh = do_ref[:, sl]
            vnh = vn_ref[:, sl]
            kk = _nt(kh, kh)
            qk = _nt(qh, kh)
            qk_t = _nt(kh, qh)
            dp = _nt(doh, vnh)
            dp_t = _nt(vnh, doh)
            t_hi, t_lo = _split(t_ref[0, h].T)
            duh, dwh = du_ref[:, sl], dw_ref[:, sl]
            dvb = _nn(t_hi, duh) + _nn(t_lo, duh)
            dkg = _nn(t_hi, dwh) + _nn(t_lo, dwh)
            yield
            dvb_hi, dvb_lo = _split(dvb)
            dkg_hi, dkg_lo = _split(dkg)
            m = (_nt(dvb_hi, uh) + _nt(dvb_lo, uh)) + (_nt(dkg_hi, wh) + _nt(dkg_lo, wh))
            m_t = (_nt(uh, dvb_hi) + _nt(uh, dvb_lo)) + (_nt(wh, dkg_hi) + _nt(wh, dkg_lo))
            yield
            da = jnp.where(strict, -m, 0.0)
            da_t = jnp.where(supper, -m_t, 0.0)
            a = jnp.where(strict, beta_s * kk * dec, 0.0)
            a_t = jnp.where(supper, beta_row * kk * dec_t, 0.0)
            dad = da * dec
            dad_t = da_t * dec_t
            dpm = jnp.where(incl, dp, 0.0)
            dpm_t = jnp.where(upper, dp_t, 0.0)
            e = da * a + dpm * (qk * dec)
            e_t = da_t * a_t + dpm_t * (qk_t * dec_t)
            dqgh = dqg_ref[:, sl].astype(F32)
            dkdh = dkd_ref[:, sl].astype(F32)
            bg_w = beta_w * egam_w
            dkh = (_nn(beta_s * dad, kh) + _nn(beta_row * dad_t, kh) + _nn(dpm_t * dec_t, qh)
                   + bg_w * dkg + ekd_w * dkdh)
            dqh = _nn(dpm * dec, kh) + egam_w * dqgh
            t_kd = dkdh * (ekd_w * kh)
            dbeta = (jnp.sum(dad * kk, axis=1, keepdims=True)
                     + jnp.sum(dkg * (egam_w * kh) + dvb * vh, axis=1, keepdims=True))
            dgam = (jnp.sum(e - e_t, axis=1, keepdims=True)
                    + jnp.sum(dkg * (bg_w * kh) + dqgh * (egam_w * qh) - t_kd, axis=1, keepdims=True))
            dgam_last = (jnp.sum(jnp.sum(t_kd, axis=0, keepdims=True), axis=1, keepdims=True)
                         + dgl_ref[0, h:h + 1, 0:1] * jnp.exp(gl))
            dgam = dgam + jnp.where(rows1 == CHUNK - 1, dgam_last, 0.0)
            dq_ref[:, sl] = (dqh * scale).astype(dq_ref.dtype)
            dk_ref[:, sl] = dkh.astype(dk_ref.dtype)
            dv_ref[:, sl] = (beta_w * dvb).astype(dv_ref.dtype)
            acc[0] = acc[0] + jnp.where(lane == h, dbeta, 0.0) + jnp.where(lane == 4 + h, dgam, 0.0)

        yield from _round_robin([head(h) for h in range(NH)])
        acc_t = acc[0]
        dg_t = _tri_sum(upper, acc_t)
        ddb = acc_t * beta_t * (1.0 - beta_t)
        dda = jnp.where(valid, dg_t * (-ea) * _sigmoid(sa + dt_ref[...]), 0.0)
        dsa_ref[...] = jnp.where(lane < 4, ddb, jnp.where(lane < 8, dda, 0.0)).astype(dsa_ref.dtype)
        in_g = jnp.logical_and(lane >= 4, lane < 8)
        dal = jnp.sum(jnp.where(in_g, dg_t * g_t, 0.0), axis=0, keepdims=True)
        ddt = jnp.sum(jnp.where(in_g, dda, 0.0), axis=0, keepdims=True)
        _accumulate(dal_ref, dal, gi)
        _accumulate(ddt_ref, ddt, gi)

    rows = grp * CHUNK
    row = pl.BlockSpec((rows, hw), lambda i: (i, 0))
    vec = pl.BlockSpec((1, LANE), lambda i: (0, 0))
    mat = pl.BlockSpec((grp, NH, CHUNK, CHUNK), lambda i: (i, 0, 0, 0))
    glb = pl.BlockSpec((grp, NH, LANE), lambda i: (i, 0, 0))
    big = jax.ShapeDtypeStruct((n, hw), F32)
    v128 = jax.ShapeDtypeStruct((1, LANE), F32)
    kinds = (["row"] * 4 + ["whole"] * 2 + ["row"] * 2 + ["lead"] + ["row"] * 6 + ["lead"]
             + ["row"] * 4 + ["whole"] * 2)
    outs, ridden = _hosted_call(
        _per_chunk(inner, kinds, grp), rider, name=name, grid=(nct // grp,),
        in_specs=[row, row, row, pl.BlockSpec((rows, LANE), lambda i: (i, 0)), vec, vec,
                  row, row, mat, row, row, row, row, row, row, glb],
        out_specs=[row, row, row, pl.BlockSpec((rows, LANE), lambda i: (i, 0)), vec, vec],
        out_shape=[big, big, big, jax.ShapeDtypeStruct((n, LANE), BF16), v128, v128],
        scratch_shapes=[], compiler_params=_cp(VMEM_BIG, ("arbitrary",)),
        args=(qn, kn, v, projp, alog_row, dtb_row, u, w, tmat, du, dw, dqg, dkd, do, vn, dgl))
    return (*outs, ridden)


GQ_W = NH * GLA_DK
GV_W = NH * GLA_DV
GLA_NORM = 16.0
MID = CHUNK // 2


def _gla_gates(sb, w2p, gb, chunk_in_seq):
    rows = _iota2((CHUNK, GQ_W), 0)
    valid = jnp.logical_or(rows >= N_PAD, chunk_in_seq > 0)
    graw = _nn(sb, w2p) + gb
    yield
    g = jnp.where(valid, _logsigmoid(graw) * (1.0 / GLA_NORM), 0.0)
    r, c = _masks64()
    bcum = _tri_sum(r >= c, g)
    yield
    return graw, bcum, valid


def _head_mask(h):
    lane = _iota2((1, GQ_W), 1)
    return jnp.logical_and(lane >= h * GLA_DK, lane < (h + 1) * GLA_DK)


def _gla_intra_fwd(projp, gates, w2p, gb, *, nc_seq, name):
    n = projp.shape[0]
    nct = n // CHUNK
    scale = GLA_DK ** -0.5

    grp = _group(nc_seq)
    rows = grp * CHUNK

    def inner(gi, qk_ref, v_ref, sb_ref, w2_ref, gb_ref, oi_ref, qg_ref, kd_ref, gl_ref):
        ci = (pl.program_id(0) * grp + gi) % nc_seq
        _, bc, _ = yield from _gla_gates(sb_ref[...], w2_ref[...], gb_ref[...], ci)
        bref = bc[MID:MID + 1, :]
        bl = bc[CHUNK - 1:CHUNK, :]
        q = qk_ref[:, 0:GQ_W].astype(F32) * scale
        k = qk_ref[:, GQ_W:2 * GQ_W].astype(F32)
        qi = q * jnp.exp(bc - bref)
        ki = k * jnp.exp(bref - bc)
        qg_ref[...] = (q * jnp.exp(bc)).astype(qg_ref.dtype)
        kd_ref[...] = (k * jnp.exp(bl - bc)).astype(kd_ref.dtype)
        gl_ref[0] = jnp.exp(bl)
        r, c = _masks64()
        incl = r >= c
        a = [jnp.where(incl, _nt(jnp.where(_head_mask(h), qi, 0.0), ki), 0.0) for h in range(NH)]
        yield
        for h in range(NH):
            oi_ref[:, h * GLA_DV:(h + 1) * GLA_DV] = _nn(a[h], v_ref[:, h * GLA_DV:(h + 1) * GLA_DV]).astype(oi_ref.dtype)

    kinds = ["row"] * 3 + ["whole"] * 2 + ["row"] * 3 + ["lead"]
    return pl.pallas_call(
        _per_chunk(inner, kinds, grp), name=name, grid=(nct // grp,),
        in_specs=[pl.BlockSpec((rows, 2 * GQ_W), lambda i: (i, C_GQK // (2 * GQ_W))),
                  pl.BlockSpec((rows, GV_W), lambda i: (i, C_GV // GV_W)),
                  pl.BlockSpec((rows, LANE), lambda i: (i, 1)),
                  pl.BlockSpec((LANE, GQ_W), lambda i: (0, 0)), pl.BlockSpec((1, GQ_W), lambda i: (0, 0))],
        out_specs=[pl.BlockSpec((rows, GV_W), lambda i: (i, 0)), pl.BlockSpec((rows, GQ_W), lambda i: (i, 0)),
                   pl.BlockSpec((rows, GQ_W), lambda i: (i, 0)), pl.BlockSpec((grp, 1, GQ_W), lambda i: (i, 0, 0))],
        out_shape=[jax.ShapeDtypeStruct((n, GV_W), BF16), jax.ShapeDtypeStruct((n, GQ_W), BF16),
                   jax.ShapeDtypeStruct((n, GQ_W), BF16), jax.ShapeDtypeStruct((nct, 1, GQ_W), F32)],
        compiler_params=_cp(VMEM_BIG),
    )(projp, projp, gates, w2p, gb)


def _gla_scan_fwd(oi, qg, kd, gl, projp, *, bsz, nc_seq, name, rider=None):
    t_seq = nc_seq * CHUNK
    oi = oi.reshape(bsz, t_seq, GV_W)
    qg, kd = qg.reshape(bsz, t_seq, GQ_W), kd.reshape(bsz, t_seq, GQ_W)
    gl = gl.reshape(bsz, nc_seq, 1, GQ_W)
    pj = projp.reshape(bsz, t_seq, PW)
    grp = _group(nc_seq)

    def body(oi_ref, qg_ref, kd_ref, gl_ref, v_ref, o_ref, hist_ref, st_ref):
        @pl.when(pl.program_id(0) == 0)
        def _():
            st_ref[...] = jnp.zeros_like(st_ref)

        for gi in range(grp):
            rows = pl.ds(gi * CHUNK, CHUNK)
            for b in range(bsz):
                st = st_ref[b]
                hist_ref[b, gi] = st.astype(hist_ref.dtype)
                qgb = qg_ref[b, rows, :]
                kdb = kd_ref[b, rows, :]
                upd = jnp.zeros((GLA_DV, GQ_W), F32)
                for h in range(NH):
                    sl = slice(h * GLA_DV, (h + 1) * GLA_DV)
                    m = _head_mask(h)
                    o_ref[b, rows, sl] = (oi_ref[b, rows, sl] + _nt(jnp.where(m, qgb, 0.0), st)).astype(o_ref.dtype)
                    upd = upd + jnp.where(m, _tn(v_ref[b, rows, sl], kdb), 0.0)
                st_ref[b] = gl_ref[b, gi] * st + upd

    rws = grp * CHUNK
    outs, ridden = _hosted_call(
        body, rider, name=name, grid=(nc_seq // grp,),
        in_specs=[pl.BlockSpec((bsz, rws, GV_W), lambda i: (0, i, 0)),
                  pl.BlockSpec((bsz, rws, GQ_W), lambda i: (0, i, 0)),
                  pl.BlockSpec((bsz, rws, GQ_W), lambda i: (0, i, 0)),
                  pl.BlockSpec((bsz, grp, 1, GQ_W), lambda i: (0, i, 0, 0)),
                  pl.BlockSpec((bsz, rws, GV_W), lambda i: (0, i, C_GV // GV_W))],
        out_specs=[pl.BlockSpec((bsz, rws, GV_W), lambda i: (0, i, 0)),
                   pl.BlockSpec((bsz, grp, GLA_DV, GQ_W), lambda i: (0, i, 0, 0))],
        out_shape=[jax.ShapeDtypeStruct((bsz, t_seq, GV_W), BF16),
                   jax.ShapeDtypeStruct((bsz, nc_seq, GLA_DV, GQ_W), F32)],
        scratch_shapes=[pltpu.VMEM((bsz, GLA_DV, GQ_W), F32)],
        compiler_params=_cp(VMEM_BIG, ("arbitrary",)), args=(oi, qg, kd, gl, pj))
    return outs[0].reshape(bsz * t_seq, GV_W), outs[1], ridden


def _gla_scan_bwd(do, qg, kd, gl, projp, hist, *, bsz, nc_seq, name):
    t_seq = nc_seq * CHUNK
    do = do.reshape(bsz, t_seq, GV_W)
    qg, kd = qg.reshape(bsz, t_seq, GQ_W), kd.reshape(bsz, t_seq, GQ_W)
    gl = gl.reshape(bsz, nc_seq, 1, GQ_W)
    pj = projp.reshape(bsz, t_seq, PW)
    grp = _group(nc_seq)

    def body(do_ref, qg_ref, kd_ref, gl_ref, v_ref, hist_ref, dqg_ref, dkd_ref, dv_ref, dgl_ref, dst_ref):
        @pl.when(pl.program_id(0) == 0)
        def _():
            dst_ref[...] = jnp.zeros_like(dst_ref)

        for gi in reversed(range(grp)):
            rows = pl.ds(gi * CHUNK, CHUNK)
            for b in range(bsz):
                st = hist_ref[b, gi]
                dst = dst_ref[b]
                qgb = qg_ref[b, rows, :]
                kdb = kd_ref[b, rows, :]
                dqg = jnp.zeros((CHUNK, GQ_W), F32)
                dkd = jnp.zeros((CHUNK, GQ_W), F32)
                add = jnp.zeros((GLA_DV, GQ_W), F32)
                for h in range(NH):
                    sl = slice(h * GLA_DV, (h + 1) * GLA_DV)
                    m = _head_mask(h)
                    doh = do_ref[b, rows, sl]
                    vh = v_ref[b, rows, sl]
                    dqg = dqg + jnp.where(m, _nn(doh, st), 0.0)
                    dkd = dkd + jnp.where(m, _nn(vh, dst), 0.0)
                    dv_ref[b, rows, sl] = _nt(jnp.where(m, kdb, 0.0), dst).astype(dv_ref.dtype)
                    add = add + jnp.where(m, _tn(doh, qgb), 0.0)
                dqg_ref[b, rows, :] = dqg.astype(dqg_ref.dtype)
                dkd_ref[b, rows, :] = dkd.astype(dkd_ref.dtype)
                dgl_ref[b, gi] = jnp.sum(dst * st, axis=0, keepdims=True)
                dst_ref[b] = gl_ref[b, gi] * dst + add

    steps = nc_seq // grp
    rws = grp * CHUNK
    rev = lambda i: steps - 1 - i
    outs = pl.pallas_call(
        body, name=name, grid=(steps,),
        in_specs=[pl.BlockSpec((bsz, rws, GV_W), lambda i: (0, rev(i), 0)),
                  pl.BlockSpec((bsz, rws, GQ_W), lambda i: (0, rev(i), 0)),
                  pl.BlockSpec((bsz, rws, GQ_W), lambda i: (0, rev(i), 0)),
                  pl.BlockSpec((bsz, grp, 1, GQ_W), lambda i: (0, rev(i), 0, 0)),
                  pl.BlockSpec((bsz, rws, GV_W), lambda i: (0, rev(i), C_GV // GV_W)),
                  pl.BlockSpec((bsz, grp, GLA_DV, GQ_W), lambda i: (0, rev(i), 0, 0))],
        out_specs=[pl.BlockSpec((bsz, rws, GQ_W), lambda i: (0, rev(i), 0)),
                   pl.BlockSpec((bsz, rws, GQ_W), lambda i: (0, rev(i), 0)),
                   pl.BlockSpec((bsz, rws, GV_W), lambda i: (0, rev(i), 0)),
                   pl.BlockSpec((bsz, grp, 1, GQ_W), lambda i: (0, rev(i), 0, 0))],
        out_shape=[jax.ShapeDtypeStruct((bsz, t_seq, GQ_W), F32), jax.ShapeDtypeStruct((bsz, t_seq, GQ_W), F32),
                   jax.ShapeDtypeStruct((bsz, t_seq, GV_W), BF16), jax.ShapeDtypeStruct((bsz, nc_seq, 1, GQ_W), F32)],
        scratch_shapes=[pltpu.VMEM((bsz, GLA_DV, GQ_W), F32)],
        compiler_params=_cp(VMEM_BIG, ("arbitrary",)),
    )(do, qg, kd, gl, pj, hist)
    n = bsz * t_seq
    return (outs[0].reshape(n, GQ_W), outs[1].reshape(n, GQ_W), outs[2].reshape(n, GV_W),
            outs[3].reshape(bsz * nc_seq, 1, GQ_W))


def _gla_intra_bwd(projp, gates, w2p, gb, do, dqg, dkd, dvi, dgl, *, nc_seq, name):
    n = projp.shape[0]
    nct = n // CHUNK
    scale = GLA_DK ** -0.5

    grp = _group(nc_seq)
    rows = grp * CHUNK

    def inner(gi, qk_ref, v_ref, sb_ref, w2_ref, gb_ref, do_ref, dqg_ref, dkd_ref, dvi_ref, dgl_ref,
              dqk_ref, dv_ref, dsb_ref, dw2_ref, dgb_ref):
        ci = (pl.program_id(0) * grp + gi) % nc_seq
        sb = sb_ref[...]
        w2 = w2_ref[...]
        graw, bc, valid = yield from _gla_gates(sb, w2, gb_ref[...], ci)
        bref = bc[MID:MID + 1, :]
        bl = bc[CHUNK - 1:CHUNK, :]
        q = qk_ref[:, 0:GQ_W].astype(F32) * scale
        k = qk_ref[:, GQ_W:2 * GQ_W].astype(F32)
        ex1 = jnp.exp(bc - bref)
        ex2 = jnp.exp(bref - bc)
        eb = jnp.exp(bc)
        ekd = jnp.exp(bl - bc)
        qi, ki = q * ex1, k * ex2
        r, c = _masks64()
        incl = r >= c
        upper = r <= c
        a_t, da, da_t = [], [], []
        for h in range(NH):
            sl = slice(h * GLA_DV, (h + 1) * GLA_DV)
            doh = do_ref[:, sl]
            vh = v_ref[:, sl]
            a_t.append(jnp.where(upper, _nt(jnp.where(_head_mask(h), ki, 0.0), qi), 0.0))
            da.append(jnp.where(incl, _nt(doh, vh), 0.0))
            da_t.append(jnp.where(upper, _nt(vh, doh), 0.0))
        yield
        dqi = jnp.zeros((CHUNK, GQ_W), F32)
        dki = jnp.zeros((CHUNK, GQ_W), F32)
        for h in range(NH):
            sl = slice(h * GLA_DV, (h + 1) * GLA_DV)
            m = _head_mask(h)
            dv_ref[:, sl] = (_nn(a_t[h], do_ref[:, sl]) + dvi_ref[:, sl]).astype(dv_ref.dtype)
            dqi = dqi + jnp.where(m, _nn(da[h], ki), 0.0)
            dki = dki + jnp.where(m, _nn(da_t[h], qi), 0.0)
        yield
        dqg = dqg_ref[...].astype(F32)
        dkd = dkd_ref[...].astype(F32)
        dqk_ref[:, 0:GQ_W] = ((dqi * ex1 + dqg * eb) * scale).astype(dqk_ref.dtype)
        dqk_ref[:, GQ_W:2 * GQ_W] = (dki * ex2 + dkd * ekd).astype(dqk_ref.dtype)
        t_qi, t_ki, t_kd = dqi * qi, dki * ki, dkd * (k * ekd)
        db = t_qi - t_ki + dqg * (q * eb) - t_kd
        dbref = jnp.sum(t_ki - t_qi, axis=0, keepdims=True)
        dbl = jnp.sum(t_kd, axis=0, keepdims=True) + dgl_ref[0] * jnp.exp(bl)
        rows = _iota2((CHUNK, GQ_W), 0)
        db = db + jnp.where(rows == MID, dbref, 0.0) + jnp.where(rows == CHUNK - 1, dbl, 0.0)
        dg = _tri_sum(upper, db)
        yield
        dgraw = jnp.where(valid, dg * (1.0 / GLA_NORM) * _sigmoid(-graw), 0.0)
        dsb_ref[...] = _nt(dgraw, w2).astype(dsb_ref.dtype)
        dw2 = _tn(sb, dgraw)
        dgb = jnp.sum(dgraw, axis=0, keepdims=True)
        _accumulate(dw2_ref, dw2, gi)
        _accumulate(dgb_ref, dgb, gi)

    rq = pl.BlockSpec((rows, GQ_W), lambda i: (i, 0))
    rv = pl.BlockSpec((rows, GV_W), lambda i: (i, 0))
    kinds = ["row"] * 3 + ["whole"] * 2 + ["row"] * 4 + ["lead"] + ["row"] * 3 + ["whole"] * 2
    return pl.pallas_call(
        _per_chunk(inner, kinds, grp), name=name, grid=(nct // grp,),
        in_specs=[pl.BlockSpec((rows, 2 * GQ_W), lambda i: (i, C_GQK // (2 * GQ_W))),
                  pl.BlockSpec((rows, GV_W), lambda i: (i, C_GV // GV_W)),
                  pl.BlockSpec((rows, LANE), lambda i: (i, 1)),
                  pl.BlockSpec((LANE, GQ_W), lambda i: (0, 0)), pl.BlockSpec((1, GQ_W), lambda i: (0, 0)),
                  rv, rq, rq, rv, pl.BlockSpec((grp, 1, GQ_W), lambda i: (i, 0, 0))],
        out_specs=[pl.BlockSpec((rows, 2 * GQ_W), lambda i: (i, 0)), rv, pl.BlockSpec((rows, LANE), lambda i: (i, 0)),
                   pl.BlockSpec((LANE, GQ_W), lambda i: (0, 0)), pl.BlockSpec((1, GQ_W), lambda i: (0, 0))],
        out_shape=[jax.ShapeDtypeStruct((n, 2 * GQ_W), BF16), jax.ShapeDtypeStruct((n, GV_W), BF16),
                   jax.ShapeDtypeStruct((n, LANE), BF16), jax.ShapeDtypeStruct((LANE, GQ_W), F32),
                   jax.ShapeDtypeStruct((1, GQ_W), F32)],
        compiler_params=_cp(VMEM_BIG, ("arbitrary",)),
    )(projp, projp, gates, w2p, gb, do, dqg, dkd, dvi, dgl)


SECTIONS = ((C_QKV, 1536), (C_DZ, 512), (C_GQK, 512), (C_GV, 512), (C_GR, 512), (C_SA, 128), (C_SB, 128))


def _inproj_bwd(secs, wp, h0, g1, dx1, *, bsz, t_seq, tr, name, rider=None):
    n, d = h0.shape
    per_seq = t_seq // tr
    steps = n // tr
    s_len = t_seq - CHUNK

    def body(*refs):
        sec_refs = refs[:len(SECTIONS)]
        wp_ref, h0_ref, g_ref, dx1_ref, gx_hbm, meta_ref, dg_ref, obuf, sems = refs[len(SECTIONS):]
        i = pl.program_id(0)
        slot = i % 2

        def put(step, slot_, act):
            b, j = step // per_seq, step % per_seq

            @pl.when(j == 0)
            def _():
                act(pltpu.make_async_copy(obuf.at[slot_, pl.ds(CHUNK, tr - CHUNK), :],
                                          gx_hbm.at[b, pl.ds(0, tr - CHUNK), :], sems.at[slot_]))

            if per_seq > 1:
                @pl.when(j > 0)
                def _():
                    act(pltpu.make_async_copy(obuf.at[slot_], gx_hbm.at[b, pl.ds(j * tr - CHUNK, tr), :],
                                              sems.at[slot_]))

        dh = None
        for s_ref, (off, wd) in zip(sec_refs, SECTIONS):
            part = _nt(s_ref[...], wp_ref[:, off:off + wd])
            dh = part if dh is None else dh + part
        dx, dg = _rms_bwd_math(h0_ref[...], g_ref[...], dh)
        tot = dx1_ref[...] + dx

        @pl.when(i >= 2)
        def _():
            put(i - 2, slot, lambda cp: cp.wait())

        obuf[slot] = tot
        put(i, slot, lambda cp: cp.start())

        @pl.when(i % per_seq == 0)
        def _():
            meta_ref[...] = tot[N_PAD:CHUNK, :]

        @pl.when(i == steps - 1)
        def _():
            if steps > 1:
                put(i - 1, 1 - slot, lambda cp: cp.wait())
            put(i, slot, lambda cp: cp.wait())

        @pl.when(i == 0)
        def _():
            dg_ref[...] = dg

        @pl.when(i > 0)
        def _():
            dg_ref[...] += dg

    row = pl.BlockSpec((tr, d), lambda i: (i, 0))
    vec = pl.BlockSpec((1, d), lambda i: (0, 0))
    outs, ridden = _hosted_call(
        body, rider, name=name, grid=(steps,),
        in_specs=[pl.BlockSpec((tr, wd), lambda i: (i, 0)) for _, wd in SECTIONS]
        + [pl.BlockSpec((d, PW), lambda i: (0, 0)), row, vec, row],
        out_specs=[ANY, pl.BlockSpec((N_META, d), lambda i: (i // per_seq, 0)), vec],
        out_shape=[jax.ShapeDtypeStruct((bsz, s_len, d), F32), jax.ShapeDtypeStruct((bsz * N_META, d), F32),
                   jax.ShapeDtypeStruct((1, d), F32)],
        scratch_shapes=[pltpu.VMEM((2, tr, d), F32), pltpu.SemaphoreType.DMA((2,))],
        compiler_params=_cp(VMEM_BIG, ("arbitrary",)), args=(*secs, wp, h0, g1, dx1))
    return (*outs, ridden)


def _adamw(w, g, m, v, *, name, emit_grad=False, col_tile=None):
    lead = w.ndim - 2
    r, c = w.shape[-2:]
    tr = r if col_tile is not None else (_tile(r, 256, 8) if r > 256 else r)
    tc = col_tile if col_tile is not None else c
    c1 = 1.0 - ADAM_B1 ** ADAM_STEP
    c2 = 1.0 - ADAM_B2 ** ADAM_STEP
    n_out = 4 if emit_grad else 3

    def body(w_ref, g_ref, m_ref, v_ref, *out_refs):
        rd = (lambda ref: ref[0]) if lead else (lambda ref: ref[...])
        gv = g_ref[:, 0:tc]
        nm = ADAM_B1 * rd(m_ref) + (1.0 - ADAM_B1) * gv
        nv = ADAM_B2 * rd(v_ref) + (1.0 - ADAM_B2) * (gv * gv)
        res = [-ADAM_LR * ((nm / c1) / (jnp.sqrt(nv / c2) + ADAM_EPS) + ADAM_WD * rd(w_ref)), nm, nv, gv]
        for o_ref, val in zip(out_refs, res):
            if lead:
                o_ref[0] = val
            else:
                o_ref[...] = val

    if col_tile is None:
        blk = pl.BlockSpec((1,) * lead + (tr, c), lambda i: (0,) * lead + (i, 0))
        gblk = pl.BlockSpec((tr, g.shape[1]), lambda i: (i, 0))
        steps = r // tr
    else:
        blk = pl.BlockSpec((1,) * lead + (r, tc), lambda j: (0,) * lead + (0, j))
        gblk = pl.BlockSpec((r, tc), lambda j: (0, j))
        steps = c // tc
    sds = jax.ShapeDtypeStruct(w.shape, F32)
    return pl.pallas_call(
        body, name=name, grid=(steps,), in_specs=[blk, gblk, blk, blk], out_specs=[blk] * n_out,
        out_shape=[sds] * n_out, compiler_params=_cp(VMEM_BIG),
    )(w, g, m, v)


def _adamw_rows(w, g, m, v, *, name):
    r, _, c = w.shape
    tr = max(t for t in range(1, 129) if r % t == 0)
    c1 = 1.0 - ADAM_B1 ** ADAM_STEP
    c2 = 1.0 - ADAM_B2 ** ADAM_STEP

    def body(w_ref, g_ref, m_ref, v_ref, d_ref, nm_ref, nv_ref, go_ref):
        gv = g_ref[...]
        nm = ADAM_B1 * m_ref[...] + (1.0 - ADAM_B1) * gv
        nv = ADAM_B2 * v_ref[...] + (1.0 - ADAM_B2) * (gv * gv)
        d_ref[...] = -ADAM_LR * ((nm / c1) / (jnp.sqrt(nv / c2) + ADAM_EPS) + ADAM_WD * w_ref[...])
        nm_ref[...] = nm
        nv_ref[...] = nv
        go_ref[...] = gv

    blk = pl.BlockSpec((tr, 1, c), lambda i: (i, 0, 0))
    sds = jax.ShapeDtypeStruct(w.shape, F32)
    return pl.pallas_call(
        body, name=name, grid=(r // tr,), in_specs=[blk] * 4, out_specs=[blk] * 4, out_shape=[sds] * 4,
        compiler_params=_cp(VMEM_BIG),
    )(w, g, m, v)


def _pair_sum(where, g, theirs, *, name):
    lead, r, cols = g.shape
    half = r // 2
    tr = _tile(half, 256, 16)
    nh = half // tr

    def body(w_ref, a_ref, b_ref, o_ref):
        o_ref[...] = (a_ref[...] + b_ref[...]).astype(o_ref.dtype)

    blk = pl.BlockSpec((1, tr, cols), lambda s, i, w: (s, i, 0))
    return pl.pallas_call(
        body, name=name,
        grid_spec=pltpu.PrefetchScalarGridSpec(
            num_scalar_prefetch=1, grid=(lead, nh),
            in_specs=[pl.BlockSpec((1, tr, cols), lambda s, i, w: (s, w[0] * nh + i, 0)), blk], out_specs=blk),
        out_shape=jax.ShapeDtypeStruct((lead, half, cols), BF16), compiler_params=_cp(VMEM_BIG),
    )(where, g, theirs)


def _chip_sum(where, pair, q, *, name):
    _, half, cols = pair.shape
    tr = _tile(half, 256, 16)
    nh = half // tr

    def body(w_ref, own_ref, q1_ref, q2_ref, q3_ref, o_ref):
        f = lambda ref: ref[0].astype(F32)
        o_ref[...] = ((f(own_ref) + f(q1_ref)) + f(q2_ref)) + f(q3_ref)

    def peer(d):
        return pl.BlockSpec((1, tr, cols), lambda i, w: ((w[1] + d) % N_CHIPS, i, 0))

    return pl.pallas_call(
        body, name=name,
        grid_spec=pltpu.PrefetchScalarGridSpec(
            num_scalar_prefetch=1, grid=(nh,),
            in_specs=[peer(0), peer(1), peer(2), peer(3)],
            out_specs=pl.BlockSpec((tr, cols), lambda i, w: (w[0] * nh + i, 0))),
        out_shape=jax.ShapeDtypeStruct((2 * half, cols), F32), compiler_params=_cp(VMEM_BIG),
    )(where, pair, q, q, q)


VM = pl.BlockSpec(memory_space=pltpu.VMEM)


def _row_chunks(rows, n_split):
    size = rows // n_split
    assert size * n_split == rows and size % 16 == 0, (rows, n_split)
    return [(s, pl.ds(s * size, size)) for s in range(n_split)], size


D2D_SPLIT = 4
ICI_SPLIT = 2


def _sibling_halves(grads):
    n_arr = len(grads)

    def body(*refs):
        ins = refs[:n_arr]
        theirs = refs[n_arr:2 * n_arr]
        send_sems, recv_sems = refs[2 * n_arr:]
        x, y, c = _place()
        copies = []
        for k in range(n_arr):
            half = ins[k].shape[1] // 2
            chunks, size = _row_chunks(half, D2D_SPLIT)
            for s, dst_rows in chunks:
                give = pltpu.make_async_remote_copy(
                    src_ref=ins[k].at[:, pl.ds((1 - c) * half + s * size, size), :], dst_ref=theirs[k].at[:, dst_rows, :],
                    send_sem=send_sems.at[k, s], recv_sem=recv_sems.at[k, s], device_id=(x, y, 1 - c),
                    device_id_type=MESH)
                give.start()
                copies.append(give)
        for give in copies:
            give.wait()

    halves = [jax.ShapeDtypeStruct((g.shape[0], g.shape[1] // 2, g.shape[2]), F32) for g in grads]
    sem = pltpu.SemaphoreType.DMA((n_arr, D2D_SPLIT))
    return pl.pallas_call(
        body, name="sibling_halves", in_specs=[ANY] * n_arr, out_specs=[ANY] * n_arr, out_shape=halves,
        scratch_shapes=[sem, sem],
    )(*grads)


def _chip_exchange(parts):
    n_arr = len(parts)

    def body(*refs):
        ins = refs[:n_arr]
        outs = refs[n_arr:2 * n_arr]
        send_sems, recv_sems = refs[2 * n_arr:]
        x, y, c = _place()
        me = 2 * x + y
        sends = []
        for k in range(n_arr):
            chunks, _ = _row_chunks(ins[k].shape[1], ICI_SPLIT)
            for d, (px, py, pj) in enumerate(_other_chips(x, y)):
                for s, rows in chunks:
                    cp = pltpu.make_async_remote_copy(
                        src_ref=ins[k].at[pj, rows, :], dst_ref=outs[k].at[me, rows, :], send_sem=send_sems.at[k, d, s],
                        recv_sem=recv_sems.at[k, d, s], device_id=(px, py, c), device_id_type=MESH)
                    cp.start()
                    sends.append(cp)
        for k in range(n_arr):
            chunks, _ = _row_chunks(ins[k].shape[1], ICI_SPLIT)
            for d, (px, py, pj) in enumerate(_other_chips(x, y)):
                for s, rows in chunks:
                    pltpu.make_async_remote_copy(
                        src_ref=ins[k].at[pj, rows, :], dst_ref=outs[k].at[pj, rows, :], send_sem=send_sems.at[k, d, s],
                        recv_sem=recv_sems.at[k, d, s], device_id=(px, py, c), device_id_type=MESH).wait_recv()
        for cp in sends:
            cp.wait_send()

    sem = pltpu.SemaphoreType.DMA((n_arr, 3, ICI_SPLIT))
    return pl.pallas_call(
        body, name="chip_exchange", in_specs=[ANY] * n_arr, out_specs=[ANY] * n_arr,
        out_shape=[jax.ShapeDtypeStruct(p.shape, p.dtype) for p in parts],
        scratch_shapes=[sem, sem],
    )(*parts)


class _SiblingHalvesRider:
    def __init__(self, grads):
        self.inputs = list(grads)
        self.out_shapes = [jax.ShapeDtypeStruct((g.shape[0], g.shape[1] // 2, g.shape[2]), F32) for g in grads]
        self.aliases = {}
        self.sems = [pltpu.SemaphoreType.DMA((len(grads), D2D_SPLIT))] * 2

    def _copies(self, ins, outs, sems):
        x, y, c = _place()
        for k in range(len(ins)):
            half = ins[k].shape[1] // 2
            chunks, size = _row_chunks(half, D2D_SPLIT)
            for s, dst_rows in chunks:
                yield pltpu.make_async_remote_copy(
                    src_ref=ins[k].at[:, pl.ds((1 - c) * half + s * size, size), :], dst_ref=outs[k].at[:, dst_rows, :],
                    send_sem=sems[0].at[k, s], recv_sem=sems[1].at[k, s], device_id=(x, y, 1 - c), device_id_type=MESH)

    def first(self, ins, outs, sems):
        for cp in self._copies(ins, outs, sems):
            cp.start()

    def last(self, ins, outs, sems):
        for cp in self._copies(ins, outs, sems):
            cp.wait()


class _ChipExchangeRider:
    def __init__(self, parts):
        self.inputs = list(parts)
        self.out_shapes = [jax.ShapeDtypeStruct(p.shape, p.dtype) for p in parts]
        self.aliases = {}
        self.sems = [pltpu.SemaphoreType.DMA((len(parts), 3, ICI_SPLIT))] * 2

    def _copies(self, ins, outs, sems, receiving):
        x, y, c = _place()
        for k in range(len(ins)):
            chunks, _ = _row_chunks(ins[k].shape[1], ICI_SPLIT)
            for d, (px, py, pj) in enumerate(_other_chips(x, y)):
                for s, rows in chunks:
                    yield pltpu.make_async_remote_copy(
                        src_ref=ins[k].at[pj, rows, :], dst_ref=outs[k].at[pj if receiving else 2 * x + y, rows, :],
                        send_sem=sems[0].at[k, d, s], recv_sem=sems[1].at[k, d, s], device_id=(px, py, c),
                        device_id_type=MESH)

    def first(self, ins, outs, sems):
        for cp in self._copies(ins, outs, sems, False):
            cp.start()

    def last(self, ins, outs, sems):
        for cp in self._copies(ins, outs, sems, True):
            cp.wait_recv()
        for cp in self._copies(ins, outs, sems, False):
            cp.wait_send()


def _sibling_join(bufs):
    n_arr = len(bufs)

    def body(*refs):
        bufs_out = refs[n_arr:2 * n_arr]
        send_sems, recv_sems = refs[2 * n_arr:]
        x, y, c = _place()
        copies = []
        for k in range(n_arr):
            half = bufs_out[k].shape[0] // 2
            chunks, size = _row_chunks(half, D2D_SPLIT)
            for s, _ in chunks:
                rows = pl.ds(c * half + s * size, size)
                give = pltpu.make_async_remote_copy(
                    src_ref=bufs_out[k].at[rows, :], dst_ref=bufs_out[k].at[rows, :], send_sem=send_sems.at[k, s],
                    recv_sem=recv_sems.at[k, s], device_id=(x, y, 1 - c), device_id_type=MESH)
                give.start()
                copies.append((k, s, half, size, give))
        for k, s, half, size, give in copies:
            rows = pl.ds((1 - c) * half + s * size, size)
            pltpu.make_async_remote_copy(
                src_ref=bufs_out[k].at[rows, :], dst_ref=bufs_out[k].at[rows, :], send_sem=send_sems.at[k, s],
                recv_sem=recv_sems.at[k, s], device_id=(x, y, 1 - c), device_id_type=MESH).wait_recv()
            give.wait_send()

    sem = pltpu.SemaphoreType.DMA((n_arr, D2D_SPLIT))
    return pl.pallas_call(
        body, name="sibling_join", in_specs=[ANY] * n_arr, out_specs=[ANY] * n_arr,
        out_shape=[jax.ShapeDtypeStruct(b.shape, F32) for b in bufs],
        input_output_aliases={k: k for k in range(n_arr)},
        scratch_shapes=[sem, sem],
    )(*bufs)


PACK_ROWS = 48


def _small_allreduce(pack):
    masks = [(dx, dy, dc) for dx in (0, 1) for dy in (0, 1) for dc in (0, 1)][1:]

    def body(p_ref, o_ref, buf, send_sems, recv_sems):
        x, y, c = _place()
        me = 4 * x + 2 * y + c
        buf[me] = p_ref[...]
        sends = []
        for k, (dx, dy, dc) in enumerate(masks):
            peer = (1 - x if dx else x, 1 - y if dy else y, 1 - c if dc else c)
            cp = pltpu.make_async_remote_copy(
                src_ref=p_ref, dst_ref=buf.at[me], send_sem=send_sems.at[k], recv_sem=recv_sems.at[k],
                device_id=peer, device_id_type=MESH)
            cp.start()
            sends.append(cp)
        for k, (dx, dy, dc) in enumerate(masks):
            peer = (1 - x if dx else x, 1 - y if dy else y, 1 - c if dc else c)
            pj = 4 * peer[0] + 2 * peer[1] + peer[2]
            pltpu.make_async_remote_copy(
                src_ref=p_ref, dst_ref=buf.at[pj], send_sem=send_sems.at[k], recv_sem=recv_sems.at[k],
                device_id=peer, device_id_type=MESH).wait_recv()
        for cp in sends:
            cp.wait_send()
        tot = buf[0]
        for k in range(1, 8):
            tot = tot + buf[k]
        o_ref[...] = tot
        o_ref[0:N_META, :] = tot[0:N_META] + tot[N_META:2 * N_META]

    return pl.pallas_call(
        body, name="small_allreduce", in_specs=[VM], out_specs=VM,
        out_shape=jax.ShapeDtypeStruct((PACK_ROWS, D_MODEL), F32),
        scratch_shapes=[pltpu.VMEM((8, PACK_ROWS, D_MODEL), F32), pltpu.SemaphoreType.DMA((7,)),
                        pltpu.SemaphoreType.DMA((7,))],
    )(pack)


def _pad_lanes(vec, offset):
    k = vec.shape[1]
    return jnp.concatenate([jnp.zeros((1, offset), F32), vec, jnp.zeros((1, LANE - offset - k), F32)], axis=1)


def _local_step(x, tgt, meta, norm1_g, wp, conv_w, a_log, dt_bias, dn_norm_g, gla_w2, gla_b, gla_norm_g,
                w_out, norm2_g, w_up, w_down, final_norm_g, late_gather=None, where=None, early_gather=None):
    bsz, s_len, d = x.shape
    t_seq = s_len + CHUNK
    nc_seq = t_seq // CHUNK
    n = bsz * t_seq
    tr = _tile(t_seq, 832)
    tt = _tile(t_seq, 416)

    lead = jnp.concatenate([jnp.zeros((N_PAD, d), F32), meta], axis=0)
    alog_row = _pad_lanes(a_log, 4)
    dtb_row = _pad_lanes(dt_bias, 4)

    h0, h, got_early = _embed_norm(x, lead, norm1_g, tr=tr, name="embed_norm1", rider=early_gather)
    if early_gather is not None:
        wp = _padded_from_shards(got_early[0])
        conv_w = got_early[1].transpose(1, 0, 2).reshape(4, QKV_W)
        gla_w2 = got_early[2][:, :, 0:GQ_W // N_CHIPS].transpose(1, 0, 2).reshape(GLA_RANK, GQ_W)
    w2p = jnp.concatenate([gla_w2, jnp.zeros((LANE - GLA_RANK, GQ_W), F32)], axis=0)
    ride_up, ride_down, ride_out = late_gather if late_gather is not None else (None, None, None)
    res = _mm(h, wp, "nn", tm=tt, tn=PW, tk=d, out_dtypes=(BF16, F32), out_widths=(PW, PW - C_SA),
              epilogue=lambda acc: (acc, acc[:, C_SA:PW]), name="in_proj", rider=ride_up)
    ((projp, gates), got_up) = res if ride_up is not None else (res, None)
    qn, kn, v = _dnprep_fwd(projp, conv_w, bsz=bsz, t_seq=t_seq, tt=tt, name="dn_prep")
    u, w, qg, kd, pmat, tmat, gl, got_down = _dn_intra_fwd(qn, kn, v, gates, alog_row, dtb_row, nc_seq=nc_seq,
                                                           name="dn_intra", rider=ride_down)
    o_dn, vn, hist, got_out = _dn_scan_fwd(u, w, qg, kd, pmat, gl, bsz=bsz, nc_seq=nc_seq, name="dn_scan",
                                           rider=ride_out)
    oi, gqg, gkd, ggl = _gla_intra_fwd(projp, gates, w2p, gla_b, nc_seq=nc_seq, name="gla_intra")
    o_gla, ghist, _ = _gla_scan_fwd(oi, gqg, gkd, ggl, projp, bsz=bsz, nc_seq=nc_seq, name="gla_scan")
    if late_gather is not None:
        w_out = got_out[0].reshape(d, d)
        w_up = got_up[0].transpose(1, 0, 2).reshape(d, D_FF)
        w_down = got_down[0].reshape(D_FF, d)
    mix = _gnorm_fwd(o_dn, o_gla, projp, dn_norm_g, gla_norm_g, tr=tr, name="gated_norm")
    def residual_norm(acc, res, g):
        x1v = res + acc
        r = lax.rsqrt(jnp.mean(x1v * x1v, axis=-1, keepdims=True) + EPS)
        return x1v, x1v * r * g

    x1, h2 = _mm(mix, w_out, "nn", tm=tr, tn=d, tk=d, out_dtypes=(F32, BF16), extras=(h0,), vec_extras=(norm2_g,),
                 epilogue=residual_norm, name="out_proj_norm2")

    (act,) = _mm(h2, w_up, "nn", tm=tt, tn=D_FF, tk=d, out_dtypes=(BF16,),
                 epilogue=lambda acc: (jnp.square(jnp.maximum(acc, 0.0)),), name="mlp_up", n_chunk=1024)
    dx2, dx2b, d_final_g, loss_tile = _mlp_down_loss(act, w_down, x1, final_norm_g, tgt, t_seq=t_seq, tr=tt,
                                                     name="mlp_down_loss")

    (dup,) = _mm(dx2b, w_down, "nt", tm=tt, tn=D_FF, tk=d, out_dtypes=(BF16,), extras=(act,),
                 epilogue=lambda acc, a: (acc * (2.0 * jnp.sqrt(a.astype(F32))),), name="mlp_down_bwd", n_chunk=1024)
    tk2 = 2 * tr if n % (2 * tr) == 0 else tr
    (d_w_down,) = _mm(act, dx2b, "tn", tm=D_FF // 2, tn=d, tk=tk2, out_dtypes=(F32,), name="w_down_grad")
    (d_w_up_sm,) = _mm(h2, dup, "tn", tm=d, tn=D_FF // 2, tk=tk2, out_dtypes=(F32,), name="w_up_grad",
                       shard_cols=D_FF // N_CHIPS)
    mlp_sm = [d_w_up_sm, d_w_down.reshape(N_CHIPS, D_FF // N_CHIPS, d)]
    ride1 = _SiblingHalvesRider(mlp_sm) if where is not None else None
    dx1, dx1b, d_norm2_g, theirs = _mlp_up_bwd_norm(dup, w_up, x1, norm2_g, dx2, tr=tt, name="mlp_up_bwd_norm",
                                                    rider=ride1)
    ride2 = None
    if where is not None:
        mlp_pair = [_pair_sum(where, a, b, name=f"pair_sum_mlp{k}") for k, (a, b) in enumerate(zip(mlp_sm, theirs))]
        ride2 = _ChipExchangeRider(mlp_pair)

    (dmix,) = _mm(dx1b, w_out, "nt", tm=tr, tn=d, tk=d, out_dtypes=(BF16,), name="out_proj_bwd")
    (d_w_out,) = _mm(mix, dx1b, "tn", tm=d, tn=d, tk=tr, out_dtypes=(F32,), name="w_out_grad")
    do_dn, ddz, do_gla, dgr, d_dn_norm_g, d_gla_norm_g = _gnorm_bwd(
        dmix, o_dn, o_gla, projp, dn_norm_g, gla_norm_g, tr=tr, name="gated_norm_bwd")
    du, dw, dqg, dkd, dgl = _dn_scan_bwd(do_dn, w, qg, kd, vn, pmat, gl, hist, bsz=bsz, nc_seq=nc_seq,
                                          name="dn_scan_bwd")
    dqn, dkn, dv, dsa, d_alog, d_dtb, mlp_parts = _dn_intra_bwd(
        qn, kn, v, gates, alog_row, dtb_row, u, w, tmat, du, dw, dqg, dkd, do_dn, vn, dgl, nc_seq=nc_seq,
        name="dn_intra_bwd", rider=ride2)
    dz, d_conv_w = _dnprep_bwd_a(projp, conv_w, dqn, dkn, dv, bsz=bsz, t_seq=t_seq, tt=tt, name="dn_prep_bwd")
    dcin = _dnprep_bwd_b(dz, conv_w, bsz=bsz, t_seq=t_seq, tt=tt, name="conv_bwd")
    gdqg, gdkd, gdvi, gdgl = _gla_scan_bwd(do_gla, gqg, gkd, ggl, projp, ghist, bsz=bsz, nc_seq=nc_seq,
                                            name="gla_scan_bwd")
    dgqk, dgv, dsb, d_w2p, d_gla_b = _gla_intra_bwd(projp, gates, w2p, gla_b, do_gla, gdqg, gdkd, gdvi, gdgl,
                                                    nc_seq=nc_seq, name="gla_intra_bwd")

    secs = (dcin, ddz, dgqk, dgv, dgr, dsa, dsb)
    g_lo = _grad_tn(h, secs[0:2], tk=tk2, name="w_in_grad_lo")
    g_hi = _grad_tn(h, secs[2:7], tk=tk2, name="w_in_grad_hi")
    ride3 = None
    if where is not None:
        late_sm = [_shards_from_padded(g_lo, g_hi), d_w_out.reshape(N_CHIPS, d // N_CHIPS, d)]
        late_theirs = _exchange_now(_SiblingHalvesRider(late_sm), name="sibling_halves")
        late_pair = [_pair_sum(where, a, b, name=f"pair_sum_{k}") for k, (a, b) in enumerate(zip(late_sm, late_theirs))]
        ride3 = _ChipExchangeRider(late_pair)
    grad_x, d_meta_rows, d_norm1_g, late_parts = _inproj_bwd(secs, wp, h0, norm1_g, dx1, bsz=bsz, t_seq=t_seq, tr=tt,
                                                             name="in_proj_bwd", rider=ride3)

    grads = dict(w_in_lo=g_lo, w_in_hi=g_hi, w_out=d_w_out, w_up_shards=d_w_up_sm, w_down=d_w_down, meta_rows=d_meta_rows,
                 norm1_g=d_norm1_g, conv_w=d_conv_w, a_log_tile=d_alog, dt_bias_tile=d_dtb, dn_norm_g=d_dn_norm_g,
                 gla_w2=d_w2p[0:GLA_RANK], gla_b=d_gla_b, gla_norm_g=d_gla_norm_g, norm2_g=d_norm2_g,
                 final_norm_g=d_final_g, loss_tile=loss_tile)
    if where is not None:
        grads["exchanged"] = (late_pair + mlp_pair, list(late_parts) + list(mlp_parts))
    return grad_x, grads


SHARD_W = IN_WIDTH // N_CHIPS
PADDED_ORDER = ((0, 2048), (2056, 3592), (2048, 2056), LANE - 8, (3592, 3608), LANE - GLA_RANK)


def _pad_layout(w_full):
    pieces = [jnp.zeros((w_full.shape[0], seg), w_full.dtype) if isinstance(seg, int) else w_full[:, seg[0]:seg[1]]
              for seg in PADDED_ORDER]
    return jnp.concatenate(pieces, axis=1)


def _padded_from_shards(stack):
    pieces = []
    for seg in PADDED_ORDER:
        if isinstance(seg, int):
            pieces.append(jnp.zeros((stack.shape[1], seg), stack.dtype))
            continue
        for j in range(N_CHIPS):
            lo, hi = max(seg[0], j * SHARD_W), min(seg[1], (j + 1) * SHARD_W)
            if lo < hi:
                pieces.append(stack[j, :, lo - j * SHARD_W:hi - j * SHARD_W])
    return jnp.concatenate(pieces, axis=1)


def _shards_from_padded(g_lo, g_hi):
    split = g_lo.shape[1]
    starts, pos = [], 0
    for seg in PADDED_ORDER:
        width = seg if isinstance(seg, int) else seg[1] - seg[0]
        if not isinstance(seg, int):
            starts.append((seg[0], seg[1], pos))
        pos += width
    shards = []
    for j in range(N_CHIPS):
        pieces = []
        for a, b, p0 in sorted(starts):
            lo, hi = max(a, j * SHARD_W), min(b, (j + 1) * SHARD_W)
            if lo < hi:
                src, off = (g_lo, 0) if p0 < split else (g_hi, split)
                pieces.append(src[:, p0 + lo - a - off:p0 + hi - a - off])
        pieces.append(jnp.zeros((g_lo.shape[0], D_MODEL - SHARD_W), g_lo.dtype))
        shards.append(jnp.concatenate(pieces, axis=1))
    return jnp.stack(shards)


def _pack_small(g, bsz):
    assert bsz * N_META == 32
    row = jnp.concatenate([g["a_log_tile"], g["dt_bias_tile"], g["dn_norm_g"], g["gla_norm_g"], g["gla_b"],
                           g["loss_tile"], jnp.zeros((1, LANE), F32)], axis=1)
    return jnp.concatenate([g["meta_rows"], g["norm1_g"], g["conv_w"].reshape(6, D_MODEL), row,
                            g["gla_w2"].reshape(4, D_MODEL), g["norm2_g"], g["final_norm_g"],
                            jnp.zeros((2, D_MODEL), F32)], axis=0)


def kernel(x, meta_tokens, norm1_g, w_in, conv_w, a_log, dt_bias, dn_norm_g, gla_w2, gla_b, gla_norm_g, w_out, norm2_g, w_up, w_down, final_norm_g, loss_target, m_meta_tokens, m_norm1_g, m_w_in, m_conv_w, m_a_log, m_dt_bias, m_dn_norm_g, m_gla_w2, m_gla_b, m_gla_norm_g, m_w_out, m_norm2_g, m_w_up, m_w_down, m_final_norm_g, v_meta_tokens, v_norm1_g, v_w_in, v_conv_w, v_a_log, v_dt_bias, v_dn_norm_g, v_gla_w2, v_gla_b, v_gla_norm_g, v_w_out, v_norm2_g, v_w_up, v_w_down, v_final_norm_g):
    bsz = x.shape[0]
    chip = 2 * lax.axis_index("x") + lax.axis_index("y")

    lane_pad = lambda a, wd: jnp.pad(a, ((0, 0), (0, wd - a.shape[1])))
    where = jnp.stack([lax.axis_index("c"), chip]).astype(jnp.int32)
    slot = lambda a, dt, nm: _to_slot(where, a, dt, name="slot_" + nm)
    (g_meta,) = _exchange_now(_GatherRider([slot(meta_tokens, F32, "meta")], [False]), name="gather_meta")
    early = _GatherRider([slot(lane_pad(w_in[0], D_MODEL), BF16, "w_in"), slot(conv_w[0], F32, "conv"),
                          slot(lane_pad(gla_w2[0], LANE), F32, "gla_w2")], [True, False, False])
    late = (_GatherRider([slot(w_up[0], BF16, "w_up")], [True]), _GatherRider([slot(w_down[0], BF16, "w_down")], [True]),
            _GatherRider([slot(w_out[0], BF16, "w_out")], [True]))
    meta_f = g_meta.transpose(1, 0, 2).reshape(N_META, D_MODEL)

    grad_x, g = _local_step(x, loss_target, meta_f, norm1_g, None, None, a_log, dt_bias, dn_norm_g, None, gla_b,
                            gla_norm_g, None, norm2_g, None, None, final_norm_g.reshape(1, D_MODEL), late_gather=late,
                            where=where, early_gather=early)

    pair, parts = g["exchanged"]
    halves = [_chip_sum(where, p, q, name=f"chip_sum_{k}") for k, (p, q) in enumerate(zip(pair, parts))]
    gw_in, gw_out, gw_up, gw_down = _sibling_join(halves)

    red = _small_allreduce(_pack_small(g, bsz))
    g_meta_full = red[0:N_META]
    g_norm1 = red[32:33]
    g_conv_full = red[33:39].reshape(4, QKV_W)
    srow = red[39:40]
    g_alog, g_dtb = srow[:, 4:8], srow[:, LANE + 4:LANE + 8]
    g_dn_norm, g_gla_norm = srow[:, 2 * LANE:3 * LANE], srow[:, 3 * LANE:4 * LANE]
    g_gla_b = srow[:, 4 * LANE:6 * LANE]
    loss = srow[0, 6 * LANE]
    g_w2_full = red[40:44].reshape(GLA_RANK, GQ_W)
    g_norm2 = red[44:45]
    g_final = red[45:46]
    g_meta_sh = lax.dynamic_slice_in_dim(g_meta_full, chip * (D_MODEL // N_CHIPS), D_MODEL // N_CHIPS, axis=1)
    g_conv_sh = lax.dynamic_slice_in_dim(g_conv_full, chip * (QKV_W // N_CHIPS), QKV_W // N_CHIPS, axis=1)
    g_w2_sh = lax.dynamic_slice_in_dim(g_w2_full, chip * (GQ_W // N_CHIPS), GQ_W // N_CHIPS, axis=1)

    names = ["meta_tokens", "norm1_g", "w_in", "conv_w", "a_log", "dt_bias", "dn_norm_g", "gla_w2", "gla_b",
             "gla_norm_g", "w_out", "norm2_g", "w_up", "w_down", "final_norm_g"]
    weights = dict(meta_tokens=meta_tokens, norm1_g=norm1_g, w_in=w_in, conv_w=conv_w, a_log=a_log, dt_bias=dt_bias,
                   dn_norm_g=dn_norm_g, gla_w2=gla_w2, gla_b=gla_b, gla_norm_g=gla_norm_g, w_out=w_out,
                   norm2_g=norm2_g, w_up=w_up, w_down=w_down, final_norm_g=final_norm_g)
    ms = dict(meta_tokens=m_meta_tokens, norm1_g=m_norm1_g, w_in=m_w_in, conv_w=m_conv_w, a_log=m_a_log,
              dt_bias=m_dt_bias, dn_norm_g=m_dn_norm_g, gla_w2=m_gla_w2, gla_b=m_gla_b, gla_norm_g=m_gla_norm_g,
              w_out=m_w_out, norm2_g=m_norm2_g, w_up=m_w_up, w_down=m_w_down, final_norm_g=m_final_norm_g)
    vs = dict(meta_tokens=v_meta_tokens, norm1_g=v_norm1_g, w_in=v_w_in, conv_w=v_conv_w, a_log=v_a_log,
              dt_bias=v_dt_bias, dn_norm_g=v_dn_norm_g, gla_w2=v_gla_w2, gla_b=v_gla_b, gla_norm_g=v_gla_norm_g,
              w_out=v_w_out, norm2_g=v_norm2_g, w_up=v_w_up, w_down=v_w_down, final_norm_g=v_final_norm_g)
    grads2d = dict(meta_tokens=g_meta_sh, norm1_g=g_norm1, w_in=gw_in, conv_w=g_conv_sh, a_log=g_alog, dt_bias=g_dtb,
                   dn_norm_g=g_dn_norm, gla_w2=g_w2_sh, gla_b=g_gla_b, gla_norm_g=g_gla_norm, w_out=gw_out,
                   norm2_g=g_norm2, w_up=gw_up, w_down=gw_down, final_norm_g=g_final)
    out_g, out_d, out_m, out_v = [], [], [], []
    for nm in names:
        shape = weights[nm].shape
        g2 = grads2d[nm]
        if nm == "w_in":
            tview = lambda a: jnp.transpose(a, (2, 0, 1))
            res = _adamw_rows(tview(weights[nm]), g2[:, 0:SHARD_W].T.reshape(SHARD_W, 1, D_MODEL), tview(ms[nm]),
                              tview(vs[nm]), name=f"adamw_{nm}")
            res = [jnp.transpose(a, (1, 2, 0)) for a in res]
            gout = res[3]
        elif len(shape) == 3:
            res = _adamw(weights[nm], g2, ms[nm], vs[nm], name=f"adamw_{nm}")
            gout = g2.reshape(shape)
        else:
            as2d = lambda a: a.reshape(g2.shape)
            res = _adamw(as2d(weights[nm]), g2, as2d(ms[nm]), as2d(vs[nm]), name=f"adamw_{nm}")
            gout = g2.reshape(shape)
        out_g.append(gout)
        out_d.append(res[0].reshape(shape))
        out_m.append(res[1].reshape(shape))
        out_v.append(res[2].reshape(shape))
    return (loss, grad_x, *out_g, *out_d, *out_m, *out_v)
```

```python
import functools

import jax
import jax.numpy as jnp
import numpy as np
from jax import lax
from jax.experimental import pallas as pl
from jax.experimental.pallas import tpu as pltpu

F32 = jnp.float32
BF16 = jnp.bfloat16
HI = lax.Precision.HIGHEST
MESH = pl.DeviceIdType.MESH

D_MODEL = 1024
N_META = 16
CHUNK = 64
N_PAD = CHUNK - N_META
NH = 4
DN_D = 128
GLA_DK = 64
GLA_DV = 128
GLA_RANK = 16
D_FF = 4 * D_MODEL
EPS = 1e-6
IN_WIDTH = 3608
C_QKV, C_DZ, C_GQK, C_GV, C_GR, C_SA, C_SB, PW = 0, 1536, 2048, 2560, 3072, 3584, 3712, 3840
LANE = 128
N_CHIPS = 4

ADAM_LR, ADAM_B1, ADAM_B2, ADAM_EPS, ADAM_WD, ADAM_STEP = 0.001, 0.9, 0.999, 1e-08, 0.01, 10

VMEM_BIG = 56 * 1024 * 1024


def _cp(vmem=None, sem=None):
    kw = {}
    if vmem is not None:
        kw["vmem_limit_bytes"] = vmem
    if sem is not None:
        kw["dimension_semantics"] = sem
    return pltpu.CompilerParams(**kw)


def _tile(n, target, mult=16):
    best = None
    for t in range(mult, min(n, target) + 1, mult):
        if n % t == 0:
            best = t
    assert best is not None, (n, target)
    return best


def _dot(a, b, dims, prec=None):
    return lax.dot_general(a, b, (dims, ((), ())), preferred_element_type=F32, precision=prec)


def _nn(a, b):
    return _dot(a.astype(BF16), b.astype(BF16), ((1,), (0,)))


def _nt(a, b):
    return _dot(a.astype(BF16), b.astype(BF16), ((1,), (1,)))


def _tn(a, b):
    return _dot(a.astype(BF16), b.astype(BF16), ((0,), (0,)))


def _split(x):
    hi = x.astype(BF16)
    return hi, (x - hi.astype(F32)).astype(BF16)


def _tri_sum(tri, x):
    t = tri.astype(BF16)
    hi = x.astype(BF16)
    r1 = x - hi.astype(F32)
    mid = r1.astype(BF16)
    lo = (r1 - mid.astype(F32)).astype(BF16)
    nn = ((1,), (0,))
    return _dot(t, hi, nn) + _dot(t, mid, nn) + _dot(t, lo, nn)


def _sigmoid(x):
    return 0.5 * jnp.tanh(0.5 * x) + 0.5


def _softplus(x):
    return jnp.maximum(x, 0.0) + jnp.log(1.0 + jnp.exp(-jnp.abs(x)))


def _logsigmoid(x):
    return -_softplus(-x)


def _iota2(shape, dim):
    return lax.broadcasted_iota(jnp.int32, shape, dim)


def _mm(a, b, mode, *, tm, tn, tk, out_dtypes, extras=(), epilogue=None, name, vmem=VMEM_BIG, rider=None,
        out_widths=None, n_chunk=None, vec_extras=(), shard_cols=None):
    if mode == "tn":
        K, M = a.shape
    else:
        M, K = a.shape
    N = b.shape[0] if mode == "nt" else b.shape[1]
    assert M % tm == 0 and N % tn == 0 and K % tk == 0, (name, M, N, K, tm, tn, tk)
    nk = K // tk
    n_ex, n_out, n_vec = len(extras), len(out_dtypes), len(vec_extras)
    if mode == "tn":
        a_spec = pl.BlockSpec((tk, tm), lambda i, j, k: (k, i))
    else:
        a_spec = pl.BlockSpec((tm, tk), lambda i, j, k: (i, k))
    if mode == "nt":
        b_spec = pl.BlockSpec((tn, tk), lambda i, j, k: (j, k))
    else:
        b_spec = pl.BlockSpec((tk, tn), lambda i, j, k: (k, j))
    mn_spec = pl.BlockSpec((tm, tn), lambda i, j, k: (i, j))
    if out_widths is None:
        o_specs = [mn_spec] * n_out
        o_shapes = [jax.ShapeDtypeStruct((M, N), dt) for dt in out_dtypes]
    else:
        assert tn == N
        o_specs = [pl.BlockSpec((tm, wd), lambda i, j, k: (i, 0)) for wd in out_widths]
        o_shapes = [jax.ShapeDtypeStruct((M, wd), dt) for wd, dt in zip(out_widths, out_dtypes)]
    if shard_cols is not None:
        o_specs = [pl.BlockSpec((tn // shard_cols, tm, shard_cols), lambda i, j, k: (j, i, 0))]
        o_shapes = [jax.ShapeDtypeStruct((N // shard_cols, M, shard_cols), F32)]
    dims = {"nn": ((1,), (0,)), "nt": ((1,), (1,)), "tn": ((0,), (0,))}[mode]

    single = nk == 1
    direct = (not single) and epilogue is None and n_out == 1 and out_dtypes[0] == F32

    def body(*refs):
        a_ref, b_ref = refs[0], refs[1]
        ex_refs = refs[2:2 + n_ex]
        vec_refs = refs[2 + n_ex:2 + n_ex + n_vec]
        out_refs = refs[2 + n_ex + n_vec:2 + n_ex + n_vec + n_out]
        if n_chunk is not None:
            assert single and out_widths is None and mode != "tn" and tn % n_chunk == 0
            av = a_ref[...].astype(BF16)
            for j in range(tn // n_chunk):
                cols = slice(j * n_chunk, (j + 1) * n_chunk)
                bv = b_ref[cols, :] if mode == "nt" else b_ref[:, cols]
                acc = _dot(av, bv.astype(BF16), dims)
                res = (acc,) if epilogue is None else epilogue(acc, *[e[:, cols] for e in ex_refs])
                for o_ref, r in zip(out_refs, res):
                    o_ref[:, cols] = r.astype(o_ref.dtype)
            return
        part = _dot(a_ref[...].astype(BF16), b_ref[...].astype(BF16), dims)

        def finish(acc):
            res = (acc,) if epilogue is None else epilogue(acc, *[e[...] for e in ex_refs], *[v[...] for v in vec_refs])
            for o_ref, r in zip(out_refs, res):
                o_ref[...] = r.astype(o_ref.dtype)

        if single:
            if shard_cols is not None:
                for sh in range(tn // shard_cols):
                    out_refs[0][sh] = part[:, sh * shard_cols:(sh + 1) * shard_cols]
            else:
                finish(part)
            return
        acc_ref = out_refs[0] if direct else refs[2 + n_ex + n_vec + n_out]
        k = pl.program_id(2)
        if shard_cols is not None:
            assert direct
            for sh in range(tn // shard_cols):
                piece = part[:, sh * shard_cols:(sh + 1) * shard_cols]

                @pl.when(k == 0)
                def _():
                    acc_ref[sh] = piece

                @pl.when(k > 0)
                def _():
                    acc_ref[sh] += piece
            return

        @pl.when(k == 0)
        def _():
            acc_ref[...] = part

        @pl.when(k > 0)
        def _():
            acc_ref[...] += part

        if not direct:
            @pl.when(k == nk - 1)
            def _():
                finish(acc_ref[...])

    outs, ridden = _hosted_call(
        body, rider, name=name, grid=(M // tm, N // tn, nk),
        in_specs=[a_spec, b_spec] + [mn_spec] * n_ex + [pl.BlockSpec((1, tn), lambda i, j, k: (0, j))] * n_vec,
        out_specs=o_specs, out_shape=o_shapes,
        scratch_shapes=[] if (single or direct) else [pltpu.VMEM((tm, tn), F32)],
        compiler_params=_cp(vmem, ("parallel", "parallel", "arbitrary")), args=(a, b, *extras, *vec_extras))
    return tuple(outs) if rider is None else (tuple(outs), ridden)


def _grad_tn(a, secs, *, tk, name):
    kk, m = a.shape
    widths = [s.shape[1] for s in secs]
    total = sum(widths)
    nk = kk // tk

    def body(*refs):
        a_ref, sec_refs, o_ref = refs[0], refs[1:-1], refs[-1]
        cat = sec_refs[0][...] if len(sec_refs) == 1 else jnp.concatenate([s[...] for s in sec_refs], axis=1)
        part = _dot(a_ref[...].astype(BF16), cat.astype(BF16), ((0,), (0,)))
        k = pl.program_id(0)

        @pl.when(k == 0)
        def _():
            o_ref[...] = part

        @pl.when(k > 0)
        def _():
            o_ref[...] += part

    return pl.pallas_call(
        body, name=name, grid=(nk,),
        in_specs=[pl.BlockSpec((tk, m), lambda k: (k, 0))] + [pl.BlockSpec((tk, w), lambda k: (k, 0)) for w in widths],
        out_specs=pl.BlockSpec((m, total), lambda k: (0, 0)),
        out_shape=jax.ShapeDtypeStruct((m, total), F32),
        compiler_params=_cp(VMEM_BIG, ("arbitrary",)),
    )(a, *secs)


class _ShiftedRows:
    def __init__(self, src, buf, sems, *, per_seq, tt, steps):
        self.src, self.buf, self.sems = src, buf, sems
        self.per_seq, self.tt, self.steps = per_seq, tt, steps

    def _do(self, step, slot, act):
        b, j = step // self.per_seq, step % self.per_seq
        tt = self.tt

        @pl.when(j == 0)
        def _():
            act(pltpu.make_async_copy(self.src.at[b, pl.ds(0, tt - CHUNK), :],
                                      self.buf.at[slot, pl.ds(CHUNK, tt - CHUNK), :], self.sems.at[slot]))

        if self.per_seq > 1:
            @pl.when(j > 0)
            def _():
                act(pltpu.make_async_copy(self.src.at[b, pl.ds(j * tt - CHUNK, tt), :], self.buf.at[slot],
                                          self.sems.at[slot]))

    def tile(self, i):
        slot = i % 2

        @pl.when(i == 0)
        def _():
            self._do(i, slot, lambda cp: cp.start())

        self._do(i, slot, lambda cp: cp.wait())

        @pl.when(i + 1 < self.steps)
        def _():
            self._do(i + 1, 1 - slot, lambda cp: cp.start())

        return slot


def _embed_norm(x, lead, g, *, tr, name, rider=None):
    bsz, s_len, d = x.shape
    t_seq = s_len + CHUNK
    per_seq = t_seq // tr
    steps = bsz * per_seq
    n = bsz * t_seq

    def body(x_hbm, lead_ref, g_ref, h0_ref, h_ref, buf, sems):
        i = pl.program_id(0)
        slot = _ShiftedRows(x_hbm, buf, sems, per_seq=per_seq, tt=tr, steps=steps).tile(i)

        @pl.when(i % per_seq == 0)
        def _():
            buf[slot, 0:CHUNK, :] = lead_ref[...]

        xv = buf[slot]
        h0_ref[...] = xv
        r = lax.rsqrt(jnp.mean(xv * xv, axis=-1, keepdims=True) + EPS)
        h_ref[...] = (xv * r * g_ref[...]).astype(h_ref.dtype)

    row = pl.BlockSpec((tr, d), lambda i: (i, 0))
    outs, ridden = _hosted_call(
        body, rider, name=name, grid=(steps,),
        in_specs=[ANY, pl.BlockSpec((CHUNK, d), lambda i: (0, 0)), pl.BlockSpec((1, d), lambda i: (0, 0))],
        out_specs=[row, row],
        out_shape=[jax.ShapeDtypeStruct((n, d), F32), jax.ShapeDtypeStruct((n, d), BF16)],
        scratch_shapes=[pltpu.VMEM((2, tr, d), F32), pltpu.SemaphoreType.DMA((2,))],
        compiler_params=_cp(VMEM_BIG, ("arbitrary",)), args=(x, lead, g))
    return (*outs, ridden)


def _rms_fwd(x, g, *, tr, name):
    n, d = x.shape

    def body(x_ref, g_ref, o_ref):
        xv = x_ref[...]
        r = lax.rsqrt(jnp.mean(xv * xv, axis=-1, keepdims=True) + EPS)
        o_ref[...] = (xv * r * g_ref[...]).astype(o_ref.dtype)

    return pl.pallas_call(
        body, name=name, grid=(n // tr,),
        in_specs=[pl.BlockSpec((tr, d), lambda i: (i, 0)), pl.BlockSpec((1, d), lambda i: (0, 0))],
        out_specs=pl.BlockSpec((tr, d), lambda i: (i, 0)),
        out_shape=jax.ShapeDtypeStruct((n, d), BF16),
        compiler_params=_cp(VMEM_BIG),
    )(x, g)


def _rms_bwd_math(xv, g, dy):
    r = lax.rsqrt(jnp.mean(xv * xv, axis=-1, keepdims=True) + EPS)
    xh = xv * r
    gdy = dy * g
    dx = r * (gdy - xh * jnp.mean(xh * gdy, axis=-1, keepdims=True))
    return dx, jnp.sum(dy * xh, axis=0, keepdims=True)


def _mlp_up_bwd_norm(dup, w_up, x, g, res, *, tr, name, rider=None):
    n, d = x.shape
    ff = dup.shape[1]

    def body(dup_ref, w_ref, x_ref, g_ref, res_ref, o_ref, ob_ref, dg_ref):
        dh = _nt(dup_ref[...], w_ref[...])
        dx, dg = _rms_bwd_math(x_ref[...], g_ref[...], dh)
        tot = res_ref[...] + dx
        o_ref[...] = tot
        ob_ref[...] = tot.astype(BF16)

        @pl.when(pl.program_id(0) == 0)
        def _():
            dg_ref[...] = dg

        @pl.when(pl.program_id(0) > 0)
        def _():
            dg_ref[...] += dg

    row = pl.BlockSpec((tr, d), lambda i: (i, 0))
    vec = pl.BlockSpec((1, d), lambda i: (0, 0))
    outs, ridden = _hosted_call(
        body, rider, name=name, grid=(n // tr,),
        in_specs=[pl.BlockSpec((tr, ff), lambda i: (i, 0)), pl.BlockSpec((d, ff), lambda i: (0, 0)), row, vec, row],
        out_specs=[row, row, vec],
        out_shape=[jax.ShapeDtypeStruct((n, d), F32), jax.ShapeDtypeStruct((n, d), BF16),
                   jax.ShapeDtypeStruct((1, d), F32)],
        scratch_shapes=[], compiler_params=_cp(VMEM_BIG, ("arbitrary",)), args=(dup, w_up, x, g, res))
    return (*outs, ridden)


def _mlp_down_loss(act, w_down, x1, gf, tgt, *, t_seq, tr, name):
    n, d = x1.shape
    ff = act.shape[1]
    per_seq = t_seq // tr
    steps = n // tr

    def body(a_ref, w_ref, x_ref, g_ref, t_hbm, dx_ref, dxb_ref, dg_ref, loss_ref, tbuf, tsems):
        i = pl.program_id(0)
        slot = _ShiftedRows(t_hbm, tbuf, tsems, per_seq=per_seq, tt=tr, steps=steps).tile(i)

        @pl.when(i % per_seq == 0)
        def _():
            tbuf[slot, 0:CHUNK, :] = jnp.zeros((CHUNK, d), F32)

        t_ref = tbuf.at[slot]
        xv = x_ref[...] + _nn(a_ref[...], w_ref[...])
        g = g_ref[...]
        r = lax.rsqrt(jnp.mean(xv * xv, axis=-1, keepdims=True) + EPS)
        xh = xv * r
        pos = (i % per_seq) * tr + _iota2((tr, 1), 0)
        real = pos >= CHUNK
        err = jnp.where(real, xh * g - t_ref[...], 0.0)
        dy = err * (1.0 / d)
        gdy = dy * g
        dx = r * (gdy - xh * jnp.mean(xh * gdy, axis=-1, keepdims=True))
        dx_ref[...] = dx
        dxb_ref[...] = dx.astype(BF16)
        dg = jnp.sum(dy * xh, axis=0, keepdims=True)
        ls = 0.5 * jnp.sum(jnp.mean(err * err, axis=-1, keepdims=True), axis=0, keepdims=True)
        ls = jnp.where(_iota2((1, LANE), 1) == 0, ls, 0.0)

        @pl.when(i == 0)
        def _():
            dg_ref[...] = dg
            loss_ref[...] = ls

        @pl.when(i > 0)
        def _():
            dg_ref[...] += dg
            loss_ref[...] += ls

    row = pl.BlockSpec((tr, d), lambda i: (i, 0))
    vec = pl.BlockSpec((1, d), lambda i: (0, 0))
    one = pl.BlockSpec((1, LANE), lambda i: (0, 0))
    return pl.pallas_call(
        body, name=name, grid=(n // tr,),
        in_specs=[pl.BlockSpec((tr, ff), lambda i: (i, 0)), pl.BlockSpec((ff, d), lambda i: (0, 0)), row, vec, ANY],
        out_specs=[row, row, vec, one],
        out_shape=[jax.ShapeDtypeStruct((n, d), F32), jax.ShapeDtypeStruct((n, d), BF16),
                   jax.ShapeDtypeStruct((1, d), F32), jax.ShapeDtypeStruct((1, LANE), F32)],
        scratch_shapes=[pltpu.VMEM((2, tr, d), F32), pltpu.SemaphoreType.DMA((2,))],
        compiler_params=_cp(VMEM_BIG, ("arbitrary",)),
    )(act, w_down, x1, gf, tgt)


def _gnorm_fwd(o_dn, o_gla, projp, g_dn, g_gla, *, tr, name):
    n = o_dn.shape[0]
    w = NH * DN_D

    def body(odn_ref, ogl_ref, z_ref, r_ref, gdn_ref, ggl_ref, mix_ref):
        for grp, (o_ref, gate_ref, gain_ref) in enumerate(((odn_ref, z_ref, gdn_ref), (ogl_ref, r_ref, ggl_ref))):
            gain = gain_ref[...]
            for h in range(NH):
                sl = slice(h * DN_D, (h + 1) * DN_D)
                o = o_ref[:, sl].astype(F32)
                z = gate_ref[:, sl].astype(F32)
                r = lax.rsqrt(jnp.mean(o * o, axis=-1, keepdims=True) + EPS)
                y = (o * r * gain) * (z * _sigmoid(z))
                mix_ref[:, grp * w + h * DN_D: grp * w + (h + 1) * DN_D] = y.astype(mix_ref.dtype)

    row = pl.BlockSpec((tr, w), lambda i: (i, 0))
    vec = pl.BlockSpec((1, DN_D), lambda i: (0, 0))
    return pl.pallas_call(
        body, name=name, grid=(n // tr,),
        in_specs=[row, row, pl.BlockSpec((tr, w), lambda i: (i, C_DZ // w)),
                  pl.BlockSpec((tr, w), lambda i: (i, C_GR // w)), vec, vec],
        out_specs=pl.BlockSpec((tr, 2 * w), lambda i: (i, 0)),
        out_shape=jax.ShapeDtypeStruct((n, 2 * w), BF16),
        compiler_params=_cp(VMEM_BIG),
    )(o_dn, o_gla, projp, projp, g_dn, g_gla)


def _gnorm_bwd(dmix, o_dn, o_gla, projp, g_dn, g_gla, *, tr, name):
    n = o_dn.shape[0]
    w = NH * DN_D

    def body(dm_ref, odn_ref, ogl_ref, z_ref, r_ref, gdn_ref, ggl_ref,
             dodn_ref, ddz_ref, dogl_ref, dgr_ref, dgdn_ref, dggl_ref):
        first = pl.program_id(0) == 0
        groups = ((odn_ref, z_ref, gdn_ref, dodn_ref, ddz_ref, dgdn_ref),
                  (ogl_ref, r_ref, ggl_ref, dogl_ref, dgr_ref, dggl_ref))
        for grp, (o_ref, gate_ref, gain_ref, do_ref, dgate_ref, dgain_ref) in enumerate(groups):
            gain = gain_ref[...]
            dgain = jnp.zeros((1, DN_D), F32)
            for h in range(NH):
                sl = slice(h * DN_D, (h + 1) * DN_D)
                o = o_ref[:, sl].astype(F32)
                z = gate_ref[:, sl].astype(F32)
                dm = dm_ref[:, grp * w + h * DN_D: grp * w + (h + 1) * DN_D].astype(F32)
                r = lax.rsqrt(jnp.mean(o * o, axis=-1, keepdims=True) + EPS)
                oh = o * r
                s = _sigmoid(z)
                dn = dm * (z * s)
                dgate_ref[:, sl] = (dm * (oh * gain) * (s * (1.0 + z * (1.0 - s)))).astype(dgate_ref.dtype)
                gdn = dn * gain
                do_ref[:, sl] = (r * (gdn - oh * jnp.mean(oh * gdn, axis=-1, keepdims=True))).astype(do_ref.dtype)
                dgain = dgain + jnp.sum(dn * oh, axis=0, keepdims=True)

            @pl.when(first)
            def _():
                dgain_ref[...] = dgain

            @pl.when(jnp.logical_not(first))
            def _():
                dgain_ref[...] += dgain

    row = pl.BlockSpec((tr, w), lambda i: (i, 0))
    vec = pl.BlockSpec((1, DN_D), lambda i: (0, 0))
    big = jax.ShapeDtypeStruct((n, w), F32)
    gate = jax.ShapeDtypeStruct((n, w), BF16)
    small = jax.ShapeDtypeStruct((1, DN_D), F32)
    return pl.pallas_call(
        body, name=name, grid=(n // tr,),
        in_specs=[pl.BlockSpec((tr, 2 * w), lambda i: (i, 0)), row, row,
                  pl.BlockSpec((tr, w), lambda i: (i, C_DZ // w)), pl.BlockSpec((tr, w), lambda i: (i, C_GR // w)), vec, vec],
        out_specs=[row, row, row, row, vec, vec],
        out_shape=[gate, gate, gate, gate, small, small],
        compiler_params=_cp(VMEM_BIG),
    )(dmix, o_dn, o_gla, projp, projp, g_dn, g_gla)


QKV_W = 3 * NH * DN_D
HALO = 8


def _conv_z(xs_ref, cw_ref, tt):
    z = cw_ref[0:1, :] * xs_ref[pl.ds(HALO - 3, tt), :]
    for j in range(1, 4):
        z = z + cw_ref[j:j + 1, :] * xs_ref[pl.ds(HALO - 3 + j, tt), :]
    return z


def _dnprep_fwd(projp, conv_w, *, bsz, t_seq, tt, name, rider=None):
    n = bsz * t_seq
    per_seq = t_seq // tt
    hw = NH * DN_D

    def body(x_ref, halo_ref, cw_ref, q_ref, k_ref, v_ref, xs_ref):
        i = pl.program_id(1)
        xs_ref[0:HALO, :] = jnp.where(i == 0, 0.0, halo_ref[...].astype(F32))
        xs_ref[HALO:HALO + tt, :] = x_ref[...].astype(F32)
        z = _conv_z(xs_ref, cw_ref, tt)
        a = z * _sigmoid(z)
        for grp, o_ref in enumerate((q_ref, k_ref)):
            for h in range(NH):
                ah = a[:, grp * hw + h * DN_D: grp * hw + (h + 1) * DN_D]
                rs = lax.rsqrt(jnp.sum(ah * ah, axis=-1, keepdims=True) + EPS)
                o_ref[:, h * DN_D:(h + 1) * DN_D] = (ah * rs).astype(o_ref.dtype)
        v_ref[...] = a[:, 2 * hw:3 * hw].astype(v_ref.dtype)

    def halo_map(b, i):
        return (jnp.maximum((b * t_seq + i * tt) // HALO - 1, 0), 0)

    out = pl.BlockSpec((tt, hw), lambda b, i: (b * per_seq + i, 0))
    sds = jax.ShapeDtypeStruct((n, hw), BF16)
    outs, ridden = _hosted_call(
        body, rider, name=name, grid=(bsz, per_seq),
        in_specs=[pl.BlockSpec((tt, QKV_W), lambda b, i: (b * per_seq + i, 0)),
                  pl.BlockSpec((HALO, QKV_W), halo_map),
                  pl.BlockSpec((4, QKV_W), lambda b, i: (0, 0))],
        out_specs=[out, out, out], out_shape=[sds, sds, sds],
        scratch_shapes=[pltpu.VMEM((tt + HALO, QKV_W), F32)],
        compiler_params=_cp(VMEM_BIG), args=(projp, projp, conv_w))
    return (*outs, ridden)


def _dnprep_bwd_a(projp, conv_w, dq, dk, dv, *, bsz, t_seq, tt, name):
    n = bsz * t_seq
    per_seq = t_seq // tt
    hw = NH * DN_D

    def body(x_ref, halo_ref, cw_ref, dq_ref, dk_ref, dv_ref, dz_ref, dcw_ref, xs_ref):
        b, i = pl.program_id(0), pl.program_id(1)
        xs_ref[0:HALO, :] = jnp.where(i == 0, 0.0, halo_ref[...].astype(F32))
        xs_ref[HALO:HALO + tt, :] = x_ref[...].astype(F32)
        z = _conv_z(xs_ref, cw_ref, tt)
        s = _sigmoid(z)
        a = z * s
        dsilu = s * (1.0 + z * (1.0 - s))
        for grp, d_ref in enumerate((dq_ref, dk_ref)):
            for h in range(NH):
                sl = slice(grp * hw + h * DN_D, grp * hw + (h + 1) * DN_D)
                ah = a[:, sl]
                rs = lax.rsqrt(jnp.sum(ah * ah, axis=-1, keepdims=True) + EPS)
                y = ah * rs
                dy = d_ref[:, h * DN_D:(h + 1) * DN_D]
                da = rs * (dy - y * jnp.sum(dy * y, axis=-1, keepdims=True))
                dz_ref[:, sl] = da * dsilu[:, sl]
        dz_ref[:, 2 * hw:3 * hw] = dv_ref[...] * dsilu[:, 2 * hw:3 * hw]
        dz = dz_ref[...]
        first = jnp.logical_and(b == 0, i == 0)
        for j in range(4):
            part = jnp.sum(dz * xs_ref[pl.ds(HALO - 3 + j, tt), :], axis=0, keepdims=True)

            @pl.when(first)
            def _():
                dcw_ref[j:j + 1, :] = part

            @pl.when(jnp.logical_not(first))
            def _():
                dcw_ref[j:j + 1, :] += part

    def halo_map(b, i):
        return (jnp.maximum((b * t_seq + i * tt) // HALO - 1, 0), 0)

    hrow = pl.BlockSpec((tt, hw), lambda b, i: (b * per_seq + i, 0))
    return pl.pallas_call(
        body, name=name, grid=(bsz, per_seq),
        in_specs=[pl.BlockSpec((tt, QKV_W), lambda b, i: (b * per_seq + i, 0)),
                  pl.BlockSpec((HALO, QKV_W), halo_map),
                  pl.BlockSpec((4, QKV_W), lambda b, i: (0, 0)), hrow, hrow, hrow],
        out_specs=[pl.BlockSpec((tt, QKV_W), lambda b, i: (b * per_seq + i, 0)),
                   pl.BlockSpec((4, QKV_W), lambda b, i: (0, 0))],
        out_shape=[jax.ShapeDtypeStruct((n, QKV_W), F32), jax.ShapeDtypeStruct((4, QKV_W), F32)],
        scratch_shapes=[pltpu.VMEM((tt + HALO, QKV_W), F32)],
        compiler_params=_cp(VMEM_BIG),
    )(projp, projp, conv_w, dq, dk, dv)


def _dnprep_bwd_b(dz, conv_w, *, bsz, t_seq, tt, name):
    n = bsz * t_seq
    per_seq = t_seq // tt
    last_blk = n // HALO - 1

    main = tt - HALO
    col_blk = 256

    def body(dz_ref, halo_ref, cw_ref, dx_ref, tail_ref):
        i = pl.program_id(1)
        for cb in range(QKV_W // col_blk):
            cols = slice(cb * col_blk, (cb + 1) * col_blk)
            dx = cw_ref[0:1, cols] * dz_ref[pl.ds(3, main), cols]
            for j in range(1, 4):
                dx = dx + cw_ref[j:j + 1, cols] * dz_ref[pl.ds(3 - j, main), cols]
            dx_ref[0:main, cols] = dx.astype(dx_ref.dtype)
        tail_ref[0:HALO, :] = dz_ref[main:tt, :]
        tail_ref[HALO:2 * HALO, :] = jnp.where(i == per_seq - 1, 0.0, halo_ref[...])
        dx = cw_ref[0:1, :] * tail_ref[pl.ds(3, HALO), :]
        for j in range(1, 4):
            dx = dx + cw_ref[j:j + 1, :] * tail_ref[pl.ds(3 - j, HALO), :]
        dx_ref[main:tt, :] = dx.astype(dx_ref.dtype)

    def halo_map(b, i):
        return (jnp.minimum((b * t_seq + (i + 1) * tt) // HALO, last_blk), 0)

    row = pl.BlockSpec((tt, QKV_W), lambda b, i: (b * per_seq + i, 0))
    return pl.pallas_call(
        body, name=name, grid=(bsz, per_seq),
        in_specs=[row, pl.BlockSpec((HALO, QKV_W), halo_map), pl.BlockSpec((4, QKV_W), lambda b, i: (0, 0))],
        out_specs=row, out_shape=jax.ShapeDtypeStruct((n, QKV_W), BF16),
        scratch_shapes=[pltpu.VMEM((2 * HALO, QKV_W), F32)],
        compiler_params=_cp(VMEM_BIG),
    )(dz, dz, conv_w)


def _masks64():
    r = _iota2((CHUNK, CHUNK), 0)
    c = _iota2((CHUNK, CHUNK), 1)
    return r, c


def _group(nc_seq, target=5):
    return max(g for g in range(1, target + 1) if nc_seq % g == 0)


def _round_robin(chains):
    live = list(chains)
    while live:
        nxt = []
        for ch in live:
            try:
                next(ch)
                nxt.append(ch)
            except StopIteration:
                pass
        live = nxt
        yield


def _run(chains):
    for _ in _round_robin(chains):
        pass


def _per_chunk(inner, kinds, grp):
    def body(*refs):
        chains = []
        for gi in range(grp):
            views = []
            for r, kind in zip(refs, kinds):
                if kind == "row":
                    views.append(r.at[pl.ds(gi * CHUNK, CHUNK)])
                elif kind == "lead":
                    views.append(r.at[pl.ds(gi, 1)])
                else:
                    views.append(r)
            chains.append(inner(gi, *views))
        _run(chains)
    return body


def _accumulate(ref, val, gi):
    if gi > 0:
        ref[...] += val
        return
    first = pl.program_id(0) == 0

    @pl.when(first)
    def _():
        ref[...] = val

    @pl.when(jnp.logical_not(first))
    def _():
        ref[...] += val


ANY = pl.BlockSpec(memory_space=pl.ANY)


def _place():
    return lax.axis_index("x"), lax.axis_index("y"), lax.axis_index("c")


def _other_chips(x, y):
    return [(1 - x, y, 2 * (1 - x) + y), (x, 1 - y, 2 * x + 1 - y), (1 - x, 1 - y, 2 * (1 - x) + 1 - y)]


class _GatherRider:
    def __init__(self, bufs, split):
        self.inputs = list(bufs)
        self.split = list(split)
        self.out_shapes = [jax.ShapeDtypeStruct(b.shape, b.dtype) for b in bufs]
        self.aliases = {i: i for i in range(len(bufs))}
        self.sems = [pltpu.SemaphoreType.DMA((len(bufs), 3))] * 4

    def _rows(self, k, buf, c, mine=True):
        r = buf.shape[1]
        if not self.split[k]:
            return pl.ds(0, r)
        return pl.ds((c if mine else 1 - c) * (r // 2), r // 2)

    def _ici(self, k, d, bufs, sems, c, px, py, block):
        rows = self._rows(k, bufs[k], c)
        return pltpu.make_async_remote_copy(
            src_ref=bufs[k].at[block, rows, :], dst_ref=bufs[k].at[block, rows, :], send_sem=sems[0].at[k, d],
            recv_sem=sems[1].at[k, d], device_id=(px, py, c), device_id_type=MESH)

    def _pass(self, k, d, bufs, sems, x, y, c, block, mine):
        rows = self._rows(k, bufs[k], c, mine)
        return pltpu.make_async_remote_copy(
            src_ref=bufs[k].at[block, rows, :], dst_ref=bufs[k].at[block, rows, :], send_sem=sems[2].at[k, d],
            recv_sem=sems[3].at[k, d], device_id=(x, y, 1 - c), device_id_type=MESH)

    def first(self, in_refs, bufs, sems):
        x, y, c = _place()
        for k in range(len(bufs)):
            for d, (px, py, _) in enumerate(_other_chips(x, y)):
                self._ici(k, d, bufs, sems, c, px, py, 2 * x + y).start()

    def last(self, in_refs, bufs, sems):
        x, y, c = _place()
        chips = _other_chips(x, y)
        for k in range(len(bufs)):
            for d, (px, py, pj) in enumerate(chips):
                self._ici(k, d, bufs, sems, c, px, py, pj).wait_recv()
                if self.split[k]:
                    self._pass(k, d, bufs, sems, x, y, c, pj, True).start()
        for k in range(len(bufs)):
            for d, (px, py, pj) in enumerate(chips):
                if self.split[k]:
                    self._pass(k, d, bufs, sems, x, y, c, pj, False).wait_recv()
                    self._pass(k, d, bufs, sems, x, y, c, pj, True).wait_send()
                self._ici(k, d, bufs, sems, c, px, py, 2 * x + y).wait_send()


def _hosted_call(body, rider, *, name, grid, in_specs, out_specs, out_shape, scratch_shapes, compiler_params, args):
    if rider is None:
        outs = pl.pallas_call(body, name=name, grid=grid, in_specs=in_specs, out_specs=out_specs, out_shape=out_shape,
                              scratch_shapes=scratch_shapes, compiler_params=compiler_params)(*args)
        return list(outs), []
    n_in, n_out, n_scr = len(in_specs), len(out_specs), len(scratch_shapes)
    r_in, r_out = len(rider.inputs), len(rider.out_shapes)
    compiler_params = _cp(compiler_params.vmem_limit_bytes, ("arbitrary",) * len(grid))

    def full_body(*refs):
        ins = refs[:n_in]
        rins = refs[n_in:n_in + r_in]
        outs = refs[n_in + r_in:n_in + r_in + n_out]
        routs = refs[n_in + r_in + n_out:n_in + r_in + n_out + r_out]
        rest = refs[n_in + r_in + n_out + r_out:]
        scr, sems = rest[:n_scr], rest[n_scr:]
        ids = [pl.program_id(a) for a in range(len(grid))]
        is_first = functools.reduce(jnp.logical_and, [i == 0 for i in ids])
        is_last = functools.reduce(jnp.logical_and, [i == g - 1 for i, g in zip(ids, grid)])

        @pl.when(is_first)
        def _():
            rider.first(rins, routs, sems)

        body(*ins, *outs, *scr)

        @pl.when(is_last)
        def _():
            rider.last(rins, routs, sems)

    res = pl.pallas_call(
        full_body, name=name, grid=grid, in_specs=list(in_specs) + [ANY] * r_in,
        out_specs=list(out_specs) + [ANY] * r_out, out_shape=list(out_shape) + list(rider.out_shapes),
        input_output_aliases={n_in + i: n_out + o for i, o in rider.aliases.items()},
        scratch_shapes=list(scratch_shapes) + list(rider.sems), compiler_params=compiler_params,
    )(*args, *rider.inputs)
    return list(res[:n_out]), list(res[n_out:])


def _exchange_now(rider, *, name):
    r_in = len(rider.inputs)

    def body(*refs):
        rins = refs[:r_in]
        routs = refs[r_in:r_in + len(rider.out_shapes)]
        sems = refs[r_in + len(rider.out_shapes):]
        rider.first(rins, routs, sems)
        rider.last(rins, routs, sems)

    return pl.pallas_call(
        body, name=name, in_specs=[ANY] * r_in, out_specs=[ANY] * len(rider.out_shapes), out_shape=list(rider.out_shapes),
        input_output_aliases=dict(rider.aliases), scratch_shapes=list(rider.sems),
    )(*rider.inputs)


def _to_slot(where, a, dtype, *, name):
    r, cols = a.shape
    tr = _tile(r, 256, 16) if r > 256 else r

    def body(w_ref, a_ref, o_ref):
        o_ref[0] = a_ref[...].astype(o_ref.dtype)

    return pl.pallas_call(
        body, name=name,
        grid_spec=pltpu.PrefetchScalarGridSpec(
            num_scalar_prefetch=1, grid=(r // tr,),
            in_specs=[pl.BlockSpec((tr, cols), lambda i, w: (i, 0))],
            out_specs=pl.BlockSpec((1, tr, cols), lambda i, w: (w[1], i, 0))),
        out_shape=jax.ShapeDtypeStruct((N_CHIPS, r, cols), dtype), compiler_params=_cp(VMEM_BIG),
    )(where, a)


def _tri_inv(a_strict):
    r, c = _masks64()
    eye = (r == c).astype(F32)
    blk16 = (r // 16) == (c // 16)
    blk32 = (r // 32) == (c // 32)
    ld = jnp.where(blk16, a_strict, 0.0)
    x = eye - ld
    p = _nn(ld, ld)
    yield
    for step in range(3):
        xp = _nn(x, p)
        if step < 2:
            p = _nn(p, p)
        x = x + xp
        yield
    for lk in (jnp.where(jnp.logical_and(blk32, jnp.logical_not(blk16)), a_strict, 0.0),
               jnp.where(blk32, 0.0, a_strict)):
        y = x - eye
        s = lk + _nn(y, lk)
        yield
        x = x - s - _nn(s, y)
        yield
    return x


def _dn_gates(sa, alog, dtb, chunk_in_seq):
    rows = _iota2((CHUNK, LANE), 0)
    valid = jnp.logical_or(rows >= N_PAD, chunk_in_seq > 0)
    beta_t = _sigmoid(sa)
    ea = jnp.exp(alog)
    g_t = jnp.where(valid, -ea * _softplus(sa + dtb), 0.0)
    r, c = _masks64()
    ltri = (r >= c).astype(F32)
    gam_t = _tri_sum(ltri, g_t)
    return beta_t, g_t, gam_t, valid, ea


def _dn_intra_fwd(qn, kn, v, projp, alog_row, dtb_row, *, nc_seq, name, rider=None):
    n = qn.shape[0]
    nct = n // CHUNK
    hw = NH * DN_D
    scale = DN_D ** -0.5

    grp = _group(nc_seq)

    def inner(gi, q_ref, k_ref, v_ref, sa_ref, al_ref, dt_ref, u_ref, w_ref, qg_ref, kd_ref, p_ref, t_ref, gl_ref):
        ci = (pl.program_id(0) * grp + gi) % nc_seq
        beta_t, _, gam_t, _, _ = _dn_gates(sa_ref[...], al_ref[...], dt_ref[...], ci)
        yield
        gam_tt = gam_t.T
        r, c = _masks64()
        incl = r >= c
        strict = r > c

        def head(h):
            sl = slice(h * DN_D, (h + 1) * DN_D)
            beta_w = jnp.broadcast_to(beta_t[:, h:h + 1], (CHUNK, DN_D))
            gam_w = jnp.broadcast_to(gam_t[:, 4 + h:5 + h], (CHUNK, DN_D))
            gam_row = gam_tt[4 + h:5 + h, :]
            gl = gam_t[CHUNK - 1:CHUNK, 4 + h:5 + h]
            dec = jnp.exp(jnp.where(incl, gam_w[:, 0:CHUNK] - gam_row, -jnp.inf))
            kh = k_ref[:, sl].astype(F32)
            qh = q_ref[:, sl].astype(F32) * scale
            vh = v_ref[:, sl].astype(F32)
            kk = _nt(kh, kh)
            qk = _nt(qh, kh)
            yield
            a = jnp.where(strict, beta_w[:, 0:CHUNK] * kk * dec, 0.0)
            tm = yield from _tri_inv(a)
            egam_w = jnp.exp(gam_w)
            u_ref[:, sl] = _nn(tm, beta_w * vh).astype(u_ref.dtype)
            w_ref[:, sl] = _nn(tm, (beta_w * egam_w) * kh).astype(w_ref.dtype)
            qg_ref[:, sl] = (egam_w * qh).astype(qg_ref.dtype)
            kd_ref[:, sl] = (jnp.exp(gl - gam_w) * kh).astype(kd_ref.dtype)
            p_ref[0, h] = qk * dec
            t_ref[0, h] = tm
            gl_ref[0, h:h + 1, :] = jnp.broadcast_to(jnp.exp(gl), (1, LANE))

        yield from _round_robin([head(h) for h in range(NH)])

    rows = grp * CHUNK
    row = pl.BlockSpec((rows, hw), lambda i: (i, 0))
    vec = pl.BlockSpec((1, LANE), lambda i: (0, 0))
    mat = pl.BlockSpec((grp, NH, CHUNK, CHUNK), lambda i: (i, 0, 0, 0))
    big = jax.ShapeDtypeStruct((n, hw), BF16)
    msd = jax.ShapeDtypeStruct((nct, NH, CHUNK, CHUNK), F32)
    kinds = ["row"] * 4 + ["whole"] * 2 + ["row"] * 4 + ["lead"] * 3
    outs, ridden = _hosted_call(
        _per_chunk(inner, kinds, grp), rider, name=name, grid=(nct // grp,),
        in_specs=[row, row, row, pl.BlockSpec((rows, LANE), lambda i: (i, 0)), vec, vec],
        out_specs=[row, row, row, row, mat, mat, pl.BlockSpec((grp, NH, LANE), lambda i: (i, 0, 0))],
        out_shape=[big, big, big, big, msd, msd, jax.ShapeDtypeStruct((nct, NH, LANE), F32)],
        scratch_shapes=[], compiler_params=_cp(VMEM_BIG, ("arbitrary",)),
        args=(qn, kn, v, projp, alog_row, dtb_row))
    return (*outs, ridden)


def _dn_scan_fwd(u, w, qg, kd, p, gl, *, bsz, nc_seq, name, rider=None):
    hw = NH * DN_D
    t_seq = nc_seq * CHUNK
    u, w, qg, kd = (z.reshape(bsz, t_seq, hw) for z in (u, w, qg, kd))
    p = p.reshape(bsz, nc_seq, NH, CHUNK, CHUNK)
    gl = gl.reshape(bsz, nc_seq, NH, LANE)
    grp = _group(nc_seq)

    def body(u_ref, w_ref, qg_ref, kd_ref, p_ref, gl_ref, o_ref, vn_ref, hist_ref, s_ref):
        @pl.when(pl.program_id(0) == 0)
        def _():
            s_ref[...] = jnp.zeros_like(s_ref)

        def chain(b, h, gi):
            sl = slice(h * DN_D, (h + 1) * DN_D)
            rows = pl.ds(gi * CHUNK, CHUNK)
            s = s_ref[b, h]
            hist_ref[b, gi, h] = s.astype(hist_ref.dtype)
            ws = _nn(w_ref[b, rows, sl], s)
            qs = _nn(qg_ref[b, rows, sl], s)
            yield
            vn = u_ref[b, rows, sl] - ws
            vn_ref[b, rows, sl] = vn.astype(vn_ref.dtype)
            o_ref[b, rows, sl] = (qs + _nn(p_ref[b, gi, h], vn)).astype(o_ref.dtype)
            s_ref[b, h] = gl_ref[b, gi, h:h + 1, :] * s + _tn(kd_ref[b, rows, sl], vn)

        for gi in range(grp):
            _run([chain(b, h, gi) for b in range(bsz) for h in range(NH)])

    row = pl.BlockSpec((bsz, grp * CHUNK, hw), lambda i: (0, i, 0))
    outs, ridden = _hosted_call(
        body, rider, name=name, grid=(nc_seq // grp,),
        in_specs=[row, row, row, row, pl.BlockSpec((bsz, grp, NH, CHUNK, CHUNK), lambda i: (0, i, 0, 0, 0)),
                  pl.BlockSpec((bsz, grp, NH, LANE), lambda i: (0, i, 0, 0))],
        out_specs=[row, row, pl.BlockSpec((bsz, grp, NH, DN_D, DN_D), lambda i: (0, i, 0, 0, 0))],
        out_shape=[jax.ShapeDtypeStruct((bsz, t_seq, hw), BF16), jax.ShapeDtypeStruct((bsz, t_seq, hw), BF16),
                   jax.ShapeDtypeStruct((bsz, nc_seq, NH, DN_D, DN_D), F32)],
        scratch_shapes=[pltpu.VMEM((bsz, NH, DN_D, DN_D), F32)],
        compiler_params=_cp(VMEM_BIG, ("arbitrary",)), args=(u, w, qg, kd, p, gl))
    o, vn, hist = outs
    return o.reshape(bsz * t_seq, hw), vn.reshape(bsz * t_seq, hw), hist, ridden


def _dn_scan_bwd(do, w, qg, kd, vn, p, gl, hist, *, bsz, nc_seq, name):
    hw = NH * DN_D
    t_seq = nc_seq * CHUNK
    do, w, qg, kd, vn = (z.reshape(bsz, t_seq, hw) for z in (do, w, qg, kd, vn))
    p = p.reshape(bsz, nc_seq, NH, CHUNK, CHUNK)
    gl = gl.reshape(bsz, nc_seq, NH, LANE)
    grp = _group(nc_seq)

    def body(do_ref, w_ref, qg_ref, kd_ref, vn_ref, p_ref, gl_ref, hist_ref,
             du_ref, dw_ref, dqg_ref, dkd_ref, dgl_ref, ds_ref):
        @pl.when(pl.program_id(0) == 0)
        def _():
            ds_ref[...] = jnp.zeros_like(ds_ref)

        def chain(b, h, gi):
            sl = slice(h * DN_D, (h + 1) * DN_D)
            rows = pl.ds(gi * CHUNK, CHUNK)
            s = hist_ref[b, gi, h]
            dsn = ds_ref[b, h]
            doh = do_ref[b, rows, sl]
            vnh = vn_ref[b, rows, sl]
            kdh = kd_ref[b, rows, sl]
            dvn = _tn(p_ref[b, gi, h], doh) + _nn(kdh, dsn)
            du_ref[b, rows, sl] = dvn.astype(du_ref.dtype)
            dqg_ref[b, rows, sl] = _nt(doh, s).astype(dqg_ref.dtype)
            dkd_ref[b, rows, sl] = _nt(vnh, dsn).astype(dkd_ref.dtype)
            ds_part = _tn(qg_ref[b, rows, sl], doh) + gl_ref[b, gi, h:h + 1, :] * dsn
            dgl = jnp.sum(jnp.sum(dsn * s, axis=0, keepdims=True), axis=1, keepdims=True)
            dgl_ref[b, gi, h:h + 1, :] = jnp.broadcast_to(dgl, (1, LANE))
            yield
            dw_ref[b, rows, sl] = (-_nt(dvn, s)).astype(dw_ref.dtype)
            ds_ref[b, h] = ds_part - _tn(w_ref[b, rows, sl], dvn)

        for gi in reversed(range(grp)):
            _run([chain(b, h, gi) for b in range(bsz) for h in range(NH)])

    steps = nc_seq // grp
    rev = lambda i: steps - 1 - i
    row = pl.BlockSpec((bsz, grp * CHUNK, hw), lambda i: (0, rev(i), 0))
    mat = pl.BlockSpec((bsz, grp, NH, CHUNK, CHUNK), lambda i: (0, rev(i), 0, 0, 0))
    glb = pl.BlockSpec((bsz, grp, NH, LANE), lambda i: (0, rev(i), 0, 0))
    big = jax.ShapeDtypeStruct((bsz, t_seq, hw), BF16)
    outs = pl.pallas_call(
        body, name=name, grid=(steps,),
        in_specs=[row, row, row, row, row, mat, glb,
                  pl.BlockSpec((bsz, grp, NH, DN_D, DN_D), lambda i: (0, rev(i), 0, 0, 0))],
        out_specs=[row, row, row, row, glb],
        out_shape=[big, big, jax.ShapeDtypeStruct(big.shape, F32), jax.ShapeDtypeStruct(big.shape, F32),
                   jax.ShapeDtypeStruct((bsz, nc_seq, NH, LANE), F32)],
        scratch_shapes=[pltpu.VMEM((bsz, NH, DN_D, DN_D), F32)],
        compiler_params=_cp(VMEM_BIG, ("arbitrary",)),
    )(do, w, qg, kd, vn, p, gl, hist)
    du, dw, dqg, dkd, dgl = outs
    n = bsz * t_seq
    return (du.reshape(n, hw), dw.reshape(n, hw), dqg.reshape(n, hw), dkd.reshape(n, hw),
            dgl.reshape(bsz * nc_seq, NH, LANE))


def _dn_intra_bwd(qn, kn, v, projp, alog_row, dtb_row, u, w, tmat, du, dw, dqg, dkd, do, vn, dgl, *, nc_seq, name,
                  rider=None):
    n = qn.shape[0]
    nct = n // CHUNK
    hw = NH * DN_D
    scale = DN_D ** -0.5

    grp = _group(nc_seq)

    def inner(gi, q_ref, k_ref, v_ref, sa_ref, al_ref, dt_ref, u_ref, w_ref, t_ref, du_ref, dw_ref, dqg_ref, dkd_ref,
              do_ref, vn_ref, dgl_ref, dq_ref, dk_ref, dv_ref, dsa_ref, dal_ref, ddt_ref):
        ci = (pl.program_id(0) * grp + gi) % nc_seq
        sa = sa_ref[...]
        beta_t, g_t, gam_t, valid, ea = _dn_gates(sa, al_ref[...], dt_ref[...], ci)
        yield
        lane = _iota2((CHUNK, LANE), 1)
        gates_t = jnp.where(lane < 4, beta_t, gam_t).T
        r, c = _masks64()
        incl, strict, upper, supper = r >= c, r > c, r <= c, r < c
        rows1 = _iota2((CHUNK, 1), 0)
        acc = [jnp.zeros((CHUNK, LANE), F32)]

        def head(h):
            sl = slice(h * DN_D, (h + 1) * DN_D)
            beta_w = jnp.broadcast_to(beta_t[:, h:h + 1], (CHUNK, DN_D))
            gam_w = jnp.broadcast_to(gam_t[:, 4 + h:5 + h], (CHUNK, DN_D))
            beta_s, gam_s = beta_w[:, 0:CHUNK], gam_w[:, 0:CHUNK]
            beta_row = gates_t[h:h + 1, :]
            gam_row = gates_t[4 + h:5 + h, :]
            gl = gam_t[CHUNK - 1:CHUNK, 4 + h:5 + h]
            dec = jnp.exp(jnp.where(incl, gam_s - gam_row, -jnp.inf))
            dec_t = jnp.exp(jnp.where(upper, gam_row - gam_s, -jnp.inf))
            egam_w = jnp.exp(gam_w)
            ekd_w = jnp.exp(gl - gam_w)
            kh = k_ref[:, sl].astype(F32)
            qh = q_ref[:, sl].astype(F32) * scale
            vh = v_ref[:, sl].astype(F32)
            uh = u_ref[:, sl]
            wh = w_ref[:, sl]
            doh = do_ref[:, sl]
            vnh = vn_ref[:, sl]
            kk = _nt(kh, kh)
            qk = _nt(qh, kh)
            qk_t = _nt(kh, qh)
            dp = _nt(doh, vnh)
            dp_t = _nt(vnh, doh)
            t_hi, t_lo = _split(t_ref[0, h].T)
            duh, dwh = du_ref[:, sl], dw_ref[:, sl]
            dvb = _nn(t_hi, duh) + _nn(t_lo, duh)
            dkg = _nn(t_hi, dwh) + _nn(t_lo, dwh)
            yield
            dvb_hi, dvb_lo = _split(dvb)
            dkg_hi, dkg_lo = _split(dkg)
            m = (_nt(dvb_hi, uh) + _nt(dvb_lo, uh)) + (_nt(dkg_hi, wh) + _nt(dkg_lo, wh))
            m_t = (_nt(uh, dvb_hi) + _nt(uh, dvb_lo)) + (_nt(wh, dkg_hi) + _nt(wh, dkg_lo))
            yield
            da = jnp.where(strict, -m, 0.0)
            da_t = jnp.where(supper, -m_t, 0.0)
            a = jnp.where(strict, beta_s * kk * dec, 0.0)
            a_t = jnp.where(supper, beta_row * kk * dec_t, 0.0)
            dad = da * dec
            dad_t = da_t * dec_t
            dpm = jnp.where(incl, dp, 0.0)
            dpm_t = jnp.where(upper, dp_t, 0.0)
            e = da * a + dpm * (qk * dec)
            e_t = da_t * a_t + dpm_t * (qk_t * dec_t)
            dqgh = dqg_ref[:, sl].astype(F32)
            dkdh = dkd_ref[:, sl].astype(F32)
            bg_w = beta_w * egam_w
            dkh = (_nn(beta_s * dad, kh) + _nn(beta_row * dad_t, kh) + _nn(dpm_t * dec_t, qh)
                   + bg_w * dkg + ekd_w * dkdh)
            dqh = _nn(dpm * dec, kh) + egam_w * dqgh
            t_kd = dkdh * (ekd_w * kh)
            dbeta = (jnp.sum(dad * kk, axis=1, keepdims=True)
                     + jnp.sum(dkg * (egam_w * kh) + dvb * vh, axis=1, keepdims=True))
            dgam = (jnp.sum(e - e_t, axis=1, keepdims=True)
                    + jnp.sum(dkg * (bg_w * kh) + dqgh * (egam_w * qh) - t_kd, axis=1, keepdims=True))
            dgam_last = (jnp.sum(jnp.sum(t_kd, axis=0, keepdims=True), axis=1, keepdims=True)
                         + dgl_ref[0, h:h + 1, 0:1] * jnp.exp(gl))
            dgam = dgam + jnp.where(rows1 == CHUNK - 1, dgam_last, 0.0)
            dq_ref[:, sl] = (dqh * scale).astype(dq_ref.dtype)
            dk_ref[:, sl] = dkh.astype(dk_ref.dtype)
            dv_ref[:, sl] = (beta_w * dvb).astype(dv_ref.dtype)
            acc[0] = acc[0] + jnp.where(lane == h, dbeta, 0.0) + jnp.where(lane == 4 + h, dgam, 0.0)

        yield from _round_robin([head(h) for h in range(NH)])
        acc_t = acc[0]
        dg_t = _tri_sum(upper, acc_t)
        ddb = acc_t * beta_t * (1.0 - beta_t)
        dda = jnp.where(valid, dg_t * (-ea) * _sigmoid(sa + dt_ref[...]), 0.0)
        dsa_ref[...] = jnp.where(lane < 4, ddb, jnp.where(lane < 8, dda, 0.0)).astype(dsa_ref.dtype)
        in_g = jnp.logical_and(lane >= 4, lane < 8)
        dal = jnp.sum(jnp.where(in_g, dg_t * g_t, 0.0), axis=0, keepdims=True)
        ddt = jnp.sum(jnp.where(in_g, dda, 0.0), axis=0, keepdims=True)
        _accumulate(dal_ref, dal, gi)
        _accumulate(ddt_ref, ddt, gi)

    rows = grp * CHUNK
    row = pl.BlockSpec((rows, hw), lambda i: (i, 0))
    vec = pl.BlockSpec((1, LANE), lambda i: (0, 0))
    mat = pl.BlockSpec((grp, NH, CHUNK, CHUNK), lambda i: (i, 0, 0, 0))
    glb = pl.BlockSpec((grp, NH, LANE), lambda i: (i, 0, 0))
    big = jax.ShapeDtypeStruct((n, hw), F32)
    v128 = jax.ShapeDtypeStruct((1, LANE), F32)
    kinds = (["row"] * 4 + ["whole"] * 2 + ["row"] * 2 + ["lead"] + ["row"] * 6 + ["lead"]
             + ["row"] * 4 + ["whole"] * 2)
    outs, ridden = _hosted_call(
        _per_chunk(inner, kinds, grp), rider, name=name, grid=(nct // grp,),
        in_specs=[row, row, row, pl.BlockSpec((rows, LANE), lambda i: (i, 0)), vec, vec,
                  row, row, mat, row, row, row, row, row, row, glb],
        out_specs=[row, row, row, pl.BlockSpec((rows, LANE), lambda i: (i, 0)), vec, vec],
        out_shape=[big, big, big, jax.ShapeDtypeStruct((n, LANE), BF16), v128, v128],
        scratch_shapes=[], compiler_params=_cp(VMEM_BIG, ("arbitrary",)),
        args=(qn, kn, v, projp, alog_row, dtb_row, u, w, tmat, du, dw, dqg, dkd, do, vn, dgl))
    return (*outs, ridden)


GQ_W = NH * GLA_DK
GV_W = NH * GLA_DV
GLA_NORM = 16.0
MID = CHUNK // 2


def _gla_gates(sb, w2p, gb, chunk_in_seq):
    rows = _iota2((CHUNK, GQ_W), 0)
    valid = jnp.logical_or(rows >= N_PAD, chunk_in_seq > 0)
    graw = _nn(sb, w2p) + gb
    yield
    g = jnp.where(valid, _logsigmoid(graw) * (1.0 / GLA_NORM), 0.0)
    r, c = _masks64()
    bcum = _tri_sum(r >= c, g)
    yield
    return graw, bcum, valid


def _head_mask(h):
    lane = _iota2((1, GQ_W), 1)
    return jnp.logical_and(lane >= h * GLA_DK, lane < (h + 1) * GLA_DK)


def _gla_intra_fwd(projp, gates, w2p, gb, *, nc_seq, name):
    n = projp.shape[0]
    nct = n // CHUNK
    scale = GLA_DK ** -0.5

    grp = _group(nc_seq)
    rows = grp * CHUNK

    def inner(gi, qk_ref, v_ref, sb_ref, w2_ref, gb_ref, oi_ref, qg_ref, kd_ref, gl_ref):
        ci = (pl.program_id(0) * grp + gi) % nc_seq
        _, bc, _ = yield from _gla_gates(sb_ref[...], w2_ref[...], gb_ref[...], ci)
        bref = bc[MID:MID + 1, :]
        bl = bc[CHUNK - 1:CHUNK, :]
        q = qk_ref[:, 0:GQ_W].astype(F32) * scale
        k = qk_ref[:, GQ_W:2 * GQ_W].astype(F32)
        qi = q * jnp.exp(bc - bref)
        ki = k * jnp.exp(bref - bc)
        qg_ref[...] = (q * jnp.exp(bc)).astype(qg_ref.dtype)
        kd_ref[...] = (k * jnp.exp(bl - bc)).astype(kd_ref.dtype)
        gl_ref[0] = jnp.exp(bl)
        r, c = _masks64()
        incl = r >= c
        a = [jnp.where(incl, _nt(jnp.where(_head_mask(h), qi, 0.0), ki), 0.0) for h in range(NH)]
        yield
        for h in range(NH):
            oi_ref[:, h * GLA_DV:(h + 1) * GLA_DV] = _nn(a[h], v_ref[:, h * GLA_DV:(h + 1) * GLA_DV]).astype(oi_ref.dtype)

    kinds = ["row"] * 3 + ["whole"] * 2 + ["row"] * 3 + ["lead"]
    return pl.pallas_call(
        _per_chunk(inner, kinds, grp), name=name, grid=(nct // grp,),
        in_specs=[pl.BlockSpec((rows, 2 * GQ_W), lambda i: (i, C_GQK // (2 * GQ_W))),
                  pl.BlockSpec((rows, GV_W), lambda i: (i, C_GV // GV_W)),
                  pl.BlockSpec((rows, LANE), lambda i: (i, 1)),
                  pl.BlockSpec((LANE, GQ_W), lambda i: (0, 0)), pl.BlockSpec((1, GQ_W), lambda i: (0, 0))],
        out_specs=[pl.BlockSpec((rows, GV_W), lambda i: (i, 0)), pl.BlockSpec((rows, GQ_W), lambda i: (i, 0)),
                   pl.BlockSpec((rows, GQ_W), lambda i: (i, 0)), pl.BlockSpec((grp, 1, GQ_W), lambda i: (i, 0, 0))],
        out_shape=[jax.ShapeDtypeStruct((n, GV_W), BF16), jax.ShapeDtypeStruct((n, GQ_W), BF16),
                   jax.ShapeDtypeStruct((n, GQ_W), BF16), jax.ShapeDtypeStruct((nct, 1, GQ_W), F32)],
        compiler_params=_cp(VMEM_BIG),
    )(projp, projp, gates, w2p, gb)


def _gla_scan_fwd(oi, qg, kd, gl, projp, *, bsz, nc_seq, name, rider=None):
    t_seq = nc_seq * CHUNK
    oi = oi.reshape(bsz, t_seq, GV_W)
    qg, kd = qg.reshape(bsz, t_seq, GQ_W), kd.reshape(bsz, t_seq, GQ_W)
    gl = gl.reshape(bsz, nc_seq, 1, GQ_W)
    pj = projp.reshape(bsz, t_seq, PW)
    grp = _group(nc_seq)

    def body(oi_ref, qg_ref, kd_ref, gl_ref, v_ref, o_ref, hist_ref, st_ref):
        @pl.when(pl.program_id(0) == 0)
        def _():
            st_ref[...] = jnp.zeros_like(st_ref)

        for gi in range(grp):
            rows = pl.ds(gi * CHUNK, CHUNK)
            for b in range(bsz):
                st = st_ref[b]
                hist_ref[b, gi] = st.astype(hist_ref.dtype)
                qgb = qg_ref[b, rows, :]
                kdb = kd_ref[b, rows, :]
                upd = jnp.zeros((GLA_DV, GQ_W), F32)
                for h in range(NH):
                    sl = slice(h * GLA_DV, (h + 1) * GLA_DV)
                    m = _head_mask(h)
                    o_ref[b, rows, sl] = (oi_ref[b, rows, sl] + _nt(jnp.where(m, qgb, 0.0), st)).astype(o_ref.dtype)
                    upd = upd + jnp.where(m, _tn(v_ref[b, rows, sl], kdb), 0.0)
                st_ref[b] = gl_ref[b, gi] * st + upd

    rws = grp * CHUNK
    outs, ridden = _hosted_call(
        body, rider, name=name, grid=(nc_seq // grp,),
        in_specs=[pl.BlockSpec((bsz, rws, GV_W), lambda i: (0, i, 0)),
                  pl.BlockSpec((bsz, rws, GQ_W), lambda i: (0, i, 0)),
                  pl.BlockSpec((bsz, rws, GQ_W), lambda i: (0, i, 0)),
                  pl.BlockSpec((bsz, grp, 1, GQ_W), lambda i: (0, i, 0, 0)),
                  pl.BlockSpec((bsz, rws, GV_W), lambda i: (0, i, C_GV // GV_W))],
        out_specs=[pl.BlockSpec((bsz, rws, GV_W), lambda i: (0, i, 0)),
                   pl.BlockSpec((bsz, grp, GLA_DV, GQ_W), lambda i: (0, i, 0, 0))],
        out_shape=[jax.ShapeDtypeStruct((bsz, t_seq, GV_W), BF16),
                   jax.ShapeDtypeStruct((bsz, nc_seq, GLA_DV, GQ_W), F32)],
        scratch_shapes=[pltpu.VMEM((bsz, GLA_DV, GQ_W), F32)],
        compiler_params=_cp(VMEM_BIG, ("arbitrary",)), args=(oi, qg, kd, gl, pj))
    return outs[0].reshape(bsz * t_seq, GV_W), outs[1], ridden


def _gla_scan_bwd(do, qg, kd, gl, projp, hist, *, bsz, nc_seq, name):
    t_seq = nc_seq * CHUNK
    do = do.reshape(bsz, t_seq, GV_W)
    qg, kd = qg.reshape(bsz, t_seq, GQ_W), kd.reshape(bsz, t_seq, GQ_W)
    gl = gl.reshape(bsz, nc_seq, 1, GQ_W)
    pj = projp.reshape(bsz, t_seq, PW)
    grp = _group(nc_seq)

    def body(do_ref, qg_ref, kd_ref, gl_ref, v_ref, hist_ref, dqg_ref, dkd_ref, dv_ref, dgl_ref, dst_ref):
        @pl.when(pl.program_id(0) == 0)
        def _():
            dst_ref[...] = jnp.zeros_like(dst_ref)

        for gi in reversed(range(grp)):
            rows = pl.ds(gi * CHUNK, CHUNK)
            for b in range(bsz):
                st = hist_ref[b, gi]
                dst = dst_ref[b]
                qgb = qg_ref[b, rows, :]
                kdb = kd_ref[b, rows, :]
                dqg = jnp.zeros((CHUNK, GQ_W), F32)
                dkd = jnp.zeros((CHUNK, GQ_W), F32)
                add = jnp.zeros((GLA_DV, GQ_W), F32)
                for h in range(NH):
                    sl = slice(h * GLA_DV, (h + 1) * GLA_DV)
                    m = _head_mask(h)
                    doh = do_ref[b, rows, sl]
                    vh = v_ref[b, rows, sl]
                    dqg = dqg + jnp.where(m, _nn(doh, st), 0.0)
                    dkd = dkd + jnp.where(m, _nn(vh, dst), 0.0)
                    dv_ref[b, rows, sl] = _nt(jnp.where(m, kdb, 0.0), dst).astype(dv_ref.dtype)
                    add = add + jnp.where(m, _tn(doh, qgb), 0.0)
                dqg_ref[b, rows, :] = dqg.astype(dqg_ref.dtype)
                dkd_ref[b, rows, :] = dkd.astype(dkd_ref.dtype)
                dgl_ref[b, gi] = jnp.sum(dst * st, axis=0, keepdims=True)
                dst_ref[b] = gl_ref[b, gi] * dst + add

    steps = nc_seq // grp
    rws = grp * CHUNK
    rev = lambda i: steps - 1 - i
    outs = pl.pallas_call(
        body, name=name, grid=(steps,),
        in_specs=[pl.BlockSpec((bsz, rws, GV_W), lambda i: (0, rev(i), 0)),
                  pl.BlockSpec((bsz, rws, GQ_W), lambda i: (0, rev(i), 0)),
                  pl.BlockSpec((bsz, rws, GQ_W), lambda i: (0, rev(i), 0)),
                  pl.BlockSpec((bsz, grp, 1, GQ_W), lambda i: (0, rev(i), 0, 0)),
                  pl.BlockSpec((bsz, rws, GV_W), lambda i: (0, rev(i), C_GV // GV_W)),
                  pl.BlockSpec((bsz, grp, GLA_DV, GQ_W), lambda i: (0, rev(i), 0, 0))],
        out_specs=[pl.BlockSpec((bsz, rws, GQ_W), lambda i: (0, rev(i), 0)),
                   pl.BlockSpec((bsz, rws, GQ_W), lambda i: (0, rev(i), 0)),
                   pl.BlockSpec((bsz, rws, GV_W), lambda i: (0, rev(i), 0)),
                   pl.BlockSpec((bsz, grp, 1, GQ_W), lambda i: (0, rev(i), 0, 0))],
        out_shape=[jax.ShapeDtypeStruct((bsz, t_seq, GQ_W), F32), jax.ShapeDtypeStruct((bsz, t_seq, GQ_W), F32),
                   jax.ShapeDtypeStruct((bsz, t_seq, GV_W), BF16), jax.ShapeDtypeStruct((bsz, nc_seq, 1, GQ_W), F32)],
        scratch_shapes=[pltpu.VMEM((bsz, GLA_DV, GQ_W), F32)],
        compiler_params=_cp(VMEM_BIG, ("arbitrary",)),
    )(do, qg, kd, gl, pj, hist)
    n = bsz * t_seq
    return (outs[0].reshape(n, GQ_W), outs[1].reshape(n, GQ_W), outs[2].reshape(n, GV_W),
            outs[3].reshape(bsz * nc_seq, 1, GQ_W))


def _gla_intra_bwd(projp, gates, w2p, gb, do, dqg, dkd, dvi, dgl, *, nc_seq, name):
    n = projp.shape[0]
    nct = n // CHUNK
    scale = GLA_DK ** -0.5

    grp = _group(nc_seq)
    rows = grp * CHUNK

    def inner(gi, qk_ref, v_ref, sb_ref, w2_ref, gb_ref, do_ref, dqg_ref, dkd_ref, dvi_ref, dgl_ref,
              dqk_ref, dv_ref, dsb_ref, dw2_ref, dgb_ref):
        ci = (pl.program_id(0) * grp + gi) % nc_seq
        sb = sb_ref[...]
        w2 = w2_ref[...]
        graw, bc, valid = yield from _gla_gates(sb, w2, gb_ref[...], ci)
        bref = bc[MID:MID + 1, :]
        bl = bc[CHUNK - 1:CHUNK, :]
        q = qk_ref[:, 0:GQ_W].astype(F32) * scale
        k = qk_ref[:, GQ_W:2 * GQ_W].astype(F32)
        ex1 = jnp.exp(bc - bref)
        ex2 = jnp.exp(bref - bc)
        eb = jnp.exp(bc)
        ekd = jnp.exp(bl - bc)
        qi, ki = q * ex1, k * ex2
        r, c = _masks64()
        incl = r >= c
        upper = r <= c
        a_t, da, da_t = [], [], []
        for h in range(NH):
            sl = slice(h * GLA_DV, (h + 1) * GLA_DV)
            doh = do_ref[:, sl]
            vh = v_ref[:, sl]
            a_t.append(jnp.where(upper, _nt(jnp.where(_head_mask(h), ki, 0.0), qi), 0.0))
            da.append(jnp.where(incl, _nt(doh, vh), 0.0))
            da_t.append(jnp.where(upper, _nt(vh, doh), 0.0))
        yield
        dqi = jnp.zeros((CHUNK, GQ_W), F32)
        dki = jnp.zeros((CHUNK, GQ_W), F32)
        for h in range(NH):
            sl = slice(h * GLA_DV, (h + 1) * GLA_DV)
            m = _head_mask(h)
            dv_ref[:, sl] = (_nn(a_t[h], do_ref[:, sl]) + dvi_ref[:, sl]).astype(dv_ref.dtype)
            dqi = dqi + jnp.where(m, _nn(da[h], ki), 0.0)
            dki = dki + jnp.where(m, _nn(da_t[h], qi), 0.0)
        yield
        dqg = dqg_ref[...].astype(F32)
        dkd = dkd_ref[...].astype(F32)
        dqk_ref[:, 0:GQ_W] = ((dqi * ex1 + dqg * eb) * scale).astype(dqk_ref.dtype)
        dqk_ref[:, GQ_W:2 * GQ_W] = (dki * ex2 + dkd * ekd).astype(dqk_ref.dtype)
        t_qi, t_ki, t_kd = dqi * qi, dki * ki, dkd * (k * ekd)
        db = t_qi - t_ki + dqg * (q * eb) - t_kd
        dbref = jnp.sum(t_ki - t_qi, axis=0, keepdims=True)
        dbl = jnp.sum(t_kd, axis=0, keepdims=True) + dgl_ref[0] * jnp.exp(bl)
        rows = _iota2((CHUNK, GQ_W), 0)
        db = db + jnp.where(rows == MID, dbref, 0.0) + jnp.where(rows == CHUNK - 1, dbl, 0.0)
        dg = _tri_sum(upper, db)
        yield
        dgraw = jnp.where(valid, dg * (1.0 / GLA_NORM) * _sigmoid(-graw), 0.0)
        dsb_ref[...] = _nt(dgraw, w2).astype(dsb_ref.dtype)
        dw2 = _tn(sb, dgraw)
        dgb = jnp.sum(dgraw, axis=0, keepdims=True)
        _accumulate(dw2_ref, dw2, gi)
        _accumulate(dgb_ref, dgb, gi)

    rq = pl.BlockSpec((rows, GQ_W), lambda i: (i, 0))
    rv = pl.BlockSpec((rows, GV_W), lambda i: (i, 0))
    kinds = ["row"] * 3 + ["whole"] * 2 + ["row"] * 4 + ["lead"] + ["row"] * 3 + ["whole"] * 2
    return pl.pallas_call(
        _per_chunk(inner, kinds, grp), name=name, grid=(nct // grp,),
        in_specs=[pl.BlockSpec((rows, 2 * GQ_W), lambda i: (i, C_GQK // (2 * GQ_W))),
                  pl.BlockSpec((rows, GV_W), lambda i: (i, C_GV // GV_W)),
                  pl.BlockSpec((rows, LANE), lambda i: (i, 1)),
                  pl.BlockSpec((LANE, GQ_W), lambda i: (0, 0)), pl.BlockSpec((1, GQ_W), lambda i: (0, 0)),
                  rv, rq, rq, rv, pl.BlockSpec((grp, 1, GQ_W), lambda i: (i, 0, 0))],
        out_specs=[pl.BlockSpec((rows, 2 * GQ_W), lambda i: (i, 0)), rv, pl.BlockSpec((rows, LANE), lambda i: (i, 0)),
                   pl.BlockSpec((LANE, GQ_W), lambda i: (0, 0)), pl.BlockSpec((1, GQ_W), lambda i: (0, 0))],
        out_shape=[jax.ShapeDtypeStruct((n, 2 * GQ_W), BF16), jax.ShapeDtypeStruct((n, GV_W), BF16),
                   jax.ShapeDtypeStruct((n, LANE), BF16), jax.ShapeDtypeStruct((LANE, GQ_W), F32),
                   jax.ShapeDtypeStruct((1, GQ_W), F32)],
        compiler_params=_cp(VMEM_BIG, ("arbitrary",)),
    )(projp, projp, gates, w2p, gb, do, dqg, dkd, dvi, dgl)


SECTIONS = ((C_QKV, 1536), (C_DZ, 512), (C_GQK, 512), (C_GV, 512), (C_GR, 512), (C_SA, 128), (C_SB, 128))


def _inproj_bwd(secs, wp, h0, g1, dx1, *, bsz, t_seq, tr, name, rider=None):
    n, d = h0.shape
    per_seq = t_seq // tr
    steps = n // tr
    s_len = t_seq - CHUNK

    def body(*refs):
        sec_refs = refs[:len(SECTIONS)]
        wp_ref, h0_ref, g_ref, dx1_ref, gx_hbm, meta_ref, dg_ref, obuf, sems = refs[len(SECTIONS):]
        i = pl.program_id(0)
        slot = i % 2

        def put(step, slot_, act):
            b, j = step // per_seq, step % per_seq

            @pl.when(j == 0)
            def _():
                act(pltpu.make_async_copy(obuf.at[slot_, pl.ds(CHUNK, tr - CHUNK), :],
                                          gx_hbm.at[b, pl.ds(0, tr - CHUNK), :], sems.at[slot_]))

            if per_seq > 1:
                @pl.when(j > 0)
                def _():
                    act(pltpu.make_async_copy(obuf.at[slot_], gx_hbm.at[b, pl.ds(j * tr - CHUNK, tr), :],
                                              sems.at[slot_]))

        dh = None
        for s_ref, (off, wd) in zip(sec_refs, SECTIONS):
            part = _nt(s_ref[...], wp_ref[:, off:off + wd])
            dh = part if dh is None else dh + part
        dx, dg = _rms_bwd_math(h0_ref[...], g_ref[...], dh)
        tot = dx1_ref[...] + dx

        @pl.when(i >= 2)
        def _():
            put(i - 2, slot, lambda cp: cp.wait())

        obuf[slot] = tot
        put(i, slot, lambda cp: cp.start())

        @pl.when(i % per_seq == 0)
        def _():
            meta_ref[...] = tot[N_PAD:CHUNK, :]

        @pl.when(i == steps - 1)
        def _():
            if steps > 1:
                put(i - 1, 1 - slot, lambda cp: cp.wait())
            put(i, slot, lambda cp: cp.wait())

        @pl.when(i == 0)
        def _():
            dg_ref[...] = dg

        @pl.when(i > 0)
        def _():
            dg_ref[...] += dg

    row = pl.BlockSpec((tr, d), lambda i: (i, 0))
    vec = pl.BlockSpec((1, d), lambda i: (0, 0))
    outs, ridden = _hosted_call(
        body, rider, name=name, grid=(steps,),
        in_specs=[pl.BlockSpec((tr, wd), lambda i: (i, 0)) for _, wd in SECTIONS]
        + [pl.BlockSpec((d, PW), lambda i: (0, 0)), row, vec, row],
        out_specs=[ANY, pl.BlockSpec((N_META, d), lambda i: (i // per_seq, 0)), vec],
        out_shape=[jax.ShapeDtypeStruct((bsz, s_len, d), F32), jax.ShapeDtypeStruct((bsz * N_META, d), F32),
                   jax.ShapeDtypeStruct((1, d), F32)],
        scratch_shapes=[pltpu.VMEM((2, tr, d), F32), pltpu.SemaphoreType.DMA((2,))],
        compiler_params=_cp(VMEM_BIG, ("arbitrary",)), args=(*secs, wp, h0, g1, dx1))
    return (*outs, ridden)


def _adamw(w, g, m, v, *, name, emit_grad=False, col_tile=None):
    lead = w.ndim - 2
    r, c = w.shape[-2:]
    tr = r if col_tile is not None else (_tile(r, 256, 8) if r > 256 else r)
    tc = col_tile if col_tile is not None else c
    c1 = 1.0 - ADAM_B1 ** ADAM_STEP
    c2 = 1.0 - ADAM_B2 ** ADAM_STEP
    n_out = 4 if emit_grad else 3

    def body(w_ref, g_ref, m_ref, v_ref, *out_refs):
        rd = (lambda ref: ref[0]) if lead else (lambda ref: ref[...])
        gv = g_ref[:, 0:tc]
        nm = ADAM_B1 * rd(m_ref) + (1.0 - ADAM_B1) * gv
        nv = ADAM_B2 * rd(v_ref) + (1.0 - ADAM_B2) * (gv * gv)
        res = [-ADAM_LR * ((nm / c1) / (jnp.sqrt(nv / c2) + ADAM_EPS) + ADAM_WD * rd(w_ref)), nm, nv, gv]
        for o_ref, val in zip(out_refs, res):
            if lead:
                o_ref[0] = val
            else:
                o_ref[...] = val

    if col_tile is None:
        blk = pl.BlockSpec((1,) * lead + (tr, c), lambda i: (0,) * lead + (i, 0))
        gblk = pl.BlockSpec((tr, g.shape[1]), lambda i: (i, 0))
        steps = r // tr
    else:
        blk = pl.BlockSpec((1,) * lead + (r, tc), lambda j: (0,) * lead + (0, j))
        gblk = pl.BlockSpec((r, tc), lambda j: (0, j))
        steps = c // tc
    sds = jax.ShapeDtypeStruct(w.shape, F32)
    return pl.pallas_call(
        body, name=name, grid=(steps,), in_specs=[blk, gblk, blk, blk], out_specs=[blk] * n_out,
        out_shape=[sds] * n_out, compiler_params=_cp(VMEM_BIG),
    )(w, g, m, v)


def _adamw_rows(w, g, m, v, *, name):
    r, _, c = w.shape
    tr = max(t for t in range(1, 129) if r % t == 0)
    c1 = 1.0 - ADAM_B1 ** ADAM_STEP
    c2 = 1.0 - ADAM_B2 ** ADAM_STEP

    def body(w_ref, g_ref, m_ref, v_ref, d_ref, nm_ref, nv_ref, go_ref):
        gv = g_ref[...]
        nm = ADAM_B1 * m_ref[...] + (1.0 - ADAM_B1) * gv
        nv = ADAM_B2 * v_ref[...] + (1.0 - ADAM_B2) * (gv * gv)
        d_ref[...] = -ADAM_LR * ((nm / c1) / (jnp.sqrt(nv / c2) + ADAM_EPS) + ADAM_WD * w_ref[...])
        nm_ref[...] = nm
        nv_ref[...] = nv
        go_ref[...] = gv

    blk = pl.BlockSpec((tr, 1, c), lambda i: (i, 0, 0))
    sds = jax.ShapeDtypeStruct(w.shape, F32)
    return pl.pallas_call(
        body, name=name, grid=(r // tr,), in_specs=[blk] * 4, out_specs=[blk] * 4, out_shape=[sds] * 4,
        compiler_params=_cp(VMEM_BIG),
    )(w, g, m, v)


def _pair_sum(where, g, theirs, *, name):
    lead, r, cols = g.shape
    half = r // 2
    tr = _tile(half, 256, 16)
    nh = half // tr

    def body(w_ref, a_ref, b_ref, o_ref):
        o_ref[...] = (a_ref[...] + b_ref[...]).astype(o_ref.dtype)

    blk = pl.BlockSpec((1, tr, cols), lambda s, i, w: (s, i, 0))
    return pl.pallas_call(
        body, name=name,
        grid_spec=pltpu.PrefetchScalarGridSpec(
            num_scalar_prefetch=1, grid=(lead, nh),
            in_specs=[pl.BlockSpec((1, tr, cols), lambda s, i, w: (s, w[0] * nh + i, 0)), blk], out_specs=blk),
        out_shape=jax.ShapeDtypeStruct((lead, half, cols), BF16), compiler_params=_cp(VMEM_BIG),
    )(where, g, theirs)


def _chip_sum(where, pair, q, *, name):
    _, half, cols = pair.shape
    tr = _tile(half, 256, 16)
    nh = half // tr

    def body(w_ref, own_ref, q1_ref, q2_ref, q3_ref, o_ref):
        f = lambda ref: ref[0].astype(F32)
        o_ref[...] = ((f(own_ref) + f(q1_ref)) + f(q2_ref)) + f(q3_ref)

    def peer(d):
        return pl.BlockSpec((1, tr, cols), lambda i, w: ((w[1] + d) % N_CHIPS, i, 0))

    return pl.pallas_call(
        body, name=name,
        grid_spec=pltpu.PrefetchScalarGridSpec(
            num_scalar_prefetch=1, grid=(nh,),
            in_specs=[peer(0), peer(1), peer(2), peer(3)],
            out_specs=pl.BlockSpec((tr, cols), lambda i, w: (w[0] * nh + i, 0))),
        out_shape=jax.ShapeDtypeStruct((2 * half, cols), F32), compiler_params=_cp(VMEM_BIG),
    )(where, pair, q, q, q)


VM = pl.BlockSpec(memory_space=pltpu.VMEM)


def _row_chunks(rows, n_split):
    size = rows // n_split
    assert size * n_split == rows and size % 16 == 0, (rows, n_split)
    return [(s, pl.ds(s * size, size)) for s in range(n_split)], size


D2D_SPLIT = 4
ICI_SPLIT = 2


def _sibling_halves(grads):
    n_arr = len(grads)

    def body(*refs):
        ins = refs[:n_arr]
        theirs = refs[n_arr:2 * n_arr]
        send_sems, recv_sems = refs[2 * n_arr:]
        x, y, c = _place()
        copies = []
        for k in range(n_arr):
            half = ins[k].shape[1] // 2
            chunks, size = _row_chunks(half, D2D_SPLIT)
            for s, dst_rows in chunks:
                give = pltpu.make_async_remote_copy(
                    src_ref=ins[k].at[:, pl.ds((1 - c) * half + s * size, size), :], dst_ref=theirs[k].at[:, dst_rows, :],
                    send_sem=send_sems.at[k, s], recv_sem=recv_sems.at[k, s], device_id=(x, y, 1 - c),
                    device_id_type=MESH)
                give.start()
                copies.append(give)
        for give in copies:
            give.wait()

    halves = [jax.ShapeDtypeStruct((g.shape[0], g.shape[1] // 2, g.shape[2]), F32) for g in grads]
    sem = pltpu.SemaphoreType.DMA((n_arr, D2D_SPLIT))
    return pl.pallas_call(
        body, name="sibling_halves", in_specs=[ANY] * n_arr, out_specs=[ANY] * n_arr, out_shape=halves,
        scratch_shapes=[sem, sem],
    )(*grads)


def _chip_exchange(parts):
    n_arr = len(parts)

    def body(*refs):
        ins = refs[:n_arr]
        outs = refs[n_arr:2 * n_arr]
        send_sems, recv_sems = refs[2 * n_arr:]
        x, y, c = _place()
        me = 2 * x + y
        sends = []
        for k in range(n_arr):
            chunks, _ = _row_chunks(ins[k].shape[1], ICI_SPLIT)
            for d, (px, py, pj) in enumerate(_other_chips(x, y)):
                for s, rows in chunks:
                    cp = pltpu.make_async_remote_copy(
                        src_ref=ins[k].at[pj, rows, :], dst_ref=outs[k].at[me, rows, :], send_sem=send_sems.at[k, d, s],
                        recv_sem=recv_sems.at[k, d, s], device_id=(px, py, c), device_id_type=MESH)
                    cp.start()
                    sends.append(cp)
        for k in range(n_arr):
            chunks, _ = _row_chunks(ins[k].shape[1], ICI_SPLIT)
            for d, (px, py, pj) in enumerate(_other_chips(x, y)):
                for s, rows in chunks:
                    pltpu.make_async_remote_copy(
                        src_ref=ins[k].at[pj, rows, :], dst_ref=outs[k].at[pj, rows, :], send_sem=send_sems.at[k, d, s],
                        recv_sem=recv_sems.at[k, d, s], device_id=(px, py, c), device_id_type=MESH).wait_recv()
        for cp in sends:
            cp.wait_send()

    sem = pltpu.SemaphoreType.DMA((n_arr, 3, ICI_SPLIT))
    return pl.pallas_call(
        body, name="chip_exchange", in_specs=[ANY] * n_arr, out_specs=[ANY] * n_arr,
        out_shape=[jax.ShapeDtypeStruct(p.shape, p.dtype) for p in parts],
        scratch_shapes=[sem, sem],
    )(*parts)


class _SiblingHalvesRider:
    def __init__(self, grads):
        self.inputs = list(grads)
        self.out_shapes = [jax.ShapeDtypeStruct((g.shape[0], g.shape[1] // 2, g.shape[2]), F32) for g in grads]
        self.aliases = {}
        self.sems = [pltpu.SemaphoreType.DMA((len(grads), D2D_SPLIT))] * 2

    def _copies(self, ins, outs, sems):
        x, y, c = _place()
        for k in range(len(ins)):
            half = ins[k].shape[1] // 2
            chunks, size = _row_chunks(half, D2D_SPLIT)
            for s, dst_rows in chunks:
                yield pltpu.make_async_remote_copy(
                    src_ref=ins[k].at[:, pl.ds((1 - c) * half + s * size, size), :], dst_ref=outs[k].at[:, dst_rows, :],
                    send_sem=sems[0].at[k, s], recv_sem=sems[1].at[k, s], device_id=(x, y, 1 - c), device_id_type=MESH)

    def first(self, ins, outs, sems):
        for cp in self._copies(ins, outs, sems):
            cp.start()

    def last(self, ins, outs, sems):
        for cp in self._copies(ins, outs, sems):
            cp.wait()


class _ChipExchangeRider:
    def __init__(self, parts):
        self.inputs = list(parts)
        self.out_shapes = [jax.ShapeDtypeStruct(p.shape, p.dtype) for p in parts]
        self.aliases = {}
        self.sems = [pltpu.SemaphoreType.DMA((len(parts), 3, ICI_SPLIT))] * 2

    def _copies(self, ins, outs, sems, receiving):
        x, y, c = _place()
        for k in range(len(ins)):
            chunks, _ = _row_chunks(ins[k].shape[1], ICI_SPLIT)
            for d, (px, py, pj) in enumerate(_other_chips(x, y)):
                for s, rows in chunks:
                    yield pltpu.make_async_remote_copy(
                        src_ref=ins[k].at[pj, rows, :], dst_ref=outs[k].at[pj if receiving else 2 * x + y, rows, :],
                        send_sem=sems[0].at[k, d, s], recv_sem=sems[1].at[k, d, s], device_id=(px, py, c),
                        device_id_type=MESH)

    def first(self, ins, outs, sems):
        for cp in self._copies(ins, outs, sems, False):
            cp.start()

    def last(self, ins, outs, sems):
        for cp in self._copies(ins, outs, sems, True):
            cp.wait_recv()
        for cp in self._copies(ins, outs, sems, False):
            cp.wait_send()


def _sibling_join(bufs):
    n_arr = len(bufs)

    def body(*refs):
        bufs_out = refs[n_arr:2 * n_arr]
        send_sems, recv_sems = refs[2 * n_arr:]
        x, y, c = _place()
        copies = []
        for k in range(n_arr):
            half = bufs_out[k].shape[0] // 2
            chunks, size = _row_chunks(half, D2D_SPLIT)
            for s, _ in chunks:
                rows = pl.ds(c * half + s * size, size)
                give = pltpu.make_async_remote_copy(
                    src_ref=bufs_out[k].at[rows, :], dst_ref=bufs_out[k].at[rows, :], send_sem=send_sems.at[k, s],
                    recv_sem=recv_sems.at[k, s], device_id=(x, y, 1 - c), device_id_type=MESH)
                give.start()
                copies.append((k, s, half, size, give))
        for k, s, half, size, give in copies:
            rows = pl.ds((1 - c) * half + s * size, size)
            pltpu.make_async_remote_copy(
                src_ref=bufs_out[k].at[rows, :], dst_ref=bufs_out[k].at[rows, :], send_sem=send_sems.at[k, s],
                recv_sem=recv_sems.at[k, s], device_id=(x, y, 1 - c), device_id_type=MESH).wait_recv()
            give.wait_send()

    sem = pltpu.SemaphoreType.DMA((n_arr, D2D_SPLIT))
    return pl.pallas_call(
        body, name="sibling_join", in_specs=[ANY] * n_arr, out_specs=[ANY] * n_arr,
        out_shape=[jax.ShapeDtypeStruct(b.shape, F32) for b in bufs],
        input_output_aliases={k: k for k in range(n_arr)},
        scratch_shapes=[sem, sem],
    )(*bufs)


PACK_ROWS = 48


def _small_allreduce(pack):
    masks = [(dx, dy, dc) for dx in (0, 1) for dy in (0, 1) for dc in (0, 1)][1:]

    def body(p_ref, o_ref, buf, send_sems, recv_sems):
        x, y, c = _place()
        me = 4 * x + 2 * y + c
        buf[me] = p_ref[...]
        sends = []
        for k, (dx, dy, dc) in enumerate(masks):
            peer = (1 - x if dx else x, 1 - y if dy else y, 1 - c if dc else c)
            cp = pltpu.make_async_remote_copy(
                src_ref=p_ref, dst_ref=buf.at[me], send_sem=send_sems.at[k], recv_sem=recv_sems.at[k],
                device_id=peer, device_id_type=MESH)
            cp.start()
            sends.append(cp)
        for k, (dx, dy, dc) in enumerate(masks):
            peer = (1 - x if dx else x, 1 - y if dy else y, 1 - c if dc else c)
            pj = 4 * peer[0] + 2 * peer[1] + peer[2]
            pltpu.make_async_remote_copy(
                src_ref=p_ref, dst_ref=buf.at[pj], send_sem=send_sems.at[k], recv_sem=recv_sems.at[k],
                device_id=peer, device_id_type=MESH).wait_recv()
        for cp in sends:
            cp.wait_send()
        tot = buf[0]
        for k in range(1, 8):
            tot = tot + buf[k]
        o_ref[...] = tot
        o_ref[0:N_META, :] = tot[0:N_META] + tot[N_META:2 * N_META]

    return pl.pallas_call(
        body, name="small_allreduce", in_specs=[VM], out_specs=VM,
        out_shape=jax.ShapeDtypeStruct((PACK_ROWS, D_MODEL), F32),
        scratch_shapes=[pltpu.VMEM((8, PACK_ROWS, D_MODEL), F32), pltpu.SemaphoreType.DMA((7,)),
                        pltpu.SemaphoreType.DMA((7,))],
    )(pack)


def _pad_lanes(vec, offset):
    k = vec.shape[1]
    return jnp.concatenate([jnp.zeros((1, offset), F32), vec, jnp.zeros((1, LANE - offset - k), F32)], axis=1)


def _local_step(x, tgt, meta, norm1_g, wp, conv_w, a_log, dt_bias, dn_norm_g, gla_w2, gla_b, gla_norm_g,
                w_out, norm2_g, w_up, w_down, final_norm_g, late_gather=None, where=None, early_gather=None):
    bsz, s_len, d = x.shape
    t_seq = s_len + CHUNK
    nc_seq = t_seq // CHUNK
    n = bsz * t_seq
    tr = _tile(t_seq, 832)
    tt = _tile(t_seq, 416)

    lead = jnp.concatenate([jnp.zeros((N_PAD, d), F32), meta], axis=0)
    alog_row = _pad_lanes(a_log, 4)
    dtb_row = _pad_lanes(dt_bias, 4)

    h0, h, got_early = _embed_norm(x, lead, norm1_g, tr=tr, name="embed_norm1", rider=early_gather)
    if early_gather is not None:
        wp = _padded_from_shards(got_early[0])
        conv_w = got_early[1].transpose(1, 0, 2).reshape(4, QKV_W)
        gla_w2 = got_early[2][:, :, 0:GQ_W // N_CHIPS].transpose(1, 0, 2).reshape(GLA_RANK, GQ_W)
    w2p = jnp.concatenate([gla_w2, jnp.zeros((LANE - GLA_RANK, GQ_W), F32)], axis=0)
    ride_up, ride_down, ride_out = late_gather if late_gather is not None else (None, None, None)
    projp, gates = _mm(h, wp, "nn", tm=tt, tn=PW, tk=d, out_dtypes=(BF16, F32), out_widths=(PW, PW - C_SA),
                       epilogue=lambda acc: (acc, acc[:, C_SA:PW]), name="in_proj")
    qn, kn, v, got_up = _dnprep_fwd(projp, conv_w, bsz=bsz, t_seq=t_seq, tt=tt, name="dn_prep", rider=ride_up)
    u, w, qg, kd, pmat, tmat, gl, got_down = _dn_intra_fwd(qn, kn, v, gates, alog_row, dtb_row, nc_seq=nc_seq,
                                                           name="dn_intra", rider=ride_down)
    o_dn, vn, hist, got_out = _dn_scan_fwd(u, w, qg, kd, pmat, gl, bsz=bsz, nc_seq=nc_seq, name="dn_scan",
                                           rider=ride_out)
    oi, gqg, gkd, ggl = _gla_intra_fwd(projp, gates, w2p, gla_b, nc_seq=nc_seq, name="gla_intra")
    o_gla, ghist, _ = _gla_scan_fwd(oi, gqg, gkd, ggl, projp, bsz=bsz, nc_seq=nc_seq, name="gla_scan")
    if late_gather is not None:
        w_out = got_out[0].reshape(d, d)
        w_up = got_up[0].transpose(1, 0, 2).reshape(d, D_FF)
        w_down = got_down[0].reshape(D_FF, d)
    mix = _gnorm_fwd(o_dn, o_gla, projp, dn_norm_g, gla_norm_g, tr=tr, name="gated_norm")
    def residual_norm(acc, res, g):
        x1v = res + acc
        r = lax.rsqrt(jnp.mean(x1v * x1v, axis=-1, keepdims=True) + EPS)
        return x1v, x1v * r * g

    x1, h2 = _mm(mix, w_out, "nn", tm=tr, tn=d, tk=d, out_dtypes=(F32, BF16), extras=(h0,), vec_extras=(norm2_g,),
                 epilogue=residual_norm, name="out_proj_norm2")

    (act,) = _mm(h2, w_up, "nn", tm=tt, tn=D_FF, tk=d, out_dtypes=(BF16,),
                 epilogue=lambda acc: (jnp.square(jnp.maximum(acc, 0.0)),), name="mlp_up", n_chunk=1024)
    dx2, dx2b, d_final_g, loss_tile = _mlp_down_loss(act, w_down, x1, final_norm_g, tgt, t_seq=t_seq, tr=tt,
                                                     name="mlp_down_loss")

    (dup,) = _mm(dx2b, w_down, "nt", tm=tt, tn=D_FF, tk=d, out_dtypes=(BF16,), extras=(act,),
                 epilogue=lambda acc, a: (acc * (2.0 * jnp.sqrt(a.astype(F32))),), name="mlp_down_bwd", n_chunk=1024)
    tk2 = 2 * tr if n % (2 * tr) == 0 else tr
    (d_w_down,) = _mm(act, dx2b, "tn", tm=D_FF // 2, tn=d, tk=tk2, out_dtypes=(F32,), name="w_down_grad")
    (d_w_up_sm,) = _mm(h2, dup, "tn", tm=d, tn=D_FF // 2, tk=tk2, out_dtypes=(F32,), name="w_up_grad",
                       shard_cols=D_FF // N_CHIPS)
    mlp_sm = [d_w_up_sm, d_w_down.reshape(N_CHIPS, D_FF // N_CHIPS, d)]
    ride1 = _SiblingHalvesRider(mlp_sm) if where is not None else None
    dx1, dx1b, d_norm2_g, theirs = _mlp_up_bwd_norm(dup, w_up, x1, norm2_g, dx2, tr=tt, name="mlp_up_bwd_norm",
                                                    rider=ride1)
    ride2 = None
    if where is not None:
        mlp_pair = [_pair_sum(where, a, b, name=f"pair_sum_mlp{k}") for k, (a, b) in enumerate(zip(mlp_sm, theirs))]
        ride2 = _ChipExchangeRider(mlp_pair)

    (dmix,) = _mm(dx1b, w_out, "nt", tm=tr, tn=d, tk=d, out_dtypes=(BF16,), name="out_proj_bwd")
    (d_w_out,) = _mm(mix, dx1b, "tn", tm=d, tn=d, tk=tr, out_dtypes=(F32,), name="w_out_grad")
    do_dn, ddz, do_gla, dgr, d_dn_norm_g, d_gla_norm_g = _gnorm_bwd(
        dmix, o_dn, o_gla, projp, dn_norm_g, gla_norm_g, tr=tr, name="gated_norm_bwd")
    du, dw, dqg, dkd, dgl = _dn_scan_bwd(do_dn, w, qg, kd, vn, pmat, gl, hist, bsz=bsz, nc_seq=nc_seq,
                                          name="dn_scan_bwd")
    dqn, dkn, dv, dsa, d_alog, d_dtb, mlp_parts = _dn_intra_bwd(
        qn, kn, v, gates, alog_row, dtb_row, u, w, tmat, du, dw, dqg, dkd, do_dn, vn, dgl, nc_seq=nc_seq,
        name="dn_intra_bwd", rider=ride2)
    dz, d_conv_w = _dnprep_bwd_a(projp, conv_w, dqn, dkn, dv, bsz=bsz, t_seq=t_seq, tt=tt, name="dn_prep_bwd")
    dcin = _dnprep_bwd_b(dz, conv_w, bsz=bsz, t_seq=t_seq, tt=tt, name="conv_bwd")
    gdqg, gdkd, gdvi, gdgl = _gla_scan_bwd(do_gla, gqg, gkd, ggl, projp, ghist, bsz=bsz, nc_seq=nc_seq,
                                            name="gla_scan_bwd")
    dgqk, dgv, dsb, d_w2p, d_gla_b = _gla_intra_bwd(projp, gates, w2p, gla_b, do_gla, gdqg, gdkd, gdvi, gdgl,
                                                    nc_seq=nc_seq, name="gla_intra_bwd")

    secs = (dcin, ddz, dgqk, dgv, dgr, dsa, dsb)
    g_lo = _grad_tn(h, secs[0:2], tk=tk2, name="w_in_grad_lo")
    g_hi = _grad_tn(h, secs[2:7], tk=tk2, name="w_in_grad_hi")
    ride3 = None
    if where is not None:
        late_sm = [_shards_from_padded(g_lo, g_hi), d_w_out.reshape(N_CHIPS, d // N_CHIPS, d)]
        late_theirs = _exchange_now(_SiblingHalvesRider(late_sm), name="sibling_halves")
        late_pair = [_pair_sum(where, a, b, name=f"pair_sum_{k}") for k, (a, b) in enumerate(zip(late_sm, late_theirs))]
        ride3 = _ChipExchangeRider(late_pair)
    grad_x, d_meta_rows, d_norm1_g, late_parts = _inproj_bwd(secs, wp, h0, norm1_g, dx1, bsz=bsz, t_seq=t_seq, tr=tt,
                                                             name="in_proj_bwd", rider=ride3)

    grads = dict(w_in_lo=g_lo, w_in_hi=g_hi, w_out=d_w_out, w_up_shards=d_w_up_sm, w_down=d_w_down, meta_rows=d_meta_rows,
                 norm1_g=d_norm1_g, conv_w=d_conv_w, a_log_tile=d_alog, dt_bias_tile=d_dtb, dn_norm_g=d_dn_norm_g,
                 gla_w2=d_w2p[0:GLA_RANK], gla_b=d_gla_b, gla_norm_g=d_gla_norm_g, norm2_g=d_norm2_g,
                 final_norm_g=d_final_g, loss_tile=loss_tile)
    if where is not None:
        grads["exchanged"] = (late_pair + mlp_pair, list(late_parts) + list(mlp_parts))
    return grad_x, grads


SHARD_W = IN_WIDTH // N_CHIPS
PADDED_ORDER = ((0, 2048), (2056, 3592), (2048, 2056), LANE - 8, (3592, 3608), LANE - GLA_RANK)


def _pad_layout(w_full):
    pieces = [jnp.zeros((w_full.shape[0], seg), w_full.dtype) if isinstance(seg, int) else w_full[:, seg[0]:seg[1]]
              for seg in PADDED_ORDER]
    return jnp.concatenate(pieces, axis=1)


def _padded_from_shards(stack):
    pieces = []
    for seg in PADDED_ORDER:
        if isinstance(seg, int):
            pieces.append(jnp.zeros((stack.shape[1], seg), stack.dtype))
            continue
        for j in range(N_CHIPS):
            lo, hi = max(seg[0], j * SHARD_W), min(seg[1], (j + 1) * SHARD_W)
            if lo < hi:
                pieces.append(stack[j, :, lo - j * SHARD_W:hi - j * SHARD_W])
    return jnp.concatenate(pieces, axis=1)


def _shards_from_padded(g_lo, g_hi):
    split = g_lo.shape[1]
    starts, pos = [], 0
    for seg in PADDED_ORDER:
        width = seg if isinstance(seg, int) else seg[1] - seg[0]
        if not isinstance(seg, int):
            starts.append((seg[0], seg[1], pos))
        pos += width
    shards = []
    for j in range(N_CHIPS):
        pieces = []
        for a, b, p0 in sorted(starts):
            lo, hi = max(a, j * SHARD_W), min(b, (j + 1) * SHARD_W)
            if lo < hi:
                src, off = (g_lo, 0) if p0 < split else (g_hi, split)
                pieces.append(src[:, p0 + lo - a - off:p0 + hi - a - off])
        pieces.append(jnp.zeros((g_lo.shape[0], D_MODEL - SHARD_W), g_lo.dtype))
        shards.append(jnp.concatenate(pieces, axis=1))
    return jnp.stack(shards)


def _pack_small(g, bsz):
    assert bsz * N_META == 32
    row = jnp.concatenate([g["a_log_tile"], g["dt_bias_tile"], g["dn_norm_g"], g["gla_norm_g"], g["gla_b"],
                           g["loss_tile"], jnp.zeros((1, LANE), F32)], axis=1)
    return jnp.concatenate([g["meta_rows"], g["norm1_g"], g["conv_w"].reshape(6, D_MODEL), row,
                            g["gla_w2"].reshape(4, D_MODEL), g["norm2_g"], g["final_norm_g"],
                            jnp.zeros((2, D_MODEL), F32)], axis=0)


def kernel(x, meta_tokens, norm1_g, w_in, conv_w, a_log, dt_bias, dn_norm_g, gla_w2, gla_b, gla_norm_g, w_out, norm2_g, w_up, w_down, final_norm_g, loss_target, m_meta_tokens, m_norm1_g, m_w_in, m_conv_w, m_a_log, m_dt_bias, m_dn_norm_g, m_gla_w2, m_gla_b, m_gla_norm_g, m_w_out, m_norm2_g, m_w_up, m_w_down, m_final_norm_g, v_meta_tokens, v_norm1_g, v_w_in, v_conv_w, v_a_log, v_dt_bias, v_dn_norm_g, v_gla_w2, v_gla_b, v_gla_norm_g, v_w_out, v_norm2_g, v_w_up, v_w_down, v_final_norm_g):
    bsz = x.shape[0]
    chip = 2 * lax.axis_index("x") + lax.axis_index("y")

    lane_pad = lambda a, wd: jnp.pad(a, ((0, 0), (0, wd - a.shape[1])))
    where = jnp.stack([lax.axis_index("c"), chip]).astype(jnp.int32)
    slot = lambda a, dt, nm: _to_slot(where, a, dt, name="slot_" + nm)
    (g_meta,) = _exchange_now(_GatherRider([slot(meta_tokens, F32, "meta")], [False]), name="gather_meta")
    early = _GatherRider([slot(lane_pad(w_in[0], D_MODEL), BF16, "w_in"), slot(conv_w[0], F32, "conv"),
                          slot(lane_pad(gla_w2[0], LANE), F32, "gla_w2")], [True, False, False])
    late = (_GatherRider([slot(w_up[0], BF16, "w_up")], [True]), _GatherRider([slot(w_down[0], BF16, "w_down")], [True]),
            _GatherRider([slot(w_out[0], BF16, "w_out")], [True]))
    meta_f = g_meta.transpose(1, 0, 2).reshape(N_META, D_MODEL)

    grad_x, g = _local_step(x, loss_target, meta_f, norm1_g, None, None, a_log, dt_bias, dn_norm_g, None, gla_b,
                            gla_norm_g, None, norm2_g, None, None, final_norm_g.reshape(1, D_MODEL), late_gather=late,
                            where=where, early_gather=early)

    pair, parts = g["exchanged"]
    halves = [_chip_sum(where, p, q, name=f"chip_sum_{k}") for k, (p, q) in enumerate(zip(pair, parts))]
    gw_in, gw_out, gw_up, gw_down = _sibling_join(halves)

    red = _small_allreduce(_pack_small(g, bsz))
    g_meta_full = red[0:N_META]
    g_norm1 = red[32:33]
    g_conv_full = red[33:39].reshape(4, QKV_W)
    srow = red[39:40]
    g_alog, g_dtb = srow[:, 4:8], srow[:, LANE + 4:LANE + 8]
    g_dn_norm, g_gla_norm = srow[:, 2 * LANE:3 * LANE], srow[:, 3 * LANE:4 * LANE]
    g_gla_b = srow[:, 4 * LANE:6 * LANE]
    loss = srow[0, 6 * LANE]
    g_w2_full = red[40:44].reshape(GLA_RANK, GQ_W)
    g_norm2 = red[44:45]
    g_final = red[45:46]
    g_meta_sh = lax.dynamic_slice_in_dim(g_meta_full, chip * (D_MODEL // N_CHIPS), D_MODEL // N_CHIPS, axis=1)
    g_conv_sh = lax.dynamic_slice_in_dim(g_conv_full, chip * (QKV_W // N_CHIPS), QKV_W // N_CHIPS, axis=1)
    g_w2_sh = lax.dynamic_slice_in_dim(g_w2_full, chip * (GQ_W // N_CHIPS), GQ_W // N_CHIPS, axis=1)

    names = ["meta_tokens", "norm1_g", "w_in", "conv_w", "a_log", "dt_bias", "dn_norm_g", "gla_w2", "gla_b",
             "gla_norm_g", "w_out", "norm2_g", "w_up", "w_down", "final_norm_g"]
    weights = dict(meta_tokens=meta_tokens, norm1_g=norm1_g, w_in=w_in, conv_w=conv_w, a_log=a_log, dt_bias=dt_bias,
                   dn_norm_g=dn_norm_g, gla_w2=gla_w2, gla_b=gla_b, gla_norm_g=gla_norm_g, w_out=w_out,
                   norm2_g=norm2_g, w_up=w_up, w_down=w_down, final_norm_g=final_norm_g)
    ms = dict(meta_tokens=m_meta_tokens, norm1_g=m_norm1_g, w_in=m_w_in, conv_w=m_conv_w, a_log=m_a_log,
              dt_bias=m_dt_bias, dn_norm_g=m_dn_norm_g, gla_w2=m_gla_w2, gla_b=m_gla_b, gla_norm_g=m_gla_norm_g,
              w_out=m_w_out, norm2_g=m_norm2_g, w_up=m_w_up, w_down=m_w_down, final_norm_g=m_final_norm_g)
    vs = dict(meta_tokens=v_meta_tokens, norm1_g=v_norm1_g, w_in=v_w_in, conv_w=v_conv_w, a_log=v_a_log,
              dt_bias=v_dt_bias, dn_norm_g=v_dn_norm_g, gla_w2=v_gla_w2, gla_b=v_gla_b, gla_norm_g=v_gla_norm_g,
              w_out=v_w_out, norm2_g=v_norm2_g, w_up=v_w_up, w_down=v_w_down, final_norm_g=v_final_norm_g)
    grads2d = dict(meta_tokens=g_meta_sh, norm1_g=g_norm1, w_in=gw_in, conv_w=g_conv_sh, a_log=g_alog, dt_bias=g_dtb,
                   dn_norm_g=g_dn_norm, gla_w2=g_w2_sh, gla_b=g_gla_b, gla_norm_g=g_gla_norm, w_out=gw_out,
                   norm2_g=g_norm2, w_up=gw_up, w_down=gw_down, final_norm_g=g_final)
    out_g, out_d, out_m, out_v = [], [], [], []
    for nm in names:
        shape = weights[nm].shape
        g2 = grads2d[nm]
        if nm == "w_in":
            tview = lambda a: jnp.transpose(a, (2, 0, 1))
            res = _adamw_rows(tview(weights[nm]), g2[:, 0:SHARD_W].T.reshape(SHARD_W, 1, D_MODEL), tview(ms[nm]),
                              tview(vs[nm]), name=f"adamw_{nm}")
            res = [jnp.transpose(a, (1, 2, 0)) for a in res]
            gout = res[3]
        elif len(shape) == 3:
            res = _adamw(weights[nm], g2, ms[nm], vs[nm], name=f"adamw_{nm}")
            gout = g2.reshape(shape)
        else:
            as2d = lambda a: a.reshape(g2.shape)
            res = _adamw(as2d(weights[nm]), g2, as2d(ms[nm]), as2d(vs[nm]), name=f"adamw_{nm}")
            gout = g2.reshape(shape)
        out_g.append(gout)
        out_d.append(res[0].reshape(shape))
        out_m.append(res[1].reshape(shape))
        out_v.append(res[2].reshape(shape))
    return (loss, grad_x, *out_g, *out_d, *out_m, *out_v)
```

```python
import functools

import jax
import jax.numpy as jnp
import numpy as np
from jax import lax
from jax.experimental import pallas as pl
from jax.experimental.pallas import tpu as pltpu

F32 = jnp.float32
BF16 = jnp.bfloat16
HI = lax.Precision.HIGHEST
MESH = pl.DeviceIdType.MESH

D_MODEL = 1024
N_META = 16
CHUNK = 64
N_PAD = CHUNK - N_META
NH = 4
DN_D = 128
GLA_DK = 64
GLA_DV = 128
GLA_RANK = 16
D_FF = 4 * D_MODEL
EPS = 1e-6
F32_TINY = 1.1754944e-38
IN_WIDTH = 3608
C_QKV, C_DZ, C_GQK, C_GV, C_GR, C_SA, C_SB, PW = 0, 1536, 2048, 2560, 3072, 3584, 3712, 3840
LANE = 128
N_CHIPS = 4

ADAM_LR, ADAM_B1, ADAM_B2, ADAM_EPS, ADAM_WD, ADAM_STEP = 0.001, 0.9, 0.999, 1e-08, 0.01, 10

VMEM_BIG = 56 * 1024 * 1024


def _cp(vmem=None, sem=None):
    kw = {}
    if vmem is not None:
        kw["vmem_limit_bytes"] = vmem
    if sem is not None:
        kw["dimension_semantics"] = sem
    return pltpu.CompilerParams(**kw)


def _tile(n, target, mult=16):
    best = None
    for t in range(mult, min(n, target) + 1, mult):
        if n % t == 0:
            best = t
    assert best is not None, (n, target)
    return best


def _dot(a, b, dims, prec=None):
    return lax.dot_general(a, b, (dims, ((), ())), preferred_element_type=F32, precision=prec)


def _nn(a, b):
    return _dot(a.astype(BF16), b.astype(BF16), ((1,), (0,)))


def _nt(a, b):
    return _dot(a.astype(BF16), b.astype(BF16), ((1,), (1,)))


def _tn(a, b):
    return _dot(a.astype(BF16), b.astype(BF16), ((0,), (0,)))


def _split(x):
    hi = x.astype(BF16)
    return hi, (x - hi.astype(F32)).astype(BF16)


def _tri_sum(tri, x):
    t = tri.astype(BF16)
    hi = x.astype(BF16)
    r1 = x - hi.astype(F32)
    mid = r1.astype(BF16)
    lo = (r1 - mid.astype(F32)).astype(BF16)
    nn = ((1,), (0,))
    return _dot(t, hi, nn) + _dot(t, mid, nn) + _dot(t, lo, nn)


def _sigmoid(x):
    return 0.5 * jnp.tanh(0.5 * x) + 0.5


def _softplus(x):
    return jnp.maximum(x, 0.0) + jnp.log(1.0 + jnp.exp(-jnp.abs(x)))


def _logsigmoid(x):
    return -_softplus(-x)


def _iota2(shape, dim):
    return lax.broadcasted_iota(jnp.int32, shape, dim)


def _mm(a, b, mode, *, tm, tn, tk, out_dtypes, extras=(), epilogue=None, name, vmem=VMEM_BIG, rider=None,
        out_widths=None, n_chunk=None, vec_extras=(), shard_cols=None):
    if mode == "tn":
        K, M = a.shape
    else:
        M, K = a.shape
    N = b.shape[0] if mode == "nt" else b.shape[1]
    assert M % tm == 0 and N % tn == 0 and K % tk == 0, (name, M, N, K, tm, tn, tk)
    nk = K // tk
    n_ex, n_out, n_vec = len(extras), len(out_dtypes), len(vec_extras)
    if mode == "tn":
        a_spec = pl.BlockSpec((tk, tm), lambda i, j, k: (k, i))
    else:
        a_spec = pl.BlockSpec((tm, tk), lambda i, j, k: (i, k))
    if mode == "nt":
        b_spec = pl.BlockSpec((tn, tk), lambda i, j, k: (j, k))
    else:
        b_spec = pl.BlockSpec((tk, tn), lambda i, j, k: (k, j))
    mn_spec = pl.BlockSpec((tm, tn), lambda i, j, k: (i, j))
    if out_widths is None:
        o_specs = [mn_spec] * n_out
        o_shapes = [jax.ShapeDtypeStruct((M, N), dt) for dt in out_dtypes]
    else:
        assert tn == N
        o_specs = [pl.BlockSpec((tm, wd), lambda i, j, k: (i, 0)) for wd in out_widths]
        o_shapes = [jax.ShapeDtypeStruct((M, wd), dt) for wd, dt in zip(out_widths, out_dtypes)]
    if shard_cols is not None:
        o_specs = [pl.BlockSpec((tn // shard_cols, tm, shard_cols), lambda i, j, k: (j, i, 0))]
        o_shapes = [jax.ShapeDtypeStruct((N // shard_cols, M, shard_cols), F32)]
    dims = {"nn": ((1,), (0,)), "nt": ((1,), (1,)), "tn": ((0,), (0,))}[mode]

    single = nk == 1
    direct = (not single) and epilogue is None and n_out == 1 and out_dtypes[0] == F32

    def body(*refs):
        a_ref, b_ref = refs[0], refs[1]
        ex_refs = refs[2:2 + n_ex]
        vec_refs = refs[2 + n_ex:2 + n_ex + n_vec]
        out_refs = refs[2 + n_ex + n_vec:2 + n_ex + n_vec + n_out]
        if n_chunk is not None:
            assert single and out_widths is None and mode != "tn" and tn % n_chunk == 0
            av = a_ref[...].astype(BF16)
            for j in range(tn // n_chunk):
                cols = slice(j * n_chunk, (j + 1) * n_chunk)
                bv = b_ref[cols, :] if mode == "nt" else b_ref[:, cols]
                acc = _dot(av, bv.astype(BF16), dims)
                res = (acc,) if epilogue is None else epilogue(acc, *[e[:, cols] for e in ex_refs])
                for o_ref, r in zip(out_refs, res):
                    o_ref[:, cols] = r.astype(o_ref.dtype)
            return
        part = _dot(a_ref[...].astype(BF16), b_ref[...].astype(BF16), dims)

        def finish(acc):
            res = (acc,) if epilogue is None else epilogue(acc, *[e[...] for e in ex_refs], *[v[...] for v in vec_refs])
            for o_ref, r in zip(out_refs, res):
                o_ref[...] = r.astype(o_ref.dtype)

        if single:
            if shard_cols is not None:
                for sh in range(tn // shard_cols):
                    out_refs[0][sh] = part[:, sh * shard_cols:(sh + 1) * shard_cols]
            else:
                finish(part)
            return
        acc_ref = out_refs[0] if direct else refs[2 + n_ex + n_vec + n_out]
        k = pl.program_id(2)
        if shard_cols is not None:
            assert direct
            for sh in range(tn // shard_cols):
                piece = part[:, sh * shard_cols:(sh + 1) * shard_cols]

                @pl.when(k == 0)
                def _():
                    acc_ref[sh] = piece

                @pl.when(k > 0)
                def _():
                    acc_ref[sh] += piece
            return

        @pl.when(k == 0)
        def _():
            acc_ref[...] = part

        @pl.when(k > 0)
        def _():
            acc_ref[...] += part

        if not direct:
            @pl.when(k == nk - 1)
            def _():
                finish(acc_ref[...])

    outs, ridden = _hosted_call(
        body, rider, name=name, grid=(M // tm, N // tn, nk),
        in_specs=[a_spec, b_spec] + [mn_spec] * n_ex + [pl.BlockSpec((1, tn), lambda i, j, k: (0, j))] * n_vec,
        out_specs=o_specs, out_shape=o_shapes,
        scratch_shapes=[] if (single or direct) else [pltpu.VMEM((tm, tn), F32)],
        compiler_params=_cp(vmem, ("parallel", "parallel", "arbitrary")), args=(a, b, *extras, *vec_extras))
    return tuple(outs) if rider is None else (tuple(outs), ridden)


def _grad_tn(a, secs, *, tk, name):
    kk, m = a.shape
    widths = [s.shape[1] for s in secs]
    total = sum(widths)
    nk = kk // tk

    def body(*refs):
        a_ref, sec_refs, o_ref = refs[0], refs[1:-1], refs[-1]
        cat = sec_refs[0][...] if len(sec_refs) == 1 else jnp.concatenate([s[...] for s in sec_refs], axis=1)
        part = _dot(a_ref[...].astype(BF16), cat.astype(BF16), ((0,), (0,)))
        k = pl.program_id(0)

        @pl.when(k == 0)
        def _():
            o_ref[...] = part

        @pl.when(k > 0)
        def _():
            o_ref[...] += part

    return pl.pallas_call(
        body, name=name, grid=(nk,),
        in_specs=[pl.BlockSpec((tk, m), lambda k: (k, 0))] + [pl.BlockSpec((tk, w), lambda k: (k, 0)) for w in widths],
        out_specs=pl.BlockSpec((m, total), lambda k: (0, 0)),
        out_shape=jax.ShapeDtypeStruct((m, total), F32),
        compiler_params=_cp(VMEM_BIG, ("arbitrary",)),
    )(a, *secs)


class _ShiftedRows:
    def __init__(self, src, buf, sems, *, per_seq, tt, steps):
        self.src, self.buf, self.sems = src, buf, sems
        self.per_seq, self.tt, self.steps = per_seq, tt, steps

    def _do(self, step, slot, act):
        b, j = step // self.per_seq, step % self.per_seq
        tt = self.tt

        @pl.when(j == 0)
        def _():
            act(pltpu.make_async_copy(self.src.at[b, pl.ds(0, tt - CHUNK), :],
                                      self.buf.at[slot, pl.ds(CHUNK, tt - CHUNK), :], self.sems.at[slot]))

        if self.per_seq > 1:
            @pl.when(j > 0)
            def _():
                act(pltpu.make_async_copy(self.src.at[b, pl.ds(j * tt - CHUNK, tt), :], self.buf.at[slot],
                                          self.sems.at[slot]))

    def tile(self, i):
        slot = i % 2

        @pl.when(i == 0)
        def _():
            self._do(i, slot, lambda cp: cp.start())

        self._do(i, slot, lambda cp: cp.wait())

        @pl.when(i + 1 < self.steps)
        def _():
            self._do(i + 1, 1 - slot, lambda cp: cp.start())

        return slot


def _embed_norm(x, lead, g, *, tr, name, rider=None):
    bsz, s_len, d = x.shape
    t_seq = s_len + CHUNK
    per_seq = t_seq // tr
    steps = bsz * per_seq
    n = bsz * t_seq

    def body(x_hbm, lead_ref, g_ref, h0_ref, h_ref, buf, sems):
        i = pl.program_id(0)
        slot = _ShiftedRows(x_hbm, buf, sems, per_seq=per_seq, tt=tr, steps=steps).tile(i)

        @pl.when(i % per_seq == 0)
        def _():
            buf[slot, 0:CHUNK, :] = lead_ref[...]

        xv = buf[slot]
        h0_ref[...] = xv
        r = lax.rsqrt(jnp.mean(xv * xv, axis=-1, keepdims=True) + EPS)
        h_ref[...] = (xv * r * g_ref[...]).astype(h_ref.dtype)

    row = pl.BlockSpec((tr, d), lambda i: (i, 0))
    outs, ridden = _hosted_call(
        body, rider, name=name, grid=(steps,),
        in_specs=[ANY, pl.BlockSpec((CHUNK, d), lambda i: (0, 0)), pl.BlockSpec((1, d), lambda i: (0, 0))],
        out_specs=[row, row],
        out_shape=[jax.ShapeDtypeStruct((n, d), F32), jax.ShapeDtypeStruct((n, d), BF16)],
        scratch_shapes=[pltpu.VMEM((2, tr, d), F32), pltpu.SemaphoreType.DMA((2,))],
        compiler_params=_cp(VMEM_BIG, ("arbitrary",)), args=(x, lead, g))
    return (*outs, ridden)


def _rms_fwd(x, g, *, tr, name):
    n, d = x.shape

    def body(x_ref, g_ref, o_ref):
        xv = x_ref[...]
        r = lax.rsqrt(jnp.mean(xv * xv, axis=-1, keepdims=True) + EPS)
        o_ref[...] = (xv * r * g_ref[...]).astype(o_ref.dtype)

    return pl.pallas_call(
        body, name=name, grid=(n // tr,),
        in_specs=[pl.BlockSpec((tr, d), lambda i: (i, 0)), pl.BlockSpec((1, d), lambda i: (0, 0))],
        out_specs=pl.BlockSpec((tr, d), lambda i: (i, 0)),
        out_shape=jax.ShapeDtypeStruct((n, d), BF16),
        compiler_params=_cp(VMEM_BIG),
    )(x, g)


def _rms_bwd_math(xv, g, dy):
    r = lax.rsqrt(jnp.mean(xv * xv, axis=-1, keepdims=True) + EPS)
    xh = xv * r
    gdy = dy * g
    dx = r * (gdy - xh * jnp.mean(xh * gdy, axis=-1, keepdims=True))
    return dx, jnp.sum(dy * xh, axis=0, keepdims=True)


def _mlp_up_bwd_norm(dup, w_up, x, g, res, *, tr, name, rider=None):
    n, d = x.shape
    ff = dup.shape[1]

    def body(dup_ref, w_ref, x_ref, g_ref, res_ref, o_ref, ob_ref, dg_ref):
        dh = _nt(dup_ref[...], w_ref[...])
        dx, dg = _rms_bwd_math(x_ref[...], g_ref[...], dh)
        tot = res_ref[...] + dx
        o_ref[...] = tot
        ob_ref[...] = tot.astype(BF16)

        @pl.when(pl.program_id(0) == 0)
        def _():
            dg_ref[...] = dg

        @pl.when(pl.program_id(0) > 0)
        def _():
            dg_ref[...] += dg

    row = pl.BlockSpec((tr, d), lambda i: (i, 0))
    vec = pl.BlockSpec((1, d), lambda i: (0, 0))
    outs, ridden = _hosted_call(
        body, rider, name=name, grid=(n // tr,),
        in_specs=[pl.BlockSpec((tr, ff), lambda i: (i, 0)), pl.BlockSpec((d, ff), lambda i: (0, 0)), row, vec, row],
        out_specs=[row, row, vec],
        out_shape=[jax.ShapeDtypeStruct((n, d), F32), jax.ShapeDtypeStruct((n, d), BF16),
                   jax.ShapeDtypeStruct((1, d), F32)],
        scratch_shapes=[], compiler_params=_cp(VMEM_BIG, ("arbitrary",)), args=(dup, w_up, x, g, res))
    return (*outs, ridden)


def _mlp_down_loss(act, w_down, x1, gf, tgt, *, t_seq, tr, name):
    n, d = x1.shape
    ff = act.shape[1]
    per_seq = t_seq // tr
    steps = n // tr

    def body(a_ref, w_ref, x_ref, g_ref, t_hbm, dx_ref, dxb_ref, dg_ref, loss_ref, tbuf, tsems):
        i = pl.program_id(0)
        slot = _ShiftedRows(t_hbm, tbuf, tsems, per_seq=per_seq, tt=tr, steps=steps).tile(i)

        @pl.when(i % per_seq == 0)
        def _():
            tbuf[slot, 0:CHUNK, :] = jnp.zeros((CHUNK, d), F32)

        t_ref = tbuf.at[slot]
        xv = x_ref[...] + _nn(a_ref[...], w_ref[...])
        g = g_ref[...]
        r = lax.rsqrt(jnp.mean(xv * xv, axis=-1, keepdims=True) + EPS)
        xh = xv * r
        pos = (i % per_seq) * tr + _iota2((tr, 1), 0)
        real = pos >= CHUNK
        err = jnp.where(real, xh * g - t_ref[...], 0.0)
        dy = err * (1.0 / d)
        gdy = dy * g
        dx = r * (gdy - xh * jnp.mean(xh * gdy, axis=-1, keepdims=True))
        dx_ref[...] = dx
        dxb_ref[...] = dx.astype(BF16)
        dg = jnp.sum(dy * xh, axis=0, keepdims=True)
        ls = 0.5 * jnp.sum(jnp.mean(err * err, axis=-1, keepdims=True), axis=0, keepdims=True)
        ls = jnp.where(_iota2((1, LANE), 1) == 0, ls, 0.0)

        @pl.when(i == 0)
        def _():
            dg_ref[...] = dg
            loss_ref[...] = ls

        @pl.when(i > 0)
        def _():
            dg_ref[...] += dg
            loss_ref[...] += ls

    row = pl.BlockSpec((tr, d), lambda i: (i, 0))
    vec = pl.BlockSpec((1, d), lambda i: (0, 0))
    one = pl.BlockSpec((1, LANE), lambda i: (0, 0))
    return pl.pallas_call(
        body, name=name, grid=(n // tr,),
        in_specs=[pl.BlockSpec((tr, ff), lambda i: (i, 0)), pl.BlockSpec((ff, d), lambda i: (0, 0)), row, vec, ANY],
        out_specs=[row, row, vec, one],
        out_shape=[jax.ShapeDtypeStruct((n, d), F32), jax.ShapeDtypeStruct((n, d), BF16),
                   jax.ShapeDtypeStruct((1, d), F32), jax.ShapeDtypeStruct((1, LANE), F32)],
        scratch_shapes=[pltpu.VMEM((2, tr, d), F32), pltpu.SemaphoreType.DMA((2,))],
        compiler_params=_cp(VMEM_BIG, ("arbitrary",)),
    )(act, w_down, x1, gf, tgt)


def _gnorm_fwd(o_dn, o_gla, projp, g_dn, g_gla, *, tr, name):
    n = o_dn.shape[0]
    w = NH * DN_D

    def body(odn_ref, ogl_ref, z_ref, r_ref, gdn_ref, ggl_ref, mix_ref):
        for grp, (o_ref, gate_ref, gain_ref) in enumerate(((odn_ref, z_ref, gdn_ref), (ogl_ref, r_ref, ggl_ref))):
            gain = gain_ref[...]
            for h in range(NH):
                sl = slice(h * DN_D, (h + 1) * DN_D)
                o = o_ref[:, sl].astype(F32)
                z = gate_ref[:, sl].astype(F32)
                r = lax.rsqrt(jnp.mean(o * o, axis=-1, keepdims=True) + EPS)
                y = (o * r * gain) * (z * _sigmoid(z))
                mix_ref[:, grp * w + h * DN_D: grp * w + (h + 1) * DN_D] = y.astype(mix_ref.dtype)

    row = pl.BlockSpec((tr, w), lambda i: (i, 0))
    vec = pl.BlockSpec((1, DN_D), lambda i: (0, 0))
    return pl.pallas_call(
        body, name=name, grid=(n // tr,),
        in_specs=[row, row, pl.BlockSpec((tr, w), lambda i: (i, C_DZ // w)),
                  pl.BlockSpec((tr, w), lambda i: (i, C_GR // w)), vec, vec],
        out_specs=pl.BlockSpec((tr, 2 * w), lambda i: (i, 0)),
        out_shape=jax.ShapeDtypeStruct((n, 2 * w), BF16),
        compiler_params=_cp(VMEM_BIG),
    )(o_dn, o_gla, projp, projp, g_dn, g_gla)


def _gnorm_bwd(dmix, o_dn, o_gla, projp, g_dn, g_gla, *, tr, name):
    n = o_dn.shape[0]
    w = NH * DN_D

    def body(dm_ref, odn_ref, ogl_ref, z_ref, r_ref, gdn_ref, ggl_ref,
             dodn_ref, ddz_ref, dogl_ref, dgr_ref, dgdn_ref, dggl_ref):
        first = pl.program_id(0) == 0
        groups = ((odn_ref, z_ref, gdn_ref, dodn_ref, ddz_ref, dgdn_ref),
                  (ogl_ref, r_ref, ggl_ref, dogl_ref, dgr_ref, dggl_ref))
        for grp, (o_ref, gate_ref, gain_ref, do_ref, dgate_ref, dgain_ref) in enumerate(groups):
            gain = gain_ref[...]
            dgain = jnp.zeros((1, DN_D), F32)
            for h in range(NH):
                sl = slice(h * DN_D, (h + 1) * DN_D)
                o = o_ref[:, sl].astype(F32)
                z = gate_ref[:, sl].astype(F32)
                dm = dm_ref[:, grp * w + h * DN_D: grp * w + (h + 1) * DN_D].astype(F32)
                r = lax.rsqrt(jnp.mean(o * o, axis=-1, keepdims=True) + EPS)
                oh = o * r
                s = _sigmoid(z)
                dn = dm * (z * s)
                dgate_ref[:, sl] = (dm * (oh * gain) * (s * (1.0 + z * (1.0 - s)))).astype(dgate_ref.dtype)
                gdn = dn * gain
                do_ref[:, sl] = (r * (gdn - oh * jnp.mean(oh * gdn, axis=-1, keepdims=True))).astype(do_ref.dtype)
                dgain = dgain + jnp.sum(dn * oh, axis=0, keepdims=True)

            @pl.when(first)
            def _():
                dgain_ref[...] = dgain

            @pl.when(jnp.logical_not(first))
            def _():
                dgain_ref[...] += dgain

    row = pl.BlockSpec((tr, w), lambda i: (i, 0))
    vec = pl.BlockSpec((1, DN_D), lambda i: (0, 0))
    big = jax.ShapeDtypeStruct((n, w), F32)
    gate = jax.ShapeDtypeStruct((n, w), BF16)
    small = jax.ShapeDtypeStruct((1, DN_D), F32)
    return pl.pallas_call(
        body, name=name, grid=(n // tr,),
        in_specs=[pl.BlockSpec((tr, 2 * w), lambda i: (i, 0)), row, row,
                  pl.BlockSpec((tr, w), lambda i: (i, C_DZ // w)), pl.BlockSpec((tr, w), lambda i: (i, C_GR // w)), vec, vec],
        out_specs=[row, row, row, row, vec, vec],
        out_shape=[gate, gate, gate, gate, small, small],
        compiler_params=_cp(VMEM_BIG),
    )(dmix, o_dn, o_gla, projp, projp, g_dn, g_gla)


QKV_W = 3 * NH * DN_D
HALO = 8


def _conv_z(xs_ref, cw_ref, tt):
    z = cw_ref[0:1, :] * xs_ref[pl.ds(HALO - 3, tt), :]
    for j in range(1, 4):
        z = z + cw_ref[j:j + 1, :] * xs_ref[pl.ds(HALO - 3 + j, tt), :]
    return z


def _dnprep_fwd(projp, conv_w, *, bsz, t_seq, tt, name, rider=None):
    n = bsz * t_seq
    per_seq = t_seq // tt
    hw = NH * DN_D

    def body(x_ref, halo_ref, cw_ref, q_ref, k_ref, v_ref, xs_ref):
        i = pl.program_id(1)
        xs_ref[0:HALO, :] = jnp.where(i == 0, 0.0, halo_ref[...].astype(F32))
        xs_ref[HALO:HALO + tt, :] = x_ref[...].astype(F32)
        z = _conv_z(xs_ref, cw_ref, tt)
        a = z * _sigmoid(z)
        for grp, o_ref in enumerate((q_ref, k_ref)):
            for h in range(NH):
                ah = a[:, grp * hw + h * DN_D: grp * hw + (h + 1) * DN_D]
                rs = lax.rsqrt(jnp.sum(ah * ah, axis=-1, keepdims=True) + EPS)
                o_ref[:, h * DN_D:(h + 1) * DN_D] = (ah * rs).astype(o_ref.dtype)
        v_ref[...] = a[:, 2 * hw:3 * hw].astype(v_ref.dtype)

    def halo_map(b, i):
        return (jnp.maximum((b * t_seq + i * tt) // HALO - 1, 0), 0)

    out = pl.BlockSpec((tt, hw), lambda b, i: (b * per_seq + i, 0))
    sds = jax.ShapeDtypeStruct((n, hw), BF16)
    outs, ridden = _hosted_call(
        body, rider, name=name, grid=(bsz, per_seq),
        in_specs=[pl.BlockSpec((tt, QKV_W), lambda b, i: (b * per_seq + i, 0)),
                  pl.BlockSpec((HALO, QKV_W), halo_map),
                  pl.BlockSpec((4, QKV_W), lambda b, i: (0, 0))],
        out_specs=[out, out, out], out_shape=[sds, sds, sds],
        scratch_shapes=[pltpu.VMEM((tt + HALO, QKV_W), F32)],
        compiler_params=_cp(VMEM_BIG), args=(projp, projp, conv_w))
    return (*outs, ridden)


def _dnprep_bwd_a(projp, conv_w, dq, dk, dv, *, bsz, t_seq, tt, name):
    n = bsz * t_seq
    per_seq = t_seq // tt
    hw = NH * DN_D

    def body(x_ref, halo_ref, cw_ref, dq_ref, dk_ref, dv_ref, dz_ref, dcw_ref, xs_ref):
        b, i = pl.program_id(0), pl.program_id(1)
        xs_ref[0:HALO, :] = jnp.where(i == 0, 0.0, halo_ref[...].astype(F32))
        xs_ref[HALO:HALO + tt, :] = x_ref[...].astype(F32)
        z = _conv_z(xs_ref, cw_ref, tt)
        s = _sigmoid(z)
        a = z * s
        dsilu = s * (1.0 + z * (1.0 - s))
        for grp, d_ref in enumerate((dq_ref, dk_ref)):
            for h in range(NH):
                sl = slice(grp * hw + h * DN_D, grp * hw + (h + 1) * DN_D)
                ah = a[:, sl]
                rs = lax.rsqrt(jnp.sum(ah * ah, axis=-1, keepdims=True) + EPS)
                y = ah * rs
                dy = d_ref[:, h * DN_D:(h + 1) * DN_D]
                da = rs * (dy - y * jnp.sum(dy * y, axis=-1, keepdims=True))
                dz_ref[:, sl] = da * dsilu[:, sl]
        dz_ref[:, 2 * hw:3 * hw] = dv_ref[...] * dsilu[:, 2 * hw:3 * hw]
        dz = dz_ref[...]
        first = jnp.logical_and(b == 0, i == 0)
        for j in range(4):
            part = jnp.sum(dz * xs_ref[pl.ds(HALO - 3 + j, tt), :], axis=0, keepdims=True)

            @pl.when(first)
            def _():
                dcw_ref[j:j + 1, :] = part

            @pl.when(jnp.logical_not(first))
            def _():
                dcw_ref[j:j + 1, :] += part

    def halo_map(b, i):
        return (jnp.maximum((b * t_seq + i * tt) // HALO - 1, 0), 0)

    hrow = pl.BlockSpec((tt, hw), lambda b, i: (b * per_seq + i, 0))
    return pl.pallas_call(
        body, name=name, grid=(bsz, per_seq),
        in_specs=[pl.BlockSpec((tt, QKV_W), lambda b, i: (b * per_seq + i, 0)),
                  pl.BlockSpec((HALO, QKV_W), halo_map),
                  pl.BlockSpec((4, QKV_W), lambda b, i: (0, 0)), hrow, hrow, hrow],
        out_specs=[pl.BlockSpec((tt, QKV_W), lambda b, i: (b * per_seq + i, 0)),
                   pl.BlockSpec((4, QKV_W), lambda b, i: (0, 0))],
        out_shape=[jax.ShapeDtypeStruct((n, QKV_W), F32), jax.ShapeDtypeStruct((4, QKV_W), F32)],
        scratch_shapes=[pltpu.VMEM((tt + HALO, QKV_W), F32)],
        compiler_params=_cp(VMEM_BIG),
    )(projp, projp, conv_w, dq, dk, dv)


def _dnprep_bwd_b(dz, conv_w, *, bsz, t_seq, tt, name):
    n = bsz * t_seq
    per_seq = t_seq // tt
    last_blk = n // HALO - 1

    main = tt - HALO
    col_blk = LANE
    row_blk = 144

    def body(dz_ref, halo_ref, cw_ref, dx_ref, tail_ref):
        i = pl.program_id(1)
        for cb in range(QKV_W // col_blk):
            cols = slice(cb * col_blk, (cb + 1) * col_blk)
            for r0 in range(0, main, row_blk):
                nr = min(row_blk, main - r0)
                dx = cw_ref[0:1, cols] * dz_ref[pl.ds(r0 + 3, nr), cols]
                for j in range(1, 4):
                    dx = dx + cw_ref[j:j + 1, cols] * dz_ref[pl.ds(r0 + 3 - j, nr), cols]
                dx_ref[r0:r0 + nr, cols] = dx.astype(dx_ref.dtype)
        tail_ref[0:HALO, :] = dz_ref[main:tt, :]
        tail_ref[HALO:2 * HALO, :] = jnp.where(i == per_seq - 1, 0.0, halo_ref[...])
        dx = cw_ref[0:1, :] * tail_ref[pl.ds(3, HALO), :]
        for j in range(1, 4):
            dx = dx + cw_ref[j:j + 1, :] * tail_ref[pl.ds(3 - j, HALO), :]
        dx_ref[main:tt, :] = dx.astype(dx_ref.dtype)

    def halo_map(b, i):
        return (jnp.minimum((b * t_seq + (i + 1) * tt) // HALO, last_blk), 0)

    row = pl.BlockSpec((tt, QKV_W), lambda b, i: (b * per_seq + i, 0))
    return pl.pallas_call(
        body, name=name, grid=(bsz, per_seq),
        in_specs=[row, pl.BlockSpec((HALO, QKV_W), halo_map), pl.BlockSpec((4, QKV_W), lambda b, i: (0, 0))],
        out_specs=row, out_shape=jax.ShapeDtypeStruct((n, QKV_W), BF16),
        scratch_shapes=[pltpu.VMEM((2 * HALO, QKV_W), F32)],
        compiler_params=_cp(VMEM_BIG),
    )(dz, dz, conv_w)


def _masks64():
    r = _iota2((CHUNK, CHUNK), 0)
    c = _iota2((CHUNK, CHUNK), 1)
    return r, c


def _group(nc_seq, target=5):
    return max(g for g in range(1, target + 1) if nc_seq % g == 0)


def _round_robin(chains):
    live = list(chains)
    while live:
        nxt = []
        for ch in live:
            try:
                next(ch)
                nxt.append(ch)
            except StopIteration:
                pass
        live = nxt
        yield


def _run(chains):
    for _ in _round_robin(chains):
        pass


def _per_chunk(inner, kinds, grp):
    def body(*refs):
        chains = []
        for gi in range(grp):
            views = []
            for r, kind in zip(refs, kinds):
                if kind == "row":
                    views.append(r.at[pl.ds(gi * CHUNK, CHUNK)])
                elif kind == "lead":
                    views.append(r.at[pl.ds(gi, 1)])
                else:
                    views.append(r)
            chains.append(inner(gi, *views))
        _run(chains)
    return body


def _accumulate(ref, val, gi):
    if gi > 0:
        ref[...] += val
        return
    first = pl.program_id(0) == 0

    @pl.when(first)
    def _():
        ref[...] = val

    @pl.when(jnp.logical_not(first))
    def _():
        ref[...] += val


ANY = pl.BlockSpec(memory_space=pl.ANY)


def _place():
    return lax.axis_index("x"), lax.axis_index("y"), lax.axis_index("c")


def _other_chips(x, y):
    return [(1 - x, y, 2 * (1 - x) + y), (x, 1 - y, 2 * x + 1 - y), (1 - x, 1 - y, 2 * (1 - x) + 1 - y)]


class _GatherRider:
    def __init__(self, bufs, split):
        self.inputs = list(bufs)
        self.split = list(split)
        self.out_shapes = [jax.ShapeDtypeStruct(b.shape, b.dtype) for b in bufs]
        self.aliases = {i: i for i in range(len(bufs))}
        self.sems = [pltpu.SemaphoreType.DMA((len(bufs), 3))] * 4

    def _rows(self, k, buf, c, mine=True):
        r = buf.shape[1]
        if not self.split[k]:
            return pl.ds(0, r)
        return pl.ds((c if mine else 1 - c) * (r // 2), r // 2)

    def _ici(self, k, d, bufs, sems, c, px, py, block):
        rows = self._rows(k, bufs[k], c)
        return pltpu.make_async_remote_copy(
            src_ref=bufs[k].at[block, rows, :], dst_ref=bufs[k].at[block, rows, :], send_sem=sems[0].at[k, d],
            recv_sem=sems[1].at[k, d], device_id=(px, py, c), device_id_type=MESH)

    def _pass(self, k, d, bufs, sems, x, y, c, block, mine):
        rows = self._rows(k, bufs[k], c, mine)
        return pltpu.make_async_remote_copy(
            src_ref=bufs[k].at[block, rows, :], dst_ref=bufs[k].at[block, rows, :], send_sem=sems[2].at[k, d],
            recv_sem=sems[3].at[k, d], device_id=(x, y, 1 - c), device_id_type=MESH)

    def first(self, in_refs, bufs, sems):
        x, y, c = _place()
        for k in range(len(bufs)):
            for d, (px, py, _) in enumerate(_other_chips(x, y)):
                self._ici(k, d, bufs, sems, c, px, py, 2 * x + y).start()

    def last(self, in_refs, bufs, sems):
        x, y, c = _place()
        chips = _other_chips(x, y)
        for k in range(len(bufs)):
            for d, (px, py, pj) in enumerate(chips):
                self._ici(k, d, bufs, sems, c, px, py, pj).wait_recv()
                if self.split[k]:
                    self._pass(k, d, bufs, sems, x, y, c, pj, True).start()
        for k in range(len(bufs)):
            for d, (px, py, pj) in enumerate(chips):
                if self.split[k]:
                    self._pass(k, d, bufs, sems, x, y, c, pj, False).wait_recv()
                    self._pass(k, d, bufs, sems, x, y, c, pj, True).wait_send()
                self._ici(k, d, bufs, sems, c, px, py, 2 * x + y).wait_send()


def _hosted_call(body, rider, *, name, grid, in_specs, out_specs, out_shape, scratch_shapes, compiler_params, args):
    if rider is None:
        outs = pl.pallas_call(body, name=name, grid=grid, in_specs=in_specs, out_specs=out_specs, out_shape=out_shape,
                              scratch_shapes=scratch_shapes, compiler_params=compiler_params)(*args)
        return list(outs), []
    n_in, n_out, n_scr = len(in_specs), len(out_specs), len(scratch_shapes)
    r_in, r_out = len(rider.inputs), len(rider.out_shapes)
    compiler_params = _cp(compiler_params.vmem_limit_bytes, ("arbitrary",) * len(grid))

    def full_body(*refs):
        ins = refs[:n_in]
        rins = refs[n_in:n_in + r_in]
        outs = refs[n_in + r_in:n_in + r_in + n_out]
        routs = refs[n_in + r_in + n_out:n_in + r_in + n_out + r_out]
        rest = refs[n_in + r_in + n_out + r_out:]
        scr, sems = rest[:n_scr], rest[n_scr:]
        ids = [pl.program_id(a) for a in range(len(grid))]
        is_first = functools.reduce(jnp.logical_and, [i == 0 for i in ids])
        is_last = functools.reduce(jnp.logical_and, [i == g - 1 for i, g in zip(ids, grid)])

        @pl.when(is_first)
        def _():
            rider.first(rins, routs, sems)

        body(*ins, *outs, *scr)

        @pl.when(is_last)
        def _():
            rider.last(rins, routs, sems)

    res = pl.pallas_call(
        full_body, name=name, grid=grid, in_specs=list(in_specs) + [ANY] * r_in,
        out_specs=list(out_specs) + [ANY] * r_out, out_shape=list(out_shape) + list(rider.out_shapes),
        input_output_aliases={n_in + i: n_out + o for i, o in rider.aliases.items()},
        scratch_shapes=list(scratch_shapes) + list(rider.sems), compiler_params=compiler_params,
    )(*args, *rider.inputs)
    return list(res[:n_out]), list(res[n_out:])


def _exchange_now(rider, *, name):
    r_in = len(rider.inputs)

    def body(*refs):
        rins = refs[:r_in]
        routs = refs[r_in:r_in + len(rider.out_shapes)]
        sems = refs[r_in + len(rider.out_shapes):]
        rider.first(rins, routs, sems)
        rider.last(rins, routs, sems)

    return pl.pallas_call(
        body, name=name, in_specs=[ANY] * r_in, out_specs=[ANY] * len(rider.out_shapes), out_shape=list(rider.out_shapes),
        input_output_aliases=dict(rider.aliases), scratch_shapes=list(rider.sems),
    )(*rider.inputs)


def _to_slot(where, a, dtype, *, name):
    r, cols = a.shape
    tr = _tile(r, 256, 16) if r > 256 else r

    def body(w_ref, a_ref, o_ref):
        o_ref[0] = a_ref[...].astype(o_ref.dtype)

    return pl.pallas_call(
        body, name=name,
        grid_spec=pltpu.PrefetchScalarGridSpec(
            num_scalar_prefetch=1, grid=(r // tr,),
            in_specs=[pl.BlockSpec((tr, cols), lambda i, w: (i, 0))],
            out_specs=pl.BlockSpec((1, tr, cols), lambda i, w: (w[1], i, 0))),
        out_shape=jax.ShapeDtypeStruct((N_CHIPS, r, cols), dtype), compiler_params=_cp(VMEM_BIG),
    )(where, a)


def _tri_inv(a_strict):
    r, c = _masks64()
    eye = (r == c).astype(F32)
    blk16 = (r // 16) == (c // 16)
    blk32 = (r // 32) == (c // 32)
    ld = jnp.where(blk16, a_strict, 0.0)
    x = eye - ld
    p = _nn(ld, ld)
    yield
    for step in range(3):
        xp = _nn(x, p)
        if step < 2:
            p = _nn(p, p)
        x = x + xp
        yield
    for lk in (jnp.where(jnp.logical_and(blk32, jnp.logical_not(blk16)), a_strict, 0.0),
               jnp.where(blk32, 0.0, a_strict)):
        y = x - eye
        s = lk + _nn(y, lk)
        yield
        x = x - s - _nn(s, y)
        yield
    return x


def _dn_gates(sa, alog, dtb, chunk_in_seq):
    rows = _iota2((CHUNK, LANE), 0)
    valid = jnp.logical_or(rows >= N_PAD, chunk_in_seq > 0)
    beta_t = _sigmoid(sa)
    ea = jnp.exp(alog)
    g_t = jnp.where(valid, -ea * _softplus(sa + dtb), 0.0)
    r, c = _masks64()
    ltri = (r >= c).astype(F32)
    gam_t = _tri_sum(ltri, g_t)
    return beta_t, g_t, gam_t, valid, ea


def _dn_intra_fwd(qn, kn, v, projp, alog_row, dtb_row, *, nc_seq, name, rider=None):
    n = qn.shape[0]
    nct = n // CHUNK
    hw = NH * DN_D
    scale = DN_D ** -0.5

    grp = _group(nc_seq)

    def inner(gi, q_ref, k_ref, v_ref, sa_ref, al_ref, dt_ref, u_ref, w_ref, qg_ref, kd_ref, p_ref, t_ref, gl_ref):
        ci = (pl.program_id(0) * grp + gi) % nc_seq
        beta_t, _, gam_t, _, _ = _dn_gates(sa_ref[...], al_ref[...], dt_ref[...], ci)
        yield
        gam_tt = gam_t.T
        r, c = _masks64()
        incl = r >= c
        strict = r > c

        def head(h):
            sl = slice(h * DN_D, (h + 1) * DN_D)
            beta_w = jnp.broadcast_to(beta_t[:, h:h + 1], (CHUNK, DN_D))
            gam_w = jnp.broadcast_to(gam_t[:, 4 + h:5 + h], (CHUNK, DN_D))
            gam_row = gam_tt[4 + h:5 + h, :]
            gl = gam_t[CHUNK - 1:CHUNK, 4 + h:5 + h]
            dec = jnp.exp(jnp.where(incl, gam_w[:, 0:CHUNK] - gam_row, -jnp.inf))
            kh = k_ref[:, sl].astype(F32)
            qh = q_ref[:, sl].astype(F32) * scale
            vh = v_ref[:, sl].astype(F32)
            kk = _nt(kh, kh)
            qk = _nt(qh, kh)
            yield
            a = jnp.where(strict, beta_w[:, 0:CHUNK] * kk * dec, 0.0)
            tm = yield from _tri_inv(a)
            egam_w = jnp.exp(gam_w)
            u_ref[:, sl] = _nn(tm, beta_w * vh).astype(u_ref.dtype)
            w_ref[:, sl] = _nn(tm, (beta_w * egam_w) * kh).astype(w_ref.dtype)
            qg_ref[:, sl] = (egam_w * qh).astype(qg_ref.dtype)
            kd_ref[:, sl] = (jnp.exp(gl - gam_w) * kh).astype(kd_ref.dtype)
            p_ref[0, h] = qk * dec
            t_ref[0, h] = tm
            gl_ref[0, h:h + 1, :] = jnp.broadcast_to(jnp.exp(gl), (1, LANE))

        yield from _round_robin([head(h) for h in range(NH)])

    rows = grp * CHUNK
    row = pl.BlockSpec((rows, hw), lambda i: (i, 0))
    vec = pl.BlockSpec((1, LANE), lambda i: (0, 0))
    mat = pl.BlockSpec((grp, NH, CHUNK, CHUNK), lambda i: (i, 0, 0, 0))
    big = jax.ShapeDtypeStruct((n, hw), BF16)
    msd = jax.ShapeDtypeStruct((nct, NH, CHUNK, CHUNK), F32)
    kinds = ["row"] * 4 + ["whole"] * 2 + ["row"] * 4 + ["lead"] * 3
    outs, ridden = _hosted_call(
        _per_chunk(inner, kinds, grp), rider, name=name, grid=(nct // grp,),
        in_specs=[row, row, row, pl.BlockSpec((rows, LANE), lambda i: (i, 0)), vec, vec],
        out_specs=[row, row, row, row, mat, mat, pl.BlockSpec((grp, NH, LANE), lambda i: (i, 0, 0))],
        out_shape=[big, big, big, big, msd, msd, jax.ShapeDtypeStruct((nct, NH, LANE), F32)],
        scratch_shapes=[], compiler_params=_cp(VMEM_BIG, ("arbitrary",)),
        args=(qn, kn, v, projp, alog_row, dtb_row))
    return (*outs, ridden)


def _dn_scan_fwd(u, w, qg, kd, p, gl, *, bsz, nc_seq, name, rider=None):
    hw = NH * DN_D
    t_seq = nc_seq * CHUNK
    u, w, qg, kd = (z.reshape(bsz, t_seq, hw) for z in (u, w, qg, kd))
    p = p.reshape(bsz, nc_seq, NH, CHUNK, CHUNK)
    gl = gl.reshape(bsz, nc_seq, NH, LANE)
    grp = _group(nc_seq)

    def body(u_ref, w_ref, qg_ref, kd_ref, p_ref, gl_ref, o_ref, vn_ref, hist_ref, s_ref):
        @pl.when(pl.program_id(0) == 0)
        def _():
            s_ref[...] = jnp.zeros_like(s_ref)

        def chain(b, h, gi):
            sl = slice(h * DN_D, (h + 1) * DN_D)
            rows = pl.ds(gi * CHUNK, CHUNK)
            s = s_ref[b, h]
            hist_ref[b, gi, h] = s.astype(hist_ref.dtype)
            ws = _nn(w_ref[b, rows, sl], s)
            qs = _nn(qg_ref[b, rows, sl], s)
            yield
            vn = u_ref[b, rows, sl] - ws
            vn_ref[b, rows, sl] = vn.astype(vn_ref.dtype)
            o_ref[b, rows, sl] = (qs + _nn(p_ref[b, gi, h], vn)).astype(o_ref.dtype)
            s_ref[b, h] = gl_ref[b, gi, h:h + 1, :] * s + _tn(kd_ref[b, rows, sl], vn)

        for gi in range(grp):
            _run([chain(b, h, gi) for b in range(bsz) for h in range(NH)])

    row = pl.BlockSpec((bsz, grp * CHUNK, hw), lambda i: (0, i, 0))
    outs, ridden = _hosted_call(
        body, rider, name=name, grid=(nc_seq // grp,),
        in_specs=[row, row, row, row, pl.BlockSpec((bsz, grp, NH, CHUNK, CHUNK), lambda i: (0, i, 0, 0, 0)),
                  pl.BlockSpec((bsz, grp, NH, LANE), lambda i: (0, i, 0, 0))],
        out_specs=[row, row, pl.BlockSpec((bsz, grp, NH, DN_D, DN_D), lambda i: (0, i, 0, 0, 0))],
        out_shape=[jax.ShapeDtypeStruct((bsz, t_seq, hw), BF16), jax.ShapeDtypeStruct((bsz, t_seq, hw), BF16),
                   jax.ShapeDtypeStruct((bsz, nc_seq, NH, DN_D, DN_D), F32)],
        scratch_shapes=[pltpu.VMEM((bsz, NH, DN_D, DN_D), F32)],
        compiler_params=_cp(VMEM_BIG, ("arbitrary",)), args=(u, w, qg, kd, p, gl))
    o, vn, hist = outs
    return o.reshape(bsz * t_seq, hw), vn.reshape(bsz * t_seq, hw), hist, ridden


def _dn_scan_bwd(do, w, qg, kd, vn, p, gl, hist, *, bsz, nc_seq, name):
    hw = NH * DN_D
    t_seq = nc_seq * CHUNK
    do, w, qg, kd, vn = (z.reshape(bsz, t_seq, hw) for z in (do, w, qg, kd, vn))
    p = p.reshape(bsz, nc_seq, NH, CHUNK, CHUNK)
    gl = gl.reshape(bsz, nc_seq, NH, LANE)
    grp = _group(nc_seq)

    def body(do_ref, w_ref, qg_ref, kd_ref, vn_ref, p_ref, gl_ref, hist_ref,
             du_ref, dw_ref, dqg_ref, dkd_ref, dgl_ref, ds_ref):
        @pl.when(pl.program_id(0) == 0)
        def _():
            ds_ref[...] = jnp.zeros_like(ds_ref)

        def chain(b, h, gi):
            sl = slice(h * DN_D, (h + 1) * DN_D)
            rows = pl.ds(gi * CHUNK, CHUNK)
            s = hist_ref[b, gi, h]
            dsn = ds_ref[b, h]
            doh = do_ref[b, rows, sl]
            vnh = vn_ref[b, rows, sl]
            kdh = kd_ref[b, rows, sl]
            dvn = _tn(p_ref[b, gi, h], doh) + _nn(kdh, dsn)
            du_ref[b, rows, sl] = dvn.astype(du_ref.dtype)
            dqg_ref[b, rows, sl] = _nt(doh, s).astype(dqg_ref.dtype)
            dkd_ref[b, rows, sl] = _nt(vnh, dsn).astype(dkd_ref.dtype)
            ds_part = _tn(qg_ref[b, rows, sl], doh) + gl_ref[b, gi, h:h + 1, :] * dsn
            dgl = jnp.sum(jnp.sum(dsn * s, axis=0, keepdims=True), axis=1, keepdims=True)
            dgl_ref[b, gi, h:h + 1, :] = jnp.broadcast_to(dgl, (1, LANE))
            yield
            dw_ref[b, rows, sl] = (-_nt(dvn, s)).astype(dw_ref.dtype)
            ds_ref[b, h] = ds_part - _tn(w_ref[b, rows, sl], dvn)

        for gi in reversed(range(grp)):
            _run([chain(b, h, gi) for b in range(bsz) for h in range(NH)])

    steps = nc_seq // grp
    rev = lambda i: steps - 1 - i
    row = pl.BlockSpec((bsz, grp * CHUNK, hw), lambda i: (0, rev(i), 0))
    mat = pl.BlockSpec((bsz, grp, NH, CHUNK, CHUNK), lambda i: (0, rev(i), 0, 0, 0))
    glb = pl.BlockSpec((bsz, grp, NH, LANE), lambda i: (0, rev(i), 0, 0))
    big = jax.ShapeDtypeStruct((bsz, t_seq, hw), BF16)
    outs = pl.pallas_call(
        body, name=name, grid=(steps,),
        in_specs=[row, row, row, row, row, mat, glb,
                  pl.BlockSpec((bsz, grp, NH, DN_D, DN_D), lambda i: (0, rev(i), 0, 0, 0))],
        out_specs=[row, row, row, row, glb],
        out_shape=[big, big, jax.ShapeDtypeStruct(big.shape, F32), jax.ShapeDtypeStruct(big.shape, F32),
                   jax.ShapeDtypeStruct((bsz, nc_seq, NH, LANE), F32)],
        scratch_shapes=[pltpu.VMEM((bsz, NH, DN_D, DN_D), F32)],
        compiler_params=_cp(VMEM_BIG, ("arbitrary",)),
    )(do, w, qg, kd, vn, p, gl, hist)
    du, dw, dqg, dkd, dgl = outs
    n = bsz * t_seq
    return (du.reshape(n, hw), dw.reshape(n, hw), dqg.reshape(n, hw), dkd.reshape(n, hw),
            dgl.reshape(bsz * nc_seq, NH, LANE))


def _dn_intra_bwd(qn, kn, v, projp, alog_row, dtb_row, u, w, tmat, du, dw, dqg, dkd, do, vn, dgl, *, nc_seq, name,
                  rider=None):
    n = qn.shape[0]
    nct = n // CHUNK
    hw = NH * DN_D
    scale = DN_D ** -0.5

    grp = _group(nc_seq)

    def inner(gi, q_ref, k_ref, v_ref, sa_ref, al_ref, dt_ref, u_ref, w_ref, t_ref, du_ref, dw_ref, dqg_ref, dkd_ref,
              do_ref, vn_ref, dgl_ref, dq_ref, dk_ref, dv_ref, dsa_ref, dal_ref, ddt_ref):
        ci = (pl.program_id(0) * grp + gi) % nc_seq
        sa = sa_ref[...]
        beta_t, g_t, gam_t, valid, ea = _dn_gates(sa, al_ref[...], dt_ref[...], ci)
        yield
        lane = _iota2((CHUNK, LANE), 1)
        gates_t = jnp.where(lane < 4, beta_t, gam_t).T
        r, c = _masks64()
        incl, strict, upper, supper = r >= c, r > c, r <= c, r < c
        rows1 = _iota2((CHUNK, 1), 0)
        acc = [jnp.zeros((CHUNK, LANE), F32)]

        def head(h):
            sl = slice(h * DN_D, (h + 1) * DN_D)
            beta_w = jnp.broadcast_to(beta_t[:, h:h + 1], (CHUNK, DN_D))
            gam_w = jnp.broadcast_to(gam_t[:, 4 + h:5 + h], (CHUNK, DN_D))
            beta_s, gam_s = beta_w[:, 0:CHUNK], gam_w[:, 0:CHUNK]
            beta_row = gates_t[h:h + 1, :]
            gam_row = gates_t[4 + h:5 + h, :]
            gl = gam_t[CHUNK - 1:CHUNK, 4 + h:5 + h]
            dec = jnp.exp(jnp.where(incl, gam_s - gam_row, -jnp.inf))
            dec_t = jnp.exp(jnp.where(upper, gam_row - gam_s, -jnp.inf))
            egam_w = jnp.exp(gam_w)
            ekd_w = jnp.exp(gl - gam_w)
            kh = k_ref[:, sl].astype(F32)
            qh = q_ref[:, sl].astype(F32) * scale
            vh = v_ref[:, sl].astype(F32)
            uh = u_ref[:, sl]
            wh = w_ref[:, sl]
            doh = do_ref[:, sl]
            vnh = vn_ref[:, sl]
            kk = _nt(kh, kh)
            qk = _nt(qh, kh)
            qk_t = _nt(kh, qh)
            dp = _nt(doh, vnh)
            dp_t = _nt(vnh, doh)
            t_hi, t_lo = _split(t_ref[0, h].T)
            duh, dwh = du_ref[:, sl], dw_ref[:, sl]
            dvb = _nn(t_hi, duh) + _nn(t_lo, duh)
            dkg = _nn(t_hi, dwh) + _nn(t_lo, dwh)
            yield
            dvb_hi, dvb_lo = _split(dvb)
            dkg_hi, dkg_lo = _split(dkg)
            m = (_nt(dvb_hi, uh) + _nt(dvb_lo, uh)) + (_nt(dkg_hi, wh) + _nt(dkg_lo, wh))
            m_t = (_nt(uh, dvb_hi) + _nt(uh, dvb_lo)) + (_nt(wh, dkg_hi) + _nt(wh, dkg_lo))
            yield
            da = jnp.where(strict, -m, 0.0)
            da_t = jnp.where(supper, -m_t, 0.0)
            a = jnp.where(strict, beta_s * kk * dec, 0.0)
            a_t = jnp.where(supper, beta_row * kk * dec_t, 0.0)
            dad = da * dec
            dad_t = da_t * dec_t
            dpm = jnp.where(incl, dp, 0.0)
            dpm_t = jnp.where(upper, dp_t, 0.0)
            e = da * a + dpm * (qk * dec)
            e_t = da_t * a_t + dpm_t * (qk_t * dec_t)
            dqgh = dqg_ref[:, sl].astype(F32)
            dkdh = dkd_ref[:, sl].astype(F32)
            bg_w = beta_w * egam_w
            dkh = (_nn(beta_s * dad, kh) + _nn(beta_row * dad_t, kh) + _nn(dpm_t * dec_t, qh)
                   + bg_w * dkg + ekd_w * dkdh)
            dqh = _nn(dpm * dec, kh) + egam_w * dqgh
            t_kd = dkdh * (ekd_w * kh)
            dbeta = (jnp.sum(dad * kk, axis=1, keepdims=True)
                     + jnp.sum(dkg * (egam_w * kh) + dvb * vh, axis=1, keepdims=True))
            dgam = (jnp.sum(e - e_t, axis=1, keepdims=True)
                    + jnp.sum(dkg * (bg_w * kh) + dqgh * (egam_w * qh) - t_kd, axis=1, keepdims=True))
            dgam_last = (jnp.sum(jnp.sum(t_kd, axis=0, keepdims=True), axis=1, keepdims=True)
                         + dgl_ref[0, h:h + 1, 0:1] * jnp.exp(gl))
            dgam = dgam + jnp.where(rows1 == CHUNK - 1, dgam_last, 0.0)
            dq_ref[:, sl] = (dqh * scale).astype(dq_ref.dtype)
            dk_ref[:, sl] = dkh.astype(dk_ref.dtype)
            dv_ref[:, sl] = (beta_w * dvb).astype(dv_ref.dtype)
            acc[0] = acc[0] + jnp.where(lane == h, dbeta, 0.0) + jnp.where(lane == 4 + h, dgam, 0.0)

        yield from _round_robin([head(h) for h in range(NH)])
        acc_t = acc[0]
        dg_t = _tri_sum(upper, acc_t)
        ddb = acc_t * beta_t * (1.0 - beta_t)
        dda = jnp.where(valid, dg_t * (-ea) * _sigmoid(sa + dt_ref[...]), 0.0)
        dsa_ref[...] = jnp.where(lane < 4, ddb, jnp.where(lane < 8, dda, 0.0)).astype(dsa_ref.dtype)
        in_g = jnp.logical_and(lane >= 4, lane < 8)
        dal = jnp.sum(jnp.where(in_g, dg_t * g_t, 0.0), axis=0, keepdims=True)
        ddt = jnp.sum(jnp.where(in_g, dda, 0.0), axis=0, keepdims=True)
        _accumulate(dal_ref, dal, gi)
        _accumulate(ddt_ref, ddt, gi)

    rows = grp * CHUNK
    row = pl.BlockSpec((rows, hw), lambda i: (i, 0))
    vec = pl.BlockSpec((1, LANE), lambda i: (0, 0))
    mat = pl.BlockSpec((grp, NH, CHUNK, CHUNK), lambda i: (i, 0, 0, 0))
    glb = pl.BlockSpec((grp, NH, LANE), lambda i: (i, 0, 0))
    big = jax.ShapeDtypeStruct((n, hw), F32)
    v128 = jax.ShapeDtypeStruct((1, LANE), F32)
    kinds = (["row"] * 4 + ["whole"] * 2 + ["row"] * 2 + ["lead"] + ["row"] * 6 + ["lead"]
             + ["row"] * 4 + ["whole"] * 2)
    outs, ridden = _hosted_call(
        _per_chunk(inner, kinds, grp), rider, name=name, grid=(nct // grp,),
        in_specs=[row, row, row, pl.BlockSpec((rows, LANE), lambda i: (i, 0)), vec, vec,
                  row, row, mat, row, row, row, row, row, row, glb],
        out_specs=[row, row, row, pl.BlockSpec((rows, LANE), lambda i: (i, 0)), vec, vec],
        out_shape=[big, big, big, jax.ShapeDtypeStruct((n, LANE), BF16), v128, v128],
        scratch_shapes=[], compiler_params=_cp(VMEM_BIG, ("arbitrary",)),
        args=(qn, kn, v, projp, alog_row, dtb_row, u, w, tmat, du, dw, dqg, dkd, do, vn, dgl))
    return (*outs, ridden)


GQ_W = NH * GLA_DK
GV_W = NH * GLA_DV
GLA_NORM = 16.0
MID = CHUNK // 2


def _gla_gates(sb, w2p, gb, chunk_in_seq):
    rows = _iota2((CHUNK, GQ_W), 0)
    valid = jnp.logical_or(rows >= N_PAD, chunk_in_seq > 0)
    graw = _nn(sb, w2p) + gb
    yield
    g = jnp.where(valid, _logsigmoid(graw) * (1.0 / GLA_NORM), 0.0)
    r, c = _masks64()
    bcum = _tri_sum(r >= c, g)
    yield
    return graw, bcum, valid


def _head_mask(h):
    lane = _iota2((1, GQ_W), 1)
    return jnp.logical_and(lane >= h * GLA_DK, lane < (h + 1) * GLA_DK)


def _gla_intra_fwd(projp, gates, w2p, gb, *, nc_seq, name):
    n = projp.shape[0]
    nct = n // CHUNK
    scale = GLA_DK ** -0.5

    grp = _group(nc_seq)
    rows = grp * CHUNK

    def inner(gi, qk_ref, v_ref, sb_ref, w2_ref, gb_ref, oi_ref, qg_ref, kd_ref, gl_ref):
        ci = (pl.program_id(0) * grp + gi) % nc_seq
        _, bc, _ = yield from _gla_gates(sb_ref[...], w2_ref[...], gb_ref[...], ci)
        bref = bc[MID:MID + 1, :]
        bl = bc[CHUNK - 1:CHUNK, :]
        q = qk_ref[:, 0:GQ_W].astype(F32) * scale
        k = qk_ref[:, GQ_W:2 * GQ_W].astype(F32)
        qi = q * jnp.exp(bc - bref)
        ki = k * jnp.exp(bref - bc)
        qg_ref[...] = (q * jnp.exp(bc)).astype(qg_ref.dtype)
        kd_ref[...] = (k * jnp.exp(bl - bc)).astype(kd_ref.dtype)
        gl_ref[0] = jnp.exp(bl)
        r, c = _masks64()
        incl = r >= c
        a = [jnp.where(incl, _nt(jnp.where(_head_mask(h), qi, 0.0), ki), 0.0) for h in range(NH)]
        yield
        for h in range(NH):
            oi_ref[:, h * GLA_DV:(h + 1) * GLA_DV] = _nn(a[h], v_ref[:, h * GLA_DV:(h + 1) * GLA_DV]).astype(oi_ref.dtype)

    kinds = ["row"] * 3 + ["whole"] * 2 + ["row"] * 3 + ["lead"]
    return pl.pallas_call(
        _per_chunk(inner, kinds, grp), name=name, grid=(nct // grp,),
        in_specs=[pl.BlockSpec((rows, 2 * GQ_W), lambda i: (i, C_GQK // (2 * GQ_W))),
                  pl.BlockSpec((rows, GV_W), lambda i: (i, C_GV // GV_W)),
                  pl.BlockSpec((rows, LANE), lambda i: (i, 1)),
                  pl.BlockSpec((LANE, GQ_W), lambda i: (0, 0)), pl.BlockSpec((1, GQ_W), lambda i: (0, 0))],
        out_specs=[pl.BlockSpec((rows, GV_W), lambda i: (i, 0)), pl.BlockSpec((rows, GQ_W), lambda i: (i, 0)),
                   pl.BlockSpec((rows, GQ_W), lambda i: (i, 0)), pl.BlockSpec((grp, 1, GQ_W), lambda i: (i, 0, 0))],
        out_shape=[jax.ShapeDtypeStruct((n, GV_W), BF16), jax.ShapeDtypeStruct((n, GQ_W), BF16),
                   jax.ShapeDtypeStruct((n, GQ_W), BF16), jax.ShapeDtypeStruct((nct, 1, GQ_W), F32)],
        compiler_params=_cp(VMEM_BIG),
    )(projp, projp, gates, w2p, gb)


def _gla_scan_fwd(oi, qg, kd, gl, projp, *, bsz, nc_seq, name, rider=None):
    t_seq = nc_seq * CHUNK
    oi = oi.reshape(bsz, t_seq, GV_W)
    qg, kd = qg.reshape(bsz, t_seq, GQ_W), kd.reshape(bsz, t_seq, GQ_W)
    gl = gl.reshape(bsz, nc_seq, 1, GQ_W)
    pj = projp.reshape(bsz, t_seq, PW)
    grp = _group(nc_seq)

    def body(oi_ref, qg_ref, kd_ref, gl_ref, v_ref, o_ref, hist_ref, st_ref):
        @pl.when(pl.program_id(0) == 0)
        def _():
            st_ref[...] = jnp.zeros_like(st_ref)

        for gi in range(grp):
            rows = pl.ds(gi * CHUNK, CHUNK)
            for b in range(bsz):
                st = st_ref[b]
                hist_ref[b, gi] = st.astype(hist_ref.dtype)
                qgb = qg_ref[b, rows, :]
                kdb = kd_ref[b, rows, :]
                upd = jnp.zeros((GLA_DV, GQ_W), F32)
                for h in range(NH):
                    sl = slice(h * GLA_DV, (h + 1) * GLA_DV)
                    m = _head_mask(h)
                    o_ref[b, rows, sl] = (oi_ref[b, rows, sl] + _nt(jnp.where(m, qgb, 0.0), st)).astype(o_ref.dtype)
                    upd = upd + jnp.where(m, _tn(v_ref[b, rows, sl], kdb), 0.0)
                st_ref[b] = gl_ref[b, gi] * st + upd

    rws = grp * CHUNK
    outs, ridden = _hosted_call(
        body, rider, name=name, grid=(nc_seq // grp,),
        in_specs=[pl.BlockSpec((bsz, rws, GV_W), lambda i: (0, i, 0)),
                  pl.BlockSpec((bsz, rws, GQ_W), lambda i: (0, i, 0)),
                  pl.BlockSpec((bsz, rws, GQ_W), lambda i: (0, i, 0)),
                  pl.BlockSpec((bsz, grp, 1, GQ_W), lambda i: (0, i, 0, 0)),
                  pl.BlockSpec((bsz, rws, GV_W), lambda i: (0, i, C_GV // GV_W))],
        out_specs=[pl.BlockSpec((bsz, rws, GV_W), lambda i: (0, i, 0)),
                   pl.BlockSpec((bsz, grp, GLA_DV, GQ_W), lambda i: (0, i, 0, 0))],
        out_shape=[jax.ShapeDtypeStruct((bsz, t_seq, GV_W), BF16),
                   jax.ShapeDtypeStruct((bsz, nc_seq, GLA_DV, GQ_W), F32)],
        scratch_shapes=[pltpu.VMEM((bsz, GLA_DV, GQ_W), F32)],
        compiler_params=_cp(VMEM_BIG, ("arbitrary",)), args=(oi, qg, kd, gl, pj))
    return outs[0].reshape(bsz * t_seq, GV_W), outs[1], ridden


def _gla_scan_bwd(do, qg, kd, gl, projp, hist, *, bsz, nc_seq, name):
    t_seq = nc_seq * CHUNK
    do = do.reshape(bsz, t_seq, GV_W)
    qg, kd = qg.reshape(bsz, t_seq, GQ_W), kd.reshape(bsz, t_seq, GQ_W)
    gl = gl.reshape(bsz, nc_seq, 1, GQ_W)
    pj = projp.reshape(bsz, t_seq, PW)
    grp = _group(nc_seq)

    def body(do_ref, qg_ref, kd_ref, gl_ref, v_ref, hist_ref, dqg_ref, dkd_ref, dv_ref, dgl_ref, dst_ref):
        @pl.when(pl.program_id(0) == 0)
        def _():
            dst_ref[...] = jnp.zeros_like(dst_ref)

        for gi in reversed(range(grp)):
            rows = pl.ds(gi * CHUNK, CHUNK)
            for b in range(bsz):
                st = hist_ref[b, gi]
                dst = dst_ref[b]
                qgb = qg_ref[b, rows, :]
                kdb = kd_ref[b, rows, :]
                dqg = jnp.zeros((CHUNK, GQ_W), F32)
                dkd = jnp.zeros((CHUNK, GQ_W), F32)
                add = jnp.zeros((GLA_DV, GQ_W), F32)
                for h in range(NH):
                    sl = slice(h * GLA_DV, (h + 1) * GLA_DV)
                    m = _head_mask(h)
                    doh = do_ref[b, rows, sl]
                    vh = v_ref[b, rows, sl]
                    dqg = dqg + jnp.where(m, _nn(doh, st), 0.0)
                    dkd = dkd + jnp.where(m, _nn(vh, dst), 0.0)
                    dv_ref[b, rows, sl] = _nt(jnp.where(m, kdb, 0.0), dst).astype(dv_ref.dtype)
                    add = add + jnp.where(m, _tn(doh, qgb), 0.0)
                dqg_ref[b, rows, :] = dqg.astype(dqg_ref.dtype)
                dkd_ref[b, rows, :] = dkd.astype(dkd_ref.dtype)
                dgl_ref[b, gi] = jnp.sum(dst * st, axis=0, keepdims=True)
                dst_ref[b] = gl_ref[b, gi] * dst + add

    steps = nc_seq // grp
    rws = grp * CHUNK
    rev = lambda i: steps - 1 - i
    outs = pl.pallas_call(
        body, name=name, grid=(steps,),
        in_specs=[pl.BlockSpec((bsz, rws, GV_W), lambda i: (0, rev(i), 0)),
                  pl.BlockSpec((bsz, rws, GQ_W), lambda i: (0, rev(i), 0)),
                  pl.BlockSpec((bsz, rws, GQ_W), lambda i: (0, rev(i), 0)),
                  pl.BlockSpec((bsz, grp, 1, GQ_W), lambda i: (0, rev(i), 0, 0)),
                  pl.BlockSpec((bsz, rws, GV_W), lambda i: (0, rev(i), C_GV // GV_W)),
                  pl.BlockSpec((bsz, grp, GLA_DV, GQ_W), lambda i: (0, rev(i), 0, 0))],
        out_specs=[pl.BlockSpec((bsz, rws, GQ_W), lambda i: (0, rev(i), 0)),
                   pl.BlockSpec((bsz, rws, GQ_W), lambda i: (0, rev(i), 0)),
                   pl.BlockSpec((bsz, rws, GV_W), lambda i: (0, rev(i), 0)),
                   pl.BlockSpec((bsz, grp, 1, GQ_W), lambda i: (0, rev(i), 0, 0))],
        out_shape=[jax.ShapeDtypeStruct((bsz, t_seq, GQ_W), F32), jax.ShapeDtypeStruct((bsz, t_seq, GQ_W), F32),
                   jax.ShapeDtypeStruct((bsz, t_seq, GV_W), BF16), jax.ShapeDtypeStruct((bsz, nc_seq, 1, GQ_W), F32)],
        scratch_shapes=[pltpu.VMEM((bsz, GLA_DV, GQ_W), F32)],
        compiler_params=_cp(VMEM_BIG, ("arbitrary",)),
    )(do, qg, kd, gl, pj, hist)
    n = bsz * t_seq
    return (outs[0].reshape(n, GQ_W), outs[1].reshape(n, GQ_W), outs[2].reshape(n, GV_W),
            outs[3].reshape(bsz * nc_seq, 1, GQ_W))


def _gla_intra_bwd(projp, gates, w2p, gb, do, dqg, dkd, dvi, dgl, *, nc_seq, name):
    n = projp.shape[0]
    nct = n // CHUNK
    scale = GLA_DK ** -0.5

    grp = _group(nc_seq)
    rows = grp * CHUNK

    def inner(gi, qk_ref, v_ref, sb_ref, w2_ref, gb_ref, do_ref, dqg_ref, dkd_ref, dvi_ref, dgl_ref,
              dqk_ref, dv_ref, dsb_ref, dw2_ref, dgb_ref):
        ci = (pl.program_id(0) * grp + gi) % nc_seq
        sb = sb_ref[...]
        w2 = w2_ref[...]
        graw, bc, valid = yield from _gla_gates(sb, w2, gb_ref[...], ci)
        bref = bc[MID:MID + 1, :]
        bl = bc[CHUNK - 1:CHUNK, :]
        q = qk_ref[:, 0:GQ_W].astype(F32) * scale
        k = qk_ref[:, GQ_W:2 * GQ_W].astype(F32)
        ex1 = jnp.exp(bc - bref)
        ex2 = jnp.exp(bref - bc)
        eb = jnp.exp(bc)
        ekd = jnp.exp(bl - bc)
        qi, ki = q * ex1, k * ex2
        r, c = _masks64()
        incl = r >= c
        upper = r <= c
        a_t, da, da_t = [], [], []
        for h in range(NH):
            sl = slice(h * GLA_DV, (h + 1) * GLA_DV)
            doh = do_ref[:, sl]
            vh = v_ref[:, sl]
            a_t.append(jnp.where(upper, _nt(jnp.where(_head_mask(h), ki, 0.0), qi), 0.0))
            da.append(jnp.where(incl, _nt(doh, vh), 0.0))
            da_t.append(jnp.where(upper, _nt(vh, doh), 0.0))
        yield
        dqi = jnp.zeros((CHUNK, GQ_W), F32)
        dki = jnp.zeros((CHUNK, GQ_W), F32)
        for h in range(NH):
            sl = slice(h * GLA_DV, (h + 1) * GLA_DV)
            m = _head_mask(h)
            dv_ref[:, sl] = (_nn(a_t[h], do_ref[:, sl]) + dvi_ref[:, sl]).astype(dv_ref.dtype)
            dqi = dqi + jnp.where(m, _nn(da[h], ki), 0.0)
            dki = dki + jnp.where(m, _nn(da_t[h], qi), 0.0)
        yield
        dqg = dqg_ref[...].astype(F32)
        dkd = dkd_ref[...].astype(F32)
        dqk_ref[:, 0:GQ_W] = ((dqi * ex1 + dqg * eb) * scale).astype(dqk_ref.dtype)
        dqk_ref[:, GQ_W:2 * GQ_W] = (dki * ex2 + dkd * ekd).astype(dqk_ref.dtype)
        t_qi, t_ki, t_kd = dqi * qi, dki * ki, dkd * (k * ekd)
        db = t_qi - t_ki + dqg * (q * eb) - t_kd
        dbref = jnp.sum(t_ki - t_qi, axis=0, keepdims=True)
        dbl = jnp.sum(t_kd, axis=0, keepdims=True) + dgl_ref[0] * jnp.exp(bl)
        rows = _iota2((CHUNK, GQ_W), 0)
        db = db + jnp.where(rows == MID, dbref, 0.0) + jnp.where(rows == CHUNK - 1, dbl, 0.0)
        dg = _tri_sum(upper, db)
        yield
        dgraw = jnp.where(valid, dg * (1.0 / GLA_NORM) * _sigmoid(-graw), 0.0)
        dsb_ref[...] = _nt(dgraw, w2).astype(dsb_ref.dtype)
        dw2 = _tn(sb, dgraw)
        dgb = jnp.sum(dgraw, axis=0, keepdims=True)
        _accumulate(dw2_ref, dw2, gi)
        _accumulate(dgb_ref, dgb, gi)

    rq = pl.BlockSpec((rows, GQ_W), lambda i: (i, 0))
    rv = pl.BlockSpec((rows, GV_W), lambda i: (i, 0))
    kinds = ["row"] * 3 + ["whole"] * 2 + ["row"] * 4 + ["lead"] + ["row"] * 3 + ["whole"] * 2
    return pl.pallas_call(
        _per_chunk(inner, kinds, grp), name=name, grid=(nct // grp,),
        in_specs=[pl.BlockSpec((rows, 2 * GQ_W), lambda i: (i, C_GQK // (2 * GQ_W))),
                  pl.BlockSpec((rows, GV_W), lambda i: (i, C_GV // GV_W)),
                  pl.BlockSpec((rows, LANE), lambda i: (i, 1)),
                  pl.BlockSpec((LANE, GQ_W), lambda i: (0, 0)), pl.BlockSpec((1, GQ_W), lambda i: (0, 0)),
                  rv, rq, rq, rv, pl.BlockSpec((grp, 1, GQ_W), lambda i: (i, 0, 0))],
        out_specs=[pl.BlockSpec((rows, 2 * GQ_W), lambda i: (i, 0)), rv, pl.BlockSpec((rows, LANE), lambda i: (i, 0)),
                   pl.BlockSpec((LANE, GQ_W), lambda i: (0, 0)), pl.BlockSpec((1, GQ_W), lambda i: (0, 0))],
        out_shape=[jax.ShapeDtypeStruct((n, 2 * GQ_W), BF16), jax.ShapeDtypeStruct((n, GV_W), BF16),
                   jax.ShapeDtypeStruct((n, LANE), BF16), jax.ShapeDtypeStruct((LANE, GQ_W), F32),
                   jax.ShapeDtypeStruct((1, GQ_W), F32)],
        compiler_params=_cp(VMEM_BIG, ("arbitrary",)),
    )(projp, projp, gates, w2p, gb, do, dqg, dkd, dvi, dgl)


SECTIONS = ((C_QKV, 1536), (C_DZ, 512), (C_GQK, 512), (C_GV, 512), (C_GR, 512), (C_SA, 128), (C_SB, 128))


def _inproj_bwd(secs, wp, h0, g1, dx1, *, bsz, t_seq, tr, name, rider=None):
    n, d = h0.shape
    per_seq = t_seq // tr
    steps = n // tr
    s_len = t_seq - CHUNK

    def body(*refs):
        sec_refs = refs[:len(SECTIONS)]
        wp_ref, h0_ref, g_ref, dx1_ref, gx_hbm, meta_ref, dg_ref, obuf, sems = refs[len(SECTIONS):]
        i = pl.program_id(0)
        slot = i % 2

        def put(step, slot_, act):
            b, j = step // per_seq, step % per_seq

            @pl.when(j == 0)
            def _():
                act(pltpu.make_async_copy(obuf.at[slot_, pl.ds(CHUNK, tr - CHUNK), :],
                                          gx_hbm.at[b, pl.ds(0, tr - CHUNK), :], sems.at[slot_]))

            if per_seq > 1:
                @pl.when(j > 0)
                def _():
                    act(pltpu.make_async_copy(obuf.at[slot_], gx_hbm.at[b, pl.ds(j * tr - CHUNK, tr), :],
                                              sems.at[slot_]))

        dh = None
        for s_ref, (off, wd) in zip(sec_refs, SECTIONS):
            part = _nt(s_ref[...], wp_ref[:, off:off + wd])
            dh = part if dh is None else dh + part
        dx, dg = _rms_bwd_math(h0_ref[...], g_ref[...], dh)
        tot = dx1_ref[...] + dx

        @pl.when(i >= 2)
        def _():
            put(i - 2, slot, lambda cp: cp.wait())

        obuf[slot] = tot
        put(i, slot, lambda cp: cp.start())

        @pl.when(i % per_seq == 0)
        def _():
            meta_ref[...] = tot[N_PAD:CHUNK, :]

        @pl.when(i == steps - 1)
        def _():
            if steps > 1:
                put(i - 1, 1 - slot, lambda cp: cp.wait())
            put(i, slot, lambda cp: cp.wait())

        @pl.when(i == 0)
        def _():
            dg_ref[...] = dg

        @pl.when(i > 0)
        def _():
            dg_ref[...] += dg

    row = pl.BlockSpec((tr, d), lambda i: (i, 0))
    vec = pl.BlockSpec((1, d), lambda i: (0, 0))
    outs, ridden = _hosted_call(
        body, rider, name=name, grid=(steps,),
        in_specs=[pl.BlockSpec((tr, wd), lambda i: (i, 0)) for _, wd in SECTIONS]
        + [pl.BlockSpec((d, PW), lambda i: (0, 0)), row, vec, row],
        out_specs=[ANY, pl.BlockSpec((N_META, d), lambda i: (i // per_seq, 0)), vec],
        out_shape=[jax.ShapeDtypeStruct((bsz, s_len, d), F32), jax.ShapeDtypeStruct((bsz * N_META, d), F32),
                   jax.ShapeDtypeStruct((1, d), F32)],
        scratch_shapes=[pltpu.VMEM((2, tr, d), F32), pltpu.SemaphoreType.DMA((2,))],
        compiler_params=_cp(VMEM_BIG, ("arbitrary",)), args=(*secs, wp, h0, g1, dx1))
    return (*outs, ridden)


def _adamw(w, g, m, v, *, name, emit_grad=False, col_tile=None):
    lead = w.ndim - 2
    r, c = w.shape[-2:]
    tr = r if col_tile is not None else (_tile(r, 256, 8) if r > 256 else r)
    tc = col_tile if col_tile is not None else c
    c1 = 1.0 - ADAM_B1 ** ADAM_STEP
    c2 = 1.0 - ADAM_B2 ** ADAM_STEP
    n_out = 4 if emit_grad else 3

    def body(w_ref, g_ref, m_ref, v_ref, *out_refs):
        rd = (lambda ref: ref[0]) if lead else (lambda ref: ref[...])
        gv = g_ref[:, 0:tc]
        nm = ADAM_B1 * rd(m_ref) + (1.0 - ADAM_B1) * gv
        nv = ADAM_B2 * rd(v_ref) + (1.0 - ADAM_B2) * (gv * gv)
        res = [-ADAM_LR * ((nm / c1) / (jnp.sqrt(nv / c2) + ADAM_EPS) + ADAM_WD * rd(w_ref)), nm, nv, gv]
        for o_ref, val in zip(out_refs, res):
            if lead:
                o_ref[0] = val
            else:
                o_ref[...] = val

    if col_tile is None:
        blk = pl.BlockSpec((1,) * lead + (tr, c), lambda i: (0,) * lead + (i, 0))
        gblk = pl.BlockSpec((tr, g.shape[1]), lambda i: (i, 0))
        steps = r // tr
    else:
        blk = pl.BlockSpec((1,) * lead + (r, tc), lambda j: (0,) * lead + (0, j))
        gblk = pl.BlockSpec((r, tc), lambda j: (0, j))
        steps = c // tc
    sds = jax.ShapeDtypeStruct(w.shape, F32)
    return pl.pallas_call(
        body, name=name, grid=(steps,), in_specs=[blk, gblk, blk, blk], out_specs=[blk] * n_out,
        out_shape=[sds] * n_out, compiler_params=_cp(VMEM_BIG),
    )(w, g, m, v)


def _adamw_rows(w, g, m, v, *, name):
    r, _, c = w.shape
    tr = max(t for t in range(1, 129) if r % t == 0)
    c1 = 1.0 - ADAM_B1 ** ADAM_STEP
    c2 = 1.0 - ADAM_B2 ** ADAM_STEP

    def body(w_ref, g_ref, m_ref, v_ref, d_ref, nm_ref, nv_ref, go_ref):
        gv = g_ref[...]
        nm = ADAM_B1 * m_ref[...] + (1.0 - ADAM_B1) * gv
        nv = ADAM_B2 * v_ref[...] + (1.0 - ADAM_B2) * (gv * gv)
        d_ref[...] = -ADAM_LR * ((nm / c1) / (jnp.sqrt(nv / c2) + ADAM_EPS) + ADAM_WD * w_ref[...])
        nm_ref[...] = nm
        nv_ref[...] = nv
        go_ref[...] = gv

    blk = pl.BlockSpec((tr, 1, c), lambda i: (i, 0, 0))
    sds = jax.ShapeDtypeStruct(w.shape, F32)
    return pl.pallas_call(
        body, name=name, grid=(r // tr,), in_specs=[blk] * 4, out_specs=[blk] * 4, out_shape=[sds] * 4,
        compiler_params=_cp(VMEM_BIG),
    )(w, g, m, v)


def _pair_sum(where, g, theirs, *, name):
    lead, r, cols = g.shape
    half = r // 2
    tr = _tile(half, 256, 16)
    nh = half // tr

    def body(w_ref, a_ref, b_ref, o_ref):
        o_ref[...] = (a_ref[...] + b_ref[...]).astype(o_ref.dtype)

    blk = pl.BlockSpec((1, tr, cols), lambda s, i, w: (s, i, 0))
    return pl.pallas_call(
        body, name=name,
        grid_spec=pltpu.PrefetchScalarGridSpec(
            num_scalar_prefetch=1, grid=(lead, nh),
            in_specs=[pl.BlockSpec((1, tr, cols), lambda s, i, w: (s, w[0] * nh + i, 0)), blk], out_specs=blk),
        out_shape=jax.ShapeDtypeStruct((lead, half, cols), BF16), compiler_params=_cp(VMEM_BIG),
    )(where, g, theirs)


def _chip_sum(where, pair, q, *, name):
    _, half, cols = pair.shape
    tr = _tile(half, 256, 16)
    nh = half // tr

    def body(w_ref, own_ref, q1_ref, q2_ref, q3_ref, o_ref):
        f = lambda ref: ref[0].astype(F32)
        o_ref[...] = ((f(own_ref) + f(q1_ref)) + f(q2_ref)) + f(q3_ref)

    def peer(d):
        return pl.BlockSpec((1, tr, cols), lambda i, w: ((w[1] + d) % N_CHIPS, i, 0))

    return pl.pallas_call(
        body, name=name,
        grid_spec=pltpu.PrefetchScalarGridSpec(
            num_scalar_prefetch=1, grid=(nh,),
            in_specs=[peer(0), peer(1), peer(2), peer(3)],
            out_specs=pl.BlockSpec((tr, cols), lambda i, w: (w[0] * nh + i, 0))),
        out_shape=jax.ShapeDtypeStruct((2 * half, cols), F32), compiler_params=_cp(VMEM_BIG),
    )(where, pair, q, q, q)


VM = pl.BlockSpec(memory_space=pltpu.VMEM)


def _row_chunks(rows, n_split):
    size = rows // n_split
    assert size * n_split == rows and size % 16 == 0, (rows, n_split)
    return [(s, pl.ds(s * size, size)) for s in range(n_split)], size


D2D_SPLIT = 4
ICI_SPLIT = 2


def _sibling_halves(grads):
    n_arr = len(grads)

    def body(*refs):
        ins = refs[:n_arr]
        theirs = refs[n_arr:2 * n_arr]
        send_sems, recv_sems = refs[2 * n_arr:]
        x, y, c = _place()
        copies = []
        for k in range(n_arr):
            half = ins[k].shape[1] // 2
            chunks, size = _row_chunks(half, D2D_SPLIT)
            for s, dst_rows in chunks:
                give = pltpu.make_async_remote_copy(
                    src_ref=ins[k].at[:, pl.ds((1 - c) * half + s * size, size), :], dst_ref=theirs[k].at[:, dst_rows, :],
                    send_sem=send_sems.at[k, s], recv_sem=recv_sems.at[k, s], device_id=(x, y, 1 - c),
                    device_id_type=MESH)
                give.start()
                copies.append(give)
        for give in copies:
            give.wait()

    halves = [jax.ShapeDtypeStruct((g.shape[0], g.shape[1] // 2, g.shape[2]), F32) for g in grads]
    sem = pltpu.SemaphoreType.DMA((n_arr, D2D_SPLIT))
    return pl.pallas_call(
        body, name="sibling_halves", in_specs=[ANY] * n_arr, out_specs=[ANY] * n_arr, out_shape=halves,
        scratch_shapes=[sem, sem],
    )(*grads)


def _chip_exchange(parts):
    n_arr = len(parts)

    def body(*refs):
        ins = refs[:n_arr]
        outs = refs[n_arr:2 * n_arr]
        send_sems, recv_sems = refs[2 * n_arr:]
        x, y, c = _place()
        me = 2 * x + y
        sends = []
        for k in range(n_arr):
            chunks, _ = _row_chunks(ins[k].shape[1], ICI_SPLIT)
            for d, (px, py, pj) in enumerate(_other_chips(x, y)):
                for s, rows in chunks:
                    cp = pltpu.make_async_remote_copy(
                        src_ref=ins[k].at[pj, rows, :], dst_ref=outs[k].at[me, rows, :], send_sem=send_sems.at[k, d, s],
                        recv_sem=recv_sems.at[k, d, s], device_id=(px, py, c), device_id_type=MESH)
                    cp.start()
                    sends.append(cp)
        for k in range(n_arr):
            chunks, _ = _row_chunks(ins[k].shape[1], ICI_SPLIT)
            for d, (px, py, pj) in enumerate(_other_chips(x, y)):
                for s, rows in chunks:
                    pltpu.make_async_remote_copy(
                        src_ref=ins[k].at[pj, rows, :], dst_ref=outs[k].at[pj, rows, :], send_sem=send_sems.at[k, d, s],
                        recv_sem=recv_sems.at[k, d, s], device_id=(px, py, c), device_id_type=MESH).wait_recv()
        for cp in sends:
            cp.wait_send()

    sem = pltpu.SemaphoreType.DMA((n_arr, 3, ICI_SPLIT))
    return pl.pallas_call(
        body, name="chip_exchange", in_specs=[ANY] * n_arr, out_specs=[ANY] * n_arr,
        out_shape=[jax.ShapeDtypeStruct(p.shape, p.dtype) for p in parts],
        scratch_shapes=[sem, sem],
    )(*parts)


class _SiblingHalvesRider:
    def __init__(self, grads):
        self.inputs = list(grads)
        self.out_shapes = [jax.ShapeDtypeStruct((g.shape[0], g.shape[1] // 2, g.shape[2]), F32) for g in grads]
        self.aliases = {}
        self.sems = [pltpu.SemaphoreType.DMA((len(grads), D2D_SPLIT))] * 2

    def _copies(self, ins, outs, sems):
        x, y, c = _place()
        for k in range(len(ins)):
            half = ins[k].shape[1] // 2
            chunks, size = _row_chunks(half, D2D_SPLIT)
            for s, dst_rows in chunks:
                yield pltpu.make_async_remote_copy(
                    src_ref=ins[k].at[:, pl.ds((1 - c) * half + s * size, size), :], dst_ref=outs[k].at[:, dst_rows, :],
                    send_sem=sems[0].at[k, s], recv_sem=sems[1].at[k, s], device_id=(x, y, 1 - c), device_id_type=MESH)

    def first(self, ins, outs, sems):
        for cp in self._copies(ins, outs, sems):
            cp.start()

    def last(self, ins, outs, sems):
        for cp in self._copies(ins, outs, sems):
            cp.wait()


class _ChipExchangeRider:
    def __init__(self, parts):
        self.inputs = list(parts)
        self.out_shapes = [jax.ShapeDtypeStruct(p.shape, p.dtype) for p in parts]
        self.aliases = {}
        self.sems = [pltpu.SemaphoreType.DMA((len(parts), 3, ICI_SPLIT))] * 2

    def _copies(self, ins, outs, sems, receiving):
        x, y, c = _place()
        for k in range(len(ins)):
            chunks, _ = _row_chunks(ins[k].shape[1], ICI_SPLIT)
            for d, (px, py, pj) in enumerate(_other_chips(x, y)):
                for s, rows in chunks:
                    yield pltpu.make_async_remote_copy(
                        src_ref=ins[k].at[pj, rows, :], dst_ref=outs[k].at[pj if receiving else 2 * x + y, rows, :],
                        send_sem=sems[0].at[k, d, s], recv_sem=sems[1].at[k, d, s], device_id=(px, py, c),
                        device_id_type=MESH)

    def first(self, ins, outs, sems):
        for cp in self._copies(ins, outs, sems, False):
            cp.start()

    def last(self, ins, outs, sems):
        for cp in self._copies(ins, outs, sems, True):
            cp.wait_recv()
        for cp in self._copies(ins, outs, sems, False):
            cp.wait_send()


def _sibling_join(bufs):
    n_arr = len(bufs)

    def body(*refs):
        bufs_out = refs[n_arr:2 * n_arr]
        send_sems, recv_sems = refs[2 * n_arr:]
        x, y, c = _place()
        copies = []
        for k in range(n_arr):
            half = bufs_out[k].shape[0] // 2
            chunks, size = _row_chunks(half, D2D_SPLIT)
            for s, _ in chunks:
                rows = pl.ds(c * half + s * size, size)
                give = pltpu.make_async_remote_copy(
                    src_ref=bufs_out[k].at[rows, :], dst_ref=bufs_out[k].at[rows, :], send_sem=send_sems.at[k, s],
                    recv_sem=recv_sems.at[k, s], device_id=(x, y, 1 - c), device_id_type=MESH)
                give.start()
                copies.append((k, s, half, size, give))
        for k, s, half, size, give in copies:
            rows = pl.ds((1 - c) * half + s * size, size)
            pltpu.make_async_remote_copy(
                src_ref=bufs_out[k].at[rows, :], dst_ref=bufs_out[k].at[rows, :], send_sem=send_sems.at[k, s],
                recv_sem=recv_sems.at[k, s], device_id=(x, y, 1 - c), device_id_type=MESH).wait_recv()
            give.wait_send()

    sem = pltpu.SemaphoreType.DMA((n_arr, D2D_SPLIT))
    return pl.pallas_call(
        body, name="sibling_join", in_specs=[ANY] * n_arr, out_specs=[ANY] * n_arr,
        out_shape=[jax.ShapeDtypeStruct(b.shape, F32) for b in bufs],
        input_output_aliases={k: k for k in range(n_arr)},
        scratch_shapes=[sem, sem],
    )(*bufs)


PACK_ROWS = 48


def _small_allreduce(pack):
    masks = [(dx, dy, dc) for dx in (0, 1) for dy in (0, 1) for dc in (0, 1)][1:]

    def body(p_ref, o_ref, buf, send_sems, recv_sems):
        x, y, c = _place()
        me = 4 * x + 2 * y + c
        buf[me] = p_ref[...]
        sends = []
        for k, (dx, dy, dc) in enumerate(masks):
            peer = (1 - x if dx else x, 1 - y if dy else y, 1 - c if dc else c)
            cp = pltpu.make_async_remote_copy(
                src_ref=p_ref, dst_ref=buf.at[me], send_sem=send_sems.at[k], recv_sem=recv_sems.at[k],
                device_id=peer, device_id_type=MESH)
            cp.start()
            sends.append(cp)
        for k, (dx, dy, dc) in enumerate(masks):
            peer = (1 - x if dx else x, 1 - y if dy else y, 1 - c if dc else c)
            pj = 4 * peer[0] + 2 * peer[1] + peer[2]
            pltpu.make_async_remote_copy(
                src_ref=p_ref, dst_ref=buf.at[pj], send_sem=send_sems.at[k], recv_sem=recv_sems.at[k],
                device_id=peer, device_id_type=MESH).wait_recv()
        for cp in sends:
            cp.wait_send()
        tot = buf[0]
        for k in range(1, 8):
            tot = tot + buf[k]
        o_ref[...] = tot
        o_ref[0:N_META, :] = tot[0:N_META] + tot[N_META:2 * N_META]

    return pl.pallas_call(
        body, name="small_allreduce", in_specs=[VM], out_specs=VM,
        out_shape=jax.ShapeDtypeStruct((PACK_ROWS, D_MODEL), F32),
        scratch_shapes=[pltpu.VMEM((8, PACK_ROWS, D_MODEL), F32), pltpu.SemaphoreType.DMA((7,)),
                        pltpu.SemaphoreType.DMA((7,))],
    )(pack)


def _pad_lanes(vec, offset):
    k = vec.shape[1]
    return jnp.concatenate([jnp.zeros((1, offset), F32), vec, jnp.zeros((1, LANE - offset - k), F32)], axis=1)


def _local_step(x, tgt, meta, norm1_g, wp, conv_w, a_log, dt_bias, dn_norm_g, gla_w2, gla_b, gla_norm_g,
                w_out, norm2_g, w_up, w_down, final_norm_g, late_gather=None, where=None, early_gather=None):
    bsz, s_len, d = x.shape
    t_seq = s_len + CHUNK
    nc_seq = t_seq // CHUNK
    n = bsz * t_seq
    tr = _tile(t_seq, 832)
    tt = _tile(t_seq, 416)

    lead = jnp.concatenate([jnp.zeros((N_PAD, d), F32), meta], axis=0)
    alog_row = _pad_lanes(a_log, 4)
    dtb_row = _pad_lanes(dt_bias, 4)

    h0, h, got_early = _embed_norm(x, lead, norm1_g, tr=tr, name="embed_norm1", rider=early_gather)
    if early_gather is not None:
        wp = _padded_from_shards(got_early[0])
        conv_w = got_early[1].transpose(1, 0, 2).reshape(4, QKV_W)
        gla_w2 = got_early[2][:, :, 0:GQ_W // N_CHIPS].transpose(1, 0, 2).reshape(GLA_RANK, GQ_W)
    w2p = jnp.concatenate([gla_w2, jnp.zeros((LANE - GLA_RANK, GQ_W), F32)], axis=0)
    ride_up, ride_down, ride_out = late_gather if late_gather is not None else (None, None, None)
    projp, gates = _mm(h, wp, "nn", tm=tt, tn=PW, tk=d, out_dtypes=(BF16, F32), out_widths=(PW, PW - C_SA),
                       epilogue=lambda acc: (acc, acc[:, C_SA:PW]), name="in_proj")
    qn, kn, v, got_up = _dnprep_fwd(projp, conv_w, bsz=bsz, t_seq=t_seq, tt=tt, name="dn_prep", rider=ride_up)
    u, w, qg, kd, pmat, tmat, gl, got_down = _dn_intra_fwd(qn, kn, v, gates, alog_row, dtb_row, nc_seq=nc_seq,
                                                           name="dn_intra", rider=ride_down)
    o_dn, vn, hist, got_out = _dn_scan_fwd(u, w, qg, kd, pmat, gl, bsz=bsz, nc_seq=nc_seq, name="dn_scan",
                                           rider=ride_out)
    oi, gqg, gkd, ggl = _gla_intra_fwd(projp, gates, w2p, gla_b, nc_seq=nc_seq, name="gla_intra")
    o_gla, ghist, _ = _gla_scan_fwd(oi, gqg, gkd, ggl, projp, bsz=bsz, nc_seq=nc_seq, name="gla_scan")
    if late_gather is not None:
        w_out = got_out[0].reshape(d, d)
        w_up = got_up[0].transpose(1, 0, 2).reshape(d, D_FF)
        w_down = got_down[0].reshape(D_FF, d)
    mix = _gnorm_fwd(o_dn, o_gla, projp, dn_norm_g, gla_norm_g, tr=tr, name="gated_norm")
    def residual_norm(acc, res, g):
        x1v = res + acc
        r = lax.rsqrt(jnp.mean(x1v * x1v, axis=-1, keepdims=True) + EPS)
        return x1v, x1v * r * g

    x1, h2 = _mm(mix, w_out, "nn", tm=tr, tn=d, tk=d, out_dtypes=(F32, BF16), extras=(h0,), vec_extras=(norm2_g,),
                 epilogue=residual_norm, name="out_proj_norm2")

    (act,) = _mm(h2, w_up, "nn", tm=tt, tn=D_FF, tk=d, out_dtypes=(BF16,),
                 epilogue=lambda acc: (jnp.square(jnp.maximum(acc, 0.0)),), name="mlp_up", n_chunk=1024)
    dx2, dx2b, d_final_g, loss_tile = _mlp_down_loss(act, w_down, x1, final_norm_g, tgt, t_seq=t_seq, tr=tt,
                                                     name="mlp_down_loss")

    def relu2_bwd(acc, a):
        a = a.astype(F32)
        return (acc * (2.0 * (a * lax.rsqrt(jnp.maximum(a, F32_TINY)))),)

    (dup,) = _mm(dx2b, w_down, "nt", tm=tt, tn=D_FF, tk=d, out_dtypes=(BF16,), extras=(act,),
                 epilogue=relu2_bwd, name="mlp_down_bwd", n_chunk=1024)
    tk2 = 2 * tr if n % (2 * tr) == 0 else tr
    (d_w_down,) = _mm(act, dx2b, "tn", tm=D_FF // 2, tn=d, tk=tk2, out_dtypes=(F32,), name="w_down_grad")
    (d_w_up_sm,) = _mm(h2, dup, "tn", tm=d, tn=D_FF // 2, tk=tk2, out_dtypes=(F32,), name="w_up_grad",
                       shard_cols=D_FF // N_CHIPS)
    mlp_sm = [d_w_up_sm, d_w_down.reshape(N_CHIPS, D_FF // N_CHIPS, d)]
    ride1 = _SiblingHalvesRider(mlp_sm) if where is not None else None
    dx1, dx1b, d_norm2_g, theirs = _mlp_up_bwd_norm(dup, w_up, x1, norm2_g, dx2, tr=tt, name="mlp_up_bwd_norm",
                                                    rider=ride1)
    ride2 = None
    if where is not None:
        mlp_pair = [_pair_sum(where, a, b, name=f"pair_sum_mlp{k}") for k, (a, b) in enumerate(zip(mlp_sm, theirs))]
        ride2 = _ChipExchangeRider(mlp_pair)

    (dmix,) = _mm(dx1b, w_out, "nt", tm=tr, tn=d, tk=d, out_dtypes=(BF16,), name="out_proj_bwd")
    (d_w_out,) = _mm(mix, dx1b, "tn", tm=d, tn=d, tk=tr, out_dtypes=(F32,), name="w_out_grad")
    do_dn, ddz, do_gla, dgr, d_dn_norm_g, d_gla_norm_g = _gnorm_bwd(
        dmix, o_dn, o_gla, projp, dn_norm_g, gla_norm_g, tr=tr, name="gated_norm_bwd")
    du, dw, dqg, dkd, dgl = _dn_scan_bwd(do_dn, w, qg, kd, vn, pmat, gl, hist, bsz=bsz, nc_seq=nc_seq,
                                          name="dn_scan_bwd")
    dqn, dkn, dv, dsa, d_alog, d_dtb, mlp_parts = _dn_intra_bwd(
        qn, kn, v, gates, alog_row, dtb_row, u, w, tmat, du, dw, dqg, dkd, do_dn, vn, dgl, nc_seq=nc_seq,
        name="dn_intra_bwd", rider=ride2)
    dz, d_conv_w = _dnprep_bwd_a(projp, conv_w, dqn, dkn, dv, bsz=bsz, t_seq=t_seq, tt=tt, name="dn_prep_bwd")
    dcin = _dnprep_bwd_b(dz, conv_w, bsz=bsz, t_seq=t_seq, tt=tt, name="conv_bwd")
    gdqg, gdkd, gdvi, gdgl = _gla_scan_bwd(do_gla, gqg, gkd, ggl, projp, ghist, bsz=bsz, nc_seq=nc_seq,
                                            name="gla_scan_bwd")
    dgqk, dgv, dsb, d_w2p, d_gla_b = _gla_intra_bwd(projp, gates, w2p, gla_b, do_gla, gdqg, gdkd, gdvi, gdgl,
                                                    nc_seq=nc_seq, name="gla_intra_bwd")

    secs = (dcin, ddz, dgqk, dgv, dgr, dsa, dsb)
    g_lo = _grad_tn(h, secs[0:2], tk=tk2, name="w_in_grad_lo")
    g_hi = _grad_tn(h, secs[2:7], tk=tk2, name="w_in_grad_hi")
    ride3 = None
    if where is not None:
        late_sm = [_shards_from_padded(g_lo, g_hi), d_w_out.reshape(N_CHIPS, d // N_CHIPS, d)]
        late_theirs = _exchange_now(_SiblingHalvesRider(late_sm), name="sibling_halves")
        late_pair = [_pair_sum(where, a, b, name=f"pair_sum_{k}") for k, (a, b) in enumerate(zip(late_sm, late_theirs))]
        ride3 = _ChipExchangeRider(late_pair)
    grad_x, d_meta_rows, d_norm1_g, late_parts = _inproj_bwd(secs, wp, h0, norm1_g, dx1, bsz=bsz, t_seq=t_seq, tr=tt,
                                                             name="in_proj_bwd", rider=ride3)

    grads = dict(w_in_lo=g_lo, w_in_hi=g_hi, w_out=d_w_out, w_up_shards=d_w_up_sm, w_down=d_w_down, meta_rows=d_meta_rows,
                 norm1_g=d_norm1_g, conv_w=d_conv_w, a_log_tile=d_alog, dt_bias_tile=d_dtb, dn_norm_g=d_dn_norm_g,
                 gla_w2=d_w2p[0:GLA_RANK], gla_b=d_gla_b, gla_norm_g=d_gla_norm_g, norm2_g=d_norm2_g,
                 final_norm_g=d_final_g, loss_tile=loss_tile)
    if where is not None:
        grads["exchanged"] = (late_pair + mlp_pair, list(late_parts) + list(mlp_parts))
    return grad_x, grads


SHARD_W = IN_WIDTH // N_CHIPS
PADDED_ORDER = ((0, 2048), (2056, 3592), (2048, 2056), LANE - 8, (3592, 3608), LANE - GLA_RANK)


def _pad_layout(w_full):
    pieces = [jnp.zeros((w_full.shape[0], seg), w_full.dtype) if isinstance(seg, int) else w_full[:, seg[0]:seg[1]]
              for seg in PADDED_ORDER]
    return jnp.concatenate(pieces, axis=1)


def _padded_from_shards(stack):
    pieces = []
    for seg in PADDED_ORDER:
        if isinstance(seg, int):
            pieces.append(jnp.zeros((stack.shape[1], seg), stack.dtype))
            continue
        for j in range(N_CHIPS):
            lo, hi = max(seg[0], j * SHARD_W), min(seg[1], (j + 1) * SHARD_W)
            if lo < hi:
                pieces.append(stack[j, :, lo - j * SHARD_W:hi - j * SHARD_W])
    return jnp.concatenate(pieces, axis=1)


def _shards_from_padded(g_lo, g_hi):
    split = g_lo.shape[1]
    starts, pos = [], 0
    for seg in PADDED_ORDER:
        width = seg if isinstance(seg, int) else seg[1] - seg[0]
        if not isinstance(seg, int):
            starts.append((seg[0], seg[1], pos))
        pos += width
    shards = []
    for j in range(N_CHIPS):
        pieces = []
        for a, b, p0 in sorted(starts):
            lo, hi = max(a, j * SHARD_W), min(b, (j + 1) * SHARD_W)
            if lo < hi:
                src, off = (g_lo, 0) if p0 < split else (g_hi, split)
                pieces.append(src[:, p0 + lo - a - off:p0 + hi - a - off])
        pieces.append(jnp.zeros((g_lo.shape[0], D_MODEL - SHARD_W), g_lo.dtype))
        shards.append(jnp.concatenate(pieces, axis=1))
    return jnp.stack(shards)


def _pack_small(g, bsz):
    assert bsz * N_META == 32
    row = jnp.concatenate([g["a_log_tile"], g["dt_bias_tile"], g["dn_norm_g"], g["gla_norm_g"], g["gla_b"],
                           g["loss_tile"], jnp.zeros((1, LANE), F32)], axis=1)
    return jnp.concatenate([g["meta_rows"], g["norm1_g"], g["conv_w"].reshape(6, D_MODEL), row,
                            g["gla_w2"].reshape(4, D_MODEL), g["norm2_g"], g["final_norm_g"],
                            jnp.zeros((2, D_MODEL), F32)], axis=0)


def kernel(x, meta_tokens, norm1_g, w_in, conv_w, a_log, dt_bias, dn_norm_g, gla_w2, gla_b, gla_norm_g, w_out, norm2_g, w_up, w_down, final_norm_g, loss_target, m_meta_tokens, m_norm1_g, m_w_in, m_conv_w, m_a_log, m_dt_bias, m_dn_norm_g, m_gla_w2, m_gla_b, m_gla_norm_g, m_w_out, m_norm2_g, m_w_up, m_w_down, m_final_norm_g, v_meta_tokens, v_norm1_g, v_w_in, v_conv_w, v_a_log, v_dt_bias, v_dn_norm_g, v_gla_w2, v_gla_b, v_gla_norm_g, v_w_out, v_norm2_g, v_w_up, v_w_down, v_final_norm_g):
    bsz = x.shape[0]
    chip = 2 * lax.axis_index("x") + lax.axis_index("y")

    lane_pad = lambda a, wd: jnp.pad(a, ((0, 0), (0, wd - a.shape[1])))
    where = jnp.stack([lax.axis_index("c"), chip]).astype(jnp.int32)
    slot = lambda a, dt, nm: _to_slot(where, a, dt, name="slot_" + nm)
    (g_meta,) = _exchange_now(_GatherRider([slot(meta_tokens, F32, "meta")], [False]), name="gather_meta")
    early = _GatherRider([slot(lane_pad(w_in[0], D_MODEL), BF16, "w_in"), slot(conv_w[0], F32, "conv"),
                          slot(lane_pad(gla_w2[0], LANE), F32, "gla_w2")], [True, False, False])
    late = (_GatherRider([slot(w_up[0], BF16, "w_up")], [True]), _GatherRider([slot(w_down[0], BF16, "w_down")], [True]),
            _GatherRider([slot(w_out[0], BF16, "w_out")], [True]))
    meta_f = g_meta.transpose(1, 0, 2).reshape(N_META, D_MODEL)

    grad_x, g = _local_step(x, loss_target, meta_f, norm1_g, None, None, a_log, dt_bias, dn_norm_g, None, gla_b,
                            gla_norm_g, None, norm2_g, None, None, final_norm_g.reshape(1, D_MODEL), late_gather=late,
                            where=where, early_gather=early)

    pair, parts = g["exchanged"]
    halves = [_chip_sum(where, p, q, name=f"chip_sum_{k}") for k, (p, q) in enumerate(zip(pair, parts))]
    gw_in, gw_out, gw_up, gw_down = _sibling_join(halves)

    red = _small_allreduce(_pack_small(g, bsz))
    g_meta_full = red[0:N_META]
    g_norm1 = red[32:33]
    g_conv_full = red[33:39].reshape(4, QKV_W)
    srow = red[39:40]
    g_alog, g_dtb = srow[:, 4:8], srow[:, LANE + 4:LANE + 8]
    g_dn_norm, g_gla_norm = srow[:, 2 * LANE:3 * LANE], srow[:, 3 * LANE:4 * LANE]
    g_gla_b = srow[:, 4 * LANE:6 * LANE]
    loss = srow[0, 6 * LANE]
    g_w2_full = red[40:44].reshape(GLA_RANK, GQ_W)
    g_norm2 = red[44:45]
    g_final = red[45:46]
    g_meta_sh = lax.dynamic_slice_in_dim(g_meta_full, chip * (D_MODEL // N_CHIPS), D_MODEL // N_CHIPS, axis=1)
    g_conv_sh = lax.dynamic_slice_in_dim(g_conv_full, chip * (QKV_W // N_CHIPS), QKV_W // N_CHIPS, axis=1)
    g_w2_sh = lax.dynamic_slice_in_dim(g_w2_full, chip * (GQ_W // N_CHIPS), GQ_W // N_CHIPS, axis=1)

    names = ["meta_tokens", "norm1_g", "w_in", "conv_w", "a_log", "dt_bias", "dn_norm_g", "gla_w2", "gla_b",
             "gla_norm_g", "w_out", "norm2_g", "w_up", "w_down", "final_norm_g"]
    weights = dict(meta_tokens=meta_tokens, norm1_g=norm1_g, w_in=w_in, conv_w=conv_w, a_log=a_log, dt_bias=dt_bias,
                   dn_norm_g=dn_norm_g, gla_w2=gla_w2, gla_b=gla_b, gla_norm_g=gla_norm_g, w_out=w_out,
                   norm2_g=norm2_g, w_up=w_up, w_down=w_down, final_norm_g=final_norm_g)
    ms = dict(meta_tokens=m_meta_tokens, norm1_g=m_norm1_g, w_in=m_w_in, conv_w=m_conv_w, a_log=m_a_log,
              dt_bias=m_dt_bias, dn_norm_g=m_dn_norm_g, gla_w2=m_gla_w2, gla_b=m_gla_b, gla_norm_g=m_gla_norm_g,
              w_out=m_w_out, norm2_g=m_norm2_g, w_up=m_w_up, w_down=m_w_down, final_norm_g=m_final_norm_g)
    vs = dict(meta_tokens=v_meta_tokens, norm1_g=v_norm1_g, w_in=v_w_in, conv_w=v_conv_w, a_log=v_a_log,
              dt_bias=v_dt_bias, dn_norm_g=v_dn_norm_g, gla_w2=v_gla_w2, gla_b=v_gla_b, gla_norm_g=v_gla_norm_g,
              w_out=v_w_out, norm2_g=v_norm2_g, w_up=v_w_up, w_down=v_w_down, final_norm_g=v_final_norm_g)
    grads2d = dict(meta_tokens=g_meta_sh, norm1_g=g_norm1, w_in=gw_in, conv_w=g_conv_sh, a_log=g_alog, dt_bias=g_dtb,
                   dn_norm_g=g_dn_norm, gla_w2=g_w2_sh, gla_b=g_gla_b, gla_norm_g=g_gla_norm, w_out=gw_out,
                   norm2_g=g_norm2, w_up=gw_up, w_down=gw_down, final_norm_g=g_final)
    out_g, out_d, out_m, out_v = [], [], [], []
    for nm in names:
        shape = weights[nm].shape
        g2 = grads2d[nm]
        if nm == "w_in":
            tview = lambda a: jnp.transpose(a, (2, 0, 1))
            res = _adamw_rows(tview(weights[nm]), g2[:, 0:SHARD_W].T.reshape(SHARD_W, 1, D_MODEL), tview(ms[nm]),
                              tview(vs[nm]), name=f"adamw_{nm}")
            res = [jnp.transpose(a, (1, 2, 0)) for a in res]
            gout = res[3]
        elif len(shape) == 3:
            res = _adamw(weights[nm], g2, ms[nm], vs[nm], name=f"adamw_{nm}")
            gout = g2.reshape(shape)
        else:
            as2d = lambda a: a.reshape(g2.shape)
            res = _adamw(as2d(weights[nm]), g2, as2d(ms[nm]), as2d(vs[nm]), name=f"adamw_{nm}")
            gout = g2.reshape(shape)
        out_g.append(gout)
        out_d.append(res[0].reshape(shape))
        out_m.append(res[1].reshape(shape))
        out_v.append(res[2].reshape(shape))
    return (loss, grad_x, *out_g, *out_d, *out_m, *out_v)
```

```python
import functools

import jax
import jax.numpy as jnp
import numpy as np
from jax import lax
from jax.experimental import pallas as pl
from jax.experimental.pallas import tpu as pltpu

F32 = jnp.float32
BF16 = jnp.bfloat16
HI = lax.Precision.HIGHEST
MESH = pl.DeviceIdType.MESH

D_MODEL = 1024
N_META = 16
CHUNK = 64
N_PAD = CHUNK - N_META
NH = 4
DN_D = 128
GLA_DK = 64
GLA_DV = 128
GLA_RANK = 16
D_FF = 4 * D_MODEL
EPS = 1e-6
F32_TINY = 1.1754944e-38
IN_WIDTH = 3608
C_QKV, C_DZ, C_GQK, C_GV, C_GR, C_SA, C_SB, PW = 0, 1536, 2048, 2560, 3072, 3584, 3712, 3840
LANE = 128
N_CHIPS = 4

ADAM_LR, ADAM_B1, ADAM_B2, ADAM_EPS, ADAM_WD, ADAM_STEP = 0.001, 0.9, 0.999, 1e-08, 0.01, 10

VMEM_BIG = 56 * 1024 * 1024


def _cp(vmem=None, sem=None):
    kw = {}
    if vmem is not None:
        kw["vmem_limit_bytes"] = vmem
    if sem is not None:
        kw["dimension_semantics"] = sem
    return pltpu.CompilerParams(**kw)


def _tile(n, target, mult=16):
    best = None
    for t in range(mult, min(n, target) + 1, mult):
        if n % t == 0:
            best = t
    assert best is not None, (n, target)
    return best


def _dot(a, b, dims, prec=None):
    return lax.dot_general(a, b, (dims, ((), ())), preferred_element_type=F32, precision=prec)


def _nn(a, b):
    return _dot(a.astype(BF16), b.astype(BF16), ((1,), (0,)))


def _nt(a, b):
    return _dot(a.astype(BF16), b.astype(BF16), ((1,), (1,)))


def _tn(a, b):
    return _dot(a.astype(BF16), b.astype(BF16), ((0,), (0,)))


def _split(x):
    hi = x.astype(BF16)
    return hi, (x - hi.astype(F32)).astype(BF16)


def _tri_sum(tri, x):
    t = tri.astype(BF16)
    hi = x.astype(BF16)
    r1 = x - hi.astype(F32)
    mid = r1.astype(BF16)
    lo = (r1 - mid.astype(F32)).astype(BF16)
    nn = ((1,), (0,))
    return _dot(t, hi, nn) + _dot(t, mid, nn) + _dot(t, lo, nn)


def _sigmoid(x):
    return 0.5 * jnp.tanh(0.5 * x) + 0.5


def _softplus(x):
    return jnp.maximum(x, 0.0) + jnp.log(1.0 + jnp.exp(-jnp.abs(x)))


def _logsigmoid(x):
    return -_softplus(-x)


def _iota2(shape, dim):
    return lax.broadcasted_iota(jnp.int32, shape, dim)


def _mm(a, b, mode, *, tm, tn, tk, out_dtypes, extras=(), epilogue=None, name, vmem=VMEM_BIG, rider=None,
        out_widths=None, n_chunk=None, vec_extras=(), shard_cols=None):
    if mode == "tn":
        K, M = a.shape
    else:
        M, K = a.shape
    N = b.shape[0] if mode == "nt" else b.shape[1]
    assert M % tm == 0 and N % tn == 0 and K % tk == 0, (name, M, N, K, tm, tn, tk)
    nk = K // tk
    n_ex, n_out, n_vec = len(extras), len(out_dtypes), len(vec_extras)
    if mode == "tn":
        a_spec = pl.BlockSpec((tk, tm), lambda i, j, k: (k, i))
    else:
        a_spec = pl.BlockSpec((tm, tk), lambda i, j, k: (i, k))
    if mode == "nt":
        b_spec = pl.BlockSpec((tn, tk), lambda i, j, k: (j, k))
    else:
        b_spec = pl.BlockSpec((tk, tn), lambda i, j, k: (k, j))
    mn_spec = pl.BlockSpec((tm, tn), lambda i, j, k: (i, j))
    if out_widths is None:
        o_specs = [mn_spec] * n_out
        o_shapes = [jax.ShapeDtypeStruct((M, N), dt) for dt in out_dtypes]
    else:
        assert tn == N
        o_specs = [pl.BlockSpec((tm, wd), lambda i, j, k: (i, 0)) for wd in out_widths]
        o_shapes = [jax.ShapeDtypeStruct((M, wd), dt) for wd, dt in zip(out_widths, out_dtypes)]
    if shard_cols is not None:
        o_specs = [pl.BlockSpec((tn // shard_cols, tm, shard_cols), lambda i, j, k: (j, i, 0))]
        o_shapes = [jax.ShapeDtypeStruct((N // shard_cols, M, shard_cols), F32)]
    dims = {"nn": ((1,), (0,)), "nt": ((1,), (1,)), "tn": ((0,), (0,))}[mode]

    single = nk == 1
    direct = (not single) and epilogue is None and n_out == 1 and out_dtypes[0] == F32

    def body(*refs):
        a_ref, b_ref = refs[0], refs[1]
        ex_refs = refs[2:2 + n_ex]
        vec_refs = refs[2 + n_ex:2 + n_ex + n_vec]
        out_refs = refs[2 + n_ex + n_vec:2 + n_ex + n_vec + n_out]
        if n_chunk is not None:
            assert single and out_widths is None and mode != "tn" and tn % n_chunk == 0
            av = a_ref[...].astype(BF16)
            for j in range(tn // n_chunk):
                cols = slice(j * n_chunk, (j + 1) * n_chunk)
                bv = b_ref[cols, :] if mode == "nt" else b_ref[:, cols]
                acc = _dot(av, bv.astype(BF16), dims)
                res = (acc,) if epilogue is None else epilogue(acc, *[e[:, cols] for e in ex_refs])
                for o_ref, r in zip(out_refs, res):
                    o_ref[:, cols] = r.astype(o_ref.dtype)
            return
        part = _dot(a_ref[...].astype(BF16), b_ref[...].astype(BF16), dims)

        def finish(acc):
            res = (acc,) if epilogue is None else epilogue(acc, *[e[...] for e in ex_refs], *[v[...] for v in vec_refs])
            for o_ref, r in zip(out_refs, res):
                o_ref[...] = r.astype(o_ref.dtype)

        if single:
            if shard_cols is not None:
                for sh in range(tn // shard_cols):
                    out_refs[0][sh] = part[:, sh * shard_cols:(sh + 1) * shard_cols]
            else:
                finish(part)
            return
        acc_ref = out_refs[0] if direct else refs[2 + n_ex + n_vec + n_out]
        k = pl.program_id(2)
        if shard_cols is not None:
            assert direct
            for sh in range(tn // shard_cols):
                piece = part[:, sh * shard_cols:(sh + 1) * shard_cols]

                @pl.when(k == 0)
                def _():
                    acc_ref[sh] = piece

                @pl.when(k > 0)
                def _():
                    acc_ref[sh] += piece
            return

        @pl.when(k == 0)
        def _():
            acc_ref[...] = part

        @pl.when(k > 0)
        def _():
            acc_ref[...] += part

        if not direct:
            @pl.when(k == nk - 1)
            def _():
                finish(acc_ref[...])

    outs, ridden = _hosted_call(
        body, rider, name=name, grid=(M // tm, N // tn, nk),
        in_specs=[a_spec, b_spec] + [mn_spec] * n_ex + [pl.BlockSpec((1, tn), lambda i, j, k: (0, j))] * n_vec,
        out_specs=o_specs, out_shape=o_shapes,
        scratch_shapes=[] if (single or direct) else [pltpu.VMEM((tm, tn), F32)],
        compiler_params=_cp(vmem, ("parallel", "parallel", "arbitrary")), args=(a, b, *extras, *vec_extras))
    return tuple(outs) if rider is None else (tuple(outs), ridden)


def _grad_tn(a, secs, *, tk, name):
    kk, m = a.shape
    widths = [s.shape[1] for s in secs]
    total = sum(widths)
    nk = kk // tk

    def body(*refs):
        a_ref, sec_refs, o_ref = refs[0], refs[1:-1], refs[-1]
        cat = sec_refs[0][...] if len(sec_refs) == 1 else jnp.concatenate([s[...] for s in sec_refs], axis=1)
        part = _dot(a_ref[...].astype(BF16), cat.astype(BF16), ((0,), (0,)))
        k = pl.program_id(0)

        @pl.when(k == 0)
        def _():
            o_ref[...] = part

        @pl.when(k > 0)
        def _():
            o_ref[...] += part

    return pl.pallas_call(
        body, name=name, grid=(nk,),
        in_specs=[pl.BlockSpec((tk, m), lambda k: (k, 0))] + [pl.BlockSpec((tk, w), lambda k: (k, 0)) for w in widths],
        out_specs=pl.BlockSpec((m, total), lambda k: (0, 0)),
        out_shape=jax.ShapeDtypeStruct((m, total), F32),
        compiler_params=_cp(VMEM_BIG, ("arbitrary",)),
    )(a, *secs)


class _ShiftedRows:
    def __init__(self, src, buf, sems, *, per_seq, tt, steps):
        self.src, self.buf, self.sems = src, buf, sems
        self.per_seq, self.tt, self.steps = per_seq, tt, steps

    def _do(self, step, slot, act):
        b, j = step // self.per_seq, step % self.per_seq
        tt = self.tt

        @pl.when(j == 0)
        def _():
            act(pltpu.make_async_copy(self.src.at[b, pl.ds(0, tt - CHUNK), :],
                                      self.buf.at[slot, pl.ds(CHUNK, tt - CHUNK), :], self.sems.at[slot]))

        if self.per_seq > 1:
            @pl.when(j > 0)
            def _():
                act(pltpu.make_async_copy(self.src.at[b, pl.ds(j * tt - CHUNK, tt), :], self.buf.at[slot],
                                          self.sems.at[slot]))

    def tile(self, i):
        slot = i % 2

        @pl.when(i == 0)
        def _():
            self._do(i, slot, lambda cp: cp.start())

        self._do(i, slot, lambda cp: cp.wait())

        @pl.when(i + 1 < self.steps)
        def _():
            self._do(i + 1, 1 - slot, lambda cp: cp.start())

        return slot


def _embed_norm(x, lead, g, *, tr, name, rider=None):
    bsz, s_len, d = x.shape
    t_seq = s_len + CHUNK
    per_seq = t_seq // tr
    steps = bsz * per_seq
    n = bsz * t_seq

    def body(x_hbm, lead_ref, g_ref, h0_ref, h_ref, buf, sems):
        i = pl.program_id(0)
        slot = _ShiftedRows(x_hbm, buf, sems, per_seq=per_seq, tt=tr, steps=steps).tile(i)

        @pl.when(i % per_seq == 0)
        def _():
            buf[slot, 0:CHUNK, :] = lead_ref[...]

        xv = buf[slot]
        h0_ref[...] = xv
        r = lax.rsqrt(jnp.mean(xv * xv, axis=-1, keepdims=True) + EPS)
        h_ref[...] = (xv * r * g_ref[...]).astype(h_ref.dtype)

    row = pl.BlockSpec((tr, d), lambda i: (i, 0))
    outs, ridden = _hosted_call(
        body, rider, name=name, grid=(steps,),
        in_specs=[ANY, pl.BlockSpec((CHUNK, d), lambda i: (0, 0)), pl.BlockSpec((1, d), lambda i: (0, 0))],
        out_specs=[row, row],
        out_shape=[jax.ShapeDtypeStruct((n, d), F32), jax.ShapeDtypeStruct((n, d), BF16)],
        scratch_shapes=[pltpu.VMEM((2, tr, d), F32), pltpu.SemaphoreType.DMA((2,))],
        compiler_params=_cp(VMEM_BIG, ("arbitrary",)), args=(x, lead, g))
    return (*outs, ridden)


def _rms_fwd(x, g, *, tr, name):
    n, d = x.shape

    def body(x_ref, g_ref, o_ref):
        xv = x_ref[...]
        r = lax.rsqrt(jnp.mean(xv * xv, axis=-1, keepdims=True) + EPS)
        o_ref[...] = (xv * r * g_ref[...]).astype(o_ref.dtype)

    return pl.pallas_call(
        body, name=name, grid=(n // tr,),
        in_specs=[pl.BlockSpec((tr, d), lambda i: (i, 0)), pl.BlockSpec((1, d), lambda i: (0, 0))],
        out_specs=pl.BlockSpec((tr, d), lambda i: (i, 0)),
        out_shape=jax.ShapeDtypeStruct((n, d), BF16),
        compiler_params=_cp(VMEM_BIG),
    )(x, g)


def _rms_bwd_math(xv, g, dy):
    r = lax.rsqrt(jnp.mean(xv * xv, axis=-1, keepdims=True) + EPS)
    xh = xv * r
    gdy = dy * g
    dx = r * (gdy - xh * jnp.mean(xh * gdy, axis=-1, keepdims=True))
    return dx, jnp.sum(dy * xh, axis=0, keepdims=True)


def _mlp_up_bwd_norm(dup, w_up, x, g, res, *, tr, name, rider=None):
    n, d = x.shape
    ff = dup.shape[1]

    def body(dup_ref, w_ref, x_ref, g_ref, res_ref, o_ref, ob_ref, dg_ref):
        dh = _nt(dup_ref[...], w_ref[...])
        dx, dg = _rms_bwd_math(x_ref[...], g_ref[...], dh)
        tot = res_ref[...] + dx
        o_ref[...] = tot
        ob_ref[...] = tot.astype(BF16)

        @pl.when(pl.program_id(0) == 0)
        def _():
            dg_ref[...] = dg

        @pl.when(pl.program_id(0) > 0)
        def _():
            dg_ref[...] += dg

    row = pl.BlockSpec((tr, d), lambda i: (i, 0))
    vec = pl.BlockSpec((1, d), lambda i: (0, 0))
    outs, ridden = _hosted_call(
        body, rider, name=name, grid=(n // tr,),
        in_specs=[pl.BlockSpec((tr, ff), lambda i: (i, 0)), pl.BlockSpec((d, ff), lambda i: (0, 0)), row, vec, row],
        out_specs=[row, row, vec],
        out_shape=[jax.ShapeDtypeStruct((n, d), F32), jax.ShapeDtypeStruct((n, d), BF16),
                   jax.ShapeDtypeStruct((1, d), F32)],
        scratch_shapes=[], compiler_params=_cp(VMEM_BIG, ("arbitrary",)), args=(dup, w_up, x, g, res))
    return (*outs, ridden)


def _mlp_down_loss(act, w_down, x1, gf, tgt, *, t_seq, tr, name):
    n, d = x1.shape
    ff = act.shape[1]
    per_seq = t_seq // tr
    steps = n // tr

    def body(a_ref, w_ref, x_ref, g_ref, t_hbm, dx_ref, dxb_ref, dg_ref, loss_ref, tbuf, tsems):
        i = pl.program_id(0)
        slot = _ShiftedRows(t_hbm, tbuf, tsems, per_seq=per_seq, tt=tr, steps=steps).tile(i)

        @pl.when(i % per_seq == 0)
        def _():
            tbuf[slot, 0:CHUNK, :] = jnp.zeros((CHUNK, d), F32)

        t_ref = tbuf.at[slot]
        xv = x_ref[...] + _nn(a_ref[...], w_ref[...])
        g = g_ref[...]
        r = lax.rsqrt(jnp.mean(xv * xv, axis=-1, keepdims=True) + EPS)
        xh = xv * r
        pos = (i % per_seq) * tr + _iota2((tr, 1), 0)
        real = pos >= CHUNK
        err = jnp.where(real, xh * g - t_ref[...], 0.0)
        dy = err * (1.0 / d)
        gdy = dy * g
        dx = r * (gdy - xh * jnp.mean(xh * gdy, axis=-1, keepdims=True))
        dx_ref[...] = dx
        dxb_ref[...] = dx.astype(BF16)
        dg = jnp.sum(dy * xh, axis=0, keepdims=True)
        ls = 0.5 * jnp.sum(jnp.mean(err * err, axis=-1, keepdims=True), axis=0, keepdims=True)
        ls = jnp.where(_iota2((1, LANE), 1) == 0, ls, 0.0)

        @pl.when(i == 0)
        def _():
            dg_ref[...] = dg
            loss_ref[...] = ls

        @pl.when(i > 0)
        def _():
            dg_ref[...] += dg
            loss_ref[...] += ls

    row = pl.BlockSpec((tr, d), lambda i: (i, 0))
    vec = pl.BlockSpec((1, d), lambda i: (0, 0))
    one = pl.BlockSpec((1, LANE), lambda i: (0, 0))
    return pl.pallas_call(
        body, name=name, grid=(n // tr,),
        in_specs=[pl.BlockSpec((tr, ff), lambda i: (i, 0)), pl.BlockSpec((ff, d), lambda i: (0, 0)), row, vec, ANY],
        out_specs=[row, row, vec, one],
        out_shape=[jax.ShapeDtypeStruct((n, d), F32), jax.ShapeDtypeStruct((n, d), BF16),
                   jax.ShapeDtypeStruct((1, d), F32), jax.ShapeDtypeStruct((1, LANE), F32)],
        scratch_shapes=[pltpu.VMEM((2, tr, d), F32), pltpu.SemaphoreType.DMA((2,))],
        compiler_params=_cp(VMEM_BIG, ("arbitrary",)),
    )(act, w_down, x1, gf, tgt)


def _gnorm_fwd(o_dn, o_gla, projp, g_dn, g_gla, *, tr, name):
    n = o_dn.shape[0]
    w = NH * DN_D

    def body(odn_ref, ogl_ref, z_ref, r_ref, gdn_ref, ggl_ref, mix_ref):
        for grp, (o_ref, gate_ref, gain_ref) in enumerate(((odn_ref, z_ref, gdn_ref), (ogl_ref, r_ref, ggl_ref))):
            gain = gain_ref[...]
            for h in range(NH):
                sl = slice(h * DN_D, (h + 1) * DN_D)
                o = o_ref[:, sl].astype(F32)
                z = gate_ref[:, sl].astype(F32)
                r = lax.rsqrt(jnp.mean(o * o, axis=-1, keepdims=True) + EPS)
                y = (o * r * gain) * (z * _sigmoid(z))
                mix_ref[:, grp * w + h * DN_D: grp * w + (h + 1) * DN_D] = y.astype(mix_ref.dtype)

    row = pl.BlockSpec((tr, w), lambda i: (i, 0))
    vec = pl.BlockSpec((1, DN_D), lambda i: (0, 0))
    return pl.pallas_call(
        body, name=name, grid=(n // tr,),
        in_specs=[row, row, pl.BlockSpec((tr, w), lambda i: (i, C_DZ // w)),
                  pl.BlockSpec((tr, w), lambda i: (i, C_GR // w)), vec, vec],
        out_specs=pl.BlockSpec((tr, 2 * w), lambda i: (i, 0)),
        out_shape=jax.ShapeDtypeStruct((n, 2 * w), BF16),
        compiler_params=_cp(VMEM_BIG),
    )(o_dn, o_gla, projp, projp, g_dn, g_gla)


def _gnorm_bwd(dmix, o_dn, o_gla, projp, g_dn, g_gla, *, tr, name):
    n = o_dn.shape[0]
    w = NH * DN_D

    def body(dm_ref, odn_ref, ogl_ref, z_ref, r_ref, gdn_ref, ggl_ref,
             dodn_ref, ddz_ref, dogl_ref, dgr_ref, dgdn_ref, dggl_ref):
        first = pl.program_id(0) == 0
        groups = ((odn_ref, z_ref, gdn_ref, dodn_ref, ddz_ref, dgdn_ref),
                  (ogl_ref, r_ref, ggl_ref, dogl_ref, dgr_ref, dggl_ref))
        for grp, (o_ref, gate_ref, gain_ref, do_ref, dgate_ref, dgain_ref) in enumerate(groups):
            gain = gain_ref[...]
            dgain = jnp.zeros((1, DN_D), F32)
            for h in range(NH):
                sl = slice(h * DN_D, (h + 1) * DN_D)
                o = o_ref[:, sl].astype(F32)
                z = gate_ref[:, sl].astype(F32)
                dm = dm_ref[:, grp * w + h * DN_D: grp * w + (h + 1) * DN_D].astype(F32)
                r = lax.rsqrt(jnp.mean(o * o, axis=-1, keepdims=True) + EPS)
                oh = o * r
                s = _sigmoid(z)
                dn = dm * (z * s)
                dgate_ref[:, sl] = (dm * (oh * gain) * (s * (1.0 + z * (1.0 - s)))).astype(dgate_ref.dtype)
                gdn = dn * gain
                do_ref[:, sl] = (r * (gdn - oh * jnp.mean(oh * gdn, axis=-1, keepdims=True))).astype(do_ref.dtype)
                dgain = dgain + jnp.sum(dn * oh, axis=0, keepdims=True)

            @pl.when(first)
            def _():
                dgain_ref[...] = dgain

            @pl.when(jnp.logical_not(first))
            def _():
                dgain_ref[...] += dgain

    row = pl.BlockSpec((tr, w), lambda i: (i, 0))
    vec = pl.BlockSpec((1, DN_D), lambda i: (0, 0))
    big = jax.ShapeDtypeStruct((n, w), F32)
    gate = jax.ShapeDtypeStruct((n, w), BF16)
    small = jax.ShapeDtypeStruct((1, DN_D), F32)
    return pl.pallas_call(
        body, name=name, grid=(n // tr,),
        in_specs=[pl.BlockSpec((tr, 2 * w), lambda i: (i, 0)), row, row,
                  pl.BlockSpec((tr, w), lambda i: (i, C_DZ // w)), pl.BlockSpec((tr, w), lambda i: (i, C_GR // w)), vec, vec],
        out_specs=[row, row, row, row, vec, vec],
        out_shape=[gate, gate, gate, gate, small, small],
        compiler_params=_cp(VMEM_BIG),
    )(dmix, o_dn, o_gla, projp, projp, g_dn, g_gla)


QKV_W = 3 * NH * DN_D
HALO = 8


TAP_ROWS = 32


def _aligned_taps(src_ref, tap_ref, slot, starts, nr, cols):
    taps = []
    for j, start in enumerate(starts):
        if start % HALO == 0:
            taps.append(src_ref.at[pl.ds(start, nr), cols])
        else:
            tap_ref[slot, j, 0:nr, :] = src_ref[pl.ds(start, nr), cols]
            taps.append(tap_ref.at[slot, j, 0:nr, :])
    return taps


def _conv_taps(cw_ref, taps, cols):
    z = cw_ref[0:1, cols] * taps[0][...]
    for j in range(1, 4):
        z = z + cw_ref[j:j + 1, cols] * taps[j][...]
    return z


def _dnprep_fwd(projp, conv_w, *, bsz, t_seq, tt, name, rider=None):
    n = bsz * t_seq
    per_seq = t_seq // tt
    hw = NH * DN_D
    row_blk = _tile(tt, TAP_ROWS)

    def body(x_ref, halo_ref, cw_ref, q_ref, k_ref, v_ref, xs_ref, tap_ref):
        i = pl.program_id(1)
        xs_ref[0:HALO, :] = jnp.where(i == 0, 0.0, halo_ref[...].astype(F32))
        xs_ref[HALO:HALO + tt, :] = x_ref[...].astype(F32)
        for c, o_ref in enumerate([q_ref] * NH + [k_ref] * NH + [v_ref] * NH):
            cols = slice(c * DN_D, (c + 1) * DN_D)
            hcols = slice((c % NH) * DN_D, (c % NH + 1) * DN_D)
            for r0 in range(0, tt, row_blk):
                taps = _aligned_taps(xs_ref, tap_ref, (r0 // row_blk) % 2,
                                     [r0 + HALO - 3 + j for j in range(4)], row_blk, cols)
                z = _conv_taps(cw_ref, taps, cols)
                a = z * _sigmoid(z)
                if o_ref is not v_ref:
                    a = a * lax.rsqrt(jnp.sum(a * a, axis=-1, keepdims=True) + EPS)
                o_ref[r0:r0 + row_blk, hcols] = a.astype(o_ref.dtype)

    def halo_map(b, i):
        return (jnp.maximum((b * t_seq + i * tt) // HALO - 1, 0), 0)

    out = pl.BlockSpec((tt, hw), lambda b, i: (b * per_seq + i, 0))
    sds = jax.ShapeDtypeStruct((n, hw), BF16)
    outs, ridden = _hosted_call(
        body, rider, name=name, grid=(bsz, per_seq),
        in_specs=[pl.BlockSpec((tt, QKV_W), lambda b, i: (b * per_seq + i, 0)),
                  pl.BlockSpec((HALO, QKV_W), halo_map),
                  pl.BlockSpec((4, QKV_W), lambda b, i: (0, 0))],
        out_specs=[out, out, out], out_shape=[sds, sds, sds],
        scratch_shapes=[pltpu.VMEM((tt + HALO, QKV_W), F32), pltpu.VMEM((2, 4, row_blk, DN_D), F32)],
        compiler_params=_cp(VMEM_BIG), args=(projp, projp, conv_w))
    return (*outs, ridden)


def _dnprep_bwd_a(projp, conv_w, dq, dk, dv, *, bsz, t_seq, tt, name):
    n = bsz * t_seq
    per_seq = t_seq // tt
    hw = NH * DN_D

    row_blk = _tile(tt, 4 * TAP_ROWS, mult=HALO)

    def body(x_ref, halo_ref, cw_ref, dq_ref, dk_ref, dv_ref, dz_ref, dcw_ref, xs_ref, part_ref, tap_ref):
        b, i = pl.program_id(0), pl.program_id(1)
        xs_ref[0:HALO, :] = jnp.where(i == 0, 0.0, halo_ref[...].astype(F32))
        xs_ref[HALO:HALO + tt, :] = x_ref[...].astype(F32)
        for c, d_ref in enumerate([dq_ref] * NH + [dk_ref] * NH + [dv_ref] * NH):
            cols = slice(c * DN_D, (c + 1) * DN_D)
            hcols = slice((c % NH) * DN_D, (c % NH + 1) * DN_D)
            parts = [None] * 4
            for r0 in range(0, tt, row_blk):
                taps = _aligned_taps(xs_ref, tap_ref, (r0 // row_blk) % 2,
                                     [r0 + HALO - 3 + j for j in range(4)], row_blk, cols)
                z = _conv_taps(cw_ref, taps, cols)
                s = _sigmoid(z)
                dsilu = s * (1.0 + z * (1.0 - s))
                dy = d_ref[r0:r0 + row_blk, hcols]
                if d_ref is dv_ref:
                    dz = dy * dsilu
                else:
                    a = z * s
                    rs = lax.rsqrt(jnp.sum(a * a, axis=-1, keepdims=True) + EPS)
                    y = a * rs
                    dz = (rs * (dy - y * jnp.sum(dy * y, axis=-1, keepdims=True))) * dsilu
                dz_ref[r0:r0 + row_blk, cols] = dz
                for j in range(4):
                    p = jnp.sum(dz * taps[j][...], axis=0, keepdims=True)
                    parts[j] = p if parts[j] is None else parts[j] + p
            for j in range(4):
                part_ref[j:j + 1, cols] = parts[j]

        first = jnp.logical_and(b == 0, i == 0)

        @pl.when(first)
        def _():
            dcw_ref[...] = part_ref[0:4, :]

        @pl.when(jnp.logical_not(first))
        def _():
            dcw_ref[...] += part_ref[0:4, :]

    def halo_map(b, i):
        return (jnp.maximum((b * t_seq + i * tt) // HALO - 1, 0), 0)

    hrow = pl.BlockSpec((tt, hw), lambda b, i: (b * per_seq + i, 0))
    return pl.pallas_call(
        body, name=name, grid=(bsz, per_seq),
        in_specs=[pl.BlockSpec((tt, QKV_W), lambda b, i: (b * per_seq + i, 0)),
                  pl.BlockSpec((HALO, QKV_W), halo_map),
                  pl.BlockSpec((4, QKV_W), lambda b, i: (0, 0)), hrow, hrow, hrow],
        out_specs=[pl.BlockSpec((tt, QKV_W), lambda b, i: (b * per_seq + i, 0)),
                   pl.BlockSpec((4, QKV_W), lambda b, i: (0, 0))],
        out_shape=[jax.ShapeDtypeStruct((n, QKV_W), F32), jax.ShapeDtypeStruct((4, QKV_W), F32)],
        scratch_shapes=[pltpu.VMEM((tt + HALO, QKV_W), F32), pltpu.VMEM((HALO, QKV_W), F32),
                        pltpu.VMEM((2, 4, row_blk, DN_D), F32)],
        compiler_params=_cp(VMEM_BIG),
    )(projp, projp, conv_w, dq, dk, dv)


def _dnprep_bwd_b(dz, conv_w, *, bsz, t_seq, tt, name):
    n = bsz * t_seq
    per_seq = t_seq // tt
    last_blk = n // HALO - 1

    main = tt - HALO
    row_blk = _tile(tt, TAP_ROWS)

    def body(dz_ref, halo_ref, cw_ref, dx_ref, tail_ref, tap_ref):
        i = pl.program_id(1)
        for cb in range(QKV_W // DN_D):
            cols = slice(cb * DN_D, (cb + 1) * DN_D)
            for r0 in range(0, main, row_blk):
                nr = min(row_blk, main - r0)
                taps = _aligned_taps(dz_ref, tap_ref, (r0 // row_blk) % 2, [r0 + 3 - j for j in range(4)], nr, cols)
                dx_ref[r0:r0 + nr, cols] = _conv_taps(cw_ref, taps, cols).astype(dx_ref.dtype)
        tail_ref[0:HALO, :] = dz_ref[main:tt, :]
        tail_ref[HALO:2 * HALO, :] = jnp.where(i == per_seq - 1, 0.0, halo_ref[...])
        dx = cw_ref[0:1, :] * tail_ref[pl.ds(3, HALO), :]
        for j in range(1, 4):
            dx = dx + cw_ref[j:j + 1, :] * tail_ref[pl.ds(3 - j, HALO), :]
        dx_ref[main:tt, :] = dx.astype(dx_ref.dtype)

    def halo_map(b, i):
        return (jnp.minimum((b * t_seq + (i + 1) * tt) // HALO, last_blk), 0)

    row = pl.BlockSpec((tt, QKV_W), lambda b, i: (b * per_seq + i, 0))
    return pl.pallas_call(
        body, name=name, grid=(bsz, per_seq),
        in_specs=[row, pl.BlockSpec((HALO, QKV_W), halo_map), pl.BlockSpec((4, QKV_W), lambda b, i: (0, 0))],
        out_specs=row, out_shape=jax.ShapeDtypeStruct((n, QKV_W), BF16),
        scratch_shapes=[pltpu.VMEM((2 * HALO, QKV_W), F32), pltpu.VMEM((2, 4, row_blk, DN_D), F32)],
        compiler_params=_cp(VMEM_BIG),
    )(dz, dz, conv_w)


def _masks64():
    r = _iota2((CHUNK, CHUNK), 0)
    c = _iota2((CHUNK, CHUNK), 1)
    return r, c


def _group(nc_seq, target=5):
    return max(g for g in range(1, target + 1) if nc_seq % g == 0)


def _round_robin(chains):
    live = list(chains)
    while live:
        nxt = []
        for ch in live:
            try:
                next(ch)
                nxt.append(ch)
            except StopIteration:
                pass
        live = nxt
        yield


def _run(chains):
    for _ in _round_robin(chains):
        pass


def _per_chunk(inner, kinds, grp):
    def body(*refs):
        chains = []
        for gi in range(grp):
            views = []
            for r, kind in zip(refs, kinds):
                if kind == "row":
                    views.append(r.at[pl.ds(gi * CHUNK, CHUNK)])
                elif kind == "lead":
                    views.append(r.at[pl.ds(gi, 1)])
                else:
                    views.append(r)
            chains.append(inner(gi, *views))
        _run(chains)
    return body


def _accumulate(ref, val, gi):
    if gi > 0:
        ref[...] += val
        return
    first = pl.program_id(0) == 0

    @pl.when(first)
    def _():
        ref[...] = val

    @pl.when(jnp.logical_not(first))
    def _():
        ref[...] += val


ANY = pl.BlockSpec(memory_space=pl.ANY)


def _place():
    return lax.axis_index("x"), lax.axis_index("y"), lax.axis_index("c")


def _other_chips(x, y):
    return [(1 - x, y, 2 * (1 - x) + y), (x, 1 - y, 2 * x + 1 - y), (1 - x, 1 - y, 2 * (1 - x) + 1 - y)]


class _GatherRider:
    def __init__(self, bufs, split):
        self.inputs = list(bufs)
        self.split = list(split)
        self.out_shapes = [jax.ShapeDtypeStruct(b.shape, b.dtype) for b in bufs]
        self.aliases = {i: i for i in range(len(bufs))}
        self.sems = [pltpu.SemaphoreType.DMA((len(bufs), 3))] * 4

    def _rows(self, k, buf, c, mine=True):
        r = buf.shape[1]
        if not self.split[k]:
            return pl.ds(0, r)
        return pl.ds((c if mine else 1 - c) * (r // 2), r // 2)

    def _ici(self, k, d, bufs, sems, c, px, py, block):
        rows = self._rows(k, bufs[k], c)
        return pltpu.make_async_remote_copy(
            src_ref=bufs[k].at[block, rows, :], dst_ref=bufs[k].at[block, rows, :], send_sem=sems[0].at[k, d],
            recv_sem=sems[1].at[k, d], device_id=(px, py, c), device_id_type=MESH)

    def _pass(self, k, d, bufs, sems, x, y, c, block, mine):
        rows = self._rows(k, bufs[k], c, mine)
        return pltpu.make_async_remote_copy(
            src_ref=bufs[k].at[block, rows, :], dst_ref=bufs[k].at[block, rows, :], send_sem=sems[2].at[k, d],
            recv_sem=sems[3].at[k, d], device_id=(x, y, 1 - c), device_id_type=MESH)

    def first(self, in_refs, bufs, sems):
        x, y, c = _place()
        for k in range(len(bufs)):
            for d, (px, py, _) in enumerate(_other_chips(x, y)):
                self._ici(k, d, bufs, sems, c, px, py, 2 * x + y).start()

    def last(self, in_refs, bufs, sems):
        x, y, c = _place()
        chips = _other_chips(x, y)
        for k in range(len(bufs)):
            for d, (px, py, pj) in enumerate(chips):
                self._ici(k, d, bufs, sems, c, px, py, pj).wait_recv()
                if self.split[k]:
                    self._pass(k, d, bufs, sems, x, y, c, pj, True).start()
        for k in range(len(bufs)):
            for d, (px, py, pj) in enumerate(chips):
                if self.split[k]:
                    self._pass(k, d, bufs, sems, x, y, c, pj, False).wait_recv()
                    self._pass(k, d, bufs, sems, x, y, c, pj, True).wait_send()
                self._ici(k, d, bufs, sems, c, px, py, 2 * x + y).wait_send()


def _hosted_call(body, rider, *, name, grid, in_specs, out_specs, out_shape, scratch_shapes, compiler_params, args):
    if rider is None:
        outs = pl.pallas_call(body, name=name, grid=grid, in_specs=in_specs, out_specs=out_specs, out_shape=out_shape,
                              scratch_shapes=scratch_shapes, compiler_params=compiler_params)(*args)
        return list(outs), []
    n_in, n_out, n_scr = len(in_specs), len(out_specs), len(scratch_shapes)
    r_in, r_out = len(rider.inputs), len(rider.out_shapes)
    compiler_params = _cp(compiler_params.vmem_limit_bytes, ("arbitrary",) * len(grid))

    def full_body(*refs):
        ins = refs[:n_in]
        rins = refs[n_in:n_in + r_in]
        outs = refs[n_in + r_in:n_in + r_in + n_out]
        routs = refs[n_in + r_in + n_out:n_in + r_in + n_out + r_out]
        rest = refs[n_in + r_in + n_out + r_out:]
        scr, sems = rest[:n_scr], rest[n_scr:]
        ids = [pl.program_id(a) for a in range(len(grid))]
        is_first = functools.reduce(jnp.logical_and, [i == 0 for i in ids])
        is_last = functools.reduce(jnp.logical_and, [i == g - 1 for i, g in zip(ids, grid)])

        @pl.when(is_first)
        def _():
            rider.first(rins, routs, sems)

        body(*ins, *outs, *scr)

        @pl.when(is_last)
        def _():
            rider.last(rins, routs, sems)

    res = pl.pallas_call(
        full_body, name=name, grid=grid, in_specs=list(in_specs) + [ANY] * r_in,
        out_specs=list(out_specs) + [ANY] * r_out, out_shape=list(out_shape) + list(rider.out_shapes),
        input_output_aliases={n_in + i: n_out + o for i, o in rider.aliases.items()},
        scratch_shapes=list(scratch_shapes) + list(rider.sems), compiler_params=compiler_params,
    )(*args, *rider.inputs)
    return list(res[:n_out]), list(res[n_out:])


def _exchange_now(rider, *, name):
    r_in = len(rider.inputs)

    def body(*refs):
        rins = refs[:r_in]
        routs = refs[r_in:r_in + len(rider.out_shapes)]
        sems = refs[r_in + len(rider.out_shapes):]
        rider.first(rins, routs, sems)
        rider.last(rins, routs, sems)

    return pl.pallas_call(
        body, name=name, in_specs=[ANY] * r_in, out_specs=[ANY] * len(rider.out_shapes), out_shape=list(rider.out_shapes),
        input_output_aliases=dict(rider.aliases), scratch_shapes=list(rider.sems),
    )(*rider.inputs)


def _to_slot(where, a, dtype, *, name):
    r, cols = a.shape
    tr = _tile(r, 256, 16) if r > 256 else r

    def body(w_ref, a_ref, o_ref):
        o_ref[0] = a_ref[...].astype(o_ref.dtype)

    return pl.pallas_call(
        body, name=name,
        grid_spec=pltpu.PrefetchScalarGridSpec(
            num_scalar_prefetch=1, grid=(r // tr,),
            in_specs=[pl.BlockSpec((tr, cols), lambda i, w: (i, 0))],
            out_specs=pl.BlockSpec((1, tr, cols), lambda i, w: (w[1], i, 0))),
        out_shape=jax.ShapeDtypeStruct((N_CHIPS, r, cols), dtype), compiler_params=_cp(VMEM_BIG),
    )(where, a)


def _tri_inv(a_strict):
    r, c = _masks64()
    eye = (r == c).astype(F32)
    blk16 = (r // 16) == (c // 16)
    blk32 = (r // 32) == (c // 32)
    ld = jnp.where(blk16, a_strict, 0.0)
    x = eye - ld
    p = _nn(ld, ld)
    yield
    for step in range(3):
        xp = _nn(x, p)
        if step < 2:
            p = _nn(p, p)
        x = x + xp
        yield
    for lk in (jnp.where(jnp.logical_and(blk32, jnp.logical_not(blk16)), a_strict, 0.0),
               jnp.where(blk32, 0.0, a_strict)):
        y = x - eye
        s = lk + _nn(y, lk)
        yield
        x = x - s - _nn(s, y)
        yield
    return x


def _dn_gates(sa, alog, dtb, chunk_in_seq):
    rows = _iota2((CHUNK, LANE), 0)
    valid = jnp.logical_or(rows >= N_PAD, chunk_in_seq > 0)
    beta_t = _sigmoid(sa)
    ea = jnp.exp(alog)
    g_t = jnp.where(valid, -ea * _softplus(sa + dtb), 0.0)
    r, c = _masks64()
    ltri = (r >= c).astype(F32)
    gam_t = _tri_sum(ltri, g_t)
    return beta_t, g_t, gam_t, valid, ea


def _dn_intra_fwd(qn, kn, v, projp, alog_row, dtb_row, *, nc_seq, name, rider=None):
    n = qn.shape[0]
    nct = n // CHUNK
    hw = NH * DN_D
    scale = DN_D ** -0.5

    grp = _group(nc_seq)

    def inner(gi, q_ref, k_ref, v_ref, sa_ref, al_ref, dt_ref, u_ref, w_ref, qg_ref, kd_ref, p_ref, t_ref, gl_ref):
        ci = (pl.program_id(0) * grp + gi) % nc_seq
        beta_t, _, gam_t, _, _ = _dn_gates(sa_ref[...], al_ref[...], dt_ref[...], ci)
        yield
        gam_tt = gam_t.T
        r, c = _masks64()
        incl = r >= c
        strict = r > c

        def head(h):
            sl = slice(h * DN_D, (h + 1) * DN_D)
            beta_w = jnp.broadcast_to(beta_t[:, h:h + 1], (CHUNK, DN_D))
            gam_w = jnp.broadcast_to(gam_t[:, 4 + h:5 + h], (CHUNK, DN_D))
            gam_row = gam_tt[4 + h:5 + h, :]
            gl = gam_t[CHUNK - 1:CHUNK, 4 + h:5 + h]
            dec = jnp.exp(jnp.where(incl, gam_w[:, 0:CHUNK] - gam_row, -jnp.inf))
            kh = k_ref[:, sl].astype(F32)
            qh = q_ref[:, sl].astype(F32) * scale
            vh = v_ref[:, sl].astype(F32)
            kk = _nt(kh, kh)
            qk = _nt(qh, kh)
            yield
            a = jnp.where(strict, beta_w[:, 0:CHUNK] * kk * dec, 0.0)
            tm = yield from _tri_inv(a)
            egam_w = jnp.exp(gam_w)
            u_ref[:, sl] = _nn(tm, beta_w * vh).astype(u_ref.dtype)
            w_ref[:, sl] = _nn(tm, (beta_w * egam_w) * kh).astype(w_ref.dtype)
            qg_ref[:, sl] = (egam_w * qh).astype(qg_ref.dtype)
            kd_ref[:, sl] = (jnp.exp(gl - gam_w) * kh).astype(kd_ref.dtype)
            p_ref[0, h] = qk * dec
            t_ref[0, h] = tm
            gl_ref[0, h:h + 1, :] = jnp.broadcast_to(jnp.exp(gl), (1, LANE))

        yield from _round_robin([head(h) for h in range(NH)])

    rows = grp * CHUNK
    row = pl.BlockSpec((rows, hw), lambda i: (i, 0))
    vec = pl.BlockSpec((1, LANE), lambda i: (0, 0))
    mat = pl.BlockSpec((grp, NH, CHUNK, CHUNK), lambda i: (i, 0, 0, 0))
    big = jax.ShapeDtypeStruct((n, hw), BF16)
    msd = jax.ShapeDtypeStruct((nct, NH, CHUNK, CHUNK), F32)
    kinds = ["row"] * 4 + ["whole"] * 2 + ["row"] * 4 + ["lead"] * 3
    outs, ridden = _hosted_call(
        _per_chunk(inner, kinds, grp), rider, name=name, grid=(nct // grp,),
        in_specs=[row, row, row, pl.BlockSpec((rows, LANE), lambda i: (i, 0)), vec, vec],
        out_specs=[row, row, row, row, mat, mat, pl.BlockSpec((grp, NH, LANE), lambda i: (i, 0, 0))],
        out_shape=[big, big, big, big, msd, msd, jax.ShapeDtypeStruct((nct, NH, LANE), F32)],
        scratch_shapes=[], compiler_params=_cp(VMEM_BIG, ("arbitrary",)),
        args=(qn, kn, v, projp, alog_row, dtb_row))
    return (*outs, ridden)


def _dn_scan_fwd(u, w, qg, kd, p, gl, *, bsz, nc_seq, name, rider=None):
    hw = NH * DN_D
    t_seq = nc_seq * CHUNK
    u, w, qg, kd = (z.reshape(bsz, t_seq, hw) for z in (u, w, qg, kd))
    p = p.reshape(bsz, nc_seq, NH, CHUNK, CHUNK)
    gl = gl.reshape(bsz, nc_seq, NH, LANE)
    grp = _group(nc_seq)

    def body(u_ref, w_ref, qg_ref, kd_ref, p_ref, gl_ref, o_ref, vn_ref, hist_ref, s_ref):
        @pl.when(pl.program_id(0) == 0)
        def _():
            s_ref[...] = jnp.zeros_like(s_ref)

        def chain(b, h, gi):
            sl = slice(h * DN_D, (h + 1) * DN_D)
            rows = pl.ds(gi * CHUNK, CHUNK)
            s = s_ref[b, h]
            hist_ref[b, gi, h] = s.astype(hist_ref.dtype)
            ws = _nn(w_ref[b, rows, sl], s)
            qs = _nn(qg_ref[b, rows, sl], s)
            yield
            vn = u_ref[b, rows, sl] - ws
            vn_ref[b, rows, sl] = vn.astype(vn_ref.dtype)
            o_ref[b, rows, sl] = (qs + _nn(p_ref[b, gi, h], vn)).astype(o_ref.dtype)
            s_ref[b, h] = gl_ref[b, gi, h:h + 1, :] * s + _tn(kd_ref[b, rows, sl], vn)

        for gi in range(grp):
            _run([chain(b, h, gi) for b in range(bsz) for h in range(NH)])

    row = pl.BlockSpec((bsz, grp * CHUNK, hw), lambda i: (0, i, 0))
    outs, ridden = _hosted_call(
        body, rider, name=name, grid=(nc_seq // grp,),
        in_specs=[row, row, row, row, pl.BlockSpec((bsz, grp, NH, CHUNK, CHUNK), lambda i: (0, i, 0, 0, 0)),
                  pl.BlockSpec((bsz, grp, NH, LANE), lambda i: (0, i, 0, 0))],
        out_specs=[row, row, pl.BlockSpec((bsz, grp, NH, DN_D, DN_D), lambda i: (0, i, 0, 0, 0))],
        out_shape=[jax.ShapeDtypeStruct((bsz, t_seq, hw), BF16), jax.ShapeDtypeStruct((bsz, t_seq, hw), BF16),
                   jax.ShapeDtypeStruct((bsz, nc_seq, NH, DN_D, DN_D), F32)],
        scratch_shapes=[pltpu.VMEM((bsz, NH, DN_D, DN_D), F32)],
        compiler_params=_cp(VMEM_BIG, ("arbitrary",)), args=(u, w, qg, kd, p, gl))
    o, vn, hist = outs
    return o.reshape(bsz * t_seq, hw), vn.reshape(bsz * t_seq, hw), hist, ridden


def _dn_scan_bwd(do, w, qg, kd, vn, p, gl, hist, *, bsz, nc_seq, name):
    hw = NH * DN_D
    t_seq = nc_seq * CHUNK
    do, w, qg, kd, vn = (z.reshape(bsz, t_seq, hw) for z in (do, w, qg, kd, vn))
    p = p.reshape(bsz, nc_seq, NH, CHUNK, CHUNK)
    gl = gl.reshape(bsz, nc_seq, NH, LANE)
    grp = _group(nc_seq)

    def body(do_ref, w_ref, qg_ref, kd_ref, vn_ref, p_ref, gl_ref, hist_ref,
             du_ref, dw_ref, dqg_ref, dkd_ref, dgl_ref, ds_ref):
        @pl.when(pl.program_id(0) == 0)
        def _():
            ds_ref[...] = jnp.zeros_like(ds_ref)

        def chain(b, h, gi):
            sl = slice(h * DN_D, (h + 1) * DN_D)
            rows = pl.ds(gi * CHUNK, CHUNK)
            s = hist_ref[b, gi, h]
            dsn = ds_ref[b, h]
            doh = do_ref[b, rows, sl]
            vnh = vn_ref[b, rows, sl]
            kdh = kd_ref[b, rows, sl]
            dvn = _tn(p_ref[b, gi, h], doh) + _nn(kdh, dsn)
            du_ref[b, rows, sl] = dvn.astype(du_ref.dtype)
            dqg_ref[b, rows, sl] = _nt(doh, s).astype(dqg_ref.dtype)
            dkd_ref[b, rows, sl] = _nt(vnh, dsn).astype(dkd_ref.dtype)
            ds_part = _tn(qg_ref[b, rows, sl], doh) + gl_ref[b, gi, h:h + 1, :] * dsn
            dgl = jnp.sum(jnp.sum(dsn * s, axis=0, keepdims=True), axis=1, keepdims=True)
            dgl_ref[b, gi, h:h + 1, :] = jnp.broadcast_to(dgl, (1, LANE))
            yield
            dw_ref[b, rows, sl] = (-_nt(dvn, s)).astype(dw_ref.dtype)
            ds_ref[b, h] = ds_part - _tn(w_ref[b, rows, sl], dvn)

        for gi in reversed(range(grp)):
            _run([chain(b, h, gi) for b in range(bsz) for h in range(NH)])

    steps = nc_seq // grp
    rev = lambda i: steps - 1 - i
    row = pl.BlockSpec((bsz, grp * CHUNK, hw), lambda i: (0, rev(i), 0))
    mat = pl.BlockSpec((bsz, grp, NH, CHUNK, CHUNK), lambda i: (0, rev(i), 0, 0, 0))
    glb = pl.BlockSpec((bsz, grp, NH, LANE), lambda i: (0, rev(i), 0, 0))
    big = jax.ShapeDtypeStruct((bsz, t_seq, hw), BF16)
    outs = pl.pallas_call(
        body, name=name, grid=(steps,),
        in_specs=[row, row, row, row, row, mat, glb,
                  pl.BlockSpec((bsz, grp, NH, DN_D, DN_D), lambda i: (0, rev(i), 0, 0, 0))],
        out_specs=[row, row, row, row, glb],
        out_shape=[big, big, jax.ShapeDtypeStruct(big.shape, F32), jax.ShapeDtypeStruct(big.shape, F32),
                   jax.ShapeDtypeStruct((bsz, nc_seq, NH, LANE), F32)],
        scratch_shapes=[pltpu.VMEM((bsz, NH, DN_D, DN_D), F32)],
        compiler_params=_cp(VMEM_BIG, ("arbitrary",)),
    )(do, w, qg, kd, vn, p, gl, hist)
    du, dw, dqg, dkd, dgl = outs
    n = bsz * t_seq
    return (du.reshape(n, hw), dw.reshape(n, hw), dqg.reshape(n, hw), dkd.reshape(n, hw),
            dgl.reshape(bsz * nc_seq, NH, LANE))


def _dn_intra_bwd(qn, kn, v, projp, alog_row, dtb_row, u, w, tmat, du, dw, dqg, dkd, do, vn, dgl, *, nc_seq, name,
                  rider=None):
    n = qn.shape[0]
    nct = n // CHUNK
    hw = NH * DN_D
    scale = DN_D ** -0.5

    grp = _group(nc_seq)

    def inner(gi, q_ref, k_ref, v_ref, sa_ref, al_ref, dt_ref, u_ref, w_ref, t_ref, du_ref, dw_ref, dqg_ref, dkd_ref,
              do_ref, vn_ref, dgl_ref, dq_ref, dk_ref, dv_ref, dsa_ref, dal_ref, ddt_ref):
        ci = (pl.program_id(0) * grp + gi) % nc_seq
        sa = sa_ref[...]
        beta_t, g_t, gam_t, valid, ea = _dn_gates(sa, al_ref[...], dt_ref[...], ci)
        yield
        lane = _iota2((CHUNK, LANE), 1)
        gates_t = jnp.where(lane < 4, beta_t, gam_t).T
        r, c = _masks64()
        incl, strict, upper, supper = r >= c, r > c, r <= c, r < c
        rows1 = _iota2((CHUNK, 1), 0)
        acc = [jnp.zeros((CHUNK, LANE), F32)]

        def head(h):
            sl = slice(h * DN_D, (h + 1) * DN_D)
            beta_w = jnp.broadcast_to(beta_t[:, h:h + 1], (CHUNK, DN_D))
            gam_w = jnp.broadcast_to(gam_t[:, 4 + h:5 + h], (CHUNK, DN_D))
            beta_s, gam_s = beta_w[:, 0:CHUNK], gam_w[:, 0:CHUNK]
            beta_row = gates_t[h:h + 1, :]
            gam_row = gates_t[4 + h:5 + h, :]
            gl = gam_t[CHUNK - 1:CHUNK, 4 + h:5 + h]
            dec = jnp.exp(jnp.where(incl, gam_s - gam_row, -jnp.inf))
            dec_t = jnp.exp(jnp.where(upper, gam_row - gam_s, -jnp.inf))
            egam_w = jnp.exp(gam_w)
            ekd_w = jnp.exp(gl - gam_w)
            kh = k_ref[:, sl].astype(F32)
            qh = q_ref[:, sl].astype(F32) * scale
            vh = v_ref[:, sl].astype(F32)
            uh = u_ref[:, sl]
            wh = w_ref[:, sl]
            doh = do_ref[:, sl]
            vnh = vn_ref[:, sl]
            kk = _nt(kh, kh)
            qk = _nt(qh, kh)
            qk_t = _nt(kh, qh)
            dp = _nt(doh, vnh)
            dp_t = _nt(vnh, doh)
            t_hi, t_lo = _split(t_ref[0, h].T)
            duh, dwh = du_ref[:, sl], dw_ref[:, sl]
            dvb = _nn(t_hi, duh) + _nn(t_lo, duh)
            dkg = _nn(t_hi, dwh) + _nn(t_lo, dwh)
            yield
            dvb_hi, dvb_lo = _split(dvb)
            dkg_hi, dkg_lo = _split(dkg)
            m = (_nt(dvb_hi, uh) + _nt(dvb_lo, uh)) + (_nt(dkg_hi, wh) + _nt(dkg_lo, wh))
            m_t = (_nt(uh, dvb_hi) + _nt(uh, dvb_lo)) + (_nt(wh, dkg_hi) + _nt(wh, dkg_lo))
            yield
            da = jnp.where(strict, -m, 0.0)
            da_t = jnp.where(supper, -m_t, 0.0)
            a = jnp.where(strict, beta_s * kk * dec, 0.0)
            a_t = jnp.where(supper, beta_row * kk * dec_t, 0.0)
            dad = da * dec
            dad_t = da_t * dec_t
            dpm = jnp.where(incl, dp, 0.0)
            dpm_t = jnp.where(upper, dp_t, 0.0)
            e = da * a + dpm * (qk * dec)
            e_t = da_t * a_t + dpm_t * (qk_t * dec_t)
            dqgh = dqg_ref[:, sl].astype(F32)
            dkdh = dkd_ref[:, sl].astype(F32)
            bg_w = beta_w * egam_w
            dkh = (_nn(beta_s * dad, kh) + _nn(beta_row * dad_t, kh) + _nn(dpm_t * dec_t, qh)
                   + bg_w * dkg + ekd_w * dkdh)
            dqh = _nn(dpm * dec, kh) + egam_w * dqgh
            t_kd = dkdh * (ekd_w * kh)
            dbeta = (jnp.sum(dad * kk, axis=1, keepdims=True)
                     + jnp.sum(dkg * (egam_w * kh) + dvb * vh, axis=1, keepdims=True))
            dgam = (jnp.sum(e - e_t, axis=1, keepdims=True)
                    + jnp.sum(dkg * (bg_w * kh) + dqgh * (egam_w * qh) - t_kd, axis=1, keepdims=True))
            dgam_last = (jnp.sum(jnp.sum(t_kd, axis=0, keepdims=True), axis=1, keepdims=True)
                         + dgl_ref[0, h:h + 1, 0:1] * jnp.exp(gl))
            dgam = dgam + jnp.where(rows1 == CHUNK - 1, dgam_last, 0.0)
            dq_ref[:, sl] = (dqh * scale).astype(dq_ref.dtype)
            dk_ref[:, sl] = dkh.astype(dk_ref.dtype)
            dv_ref[:, sl] = (beta_w * dvb).astype(dv_ref.dtype)
            acc[0] = acc[0] + jnp.where(lane == h, dbeta, 0.0) + jnp.where(lane == 4 + h, dgam, 0.0)

        yield from _round_robin([head(h) for h in range(NH)])
        acc_t = acc[0]
        dg_t = _tri_sum(upper, acc_t)
        ddb = acc_t * beta_t * (1.0 - beta_t)
        dda = jnp.where(valid, dg_t * (-ea) * _sigmoid(sa + dt_ref[...]), 0.0)
        dsa_ref[...] = jnp.where(lane < 4, ddb, jnp.where(lane < 8, dda, 0.0)).astype(dsa_ref.dtype)
        in_g = jnp.logical_and(lane >= 4, lane < 8)
        dal = jnp.sum(jnp.where(in_g, dg_t * g_t, 0.0), axis=0, keepdims=True)
        ddt = jnp.sum(jnp.where(in_g, dda, 0.0), axis=0, keepdims=True)
        _accumulate(dal_ref, dal, gi)
        _accumulate(ddt_ref, ddt, gi)

    rows = grp * CHUNK
    row = pl.BlockSpec((rows, hw), lambda i: (i, 0))
    vec = pl.BlockSpec((1, LANE), lambda i: (0, 0))
    mat = pl.BlockSpec((grp, NH, CHUNK, CHUNK), lambda i: (i, 0, 0, 0))
    glb = pl.BlockSpec((grp, NH, LANE), lambda i: (i, 0, 0))
    big = jax.ShapeDtypeStruct((n, hw), F32)
    v128 = jax.ShapeDtypeStruct((1, LANE), F32)
    kinds = (["row"] * 4 + ["whole"] * 2 + ["row"] * 2 + ["lead"] + ["row"] * 6 + ["lead"]
             + ["row"] * 4 + ["whole"] * 2)
    outs, ridden = _hosted_call(
        _per_chunk(inner, kinds, grp), rider, name=name, grid=(nct // grp,),
        in_specs=[row, row, row, pl.BlockSpec((rows, LANE), lambda i: (i, 0)), vec, vec,
                  row, row, mat, row, row, row, row, row, row, glb],
        out_specs=[row, row, row, pl.BlockSpec((rows, LANE), lambda i: (i, 0)), vec, vec],
        out_shape=[big, big, big, jax.ShapeDtypeStruct((n, LANE), BF16), v128, v128],
        scratch_shapes=[], compiler_params=_cp(VMEM_BIG, ("arbitrary",)),
        args=(qn, kn, v, projp, alog_row, dtb_row, u, w, tmat, du, dw, dqg, dkd, do, vn, dgl))
    return (*outs, ridden)


GQ_W = NH * GLA_DK
GV_W = NH * GLA_DV
GLA_NORM = 16.0
MID = CHUNK // 2


def _gla_gates(sb, w2p, gb, chunk_in_seq):
    rows = _iota2((CHUNK, GQ_W), 0)
    valid = jnp.logical_or(rows >= N_PAD, chunk_in_seq > 0)
    graw = _nn(sb, w2p) + gb
    yield
    g = jnp.where(valid, _logsigmoid(graw) * (1.0 / GLA_NORM), 0.0)
    r, c = _masks64()
    bcum = _tri_sum(r >= c, g)
    yield
    return graw, bcum, valid


def _head_mask(h):
    lane = _iota2((1, GQ_W), 1)
    return jnp.logical_and(lane >= h * GLA_DK, lane < (h + 1) * GLA_DK)


def _gla_intra_fwd(projp, gates, w2p, gb, *, nc_seq, name):
    n = projp.shape[0]
    nct = n // CHUNK
    scale = GLA_DK ** -0.5

    grp = _group(nc_seq)
    rows = grp * CHUNK

    def inner(gi, qk_ref, v_ref, sb_ref, w2_ref, gb_ref, oi_ref, qg_ref, kd_ref, gl_ref):
        ci = (pl.program_id(0) * grp + gi) % nc_seq
        _, bc, _ = yield from _gla_gates(sb_ref[...], w2_ref[...], gb_ref[...], ci)
        bref = bc[MID:MID + 1, :]
        bl = bc[CHUNK - 1:CHUNK, :]
        q = qk_ref[:, 0:GQ_W].astype(F32) * scale
        k = qk_ref[:, GQ_W:2 * GQ_W].astype(F32)
        qi = q * jnp.exp(bc - bref)
        ki = k * jnp.exp(bref - bc)
        qg_ref[...] = (q * jnp.exp(bc)).astype(qg_ref.dtype)
        kd_ref[...] = (k * jnp.exp(bl - bc)).astype(kd_ref.dtype)
        gl_ref[0] = jnp.exp(bl)
        r, c = _masks64()
        incl = r >= c
        a = [jnp.where(incl, _nt(jnp.where(_head_mask(h), qi, 0.0), ki), 0.0) for h in range(NH)]
        yield
        for h in range(NH):
            oi_ref[:, h * GLA_DV:(h + 1) * GLA_DV] = _nn(a[h], v_ref[:, h * GLA_DV:(h + 1) * GLA_DV]).astype(oi_ref.dtype)

    kinds = ["row"] * 3 + ["whole"] * 2 + ["row"] * 3 + ["lead"]
    return pl.pallas_call(
        _per_chunk(inner, kinds, grp), name=name, grid=(nct // grp,),
        in_specs=[pl.BlockSpec((rows, 2 * GQ_W), lambda i: (i, C_GQK // (2 * GQ_W))),
                  pl.BlockSpec((rows, GV_W), lambda i: (i, C_GV // GV_W)),
                  pl.BlockSpec((rows, LANE), lambda i: (i, 1)),
                  pl.BlockSpec((LANE, GQ_W), lambda i: (0, 0)), pl.BlockSpec((1, GQ_W), lambda i: (0, 0))],
        out_specs=[pl.BlockSpec((rows, GV_W), lambda i: (i, 0)), pl.BlockSpec((rows, GQ_W), lambda i: (i, 0)),
                   pl.BlockSpec((rows, GQ_W), lambda i: (i, 0)), pl.BlockSpec((grp, 1, GQ_W), lambda i: (i, 0, 0))],
        out_shape=[jax.ShapeDtypeStruct((n, GV_W), BF16), jax.ShapeDtypeStruct((n, GQ_W), BF16),
                   jax.ShapeDtypeStruct((n, GQ_W), BF16), jax.ShapeDtypeStruct((nct, 1, GQ_W), F32)],
        compiler_params=_cp(VMEM_BIG),
    )(projp, projp, gates, w2p, gb)


def _gla_scan_fwd(oi, qg, kd, gl, projp, *, bsz, nc_seq, name, rider=None):
    t_seq = nc_seq * CHUNK
    oi = oi.reshape(bsz, t_seq, GV_W)
    qg, kd = qg.reshape(bsz, t_seq, GQ_W), kd.reshape(bsz, t_seq, GQ_W)
    gl = gl.reshape(bsz, nc_seq, 1, GQ_W)
    pj = projp.reshape(bsz, t_seq, PW)
    grp = _group(nc_seq)

    def body(oi_ref, qg_ref, kd_ref, gl_ref, v_ref, o_ref, hist_ref, st_ref):
        @pl.when(pl.program_id(0) == 0)
        def _():
            st_ref[...] = jnp.zeros_like(st_ref)

        for gi in range(grp):
            rows = pl.ds(gi * CHUNK, CHUNK)
            for b in range(bsz):
                st = st_ref[b]
                hist_ref[b, gi] = st.astype(hist_ref.dtype)
                qgb = qg_ref[b, rows, :]
                kdb = kd_ref[b, rows, :]
                upd = jnp.zeros((GLA_DV, GQ_W), F32)
                for h in range(NH):
                    sl = slice(h * GLA_DV, (h + 1) * GLA_DV)
                    m = _head_mask(h)
                    o_ref[b, rows, sl] = (oi_ref[b, rows, sl] + _nt(jnp.where(m, qgb, 0.0), st)).astype(o_ref.dtype)
                    upd = upd + jnp.where(m, _tn(v_ref[b, rows, sl], kdb), 0.0)
                st_ref[b] = gl_ref[b, gi] * st + upd

    rws = grp * CHUNK
    outs, ridden = _hosted_call(
        body, rider, name=name, grid=(nc_seq // grp,),
        in_specs=[pl.BlockSpec((bsz, rws, GV_W), lambda i: (0, i, 0)),
                  pl.BlockSpec((bsz, rws, GQ_W), lambda i: (0, i, 0)),
                  pl.BlockSpec((bsz, rws, GQ_W), lambda i: (0, i, 0)),
                  pl.BlockSpec((bsz, grp, 1, GQ_W), lambda i: (0, i, 0, 0)),
                  pl.BlockSpec((bsz, rws, GV_W), lambda i: (0, i, C_GV // GV_W))],
        out_specs=[pl.BlockSpec((bsz, rws, GV_W), lambda i: (0, i, 0)),
                   pl.BlockSpec((bsz, grp, GLA_DV, GQ_W), lambda i: (0, i, 0, 0))],
        out_shape=[jax.ShapeDtypeStruct((bsz, t_seq, GV_W), BF16),
                   jax.ShapeDtypeStruct((bsz, nc_seq, GLA_DV, GQ_W), F32)],
        scratch_shapes=[pltpu.VMEM((bsz, GLA_DV, GQ_W), F32)],
        compiler_params=_cp(VMEM_BIG, ("arbitrary",)), args=(oi, qg, kd, gl, pj))
    return outs[0].reshape(bsz * t_seq, GV_W), outs[1], ridden


def _gla_scan_bwd(do, qg, kd, gl, projp, hist, *, bsz, nc_seq, name):
    t_seq = nc_seq * CHUNK
    do = do.reshape(bsz, t_seq, GV_W)
    qg, kd = qg.reshape(bsz, t_seq, GQ_W), kd.reshape(bsz, t_seq, GQ_W)
    gl = gl.reshape(bsz, nc_seq, 1, GQ_W)
    pj = projp.reshape(bsz, t_seq, PW)
    grp = _group(nc_seq)

    def body(do_ref, qg_ref, kd_ref, gl_ref, v_ref, hist_ref, dqg_ref, dkd_ref, dv_ref, dgl_ref, dst_ref):
        @pl.when(pl.program_id(0) == 0)
        def _():
            dst_ref[...] = jnp.zeros_like(dst_ref)

        for gi in reversed(range(grp)):
            rows = pl.ds(gi * CHUNK, CHUNK)
            for b in range(bsz):
                st = hist_ref[b, gi]
                dst = dst_ref[b]
                qgb = qg_ref[b, rows, :]
                kdb = kd_ref[b, rows, :]
                dqg = jnp.zeros((CHUNK, GQ_W), F32)
                dkd = jnp.zeros((CHUNK, GQ_W), F32)
                add = jnp.zeros((GLA_DV, GQ_W), F32)
                for h in range(NH):
                    sl = slice(h * GLA_DV, (h + 1) * GLA_DV)
                    m = _head_mask(h)
                    doh = do_ref[b, rows, sl]
                    vh = v_ref[b, rows, sl]
                    dqg = dqg + jnp.where(m, _nn(doh, st), 0.0)
                    dkd = dkd + jnp.where(m, _nn(vh, dst), 0.0)
                    dv_ref[b, rows, sl] = _nt(jnp.where(m, kdb, 0.0), dst).astype(dv_ref.dtype)
                    add = add + jnp.where(m, _tn(doh, qgb), 0.0)
                dqg_ref[b, rows, :] = dqg.astype(dqg_ref.dtype)
                dkd_ref[b, rows, :] = dkd.astype(dkd_ref.dtype)
                dgl_ref[b, gi] = jnp.sum(dst * st, axis=0, keepdims=True)
                dst_ref[b] = gl_ref[b, gi] * dst + add

    steps = nc_seq // grp
    rws = grp * CHUNK
    rev = lambda i: steps - 1 - i
    outs = pl.pallas_call(
        body, name=name, grid=(steps,),
        in_specs=[pl.BlockSpec((bsz, rws, GV_W), lambda i: (0, rev(i), 0)),
                  pl.BlockSpec((bsz, rws, GQ_W), lambda i: (0, rev(i), 0)),
                  pl.BlockSpec((bsz, rws, GQ_W), lambda i: (0, rev(i), 0)),
                  pl.BlockSpec((bsz, grp, 1, GQ_W), lambda i: (0, rev(i), 0, 0)),
                  pl.BlockSpec((bsz, rws, GV_W), lambda i: (0, rev(i), C_GV // GV_W)),
                  pl.BlockSpec((bsz, grp, GLA_DV, GQ_W), lambda i: (0, rev(i), 0, 0))],
        out_specs=[pl.BlockSpec((bsz, rws, GQ_W), lambda i: (0, rev(i), 0)),
                   pl.BlockSpec((bsz, rws, GQ_W), lambda i: (0, rev(i), 0)),
                   pl.BlockSpec((bsz, rws, GV_W), lambda i: (0, rev(i), 0)),
                   pl.BlockSpec((bsz, grp, 1, GQ_W), lambda i: (0, rev(i), 0, 0))],
        out_shape=[jax.ShapeDtypeStruct((bsz, t_seq, GQ_W), F32), jax.ShapeDtypeStruct((bsz, t_seq, GQ_W), F32),
                   jax.ShapeDtypeStruct((bsz, t_seq, GV_W), BF16), jax.ShapeDtypeStruct((bsz, nc_seq, 1, GQ_W), F32)],
        scratch_shapes=[pltpu.VMEM((bsz, GLA_DV, GQ_W), F32)],
        compiler_params=_cp(VMEM_BIG, ("arbitrary",)),
    )(do, qg, kd, gl, pj, hist)
    n = bsz * t_seq
    return (outs[0].reshape(n, GQ_W), outs[1].reshape(n, GQ_W), outs[2].reshape(n, GV_W),
            outs[3].reshape(bsz * nc_seq, 1, GQ_W))


def _gla_intra_bwd(projp, gates, w2p, gb, do, dqg, dkd, dvi, dgl, *, nc_seq, name):
    n = projp.shape[0]
    nct = n // CHUNK
    scale = GLA_DK ** -0.5

    grp = _group(nc_seq)
    rows = grp * CHUNK

    def inner(gi, qk_ref, v_ref, sb_ref, w2_ref, gb_ref, do_ref, dqg_ref, dkd_ref, dvi_ref, dgl_ref,
              dqk_ref, dv_ref, dsb_ref, dw2_ref, dgb_ref):
        ci = (pl.program_id(0) * grp + gi) % nc_seq
        sb = sb_ref[...]
        w2 = w2_ref[...]
        graw, bc, valid = yield from _gla_gates(sb, w2, gb_ref[...], ci)
        bref = bc[MID:MID + 1, :]
        bl = bc[CHUNK - 1:CHUNK, :]
        q = qk_ref[:, 0:GQ_W].astype(F32) * scale
        k = qk_ref[:, GQ_W:2 * GQ_W].astype(F32)
        ex1 = jnp.exp(bc - bref)
        ex2 = jnp.exp(bref - bc)
        eb = jnp.exp(bc)
        ekd = jnp.exp(bl - bc)
        qi, ki = q * ex1, k * ex2
        r, c = _masks64()
        incl = r >= c
        upper = r <= c
        a_t, da, da_t = [], [], []
        for h in range(NH):
            sl = slice(h * GLA_DV, (h + 1) * GLA_DV)
            doh = do_ref[:, sl]
            vh = v_ref[:, sl]
            a_t.append(jnp.where(upper, _nt(jnp.where(_head_mask(h), ki, 0.0), qi), 0.0))
            da.append(jnp.where(incl, _nt(doh, vh), 0.0))
            da_t.append(jnp.where(upper, _nt(vh, doh), 0.0))
        yield
        dqi = jnp.zeros((CHUNK, GQ_W), F32)
        dki = jnp.zeros((CHUNK, GQ_W), F32)
        for h in range(NH):
            sl = slice(h * GLA_DV, (h + 1) * GLA_DV)
            m = _head_mask(h)
            dv_ref[:, sl] = (_nn(a_t[h], do_ref[:, sl]) + dvi_ref[:, sl]).astype(dv_ref.dtype)
            dqi = dqi + jnp.where(m, _nn(da[h], ki), 0.0)
            dki = dki + jnp.where(m, _nn(da_t[h], qi), 0.0)
        yield
        dqg = dqg_ref[...].astype(F32)
        dkd = dkd_ref[...].astype(F32)
        dqk_ref[:, 0:GQ_W] = ((dqi * ex1 + dqg * eb) * scale).astype(dqk_ref.dtype)
        dqk_ref[:, GQ_W:2 * GQ_W] = (dki * ex2 + dkd * ekd).astype(dqk_ref.dtype)
        t_qi, t_ki, t_kd = dqi * qi, dki * ki, dkd * (k * ekd)
        db = t_qi - t_ki + dqg * (q * eb) - t_kd
        dbref = jnp.sum(t_ki - t_qi, axis=0, keepdims=True)
        dbl = jnp.sum(t_kd, axis=0, keepdims=True) + dgl_ref[0] * jnp.exp(bl)
        rows = _iota2((CHUNK, GQ_W), 0)
        db = db + jnp.where(rows == MID, dbref, 0.0) + jnp.where(rows == CHUNK - 1, dbl, 0.0)
        dg = _tri_sum(upper, db)
        yield
        dgraw = jnp.where(valid, dg * (1.0 / GLA_NORM) * _sigmoid(-graw), 0.0)
        dsb_ref[...] = _nt(dgraw, w2).astype(dsb_ref.dtype)
        dw2 = _tn(sb, dgraw)
        dgb = jnp.sum(dgraw, axis=0, keepdims=True)
        _accumulate(dw2_ref, dw2, gi)
        _accumulate(dgb_ref, dgb, gi)

    rq = pl.BlockSpec((rows, GQ_W), lambda i: (i, 0))
    rv = pl.BlockSpec((rows, GV_W), lambda i: (i, 0))
    kinds = ["row"] * 3 + ["whole"] * 2 + ["row"] * 4 + ["lead"] + ["row"] * 3 + ["whole"] * 2
    return pl.pallas_call(
        _per_chunk(inner, kinds, grp), name=name, grid=(nct // grp,),
        in_specs=[pl.BlockSpec((rows, 2 * GQ_W), lambda i: (i, C_GQK // (2 * GQ_W))),
                  pl.BlockSpec((rows, GV_W), lambda i: (i, C_GV // GV_W)),
                  pl.BlockSpec((rows, LANE), lambda i: (i, 1)),
                  pl.BlockSpec((LANE, GQ_W), lambda i: (0, 0)), pl.BlockSpec((1, GQ_W), lambda i: (0, 0)),
                  rv, rq, rq, rv, pl.BlockSpec((grp, 1, GQ_W), lambda i: (i, 0, 0))],
        out_specs=[pl.BlockSpec((rows, 2 * GQ_W), lambda i: (i, 0)), rv, pl.BlockSpec((rows, LANE), lambda i: (i, 0)),
                   pl.BlockSpec((LANE, GQ_W), lambda i: (0, 0)), pl.BlockSpec((1, GQ_W), lambda i: (0, 0))],
        out_shape=[jax.ShapeDtypeStruct((n, 2 * GQ_W), BF16), jax.ShapeDtypeStruct((n, GV_W), BF16),
                   jax.ShapeDtypeStruct((n, LANE), BF16), jax.ShapeDtypeStruct((LANE, GQ_W), F32),
                   jax.ShapeDtypeStruct((1, GQ_W), F32)],
        compiler_params=_cp(VMEM_BIG, ("arbitrary",)),
    )(projp, projp, gates, w2p, gb, do, dqg, dkd, dvi, dgl)


SECTIONS = ((C_QKV, 1536), (C_DZ, 512), (C_GQK, 512), (C_GV, 512), (C_GR, 512), (C_SA, 128), (C_SB, 128))


def _inproj_bwd(secs, wp, h0, g1, dx1, *, bsz, t_seq, tr, name, rider=None):
    n, d = h0.shape
    per_seq = t_seq // tr
    steps = n // tr
    s_len = t_seq - CHUNK

    def body(*refs):
        sec_refs = refs[:len(SECTIONS)]
        wp_ref, h0_ref, g_ref, dx1_ref, gx_hbm, meta_ref, dg_ref, obuf, sems = refs[len(SECTIONS):]
        i = pl.program_id(0)
        slot = i % 2

        def put(step, slot_, act):
            b, j = step // per_seq, step % per_seq

            @pl.when(j == 0)
            def _():
                act(pltpu.make_async_copy(obuf.at[slot_, pl.ds(CHUNK, tr - CHUNK), :],
                                          gx_hbm.at[b, pl.ds(0, tr - CHUNK), :], sems.at[slot_]))

            if per_seq > 1:
                @pl.when(j > 0)
                def _():
                    act(pltpu.make_async_copy(obuf.at[slot_], gx_hbm.at[b, pl.ds(j * tr - CHUNK, tr), :],
                                              sems.at[slot_]))

        dh = None
        for s_ref, (off, wd) in zip(sec_refs, SECTIONS):
            part = _nt(s_ref[...], wp_ref[:, off:off + wd])
            dh = part if dh is None else dh + part
        dx, dg = _rms_bwd_math(h0_ref[...], g_ref[...], dh)
        tot = dx1_ref[...] + dx

        @pl.when(i >= 2)
        def _():
            put(i - 2, slot, lambda cp: cp.wait())

        obuf[slot] = tot
        put(i, slot, lambda cp: cp.start())

        @pl.when(i % per_seq == 0)
        def _():
            meta_ref[...] = tot[N_PAD:CHUNK, :]

        @pl.when(i == steps - 1)
        def _():
            if steps > 1:
                put(i - 1, 1 - slot, lambda cp: cp.wait())
            put(i, slot, lambda cp: cp.wait())

        @pl.when(i == 0)
        def _():
            dg_ref[...] = dg

        @pl.when(i > 0)
        def _():
            dg_ref[...] += dg

    row = pl.BlockSpec((tr, d), lambda i: (i, 0))
    vec = pl.BlockSpec((1, d), lambda i: (0, 0))
    outs, ridden = _hosted_call(
        body, rider, name=name, grid=(steps,),
        in_specs=[pl.BlockSpec((tr, wd), lambda i: (i, 0)) for _, wd in SECTIONS]
        + [pl.BlockSpec((d, PW), lambda i: (0, 0)), row, vec, row],
        out_specs=[ANY, pl.BlockSpec((N_META, d), lambda i: (i // per_seq, 0)), vec],
        out_shape=[jax.ShapeDtypeStruct((bsz, s_len, d), F32), jax.ShapeDtypeStruct((bsz * N_META, d), F32),
                   jax.ShapeDtypeStruct((1, d), F32)],
        scratch_shapes=[pltpu.VMEM((2, tr, d), F32), pltpu.SemaphoreType.DMA((2,))],
        compiler_params=_cp(VMEM_BIG, ("arbitrary",)), args=(*secs, wp, h0, g1, dx1))
    return (*outs, ridden)


def _adamw(w, g, m, v, *, name, emit_grad=False, col_tile=None):
    lead = w.ndim - 2
    r, c = w.shape[-2:]
    tr = r if col_tile is not None else (_tile(r, 256, 8) if r > 256 else r)
    tc = col_tile if col_tile is not None else c
    c1 = 1.0 - ADAM_B1 ** ADAM_STEP
    c2 = 1.0 - ADAM_B2 ** ADAM_STEP
    n_out = 4 if emit_grad else 3

    def body(w_ref, g_ref, m_ref, v_ref, *out_refs):
        rd = (lambda ref: ref[0]) if lead else (lambda ref: ref[...])
        gv = g_ref[:, 0:tc]
        nm = ADAM_B1 * rd(m_ref) + (1.0 - ADAM_B1) * gv
        nv = ADAM_B2 * rd(v_ref) + (1.0 - ADAM_B2) * (gv * gv)
        res = [-ADAM_LR * ((nm / c1) / (jnp.sqrt(nv / c2) + ADAM_EPS) + ADAM_WD * rd(w_ref)), nm, nv, gv]
        for o_ref, val in zip(out_refs, res):
            if lead:
                o_ref[0] = val
            else:
                o_ref[...] = val

    if col_tile is None:
        blk = pl.BlockSpec((1,) * lead + (tr, c), lambda i: (0,) * lead + (i, 0))
        gblk = pl.BlockSpec((tr, g.shape[1]), lambda i: (i, 0))
        steps = r // tr
    else:
        blk = pl.BlockSpec((1,) * lead + (r, tc), lambda j: (0,) * lead + (0, j))
        gblk = pl.BlockSpec((r, tc), lambda j: (0, j))
        steps = c // tc
    sds = jax.ShapeDtypeStruct(w.shape, F32)
    return pl.pallas_call(
        body, name=name, grid=(steps,), in_specs=[blk, gblk, blk, blk], out_specs=[blk] * n_out,
        out_shape=[sds] * n_out, compiler_params=_cp(VMEM_BIG),
    )(w, g, m, v)


def _adamw_rows(w, g, m, v, *, name):
    r, _, c = w.shape
    tr = max(t for t in range(1, 129) if r % t == 0)
    c1 = 1.0 - ADAM_B1 ** ADAM_STEP
    c2 = 1.0 - ADAM_B2 ** ADAM_STEP

    def body(w_ref, g_ref, m_ref, v_ref, d_ref, nm_ref, nv_ref, go_ref):
        gv = g_ref[...]
        nm = ADAM_B1 * m_ref[...] + (1.0 - ADAM_B1) * gv
        nv = ADAM_B2 * v_ref[...] + (1.0 - ADAM_B2) * (gv * gv)
        d_ref[...] = -ADAM_LR * ((nm / c1) / (jnp.sqrt(nv / c2) + ADAM_EPS) + ADAM_WD * w_ref[...])
        nm_ref[...] = nm
        nv_ref[...] = nv
        go_ref[...] = gv

    blk = pl.BlockSpec((tr, 1, c), lambda i: (i, 0, 0))
    sds = jax.ShapeDtypeStruct(w.shape, F32)
    return pl.pallas_call(
        body, name=name, grid=(r // tr,), in_specs=[blk] * 4, out_specs=[blk] * 4, out_shape=[sds] * 4,
        compiler_params=_cp(VMEM_BIG),
    )(w, g, m, v)


def _pair_sum(where, g, theirs, *, name):
    lead, r, cols = g.shape
    half = r // 2
    tr = _tile(half, 256, 16)
    nh = half // tr

    def body(w_ref, a_ref, b_ref, o_ref):
        o_ref[...] = (a_ref[...] + b_ref[...]).astype(o_ref.dtype)

    blk = pl.BlockSpec((1, tr, cols), lambda s, i, w: (s, i, 0))
    return pl.pallas_call(
        body, name=name,
        grid_spec=pltpu.PrefetchScalarGridSpec(
            num_scalar_prefetch=1, grid=(lead, nh),
            in_specs=[pl.BlockSpec((1, tr, cols), lambda s, i, w: (s, w[0] * nh + i, 0)), blk], out_specs=blk),
        out_shape=jax.ShapeDtypeStruct((lead, half, cols), BF16), compiler_params=_cp(VMEM_BIG),
    )(where, g, theirs)


def _chip_sum(where, pair, q, *, name):
    _, half, cols = pair.shape
    tr = _tile(half, 256, 16)
    nh = half // tr

    def body(w_ref, own_ref, q1_ref, q2_ref, q3_ref, o_ref):
        f = lambda ref: ref[0].astype(F32)
        o_ref[...] = ((f(own_ref) + f(q1_ref)) + f(q2_ref)) + f(q3_ref)

    def peer(d):
        return pl.BlockSpec((1, tr, cols), lambda i, w: ((w[1] + d) % N_CHIPS, i, 0))

    return pl.pallas_call(
        body, name=name,
        grid_spec=pltpu.PrefetchScalarGridSpec(
            num_scalar_prefetch=1, grid=(nh,),
            in_specs=[peer(0), peer(1), peer(2), peer(3)],
            out_specs=pl.BlockSpec((tr, cols), lambda i, w: (w[0] * nh + i, 0))),
        out_shape=jax.ShapeDtypeStruct((2 * half, cols), F32), compiler_params=_cp(VMEM_BIG),
    )(where, pair, q, q, q)


VM = pl.BlockSpec(memory_space=pltpu.VMEM)


def _row_chunks(rows, n_split):
    size = rows // n_split
    assert size * n_split == rows and size % 16 == 0, (rows, n_split)
    return [(s, pl.ds(s * size, size)) for s in range(n_split)], size


D2D_SPLIT = 4
ICI_SPLIT = 2


def _sibling_halves(grads):
    n_arr = len(grads)

    def body(*refs):
        ins = refs[:n_arr]
        theirs = refs[n_arr:2 * n_arr]
        send_sems, recv_sems = refs[2 * n_arr:]
        x, y, c = _place()
        copies = []
        for k in range(n_arr):
            half = ins[k].shape[1] // 2
            chunks, size = _row_chunks(half, D2D_SPLIT)
            for s, dst_rows in chunks:
                give = pltpu.make_async_remote_copy(
                    src_ref=ins[k].at[:, pl.ds((1 - c) * half + s * size, size), :], dst_ref=theirs[k].at[:, dst_rows, :],
                    send_sem=send_sems.at[k, s], recv_sem=recv_sems.at[k, s], device_id=(x, y, 1 - c),
                    device_id_type=MESH)
                give.start()
                copies.append(give)
        for give in copies:
            give.wait()

    halves = [jax.ShapeDtypeStruct((g.shape[0], g.shape[1] // 2, g.shape[2]), F32) for g in grads]
    sem = pltpu.SemaphoreType.DMA((n_arr, D2D_SPLIT))
    return pl.pallas_call(
        body, name="sibling_halves", in_specs=[ANY] * n_arr, out_specs=[ANY] * n_arr, out_shape=halves,
        scratch_shapes=[sem, sem],
    )(*grads)


def _chip_exchange(parts):
    n_arr = len(parts)

    def body(*refs):
        ins = refs[:n_arr]
        outs = refs[n_arr:2 * n_arr]
        send_sems, recv_sems = refs[2 * n_arr:]
        x, y, c = _place()
        me = 2 * x + y
        sends = []
        for k in range(n_arr):
            chunks, _ = _row_chunks(ins[k].shape[1], ICI_SPLIT)
            for d, (px, py, pj) in enumerate(_other_chips(x, y)):
                for s, rows in chunks:
                    cp = pltpu.make_async_remote_copy(
                        src_ref=ins[k].at[pj, rows, :], dst_ref=outs[k].at[me, rows, :], send_sem=send_sems.at[k, d, s],
                        recv_sem=recv_sems.at[k, d, s], device_id=(px, py, c), device_id_type=MESH)
                    cp.start()
                    sends.append(cp)
        for k in range(n_arr):
            chunks, _ = _row_chunks(ins[k].shape[1], ICI_SPLIT)
            for d, (px, py, pj) in enumerate(_other_chips(x, y)):
                for s, rows in chunks:
                    pltpu.make_async_remote_copy(
                        src_ref=ins[k].at[pj, rows, :], dst_ref=outs[k].at[pj, rows, :], send_sem=send_sems.at[k, d, s],
                        recv_sem=recv_sems.at[k, d, s], device_id=(px, py, c), device_id_type=MESH).wait_recv()
        for cp in sends:
            cp.wait_send()

    sem = pltpu.SemaphoreType.DMA((n_arr, 3, ICI_SPLIT))
    return pl.pallas_call(
        body, name="chip_exchange", in_specs=[ANY] * n_arr, out_specs=[ANY] * n_arr,
        out_shape=[jax.ShapeDtypeStruct(p.shape, p.dtype) for p in parts],
        scratch_shapes=[sem, sem],
    )(*parts)


class _SiblingHalvesRider:
    def __init__(self, grads):
        self.inputs = list(grads)
        self.out_shapes = [jax.ShapeDtypeStruct((g.shape[0], g.shape[1] // 2, g.shape[2]), F32) for g in grads]
        self.aliases = {}
        self.sems = [pltpu.SemaphoreType.DMA((len(grads), D2D_SPLIT))] * 2

    def _copies(self, ins, outs, sems):
        x, y, c = _place()
        for k in range(len(ins)):
            half = ins[k].shape[1] // 2
            chunks, size = _row_chunks(half, D2D_SPLIT)
            for s, dst_rows in chunks:
                yield pltpu.make_async_remote_copy(
                    src_ref=ins[k].at[:, pl.ds((1 - c) * half + s * size, size), :], dst_ref=outs[k].at[:, dst_rows, :],
                    send_sem=sems[0].at[k, s], recv_sem=sems[1].at[k, s], device_id=(x, y, 1 - c), device_id_type=MESH)

    def first(self, ins, outs, sems):
        for cp in self._copies(ins, outs, sems):
            cp.start()

    def last(self, ins, outs, sems):
        for cp in self._copies(ins, outs, sems):
            cp.wait()


class _ChipExchangeRider:
    def __init__(self, parts):
        self.inputs = list(parts)
        self.out_shapes = [jax.ShapeDtypeStruct(p.shape, p.dtype) for p in parts]
        self.aliases = {}
        self.sems = [pltpu.SemaphoreType.DMA((len(parts), 3, ICI_SPLIT))] * 2

    def _copies(self, ins, outs, sems, receiving):
        x, y, c = _place()
        for k in range(len(ins)):
            chunks, _ = _row_chunks(ins[k].shape[1], ICI_SPLIT)
            for d, (px, py, pj) in enumerate(_other_chips(x, y)):
                for s, rows in chunks:
                    yield pltpu.make_async_remote_copy(
                        src_ref=ins[k].at[pj, rows, :], dst_ref=outs[k].at[pj if receiving else 2 * x + y, rows, :],
                        send_sem=sems[0].at[k, d, s], recv_sem=sems[1].at[k, d, s], device_id=(px, py, c),
                        device_id_type=MESH)

    def first(self, ins, outs, sems):
        for cp in self._copies(ins, outs, sems, False):
            cp.start()

    def last(self, ins, outs, sems):
        for cp in self._copies(ins, outs, sems, True):
            cp.wait_recv()
        for cp in self._copies(ins, outs, sems, False):
            cp.wait_send()


def _sibling_join(bufs):
    n_arr = len(bufs)

    def body(*refs):
        bufs_out = refs[n_arr:2 * n_arr]
        send_sems, recv_sems = refs[2 * n_arr:]
        x, y, c = _place()
        copies = []
        for k in range(n_arr):
            half = bufs_out[k].shape[0] // 2
            chunks, size = _row_chunks(half, D2D_SPLIT)
            for s, _ in chunks:
                rows = pl.ds(c * half + s * size, size)
                give = pltpu.make_async_remote_copy(
                    src_ref=bufs_out[k].at[rows, :], dst_ref=bufs_out[k].at[rows, :], send_sem=send_sems.at[k, s],
                    recv_sem=recv_sems.at[k, s], device_id=(x, y, 1 - c), device_id_type=MESH)
                give.start()
                copies.append((k, s, half, size, give))
        for k, s, half, size, give in copies:
            rows = pl.ds((1 - c) * half + s * size, size)
            pltpu.make_async_remote_copy(
                src_ref=bufs_out[k].at[rows, :], dst_ref=bufs_out[k].at[rows, :], send_sem=send_sems.at[k, s],
                recv_sem=recv_sems.at[k, s], device_id=(x, y, 1 - c), device_id_type=MESH).wait_recv()
            give.wait_send()

    sem = pltpu.SemaphoreType.DMA((n_arr, D2D_SPLIT))
    return pl.pallas_call(
        body, name="sibling_join", in_specs=[ANY] * n_arr, out_specs=[ANY] * n_arr,
        out_shape=[jax.ShapeDtypeStruct(b.shape, F32) for b in bufs],
        input_output_aliases={k: k for k in range(n_arr)},
        scratch_shapes=[sem, sem],
    )(*bufs)


PACK_ROWS = 48


def _small_allreduce(pack):
    masks = [(dx, dy, dc) for dx in (0, 1) for dy in (0, 1) for dc in (0, 1)][1:]

    def body(p_ref, o_ref, buf, send_sems, recv_sems):
        x, y, c = _place()
        me = 4 * x + 2 * y + c
        buf[me] = p_ref[...]
        sends = []
        for k, (dx, dy, dc) in enumerate(masks):
            peer = (1 - x if dx else x, 1 - y if dy else y, 1 - c if dc else c)
            cp = pltpu.make_async_remote_copy(
                src_ref=p_ref, dst_ref=buf.at[me], send_sem=send_sems.at[k], recv_sem=recv_sems.at[k],
                device_id=peer, device_id_type=MESH)
            cp.start()
            sends.append(cp)
        for k, (dx, dy, dc) in enumerate(masks):
            peer = (1 - x if dx else x, 1 - y if dy else y, 1 - c if dc else c)
            pj = 4 * peer[0] + 2 * peer[1] + peer[2]
            pltpu.make_async_remote_copy(
                src_ref=p_ref, dst_ref=buf.at[pj], send_sem=send_sems.at[k], recv_sem=recv_sems.at[k],
                device_id=peer, device_id_type=MESH).wait_recv()
        for cp in sends:
            cp.wait_send()
        tot = buf[0]
        for k in range(1, 8):
            tot = tot + buf[k]
        o_ref[...] = tot
        o_ref[0:N_META, :] = tot[0:N_META] + tot[N_META:2 * N_META]

    return pl.pallas_call(
        body, name="small_allreduce", in_specs=[VM], out_specs=VM,
        out_shape=jax.ShapeDtypeStruct((PACK_ROWS, D_MODEL), F32),
        scratch_shapes=[pltpu.VMEM((8, PACK_ROWS, D_MODEL), F32), pltpu.SemaphoreType.DMA((7,)),
                        pltpu.SemaphoreType.DMA((7,))],
    )(pack)


def _pad_lanes(vec, offset):
    k = vec.shape[1]
    return jnp.concatenate([jnp.zeros((1, offset), F32), vec, jnp.zeros((1, LANE - offset - k), F32)], axis=1)


def _local_step(x, tgt, meta, norm1_g, wp, conv_w, a_log, dt_bias, dn_norm_g, gla_w2, gla_b, gla_norm_g,
                w_out, norm2_g, w_up, w_down, final_norm_g, late_gather=None, where=None, early_gather=None):
    bsz, s_len, d = x.shape
    t_seq = s_len + CHUNK
    nc_seq = t_seq // CHUNK
    n = bsz * t_seq
    tr = _tile(t_seq, 832)
    tt = _tile(t_seq, 416)

    lead = jnp.concatenate([jnp.zeros((N_PAD, d), F32), meta], axis=0)
    alog_row = _pad_lanes(a_log, 4)
    dtb_row = _pad_lanes(dt_bias, 4)

    h0, h, got_early = _embed_norm(x, lead, norm1_g, tr=tr, name="embed_norm1", rider=early_gather)
    if early_gather is not None:
        wp = _padded_from_shards(got_early[0])
        conv_w = got_early[1].transpose(1, 0, 2).reshape(4, QKV_W)
        gla_w2 = got_early[2][:, :, 0:GQ_W // N_CHIPS].transpose(1, 0, 2).reshape(GLA_RANK, GQ_W)
    w2p = jnp.concatenate([gla_w2, jnp.zeros((LANE - GLA_RANK, GQ_W), F32)], axis=0)
    ride_up, ride_down, ride_out = late_gather if late_gather is not None else (None, None, None)
    projp, gates = _mm(h, wp, "nn", tm=tt, tn=PW, tk=d, out_dtypes=(BF16, F32), out_widths=(PW, PW - C_SA),
                       epilogue=lambda acc: (acc, acc[:, C_SA:PW]), name="in_proj")
    qn, kn, v, got_up = _dnprep_fwd(projp, conv_w, bsz=bsz, t_seq=t_seq, tt=tt, name="dn_prep", rider=ride_up)
    u, w, qg, kd, pmat, tmat, gl, got_down = _dn_intra_fwd(qn, kn, v, gates, alog_row, dtb_row, nc_seq=nc_seq,
                                                           name="dn_intra", rider=ride_down)
    o_dn, vn, hist, got_out = _dn_scan_fwd(u, w, qg, kd, pmat, gl, bsz=bsz, nc_seq=nc_seq, name="dn_scan",
                                           rider=ride_out)
    oi, gqg, gkd, ggl = _gla_intra_fwd(projp, gates, w2p, gla_b, nc_seq=nc_seq, name="gla_intra")
    o_gla, ghist, _ = _gla_scan_fwd(oi, gqg, gkd, ggl, projp, bsz=bsz, nc_seq=nc_seq, name="gla_scan")
    if late_gather is not None:
        w_out = got_out[0].reshape(d, d)
        w_up = got_up[0].transpose(1, 0, 2).reshape(d, D_FF)
        w_down = got_down[0].reshape(D_FF, d)
    mix = _gnorm_fwd(o_dn, o_gla, projp, dn_norm_g, gla_norm_g, tr=tr, name="gated_norm")
    def residual_norm(acc, res, g):
        x1v = res + acc
        r = lax.rsqrt(jnp.mean(x1v * x1v, axis=-1, keepdims=True) + EPS)
        return x1v, x1v * r * g

    x1, h2 = _mm(mix, w_out, "nn", tm=tr, tn=d, tk=d, out_dtypes=(F32, BF16), extras=(h0,), vec_extras=(norm2_g,),
                 epilogue=residual_norm, name="out_proj_norm2")

    (act,) = _mm(h2, w_up, "nn", tm=tt, tn=D_FF, tk=d, out_dtypes=(BF16,),
                 epilogue=lambda acc: (jnp.square(jnp.maximum(acc, 0.0)),), name="mlp_up", n_chunk=1024)
    dx2, dx2b, d_final_g, loss_tile = _mlp_down_loss(act, w_down, x1, final_norm_g, tgt, t_seq=t_seq, tr=tt,
                                                     name="mlp_down_loss")

    def relu2_bwd(acc, a):
        a = a.astype(F32)
        return (acc * (2.0 * (a * lax.rsqrt(jnp.maximum(a, F32_TINY)))),)

    (dup,) = _mm(dx2b, w_down, "nt", tm=tt, tn=D_FF, tk=d, out_dtypes=(BF16,), extras=(act,),
                 epilogue=relu2_bwd, name="mlp_down_bwd", n_chunk=1024)
    tk2 = 2 * tr if n % (2 * tr) == 0 else tr
    (d_w_down,) = _mm(act, dx2b, "tn", tm=D_FF // 2, tn=d, tk=tk2, out_dtypes=(F32,), name="w_down_grad")
    (d_w_up_sm,) = _mm(h2, dup, "tn", tm=d, tn=D_FF // 2, tk=tk2, out_dtypes=(F32,), name="w_up_grad",
                       shard_cols=D_FF // N_CHIPS)
    mlp_sm = [d_w_up_sm, d_w_down.reshape(N_CHIPS, D_FF // N_CHIPS, d)]
    ride1 = _SiblingHalvesRider(mlp_sm) if where is not None else None
    dx1, dx1b, d_norm2_g, theirs = _mlp_up_bwd_norm(dup, w_up, x1, norm2_g, dx2, tr=tt, name="mlp_up_bwd_norm",
                                                    rider=ride1)
    ride2 = None
    if where is not None:
        mlp_pair = [_pair_sum(where, a, b, name=f"pair_sum_mlp{k}") for k, (a, b) in enumerate(zip(mlp_sm, theirs))]
        ride2 = _ChipExchangeRider(mlp_pair)

    (dmix,) = _mm(dx1b, w_out, "nt", tm=tr, tn=d, tk=d, out_dtypes=(BF16,), name="out_proj_bwd")
    (d_w_out,) = _mm(mix, dx1b, "tn", tm=d, tn=d, tk=tr, out_dtypes=(F32,), name="w_out_grad")
    do_dn, ddz, do_gla, dgr, d_dn_norm_g, d_gla_norm_g = _gnorm_bwd(
        dmix, o_dn, o_gla, projp, dn_norm_g, gla_norm_g, tr=tr, name="gated_norm_bwd")
    du, dw, dqg, dkd, dgl = _dn_scan_bwd(do_dn, w, qg, kd, vn, pmat, gl, hist, bsz=bsz, nc_seq=nc_seq,
                                          name="dn_scan_bwd")
    dqn, dkn, dv, dsa, d_alog, d_dtb, mlp_parts = _dn_intra_bwd(
        qn, kn, v, gates, alog_row, dtb_row, u, w, tmat, du, dw, dqg, dkd, do_dn, vn, dgl, nc_seq=nc_seq,
        name="dn_intra_bwd", rider=ride2)
    dz, d_conv_w = _dnprep_bwd_a(projp, conv_w, dqn, dkn, dv, bsz=bsz, t_seq=t_seq, tt=tt, name="dn_prep_bwd")
    dcin = _dnprep_bwd_b(dz, conv_w, bsz=bsz, t_seq=t_seq, tt=tt, name="conv_bwd")
    gdqg, gdkd, gdvi, gdgl = _gla_scan_bwd(do_gla, gqg, gkd, ggl, projp, ghist, bsz=bsz, nc_seq=nc_seq,
                                            name="gla_scan_bwd")
    dgqk, dgv, dsb, d_w2p, d_gla_b = _gla_intra_bwd(projp, gates, w2p, gla_b, do_gla, gdqg, gdkd, gdvi, gdgl,
                                                    nc_seq=nc_seq, name="gla_intra_bwd")

    secs = (dcin, ddz, dgqk, dgv, dgr, dsa, dsb)
    g_lo = _grad_tn(h, secs[0:2], tk=tk2, name="w_in_grad_lo")
    g_hi = _grad_tn(h, secs[2:7], tk=tk2, name="w_in_grad_hi")
    ride3 = None
    if where is not None:
        late_sm = [_shards_from_padded(g_lo, g_hi), d_w_out.reshape(N_CHIPS, d // N_CHIPS, d)]
        late_theirs = _exchange_now(_SiblingHalvesRider(late_sm), name="sibling_halves")
        late_pair = [_pair_sum(where, a, b, name=f"pair_sum_{k}") for k, (a, b) in enumerate(zip(late_sm, late_theirs))]
        ride3 = _ChipExchangeRider(late_pair)
    grad_x, d_meta_rows, d_norm1_g, late_parts = _inproj_bwd(secs, wp, h0, norm1_g, dx1, bsz=bsz, t_seq=t_seq, tr=tt,
                                                             name="in_proj_bwd", rider=ride3)

    grads = dict(w_in_lo=g_lo, w_in_hi=g_hi, w_out=d_w_out, w_up_shards=d_w_up_sm, w_down=d_w_down, meta_rows=d_meta_rows,
                 norm1_g=d_norm1_g, conv_w=d_conv_w, a_log_tile=d_alog, dt_bias_tile=d_dtb, dn_norm_g=d_dn_norm_g,
                 gla_w2=d_w2p[0:GLA_RANK], gla_b=d_gla_b, gla_norm_g=d_gla_norm_g, norm2_g=d_norm2_g,
                 final_norm_g=d_final_g, loss_tile=loss_tile)
    if where is not None:
        grads["exchanged"] = (late_pair + mlp_pair, list(late_parts) + list(mlp_parts))
    return grad_x, grads


SHARD_W = IN_WIDTH // N_CHIPS
PADDED_ORDER = ((0, 2048), (2056, 3592), (2048, 2056), LANE - 8, (3592, 3608), LANE - GLA_RANK)


def _pad_layout(w_full):
    pieces = [jnp.zeros((w_full.shape[0], seg), w_full.dtype) if isinstance(seg, int) else w_full[:, seg[0]:seg[1]]
              for seg in PADDED_ORDER]
    return jnp.concatenate(pieces, axis=1)


def _padded_from_shards(stack):
    pieces = []
    for seg in PADDED_ORDER:
        if isinstance(seg, int):
            pieces.append(jnp.zeros((stack.shape[1], seg), stack.dtype))
            continue
        for j in range(N_CHIPS):
            lo, hi = max(seg[0], j * SHARD_W), min(seg[1], (j + 1) * SHARD_W)
            if lo < hi:
                pieces.append(stack[j, :, lo - j * SHARD_W:hi - j * SHARD_W])
    return jnp.concatenate(pieces, axis=1)


def _shards_from_padded(g_lo, g_hi):
    split = g_lo.shape[1]
    starts, pos = [], 0
    for seg in PADDED_ORDER:
        width = seg if isinstance(seg, int) else seg[1] - seg[0]
        if not isinstance(seg, int):
            starts.append((seg[0], seg[1], pos))
        pos += width
    shards = []
    for j in range(N_CHIPS):
        pieces = []
        for a, b, p0 in sorted(starts):
            lo, hi = max(a, j * SHARD_W), min(b, (j + 1) * SHARD_W)
            if lo < hi:
                src, off = (g_lo, 0) if p0 < split else (g_hi, split)
                pieces.append(src[:, p0 + lo - a - off:p0 + hi - a - off])
        pieces.append(jnp.zeros((g_lo.shape[0], D_MODEL - SHARD_W), g_lo.dtype))
        shards.append(jnp.concatenate(pieces, axis=1))
    return jnp.stack(shards)


def _pack_small(g, bsz):
    assert bsz * N_META == 32
    row = jnp.concatenate([g["a_log_tile"], g["dt_bias_tile"], g["dn_norm_g"], g["gla_norm_g"], g["gla_b"],
                           g["loss_tile"], jnp.zeros((1, LANE), F32)], axis=1)
    return jnp.concatenate([g["meta_rows"], g["norm1_g"], g["conv_w"].reshape(6, D_MODEL), row,
                            g["gla_w2"].reshape(4, D_MODEL), g["norm2_g"], g["final_norm_g"],
                            jnp.zeros((2, D_MODEL), F32)], axis=0)


def kernel(x, meta_tokens, norm1_g, w_in, conv_w, a_log, dt_bias, dn_norm_g, gla_w2, gla_b, gla_norm_g, w_out, norm2_g, w_up, w_down, final_norm_g, loss_target, m_meta_tokens, m_norm1_g, m_w_in, m_conv_w, m_a_log, m_dt_bias, m_dn_norm_g, m_gla_w2, m_gla_b, m_gla_norm_g, m_w_out, m_norm2_g, m_w_up, m_w_down, m_final_norm_g, v_meta_tokens, v_norm1_g, v_w_in, v_conv_w, v_a_log, v_dt_bias, v_dn_norm_g, v_gla_w2, v_gla_b, v_gla_norm_g, v_w_out, v_norm2_g, v_w_up, v_w_down, v_final_norm_g):
    bsz = x.shape[0]
    chip = 2 * lax.axis_index("x") + lax.axis_index("y")

    lane_pad = lambda a, wd: jnp.pad(a, ((0, 0), (0, wd - a.shape[1])))
    where = jnp.stack([lax.axis_index("c"), chip]).astype(jnp.int32)
    slot = lambda a, dt, nm: _to_slot(where, a, dt, name="slot_" + nm)
    (g_meta,) = _exchange_now(_GatherRider([slot(meta_tokens, F32, "meta")], [False]), name="gather_meta")
    early = _GatherRider([slot(lane_pad(w_in[0], D_MODEL), BF16, "w_in"), slot(conv_w[0], F32, "conv"),
                          slot(lane_pad(gla_w2[0], LANE), F32, "gla_w2")], [True, False, False])
    late = (_GatherRider([slot(w_up[0], BF16, "w_up")], [True]), _GatherRider([slot(w_down[0], BF16, "w_down")], [True]),
            _GatherRider([slot(w_out[0], BF16, "w_out")], [True]))
    meta_f = g_meta.transpose(1, 0, 2).reshape(N_META, D_MODEL)

    grad_x, g = _local_step(x, loss_target, meta_f, norm1_g, None, None, a_log, dt_bias, dn_norm_g, None, gla_b,
                            gla_norm_g, None, norm2_g, None, None, final_norm_g.reshape(1, D_MODEL), late_gather=late,
                            where=where, early_gather=early)

    pair, parts = g["exchanged"]
    halves = [_chip_sum(where, p, q, name=f"chip_sum_{k}") for k, (p, q) in enumerate(zip(pair, parts))]
    gw_in, gw_out, gw_up, gw_down = _sibling_join(halves)

    red = _small_allreduce(_pack_small(g, bsz))
    g_meta_full = red[0:N_META]
    g_norm1 = red[32:33]
    g_conv_full = red[33:39].reshape(4, QKV_W)
    srow = red[39:40]
    g_alog, g_dtb = srow[:, 4:8], srow[:, LANE + 4:LANE + 8]
    g_dn_norm, g_gla_norm = srow[:, 2 * LANE:3 * LANE], srow[:, 3 * LANE:4 * LANE]
    g_gla_b = srow[:, 4 * LANE:6 * LANE]
    loss = srow[0, 6 * LANE]
    g_w2_full = red[40:44].reshape(GLA_RANK, GQ_W)
    g_norm2 = red[44:45]
    g_final = red[45:46]
    g_meta_sh = lax.dynamic_slice_in_dim(g_meta_full, chip * (D_MODEL // N_CHIPS), D_MODEL // N_CHIPS, axis=1)
    g_conv_sh = lax.dynamic_slice_in_dim(g_conv_full, chip * (QKV_W // N_CHIPS), QKV_W // N_CHIPS, axis=1)
    g_w2_sh = lax.dynamic_slice_in_dim(g_w2_full, chip * (GQ_W // N_CHIPS), GQ_W // N_CHIPS, axis=1)

    names = ["meta_tokens", "norm1_g", "w_in", "conv_w", "a_log", "dt_bias", "dn_norm_g", "gla_w2", "gla_b",
             "gla_norm_g", "w_out", "norm2_g", "w_up", "w_down", "final_norm_g"]
    weights = dict(meta_tokens=meta_tokens, norm1_g=norm1_g, w_in=w_in, conv_w=conv_w, a_log=a_log, dt_bias=dt_bias,
                   dn_norm_g=dn_norm_g, gla_w2=gla_w2, gla_b=gla_b, gla_norm_g=gla_norm_g, w_out=w_out,
                   norm2_g=norm2_g, w_up=w_up, w_down=w_down, final_norm_g=final_norm_g)
    ms = dict(meta_tokens=m_meta_tokens, norm1_g=m_norm1_g, w_in=m_w_in, conv_w=m_conv_w, a_log=m_a_log,
              dt_bias=m_dt_bias, dn_norm_g=m_dn_norm_g, gla_w2=m_gla_w2, gla_b=m_gla_b, gla_norm_g=m_gla_norm_g,
              w_out=m_w_out, norm2_g=m_norm2_g, w_up=m_w_up, w_down=m_w_down, final_norm_g=m_final_norm_g)
    vs = dict(meta_tokens=v_meta_tokens, norm1_g=v_norm1_g, w_in=v_w_in, conv_w=v_conv_w, a_log=v_a_log,
              dt_bias=v_dt_bias, dn_norm_g=v_dn_norm_g, gla_w2=v_gla_w2, gla_b=v_gla_b, gla_norm_g=v_gla_norm_g,
              w_out=v_w_out, norm2_g=v_norm2_g, w_up=v_w_up, w_down=v_w_down, final_norm_g=v_final_norm_g)
    grads2d = dict(meta_tokens=g_meta_sh, norm1_g=g_norm1, w_in=gw_in, conv_w=g_conv_sh, a_log=g_alog, dt_bias=g_dtb,
                   dn_norm_g=g_dn_norm, gla_w2=g_w2_sh, gla_b=g_gla_b, gla_norm_g=g_gla_norm, w_out=gw_out,
                   norm2_g=g_norm2, w_up=gw_up, w_down=gw_down, final_norm_g=g_final)
    out_g, out_d, out_m, out_v = [], [], [], []
    for nm in names:
        shape = weights[nm].shape
        g2 = grads2d[nm]
        if nm == "w_in":
            tview = lambda a: jnp.transpose(a, (2, 0, 1))
            res = _adamw_rows(tview(weights[nm]), g2[:, 0:SHARD_W].T.reshape(SHARD_W, 1, D_MODEL), tview(ms[nm]),
                              tview(vs[nm]), name=f"adamw_{nm}")
            res = [jnp.transpose(a, (1, 2, 0)) for a in res]
            gout = res[3]
        elif len(shape) == 3:
            res = _adamw(weights[nm], g2, ms[nm], vs[nm], name=f"adamw_{nm}")
            gout = g2.reshape(shape)
        else:
            as2d = lambda a: a.reshape(g2.shape)
            res = _adamw(as2d(weights[nm]), g2, as2d(ms[nm]), as2d(vs[nm]), name=f"adamw_{nm}")
            gout = g2.reshape(shape)
        out_g.append(gout)
        out_d.append(res[0].reshape(shape))
        out_m.append(res[1].reshape(shape))
        out_v.append(res[2].reshape(shape))
    return (loss, grad_x, *out_g, *out_d, *out_m, *out_v)
```

```python
import functools

import jax
import jax.numpy as jnp
import numpy as np
from jax import lax
from jax.experimental import pallas as pl
from jax.experimental.pallas import tpu as pltpu

F32 = jnp.float32
BF16 = jnp.bfloat16
HI = lax.Precision.HIGHEST
MESH = pl.DeviceIdType.MESH

D_MODEL = 1024
N_META = 16
CHUNK = 64
N_PAD = CHUNK - N_META
NH = 4
DN_D = 128
GLA_DK = 64
GLA_DV = 128
GLA_RANK = 16
D_FF = 4 * D_MODEL
EPS = 1e-6
F32_TINY = 1.1754944e-38
IN_WIDTH = 3608
C_QKV, C_DZ, C_GQK, C_GV, C_GR, C_SA, C_SB, PW = 0, 1536, 2048, 2560, 3072, 3584, 3712, 3840
LANE = 128
N_CHIPS = 4

ADAM_LR, ADAM_B1, ADAM_B2, ADAM_EPS, ADAM_WD, ADAM_STEP = 0.001, 0.9, 0.999, 1e-08, 0.01, 10

VMEM_BIG = 56 * 1024 * 1024


def _cp(vmem=None, sem=None):
    kw = {}
    if vmem is not None:
        kw["vmem_limit_bytes"] = vmem
    if sem is not None:
        kw["dimension_semantics"] = sem
    return pltpu.CompilerParams(**kw)


def _tile(n, target, mult=16):
    best = None
    for t in range(mult, min(n, target) + 1, mult):
        if n % t == 0:
            best = t
    assert best is not None, (n, target)
    return best


def _dot(a, b, dims, prec=None):
    return lax.dot_general(a, b, (dims, ((), ())), preferred_element_type=F32, precision=prec)


def _nn(a, b):
    return _dot(a.astype(BF16), b.astype(BF16), ((1,), (0,)))


def _nt(a, b):
    return _dot(a.astype(BF16), b.astype(BF16), ((1,), (1,)))


def _tn(a, b):
    return _dot(a.astype(BF16), b.astype(BF16), ((0,), (0,)))


def _split(x):
    hi = x.astype(BF16)
    return hi, (x - hi.astype(F32)).astype(BF16)


def _tri_sum(tri, x):
    t = tri.astype(BF16)
    hi = x.astype(BF16)
    r1 = x - hi.astype(F32)
    mid = r1.astype(BF16)
    lo = (r1 - mid.astype(F32)).astype(BF16)
    nn = ((1,), (0,))
    return _dot(t, hi, nn) + _dot(t, mid, nn) + _dot(t, lo, nn)


def _sigmoid(x):
    return 0.5 * jnp.tanh(0.5 * x) + 0.5


def _softplus(x):
    return jnp.maximum(x, 0.0) + jnp.log(1.0 + jnp.exp(-jnp.abs(x)))


def _logsigmoid(x):
    return -_softplus(-x)


def _iota2(shape, dim):
    return lax.broadcasted_iota(jnp.int32, shape, dim)


def _mm(a, b, mode, *, tm, tn, tk, out_dtypes, extras=(), epilogue=None, name, vmem=VMEM_BIG, rider=None,
        out_widths=None, n_chunk=None, vec_extras=(), shard_cols=None):
    if mode == "tn":
        K, M = a.shape
    else:
        M, K = a.shape
    N = b.shape[0] if mode == "nt" else b.shape[1]
    assert M % tm == 0 and N % tn == 0 and K % tk == 0, (name, M, N, K, tm, tn, tk)
    nk = K // tk
    n_ex, n_out, n_vec = len(extras), len(out_dtypes), len(vec_extras)
    if mode == "tn":
        a_spec = pl.BlockSpec((tk, tm), lambda i, j, k: (k, i))
    else:
        a_spec = pl.BlockSpec((tm, tk), lambda i, j, k: (i, k))
    if mode == "nt":
        b_spec = pl.BlockSpec((tn, tk), lambda i, j, k: (j, k))
    else:
        b_spec = pl.BlockSpec((tk, tn), lambda i, j, k: (k, j))
    mn_spec = pl.BlockSpec((tm, tn), lambda i, j, k: (i, j))
    if out_widths is None:
        o_specs = [mn_spec] * n_out
        o_shapes = [jax.ShapeDtypeStruct((M, N), dt) for dt in out_dtypes]
    else:
        assert tn == N
        o_specs = [pl.BlockSpec((tm, wd), lambda i, j, k: (i, 0)) for wd in out_widths]
        o_shapes = [jax.ShapeDtypeStruct((M, wd), dt) for wd, dt in zip(out_widths, out_dtypes)]
    if shard_cols is not None:
        o_specs = [pl.BlockSpec((tn // shard_cols, tm, shard_cols), lambda i, j, k: (j, i, 0))]
        o_shapes = [jax.ShapeDtypeStruct((N // shard_cols, M, shard_cols), F32)]
    dims = {"nn": ((1,), (0,)), "nt": ((1,), (1,)), "tn": ((0,), (0,))}[mode]

    single = nk == 1
    direct = (not single) and epilogue is None and n_out == 1 and out_dtypes[0] == F32

    def body(*refs):
        a_ref, b_ref = refs[0], refs[1]
        ex_refs = refs[2:2 + n_ex]
        vec_refs = refs[2 + n_ex:2 + n_ex + n_vec]
        out_refs = refs[2 + n_ex + n_vec:2 + n_ex + n_vec + n_out]
        if n_chunk is not None:
            assert single and out_widths is None and mode != "tn" and tn % n_chunk == 0
            av = a_ref[...].astype(BF16)
            for j in range(tn // n_chunk):
                cols = slice(j * n_chunk, (j + 1) * n_chunk)
                bv = b_ref[cols, :] if mode == "nt" else b_ref[:, cols]
                acc = _dot(av, bv.astype(BF16), dims)
                res = (acc,) if epilogue is None else epilogue(acc, *[e[:, cols] for e in ex_refs])
                for o_ref, r in zip(out_refs, res):
                    o_ref[:, cols] = r.astype(o_ref.dtype)
            return
        part = _dot(a_ref[...].astype(BF16), b_ref[...].astype(BF16), dims)

        def finish(acc):
            res = (acc,) if epilogue is None else epilogue(acc, *[e[...] for e in ex_refs], *[v[...] for v in vec_refs])
            for o_ref, r in zip(out_refs, res):
                o_ref[...] = r.astype(o_ref.dtype)

        if single:
            if shard_cols is not None:
                for sh in range(tn // shard_cols):
                    out_refs[0][sh] = part[:, sh * shard_cols:(sh + 1) * shard_cols]
            else:
                finish(part)
            return
        acc_ref = out_refs[0] if direct else refs[2 + n_ex + n_vec + n_out]
        k = pl.program_id(2)
        if shard_cols is not None:
            assert direct
            for sh in range(tn // shard_cols):
                piece = part[:, sh * shard_cols:(sh + 1) * shard_cols]

                @pl.when(k == 0)
                def _():
                    acc_ref[sh] = piece

                @pl.when(k > 0)
                def _():
                    acc_ref[sh] += piece
            return

        @pl.when(k == 0)
        def _():
            acc_ref[...] = part

        @pl.when(k > 0)
        def _():
            acc_ref[...] += part

        if not direct:
            @pl.when(k == nk - 1)
            def _():
                finish(acc_ref[...])

    outs, ridden = _hosted_call(
        body, rider, name=name, grid=(M // tm, N // tn, nk),
        in_specs=[a_spec, b_spec] + [mn_spec] * n_ex + [pl.BlockSpec((1, tn), lambda i, j, k: (0, j))] * n_vec,
        out_specs=o_specs, out_shape=o_shapes,
        scratch_shapes=[] if (single or direct) else [pltpu.VMEM((tm, tn), F32)],
        compiler_params=_cp(vmem, ("parallel", "parallel", "arbitrary")), args=(a, b, *extras, *vec_extras))
    return tuple(outs) if rider is None else (tuple(outs), ridden)


def _grad_tn(a, secs, *, tk, name):
    kk, m = a.shape
    widths = [s.shape[1] for s in secs]
    total = sum(widths)
    nk = kk // tk

    def body(*refs):
        a_ref, sec_refs, o_ref = refs[0], refs[1:-1], refs[-1]
        cat = sec_refs[0][...] if len(sec_refs) == 1 else jnp.concatenate([s[...] for s in sec_refs], axis=1)
        part = _dot(a_ref[...].astype(BF16), cat.astype(BF16), ((0,), (0,)))
        k = pl.program_id(0)

        @pl.when(k == 0)
        def _():
            o_ref[...] = part

        @pl.when(k > 0)
        def _():
            o_ref[...] += part

    return pl.pallas_call(
        body, name=name, grid=(nk,),
        in_specs=[pl.BlockSpec((tk, m), lambda k: (k, 0))] + [pl.BlockSpec((tk, w), lambda k: (k, 0)) for w in widths],
        out_specs=pl.BlockSpec((m, total), lambda k: (0, 0)),
        out_shape=jax.ShapeDtypeStruct((m, total), F32),
        compiler_params=_cp(VMEM_BIG, ("arbitrary",)),
    )(a, *secs)


class _ShiftedRows:
    def __init__(self, src, buf, sems, *, per_seq, tt, steps):
        self.src, self.buf, self.sems = src, buf, sems
        self.per_seq, self.tt, self.steps = per_seq, tt, steps

    def _do(self, step, slot, act):
        b, j = step // self.per_seq, step % self.per_seq
        tt = self.tt

        @pl.when(j == 0)
        def _():
            act(pltpu.make_async_copy(self.src.at[b, pl.ds(0, tt - CHUNK), :],
                                      self.buf.at[slot, pl.ds(CHUNK, tt - CHUNK), :], self.sems.at[slot]))

        if self.per_seq > 1:
            @pl.when(j > 0)
            def _():
                act(pltpu.make_async_copy(self.src.at[b, pl.ds(j * tt - CHUNK, tt), :], self.buf.at[slot],
                                          self.sems.at[slot]))

    def tile(self, i):
        slot = i % 2

        @pl.when(i == 0)
        def _():
            self._do(i, slot, lambda cp: cp.start())

        self._do(i, slot, lambda cp: cp.wait())

        @pl.when(i + 1 < self.steps)
        def _():
            self._do(i + 1, 1 - slot, lambda cp: cp.start())

        return slot


def _embed_norm(x, lead, g, *, tr, name, rider=None):
    bsz, s_len, d = x.shape
    t_seq = s_len + CHUNK
    per_seq = t_seq // tr
    steps = bsz * per_seq
    n = bsz * t_seq

    def body(x_hbm, lead_ref, g_ref, h0_ref, h_ref, buf, sems):
        i = pl.program_id(0)
        slot = _ShiftedRows(x_hbm, buf, sems, per_seq=per_seq, tt=tr, steps=steps).tile(i)

        @pl.when(i % per_seq == 0)
        def _():
            buf[slot, 0:CHUNK, :] = lead_ref[...]

        xv = buf[slot]
        h0_ref[...] = xv
        r = lax.rsqrt(jnp.mean(xv * xv, axis=-1, keepdims=True) + EPS)
        h_ref[...] = (xv * r * g_ref[...]).astype(h_ref.dtype)

    row = pl.BlockSpec((tr, d), lambda i: (i, 0))
    outs, ridden = _hosted_call(
        body, rider, name=name, grid=(steps,),
        in_specs=[ANY, pl.BlockSpec((CHUNK, d), lambda i: (0, 0)), pl.BlockSpec((1, d), lambda i: (0, 0))],
        out_specs=[row, row],
        out_shape=[jax.ShapeDtypeStruct((n, d), F32), jax.ShapeDtypeStruct((n, d), BF16)],
        scratch_shapes=[pltpu.VMEM((2, tr, d), F32), pltpu.SemaphoreType.DMA((2,))],
        compiler_params=_cp(VMEM_BIG, ("arbitrary",)), args=(x, lead, g))
    return (*outs, ridden)


def _rms_fwd(x, g, *, tr, name):
    n, d = x.shape

    def body(x_ref, g_ref, o_ref):
        xv = x_ref[...]
        r = lax.rsqrt(jnp.mean(xv * xv, axis=-1, keepdims=True) + EPS)
        o_ref[...] = (xv * r * g_ref[...]).astype(o_ref.dtype)

    return pl.pallas_call(
        body, name=name, grid=(n // tr,),
        in_specs=[pl.BlockSpec((tr, d), lambda i: (i, 0)), pl.BlockSpec((1, d), lambda i: (0, 0))],
        out_specs=pl.BlockSpec((tr, d), lambda i: (i, 0)),
        out_shape=jax.ShapeDtypeStruct((n, d), BF16),
        compiler_params=_cp(VMEM_BIG),
    )(x, g)


def _rms_bwd_math(xv, g, dy):
    r = lax.rsqrt(jnp.mean(xv * xv, axis=-1, keepdims=True) + EPS)
    xh = xv * r
    gdy = dy * g
    dx = r * (gdy - xh * jnp.mean(xh * gdy, axis=-1, keepdims=True))
    return dx, jnp.sum(dy * xh, axis=0, keepdims=True)


def _mlp_up_bwd_norm(dup, w_up, x, g, res, *, tr, name, rider=None):
    n, d = x.shape
    ff = dup.shape[1]

    def body(dup_ref, w_ref, x_ref, g_ref, res_ref, o_ref, ob_ref, dg_ref):
        dh = _nt(dup_ref[...], w_ref[...])
        dx, dg = _rms_bwd_math(x_ref[...], g_ref[...], dh)
        tot = res_ref[...] + dx
        o_ref[...] = tot
        ob_ref[...] = tot.astype(BF16)

        @pl.when(pl.program_id(0) == 0)
        def _():
            dg_ref[...] = dg

        @pl.when(pl.program_id(0) > 0)
        def _():
            dg_ref[...] += dg

    row = pl.BlockSpec((tr, d), lambda i: (i, 0))
    vec = pl.BlockSpec((1, d), lambda i: (0, 0))
    outs, ridden = _hosted_call(
        body, rider, name=name, grid=(n // tr,),
        in_specs=[pl.BlockSpec((tr, ff), lambda i: (i, 0)), pl.BlockSpec((d, ff), lambda i: (0, 0)), row, vec, row],
        out_specs=[row, row, vec],
        out_shape=[jax.ShapeDtypeStruct((n, d), F32), jax.ShapeDtypeStruct((n, d), BF16),
                   jax.ShapeDtypeStruct((1, d), F32)],
        scratch_shapes=[], compiler_params=_cp(VMEM_BIG, ("arbitrary",)), args=(dup, w_up, x, g, res))
    return (*outs, ridden)


def _mlp_down_loss(act, w_down, x1, gf, tgt, *, t_seq, tr, name):
    n, d = x1.shape
    ff = act.shape[1]
    per_seq = t_seq // tr
    steps = n // tr

    def body(a_ref, w_ref, x_ref, g_ref, t_hbm, dx_ref, dxb_ref, dg_ref, loss_ref, tbuf, tsems):
        i = pl.program_id(0)
        slot = _ShiftedRows(t_hbm, tbuf, tsems, per_seq=per_seq, tt=tr, steps=steps).tile(i)

        @pl.when(i % per_seq == 0)
        def _():
            tbuf[slot, 0:CHUNK, :] = jnp.zeros((CHUNK, d), F32)

        t_ref = tbuf.at[slot]
        xv = x_ref[...] + _nn(a_ref[...], w_ref[...])
        g = g_ref[...]
        r = lax.rsqrt(jnp.mean(xv * xv, axis=-1, keepdims=True) + EPS)
        xh = xv * r
        pos = (i % per_seq) * tr + _iota2((tr, 1), 0)
        real = pos >= CHUNK
        err = jnp.where(real, xh * g - t_ref[...], 0.0)
        dy = err * (1.0 / d)
        gdy = dy * g
        dx = r * (gdy - xh * jnp.mean(xh * gdy, axis=-1, keepdims=True))
        dx_ref[...] = dx
        dxb_ref[...] = dx.astype(BF16)
        dg = jnp.sum(dy * xh, axis=0, keepdims=True)
        ls = 0.5 * jnp.sum(jnp.mean(err * err, axis=-1, keepdims=True), axis=0, keepdims=True)
        ls = jnp.where(_iota2((1, LANE), 1) == 0, ls, 0.0)

        @pl.when(i == 0)
        def _():
            dg_ref[...] = dg
            loss_ref[...] = ls

        @pl.when(i > 0)
        def _():
            dg_ref[...] += dg
            loss_ref[...] += ls

    row = pl.BlockSpec((tr, d), lambda i: (i, 0))
    vec = pl.BlockSpec((1, d), lambda i: (0, 0))
    one = pl.BlockSpec((1, LANE), lambda i: (0, 0))
    return pl.pallas_call(
        body, name=name, grid=(n // tr,),
        in_specs=[pl.BlockSpec((tr, ff), lambda i: (i, 0)), pl.BlockSpec((ff, d), lambda i: (0, 0)), row, vec, ANY],
        out_specs=[row, row, vec, one],
        out_shape=[jax.ShapeDtypeStruct((n, d), F32), jax.ShapeDtypeStruct((n, d), BF16),
                   jax.ShapeDtypeStruct((1, d), F32), jax.ShapeDtypeStruct((1, LANE), F32)],
        scratch_shapes=[pltpu.VMEM((2, tr, d), F32), pltpu.SemaphoreType.DMA((2,))],
        compiler_params=_cp(VMEM_BIG, ("arbitrary",)),
    )(act, w_down, x1, gf, tgt)


def _gnorm_fwd(o_dn, o_gla, projp, g_dn, g_gla, *, tr, name):
    n = o_dn.shape[0]
    w = NH * DN_D

    def body(odn_ref, ogl_ref, z_ref, r_ref, gdn_ref, ggl_ref, mix_ref):
        for grp, (o_ref, gate_ref, gain_ref) in enumerate(((odn_ref, z_ref, gdn_ref), (ogl_ref, r_ref, ggl_ref))):
            gain = gain_ref[...]
            for h in range(NH):
                sl = slice(h * DN_D, (h + 1) * DN_D)
                o = o_ref[:, sl].astype(F32)
                z = gate_ref[:, sl].astype(F32)
                r = lax.rsqrt(jnp.mean(o * o, axis=-1, keepdims=True) + EPS)
                y = (o * r * gain) * (z * _sigmoid(z))
                mix_ref[:, grp * w + h * DN_D: grp * w + (h + 1) * DN_D] = y.astype(mix_ref.dtype)

    row = pl.BlockSpec((tr, w), lambda i: (i, 0))
    vec = pl.BlockSpec((1, DN_D), lambda i: (0, 0))
    return pl.pallas_call(
        body, name=name, grid=(n // tr,),
        in_specs=[row, row, pl.BlockSpec((tr, w), lambda i: (i, C_DZ // w)),
                  pl.BlockSpec((tr, w), lambda i: (i, C_GR // w)), vec, vec],
        out_specs=pl.BlockSpec((tr, 2 * w), lambda i: (i, 0)),
        out_shape=jax.ShapeDtypeStruct((n, 2 * w), BF16),
        compiler_params=_cp(VMEM_BIG),
    )(o_dn, o_gla, projp, projp, g_dn, g_gla)


def _gnorm_bwd(dmix, o_dn, o_gla, projp, g_dn, g_gla, *, tr, name):
    n = o_dn.shape[0]
    w = NH * DN_D

    def body(dm_ref, odn_ref, ogl_ref, z_ref, r_ref, gdn_ref, ggl_ref,
             dodn_ref, ddz_ref, dogl_ref, dgr_ref, dgdn_ref, dggl_ref):
        first = pl.program_id(0) == 0
        groups = ((odn_ref, z_ref, gdn_ref, dodn_ref, ddz_ref, dgdn_ref),
                  (ogl_ref, r_ref, ggl_ref, dogl_ref, dgr_ref, dggl_ref))
        for grp, (o_ref, gate_ref, gain_ref, do_ref, dgate_ref, dgain_ref) in enumerate(groups):
            gain = gain_ref[...]
            dgain = jnp.zeros((1, DN_D), F32)
            for h in range(NH):
                sl = slice(h * DN_D, (h + 1) * DN_D)
                o = o_ref[:, sl].astype(F32)
                z = gate_ref[:, sl].astype(F32)
                dm = dm_ref[:, grp * w + h * DN_D: grp * w + (h + 1) * DN_D].astype(F32)
                r = lax.rsqrt(jnp.mean(o * o, axis=-1, keepdims=True) + EPS)
                oh = o * r
                s = _sigmoid(z)
                dn = dm * (z * s)
                dgate_ref[:, sl] = (dm * (oh * gain) * (s * (1.0 + z * (1.0 - s)))).astype(dgate_ref.dtype)
                gdn = dn * gain
                do_ref[:, sl] = (r * (gdn - oh * jnp.mean(oh * gdn, axis=-1, keepdims=True))).astype(do_ref.dtype)
                dgain = dgain + jnp.sum(dn * oh, axis=0, keepdims=True)

            @pl.when(first)
            def _():
                dgain_ref[...] = dgain

            @pl.when(jnp.logical_not(first))
            def _():
                dgain_ref[...] += dgain

    row = pl.BlockSpec((tr, w), lambda i: (i, 0))
    vec = pl.BlockSpec((1, DN_D), lambda i: (0, 0))
    big = jax.ShapeDtypeStruct((n, w), F32)
    gate = jax.ShapeDtypeStruct((n, w), BF16)
    small = jax.ShapeDtypeStruct((1, DN_D), F32)
    return pl.pallas_call(
        body, name=name, grid=(n // tr,),
        in_specs=[pl.BlockSpec((tr, 2 * w), lambda i: (i, 0)), row, row,
                  pl.BlockSpec((tr, w), lambda i: (i, C_DZ // w)), pl.BlockSpec((tr, w), lambda i: (i, C_GR // w)), vec, vec],
        out_specs=[row, row, row, row, vec, vec],
        out_shape=[gate, gate, gate, gate, small, small],
        compiler_params=_cp(VMEM_BIG),
    )(dmix, o_dn, o_gla, projp, projp, g_dn, g_gla)


QKV_W = 3 * NH * DN_D
HALO = 8


RING_SLOTS = 3
TAP_ROWS = 32


def _aligned_taps(src_ref, tap_ref, slot, starts, nr, cols):
    taps = []
    for j, start in enumerate(starts):
        if start % HALO == 0:
            taps.append(src_ref.at[pl.ds(start, nr), cols])
        else:
            tap_ref[slot, j, 0:nr, :] = src_ref[pl.ds(start, nr), cols]
            taps.append(tap_ref.at[slot, j, 0:nr, :])
    return taps


def _conv_taps(cw_ref, taps, cols):
    z = cw_ref[0:1, cols] * taps[0][...]
    for j in range(1, 4):
        z = z + cw_ref[j:j + 1, cols] * taps[j][...]
    return z


def _dnprep_fwd(projp, conv_w, *, bsz, t_seq, tt, name, rider=None):
    n = bsz * t_seq
    per_seq = t_seq // tt
    hw = NH * DN_D
    row_blk = _tile(tt, TAP_ROWS)

    def body(x_ref, halo_ref, cw_ref, q_ref, k_ref, v_ref, xs_ref, tap_ref):
        i = pl.program_id(1)
        xs_ref[0:HALO, :] = jnp.where(i == 0, 0.0, halo_ref[...].astype(F32))
        xs_ref[HALO:HALO + tt, :] = x_ref[...].astype(F32)
        for c, o_ref in enumerate([q_ref] * NH + [k_ref] * NH + [v_ref] * NH):
            cols = slice(c * DN_D, (c + 1) * DN_D)
            hcols = slice((c % NH) * DN_D, (c % NH + 1) * DN_D)
            for r0 in range(0, tt, row_blk):
                taps = _aligned_taps(xs_ref, tap_ref, (r0 // row_blk) % 2,
                                     [r0 + HALO - 3 + j for j in range(4)], row_blk, cols)
                z = _conv_taps(cw_ref, taps, cols)
                a = z * _sigmoid(z)
                if o_ref is not v_ref:
                    a = a * lax.rsqrt(jnp.sum(a * a, axis=-1, keepdims=True) + EPS)
                o_ref[r0:r0 + row_blk, hcols] = a.astype(o_ref.dtype)

    def halo_map(b, i):
        return (jnp.maximum((b * t_seq + i * tt) // HALO - 1, 0), 0)

    out = pl.BlockSpec((tt, hw), lambda b, i: (b * per_seq + i, 0))
    sds = jax.ShapeDtypeStruct((n, hw), BF16)
    outs, ridden = _hosted_call(
        body, rider, name=name, grid=(bsz, per_seq),
        in_specs=[pl.BlockSpec((tt, QKV_W), lambda b, i: (b * per_seq + i, 0)),
                  pl.BlockSpec((HALO, QKV_W), halo_map),
                  pl.BlockSpec((4, QKV_W), lambda b, i: (0, 0))],
        out_specs=[out, out, out], out_shape=[sds, sds, sds],
        scratch_shapes=[pltpu.VMEM((tt + HALO, QKV_W), F32), pltpu.VMEM((2, 4, row_blk, DN_D), F32)],
        compiler_params=_cp(VMEM_BIG), args=(projp, projp, conv_w))
    return (*outs, ridden)


def _dnprep_bwd_a(projp, conv_w, dq, dk, dv, *, bsz, t_seq, tt, name):
    n = bsz * t_seq
    per_seq = t_seq // tt
    hw = NH * DN_D

    row_blk = _tile(tt, 4 * TAP_ROWS, mult=HALO)

    def body(x_ref, halo_ref, cw_ref, dq_ref, dk_ref, dv_ref, dz_ref, dcw_ref, xs_ref, part_ref, tap_ref):
        b, i = pl.program_id(0), pl.program_id(1)
        xs_ref[0:HALO, :] = jnp.where(i == 0, 0.0, halo_ref[...].astype(F32))
        xs_ref[HALO:HALO + tt, :] = x_ref[...].astype(F32)
        for c, d_ref in enumerate([dq_ref] * NH + [dk_ref] * NH + [dv_ref] * NH):
            cols = slice(c * DN_D, (c + 1) * DN_D)
            hcols = slice((c % NH) * DN_D, (c % NH + 1) * DN_D)
            parts = [None] * 4
            for r0 in range(0, tt, row_blk):
                taps = _aligned_taps(xs_ref, tap_ref, (r0 // row_blk) % 2,
                                     [r0 + HALO - 3 + j for j in range(4)], row_blk, cols)
                z = _conv_taps(cw_ref, taps, cols)
                s = _sigmoid(z)
                dsilu = s * (1.0 + z * (1.0 - s))
                dy = d_ref[r0:r0 + row_blk, hcols]
                if d_ref is dv_ref:
                    dz = dy * dsilu
                else:
                    a = z * s
                    rs = lax.rsqrt(jnp.sum(a * a, axis=-1, keepdims=True) + EPS)
                    y = a * rs
                    dz = (rs * (dy - y * jnp.sum(dy * y, axis=-1, keepdims=True))) * dsilu
                dz_ref[r0:r0 + row_blk, cols] = dz
                for j in range(4):
                    p = jnp.sum(dz * taps[j][...], axis=0, keepdims=True)
                    parts[j] = p if parts[j] is None else parts[j] + p
            for j in range(4):
                part_ref[j:j + 1, cols] = parts[j]

        first = jnp.logical_and(b == 0, i == 0)

        @pl.when(first)
        def _():
            dcw_ref[...] = part_ref[0:4, :]

        @pl.when(jnp.logical_not(first))
        def _():
            dcw_ref[...] += part_ref[0:4, :]

    def halo_map(b, i):
        return (jnp.maximum((b * t_seq + i * tt) // HALO - 1, 0), 0)

    hrow = pl.BlockSpec((tt, hw), lambda b, i: (b * per_seq + i, 0))
    return pl.pallas_call(
        body, name=name, grid=(bsz, per_seq),
        in_specs=[pl.BlockSpec((tt, QKV_W), lambda b, i: (b * per_seq + i, 0)),
                  pl.BlockSpec((HALO, QKV_W), halo_map),
                  pl.BlockSpec((4, QKV_W), lambda b, i: (0, 0)), hrow, hrow, hrow],
        out_specs=[pl.BlockSpec((tt, QKV_W), lambda b, i: (b * per_seq + i, 0)),
                   pl.BlockSpec((4, QKV_W), lambda b, i: (0, 0))],
        out_shape=[jax.ShapeDtypeStruct((n, QKV_W), F32), jax.ShapeDtypeStruct((4, QKV_W), F32)],
        scratch_shapes=[pltpu.VMEM((tt + HALO, QKV_W), F32), pltpu.VMEM((HALO, QKV_W), F32),
                        pltpu.VMEM((2, 4, row_blk, DN_D), F32)],
        compiler_params=_cp(VMEM_BIG),
    )(projp, projp, conv_w, dq, dk, dv)


def _dnprep_bwd_b(dz, conv_w, *, bsz, t_seq, tt, name):
    n = bsz * t_seq
    per_seq = t_seq // tt
    last_blk = n // HALO - 1

    main = tt - HALO
    row_blk = _tile(tt, TAP_ROWS)
    steps = bsz * per_seq

    def body(dz_hbm, halo_ref, cw_ref, dx_ref, tail_ref, tap_ref, ring_ref, ring_sems):
        i = pl.program_id(1)
        step = pl.program_id(0) * per_seq + i

        def fetch(k):
            return pltpu.make_async_copy(dz_hbm.at[pl.ds(k * tt, tt), :], ring_ref.at[k % RING_SLOTS],
                                         ring_sems.at[k % RING_SLOTS])

        @pl.when(step == 0)
        def _():
            for k in range(min(RING_SLOTS - 1, steps)):
                fetch(k).start()

        @pl.when(step + RING_SLOTS - 1 < steps)
        def _():
            fetch(step + RING_SLOTS - 1).start()

        fetch(step).wait()
        dz_ref = ring_ref.at[step % RING_SLOTS]
        for cb in range(QKV_W // DN_D):
            cols = slice(cb * DN_D, (cb + 1) * DN_D)
            for r0 in range(0, main, row_blk):
                nr = min(row_blk, main - r0)
                taps = _aligned_taps(dz_ref, tap_ref, (r0 // row_blk) % 2, [r0 + 3 - j for j in range(4)], nr, cols)
                dx_ref[r0:r0 + nr, cols] = _conv_taps(cw_ref, taps, cols).astype(dx_ref.dtype)
        tail_ref[0:HALO, :] = dz_ref[main:tt, :]
        tail_ref[HALO:2 * HALO, :] = jnp.where(i == per_seq - 1, 0.0, halo_ref[...])
        dx = cw_ref[0:1, :] * tail_ref[pl.ds(3, HALO), :]
        for j in range(1, 4):
            dx = dx + cw_ref[j:j + 1, :] * tail_ref[pl.ds(3 - j, HALO), :]
        dx_ref[main:tt, :] = dx.astype(dx_ref.dtype)

    def halo_map(b, i):
        return (jnp.minimum((b * t_seq + (i + 1) * tt) // HALO, last_blk), 0)

    row = pl.BlockSpec((tt, QKV_W), lambda b, i: (b * per_seq + i, 0))
    return pl.pallas_call(
        body, name=name, grid=(bsz, per_seq),
        in_specs=[ANY, pl.BlockSpec((HALO, QKV_W), halo_map), pl.BlockSpec((4, QKV_W), lambda b, i: (0, 0))],
        out_specs=row, out_shape=jax.ShapeDtypeStruct((n, QKV_W), BF16),
        scratch_shapes=[pltpu.VMEM((2 * HALO, QKV_W), F32), pltpu.VMEM((2, 4, row_blk, DN_D), F32),
                        pltpu.VMEM((RING_SLOTS, tt, QKV_W), F32), pltpu.SemaphoreType.DMA((RING_SLOTS,))],
        compiler_params=_cp(VMEM_BIG, ("arbitrary", "arbitrary")),
    )(dz, dz, conv_w)


def _masks64():
    r = _iota2((CHUNK, CHUNK), 0)
    c = _iota2((CHUNK, CHUNK), 1)
    return r, c


def _group(nc_seq, target=5):
    return max(g for g in range(1, target + 1) if nc_seq % g == 0)


def _round_robin(chains):
    live = list(chains)
    while live:
        nxt = []
        for ch in live:
            try:
                next(ch)
                nxt.append(ch)
            except StopIteration:
                pass
        live = nxt
        yield


def _run(chains):
    for _ in _round_robin(chains):
        pass


def _per_chunk(inner, kinds, grp):
    def body(*refs):
        chains = []
        for gi in range(grp):
            views = []
            for r, kind in zip(refs, kinds):
                if kind == "row":
                    views.append(r.at[pl.ds(gi * CHUNK, CHUNK)])
                elif kind == "lead":
                    views.append(r.at[pl.ds(gi, 1)])
                else:
                    views.append(r)
            chains.append(inner(gi, *views))
        _run(chains)
    return body


def _accumulate(ref, val, gi):
    if gi > 0:
        ref[...] += val
        return
    first = pl.program_id(0) == 0

    @pl.when(first)
    def _():
        ref[...] = val

    @pl.when(jnp.logical_not(first))
    def _():
        ref[...] += val


ANY = pl.BlockSpec(memory_space=pl.ANY)


def _place():
    return lax.axis_index("x"), lax.axis_index("y"), lax.axis_index("c")


def _other_chips(x, y):
    return [(1 - x, y, 2 * (1 - x) + y), (x, 1 - y, 2 * x + 1 - y), (1 - x, 1 - y, 2 * (1 - x) + 1 - y)]


class _GatherRider:
    def __init__(self, bufs, split):
        self.inputs = list(bufs)
        self.split = list(split)
        self.out_shapes = [jax.ShapeDtypeStruct(b.shape, b.dtype) for b in bufs]
        self.aliases = {i: i for i in range(len(bufs))}
        self.sems = [pltpu.SemaphoreType.DMA((len(bufs), 3))] * 4

    def _rows(self, k, buf, c, mine=True):
        r = buf.shape[1]
        if not self.split[k]:
            return pl.ds(0, r)
        return pl.ds((c if mine else 1 - c) * (r // 2), r // 2)

    def _ici(self, k, d, bufs, sems, c, px, py, block):
        rows = self._rows(k, bufs[k], c)
        return pltpu.make_async_remote_copy(
            src_ref=bufs[k].at[block, rows, :], dst_ref=bufs[k].at[block, rows, :], send_sem=sems[0].at[k, d],
            recv_sem=sems[1].at[k, d], device_id=(px, py, c), device_id_type=MESH)

    def _pass(self, k, d, bufs, sems, x, y, c, block, mine):
        rows = self._rows(k, bufs[k], c, mine)
        return pltpu.make_async_remote_copy(
            src_ref=bufs[k].at[block, rows, :], dst_ref=bufs[k].at[block, rows, :], send_sem=sems[2].at[k, d],
            recv_sem=sems[3].at[k, d], device_id=(x, y, 1 - c), device_id_type=MESH)

    def first(self, in_refs, bufs, sems):
        x, y, c = _place()
        for k in range(len(bufs)):
            for d, (px, py, _) in enumerate(_other_chips(x, y)):
                self._ici(k, d, bufs, sems, c, px, py, 2 * x + y).start()

    def last(self, in_refs, bufs, sems):
        x, y, c = _place()
        chips = _other_chips(x, y)
        for k in range(len(bufs)):
            for d, (px, py, pj) in enumerate(chips):
                self._ici(k, d, bufs, sems, c, px, py, pj).wait_recv()
                if self.split[k]:
                    self._pass(k, d, bufs, sems, x, y, c, pj, True).start()
        for k in range(len(bufs)):
            for d, (px, py, pj) in enumerate(chips):
                if self.split[k]:
                    self._pass(k, d, bufs, sems, x, y, c, pj, False).wait_recv()
                    self._pass(k, d, bufs, sems, x, y, c, pj, True).wait_send()
                self._ici(k, d, bufs, sems, c, px, py, 2 * x + y).wait_send()


def _hosted_call(body, rider, *, name, grid, in_specs, out_specs, out_shape, scratch_shapes, compiler_params, args):
    if rider is None:
        outs = pl.pallas_call(body, name=name, grid=grid, in_specs=in_specs, out_specs=out_specs, out_shape=out_shape,
                              scratch_shapes=scratch_shapes, compiler_params=compiler_params)(*args)
        return list(outs), []
    n_in, n_out, n_scr = len(in_specs), len(out_specs), len(scratch_shapes)
    r_in, r_out = len(rider.inputs), len(rider.out_shapes)
    compiler_params = _cp(compiler_params.vmem_limit_bytes, ("arbitrary",) * len(grid))

    def full_body(*refs):
        ins = refs[:n_in]
        rins = refs[n_in:n_in + r_in]
        outs = refs[n_in + r_in:n_in + r_in + n_out]
        routs = refs[n_in + r_in + n_out:n_in + r_in + n_out + r_out]
        rest = refs[n_in + r_in + n_out + r_out:]
        scr, sems = rest[:n_scr], rest[n_scr:]
        ids = [pl.program_id(a) for a in range(len(grid))]
        is_first = functools.reduce(jnp.logical_and, [i == 0 for i in ids])
        is_last = functools.reduce(jnp.logical_and, [i == g - 1 for i, g in zip(ids, grid)])

        @pl.when(is_first)
        def _():
            rider.first(rins, routs, sems)

        body(*ins, *outs, *scr)

        @pl.when(is_last)
        def _():
            rider.last(rins, routs, sems)

    res = pl.pallas_call(
        full_body, name=name, grid=grid, in_specs=list(in_specs) + [ANY] * r_in,
        out_specs=list(out_specs) + [ANY] * r_out, out_shape=list(out_shape) + list(rider.out_shapes),
        input_output_aliases={n_in + i: n_out + o for i, o in rider.aliases.items()},
        scratch_shapes=list(scratch_shapes) + list(rider.sems), compiler_params=compiler_params,
    )(*args, *rider.inputs)
    return list(res[:n_out]), list(res[n_out:])


def _exchange_now(rider, *, name):
    r_in = len(rider.inputs)

    def body(*refs):
        rins = refs[:r_in]
        routs = refs[r_in:r_in + len(rider.out_shapes)]
        sems = refs[r_in + len(rider.out_shapes):]
        rider.first(rins, routs, sems)
        rider.last(rins, routs, sems)

    return pl.pallas_call(
        body, name=name, in_specs=[ANY] * r_in, out_specs=[ANY] * len(rider.out_shapes), out_shape=list(rider.out_shapes),
        input_output_aliases=dict(rider.aliases), scratch_shapes=list(rider.sems),
    )(*rider.inputs)


def _to_slot(where, a, dtype, *, name):
    r, cols = a.shape
    tr = _tile(r, 256, 16) if r > 256 else r

    def body(w_ref, a_ref, o_ref):
        o_ref[0] = a_ref[...].astype(o_ref.dtype)

    return pl.pallas_call(
        body, name=name,
        grid_spec=pltpu.PrefetchScalarGridSpec(
            num_scalar_prefetch=1, grid=(r // tr,),
            in_specs=[pl.BlockSpec((tr, cols), lambda i, w: (i, 0))],
            out_specs=pl.BlockSpec((1, tr, cols), lambda i, w: (w[1], i, 0))),
        out_shape=jax.ShapeDtypeStruct((N_CHIPS, r, cols), dtype), compiler_params=_cp(VMEM_BIG),
    )(where, a)


def _tri_inv(a_strict):
    r, c = _masks64()
    eye = (r == c).astype(F32)
    blk16 = (r // 16) == (c // 16)
    blk32 = (r // 32) == (c // 32)
    ld = jnp.where(blk16, a_strict, 0.0)
    x = eye - ld
    p = _nn(ld, ld)
    yield
    for step in range(3):
        xp = _nn(x, p)
        if step < 2:
            p = _nn(p, p)
        x = x + xp
        yield
    for lk in (jnp.where(jnp.logical_and(blk32, jnp.logical_not(blk16)), a_strict, 0.0),
               jnp.where(blk32, 0.0, a_strict)):
        y = x - eye
        s = lk + _nn(y, lk)
        yield
        x = x - s - _nn(s, y)
        yield
    return x


def _dn_gates(sa, alog, dtb, chunk_in_seq):
    rows = _iota2((CHUNK, LANE), 0)
    valid = jnp.logical_or(rows >= N_PAD, chunk_in_seq > 0)
    beta_t = _sigmoid(sa)
    ea = jnp.exp(alog)
    g_t = jnp.where(valid, -ea * _softplus(sa + dtb), 0.0)
    r, c = _masks64()
    ltri = (r >= c).astype(F32)
    gam_t = _tri_sum(ltri, g_t)
    return beta_t, g_t, gam_t, valid, ea


def _dn_intra_fwd(qn, kn, v, projp, alog_row, dtb_row, *, nc_seq, name, rider=None):
    n = qn.shape[0]
    nct = n // CHUNK
    hw = NH * DN_D
    scale = DN_D ** -0.5

    grp = _group(nc_seq)

    def inner(gi, q_ref, k_ref, v_ref, sa_ref, al_ref, dt_ref, u_ref, w_ref, qg_ref, kd_ref, p_ref, t_ref, gl_ref):
        ci = (pl.program_id(0) * grp + gi) % nc_seq
        beta_t, _, gam_t, _, _ = _dn_gates(sa_ref[...], al_ref[...], dt_ref[...], ci)
        yield
        gam_tt = gam_t.T
        r, c = _masks64()
        incl = r >= c
        strict = r > c

        def head(h):
            sl = slice(h * DN_D, (h + 1) * DN_D)
            beta_w = jnp.broadcast_to(beta_t[:, h:h + 1], (CHUNK, DN_D))
            gam_w = jnp.broadcast_to(gam_t[:, 4 + h:5 + h], (CHUNK, DN_D))
            gam_row = gam_tt[4 + h:5 + h, :]
            gl = gam_t[CHUNK - 1:CHUNK, 4 + h:5 + h]
            dec = jnp.exp(jnp.where(incl, gam_w[:, 0:CHUNK] - gam_row, -jnp.inf))
            kh = k_ref[:, sl].astype(F32)
            qh = q_ref[:, sl].astype(F32) * scale
            vh = v_ref[:, sl].astype(F32)
            kk = _nt(kh, kh)
            qk = _nt(qh, kh)
            yield
            a = jnp.where(strict, beta_w[:, 0:CHUNK] * kk * dec, 0.0)
            tm = yield from _tri_inv(a)
            egam_w = jnp.exp(gam_w)
            u_ref[:, sl] = _nn(tm, beta_w * vh).astype(u_ref.dtype)
            w_ref[:, sl] = _nn(tm, (beta_w * egam_w) * kh).astype(w_ref.dtype)
            qg_ref[:, sl] = (egam_w * qh).astype(qg_ref.dtype)
            kd_ref[:, sl] = (jnp.exp(gl - gam_w) * kh).astype(kd_ref.dtype)
            p_ref[0, h] = qk * dec
            t_ref[0, h] = tm
            gl_ref[0, h:h + 1, :] = jnp.broadcast_to(jnp.exp(gl), (1, LANE))

        yield from _round_robin([head(h) for h in range(NH)])

    rows = grp * CHUNK
    row = pl.BlockSpec((rows, hw), lambda i: (i, 0))
    vec = pl.BlockSpec((1, LANE), lambda i: (0, 0))
    mat = pl.BlockSpec((grp, NH, CHUNK, CHUNK), lambda i: (i, 0, 0, 0))
    big = jax.ShapeDtypeStruct((n, hw), BF16)
    msd = jax.ShapeDtypeStruct((nct, NH, CHUNK, CHUNK), F32)
    kinds = ["row"] * 4 + ["whole"] * 2 + ["row"] * 4 + ["lead"] * 3
    outs, ridden = _hosted_call(
        _per_chunk(inner, kinds, grp), rider, name=name, grid=(nct // grp,),
        in_specs=[row, row, row, pl.BlockSpec((rows, LANE), lambda i: (i, 0)), vec, vec],
        out_specs=[row, row, row, row, mat, mat, pl.BlockSpec((grp, NH, LANE), lambda i: (i, 0, 0))],
        out_shape=[big, big, big, big, msd, msd, jax.ShapeDtypeStruct((nct, NH, LANE), F32)],
        scratch_shapes=[], compiler_params=_cp(VMEM_BIG, ("arbitrary",)),
        args=(qn, kn, v, projp, alog_row, dtb_row))
    return (*outs, ridden)


def _dn_scan_fwd(u, w, qg, kd, p, gl, *, bsz, nc_seq, name, rider=None):
    hw = NH * DN_D
    t_seq = nc_seq * CHUNK
    u, w, qg, kd = (z.reshape(bsz, t_seq, hw) for z in (u, w, qg, kd))
    p = p.reshape(bsz, nc_seq, NH, CHUNK, CHUNK)
    gl = gl.reshape(bsz, nc_seq, NH, LANE)
    grp = _group(nc_seq)

    def body(u_ref, w_ref, qg_ref, kd_ref, p_ref, gl_ref, o_ref, vn_ref, hist_ref, s_ref):
        @pl.when(pl.program_id(0) == 0)
        def _():
            s_ref[...] = jnp.zeros_like(s_ref)

        def chain(b, h, gi):
            sl = slice(h * DN_D, (h + 1) * DN_D)
            rows = pl.ds(gi * CHUNK, CHUNK)
            s = s_ref[b, h]
            hist_ref[b, gi, h] = s.astype(hist_ref.dtype)
            ws = _nn(w_ref[b, rows, sl], s)
            qs = _nn(qg_ref[b, rows, sl], s)
            yield
            vn = u_ref[b, rows, sl] - ws
            vn_ref[b, rows, sl] = vn.astype(vn_ref.dtype)
            o_ref[b, rows, sl] = (qs + _nn(p_ref[b, gi, h], vn)).astype(o_ref.dtype)
            s_ref[b, h] = gl_ref[b, gi, h:h + 1, :] * s + _tn(kd_ref[b, rows, sl], vn)

        for gi in range(grp):
            _run([chain(b, h, gi) for b in range(bsz) for h in range(NH)])

    row = pl.BlockSpec((bsz, grp * CHUNK, hw), lambda i: (0, i, 0))
    outs, ridden = _hosted_call(
        body, rider, name=name, grid=(nc_seq // grp,),
        in_specs=[row, row, row, row, pl.BlockSpec((bsz, grp, NH, CHUNK, CHUNK), lambda i: (0, i, 0, 0, 0)),
                  pl.BlockSpec((bsz, grp, NH, LANE), lambda i: (0, i, 0, 0))],
        out_specs=[row, row, pl.BlockSpec((bsz, grp, NH, DN_D, DN_D), lambda i: (0, i, 0, 0, 0))],
        out_shape=[jax.ShapeDtypeStruct((bsz, t_seq, hw), BF16), jax.ShapeDtypeStruct((bsz, t_seq, hw), BF16),
                   jax.ShapeDtypeStruct((bsz, nc_seq, NH, DN_D, DN_D), F32)],
        scratch_shapes=[pltpu.VMEM((bsz, NH, DN_D, DN_D), F32)],
        compiler_params=_cp(VMEM_BIG, ("arbitrary",)), args=(u, w, qg, kd, p, gl))
    o, vn, hist = outs
    return o.reshape(bsz * t_seq, hw), vn.reshape(bsz * t_seq, hw), hist, ridden


def _dn_scan_bwd(do, w, qg, kd, vn, p, gl, hist, *, bsz, nc_seq, name):
    hw = NH * DN_D
    t_seq = nc_seq * CHUNK
    do, w, qg, kd, vn = (z.reshape(bsz, t_seq, hw) for z in (do, w, qg, kd, vn))
    p = p.reshape(bsz, nc_seq, NH, CHUNK, CHUNK)
    gl = gl.reshape(bsz, nc_seq, NH, LANE)
    grp = _group(nc_seq)

    def body(do_ref, w_ref, qg_ref, kd_ref, vn_ref, p_ref, gl_ref, hist_ref,
             du_ref, dw_ref, dqg_ref, dkd_ref, dgl_ref, ds_ref):
        @pl.when(pl.program_id(0) == 0)
        def _():
            ds_ref[...] = jnp.zeros_like(ds_ref)

        def chain(b, h, gi):
            sl = slice(h * DN_D, (h + 1) * DN_D)
            rows = pl.ds(gi * CHUNK, CHUNK)
            s = hist_ref[b, gi, h]
            dsn = ds_ref[b, h]
            doh = do_ref[b, rows, sl]
            vnh = vn_ref[b, rows, sl]
            kdh = kd_ref[b, rows, sl]
            dvn = _tn(p_ref[b, gi, h], doh) + _nn(kdh, dsn)
            du_ref[b, rows, sl] = dvn.astype(du_ref.dtype)
            dqg_ref[b, rows, sl] = _nt(doh, s).astype(dqg_ref.dtype)
            dkd_ref[b, rows, sl] = _nt(vnh, dsn).astype(dkd_ref.dtype)
            ds_part = _tn(qg_ref[b, rows, sl], doh) + gl_ref[b, gi, h:h + 1, :] * dsn
            dgl = jnp.sum(jnp.sum(dsn * s, axis=0, keepdims=True), axis=1, keepdims=True)
            dgl_ref[b, gi, h:h + 1, :] = jnp.broadcast_to(dgl, (1, LANE))
            yield
            dw_ref[b, rows, sl] = (-_nt(dvn, s)).astype(dw_ref.dtype)
            ds_ref[b, h] = ds_part - _tn(w_ref[b, rows, sl], dvn)

        for gi in reversed(range(grp)):
            _run([chain(b, h, gi) for b in range(bsz) for h in range(NH)])

    steps = nc_seq // grp
    rev = lambda i: steps - 1 - i
    row = pl.BlockSpec((bsz, grp * CHUNK, hw), lambda i: (0, rev(i), 0))
    mat = pl.BlockSpec((bsz, grp, NH, CHUNK, CHUNK), lambda i: (0, rev(i), 0, 0, 0))
    glb = pl.BlockSpec((bsz, grp, NH, LANE), lambda i: (0, rev(i), 0, 0))
    big = jax.ShapeDtypeStruct((bsz, t_seq, hw), BF16)
    outs = pl.pallas_call(
        body, name=name, grid=(steps,),
        in_specs=[row, row, row, row, row, mat, glb,
                  pl.BlockSpec((bsz, grp, NH, DN_D, DN_D), lambda i: (0, rev(i), 0, 0, 0))],
        out_specs=[row, row, row, row, glb],
        out_shape=[big, big, jax.ShapeDtypeStruct(big.shape, F32), jax.ShapeDtypeStruct(big.shape, F32),
                   jax.ShapeDtypeStruct((bsz, nc_seq, NH, LANE), F32)],
        scratch_shapes=[pltpu.VMEM((bsz, NH, DN_D, DN_D), F32)],
        compiler_params=_cp(VMEM_BIG, ("arbitrary",)),
    )(do, w, qg, kd, vn, p, gl, hist)
    du, dw, dqg, dkd, dgl = outs
    n = bsz * t_seq
    return (du.reshape(n, hw), dw.reshape(n, hw), dqg.reshape(n, hw), dkd.reshape(n, hw),
            dgl.reshape(bsz * nc_seq, NH, LANE))


def _dn_intra_bwd(qn, kn, v, projp, alog_row, dtb_row, u, w, tmat, du, dw, dqg, dkd, do, vn, dgl, *, nc_seq, name,
                  rider=None):
    n = qn.shape[0]
    nct = n // CHUNK
    hw = NH * DN_D
    scale = DN_D ** -0.5

    grp = _group(nc_seq)

    def inner(gi, q_ref, k_ref, v_ref, sa_ref, al_ref, dt_ref, u_ref, w_ref, t_ref, du_ref, dw_ref, dqg_ref, dkd_ref,
              do_ref, vn_ref, dgl_ref, dq_ref, dk_ref, dv_ref, dsa_ref, dal_ref, ddt_ref):
        ci = (pl.program_id(0) * grp + gi) % nc_seq
        sa = sa_ref[...]
        beta_t, g_t, gam_t, valid, ea = _dn_gates(sa, al_ref[...], dt_ref[...], ci)
        yield
        lane = _iota2((CHUNK, LANE), 1)
        gates_t = jnp.where(lane < 4, beta_t, gam_t).T
        r, c = _masks64()
        incl, strict, upper, supper = r >= c, r > c, r <= c, r < c
        rows1 = _iota2((CHUNK, 1), 0)
        acc = [jnp.zeros((CHUNK, LANE), F32)]

        def head(h):
            sl = slice(h * DN_D, (h + 1) * DN_D)
            beta_w = jnp.broadcast_to(beta_t[:, h:h + 1], (CHUNK, DN_D))
            gam_w = jnp.broadcast_to(gam_t[:, 4 + h:5 + h], (CHUNK, DN_D))
            beta_s, gam_s = beta_w[:, 0:CHUNK], gam_w[:, 0:CHUNK]
            beta_row = gates_t[h:h + 1, :]
            gam_row = gates_t[4 + h:5 + h, :]
            gl = gam_t[CHUNK - 1:CHUNK, 4 + h:5 + h]
            dec = jnp.exp(jnp.where(incl, gam_s - gam_row, -jnp.inf))
            dec_t = jnp.exp(jnp.where(upper, gam_row - gam_s, -jnp.inf))
            egam_w = jnp.exp(gam_w)
            ekd_w = jnp.exp(gl - gam_w)
            kh = k_ref[:, sl].astype(F32)
            qh = q_ref[:, sl].astype(F32) * scale
            vh = v_ref[:, sl].astype(F32)
            uh = u_ref[:, sl]
            wh = w_ref[:, sl]
            doh = do_ref[:, sl]
            vnh = vn_ref[:, sl]
            kk = _nt(kh, kh)
            qk = _nt(qh, kh)
            qk_t = _nt(kh, qh)
            dp = _nt(doh, vnh)
            dp_t = _nt(vnh, doh)
            t_hi, t_lo = _split(t_ref[0, h].T)
            duh, dwh = du_ref[:, sl], dw_ref[:, sl]
            dvb = _nn(t_hi, duh) + _nn(t_lo, duh)
            dkg = _nn(t_hi, dwh) + _nn(t_lo, dwh)
            yield
            dvb_hi, dvb_lo = _split(dvb)
            dkg_hi, dkg_lo = _split(dkg)
            m = (_nt(dvb_hi, uh) + _nt(dvb_lo, uh)) + (_nt(dkg_hi, wh) + _nt(dkg_lo, wh))
            m_t = (_nt(uh, dvb_hi) + _nt(uh, dvb_lo)) + (_nt(wh, dkg_hi) + _nt(wh, dkg_lo))
            yield
            da = jnp.where(strict, -m, 0.0)
            da_t = jnp.where(supper, -m_t, 0.0)
            a = jnp.where(strict, beta_s * kk * dec, 0.0)
            a_t = jnp.where(supper, beta_row * kk * dec_t, 0.0)
            dad = da * dec
            dad_t = da_t * dec_t
            dpm = jnp.where(incl, dp, 0.0)
            dpm_t = jnp.where(upper, dp_t, 0.0)
            e = da * a + dpm * (qk * dec)
            e_t = da_t * a_t + dpm_t * (qk_t * dec_t)
            dqgh = dqg_ref[:, sl].astype(F32)
            dkdh = dkd_ref[:, sl].astype(F32)
            bg_w = beta_w * egam_w
            dkh = (_nn(beta_s * dad, kh) + _nn(beta_row * dad_t, kh) + _nn(dpm_t * dec_t, qh)
                   + bg_w * dkg + ekd_w * dkdh)
            dqh = _nn(dpm * dec, kh) + egam_w * dqgh
            t_kd = dkdh * (ekd_w * kh)
            dbeta = (jnp.sum(dad * kk, axis=1, keepdims=True)
                     + jnp.sum(dkg * (egam_w * kh) + dvb * vh, axis=1, keepdims=True))
            dgam = (jnp.sum(e - e_t, axis=1, keepdims=True)
                    + jnp.sum(dkg * (bg_w * kh) + dqgh * (egam_w * qh) - t_kd, axis=1, keepdims=True))
            dgam_last = (jnp.sum(jnp.sum(t_kd, axis=0, keepdims=True), axis=1, keepdims=True)
                         + dgl_ref[0, h:h + 1, 0:1] * jnp.exp(gl))
            dgam = dgam + jnp.where(rows1 == CHUNK - 1, dgam_last, 0.0)
            dq_ref[:, sl] = (dqh * scale).astype(dq_ref.dtype)
            dk_ref[:, sl] = dkh.astype(dk_ref.dtype)
            dv_ref[:, sl] = (beta_w * dvb).astype(dv_ref.dtype)
            acc[0] = acc[0] + jnp.where(lane == h, dbeta, 0.0) + jnp.where(lane == 4 + h, dgam, 0.0)

        yield from _round_robin([head(h) for h in range(NH)])
        acc_t = acc[0]
        dg_t = _tri_sum(upper, acc_t)
        ddb = acc_t * beta_t * (1.0 - beta_t)
        dda = jnp.where(valid, dg_t * (-ea) * _sigmoid(sa + dt_ref[...]), 0.0)
        dsa_ref[...] = jnp.where(lane < 4, ddb, jnp.where(lane < 8, dda, 0.0)).astype(dsa_ref.dtype)
        in_g = jnp.logical_and(lane >= 4, lane < 8)
        dal = jnp.sum(jnp.where(in_g, dg_t * g_t, 0.0), axis=0, keepdims=True)
        ddt = jnp.sum(jnp.where(in_g, dda, 0.0), axis=0, keepdims=True)
        _accumulate(dal_ref, dal, gi)
        _accumulate(ddt_ref, ddt, gi)

    rows = grp * CHUNK
    row = pl.BlockSpec((rows, hw), lambda i: (i, 0))
    vec = pl.BlockSpec((1, LANE), lambda i: (0, 0))
    mat = pl.BlockSpec((grp, NH, CHUNK, CHUNK), lambda i: (i, 0, 0, 0))
    glb = pl.BlockSpec((grp, NH, LANE), lambda i: (i, 0, 0))
    big = jax.ShapeDtypeStruct((n, hw), F32)
    v128 = jax.ShapeDtypeStruct((1, LANE), F32)
    kinds = (["row"] * 4 + ["whole"] * 2 + ["row"] * 2 + ["lead"] + ["row"] * 6 + ["lead"]
             + ["row"] * 4 + ["whole"] * 2)
    outs, ridden = _hosted_call(
        _per_chunk(inner, kinds, grp), rider, name=name, grid=(nct // grp,),
        in_specs=[row, row, row, pl.BlockSpec((rows, LANE), lambda i: (i, 0)), vec, vec,
                  row, row, mat, row, row, row, row, row, row, glb],
        out_specs=[row, row, row, pl.BlockSpec((rows, LANE), lambda i: (i, 0)), vec, vec],
        out_shape=[big, big, big, jax.ShapeDtypeStruct((n, LANE), BF16), v128, v128],
        scratch_shapes=[], compiler_params=_cp(VMEM_BIG, ("arbitrary",)),
        args=(qn, kn, v, projp, alog_row, dtb_row, u, w, tmat, du, dw, dqg, dkd, do, vn, dgl))
    return (*outs, ridden)


GQ_W = NH * GLA_DK
GV_W = NH * GLA_DV
GLA_NORM = 16.0
MID = CHUNK // 2


def _gla_gates(sb, w2p, gb, chunk_in_seq):
    rows = _iota2((CHUNK, GQ_W), 0)
    valid = jnp.logical_or(rows >= N_PAD, chunk_in_seq > 0)
    graw = _nn(sb, w2p) + gb
    yield
    g = jnp.where(valid, _logsigmoid(graw) * (1.0 / GLA_NORM), 0.0)
    r, c = _masks64()
    bcum = _tri_sum(r >= c, g)
    yield
    return graw, bcum, valid


def _head_mask(h):
    lane = _iota2((1, GQ_W), 1)
    return jnp.logical_and(lane >= h * GLA_DK, lane < (h + 1) * GLA_DK)


def _gla_intra_fwd(projp, gates, w2p, gb, *, nc_seq, name):
    n = projp.shape[0]
    nct = n // CHUNK
    scale = GLA_DK ** -0.5

    grp = _group(nc_seq)
    rows = grp * CHUNK

    def inner(gi, qk_ref, v_ref, sb_ref, w2_ref, gb_ref, oi_ref, qg_ref, kd_ref, gl_ref):
        ci = (pl.program_id(0) * grp + gi) % nc_seq
        _, bc, _ = yield from _gla_gates(sb_ref[...], w2_ref[...], gb_ref[...], ci)
        bref = bc[MID:MID + 1, :]
        bl = bc[CHUNK - 1:CHUNK, :]
        q = qk_ref[:, 0:GQ_W].astype(F32) * scale
        k = qk_ref[:, GQ_W:2 * GQ_W].astype(F32)
        qi = q * jnp.exp(bc - bref)
        ki = k * jnp.exp(bref - bc)
        qg_ref[...] = (q * jnp.exp(bc)).astype(qg_ref.dtype)
        kd_ref[...] = (k * jnp.exp(bl - bc)).astype(kd_ref.dtype)
        gl_ref[0] = jnp.exp(bl)
        r, c = _masks64()
        incl = r >= c
        a = [jnp.where(incl, _nt(jnp.where(_head_mask(h), qi, 0.0), ki), 0.0) for h in range(NH)]
        yield
        for h in range(NH):
            oi_ref[:, h * GLA_DV:(h + 1) * GLA_DV] = _nn(a[h], v_ref[:, h * GLA_DV:(h + 1) * GLA_DV]).astype(oi_ref.dtype)

    kinds = ["row"] * 3 + ["whole"] * 2 + ["row"] * 3 + ["lead"]
    return pl.pallas_call(
        _per_chunk(inner, kinds, grp), name=name, grid=(nct // grp,),
        in_specs=[pl.BlockSpec((rows, 2 * GQ_W), lambda i: (i, C_GQK // (2 * GQ_W))),
                  pl.BlockSpec((rows, GV_W), lambda i: (i, C_GV // GV_W)),
                  pl.BlockSpec((rows, LANE), lambda i: (i, 1)),
                  pl.BlockSpec((LANE, GQ_W), lambda i: (0, 0)), pl.BlockSpec((1, GQ_W), lambda i: (0, 0))],
        out_specs=[pl.BlockSpec((rows, GV_W), lambda i: (i, 0)), pl.BlockSpec((rows, GQ_W), lambda i: (i, 0)),
                   pl.BlockSpec((rows, GQ_W), lambda i: (i, 0)), pl.BlockSpec((grp, 1, GQ_W), lambda i: (i, 0, 0))],
        out_shape=[jax.ShapeDtypeStruct((n, GV_W), BF16), jax.ShapeDtypeStruct((n, GQ_W), BF16),
                   jax.ShapeDtypeStruct((n, GQ_W), BF16), jax.ShapeDtypeStruct((nct, 1, GQ_W), F32)],
        compiler_params=_cp(VMEM_BIG),
    )(projp, projp, gates, w2p, gb)


def _gla_scan_fwd(oi, qg, kd, gl, projp, *, bsz, nc_seq, name, rider=None):
    t_seq = nc_seq * CHUNK
    oi = oi.reshape(bsz, t_seq, GV_W)
    qg, kd = qg.reshape(bsz, t_seq, GQ_W), kd.reshape(bsz, t_seq, GQ_W)
    gl = gl.reshape(bsz, nc_seq, 1, GQ_W)
    pj = projp.reshape(bsz, t_seq, PW)
    grp = _group(nc_seq)

    def body(oi_ref, qg_ref, kd_ref, gl_ref, v_ref, o_ref, hist_ref, st_ref):
        @pl.when(pl.program_id(0) == 0)
        def _():
            st_ref[...] = jnp.zeros_like(st_ref)

        for gi in range(grp):
            rows = pl.ds(gi * CHUNK, CHUNK)
            for b in range(bsz):
                st = st_ref[b]
                hist_ref[b, gi] = st.astype(hist_ref.dtype)
                qgb = qg_ref[b, rows, :]
                kdb = kd_ref[b, rows, :]
                upd = jnp.zeros((GLA_DV, GQ_W), F32)
                for h in range(NH):
                    sl = slice(h * GLA_DV, (h + 1) * GLA_DV)
                    m = _head_mask(h)
                    o_ref[b, rows, sl] = (oi_ref[b, rows, sl] + _nt(jnp.where(m, qgb, 0.0), st)).astype(o_ref.dtype)
                    upd = upd + jnp.where(m, _tn(v_ref[b, rows, sl], kdb), 0.0)
                st_ref[b] = gl_ref[b, gi] * st + upd

    rws = grp * CHUNK
    outs, ridden = _hosted_call(
        body, rider, name=name, grid=(nc_seq // grp,),
        in_specs=[pl.BlockSpec((bsz, rws, GV_W), lambda i: (0, i, 0)),
                  pl.BlockSpec((bsz, rws, GQ_W), lambda i: (0, i, 0)),
                  pl.BlockSpec((bsz, rws, GQ_W), lambda i: (0, i, 0)),
                  pl.BlockSpec((bsz, grp, 1, GQ_W), lambda i: (0, i, 0, 0)),
                  pl.BlockSpec((bsz, rws, GV_W), lambda i: (0, i, C_GV // GV_W))],
        out_specs=[pl.BlockSpec((bsz, rws, GV_W), lambda i: (0, i, 0)),
                   pl.BlockSpec((bsz, grp, GLA_DV, GQ_W), lambda i: (0, i, 0, 0))],
        out_shape=[jax.ShapeDtypeStruct((bsz, t_seq, GV_W), BF16),
                   jax.ShapeDtypeStruct((bsz, nc_seq, GLA_DV, GQ_W), F32)],
        scratch_shapes=[pltpu.VMEM((bsz, GLA_DV, GQ_W), F32)],
        compiler_params=_cp(VMEM_BIG, ("arbitrary",)), args=(oi, qg, kd, gl, pj))
    return outs[0].reshape(bsz * t_seq, GV_W), outs[1], ridden


def _gla_scan_bwd(do, qg, kd, gl, projp, hist, *, bsz, nc_seq, name):
    t_seq = nc_seq * CHUNK
    do = do.reshape(bsz, t_seq, GV_W)
    qg, kd = qg.reshape(bsz, t_seq, GQ_W), kd.reshape(bsz, t_seq, GQ_W)
    gl = gl.reshape(bsz, nc_seq, 1, GQ_W)
    pj = projp.reshape(bsz, t_seq, PW)
    grp = _group(nc_seq)

    def body(do_ref, qg_ref, kd_ref, gl_ref, v_ref, hist_ref, dqg_ref, dkd_ref, dv_ref, dgl_ref, dst_ref):
        @pl.when(pl.program_id(0) == 0)
        def _():
            dst_ref[...] = jnp.zeros_like(dst_ref)

        for gi in reversed(range(grp)):
            rows = pl.ds(gi * CHUNK, CHUNK)
            for b in range(bsz):
                st = hist_ref[b, gi]
                dst = dst_ref[b]
                qgb = qg_ref[b, rows, :]
                kdb = kd_ref[b, rows, :]
                dqg = jnp.zeros((CHUNK, GQ_W), F32)
                dkd = jnp.zeros((CHUNK, GQ_W), F32)
                add = jnp.zeros((GLA_DV, GQ_W), F32)
                for h in range(NH):
                    sl = slice(h * GLA_DV, (h + 1) * GLA_DV)
                    m = _head_mask(h)
                    doh = do_ref[b, rows, sl]
                    vh = v_ref[b, rows, sl]
                    dqg = dqg + jnp.where(m, _nn(doh, st), 0.0)
                    dkd = dkd + jnp.where(m, _nn(vh, dst), 0.0)
                    dv_ref[b, rows, sl] = _nt(jnp.where(m, kdb, 0.0), dst).astype(dv_ref.dtype)
                    add = add + jnp.where(m, _tn(doh, qgb), 0.0)
                dqg_ref[b, rows, :] = dqg.astype(dqg_ref.dtype)
                dkd_ref[b, rows, :] = dkd.astype(dkd_ref.dtype)
                dgl_ref[b, gi] = jnp.sum(dst * st, axis=0, keepdims=True)
                dst_ref[b] = gl_ref[b, gi] * dst + add

    steps = nc_seq // grp
    rws = grp * CHUNK
    rev = lambda i: steps - 1 - i
    outs = pl.pallas_call(
        body, name=name, grid=(steps,),
        in_specs=[pl.BlockSpec((bsz, rws, GV_W), lambda i: (0, rev(i), 0)),
                  pl.BlockSpec((bsz, rws, GQ_W), lambda i: (0, rev(i), 0)),
                  pl.BlockSpec((bsz, rws, GQ_W), lambda i: (0, rev(i), 0)),
                  pl.BlockSpec((bsz, grp, 1, GQ_W), lambda i: (0, rev(i), 0, 0)),
                  pl.BlockSpec((bsz, rws, GV_W), lambda i: (0, rev(i), C_GV // GV_W)),
                  pl.BlockSpec((bsz, grp, GLA_DV, GQ_W), lambda i: (0, rev(i), 0, 0))],
        out_specs=[pl.BlockSpec((bsz, rws, GQ_W), lambda i: (0, rev(i), 0)),
                   pl.BlockSpec((bsz, rws, GQ_W), lambda i: (0, rev(i), 0)),
                   pl.BlockSpec((bsz, rws, GV_W), lambda i: (0, rev(i), 0)),
                   pl.BlockSpec((bsz, grp, 1, GQ_W), lambda i: (0, rev(i), 0, 0))],
        out_shape=[jax.ShapeDtypeStruct((bsz, t_seq, GQ_W), F32), jax.ShapeDtypeStruct((bsz, t_seq, GQ_W), F32),
                   jax.ShapeDtypeStruct((bsz, t_seq, GV_W), BF16), jax.ShapeDtypeStruct((bsz, nc_seq, 1, GQ_W), F32)],
        scratch_shapes=[pltpu.VMEM((bsz, GLA_DV, GQ_W), F32)],
        compiler_params=_cp(VMEM_BIG, ("arbitrary",)),
    )(do, qg, kd, gl, pj, hist)
    n = bsz * t_seq
    return (outs[0].reshape(n, GQ_W), outs[1].reshape(n, GQ_W), outs[2].reshape(n, GV_W),
            outs[3].reshape(bsz * nc_seq, 1, GQ_W))


def _gla_intra_bwd(projp, gates, w2p, gb, do, dqg, dkd, dvi, dgl, *, nc_seq, name):
    n = projp.shape[0]
    nct = n // CHUNK
    scale = GLA_DK ** -0.5

    grp = _group(nc_seq)
    rows = grp * CHUNK

    def inner(gi, qk_ref, v_ref, sb_ref, w2_ref, gb_ref, do_ref, dqg_ref, dkd_ref, dvi_ref, dgl_ref,
              dqk_ref, dv_ref, dsb_ref, dw2_ref, dgb_ref):
        ci = (pl.program_id(0) * grp + gi) % nc_seq
        sb = sb_ref[...]
        w2 = w2_ref[...]
        graw, bc, valid = yield from _gla_gates(sb, w2, gb_ref[...], ci)
        bref = bc[MID:MID + 1, :]
        bl = bc[CHUNK - 1:CHUNK, :]
        q = qk_ref[:, 0:GQ_W].astype(F32) * scale
        k = qk_ref[:, GQ_W:2 * GQ_W].astype(F32)
        ex1 = jnp.exp(bc - bref)
        ex2 = jnp.exp(bref - bc)
        eb = jnp.exp(bc)
        ekd = jnp.exp(bl - bc)
        qi, ki = q * ex1, k * ex2
        r, c = _masks64()
        incl = r >= c
        upper = r <= c
        a_t, da, da_t = [], [], []
        for h in range(NH):
            sl = slice(h * GLA_DV, (h + 1) * GLA_DV)
            doh = do_ref[:, sl]
            vh = v_ref[:, sl]
            a_t.append(jnp.where(upper, _nt(jnp.where(_head_mask(h), ki, 0.0), qi), 0.0))
            da.append(jnp.where(incl, _nt(doh, vh), 0.0))
            da_t.append(jnp.where(upper, _nt(vh, doh), 0.0))
        yield
        dqi = jnp.zeros((CHUNK, GQ_W), F32)
        dki = jnp.zeros((CHUNK, GQ_W), F32)
        for h in range(NH):
            sl = slice(h * GLA_DV, (h + 1) * GLA_DV)
            m = _head_mask(h)
            dv_ref[:, sl] = (_nn(a_t[h], do_ref[:, sl]) + dvi_ref[:, sl]).astype(dv_ref.dtype)
            dqi = dqi + jnp.where(m, _nn(da[h], ki), 0.0)
            dki = dki + jnp.where(m, _nn(da_t[h], qi), 0.0)
        yield
        dqg = dqg_ref[...].astype(F32)
        dkd = dkd_ref[...].astype(F32)
        dqk_ref[:, 0:GQ_W] = ((dqi * ex1 + dqg * eb) * scale).astype(dqk_ref.dtype)
        dqk_ref[:, GQ_W:2 * GQ_W] = (dki * ex2 + dkd * ekd).astype(dqk_ref.dtype)
        t_qi, t_ki, t_kd = dqi * qi, dki * ki, dkd * (k * ekd)
        db = t_qi - t_ki + dqg * (q * eb) - t_kd
        dbref = jnp.sum(t_ki - t_qi, axis=0, keepdims=True)
        dbl = jnp.sum(t_kd, axis=0, keepdims=True) + dgl_ref[0] * jnp.exp(bl)
        rows = _iota2((CHUNK, GQ_W), 0)
        db = db + jnp.where(rows == MID, dbref, 0.0) + jnp.where(rows == CHUNK - 1, dbl, 0.0)
        dg = _tri_sum(upper, db)
        yield
        dgraw = jnp.where(valid, dg * (1.0 / GLA_NORM) * _sigmoid(-graw), 0.0)
        dsb_ref[...] = _nt(dgraw, w2).astype(dsb_ref.dtype)
        dw2 = _tn(sb, dgraw)
        dgb = jnp.sum(dgraw, axis=0, keepdims=True)
        _accumulate(dw2_ref, dw2, gi)
        _accumulate(dgb_ref, dgb, gi)

    rq = pl.BlockSpec((rows, GQ_W), lambda i: (i, 0))
    rv = pl.BlockSpec((rows, GV_W), lambda i: (i, 0))
    kinds = ["row"] * 3 + ["whole"] * 2 + ["row"] * 4 + ["lead"] + ["row"] * 3 + ["whole"] * 2
    return pl.pallas_call(
        _per_chunk(inner, kinds, grp), name=name, grid=(nct // grp,),
        in_specs=[pl.BlockSpec((rows, 2 * GQ_W), lambda i: (i, C_GQK // (2 * GQ_W))),
                  pl.BlockSpec((rows, GV_W), lambda i: (i, C_GV // GV_W)),
                  pl.BlockSpec((rows, LANE), lambda i: (i, 1)),
                  pl.BlockSpec((LANE, GQ_W), lambda i: (0, 0)), pl.BlockSpec((1, GQ_W), lambda i: (0, 0)),
                  rv, rq, rq, rv, pl.BlockSpec((grp, 1, GQ_W), lambda i: (i, 0, 0))],
        out_specs=[pl.BlockSpec((rows, 2 * GQ_W), lambda i: (i, 0)), rv, pl.BlockSpec((rows, LANE), lambda i: (i, 0)),
                   pl.BlockSpec((LANE, GQ_W), lambda i: (0, 0)), pl.BlockSpec((1, GQ_W), lambda i: (0, 0))],
        out_shape=[jax.ShapeDtypeStruct((n, 2 * GQ_W), BF16), jax.ShapeDtypeStruct((n, GV_W), BF16),
                   jax.ShapeDtypeStruct((n, LANE), BF16), jax.ShapeDtypeStruct((LANE, GQ_W), F32),
                   jax.ShapeDtypeStruct((1, GQ_W), F32)],
        compiler_params=_cp(VMEM_BIG, ("arbitrary",)),
    )(projp, projp, gates, w2p, gb, do, dqg, dkd, dvi, dgl)


SECTIONS = ((C_QKV, 1536), (C_DZ, 512), (C_GQK, 512), (C_GV, 512), (C_GR, 512), (C_SA, 128), (C_SB, 128))


def _inproj_bwd(secs, wp, h0, g1, dx1, *, bsz, t_seq, tr, name, rider=None):
    n, d = h0.shape
    per_seq = t_seq // tr
    steps = n // tr
    s_len = t_seq - CHUNK

    def body(*refs):
        sec_refs = refs[:len(SECTIONS)]
        wp_ref, h0_ref, g_ref, dx1_ref, gx_hbm, meta_ref, dg_ref, obuf, sems = refs[len(SECTIONS):]
        i = pl.program_id(0)
        slot = i % 2

        def put(step, slot_, act):
            b, j = step // per_seq, step % per_seq

            @pl.when(j == 0)
            def _():
                act(pltpu.make_async_copy(obuf.at[slot_, pl.ds(CHUNK, tr - CHUNK), :],
                                          gx_hbm.at[b, pl.ds(0, tr - CHUNK), :], sems.at[slot_]))

            if per_seq > 1:
                @pl.when(j > 0)
                def _():
                    act(pltpu.make_async_copy(obuf.at[slot_], gx_hbm.at[b, pl.ds(j * tr - CHUNK, tr), :],
                                              sems.at[slot_]))

        dh = None
        for s_ref, (off, wd) in zip(sec_refs, SECTIONS):
            part = _nt(s_ref[...], wp_ref[:, off:off + wd])
            dh = part if dh is None else dh + part
        dx, dg = _rms_bwd_math(h0_ref[...], g_ref[...], dh)
        tot = dx1_ref[...] + dx

        @pl.when(i >= 2)
        def _():
            put(i - 2, slot, lambda cp: cp.wait())

        obuf[slot] = tot
        put(i, slot, lambda cp: cp.start())

        @pl.when(i % per_seq == 0)
        def _():
            meta_ref[...] = tot[N_PAD:CHUNK, :]

        @pl.when(i == steps - 1)
        def _():
            if steps > 1:
                put(i - 1, 1 - slot, lambda cp: cp.wait())
            put(i, slot, lambda cp: cp.wait())

        @pl.when(i == 0)
        def _():
            dg_ref[...] = dg

        @pl.when(i > 0)
        def _():
            dg_ref[...] += dg

    row = pl.BlockSpec((tr, d), lambda i: (i, 0))
    vec = pl.BlockSpec((1, d), lambda i: (0, 0))
    outs, ridden = _hosted_call(
        body, rider, name=name, grid=(steps,),
        in_specs=[pl.BlockSpec((tr, wd), lambda i: (i, 0)) for _, wd in SECTIONS]
        + [pl.BlockSpec((d, PW), lambda i: (0, 0)), row, vec, row],
        out_specs=[ANY, pl.BlockSpec((N_META, d), lambda i: (i // per_seq, 0)), vec],
        out_shape=[jax.ShapeDtypeStruct((bsz, s_len, d), F32), jax.ShapeDtypeStruct((bsz * N_META, d), F32),
                   jax.ShapeDtypeStruct((1, d), F32)],
        scratch_shapes=[pltpu.VMEM((2, tr, d), F32), pltpu.SemaphoreType.DMA((2,))],
        compiler_params=_cp(VMEM_BIG, ("arbitrary",)), args=(*secs, wp, h0, g1, dx1))
    return (*outs, ridden)


def _adamw(w, g, m, v, *, name, emit_grad=False, col_tile=None):
    lead = w.ndim - 2
    r, c = w.shape[-2:]
    tr = r if col_tile is not None else (_tile(r, 256, 8) if r > 256 else r)
    tc = col_tile if col_tile is not None else c
    c1 = 1.0 - ADAM_B1 ** ADAM_STEP
    c2 = 1.0 - ADAM_B2 ** ADAM_STEP
    n_out = 4 if emit_grad else 3

    def body(w_ref, g_ref, m_ref, v_ref, *out_refs):
        rd = (lambda ref: ref[0]) if lead else (lambda ref: ref[...])
        gv = g_ref[:, 0:tc]
        nm = ADAM_B1 * rd(m_ref) + (1.0 - ADAM_B1) * gv
        nv = ADAM_B2 * rd(v_ref) + (1.0 - ADAM_B2) * (gv * gv)
        res = [-ADAM_LR * ((nm / c1) / (jnp.sqrt(nv / c2) + ADAM_EPS) + ADAM_WD * rd(w_ref)), nm, nv, gv]
        for o_ref, val in zip(out_refs, res):
            if lead:
                o_ref[0] = val
            else:
                o_ref[...] = val

    if col_tile is None:
        blk = pl.BlockSpec((1,) * lead + (tr, c), lambda i: (0,) * lead + (i, 0))
        gblk = pl.BlockSpec((tr, g.shape[1]), lambda i: (i, 0))
        steps = r // tr
    else:
        blk = pl.BlockSpec((1,) * lead + (r, tc), lambda j: (0,) * lead + (0, j))
        gblk = pl.BlockSpec((r, tc), lambda j: (0, j))
        steps = c // tc
    sds = jax.ShapeDtypeStruct(w.shape, F32)
    return pl.pallas_call(
        body, name=name, grid=(steps,), in_specs=[blk, gblk, blk, blk], out_specs=[blk] * n_out,
        out_shape=[sds] * n_out, compiler_params=_cp(VMEM_BIG),
    )(w, g, m, v)


def _adamw_rows(w, g, m, v, *, name):
    r, _, c = w.shape
    tr = max(t for t in range(1, 129) if r % t == 0)
    c1 = 1.0 - ADAM_B1 ** ADAM_STEP
    c2 = 1.0 - ADAM_B2 ** ADAM_STEP

    def body(w_ref, g_ref, m_ref, v_ref, d_ref, nm_ref, nv_ref, go_ref):
        gv = g_ref[...]
        nm = ADAM_B1 * m_ref[...] + (1.0 - ADAM_B1) * gv
        nv = ADAM_B2 * v_ref[...] + (1.0 - ADAM_B2) * (gv * gv)
        d_ref[...] = -ADAM_LR * ((nm / c1) / (jnp.sqrt(nv / c2) + ADAM_EPS) + ADAM_WD * w_ref[...])
        nm_ref[...] = nm
        nv_ref[...] = nv
        go_ref[...] = gv

    blk = pl.BlockSpec((tr, 1, c), lambda i: (i, 0, 0))
    sds = jax.ShapeDtypeStruct(w.shape, F32)
    return pl.pallas_call(
        body, name=name, grid=(r // tr,), in_specs=[blk] * 4, out_specs=[blk] * 4, out_shape=[sds] * 4,
        compiler_params=_cp(VMEM_BIG),
    )(w, g, m, v)


def _pair_sum(where, g, theirs, *, name):
    lead, r, cols = g.shape
    half = r // 2
    tr = _tile(half, 256, 16)
    nh = half // tr

    def body(w_ref, a_ref, b_ref, o_ref):
        o_ref[...] = (a_ref[...] + b_ref[...]).astype(o_ref.dtype)

    blk = pl.BlockSpec((1, tr, cols), lambda s, i, w: (s, i, 0))
    return pl.pallas_call(
        body, name=name,
        grid_spec=pltpu.PrefetchScalarGridSpec(
            num_scalar_prefetch=1, grid=(lead, nh),
            in_specs=[pl.BlockSpec((1, tr, cols), lambda s, i, w: (s, w[0] * nh + i, 0)), blk], out_specs=blk),
        out_shape=jax.ShapeDtypeStruct((lead, half, cols), BF16), compiler_params=_cp(VMEM_BIG),
    )(where, g, theirs)


def _chip_sum(where, pair, q, *, name):
    _, half, cols = pair.shape
    tr = _tile(half, 256, 16)
    nh = half // tr

    def body(w_ref, own_ref, q1_ref, q2_ref, q3_ref, o_ref):
        f = lambda ref: ref[0].astype(F32)
        o_ref[...] = ((f(own_ref) + f(q1_ref)) + f(q2_ref)) + f(q3_ref)

    def peer(d):
        return pl.BlockSpec((1, tr, cols), lambda i, w: ((w[1] + d) % N_CHIPS, i, 0))

    return pl.pallas_call(
        body, name=name,
        grid_spec=pltpu.PrefetchScalarGridSpec(
            num_scalar_prefetch=1, grid=(nh,),
            in_specs=[peer(0), peer(1), peer(2), peer(3)],
            out_specs=pl.BlockSpec((tr, cols), lambda i, w: (w[0] * nh + i, 0))),
        out_shape=jax.ShapeDtypeStruct((2 * half, cols), F32), compiler_params=_cp(VMEM_BIG),
    )(where, pair, q, q, q)


VM = pl.BlockSpec(memory_space=pltpu.VMEM)


def _row_chunks(rows, n_split):
    size = rows // n_split
    assert size * n_split == rows and size % 16 == 0, (rows, n_split)
    return [(s, pl.ds(s * size, size)) for s in range(n_split)], size


D2D_SPLIT = 4
ICI_SPLIT = 2


def _sibling_halves(grads):
    n_arr = len(grads)

    def body(*refs):
        ins = refs[:n_arr]
        theirs = refs[n_arr:2 * n_arr]
        send_sems, recv_sems = refs[2 * n_arr:]
        x, y, c = _place()
        copies = []
        for k in range(n_arr):
            half = ins[k].shape[1] // 2
            chunks, size = _row_chunks(half, D2D_SPLIT)
            for s, dst_rows in chunks:
                give = pltpu.make_async_remote_copy(
                    src_ref=ins[k].at[:, pl.ds((1 - c) * half + s * size, size), :], dst_ref=theirs[k].at[:, dst_rows, :],
                    send_sem=send_sems.at[k, s], recv_sem=recv_sems.at[k, s], device_id=(x, y, 1 - c),
                    device_id_type=MESH)
                give.start()
                copies.append(give)
        for give in copies:
            give.wait()

    halves = [jax.ShapeDtypeStruct((g.shape[0], g.shape[1] // 2, g.shape[2]), F32) for g in grads]
    sem = pltpu.SemaphoreType.DMA((n_arr, D2D_SPLIT))
    return pl.pallas_call(
        body, name="sibling_halves", in_specs=[ANY] * n_arr, out_specs=[ANY] * n_arr, out_shape=halves,
        scratch_shapes=[sem, sem],
    )(*grads)


def _chip_exchange(parts):
    n_arr = len(parts)

    def body(*refs):
        ins = refs[:n_arr]
        outs = refs[n_arr:2 * n_arr]
        send_sems, recv_sems = refs[2 * n_arr:]
        x, y, c = _place()
        me = 2 * x + y
        sends = []
        for k in range(n_arr):
            chunks, _ = _row_chunks(ins[k].shape[1], ICI_SPLIT)
            for d, (px, py, pj) in enumerate(_other_chips(x, y)):
                for s, rows in chunks:
                    cp = pltpu.make_async_remote_copy(
                        src_ref=ins[k].at[pj, rows, :], dst_ref=outs[k].at[me, rows, :], send_sem=send_sems.at[k, d, s],
                        recv_sem=recv_sems.at[k, d, s], device_id=(px, py, c), device_id_type=MESH)
                    cp.start()
                    sends.append(cp)
        for k in range(n_arr):
            chunks, _ = _row_chunks(ins[k].shape[1], ICI_SPLIT)
            for d, (px, py, pj) in enumerate(_other_chips(x, y)):
                for s, rows in chunks:
                    pltpu.make_async_remote_copy(
                        src_ref=ins[k].at[pj, rows, :], dst_ref=outs[k].at[pj, rows, :], send_sem=send_sems.at[k, d, s],
                        recv_sem=recv_sems.at[k, d, s], device_id=(px, py, c), device_id_type=MESH).wait_recv()
        for cp in sends:
            cp.wait_send()

    sem = pltpu.SemaphoreType.DMA((n_arr, 3, ICI_SPLIT))
    return pl.pallas_call(
        body, name="chip_exchange", in_specs=[ANY] * n_arr, out_specs=[ANY] * n_arr,
        out_shape=[jax.ShapeDtypeStruct(p.shape, p.dtype) for p in parts],
        scratch_shapes=[sem, sem],
    )(*parts)


class _SiblingHalvesRider:
    def __init__(self, grads):
        self.inputs = list(grads)
        self.out_shapes = [jax.ShapeDtypeStruct((g.shape[0], g.shape[1] // 2, g.shape[2]), F32) for g in grads]
        self.aliases = {}
        self.sems = [pltpu.SemaphoreType.DMA((len(grads), D2D_SPLIT))] * 2

    def _copies(self, ins, outs, sems):
        x, y, c = _place()
        for k in range(len(ins)):
            half = ins[k].shape[1] // 2
            chunks, size = _row_chunks(half, D2D_SPLIT)
            for s, dst_rows in chunks:
                yield pltpu.make_async_remote_copy(
                    src_ref=ins[k].at[:, pl.ds((1 - c) * half + s * size, size), :], dst_ref=outs[k].at[:, dst_rows, :],
                    send_sem=sems[0].at[k, s], recv_sem=sems[1].at[k, s], device_id=(x, y, 1 - c), device_id_type=MESH)

    def first(self, ins, outs, sems):
        for cp in self._copies(ins, outs, sems):
            cp.start()

    def last(self, ins, outs, sems):
        for cp in self._copies(ins, outs, sems):
            cp.wait()


class _ChipExchangeRider:
    def __init__(self, parts):
        self.inputs = list(parts)
        self.out_shapes = [jax.ShapeDtypeStruct(p.shape, p.dtype) for p in parts]
        self.aliases = {}
        self.sems = [pltpu.SemaphoreType.DMA((len(parts), 3, ICI_SPLIT))] * 2

    def _copies(self, ins, outs, sems, receiving):
        x, y, c = _place()
        for k in range(len(ins)):
            chunks, _ = _row_chunks(ins[k].shape[1], ICI_SPLIT)
            for d, (px, py, pj) in enumerate(_other_chips(x, y)):
                for s, rows in chunks:
                    yield pltpu.make_async_remote_copy(
                        src_ref=ins[k].at[pj, rows, :], dst_ref=outs[k].at[pj if receiving else 2 * x + y, rows, :],
                        send_sem=sems[0].at[k, d, s], recv_sem=sems[1].at[k, d, s], device_id=(px, py, c),
                        device_id_type=MESH)

    def first(self, ins, outs, sems):
        for cp in self._copies(ins, outs, sems, False):
            cp.start()

    def last(self, ins, outs, sems):
        for cp in self._copies(ins, outs, sems, True):
            cp.wait_recv()
        for cp in self._copies(ins, outs, sems, False):
            cp.wait_send()


def _sibling_join(bufs):
    n_arr = len(bufs)

    def body(*refs):
        bufs_out = refs[n_arr:2 * n_arr]
        send_sems, recv_sems = refs[2 * n_arr:]
        x, y, c = _place()
        copies = []
        for k in range(n_arr):
            half = bufs_out[k].shape[0] // 2
            chunks, size = _row_chunks(half, D2D_SPLIT)
            for s, _ in chunks:
                rows = pl.ds(c * half + s * size, size)
                give = pltpu.make_async_remote_copy(
                    src_ref=bufs_out[k].at[rows, :], dst_ref=bufs_out[k].at[rows, :], send_sem=send_sems.at[k, s],
                    recv_sem=recv_sems.at[k, s], device_id=(x, y, 1 - c), device_id_type=MESH)
                give.start()
                copies.append((k, s, half, size, give))
        for k, s, half, size, give in copies:
            rows = pl.ds((1 - c) * half + s * size, size)
            pltpu.make_async_remote_copy(
                src_ref=bufs_out[k].at[rows, :], dst_ref=bufs_out[k].at[rows, :], send_sem=send_sems.at[k, s],
                recv_sem=recv_sems.at[k, s], device_id=(x, y, 1 - c), device_id_type=MESH).wait_recv()
            give.wait_send()

    sem = pltpu.SemaphoreType.DMA((n_arr, D2D_SPLIT))
    return pl.pallas_call(
        body, name="sibling_join", in_specs=[ANY] * n_arr, out_specs=[ANY] * n_arr,
        out_shape=[jax.ShapeDtypeStruct(b.shape, F32) for b in bufs],
        input_output_aliases={k: k for k in range(n_arr)},
        scratch_shapes=[sem, sem],
    )(*bufs)


PACK_ROWS = 48


def _small_allreduce(pack):
    masks = [(dx, dy, dc) for dx in (0, 1) for dy in (0, 1) for dc in (0, 1)][1:]

    def body(p_ref, o_ref, buf, send_sems, recv_sems):
        x, y, c = _place()
        me = 4 * x + 2 * y + c
        buf[me] = p_ref[...]
        sends = []
        for k, (dx, dy, dc) in enumerate(masks):
            peer = (1 - x if dx else x, 1 - y if dy else y, 1 - c if dc else c)
            cp = pltpu.make_async_remote_copy(
                src_ref=p_ref, dst_ref=buf.at[me], send_sem=send_sems.at[k], recv_sem=recv_sems.at[k],
                device_id=peer, device_id_type=MESH)
            cp.start()
            sends.append(cp)
        for k, (dx, dy, dc) in enumerate(masks):
            peer = (1 - x if dx else x, 1 - y if dy else y, 1 - c if dc else c)
            pj = 4 * peer[0] + 2 * peer[1] + peer[2]
            pltpu.make_async_remote_copy(
                src_ref=p_ref, dst_ref=buf.at[pj], send_sem=send_sems.at[k], recv_sem=recv_sems.at[k],
                device_id=peer, device_id_type=MESH).wait_recv()
        for cp in sends:
            cp.wait_send()
        tot = buf[0]
        for k in range(1, 8):
            tot = tot + buf[k]
        o_ref[...] = tot
        o_ref[0:N_META, :] = tot[0:N_META] + tot[N_META:2 * N_META]

    return pl.pallas_call(
        body, name="small_allreduce", in_specs=[VM], out_specs=VM,
        out_shape=jax.ShapeDtypeStruct((PACK_ROWS, D_MODEL), F32),
        scratch_shapes=[pltpu.VMEM((8, PACK_ROWS, D_MODEL), F32), pltpu.SemaphoreType.DMA((7,)),
                        pltpu.SemaphoreType.DMA((7,))],
    )(pack)


def _pad_lanes(vec, offset):
    k = vec.shape[1]
    return jnp.concatenate([jnp.zeros((1, offset), F32), vec, jnp.zeros((1, LANE - offset - k), F32)], axis=1)


def _local_step(x, tgt, meta, norm1_g, wp, conv_w, a_log, dt_bias, dn_norm_g, gla_w2, gla_b, gla_norm_g,
                w_out, norm2_g, w_up, w_down, final_norm_g, late_gather=None, where=None, early_gather=None):
    bsz, s_len, d = x.shape
    t_seq = s_len + CHUNK
    nc_seq = t_seq // CHUNK
    n = bsz * t_seq
    tr = _tile(t_seq, 832)
    tt = _tile(t_seq, 416)

    lead = jnp.concatenate([jnp.zeros((N_PAD, d), F32), meta], axis=0)
    alog_row = _pad_lanes(a_log, 4)
    dtb_row = _pad_lanes(dt_bias, 4)

    h0, h, got_early = _embed_norm(x, lead, norm1_g, tr=tr, name="embed_norm1", rider=early_gather)
    if early_gather is not None:
        wp = _padded_from_shards(got_early[0])
        conv_w = got_early[1].transpose(1, 0, 2).reshape(4, QKV_W)
        gla_w2 = got_early[2][:, :, 0:GQ_W // N_CHIPS].transpose(1, 0, 2).reshape(GLA_RANK, GQ_W)
    w2p = jnp.concatenate([gla_w2, jnp.zeros((LANE - GLA_RANK, GQ_W), F32)], axis=0)
    ride_up, ride_down, ride_out = late_gather if late_gather is not None else (None, None, None)
    projp, gates = _mm(h, wp, "nn", tm=tt, tn=PW, tk=d, out_dtypes=(BF16, F32), out_widths=(PW, PW - C_SA),
                       epilogue=lambda acc: (acc, acc[:, C_SA:PW]), name="in_proj")
    qn, kn, v, got_up = _dnprep_fwd(projp, conv_w, bsz=bsz, t_seq=t_seq, tt=tt, name="dn_prep", rider=ride_up)
    u, w, qg, kd, pmat, tmat, gl, got_down = _dn_intra_fwd(qn, kn, v, gates, alog_row, dtb_row, nc_seq=nc_seq,
                                                           name="dn_intra", rider=ride_down)
    o_dn, vn, hist, got_out = _dn_scan_fwd(u, w, qg, kd, pmat, gl, bsz=bsz, nc_seq=nc_seq, name="dn_scan",
                                           rider=ride_out)
    oi, gqg, gkd, ggl = _gla_intra_fwd(projp, gates, w2p, gla_b, nc_seq=nc_seq, name="gla_intra")
    o_gla, ghist, _ = _gla_scan_fwd(oi, gqg, gkd, ggl, projp, bsz=bsz, nc_seq=nc_seq, name="gla_scan")
    if late_gather is not None:
        w_out = got_out[0].reshape(d, d)
        w_up = got_up[0].transpose(1, 0, 2).reshape(d, D_FF)
        w_down = got_down[0].reshape(D_FF, d)
    mix = _gnorm_fwd(o_dn, o_gla, projp, dn_norm_g, gla_norm_g, tr=tr, name="gated_norm")
    def residual_norm(acc, res, g):
        x1v = res + acc
        r = lax.rsqrt(jnp.mean(x1v * x1v, axis=-1, keepdims=True) + EPS)
        return x1v, x1v * r * g

    x1, h2 = _mm(mix, w_out, "nn", tm=tr, tn=d, tk=d, out_dtypes=(F32, BF16), extras=(h0,), vec_extras=(norm2_g,),
                 epilogue=residual_norm, name="out_proj_norm2")

    (act,) = _mm(h2, w_up, "nn", tm=tt, tn=D_FF, tk=d, out_dtypes=(BF16,),
                 epilogue=lambda acc: (jnp.square(jnp.maximum(acc, 0.0)),), name="mlp_up", n_chunk=1024)
    dx2, dx2b, d_final_g, loss_tile = _mlp_down_loss(act, w_down, x1, final_norm_g, tgt, t_seq=t_seq, tr=tt,
                                                     name="mlp_down_loss")

    def relu2_bwd(acc, a):
        a = a.astype(F32)
        return (acc * (2.0 * (a * lax.rsqrt(jnp.maximum(a, F32_TINY)))),)

    (dup,) = _mm(dx2b, w_down, "nt", tm=tt, tn=D_FF, tk=d, out_dtypes=(BF16,), extras=(act,),
                 epilogue=relu2_bwd, name="mlp_down_bwd", n_chunk=1024)
    tk2 = 2 * tr if n % (2 * tr) == 0 else tr
    (d_w_down,) = _mm(act, dx2b, "tn", tm=D_FF // 2, tn=d, tk=tk2, out_dtypes=(F32,), name="w_down_grad")
    (d_w_up_sm,) = _mm(h2, dup, "tn", tm=d, tn=D_FF // 2, tk=tk2, out_dtypes=(F32,), name="w_up_grad",
                       shard_cols=D_FF // N_CHIPS)
    mlp_sm = [d_w_up_sm, d_w_down.reshape(N_CHIPS, D_FF // N_CHIPS, d)]
    ride1 = _SiblingHalvesRider(mlp_sm) if where is not None else None
    dx1, dx1b, d_norm2_g, theirs = _mlp_up_bwd_norm(dup, w_up, x1, norm2_g, dx2, tr=tt, name="mlp_up_bwd_norm",
                                                    rider=ride1)
    ride2 = None
    if where is not None:
        mlp_pair = [_pair_sum(where, a, b, name=f"pair_sum_mlp{k}") for k, (a, b) in enumerate(zip(mlp_sm, theirs))]
        ride2 = _ChipExchangeRider(mlp_pair)

    (dmix,) = _mm(dx1b, w_out, "nt", tm=tr, tn=d, tk=d, out_dtypes=(BF16,), name="out_proj_bwd")
    (d_w_out,) = _mm(mix, dx1b, "tn", tm=d, tn=d, tk=tr, out_dtypes=(F32,), name="w_out_grad")
    do_dn, ddz, do_gla, dgr, d_dn_norm_g, d_gla_norm_g = _gnorm_bwd(
        dmix, o_dn, o_gla, projp, dn_norm_g, gla_norm_g, tr=tr, name="gated_norm_bwd")
    du, dw, dqg, dkd, dgl = _dn_scan_bwd(do_dn, w, qg, kd, vn, pmat, gl, hist, bsz=bsz, nc_seq=nc_seq,
                                          name="dn_scan_bwd")
    dqn, dkn, dv, dsa, d_alog, d_dtb, mlp_parts = _dn_intra_bwd(
        qn, kn, v, gates, alog_row, dtb_row, u, w, tmat, du, dw, dqg, dkd, do_dn, vn, dgl, nc_seq=nc_seq,
        name="dn_intra_bwd", rider=ride2)
    dz, d_conv_w = _dnprep_bwd_a(projp, conv_w, dqn, dkn, dv, bsz=bsz, t_seq=t_seq, tt=tt, name="dn_prep_bwd")
    dcin = _dnprep_bwd_b(dz, conv_w, bsz=bsz, t_seq=t_seq, tt=tt, name="conv_bwd")
    gdqg, gdkd, gdvi, gdgl = _gla_scan_bwd(do_gla, gqg, gkd, ggl, projp, ghist, bsz=bsz, nc_seq=nc_seq,
                                            name="gla_scan_bwd")
    dgqk, dgv, dsb, d_w2p, d_gla_b = _gla_intra_bwd(projp, gates, w2p, gla_b, do_gla, gdqg, gdkd, gdvi, gdgl,
                                                    nc_seq=nc_seq, name="gla_intra_bwd")

    secs = (dcin, ddz, dgqk, dgv, dgr, dsa, dsb)
    g_lo = _grad_tn(h, secs[0:2], tk=tk2, name="w_in_grad_lo")
    g_hi = _grad_tn(h, secs[2:7], tk=tk2, name="w_in_grad_hi")
    ride3 = None
    if where is not None:
        late_sm = [_shards_from_padded(g_lo, g_hi), d_w_out.reshape(N_CHIPS, d // N_CHIPS, d)]
        late_theirs = _exchange_now(_SiblingHalvesRider(late_sm), name="sibling_halves")
        late_pair = [_pair_sum(where, a, b, name=f"pair_sum_{k}") for k, (a, b) in enumerate(zip(late_sm, late_theirs))]
        ride3 = _ChipExchangeRider(late_pair)
    grad_x, d_meta_rows, d_norm1_g, late_parts = _inproj_bwd(secs, wp, h0, norm1_g, dx1, bsz=bsz, t_seq=t_seq, tr=tt,
                                                             name="in_proj_bwd", rider=ride3)

    grads = dict(w_in_lo=g_lo, w_in_hi=g_hi, w_out=d_w_out, w_up_shards=d_w_up_sm, w_down=d_w_down, meta_rows=d_meta_rows,
                 norm1_g=d_norm1_g, conv_w=d_conv_w, a_log_tile=d_alog, dt_bias_tile=d_dtb, dn_norm_g=d_dn_norm_g,
                 gla_w2=d_w2p[0:GLA_RANK], gla_b=d_gla_b, gla_norm_g=d_gla_norm_g, norm2_g=d_norm2_g,
                 final_norm_g=d_final_g, loss_tile=loss_tile)
    if where is not None:
        grads["exchanged"] = (late_pair + mlp_pair, list(late_parts) + list(mlp_parts))
    return grad_x, grads


SHARD_W = IN_WIDTH // N_CHIPS
PADDED_ORDER = ((0, 2048), (2056, 3592), (2048, 2056), LANE - 8, (3592, 3608), LANE - GLA_RANK)


def _pad_layout(w_full):
    pieces = [jnp.zeros((w_full.shape[0], seg), w_full.dtype) if isinstance(seg, int) else w_full[:, seg[0]:seg[1]]
              for seg in PADDED_ORDER]
    return jnp.concatenate(pieces, axis=1)


def _padded_from_shards(stack):
    pieces = []
    for seg in PADDED_ORDER:
        if isinstance(seg, int):
            pieces.append(jnp.zeros((stack.shape[1], seg), stack.dtype))
            continue
        for j in range(N_CHIPS):
            lo, hi = max(seg[0], j * SHARD_W), min(seg[1], (j + 1) * SHARD_W)
            if lo < hi:
                pieces.append(stack[j, :, lo - j * SHARD_W:hi - j * SHARD_W])
    return jnp.concatenate(pieces, axis=1)


def _shards_from_padded(g_lo, g_hi):
    split = g_lo.shape[1]
    starts, pos = [], 0
    for seg in PADDED_ORDER:
        width = seg if isinstance(seg, int) else seg[1] - seg[0]
        if not isinstance(seg, int):
            starts.append((seg[0], seg[1], pos))
        pos += width
    shards = []
    for j in range(N_CHIPS):
        pieces = []
        for a, b, p0 in sorted(starts):
            lo, hi = max(a, j * SHARD_W), min(b, (j + 1) * SHARD_W)
            if lo < hi:
                src, off = (g_lo, 0) if p0 < split else (g_hi, split)
                pieces.append(src[:, p0 + lo - a - off:p0 + hi - a - off])
        pieces.append(jnp.zeros((g_lo.shape[0], D_MODEL - SHARD_W), g_lo.dtype))
        shards.append(jnp.concatenate(pieces, axis=1))
    return jnp.stack(shards)


def _pack_small(g, bsz):
    assert bsz * N_META == 32
    row = jnp.concatenate([g["a_log_tile"], g["dt_bias_tile"], g["dn_norm_g"], g["gla_norm_g"], g["gla_b"],
                           g["loss_tile"], jnp.zeros((1, LANE), F32)], axis=1)
    return jnp.concatenate([g["meta_rows"], g["norm1_g"], g["conv_w"].reshape(6, D_MODEL), row,
                            g["gla_w2"].reshape(4, D_MODEL), g["norm2_g"], g["final_norm_g"],
                            jnp.zeros((2, D_MODEL), F32)], axis=0)


def kernel(x, meta_tokens, norm1_g, w_in, conv_w, a_log, dt_bias, dn_norm_g, gla_w2, gla_b, gla_norm_g, w_out, norm2_g, w_up, w_down, final_norm_g, loss_target, m_meta_tokens, m_norm1_g, m_w_in, m_conv_w, m_a_log, m_dt_bias, m_dn_norm_g, m_gla_w2, m_gla_b, m_gla_norm_g, m_w_out, m_norm2_g, m_w_up, m_w_down, m_final_norm_g, v_meta_tokens, v_norm1_g, v_w_in, v_conv_w, v_a_log, v_dt_bias, v_dn_norm_g, v_gla_w2, v_gla_b, v_gla_norm_g, v_w_out, v_norm2_g, v_w_up, v_w_down, v_final_norm_g):
    bsz = x.shape[0]
    chip = 2 * lax.axis_index("x") + lax.axis_index("y")

    lane_pad = lambda a, wd: jnp.pad(a, ((0, 0), (0, wd - a.shape[1])))
    where = jnp.stack([lax.axis_index("c"), chip]).astype(jnp.int32)
    slot = lambda a, dt, nm: _to_slot(where, a, dt, name="slot_" + nm)
    (g_meta,) = _exchange_now(_GatherRider([slot(meta_tokens, F32, "meta")], [False]), name="gather_meta")
    early = _GatherRider([slot(lane_pad(w_in[0], D_MODEL), BF16, "w_in"), slot(conv_w[0], F32, "conv"),
                          slot(lane_pad(gla_w2[0], LANE), F32, "gla_w2")], [True, False, False])
    late = (_GatherRider([slot(w_up[0], BF16, "w_up")], [True]), _GatherRider([slot(w_down[0], BF16, "w_down")], [True]),
            _GatherRider([slot(w_out[0], BF16, "w_out")], [True]))
    meta_f = g_meta.transpose(1, 0, 2).reshape(N_META, D_MODEL)

    grad_x, g = _local_step(x, loss_target, meta_f, norm1_g, None, None, a_log, dt_bias, dn_norm_g, None, gla_b,
                            gla_norm_g, None, norm2_g, None, None, final_norm_g.reshape(1, D_MODEL), late_gather=late,
                            where=where, early_gather=early)

    pair, parts = g["exchanged"]
    halves = [_chip_sum(where, p, q, name=f"chip_sum_{k}") for k, (p, q) in enumerate(zip(pair, parts))]
    gw_in, gw_out, gw_up, gw_down = _sibling_join(halves)

    red = _small_allreduce(_pack_small(g, bsz))
    g_meta_full = red[0:N_META]
    g_norm1 = red[32:33]
    g_conv_full = red[33:39].reshape(4, QKV_W)
    srow = red[39:40]
    g_alog, g_dtb = srow[:, 4:8], srow[:, LANE + 4:LANE + 8]
    g_dn_norm, g_gla_norm = srow[:, 2 * LANE:3 * LANE], srow[:, 3 * LANE:4 * LANE]
    g_gla_b = srow[:, 4 * LANE:6 * LANE]
    loss = srow[0, 6 * LANE]
    g_w2_full = red[40:44].reshape(GLA_RANK, GQ_W)
    g_norm2 = red[44:45]
    g_final = red[45:46]
    g_meta_sh = lax.dynamic_slice_in_dim(g_meta_full, chip * (D_MODEL // N_CHIPS), D_MODEL // N_CHIPS, axis=1)
    g_conv_sh = lax.dynamic_slice_in_dim(g_conv_full, chip * (QKV_W // N_CHIPS), QKV_W // N_CHIPS, axis=1)
    g_w2_sh = lax.dynamic_slice_in_dim(g_w2_full, chip * (GQ_W // N_CHIPS), GQ_W // N_CHIPS, axis=1)

    names = ["meta_tokens", "norm1_g", "w_in", "conv_w", "a_log", "dt_bias", "dn_norm_g", "gla_w2", "gla_b",
             "gla_norm_g", "w_out", "norm2_g", "w_up", "w_down", "final_norm_g"]
    weights = dict(meta_tokens=meta_tokens, norm1_g=norm1_g, w_in=w_in, conv_w=conv_w, a_log=a_log, dt_bias=dt_bias,
                   dn_norm_g=dn_norm_g, gla_w2=gla_w2, gla_b=gla_b, gla_norm_g=gla_norm_g, w_out=w_out,
                   norm2_g=norm2_g, w_up=w_up, w_down=w_down, final_norm_g=final_norm_g)
    ms = dict(meta_tokens=m_meta_tokens, norm1_g=m_norm1_g, w_in=m_w_in, conv_w=m_conv_w, a_log=m_a_log,
              dt_bias=m_dt_bias, dn_norm_g=m_dn_norm_g, gla_w2=m_gla_w2, gla_b=m_gla_b, gla_norm_g=m_gla_norm_g,
              w_out=m_w_out, norm2_g=m_norm2_g, w_up=m_w_up, w_down=m_w_down, final_norm_g=m_final_norm_g)
    vs = dict(meta_tokens=v_meta_tokens, norm1_g=v_norm1_g, w_in=v_w_in, conv_w=v_conv_w, a_log=v_a_log,
              dt_bias=v_dt_bias, dn_norm_g=v_dn_norm_g, gla_w2=v_gla_w2, gla_b=v_gla_b, gla_norm_g=v_gla_norm_g,
              w_out=v_w_out, norm2_g=v_norm2_g, w_up=v_w_up, w_down=v_w_down, final_norm_g=v_final_norm_g)
    grads2d = dict(meta_tokens=g_meta_sh, norm1_g=g_norm1, w_in=gw_in, conv_w=g_conv_sh, a_log=g_alog, dt_bias=g_dtb,
                   dn_norm_g=g_dn_norm, gla_w2=g_w2_sh, gla_b=g_gla_b, gla_norm_g=g_gla_norm, w_out=gw_out,
                   norm2_g=g_norm2, w_up=gw_up, w_down=gw_down, final_norm_g=g_final)
    out_g, out_d, out_m, out_v = [], [], [], []
    for nm in names:
        shape = weights[nm].shape
        g2 = grads2d[nm]
        if nm == "w_in":
            tview = lambda a: jnp.transpose(a, (2, 0, 1))
            res = _adamw_rows(tview(weights[nm]), g2[:, 0:SHARD_W].T.reshape(SHARD_W, 1, D_MODEL), tview(ms[nm]),
                              tview(vs[nm]), name=f"adamw_{nm}")
            res = [jnp.transpose(a, (1, 2, 0)) for a in res]
            gout = res[3]
        elif len(shape) == 3:
            res = _adamw(weights[nm], g2, ms[nm], vs[nm], name=f"adamw_{nm}")
            gout = g2.reshape(shape)
        else:
            as2d = lambda a: a.reshape(g2.shape)
            res = _adamw(as2d(weights[nm]), g2, as2d(ms[nm]), as2d(vs[nm]), name=f"adamw_{nm}")
            gout = g2.reshape(shape)
        out_g.append(gout)
        out_d.append(res[0].reshape(shape))
        out_m.append(res[1].reshape(shape))
        out_v.append(res[2].reshape(shape))
    return (loss, grad_x, *out_g, *out_d, *out_m, *out_v)
```

```python
import functools

import jax
import jax.numpy as jnp
import numpy as np
from jax import lax
from jax.experimental import pallas as pl
from jax.experimental.pallas import tpu as pltpu

F32 = jnp.float32
BF16 = jnp.bfloat16
HI = lax.Precision.HIGHEST
MESH = pl.DeviceIdType.MESH

D_MODEL = 1024
N_META = 16
CHUNK = 64
N_PAD = CHUNK - N_META
NH = 4
DN_D = 128
GLA_DK = 64
GLA_DV = 128
GLA_RANK = 16
D_FF = 4 * D_MODEL
EPS = 1e-6
F32_TINY = 1.1754944e-38
IN_WIDTH = 3608
C_QKV, C_DZ, C_GQK, C_GV, C_GR, C_SA, C_SB, PW = 0, 1536, 2048, 2560, 3072, 3584, 3712, 3840
LANE = 128
N_CHIPS = 4

ADAM_LR, ADAM_B1, ADAM_B2, ADAM_EPS, ADAM_WD, ADAM_STEP = 0.001, 0.9, 0.999, 1e-08, 0.01, 10

VMEM_BIG = 56 * 1024 * 1024


def _cp(vmem=None, sem=None):
    kw = {}
    if vmem is not None:
        kw["vmem_limit_bytes"] = vmem
    if sem is not None:
        kw["dimension_semantics"] = sem
    return pltpu.CompilerParams(**kw)


def _tile(n, target, mult=16):
    best = None
    for t in range(mult, min(n, target) + 1, mult):
        if n % t == 0:
            best = t
    assert best is not None, (n, target)
    return best


def _dot(a, b, dims, prec=None):
    return lax.dot_general(a, b, (dims, ((), ())), preferred_element_type=F32, precision=prec)


def _nn(a, b):
    return _dot(a.astype(BF16), b.astype(BF16), ((1,), (0,)))


def _nt(a, b):
    return _dot(a.astype(BF16), b.astype(BF16), ((1,), (1,)))


def _tn(a, b):
    return _dot(a.astype(BF16), b.astype(BF16), ((0,), (0,)))


def _split(x):
    hi = x.astype(BF16)
    return hi, (x - hi.astype(F32)).astype(BF16)


def _tri_sum(tri, x):
    t = tri.astype(BF16)
    hi = x.astype(BF16)
    r1 = x - hi.astype(F32)
    mid = r1.astype(BF16)
    lo = (r1 - mid.astype(F32)).astype(BF16)
    nn = ((1,), (0,))
    return _dot(t, hi, nn) + _dot(t, mid, nn) + _dot(t, lo, nn)


def _sigmoid(x):
    return 0.5 * jnp.tanh(0.5 * x) + 0.5


def _softplus(x):
    return jnp.maximum(x, 0.0) + jnp.log(1.0 + jnp.exp(-jnp.abs(x)))


def _logsigmoid(x):
    return -_softplus(-x)


def _iota2(shape, dim):
    return lax.broadcasted_iota(jnp.int32, shape, dim)


def _mm(a, b, mode, *, tm, tn, tk, out_dtypes, extras=(), epilogue=None, name, vmem=VMEM_BIG, rider=None,
        out_widths=None, n_chunk=None, vec_extras=(), shard_cols=None):
    if mode == "tn":
        K, M = a.shape
    else:
        M, K = a.shape
    N = b.shape[0] if mode == "nt" else b.shape[1]
    assert M % tm == 0 and N % tn == 0 and K % tk == 0, (name, M, N, K, tm, tn, tk)
    nk = K // tk
    n_ex, n_out, n_vec = len(extras), len(out_dtypes), len(vec_extras)
    if mode == "tn":
        a_spec = pl.BlockSpec((tk, tm), lambda i, j, k: (k, i))
    else:
        a_spec = pl.BlockSpec((tm, tk), lambda i, j, k: (i, k))
    if mode == "nt":
        b_spec = pl.BlockSpec((tn, tk), lambda i, j, k: (j, k))
    else:
        b_spec = pl.BlockSpec((tk, tn), lambda i, j, k: (k, j))
    mn_spec = pl.BlockSpec((tm, tn), lambda i, j, k: (i, j))
    if out_widths is None:
        o_specs = [mn_spec] * n_out
        o_shapes = [jax.ShapeDtypeStruct((M, N), dt) for dt in out_dtypes]
    else:
        assert tn == N
        o_specs = [pl.BlockSpec((tm, wd), lambda i, j, k: (i, 0)) for wd in out_widths]
        o_shapes = [jax.ShapeDtypeStruct((M, wd), dt) for wd, dt in zip(out_widths, out_dtypes)]
    if shard_cols is not None:
        o_specs = [pl.BlockSpec((tn // shard_cols, tm, shard_cols), lambda i, j, k: (j, i, 0))]
        o_shapes = [jax.ShapeDtypeStruct((N // shard_cols, M, shard_cols), F32)]
    dims = {"nn": ((1,), (0,)), "nt": ((1,), (1,)), "tn": ((0,), (0,))}[mode]

    single = nk == 1
    direct = (not single) and epilogue is None and n_out == 1 and out_dtypes[0] == F32

    def body(*refs):
        a_ref, b_ref = refs[0], refs[1]
        ex_refs = refs[2:2 + n_ex]
        vec_refs = refs[2 + n_ex:2 + n_ex + n_vec]
        out_refs = refs[2 + n_ex + n_vec:2 + n_ex + n_vec + n_out]
        if n_chunk is not None:
            assert single and out_widths is None and mode != "tn" and tn % n_chunk == 0
            av = a_ref[...].astype(BF16)
            for j in range(tn // n_chunk):
                cols = slice(j * n_chunk, (j + 1) * n_chunk)
                bv = b_ref[cols, :] if mode == "nt" else b_ref[:, cols]
                acc = _dot(av, bv.astype(BF16), dims)
                res = (acc,) if epilogue is None else epilogue(acc, *[e[:, cols] for e in ex_refs])
                for o_ref, r in zip(out_refs, res):
                    o_ref[:, cols] = r.astype(o_ref.dtype)
            return
        part = _dot(a_ref[...].astype(BF16), b_ref[...].astype(BF16), dims)

        def finish(acc):
            res = (acc,) if epilogue is None else epilogue(acc, *[e[...] for e in ex_refs], *[v[...] for v in vec_refs])
            for o_ref, r in zip(out_refs, res):
                o_ref[...] = r.astype(o_ref.dtype)

        if single:
            if shard_cols is not None:
                for sh in range(tn // shard_cols):
                    out_refs[0][sh] = part[:, sh * shard_cols:(sh + 1) * shard_cols]
            else:
                finish(part)
            return
        acc_ref = out_refs[0] if direct else refs[2 + n_ex + n_vec + n_out]
        k = pl.program_id(2)
        if shard_cols is not None:
            assert direct
            for sh in range(tn // shard_cols):
                piece = part[:, sh * shard_cols:(sh + 1) * shard_cols]

                @pl.when(k == 0)
                def _():
                    acc_ref[sh] = piece

                @pl.when(k > 0)
                def _():
                    acc_ref[sh] += piece
            return

        @pl.when(k == 0)
        def _():
            acc_ref[...] = part

        @pl.when(k > 0)
        def _():
            acc_ref[...] += part

        if not direct:
            @pl.when(k == nk - 1)
            def _():
                finish(acc_ref[...])

    outs, ridden = _hosted_call(
        body, rider, name=name, grid=(M // tm, N // tn, nk),
        in_specs=[a_spec, b_spec] + [mn_spec] * n_ex + [pl.BlockSpec((1, tn), lambda i, j, k: (0, j))] * n_vec,
        out_specs=o_specs, out_shape=o_shapes,
        scratch_shapes=[] if (single or direct) else [pltpu.VMEM((tm, tn), F32)],
        compiler_params=_cp(vmem, ("parallel", "parallel", "arbitrary")), args=(a, b, *extras, *vec_extras))
    return tuple(outs) if rider is None else (tuple(outs), ridden)


def _grad_tn(a, secs, *, tk, name):
    kk, m = a.shape
    widths = [s.shape[1] for s in secs]
    total = sum(widths)
    nk = kk // tk

    def body(*refs):
        a_ref, sec_refs, o_ref = refs[0], refs[1:-1], refs[-1]
        cat = sec_refs[0][...] if len(sec_refs) == 1 else jnp.concatenate([s[...] for s in sec_refs], axis=1)
        part = _dot(a_ref[...].astype(BF16), cat.astype(BF16), ((0,), (0,)))
        k = pl.program_id(0)

        @pl.when(k == 0)
        def _():
            o_ref[...] = part

        @pl.when(k > 0)
        def _():
            o_ref[...] += part

    return pl.pallas_call(
        body, name=name, grid=(nk,),
        in_specs=[pl.BlockSpec((tk, m), lambda k: (k, 0))] + [pl.BlockSpec((tk, w), lambda k: (k, 0)) for w in widths],
        out_specs=pl.BlockSpec((m, total), lambda k: (0, 0)),
        out_shape=jax.ShapeDtypeStruct((m, total), F32),
        compiler_params=_cp(VMEM_BIG, ("arbitrary",)),
    )(a, *secs)


class _ShiftedRows:
    def __init__(self, src, buf, sems, *, per_seq, tt, steps):
        self.src, self.buf, self.sems = src, buf, sems
        self.per_seq, self.tt, self.steps = per_seq, tt, steps

    def _do(self, step, slot, act):
        b, j = step // self.per_seq, step % self.per_seq
        tt = self.tt

        @pl.when(j == 0)
        def _():
            act(pltpu.make_async_copy(self.src.at[b, pl.ds(0, tt - CHUNK), :],
                                      self.buf.at[slot, pl.ds(CHUNK, tt - CHUNK), :], self.sems.at[slot]))

        if self.per_seq > 1:
            @pl.when(j > 0)
            def _():
                act(pltpu.make_async_copy(self.src.at[b, pl.ds(j * tt - CHUNK, tt), :], self.buf.at[slot],
                                          self.sems.at[slot]))

    def tile(self, i):
        slot = i % 2

        @pl.when(i == 0)
        def _():
            self._do(i, slot, lambda cp: cp.start())

        self._do(i, slot, lambda cp: cp.wait())

        @pl.when(i + 1 < self.steps)
        def _():
            self._do(i + 1, 1 - slot, lambda cp: cp.start())

        return slot


def _embed_norm(x, lead, g, *, tr, name, rider=None):
    bsz, s_len, d = x.shape
    t_seq = s_len + CHUNK
    per_seq = t_seq // tr
    steps = bsz * per_seq
    n = bsz * t_seq

    def body(x_hbm, lead_ref, g_ref, h0_ref, h_ref, buf, sems):
        i = pl.program_id(0)
        slot = _ShiftedRows(x_hbm, buf, sems, per_seq=per_seq, tt=tr, steps=steps).tile(i)

        @pl.when(i % per_seq == 0)
        def _():
            buf[slot, 0:CHUNK, :] = lead_ref[...]

        xv = buf[slot]
        h0_ref[...] = xv
        r = lax.rsqrt(jnp.mean(xv * xv, axis=-1, keepdims=True) + EPS)
        h_ref[...] = (xv * r * g_ref[...]).astype(h_ref.dtype)

    row = pl.BlockSpec((tr, d), lambda i: (i, 0))
    outs, ridden = _hosted_call(
        body, rider, name=name, grid=(steps,),
        in_specs=[ANY, pl.BlockSpec((CHUNK, d), lambda i: (0, 0)), pl.BlockSpec((1, d), lambda i: (0, 0))],
        out_specs=[row, row],
        out_shape=[jax.ShapeDtypeStruct((n, d), F32), jax.ShapeDtypeStruct((n, d), BF16)],
        scratch_shapes=[pltpu.VMEM((2, tr, d), F32), pltpu.SemaphoreType.DMA((2,))],
        compiler_params=_cp(VMEM_BIG, ("arbitrary",)), args=(x, lead, g))
    return (*outs, ridden)


def _rms_fwd(x, g, *, tr, name):
    n, d = x.shape

    def body(x_ref, g_ref, o_ref):
        xv = x_ref[...]
        r = lax.rsqrt(jnp.mean(xv * xv, axis=-1, keepdims=True) + EPS)
        o_ref[...] = (xv * r * g_ref[...]).astype(o_ref.dtype)

    return pl.pallas_call(
        body, name=name, grid=(n // tr,),
        in_specs=[pl.BlockSpec((tr, d), lambda i: (i, 0)), pl.BlockSpec((1, d), lambda i: (0, 0))],
        out_specs=pl.BlockSpec((tr, d), lambda i: (i, 0)),
        out_shape=jax.ShapeDtypeStruct((n, d), BF16),
        compiler_params=_cp(VMEM_BIG),
    )(x, g)


RING_SLOTS = 3


class _RingStream:
    def __init__(self, hbm_ref, ring_ref, sems, *, tr, steps):
        self.hbm_ref, self.ring_ref, self.sems, self.tr, self.steps = hbm_ref, ring_ref, sems, tr, steps

    def _copy(self, k):
        slot = k % RING_SLOTS
        return pltpu.make_async_copy(self.hbm_ref.at[pl.ds(k * self.tr, self.tr), :], self.ring_ref.at[slot],
                                     self.sems.at[slot])

    def tile(self, step):
        @pl.when(step == 0)
        def _():
            for k in range(min(RING_SLOTS - 1, self.steps)):
                self._copy(k).start()

        @pl.when(step + RING_SLOTS - 1 < self.steps)
        def _():
            self._copy(step + RING_SLOTS - 1).start()

        self._copy(step).wait()
        return self.ring_ref.at[step % RING_SLOTS]


def _rms_bwd_math(xv, g, dy):
    r = lax.rsqrt(jnp.mean(xv * xv, axis=-1, keepdims=True) + EPS)
    xh = xv * r
    gdy = dy * g
    dx = r * (gdy - xh * jnp.mean(xh * gdy, axis=-1, keepdims=True))
    return dx, jnp.sum(dy * xh, axis=0, keepdims=True)


def _mlp_up_bwd_norm(dup, w_up, x, g, res, *, tr, name, rider=None):
    n, d = x.shape
    ff = dup.shape[1]

    steps = n // tr

    def body(dup_hbm, w_ref, x_ref, g_ref, res_ref, o_ref, ob_ref, dg_ref, ring_ref, ring_sems):
        dup_ref = _RingStream(dup_hbm, ring_ref, ring_sems, tr=tr, steps=steps).tile(pl.program_id(0))
        dh = _nt(dup_ref[...], w_ref[...])
        dx, dg = _rms_bwd_math(x_ref[...], g_ref[...], dh)
        tot = res_ref[...] + dx
        o_ref[...] = tot
        ob_ref[...] = tot.astype(BF16)

        @pl.when(pl.program_id(0) == 0)
        def _():
            dg_ref[...] = dg

        @pl.when(pl.program_id(0) > 0)
        def _():
            dg_ref[...] += dg

    row = pl.BlockSpec((tr, d), lambda i: (i, 0))
    vec = pl.BlockSpec((1, d), lambda i: (0, 0))
    outs, ridden = _hosted_call(
        body, rider, name=name, grid=(steps,),
        in_specs=[ANY, pl.BlockSpec((d, ff), lambda i: (0, 0)), row, vec, row],
        out_specs=[row, row, vec],
        out_shape=[jax.ShapeDtypeStruct((n, d), F32), jax.ShapeDtypeStruct((n, d), BF16),
                   jax.ShapeDtypeStruct((1, d), F32)],
        scratch_shapes=[pltpu.VMEM((RING_SLOTS, tr, ff), dup.dtype), pltpu.SemaphoreType.DMA((RING_SLOTS,))],
        compiler_params=_cp(VMEM_BIG, ("arbitrary",)), args=(dup, w_up, x, g, res))
    return (*outs, ridden)


def _mlp_down_loss(act, w_down, x1, gf, tgt, *, t_seq, tr, name):
    n, d = x1.shape
    ff = act.shape[1]
    per_seq = t_seq // tr
    steps = n // tr

    def body(a_hbm, w_ref, x_ref, g_ref, t_hbm, dx_ref, dxb_ref, dg_ref, loss_ref, tbuf, tsems, ring_ref, ring_sems):
        i = pl.program_id(0)
        a_ref = _RingStream(a_hbm, ring_ref, ring_sems, tr=tr, steps=steps).tile(i)
        slot = _ShiftedRows(t_hbm, tbuf, tsems, per_seq=per_seq, tt=tr, steps=steps).tile(i)

        @pl.when(i % per_seq == 0)
        def _():
            tbuf[slot, 0:CHUNK, :] = jnp.zeros((CHUNK, d), F32)

        t_ref = tbuf.at[slot]
        xv = x_ref[...] + _nn(a_ref[...], w_ref[...])
        g = g_ref[...]
        r = lax.rsqrt(jnp.mean(xv * xv, axis=-1, keepdims=True) + EPS)
        xh = xv * r
        pos = (i % per_seq) * tr + _iota2((tr, 1), 0)
        real = pos >= CHUNK
        err = jnp.where(real, xh * g - t_ref[...], 0.0)
        dy = err * (1.0 / d)
        gdy = dy * g
        dx = r * (gdy - xh * jnp.mean(xh * gdy, axis=-1, keepdims=True))
        dx_ref[...] = dx
        dxb_ref[...] = dx.astype(BF16)
        dg = jnp.sum(dy * xh, axis=0, keepdims=True)
        ls = 0.5 * jnp.sum(jnp.mean(err * err, axis=-1, keepdims=True), axis=0, keepdims=True)
        ls = jnp.where(_iota2((1, LANE), 1) == 0, ls, 0.0)

        @pl.when(i == 0)
        def _():
            dg_ref[...] = dg
            loss_ref[...] = ls

        @pl.when(i > 0)
        def _():
            dg_ref[...] += dg
            loss_ref[...] += ls

    row = pl.BlockSpec((tr, d), lambda i: (i, 0))
    vec = pl.BlockSpec((1, d), lambda i: (0, 0))
    one = pl.BlockSpec((1, LANE), lambda i: (0, 0))
    return pl.pallas_call(
        body, name=name, grid=(n // tr,),
        in_specs=[ANY, pl.BlockSpec((ff, d), lambda i: (0, 0)), row, vec, ANY],
        out_specs=[row, row, vec, one],
        out_shape=[jax.ShapeDtypeStruct((n, d), F32), jax.ShapeDtypeStruct((n, d), BF16),
                   jax.ShapeDtypeStruct((1, d), F32), jax.ShapeDtypeStruct((1, LANE), F32)],
        scratch_shapes=[pltpu.VMEM((2, tr, d), F32), pltpu.SemaphoreType.DMA((2,)),
                        pltpu.VMEM((RING_SLOTS, tr, ff), act.dtype), pltpu.SemaphoreType.DMA((RING_SLOTS,))],
        compiler_params=_cp(VMEM_BIG, ("arbitrary",)),
    )(act, w_down, x1, gf, tgt)


def _gnorm_fwd(o_dn, o_gla, projp, g_dn, g_gla, *, tr, name):
    n = o_dn.shape[0]
    w = NH * DN_D

    def body(odn_ref, ogl_ref, z_ref, r_ref, gdn_ref, ggl_ref, mix_ref):
        for grp, (o_ref, gate_ref, gain_ref) in enumerate(((odn_ref, z_ref, gdn_ref), (ogl_ref, r_ref, ggl_ref))):
            gain = gain_ref[...]
            for h in range(NH):
                sl = slice(h * DN_D, (h + 1) * DN_D)
                o = o_ref[:, sl].astype(F32)
                z = gate_ref[:, sl].astype(F32)
                r = lax.rsqrt(jnp.mean(o * o, axis=-1, keepdims=True) + EPS)
                y = (o * r * gain) * (z * _sigmoid(z))
                mix_ref[:, grp * w + h * DN_D: grp * w + (h + 1) * DN_D] = y.astype(mix_ref.dtype)

    row = pl.BlockSpec((tr, w), lambda i: (i, 0))
    vec = pl.BlockSpec((1, DN_D), lambda i: (0, 0))
    return pl.pallas_call(
        body, name=name, grid=(n // tr,),
        in_specs=[row, row, pl.BlockSpec((tr, w), lambda i: (i, C_DZ // w)),
                  pl.BlockSpec((tr, w), lambda i: (i, C_GR // w)), vec, vec],
        out_specs=pl.BlockSpec((tr, 2 * w), lambda i: (i, 0)),
        out_shape=jax.ShapeDtypeStruct((n, 2 * w), BF16),
        compiler_params=_cp(VMEM_BIG),
    )(o_dn, o_gla, projp, projp, g_dn, g_gla)


def _gnorm_bwd(dmix, o_dn, o_gla, projp, g_dn, g_gla, *, tr, name):
    n = o_dn.shape[0]
    w = NH * DN_D

    def body(dm_ref, odn_ref, ogl_ref, z_ref, r_ref, gdn_ref, ggl_ref,
             dodn_ref, ddz_ref, dogl_ref, dgr_ref, dgdn_ref, dggl_ref):
        first = pl.program_id(0) == 0
        groups = ((odn_ref, z_ref, gdn_ref, dodn_ref, ddz_ref, dgdn_ref),
                  (ogl_ref, r_ref, ggl_ref, dogl_ref, dgr_ref, dggl_ref))
        for grp, (o_ref, gate_ref, gain_ref, do_ref, dgate_ref, dgain_ref) in enumerate(groups):
            gain = gain_ref[...]
            dgain = jnp.zeros((1, DN_D), F32)
            for h in range(NH):
                sl = slice(h * DN_D, (h + 1) * DN_D)
                o = o_ref[:, sl].astype(F32)
                z = gate_ref[:, sl].astype(F32)
                dm = dm_ref[:, grp * w + h * DN_D: grp * w + (h + 1) * DN_D].astype(F32)
                r = lax.rsqrt(jnp.mean(o * o, axis=-1, keepdims=True) + EPS)
                oh = o * r
                s = _sigmoid(z)
                dn = dm * (z * s)
                dgate_ref[:, sl] = (dm * (oh * gain) * (s * (1.0 + z * (1.0 - s)))).astype(dgate_ref.dtype)
                gdn = dn * gain
                do_ref[:, sl] = (r * (gdn - oh * jnp.mean(oh * gdn, axis=-1, keepdims=True))).astype(do_ref.dtype)
                dgain = dgain + jnp.sum(dn * oh, axis=0, keepdims=True)

            @pl.when(first)
            def _():
                dgain_ref[...] = dgain

            @pl.when(jnp.logical_not(first))
            def _():
                dgain_ref[...] += dgain

    row = pl.BlockSpec((tr, w), lambda i: (i, 0))
    vec = pl.BlockSpec((1, DN_D), lambda i: (0, 0))
    big = jax.ShapeDtypeStruct((n, w), F32)
    gate = jax.ShapeDtypeStruct((n, w), BF16)
    small = jax.ShapeDtypeStruct((1, DN_D), F32)
    return pl.pallas_call(
        body, name=name, grid=(n // tr,),
        in_specs=[pl.BlockSpec((tr, 2 * w), lambda i: (i, 0)), row, row,
                  pl.BlockSpec((tr, w), lambda i: (i, C_DZ // w)), pl.BlockSpec((tr, w), lambda i: (i, C_GR // w)), vec, vec],
        out_specs=[row, row, row, row, vec, vec],
        out_shape=[gate, gate, gate, gate, small, small],
        compiler_params=_cp(VMEM_BIG),
    )(dmix, o_dn, o_gla, projp, projp, g_dn, g_gla)


QKV_W = 3 * NH * DN_D
HALO = 8


TAP_ROWS = 32


def _aligned_taps(src_ref, tap_ref, slot, starts, nr, cols):
    taps = []
    for j, start in enumerate(starts):
        if start % HALO == 0:
            taps.append(src_ref.at[pl.ds(start, nr), cols])
        else:
            tap_ref[slot, j, 0:nr, :] = src_ref[pl.ds(start, nr), cols]
            taps.append(tap_ref.at[slot, j, 0:nr, :])
    return taps


def _conv_taps(cw_ref, taps, cols):
    z = cw_ref[0:1, cols] * taps[0][...]
    for j in range(1, 4):
        z = z + cw_ref[j:j + 1, cols] * taps[j][...]
    return z


def _dnprep_fwd(projp, conv_w, *, bsz, t_seq, tt, name, rider=None):
    n = bsz * t_seq
    per_seq = t_seq // tt
    hw = NH * DN_D
    row_blk = _tile(tt, TAP_ROWS)

    def body(x_ref, halo_ref, cw_ref, q_ref, k_ref, v_ref, xs_ref, tap_ref):
        i = pl.program_id(1)
        xs_ref[0:HALO, :] = jnp.where(i == 0, 0.0, halo_ref[...].astype(F32))
        xs_ref[HALO:HALO + tt, :] = x_ref[...].astype(F32)
        for c, o_ref in enumerate([q_ref] * NH + [k_ref] * NH + [v_ref] * NH):
            cols = slice(c * DN_D, (c + 1) * DN_D)
            hcols = slice((c % NH) * DN_D, (c % NH + 1) * DN_D)
            for r0 in range(0, tt, row_blk):
                taps = _aligned_taps(xs_ref, tap_ref, (r0 // row_blk) % 2,
                                     [r0 + HALO - 3 + j for j in range(4)], row_blk, cols)
                z = _conv_taps(cw_ref, taps, cols)
                a = z * _sigmoid(z)
                if o_ref is not v_ref:
                    a = a * lax.rsqrt(jnp.sum(a * a, axis=-1, keepdims=True) + EPS)
                o_ref[r0:r0 + row_blk, hcols] = a.astype(o_ref.dtype)

    def halo_map(b, i):
        return (jnp.maximum((b * t_seq + i * tt) // HALO - 1, 0), 0)

    out = pl.BlockSpec((tt, hw), lambda b, i: (b * per_seq + i, 0))
    sds = jax.ShapeDtypeStruct((n, hw), BF16)
    outs, ridden = _hosted_call(
        body, rider, name=name, grid=(bsz, per_seq),
        in_specs=[pl.BlockSpec((tt, QKV_W), lambda b, i: (b * per_seq + i, 0)),
                  pl.BlockSpec((HALO, QKV_W), halo_map),
                  pl.BlockSpec((4, QKV_W), lambda b, i: (0, 0))],
        out_specs=[out, out, out], out_shape=[sds, sds, sds],
        scratch_shapes=[pltpu.VMEM((tt + HALO, QKV_W), F32), pltpu.VMEM((2, 4, row_blk, DN_D), F32)],
        compiler_params=_cp(VMEM_BIG), args=(projp, projp, conv_w))
    return (*outs, ridden)


def _dnprep_bwd_a(projp, conv_w, dq, dk, dv, *, bsz, t_seq, tt, name):
    n = bsz * t_seq
    per_seq = t_seq // tt
    hw = NH * DN_D

    row_blk = _tile(tt, 4 * TAP_ROWS, mult=HALO)

    def body(x_ref, halo_ref, cw_ref, dq_ref, dk_ref, dv_ref, dz_ref, dcw_ref, xs_ref, part_ref, tap_ref):
        b, i = pl.program_id(0), pl.program_id(1)
        xs_ref[0:HALO, :] = jnp.where(i == 0, 0.0, halo_ref[...].astype(F32))
        xs_ref[HALO:HALO + tt, :] = x_ref[...].astype(F32)
        for c, d_ref in enumerate([dq_ref] * NH + [dk_ref] * NH + [dv_ref] * NH):
            cols = slice(c * DN_D, (c + 1) * DN_D)
            hcols = slice((c % NH) * DN_D, (c % NH + 1) * DN_D)
            parts = [None] * 4
            for r0 in range(0, tt, row_blk):
                taps = _aligned_taps(xs_ref, tap_ref, (r0 // row_blk) % 2,
                                     [r0 + HALO - 3 + j for j in range(4)], row_blk, cols)
                z = _conv_taps(cw_ref, taps, cols)
                s = _sigmoid(z)
                dsilu = s * (1.0 + z * (1.0 - s))
                dy = d_ref[r0:r0 + row_blk, hcols]
                if d_ref is dv_ref:
                    dz = dy * dsilu
                else:
                    a = z * s
                    rs = lax.rsqrt(jnp.sum(a * a, axis=-1, keepdims=True) + EPS)
                    y = a * rs
                    dz = (rs * (dy - y * jnp.sum(dy * y, axis=-1, keepdims=True))) * dsilu
                dz_ref[r0:r0 + row_blk, cols] = dz
                for j in range(4):
                    p = jnp.sum(dz * taps[j][...], axis=0, keepdims=True)
                    parts[j] = p if parts[j] is None else parts[j] + p
            for j in range(4):
                part_ref[j:j + 1, cols] = parts[j]

        first = jnp.logical_and(b == 0, i == 0)

        @pl.when(first)
        def _():
            dcw_ref[...] = part_ref[0:4, :]

        @pl.when(jnp.logical_not(first))
        def _():
            dcw_ref[...] += part_ref[0:4, :]

    def halo_map(b, i):
        return (jnp.maximum((b * t_seq + i * tt) // HALO - 1, 0), 0)

    hrow = pl.BlockSpec((tt, hw), lambda b, i: (b * per_seq + i, 0))
    return pl.pallas_call(
        body, name=name, grid=(bsz, per_seq),
        in_specs=[pl.BlockSpec((tt, QKV_W), lambda b, i: (b * per_seq + i, 0)),
                  pl.BlockSpec((HALO, QKV_W), halo_map),
                  pl.BlockSpec((4, QKV_W), lambda b, i: (0, 0)), hrow, hrow, hrow],
        out_specs=[pl.BlockSpec((tt, QKV_W), lambda b, i: (b * per_seq + i, 0)),
                   pl.BlockSpec((4, QKV_W), lambda b, i: (0, 0))],
        out_shape=[jax.ShapeDtypeStruct((n, QKV_W), F32), jax.ShapeDtypeStruct((4, QKV_W), F32)],
        scratch_shapes=[pltpu.VMEM((tt + HALO, QKV_W), F32), pltpu.VMEM((HALO, QKV_W), F32),
                        pltpu.VMEM((2, 4, row_blk, DN_D), F32)],
        compiler_params=_cp(VMEM_BIG),
    )(projp, projp, conv_w, dq, dk, dv)


def _dnprep_bwd_b(dz, conv_w, *, bsz, t_seq, tt, name):
    n = bsz * t_seq
    per_seq = t_seq // tt
    last_blk = n // HALO - 1

    main = tt - HALO
    row_blk = _tile(tt, TAP_ROWS)
    steps = bsz * per_seq

    def body(dz_hbm, halo_ref, cw_ref, dx_ref, tail_ref, tap_ref, ring_ref, ring_sems):
        i = pl.program_id(1)
        dz_ref = _RingStream(dz_hbm, ring_ref, ring_sems, tr=tt, steps=steps).tile(pl.program_id(0) * per_seq + i)
        for cb in range(QKV_W // DN_D):
            cols = slice(cb * DN_D, (cb + 1) * DN_D)
            for r0 in range(0, main, row_blk):
                nr = min(row_blk, main - r0)
                taps = _aligned_taps(dz_ref, tap_ref, (r0 // row_blk) % 2, [r0 + 3 - j for j in range(4)], nr, cols)
                dx_ref[r0:r0 + nr, cols] = _conv_taps(cw_ref, taps, cols).astype(dx_ref.dtype)
        tail_ref[0:HALO, :] = dz_ref[main:tt, :]
        tail_ref[HALO:2 * HALO, :] = jnp.where(i == per_seq - 1, 0.0, halo_ref[...])
        dx = cw_ref[0:1, :] * tail_ref[pl.ds(3, HALO), :]
        for j in range(1, 4):
            dx = dx + cw_ref[j:j + 1, :] * tail_ref[pl.ds(3 - j, HALO), :]
        dx_ref[main:tt, :] = dx.astype(dx_ref.dtype)

    def halo_map(b, i):
        return (jnp.minimum((b * t_seq + (i + 1) * tt) // HALO, last_blk), 0)

    row = pl.BlockSpec((tt, QKV_W), lambda b, i: (b * per_seq + i, 0))
    return pl.pallas_call(
        body, name=name, grid=(bsz, per_seq),
        in_specs=[ANY, pl.BlockSpec((HALO, QKV_W), halo_map), pl.BlockSpec((4, QKV_W), lambda b, i: (0, 0))],
        out_specs=row, out_shape=jax.ShapeDtypeStruct((n, QKV_W), BF16),
        scratch_shapes=[pltpu.VMEM((2 * HALO, QKV_W), F32), pltpu.VMEM((2, 4, row_blk, DN_D), F32),
                        pltpu.VMEM((RING_SLOTS, tt, QKV_W), F32), pltpu.SemaphoreType.DMA((RING_SLOTS,))],
        compiler_params=_cp(VMEM_BIG, ("arbitrary", "arbitrary")),
    )(dz, dz, conv_w)


def _masks64():
    r = _iota2((CHUNK, CHUNK), 0)
    c = _iota2((CHUNK, CHUNK), 1)
    return r, c


def _group(nc_seq, target=5):
    return max(g for g in range(1, target + 1) if nc_seq % g == 0)


def _round_robin(chains):
    live = list(chains)
    while live:
        nxt = []
        for ch in live:
            try:
                next(ch)
                nxt.append(ch)
            except StopIteration:
                pass
        live = nxt
        yield


def _run(chains):
    for _ in _round_robin(chains):
        pass


def _per_chunk(inner, kinds, grp):
    def body(*refs):
        chains = []
        for gi in range(grp):
            views = []
            for r, kind in zip(refs, kinds):
                if kind == "row":
                    views.append(r.at[pl.ds(gi * CHUNK, CHUNK)])
                elif kind == "lead":
                    views.append(r.at[pl.ds(gi, 1)])
                else:
                    views.append(r)
            chains.append(inner(gi, *views))
        _run(chains)
    return body


def _accumulate(ref, val, gi):
    if gi > 0:
        ref[...] += val
        return
    first = pl.program_id(0) == 0

    @pl.when(first)
    def _():
        ref[...] = val

    @pl.when(jnp.logical_not(first))
    def _():
        ref[...] += val


ANY = pl.BlockSpec(memory_space=pl.ANY)


def _place():
    return lax.axis_index("x"), lax.axis_index("y"), lax.axis_index("c")


def _other_chips(x, y):
    return [(1 - x, y, 2 * (1 - x) + y), (x, 1 - y, 2 * x + 1 - y), (1 - x, 1 - y, 2 * (1 - x) + 1 - y)]


class _GatherRider:
    def __init__(self, bufs, split):
        self.inputs = list(bufs)
        self.split = list(split)
        self.out_shapes = [jax.ShapeDtypeStruct(b.shape, b.dtype) for b in bufs]
        self.aliases = {i: i for i in range(len(bufs))}
        self.sems = [pltpu.SemaphoreType.DMA((len(bufs), 3))] * 4

    def _rows(self, k, buf, c, mine=True):
        r = buf.shape[1]
        if not self.split[k]:
            return pl.ds(0, r)
        return pl.ds((c if mine else 1 - c) * (r // 2), r // 2)

    def _ici(self, k, d, bufs, sems, c, px, py, block):
        rows = self._rows(k, bufs[k], c)
        return pltpu.make_async_remote_copy(
            src_ref=bufs[k].at[block, rows, :], dst_ref=bufs[k].at[block, rows, :], send_sem=sems[0].at[k, d],
            recv_sem=sems[1].at[k, d], device_id=(px, py, c), device_id_type=MESH)

    def _pass(self, k, d, bufs, sems, x, y, c, block, mine):
        rows = self._rows(k, bufs[k], c, mine)
        return pltpu.make_async_remote_copy(
            src_ref=bufs[k].at[block, rows, :], dst_ref=bufs[k].at[block, rows, :], send_sem=sems[2].at[k, d],
            recv_sem=sems[3].at[k, d], device_id=(x, y, 1 - c), device_id_type=MESH)

    def first(self, in_refs, bufs, sems):
        x, y, c = _place()
        for k in range(len(bufs)):
            for d, (px, py, _) in enumerate(_other_chips(x, y)):
                self._ici(k, d, bufs, sems, c, px, py, 2 * x + y).start()

    def last(self, in_refs, bufs, sems):
        x, y, c = _place()
        chips = _other_chips(x, y)
        for k in range(len(bufs)):
            for d, (px, py, pj) in enumerate(chips):
                self._ici(k, d, bufs, sems, c, px, py, pj).wait_recv()
                if self.split[k]:
                    self._pass(k, d, bufs, sems, x, y, c, pj, True).start()
        for k in range(len(bufs)):
            for d, (px, py, pj) in enumerate(chips):
                if self.split[k]:
                    self._pass(k, d, bufs, sems, x, y, c, pj, False).wait_recv()
                    self._pass(k, d, bufs, sems, x, y, c, pj, True).wait_send()
                self._ici(k, d, bufs, sems, c, px, py, 2 * x + y).wait_send()


def _hosted_call(body, rider, *, name, grid, in_specs, out_specs, out_shape, scratch_shapes, compiler_params, args):
    if rider is None:
        outs = pl.pallas_call(body, name=name, grid=grid, in_specs=in_specs, out_specs=out_specs, out_shape=out_shape,
                              scratch_shapes=scratch_shapes, compiler_params=compiler_params)(*args)
        return list(outs), []
    n_in, n_out, n_scr = len(in_specs), len(out_specs), len(scratch_shapes)
    r_in, r_out = len(rider.inputs), len(rider.out_shapes)
    compiler_params = _cp(compiler_params.vmem_limit_bytes, ("arbitrary",) * len(grid))

    def full_body(*refs):
        ins = refs[:n_in]
        rins = refs[n_in:n_in + r_in]
        outs = refs[n_in + r_in:n_in + r_in + n_out]
        routs = refs[n_in + r_in + n_out:n_in + r_in + n_out + r_out]
        rest = refs[n_in + r_in + n_out + r_out:]
        scr, sems = rest[:n_scr], rest[n_scr:]
        ids = [pl.program_id(a) for a in range(len(grid))]
        is_first = functools.reduce(jnp.logical_and, [i == 0 for i in ids])
        is_last = functools.reduce(jnp.logical_and, [i == g - 1 for i, g in zip(ids, grid)])

        @pl.when(is_first)
        def _():
            rider.first(rins, routs, sems)

        body(*ins, *outs, *scr)

        @pl.when(is_last)
        def _():
            rider.last(rins, routs, sems)

    res = pl.pallas_call(
        full_body, name=name, grid=grid, in_specs=list(in_specs) + [ANY] * r_in,
        out_specs=list(out_specs) + [ANY] * r_out, out_shape=list(out_shape) + list(rider.out_shapes),
        input_output_aliases={n_in + i: n_out + o for i, o in rider.aliases.items()},
        scratch_shapes=list(scratch_shapes) + list(rider.sems), compiler_params=compiler_params,
    )(*args, *rider.inputs)
    return list(res[:n_out]), list(res[n_out:])


def _exchange_now(rider, *, name):
    r_in = len(rider.inputs)

    def body(*refs):
        rins = refs[:r_in]
        routs = refs[r_in:r_in + len(rider.out_shapes)]
        sems = refs[r_in + len(rider.out_shapes):]
        rider.first(rins, routs, sems)
        rider.last(rins, routs, sems)

    return pl.pallas_call(
        body, name=name, in_specs=[ANY] * r_in, out_specs=[ANY] * len(rider.out_shapes), out_shape=list(rider.out_shapes),
        input_output_aliases=dict(rider.aliases), scratch_shapes=list(rider.sems),
    )(*rider.inputs)


def _to_slot(where, a, dtype, *, name):
    r, cols = a.shape
    tr = _tile(r, 256, 16) if r > 256 else r

    def body(w_ref, a_ref, o_ref):
        o_ref[0] = a_ref[...].astype(o_ref.dtype)

    return pl.pallas_call(
        body, name=name,
        grid_spec=pltpu.PrefetchScalarGridSpec(
            num_scalar_prefetch=1, grid=(r // tr,),
            in_specs=[pl.BlockSpec((tr, cols), lambda i, w: (i, 0))],
            out_specs=pl.BlockSpec((1, tr, cols), lambda i, w: (w[1], i, 0))),
        out_shape=jax.ShapeDtypeStruct((N_CHIPS, r, cols), dtype), compiler_params=_cp(VMEM_BIG),
    )(where, a)


def _tri_inv(a_strict):
    r, c = _masks64()
    eye = (r == c).astype(F32)
    blk16 = (r // 16) == (c // 16)
    blk32 = (r // 32) == (c // 32)
    ld = jnp.where(blk16, a_strict, 0.0)
    x = eye - ld
    p = _nn(ld, ld)
    yield
    for step in range(3):
        xp = _nn(x, p)
        if step < 2:
            p = _nn(p, p)
        x = x + xp
        yield
    for lk in (jnp.where(jnp.logical_and(blk32, jnp.logical_not(blk16)), a_strict, 0.0),
               jnp.where(blk32, 0.0, a_strict)):
        y = x - eye
        s = lk + _nn(y, lk)
        yield
        x = x - s - _nn(s, y)
        yield
    return x


def _dn_gates(sa, alog, dtb, chunk_in_seq):
    rows = _iota2((CHUNK, LANE), 0)
    valid = jnp.logical_or(rows >= N_PAD, chunk_in_seq > 0)
    beta_t = _sigmoid(sa)
    ea = jnp.exp(alog)
    g_t = jnp.where(valid, -ea * _softplus(sa + dtb), 0.0)
    r, c = _masks64()
    ltri = (r >= c).astype(F32)
    gam_t = _tri_sum(ltri, g_t)
    return beta_t, g_t, gam_t, valid, ea


def _dn_intra_fwd(qn, kn, v, projp, alog_row, dtb_row, *, nc_seq, name, rider=None):
    n = qn.shape[0]
    nct = n // CHUNK
    hw = NH * DN_D
    scale = DN_D ** -0.5

    grp = _group(nc_seq)

    def inner(gi, q_ref, k_ref, v_ref, sa_ref, al_ref, dt_ref, u_ref, w_ref, qg_ref, kd_ref, p_ref, t_ref, gl_ref):
        ci = (pl.program_id(0) * grp + gi) % nc_seq
        beta_t, _, gam_t, _, _ = _dn_gates(sa_ref[...], al_ref[...], dt_ref[...], ci)
        yield
        gam_tt = gam_t.T
        r, c = _masks64()
        incl = r >= c
        strict = r > c

        def head(h):
            sl = slice(h * DN_D, (h + 1) * DN_D)
            beta_w = jnp.broadcast_to(beta_t[:, h:h + 1], (CHUNK, DN_D))
            gam_w = jnp.broadcast_to(gam_t[:, 4 + h:5 + h], (CHUNK, DN_D))
            gam_row = gam_tt[4 + h:5 + h, :]
            gl = gam_t[CHUNK - 1:CHUNK, 4 + h:5 + h]
            dec = jnp.exp(jnp.where(incl, gam_w[:, 0:CHUNK] - gam_row, -jnp.inf))
            kh = k_ref[:, sl].astype(F32)
            qh = q_ref[:, sl].astype(F32) * scale
            vh = v_ref[:, sl].astype(F32)
            kk = _nt(kh, kh)
            qk = _nt(qh, kh)
            yield
            a = jnp.where(strict, beta_w[:, 0:CHUNK] * kk * dec, 0.0)
            tm = yield from _tri_inv(a)
            egam_w = jnp.exp(gam_w)
            u_ref[:, sl] = _nn(tm, beta_w * vh).astype(u_ref.dtype)
            w_ref[:, sl] = _nn(tm, (beta_w * egam_w) * kh).astype(w_ref.dtype)
            qg_ref[:, sl] = (egam_w * qh).astype(qg_ref.dtype)
            kd_ref[:, sl] = (jnp.exp(gl - gam_w) * kh).astype(kd_ref.dtype)
            p_ref[0, h] = qk * dec
            t_ref[0, h] = tm
            gl_ref[0, h:h + 1, :] = jnp.broadcast_to(jnp.exp(gl), (1, LANE))

        yield from _round_robin([head(h) for h in range(NH)])

    rows = grp * CHUNK
    row = pl.BlockSpec((rows, hw), lambda i: (i, 0))
    vec = pl.BlockSpec((1, LANE), lambda i: (0, 0))
    mat = pl.BlockSpec((grp, NH, CHUNK, CHUNK), lambda i: (i, 0, 0, 0))
    big = jax.ShapeDtypeStruct((n, hw), BF16)
    msd = jax.ShapeDtypeStruct((nct, NH, CHUNK, CHUNK), F32)
    kinds = ["row"] * 4 + ["whole"] * 2 + ["row"] * 4 + ["lead"] * 3
    outs, ridden = _hosted_call(
        _per_chunk(inner, kinds, grp), rider, name=name, grid=(nct // grp,),
        in_specs=[row, row, row, pl.BlockSpec((rows, LANE), lambda i: (i, 0)), vec, vec],
        out_specs=[row, row, row, row, mat, mat, pl.BlockSpec((grp, NH, LANE), lambda i: (i, 0, 0))],
        out_shape=[big, big, big, big, msd, msd, jax.ShapeDtypeStruct((nct, NH, LANE), F32)],
        scratch_shapes=[], compiler_params=_cp(VMEM_BIG, ("arbitrary",)),
        args=(qn, kn, v, projp, alog_row, dtb_row))
    return (*outs, ridden)


def _dn_scan_fwd(u, w, qg, kd, p, gl, *, bsz, nc_seq, name, rider=None):
    hw = NH * DN_D
    t_seq = nc_seq * CHUNK
    u, w, qg, kd = (z.reshape(bsz, t_seq, hw) for z in (u, w, qg, kd))
    p = p.reshape(bsz, nc_seq, NH, CHUNK, CHUNK)
    gl = gl.reshape(bsz, nc_seq, NH, LANE)
    grp = _group(nc_seq)

    def body(u_ref, w_ref, qg_ref, kd_ref, p_ref, gl_ref, o_ref, vn_ref, hist_ref, s_ref):
        @pl.when(pl.program_id(0) == 0)
        def _():
            s_ref[...] = jnp.zeros_like(s_ref)

        def chain(b, h, gi):
            sl = slice(h * DN_D, (h + 1) * DN_D)
            rows = pl.ds(gi * CHUNK, CHUNK)
            s = s_ref[b, h]
            hist_ref[b, gi, h] = s.astype(hist_ref.dtype)
            ws = _nn(w_ref[b, rows, sl], s)
            qs = _nn(qg_ref[b, rows, sl], s)
            yield
            vn = u_ref[b, rows, sl] - ws
            vn_ref[b, rows, sl] = vn.astype(vn_ref.dtype)
            o_ref[b, rows, sl] = (qs + _nn(p_ref[b, gi, h], vn)).astype(o_ref.dtype)
            s_ref[b, h] = gl_ref[b, gi, h:h + 1, :] * s + _tn(kd_ref[b, rows, sl], vn)

        for gi in range(grp):
            _run([chain(b, h, gi) for b in range(bsz) for h in range(NH)])

    row = pl.BlockSpec((bsz, grp * CHUNK, hw), lambda i: (0, i, 0))
    outs, ridden = _hosted_call(
        body, rider, name=name, grid=(nc_seq // grp,),
        in_specs=[row, row, row, row, pl.BlockSpec((bsz, grp, NH, CHUNK, CHUNK), lambda i: (0, i, 0, 0, 0)),
                  pl.BlockSpec((bsz, grp, NH, LANE), lambda i: (0, i, 0, 0))],
        out_specs=[row, row, pl.BlockSpec((bsz, grp, NH, DN_D, DN_D), lambda i: (0, i, 0, 0, 0))],
        out_shape=[jax.ShapeDtypeStruct((bsz, t_seq, hw), BF16), jax.ShapeDtypeStruct((bsz, t_seq, hw), BF16),
                   jax.ShapeDtypeStruct((bsz, nc_seq, NH, DN_D, DN_D), F32)],
        scratch_shapes=[pltpu.VMEM((bsz, NH, DN_D, DN_D), F32)],
        compiler_params=_cp(VMEM_BIG, ("arbitrary",)), args=(u, w, qg, kd, p, gl))
    o, vn, hist = outs
    return o.reshape(bsz * t_seq, hw), vn.reshape(bsz * t_seq, hw), hist, ridden


def _dn_scan_bwd(do, w, qg, kd, vn, p, gl, hist, *, bsz, nc_seq, name):
    hw = NH * DN_D
    t_seq = nc_seq * CHUNK
    do, w, qg, kd, vn = (z.reshape(bsz, t_seq, hw) for z in (do, w, qg, kd, vn))
    p = p.reshape(bsz, nc_seq, NH, CHUNK, CHUNK)
    gl = gl.reshape(bsz, nc_seq, NH, LANE)
    grp = _group(nc_seq)

    def body(do_ref, w_ref, qg_ref, kd_ref, vn_ref, p_ref, gl_ref, hist_ref,
             du_ref, dw_ref, dqg_ref, dkd_ref, dgl_ref, ds_ref):
        @pl.when(pl.program_id(0) == 0)
        def _():
            ds_ref[...] = jnp.zeros_like(ds_ref)

        def chain(b, h, gi):
            sl = slice(h * DN_D, (h + 1) * DN_D)
            rows = pl.ds(gi * CHUNK, CHUNK)
            s = hist_ref[b, gi, h]
            dsn = ds_ref[b, h]
            doh = do_ref[b, rows, sl]
            vnh = vn_ref[b, rows, sl]
            kdh = kd_ref[b, rows, sl]
            dvn = _tn(p_ref[b, gi, h], doh) + _nn(kdh, dsn)
            du_ref[b, rows, sl] = dvn.astype(du_ref.dtype)
            dqg_ref[b, rows, sl] = _nt(doh, s).astype(dqg_ref.dtype)
            dkd_ref[b, rows, sl] = _nt(vnh, dsn).astype(dkd_ref.dtype)
            ds_part = _tn(qg_ref[b, rows, sl], doh) + gl_ref[b, gi, h:h + 1, :] * dsn
            dgl = jnp.sum(jnp.sum(dsn * s, axis=0, keepdims=True), axis=1, keepdims=True)
            dgl_ref[b, gi, h:h + 1, :] = jnp.broadcast_to(dgl, (1, LANE))
            yield
            dw_ref[b, rows, sl] = (-_nt(dvn, s)).astype(dw_ref.dtype)
            ds_ref[b, h] = ds_part - _tn(w_ref[b, rows, sl], dvn)

        for gi in reversed(range(grp)):
            _run([chain(b, h, gi) for b in range(bsz) for h in range(NH)])

    steps = nc_seq // grp
    rev = lambda i: steps - 1 - i
    row = pl.BlockSpec((bsz, grp * CHUNK, hw), lambda i: (0, rev(i), 0))
    mat = pl.BlockSpec((bsz, grp, NH, CHUNK, CHUNK), lambda i: (0, rev(i), 0, 0, 0))
    glb = pl.BlockSpec((bsz, grp, NH, LANE), lambda i: (0, rev(i), 0, 0))
    big = jax.ShapeDtypeStruct((bsz, t_seq, hw), BF16)
    outs = pl.pallas_call(
        body, name=name, grid=(steps,),
        in_specs=[row, row, row, row, row, mat, glb,
                  pl.BlockSpec((bsz, grp, NH, DN_D, DN_D), lambda i: (0, rev(i), 0, 0, 0))],
        out_specs=[row, row, row, row, glb],
        out_shape=[big, big, jax.ShapeDtypeStruct(big.shape, F32), jax.ShapeDtypeStruct(big.shape, F32),
                   jax.ShapeDtypeStruct((bsz, nc_seq, NH, LANE), F32)],
        scratch_shapes=[pltpu.VMEM((bsz, NH, DN_D, DN_D), F32)],
        compiler_params=_cp(VMEM_BIG, ("arbitrary",)),
    )(do, w, qg, kd, vn, p, gl, hist)
    du, dw, dqg, dkd, dgl = outs
    n = bsz * t_seq
    return (du.reshape(n, hw), dw.reshape(n, hw), dqg.reshape(n, hw), dkd.reshape(n, hw),
            dgl.reshape(bsz * nc_seq, NH, LANE))


def _dn_intra_bwd(qn, kn, v, projp, alog_row, dtb_row, u, w, tmat, du, dw, dqg, dkd, do, vn, dgl, *, nc_seq, name,
                  rider=None):
    n = qn.shape[0]
    nct = n // CHUNK
    hw = NH * DN_D
    scale = DN_D ** -0.5

    grp = _group(nc_seq)

    def inner(gi, q_ref, k_ref, v_ref, sa_ref, al_ref, dt_ref, u_ref, w_ref, t_ref, du_ref, dw_ref, dqg_ref, dkd_ref,
              do_ref, vn_ref, dgl_ref, dq_ref, dk_ref, dv_ref, dsa_ref, dal_ref, ddt_ref):
        ci = (pl.program_id(0) * grp + gi) % nc_seq
        sa = sa_ref[...]
        beta_t, g_t, gam_t, valid, ea = _dn_gates(sa, al_ref[...], dt_ref[...], ci)
        yield
        lane = _iota2((CHUNK, LANE), 1)
        gates_t = jnp.where(lane < 4, beta_t, gam_t).T
        r, c = _masks64()
        incl, strict, upper, supper = r >= c, r > c, r <= c, r < c
        rows1 = _iota2((CHUNK, 1), 0)
        acc = [jnp.zeros((CHUNK, LANE), F32)]

        def head(h):
            sl = slice(h * DN_D, (h + 1) * DN_D)
            beta_w = jnp.broadcast_to(beta_t[:, h:h + 1], (CHUNK, DN_D))
            gam_w = jnp.broadcast_to(gam_t[:, 4 + h:5 + h], (CHUNK, DN_D))
            beta_s, gam_s = beta_w[:, 0:CHUNK], gam_w[:, 0:CHUNK]
            beta_row = gates_t[h:h + 1, :]
            gam_row = gates_t[4 + h:5 + h, :]
            gl = gam_t[CHUNK - 1:CHUNK, 4 + h:5 + h]
            dec = jnp.exp(jnp.where(incl, gam_s - gam_row, -jnp.inf))
            dec_t = jnp.exp(jnp.where(upper, gam_row - gam_s, -jnp.inf))
            egam_w = jnp.exp(gam_w)
            ekd_w = jnp.exp(gl - gam_w)
            kh = k_ref[:, sl].astype(F32)
            qh = q_ref[:, sl].astype(F32) * scale
            vh = v_ref[:, sl].astype(F32)
            uh = u_ref[:, sl]
            wh = w_ref[:, sl]
            doh = do_ref[:, sl]
            vnh = vn_ref[:, sl]
            kk = _nt(kh, kh)
            qk = _nt(qh, kh)
            qk_t = _nt(kh, qh)
            dp = _nt(doh, vnh)
            dp_t = _nt(vnh, doh)
            t_hi, t_lo = _split(t_ref[0, h].T)
            duh, dwh = du_ref[:, sl], dw_ref[:, sl]
            dvb = _nn(t_hi, duh) + _nn(t_lo, duh)
            dkg = _nn(t_hi, dwh) + _nn(t_lo, dwh)
            yield
            dvb_hi, dvb_lo = _split(dvb)
            dkg_hi, dkg_lo = _split(dkg)
            m = (_nt(dvb_hi, uh) + _nt(dvb_lo, uh)) + (_nt(dkg_hi, wh) + _nt(dkg_lo, wh))
            m_t = (_nt(uh, dvb_hi) + _nt(uh, dvb_lo)) + (_nt(wh, dkg_hi) + _nt(wh, dkg_lo))
            yield
            da = jnp.where(strict, -m, 0.0)
            da_t = jnp.where(supper, -m_t, 0.0)
            a = jnp.where(strict, beta_s * kk * dec, 0.0)
            a_t = jnp.where(supper, beta_row * kk * dec_t, 0.0)
            dad = da * dec
            dad_t = da_t * dec_t
            dpm = jnp.where(incl, dp, 0.0)
            dpm_t = jnp.where(upper, dp_t, 0.0)
            e = da * a + dpm * (qk * dec)
            e_t = da_t * a_t + dpm_t * (qk_t * dec_t)
            dqgh = dqg_ref[:, sl].astype(F32)
            dkdh = dkd_ref[:, sl].astype(F32)
            bg_w = beta_w * egam_w
            dkh = (_nn(beta_s * dad, kh) + _nn(beta_row * dad_t, kh) + _nn(dpm_t * dec_t, qh)
                   + bg_w * dkg + ekd_w * dkdh)
            dqh = _nn(dpm * dec, kh) + egam_w * dqgh
            t_kd = dkdh * (ekd_w * kh)
            dbeta = (jnp.sum(dad * kk, axis=1, keepdims=True)
                     + jnp.sum(dkg * (egam_w * kh) + dvb * vh, axis=1, keepdims=True))
            dgam = (jnp.sum(e - e_t, axis=1, keepdims=True)
                    + jnp.sum(dkg * (bg_w * kh) + dqgh * (egam_w * qh) - t_kd, axis=1, keepdims=True))
            dgam_last = (jnp.sum(jnp.sum(t_kd, axis=0, keepdims=True), axis=1, keepdims=True)
                         + dgl_ref[0, h:h + 1, 0:1] * jnp.exp(gl))
            dgam = dgam + jnp.where(rows1 == CHUNK - 1, dgam_last, 0.0)
            dq_ref[:, sl] = (dqh * scale).astype(dq_ref.dtype)
            dk_ref[:, sl] = dkh.astype(dk_ref.dtype)
            dv_ref[:, sl] = (beta_w * dvb).astype(dv_ref.dtype)
            acc[0] = acc[0] + jnp.where(lane == h, dbeta, 0.0) + jnp.where(lane == 4 + h, dgam, 0.0)

        yield from _round_robin([head(h) for h in range(NH)])
        acc_t = acc[0]
        dg_t = _tri_sum(upper, acc_t)
        ddb = acc_t * beta_t * (1.0 - beta_t)
        dda = jnp.where(valid, dg_t * (-ea) * _sigmoid(sa + dt_ref[...]), 0.0)
        dsa_ref[...] = jnp.where(lane < 4, ddb, jnp.where(lane < 8, dda, 0.0)).astype(dsa_ref.dtype)
        in_g = jnp.logical_and(lane >= 4, lane < 8)
        dal = jnp.sum(jnp.where(in_g, dg_t * g_t, 0.0), axis=0, keepdims=True)
        ddt = jnp.sum(jnp.where(in_g, dda, 0.0), axis=0, keepdims=True)
        _accumulate(dal_ref, dal, gi)
        _accumulate(ddt_ref, ddt, gi)

    rows = grp * CHUNK
    row = pl.BlockSpec((rows, hw), lambda i: (i, 0))
    vec = pl.BlockSpec((1, LANE), lambda i: (0, 0))
    mat = pl.BlockSpec((grp, NH, CHUNK, CHUNK), lambda i: (i, 0, 0, 0))
    glb = pl.BlockSpec((grp, NH, LANE), lambda i: (i, 0, 0))
    big = jax.ShapeDtypeStruct((n, hw), F32)
    v128 = jax.ShapeDtypeStruct((1, LANE), F32)
    kinds = (["row"] * 4 + ["whole"] * 2 + ["row"] * 2 + ["lead"] + ["row"] * 6 + ["lead"]
             + ["row"] * 4 + ["whole"] * 2)
    outs, ridden = _hosted_call(
        _per_chunk(inner, kinds, grp), rider, name=name, grid=(nct // grp,),
        in_specs=[row, row, row, pl.BlockSpec((rows, LANE), lambda i: (i, 0)), vec, vec,
                  row, row, mat, row, row, row, row, row, row, glb],
        out_specs=[row, row, row, pl.BlockSpec((rows, LANE), lambda i: (i, 0)), vec, vec],
        out_shape=[big, big, big, jax.ShapeDtypeStruct((n, LANE), BF16), v128, v128],
        scratch_shapes=[], compiler_params=_cp(VMEM_BIG, ("arbitrary",)),
        args=(qn, kn, v, projp, alog_row, dtb_row, u, w, tmat, du, dw, dqg, dkd, do, vn, dgl))
    return (*outs, ridden)


GQ_W = NH * GLA_DK
GV_W = NH * GLA_DV
GLA_NORM = 16.0
MID = CHUNK // 2


def _gla_gates(sb, w2p, gb, chunk_in_seq):
    rows = _iota2((CHUNK, GQ_W), 0)
    valid = jnp.logical_or(rows >= N_PAD, chunk_in_seq > 0)
    graw = _nn(sb, w2p) + gb
    yield
    g = jnp.where(valid, _logsigmoid(graw) * (1.0 / GLA_NORM), 0.0)
    r, c = _masks64()
    bcum = _tri_sum(r >= c, g)
    yield
    return graw, bcum, valid


def _head_mask(h):
    lane = _iota2((1, GQ_W), 1)
    return jnp.logical_and(lane >= h * GLA_DK, lane < (h + 1) * GLA_DK)


def _gla_intra_fwd(projp, gates, w2p, gb, *, nc_seq, name):
    n = projp.shape[0]
    nct = n // CHUNK
    scale = GLA_DK ** -0.5

    grp = _group(nc_seq)
    rows = grp * CHUNK

    def inner(gi, qk_ref, v_ref, sb_ref, w2_ref, gb_ref, oi_ref, qg_ref, kd_ref, gl_ref):
        ci = (pl.program_id(0) * grp + gi) % nc_seq
        _, bc, _ = yield from _gla_gates(sb_ref[...], w2_ref[...], gb_ref[...], ci)
        bref = bc[MID:MID + 1, :]
        bl = bc[CHUNK - 1:CHUNK, :]
        q = qk_ref[:, 0:GQ_W].astype(F32) * scale
        k = qk_ref[:, GQ_W:2 * GQ_W].astype(F32)
        qi = q * jnp.exp(bc - bref)
        ki = k * jnp.exp(bref - bc)
        qg_ref[...] = (q * jnp.exp(bc)).astype(qg_ref.dtype)
        kd_ref[...] = (k * jnp.exp(bl - bc)).astype(kd_ref.dtype)
        gl_ref[0] = jnp.exp(bl)
        r, c = _masks64()
        incl = r >= c
        a = [jnp.where(incl, _nt(jnp.where(_head_mask(h), qi, 0.0), ki), 0.0) for h in range(NH)]
        yield
        for h in range(NH):
            oi_ref[:, h * GLA_DV:(h + 1) * GLA_DV] = _nn(a[h], v_ref[:, h * GLA_DV:(h + 1) * GLA_DV]).astype(oi_ref.dtype)

    kinds = ["row"] * 3 + ["whole"] * 2 + ["row"] * 3 + ["lead"]
    return pl.pallas_call(
        _per_chunk(inner, kinds, grp), name=name, grid=(nct // grp,),
        in_specs=[pl.BlockSpec((rows, 2 * GQ_W), lambda i: (i, C_GQK // (2 * GQ_W))),
                  pl.BlockSpec((rows, GV_W), lambda i: (i, C_GV // GV_W)),
                  pl.BlockSpec((rows, LANE), lambda i: (i, 1)),
                  pl.BlockSpec((LANE, GQ_W), lambda i: (0, 0)), pl.BlockSpec((1, GQ_W), lambda i: (0, 0))],
        out_specs=[pl.BlockSpec((rows, GV_W), lambda i: (i, 0)), pl.BlockSpec((rows, GQ_W), lambda i: (i, 0)),
                   pl.BlockSpec((rows, GQ_W), lambda i: (i, 0)), pl.BlockSpec((grp, 1, GQ_W), lambda i: (i, 0, 0))],
        out_shape=[jax.ShapeDtypeStruct((n, GV_W), BF16), jax.ShapeDtypeStruct((n, GQ_W), BF16),
                   jax.ShapeDtypeStruct((n, GQ_W), BF16), jax.ShapeDtypeStruct((nct, 1, GQ_W), F32)],
        compiler_params=_cp(VMEM_BIG),
    )(projp, projp, gates, w2p, gb)


def _gla_scan_fwd(oi, qg, kd, gl, projp, *, bsz, nc_seq, name, rider=None):
    t_seq = nc_seq * CHUNK
    oi = oi.reshape(bsz, t_seq, GV_W)
    qg, kd = qg.reshape(bsz, t_seq, GQ_W), kd.reshape(bsz, t_seq, GQ_W)
    gl = gl.reshape(bsz, nc_seq, 1, GQ_W)
    pj = projp.reshape(bsz, t_seq, PW)
    grp = _group(nc_seq)

    def body(oi_ref, qg_ref, kd_ref, gl_ref, v_ref, o_ref, hist_ref, st_ref):
        @pl.when(pl.program_id(0) == 0)
        def _():
            st_ref[...] = jnp.zeros_like(st_ref)

        for gi in range(grp):
            rows = pl.ds(gi * CHUNK, CHUNK)
            for b in range(bsz):
                st = st_ref[b]
                hist_ref[b, gi] = st.astype(hist_ref.dtype)
                qgb = qg_ref[b, rows, :]
                kdb = kd_ref[b, rows, :]
                upd = jnp.zeros((GLA_DV, GQ_W), F32)
                for h in range(NH):
                    sl = slice(h * GLA_DV, (h + 1) * GLA_DV)
                    m = _head_mask(h)
                    o_ref[b, rows, sl] = (oi_ref[b, rows, sl] + _nt(jnp.where(m, qgb, 0.0), st)).astype(o_ref.dtype)
                    upd = upd + jnp.where(m, _tn(v_ref[b, rows, sl], kdb), 0.0)
                st_ref[b] = gl_ref[b, gi] * st + upd

    rws = grp * CHUNK
    outs, ridden = _hosted_call(
        body, rider, name=name, grid=(nc_seq // grp,),
        in_specs=[pl.BlockSpec((bsz, rws, GV_W), lambda i: (0, i, 0)),
                  pl.BlockSpec((bsz, rws, GQ_W), lambda i: (0, i, 0)),
                  pl.BlockSpec((bsz, rws, GQ_W), lambda i: (0, i, 0)),
                  pl.BlockSpec((bsz, grp, 1, GQ_W), lambda i: (0, i, 0, 0)),
                  pl.BlockSpec((bsz, rws, GV_W), lambda i: (0, i, C_GV // GV_W))],
        out_specs=[pl.BlockSpec((bsz, rws, GV_W), lambda i: (0, i, 0)),
                   pl.BlockSpec((bsz, grp, GLA_DV, GQ_W), lambda i: (0, i, 0, 0))],
        out_shape=[jax.ShapeDtypeStruct((bsz, t_seq, GV_W), BF16),
                   jax.ShapeDtypeStruct((bsz, nc_seq, GLA_DV, GQ_W), F32)],
        scratch_shapes=[pltpu.VMEM((bsz, GLA_DV, GQ_W), F32)],
        compiler_params=_cp(VMEM_BIG, ("arbitrary",)), args=(oi, qg, kd, gl, pj))
    return outs[0].reshape(bsz * t_seq, GV_W), outs[1], ridden


def _gla_scan_bwd(do, qg, kd, gl, projp, hist, *, bsz, nc_seq, name):
    t_seq = nc_seq * CHUNK
    do = do.reshape(bsz, t_seq, GV_W)
    qg, kd = qg.reshape(bsz, t_seq, GQ_W), kd.reshape(bsz, t_seq, GQ_W)
    gl = gl.reshape(bsz, nc_seq, 1, GQ_W)
    pj = projp.reshape(bsz, t_seq, PW)
    grp = _group(nc_seq)

    def body(do_ref, qg_ref, kd_ref, gl_ref, v_ref, hist_ref, dqg_ref, dkd_ref, dv_ref, dgl_ref, dst_ref):
        @pl.when(pl.program_id(0) == 0)
        def _():
            dst_ref[...] = jnp.zeros_like(dst_ref)

        for gi in reversed(range(grp)):
            rows = pl.ds(gi * CHUNK, CHUNK)
            for b in range(bsz):
                st = hist_ref[b, gi]
                dst = dst_ref[b]
                qgb = qg_ref[b, rows, :]
                kdb = kd_ref[b, rows, :]
                dqg = jnp.zeros((CHUNK, GQ_W), F32)
                dkd = jnp.zeros((CHUNK, GQ_W), F32)
                add = jnp.zeros((GLA_DV, GQ_W), F32)
                for h in range(NH):
                    sl = slice(h * GLA_DV, (h + 1) * GLA_DV)
                    m = _head_mask(h)
                    doh = do_ref[b, rows, sl]
                    vh = v_ref[b, rows, sl]
                    dqg = dqg + jnp.where(m, _nn(doh, st), 0.0)
                    dkd = dkd + jnp.where(m, _nn(vh, dst), 0.0)
                    dv_ref[b, rows, sl] = _nt(jnp.where(m, kdb, 0.0), dst).astype(dv_ref.dtype)
                    add = add + jnp.where(m, _tn(doh, qgb), 0.0)
                dqg_ref[b, rows, :] = dqg.astype(dqg_ref.dtype)
                dkd_ref[b, rows, :] = dkd.astype(dkd_ref.dtype)
                dgl_ref[b, gi] = jnp.sum(dst * st, axis=0, keepdims=True)
                dst_ref[b] = gl_ref[b, gi] * dst + add

    steps = nc_seq // grp
    rws = grp * CHUNK
    rev = lambda i: steps - 1 - i
    outs = pl.pallas_call(
        body, name=name, grid=(steps,),
        in_specs=[pl.BlockSpec((bsz, rws, GV_W), lambda i: (0, rev(i), 0)),
                  pl.BlockSpec((bsz, rws, GQ_W), lambda i: (0, rev(i), 0)),
                  pl.BlockSpec((bsz, rws, GQ_W), lambda i: (0, rev(i), 0)),
                  pl.BlockSpec((bsz, grp, 1, GQ_W), lambda i: (0, rev(i), 0, 0)),
                  pl.BlockSpec((bsz, rws, GV_W), lambda i: (0, rev(i), C_GV // GV_W)),
                  pl.BlockSpec((bsz, grp, GLA_DV, GQ_W), lambda i: (0, rev(i), 0, 0))],
        out_specs=[pl.BlockSpec((bsz, rws, GQ_W), lambda i: (0, rev(i), 0)),
                   pl.BlockSpec((bsz, rws, GQ_W), lambda i: (0, rev(i), 0)),
                   pl.BlockSpec((bsz, rws, GV_W), lambda i: (0, rev(i), 0)),
                   pl.BlockSpec((bsz, grp, 1, GQ_W), lambda i: (0, rev(i), 0, 0))],
        out_shape=[jax.ShapeDtypeStruct((bsz, t_seq, GQ_W), F32), jax.ShapeDtypeStruct((bsz, t_seq, GQ_W), F32),
                   jax.ShapeDtypeStruct((bsz, t_seq, GV_W), BF16), jax.ShapeDtypeStruct((bsz, nc_seq, 1, GQ_W), F32)],
        scratch_shapes=[pltpu.VMEM((bsz, GLA_DV, GQ_W), F32)],
        compiler_params=_cp(VMEM_BIG, ("arbitrary",)),
    )(do, qg, kd, gl, pj, hist)
    n = bsz * t_seq
    return (outs[0].reshape(n, GQ_W), outs[1].reshape(n, GQ_W), outs[2].reshape(n, GV_W),
            outs[3].reshape(bsz * nc_seq, 1, GQ_W))


def _gla_intra_bwd(projp, gates, w2p, gb, do, dqg, dkd, dvi, dgl, *, nc_seq, name):
    n = projp.shape[0]
    nct = n // CHUNK
    scale = GLA_DK ** -0.5

    grp = _group(nc_seq)
    rows = grp * CHUNK

    def inner(gi, qk_ref, v_ref, sb_ref, w2_ref, gb_ref, do_ref, dqg_ref, dkd_ref, dvi_ref, dgl_ref,
              dqk_ref, dv_ref, dsb_ref, dw2_ref, dgb_ref):
        ci = (pl.program_id(0) * grp + gi) % nc_seq
        sb = sb_ref[...]
        w2 = w2_ref[...]
        graw, bc, valid = yield from _gla_gates(sb, w2, gb_ref[...], ci)
        bref = bc[MID:MID + 1, :]
        bl = bc[CHUNK - 1:CHUNK, :]
        q = qk_ref[:, 0:GQ_W].astype(F32) * scale
        k = qk_ref[:, GQ_W:2 * GQ_W].astype(F32)
        ex1 = jnp.exp(bc - bref)
        ex2 = jnp.exp(bref - bc)
        eb = jnp.exp(bc)
        ekd = jnp.exp(bl - bc)
        qi, ki = q * ex1, k * ex2
        r, c = _masks64()
        incl = r >= c
        upper = r <= c
        a_t, da, da_t = [], [], []
        for h in range(NH):
            sl = slice(h * GLA_DV, (h + 1) * GLA_DV)
            doh = do_ref[:, sl]
            vh = v_ref[:, sl]
            a_t.append(jnp.where(upper, _nt(jnp.where(_head_mask(h), ki, 0.0), qi), 0.0))
            da.append(jnp.where(incl, _nt(doh, vh), 0.0))
            da_t.append(jnp.where(upper, _nt(vh, doh), 0.0))
        yield
        dqi = jnp.zeros((CHUNK, GQ_W), F32)
        dki = jnp.zeros((CHUNK, GQ_W), F32)
        for h in range(NH):
            sl = slice(h * GLA_DV, (h + 1) * GLA_DV)
            m = _head_mask(h)
            dv_ref[:, sl] = (_nn(a_t[h], do_ref[:, sl]) + dvi_ref[:, sl]).astype(dv_ref.dtype)
            dqi = dqi + jnp.where(m, _nn(da[h], ki), 0.0)
            dki = dki + jnp.where(m, _nn(da_t[h], qi), 0.0)
        yield
        dqg = dqg_ref[...].astype(F32)
        dkd = dkd_ref[...].astype(F32)
        dqk_ref[:, 0:GQ_W] = ((dqi * ex1 + dqg * eb) * scale).astype(dqk_ref.dtype)
        dqk_ref[:, GQ_W:2 * GQ_W] = (dki * ex2 + dkd * ekd).astype(dqk_ref.dtype)
        t_qi, t_ki, t_kd = dqi * qi, dki * ki, dkd * (k * ekd)
        db = t_qi - t_ki + dqg * (q * eb) - t_kd
        dbref = jnp.sum(t_ki - t_qi, axis=0, keepdims=True)
        dbl = jnp.sum(t_kd, axis=0, keepdims=True) + dgl_ref[0] * jnp.exp(bl)
        rows = _iota2((CHUNK, GQ_W), 0)
        db = db + jnp.where(rows == MID, dbref, 0.0) + jnp.where(rows == CHUNK - 1, dbl, 0.0)
        dg = _tri_sum(upper, db)
        yield
        dgraw = jnp.where(valid, dg * (1.0 / GLA_NORM) * _sigmoid(-graw), 0.0)
        dsb_ref[...] = _nt(dgraw, w2).astype(dsb_ref.dtype)
        dw2 = _tn(sb, dgraw)
        dgb = jnp.sum(dgraw, axis=0, keepdims=True)
        _accumulate(dw2_ref, dw2, gi)
        _accumulate(dgb_ref, dgb, gi)

    rq = pl.BlockSpec((rows, GQ_W), lambda i: (i, 0))
    rv = pl.BlockSpec((rows, GV_W), lambda i: (i, 0))
    kinds = ["row"] * 3 + ["whole"] * 2 + ["row"] * 4 + ["lead"] + ["row"] * 3 + ["whole"] * 2
    return pl.pallas_call(
        _per_chunk(inner, kinds, grp), name=name, grid=(nct // grp,),
        in_specs=[pl.BlockSpec((rows, 2 * GQ_W), lambda i: (i, C_GQK // (2 * GQ_W))),
                  pl.BlockSpec((rows, GV_W), lambda i: (i, C_GV // GV_W)),
                  pl.BlockSpec((rows, LANE), lambda i: (i, 1)),
                  pl.BlockSpec((LANE, GQ_W), lambda i: (0, 0)), pl.BlockSpec((1, GQ_W), lambda i: (0, 0)),
                  rv, rq, rq, rv, pl.BlockSpec((grp, 1, GQ_W), lambda i: (i, 0, 0))],
        out_specs=[pl.BlockSpec((rows, 2 * GQ_W), lambda i: (i, 0)), rv, pl.BlockSpec((rows, LANE), lambda i: (i, 0)),
                   pl.BlockSpec((LANE, GQ_W), lambda i: (0, 0)), pl.BlockSpec((1, GQ_W), lambda i: (0, 0))],
        out_shape=[jax.ShapeDtypeStruct((n, 2 * GQ_W), BF16), jax.ShapeDtypeStruct((n, GV_W), BF16),
                   jax.ShapeDtypeStruct((n, LANE), BF16), jax.ShapeDtypeStruct((LANE, GQ_W), F32),
                   jax.ShapeDtypeStruct((1, GQ_W), F32)],
        compiler_params=_cp(VMEM_BIG, ("arbitrary",)),
    )(projp, projp, gates, w2p, gb, do, dqg, dkd, dvi, dgl)


SECTIONS = ((C_QKV, 1536), (C_DZ, 512), (C_GQK, 512), (C_GV, 512), (C_GR, 512), (C_SA, 128), (C_SB, 128))


def _inproj_bwd(secs, wp, h0, g1, dx1, *, bsz, t_seq, tr, name, rider=None):
    n, d = h0.shape
    per_seq = t_seq // tr
    steps = n // tr
    s_len = t_seq - CHUNK

    def body(*refs):
        sec_refs = refs[:len(SECTIONS)]
        wp_ref, h0_ref, g_ref, dx1_ref, gx_hbm, meta_ref, dg_ref, obuf, sems = refs[len(SECTIONS):]
        i = pl.program_id(0)
        slot = i % 2

        def put(step, slot_, act):
            b, j = step // per_seq, step % per_seq

            @pl.when(j == 0)
            def _():
                act(pltpu.make_async_copy(obuf.at[slot_, pl.ds(CHUNK, tr - CHUNK), :],
                                          gx_hbm.at[b, pl.ds(0, tr - CHUNK), :], sems.at[slot_]))

            if per_seq > 1:
                @pl.when(j > 0)
                def _():
                    act(pltpu.make_async_copy(obuf.at[slot_], gx_hbm.at[b, pl.ds(j * tr - CHUNK, tr), :],
                                              sems.at[slot_]))

        dh = None
        for s_ref, (off, wd) in zip(sec_refs, SECTIONS):
            part = _nt(s_ref[...], wp_ref[:, off:off + wd])
            dh = part if dh is None else dh + part
        dx, dg = _rms_bwd_math(h0_ref[...], g_ref[...], dh)
        tot = dx1_ref[...] + dx

        @pl.when(i >= 2)
        def _():
            put(i - 2, slot, lambda cp: cp.wait())

        obuf[slot] = tot
        put(i, slot, lambda cp: cp.start())

        @pl.when(i % per_seq == 0)
        def _():
            meta_ref[...] = tot[N_PAD:CHUNK, :]

        @pl.when(i == steps - 1)
        def _():
            if steps > 1:
                put(i - 1, 1 - slot, lambda cp: cp.wait())
            put(i, slot, lambda cp: cp.wait())

        @pl.when(i == 0)
        def _():
            dg_ref[...] = dg

        @pl.when(i > 0)
        def _():
            dg_ref[...] += dg

    row = pl.BlockSpec((tr, d), lambda i: (i, 0))
    vec = pl.BlockSpec((1, d), lambda i: (0, 0))
    outs, ridden = _hosted_call(
        body, rider, name=name, grid=(steps,),
        in_specs=[pl.BlockSpec((tr, wd), lambda i: (i, 0)) for _, wd in SECTIONS]
        + [pl.BlockSpec((d, PW), lambda i: (0, 0)), row, vec, row],
        out_specs=[ANY, pl.BlockSpec((N_META, d), lambda i: (i // per_seq, 0)), vec],
        out_shape=[jax.ShapeDtypeStruct((bsz, s_len, d), F32), jax.ShapeDtypeStruct((bsz * N_META, d), F32),
                   jax.ShapeDtypeStruct((1, d), F32)],
        scratch_shapes=[pltpu.VMEM((2, tr, d), F32), pltpu.SemaphoreType.DMA((2,))],
        compiler_params=_cp(VMEM_BIG, ("arbitrary",)), args=(*secs, wp, h0, g1, dx1))
    return (*outs, ridden)


def _adamw(w, g, m, v, *, name, emit_grad=False, col_tile=None):
    lead = w.ndim - 2
    r, c = w.shape[-2:]
    tr = r if col_tile is not None else (_tile(r, 256, 8) if r > 256 else r)
    tc = col_tile if col_tile is not None else c
    c1 = 1.0 - ADAM_B1 ** ADAM_STEP
    c2 = 1.0 - ADAM_B2 ** ADAM_STEP
    n_out = 4 if emit_grad else 3

    def body(w_ref, g_ref, m_ref, v_ref, *out_refs):
        rd = (lambda ref: ref[0]) if lead else (lambda ref: ref[...])
        gv = g_ref[:, 0:tc]
        nm = ADAM_B1 * rd(m_ref) + (1.0 - ADAM_B1) * gv
        nv = ADAM_B2 * rd(v_ref) + (1.0 - ADAM_B2) * (gv * gv)
        res = [-ADAM_LR * ((nm / c1) / (jnp.sqrt(nv / c2) + ADAM_EPS) + ADAM_WD * rd(w_ref)), nm, nv, gv]
        for o_ref, val in zip(out_refs, res):
            if lead:
                o_ref[0] = val
            else:
                o_ref[...] = val

    if col_tile is None:
        blk = pl.BlockSpec((1,) * lead + (tr, c), lambda i: (0,) * lead + (i, 0))
        gblk = pl.BlockSpec((tr, g.shape[1]), lambda i: (i, 0))
        steps = r // tr
    else:
        blk = pl.BlockSpec((1,) * lead + (r, tc), lambda j: (0,) * lead + (0, j))
        gblk = pl.BlockSpec((r, tc), lambda j: (0, j))
        steps = c // tc
    sds = jax.ShapeDtypeStruct(w.shape, F32)
    return pl.pallas_call(
        body, name=name, grid=(steps,), in_specs=[blk, gblk, blk, blk], out_specs=[blk] * n_out,
        out_shape=[sds] * n_out, compiler_params=_cp(VMEM_BIG),
    )(w, g, m, v)


def _adamw_rows(w, g, m, v, *, name):
    r, _, c = w.shape
    tr = max(t for t in range(1, 129) if r % t == 0)
    c1 = 1.0 - ADAM_B1 ** ADAM_STEP
    c2 = 1.0 - ADAM_B2 ** ADAM_STEP

    def body(w_ref, g_ref, m_ref, v_ref, d_ref, nm_ref, nv_ref, go_ref):
        gv = g_ref[...]
        nm = ADAM_B1 * m_ref[...] + (1.0 - ADAM_B1) * gv
        nv = ADAM_B2 * v_ref[...] + (1.0 - ADAM_B2) * (gv * gv)
        d_ref[...] = -ADAM_LR * ((nm / c1) / (jnp.sqrt(nv / c2) + ADAM_EPS) + ADAM_WD * w_ref[...])
        nm_ref[...] = nm
        nv_ref[...] = nv
        go_ref[...] = gv

    blk = pl.BlockSpec((tr, 1, c), lambda i: (i, 0, 0))
    sds = jax.ShapeDtypeStruct(w.shape, F32)
    return pl.pallas_call(
        body, name=name, grid=(r // tr,), in_specs=[blk] * 4, out_specs=[blk] * 4, out_shape=[sds] * 4,
        compiler_params=_cp(VMEM_BIG),
    )(w, g, m, v)


def _pair_sum(where, g, theirs, *, name):
    lead, r, cols = g.shape
    half = r // 2
    tr = _tile(half, 256, 16)
    nh = half // tr

    def body(w_ref, a_ref, b_ref, o_ref):
        o_ref[...] = (a_ref[...] + b_ref[...]).astype(o_ref.dtype)

    blk = pl.BlockSpec((1, tr, cols), lambda s, i, w: (s, i, 0))
    return pl.pallas_call(
        body, name=name,
        grid_spec=pltpu.PrefetchScalarGridSpec(
            num_scalar_prefetch=1, grid=(lead, nh),
            in_specs=[pl.BlockSpec((1, tr, cols), lambda s, i, w: (s, w[0] * nh + i, 0)), blk], out_specs=blk),
        out_shape=jax.ShapeDtypeStruct((lead, half, cols), BF16), compiler_params=_cp(VMEM_BIG),
    )(where, g, theirs)


def _chip_sum(where, pair, q, *, name):
    _, half, cols = pair.shape
    tr = _tile(half, 256, 16)
    nh = half // tr

    def body(w_ref, own_ref, q1_ref, q2_ref, q3_ref, o_ref):
        f = lambda ref: ref[0].astype(F32)
        o_ref[...] = ((f(own_ref) + f(q1_ref)) + f(q2_ref)) + f(q3_ref)

    def peer(d):
        return pl.BlockSpec((1, tr, cols), lambda i, w: ((w[1] + d) % N_CHIPS, i, 0))

    return pl.pallas_call(
        body, name=name,
        grid_spec=pltpu.PrefetchScalarGridSpec(
            num_scalar_prefetch=1, grid=(nh,),
            in_specs=[peer(0), peer(1), peer(2), peer(3)],
            out_specs=pl.BlockSpec((tr, cols), lambda i, w: (w[0] * nh + i, 0))),
        out_shape=jax.ShapeDtypeStruct((2 * half, cols), F32), compiler_params=_cp(VMEM_BIG),
    )(where, pair, q, q, q)


VM = pl.BlockSpec(memory_space=pltpu.VMEM)


def _row_chunks(rows, n_split):
    size = rows // n_split
    assert size * n_split == rows and size % 16 == 0, (rows, n_split)
    return [(s, pl.ds(s * size, size)) for s in range(n_split)], size


D2D_SPLIT = 4
ICI_SPLIT = 2


def _sibling_halves(grads):
    n_arr = len(grads)

    def body(*refs):
        ins = refs[:n_arr]
        theirs = refs[n_arr:2 * n_arr]
        send_sems, recv_sems = refs[2 * n_arr:]
        x, y, c = _place()
        copies = []
        for k in range(n_arr):
            half = ins[k].shape[1] // 2
            chunks, size = _row_chunks(half, D2D_SPLIT)
            for s, dst_rows in chunks:
                give = pltpu.make_async_remote_copy(
                    src_ref=ins[k].at[:, pl.ds((1 - c) * half + s * size, size), :], dst_ref=theirs[k].at[:, dst_rows, :],
                    send_sem=send_sems.at[k, s], recv_sem=recv_sems.at[k, s], device_id=(x, y, 1 - c),
                    device_id_type=MESH)
                give.start()
                copies.append(give)
        for give in copies:
            give.wait()

    halves = [jax.ShapeDtypeStruct((g.shape[0], g.shape[1] // 2, g.shape[2]), F32) for g in grads]
    sem = pltpu.SemaphoreType.DMA((n_arr, D2D_SPLIT))
    return pl.pallas_call(
        body, name="sibling_halves", in_specs=[ANY] * n_arr, out_specs=[ANY] * n_arr, out_shape=halves,
        scratch_shapes=[sem, sem],
    )(*grads)


def _chip_exchange(parts):
    n_arr = len(parts)

    def body(*refs):
        ins = refs[:n_arr]
        outs = refs[n_arr:2 * n_arr]
        send_sems, recv_sems = refs[2 * n_arr:]
        x, y, c = _place()
        me = 2 * x + y
        sends = []
        for k in range(n_arr):
            chunks, _ = _row_chunks(ins[k].shape[1], ICI_SPLIT)
            for d, (px, py, pj) in enumerate(_other_chips(x, y)):
                for s, rows in chunks:
                    cp = pltpu.make_async_remote_copy(
                        src_ref=ins[k].at[pj, rows, :], dst_ref=outs[k].at[me, rows, :], send_sem=send_sems.at[k, d, s],
                        recv_sem=recv_sems.at[k, d, s], device_id=(px, py, c), device_id_type=MESH)
                    cp.start()
                    sends.append(cp)
        for k in range(n_arr):
            chunks, _ = _row_chunks(ins[k].shape[1], ICI_SPLIT)
            for d, (px, py, pj) in enumerate(_other_chips(x, y)):
                for s, rows in chunks:
                    pltpu.make_async_remote_copy(
                        src_ref=ins[k].at[pj, rows, :], dst_ref=outs[k].at[pj, rows, :], send_sem=send_sems.at[k, d, s],
                        recv_sem=recv_sems.at[k, d, s], device_id=(px, py, c), device_id_type=MESH).wait_recv()
        for cp in sends:
            cp.wait_send()

    sem = pltpu.SemaphoreType.DMA((n_arr, 3, ICI_SPLIT))
    return pl.pallas_call(
        body, name="chip_exchange", in_specs=[ANY] * n_arr, out_specs=[ANY] * n_arr,
        out_shape=[jax.ShapeDtypeStruct(p.shape, p.dtype) for p in parts],
        scratch_shapes=[sem, sem],
    )(*parts)


class _SiblingHalvesRider:
    def __init__(self, grads):
        self.inputs = list(grads)
        self.out_shapes = [jax.ShapeDtypeStruct((g.shape[0], g.shape[1] // 2, g.shape[2]), F32) for g in grads]
        self.aliases = {}
        self.sems = [pltpu.SemaphoreType.DMA((len(grads), D2D_SPLIT))] * 2

    def _copies(self, ins, outs, sems):
        x, y, c = _place()
        for k in range(len(ins)):
            half = ins[k].shape[1] // 2
            chunks, size = _row_chunks(half, D2D_SPLIT)
            for s, dst_rows in chunks:
                yield pltpu.make_async_remote_copy(
                    src_ref=ins[k].at[:, pl.ds((1 - c) * half + s * size, size), :], dst_ref=outs[k].at[:, dst_rows, :],
                    send_sem=sems[0].at[k, s], recv_sem=sems[1].at[k, s], device_id=(x, y, 1 - c), device_id_type=MESH)

    def first(self, ins, outs, sems):
        for cp in self._copies(ins, outs, sems):
            cp.start()

    def last(self, ins, outs, sems):
        for cp in self._copies(ins, outs, sems):
            cp.wait()


class _ChipExchangeRider:
    def __init__(self, parts):
        self.inputs = list(parts)
        self.out_shapes = [jax.ShapeDtypeStruct(p.shape, p.dtype) for p in parts]
        self.aliases = {}
        self.sems = [pltpu.SemaphoreType.DMA((len(parts), 3, ICI_SPLIT))] * 2

    def _copies(self, ins, outs, sems, receiving):
        x, y, c = _place()
        for k in range(len(ins)):
            chunks, _ = _row_chunks(ins[k].shape[1], ICI_SPLIT)
            for d, (px, py, pj) in enumerate(_other_chips(x, y)):
                for s, rows in chunks:
                    yield pltpu.make_async_remote_copy(
                        src_ref=ins[k].at[pj, rows, :], dst_ref=outs[k].at[pj if receiving else 2 * x + y, rows, :],
                        send_sem=sems[0].at[k, d, s], recv_sem=sems[1].at[k, d, s], device_id=(px, py, c),
                        device_id_type=MESH)

    def first(self, ins, outs, sems):
        for cp in self._copies(ins, outs, sems, False):
            cp.start()

    def last(self, ins, outs, sems):
        for cp in self._copies(ins, outs, sems, True):
            cp.wait_recv()
        for cp in self._copies(ins, outs, sems, False):
            cp.wait_send()


def _sibling_join(bufs):
    n_arr = len(bufs)

    def body(*refs):
        bufs_out = refs[n_arr:2 * n_arr]
        send_sems, recv_sems = refs[2 * n_arr:]
        x, y, c = _place()
        copies = []
        for k in range(n_arr):
            half = bufs_out[k].shape[0] // 2
            chunks, size = _row_chunks(half, D2D_SPLIT)
            for s, _ in chunks:
                rows = pl.ds(c * half + s * size, size)
                give = pltpu.make_async_remote_copy(
                    src_ref=bufs_out[k].at[rows, :], dst_ref=bufs_out[k].at[rows, :], send_sem=send_sems.at[k, s],
                    recv_sem=recv_sems.at[k, s], device_id=(x, y, 1 - c), device_id_type=MESH)
                give.start()
                copies.append((k, s, half, size, give))
        for k, s, half, size, give in copies:
            rows = pl.ds((1 - c) * half + s * size, size)
            pltpu.make_async_remote_copy(
                src_ref=bufs_out[k].at[rows, :], dst_ref=bufs_out[k].at[rows, :], send_sem=send_sems.at[k, s],
                recv_sem=recv_sems.at[k, s], device_id=(x, y, 1 - c), device_id_type=MESH).wait_recv()
            give.wait_send()

    sem = pltpu.SemaphoreType.DMA((n_arr, D2D_SPLIT))
    return pl.pallas_call(
        body, name="sibling_join", in_specs=[ANY] * n_arr, out_specs=[ANY] * n_arr,
        out_shape=[jax.ShapeDtypeStruct(b.shape, F32) for b in bufs],
        input_output_aliases={k: k for k in range(n_arr)},
        scratch_shapes=[sem, sem],
    )(*bufs)


PACK_ROWS = 48


def _small_allreduce(pack):
    masks = [(dx, dy, dc) for dx in (0, 1) for dy in (0, 1) for dc in (0, 1)][1:]

    def body(p_ref, o_ref, buf, send_sems, recv_sems):
        x, y, c = _place()
        me = 4 * x + 2 * y + c
        buf[me] = p_ref[...]
        sends = []
        for k, (dx, dy, dc) in enumerate(masks):
            peer = (1 - x if dx else x, 1 - y if dy else y, 1 - c if dc else c)
            cp = pltpu.make_async_remote_copy(
                src_ref=p_ref, dst_ref=buf.at[me], send_sem=send_sems.at[k], recv_sem=recv_sems.at[k],
                device_id=peer, device_id_type=MESH)
            cp.start()
            sends.append(cp)
        for k, (dx, dy, dc) in enumerate(masks):
            peer = (1 - x if dx else x, 1 - y if dy else y, 1 - c if dc else c)
            pj = 4 * peer[0] + 2 * peer[1] + peer[2]
            pltpu.make_async_remote_copy(
                src_ref=p_ref, dst_ref=buf.at[pj], send_sem=send_sems.at[k], recv_sem=recv_sems.at[k],
                device_id=peer, device_id_type=MESH).wait_recv()
        for cp in sends:
            cp.wait_send()
        tot = buf[0]
        for k in range(1, 8):
            tot = tot + buf[k]
        o_ref[...] = tot
        o_ref[0:N_META, :] = tot[0:N_META] + tot[N_META:2 * N_META]

    return pl.pallas_call(
        body, name="small_allreduce", in_specs=[VM], out_specs=VM,
        out_shape=jax.ShapeDtypeStruct((PACK_ROWS, D_MODEL), F32),
        scratch_shapes=[pltpu.VMEM((8, PACK_ROWS, D_MODEL), F32), pltpu.SemaphoreType.DMA((7,)),
                        pltpu.SemaphoreType.DMA((7,))],
    )(pack)


def _pad_lanes(vec, offset):
    k = vec.shape[1]
    return jnp.concatenate([jnp.zeros((1, offset), F32), vec, jnp.zeros((1, LANE - offset - k), F32)], axis=1)


def _local_step(x, tgt, meta, norm1_g, wp, conv_w, a_log, dt_bias, dn_norm_g, gla_w2, gla_b, gla_norm_g,
                w_out, norm2_g, w_up, w_down, final_norm_g, late_gather=None, where=None, early_gather=None):
    bsz, s_len, d = x.shape
    t_seq = s_len + CHUNK
    nc_seq = t_seq // CHUNK
    n = bsz * t_seq
    tr = _tile(t_seq, 832)
    tt = _tile(t_seq, 416)

    lead = jnp.concatenate([jnp.zeros((N_PAD, d), F32), meta], axis=0)
    alog_row = _pad_lanes(a_log, 4)
    dtb_row = _pad_lanes(dt_bias, 4)

    h0, h, got_early = _embed_norm(x, lead, norm1_g, tr=tr, name="embed_norm1", rider=early_gather)
    if early_gather is not None:
        wp = _padded_from_shards(got_early[0])
        conv_w = got_early[1].transpose(1, 0, 2).reshape(4, QKV_W)
        gla_w2 = got_early[2][:, :, 0:GQ_W // N_CHIPS].transpose(1, 0, 2).reshape(GLA_RANK, GQ_W)
    w2p = jnp.concatenate([gla_w2, jnp.zeros((LANE - GLA_RANK, GQ_W), F32)], axis=0)
    ride_up, ride_down, ride_out = late_gather if late_gather is not None else (None, None, None)
    projp, gates = _mm(h, wp, "nn", tm=tt, tn=PW, tk=d, out_dtypes=(BF16, F32), out_widths=(PW, PW - C_SA),
                       epilogue=lambda acc: (acc, acc[:, C_SA:PW]), name="in_proj")
    qn, kn, v, got_up = _dnprep_fwd(projp, conv_w, bsz=bsz, t_seq=t_seq, tt=tt, name="dn_prep", rider=ride_up)
    u, w, qg, kd, pmat, tmat, gl, got_down = _dn_intra_fwd(qn, kn, v, gates, alog_row, dtb_row, nc_seq=nc_seq,
                                                           name="dn_intra", rider=ride_down)
    o_dn, vn, hist, got_out = _dn_scan_fwd(u, w, qg, kd, pmat, gl, bsz=bsz, nc_seq=nc_seq, name="dn_scan",
                                           rider=ride_out)
    oi, gqg, gkd, ggl = _gla_intra_fwd(projp, gates, w2p, gla_b, nc_seq=nc_seq, name="gla_intra")
    o_gla, ghist, _ = _gla_scan_fwd(oi, gqg, gkd, ggl, projp, bsz=bsz, nc_seq=nc_seq, name="gla_scan")
    if late_gather is not None:
        w_out = got_out[0].reshape(d, d)
        w_up = got_up[0].transpose(1, 0, 2).reshape(d, D_FF)
        w_down = got_down[0].reshape(D_FF, d)
    mix = _gnorm_fwd(o_dn, o_gla, projp, dn_norm_g, gla_norm_g, tr=tr, name="gated_norm")
    def residual_norm(acc, res, g):
        x1v = res + acc
        r = lax.rsqrt(jnp.mean(x1v * x1v, axis=-1, keepdims=True) + EPS)
        return x1v, x1v * r * g

    x1, h2 = _mm(mix, w_out, "nn", tm=tr, tn=d, tk=d, out_dtypes=(F32, BF16), extras=(h0,), vec_extras=(norm2_g,),
                 epilogue=residual_norm, name="out_proj_norm2")

    (act,) = _mm(h2, w_up, "nn", tm=tt, tn=D_FF, tk=d, out_dtypes=(BF16,),
                 epilogue=lambda acc: (jnp.square(jnp.maximum(acc, 0.0)),), name="mlp_up", n_chunk=1024)
    dx2, dx2b, d_final_g, loss_tile = _mlp_down_loss(act, w_down, x1, final_norm_g, tgt, t_seq=t_seq, tr=tt,
                                                     name="mlp_down_loss")

    def relu2_bwd(acc, a):
        a = a.astype(F32)
        return (acc * (2.0 * (a * lax.rsqrt(jnp.maximum(a, F32_TINY)))),)

    (dup,) = _mm(dx2b, w_down, "nt", tm=tt, tn=D_FF, tk=d, out_dtypes=(BF16,), extras=(act,),
                 epilogue=relu2_bwd, name="mlp_down_bwd", n_chunk=1024)
    tk2 = 2 * tr if n % (2 * tr) == 0 else tr
    (d_w_down,) = _mm(act, dx2b, "tn", tm=D_FF // 2, tn=d, tk=tk2, out_dtypes=(F32,), name="w_down_grad")
    (d_w_up_sm,) = _mm(h2, dup, "tn", tm=d, tn=D_FF // 2, tk=tk2, out_dtypes=(F32,), name="w_up_grad",
                       shard_cols=D_FF // N_CHIPS)
    mlp_sm = [d_w_up_sm, d_w_down.reshape(N_CHIPS, D_FF // N_CHIPS, d)]
    ride1 = _SiblingHalvesRider(mlp_sm) if where is not None else None
    dx1, dx1b, d_norm2_g, theirs = _mlp_up_bwd_norm(dup, w_up, x1, norm2_g, dx2, tr=tt, name="mlp_up_bwd_norm",
                                                    rider=ride1)
    ride2 = None
    if where is not None:
        mlp_pair = [_pair_sum(where, a, b, name=f"pair_sum_mlp{k}") for k, (a, b) in enumerate(zip(mlp_sm, theirs))]
        ride2 = _ChipExchangeRider(mlp_pair)

    (dmix,) = _mm(dx1b, w_out, "nt", tm=tr, tn=d, tk=d, out_dtypes=(BF16,), name="out_proj_bwd")
    (d_w_out,) = _mm(mix, dx1b, "tn", tm=d, tn=d, tk=tr, out_dtypes=(F32,), name="w_out_grad")
    do_dn, ddz, do_gla, dgr, d_dn_norm_g, d_gla_norm_g = _gnorm_bwd(
        dmix, o_dn, o_gla, projp, dn_norm_g, gla_norm_g, tr=tr, name="gated_norm_bwd")
    du, dw, dqg, dkd, dgl = _dn_scan_bwd(do_dn, w, qg, kd, vn, pmat, gl, hist, bsz=bsz, nc_seq=nc_seq,
                                          name="dn_scan_bwd")
    dqn, dkn, dv, dsa, d_alog, d_dtb, mlp_parts = _dn_intra_bwd(
        qn, kn, v, gates, alog_row, dtb_row, u, w, tmat, du, dw, dqg, dkd, do_dn, vn, dgl, nc_seq=nc_seq,
        name="dn_intra_bwd", rider=ride2)
    dz, d_conv_w = _dnprep_bwd_a(projp, conv_w, dqn, dkn, dv, bsz=bsz, t_seq=t_seq, tt=tt, name="dn_prep_bwd")
    dcin = _dnprep_bwd_b(dz, conv_w, bsz=bsz, t_seq=t_seq, tt=tt, name="conv_bwd")
    gdqg, gdkd, gdvi, gdgl = _gla_scan_bwd(do_gla, gqg, gkd, ggl, projp, ghist, bsz=bsz, nc_seq=nc_seq,
                                            name="gla_scan_bwd")
    dgqk, dgv, dsb, d_w2p, d_gla_b = _gla_intra_bwd(projp, gates, w2p, gla_b, do_gla, gdqg, gdkd, gdvi, gdgl,
                                                    nc_seq=nc_seq, name="gla_intra_bwd")

    secs = (dcin, ddz, dgqk, dgv, dgr, dsa, dsb)
    g_lo = _grad_tn(h, secs[0:2], tk=tk2, name="w_in_grad_lo")
    g_hi = _grad_tn(h, secs[2:7], tk=tk2, name="w_in_grad_hi")
    ride3 = None
    if where is not None:
        late_sm = [_shards_from_padded(g_lo, g_hi), d_w_out.reshape(N_CHIPS, d // N_CHIPS, d)]
        late_theirs = _exchange_now(_SiblingHalvesRider(late_sm), name="sibling_halves")
        late_pair = [_pair_sum(where, a, b, name=f"pair_sum_{k}") for k, (a, b) in enumerate(zip(late_sm, late_theirs))]
        ride3 = _ChipExchangeRider(late_pair)
    grad_x, d_meta_rows, d_norm1_g, late_parts = _inproj_bwd(secs, wp, h0, norm1_g, dx1, bsz=bsz, t_seq=t_seq, tr=tt,
                                                             name="in_proj_bwd", rider=ride3)

    grads = dict(w_in_lo=g_lo, w_in_hi=g_hi, w_out=d_w_out, w_up_shards=d_w_up_sm, w_down=d_w_down, meta_rows=d_meta_rows,
                 norm1_g=d_norm1_g, conv_w=d_conv_w, a_log_tile=d_alog, dt_bias_tile=d_dtb, dn_norm_g=d_dn_norm_g,
                 gla_w2=d_w2p[0:GLA_RANK], gla_b=d_gla_b, gla_norm_g=d_gla_norm_g, norm2_g=d_norm2_g,
                 final_norm_g=d_final_g, loss_tile=loss_tile)
    if where is not None:
        grads["exchanged"] = (late_pair + mlp_pair, list(late_parts) + list(mlp_parts))
    return grad_x, grads


SHARD_W = IN_WIDTH // N_CHIPS
PADDED_ORDER = ((0, 2048), (2056, 3592), (2048, 2056), LANE - 8, (3592, 3608), LANE - GLA_RANK)


def _pad_layout(w_full):
    pieces = [jnp.zeros((w_full.shape[0], seg), w_full.dtype) if isinstance(seg, int) else w_full[:, seg[0]:seg[1]]
              for seg in PADDED_ORDER]
    return jnp.concatenate(pieces, axis=1)


def _padded_from_shards(stack):
    pieces = []
    for seg in PADDED_ORDER:
        if isinstance(seg, int):
            pieces.append(jnp.zeros((stack.shape[1], seg), stack.dtype))
            continue
        for j in range(N_CHIPS):
            lo, hi = max(seg[0], j * SHARD_W), min(seg[1], (j + 1) * SHARD_W)
            if lo < hi:
                pieces.append(stack[j, :, lo - j * SHARD_W:hi - j * SHARD_W])
    return jnp.concatenate(pieces, axis=1)


def _shards_from_padded(g_lo, g_hi):
    split = g_lo.shape[1]
    starts, pos = [], 0
    for seg in PADDED_ORDER:
        width = seg if isinstance(seg, int) else seg[1] - seg[0]
        if not isinstance(seg, int):
            starts.append((seg[0], seg[1], pos))
        pos += width
    shards = []
    for j in range(N_CHIPS):
        pieces = []
        for a, b, p0 in sorted(starts):
            lo, hi = max(a, j * SHARD_W), min(b, (j + 1) * SHARD_W)
            if lo < hi:
                src, off = (g_lo, 0) if p0 < split else (g_hi, split)
                pieces.append(src[:, p0 + lo - a - off:p0 + hi - a - off])
        pieces.append(jnp.zeros((g_lo.shape[0], D_MODEL - SHARD_W), g_lo.dtype))
        shards.append(jnp.concatenate(pieces, axis=1))
    return jnp.stack(shards)


def _pack_small(g, bsz):
    assert bsz * N_META == 32
    row = jnp.concatenate([g["a_log_tile"], g["dt_bias_tile"], g["dn_norm_g"], g["gla_norm_g"], g["gla_b"],
                           g["loss_tile"], jnp.zeros((1, LANE), F32)], axis=1)
    return jnp.concatenate([g["meta_rows"], g["norm1_g"], g["conv_w"].reshape(6, D_MODEL), row,
                            g["gla_w2"].reshape(4, D_MODEL), g["norm2_g"], g["final_norm_g"],
                            jnp.zeros((2, D_MODEL), F32)], axis=0)


def kernel(x, meta_tokens, norm1_g, w_in, conv_w, a_log, dt_bias, dn_norm_g, gla_w2, gla_b, gla_norm_g, w_out, norm2_g, w_up, w_down, final_norm_g, loss_target, m_meta_tokens, m_norm1_g, m_w_in, m_conv_w, m_a_log, m_dt_bias, m_dn_norm_g, m_gla_w2, m_gla_b, m_gla_norm_g, m_w_out, m_norm2_g, m_w_up, m_w_down, m_final_norm_g, v_meta_tokens, v_norm1_g, v_w_in, v_conv_w, v_a_log, v_dt_bias, v_dn_norm_g, v_gla_w2, v_gla_b, v_gla_norm_g, v_w_out, v_norm2_g, v_w_up, v_w_down, v_final_norm_g):
    bsz = x.shape[0]
    chip = 2 * lax.axis_index("x") + lax.axis_index("y")

    lane_pad = lambda a, wd: jnp.pad(a, ((0, 0), (0, wd - a.shape[1])))
    where = jnp.stack([lax.axis_index("c"), chip]).astype(jnp.int32)
    slot = lambda a, dt, nm: _to_slot(where, a, dt, name="slot_" + nm)
    (g_meta,) = _exchange_now(_GatherRider([slot(meta_tokens, F32, "meta")], [False]), name="gather_meta")
    early = _GatherRider([slot(lane_pad(w_in[0], D_MODEL), BF16, "w_in"), slot(conv_w[0], F32, "conv"),
                          slot(lane_pad(gla_w2[0], LANE), F32, "gla_w2")], [True, False, False])
    late = (_GatherRider([slot(w_up[0], BF16, "w_up")], [True]), _GatherRider([slot(w_down[0], BF16, "w_down")], [True]),
            _GatherRider([slot(w_out[0], BF16, "w_out")], [True]))
    meta_f = g_meta.transpose(1, 0, 2).reshape(N_META, D_MODEL)

    grad_x, g = _local_step(x, loss_target, meta_f, norm1_g, None, None, a_log, dt_bias, dn_norm_g, None, gla_b,
                            gla_norm_g, None, norm2_g, None, None, final_norm_g.reshape(1, D_MODEL), late_gather=late,
                            where=where, early_gather=early)

    pair, parts = g["exchanged"]
    halves = [_chip_sum(where, p, q, name=f"chip_sum_{k}") for k, (p, q) in enumerate(zip(pair, parts))]
    gw_in, gw_out, gw_up, gw_down = _sibling_join(halves)

    red = _small_allreduce(_pack_small(g, bsz))
    g_meta_full = red[0:N_META]
    g_norm1 = red[32:33]
    g_conv_full = red[33:39].reshape(4, QKV_W)
    srow = red[39:40]
    g_alog, g_dtb = srow[:, 4:8], srow[:, LANE + 4:LANE + 8]
    g_dn_norm, g_gla_norm = srow[:, 2 * LANE:3 * LANE], srow[:, 3 * LANE:4 * LANE]
    g_gla_b = srow[:, 4 * LANE:6 * LANE]
    loss = srow[0, 6 * LANE]
    g_w2_full = red[40:44].reshape(GLA_RANK, GQ_W)
    g_norm2 = red[44:45]
    g_final = red[45:46]
    g_meta_sh = lax.dynamic_slice_in_dim(g_meta_full, chip * (D_MODEL // N_CHIPS), D_MODEL // N_CHIPS, axis=1)
    g_conv_sh = lax.dynamic_slice_in_dim(g_conv_full, chip * (QKV_W // N_CHIPS), QKV_W // N_CHIPS, axis=1)
    g_w2_sh = lax.dynamic_slice_in_dim(g_w2_full, chip * (GQ_W // N_CHIPS), GQ_W // N_CHIPS, axis=1)

    names = ["meta_tokens", "norm1_g", "w_in", "conv_w", "a_log", "dt_bias", "dn_norm_g", "gla_w2", "gla_b",
             "gla_norm_g", "w_out", "norm2_g", "w_up", "w_down", "final_norm_g"]
    weights = dict(meta_tokens=meta_tokens, norm1_g=norm1_g, w_in=w_in, conv_w=conv_w, a_log=a_log, dt_bias=dt_bias,
                   dn_norm_g=dn_norm_g, gla_w2=gla_w2, gla_b=gla_b, gla_norm_g=gla_norm_g, w_out=w_out,
                   norm2_g=norm2_g, w_up=w_up, w_down=w_down, final_norm_g=final_norm_g)
    ms = dict(meta_tokens=m_meta_tokens, norm1_g=m_norm1_g, w_in=m_w_in, conv_w=m_conv_w, a_log=m_a_log,
              dt_bias=m_dt_bias, dn_norm_g=m_dn_norm_g, gla_w2=m_gla_w2, gla_b=m_gla_b, gla_norm_g=m_gla_norm_g,
              w_out=m_w_out, norm2_g=m_norm2_g, w_up=m_w_up, w_down=m_w_down, final_norm_g=m_final_norm_g)
    vs = dict(meta_tokens=v_meta_tokens, norm1_g=v_norm1_g, w_in=v_w_in, conv_w=v_conv_w, a_log=v_a_log,
              dt_bias=v_dt_bias, dn_norm_g=v_dn_norm_g, gla_w2=v_gla_w2, gla_b=v_gla_b, gla_norm_g=v_gla_norm_g,
              w_out=v_w_out, norm2_g=v_norm2_g, w_up=v_w_up, w_down=v_w_down, final_norm_g=v_final_norm_g)
    grads2d = dict(meta_tokens=g_meta_sh, norm1_g=g_norm1, w_in=gw_in, conv_w=g_conv_sh, a_log=g_alog, dt_bias=g_dtb,
                   dn_norm_g=g_dn_norm, gla_w2=g_w2_sh, gla_b=g_gla_b, gla_norm_g=g_gla_norm, w_out=gw_out,
                   norm2_g=g_norm2, w_up=gw_up, w_down=gw_down, final_norm_g=g_final)
    out_g, out_d, out_m, out_v = [], [], [], []
    for nm in names:
        shape = weights[nm].shape
        g2 = grads2d[nm]
        if nm == "w_in":
            tview = lambda a: jnp.transpose(a, (2, 0, 1))
            res = _adamw_rows(tview(weights[nm]), g2[:, 0:SHARD_W].T.reshape(SHARD_W, 1, D_MODEL), tview(ms[nm]),
                              tview(vs[nm]), name=f"adamw_{nm}")
            res = [jnp.transpose(a, (1, 2, 0)) for a in res]
            gout = res[3]
        elif len(shape) == 3:
            res = _adamw(weights[nm], g2, ms[nm], vs[nm], name=f"adamw_{nm}")
            gout = g2.reshape(shape)
        else:
            as2d = lambda a: a.reshape(g2.shape)
            res = _adamw(as2d(weights[nm]), g2, as2d(ms[nm]), as2d(vs[nm]), name=f"adamw_{nm}")
            gout = g2.reshape(shape)
        out_g.append(gout)
        out_d.append(res[0].reshape(shape))
        out_m.append(res[1].reshape(shape))
        out_v.append(res[2].reshape(shape))
    return (loss, grad_x, *out_g, *out_d, *out_m, *out_v)
```

```python
import functools

import jax
import jax.numpy as jnp
import numpy as np
from jax import lax
from jax.experimental import pallas as pl
from jax.experimental.pallas import tpu as pltpu

F32 = jnp.float32
BF16 = jnp.bfloat16
HI = lax.Precision.HIGHEST
MESH = pl.DeviceIdType.MESH

D_MODEL = 1024
N_META = 16
CHUNK = 64
N_PAD = CHUNK - N_META
NH = 4
DN_D = 128
GLA_DK = 64
GLA_DV = 128
GLA_RANK = 16
D_FF = 4 * D_MODEL
EPS = 1e-6
F32_TINY = 1.1754944e-38
IN_WIDTH = 3608
C_QKV, C_DZ, C_GQK, C_GV, C_GR, C_SA, C_SB, PW = 0, 1536, 2048, 2560, 3072, 3584, 3712, 3840
LANE = 128
N_CHIPS = 4

ADAM_LR, ADAM_B1, ADAM_B2, ADAM_EPS, ADAM_WD, ADAM_STEP = 0.001, 0.9, 0.999, 1e-08, 0.01, 10

VMEM_BIG = 56 * 1024 * 1024


def _cp(vmem=None, sem=None):
    kw = {}
    if vmem is not None:
        kw["vmem_limit_bytes"] = vmem
    if sem is not None:
        kw["dimension_semantics"] = sem
    return pltpu.CompilerParams(**kw)


def _tile(n, target, mult=16):
    best = None
    for t in range(mult, min(n, target) + 1, mult):
        if n % t == 0:
            best = t
    assert best is not None, (n, target)
    return best


def _dot(a, b, dims, prec=None):
    return lax.dot_general(a, b, (dims, ((), ())), preferred_element_type=F32, precision=prec)


def _nn(a, b):
    return _dot(a.astype(BF16), b.astype(BF16), ((1,), (0,)))


def _nt(a, b):
    return _dot(a.astype(BF16), b.astype(BF16), ((1,), (1,)))


def _tn(a, b):
    return _dot(a.astype(BF16), b.astype(BF16), ((0,), (0,)))


def _split(x):
    hi = x.astype(BF16)
    return hi, (x - hi.astype(F32)).astype(BF16)


def _tri_sum(tri, x):
    t = tri.astype(BF16)
    hi = x.astype(BF16)
    r1 = x - hi.astype(F32)
    mid = r1.astype(BF16)
    lo = (r1 - mid.astype(F32)).astype(BF16)
    nn = ((1,), (0,))
    return _dot(t, hi, nn) + _dot(t, mid, nn) + _dot(t, lo, nn)


def _sigmoid(x):
    return 0.5 * jnp.tanh(0.5 * x) + 0.5


def _softplus(x):
    return jnp.maximum(x, 0.0) + jnp.log(1.0 + jnp.exp(-jnp.abs(x)))


def _logsigmoid(x):
    return -_softplus(-x)


def _iota2(shape, dim):
    return lax.broadcasted_iota(jnp.int32, shape, dim)


def _mm(a, b, mode, *, tm, tn, tk, out_dtypes, extras=(), epilogue=None, name, vmem=VMEM_BIG, rider=None,
        out_widths=None, n_chunk=None, vec_extras=(), shard_cols=None):
    if mode == "tn":
        K, M = a.shape
    else:
        M, K = a.shape
    N = b.shape[0] if mode == "nt" else b.shape[1]
    assert M % tm == 0 and N % tn == 0 and K % tk == 0, (name, M, N, K, tm, tn, tk)
    nk = K // tk
    n_ex, n_out, n_vec = len(extras), len(out_dtypes), len(vec_extras)
    if mode == "tn":
        a_spec = pl.BlockSpec((tk, tm), lambda i, j, k: (k, i))
    else:
        a_spec = pl.BlockSpec((tm, tk), lambda i, j, k: (i, k))
    if mode == "nt":
        b_spec = pl.BlockSpec((tn, tk), lambda i, j, k: (j, k))
    else:
        b_spec = pl.BlockSpec((tk, tn), lambda i, j, k: (k, j))
    mn_spec = pl.BlockSpec((tm, tn), lambda i, j, k: (i, j))
    if out_widths is None:
        o_specs = [mn_spec] * n_out
        o_shapes = [jax.ShapeDtypeStruct((M, N), dt) for dt in out_dtypes]
    else:
        assert tn == N
        o_specs = [pl.BlockSpec((tm, wd), lambda i, j, k: (i, 0)) for wd in out_widths]
        o_shapes = [jax.ShapeDtypeStruct((M, wd), dt) for wd, dt in zip(out_widths, out_dtypes)]
    if shard_cols is not None:
        o_specs = [pl.BlockSpec((tn // shard_cols, tm, shard_cols), lambda i, j, k: (j, i, 0))]
        o_shapes = [jax.ShapeDtypeStruct((N // shard_cols, M, shard_cols), F32)]
    dims = {"nn": ((1,), (0,)), "nt": ((1,), (1,)), "tn": ((0,), (0,))}[mode]

    single = nk == 1
    direct = (not single) and epilogue is None and n_out == 1 and out_dtypes[0] == F32

    def body(*refs):
        a_ref, b_ref = refs[0], refs[1]
        ex_refs = refs[2:2 + n_ex]
        vec_refs = refs[2 + n_ex:2 + n_ex + n_vec]
        out_refs = refs[2 + n_ex + n_vec:2 + n_ex + n_vec + n_out]
        if n_chunk is not None:
            assert single and out_widths is None and mode != "tn" and tn % n_chunk == 0
            av = a_ref[...].astype(BF16)
            for j in range(tn // n_chunk):
                cols = slice(j * n_chunk, (j + 1) * n_chunk)
                bv = b_ref[cols, :] if mode == "nt" else b_ref[:, cols]
                acc = _dot(av, bv.astype(BF16), dims)
                res = (acc,) if epilogue is None else epilogue(acc, *[e[:, cols] for e in ex_refs])
                for o_ref, r in zip(out_refs, res):
                    o_ref[:, cols] = r.astype(o_ref.dtype)
            return
        part = _dot(a_ref[...].astype(BF16), b_ref[...].astype(BF16), dims)

        def finish(acc):
            res = (acc,) if epilogue is None else epilogue(acc, *[e[...] for e in ex_refs], *[v[...] for v in vec_refs])
            for o_ref, r in zip(out_refs, res):
                o_ref[...] = r.astype(o_ref.dtype)

        if single:
            if shard_cols is not None:
                for sh in range(tn // shard_cols):
                    out_refs[0][sh] = part[:, sh * shard_cols:(sh + 1) * shard_cols]
            else:
                finish(part)
            return
        acc_ref = out_refs[0] if direct else refs[2 + n_ex + n_vec + n_out]
        k = pl.program_id(2)
        if shard_cols is not None:
            assert direct
            for sh in range(tn // shard_cols):
                piece = part[:, sh * shard_cols:(sh + 1) * shard_cols]

                @pl.when(k == 0)
                def _():
                    acc_ref[sh] = piece

                @pl.when(k > 0)
                def _():
                    acc_ref[sh] += piece
            return

        @pl.when(k == 0)
        def _():
            acc_ref[...] = part

        @pl.when(k > 0)
        def _():
            acc_ref[...] += part

        if not direct:
            @pl.when(k == nk - 1)
            def _():
                finish(acc_ref[...])

    outs, ridden = _hosted_call(
        body, rider, name=name, grid=(M // tm, N // tn, nk),
        in_specs=[a_spec, b_spec] + [mn_spec] * n_ex + [pl.BlockSpec((1, tn), lambda i, j, k: (0, j))] * n_vec,
        out_specs=o_specs, out_shape=o_shapes,
        scratch_shapes=[] if (single or direct) else [pltpu.VMEM((tm, tn), F32)],
        compiler_params=_cp(vmem, ("parallel", "parallel", "arbitrary")), args=(a, b, *extras, *vec_extras))
    return tuple(outs) if rider is None else (tuple(outs), ridden)


def _grad_tn(a, secs, *, tk, name):
    kk, m = a.shape
    widths = [s.shape[1] for s in secs]
    total = sum(widths)
    nk = kk // tk

    def body(*refs):
        a_ref, sec_refs, o_ref = refs[0], refs[1:-1], refs[-1]
        cat = sec_refs[0][...] if len(sec_refs) == 1 else jnp.concatenate([s[...] for s in sec_refs], axis=1)
        part = _dot(a_ref[...].astype(BF16), cat.astype(BF16), ((0,), (0,)))
        k = pl.program_id(0)

        @pl.when(k == 0)
        def _():
            o_ref[...] = part

        @pl.when(k > 0)
        def _():
            o_ref[...] += part

    return pl.pallas_call(
        body, name=name, grid=(nk,),
        in_specs=[pl.BlockSpec((tk, m), lambda k: (k, 0))] + [pl.BlockSpec((tk, w), lambda k: (k, 0)) for w in widths],
        out_specs=pl.BlockSpec((m, total), lambda k: (0, 0)),
        out_shape=jax.ShapeDtypeStruct((m, total), F32),
        compiler_params=_cp(VMEM_BIG, ("arbitrary",)),
    )(a, *secs)


class _ShiftedRows:
    def __init__(self, src, buf, sems, *, per_seq, tt, steps):
        self.src, self.buf, self.sems = src, buf, sems
        self.per_seq, self.tt, self.steps = per_seq, tt, steps

    def _do(self, step, slot, act):
        b, j = step // self.per_seq, step % self.per_seq
        tt = self.tt

        @pl.when(j == 0)
        def _():
            act(pltpu.make_async_copy(self.src.at[b, pl.ds(0, tt - CHUNK), :],
                                      self.buf.at[slot, pl.ds(CHUNK, tt - CHUNK), :], self.sems.at[slot]))

        if self.per_seq > 1:
            @pl.when(j > 0)
            def _():
                act(pltpu.make_async_copy(self.src.at[b, pl.ds(j * tt - CHUNK, tt), :], self.buf.at[slot],
                                          self.sems.at[slot]))

    def tile(self, i):
        slot = i % 2

        @pl.when(i == 0)
        def _():
            self._do(i, slot, lambda cp: cp.start())

        self._do(i, slot, lambda cp: cp.wait())

        @pl.when(i + 1 < self.steps)
        def _():
            self._do(i + 1, 1 - slot, lambda cp: cp.start())

        return slot


def _embed_norm(x, lead, g, *, tr, name, rider=None):
    bsz, s_len, d = x.shape
    t_seq = s_len + CHUNK
    per_seq = t_seq // tr
    steps = bsz * per_seq
    n = bsz * t_seq

    def body(x_hbm, lead_ref, g_ref, h0_ref, h_ref, buf, sems):
        i = pl.program_id(0)
        slot = _ShiftedRows(x_hbm, buf, sems, per_seq=per_seq, tt=tr, steps=steps).tile(i)

        @pl.when(i % per_seq == 0)
        def _():
            buf[slot, 0:CHUNK, :] = lead_ref[...]

        xv = buf[slot]
        h0_ref[...] = xv
        r = lax.rsqrt(jnp.mean(xv * xv, axis=-1, keepdims=True) + EPS)
        h_ref[...] = (xv * r * g_ref[...]).astype(h_ref.dtype)

    row = pl.BlockSpec((tr, d), lambda i: (i, 0))
    outs, ridden = _hosted_call(
        body, rider, name=name, grid=(steps,),
        in_specs=[ANY, pl.BlockSpec((CHUNK, d), lambda i: (0, 0)), pl.BlockSpec((1, d), lambda i: (0, 0))],
        out_specs=[row, row],
        out_shape=[jax.ShapeDtypeStruct((n, d), F32), jax.ShapeDtypeStruct((n, d), BF16)],
        scratch_shapes=[pltpu.VMEM((2, tr, d), F32), pltpu.SemaphoreType.DMA((2,))],
        compiler_params=_cp(VMEM_BIG, ("arbitrary",)), args=(x, lead, g))
    return (*outs, ridden)


def _rms_fwd(x, g, *, tr, name):
    n, d = x.shape

    def body(x_ref, g_ref, o_ref):
        xv = x_ref[...]
        r = lax.rsqrt(jnp.mean(xv * xv, axis=-1, keepdims=True) + EPS)
        o_ref[...] = (xv * r * g_ref[...]).astype(o_ref.dtype)

    return pl.pallas_call(
        body, name=name, grid=(n // tr,),
        in_specs=[pl.BlockSpec((tr, d), lambda i: (i, 0)), pl.BlockSpec((1, d), lambda i: (0, 0))],
        out_specs=pl.BlockSpec((tr, d), lambda i: (i, 0)),
        out_shape=jax.ShapeDtypeStruct((n, d), BF16),
        compiler_params=_cp(VMEM_BIG),
    )(x, g)


RING_SLOTS = 3


class _RingStream:
    def __init__(self, hbm_ref, ring_ref, sems, *, tr, steps, cols=slice(None)):
        self.hbm_ref, self.ring_ref, self.sems, self.tr, self.steps, self.cols = hbm_ref, ring_ref, sems, tr, steps, cols

    def _copy(self, k):
        slot = k % RING_SLOTS
        return pltpu.make_async_copy(self.hbm_ref.at[pl.ds(k * self.tr, self.tr), self.cols], self.ring_ref.at[slot],
                                     self.sems.at[slot])

    def tile(self, step):
        @pl.when(step == 0)
        def _():
            for k in range(min(RING_SLOTS - 1, self.steps)):
                self._copy(k).start()

        @pl.when(step + RING_SLOTS - 1 < self.steps)
        def _():
            self._copy(step + RING_SLOTS - 1).start()

        self._copy(step).wait()
        return self.ring_ref.at[step % RING_SLOTS]


def _rms_bwd_math(xv, g, dy):
    r = lax.rsqrt(jnp.mean(xv * xv, axis=-1, keepdims=True) + EPS)
    xh = xv * r
    gdy = dy * g
    dx = r * (gdy - xh * jnp.mean(xh * gdy, axis=-1, keepdims=True))
    return dx, jnp.sum(dy * xh, axis=0, keepdims=True)


def _mlp_up_bwd_norm(dup, w_up, x, g, res, *, tr, name, rider=None):
    n, d = x.shape
    ff = dup.shape[1]

    steps = n // tr

    def body(dup_hbm, w_ref, x_ref, g_ref, res_ref, o_ref, ob_ref, dg_ref, ring_ref, ring_sems):
        dup_ref = _RingStream(dup_hbm, ring_ref, ring_sems, tr=tr, steps=steps).tile(pl.program_id(0))
        dh = _nt(dup_ref[...], w_ref[...])
        dx, dg = _rms_bwd_math(x_ref[...], g_ref[...], dh)
        tot = res_ref[...] + dx
        o_ref[...] = tot
        ob_ref[...] = tot.astype(BF16)

        @pl.when(pl.program_id(0) == 0)
        def _():
            dg_ref[...] = dg

        @pl.when(pl.program_id(0) > 0)
        def _():
            dg_ref[...] += dg

    row = pl.BlockSpec((tr, d), lambda i: (i, 0))
    vec = pl.BlockSpec((1, d), lambda i: (0, 0))
    outs, ridden = _hosted_call(
        body, rider, name=name, grid=(steps,),
        in_specs=[ANY, pl.BlockSpec((d, ff), lambda i: (0, 0)), row, vec, row],
        out_specs=[row, row, vec],
        out_shape=[jax.ShapeDtypeStruct((n, d), F32), jax.ShapeDtypeStruct((n, d), BF16),
                   jax.ShapeDtypeStruct((1, d), F32)],
        scratch_shapes=[pltpu.VMEM((RING_SLOTS, tr, ff), dup.dtype), pltpu.SemaphoreType.DMA((RING_SLOTS,))],
        compiler_params=_cp(VMEM_BIG, ("arbitrary",)), args=(dup, w_up, x, g, res))
    return (*outs, ridden)


def _mlp_down_loss(act, w_down, x1, gf, tgt, *, t_seq, tr, name):
    n, d = x1.shape
    ff = act.shape[1]
    per_seq = t_seq // tr
    steps = n // tr

    def body(a_hbm, w_ref, x_ref, g_ref, t_hbm, dx_ref, dxb_ref, dg_ref, loss_ref, tbuf, tsems, ring_ref, ring_sems):
        i = pl.program_id(0)
        a_ref = _RingStream(a_hbm, ring_ref, ring_sems, tr=tr, steps=steps).tile(i)
        slot = _ShiftedRows(t_hbm, tbuf, tsems, per_seq=per_seq, tt=tr, steps=steps).tile(i)

        @pl.when(i % per_seq == 0)
        def _():
            tbuf[slot, 0:CHUNK, :] = jnp.zeros((CHUNK, d), F32)

        t_ref = tbuf.at[slot]
        xv = x_ref[...] + _nn(a_ref[...], w_ref[...])
        g = g_ref[...]
        r = lax.rsqrt(jnp.mean(xv * xv, axis=-1, keepdims=True) + EPS)
        xh = xv * r
        pos = (i % per_seq) * tr + _iota2((tr, 1), 0)
        real = pos >= CHUNK
        err = jnp.where(real, xh * g - t_ref[...], 0.0)
        dy = err * (1.0 / d)
        gdy = dy * g
        dx = r * (gdy - xh * jnp.mean(xh * gdy, axis=-1, keepdims=True))
        dx_ref[...] = dx
        dxb_ref[...] = dx.astype(BF16)
        dg = jnp.sum(dy * xh, axis=0, keepdims=True)
        ls = 0.5 * jnp.sum(jnp.mean(err * err, axis=-1, keepdims=True), axis=0, keepdims=True)
        ls = jnp.where(_iota2((1, LANE), 1) == 0, ls, 0.0)

        @pl.when(i == 0)
        def _():
            dg_ref[...] = dg
            loss_ref[...] = ls

        @pl.when(i > 0)
        def _():
            dg_ref[...] += dg
            loss_ref[...] += ls

    row = pl.BlockSpec((tr, d), lambda i: (i, 0))
    vec = pl.BlockSpec((1, d), lambda i: (0, 0))
    one = pl.BlockSpec((1, LANE), lambda i: (0, 0))
    return pl.pallas_call(
        body, name=name, grid=(n // tr,),
        in_specs=[ANY, pl.BlockSpec((ff, d), lambda i: (0, 0)), row, vec, ANY],
        out_specs=[row, row, vec, one],
        out_shape=[jax.ShapeDtypeStruct((n, d), F32), jax.ShapeDtypeStruct((n, d), BF16),
                   jax.ShapeDtypeStruct((1, d), F32), jax.ShapeDtypeStruct((1, LANE), F32)],
        scratch_shapes=[pltpu.VMEM((2, tr, d), F32), pltpu.SemaphoreType.DMA((2,)),
                        pltpu.VMEM((RING_SLOTS, tr, ff), act.dtype), pltpu.SemaphoreType.DMA((RING_SLOTS,))],
        compiler_params=_cp(VMEM_BIG, ("arbitrary",)),
    )(act, w_down, x1, gf, tgt)


def _gnorm_fwd(o_dn, o_gla, projp, g_dn, g_gla, *, tr, name):
    n = o_dn.shape[0]
    w = NH * DN_D

    steps = n // tr

    def body(odn_hbm, ogl_hbm, proj_hbm, gdn_ref, ggl_ref, mix_ref, rings, sems):
        i = pl.program_id(0)
        sources = ((odn_hbm, slice(None)), (ogl_hbm, slice(None)), (proj_hbm, slice(C_DZ, C_DZ + w)),
                   (proj_hbm, slice(C_GR, C_GR + w)))
        odn_ref, ogl_ref, z_ref, r_ref = [
            _RingStream(src, rings.at[k], sems.at[k], tr=tr, steps=steps, cols=cols).tile(i)
            for k, (src, cols) in enumerate(sources)]
        for grp, (o_ref, gate_ref, gain_ref) in enumerate(((odn_ref, z_ref, gdn_ref), (ogl_ref, r_ref, ggl_ref))):
            gain = gain_ref[...]
            for h in range(NH):
                sl = slice(h * DN_D, (h + 1) * DN_D)
                o = o_ref[:, sl].astype(F32)
                z = gate_ref[:, sl].astype(F32)
                r = lax.rsqrt(jnp.mean(o * o, axis=-1, keepdims=True) + EPS)
                y = (o * r * gain) * (z * _sigmoid(z))
                mix_ref[:, grp * w + h * DN_D: grp * w + (h + 1) * DN_D] = y.astype(mix_ref.dtype)

    assert o_dn.dtype == o_gla.dtype == projp.dtype
    vec = pl.BlockSpec((1, DN_D), lambda i: (0, 0))
    return pl.pallas_call(
        body, name=name, grid=(steps,),
        in_specs=[ANY, ANY, ANY, vec, vec],
        out_specs=pl.BlockSpec((tr, 2 * w), lambda i: (i, 0)),
        out_shape=jax.ShapeDtypeStruct((n, 2 * w), BF16),
        scratch_shapes=[pltpu.VMEM((4, RING_SLOTS, tr, w), projp.dtype), pltpu.SemaphoreType.DMA((4, RING_SLOTS))],
        compiler_params=_cp(VMEM_BIG, ("arbitrary",)),
    )(o_dn, o_gla, projp, g_dn, g_gla)


def _gnorm_bwd(dmix, o_dn, o_gla, projp, g_dn, g_gla, *, tr, name):
    n = o_dn.shape[0]
    w = NH * DN_D

    def body(dm_ref, odn_ref, ogl_ref, z_ref, r_ref, gdn_ref, ggl_ref,
             dodn_ref, ddz_ref, dogl_ref, dgr_ref, dgdn_ref, dggl_ref):
        first = pl.program_id(0) == 0
        groups = ((odn_ref, z_ref, gdn_ref, dodn_ref, ddz_ref, dgdn_ref),
                  (ogl_ref, r_ref, ggl_ref, dogl_ref, dgr_ref, dggl_ref))
        for grp, (o_ref, gate_ref, gain_ref, do_ref, dgate_ref, dgain_ref) in enumerate(groups):
            gain = gain_ref[...]
            dgain = jnp.zeros((1, DN_D), F32)
            for h in range(NH):
                sl = slice(h * DN_D, (h + 1) * DN_D)
                o = o_ref[:, sl].astype(F32)
                z = gate_ref[:, sl].astype(F32)
                dm = dm_ref[:, grp * w + h * DN_D: grp * w + (h + 1) * DN_D].astype(F32)
                r = lax.rsqrt(jnp.mean(o * o, axis=-1, keepdims=True) + EPS)
                oh = o * r
                s = _sigmoid(z)
                dn = dm * (z * s)
                dgate_ref[:, sl] = (dm * (oh * gain) * (s * (1.0 + z * (1.0 - s)))).astype(dgate_ref.dtype)
                gdn = dn * gain
                do_ref[:, sl] = (r * (gdn - oh * jnp.mean(oh * gdn, axis=-1, keepdims=True))).astype(do_ref.dtype)
                dgain = dgain + jnp.sum(dn * oh, axis=0, keepdims=True)

            @pl.when(first)
            def _():
                dgain_ref[...] = dgain

            @pl.when(jnp.logical_not(first))
            def _():
                dgain_ref[...] += dgain

    row = pl.BlockSpec((tr, w), lambda i: (i, 0))
    vec = pl.BlockSpec((1, DN_D), lambda i: (0, 0))
    big = jax.ShapeDtypeStruct((n, w), F32)
    gate = jax.ShapeDtypeStruct((n, w), BF16)
    small = jax.ShapeDtypeStruct((1, DN_D), F32)
    return pl.pallas_call(
        body, name=name, grid=(n // tr,),
        in_specs=[pl.BlockSpec((tr, 2 * w), lambda i: (i, 0)), row, row,
                  pl.BlockSpec((tr, w), lambda i: (i, C_DZ // w)), pl.BlockSpec((tr, w), lambda i: (i, C_GR // w)), vec, vec],
        out_specs=[row, row, row, row, vec, vec],
        out_shape=[gate, gate, gate, gate, small, small],
        compiler_params=_cp(VMEM_BIG),
    )(dmix, o_dn, o_gla, projp, projp, g_dn, g_gla)


QKV_W = 3 * NH * DN_D
HALO = 8


TAP_ROWS = 32


def _aligned_taps(src_ref, tap_ref, slot, starts, nr, cols):
    taps = []
    for j, start in enumerate(starts):
        if start % HALO == 0:
            taps.append(src_ref.at[pl.ds(start, nr), cols])
        else:
            tap_ref[slot, j, 0:nr, :] = src_ref[pl.ds(start, nr), cols]
            taps.append(tap_ref.at[slot, j, 0:nr, :])
    return taps


def _conv_taps(cw_ref, taps, cols):
    z = cw_ref[0:1, cols] * taps[0][...]
    for j in range(1, 4):
        z = z + cw_ref[j:j + 1, cols] * taps[j][...]
    return z


def _dnprep_fwd(projp, conv_w, *, bsz, t_seq, tt, name, rider=None):
    n = bsz * t_seq
    per_seq = t_seq // tt
    hw = NH * DN_D
    row_blk = _tile(tt, TAP_ROWS)

    def body(x_ref, halo_ref, cw_ref, q_ref, k_ref, v_ref, xs_ref, tap_ref):
        i = pl.program_id(1)
        xs_ref[0:HALO, :] = jnp.where(i == 0, 0.0, halo_ref[...].astype(F32))
        xs_ref[HALO:HALO + tt, :] = x_ref[...].astype(F32)
        for c, o_ref in enumerate([q_ref] * NH + [k_ref] * NH + [v_ref] * NH):
            cols = slice(c * DN_D, (c + 1) * DN_D)
            hcols = slice((c % NH) * DN_D, (c % NH + 1) * DN_D)
            for r0 in range(0, tt, row_blk):
                taps = _aligned_taps(xs_ref, tap_ref, (r0 // row_blk) % 2,
                                     [r0 + HALO - 3 + j for j in range(4)], row_blk, cols)
                z = _conv_taps(cw_ref, taps, cols)
                a = z * _sigmoid(z)
                if o_ref is not v_ref:
                    a = a * lax.rsqrt(jnp.sum(a * a, axis=-1, keepdims=True) + EPS)
                o_ref[r0:r0 + row_blk, hcols] = a.astype(o_ref.dtype)

    def halo_map(b, i):
        return (jnp.maximum((b * t_seq + i * tt) // HALO - 1, 0), 0)

    out = pl.BlockSpec((tt, hw), lambda b, i: (b * per_seq + i, 0))
    sds = jax.ShapeDtypeStruct((n, hw), BF16)
    outs, ridden = _hosted_call(
        body, rider, name=name, grid=(bsz, per_seq),
        in_specs=[pl.BlockSpec((tt, QKV_W), lambda b, i: (b * per_seq + i, 0)),
                  pl.BlockSpec((HALO, QKV_W), halo_map),
                  pl.BlockSpec((4, QKV_W), lambda b, i: (0, 0))],
        out_specs=[out, out, out], out_shape=[sds, sds, sds],
        scratch_shapes=[pltpu.VMEM((tt + HALO, QKV_W), F32), pltpu.VMEM((2, 4, row_blk, DN_D), F32)],
        compiler_params=_cp(VMEM_BIG), args=(projp, projp, conv_w))
    return (*outs, ridden)


def _dnprep_bwd_a(projp, conv_w, dq, dk, dv, *, bsz, t_seq, tt, name):
    n = bsz * t_seq
    per_seq = t_seq // tt
    hw = NH * DN_D

    row_blk = _tile(tt, 4 * TAP_ROWS, mult=HALO)

    def body(x_ref, halo_ref, cw_ref, dq_ref, dk_ref, dv_ref, dz_ref, dcw_ref, xs_ref, part_ref, tap_ref):
        b, i = pl.program_id(0), pl.program_id(1)
        xs_ref[0:HALO, :] = jnp.where(i == 0, 0.0, halo_ref[...].astype(F32))
        xs_ref[HALO:HALO + tt, :] = x_ref[...].astype(F32)
        for c, d_ref in enumerate([dq_ref] * NH + [dk_ref] * NH + [dv_ref] * NH):
            cols = slice(c * DN_D, (c + 1) * DN_D)
            hcols = slice((c % NH) * DN_D, (c % NH + 1) * DN_D)
            parts = [None] * 4
            for r0 in range(0, tt, row_blk):
                taps = _aligned_taps(xs_ref, tap_ref, (r0 // row_blk) % 2,
                                     [r0 + HALO - 3 + j for j in range(4)], row_blk, cols)
                z = _conv_taps(cw_ref, taps, cols)
                s = _sigmoid(z)
                dsilu = s * (1.0 + z * (1.0 - s))
                dy = d_ref[r0:r0 + row_blk, hcols]
                if d_ref is dv_ref:
                    dz = dy * dsilu
                else:
                    a = z * s
                    rs = lax.rsqrt(jnp.sum(a * a, axis=-1, keepdims=True) + EPS)
                    y = a * rs
                    dz = (rs * (dy - y * jnp.sum(dy * y, axis=-1, keepdims=True))) * dsilu
                dz_ref[r0:r0 + row_blk, cols] = dz
                for j in range(4):
                    p = jnp.sum(dz * taps[j][...], axis=0, keepdims=True)
                    parts[j] = p if parts[j] is None else parts[j] + p
            for j in range(4):
                part_ref[j:j + 1, cols] = parts[j]

        first = jnp.logical_and(b == 0, i == 0)

        @pl.when(first)
        def _():
            dcw_ref[...] = part_ref[0:4, :]

        @pl.when(jnp.logical_not(first))
        def _():
            dcw_ref[...] += part_ref[0:4, :]

    def halo_map(b, i):
        return (jnp.maximum((b * t_seq + i * tt) // HALO - 1, 0), 0)

    hrow = pl.BlockSpec((tt, hw), lambda b, i: (b * per_seq + i, 0))
    return pl.pallas_call(
        body, name=name, grid=(bsz, per_seq),
        in_specs=[pl.BlockSpec((tt, QKV_W), lambda b, i: (b * per_seq + i, 0)),
                  pl.BlockSpec((HALO, QKV_W), halo_map),
                  pl.BlockSpec((4, QKV_W), lambda b, i: (0, 0)), hrow, hrow, hrow],
        out_specs=[pl.BlockSpec((tt, QKV_W), lambda b, i: (b * per_seq + i, 0)),
                   pl.BlockSpec((4, QKV_W), lambda b, i: (0, 0))],
        out_shape=[jax.ShapeDtypeStruct((n, QKV_W), F32), jax.ShapeDtypeStruct((4, QKV_W), F32)],
        scratch_shapes=[pltpu.VMEM((tt + HALO, QKV_W), F32), pltpu.VMEM((HALO, QKV_W), F32),
                        pltpu.VMEM((2, 4, row_blk, DN_D), F32)],
        compiler_params=_cp(VMEM_BIG),
    )(projp, projp, conv_w, dq, dk, dv)


def _dnprep_bwd_b(dz, conv_w, *, bsz, t_seq, tt, name):
    n = bsz * t_seq
    per_seq = t_seq // tt
    last_blk = n // HALO - 1

    main = tt - HALO
    row_blk = _tile(tt, TAP_ROWS)
    steps = bsz * per_seq

    def body(dz_hbm, halo_ref, cw_ref, dx_ref, tail_ref, tap_ref, ring_ref, ring_sems):
        i = pl.program_id(1)
        dz_ref = _RingStream(dz_hbm, ring_ref, ring_sems, tr=tt, steps=steps).tile(pl.program_id(0) * per_seq + i)
        for cb in range(QKV_W // DN_D):
            cols = slice(cb * DN_D, (cb + 1) * DN_D)
            for r0 in range(0, main, row_blk):
                nr = min(row_blk, main - r0)
                taps = _aligned_taps(dz_ref, tap_ref, (r0 // row_blk) % 2, [r0 + 3 - j for j in range(4)], nr, cols)
                dx_ref[r0:r0 + nr, cols] = _conv_taps(cw_ref, taps, cols).astype(dx_ref.dtype)
        tail_ref[0:HALO, :] = dz_ref[main:tt, :]
        tail_ref[HALO:2 * HALO, :] = jnp.where(i == per_seq - 1, 0.0, halo_ref[...])
        dx = cw_ref[0:1, :] * tail_ref[pl.ds(3, HALO), :]
        for j in range(1, 4):
            dx = dx + cw_ref[j:j + 1, :] * tail_ref[pl.ds(3 - j, HALO), :]
        dx_ref[main:tt, :] = dx.astype(dx_ref.dtype)

    def halo_map(b, i):
        return (jnp.minimum((b * t_seq + (i + 1) * tt) // HALO, last_blk), 0)

    row = pl.BlockSpec((tt, QKV_W), lambda b, i: (b * per_seq + i, 0))
    return pl.pallas_call(
        body, name=name, grid=(bsz, per_seq),
        in_specs=[ANY, pl.BlockSpec((HALO, QKV_W), halo_map), pl.BlockSpec((4, QKV_W), lambda b, i: (0, 0))],
        out_specs=row, out_shape=jax.ShapeDtypeStruct((n, QKV_W), BF16),
        scratch_shapes=[pltpu.VMEM((2 * HALO, QKV_W), F32), pltpu.VMEM((2, 4, row_blk, DN_D), F32),
                        pltpu.VMEM((RING_SLOTS, tt, QKV_W), F32), pltpu.SemaphoreType.DMA((RING_SLOTS,))],
        compiler_params=_cp(VMEM_BIG, ("arbitrary", "arbitrary")),
    )(dz, dz, conv_w)


def _masks64():
    r = _iota2((CHUNK, CHUNK), 0)
    c = _iota2((CHUNK, CHUNK), 1)
    return r, c


def _group(nc_seq, target=5):
    return max(g for g in range(1, target + 1) if nc_seq % g == 0)


def _round_robin(chains):
    live = list(chains)
    while live:
        nxt = []
        for ch in live:
            try:
                next(ch)
                nxt.append(ch)
            except StopIteration:
                pass
        live = nxt
        yield


def _run(chains):
    for _ in _round_robin(chains):
        pass


def _per_chunk(inner, kinds, grp):
    def body(*refs):
        chains = []
        for gi in range(grp):
            views = []
            for r, kind in zip(refs, kinds):
                if kind == "row":
                    views.append(r.at[pl.ds(gi * CHUNK, CHUNK)])
                elif kind == "lead":
                    views.append(r.at[pl.ds(gi, 1)])
                else:
                    views.append(r)
            chains.append(inner(gi, *views))
        _run(chains)
    return body


def _accumulate(ref, val, gi):
    if gi > 0:
        ref[...] += val
        return
    first = pl.program_id(0) == 0

    @pl.when(first)
    def _():
        ref[...] = val

    @pl.when(jnp.logical_not(first))
    def _():
        ref[...] += val


ANY = pl.BlockSpec(memory_space=pl.ANY)


def _place():
    return lax.axis_index("x"), lax.axis_index("y"), lax.axis_index("c")


def _other_chips(x, y):
    return [(1 - x, y, 2 * (1 - x) + y), (x, 1 - y, 2 * x + 1 - y), (1 - x, 1 - y, 2 * (1 - x) + 1 - y)]


class _GatherRider:
    def __init__(self, bufs, split):
        self.inputs = list(bufs)
        self.split = list(split)
        self.out_shapes = [jax.ShapeDtypeStruct(b.shape, b.dtype) for b in bufs]
        self.aliases = {i: i for i in range(len(bufs))}
        self.sems = [pltpu.SemaphoreType.DMA((len(bufs), 3))] * 4

    def _rows(self, k, buf, c, mine=True):
        r = buf.shape[1]
        if not self.split[k]:
            return pl.ds(0, r)
        return pl.ds((c if mine else 1 - c) * (r // 2), r // 2)

    def _ici(self, k, d, bufs, sems, c, px, py, block):
        rows = self._rows(k, bufs[k], c)
        return pltpu.make_async_remote_copy(
            src_ref=bufs[k].at[block, rows, :], dst_ref=bufs[k].at[block, rows, :], send_sem=sems[0].at[k, d],
            recv_sem=sems[1].at[k, d], device_id=(px, py, c), device_id_type=MESH)

    def _pass(self, k, d, bufs, sems, x, y, c, block, mine):
        rows = self._rows(k, bufs[k], c, mine)
        return pltpu.make_async_remote_copy(
            src_ref=bufs[k].at[block, rows, :], dst_ref=bufs[k].at[block, rows, :], send_sem=sems[2].at[k, d],
            recv_sem=sems[3].at[k, d], device_id=(x, y, 1 - c), device_id_type=MESH)

    def first(self, in_refs, bufs, sems):
        x, y, c = _place()
        for k in range(len(bufs)):
            for d, (px, py, _) in enumerate(_other_chips(x, y)):
                self._ici(k, d, bufs, sems, c, px, py, 2 * x + y).start()

    def last(self, in_refs, bufs, sems):
        x, y, c = _place()
        chips = _other_chips(x, y)
        for k in range(len(bufs)):
            for d, (px, py, pj) in enumerate(chips):
                self._ici(k, d, bufs, sems, c, px, py, pj).wait_recv()
                if self.split[k]:
                    self._pass(k, d, bufs, sems, x, y, c, pj, True).start()
        for k in range(len(bufs)):
            for d, (px, py, pj) in enumerate(chips):
                if self.split[k]:
                    self._pass(k, d, bufs, sems, x, y, c, pj, False).wait_recv()
                    self._pass(k, d, bufs, sems, x, y, c, pj, True).wait_send()
                self._ici(k, d, bufs, sems, c, px, py, 2 * x + y).wait_send()


def _hosted_call(body, rider, *, name, grid, in_specs, out_specs, out_shape, scratch_shapes, compiler_params, args):
    if rider is None:
        outs = pl.pallas_call(body, name=name, grid=grid, in_specs=in_specs, out_specs=out_specs, out_shape=out_shape,
                              scratch_shapes=scratch_shapes, compiler_params=compiler_params)(*args)
        return list(outs), []
    n_in, n_out, n_scr = len(in_specs), len(out_specs), len(scratch_shapes)
    r_in, r_out = len(rider.inputs), len(rider.out_shapes)
    compiler_params = _cp(compiler_params.vmem_limit_bytes, ("arbitrary",) * len(grid))

    def full_body(*refs):
        ins = refs[:n_in]
        rins = refs[n_in:n_in + r_in]
        outs = refs[n_in + r_in:n_in + r_in + n_out]
        routs = refs[n_in + r_in + n_out:n_in + r_in + n_out + r_out]
        rest = refs[n_in + r_in + n_out + r_out:]
        scr, sems = rest[:n_scr], rest[n_scr:]
        ids = [pl.program_id(a) for a in range(len(grid))]
        is_first = functools.reduce(jnp.logical_and, [i == 0 for i in ids])
        is_last = functools.reduce(jnp.logical_and, [i == g - 1 for i, g in zip(ids, grid)])

        @pl.when(is_first)
        def _():
            rider.first(rins, routs, sems)

        body(*ins, *outs, *scr)

        @pl.when(is_last)
        def _():
            rider.last(rins, routs, sems)

    res = pl.pallas_call(
        full_body, name=name, grid=grid, in_specs=list(in_specs) + [ANY] * r_in,
        out_specs=list(out_specs) + [ANY] * r_out, out_shape=list(out_shape) + list(rider.out_shapes),
        input_output_aliases={n_in + i: n_out + o for i, o in rider.aliases.items()},
        scratch_shapes=list(scratch_shapes) + list(rider.sems), compiler_params=compiler_params,
    )(*args, *rider.inputs)
    return list(res[:n_out]), list(res[n_out:])


def _exchange_now(rider, *, name):
    r_in = len(rider.inputs)

    def body(*refs):
        rins = refs[:r_in]
        routs = refs[r_in:r_in + len(rider.out_shapes)]
        sems = refs[r_in + len(rider.out_shapes):]
        rider.first(rins, routs, sems)
        rider.last(rins, routs, sems)

    return pl.pallas_call(
        body, name=name, in_specs=[ANY] * r_in, out_specs=[ANY] * len(rider.out_shapes), out_shape=list(rider.out_shapes),
        input_output_aliases=dict(rider.aliases), scratch_shapes=list(rider.sems),
    )(*rider.inputs)


def _to_slot(where, a, dtype, *, name):
    r, cols = a.shape
    tr = _tile(r, 256, 16) if r > 256 else r

    def body(w_ref, a_ref, o_ref):
        o_ref[0] = a_ref[...].astype(o_ref.dtype)

    return pl.pallas_call(
        body, name=name,
        grid_spec=pltpu.PrefetchScalarGridSpec(
            num_scalar_prefetch=1, grid=(r // tr,),
            in_specs=[pl.BlockSpec((tr, cols), lambda i, w: (i, 0))],
            out_specs=pl.BlockSpec((1, tr, cols), lambda i, w: (w[1], i, 0))),
        out_shape=jax.ShapeDtypeStruct((N_CHIPS, r, cols), dtype), compiler_params=_cp(VMEM_BIG),
    )(where, a)


def _tri_inv(a_strict):
    r, c = _masks64()
    eye = (r == c).astype(F32)
    blk16 = (r // 16) == (c // 16)
    blk32 = (r // 32) == (c // 32)
    ld = jnp.where(blk16, a_strict, 0.0)
    x = eye - ld
    p = _nn(ld, ld)
    yield
    for step in range(3):
        xp = _nn(x, p)
        if step < 2:
            p = _nn(p, p)
        x = x + xp
        yield
    for lk in (jnp.where(jnp.logical_and(blk32, jnp.logical_not(blk16)), a_strict, 0.0),
               jnp.where(blk32, 0.0, a_strict)):
        y = x - eye
        s = lk + _nn(y, lk)
        yield
        x = x - s - _nn(s, y)
        yield
    return x


def _dn_gates(sa, alog, dtb, chunk_in_seq):
    rows = _iota2((CHUNK, LANE), 0)
    valid = jnp.logical_or(rows >= N_PAD, chunk_in_seq > 0)
    beta_t = _sigmoid(sa)
    ea = jnp.exp(alog)
    g_t = jnp.where(valid, -ea * _softplus(sa + dtb), 0.0)
    r, c = _masks64()
    ltri = (r >= c).astype(F32)
    gam_t = _tri_sum(ltri, g_t)
    return beta_t, g_t, gam_t, valid, ea


def _dn_intra_fwd(qn, kn, v, projp, alog_row, dtb_row, *, nc_seq, name, rider=None):
    n = qn.shape[0]
    nct = n // CHUNK
    hw = NH * DN_D
    scale = DN_D ** -0.5

    grp = _group(nc_seq)

    def inner(gi, q_ref, k_ref, v_ref, sa_ref, al_ref, dt_ref, u_ref, w_ref, qg_ref, kd_ref, p_ref, t_ref, gl_ref):
        ci = (pl.program_id(0) * grp + gi) % nc_seq
        beta_t, _, gam_t, _, _ = _dn_gates(sa_ref[...], al_ref[...], dt_ref[...], ci)
        yield
        gam_tt = gam_t.T
        r, c = _masks64()
        incl = r >= c
        strict = r > c

        def head(h):
            sl = slice(h * DN_D, (h + 1) * DN_D)
            beta_w = jnp.broadcast_to(beta_t[:, h:h + 1], (CHUNK, DN_D))
            gam_w = jnp.broadcast_to(gam_t[:, 4 + h:5 + h], (CHUNK, DN_D))
            gam_row = gam_tt[4 + h:5 + h, :]
            gl = gam_t[CHUNK - 1:CHUNK, 4 + h:5 + h]
            dec = jnp.exp(jnp.where(incl, gam_w[:, 0:CHUNK] - gam_row, -jnp.inf))
            kh = k_ref[:, sl].astype(F32)
            qh = q_ref[:, sl].astype(F32) * scale
            vh = v_ref[:, sl].astype(F32)
            kk = _nt(kh, kh)
            qk = _nt(qh, kh)
            yield
            a = jnp.where(strict, beta_w[:, 0:CHUNK] * kk * dec, 0.0)
            tm = yield from _tri_inv(a)
            egam_w = jnp.exp(gam_w)
            u_ref[:, sl] = _nn(tm, beta_w * vh).astype(u_ref.dtype)
            w_ref[:, sl] = _nn(tm, (beta_w * egam_w) * kh).astype(w_ref.dtype)
            qg_ref[:, sl] = (egam_w * qh).astype(qg_ref.dtype)
            kd_ref[:, sl] = (jnp.exp(gl - gam_w) * kh).astype(kd_ref.dtype)
            p_ref[0, h] = qk * dec
            t_ref[0, h] = tm
            gl_ref[0, h:h + 1, :] = jnp.broadcast_to(jnp.exp(gl), (1, LANE))

        yield from _round_robin([head(h) for h in range(NH)])

    rows = grp * CHUNK
    row = pl.BlockSpec((rows, hw), lambda i: (i, 0))
    vec = pl.BlockSpec((1, LANE), lambda i: (0, 0))
    mat = pl.BlockSpec((grp, NH, CHUNK, CHUNK), lambda i: (i, 0, 0, 0))
    big = jax.ShapeDtypeStruct((n, hw), BF16)
    msd = jax.ShapeDtypeStruct((nct, NH, CHUNK, CHUNK), F32)
    kinds = ["row"] * 4 + ["whole"] * 2 + ["row"] * 4 + ["lead"] * 3
    outs, ridden = _hosted_call(
        _per_chunk(inner, kinds, grp), rider, name=name, grid=(nct // grp,),
        in_specs=[row, row, row, pl.BlockSpec((rows, LANE), lambda i: (i, 0)), vec, vec],
        out_specs=[row, row, row, row, mat, mat, pl.BlockSpec((grp, NH, LANE), lambda i: (i, 0, 0))],
        out_shape=[big, big, big, big, msd, msd, jax.ShapeDtypeStruct((nct, NH, LANE), F32)],
        scratch_shapes=[], compiler_params=_cp(VMEM_BIG, ("arbitrary",)),
        args=(qn, kn, v, projp, alog_row, dtb_row))
    return (*outs, ridden)


def _dn_scan_fwd(u, w, qg, kd, p, gl, *, bsz, nc_seq, name, rider=None):
    hw = NH * DN_D
    t_seq = nc_seq * CHUNK
    u, w, qg, kd = (z.reshape(bsz, t_seq, hw) for z in (u, w, qg, kd))
    p = p.reshape(bsz, nc_seq, NH, CHUNK, CHUNK)
    gl = gl.reshape(bsz, nc_seq, NH, LANE)
    grp = _group(nc_seq)

    def body(u_ref, w_ref, qg_ref, kd_ref, p_ref, gl_ref, o_ref, vn_ref, hist_ref, s_ref):
        @pl.when(pl.program_id(0) == 0)
        def _():
            s_ref[...] = jnp.zeros_like(s_ref)

        def chain(b, h, gi):
            sl = slice(h * DN_D, (h + 1) * DN_D)
            rows = pl.ds(gi * CHUNK, CHUNK)
            s = s_ref[b, h]
            hist_ref[b, gi, h] = s.astype(hist_ref.dtype)
            ws = _nn(w_ref[b, rows, sl], s)
            qs = _nn(qg_ref[b, rows, sl], s)
            yield
            vn = u_ref[b, rows, sl] - ws
            vn_ref[b, rows, sl] = vn.astype(vn_ref.dtype)
            o_ref[b, rows, sl] = (qs + _nn(p_ref[b, gi, h], vn)).astype(o_ref.dtype)
            s_ref[b, h] = gl_ref[b, gi, h:h + 1, :] * s + _tn(kd_ref[b, rows, sl], vn)

        for gi in range(grp):
            _run([chain(b, h, gi) for b in range(bsz) for h in range(NH)])

    row = pl.BlockSpec((bsz, grp * CHUNK, hw), lambda i: (0, i, 0))
    outs, ridden = _hosted_call(
        body, rider, name=name, grid=(nc_seq // grp,),
        in_specs=[row, row, row, row, pl.BlockSpec((bsz, grp, NH, CHUNK, CHUNK), lambda i: (0, i, 0, 0, 0)),
                  pl.BlockSpec((bsz, grp, NH, LANE), lambda i: (0, i, 0, 0))],
        out_specs=[row, row, pl.BlockSpec((bsz, grp, NH, DN_D, DN_D), lambda i: (0, i, 0, 0, 0))],
        out_shape=[jax.ShapeDtypeStruct((bsz, t_seq, hw), BF16), jax.ShapeDtypeStruct((bsz, t_seq, hw), BF16),
                   jax.ShapeDtypeStruct((bsz, nc_seq, NH, DN_D, DN_D), F32)],
        scratch_shapes=[pltpu.VMEM((bsz, NH, DN_D, DN_D), F32)],
        compiler_params=_cp(VMEM_BIG, ("arbitrary",)), args=(u, w, qg, kd, p, gl))
    o, vn, hist = outs
    return o.reshape(bsz * t_seq, hw), vn.reshape(bsz * t_seq, hw), hist, ridden


def _dn_scan_bwd(do, w, qg, kd, vn, p, gl, hist, *, bsz, nc_seq, name):
    hw = NH * DN_D
    t_seq = nc_seq * CHUNK
    do, w, qg, kd, vn = (z.reshape(bsz, t_seq, hw) for z in (do, w, qg, kd, vn))
    p = p.reshape(bsz, nc_seq, NH, CHUNK, CHUNK)
    gl = gl.reshape(bsz, nc_seq, NH, LANE)
    grp = _group(nc_seq)

    def body(do_ref, w_ref, qg_ref, kd_ref, vn_ref, p_ref, gl_ref, hist_ref,
             du_ref, dw_ref, dqg_ref, dkd_ref, dgl_ref, ds_ref):
        @pl.when(pl.program_id(0) == 0)
        def _():
            ds_ref[...] = jnp.zeros_like(ds_ref)

        def chain(b, h, gi):
            sl = slice(h * DN_D, (h + 1) * DN_D)
            rows = pl.ds(gi * CHUNK, CHUNK)
            s = hist_ref[b, gi, h]
            dsn = ds_ref[b, h]
            doh = do_ref[b, rows, sl]
            vnh = vn_ref[b, rows, sl]
            kdh = kd_ref[b, rows, sl]
            dvn = _tn(p_ref[b, gi, h], doh) + _nn(kdh, dsn)
            du_ref[b, rows, sl] = dvn.astype(du_ref.dtype)
            dqg_ref[b, rows, sl] = _nt(doh, s).astype(dqg_ref.dtype)
            dkd_ref[b, rows, sl] = _nt(vnh, dsn).astype(dkd_ref.dtype)
            ds_part = _tn(qg_ref[b, rows, sl], doh) + gl_ref[b, gi, h:h + 1, :] * dsn
            dgl = jnp.sum(jnp.sum(dsn * s, axis=0, keepdims=True), axis=1, keepdims=True)
            dgl_ref[b, gi, h:h + 1, :] = jnp.broadcast_to(dgl, (1, LANE))
            yield
            dw_ref[b, rows, sl] = (-_nt(dvn, s)).astype(dw_ref.dtype)
            ds_ref[b, h] = ds_part - _tn(w_ref[b, rows, sl], dvn)

        for gi in reversed(range(grp)):
            _run([chain(b, h, gi) for b in range(bsz) for h in range(NH)])

    steps = nc_seq // grp
    rev = lambda i: steps - 1 - i
    row = pl.BlockSpec((bsz, grp * CHUNK, hw), lambda i: (0, rev(i), 0))
    mat = pl.BlockSpec((bsz, grp, NH, CHUNK, CHUNK), lambda i: (0, rev(i), 0, 0, 0))
    glb = pl.BlockSpec((bsz, grp, NH, LANE), lambda i: (0, rev(i), 0, 0))
    big = jax.ShapeDtypeStruct((bsz, t_seq, hw), BF16)
    outs = pl.pallas_call(
        body, name=name, grid=(steps,),
        in_specs=[row, row, row, row, row, mat, glb,
                  pl.BlockSpec((bsz, grp, NH, DN_D, DN_D), lambda i: (0, rev(i), 0, 0, 0))],
        out_specs=[row, row, row, row, glb],
        out_shape=[big, big, jax.ShapeDtypeStruct(big.shape, F32), jax.ShapeDtypeStruct(big.shape, F32),
                   jax.ShapeDtypeStruct((bsz, nc_seq, NH, LANE), F32)],
        scratch_shapes=[pltpu.VMEM((bsz, NH, DN_D, DN_D), F32)],
        compiler_params=_cp(VMEM_BIG, ("arbitrary",)),
    )(do, w, qg, kd, vn, p, gl, hist)
    du, dw, dqg, dkd, dgl = outs
    n = bsz * t_seq
    return (du.reshape(n, hw), dw.reshape(n, hw), dqg.reshape(n, hw), dkd.reshape(n, hw),
            dgl.reshape(bsz * nc_seq, NH, LANE))


def _dn_intra_bwd(qn, kn, v, projp, alog_row, dtb_row, u, w, tmat, du, dw, dqg, dkd, do, vn, dgl, *, nc_seq, name,
                  rider=None):
    n = qn.shape[0]
    nct = n // CHUNK
    hw = NH * DN_D
    scale = DN_D ** -0.5

    grp = _group(nc_seq)

    def inner(gi, q_ref, k_ref, v_ref, sa_ref, al_ref, dt_ref, u_ref, w_ref, t_ref, du_ref, dw_ref, dqg_ref, dkd_ref,
              do_ref, vn_ref, dgl_ref, dq_ref, dk_ref, dv_ref, dsa_ref, dal_ref, ddt_ref):
        ci = (pl.program_id(0) * grp + gi) % nc_seq
        sa = sa_ref[...]
        beta_t, g_t, gam_t, valid, ea = _dn_gates(sa, al_ref[...], dt_ref[...], ci)
        yield
        lane = _iota2((CHUNK, LANE), 1)
        gates_t = jnp.where(lane < 4, beta_t, gam_t).T
        r, c = _masks64()
        incl, strict, upper, supper = r >= c, r > c, r <= c, r < c
        rows1 = _iota2((CHUNK, 1), 0)
        acc = [jnp.zeros((CHUNK, LANE), F32)]

        def head(h):
            sl = slice(h * DN_D, (h + 1) * DN_D)
            beta_w = jnp.broadcast_to(beta_t[:, h:h + 1], (CHUNK, DN_D))
            gam_w = jnp.broadcast_to(gam_t[:, 4 + h:5 + h], (CHUNK, DN_D))
            beta_s, gam_s = beta_w[:, 0:CHUNK], gam_w[:, 0:CHUNK]
            beta_row = gates_t[h:h + 1, :]
            gam_row = gates_t[4 + h:5 + h, :]
            gl = gam_t[CHUNK - 1:CHUNK, 4 + h:5 + h]
            dec = jnp.exp(jnp.where(incl, gam_s - gam_row, -jnp.inf))
            dec_t = jnp.exp(jnp.where(upper, gam_row - gam_s, -jnp.inf))
            egam_w = jnp.exp(gam_w)
            ekd_w = jnp.exp(gl - gam_w)
            kh = k_ref[:, sl].astype(F32)
            qh = q_ref[:, sl].astype(F32) * scale
            vh = v_ref[:, sl].astype(F32)
            uh = u_ref[:, sl]
            wh = w_ref[:, sl]
            doh = do_ref[:, sl]
            vnh = vn_ref[:, sl]
            kk = _nt(kh, kh)
            qk = _nt(qh, kh)
            qk_t = _nt(kh, qh)
            dp = _nt(doh, vnh)
            dp_t = _nt(vnh, doh)
            t_hi, t_lo = _split(t_ref[0, h].T)
            duh, dwh = du_ref[:, sl], dw_ref[:, sl]
            dvb = _nn(t_hi, duh) + _nn(t_lo, duh)
            dkg = _nn(t_hi, dwh) + _nn(t_lo, dwh)
            yield
            dvb_hi, dvb_lo = _split(dvb)
            dkg_hi, dkg_lo = _split(dkg)
            m = (_nt(dvb_hi, uh) + _nt(dvb_lo, uh)) + (_nt(dkg_hi, wh) + _nt(dkg_lo, wh))
            m_t = (_nt(uh, dvb_hi) + _nt(uh, dvb_lo)) + (_nt(wh, dkg_hi) + _nt(wh, dkg_lo))
            yield
            da = jnp.where(strict, -m, 0.0)
            da_t = jnp.where(supper, -m_t, 0.0)
            a = jnp.where(strict, beta_s * kk * dec, 0.0)
            a_t = jnp.where(supper, beta_row * kk * dec_t, 0.0)
            dad = da * dec
            dad_t = da_t * dec_t
            dpm = jnp.where(incl, dp, 0.0)
            dpm_t = jnp.where(upper, dp_t, 0.0)
            e = da * a + dpm * (qk * dec)
            e_t = da_t * a_t + dpm_t * (qk_t * dec_t)
            dqgh = dqg_ref[:, sl].astype(F32)
            dkdh = dkd_ref[:, sl].astype(F32)
            bg_w = beta_w * egam_w
            dkh = (_nn(beta_s * dad, kh) + _nn(beta_row * dad_t, kh) + _nn(dpm_t * dec_t, qh)
                   + bg_w * dkg + ekd_w * dkdh)
            dqh = _nn(dpm * dec, kh) + egam_w * dqgh
            t_kd = dkdh * (ekd_w * kh)
            dbeta = (jnp.sum(dad * kk, axis=1, keepdims=True)
                     + jnp.sum(dkg * (egam_w * kh) + dvb * vh, axis=1, keepdims=True))
            dgam = (jnp.sum(e - e_t, axis=1, keepdims=True)
                    + jnp.sum(dkg * (bg_w * kh) + dqgh * (egam_w * qh) - t_kd, axis=1, keepdims=True))
            dgam_last = (jnp.sum(jnp.sum(t_kd, axis=0, keepdims=True), axis=1, keepdims=True)
                         + dgl_ref[0, h:h + 1, 0:1] * jnp.exp(gl))
            dgam = dgam + jnp.where(rows1 == CHUNK - 1, dgam_last, 0.0)
            dq_ref[:, sl] = (dqh * scale).astype(dq_ref.dtype)
            dk_ref[:, sl] = dkh.astype(dk_ref.dtype)
            dv_ref[:, sl] = (beta_w * dvb).astype(dv_ref.dtype)
            acc[0] = acc[0] + jnp.where(lane == h, dbeta, 0.0) + jnp.where(lane == 4 + h, dgam, 0.0)

        yield from _round_robin([head(h) for h in range(NH)])
        acc_t = acc[0]
        dg_t = _tri_sum(upper, acc_t)
        ddb = acc_t * beta_t * (1.0 - beta_t)
        dda = jnp.where(valid, dg_t * (-ea) * _sigmoid(sa + dt_ref[...]), 0.0)
        dsa_ref[...] = jnp.where(lane < 4, ddb, jnp.where(lane < 8, dda, 0.0)).astype(dsa_ref.dtype)
        in_g = jnp.logical_and(lane >= 4, lane < 8)
        dal = jnp.sum(jnp.where(in_g, dg_t * g_t, 0.0), axis=0, keepdims=True)
        ddt = jnp.sum(jnp.where(in_g, dda, 0.0), axis=0, keepdims=True)
        _accumulate(dal_ref, dal, gi)
        _accumulate(ddt_ref, ddt, gi)

    rows = grp * CHUNK
    row = pl.BlockSpec((rows, hw), lambda i: (i, 0))
    vec = pl.BlockSpec((1, LANE), lambda i: (0, 0))
    mat = pl.BlockSpec((grp, NH, CHUNK, CHUNK), lambda i: (i, 0, 0, 0))
    glb = pl.BlockSpec((grp, NH, LANE), lambda i: (i, 0, 0))
    big = jax.ShapeDtypeStruct((n, hw), F32)
    v128 = jax.ShapeDtypeStruct((1, LANE), F32)
    kinds = (["row"] * 4 + ["whole"] * 2 + ["row"] * 2 + ["lead"] + ["row"] * 6 + ["lead"]
             + ["row"] * 4 + ["whole"] * 2)
    outs, ridden = _hosted_call(
        _per_chunk(inner, kinds, grp), rider, name=name, grid=(nct // grp,),
        in_specs=[row, row, row, pl.BlockSpec((rows, LANE), lambda i: (i, 0)), vec, vec,
                  row, row, mat, row, row, row, row, row, row, glb],
        out_specs=[row, row, row, pl.BlockSpec((rows, LANE), lambda i: (i, 0)), vec, vec],
        out_shape=[big, big, big, jax.ShapeDtypeStruct((n, LANE), BF16), v128, v128],
        scratch_shapes=[], compiler_params=_cp(VMEM_BIG, ("arbitrary",)),
        args=(qn, kn, v, projp, alog_row, dtb_row, u, w, tmat, du, dw, dqg, dkd, do, vn, dgl))
    return (*outs, ridden)


GQ_W = NH * GLA_DK
GV_W = NH * GLA_DV
GLA_NORM = 16.0
MID = CHUNK // 2


def _gla_gates(sb, w2p, gb, chunk_in_seq):
    rows = _iota2((CHUNK, GQ_W), 0)
    valid = jnp.logical_or(rows >= N_PAD, chunk_in_seq > 0)
    graw = _nn(sb, w2p) + gb
    yield
    g = jnp.where(valid, _logsigmoid(graw) * (1.0 / GLA_NORM), 0.0)
    r, c = _masks64()
    bcum = _tri_sum(r >= c, g)
    yield
    return graw, bcum, valid


def _head_mask(h):
    lane = _iota2((1, GQ_W), 1)
    return jnp.logical_and(lane >= h * GLA_DK, lane < (h + 1) * GLA_DK)


def _gla_intra_fwd(projp, gates, w2p, gb, *, nc_seq, name):
    n = projp.shape[0]
    nct = n // CHUNK
    scale = GLA_DK ** -0.5

    grp = _group(nc_seq)
    rows = grp * CHUNK

    def inner(gi, qk_ref, v_ref, sb_ref, w2_ref, gb_ref, oi_ref, qg_ref, kd_ref, gl_ref):
        ci = (pl.program_id(0) * grp + gi) % nc_seq
        _, bc, _ = yield from _gla_gates(sb_ref[...], w2_ref[...], gb_ref[...], ci)
        bref = bc[MID:MID + 1, :]
        bl = bc[CHUNK - 1:CHUNK, :]
        q = qk_ref[:, 0:GQ_W].astype(F32) * scale
        k = qk_ref[:, GQ_W:2 * GQ_W].astype(F32)
        qi = q * jnp.exp(bc - bref)
        ki = k * jnp.exp(bref - bc)
        qg_ref[...] = (q * jnp.exp(bc)).astype(qg_ref.dtype)
        kd_ref[...] = (k * jnp.exp(bl - bc)).astype(kd_ref.dtype)
        gl_ref[0] = jnp.exp(bl)
        r, c = _masks64()
        incl = r >= c
        a = [jnp.where(incl, _nt(jnp.where(_head_mask(h), qi, 0.0), ki), 0.0) for h in range(NH)]
        yield
        for h in range(NH):
            oi_ref[:, h * GLA_DV:(h + 1) * GLA_DV] = _nn(a[h], v_ref[:, h * GLA_DV:(h + 1) * GLA_DV]).astype(oi_ref.dtype)

    kinds = ["row"] * 3 + ["whole"] * 2 + ["row"] * 3 + ["lead"]
    return pl.pallas_call(
        _per_chunk(inner, kinds, grp), name=name, grid=(nct // grp,),
        in_specs=[pl.BlockSpec((rows, 2 * GQ_W), lambda i: (i, C_GQK // (2 * GQ_W))),
                  pl.BlockSpec((rows, GV_W), lambda i: (i, C_GV // GV_W)),
                  pl.BlockSpec((rows, LANE), lambda i: (i, 1)),
                  pl.BlockSpec((LANE, GQ_W), lambda i: (0, 0)), pl.BlockSpec((1, GQ_W), lambda i: (0, 0))],
        out_specs=[pl.BlockSpec((rows, GV_W), lambda i: (i, 0)), pl.BlockSpec((rows, GQ_W), lambda i: (i, 0)),
                   pl.BlockSpec((rows, GQ_W), lambda i: (i, 0)), pl.BlockSpec((grp, 1, GQ_W), lambda i: (i, 0, 0))],
        out_shape=[jax.ShapeDtypeStruct((n, GV_W), BF16), jax.ShapeDtypeStruct((n, GQ_W), BF16),
                   jax.ShapeDtypeStruct((n, GQ_W), BF16), jax.ShapeDtypeStruct((nct, 1, GQ_W), F32)],
        compiler_params=_cp(VMEM_BIG),
    )(projp, projp, gates, w2p, gb)


def _gla_scan_fwd(oi, qg, kd, gl, projp, *, bsz, nc_seq, name, rider=None):
    t_seq = nc_seq * CHUNK
    oi = oi.reshape(bsz, t_seq, GV_W)
    qg, kd = qg.reshape(bsz, t_seq, GQ_W), kd.reshape(bsz, t_seq, GQ_W)
    gl = gl.reshape(bsz, nc_seq, 1, GQ_W)
    pj = projp.reshape(bsz, t_seq, PW)
    grp = _group(nc_seq)

    def body(oi_ref, qg_ref, kd_ref, gl_ref, v_ref, o_ref, hist_ref, st_ref):
        @pl.when(pl.program_id(0) == 0)
        def _():
            st_ref[...] = jnp.zeros_like(st_ref)

        for gi in range(grp):
            rows = pl.ds(gi * CHUNK, CHUNK)
            for b in range(bsz):
                st = st_ref[b]
                hist_ref[b, gi] = st.astype(hist_ref.dtype)
                qgb = qg_ref[b, rows, :]
                kdb = kd_ref[b, rows, :]
                upd = jnp.zeros((GLA_DV, GQ_W), F32)
                for h in range(NH):
                    sl = slice(h * GLA_DV, (h + 1) * GLA_DV)
                    m = _head_mask(h)
                    o_ref[b, rows, sl] = (oi_ref[b, rows, sl] + _nt(jnp.where(m, qgb, 0.0), st)).astype(o_ref.dtype)
                    upd = upd + jnp.where(m, _tn(v_ref[b, rows, sl], kdb), 0.0)
                st_ref[b] = gl_ref[b, gi] * st + upd

    rws = grp * CHUNK
    outs, ridden = _hosted_call(
        body, rider, name=name, grid=(nc_seq // grp,),
        in_specs=[pl.BlockSpec((bsz, rws, GV_W), lambda i: (0, i, 0)),
                  pl.BlockSpec((bsz, rws, GQ_W), lambda i: (0, i, 0)),
                  pl.BlockSpec((bsz, rws, GQ_W), lambda i: (0, i, 0)),
                  pl.BlockSpec((bsz, grp, 1, GQ_W), lambda i: (0, i, 0, 0)),
                  pl.BlockSpec((bsz, rws, GV_W), lambda i: (0, i, C_GV // GV_W))],
        out_specs=[pl.BlockSpec((bsz, rws, GV_W), lambda i: (0, i, 0)),
                   pl.BlockSpec((bsz, grp, GLA_DV, GQ_W), lambda i: (0, i, 0, 0))],
        out_shape=[jax.ShapeDtypeStruct((bsz, t_seq, GV_W), BF16),
                   jax.ShapeDtypeStruct((bsz, nc_seq, GLA_DV, GQ_W), F32)],
        scratch_shapes=[pltpu.VMEM((bsz, GLA_DV, GQ_W), F32)],
        compiler_params=_cp(VMEM_BIG, ("arbitrary",)), args=(oi, qg, kd, gl, pj))
    return outs[0].reshape(bsz * t_seq, GV_W), outs[1], ridden


def _gla_scan_bwd(do, qg, kd, gl, projp, hist, *, bsz, nc_seq, name):
    t_seq = nc_seq * CHUNK
    do = do.reshape(bsz, t_seq, GV_W)
    qg, kd = qg.reshape(bsz, t_seq, GQ_W), kd.reshape(bsz, t_seq, GQ_W)
    gl = gl.reshape(bsz, nc_seq, 1, GQ_W)
    pj = projp.reshape(bsz, t_seq, PW)
    grp = _group(nc_seq)

    def body(do_ref, qg_ref, kd_ref, gl_ref, v_ref, hist_ref, dqg_ref, dkd_ref, dv_ref, dgl_ref, dst_ref):
        @pl.when(pl.program_id(0) == 0)
        def _():
            dst_ref[...] = jnp.zeros_like(dst_ref)

        for gi in reversed(range(grp)):
            rows = pl.ds(gi * CHUNK, CHUNK)
            for b in range(bsz):
                st = hist_ref[b, gi]
                dst = dst_ref[b]
                qgb = qg_ref[b, rows, :]
                kdb = kd_ref[b, rows, :]
                dqg = jnp.zeros((CHUNK, GQ_W), F32)
                dkd = jnp.zeros((CHUNK, GQ_W), F32)
                add = jnp.zeros((GLA_DV, GQ_W), F32)
                for h in range(NH):
                    sl = slice(h * GLA_DV, (h + 1) * GLA_DV)
                    m = _head_mask(h)
                    doh = do_ref[b, rows, sl]
                    vh = v_ref[b, rows, sl]
                    dqg = dqg + jnp.where(m, _nn(doh, st), 0.0)
                    dkd = dkd + jnp.where(m, _nn(vh, dst), 0.0)
                    dv_ref[b, rows, sl] = _nt(jnp.where(m, kdb, 0.0), dst).astype(dv_ref.dtype)
                    add = add + jnp.where(m, _tn(doh, qgb), 0.0)
                dqg_ref[b, rows, :] = dqg.astype(dqg_ref.dtype)
                dkd_ref[b, rows, :] = dkd.astype(dkd_ref.dtype)
                dgl_ref[b, gi] = jnp.sum(dst * st, axis=0, keepdims=True)
                dst_ref[b] = gl_ref[b, gi] * dst + add

    steps = nc_seq // grp
    rws = grp * CHUNK
    rev = lambda i: steps - 1 - i
    outs = pl.pallas_call(
        body, name=name, grid=(steps,),
        in_specs=[pl.BlockSpec((bsz, rws, GV_W), lambda i: (0, rev(i), 0)),
                  pl.BlockSpec((bsz, rws, GQ_W), lambda i: (0, rev(i), 0)),
                  pl.BlockSpec((bsz, rws, GQ_W), lambda i: (0, rev(i), 0)),
                  pl.BlockSpec((bsz, grp, 1, GQ_W), lambda i: (0, rev(i), 0, 0)),
                  pl.BlockSpec((bsz, rws, GV_W), lambda i: (0, rev(i), C_GV // GV_W)),
                  pl.BlockSpec((bsz, grp, GLA_DV, GQ_W), lambda i: (0, rev(i), 0, 0))],
        out_specs=[pl.BlockSpec((bsz, rws, GQ_W), lambda i: (0, rev(i), 0)),
                   pl.BlockSpec((bsz, rws, GQ_W), lambda i: (0, rev(i), 0)),
                   pl.BlockSpec((bsz, rws, GV_W), lambda i: (0, rev(i), 0)),
                   pl.BlockSpec((bsz, grp, 1, GQ_W), lambda i: (0, rev(i), 0, 0))],
        out_shape=[jax.ShapeDtypeStruct((bsz, t_seq, GQ_W), F32), jax.ShapeDtypeStruct((bsz, t_seq, GQ_W), F32),
                   jax.ShapeDtypeStruct((bsz, t_seq, GV_W), BF16), jax.ShapeDtypeStruct((bsz, nc_seq, 1, GQ_W), F32)],
        scratch_shapes=[pltpu.VMEM((bsz, GLA_DV, GQ_W), F32)],
        compiler_params=_cp(VMEM_BIG, ("arbitrary",)),
    )(do, qg, kd, gl, pj, hist)
    n = bsz * t_seq
    return (outs[0].reshape(n, GQ_W), outs[1].reshape(n, GQ_W), outs[2].reshape(n, GV_W),
            outs[3].reshape(bsz * nc_seq, 1, GQ_W))


def _gla_intra_bwd(projp, gates, w2p, gb, do, dqg, dkd, dvi, dgl, *, nc_seq, name):
    n = projp.shape[0]
    nct = n // CHUNK
    scale = GLA_DK ** -0.5

    grp = _group(nc_seq)
    rows = grp * CHUNK

    def inner(gi, qk_ref, v_ref, sb_ref, w2_ref, gb_ref, do_ref, dqg_ref, dkd_ref, dvi_ref, dgl_ref,
              dqk_ref, dv_ref, dsb_ref, dw2_ref, dgb_ref):
        ci = (pl.program_id(0) * grp + gi) % nc_seq
        sb = sb_ref[...]
        w2 = w2_ref[...]
        graw, bc, valid = yield from _gla_gates(sb, w2, gb_ref[...], ci)
        bref = bc[MID:MID + 1, :]
        bl = bc[CHUNK - 1:CHUNK, :]
        q = qk_ref[:, 0:GQ_W].astype(F32) * scale
        k = qk_ref[:, GQ_W:2 * GQ_W].astype(F32)
        ex1 = jnp.exp(bc - bref)
        ex2 = jnp.exp(bref - bc)
        eb = jnp.exp(bc)
        ekd = jnp.exp(bl - bc)
        qi, ki = q * ex1, k * ex2
        r, c = _masks64()
        incl = r >= c
        upper = r <= c
        a_t, da, da_t = [], [], []
        for h in range(NH):
            sl = slice(h * GLA_DV, (h + 1) * GLA_DV)
            doh = do_ref[:, sl]
            vh = v_ref[:, sl]
            a_t.append(jnp.where(upper, _nt(jnp.where(_head_mask(h), ki, 0.0), qi), 0.0))
            da.append(jnp.where(incl, _nt(doh, vh), 0.0))
            da_t.append(jnp.where(upper, _nt(vh, doh), 0.0))
        yield
        dqi = jnp.zeros((CHUNK, GQ_W), F32)
        dki = jnp.zeros((CHUNK, GQ_W), F32)
        for h in range(NH):
            sl = slice(h * GLA_DV, (h + 1) * GLA_DV)
            m = _head_mask(h)
            dv_ref[:, sl] = (_nn(a_t[h], do_ref[:, sl]) + dvi_ref[:, sl]).astype(dv_ref.dtype)
            dqi = dqi + jnp.where(m, _nn(da[h], ki), 0.0)
            dki = dki + jnp.where(m, _nn(da_t[h], qi), 0.0)
        yield
        dqg = dqg_ref[...].astype(F32)
        dkd = dkd_ref[...].astype(F32)
        dqk_ref[:, 0:GQ_W] = ((dqi * ex1 + dqg * eb) * scale).astype(dqk_ref.dtype)
        dqk_ref[:, GQ_W:2 * GQ_W] = (dki * ex2 + dkd * ekd).astype(dqk_ref.dtype)
        t_qi, t_ki, t_kd = dqi * qi, dki * ki, dkd * (k * ekd)
        db = t_qi - t_ki + dqg * (q * eb) - t_kd
        dbref = jnp.sum(t_ki - t_qi, axis=0, keepdims=True)
        dbl = jnp.sum(t_kd, axis=0, keepdims=True) + dgl_ref[0] * jnp.exp(bl)
        rows = _iota2((CHUNK, GQ_W), 0)
        db = db + jnp.where(rows == MID, dbref, 0.0) + jnp.where(rows == CHUNK - 1, dbl, 0.0)
        dg = _tri_sum(upper, db)
        yield
        dgraw = jnp.where(valid, dg * (1.0 / GLA_NORM) * _sigmoid(-graw), 0.0)
        dsb_ref[...] = _nt(dgraw, w2).astype(dsb_ref.dtype)
        dw2 = _tn(sb, dgraw)
        dgb = jnp.sum(dgraw, axis=0, keepdims=True)
        _accumulate(dw2_ref, dw2, gi)
        _accumulate(dgb_ref, dgb, gi)

    rq = pl.BlockSpec((rows, GQ_W), lambda i: (i, 0))
    rv = pl.BlockSpec((rows, GV_W), lambda i: (i, 0))
    kinds = ["row"] * 3 + ["whole"] * 2 + ["row"] * 4 + ["lead"] + ["row"] * 3 + ["whole"] * 2
    return pl.pallas_call(
        _per_chunk(inner, kinds, grp), name=name, grid=(nct // grp,),
        in_specs=[pl.BlockSpec((rows, 2 * GQ_W), lambda i: (i, C_GQK // (2 * GQ_W))),
                  pl.BlockSpec((rows, GV_W), lambda i: (i, C_GV // GV_W)),
                  pl.BlockSpec((rows, LANE), lambda i: (i, 1)),
                  pl.BlockSpec((LANE, GQ_W), lambda i: (0, 0)), pl.BlockSpec((1, GQ_W), lambda i: (0, 0)),
                  rv, rq, rq, rv, pl.BlockSpec((grp, 1, GQ_W), lambda i: (i, 0, 0))],
        out_specs=[pl.BlockSpec((rows, 2 * GQ_W), lambda i: (i, 0)), rv, pl.BlockSpec((rows, LANE), lambda i: (i, 0)),
                   pl.BlockSpec((LANE, GQ_W), lambda i: (0, 0)), pl.BlockSpec((1, GQ_W), lambda i: (0, 0))],
        out_shape=[jax.ShapeDtypeStruct((n, 2 * GQ_W), BF16), jax.ShapeDtypeStruct((n, GV_W), BF16),
                   jax.ShapeDtypeStruct((n, LANE), BF16), jax.ShapeDtypeStruct((LANE, GQ_W), F32),
                   jax.ShapeDtypeStruct((1, GQ_W), F32)],
        compiler_params=_cp(VMEM_BIG, ("arbitrary",)),
    )(projp, projp, gates, w2p, gb, do, dqg, dkd, dvi, dgl)


SECTIONS = ((C_QKV, 1536), (C_DZ, 512), (C_GQK, 512), (C_GV, 512), (C_GR, 512), (C_SA, 128), (C_SB, 128))


def _inproj_bwd(secs, wp, h0, g1, dx1, *, bsz, t_seq, tr, name, rider=None):
    n, d = h0.shape
    per_seq = t_seq // tr
    steps = n // tr
    s_len = t_seq - CHUNK

    def body(*refs):
        sec_refs = refs[:len(SECTIONS)]
        wp_ref, h0_ref, g_ref, dx1_ref, gx_hbm, meta_ref, dg_ref, obuf, sems = refs[len(SECTIONS):]
        i = pl.program_id(0)
        slot = i % 2

        def put(step, slot_, act):
            b, j = step // per_seq, step % per_seq

            @pl.when(j == 0)
            def _():
                act(pltpu.make_async_copy(obuf.at[slot_, pl.ds(CHUNK, tr - CHUNK), :],
                                          gx_hbm.at[b, pl.ds(0, tr - CHUNK), :], sems.at[slot_]))

            if per_seq > 1:
                @pl.when(j > 0)
                def _():
                    act(pltpu.make_async_copy(obuf.at[slot_], gx_hbm.at[b, pl.ds(j * tr - CHUNK, tr), :],
                                              sems.at[slot_]))

        dh = None
        for s_ref, (off, wd) in zip(sec_refs, SECTIONS):
            part = _nt(s_ref[...], wp_ref[:, off:off + wd])
            dh = part if dh is None else dh + part
        dx, dg = _rms_bwd_math(h0_ref[...], g_ref[...], dh)
        tot = dx1_ref[...] + dx

        @pl.when(i >= 2)
        def _():
            put(i - 2, slot, lambda cp: cp.wait())

        obuf[slot] = tot
        put(i, slot, lambda cp: cp.start())

        @pl.when(i % per_seq == 0)
        def _():
            meta_ref[...] = tot[N_PAD:CHUNK, :]

        @pl.when(i == steps - 1)
        def _():
            if steps > 1:
                put(i - 1, 1 - slot, lambda cp: cp.wait())
            put(i, slot, lambda cp: cp.wait())

        @pl.when(i == 0)
        def _():
            dg_ref[...] = dg

        @pl.when(i > 0)
        def _():
            dg_ref[...] += dg

    row = pl.BlockSpec((tr, d), lambda i: (i, 0))
    vec = pl.BlockSpec((1, d), lambda i: (0, 0))
    outs, ridden = _hosted_call(
        body, rider, name=name, grid=(steps,),
        in_specs=[pl.BlockSpec((tr, wd), lambda i: (i, 0)) for _, wd in SECTIONS]
        + [pl.BlockSpec((d, PW), lambda i: (0, 0)), row, vec, row],
        out_specs=[ANY, pl.BlockSpec((N_META, d), lambda i: (i // per_seq, 0)), vec],
        out_shape=[jax.ShapeDtypeStruct((bsz, s_len, d), F32), jax.ShapeDtypeStruct((bsz * N_META, d), F32),
                   jax.ShapeDtypeStruct((1, d), F32)],
        scratch_shapes=[pltpu.VMEM((2, tr, d), F32), pltpu.SemaphoreType.DMA((2,))],
        compiler_params=_cp(VMEM_BIG, ("arbitrary",)), args=(*secs, wp, h0, g1, dx1))
    return (*outs, ridden)


def _adamw(w, g, m, v, *, name, emit_grad=False, col_tile=None):
    lead = w.ndim - 2
    r, c = w.shape[-2:]
    tr = r if col_tile is not None else (_tile(r, 256, 8) if r > 256 else r)
    tc = col_tile if col_tile is not None else c
    c1 = 1.0 - ADAM_B1 ** ADAM_STEP
    c2 = 1.0 - ADAM_B2 ** ADAM_STEP
    n_out = 4 if emit_grad else 3

    def body(w_ref, g_ref, m_ref, v_ref, *out_refs):
        rd = (lambda ref: ref[0]) if lead else (lambda ref: ref[...])
        gv = g_ref[:, 0:tc]
        nm = ADAM_B1 * rd(m_ref) + (1.0 - ADAM_B1) * gv
        nv = ADAM_B2 * rd(v_ref) + (1.0 - ADAM_B2) * (gv * gv)
        res = [-ADAM_LR * ((nm / c1) / (jnp.sqrt(nv / c2) + ADAM_EPS) + ADAM_WD * rd(w_ref)), nm, nv, gv]
        for o_ref, val in zip(out_refs, res):
            if lead:
                o_ref[0] = val
            else:
                o_ref[...] = val

    if col_tile is None:
        blk = pl.BlockSpec((1,) * lead + (tr, c), lambda i: (0,) * lead + (i, 0))
        gblk = pl.BlockSpec((tr, g.shape[1]), lambda i: (i, 0))
        steps = r // tr
    else:
        blk = pl.BlockSpec((1,) * lead + (r, tc), lambda j: (0,) * lead + (0, j))
        gblk = pl.BlockSpec((r, tc), lambda j: (0, j))
        steps = c // tc
    sds = jax.ShapeDtypeStruct(w.shape, F32)
    return pl.pallas_call(
        body, name=name, grid=(steps,), in_specs=[blk, gblk, blk, blk], out_specs=[blk] * n_out,
        out_shape=[sds] * n_out, compiler_params=_cp(VMEM_BIG),
    )(w, g, m, v)


def _adamw_rows(w, g, m, v, *, name):
    r, _, c = w.shape
    tr = max(t for t in range(1, 129) if r % t == 0)
    c1 = 1.0 - ADAM_B1 ** ADAM_STEP
    c2 = 1.0 - ADAM_B2 ** ADAM_STEP

    def body(w_ref, g_ref, m_ref, v_ref, d_ref, nm_ref, nv_ref, go_ref):
        gv = g_ref[...]
        nm = ADAM_B1 * m_ref[...] + (1.0 - ADAM_B1) * gv
        nv = ADAM_B2 * v_ref[...] + (1.0 - ADAM_B2) * (gv * gv)
        d_ref[...] = -ADAM_LR * ((nm / c1) / (jnp.sqrt(nv / c2) + ADAM_EPS) + ADAM_WD * w_ref[...])
        nm_ref[...] = nm
        nv_ref[...] = nv
        go_ref[...] = gv

    blk = pl.BlockSpec((tr, 1, c), lambda i: (i, 0, 0))
    sds = jax.ShapeDtypeStruct(w.shape, F32)
    return pl.pallas_call(
        body, name=name, grid=(r // tr,), in_specs=[blk] * 4, out_specs=[blk] * 4, out_shape=[sds] * 4,
        compiler_params=_cp(VMEM_BIG),
    )(w, g, m, v)


def _pair_sum(where, g, theirs, *, name):
    lead, r, cols = g.shape
    half = r // 2
    tr = _tile(half, 256, 16)
    nh = half // tr

    def body(w_ref, a_ref, b_ref, o_ref):
        o_ref[...] = (a_ref[...] + b_ref[...]).astype(o_ref.dtype)

    blk = pl.BlockSpec((1, tr, cols), lambda s, i, w: (s, i, 0))
    return pl.pallas_call(
        body, name=name,
        grid_spec=pltpu.PrefetchScalarGridSpec(
            num_scalar_prefetch=1, grid=(lead, nh),
            in_specs=[pl.BlockSpec((1, tr, cols), lambda s, i, w: (s, w[0] * nh + i, 0)), blk], out_specs=blk),
        out_shape=jax.ShapeDtypeStruct((lead, half, cols), BF16), compiler_params=_cp(VMEM_BIG),
    )(where, g, theirs)


def _chip_sum(where, pair, q, *, name):
    _, half, cols = pair.shape
    tr = _tile(half, 256, 16)
    nh = half // tr

    def body(w_ref, own_ref, q1_ref, q2_ref, q3_ref, o_ref):
        f = lambda ref: ref[0].astype(F32)
        o_ref[...] = ((f(own_ref) + f(q1_ref)) + f(q2_ref)) + f(q3_ref)

    def peer(d):
        return pl.BlockSpec((1, tr, cols), lambda i, w: ((w[1] + d) % N_CHIPS, i, 0))

    return pl.pallas_call(
        body, name=name,
        grid_spec=pltpu.PrefetchScalarGridSpec(
            num_scalar_prefetch=1, grid=(nh,),
            in_specs=[peer(0), peer(1), peer(2), peer(3)],
            out_specs=pl.BlockSpec((tr, cols), lambda i, w: (w[0] * nh + i, 0))),
        out_shape=jax.ShapeDtypeStruct((2 * half, cols), F32), compiler_params=_cp(VMEM_BIG),
    )(where, pair, q, q, q)


VM = pl.BlockSpec(memory_space=pltpu.VMEM)


def _row_chunks(rows, n_split):
    size = rows // n_split
    assert size * n_split == rows and size % 16 == 0, (rows, n_split)
    return [(s, pl.ds(s * size, size)) for s in range(n_split)], size


D2D_SPLIT = 4
ICI_SPLIT = 2


def _sibling_halves(grads):
    n_arr = len(grads)

    def body(*refs):
        ins = refs[:n_arr]
        theirs = refs[n_arr:2 * n_arr]
        send_sems, recv_sems = refs[2 * n_arr:]
        x, y, c = _place()
        copies = []
        for k in range(n_arr):
            half = ins[k].shape[1] // 2
            chunks, size = _row_chunks(half, D2D_SPLIT)
            for s, dst_rows in chunks:
                give = pltpu.make_async_remote_copy(
                    src_ref=ins[k].at[:, pl.ds((1 - c) * half + s * size, size), :], dst_ref=theirs[k].at[:, dst_rows, :],
                    send_sem=send_sems.at[k, s], recv_sem=recv_sems.at[k, s], device_id=(x, y, 1 - c),
                    device_id_type=MESH)
                give.start()
                copies.append(give)
        for give in copies:
            give.wait()

    halves = [jax.ShapeDtypeStruct((g.shape[0], g.shape[1] // 2, g.shape[2]), F32) for g in grads]
    sem = pltpu.SemaphoreType.DMA((n_arr, D2D_SPLIT))
    return pl.pallas_call(
        body, name="sibling_halves", in_specs=[ANY] * n_arr, out_specs=[ANY] * n_arr, out_shape=halves,
        scratch_shapes=[sem, sem],
    )(*grads)


def _chip_exchange(parts):
    n_arr = len(parts)

    def body(*refs):
        ins = refs[:n_arr]
        outs = refs[n_arr:2 * n_arr]
        send_sems, recv_sems = refs[2 * n_arr:]
        x, y, c = _place()
        me = 2 * x + y
        sends = []
        for k in range(n_arr):
            chunks, _ = _row_chunks(ins[k].shape[1], ICI_SPLIT)
            for d, (px, py, pj) in enumerate(_other_chips(x, y)):
                for s, rows in chunks:
                    cp = pltpu.make_async_remote_copy(
                        src_ref=ins[k].at[pj, rows, :], dst_ref=outs[k].at[me, rows, :], send_sem=send_sems.at[k, d, s],
                        recv_sem=recv_sems.at[k, d, s], device_id=(px, py, c), device_id_type=MESH)
                    cp.start()
                    sends.append(cp)
        for k in range(n_arr):
            chunks, _ = _row_chunks(ins[k].shape[1], ICI_SPLIT)
            for d, (px, py, pj) in enumerate(_other_chips(x, y)):
                for s, rows in chunks:
                    pltpu.make_async_remote_copy(
                        src_ref=ins[k].at[pj, rows, :], dst_ref=outs[k].at[pj, rows, :], send_sem=send_sems.at[k, d, s],
                        recv_sem=recv_sems.at[k, d, s], device_id=(px, py, c), device_id_type=MESH).wait_recv()
        for cp in sends:
            cp.wait_send()

    sem = pltpu.SemaphoreType.DMA((n_arr, 3, ICI_SPLIT))
    return pl.pallas_call(
        body, name="chip_exchange", in_specs=[ANY] * n_arr, out_specs=[ANY] * n_arr,
        out_shape=[jax.ShapeDtypeStruct(p.shape, p.dtype) for p in parts],
        scratch_shapes=[sem, sem],
    )(*parts)


class _SiblingHalvesRider:
    def __init__(self, grads):
        self.inputs = list(grads)
        self.out_shapes = [jax.ShapeDtypeStruct((g.shape[0], g.shape[1] // 2, g.shape[2]), F32) for g in grads]
        self.aliases = {}
        self.sems = [pltpu.SemaphoreType.DMA((len(grads), D2D_SPLIT))] * 2

    def _copies(self, ins, outs, sems):
        x, y, c = _place()
        for k in range(len(ins)):
            half = ins[k].shape[1] // 2
            chunks, size = _row_chunks(half, D2D_SPLIT)
            for s, dst_rows in chunks:
                yield pltpu.make_async_remote_copy(
                    src_ref=ins[k].at[:, pl.ds((1 - c) * half + s * size, size), :], dst_ref=outs[k].at[:, dst_rows, :],
                    send_sem=sems[0].at[k, s], recv_sem=sems[1].at[k, s], device_id=(x, y, 1 - c), device_id_type=MESH)

    def first(self, ins, outs, sems):
        for cp in self._copies(ins, outs, sems):
            cp.start()

    def last(self, ins, outs, sems):
        for cp in self._copies(ins, outs, sems):
            cp.wait()


class _ChipExchangeRider:
    def __init__(self, parts):
        self.inputs = list(parts)
        self.out_shapes = [jax.ShapeDtypeStruct(p.shape, p.dtype) for p in parts]
        self.aliases = {}
        self.sems = [pltpu.SemaphoreType.DMA((len(parts), 3, ICI_SPLIT))] * 2

    def _copies(self, ins, outs, sems, receiving):
        x, y, c = _place()
        for k in range(len(ins)):
            chunks, _ = _row_chunks(ins[k].shape[1], ICI_SPLIT)
            for d, (px, py, pj) in enumerate(_other_chips(x, y)):
                for s, rows in chunks:
                    yield pltpu.make_async_remote_copy(
                        src_ref=ins[k].at[pj, rows, :], dst_ref=outs[k].at[pj if receiving else 2 * x + y, rows, :],
                        send_sem=sems[0].at[k, d, s], recv_sem=sems[1].at[k, d, s], device_id=(px, py, c),
                        device_id_type=MESH)

    def first(self, ins, outs, sems):
        for cp in self._copies(ins, outs, sems, False):
            cp.start()

    def last(self, ins, outs, sems):
        for cp in self._copies(ins, outs, sems, True):
            cp.wait_recv()
        for cp in self._copies(ins, outs, sems, False):
            cp.wait_send()


def _sibling_join(bufs):
    n_arr = len(bufs)

    def body(*refs):
        bufs_out = refs[n_arr:2 * n_arr]
        send_sems, recv_sems = refs[2 * n_arr:]
        x, y, c = _place()
        copies = []
        for k in range(n_arr):
            half = bufs_out[k].shape[0] // 2
            chunks, size = _row_chunks(half, D2D_SPLIT)
            for s, _ in chunks:
                rows = pl.ds(c * half + s * size, size)
                give = pltpu.make_async_remote_copy(
                    src_ref=bufs_out[k].at[rows, :], dst_ref=bufs_out[k].at[rows, :], send_sem=send_sems.at[k, s],
                    recv_sem=recv_sems.at[k, s], device_id=(x, y, 1 - c), device_id_type=MESH)
                give.start()
                copies.append((k, s, half, size, give))
        for k, s, half, size, give in copies:
            rows = pl.ds((1 - c) * half + s * size, size)
            pltpu.make_async_remote_copy(
                src_ref=bufs_out[k].at[rows, :], dst_ref=bufs_out[k].at[rows, :], send_sem=send_sems.at[k, s],
                recv_sem=recv_sems.at[k, s], device_id=(x, y, 1 - c), device_id_type=MESH).wait_recv()
            give.wait_send()

    sem = pltpu.SemaphoreType.DMA((n_arr, D2D_SPLIT))
    return pl.pallas_call(
        body, name="sibling_join", in_specs=[ANY] * n_arr, out_specs=[ANY] * n_arr,
        out_shape=[jax.ShapeDtypeStruct(b.shape, F32) for b in bufs],
        input_output_aliases={k: k for k in range(n_arr)},
        scratch_shapes=[sem, sem],
    )(*bufs)


PACK_ROWS = 48


def _small_allreduce(pack):
    masks = [(dx, dy, dc) for dx in (0, 1) for dy in (0, 1) for dc in (0, 1)][1:]

    def body(p_ref, o_ref, buf, send_sems, recv_sems):
        x, y, c = _place()
        me = 4 * x + 2 * y + c
        buf[me] = p_ref[...]
        sends = []
        for k, (dx, dy, dc) in enumerate(masks):
            peer = (1 - x if dx else x, 1 - y if dy else y, 1 - c if dc else c)
            cp = pltpu.make_async_remote_copy(
                src_ref=p_ref, dst_ref=buf.at[me], send_sem=send_sems.at[k], recv_sem=recv_sems.at[k],
                device_id=peer, device_id_type=MESH)
            cp.start()
            sends.append(cp)
        for k, (dx, dy, dc) in enumerate(masks):
            peer = (1 - x if dx else x, 1 - y if dy else y, 1 - c if dc else c)
            pj = 4 * peer[0] + 2 * peer[1] + peer[2]
            pltpu.make_async_remote_copy(
                src_ref=p_ref, dst_ref=buf.at[pj], send_sem=send_sems.at[k], recv_sem=recv_sems.at[k],
                device_id=peer, device_id_type=MESH).wait_recv()
        for cp in sends:
            cp.wait_send()
        tot = buf[0]
        for k in range(1, 8):
            tot = tot + buf[k]
        o_ref[...] = tot
        o_ref[0:N_META, :] = tot[0:N_META] + tot[N_META:2 * N_META]

    return pl.pallas_call(
        body, name="small_allreduce", in_specs=[VM], out_specs=VM,
        out_shape=jax.ShapeDtypeStruct((PACK_ROWS, D_MODEL), F32),
        scratch_shapes=[pltpu.VMEM((8, PACK_ROWS, D_MODEL), F32), pltpu.SemaphoreType.DMA((7,)),
                        pltpu.SemaphoreType.DMA((7,))],
    )(pack)


def _pad_lanes(vec, offset):
    k = vec.shape[1]
    return jnp.concatenate([jnp.zeros((1, offset), F32), vec, jnp.zeros((1, LANE - offset - k), F32)], axis=1)


def _local_step(x, tgt, meta, norm1_g, wp, conv_w, a_log, dt_bias, dn_norm_g, gla_w2, gla_b, gla_norm_g,
                w_out, norm2_g, w_up, w_down, final_norm_g, late_gather=None, where=None, early_gather=None):
    bsz, s_len, d = x.shape
    t_seq = s_len + CHUNK
    nc_seq = t_seq // CHUNK
    n = bsz * t_seq
    tr = _tile(t_seq, 832)
    tt = _tile(t_seq, 416)

    lead = jnp.concatenate([jnp.zeros((N_PAD, d), F32), meta], axis=0)
    alog_row = _pad_lanes(a_log, 4)
    dtb_row = _pad_lanes(dt_bias, 4)

    h0, h, got_early = _embed_norm(x, lead, norm1_g, tr=tr, name="embed_norm1", rider=early_gather)
    if early_gather is not None:
        wp = _padded_from_shards(got_early[0])
        conv_w = got_early[1].transpose(1, 0, 2).reshape(4, QKV_W)
        gla_w2 = got_early[2][:, :, 0:GQ_W // N_CHIPS].transpose(1, 0, 2).reshape(GLA_RANK, GQ_W)
    w2p = jnp.concatenate([gla_w2, jnp.zeros((LANE - GLA_RANK, GQ_W), F32)], axis=0)
    ride_up, ride_down, ride_out = late_gather if late_gather is not None else (None, None, None)
    projp, gates = _mm(h, wp, "nn", tm=tt, tn=PW, tk=d, out_dtypes=(BF16, F32), out_widths=(PW, PW - C_SA),
                       epilogue=lambda acc: (acc, acc[:, C_SA:PW]), name="in_proj")
    qn, kn, v, got_up = _dnprep_fwd(projp, conv_w, bsz=bsz, t_seq=t_seq, tt=tt, name="dn_prep", rider=ride_up)
    u, w, qg, kd, pmat, tmat, gl, got_down = _dn_intra_fwd(qn, kn, v, gates, alog_row, dtb_row, nc_seq=nc_seq,
                                                           name="dn_intra", rider=ride_down)
    o_dn, vn, hist, got_out = _dn_scan_fwd(u, w, qg, kd, pmat, gl, bsz=bsz, nc_seq=nc_seq, name="dn_scan",
                                           rider=ride_out)
    oi, gqg, gkd, ggl = _gla_intra_fwd(projp, gates, w2p, gla_b, nc_seq=nc_seq, name="gla_intra")
    o_gla, ghist, _ = _gla_scan_fwd(oi, gqg, gkd, ggl, projp, bsz=bsz, nc_seq=nc_seq, name="gla_scan")
    if late_gather is not None:
        w_out = got_out[0].reshape(d, d)
        w_up = got_up[0].transpose(1, 0, 2).reshape(d, D_FF)
        w_down = got_down[0].reshape(D_FF, d)
    mix = _gnorm_fwd(o_dn, o_gla, projp, dn_norm_g, gla_norm_g, tr=tr, name="gated_norm")
    def residual_norm(acc, res, g):
        x1v = res + acc
        r = lax.rsqrt(jnp.mean(x1v * x1v, axis=-1, keepdims=True) + EPS)
        return x1v, x1v * r * g

    x1, h2 = _mm(mix, w_out, "nn", tm=tr, tn=d, tk=d, out_dtypes=(F32, BF16), extras=(h0,), vec_extras=(norm2_g,),
                 epilogue=residual_norm, name="out_proj_norm2")

    (act,) = _mm(h2, w_up, "nn", tm=tt, tn=D_FF, tk=d, out_dtypes=(BF16,),
                 epilogue=lambda acc: (jnp.square(jnp.maximum(acc, 0.0)),), name="mlp_up", n_chunk=1024)
    dx2, dx2b, d_final_g, loss_tile = _mlp_down_loss(act, w_down, x1, final_norm_g, tgt, t_seq=t_seq, tr=tt,
                                                     name="mlp_down_loss")

    def relu2_bwd(acc, a):
        a = a.astype(F32)
        return (acc * (2.0 * (a * lax.rsqrt(jnp.maximum(a, F32_TINY)))),)

    (dup,) = _mm(dx2b, w_down, "nt", tm=tt, tn=D_FF, tk=d, out_dtypes=(BF16,), extras=(act,),
                 epilogue=relu2_bwd, name="mlp_down_bwd", n_chunk=1024)
    tk2 = 2 * tr if n % (2 * tr) == 0 else tr
    (d_w_down,) = _mm(act, dx2b, "tn", tm=D_FF // 2, tn=d, tk=tk2, out_dtypes=(F32,), name="w_down_grad")
    (d_w_up_sm,) = _mm(h2, dup, "tn", tm=d, tn=D_FF // 2, tk=tk2, out_dtypes=(F32,), name="w_up_grad",
                       shard_cols=D_FF // N_CHIPS)
    mlp_sm = [d_w_up_sm, d_w_down.reshape(N_CHIPS, D_FF // N_CHIPS, d)]
    ride1 = _SiblingHalvesRider(mlp_sm) if where is not None else None
    dx1, dx1b, d_norm2_g, theirs = _mlp_up_bwd_norm(dup, w_up, x1, norm2_g, dx2, tr=tt, name="mlp_up_bwd_norm",
                                                    rider=ride1)
    ride2 = None
    if where is not None:
        mlp_pair = [_pair_sum(where, a, b, name=f"pair_sum_mlp{k}") for k, (a, b) in enumerate(zip(mlp_sm, theirs))]
        ride2 = _ChipExchangeRider(mlp_pair)

    (dmix,) = _mm(dx1b, w_out, "nt", tm=tr, tn=d, tk=d, out_dtypes=(BF16,), name="out_proj_bwd")
    (d_w_out,) = _mm(mix, dx1b, "tn", tm=d, tn=d, tk=tr, out_dtypes=(F32,), name="w_out_grad")
    do_dn, ddz, do_gla, dgr, d_dn_norm_g, d_gla_norm_g = _gnorm_bwd(
        dmix, o_dn, o_gla, projp, dn_norm_g, gla_norm_g, tr=tr, name="gated_norm_bwd")
    du, dw, dqg, dkd, dgl = _dn_scan_bwd(do_dn, w, qg, kd, vn, pmat, gl, hist, bsz=bsz, nc_seq=nc_seq,
                                          name="dn_scan_bwd")
    dqn, dkn, dv, dsa, d_alog, d_dtb, mlp_parts = _dn_intra_bwd(
        qn, kn, v, gates, alog_row, dtb_row, u, w, tmat, du, dw, dqg, dkd, do_dn, vn, dgl, nc_seq=nc_seq,
        name="dn_intra_bwd", rider=ride2)
    dz, d_conv_w = _dnprep_bwd_a(projp, conv_w, dqn, dkn, dv, bsz=bsz, t_seq=t_seq, tt=tt, name="dn_prep_bwd")
    dcin = _dnprep_bwd_b(dz, conv_w, bsz=bsz, t_seq=t_seq, tt=tt, name="conv_bwd")
    gdqg, gdkd, gdvi, gdgl = _gla_scan_bwd(do_gla, gqg, gkd, ggl, projp, ghist, bsz=bsz, nc_seq=nc_seq,
                                            name="gla_scan_bwd")
    dgqk, dgv, dsb, d_w2p, d_gla_b = _gla_intra_bwd(projp, gates, w2p, gla_b, do_gla, gdqg, gdkd, gdvi, gdgl,
                                                    nc_seq=nc_seq, name="gla_intra_bwd")

    secs = (dcin, ddz, dgqk, dgv, dgr, dsa, dsb)
    g_lo = _grad_tn(h, secs[0:2], tk=tk2, name="w_in_grad_lo")
    g_hi = _grad_tn(h, secs[2:7], tk=tk2, name="w_in_grad_hi")
    ride3 = None
    if where is not None:
        late_sm = [_shards_from_padded(g_lo, g_hi), d_w_out.reshape(N_CHIPS, d // N_CHIPS, d)]
        late_theirs = _exchange_now(_SiblingHalvesRider(late_sm), name="sibling_halves")
        late_pair = [_pair_sum(where, a, b, name=f"pair_sum_{k}") for k, (a, b) in enumerate(zip(late_sm, late_theirs))]
        ride3 = _ChipExchangeRider(late_pair)
    grad_x, d_meta_rows, d_norm1_g, late_parts = _inproj_bwd(secs, wp, h0, norm1_g, dx1, bsz=bsz, t_seq=t_seq, tr=tt,
                                                             name="in_proj_bwd", rider=ride3)

    grads = dict(w_in_lo=g_lo, w_in_hi=g_hi, w_out=d_w_out, w_up_shards=d_w_up_sm, w_down=d_w_down, meta_rows=d_meta_rows,
                 norm1_g=d_norm1_g, conv_w=d_conv_w, a_log_tile=d_alog, dt_bias_tile=d_dtb, dn_norm_g=d_dn_norm_g,
                 gla_w2=d_w2p[0:GLA_RANK], gla_b=d_gla_b, gla_norm_g=d_gla_norm_g, norm2_g=d_norm2_g,
                 final_norm_g=d_final_g, loss_tile=loss_tile)
    if where is not None:
        grads["exchanged"] = (late_pair + mlp_pair, list(late_parts) + list(mlp_parts))
    return grad_x, grads


SHARD_W = IN_WIDTH // N_CHIPS
PADDED_ORDER = ((0, 2048), (2056, 3592), (2048, 2056), LANE - 8, (3592, 3608), LANE - GLA_RANK)


def _pad_layout(w_full):
    pieces = [jnp.zeros((w_full.shape[0], seg), w_full.dtype) if isinstance(seg, int) else w_full[:, seg[0]:seg[1]]
              for seg in PADDED_ORDER]
    return jnp.concatenate(pieces, axis=1)


def _padded_from_shards(stack):
    pieces = []
    for seg in PADDED_ORDER:
        if isinstance(seg, int):
            pieces.append(jnp.zeros((stack.shape[1], seg), stack.dtype))
            continue
        for j in range(N_CHIPS):
            lo, hi = max(seg[0], j * SHARD_W), min(seg[1], (j + 1) * SHARD_W)
            if lo < hi:
                pieces.append(stack[j, :, lo - j * SHARD_W:hi - j * SHARD_W])
    return jnp.concatenate(pieces, axis=1)


def _shards_from_padded(g_lo, g_hi):
    split = g_lo.shape[1]
    starts, pos = [], 0
    for seg in PADDED_ORDER:
        width = seg if isinstance(seg, int) else seg[1] - seg[0]
        if not isinstance(seg, int):
            starts.append((seg[0], seg[1], pos))
        pos += width
    shards = []
    for j in range(N_CHIPS):
        pieces = []
        for a, b, p0 in sorted(starts):
            lo, hi = max(a, j * SHARD_W), min(b, (j + 1) * SHARD_W)
            if lo < hi:
                src, off = (g_lo, 0) if p0 < split else (g_hi, split)
                pieces.append(src[:, p0 + lo - a - off:p0 + hi - a - off])
        pieces.append(jnp.zeros((g_lo.shape[0], D_MODEL - SHARD_W), g_lo.dtype))
        shards.append(jnp.concatenate(pieces, axis=1))
    return jnp.stack(shards)


def _pack_small(g, bsz):
    assert bsz * N_META == 32
    row = jnp.concatenate([g["a_log_tile"], g["dt_bias_tile"], g["dn_norm_g"], g["gla_norm_g"], g["gla_b"],
                           g["loss_tile"], jnp.zeros((1, LANE), F32)], axis=1)
    return jnp.concatenate([g["meta_rows"], g["norm1_g"], g["conv_w"].reshape(6, D_MODEL), row,
                            g["gla_w2"].reshape(4, D_MODEL), g["norm2_g"], g["final_norm_g"],
                            jnp.zeros((2, D_MODEL), F32)], axis=0)


def kernel(x, meta_tokens, norm1_g, w_in, conv_w, a_log, dt_bias, dn_norm_g, gla_w2, gla_b, gla_norm_g, w_out, norm2_g, w_up, w_down, final_norm_g, loss_target, m_meta_tokens, m_norm1_g, m_w_in, m_conv_w, m_a_log, m_dt_bias, m_dn_norm_g, m_gla_w2, m_gla_b, m_gla_norm_g, m_w_out, m_norm2_g, m_w_up, m_w_down, m_final_norm_g, v_meta_tokens, v_norm1_g, v_w_in, v_conv_w, v_a_log, v_dt_bias, v_dn_norm_g, v_gla_w2, v_gla_b, v_gla_norm_g, v_w_out, v_norm2_g, v_w_up, v_w_down, v_final_norm_g):
    bsz = x.shape[0]
    chip = 2 * lax.axis_index("x") + lax.axis_index("y")

    lane_pad = lambda a, wd: jnp.pad(a, ((0, 0), (0, wd - a.shape[1])))
    where = jnp.stack([lax.axis_index("c"), chip]).astype(jnp.int32)
    slot = lambda a, dt, nm: _to_slot(where, a, dt, name="slot_" + nm)
    (g_meta,) = _exchange_now(_GatherRider([slot(meta_tokens, F32, "meta")], [False]), name="gather_meta")
    early = _GatherRider([slot(lane_pad(w_in[0], D_MODEL), BF16, "w_in"), slot(conv_w[0], F32, "conv"),
                          slot(lane_pad(gla_w2[0], LANE), F32, "gla_w2")], [True, False, False])
    late = (_GatherRider([slot(w_up[0], BF16, "w_up")], [True]), _GatherRider([slot(w_down[0], BF16, "w_down")], [True]),
            _GatherRider([slot(w_out[0], BF16, "w_out")], [True]))
    meta_f = g_meta.transpose(1, 0, 2).reshape(N_META, D_MODEL)

    grad_x, g = _local_step(x, loss_target, meta_f, norm1_g, None, None, a_log, dt_bias, dn_norm_g, None, gla_b,
                            gla_norm_g, None, norm2_g, None, None, final_norm_g.reshape(1, D_MODEL), late_gather=late,
                            where=where, early_gather=early)

    pair, parts = g["exchanged"]
    halves = [_chip_sum(where, p, q, name=f"chip_sum_{k}") for k, (p, q) in enumerate(zip(pair, parts))]
    gw_in, gw_out, gw_up, gw_down = _sibling_join(halves)

    red = _small_allreduce(_pack_small(g, bsz))
    g_meta_full = red[0:N_META]
    g_norm1 = red[32:33]
    g_conv_full = red[33:39].reshape(4, QKV_W)
    srow = red[39:40]
    g_alog, g_dtb = srow[:, 4:8], srow[:, LANE + 4:LANE + 8]
    g_dn_norm, g_gla_norm = srow[:, 2 * LANE:3 * LANE], srow[:, 3 * LANE:4 * LANE]
    g_gla_b = srow[:, 4 * LANE:6 * LANE]
    loss = srow[0, 6 * LANE]
    g_w2_full = red[40:44].reshape(GLA_RANK, GQ_W)
    g_norm2 = red[44:45]
    g_final = red[45:46]
    g_meta_sh = lax.dynamic_slice_in_dim(g_meta_full, chip * (D_MODEL // N_CHIPS), D_MODEL // N_CHIPS, axis=1)
    g_conv_sh = lax.dynamic_slice_in_dim(g_conv_full, chip * (QKV_W // N_CHIPS), QKV_W // N_CHIPS, axis=1)
    g_w2_sh = lax.dynamic_slice_in_dim(g_w2_full, chip * (GQ_W // N_CHIPS), GQ_W // N_CHIPS, axis=1)

    names = ["meta_tokens", "norm1_g", "w_in", "conv_w", "a_log", "dt_bias", "dn_norm_g", "gla_w2", "gla_b",
             "gla_norm_g", "w_out", "norm2_g", "w_up", "w_down", "final_norm_g"]
    weights = dict(meta_tokens=meta_tokens, norm1_g=norm1_g, w_in=w_in, conv_w=conv_w, a_log=a_log, dt_bias=dt_bias,
                   dn_norm_g=dn_norm_g, gla_w2=gla_w2, gla_b=gla_b, gla_norm_g=gla_norm_g, w_out=w_out,
                   norm2_g=norm2_g, w_up=w_up, w_down=w_down, final_norm_g=final_norm_g)
    ms = dict(meta_tokens=m_meta_tokens, norm1_g=m_norm1_g, w_in=m_w_in, conv_w=m_conv_w, a_log=m_a_log,
              dt_bias=m_dt_bias, dn_norm_g=m_dn_norm_g, gla_w2=m_gla_w2, gla_b=m_gla_b, gla_norm_g=m_gla_norm_g,
              w_out=m_w_out, norm2_g=m_norm2_g, w_up=m_w_up, w_down=m_w_down, final_norm_g=m_final_norm_g)
    vs = dict(meta_tokens=v_meta_tokens, norm1_g=v_norm1_g, w_in=v_w_in, conv_w=v_conv_w, a_log=v_a_log,
              dt_bias=v_dt_bias, dn_norm_g=v_dn_norm_g, gla_w2=v_gla_w2, gla_b=v_gla_b, gla_norm_g=v_gla_norm_g,
              w_out=v_w_out, norm2_g=v_norm2_g, w_up=v_w_up, w_down=v_w_down, final_norm_g=v_final_norm_g)
    grads2d = dict(meta_tokens=g_meta_sh, norm1_g=g_norm1, w_in=gw_in, conv_w=g_conv_sh, a_log=g_alog, dt_bias=g_dtb,
                   dn_norm_g=g_dn_norm, gla_w2=g_w2_sh, gla_b=g_gla_b, gla_norm_g=g_gla_norm, w_out=gw_out,
                   norm2_g=g_norm2, w_up=gw_up, w_down=gw_down, final_norm_g=g_final)
    out_g, out_d, out_m, out_v = [], [], [], []
    for nm in names:
        shape = weights[nm].shape
        g2 = grads2d[nm]
        if nm == "w_in":
            tview = lambda a: jnp.transpose(a, (2, 0, 1))
            res = _adamw_rows(tview(weights[nm]), g2[:, 0:SHARD_W].T.reshape(SHARD_W, 1, D_MODEL), tview(ms[nm]),
                              tview(vs[nm]), name=f"adamw_{nm}")
            res = [jnp.transpose(a, (1, 2, 0)) for a in res]
            gout = res[3]
        elif len(shape) == 3:
            res = _adamw(weights[nm], g2, ms[nm], vs[nm], name=f"adamw_{nm}")
            gout = g2.reshape(shape)
        else:
            as2d = lambda a: a.reshape(g2.shape)
            res = _adamw(as2d(weights[nm]), g2, as2d(ms[nm]), as2d(vs[nm]), name=f"adamw_{nm}")
            gout = g2.reshape(shape)
        out_g.append(gout)
        out_d.append(res[0].reshape(shape))
        out_m.append(res[1].reshape(shape))
        out_v.append(res[2].reshape(shape))
    return (loss, grad_x, *out_g, *out_d, *out_m, *out_v)
```
